```python
import math
import jax, jax.numpy as jnp
from jax import lax
import numpy as np

D_MODEL = 1024
BATCH = 8
SEQ = 8192
DEPTH = 2

N_META = 16
HEAD_DIM = 64
N_Q_HEADS = 16
N_KV_HEADS = 4
Q_PER_KV = N_Q_HEADS // N_KV_HEADS
D_ATTN = N_Q_HEADS * HEAD_DIM
D_KV = N_KV_HEADS * HEAD_DIM
WINDOW = 128
BLOCK = 128
ROPE_THETA = 10000.0
ATTN_SCALE = HEAD_DIM ** -0.5
NEG_INF = -1e30
D_SSM = D_MODEL // 2
SSM_GROUP = 16
N_SSM_GROUPS = D_SSM // SSM_GROUP
SSM_STATE = 64
DT_MIN = 1e-3
DT_MAX = 1e-1
D_FF = 4 * D_MODEL
RMS_EPS = 1e-6
D_IN = D_SSM + D_ATTN + 2 * D_KV + 2 * D_MODEL

kernel_name = "hybrid_s5_swa_sink_gated_block"


def rmsnorm(x, gain):
    xf = x.astype(jnp.float32)
    xf = xf * lax.rsqrt(jnp.mean(xf * xf, axis=-1, keepdims=True) + RMS_EPS)
    return (xf * gain.astype(jnp.float32)).astype(x.dtype)


def rotary(t, pos):
    inv_freq = 1.0 / (ROPE_THETA ** (jnp.arange(0, HEAD_DIM, 2, dtype=jnp.float32) / HEAD_DIM))
    ang = pos.astype(jnp.float32)[:, None] * inv_freq[None, :]
    ang = jnp.concatenate([ang, ang], axis=-1)[None, :, None, :]
    tf = t.astype(jnp.float32)
    half = HEAD_DIM // 2
    rot = jnp.concatenate([-tf[..., half:], tf[..., :half]], axis=-1)
    return (tf * jnp.cos(ang) + rot * jnp.sin(ang)).astype(t.dtype)


def s5_branch(u, a_re, a_im, log_dt, b_re, b_im, c_re, c_im, d_skip, w_glu, b_glu):
    bsz, seq_len, _ = u.shape
    f32 = jnp.float32
    uf = u.astype(f32).reshape(bsz, seq_len, N_SSM_GROUPS, SSM_GROUP)
    a = lax.complex(a_re.astype(f32), a_im.astype(f32))
    dt = jnp.exp(log_dt.astype(f32))[:, None]
    a_bar = jnp.exp(a * dt)
    b = lax.complex(b_re.astype(f32), b_im.astype(f32))
    b_bar = ((a_bar - 1.0) / a)[..., None] * b
    bu = jnp.einsum('blgc,gnc->blgn', uf.astype(jnp.complex64), b_bar)

    def combine(e1, e2):
        a1, s1 = e1
        a2, s2 = e2
        return a1 * a2, a2 * s1 + s2

    a_seq = jnp.broadcast_to(a_bar, bu.shape)
    _, states = lax.associative_scan(combine, (a_seq, bu), axis=1)
    c = lax.complex(c_re.astype(f32), c_im.astype(f32))
    y = jnp.einsum('blgn,gcn->blgc', states, c).real + d_skip.astype(f32) * uf
    y = y.reshape(bsz, seq_len, D_SSM)
    z = jax.nn.gelu(y)
    z = z * jax.nn.sigmoid(z @ w_glu.astype(f32) + b_glu.astype(f32))
    return z.astype(u.dtype)


def sliding_window_attention(q, k, v, sinks):
    bsz, seq_len = q.shape[0], q.shape[1]
    pad = (-seq_len) % BLOCK
    n_blk = (seq_len + pad) // BLOCK
    qb = jnp.pad(q, ((0, 0), (pad, 0), (0, 0), (0, 0))).reshape(
        bsz, n_blk, BLOCK, N_KV_HEADS, Q_PER_KV, HEAD_DIM)
    front = ((0, 0), (pad + BLOCK, 0), (0, 0), (0, 0))
    kb = jnp.pad(k, front).reshape(bsz, n_blk + 1, BLOCK, N_KV_HEADS, HEAD_DIM)
    vb = jnp.pad(v, front).reshape(bsz, n_blk + 1, BLOCK, N_KV_HEADS, HEAD_DIM)
    k_win = jnp.concatenate([kb[:, :-1], kb[:, 1:]], axis=2)
    v_win = jnp.concatenate([vb[:, :-1], vb[:, 1:]], axis=2)
    k_meta, v_meta = k[:, :N_META], v[:, :N_META]

    s_meta = jnp.einsum('bnqkgd,bskd->bnkgqs', qb, k_meta, preferred_element_type=jnp.float32)
    s_win = jnp.einsum('bnqkgd,bnskd->bnkgqs', qb, k_win, preferred_element_type=jnp.float32)
    s = jnp.concatenate([s_meta, s_win], axis=-1) * ATTN_SCALE

    q_idx = (jnp.arange(n_blk)[:, None] * BLOCK + jnp.arange(BLOCK)[None, :])[:, :, None]
    kw_idx = ((jnp.arange(n_blk)[:, None] - 1) * BLOCK + jnp.arange(2 * BLOCK)[None, :])[:, None, :]
    km_idx = (pad + jnp.arange(N_META))[None, None, :]
    win_ok = (kw_idx >= pad) & (kw_idx <= q_idx) & (q_idx - kw_idx < WINDOW)
    meta_ok = (km_idx <= q_idx) & (q_idx - km_idx >= WINDOW)
    ok = jnp.concatenate([jnp.broadcast_to(meta_ok, (n_blk, BLOCK, N_META)), win_ok], axis=-1)
    s = jnp.where(ok[None, :, None, None], s, NEG_INF)

    sink = sinks.astype(jnp.float32).reshape(N_KV_HEADS, Q_PER_KV)[None, None, :, :, None, None]
    m = jnp.maximum(jnp.max(s, axis=-1, keepdims=True), sink)
    p = jnp.exp(s - m)
    p = p / (jnp.sum(p, axis=-1, keepdims=True) + jnp.exp(sink - m))
    p = p.astype(v.dtype)
    o = (jnp.einsum('bnkgqs,bskd->bnqkgd', p[..., :N_META], v_meta)
         + jnp.einsum('bnkgqs,bnskd->bnqkgd', p[..., N_META:], v_win))
    return o.reshape(bsz, n_blk * BLOCK, D_ATTN)[:, pad:]


def hybrid_mixer(h, pos, w_in, a_re, a_im, log_dt, b_re, b_im, c_re, c_im, d_skip, w_glu, b_glu,
                 sinks, w_o_ssm, w_o_attn, w_out):
    bsz, seq_len, _ = h.shape
    proj = h @ w_in
    cuts = [D_SSM, D_SSM + D_ATTN, D_SSM + D_ATTN + D_KV, D_SSM + D_ATTN + 2 * D_KV,
            D_SSM + D_ATTN + 2 * D_KV + D_MODEL]
    u, q, k, v, g_ssm, g_attn = jnp.split(proj, cuts, axis=-1)
    y_ssm = s5_branch(u, a_re, a_im, log_dt, b_re, b_im, c_re, c_im, d_skip, w_glu, b_glu)
    q = rotary(q.reshape(bsz, seq_len, N_Q_HEADS, HEAD_DIM), pos)
    k = rotary(k.reshape(bsz, seq_len, N_KV_HEADS, HEAD_DIM), pos)
    v = v.reshape(bsz, seq_len, N_KV_HEADS, HEAD_DIM)
    y_attn = sliding_window_attention(q, k, v, sinks)
    merged = jax.nn.sigmoid(g_ssm) * (y_ssm @ w_o_ssm) + jax.nn.sigmoid(g_attn) * (y_attn @ w_o_attn)
    return merged @ w_out


def _fwd_setup_inputs(seed: int = 0) -> dict:
    key = jax.random.key(seed)
    ks = jax.random.split(key, 24)
    f32 = jnp.float32

    def nrm(k, shape, std):
        return std * jax.random.normal(k, shape, f32)

    G, N, C = N_SSM_GROUPS, SSM_STATE, SSM_GROUP
    a_im_init = math.pi * jnp.arange(N, dtype=f32)[None, None, :]
    return {
        "x": nrm(ks[0], (BATCH, SEQ, D_MODEL), 1.0),
        "meta_tokens": nrm(ks[1], (N_META, D_MODEL), 1.0),
        "norm_mix_pre": 1.0 + nrm(ks[2], (DEPTH, D_MODEL), 0.02),
        "norm_mix_post": 1.0 + nrm(ks[3], (DEPTH, D_MODEL), 0.02),
        "norm_mlp_pre": 1.0 + nrm(ks[4], (DEPTH, D_MODEL), 0.02),
        "norm_mlp_post": 1.0 + nrm(ks[5], (DEPTH, D_MODEL), 0.02),
        "w_in": nrm(ks[6], (DEPTH, D_MODEL, D_IN), D_MODEL ** -0.5),
        "ssm_a_re": -0.5 + nrm(ks[7], (DEPTH, G, N), 0.01),
        "ssm_a_im": a_im_init + nrm(ks[8], (DEPTH, G, N), 0.01),
        "ssm_log_dt": jax.random.uniform(ks[9], (DEPTH, G), f32, math.log(DT_MIN), math.log(DT_MAX)),
        "ssm_b_re": nrm(ks[10], (DEPTH, G, N, C), (2.0 * C) ** -0.5),
        "ssm_b_im": nrm(ks[11], (DEPTH, G, N, C), (2.0 * C) ** -0.5),
        "ssm_c_re": nrm(ks[12], (DEPTH, G, C, N), 0.5),
        "ssm_c_im": nrm(ks[13], (DEPTH, G, C, N), 0.5),
        "ssm_d": nrm(ks[14], (DEPTH, G, C), 1.0),
        "w_glu": nrm(ks[15], (DEPTH, D_SSM, D_SSM), D_SSM ** -0.5),
        "b_glu": nrm(ks[16], (DEPTH, D_SSM), 0.01),
        "attn_sinks": nrm(ks[17], (DEPTH, N_Q_HEADS), 0.5),
        "w_o_ssm": nrm(ks[18], (DEPTH, D_SSM, D_MODEL), D_SSM ** -0.5),
        "w_o_attn": nrm(ks[19], (DEPTH, D_ATTN, D_MODEL), D_ATTN ** -0.5),
        "w_out": nrm(ks[20], (DEPTH, D_MODEL, D_MODEL), D_MODEL ** -0.5),
        "w_up": nrm(ks[21], (DEPTH, D_MODEL, D_FF), D_MODEL ** -0.5),
        "w_down": nrm(ks[22], (DEPTH, D_FF, D_MODEL), D_FF ** -0.5),
    }


def _fwd_reference(x, meta_tokens, norm_mix_pre, norm_mix_post, norm_mlp_pre, norm_mlp_post, w_in,
              ssm_a_re, ssm_a_im, ssm_log_dt, ssm_b_re, ssm_b_im, ssm_c_re, ssm_c_im, ssm_d,
              w_glu, b_glu, attn_sinks, w_o_ssm, w_o_attn, w_out, w_up, w_down):
    bsz = x.shape[0]
    meta = jnp.broadcast_to(meta_tokens[None].astype(x.dtype), (bsz, N_META, D_MODEL))
    hres = jnp.concatenate([meta, x], axis=1)
    pos = jnp.arange(hres.shape[1], dtype=jnp.int32)
    for l in range(DEPTH):
        h = rmsnorm(hres, norm_mix_pre[l])
        mix = hybrid_mixer(h, pos, w_in[l], ssm_a_re[l], ssm_a_im[l], ssm_log_dt[l], ssm_b_re[l],
                           ssm_b_im[l], ssm_c_re[l], ssm_c_im[l], ssm_d[l], w_glu[l], b_glu[l],
                           attn_sinks[l], w_o_ssm[l], w_o_attn[l], w_out[l])
        hres = hres + rmsnorm(mix, norm_mix_post[l])
        h = rmsnorm(hres, norm_mlp_pre[l])
        ff = jnp.square(jax.nn.relu(h @ w_up[l])) @ w_down[l]
        hres = hres + rmsnorm(ff, norm_mlp_post[l])
    return hres[:, N_META:]


import jax as _jax
import jax.numpy as _jnp

TWIN_FORMAT = 'train_step'
FWD_PARAMS = ['x', 'meta_tokens', 'norm_mix_pre', 'norm_mix_post', 'norm_mlp_pre', 'norm_mlp_post', 'w_in', 'ssm_a_re', 'ssm_a_im', 'ssm_log_dt', 'ssm_b_re', 'ssm_b_im', 'ssm_c_re', 'ssm_c_im', 'ssm_d', 'w_glu', 'b_glu', 'attn_sinks', 'w_o_ssm', 'w_o_attn', 'w_out', 'w_up', 'w_down']
TWIN_WEIGHTS = ['meta_tokens', 'norm_mix_pre', 'norm_mix_post', 'norm_mlp_pre', 'norm_mlp_post', 'w_in', 'ssm_a_re', 'ssm_a_im', 'ssm_log_dt', 'ssm_b_re', 'ssm_b_im', 'ssm_c_re', 'ssm_c_im', 'ssm_d', 'w_glu', 'b_glu', 'attn_sinks', 'w_o_ssm', 'w_o_attn', 'w_out', 'w_up', 'w_down']
TWIN_DIFF_INPUT = 'x'
TWIN_INPUTS = ['x', 'meta_tokens', 'norm_mix_pre', 'norm_mix_post', 'norm_mlp_pre', 'norm_mlp_post', 'w_in', 'ssm_a_re', 'ssm_a_im', 'ssm_log_dt', 'ssm_b_re', 'ssm_b_im', 'ssm_c_re', 'ssm_c_im', 'ssm_d', 'w_glu', 'b_glu', 'attn_sinks', 'w_o_ssm', 'w_o_attn', 'w_out', 'w_up', 'w_down', 'loss_target', 'm_meta_tokens', 'm_norm_mix_pre', 'm_norm_mix_post', 'm_norm_mlp_pre', 'm_norm_mlp_post', 'm_w_in', 'm_ssm_a_re', 'm_ssm_a_im', 'm_ssm_log_dt', 'm_ssm_b_re', 'm_ssm_b_im', 'm_ssm_c_re', 'm_ssm_c_im', 'm_ssm_d', 'm_w_glu', 'm_b_glu', 'm_attn_sinks', 'm_w_o_ssm', 'm_w_o_attn', 'm_w_out', 'm_w_up', 'm_w_down', 'v_meta_tokens', 'v_norm_mix_pre', 'v_norm_mix_post', 'v_norm_mlp_pre', 'v_norm_mlp_post', 'v_w_in', 'v_ssm_a_re', 'v_ssm_a_im', 'v_ssm_log_dt', 'v_ssm_b_re', 'v_ssm_b_im', 'v_ssm_c_re', 'v_ssm_c_im', 'v_ssm_d', 'v_w_glu', 'v_b_glu', 'v_attn_sinks', 'v_w_o_ssm', 'v_w_o_attn', 'v_w_out', 'v_w_up', 'v_w_down']
TWIN_OUTPUTS = ['loss', 'grad_x', 'grad_meta_tokens', 'grad_norm_mix_pre', 'grad_norm_mix_post', 'grad_norm_mlp_pre', 'grad_norm_mlp_post', 'grad_w_in', 'grad_ssm_a_re', 'grad_ssm_a_im', 'grad_ssm_log_dt', 'grad_ssm_b_re', 'grad_ssm_b_im', 'grad_ssm_c_re', 'grad_ssm_c_im', 'grad_ssm_d', 'grad_w_glu', 'grad_b_glu', 'grad_attn_sinks', 'grad_w_o_ssm', 'grad_w_o_attn', 'grad_w_out', 'grad_w_up', 'grad_w_down', 'delta_meta_tokens', 'delta_norm_mix_pre', 'delta_norm_mix_post', 'delta_norm_mlp_pre', 'delta_norm_mlp_post', 'delta_w_in', 'delta_ssm_a_re', 'delta_ssm_a_im', 'delta_ssm_log_dt', 'delta_ssm_b_re', 'delta_ssm_b_im', 'delta_ssm_c_re', 'delta_ssm_c_im', 'delta_ssm_d', 'delta_w_glu', 'delta_b_glu', 'delta_attn_sinks', 'delta_w_o_ssm', 'delta_w_o_attn', 'delta_w_out', 'delta_w_up', 'delta_w_down', 'new_m_meta_tokens', 'new_m_norm_mix_pre', 'new_m_norm_mix_post', 'new_m_norm_mlp_pre', 'new_m_norm_mlp_post', 'new_m_w_in', 'new_m_ssm_a_re', 'new_m_ssm_a_im', 'new_m_ssm_log_dt', 'new_m_ssm_b_re', 'new_m_ssm_b_im', 'new_m_ssm_c_re', 'new_m_ssm_c_im', 'new_m_ssm_d', 'new_m_w_glu', 'new_m_b_glu', 'new_m_attn_sinks', 'new_m_w_o_ssm', 'new_m_w_o_attn', 'new_m_w_out', 'new_m_w_up', 'new_m_w_down', 'new_v_meta_tokens', 'new_v_norm_mix_pre', 'new_v_norm_mix_post', 'new_v_norm_mlp_pre', 'new_v_norm_mlp_post', 'new_v_w_in', 'new_v_ssm_a_re', 'new_v_ssm_a_im', 'new_v_ssm_log_dt', 'new_v_ssm_b_re', 'new_v_ssm_b_im', 'new_v_ssm_c_re', 'new_v_ssm_c_im', 'new_v_ssm_d', 'new_v_w_glu', 'new_v_b_glu', 'new_v_attn_sinks', 'new_v_w_o_ssm', 'new_v_w_o_attn', 'new_v_w_out', 'new_v_w_up', 'new_v_w_down']
TWIN_LEAF_KINDS = {'loss': 'loss', 'grad_x': 'grad_x', 'grad_meta_tokens': 'grad_w', 'grad_norm_mix_pre': 'grad_w', 'grad_norm_mix_post': 'grad_w', 'grad_norm_mlp_pre': 'grad_w', 'grad_norm_mlp_post': 'grad_w', 'grad_w_in': 'grad_w', 'grad_ssm_a_re': 'grad_w', 'grad_ssm_a_im': 'grad_w', 'grad_ssm_log_dt': 'grad_w', 'grad_ssm_b_re': 'grad_w', 'grad_ssm_b_im': 'grad_w', 'grad_ssm_c_re': 'grad_w', 'grad_ssm_c_im': 'grad_w', 'grad_ssm_d': 'grad_w', 'grad_w_glu': 'grad_w', 'grad_b_glu': 'grad_w', 'grad_attn_sinks': 'grad_w', 'grad_w_o_ssm': 'grad_w', 'grad_w_o_attn': 'grad_w', 'grad_w_out': 'grad_w', 'grad_w_up': 'grad_w', 'grad_w_down': 'grad_w', 'delta_meta_tokens': 'delta_w', 'delta_norm_mix_pre': 'delta_w', 'delta_norm_mix_post': 'delta_w', 'delta_norm_mlp_pre': 'delta_w', 'delta_norm_mlp_post': 'delta_w', 'delta_w_in': 'delta_w', 'delta_ssm_a_re': 'delta_w', 'delta_ssm_a_im': 'delta_w', 'delta_ssm_log_dt': 'delta_w', 'delta_ssm_b_re': 'delta_w', 'delta_ssm_b_im': 'delta_w', 'delta_ssm_c_re': 'delta_w', 'delta_ssm_c_im': 'delta_w', 'delta_ssm_d': 'delta_w', 'delta_w_glu': 'delta_w', 'delta_b_glu': 'delta_w', 'delta_attn_sinks': 'delta_w', 'delta_w_o_ssm': 'delta_w', 'delta_w_o_attn': 'delta_w', 'delta_w_out': 'delta_w', 'delta_w_up': 'delta_w', 'delta_w_down': 'delta_w', 'new_m_meta_tokens': 'new_m', 'new_m_norm_mix_pre': 'new_m', 'new_m_norm_mix_post': 'new_m', 'new_m_norm_mlp_pre': 'new_m', 'new_m_norm_mlp_post': 'new_m', 'new_m_w_in': 'new_m', 'new_m_ssm_a_re': 'new_m', 'new_m_ssm_a_im': 'new_m', 'new_m_ssm_log_dt': 'new_m', 'new_m_ssm_b_re': 'new_m', 'new_m_ssm_b_im': 'new_m', 'new_m_ssm_c_re': 'new_m', 'new_m_ssm_c_im': 'new_m', 'new_m_ssm_d': 'new_m', 'new_m_w_glu': 'new_m', 'new_m_b_glu': 'new_m', 'new_m_attn_sinks': 'new_m', 'new_m_w_o_ssm': 'new_m', 'new_m_w_o_attn': 'new_m', 'new_m_w_out': 'new_m', 'new_m_w_up': 'new_m', 'new_m_w_down': 'new_m', 'new_v_meta_tokens': 'new_v', 'new_v_norm_mix_pre': 'new_v', 'new_v_norm_mix_post': 'new_v', 'new_v_norm_mlp_pre': 'new_v', 'new_v_norm_mlp_post': 'new_v', 'new_v_w_in': 'new_v', 'new_v_ssm_a_re': 'new_v', 'new_v_ssm_a_im': 'new_v', 'new_v_ssm_log_dt': 'new_v', 'new_v_ssm_b_re': 'new_v', 'new_v_ssm_b_im': 'new_v', 'new_v_ssm_c_re': 'new_v', 'new_v_ssm_c_im': 'new_v', 'new_v_ssm_d': 'new_v', 'new_v_w_glu': 'new_v', 'new_v_b_glu': 'new_v', 'new_v_attn_sinks': 'new_v', 'new_v_w_o_ssm': 'new_v', 'new_v_w_o_attn': 'new_v', 'new_v_w_out': 'new_v', 'new_v_w_up': 'new_v', 'new_v_w_down': 'new_v'}


def _forward(args):
    return _fwd_reference(*[args[k] for k in FWD_PARAMS])


def _output_shape():
    def fwd():
        inp = _fwd_setup_inputs(0)
        return _fwd_reference(*[inp[k] for k in FWD_PARAMS])
    out = _jax.eval_shape(fwd)
    return out.shape, out.dtype

N_MICROBATCH = 1
ADAM_LR = 0.001
ADAM_B1 = 0.9
ADAM_B2 = 0.999
ADAM_EPS = 1e-08
ADAM_WD = 0.01
ADAM_STEP = 10
PER_EXAMPLE_BATCH_AXIS = {'x': 0, 'loss_target': 0}
SHARED_INPUTS = []
_WEIGHT_DTYPES = {'meta_tokens': _jnp.float32, 'norm_mix_pre': _jnp.float32, 'norm_mix_post': _jnp.float32, 'norm_mlp_pre': _jnp.float32, 'norm_mlp_post': _jnp.float32, 'w_in': _jnp.float32, 'ssm_a_re': _jnp.float32, 'ssm_a_im': _jnp.float32, 'ssm_log_dt': _jnp.float32, 'ssm_b_re': _jnp.float32, 'ssm_b_im': _jnp.float32, 'ssm_c_re': _jnp.float32, 'ssm_c_im': _jnp.float32, 'ssm_d': _jnp.float32, 'w_glu': _jnp.float32, 'b_glu': _jnp.float32, 'attn_sinks': _jnp.float32, 'w_o_ssm': _jnp.float32, 'w_o_attn': _jnp.float32, 'w_out': _jnp.float32, 'w_up': _jnp.float32, 'w_down': _jnp.float32}
MOMENT_SCALE = {'meta_tokens': 1.658412e+00, 'norm_mix_pre': 2.622887e+01, 'norm_mix_post': 8.561863e+01, 'norm_mlp_pre': 1.944674e+01, 'norm_mlp_post': 7.918685e+01, 'w_in': 1.372991e+01, 'ssm_a_re': 4.557846e+00, 'ssm_a_im': 3.327982e+00, 'ssm_log_dt': 2.720499e+02, 'ssm_b_re': 3.372349e+00, 'ssm_b_im': 2.989683e+00, 'ssm_c_re': 1.187754e+00, 'ssm_c_im': 9.099321e-01, 'ssm_d': 4.203838e+01, 'w_glu': 6.711408e+00, 'b_glu': 1.756554e+01, 'attn_sinks': 1.613092e+00, 'w_o_ssm': 3.138957e+01, 'w_o_attn': 2.381693e+01, 'w_out': 4.171610e+01, 'w_up': 9.931020e+00, 'w_down': 3.676248e+01}


def _to_microbatches(a, axis):
    t = _jnp.moveaxis(a, axis, 0)
    t = t.reshape((N_MICROBATCH, t.shape[0] // N_MICROBATCH) + t.shape[1:])
    return _jnp.moveaxis(t, 1, axis + 1)


def setup_inputs(seed: int = 0) -> dict:
    inp = _fwd_setup_inputs(seed)
    key = _jax.random.fold_in(_jax.random.key(seed), 7919)
    shape, _ = _output_shape()
    out = dict(inp)
    out["loss_target"] = _jax.random.normal(_jax.random.fold_in(key, 0), shape, _jnp.float32)
    for i, name in enumerate(TWIN_WEIGHTS):
        w = inp[name].astype(_jnp.float32)
        if MOMENT_SCALE is None:
            s = _jnp.sqrt(_jnp.mean(_jnp.square(w)) + 1e-30)
        else:
            s = MOMENT_SCALE[name]
        km, kv = _jax.random.split(_jax.random.fold_in(key, i + 1))
        out[name] = w
        out["m_" + name] = s * _jax.random.normal(km, w.shape, _jnp.float32)
        out["v_" + name] = (s * s) * _jax.random.uniform(kv, w.shape, _jnp.float32, 0.5, 1.5)
    if N_MICROBATCH > 1:
        for name, axis in PER_EXAMPLE_BATCH_AXIS.items():
            out[name] = _to_microbatches(out[name], axis)
    return {'x': out['x'], 'meta_tokens': out['meta_tokens'], 'norm_mix_pre': out['norm_mix_pre'], 'norm_mix_post': out['norm_mix_post'], 'norm_mlp_pre': out['norm_mlp_pre'], 'norm_mlp_post': out['norm_mlp_post'], 'w_in': out['w_in'], 'ssm_a_re': out['ssm_a_re'], 'ssm_a_im': out['ssm_a_im'], 'ssm_log_dt': out['ssm_log_dt'], 'ssm_b_re': out['ssm_b_re'], 'ssm_b_im': out['ssm_b_im'], 'ssm_c_re': out['ssm_c_re'], 'ssm_c_im': out['ssm_c_im'], 'ssm_d': out['ssm_d'], 'w_glu': out['w_glu'], 'b_glu': out['b_glu'], 'attn_sinks': out['attn_sinks'], 'w_o_ssm': out['w_o_ssm'], 'w_o_attn': out['w_o_attn'], 'w_out': out['w_out'], 'w_up': out['w_up'], 'w_down': out['w_down'], 'loss_target': out['loss_target'], 'm_meta_tokens': out['m_meta_tokens'], 'm_norm_mix_pre': out['m_norm_mix_pre'], 'm_norm_mix_post': out['m_norm_mix_post'], 'm_norm_mlp_pre': out['m_norm_mlp_pre'], 'm_norm_mlp_post': out['m_norm_mlp_post'], 'm_w_in': out['m_w_in'], 'm_ssm_a_re': out['m_ssm_a_re'], 'm_ssm_a_im': out['m_ssm_a_im'], 'm_ssm_log_dt': out['m_ssm_log_dt'], 'm_ssm_b_re': out['m_ssm_b_re'], 'm_ssm_b_im': out['m_ssm_b_im'], 'm_ssm_c_re': out['m_ssm_c_re'], 'm_ssm_c_im': out['m_ssm_c_im'], 'm_ssm_d': out['m_ssm_d'], 'm_w_glu': out['m_w_glu'], 'm_b_glu': out['m_b_glu'], 'm_attn_sinks': out['m_attn_sinks'], 'm_w_o_ssm': out['m_w_o_ssm'], 'm_w_o_attn': out['m_w_o_attn'], 'm_w_out': out['m_w_out'], 'm_w_up': out['m_w_up'], 'm_w_down': out['m_w_down'], 'v_meta_tokens': out['v_meta_tokens'], 'v_norm_mix_pre': out['v_norm_mix_pre'], 'v_norm_mix_post': out['v_norm_mix_post'], 'v_norm_mlp_pre': out['v_norm_mlp_pre'], 'v_norm_mlp_post': out['v_norm_mlp_post'], 'v_w_in': out['v_w_in'], 'v_ssm_a_re': out['v_ssm_a_re'], 'v_ssm_a_im': out['v_ssm_a_im'], 'v_ssm_log_dt': out['v_ssm_log_dt'], 'v_ssm_b_re': out['v_ssm_b_re'], 'v_ssm_b_im': out['v_ssm_b_im'], 'v_ssm_c_re': out['v_ssm_c_re'], 'v_ssm_c_im': out['v_ssm_c_im'], 'v_ssm_d': out['v_ssm_d'], 'v_w_glu': out['v_w_glu'], 'v_b_glu': out['v_b_glu'], 'v_attn_sinks': out['v_attn_sinks'], 'v_w_o_ssm': out['v_w_o_ssm'], 'v_w_o_attn': out['v_w_o_attn'], 'v_w_out': out['v_w_out'], 'v_w_up': out['v_w_up'], 'v_w_down': out['v_w_down']}


def _loss(weights, diff, rest, loss_target):
    with _jax.named_scope("forward"):
        args = {**rest, TWIN_DIFF_INPUT: diff, **{k: w.astype(_WEIGHT_DTYPES[k]) for k, w in weights.items()}}
        y = _forward(args)
    with _jax.named_scope("loss_head"):
        err = _jnp.square(y.astype(_jnp.float32) - loss_target)
        return 0.5 * _jnp.sum(_jnp.mean(err, axis=-1)) if err.ndim else 0.5 * err


def _adamw(w, g, m, v):
    m = ADAM_B1 * m + (1.0 - ADAM_B1) * g
    v = ADAM_B2 * v + (1.0 - ADAM_B2) * _jnp.square(g)
    m_hat = m / (1.0 - ADAM_B1 ** ADAM_STEP)
    v_hat = v / (1.0 - ADAM_B2 ** ADAM_STEP)
    delta = -ADAM_LR * (m_hat / (_jnp.sqrt(v_hat) + ADAM_EPS) + ADAM_WD * w)
    return delta, m, v


def reference(x, meta_tokens, norm_mix_pre, norm_mix_post, norm_mlp_pre, norm_mlp_post, w_in, ssm_a_re, ssm_a_im, ssm_log_dt, ssm_b_re, ssm_b_im, ssm_c_re, ssm_c_im, ssm_d, w_glu, b_glu, attn_sinks, w_o_ssm, w_o_attn, w_out, w_up, w_down, loss_target, m_meta_tokens, m_norm_mix_pre, m_norm_mix_post, m_norm_mlp_pre, m_norm_mlp_post, m_w_in, m_ssm_a_re, m_ssm_a_im, m_ssm_log_dt, m_ssm_b_re, m_ssm_b_im, m_ssm_c_re, m_ssm_c_im, m_ssm_d, m_w_glu, m_b_glu, m_attn_sinks, m_w_o_ssm, m_w_o_attn, m_w_out, m_w_up, m_w_down, v_meta_tokens, v_norm_mix_pre, v_norm_mix_post, v_norm_mlp_pre, v_norm_mlp_post, v_w_in, v_ssm_a_re, v_ssm_a_im, v_ssm_log_dt, v_ssm_b_re, v_ssm_b_im, v_ssm_c_re, v_ssm_c_im, v_ssm_d, v_w_glu, v_b_glu, v_attn_sinks, v_w_o_ssm, v_w_o_attn, v_w_out, v_w_up, v_w_down):
    given = dict(x=x, meta_tokens=meta_tokens, norm_mix_pre=norm_mix_pre, norm_mix_post=norm_mix_post, norm_mlp_pre=norm_mlp_pre, norm_mlp_post=norm_mlp_post, w_in=w_in, ssm_a_re=ssm_a_re, ssm_a_im=ssm_a_im, ssm_log_dt=ssm_log_dt, ssm_b_re=ssm_b_re, ssm_b_im=ssm_b_im, ssm_c_re=ssm_c_re, ssm_c_im=ssm_c_im, ssm_d=ssm_d, w_glu=w_glu, b_glu=b_glu, attn_sinks=attn_sinks, w_o_ssm=w_o_ssm, w_o_attn=w_o_attn, w_out=w_out, w_up=w_up, w_down=w_down, loss_target=loss_target, m_meta_tokens=m_meta_tokens, m_norm_mix_pre=m_norm_mix_pre, m_norm_mix_post=m_norm_mix_post, m_norm_mlp_pre=m_norm_mlp_pre, m_norm_mlp_post=m_norm_mlp_post, m_w_in=m_w_in, m_ssm_a_re=m_ssm_a_re, m_ssm_a_im=m_ssm_a_im, m_ssm_log_dt=m_ssm_log_dt, m_ssm_b_re=m_ssm_b_re, m_ssm_b_im=m_ssm_b_im, m_ssm_c_re=m_ssm_c_re, m_ssm_c_im=m_ssm_c_im, m_ssm_d=m_ssm_d, m_w_glu=m_w_glu, m_b_glu=m_b_glu, m_attn_sinks=m_attn_sinks, m_w_o_ssm=m_w_o_ssm, m_w_o_attn=m_w_o_attn, m_w_out=m_w_out, m_w_up=m_w_up, m_w_down=m_w_down, v_meta_tokens=v_meta_tokens, v_norm_mix_pre=v_norm_mix_pre, v_norm_mix_post=v_norm_mix_post, v_norm_mlp_pre=v_norm_mlp_pre, v_norm_mlp_post=v_norm_mlp_post, v_w_in=v_w_in, v_ssm_a_re=v_ssm_a_re, v_ssm_a_im=v_ssm_a_im, v_ssm_log_dt=v_ssm_log_dt, v_ssm_b_re=v_ssm_b_re, v_ssm_b_im=v_ssm_b_im, v_ssm_c_re=v_ssm_c_re, v_ssm_c_im=v_ssm_c_im, v_ssm_d=v_ssm_d, v_w_glu=v_w_glu, v_b_glu=v_b_glu, v_attn_sinks=v_attn_sinks, v_w_o_ssm=v_w_o_ssm, v_w_o_attn=v_w_o_attn, v_w_out=v_w_out, v_w_up=v_w_up, v_w_down=v_w_down)
    weights = {n: given[n] for n in TWIN_WEIGHTS}
    shared = {n: given[n] for n in SHARED_INPUTS}
    per_example = {n: given[n] for n in ['x']}
    grad_fn = _jax.value_and_grad(_loss, argnums=(0, 1))

    def one_microbatch(ex, loss_target):
        ex = dict(ex)
        diff = ex.pop(TWIN_DIFF_INPUT)
        return grad_fn(weights, diff, {**shared, **ex}, loss_target)

    if N_MICROBATCH == 1:
        loss, (grad_w, grad_x) = one_microbatch(per_example, given["loss_target"])
    else:
        def body(carry, xs):
            loss_sum, grad_sum = carry
            l_k, (gw_k, gx_k) = one_microbatch(xs[0], xs[1])
            with _jax.named_scope("update"):
                return (loss_sum + l_k, _jax.tree.map(_jnp.add, grad_sum, gw_k)), gx_k

        init = (_jnp.zeros((), _jnp.float32), _jax.tree.map(_jnp.zeros_like, weights))
        (loss, grad_w), grad_x = _jax.lax.scan(body, init, (per_example, given["loss_target"]))
    with _jax.named_scope("update"):
        delta_w, new_m, new_v = {}, {}, {}
        for n in TWIN_WEIGHTS:
            delta_w[n], new_m[n], new_v[n] = _adamw(weights[n], grad_w[n], given["m_" + n], given["v_" + n])
    return (loss, grad_x, *[grad_w[n] for n in TWIN_WEIGHTS], *[delta_w[n] for n in TWIN_WEIGHTS],
            *[new_m[n] for n in TWIN_WEIGHTS], *[new_v[n] for n in TWIN_WEIGHTS])
```

```python
import functools
import math

import jax
import jax.numpy as jnp
from jax import lax
from jax.experimental import pallas as pl
from jax.experimental.pallas import tpu as pltpu

F32 = jnp.float32
MXU_DTYPE = jnp.bfloat16
_pcall = pl.pallas_call
SDS = jax.ShapeDtypeStruct

D = 1024
D_SSM = 512
D_ATTN = 1024
D_KV = 256
D_FF = 4096
D_IN = 4096
HEAD_DIM = 64
N_Q_HEADS = 16
N_KV_HEADS = 4
Q_PER_KV = 4
N_META = 16
BLK = 128
PAD_ROWS = BLK - N_META
N_GROUPS = 32
N_STATE = 64
GROUP_CH = 16
N_SB = 4
SB_STATES = 512
ROPE_THETA = 10000.0
ATTN_SCALE = HEAD_DIM ** -0.5
NEG_INF = -1e30
RMS_EPS = 1e-6
N_DEV = 8
COL_SHARD = 512

ADAM_LR = 0.001
ADAM_B1 = 0.9
ADAM_B2 = 0.999
ADAM_EPS = 1e-08
ADAM_WD = 0.01
ADAM_STEP = 10

VMEM_LIMIT = 56 * 1024 * 1024
MESH_AXES = ("x", "y", "c")

_NT = (((1,), (1,)), ((), ()))
_TN = (((0,), (0,)), ((), ()))


def _cparams(*sem):
    return pltpu.CompilerParams(dimension_semantics=tuple(sem) if sem else None,
                                vmem_limit_bytes=VMEM_LIMIT)


def _row_tile(rows, cap=640):
    for t in (640, 512, 320, 256, 128):
        if t <= cap and rows % t == 0:
            return t
    raise ValueError(f"unsupported row count {rows}")


def _dot(a, b):
    return jnp.dot(a, b, preferred_element_type=F32)


def _dot_nt(a, b):
    return lax.dot_general(a, b, _NT, preferred_element_type=F32)


def _dot_tn(a, b):
    return lax.dot_general(a, b, _TN, preferred_element_type=F32)


def _sigmoid(x):
    return 1.0 / (1.0 + jnp.exp(-x))


_GELU_C = math.sqrt(2.0 / math.pi)


def _gelu_parts(y):
    t = jnp.tanh(_GELU_C * (y + 0.044715 * (y * y * y)))
    return 0.5 * y * (1.0 + t), t


def _gelu_grad(y, t):
    return 0.5 * (1.0 + t) + 0.5 * y * (1.0 - t * t) * (_GELU_C * (1.0 + 0.134145 * (y * y)))


def _rms_fwd(x, gain):
    r = lax.rsqrt(jnp.mean(x * x, axis=-1, keepdims=True) + RMS_EPS)
    return (x * r) * gain


def _rms_bwd(x, gain, dout):
    r = lax.rsqrt(jnp.mean(x * x, axis=-1, keepdims=True) + RMS_EPS)
    xh = x * r
    dxh = dout * gain
    dx = r * (dxh - xh * jnp.mean(dxh * xh, axis=-1, keepdims=True))
    return dx, jnp.sum(dout * xh, axis=0, keepdims=True)


def _mesh_pos():
    return lax.axis_index("x"), lax.axis_index("y"), lax.axis_index("c")


def _peer(pos, d):
    x, y, c = pos
    return (1 - x if d & 4 else x, 1 - y if d & 2 else y, 1 - c if d & 1 else c)


def _slot(pos):
    return 4 * pos[0] + 2 * pos[1] + pos[2]


def _all_gather(shards, name):
    n = len(shards)

    def body(*refs):
        ins, outs = refs[:n], refs[n:2 * n]
        send_sems, recv_sems, local_sems = refs[2 * n:]
        me = _mesh_pos()
        my_slot = _slot(me)
        local = [pltpu.make_async_copy(ins[k], outs[k].at[my_slot], local_sems.at[k]) for k in range(n)]
        for cp in local:
            cp.start()
        sends = []
        for k in range(n):
            for d in range(1, N_DEV):
                sends.append(pltpu.make_async_remote_copy(
                    src_ref=ins[k], dst_ref=outs[k].at[my_slot],
                    send_sem=send_sems.at[k, d - 1], recv_sem=recv_sems.at[k, d - 1],
                    device_id=_peer(me, d), device_id_type=pl.DeviceIdType.MESH))
        for cp in sends:
            cp.start()
        for k in range(n):
            for d in range(1, N_DEV):
                pltpu.make_async_remote_copy(
                    src_ref=ins[k], dst_ref=outs[k].at[_slot(_peer(me, d))],
                    send_sem=send_sems.at[k, d - 1], recv_sem=recv_sems.at[k, d - 1],
                    device_id=_peer(me, d), device_id_type=pl.DeviceIdType.MESH).wait_recv()
        for cp in sends:
            cp.wait_send()
        for cp in local:
            cp.wait()

    any_spec = pl.BlockSpec(memory_space=pl.ANY)
    return _pcall(
        body, name=name,
        out_shape=[SDS((N_DEV,) + s.shape, s.dtype) for s in shards],
        in_specs=[any_spec] * n, out_specs=[any_spec] * n,
        scratch_shapes=[pltpu.SemaphoreType.DMA((n, N_DEV - 1)), pltpu.SemaphoreType.DMA((n, N_DEV - 1)),
                        pltpu.SemaphoreType.DMA((n,))],
    )(*shards)


def _exchange_slots(parts, name):
    n = len(parts)

    def body(*refs):
        ins, outs = refs[:n], refs[n:2 * n]
        send_sems, recv_sems, local_sems = refs[2 * n:]
        me = _mesh_pos()
        my_slot = _slot(me)
        local = [pltpu.make_async_copy(ins[k].at[my_slot], outs[k].at[my_slot], local_sems.at[k]) for k in range(n)]
        for cp in local:
            cp.start()
        sends = []
        for k in range(n):
            for d in range(1, N_DEV):
                sends.append(pltpu.make_async_remote_copy(
                    src_ref=ins[k].at[_slot(_peer(me, d))], dst_ref=outs[k].at[my_slot],
                    send_sem=send_sems.at[k, d - 1], recv_sem=recv_sems.at[k, d - 1],
                    device_id=_peer(me, d), device_id_type=pl.DeviceIdType.MESH))
        for cp in sends:
            cp.start()
        for k in range(n):
            for d in range(1, N_DEV):
                pltpu.make_async_remote_copy(
                    src_ref=ins[k].at[my_slot], dst_ref=outs[k].at[_slot(_peer(me, d))],
                    send_sem=send_sems.at[k, d - 1], recv_sem=recv_sems.at[k, d - 1],
                    device_id=_peer(me, d), device_id_type=pl.DeviceIdType.MESH).wait_recv()
        for cp in sends:
            cp.wait_send()
        for cp in local:
            cp.wait()

    any_spec = pl.BlockSpec(memory_space=pl.ANY)
    return _pcall(
        body, name=name,
        out_shape=[SDS(p.shape, p.dtype) for p in parts],
        in_specs=[any_spec] * n, out_specs=[any_spec] * n,
        scratch_shapes=[pltpu.SemaphoreType.DMA((n, N_DEV - 1)), pltpu.SemaphoreType.DMA((n, N_DEV - 1)),
                        pltpu.SemaphoreType.DMA((n,))],
    )(*parts)


def _all_reduce_small(packed, name):
    rows = packed.shape[0]

    def body(x_ref, out_ref, gath, send_sems, recv_sems):
        me = _mesh_pos()
        my_slot = _slot(me)
        gath[my_slot] = x_ref[...]
        sends = [pltpu.make_async_remote_copy(
            src_ref=x_ref, dst_ref=gath.at[my_slot],
            send_sem=send_sems.at[d - 1], recv_sem=recv_sems.at[d - 1],
            device_id=_peer(me, d), device_id_type=pl.DeviceIdType.MESH) for d in range(1, N_DEV)]
        for cp in sends:
            cp.start()
        for d in range(1, N_DEV):
            pltpu.make_async_remote_copy(
                src_ref=x_ref, dst_ref=gath.at[_slot(_peer(me, d))],
                send_sem=send_sems.at[d - 1], recv_sem=recv_sems.at[d - 1],
                device_id=_peer(me, d), device_id_type=pl.DeviceIdType.MESH).wait_recv()
        for cp in sends:
            cp.wait_send()
        acc = gath[0]
        for s in range(1, N_DEV):
            acc = acc + gath[s]
        out_ref[...] = acc

    vmem = pl.BlockSpec(memory_space=pltpu.VMEM)
    return _pcall(
        body, name=name, out_shape=SDS(packed.shape, F32), in_specs=[vmem], out_specs=vmem,
        scratch_shapes=[pltpu.VMEM((N_DEV, rows, 128), F32),
                        pltpu.SemaphoreType.DMA((N_DEV - 1,)), pltpu.SemaphoreType.DMA((N_DEV - 1,))],
        compiler_params=_cparams(),
    )(packed)


def _in_proj(hres, gain3, w_in_g, layer):
    rows = hres.shape[0]
    tm = _row_tile(rows)

    def body(x_ref, g_ref, w_ref, proj_ref, h_ref, h_scr):
        @pl.when(pl.program_id(1) == 0)
        def _():
            hn = _rms_fwd(x_ref[...], g_ref[...]).astype(MXU_DTYPE)
            h_scr[...] = hn
            h_ref[...] = hn

        proj_ref[...] = _dot(h_scr[...], w_ref[...])

    return _pcall(
        body, name=f"in_proj_l{layer}", grid=(rows // tm, N_DEV),
        in_specs=[pl.BlockSpec((tm, D), lambda i, j: (i, 0)),
                  pl.BlockSpec((None, 1, D), lambda i, j: (layer, 0, 0)),
                  pl.BlockSpec((None, None, D, COL_SHARD), lambda i, j: (j, layer, 0, 0))],
        out_specs=[pl.BlockSpec((tm, COL_SHARD), lambda i, j: (i, j)),
                   pl.BlockSpec((tm, D), lambda i, j: (i, 0))],
        out_shape=[SDS((rows, D_IN), F32), SDS((rows, D), MXU_DTYPE)],
        scratch_shapes=[pltpu.VMEM((tm, D), MXU_DTYPE)],
        compiler_params=_cparams("parallel", "arbitrary"),
    )(hres, gain3, w_in_g)


def _rope_lanes(t, cos, sin_a, sin_b):
    return t * cos + pltpu.roll(t, 96, 1) * sin_a + pltpu.roll(t, 32, 1) * sin_b


def _rope_fwd(proj, cos, sin_a, sin_b, layer):
    rows = proj.shape[0]
    tm = _row_tile(rows)

    def body(q0_ref, q1_ref, kv_ref, c_ref, a_ref, b_ref, qo_ref, ko_ref, vo_ref):
        c, a, b = c_ref[...], a_ref[...], b_ref[...]
        for half, q_ref in enumerate((q0_ref, q1_ref)):
            for t in range(4):
                x = q_ref[:, t * 128:(t + 1) * 128]
                lo = half * 512 + t * 128
                qo_ref[:, lo:lo + 128] = (_rope_lanes(x, c, a, b) * ATTN_SCALE).astype(MXU_DTYPE)
        for t in range(2):
            x = kv_ref[:, t * 128:(t + 1) * 128]
            ko_ref[:, t * 128:(t + 1) * 128] = _rope_lanes(x, c, a, b).astype(MXU_DTYPE)
        vo_ref[...] = kv_ref[:, D_KV:2 * D_KV].astype(MXU_DTYPE)

    tab = pl.BlockSpec((tm, 128), lambda i: (i, 0))
    return _pcall(
        body, name=f"rope_fwd_l{layer}", grid=(rows // tm,),
        in_specs=[pl.BlockSpec((tm, 512), lambda i: (i, 1)), pl.BlockSpec((tm, 512), lambda i: (i, 2)),
                  pl.BlockSpec((tm, 512), lambda i: (i, 3)), tab, tab, tab],
        out_specs=[pl.BlockSpec((tm, D_ATTN), lambda i: (i, 0)), pl.BlockSpec((tm, D_KV), lambda i: (i, 0)),
                   pl.BlockSpec((tm, D_KV), lambda i: (i, 0))],
        out_shape=[SDS((rows, D_ATTN), MXU_DTYPE), SDS((rows, D_KV), MXU_DTYPE), SDS((rows, D_KV), MXU_DTYPE)],
        compiler_params=_cparams("parallel"),
    )(proj, proj, proj, cos, sin_a, sin_b)


def _scan_rows(xr, xi, pw_re, pw_im, reverse):
    n = xr.shape[0]
    row = lax.broadcasted_iota(jnp.int32, xr.shape, 0)
    for k in range(n.bit_length() - 1):
        s = 1 << k
        ar = pw_re[k:k + 1, :]
        ai = pw_im[k:k + 1, :]
        if reverse:
            ai = -ai
            keep = row < (n - s)
            shift = n - s
        else:
            keep = row >= s
            shift = s
        sr = jnp.where(keep, pltpu.roll(xr, shift, 0), 0.0)
        si = jnp.where(keep, pltpu.roll(xi, shift, 0), 0.0)
        xr, xi = xr + (ar * sr - ai * si), xi + (ar * si + ai * sr)
    return xr, xi


def _s5_fwd(proj, ssm, w_glu, b_glu3, layer):
    rows = proj.shape[0]
    n_chunks = rows // BLK
    b_mat, c_mat, pw_re, pw_im, p_re, p_im, d_skip = (ssm[k] for k in ("b_mat", "c_mat", "pw_re", "pw_im", "p_re", "p_im", "d_skip"))

    def body(u_ref, bm_ref, cm_ref, pwr_ref, pwi_ref, pr_ref, pi_ref, d_ref, wg_ref, bg_ref,
             y_ref, ys_ref, cin_ref, carry):
        @pl.when(pl.program_id(0) == 0)
        def _():
            carry[...] = jnp.zeros_like(carry)

        cin_ref[...] = carry[...]
        u = u_ref[...]
        for sb in range(N_SB):
            u_sb = u[:, sb * 128:(sb + 1) * 128]
            bu = _dot(u_sb.astype(MXU_DTYPE), bm_ref[sb])
            sr, si = _scan_rows(bu[:, :SB_STATES], bu[:, SB_STATES:], pwr_ref[sb], pwi_ref[sb], False)
            cr = carry[2 * sb:2 * sb + 1, :]
            ci = carry[2 * sb + 1:2 * sb + 2, :]
            pr, pi = pr_ref[sb], pi_ref[sb]
            sr = sr + (pr * cr - pi * ci)
            si = si + (pr * ci + pi * cr)
            carry[2 * sb:2 * sb + 1, :] = sr[BLK - 1:BLK, :]
            carry[2 * sb + 1:2 * sb + 2, :] = si[BLK - 1:BLK, :]
            s_cat = jnp.concatenate([sr, si], axis=1).astype(MXU_DTYPE)
            y_ref[:, sb * 128:(sb + 1) * 128] = _dot(s_cat, cm_ref[sb]) + d_ref[:, sb * 128:(sb + 1) * 128] * u_sb
        z, _ = _gelu_parts(y_ref[...])
        gl = _dot(z.astype(MXU_DTYPE), wg_ref[...]) + bg_ref[...]
        ys_ref[...] = (z * _sigmoid(gl)).astype(MXU_DTYPE)

    full = lambda shape: pl.BlockSpec(shape, lambda j: (0,) * len(shape))
    return _pcall(
        body, name=f"s5_fwd_l{layer}", grid=(n_chunks,),
        in_specs=[pl.BlockSpec((BLK, D_SSM), lambda j: (j, 0)),
                  full((N_SB, 128, 2 * SB_STATES)), full((N_SB, 2 * SB_STATES, 128)),
                  full((N_SB, 8, SB_STATES)), full((N_SB, 8, SB_STATES)),
                  full((N_SB, BLK, SB_STATES)), full((N_SB, BLK, SB_STATES)),
                  full((1, D_SSM)), full((D_SSM, D_SSM)),
                  pl.BlockSpec((None, 1, D_SSM), lambda j: (layer, 0, 0))],
        out_specs=[pl.BlockSpec((BLK, D_SSM), lambda j: (j, 0)), pl.BlockSpec((BLK, D_SSM), lambda j: (j, 0)),
                   pl.BlockSpec((None, 8, SB_STATES), lambda j: (j, 0, 0))],
        out_shape=[SDS((rows, D_SSM), F32), SDS((rows, D_SSM), MXU_DTYPE), SDS((n_chunks, 8, SB_STATES), F32)],
        scratch_shapes=[pltpu.VMEM((8, SB_STATES), F32)],
        compiler_params=_cparams("arbitrary"),
    )(proj, b_mat, c_mat, pw_re, pw_im, p_re, p_im, d_skip, w_glu, b_glu3)


def _attn_mask(i):
    row = lax.broadcasted_iota(jnp.int32, (BLK, 3 * BLK), 0) + i * BLK
    col = lax.broadcasted_iota(jnp.int32, (BLK, 3 * BLK), 1)
    seg = jnp.right_shift(col, 7)
    c = jnp.bitwise_and(col, BLK - 1)
    kidx = c + (i + seg - 2) * BLK
    ok_meta = (seg == 0) & (c >= PAD_ROWS) & (row - c >= BLK)
    ok_win = (seg > 0) & (kidx >= PAD_ROWS) & (kidx <= row) & (row - kidx < BLK)
    return ok_meta | ok_win


def _attn_probs(q_h, k3, ok, sink):
    s = jnp.where(ok, _dot_nt(q_h, k3), NEG_INF)
    m = jnp.maximum(jnp.max(s, axis=-1, keepdims=True), sink)
    e = jnp.exp(s - m)
    e_sink = jnp.exp(sink - m)
    inv = 1.0 / (jnp.sum(e, axis=-1, keepdims=True) + e_sink)
    return e * inv, e_sink * inv


def _attn_fwd(q, k, v, sinks, layer):
    rows = q.shape[0]
    n_blk = rows // BLK

    def body(sink_ref, q_ref, km_ref, kp_ref, kc_ref, vm_ref, vp_ref, vc_ref, o_ref):
        i = pl.program_id(0)
        ok = _attn_mask(i)
        for kvh in range(N_KV_HEADS):
            lanes = slice(kvh * HEAD_DIM, (kvh + 1) * HEAD_DIM)
            k3 = jnp.concatenate([km_ref[:, lanes], kp_ref[:, lanes], kc_ref[:, lanes]], axis=0)
            v3 = jnp.concatenate([vm_ref[:, lanes], vp_ref[:, lanes], vc_ref[:, lanes]], axis=0)
            for g in range(Q_PER_KV):
                h = kvh * Q_PER_KV + g
                hl = slice(h * HEAD_DIM, (h + 1) * HEAD_DIM)
                p, _ = _attn_probs(q_ref[:, hl], k3, ok, sink_ref[layer, h])
                o_ref[:, hl] = _dot(p.astype(MXU_DTYPE), v3).astype(MXU_DTYPE)

    kv_meta = pl.BlockSpec((BLK, D_KV), lambda i: (0, 0))
    kv_prev = pl.BlockSpec((BLK, D_KV), lambda i: (jnp.maximum(i - 1, 0), 0))
    kv_cur = pl.BlockSpec((BLK, D_KV), lambda i: (i, 0))
    return _pcall(
        body, name=f"attn_fwd_l{layer}", grid=(n_blk,),
        in_specs=[pl.BlockSpec(memory_space=pltpu.SMEM),
                  pl.BlockSpec((BLK, D_ATTN), lambda i: (i, 0)),
                  kv_meta, kv_prev, kv_cur, kv_meta, kv_prev, kv_cur],
        out_specs=pl.BlockSpec((BLK, D_ATTN), lambda i: (i, 0)),
        out_shape=SDS((rows, D_ATTN), MXU_DTYPE),
        compiler_params=_cparams("parallel"),
    )(sinks, q, k, k, k, v, v, v)


def _merge_fwd(y_ssm, y_attn, proj, hres, w_o_ssm, w_o_attn, w_out, gain3, layer):
    rows = hres.shape[0]
    tm = _row_tile(rows, 320)

    def body(ys_ref, ya_ref, gs_ref, ga_ref, x_ref, wos_ref, woa_ref, wout_ref, g_ref,
             mg_ref, mix_ref, out_ref):
        a1 = _dot(ys_ref[...], wos_ref[...])
        a2 = _dot(ya_ref[...], woa_ref[...])
        merged = (_sigmoid(gs_ref[...]) * a1 + _sigmoid(ga_ref[...]) * a2).astype(MXU_DTYPE)
        mg_ref[...] = merged
        mix = _dot(merged, wout_ref[...])
        mix_ref[...] = mix
        out_ref[...] = x_ref[...] + _rms_fwd(mix, g_ref[...])

    row_d = pl.BlockSpec((tm, D), lambda i: (i, 0))
    full = lambda shape: pl.BlockSpec(shape, lambda i: (0,) * len(shape))
    return _pcall(
        body, name=f"merge_fwd_l{layer}", grid=(rows // tm,),
        in_specs=[pl.BlockSpec((tm, D_SSM), lambda i: (i, 0)), row_d,
                  pl.BlockSpec((tm, D), lambda i: (i, 2)), pl.BlockSpec((tm, D), lambda i: (i, 3)), row_d,
                  full((D_SSM, D)), full((D_ATTN, D)), full((D, D)),
                  pl.BlockSpec((None, 1, D), lambda i: (layer, 0, 0))],
        out_specs=[row_d, row_d, row_d],
        out_shape=[SDS((rows, D), MXU_DTYPE), SDS((rows, D), F32), SDS((rows, D), F32)],
        compiler_params=_cparams("parallel"),
    )(y_ssm, y_attn, proj, proj, hres, w_o_ssm, w_o_attn, w_out, gain3)


def _mlp_fwd(hres, gain_pre3, gain_post3, w_up_g, w_down_g, layer):
    rows = hres.shape[0]
    tm = _row_tile(rows)

    def body(x_ref, gp_ref, gq_ref, wu_ref, wd_ref, up_ref, act_ref, h_ref, ff_ref, out_ref, h_scr, acc):
        kf = pl.program_id(1)

        @pl.when(kf == 0)
        def _():
            hn = _rms_fwd(x_ref[...], gp_ref[...]).astype(MXU_DTYPE)
            h_scr[...] = hn
            h_ref[...] = hn
            acc[...] = jnp.zeros_like(acc)

        up = _dot(h_scr[...], wu_ref[...])
        up_ref[...] = up
        r = jnp.maximum(up, 0.0)
        act = (r * r).astype(MXU_DTYPE)
        act_ref[...] = act
        acc[...] += _dot(act, wd_ref[...])

        @pl.when(kf == N_DEV - 1)
        def _():
            ff = acc[...]
            ff_ref[...] = ff
            out_ref[...] = x_ref[...] + _rms_fwd(ff, gq_ref[...])

    row_d = pl.BlockSpec((tm, D), lambda i, k: (i, 0))
    gain = pl.BlockSpec((None, 1, D), lambda i, k: (layer, 0, 0))
    return _pcall(
        body, name=f"mlp_fwd_l{layer}", grid=(rows // tm, N_DEV),
        in_specs=[row_d, gain, gain,
                  pl.BlockSpec((None, None, D, COL_SHARD), lambda i, k: (k, layer, 0, 0)),
                  pl.BlockSpec((None, None, COL_SHARD, D), lambda i, k: (k, layer, 0, 0))],
        out_specs=[pl.BlockSpec((tm, COL_SHARD), lambda i, k: (i, k)),
                   pl.BlockSpec((tm, COL_SHARD), lambda i, k: (i, k)), row_d, row_d, row_d],
        out_shape=[SDS((rows, D_FF), F32), SDS((rows, D_FF), MXU_DTYPE), SDS((rows, D), MXU_DTYPE),
                   SDS((rows, D), F32), SDS((rows, D), F32)],
        scratch_shapes=[pltpu.VMEM((tm, D), MXU_DTYPE), pltpu.VMEM((tm, D), F32)],
        compiler_params=_cparams("parallel", "arbitrary"),
    )(hres, gain_pre3, gain_post3, w_up_g, w_down_g)


def _loss_and_grad(hres, target):
    rows = hres.shape[0]
    n_blk = rows // BLK

    def body(y_ref, t_ref, dy_ref, loss_ref):
        i = pl.program_id(0)

        @pl.when(i == 0)
        def _():
            dy_ref[...] = jnp.zeros_like(dy_ref)
            loss_ref[...] = jnp.zeros_like(loss_ref)

        @pl.when(i > 0)
        def _():
            err = y_ref[...] - t_ref[...]
            dy_ref[...] = err * (1.0 / D)
            loss_ref[...] += jnp.sum(err * err) * (0.5 / D)

    return _pcall(
        body, name="loss", grid=(n_blk,),
        in_specs=[pl.BlockSpec((BLK, D), lambda i: (i, 0)),
                  pl.BlockSpec((BLK, D), lambda i: (jnp.maximum(i - 1, 0), 0))],
        out_specs=[pl.BlockSpec((BLK, D), lambda i: (i, 0)), pl.BlockSpec((1, 128), lambda i: (0, 0))],
        out_shape=[SDS((rows, D), F32), SDS((1, 128), F32)],
        compiler_params=_cparams("arbitrary"),
    )(hres, target)


def _matmul_tn(a, b, name, dev_major_cols=None):
    rows, ka = a.shape
    n = b.shape[1]
    ta = min(ka, 1024)
    tn = min(n, 1024)
    tr = _row_tile(rows)
    n_r = rows // tr

    def body(a_ref, b_ref, o_ref, acc):
        r = pl.program_id(2)

        @pl.when(r == 0)
        def _():
            acc[...] = jnp.zeros_like(acc)

        acc[...] += _dot_tn(a_ref[...], b_ref[...])

        @pl.when(r == n_r - 1)
        def _():
            if dev_major_cols is None:
                o_ref[...] = acc[...]
            else:
                for s in range(tn // dev_major_cols):
                    o_ref[s] = acc[:, s * dev_major_cols:(s + 1) * dev_major_cols]

    if dev_major_cols is None:
        out_spec = pl.BlockSpec((ta, tn), lambda i, j, r: (i, j))
        out_shape = SDS((ka, n), F32)
    else:
        w = dev_major_cols
        out_spec = pl.BlockSpec((tn // w, ta, w), lambda i, j, r: (j, i, 0))
        out_shape = SDS((n // w, ka, w), F32)
    return _pcall(
        body, name=name, grid=(ka // ta, n // tn, n_r),
        in_specs=[pl.BlockSpec((tr, ta), lambda i, j, r: (r, i)), pl.BlockSpec((tr, tn), lambda i, j, r: (r, j))],
        out_specs=out_spec, out_shape=out_shape,
        scratch_shapes=[pltpu.VMEM((ta, tn), F32)],
        compiler_params=_cparams("parallel", "parallel", "arbitrary"),
    )(a, b)


def _mlp_bwd(dout, ff, up, hres_mid, gain_pre3, gain_post3, w_up_g, w_down_g, layer):
    rows = dout.shape[0]
    tm = _row_tile(rows)

    def body(do_ref, ff_ref, up_ref, x_ref, gp_ref, gq_ref, wu_ref, wd_ref,
             dff_ref, dup_ref, dx_ref, dgq_ref, dgp_ref, dff_scr, acc):
        i = pl.program_id(0)
        kf = pl.program_id(1)

        @pl.when((i == 0) & (kf == 0))
        def _():
            dgq_ref[...] = jnp.zeros_like(dgq_ref)
            dgp_ref[...] = jnp.zeros_like(dgp_ref)

        @pl.when(kf == 0)
        def _():
            dff, dg = _rms_bwd(ff_ref[...], gq_ref[...], do_ref[...])
            dgq_ref[...] += dg
            dffb = dff.astype(MXU_DTYPE)
            dff_scr[...] = dffb
            dff_ref[...] = dffb
            acc[...] = jnp.zeros_like(acc)

        dact = _dot_nt(dff_scr[...], wd_ref[...])
        dup = (dact * (2.0 * jnp.maximum(up_ref[...], 0.0))).astype(MXU_DTYPE)
        dup_ref[...] = dup
        acc[...] += _dot_nt(dup, wu_ref[...])

        @pl.when(kf == N_DEV - 1)
        def _():
            dx, dg = _rms_bwd(x_ref[...], gp_ref[...], acc[...])
            dgp_ref[...] += dg
            dx_ref[...] = do_ref[...] + dx

    row_d = pl.BlockSpec((tm, D), lambda i, k: (i, 0))
    gain = pl.BlockSpec((None, 1, D), lambda i, k: (layer, 0, 0))
    dgain = pl.BlockSpec((1, D), lambda i, k: (0, 0))
    return _pcall(
        body, name=f"mlp_bwd_l{layer}", grid=(rows // tm, N_DEV),
        in_specs=[row_d, row_d, pl.BlockSpec((tm, COL_SHARD), lambda i, k: (i, k)), row_d, gain, gain,
                  pl.BlockSpec((None, None, D, COL_SHARD), lambda i, k: (k, layer, 0, 0)),
                  pl.BlockSpec((None, None, COL_SHARD, D), lambda i, k: (k, layer, 0, 0))],
        out_specs=[row_d, pl.BlockSpec((tm, COL_SHARD), lambda i, k: (i, k)), row_d, dgain, dgain],
        out_shape=[SDS((rows, D), MXU_DTYPE), SDS((rows, D_FF), MXU_DTYPE), SDS((rows, D), F32),
                   SDS((1, D), F32), SDS((1, D), F32)],
        scratch_shapes=[pltpu.VMEM((tm, D), MXU_DTYPE), pltpu.VMEM((tm, D), F32)],
        compiler_params=_cparams("arbitrary", "arbitrary"),
    )(dout, ff, up, hres_mid, gain_pre3, gain_post3, w_up_g, w_down_g)


def _merge_bwd(dhm, mix, y_ssm, y_attn, proj, w_o_ssm, w_o_attn, w_out, gain3, layer):
    rows = dhm.shape[0]
    tm = _row_tile(rows, 320)

    def body(dh_ref, mix_ref, ys_ref, ya_ref, gs_ref, ga_ref, wos_ref, woa_ref, wout_ref, g_ref,
             dmix_ref, da1_ref, da2_ref, dgs_ref, dga_ref, dys_ref, dya_ref, dg_ref):
        @pl.when(pl.program_id(0) == 0)
        def _():
            dg_ref[...] = jnp.zeros_like(dg_ref)

        dmix, dg = _rms_bwd(mix_ref[...], g_ref[...], dh_ref[...])
        dg_ref[...] += dg
        dmixb = dmix.astype(MXU_DTYPE)
        dmix_ref[...] = dmixb
        dmerged = _dot_nt(dmixb, wout_ref[...])
        sg_s = _sigmoid(gs_ref[...])
        sg_a = _sigmoid(ga_ref[...])
        da1 = (dmerged * sg_s).astype(MXU_DTYPE)
        da2 = (dmerged * sg_a).astype(MXU_DTYPE)
        da1_ref[...] = da1
        da2_ref[...] = da2
        a1 = _dot(ys_ref[...], wos_ref[...])
        a2 = _dot(ya_ref[...], woa_ref[...])
        dgs_ref[...] = (dmerged * a1 * (sg_s * (1.0 - sg_s))).astype(MXU_DTYPE)
        dga_ref[...] = (dmerged * a2 * (sg_a * (1.0 - sg_a))).astype(MXU_DTYPE)
        dys_ref[...] = _dot_nt(da1, wos_ref[...])
        dya_ref[...] = _dot_nt(da2, woa_ref[...])

    row_d = pl.BlockSpec((tm, D), lambda i: (i, 0))
    full = lambda shape: pl.BlockSpec(shape, lambda i: (0,) * len(shape))
    return _pcall(
        body, name=f"merge_bwd_l{layer}", grid=(rows // tm,),
        in_specs=[row_d, row_d, pl.BlockSpec((tm, D_SSM), lambda i: (i, 0)), row_d,
                  pl.BlockSpec((tm, D), lambda i: (i, 2)), pl.BlockSpec((tm, D), lambda i: (i, 3)),
                  full((D_SSM, D)), full((D_ATTN, D)), full((D, D)),
                  pl.BlockSpec((None, 1, D), lambda i: (layer, 0, 0))],
        out_specs=[row_d, row_d, row_d, row_d, row_d, pl.BlockSpec((tm, D_SSM), lambda i: (i, 0)), row_d,
                   pl.BlockSpec((1, D), lambda i: (0, 0))],
        out_shape=[SDS((rows, D), MXU_DTYPE)] * 5 + [SDS((rows, D_SSM), F32), SDS((rows, D_ATTN), F32),
                                                      SDS((1, D), F32)],
        compiler_params=_cparams("arbitrary"),
    )(dhm, mix, y_ssm, y_attn, proj, proj, w_o_ssm, w_o_attn, w_out, gain3)


def _attn_bwd(q, k, v, d_out, sinks, layer):
    rows = q.shape[0]
    n_blk = rows // BLK
    last = n_blk - 1

    def body(sink_ref, q_ref, km_ref, kp_ref, kc_ref, vm_ref, vp_ref, vc_ref, do_ref,
             dq_ref, dk_ref, dv_ref, dkm_ref, dvm_ref, ds_ref, dk_carry, dv_carry):
        i = pl.program_id(0)

        @pl.when(i == 0)
        def _():
            dkm_ref[...] = jnp.zeros_like(dkm_ref)
            dvm_ref[...] = jnp.zeros_like(dvm_ref)
            ds_ref[...] = jnp.zeros_like(ds_ref)
            dk_carry[...] = jnp.zeros_like(dk_carry)
            dv_carry[...] = jnp.zeros_like(dv_carry)

        @pl.when(i <= last)
        def _():
            ok = _attn_mask(i)
            lane = lax.broadcasted_iota(jnp.int32, (1, 128), 1)
            dsink = jnp.zeros((1, 128), F32)
            for kvh in range(N_KV_HEADS):
                lanes = slice(kvh * HEAD_DIM, (kvh + 1) * HEAD_DIM)
                k3 = jnp.concatenate([km_ref[:, lanes], kp_ref[:, lanes], kc_ref[:, lanes]], axis=0)
                v3 = jnp.concatenate([vm_ref[:, lanes], vp_ref[:, lanes], vc_ref[:, lanes]], axis=0)
                dk3 = jnp.zeros((3 * BLK, HEAD_DIM), F32)
                dv3 = jnp.zeros((3 * BLK, HEAD_DIM), F32)
                for g in range(Q_PER_KV):
                    h = kvh * Q_PER_KV + g
                    hl = slice(h * HEAD_DIM, (h + 1) * HEAD_DIM)
                    q_h = q_ref[:, hl]
                    do_h = do_ref[:, hl].astype(MXU_DTYPE)
                    p, p_sink = _attn_probs(q_h, k3, ok, sink_ref[layer, h])
                    dp = _dot_nt(do_h, v3)
                    delta = jnp.sum(dp * p, axis=-1, keepdims=True)
                    dsc = (p * (dp - delta)).astype(MXU_DTYPE)
                    dsink = dsink + jnp.where(lane == h, -jnp.sum(p_sink * delta), 0.0)
                    dv3 = dv3 + _dot_tn(p.astype(MXU_DTYPE), do_h)
                    dk3 = dk3 + _dot_tn(dsc, q_h)
                    dq_ref[:, hl] = _dot(dsc, k3)
                dkm_ref[:, lanes] += dk3[0:BLK]
                dvm_ref[:, lanes] += dv3[0:BLK]
                dk_ref[:, lanes] = dk_carry[:, lanes] + dk3[BLK:2 * BLK]
                dv_ref[:, lanes] = dv_carry[:, lanes] + dv3[BLK:2 * BLK]
                dk_carry[:, lanes] = dk3[2 * BLK:3 * BLK]
                dv_carry[:, lanes] = dv3[2 * BLK:3 * BLK]
            ds_ref[...] += dsink

        @pl.when(i == last + 1)
        def _():
            dk_ref[...] = dk_carry[...]
            dv_ref[...] = dv_carry[...]

    cur = lambda i: (jnp.minimum(i, last), 0)
    prev = lambda i: (jnp.clip(i - 1, 0, last), 0)
    kv_meta = pl.BlockSpec((BLK, D_KV), lambda i: (0, 0))
    kv_prev = pl.BlockSpec((BLK, D_KV), prev)
    kv_cur = pl.BlockSpec((BLK, D_KV), cur)
    return _pcall(
        body, name=f"attn_bwd_l{layer}", grid=(n_blk + 1,),
        in_specs=[pl.BlockSpec(memory_space=pltpu.SMEM),
                  pl.BlockSpec((BLK, D_ATTN), cur),
                  kv_meta, kv_prev, kv_cur, kv_meta, kv_prev, kv_cur,
                  pl.BlockSpec((BLK, D_ATTN), cur)],
        out_specs=[pl.BlockSpec((BLK, D_ATTN), cur), kv_prev, kv_prev, kv_meta, kv_meta,
                   pl.BlockSpec((1, 128), lambda i: (0, 0))],
        out_shape=[SDS((rows, D_ATTN), F32), SDS((rows, D_KV), F32), SDS((rows, D_KV), F32),
                   SDS((BLK, D_KV), F32), SDS((BLK, D_KV), F32), SDS((1, 128), F32)],
        scratch_shapes=[pltpu.VMEM((BLK, D_KV), F32), pltpu.VMEM((BLK, D_KV), F32)],
        compiler_params=_cparams("arbitrary"),
    )(sinks, q, k, k, k, v, v, v, d_out)


def _rope_bwd(dq, dk, dv, dk_meta, dv_meta, cos, sin_a, sin_b, layer):
    rows = dq.shape[0]
    tm = _row_tile(rows)

    def body(dq_ref, dk_ref, dv_ref, dkm_ref, dvm_ref, c_ref, a_ref, b_ref, o_ref):
        c, a, b = c_ref[...], -a_ref[...], -b_ref[...]
        for t in range(8):
            x = dq_ref[:, t * 128:(t + 1) * 128]
            o_ref[:, t * 128:(t + 1) * 128] = (_rope_lanes(x, c, a, b) * ATTN_SCALE).astype(MXU_DTYPE)
        for t in range(2):
            x = dk_ref[:, t * 128:(t + 1) * 128]
            o_ref[:, D_ATTN + t * 128:D_ATTN + (t + 1) * 128] = _rope_lanes(x, c, a, b).astype(MXU_DTYPE)
        o_ref[:, D_ATTN + D_KV:] = dv_ref[...].astype(MXU_DTYPE)

        @pl.when(pl.program_id(0) == 0)
        def _():
            cb, ab, bb = c[0:BLK], a[0:BLK], b[0:BLK]
            for t in range(2):
                x = dk_ref[0:BLK, t * 128:(t + 1) * 128] + dkm_ref[:, t * 128:(t + 1) * 128]
                o_ref[0:BLK, D_ATTN + t * 128:D_ATTN + (t + 1) * 128] = _rope_lanes(x, cb, ab, bb).astype(MXU_DTYPE)
            o_ref[0:BLK, D_ATTN + D_KV:] = (dv_ref[0:BLK, :] + dvm_ref[...]).astype(MXU_DTYPE)

    tab = pl.BlockSpec((tm, 128), lambda i: (i, 0))
    kv = pl.BlockSpec((tm, D_KV), lambda i: (i, 0))
    meta = pl.BlockSpec((BLK, D_KV), lambda i: (0, 0))
    return _pcall(
        body, name=f"rope_bwd_l{layer}", grid=(rows // tm,),
        in_specs=[pl.BlockSpec((tm, D_ATTN), lambda i: (i, 0)), kv, kv, meta, meta, tab, tab, tab],
        out_specs=pl.BlockSpec((tm, D_ATTN + 2 * D_KV), lambda i: (i, 0)),
        out_shape=SDS((rows, D_ATTN + 2 * D_KV), MXU_DTYPE),
        compiler_params=_cparams("parallel"),
    )(dq, dk, dv, dk_meta, dv_meta, cos, sin_a, sin_b)


def _s5_bwd(d_gated, y, proj, carry_in, ssm, w_glu, b_glu3, layer):
    rows = y.shape[0]
    n_chunks = rows // BLK
    b_mat, c_mat, pw_re, pw_im, p_re, p_im, q_re, q_im, d_skip = (
        ssm[k] for k in ("b_mat", "c_mat", "pw_re", "pw_im", "p_re", "p_im", "q_re", "q_im", "d_skip"))

    def body(dz_ref, y_ref, u_ref, cin_ref, bm_ref, cm_ref, pwr_ref, pwi_ref, pr_ref, pi_ref, qr_ref, qi_ref,
             d_ref, wg_ref, bg_ref,
             du_ref, dwg_ref, dbg_ref, dd_ref, dbm_ref, dcm_ref, dab_ref, lam_carry):
        step = pl.program_id(0)
        chunk = n_chunks - 1 - step

        @pl.when(step == 0)
        def _():
            for r in (dwg_ref, dbg_ref, dd_ref, dbm_ref, dcm_ref, dab_ref, lam_carry):
                r[...] = jnp.zeros_like(r)

        y = y_ref[...]
        u = u_ref[...]
        d_o = dz_ref[...]
        z, t = _gelu_parts(y)
        zb = z.astype(MXU_DTYPE)
        sg = _sigmoid(_dot(zb, wg_ref[...]) + bg_ref[...])
        dgl = d_o * z * (sg * (1.0 - sg))
        dglb = dgl.astype(MXU_DTYPE)
        dz = d_o * sg + _dot_nt(dglb, wg_ref[...])
        dwg_ref[...] += _dot_tn(zb, dglb)
        dbg_ref[...] += jnp.sum(dgl, axis=0, keepdims=True)
        dy = dz * _gelu_grad(y, t)
        dd_ref[...] += jnp.sum(dy * u, axis=0, keepdims=True)
        row = lax.broadcasted_iota(jnp.int32, (BLK, SB_STATES), 0)
        grow = lax.broadcasted_iota(jnp.int32, (BLK, 128), 0) + chunk * BLK
        for sb in range(N_SB):
            cols = slice(sb * 128, (sb + 1) * 128)
            u_sb = u[:, cols].astype(MXU_DTYPE)
            dy_sb = dy[:, cols]
            dyb = dy_sb.astype(MXU_DTYPE)
            bu = _dot(u_sb, bm_ref[sb])
            sr, si = _scan_rows(bu[:, :SB_STATES], bu[:, SB_STATES:], pwr_ref[sb], pwi_ref[sb], False)
            cr = cin_ref[2 * sb:2 * sb + 1, :]
            ci = cin_ref[2 * sb + 1:2 * sb + 2, :]
            pr, pi = pr_ref[sb], pi_ref[sb]
            sr = sr + (pr * cr - pi * ci)
            si = si + (pr * ci + pi * cr)
            dcm_ref[sb] += _dot_tn(jnp.concatenate([sr, si], axis=1).astype(MXU_DTYPE), dyb)
            gs = _dot_nt(dyb, cm_ref[sb])
            lr, li = _scan_rows(gs[:, :SB_STATES], gs[:, SB_STATES:], pwr_ref[sb], pwi_ref[sb], True)
            ncr = lam_carry[2 * sb:2 * sb + 1, :]
            nci = lam_carry[2 * sb + 1:2 * sb + 2, :]
            qr, qi = qr_ref[sb], qi_ref[sb]
            lr = lr + (qr * ncr - qi * nci)
            li = li + (qr * nci + qi * ncr)
            lam_carry[2 * sb:2 * sb + 1, :] = lr[0:1, :]
            lam_carry[2 * sb + 1:2 * sb + 2, :] = li[0:1, :]
            spr = jnp.where(row == 0, cr, pltpu.roll(sr, 1, 0))
            spi = jnp.where(row == 0, ci, pltpu.roll(si, 1, 0))
            dab_ref[2 * sb:2 * sb + 1, :] += jnp.sum(spr * lr + spi * li, axis=0, keepdims=True)
            dab_ref[2 * sb + 1:2 * sb + 2, :] += jnp.sum(spr * li - spi * lr, axis=0, keepdims=True)
            lam = jnp.concatenate([lr, li], axis=1).astype(MXU_DTYPE)
            dbm_ref[sb] += _dot_tn(u_sb, lam)
            du = _dot_nt(lam, bm_ref[sb]) + d_ref[:, cols] * dy_sb
            du_ref[:, cols] = jnp.where(grow >= PAD_ROWS, du, 0.0).astype(MXU_DTYPE)

    rev = lambda j: (n_chunks - 1 - j, 0)
    full = lambda shape: pl.BlockSpec(shape, lambda j: (0,) * len(shape))
    tables = [full((N_SB, 8, SB_STATES))] * 2 + [full((N_SB, BLK, SB_STATES))] * 4
    return _pcall(
        body, name=f"s5_bwd_l{layer}", grid=(n_chunks,),
        in_specs=[pl.BlockSpec((BLK, D_SSM), rev), pl.BlockSpec((BLK, D_SSM), rev), pl.BlockSpec((BLK, D_SSM), rev),
                  pl.BlockSpec((None, 8, SB_STATES), lambda j: (n_chunks - 1 - j, 0, 0)),
                  full((N_SB, 128, 2 * SB_STATES)), full((N_SB, 2 * SB_STATES, 128))] + tables + [
                  full((1, D_SSM)), full((D_SSM, D_SSM)),
                  pl.BlockSpec((None, 1, D_SSM), lambda j: (layer, 0, 0))],
        out_specs=[pl.BlockSpec((BLK, D_SSM), rev), full((D_SSM, D_SSM)), full((1, D_SSM)), full((1, D_SSM)),
                   full((N_SB, 128, 2 * SB_STATES)), full((N_SB, 2 * SB_STATES, 128)), full((8, SB_STATES))],
        out_shape=[SDS((rows, D_SSM), MXU_DTYPE), SDS((D_SSM, D_SSM), F32), SDS((1, D_SSM), F32), SDS((1, D_SSM), F32),
                   SDS((N_SB, 128, 2 * SB_STATES), F32), SDS((N_SB, 2 * SB_STATES, 128), F32), SDS((8, SB_STATES), F32)],
        scratch_shapes=[pltpu.VMEM((8, SB_STATES), F32)],
        compiler_params=_cparams("arbitrary"),
    )(d_gated, y, proj, carry_in, b_mat, c_mat, pw_re, pw_im, p_re, p_im, q_re, q_im, d_skip, w_glu, b_glu3)


def _in_bwd(dproj, dhm, hres, gain3, w_in_g, layer):
    rows = hres.shape[0]
    tm = _row_tile(rows)

    def body(dp_ref, dh_ref, x_ref, g_ref, w_ref, dx_ref, dg_ref, acc):
        i = pl.program_id(0)
        j = pl.program_id(1)

        @pl.when((i == 0) & (j == 0))
        def _():
            dg_ref[...] = jnp.zeros_like(dg_ref)

        @pl.when(j == 0)
        def _():
            acc[...] = jnp.zeros_like(acc)

        acc[...] += _dot_nt(dp_ref[...], w_ref[...])

        @pl.when(j == N_DEV - 1)
        def _():
            dx, dg = _rms_bwd(x_ref[...], g_ref[...], acc[...])
            dg_ref[...] += dg
            dx_ref[...] = dh_ref[...] + dx

    row_d = pl.BlockSpec((tm, D), lambda i, j: (i, 0))
    return _pcall(
        body, name=f"in_bwd_l{layer}", grid=(rows // tm, N_DEV),
        in_specs=[pl.BlockSpec((tm, COL_SHARD), lambda i, j: (i, j)), row_d, row_d,
                  pl.BlockSpec((None, 1, D), lambda i, j: (layer, 0, 0)),
                  pl.BlockSpec((None, None, D, COL_SHARD), lambda i, j: (j, layer, 0, 0))],
        out_specs=[row_d, pl.BlockSpec((1, D), lambda i, j: (0, 0))],
        out_shape=[SDS((rows, D), F32), SDS((1, D), F32)],
        scratch_shapes=[pltpu.VMEM((tm, D), F32)],
        compiler_params=_cparams("arbitrary", "arbitrary"),
    )(dproj, dhm, hres, gain3, w_in_g)


_ADAM_C1 = 1.0 / (1.0 - ADAM_B1 ** ADAM_STEP)
_ADAM_C2 = 1.0 / (1.0 - ADAM_B2 ** ADAM_STEP)


def _adam_math(w, g, m, v):
    m = ADAM_B1 * m + (1.0 - ADAM_B1) * g
    v = ADAM_B2 * v + (1.0 - ADAM_B2) * (g * g)
    delta = -ADAM_LR * ((m * _ADAM_C1) / (jnp.sqrt(v * _ADAM_C2) + ADAM_EPS) + ADAM_WD * w)
    return delta, m, v


def _adamw_layers(parts0, parts1, w, m, v, name):
    _, rows, cols = w.shape
    tr = min(rows, (1 << 16) // cols)
    nt = rows // tr

    def body(p0_ref, p1_ref, w_ref, m_ref, v_ref, g_ref, d_ref, nm_ref, nv_ref):
        layer = pl.program_id(0)

        def run(p_ref):
            g = p_ref[0]
            for s in range(1, N_DEV):
                g = g + p_ref[s]
            delta, nm, nv = _adam_math(w_ref[...], g, m_ref[...], v_ref[...])
            g_ref[...] = g
            d_ref[...] = delta
            nm_ref[...] = nm
            nv_ref[...] = nv

        @pl.when(layer == 0)
        def _():
            run(p0_ref)

        @pl.when(layer == 1)
        def _():
            run(p1_ref)

    wspec = pl.BlockSpec((None, tr, cols), lambda l, i: (l, i, 0))
    return _pcall(
        body, name=name, grid=(2, nt),
        in_specs=[pl.BlockSpec((N_DEV, tr, cols), lambda l, i: (0, jnp.where(l == 0, i, nt - 1), 0)),
                  pl.BlockSpec((N_DEV, tr, cols), lambda l, i: (0, jnp.where(l == 1, i, 0), 0)),
                  wspec, wspec, wspec],
        out_specs=[wspec] * 4, out_shape=[SDS(w.shape, F32)] * 4,
        compiler_params=_cparams("arbitrary", "arbitrary"),
    )(parts0, parts1, w, m, v)


def _adamw_packed(g, w, m, v, name):
    def body(g_ref, w_ref, m_ref, v_ref, d_ref, nm_ref, nv_ref):
        delta, nm, nv = _adam_math(w_ref[...], g_ref[...], m_ref[...], v_ref[...])
        d_ref[...] = delta
        nm_ref[...] = nm
        nv_ref[...] = nv

    vmem = pl.BlockSpec(memory_space=pltpu.VMEM)
    return _pcall(body, name=name, out_shape=[SDS(g.shape, F32)] * 3, in_specs=[vmem] * 4, out_specs=[vmem] * 3,
                  compiler_params=_cparams())(g, w, m, v)


def _ssm_discretize(a_re, a_im, log_dt, b_re, b_im):
    dt = jnp.exp(log_dt)[:, None]
    mag = jnp.exp(a_re * dt)
    ang = a_im * dt
    ab_re, ab_im = mag * jnp.cos(ang), mag * jnp.sin(ang)
    xr, xi = ab_re - 1.0, ab_im
    den = a_re * a_re + a_im * a_im
    q_re = (xr * a_re + xi * a_im) / den
    q_im = (xi * a_re - xr * a_im) / den
    bb_re = q_re[..., None] * b_re - q_im[..., None] * b_im
    bb_im = q_re[..., None] * b_im + q_im[..., None] * b_re
    return ab_re, ab_im, bb_re, bb_im


def _block_diag_b(bb):
    m = jnp.einsum("sgnc,gh->sgchn", bb.reshape(N_SB, 8, N_STATE, GROUP_CH), jnp.eye(8, dtype=F32))
    return m.reshape(N_SB, 128, SB_STATES)


def _block_diag_b_t(dm):
    return jnp.einsum("sgchn,gh->sgnc", dm.reshape(N_SB, 8, GROUP_CH, 8, N_STATE),
                      jnp.eye(8, dtype=F32)).reshape(N_GROUPS, N_STATE, GROUP_CH)


def _block_diag_c(cc):
    m = jnp.einsum("sgcn,gh->sgnhc", cc.reshape(N_SB, 8, GROUP_CH, N_STATE), jnp.eye(8, dtype=F32))
    return m.reshape(N_SB, SB_STATES, 128)


def _block_diag_c_t(dm):
    return jnp.einsum("sgnhc,gh->sgcn", dm.reshape(N_SB, 8, N_STATE, 8, GROUP_CH),
                      jnp.eye(8, dtype=F32)).reshape(N_GROUPS, GROUP_CH, N_STATE)


def _ssm_tables(ab_re, ab_im, bb_re, bb_im, c_re, c_im, d_skip):
    lanes = lambda t: t.reshape(t.shape[0], N_SB, SB_STATES).transpose(1, 0, 2)
    pr, pi = ab_re[None], ab_im[None]
    cr, ci = ab_re, ab_im
    squares = [(cr, ci)]
    for _ in range(7):
        pr, pi = (jnp.concatenate([pr, pr * cr - pi * ci], axis=0),
                  jnp.concatenate([pi, pr * ci + pi * cr], axis=0))
        cr, ci = cr * cr - ci * ci, 2.0 * cr * ci
        squares.append((cr, ci))
    pr, pi = pr.reshape(BLK, -1), pi.reshape(BLK, -1)
    return dict(
        b_mat=jnp.concatenate([_block_diag_b(bb_re), _block_diag_b(bb_im)], axis=-1).astype(MXU_DTYPE),
        c_mat=jnp.concatenate([_block_diag_c(c_re), -_block_diag_c(c_im)], axis=1).astype(MXU_DTYPE),
        pw_re=lanes(jnp.stack([s[0].reshape(-1) for s in squares])),
        pw_im=lanes(jnp.stack([s[1].reshape(-1) for s in squares])),
        p_re=lanes(pr), p_im=lanes(pi),
        q_re=lanes(pr[::-1]), q_im=lanes(-pi[::-1]),
        d_skip=d_skip.reshape(1, D_SSM))


def _rope_tables(rows):
    pos = (jnp.arange(rows, dtype=jnp.int32) - PAD_ROWS).astype(F32)
    inv_freq = 1.0 / (ROPE_THETA ** (jnp.arange(0, HEAD_DIM, 2, dtype=F32) / HEAD_DIM))
    ang = pos[:, None] * inv_freq[None, :]
    ang = jnp.concatenate([ang, ang, ang, ang], axis=-1)
    first_half = (jnp.arange(128) % HEAD_DIM) < HEAD_DIM // 2
    sin = jnp.sin(ang)
    return jnp.cos(ang), jnp.where(first_half, -sin, 0.0), jnp.where(first_half, 0.0, sin)


def _pack(arrays):
    flat = jnp.concatenate([a.reshape(-1).astype(F32) for a in arrays])
    pad = (-flat.shape[0]) % 1024
    return jnp.pad(flat, (0, pad)).reshape(-1, 128)


def _unpack(packed, like):
    flat = packed.reshape(-1)
    out, off = [], 0
    for a in like:
        n = math.prod(a.shape)
        out.append(flat[off:off + n].reshape(a.shape))
        off += n
    return out


BIG = ("w_in", "w_glu", "w_o_ssm", "w_o_attn", "w_out", "w_up", "w_down")
WEIGHTS = ("meta_tokens", "norm_mix_pre", "norm_mix_post", "norm_mlp_pre", "norm_mlp_post", "w_in",
           "ssm_a_re", "ssm_a_im", "ssm_log_dt", "ssm_b_re", "ssm_b_im", "ssm_c_re", "ssm_c_im", "ssm_d",
           "w_glu", "b_glu", "attn_sinks", "w_o_ssm", "w_o_attn", "w_out", "w_up", "w_down")
SMALL = tuple(n for n in WEIGHTS if n not in BIG)


def kernel(x, meta_tokens, norm_mix_pre, norm_mix_post, norm_mlp_pre, norm_mlp_post, w_in, ssm_a_re, ssm_a_im, ssm_log_dt, ssm_b_re, ssm_b_im, ssm_c_re, ssm_c_im, ssm_d, w_glu, b_glu, attn_sinks, w_o_ssm, w_o_attn, w_out, w_up, w_down, loss_target, m_meta_tokens, m_norm_mix_pre, m_norm_mix_post, m_norm_mlp_pre, m_norm_mlp_post, m_w_in, m_ssm_a_re, m_ssm_a_im, m_ssm_log_dt, m_ssm_b_re, m_ssm_b_im, m_ssm_c_re, m_ssm_c_im, m_ssm_d, m_w_glu, m_b_glu, m_attn_sinks, m_w_o_ssm, m_w_o_attn, m_w_out, m_w_up, m_w_down, v_meta_tokens, v_norm_mix_pre, v_norm_mix_post, v_norm_mlp_pre, v_norm_mlp_post, v_w_in, v_ssm_a_re, v_ssm_a_im, v_ssm_log_dt, v_ssm_b_re, v_ssm_b_im, v_ssm_c_re, v_ssm_c_im, v_ssm_d, v_w_glu, v_b_glu, v_attn_sinks, v_w_o_ssm, v_w_o_attn, v_w_out, v_w_up, v_w_down):
    args = locals()
    w = {n: args[n] for n in WEIGHTS}
    m = {n: args["m_" + n] for n in WEIGHTS}
    v = {n: args["v_" + n] for n in WEIGHTS}
    n_layers = w_in.shape[0]
    seq = x.shape[1]
    rows = seq + BLK
    my_slot = _slot(_mesh_pos())

    gathered = _all_gather([w[n].astype(MXU_DTYPE) for n in BIG] + [meta_tokens], "gather_weights")
    w_in_g, w_glu_g, w_o_ssm_g, w_o_attn_g, w_out_g, w_up_g, w_down_g, meta_g = gathered
    meta_full = meta_g.transpose(1, 0, 2).reshape(N_META, D)
    w_glu_f = w_glu_g.transpose(1, 0, 2, 3).reshape(n_layers, D_SSM, D_SSM)
    w_o_ssm_f = w_o_ssm_g.transpose(1, 2, 0, 3).reshape(n_layers, D_SSM, D)
    w_o_attn_f = w_o_attn_g.transpose(1, 0, 2, 3).reshape(n_layers, D_ATTN, D)
    w_out_f = w_out_g.transpose(1, 0, 2, 3).reshape(n_layers, D, D)

    gains = {n: w[n].reshape(n_layers, 1, D) for n in ("norm_mix_pre", "norm_mix_post", "norm_mlp_pre", "norm_mlp_post")}
    b_glu3 = b_glu.reshape(n_layers, 1, D_SSM)
    cos, sin_a, sin_b = _rope_tables(rows)

    def ssm_setup(l):
        disc, disc_vjp = jax.vjp(_ssm_discretize, ssm_a_re[l], ssm_a_im[l], ssm_log_dt[l], ssm_b_re[l], ssm_b_im[l])
        return _ssm_tables(*disc, ssm_c_re[l], ssm_c_im[l], ssm_d[l]), disc_vjp

    hres = jnp.concatenate([jnp.zeros((PAD_ROWS, D), F32), meta_full, x[0]], axis=0)

    saved = []
    for l in range(n_layers):
        ssm, disc_vjp = ssm_setup(l)
        proj, h = _in_proj(hres, gains["norm_mix_pre"], w_in_g, l)
        q, k, vv = _rope_fwd(proj, cos, sin_a, sin_b, l)
        y, y_ssm, carry_in = _s5_fwd(proj, ssm, w_glu_f[l], b_glu3, l)
        y_attn = _attn_fwd(q, k, vv, attn_sinks, l)
        merged, mix, hres_mid = _merge_fwd(y_ssm, y_attn, proj, hres, w_o_ssm_f[l], w_o_attn_f[l], w_out_f[l],
                                                   gains["norm_mix_post"], l)
        up, act, h2, ff, hres_out = _mlp_fwd(hres_mid, gains["norm_mlp_pre"], gains["norm_mlp_post"], w_up_g, w_down_g, l)
        saved.append(dict(ssm=ssm, disc_vjp=disc_vjp, hres=hres, proj=proj, h=h, q=q, k=k, v=vv, y=y, y_ssm=y_ssm,
                          carry_in=carry_in, y_attn=y_attn, merged=merged, mix=mix, hres_mid=hres_mid,
                          up=up, act=act, h2=h2, ff=ff))
        hres = hres_out

    dhres, loss_vec = _loss_and_grad(hres, loss_target[0])
    loss = lax.psum(loss_vec[0, 0], MESH_AXES)

    small_grads = {n: [None] * n_layers for n in SMALL if n != "meta_tokens"}
    recv = [None] * n_layers
    for l in reversed(range(n_layers)):
        s = saved[l]
        dff, dup, dhm, dg_mlp_post, dg_mlp_pre = _mlp_bwd(dhres, s["ff"], s["up"], s["hres_mid"], gains["norm_mlp_pre"],
                                                          gains["norm_mlp_post"], w_up_g, w_down_g, l)
        dw_down = _matmul_tn(s["act"], dff, f"dw_down_l{l}").reshape(N_DEV, COL_SHARD, D)
        dw_up = _matmul_tn(s["h2"], dup, f"dw_up_l{l}", dev_major_cols=COL_SHARD)
        dmix, da1, da2, dgs, dga, dy_ssm, dy_attn, dg_mix_post = _merge_bwd(
            dhm, s["mix"], s["y_ssm"], s["y_attn"], s["proj"], w_o_ssm_f[l], w_o_attn_f[l], w_out_f[l], gains["norm_mix_post"], l)
        dw_out = _matmul_tn(s["merged"], dmix, f"dw_out_l{l}").reshape(N_DEV, D // N_DEV, D)
        dw_o_attn = _matmul_tn(s["y_attn"], da2, f"dw_o_attn_l{l}").reshape(N_DEV, D_ATTN // N_DEV, D)
        dw_o_ssm = _matmul_tn(s["y_ssm"], da1, f"dw_o_ssm_l{l}", dev_major_cols=D // N_DEV)
        dq, dk, dv, dk_meta, dv_meta, dsink = _attn_bwd(s["q"], s["k"], s["v"], dy_attn, attn_sinks, l)
        dqkv = _rope_bwd(dq, dk, dv, dk_meta, dv_meta, cos, sin_a, sin_b, l)
        du, dw_glu, db_glu, dd_skip, db_mat, dc_mat, dab = _s5_bwd(dy_ssm, s["y"], s["proj"], s["carry_in"], s["ssm"],
                                                                    w_glu_f[l], b_glu3, l)
        dproj = jnp.concatenate([du, dqkv, dgs, dga], axis=1)
        dw_in = _matmul_tn(s["h"], dproj, f"dw_in_l{l}", dev_major_cols=COL_SHARD)
        dhres, dg_mix_pre = _in_bwd(dproj, dhm, s["hres"], gains["norm_mix_pre"], w_in_g, l)

        recv[l] = _exchange_slots([dw_in, dw_glu.reshape(N_DEV, D_SSM // N_DEV, D_SSM), dw_o_ssm, dw_o_attn, dw_out,
                                   dw_up, dw_down], f"scatter_grads_l{l}")

        dab = dab.reshape(N_SB, 2, SB_STATES)
        da_re, da_im, dlog_dt, db_re, db_im = s["disc_vjp"]((
            dab[:, 0].reshape(N_GROUPS, N_STATE), dab[:, 1].reshape(N_GROUPS, N_STATE),
            _block_diag_b_t(db_mat[..., :SB_STATES]), _block_diag_b_t(db_mat[..., SB_STATES:])))
        for name, val in (("norm_mix_pre", dg_mix_pre[0]), ("norm_mix_post", dg_mix_post[0]),
                          ("norm_mlp_pre", dg_mlp_pre[0]), ("norm_mlp_post", dg_mlp_post[0]),
                          ("ssm_a_re", da_re), ("ssm_a_im", da_im), ("ssm_log_dt", dlog_dt),
                          ("ssm_b_re", db_re), ("ssm_b_im", db_im),
                          ("ssm_c_re", _block_diag_c_t(dc_mat[:, :SB_STATES])),
                          ("ssm_c_im", -_block_diag_c_t(dc_mat[:, SB_STATES:])),
                          ("ssm_d", dd_skip.reshape(N_GROUPS, GROUP_CH)), ("b_glu", db_glu[0]),
                          ("attn_sinks", dsink[0, :N_Q_HEADS])):
            small_grads[name][l] = val

    grad_x = dhres[BLK:][None]
    small_names = [n for n in SMALL if n != "meta_tokens"]
    partial_small = [dhres[PAD_ROWS:BLK]] + [jnp.stack(small_grads[n]) for n in small_names]
    summed = _unpack(_all_reduce_small(_pack(partial_small), "reduce_small_grads"), partial_small)
    grads = dict(zip(small_names, summed[1:]))
    grads["meta_tokens"] = lax.dynamic_slice_in_dim(summed[0], my_slot * (D // N_DEV), D // N_DEV, axis=1)

    delta, new_m, new_v = {}, {}, {}
    for idx, n in enumerate(BIG):
        grads[n], delta[n], new_m[n], new_v[n] = _adamw_layers(recv[0][idx], recv[1][idx], w[n], m[n], v[n], f"adamw_{n}")
    like = [w[n] for n in SMALL]
    d_s, m_s, v_s = _adamw_packed(_pack([grads[n] for n in SMALL]), _pack(like), _pack([m[n] for n in SMALL]),
                                  _pack([v[n] for n in SMALL]), "adamw_small")
    for n, dd, mm, vs in zip(SMALL, _unpack(d_s, like), _unpack(m_s, like), _unpack(v_s, like)):
        delta[n], new_m[n], new_v[n] = dd, mm, vs

    return (loss, grad_x, *[grads[n] for n in WEIGHTS], *[delta[n] for n in WEIGHTS],
            *[new_m[n] for n in WEIGHTS], *[new_v[n] for n in WEIGHTS])
```

```python
import functools
import math

import jax
import jax.numpy as jnp
from jax import lax
from jax.experimental import pallas as pl
from jax.experimental.pallas import tpu as pltpu

F32 = jnp.float32
MXU_DTYPE = jnp.bfloat16
XFER_DTYPE = MXU_DTYPE
_pcall = pl.pallas_call
SDS = jax.ShapeDtypeStruct

D = 1024
D_SSM = 512
D_ATTN = 1024
D_KV = 256
D_FF = 4096
D_IN = 4096
HEAD_DIM = 64
N_Q_HEADS = 16
N_KV_HEADS = 4
Q_PER_KV = 4
N_META = 16
BLK = 128
PAD_ROWS = BLK - N_META
N_GROUPS = 32
N_STATE = 64
GROUP_CH = 16
N_SB = 4
SB_STATES = 512
ROPE_THETA = 10000.0
ATTN_SCALE = HEAD_DIM ** -0.5
NEG_INF = -1e30
RMS_EPS = 1e-6
N_DEV = 8
COL_SHARD = 512

ADAM_LR = 0.001
ADAM_B1 = 0.9
ADAM_B2 = 0.999
ADAM_EPS = 1e-08
ADAM_WD = 0.01
ADAM_STEP = 10

VMEM_LIMIT = 56 * 1024 * 1024
MESH_AXES = ("x", "y", "c")

_NT = (((1,), (1,)), ((), ()))
_TN = (((0,), (0,)), ((), ()))


def _cparams(*sem):
    return pltpu.CompilerParams(dimension_semantics=tuple(sem) if sem else None,
                                vmem_limit_bytes=VMEM_LIMIT)


def _row_tile(rows, cap=640):
    for t in (640, 512, 320, 256, 128):
        if t <= cap and rows % t == 0:
            return t
    raise ValueError(f"unsupported row count {rows}")


def _dot(a, b):
    return jnp.dot(a, b, preferred_element_type=F32)


def _dot_nt(a, b):
    return lax.dot_general(a, b, _NT, preferred_element_type=F32)


def _dot_tn(a, b):
    return lax.dot_general(a, b, _TN, preferred_element_type=F32)


def _sigmoid(x):
    return 1.0 / (1.0 + jnp.exp(-x))


_GELU_C = math.sqrt(2.0 / math.pi)


def _gelu_parts(y):
    t = jnp.tanh(_GELU_C * (y + 0.044715 * (y * y * y)))
    return 0.5 * y * (1.0 + t), t


def _gelu_grad(y, t):
    return 0.5 * (1.0 + t) + 0.5 * y * (1.0 - t * t) * (_GELU_C * (1.0 + 0.134145 * (y * y)))


def _rms_fwd(x, gain):
    r = lax.rsqrt(jnp.mean(x * x, axis=-1, keepdims=True) + RMS_EPS)
    return (x * r) * gain


def _rms_bwd(x, gain, dout):
    r = lax.rsqrt(jnp.mean(x * x, axis=-1, keepdims=True) + RMS_EPS)
    xh = x * r
    dxh = dout * gain
    dx = r * (dxh - xh * jnp.mean(dxh * xh, axis=-1, keepdims=True))
    return dx, jnp.sum(dout * xh, axis=0, keepdims=True)


def _mesh_pos():
    return lax.axis_index("x"), lax.axis_index("y"), lax.axis_index("c")


def _peer(pos, d):
    x, y, c = pos
    return (1 - x if d & 4 else x, 1 - y if d & 2 else y, 1 - c if d & 1 else c)


def _slot(pos):
    return 4 * pos[0] + 2 * pos[1] + pos[2]


def _all_gather(shards, name):
    n = len(shards)

    def body(*refs):
        ins, outs = refs[:n], refs[n:2 * n]
        send_sems, recv_sems, local_sems = refs[2 * n:]
        me = _mesh_pos()
        my_slot = _slot(me)
        local = [pltpu.make_async_copy(ins[k], outs[k].at[my_slot], local_sems.at[k]) for k in range(n)]
        for cp in local:
            cp.start()
        sends = []
        for k in range(n):
            for d in range(1, N_DEV):
                sends.append(pltpu.make_async_remote_copy(
                    src_ref=ins[k], dst_ref=outs[k].at[my_slot],
                    send_sem=send_sems.at[k, d - 1], recv_sem=recv_sems.at[k, d - 1],
                    device_id=_peer(me, d), device_id_type=pl.DeviceIdType.MESH))
        for cp in sends:
            cp.start()
        for k in range(n):
            for d in range(1, N_DEV):
                pltpu.make_async_remote_copy(
                    src_ref=ins[k], dst_ref=outs[k].at[_slot(_peer(me, d))],
                    send_sem=send_sems.at[k, d - 1], recv_sem=recv_sems.at[k, d - 1],
                    device_id=_peer(me, d), device_id_type=pl.DeviceIdType.MESH).wait_recv()
        for cp in sends:
            cp.wait_send()
        for cp in local:
            cp.wait()

    any_spec = pl.BlockSpec(memory_space=pl.ANY)
    return _pcall(
        body, name=name,
        out_shape=[SDS((N_DEV,) + s.shape, s.dtype) for s in shards],
        in_specs=[any_spec] * n, out_specs=[any_spec] * n,
        scratch_shapes=[pltpu.SemaphoreType.DMA((n, N_DEV - 1)), pltpu.SemaphoreType.DMA((n, N_DEV - 1)),
                        pltpu.SemaphoreType.DMA((n,))],
    )(*shards)


def _exchange_slots(parts, name):
    n = len(parts)

    def body(*refs):
        ins, outs = refs[:n], refs[n:2 * n]
        send_sems, recv_sems, local_sems = refs[2 * n:]
        me = _mesh_pos()
        my_slot = _slot(me)
        local = [pltpu.make_async_copy(ins[k].at[my_slot], outs[k].at[my_slot], local_sems.at[k]) for k in range(n)]
        for cp in local:
            cp.start()
        sends = []
        for k in range(n):
            for d in range(1, N_DEV):
                sends.append(pltpu.make_async_remote_copy(
                    src_ref=ins[k].at[_slot(_peer(me, d))], dst_ref=outs[k].at[my_slot],
                    send_sem=send_sems.at[k, d - 1], recv_sem=recv_sems.at[k, d - 1],
                    device_id=_peer(me, d), device_id_type=pl.DeviceIdType.MESH))
        for cp in sends:
            cp.start()
        for k in range(n):
            for d in range(1, N_DEV):
                pltpu.make_async_remote_copy(
                    src_ref=ins[k].at[my_slot], dst_ref=outs[k].at[_slot(_peer(me, d))],
                    send_sem=send_sems.at[k, d - 1], recv_sem=recv_sems.at[k, d - 1],
                    device_id=_peer(me, d), device_id_type=pl.DeviceIdType.MESH).wait_recv()
        for cp in sends:
            cp.wait_send()
        for cp in local:
            cp.wait()

    any_spec = pl.BlockSpec(memory_space=pl.ANY)
    return _pcall(
        body, name=name,
        out_shape=[SDS(p.shape, p.dtype) for p in parts],
        in_specs=[any_spec] * n, out_specs=[any_spec] * n,
        scratch_shapes=[pltpu.SemaphoreType.DMA((n, N_DEV - 1)), pltpu.SemaphoreType.DMA((n, N_DEV - 1)),
                        pltpu.SemaphoreType.DMA((n,))],
    )(*parts)


def _all_reduce_small(packed, name):
    rows = packed.shape[0]

    def body(x_ref, out_ref, gath, send_sems, recv_sems):
        me = _mesh_pos()
        my_slot = _slot(me)
        gath[my_slot] = x_ref[...]
        sends = [pltpu.make_async_remote_copy(
            src_ref=x_ref, dst_ref=gath.at[my_slot],
            send_sem=send_sems.at[d - 1], recv_sem=recv_sems.at[d - 1],
            device_id=_peer(me, d), device_id_type=pl.DeviceIdType.MESH) for d in range(1, N_DEV)]
        for cp in sends:
            cp.start()
        for d in range(1, N_DEV):
            pltpu.make_async_remote_copy(
                src_ref=x_ref, dst_ref=gath.at[_slot(_peer(me, d))],
                send_sem=send_sems.at[d - 1], recv_sem=recv_sems.at[d - 1],
                device_id=_peer(me, d), device_id_type=pl.DeviceIdType.MESH).wait_recv()
        for cp in sends:
            cp.wait_send()
        acc = gath[0]
        for s in range(1, N_DEV):
            acc = acc + gath[s]
        out_ref[...] = acc

    vmem = pl.BlockSpec(memory_space=pltpu.VMEM)
    return _pcall(
        body, name=name, out_shape=SDS(packed.shape, F32), in_specs=[vmem], out_specs=vmem,
        scratch_shapes=[pltpu.VMEM((N_DEV, rows, 128), F32),
                        pltpu.SemaphoreType.DMA((N_DEV - 1,)), pltpu.SemaphoreType.DMA((N_DEV - 1,))],
        compiler_params=_cparams(),
    )(packed)


def _in_proj(hres, gain3, w_in_g, layer):
    rows = hres.shape[0]
    tm = _row_tile(rows)

    def body(x_ref, g_ref, w_ref, proj_ref, h_ref, h_scr):
        @pl.when(pl.program_id(1) == 0)
        def _():
            hn = _rms_fwd(x_ref[...], g_ref[...]).astype(MXU_DTYPE)
            h_scr[...] = hn
            h_ref[...] = hn

        proj_ref[...] = _dot(h_scr[...], w_ref[...])

    return _pcall(
        body, name=f"in_proj_l{layer}", grid=(rows // tm, N_DEV),
        in_specs=[pl.BlockSpec((tm, D), lambda i, j: (i, 0)),
                  pl.BlockSpec((None, 1, D), lambda i, j: (layer, 0, 0)),
                  pl.BlockSpec((None, None, D, COL_SHARD), lambda i, j: (j, layer, 0, 0))],
        out_specs=[pl.BlockSpec((tm, COL_SHARD), lambda i, j: (i, j)),
                   pl.BlockSpec((tm, D), lambda i, j: (i, 0))],
        out_shape=[SDS((rows, D_IN), F32), SDS((rows, D), MXU_DTYPE)],
        scratch_shapes=[pltpu.VMEM((tm, D), MXU_DTYPE)],
        compiler_params=_cparams("parallel", "arbitrary"),
    )(hres, gain3, w_in_g)


def _rope_lanes(t, cos, sin_a, sin_b):
    return t * cos + pltpu.roll(t, 96, 1) * sin_a + pltpu.roll(t, 32, 1) * sin_b


def _rope_fwd(proj, cos, sin_a, sin_b, layer):
    rows = proj.shape[0]
    tm = _row_tile(rows)

    def body(q0_ref, q1_ref, kv_ref, c_ref, a_ref, b_ref, qo_ref, ko_ref, vo_ref):
        c, a, b = c_ref[...], a_ref[...], b_ref[...]
        for half, q_ref in enumerate((q0_ref, q1_ref)):
            for t in range(4):
                x = q_ref[:, t * 128:(t + 1) * 128]
                lo = half * 512 + t * 128
                qo_ref[:, lo:lo + 128] = (_rope_lanes(x, c, a, b) * ATTN_SCALE).astype(MXU_DTYPE)
        for t in range(2):
            x = kv_ref[:, t * 128:(t + 1) * 128]
            ko_ref[:, t * 128:(t + 1) * 128] = _rope_lanes(x, c, a, b).astype(MXU_DTYPE)
        vo_ref[...] = kv_ref[:, D_KV:2 * D_KV].astype(MXU_DTYPE)

    tab = pl.BlockSpec((tm, 128), lambda i: (i, 0))
    return _pcall(
        body, name=f"rope_fwd_l{layer}", grid=(rows // tm,),
        in_specs=[pl.BlockSpec((tm, 512), lambda i: (i, 1)), pl.BlockSpec((tm, 512), lambda i: (i, 2)),
                  pl.BlockSpec((tm, 512), lambda i: (i, 3)), tab, tab, tab],
        out_specs=[pl.BlockSpec((tm, D_ATTN), lambda i: (i, 0)), pl.BlockSpec((tm, D_KV), lambda i: (i, 0)),
                   pl.BlockSpec((tm, D_KV), lambda i: (i, 0))],
        out_shape=[SDS((rows, D_ATTN), MXU_DTYPE), SDS((rows, D_KV), MXU_DTYPE), SDS((rows, D_KV), MXU_DTYPE)],
        compiler_params=_cparams("parallel"),
    )(proj, proj, proj, cos, sin_a, sin_b)


SCAN_TILE = 8


def _scan_tiles(x_ref, out_ref, tre_ref, tim_ref, sb, t_r, t_i, reverse, prev_ref=None):
    base = 4 if reverse else 0
    n_tiles = BLK // SCAN_TILE
    row = lax.broadcasted_iota(jnp.int32, (SCAN_TILE, SB_STATES), 0)
    for j in (range(n_tiles - 1, -1, -1) if reverse else range(n_tiles)):
        rows = slice(SCAN_TILE * j, SCAN_TILE * (j + 1))
        xr = x_ref[rows, :SB_STATES]
        xi = x_ref[rows, SB_STATES:]
        for k in range(3):
            shift = SCAN_TILE - (1 << k) if reverse else (1 << k)
            rr = pltpu.roll(xr, shift, 0)
            ri = pltpu.roll(xi, shift, 0)
            ar = tre_ref[sb, base + k]
            ai = tim_ref[sb, base + k]
            xr, xi = xr + (ar * rr - ai * ri), xi + (ar * ri + ai * rr)
        pr = tre_ref[sb, base + 3]
        pi = tim_ref[sb, base + 3]
        xr, xi = xr + (pr * t_r - pi * t_i), xi + (pr * t_i + pi * t_r)
        out_ref[rows, :SB_STATES] = xr
        out_ref[rows, SB_STATES:] = xi
        if prev_ref is not None:
            prev_ref[rows, :SB_STATES] = jnp.where(row == 0, t_r, pltpu.roll(xr, 1, 0))
            prev_ref[rows, SB_STATES:] = jnp.where(row == 0, t_i, pltpu.roll(xi, 1, 0))
        edge = slice(0, 1) if reverse else slice(SCAN_TILE - 1, SCAN_TILE)
        t_r, t_i = xr[edge], xi[edge]
    return t_r, t_i


def _s5_fwd(proj, ssm, w_glu, b_glu3, layer):
    rows = proj.shape[0]
    n_chunks = rows // BLK
    b_mat, c_mat, t_re, t_im, d_skip = (ssm[k] for k in ("b_mat", "c_mat", "t_re", "t_im", "d_skip"))

    def body(u_ref, bm_ref, cm_ref, tre_ref, tim_ref, d_ref, wg_ref, bg_ref,
             y_ref, ys_ref, cin_ref, carry, bu_scr, s_scr):
        @pl.when(pl.program_id(0) == 0)
        def _():
            carry[...] = jnp.zeros_like(carry)

        cin_ref[...] = carry[...]
        u = u_ref[...]
        for sb in range(N_SB):
            cols = slice(sb * 128, (sb + 1) * 128)
            u_sb = u[:, cols]
            bu_scr[sb] = _dot(u_sb.astype(MXU_DTYPE), bm_ref[sb])
            t_r, t_i = _scan_tiles(bu_scr.at[sb], s_scr.at[sb], tre_ref, tim_ref, sb,
                                   carry[2 * sb:2 * sb + 1, :], carry[2 * sb + 1:2 * sb + 2, :], False)
            carry[2 * sb:2 * sb + 1, :] = t_r
            carry[2 * sb + 1:2 * sb + 2, :] = t_i
            y_ref[:, cols] = _dot(s_scr[sb].astype(MXU_DTYPE), cm_ref[sb]) + d_ref[:, cols] * u_sb
        z, _ = _gelu_parts(y_ref[...])
        gl = _dot(z.astype(MXU_DTYPE), wg_ref[...]) + bg_ref[...]
        ys_ref[...] = (z * _sigmoid(gl)).astype(MXU_DTYPE)

    full = lambda shape: pl.BlockSpec(shape, lambda j: (0,) * len(shape))
    return _pcall(
        body, name=f"s5_fwd_l{layer}", grid=(n_chunks,),
        in_specs=[pl.BlockSpec((BLK, D_SSM), lambda j: (j, 0)),
                  full((N_SB, 128, 2 * SB_STATES)), full((N_SB, 2 * SB_STATES, 128)),
                  full((N_SB, 8, SCAN_TILE, SB_STATES)), full((N_SB, 8, SCAN_TILE, SB_STATES)),
                  full((1, D_SSM)), full((D_SSM, D_SSM)),
                  pl.BlockSpec((None, 1, D_SSM), lambda j: (layer, 0, 0))],
        out_specs=[pl.BlockSpec((BLK, D_SSM), lambda j: (j, 0)), pl.BlockSpec((BLK, D_SSM), lambda j: (j, 0)),
                   pl.BlockSpec((None, 8, SB_STATES), lambda j: (j, 0, 0))],
        out_shape=[SDS((rows, D_SSM), F32), SDS((rows, D_SSM), MXU_DTYPE), SDS((n_chunks, 8, SB_STATES), F32)],
        scratch_shapes=[pltpu.VMEM((8, SB_STATES), F32), pltpu.VMEM((N_SB, BLK, 2 * SB_STATES), F32),
                        pltpu.VMEM((N_SB, BLK, 2 * SB_STATES), F32)],
        compiler_params=_cparams("arbitrary"),
    )(proj, b_mat, c_mat, t_re, t_im, d_skip, w_glu, b_glu3)


def _attn_mask(i):
    row = lax.broadcasted_iota(jnp.int32, (BLK, 3 * BLK), 0) + i * BLK
    col = lax.broadcasted_iota(jnp.int32, (BLK, 3 * BLK), 1)
    seg = jnp.right_shift(col, 7)
    c = jnp.bitwise_and(col, BLK - 1)
    kidx = c + (i + seg - 2) * BLK
    ok_meta = (seg == 0) & (c >= PAD_ROWS) & (row - c >= BLK)
    ok_win = (seg > 0) & (kidx >= PAD_ROWS) & (kidx <= row) & (row - kidx < BLK)
    bias = jnp.where(ok_meta | ok_win, 0.0, NEG_INF)
    return jnp.concatenate([bias] * Q_PER_KV, axis=0)


def _head_lanes(h):
    return slice(h * HEAD_DIM, (h + 1) * HEAD_DIM)


def _group_rows(ref, kvh):
    return jnp.concatenate([ref[:, _head_lanes(kvh * Q_PER_KV + g)] for g in range(Q_PER_KV)], axis=0)


def _group_sinks(sink_ref, layer, kvh):
    return jnp.concatenate([jnp.full((BLK, 1), sink_ref[layer, kvh * Q_PER_KV + g], F32)
                            for g in range(Q_PER_KV)], axis=0)


def _attn_probs(q4, k3, bias4, sink4):
    s = _dot_nt(q4, k3) + bias4
    m = jnp.maximum(jnp.max(s, axis=-1, keepdims=True), sink4)
    e = jnp.exp(s - m)
    e_sink = jnp.exp(sink4 - m)
    inv = 1.0 / (jnp.sum(e, axis=-1, keepdims=True) + e_sink)
    return e * inv, e_sink * inv


def _attn_fwd(q, k, v, sinks, layer):
    rows = q.shape[0]
    n_blk = rows // BLK

    def body(sink_ref, q_ref, km_ref, kp_ref, kc_ref, vm_ref, vp_ref, vc_ref, o_ref):
        bias4 = _attn_mask(pl.program_id(0))
        for kvh in range(N_KV_HEADS):
            lanes = _head_lanes(kvh)
            k3 = jnp.concatenate([km_ref[:, lanes], kp_ref[:, lanes], kc_ref[:, lanes]], axis=0)
            v3 = jnp.concatenate([vm_ref[:, lanes], vp_ref[:, lanes], vc_ref[:, lanes]], axis=0)
            p, _ = _attn_probs(_group_rows(q_ref, kvh), k3, bias4, _group_sinks(sink_ref, layer, kvh))
            o4 = _dot(p.astype(MXU_DTYPE), v3).astype(MXU_DTYPE)
            for g in range(Q_PER_KV):
                o_ref[:, _head_lanes(kvh * Q_PER_KV + g)] = o4[g * BLK:(g + 1) * BLK]

    kv_meta = pl.BlockSpec((BLK, D_KV), lambda i: (0, 0))
    kv_prev = pl.BlockSpec((BLK, D_KV), lambda i: (jnp.maximum(i - 1, 0), 0))
    kv_cur = pl.BlockSpec((BLK, D_KV), lambda i: (i, 0))
    return _pcall(
        body, name=f"attn_fwd_l{layer}", grid=(n_blk,),
        in_specs=[pl.BlockSpec(memory_space=pltpu.SMEM),
                  pl.BlockSpec((BLK, D_ATTN), lambda i: (i, 0)),
                  kv_meta, kv_prev, kv_cur, kv_meta, kv_prev, kv_cur],
        out_specs=pl.BlockSpec((BLK, D_ATTN), lambda i: (i, 0)),
        out_shape=SDS((rows, D_ATTN), MXU_DTYPE),
        compiler_params=_cparams("parallel"),
    )(sinks, q, k, k, k, v, v, v)


def _merge_fwd(y_ssm, y_attn, proj, hres, w_o_ssm, w_o_attn, w_out, gain3, layer):
    rows = hres.shape[0]
    tm = _row_tile(rows, 320)

    def body(ys_ref, ya_ref, gs_ref, ga_ref, x_ref, wos_ref, woa_ref, wout_ref, g_ref,
             mg_ref, mix_ref, out_ref):
        a1 = _dot(ys_ref[...], wos_ref[...])
        a2 = _dot(ya_ref[...], woa_ref[...])
        merged = (_sigmoid(gs_ref[...]) * a1 + _sigmoid(ga_ref[...]) * a2).astype(MXU_DTYPE)
        mg_ref[...] = merged
        mix = _dot(merged, wout_ref[...])
        mix_ref[...] = mix
        out_ref[...] = x_ref[...] + _rms_fwd(mix, g_ref[...])

    row_d = pl.BlockSpec((tm, D), lambda i: (i, 0))
    full = lambda shape: pl.BlockSpec(shape, lambda i: (0,) * len(shape))
    return _pcall(
        body, name=f"merge_fwd_l{layer}", grid=(rows // tm,),
        in_specs=[pl.BlockSpec((tm, D_SSM), lambda i: (i, 0)), row_d,
                  pl.BlockSpec((tm, D), lambda i: (i, 2)), pl.BlockSpec((tm, D), lambda i: (i, 3)), row_d,
                  full((D_SSM, D)), full((D_ATTN, D)), full((D, D)),
                  pl.BlockSpec((None, 1, D), lambda i: (layer, 0, 0))],
        out_specs=[row_d, row_d, row_d],
        out_shape=[SDS((rows, D), MXU_DTYPE), SDS((rows, D), F32), SDS((rows, D), F32)],
        compiler_params=_cparams("parallel"),
    )(y_ssm, y_attn, proj, proj, hres, w_o_ssm, w_o_attn, w_out, gain3)


def _mlp_fwd(hres, gain_pre3, gain_post3, w_up_g, w_down_g, layer):
    rows = hres.shape[0]
    tm = _row_tile(rows)

    def body(x_ref, gp_ref, gq_ref, wu_ref, wd_ref, up_ref, act_ref, h_ref, ff_ref, out_ref, h_scr, acc):
        kf = pl.program_id(1)

        @pl.when(kf == 0)
        def _():
            hn = _rms_fwd(x_ref[...], gp_ref[...]).astype(MXU_DTYPE)
            h_scr[...] = hn
            h_ref[...] = hn
            acc[...] = jnp.zeros_like(acc)

        up = _dot(h_scr[...], wu_ref[...])
        up_ref[...] = up
        r = jnp.maximum(up, 0.0)
        act = (r * r).astype(MXU_DTYPE)
        act_ref[...] = act
        acc[...] += _dot(act, wd_ref[...])

        @pl.when(kf == N_DEV - 1)
        def _():
            ff = acc[...]
            ff_ref[...] = ff
            out_ref[...] = x_ref[...] + _rms_fwd(ff, gq_ref[...])

    row_d = pl.BlockSpec((tm, D), lambda i, k: (i, 0))
    gain = pl.BlockSpec((None, 1, D), lambda i, k: (layer, 0, 0))
    return _pcall(
        body, name=f"mlp_fwd_l{layer}", grid=(rows // tm, N_DEV),
        in_specs=[row_d, gain, gain,
                  pl.BlockSpec((None, None, D, COL_SHARD), lambda i, k: (k, layer, 0, 0)),
                  pl.BlockSpec((None, None, COL_SHARD, D), lambda i, k: (k, layer, 0, 0))],
        out_specs=[pl.BlockSpec((tm, COL_SHARD), lambda i, k: (i, k)),
                   pl.BlockSpec((tm, COL_SHARD), lambda i, k: (i, k)), row_d, row_d, row_d],
        out_shape=[SDS((rows, D_FF), F32), SDS((rows, D_FF), MXU_DTYPE), SDS((rows, D), MXU_DTYPE),
                   SDS((rows, D), F32), SDS((rows, D), F32)],
        scratch_shapes=[pltpu.VMEM((tm, D), MXU_DTYPE), pltpu.VMEM((tm, D), F32)],
        compiler_params=_cparams("parallel", "arbitrary"),
    )(hres, gain_pre3, gain_post3, w_up_g, w_down_g)


def _loss_and_grad(hres, target):
    rows = hres.shape[0]
    n_blk = rows // BLK

    def body(y_ref, t_ref, dy_ref, loss_ref):
        i = pl.program_id(0)

        @pl.when(i == 0)
        def _():
            dy_ref[...] = jnp.zeros_like(dy_ref)
            loss_ref[...] = jnp.zeros_like(loss_ref)

        @pl.when(i > 0)
        def _():
            err = y_ref[...] - t_ref[...]
            dy_ref[...] = err * (1.0 / D)
            loss_ref[...] += jnp.sum(err * err) * (0.5 / D)

    return _pcall(
        body, name="loss", grid=(n_blk,),
        in_specs=[pl.BlockSpec((BLK, D), lambda i: (i, 0)),
                  pl.BlockSpec((BLK, D), lambda i: (jnp.maximum(i - 1, 0), 0))],
        out_specs=[pl.BlockSpec((BLK, D), lambda i: (i, 0)), pl.BlockSpec((1, 128), lambda i: (0, 0))],
        out_shape=[SDS((rows, D), F32), SDS((1, 128), F32)],
        compiler_params=_cparams("arbitrary"),
    )(hres, target)


def _matmul_tn(a, b, name, dev_major_cols=None):
    rows, ka = a.shape
    n = b.shape[1]
    ta = min(ka, 1024)
    tn = 1024 if n % 1024 == 0 else 512
    tr = _row_tile(rows)
    n_r = rows // tr

    def body(a_ref, b_ref, o_ref, acc):
        r = pl.program_id(2)

        @pl.when(r == 0)
        def _():
            acc[...] = jnp.zeros_like(acc)

        acc[...] += _dot_tn(a_ref[...], b_ref[...])

        @pl.when(r == n_r - 1)
        def _():
            if dev_major_cols is None:
                o_ref[...] = acc[...].astype(XFER_DTYPE)
            else:
                for s in range(tn // dev_major_cols):
                    o_ref[s] = acc[:, s * dev_major_cols:(s + 1) * dev_major_cols].astype(XFER_DTYPE)

    if dev_major_cols is None:
        out_spec = pl.BlockSpec((ta, tn), lambda i, j, r: (i, j))
        out_shape = SDS((ka, n), XFER_DTYPE)
    else:
        w = dev_major_cols
        out_spec = pl.BlockSpec((tn // w, ta, w), lambda i, j, r: (j, i, 0))
        out_shape = SDS((n // w, ka, w), XFER_DTYPE)
    return _pcall(
        body, name=name, grid=(ka // ta, n // tn, n_r),
        in_specs=[pl.BlockSpec((tr, ta), lambda i, j, r: (r, i)), pl.BlockSpec((tr, tn), lambda i, j, r: (r, j))],
        out_specs=out_spec, out_shape=out_shape,
        scratch_shapes=[pltpu.VMEM((ta, tn), F32)],
        compiler_params=_cparams("parallel", "parallel", "arbitrary"),
    )(a, b)


def _mlp_bwd(dout, ff, up, hres_mid, gain_pre3, gain_post3, w_up_g, w_down_g, layer):
    rows = dout.shape[0]
    tm = _row_tile(rows)

    def body(do_ref, ff_ref, up_ref, x_ref, gp_ref, gq_ref, wu_ref, wd_ref,
             dff_ref, dup_ref, dx_ref, dgq_ref, dgp_ref, dff_scr, acc):
        i = pl.program_id(0)
        kf = pl.program_id(1)

        @pl.when((i == 0) & (kf == 0))
        def _():
            dgq_ref[...] = jnp.zeros_like(dgq_ref)
            dgp_ref[...] = jnp.zeros_like(dgp_ref)

        @pl.when(kf == 0)
        def _():
            dff, dg = _rms_bwd(ff_ref[...], gq_ref[...], do_ref[...])
            dgq_ref[...] += dg
            dffb = dff.astype(MXU_DTYPE)
            dff_scr[...] = dffb
            dff_ref[...] = dffb
            acc[...] = jnp.zeros_like(acc)

        dact = _dot_nt(dff_scr[...], wd_ref[...])
        dup = (dact * (2.0 * jnp.maximum(up_ref[...], 0.0))).astype(MXU_DTYPE)
        dup_ref[...] = dup
        acc[...] += _dot_nt(dup, wu_ref[...])

        @pl.when(kf == N_DEV - 1)
        def _():
            dx, dg = _rms_bwd(x_ref[...], gp_ref[...], acc[...])
            dgp_ref[...] += dg
            dx_ref[...] = do_ref[...] + dx

    row_d = pl.BlockSpec((tm, D), lambda i, k: (i, 0))
    gain = pl.BlockSpec((None, 1, D), lambda i, k: (layer, 0, 0))
    dgain = pl.BlockSpec((1, D), lambda i, k: (0, 0))
    return _pcall(
        body, name=f"mlp_bwd_l{layer}", grid=(rows // tm, N_DEV),
        in_specs=[row_d, row_d, pl.BlockSpec((tm, COL_SHARD), lambda i, k: (i, k)), row_d, gain, gain,
                  pl.BlockSpec((None, None, D, COL_SHARD), lambda i, k: (k, layer, 0, 0)),
                  pl.BlockSpec((None, None, COL_SHARD, D), lambda i, k: (k, layer, 0, 0))],
        out_specs=[row_d, pl.BlockSpec((tm, COL_SHARD), lambda i, k: (i, k)), row_d, dgain, dgain],
        out_shape=[SDS((rows, D), MXU_DTYPE), SDS((rows, D_FF), MXU_DTYPE), SDS((rows, D), F32),
                   SDS((1, D), F32), SDS((1, D), F32)],
        scratch_shapes=[pltpu.VMEM((tm, D), MXU_DTYPE), pltpu.VMEM((tm, D), F32)],
        compiler_params=_cparams("arbitrary", "arbitrary"),
    )(dout, ff, up, hres_mid, gain_pre3, gain_post3, w_up_g, w_down_g)


def _merge_bwd(dhm, mix, y_ssm, y_attn, proj, w_o_ssm, w_o_attn, w_out, gain3, layer):
    rows = dhm.shape[0]
    tm = _row_tile(rows, 320)

    def body(dh_ref, mix_ref, ys_ref, ya_ref, gs_ref, ga_ref, wos_ref, woa_ref, wout_ref, g_ref,
             dmix_ref, da1_ref, da2_ref, dgs_ref, dga_ref, dys_ref, dya_ref, dg_ref):
        @pl.when(pl.program_id(0) == 0)
        def _():
            dg_ref[...] = jnp.zeros_like(dg_ref)

        dmix, dg = _rms_bwd(mix_ref[...], g_ref[...], dh_ref[...])
        dg_ref[...] += dg
        dmixb = dmix.astype(MXU_DTYPE)
        dmix_ref[...] = dmixb
        dmerged = _dot_nt(dmixb, wout_ref[...])
        sg_s = _sigmoid(gs_ref[...])
        sg_a = _sigmoid(ga_ref[...])
        da1 = (dmerged * sg_s).astype(MXU_DTYPE)
        da2 = (dmerged * sg_a).astype(MXU_DTYPE)
        da1_ref[...] = da1
        da2_ref[...] = da2
        a1 = _dot(ys_ref[...], wos_ref[...])
        a2 = _dot(ya_ref[...], woa_ref[...])
        dgs_ref[...] = (dmerged * a1 * (sg_s * (1.0 - sg_s))).astype(MXU_DTYPE)
        dga_ref[...] = (dmerged * a2 * (sg_a * (1.0 - sg_a))).astype(MXU_DTYPE)
        dys_ref[...] = _dot_nt(da1, wos_ref[...])
        dya_ref[...] = _dot_nt(da2, woa_ref[...])

    row_d = pl.BlockSpec((tm, D), lambda i: (i, 0))
    full = lambda shape: pl.BlockSpec(shape, lambda i: (0,) * len(shape))
    return _pcall(
        body, name=f"merge_bwd_l{layer}", grid=(rows // tm,),
        in_specs=[row_d, row_d, pl.BlockSpec((tm, D_SSM), lambda i: (i, 0)), row_d,
                  pl.BlockSpec((tm, D), lambda i: (i, 2)), pl.BlockSpec((tm, D), lambda i: (i, 3)),
                  full((D_SSM, D)), full((D_ATTN, D)), full((D, D)),
                  pl.BlockSpec((None, 1, D), lambda i: (layer, 0, 0))],
        out_specs=[row_d, row_d, row_d, row_d, row_d, pl.BlockSpec((tm, D_SSM), lambda i: (i, 0)), row_d,
                   pl.BlockSpec((1, D), lambda i: (0, 0))],
        out_shape=[SDS((rows, D), MXU_DTYPE)] * 5 + [SDS((rows, D_SSM), F32), SDS((rows, D_ATTN), F32),
                                                      SDS((1, D), F32)],
        compiler_params=_cparams("arbitrary"),
    )(dhm, mix, y_ssm, y_attn, proj, proj, w_o_ssm, w_o_attn, w_out, gain3)


def _attn_bwd(q, k, v, d_out, sinks, layer):
    rows = q.shape[0]
    n_blk = rows // BLK
    last = n_blk - 1

    def body(sink_ref, q_ref, km_ref, kp_ref, kc_ref, vm_ref, vp_ref, vc_ref, do_ref,
             dq_ref, dk_ref, dv_ref, dkm_ref, dvm_ref, ds_ref, dk_carry, dv_carry):
        i = pl.program_id(0)

        @pl.when(i == 0)
        def _():
            dkm_ref[...] = jnp.zeros_like(dkm_ref)
            dvm_ref[...] = jnp.zeros_like(dvm_ref)
            ds_ref[...] = jnp.zeros_like(ds_ref)
            dk_carry[...] = jnp.zeros_like(dk_carry)
            dv_carry[...] = jnp.zeros_like(dv_carry)

        @pl.when(i <= last)
        def _():
            bias4 = _attn_mask(i)
            lane = lax.broadcasted_iota(jnp.int32, (1, 128), 1)
            dsink = jnp.zeros((1, 128), F32)
            for kvh in range(N_KV_HEADS):
                lanes = _head_lanes(kvh)
                k3 = jnp.concatenate([km_ref[:, lanes], kp_ref[:, lanes], kc_ref[:, lanes]], axis=0)
                v3 = jnp.concatenate([vm_ref[:, lanes], vp_ref[:, lanes], vc_ref[:, lanes]], axis=0)
                q4 = _group_rows(q_ref, kvh)
                do4 = _group_rows(do_ref, kvh).astype(MXU_DTYPE)
                p, p_sink = _attn_probs(q4, k3, bias4, _group_sinks(sink_ref, layer, kvh))
                dp = _dot_nt(do4, v3)
                delta = jnp.sum(dp * p, axis=-1, keepdims=True)
                dsc = (p * (dp - delta)).astype(MXU_DTYPE)
                sink_term = p_sink * delta
                dv3 = _dot_tn(p.astype(MXU_DTYPE), do4)
                dk3 = _dot_tn(dsc, q4)
                dq4 = _dot(dsc, k3)
                for g in range(Q_PER_KV):
                    h = kvh * Q_PER_KV + g
                    dq_ref[:, _head_lanes(h)] = dq4[g * BLK:(g + 1) * BLK]
                    dsink = dsink + jnp.where(lane == h, -jnp.sum(sink_term[g * BLK:(g + 1) * BLK]), 0.0)
                dkm_ref[:, lanes] += dk3[0:BLK]
                dvm_ref[:, lanes] += dv3[0:BLK]
                dk_ref[:, lanes] = dk_carry[:, lanes] + dk3[BLK:2 * BLK]
                dv_ref[:, lanes] = dv_carry[:, lanes] + dv3[BLK:2 * BLK]
                dk_carry[:, lanes] = dk3[2 * BLK:3 * BLK]
                dv_carry[:, lanes] = dv3[2 * BLK:3 * BLK]
            ds_ref[...] += dsink

        @pl.when(i == last + 1)
        def _():
            dk_ref[...] = dk_carry[...]
            dv_ref[...] = dv_carry[...]

    cur = lambda i: (jnp.minimum(i, last), 0)
    prev = lambda i: (jnp.clip(i - 1, 0, last), 0)
    kv_meta = pl.BlockSpec((BLK, D_KV), lambda i: (0, 0))
    kv_prev = pl.BlockSpec((BLK, D_KV), prev)
    kv_cur = pl.BlockSpec((BLK, D_KV), cur)
    return _pcall(
        body, name=f"attn_bwd_l{layer}", grid=(n_blk + 1,),
        in_specs=[pl.BlockSpec(memory_space=pltpu.SMEM),
                  pl.BlockSpec((BLK, D_ATTN), cur),
                  kv_meta, kv_prev, kv_cur, kv_meta, kv_prev, kv_cur,
                  pl.BlockSpec((BLK, D_ATTN), cur)],
        out_specs=[pl.BlockSpec((BLK, D_ATTN), cur), kv_prev, kv_prev, kv_meta, kv_meta,
                   pl.BlockSpec((1, 128), lambda i: (0, 0))],
        out_shape=[SDS((rows, D_ATTN), F32), SDS((rows, D_KV), F32), SDS((rows, D_KV), F32),
                   SDS((BLK, D_KV), F32), SDS((BLK, D_KV), F32), SDS((1, 128), F32)],
        scratch_shapes=[pltpu.VMEM((BLK, D_KV), F32), pltpu.VMEM((BLK, D_KV), F32)],
        compiler_params=_cparams("arbitrary"),
    )(sinks, q, k, k, k, v, v, v, d_out)


def _rope_bwd(dq, dk, dv, dk_meta, dv_meta, cos, sin_a, sin_b, layer):
    rows = dq.shape[0]
    tm = _row_tile(rows)

    def body(dq_ref, dk_ref, dv_ref, dkm_ref, dvm_ref, c_ref, a_ref, b_ref, o_ref):
        c, a, b = c_ref[...], -a_ref[...], -b_ref[...]
        for t in range(8):
            x = dq_ref[:, t * 128:(t + 1) * 128]
            o_ref[:, t * 128:(t + 1) * 128] = (_rope_lanes(x, c, a, b) * ATTN_SCALE).astype(MXU_DTYPE)
        for t in range(2):
            x = dk_ref[:, t * 128:(t + 1) * 128]
            o_ref[:, D_ATTN + t * 128:D_ATTN + (t + 1) * 128] = _rope_lanes(x, c, a, b).astype(MXU_DTYPE)
        o_ref[:, D_ATTN + D_KV:] = dv_ref[...].astype(MXU_DTYPE)

        @pl.when(pl.program_id(0) == 0)
        def _():
            cb, ab, bb = c[0:BLK], a[0:BLK], b[0:BLK]
            for t in range(2):
                x = dk_ref[0:BLK, t * 128:(t + 1) * 128] + dkm_ref[:, t * 128:(t + 1) * 128]
                o_ref[0:BLK, D_ATTN + t * 128:D_ATTN + (t + 1) * 128] = _rope_lanes(x, cb, ab, bb).astype(MXU_DTYPE)
            o_ref[0:BLK, D_ATTN + D_KV:] = (dv_ref[0:BLK, :] + dvm_ref[...]).astype(MXU_DTYPE)

    tab = pl.BlockSpec((tm, 128), lambda i: (i, 0))
    kv = pl.BlockSpec((tm, D_KV), lambda i: (i, 0))
    meta = pl.BlockSpec((BLK, D_KV), lambda i: (0, 0))
    return _pcall(
        body, name=f"rope_bwd_l{layer}", grid=(rows // tm,),
        in_specs=[pl.BlockSpec((tm, D_ATTN), lambda i: (i, 0)), kv, kv, meta, meta, tab, tab, tab],
        out_specs=pl.BlockSpec((tm, D_ATTN + 2 * D_KV), lambda i: (i, 0)),
        out_shape=SDS((rows, D_ATTN + 2 * D_KV), MXU_DTYPE),
        compiler_params=_cparams("parallel"),
    )(dq, dk, dv, dk_meta, dv_meta, cos, sin_a, sin_b)


def _s5_bwd(d_gated, y, proj, carry_in, ssm, w_glu, b_glu3, layer):
    rows = y.shape[0]
    n_chunks = rows // BLK
    b_mat, c_mat, t_re, t_im, d_skip = (ssm[k] for k in ("b_mat", "c_mat", "t_re", "t_im", "d_skip"))

    def body(dz_ref, y_ref, u_ref, cin_ref, bm_ref, cm_ref, tre_ref, tim_ref, d_ref, wg_ref, bg_ref,
             du_ref, dwg_ref, dbg_ref, dd_ref, dbm_ref, dcm_ref, dab_ref,
             lam_carry, bu_scr, s_scr, sp_scr, g_scr, lam_scr):
        step = pl.program_id(0)
        chunk = n_chunks - 1 - step

        @pl.when(step == 0)
        def _():
            for r in (dwg_ref, dbg_ref, dd_ref, dbm_ref, dcm_ref, dab_ref, lam_carry):
                r[...] = jnp.zeros_like(r)

        y = y_ref[...]
        u = u_ref[...]
        d_o = dz_ref[...]
        z, t = _gelu_parts(y)
        zb = z.astype(MXU_DTYPE)
        sg = _sigmoid(_dot(zb, wg_ref[...]) + bg_ref[...])
        dgl = d_o * z * (sg * (1.0 - sg))
        dglb = dgl.astype(MXU_DTYPE)
        dz = d_o * sg + _dot_nt(dglb, wg_ref[...])
        dwg_ref[...] += _dot_tn(zb, dglb)
        dbg_ref[...] += jnp.sum(dgl, axis=0, keepdims=True)
        dy = dz * _gelu_grad(y, t)
        dd_ref[...] += jnp.sum(dy * u, axis=0, keepdims=True)
        grow = lax.broadcasted_iota(jnp.int32, (BLK, 128), 0) + chunk * BLK
        for sb in range(N_SB):
            cols = slice(sb * 128, (sb + 1) * 128)
            u_sb = u[:, cols].astype(MXU_DTYPE)
            dy_sb = dy[:, cols]
            dyb = dy_sb.astype(MXU_DTYPE)
            bu_scr[sb] = _dot(u_sb, bm_ref[sb])
            _scan_tiles(bu_scr.at[sb], s_scr.at[sb], tre_ref, tim_ref, sb,
                        cin_ref[2 * sb:2 * sb + 1, :], cin_ref[2 * sb + 1:2 * sb + 2, :], False, prev_ref=sp_scr.at[sb])
            dcm_ref[sb] += _dot_tn(s_scr[sb].astype(MXU_DTYPE), dyb)
            g_scr[sb] = _dot_nt(dyb, cm_ref[sb])
            n_r, n_i = _scan_tiles(g_scr.at[sb], lam_scr.at[sb], tre_ref, tim_ref, sb,
                                   lam_carry[2 * sb:2 * sb + 1, :], lam_carry[2 * sb + 1:2 * sb + 2, :], True)
            lam_carry[2 * sb:2 * sb + 1, :] = n_r
            lam_carry[2 * sb + 1:2 * sb + 2, :] = n_i
            lr, li = lam_scr[sb, :, :SB_STATES], lam_scr[sb, :, SB_STATES:]
            spr, spi = sp_scr[sb, :, :SB_STATES], sp_scr[sb, :, SB_STATES:]
            dab_ref[2 * sb:2 * sb + 1, :] += jnp.sum(spr * lr + spi * li, axis=0, keepdims=True)
            dab_ref[2 * sb + 1:2 * sb + 2, :] += jnp.sum(spr * li - spi * lr, axis=0, keepdims=True)
            lam = lam_scr[sb].astype(MXU_DTYPE)
            dbm_ref[sb] += _dot_tn(u_sb, lam)
            du = _dot_nt(lam, bm_ref[sb]) + d_ref[:, cols] * dy_sb
            du_ref[:, cols] = jnp.where(grow >= PAD_ROWS, du, 0.0).astype(MXU_DTYPE)

    rev = lambda j: (n_chunks - 1 - j, 0)
    full = lambda shape: pl.BlockSpec(shape, lambda j: (0,) * len(shape))
    tables = [full((N_SB, 8, SCAN_TILE, SB_STATES))] * 2
    chunk_scratch = pltpu.VMEM((N_SB, BLK, 2 * SB_STATES), F32)
    return _pcall(
        body, name=f"s5_bwd_l{layer}", grid=(n_chunks,),
        in_specs=[pl.BlockSpec((BLK, D_SSM), rev), pl.BlockSpec((BLK, D_SSM), rev), pl.BlockSpec((BLK, D_SSM), rev),
                  pl.BlockSpec((None, 8, SB_STATES), lambda j: (n_chunks - 1 - j, 0, 0)),
                  full((N_SB, 128, 2 * SB_STATES)), full((N_SB, 2 * SB_STATES, 128))] + tables + [
                  full((1, D_SSM)), full((D_SSM, D_SSM)),
                  pl.BlockSpec((None, 1, D_SSM), lambda j: (layer, 0, 0))],
        out_specs=[pl.BlockSpec((BLK, D_SSM), rev), full((D_SSM, D_SSM)), full((1, D_SSM)), full((1, D_SSM)),
                   full((N_SB, 128, 2 * SB_STATES)), full((N_SB, 2 * SB_STATES, 128)), full((8, SB_STATES))],
        out_shape=[SDS((rows, D_SSM), MXU_DTYPE), SDS((D_SSM, D_SSM), F32), SDS((1, D_SSM), F32), SDS((1, D_SSM), F32),
                   SDS((N_SB, 128, 2 * SB_STATES), F32), SDS((N_SB, 2 * SB_STATES, 128), F32), SDS((8, SB_STATES), F32)],
        scratch_shapes=[pltpu.VMEM((8, SB_STATES), F32)] + [chunk_scratch] * 5,
        compiler_params=_cparams("arbitrary"),
    )(d_gated, y, proj, carry_in, b_mat, c_mat, t_re, t_im, d_skip, w_glu, b_glu3)


DPROJ_PIECES = ((0, 1), (1, 3), (4, 2), (6, 2))


def _in_bwd(dproj_pieces, dhm, hres, gain3, w_in_g, layer):
    rows = hres.shape[0]
    tm = _row_tile(rows)

    def body(du_ref, dqkv_ref, dgs_ref, dga_ref, dh_ref, x_ref, g_ref, w_ref, dx_ref, dg_ref, acc):
        i = pl.program_id(0)
        j = pl.program_id(1)

        @pl.when((i == 0) & (j == 0))
        def _():
            dg_ref[...] = jnp.zeros_like(dg_ref)

        @pl.when(j == 0)
        def _():
            acc[...] = jnp.zeros_like(acc)

        for piece_ref, (first, count) in zip((du_ref, dqkv_ref, dgs_ref, dga_ref), DPROJ_PIECES):
            @pl.when((j >= first) & (j < first + count))
            def _():
                acc[...] += _dot_nt(piece_ref[...], w_ref[...])

        @pl.when(j == N_DEV - 1)
        def _():
            dx, dg = _rms_bwd(x_ref[...], g_ref[...], acc[...])
            dg_ref[...] += dg
            dx_ref[...] = dh_ref[...] + dx

    row_d = pl.BlockSpec((tm, D), lambda i, j: (i, 0))

    def piece_spec(first, count):
        return pl.BlockSpec((tm, COL_SHARD), lambda i, j: (i, jnp.clip(j - first, 0, count - 1)))

    return _pcall(
        body, name=f"in_bwd_l{layer}", grid=(rows // tm, N_DEV),
        in_specs=[piece_spec(*p) for p in DPROJ_PIECES] + [
                  row_d, row_d,
                  pl.BlockSpec((None, 1, D), lambda i, j: (layer, 0, 0)),
                  pl.BlockSpec((None, None, D, COL_SHARD), lambda i, j: (j, layer, 0, 0))],
        out_specs=[row_d, pl.BlockSpec((1, D), lambda i, j: (0, 0))],
        out_shape=[SDS((rows, D), F32), SDS((1, D), F32)],
        scratch_shapes=[pltpu.VMEM((tm, D), F32)],
        compiler_params=_cparams("arbitrary", "arbitrary"),
    )(*dproj_pieces, dhm, hres, gain3, w_in_g)


_ADAM_C1 = 1.0 / (1.0 - ADAM_B1 ** ADAM_STEP)
_ADAM_C2 = 1.0 / (1.0 - ADAM_B2 ** ADAM_STEP)


def _adam_math(w, g, m, v):
    m = ADAM_B1 * m + (1.0 - ADAM_B1) * g
    v = ADAM_B2 * v + (1.0 - ADAM_B2) * (g * g)
    delta = -ADAM_LR * ((m * _ADAM_C1) / (jnp.sqrt(v * _ADAM_C2) + ADAM_EPS) + ADAM_WD * w)
    return delta, m, v


def _adamw_layers(parts0, parts1, w, m, v, name):
    _, rows, cols = w.shape
    tr = min(rows, (1 << 16) // cols)
    nt = rows // tr

    def body(p0_ref, p1_ref, w_ref, m_ref, v_ref, g_ref, d_ref, nm_ref, nv_ref):
        layer = pl.program_id(0)

        def run(p_ref):
            g = p_ref[0].astype(F32)
            for s in range(1, N_DEV):
                g = g + p_ref[s].astype(F32)
            delta, nm, nv = _adam_math(w_ref[...], g, m_ref[...], v_ref[...])
            g_ref[...] = g
            d_ref[...] = delta
            nm_ref[...] = nm
            nv_ref[...] = nv

        @pl.when(layer == 0)
        def _():
            run(p0_ref)

        @pl.when(layer == 1)
        def _():
            run(p1_ref)

    wspec = pl.BlockSpec((None, tr, cols), lambda l, i: (l, i, 0))
    return _pcall(
        body, name=name, grid=(2, nt),
        in_specs=[pl.BlockSpec((N_DEV, tr, cols), lambda l, i: (0, jnp.where(l == 0, i, nt - 1), 0)),
                  pl.BlockSpec((N_DEV, tr, cols), lambda l, i: (0, jnp.where(l == 1, i, 0), 0)),
                  wspec, wspec, wspec],
        out_specs=[wspec] * 4, out_shape=[SDS(w.shape, F32)] * 4,
        compiler_params=_cparams("arbitrary", "arbitrary"),
    )(parts0, parts1, w, m, v)


def _adamw_packed(g, w, m, v, name):
    def body(g_ref, w_ref, m_ref, v_ref, d_ref, nm_ref, nv_ref):
        delta, nm, nv = _adam_math(w_ref[...], g_ref[...], m_ref[...], v_ref[...])
        d_ref[...] = delta
        nm_ref[...] = nm
        nv_ref[...] = nv

    vmem = pl.BlockSpec(memory_space=pltpu.VMEM)
    return _pcall(body, name=name, out_shape=[SDS(g.shape, F32)] * 3, in_specs=[vmem] * 4, out_specs=[vmem] * 3,
                  compiler_params=_cparams())(g, w, m, v)


def _ssm_discretize(a_re, a_im, log_dt, b_re, b_im):
    dt = jnp.exp(log_dt)[:, None]
    mag = jnp.exp(a_re * dt)
    ang = a_im * dt
    ab_re, ab_im = mag * jnp.cos(ang), mag * jnp.sin(ang)
    xr, xi = ab_re - 1.0, ab_im
    den = a_re * a_re + a_im * a_im
    q_re = (xr * a_re + xi * a_im) / den
    q_im = (xi * a_re - xr * a_im) / den
    bb_re = q_re[..., None] * b_re - q_im[..., None] * b_im
    bb_im = q_re[..., None] * b_im + q_im[..., None] * b_re
    return ab_re, ab_im, bb_re, bb_im


def _block_diag_b(bb):
    m = jnp.einsum("sgnc,gh->sgchn", bb.reshape(N_SB, 8, N_STATE, GROUP_CH), jnp.eye(8, dtype=F32))
    return m.reshape(N_SB, 128, SB_STATES)


def _block_diag_b_t(dm):
    return jnp.einsum("sgchn,gh->sgnc", dm.reshape(N_SB, 8, GROUP_CH, 8, N_STATE),
                      jnp.eye(8, dtype=F32)).reshape(N_GROUPS, N_STATE, GROUP_CH)


def _block_diag_c(cc):
    m = jnp.einsum("sgcn,gh->sgnhc", cc.reshape(N_SB, 8, GROUP_CH, N_STATE), jnp.eye(8, dtype=F32))
    return m.reshape(N_SB, SB_STATES, 128)


def _block_diag_c_t(dm):
    return jnp.einsum("sgnhc,gh->sgcn", dm.reshape(N_SB, 8, N_STATE, 8, GROUP_CH),
                      jnp.eye(8, dtype=F32)).reshape(N_GROUPS, GROUP_CH, N_STATE)


def _ssm_tables(ab_re, ab_im, bb_re, bb_im, c_re, c_im, d_skip):
    pr, pi = ab_re.reshape(1, -1), ab_im.reshape(1, -1)
    cr, ci = pr, pi
    squares = []
    for _ in range(3):
        squares.append((cr, ci))
        pr, pi = (jnp.concatenate([pr, pr * cr - pi * ci], axis=0),
                  jnp.concatenate([pi, pr * ci + pi * cr], axis=0))
        cr, ci = cr * cr - ci * ci, 2.0 * cr * ci
    r = jnp.arange(SCAN_TILE)[:, None]
    fwd = [(jnp.where(r >= (1 << k), squares[k][0], 0.0), jnp.where(r >= (1 << k), squares[k][1], 0.0))
           for k in range(3)] + [(pr, pi)]
    rev = [(jnp.where(r < SCAN_TILE - (1 << k), squares[k][0], 0.0),
            jnp.where(r < SCAN_TILE - (1 << k), -squares[k][1], 0.0)) for k in range(3)] + [(pr[::-1], -pi[::-1])]
    table = lambda part: jnp.stack([e[part] for e in fwd + rev]).reshape(
        8, SCAN_TILE, N_SB, SB_STATES).transpose(2, 0, 1, 3)
    return dict(
        b_mat=jnp.concatenate([_block_diag_b(bb_re), _block_diag_b(bb_im)], axis=-1).astype(MXU_DTYPE),
        c_mat=jnp.concatenate([_block_diag_c(c_re), -_block_diag_c(c_im)], axis=1).astype(MXU_DTYPE),
        t_re=table(0), t_im=table(1),
        d_skip=d_skip.reshape(1, D_SSM))


def _rope_tables(rows):
    pos = (jnp.arange(rows, dtype=jnp.int32) - PAD_ROWS).astype(F32)
    inv_freq = 1.0 / (ROPE_THETA ** (jnp.arange(0, HEAD_DIM, 2, dtype=F32) / HEAD_DIM))
    ang = pos[:, None] * inv_freq[None, :]
    ang = jnp.concatenate([ang, ang, ang, ang], axis=-1)
    first_half = (jnp.arange(128) % HEAD_DIM) < HEAD_DIM // 2
    sin = jnp.sin(ang)
    return jnp.cos(ang), jnp.where(first_half, -sin, 0.0), jnp.where(first_half, 0.0, sin)


def _pack(arrays):
    flat = jnp.concatenate([a.reshape(-1).astype(F32) for a in arrays])
    pad = (-flat.shape[0]) % 1024
    return jnp.pad(flat, (0, pad)).reshape(-1, 128)


def _unpack(packed, like):
    flat = packed.reshape(-1)
    out, off = [], 0
    for a in like:
        n = math.prod(a.shape)
        out.append(flat[off:off + n].reshape(a.shape))
        off += n
    return out


BIG = ("w_in", "w_glu", "w_o_ssm", "w_o_attn", "w_out", "w_up", "w_down")
WEIGHTS = ("meta_tokens", "norm_mix_pre", "norm_mix_post", "norm_mlp_pre", "norm_mlp_post", "w_in",
           "ssm_a_re", "ssm_a_im", "ssm_log_dt", "ssm_b_re", "ssm_b_im", "ssm_c_re", "ssm_c_im", "ssm_d",
           "w_glu", "b_glu", "attn_sinks", "w_o_ssm", "w_o_attn", "w_out", "w_up", "w_down")
SMALL = tuple(n for n in WEIGHTS if n not in BIG)


def kernel(x, meta_tokens, norm_mix_pre, norm_mix_post, norm_mlp_pre, norm_mlp_post, w_in, ssm_a_re, ssm_a_im, ssm_log_dt, ssm_b_re, ssm_b_im, ssm_c_re, ssm_c_im, ssm_d, w_glu, b_glu, attn_sinks, w_o_ssm, w_o_attn, w_out, w_up, w_down, loss_target, m_meta_tokens, m_norm_mix_pre, m_norm_mix_post, m_norm_mlp_pre, m_norm_mlp_post, m_w_in, m_ssm_a_re, m_ssm_a_im, m_ssm_log_dt, m_ssm_b_re, m_ssm_b_im, m_ssm_c_re, m_ssm_c_im, m_ssm_d, m_w_glu, m_b_glu, m_attn_sinks, m_w_o_ssm, m_w_o_attn, m_w_out, m_w_up, m_w_down, v_meta_tokens, v_norm_mix_pre, v_norm_mix_post, v_norm_mlp_pre, v_norm_mlp_post, v_w_in, v_ssm_a_re, v_ssm_a_im, v_ssm_log_dt, v_ssm_b_re, v_ssm_b_im, v_ssm_c_re, v_ssm_c_im, v_ssm_d, v_w_glu, v_b_glu, v_attn_sinks, v_w_o_ssm, v_w_o_attn, v_w_out, v_w_up, v_w_down):
    args = locals()
    w = {n: args[n] for n in WEIGHTS}
    m = {n: args["m_" + n] for n in WEIGHTS}
    v = {n: args["v_" + n] for n in WEIGHTS}
    n_layers = w_in.shape[0]
    seq = x.shape[1]
    rows = seq + BLK
    my_slot = _slot(_mesh_pos())

    gathered = _all_gather([w[n].astype(MXU_DTYPE) for n in BIG] + [meta_tokens], "gather_weights")
    w_in_g, w_glu_g, w_o_ssm_g, w_o_attn_g, w_out_g, w_up_g, w_down_g, meta_g = gathered
    meta_full = meta_g.transpose(1, 0, 2).reshape(N_META, D)
    w_glu_f = w_glu_g.transpose(1, 0, 2, 3).reshape(n_layers, D_SSM, D_SSM)
    w_o_ssm_f = w_o_ssm_g.transpose(1, 2, 0, 3).reshape(n_layers, D_SSM, D)
    w_o_attn_f = w_o_attn_g.transpose(1, 0, 2, 3).reshape(n_layers, D_ATTN, D)
    w_out_f = w_out_g.transpose(1, 0, 2, 3).reshape(n_layers, D, D)

    gains = {n: w[n].reshape(n_layers, 1, D) for n in ("norm_mix_pre", "norm_mix_post", "norm_mlp_pre", "norm_mlp_post")}
    b_glu3 = b_glu.reshape(n_layers, 1, D_SSM)
    cos, sin_a, sin_b = _rope_tables(rows)

    def ssm_setup(l):
        disc, disc_vjp = jax.vjp(_ssm_discretize, ssm_a_re[l], ssm_a_im[l], ssm_log_dt[l], ssm_b_re[l], ssm_b_im[l])
        return _ssm_tables(*disc, ssm_c_re[l], ssm_c_im[l], ssm_d[l]), disc_vjp

    hres = jnp.concatenate([jnp.zeros((PAD_ROWS, D), F32), meta_full, x[0]], axis=0)

    saved = []
    for l in range(n_layers):
        ssm, disc_vjp = ssm_setup(l)
        proj, h = _in_proj(hres, gains["norm_mix_pre"], w_in_g, l)
        q, k, vv = _rope_fwd(proj, cos, sin_a, sin_b, l)
        y, y_ssm, carry_in = _s5_fwd(proj, ssm, w_glu_f[l], b_glu3, l)
        y_attn = _attn_fwd(q, k, vv, attn_sinks, l)
        merged, mix, hres_mid = _merge_fwd(y_ssm, y_attn, proj, hres, w_o_ssm_f[l], w_o_attn_f[l], w_out_f[l],
                                                   gains["norm_mix_post"], l)
        up, act, h2, ff, hres_out = _mlp_fwd(hres_mid, gains["norm_mlp_pre"], gains["norm_mlp_post"], w_up_g, w_down_g, l)
        saved.append(dict(ssm=ssm, disc_vjp=disc_vjp, hres=hres, proj=proj, h=h, q=q, k=k, v=vv, y=y, y_ssm=y_ssm,
                          carry_in=carry_in, y_attn=y_attn, merged=merged, mix=mix, hres_mid=hres_mid,
                          up=up, act=act, h2=h2, ff=ff))
        hres = hres_out

    dhres, loss_vec = _loss_and_grad(hres, loss_target[0])
    loss = lax.psum(loss_vec[0, 0], MESH_AXES)

    small_grads = {n: [None] * n_layers for n in SMALL if n != "meta_tokens"}
    recv = [None] * n_layers
    for l in reversed(range(n_layers)):
        s = saved[l]
        dff, dup, dhm, dg_mlp_post, dg_mlp_pre = _mlp_bwd(dhres, s["ff"], s["up"], s["hres_mid"], gains["norm_mlp_pre"],
                                                          gains["norm_mlp_post"], w_up_g, w_down_g, l)
        dw_down = _matmul_tn(s["act"], dff, f"dw_down_l{l}").reshape(N_DEV, COL_SHARD, D)
        dw_up = _matmul_tn(s["h2"], dup, f"dw_up_l{l}", dev_major_cols=COL_SHARD)
        dmix, da1, da2, dgs, dga, dy_ssm, dy_attn, dg_mix_post = _merge_bwd(
            dhm, s["mix"], s["y_ssm"], s["y_attn"], s["proj"], w_o_ssm_f[l], w_o_attn_f[l], w_out_f[l], gains["norm_mix_post"], l)
        dw_out = _matmul_tn(s["merged"], dmix, f"dw_out_l{l}").reshape(N_DEV, D // N_DEV, D)
        dw_o_attn = _matmul_tn(s["y_attn"], da2, f"dw_o_attn_l{l}").reshape(N_DEV, D_ATTN // N_DEV, D)
        dw_o_ssm = _matmul_tn(s["y_ssm"], da1, f"dw_o_ssm_l{l}", dev_major_cols=D // N_DEV)
        dq, dk, dv, dk_meta, dv_meta, dsink = _attn_bwd(s["q"], s["k"], s["v"], dy_attn, attn_sinks, l)
        dqkv = _rope_bwd(dq, dk, dv, dk_meta, dv_meta, cos, sin_a, sin_b, l)
        du, dw_glu, db_glu, dd_skip, db_mat, dc_mat, dab = _s5_bwd(dy_ssm, s["y"], s["proj"], s["carry_in"], s["ssm"],
                                                                    w_glu_f[l], b_glu3, l)
        dproj = (du, dqkv, dgs, dga)
        dw_in = jnp.concatenate([_matmul_tn(s["h"], piece, f"dw_in{k}_l{l}", dev_major_cols=COL_SHARD)
                                 for k, piece in enumerate(dproj)], axis=0)
        dhres, dg_mix_pre = _in_bwd(dproj, dhm, s["hres"], gains["norm_mix_pre"], w_in_g, l)

        recv[l] = _exchange_slots([dw_in, dw_glu.astype(XFER_DTYPE).reshape(N_DEV, D_SSM // N_DEV, D_SSM), dw_o_ssm,
                                   dw_o_attn, dw_out, dw_up, dw_down], f"scatter_grads_l{l}")

        dab = dab.reshape(N_SB, 2, SB_STATES)
        da_re, da_im, dlog_dt, db_re, db_im = s["disc_vjp"]((
            dab[:, 0].reshape(N_GROUPS, N_STATE), dab[:, 1].reshape(N_GROUPS, N_STATE),
            _block_diag_b_t(db_mat[..., :SB_STATES]), _block_diag_b_t(db_mat[..., SB_STATES:])))
        for name, val in (("norm_mix_pre", dg_mix_pre[0]), ("norm_mix_post", dg_mix_post[0]),
                          ("norm_mlp_pre", dg_mlp_pre[0]), ("norm_mlp_post", dg_mlp_post[0]),
                          ("ssm_a_re", da_re), ("ssm_a_im", da_im), ("ssm_log_dt", dlog_dt),
                          ("ssm_b_re", db_re), ("ssm_b_im", db_im),
                          ("ssm_c_re", _block_diag_c_t(dc_mat[:, :SB_STATES])),
                          ("ssm_c_im", -_block_diag_c_t(dc_mat[:, SB_STATES:])),
                          ("ssm_d", dd_skip.reshape(N_GROUPS, GROUP_CH)), ("b_glu", db_glu[0]),
                          ("attn_sinks", dsink[0, :N_Q_HEADS])):
            small_grads[name][l] = val

    grad_x = dhres[BLK:][None]
    small_names = [n for n in SMALL if n != "meta_tokens"]
    partial_small = [dhres[PAD_ROWS:BLK]] + [jnp.stack(small_grads[n]) for n in small_names]
    summed = _unpack(_all_reduce_small(_pack(partial_small), "reduce_small_grads"), partial_small)
    grads = dict(zip(small_names, summed[1:]))
    grads["meta_tokens"] = lax.dynamic_slice_in_dim(summed[0], my_slot * (D // N_DEV), D // N_DEV, axis=1)

    delta, new_m, new_v = {}, {}, {}
    for idx, n in enumerate(BIG):
        grads[n], delta[n], new_m[n], new_v[n] = _adamw_layers(recv[0][idx], recv[1][idx], w[n], m[n], v[n], f"adamw_{n}")
    like = [w[n] for n in SMALL]
    d_s, m_s, v_s = _adamw_packed(_pack([grads[n] for n in SMALL]), _pack(like), _pack([m[n] for n in SMALL]),
                                  _pack([v[n] for n in SMALL]), "adamw_small")
    for n, dd, mm, vs in zip(SMALL, _unpack(d_s, like), _unpack(m_s, like), _unpack(v_s, like)):
        delta[n], new_m[n], new_v[n] = dd, mm, vs

    return (loss, grad_x, *[grads[n] for n in WEIGHTS], *[delta[n] for n in WEIGHTS],
            *[new_m[n] for n in WEIGHTS], *[new_v[n] for n in WEIGHTS])
```

```python
import functools
import math

import jax
import jax.numpy as jnp
from jax import lax
from jax.experimental import pallas as pl
from jax.experimental.pallas import tpu as pltpu

F32 = jnp.float32
MXU_DTYPE = jnp.bfloat16
XFER_DTYPE = MXU_DTYPE
_pcall = pl.pallas_call
SDS = jax.ShapeDtypeStruct

D = 1024
D_SSM = 512
D_ATTN = 1024
D_KV = 256
D_FF = 4096
D_IN = 4096
HEAD_DIM = 64
N_Q_HEADS = 16
N_KV_HEADS = 4
Q_PER_KV = 4
N_META = 16
BLK = 128
PAD_ROWS = BLK - N_META
N_GROUPS = 32
N_STATE = 64
GROUP_CH = 16
N_SB = 4
SB_STATES = 512
ROPE_THETA = 10000.0
ATTN_SCALE = HEAD_DIM ** -0.5
NEG_INF = -1e30
RMS_EPS = 1e-6
N_DEV = 8
COL_SHARD = 512

ADAM_LR = 0.001
ADAM_B1 = 0.9
ADAM_B2 = 0.999
ADAM_EPS = 1e-08
ADAM_WD = 0.01
ADAM_STEP = 10

VMEM_LIMIT = 56 * 1024 * 1024
MESH_AXES = ("x", "y", "c")

_NT = (((1,), (1,)), ((), ()))
_TN = (((0,), (0,)), ((), ()))


def _cparams(*sem):
    return pltpu.CompilerParams(dimension_semantics=tuple(sem) if sem else None,
                                vmem_limit_bytes=VMEM_LIMIT)


def _row_tile(rows, cap=640):
    for t in (640, 512, 320, 256, 128):
        if t <= cap and rows % t == 0:
            return t
    raise ValueError(f"unsupported row count {rows}")


def _dot(a, b):
    return jnp.dot(a, b, preferred_element_type=F32)


def _dot_nt(a, b):
    return lax.dot_general(a, b, _NT, preferred_element_type=F32)


def _dot_tn(a, b):
    return lax.dot_general(a, b, _TN, preferred_element_type=F32)


def _sigmoid(x):
    return 1.0 / (1.0 + jnp.exp(-x))


_GELU_C = math.sqrt(2.0 / math.pi)


def _gelu_parts(y):
    t = jnp.tanh(_GELU_C * (y + 0.044715 * (y * y * y)))
    return 0.5 * y * (1.0 + t), t


def _gelu_grad(y, t):
    return 0.5 * (1.0 + t) + 0.5 * y * (1.0 - t * t) * (_GELU_C * (1.0 + 0.134145 * (y * y)))


def _rms_fwd(x, gain):
    r = lax.rsqrt(jnp.mean(x * x, axis=-1, keepdims=True) + RMS_EPS)
    return (x * r) * gain


def _rms_bwd(x, gain, dout):
    r = lax.rsqrt(jnp.mean(x * x, axis=-1, keepdims=True) + RMS_EPS)
    xh = x * r
    dxh = dout * gain
    dx = r * (dxh - xh * jnp.mean(dxh * xh, axis=-1, keepdims=True))
    return dx, jnp.sum(dout * xh, axis=0, keepdims=True)


def _mesh_pos():
    return lax.axis_index("x"), lax.axis_index("y"), lax.axis_index("c")


def _peer(pos, d):
    x, y, c = pos
    return (1 - x if d & 4 else x, 1 - y if d & 2 else y, 1 - c if d & 1 else c)


def _slot(pos):
    return 4 * pos[0] + 2 * pos[1] + pos[2]


def _all_gather(shards, name):
    n = len(shards)

    def body(*refs):
        ins, outs = refs[:n], refs[n:2 * n]
        send_sems, recv_sems, local_sems = refs[2 * n:]
        me = _mesh_pos()
        my_slot = _slot(me)
        local = [pltpu.make_async_copy(ins[k], outs[k].at[my_slot], local_sems.at[k]) for k in range(n)]
        for cp in local:
            cp.start()
        sends = []
        for k in range(n):
            for d in range(1, N_DEV):
                sends.append(pltpu.make_async_remote_copy(
                    src_ref=ins[k], dst_ref=outs[k].at[my_slot],
                    send_sem=send_sems.at[k, d - 1], recv_sem=recv_sems.at[k, d - 1],
                    device_id=_peer(me, d), device_id_type=pl.DeviceIdType.MESH))
        for cp in sends:
            cp.start()
        for k in range(n):
            for d in range(1, N_DEV):
                pltpu.make_async_remote_copy(
                    src_ref=ins[k], dst_ref=outs[k].at[_slot(_peer(me, d))],
                    send_sem=send_sems.at[k, d - 1], recv_sem=recv_sems.at[k, d - 1],
                    device_id=_peer(me, d), device_id_type=pl.DeviceIdType.MESH).wait_recv()
        for cp in sends:
            cp.wait_send()
        for cp in local:
            cp.wait()

    any_spec = pl.BlockSpec(memory_space=pl.ANY)
    return _pcall(
        body, name=name,
        out_shape=[SDS((N_DEV,) + s.shape, s.dtype) for s in shards],
        in_specs=[any_spec] * n, out_specs=[any_spec] * n,
        scratch_shapes=[pltpu.SemaphoreType.DMA((n, N_DEV - 1)), pltpu.SemaphoreType.DMA((n, N_DEV - 1)),
                        pltpu.SemaphoreType.DMA((n,))],
    )(*shards)


def _exchange_slots(parts, name):
    n = len(parts)

    def body(*refs):
        ins, outs = refs[:n], refs[n:2 * n]
        send_sems, recv_sems, local_sems = refs[2 * n:]
        me = _mesh_pos()
        my_slot = _slot(me)
        local = [pltpu.make_async_copy(ins[k].at[my_slot], outs[k].at[my_slot], local_sems.at[k]) for k in range(n)]
        for cp in local:
            cp.start()
        sends = []
        for k in range(n):
            for d in range(1, N_DEV):
                sends.append(pltpu.make_async_remote_copy(
                    src_ref=ins[k].at[_slot(_peer(me, d))], dst_ref=outs[k].at[my_slot],
                    send_sem=send_sems.at[k, d - 1], recv_sem=recv_sems.at[k, d - 1],
                    device_id=_peer(me, d), device_id_type=pl.DeviceIdType.MESH))
        for cp in sends:
            cp.start()
        for k in range(n):
            for d in range(1, N_DEV):
                pltpu.make_async_remote_copy(
                    src_ref=ins[k].at[my_slot], dst_ref=outs[k].at[_slot(_peer(me, d))],
                    send_sem=send_sems.at[k, d - 1], recv_sem=recv_sems.at[k, d - 1],
                    device_id=_peer(me, d), device_id_type=pl.DeviceIdType.MESH).wait_recv()
        for cp in sends:
            cp.wait_send()
        for cp in local:
            cp.wait()

    any_spec = pl.BlockSpec(memory_space=pl.ANY)
    return _pcall(
        body, name=name,
        out_shape=[SDS(p.shape, p.dtype) for p in parts],
        in_specs=[any_spec] * n, out_specs=[any_spec] * n,
        scratch_shapes=[pltpu.SemaphoreType.DMA((n, N_DEV - 1)), pltpu.SemaphoreType.DMA((n, N_DEV - 1)),
                        pltpu.SemaphoreType.DMA((n,))],
    )(*parts)


def _all_reduce_small(packed, name):
    rows = packed.shape[0]

    def body(x_ref, out_ref, gath, send_sems, recv_sems):
        me = _mesh_pos()
        my_slot = _slot(me)
        gath[my_slot] = x_ref[...]
        sends = [pltpu.make_async_remote_copy(
            src_ref=x_ref, dst_ref=gath.at[my_slot],
            send_sem=send_sems.at[d - 1], recv_sem=recv_sems.at[d - 1],
            device_id=_peer(me, d), device_id_type=pl.DeviceIdType.MESH) for d in range(1, N_DEV)]
        for cp in sends:
            cp.start()
        for d in range(1, N_DEV):
            pltpu.make_async_remote_copy(
                src_ref=x_ref, dst_ref=gath.at[_slot(_peer(me, d))],
                send_sem=send_sems.at[d - 1], recv_sem=recv_sems.at[d - 1],
                device_id=_peer(me, d), device_id_type=pl.DeviceIdType.MESH).wait_recv()
        for cp in sends:
            cp.wait_send()
        acc = gath[0]
        for s in range(1, N_DEV):
            acc = acc + gath[s]
        out_ref[...] = acc

    vmem = pl.BlockSpec(memory_space=pltpu.VMEM)
    return _pcall(
        body, name=name, out_shape=SDS(packed.shape, F32), in_specs=[vmem], out_specs=vmem,
        scratch_shapes=[pltpu.VMEM((N_DEV, rows, 128), F32),
                        pltpu.SemaphoreType.DMA((N_DEV - 1,)), pltpu.SemaphoreType.DMA((N_DEV - 1,))],
        compiler_params=_cparams(),
    )(packed)


def _in_proj(hres, gain3, w_in_g, layer):
    rows = hres.shape[0]
    tm = _row_tile(rows)

    def body(x_ref, g_ref, w_ref, proj_ref, h_ref, h_scr):
        @pl.when(pl.program_id(1) == 0)
        def _():
            hn = _rms_fwd(x_ref[...], g_ref[...]).astype(MXU_DTYPE)
            h_scr[...] = hn
            h_ref[...] = hn

        proj_ref[...] = _dot(h_scr[...], w_ref[...])

    return _pcall(
        body, name=f"in_proj_l{layer}", grid=(rows // tm, N_DEV),
        in_specs=[pl.BlockSpec((tm, D), lambda i, j: (i, 0)),
                  pl.BlockSpec((None, 1, D), lambda i, j: (layer, 0, 0)),
                  pl.BlockSpec((None, None, D, COL_SHARD), lambda i, j: (j, layer, 0, 0))],
        out_specs=[pl.BlockSpec((tm, COL_SHARD), lambda i, j: (i, j)),
                   pl.BlockSpec((tm, D), lambda i, j: (i, 0))],
        out_shape=[SDS((rows, D_IN), F32), SDS((rows, D), MXU_DTYPE)],
        scratch_shapes=[pltpu.VMEM((tm, D), MXU_DTYPE)],
        compiler_params=_cparams("parallel", "arbitrary"),
    )(hres, gain3, w_in_g)


def _rope_lanes(t, cos, sin_a, sin_b):
    return t * cos + pltpu.roll(t, 96, 1) * sin_a + pltpu.roll(t, 32, 1) * sin_b


def _rope_fwd(proj, cos, sin_a, sin_b, layer):
    rows = proj.shape[0]
    tm = _row_tile(rows)

    def body(q0_ref, q1_ref, kv_ref, c_ref, a_ref, b_ref, qo_ref, ko_ref, vo_ref):
        c, a, b = c_ref[...], a_ref[...], b_ref[...]
        for half, q_ref in enumerate((q0_ref, q1_ref)):
            for t in range(4):
                x = q_ref[:, t * 128:(t + 1) * 128]
                lo = half * 512 + t * 128
                qo_ref[:, lo:lo + 128] = (_rope_lanes(x, c, a, b) * ATTN_SCALE).astype(MXU_DTYPE)
        for t in range(2):
            x = kv_ref[:, t * 128:(t + 1) * 128]
            ko_ref[:, t * 128:(t + 1) * 128] = _rope_lanes(x, c, a, b).astype(MXU_DTYPE)
        vo_ref[...] = kv_ref[:, D_KV:2 * D_KV].astype(MXU_DTYPE)

    tab = pl.BlockSpec((tm, 128), lambda i: (i, 0))
    return _pcall(
        body, name=f"rope_fwd_l{layer}", grid=(rows // tm,),
        in_specs=[pl.BlockSpec((tm, 512), lambda i: (i, 1)), pl.BlockSpec((tm, 512), lambda i: (i, 2)),
                  pl.BlockSpec((tm, 512), lambda i: (i, 3)), tab, tab, tab],
        out_specs=[pl.BlockSpec((tm, D_ATTN), lambda i: (i, 0)), pl.BlockSpec((tm, D_KV), lambda i: (i, 0)),
                   pl.BlockSpec((tm, D_KV), lambda i: (i, 0))],
        out_shape=[SDS((rows, D_ATTN), MXU_DTYPE), SDS((rows, D_KV), MXU_DTYPE), SDS((rows, D_KV), MXU_DTYPE)],
        compiler_params=_cparams("parallel"),
    )(proj, proj, proj, cos, sin_a, sin_b)


SCAN_TILE = 8


def _scan_tiles(x_ref, out_ref, tre_ref, tim_ref, sb, t_r, t_i, reverse, prev_ref=None):
    base = 4 if reverse else 0
    n_tiles = BLK // SCAN_TILE
    row = lax.broadcasted_iota(jnp.int32, (SCAN_TILE, SB_STATES), 0)
    for j in (range(n_tiles - 1, -1, -1) if reverse else range(n_tiles)):
        rows = slice(SCAN_TILE * j, SCAN_TILE * (j + 1))
        xr = x_ref[rows, :SB_STATES]
        xi = x_ref[rows, SB_STATES:]
        for k in range(3):
            shift = SCAN_TILE - (1 << k) if reverse else (1 << k)
            rr = pltpu.roll(xr, shift, 0)
            ri = pltpu.roll(xi, shift, 0)
            ar = tre_ref[sb, base + k]
            ai = tim_ref[sb, base + k]
            xr, xi = xr + (ar * rr - ai * ri), xi + (ar * ri + ai * rr)
        pr = tre_ref[sb, base + 3]
        pi = tim_ref[sb, base + 3]
        xr, xi = xr + (pr * t_r - pi * t_i), xi + (pr * t_i + pi * t_r)
        out_ref[rows, :SB_STATES] = xr
        out_ref[rows, SB_STATES:] = xi
        if prev_ref is not None:
            prev_ref[rows, :SB_STATES] = jnp.where(row == 0, t_r, pltpu.roll(xr, 1, 0))
            prev_ref[rows, SB_STATES:] = jnp.where(row == 0, t_i, pltpu.roll(xi, 1, 0))
        edge = slice(0, 1) if reverse else slice(SCAN_TILE - 1, SCAN_TILE)
        t_r, t_i = xr[edge], xi[edge]
    return t_r, t_i


def _s5_fwd(proj, ssm, w_glu, b_glu3, layer):
    rows = proj.shape[0]
    n_chunks = rows // BLK
    b_mat, c_mat, t_re, t_im, d_skip = (ssm[k] for k in ("b_mat", "c_mat", "t_re", "t_im", "d_skip"))

    def body(u_ref, bm_ref, cm_ref, tre_ref, tim_ref, d_ref, wg_ref, bg_ref,
             y_ref, ys_ref, cin_ref, carry, bu_scr, s_scr):
        @pl.when(pl.program_id(0) == 0)
        def _():
            carry[...] = jnp.zeros_like(carry)

        cin_ref[...] = carry[...]
        u = u_ref[...]
        for sb in range(N_SB):
            cols = slice(sb * 128, (sb + 1) * 128)
            u_sb = u[:, cols]
            bu_scr[sb] = _dot(u_sb.astype(MXU_DTYPE), bm_ref[sb])
            t_r, t_i = _scan_tiles(bu_scr.at[sb], s_scr.at[sb], tre_ref, tim_ref, sb,
                                   carry[2 * sb:2 * sb + 1, :], carry[2 * sb + 1:2 * sb + 2, :], False)
            carry[2 * sb:2 * sb + 1, :] = t_r
            carry[2 * sb + 1:2 * sb + 2, :] = t_i
            y_ref[:, cols] = _dot(s_scr[sb].astype(MXU_DTYPE), cm_ref[sb]) + d_ref[:, cols] * u_sb
        z, _ = _gelu_parts(y_ref[...])
        gl = _dot(z.astype(MXU_DTYPE), wg_ref[...]) + bg_ref[...]
        ys_ref[...] = (z * _sigmoid(gl)).astype(MXU_DTYPE)

    full = lambda shape: pl.BlockSpec(shape, lambda j: (0,) * len(shape))
    return _pcall(
        body, name=f"s5_fwd_l{layer}", grid=(n_chunks,),
        in_specs=[pl.BlockSpec((BLK, D_SSM), lambda j: (j, 0)),
                  full((N_SB, 128, 2 * SB_STATES)), full((N_SB, 2 * SB_STATES, 128)),
                  full((N_SB, 8, SCAN_TILE, SB_STATES)), full((N_SB, 8, SCAN_TILE, SB_STATES)),
                  full((1, D_SSM)), full((D_SSM, D_SSM)),
                  pl.BlockSpec((None, 1, D_SSM), lambda j: (layer, 0, 0))],
        out_specs=[pl.BlockSpec((BLK, D_SSM), lambda j: (j, 0)), pl.BlockSpec((BLK, D_SSM), lambda j: (j, 0)),
                   pl.BlockSpec((None, 8, SB_STATES), lambda j: (j, 0, 0))],
        out_shape=[SDS((rows, D_SSM), F32), SDS((rows, D_SSM), MXU_DTYPE), SDS((n_chunks, 8, SB_STATES), F32)],
        scratch_shapes=[pltpu.VMEM((8, SB_STATES), F32), pltpu.VMEM((N_SB, BLK, 2 * SB_STATES), F32),
                        pltpu.VMEM((N_SB, BLK, 2 * SB_STATES), F32)],
        compiler_params=_cparams("arbitrary"),
    )(proj, b_mat, c_mat, t_re, t_im, d_skip, w_glu, b_glu3)


def _attn_mask(i):
    row = lax.broadcasted_iota(jnp.int32, (BLK, 3 * BLK), 0) + i * BLK
    col = lax.broadcasted_iota(jnp.int32, (BLK, 3 * BLK), 1)
    seg = jnp.right_shift(col, 7)
    c = jnp.bitwise_and(col, BLK - 1)
    kidx = c + (i + seg - 2) * BLK
    ok_meta = (seg == 0) & (c >= PAD_ROWS) & (row - c >= BLK)
    ok_win = (seg > 0) & (kidx >= PAD_ROWS) & (kidx <= row) & (row - kidx < BLK)
    return jnp.where(ok_meta | ok_win, 0.0, NEG_INF)


def _head_lanes(h):
    return slice(h * HEAD_DIM, (h + 1) * HEAD_DIM)


def _group_rows(ref, kvh):
    return jnp.concatenate([ref[:, _head_lanes(kvh * Q_PER_KV + g)] for g in range(Q_PER_KV)], axis=0)


def _group_bias(bias, sink_ref, layer, kvh):
    first_col = lax.broadcasted_iota(jnp.int32, (BLK, BLK), 1) == 0
    slabs = []
    for g in range(Q_PER_KV):
        first = jnp.where(first_col, sink_ref[layer, kvh * Q_PER_KV + g], bias[:, :BLK])
        slabs.append(jnp.concatenate([first, bias[:, BLK:]], axis=1))
    return jnp.concatenate(slabs, axis=0)


def _attn_probs(q4, k3, bias4):
    s = _dot_nt(q4, k3) + bias4
    e = jnp.exp(s - jnp.max(s, axis=-1, keepdims=True))
    return e * (1.0 / jnp.sum(e, axis=-1, keepdims=True))


def _attn_fwd(q, k, v, sinks, layer):
    rows = q.shape[0]
    n_blk = rows // BLK

    def body(sink_ref, q_ref, km_ref, kp_ref, kc_ref, vm_ref, vp_ref, vc_ref, o_ref):
        bias = _attn_mask(pl.program_id(0))
        for kvh in range(N_KV_HEADS):
            lanes = _head_lanes(kvh)
            k3 = jnp.concatenate([km_ref[:, lanes], kp_ref[:, lanes], kc_ref[:, lanes]], axis=0)
            v3 = jnp.concatenate([vm_ref[:, lanes], vp_ref[:, lanes], vc_ref[:, lanes]], axis=0)
            p = _attn_probs(_group_rows(q_ref, kvh), k3, _group_bias(bias, sink_ref, layer, kvh))
            o4 = _dot(p.astype(MXU_DTYPE), v3).astype(MXU_DTYPE)
            for g in range(Q_PER_KV):
                o_ref[:, _head_lanes(kvh * Q_PER_KV + g)] = o4[g * BLK:(g + 1) * BLK]

    kv_meta = pl.BlockSpec((BLK, D_KV), lambda i: (0, 0))
    kv_prev = pl.BlockSpec((BLK, D_KV), lambda i: (jnp.maximum(i - 1, 0), 0))
    kv_cur = pl.BlockSpec((BLK, D_KV), lambda i: (i, 0))
    return _pcall(
        body, name=f"attn_fwd_l{layer}", grid=(n_blk,),
        in_specs=[pl.BlockSpec(memory_space=pltpu.SMEM),
                  pl.BlockSpec((BLK, D_ATTN), lambda i: (i, 0)),
                  kv_meta, kv_prev, kv_cur, kv_meta, kv_prev, kv_cur],
        out_specs=pl.BlockSpec((BLK, D_ATTN), lambda i: (i, 0)),
        out_shape=SDS((rows, D_ATTN), MXU_DTYPE),
        compiler_params=_cparams("parallel"),
    )(sinks, q, k, k, k, v, v, v)


def _merge_fwd(y_ssm, y_attn, proj, hres, w_o_ssm, w_o_attn, w_out, gain3, layer):
    rows = hres.shape[0]
    tm = _row_tile(rows, 320)

    def body(ys_ref, ya_ref, gs_ref, ga_ref, x_ref, wos_ref, woa_ref, wout_ref, g_ref,
             mg_ref, mix_ref, out_ref):
        a1 = _dot(ys_ref[...], wos_ref[...])
        a2 = _dot(ya_ref[...], woa_ref[...])
        merged = (_sigmoid(gs_ref[...]) * a1 + _sigmoid(ga_ref[...]) * a2).astype(MXU_DTYPE)
        mg_ref[...] = merged
        mix = _dot(merged, wout_ref[...])
        mix_ref[...] = mix
        out_ref[...] = x_ref[...] + _rms_fwd(mix, g_ref[...])

    row_d = pl.BlockSpec((tm, D), lambda i: (i, 0))
    full = lambda shape: pl.BlockSpec(shape, lambda i: (0,) * len(shape))
    return _pcall(
        body, name=f"merge_fwd_l{layer}", grid=(rows // tm,),
        in_specs=[pl.BlockSpec((tm, D_SSM), lambda i: (i, 0)), row_d,
                  pl.BlockSpec((tm, D), lambda i: (i, 2)), pl.BlockSpec((tm, D), lambda i: (i, 3)), row_d,
                  full((D_SSM, D)), full((D_ATTN, D)), full((D, D)),
                  pl.BlockSpec((None, 1, D), lambda i: (layer, 0, 0))],
        out_specs=[row_d, row_d, row_d],
        out_shape=[SDS((rows, D), MXU_DTYPE), SDS((rows, D), F32), SDS((rows, D), F32)],
        compiler_params=_cparams("parallel"),
    )(y_ssm, y_attn, proj, proj, hres, w_o_ssm, w_o_attn, w_out, gain3)


def _mlp_fwd(hres, gain_pre3, gain_post3, w_up_g, w_down_g, layer):
    rows = hres.shape[0]
    tm = _row_tile(rows)

    def body(x_ref, gp_ref, gq_ref, wu_ref, wd_ref, up_ref, act_ref, h_ref, ff_ref, out_ref, h_scr, acc):
        kf = pl.program_id(1)

        @pl.when(kf == 0)
        def _():
            hn = _rms_fwd(x_ref[...], gp_ref[...]).astype(MXU_DTYPE)
            h_scr[...] = hn
            h_ref[...] = hn
            acc[...] = jnp.zeros_like(acc)

        up = _dot(h_scr[...], wu_ref[...])
        up_ref[...] = up
        r = jnp.maximum(up, 0.0)
        act = (r * r).astype(MXU_DTYPE)
        act_ref[...] = act
        acc[...] += _dot(act, wd_ref[...])

        @pl.when(kf == N_DEV - 1)
        def _():
            ff = acc[...]
            ff_ref[...] = ff
            out_ref[...] = x_ref[...] + _rms_fwd(ff, gq_ref[...])

    row_d = pl.BlockSpec((tm, D), lambda i, k: (i, 0))
    gain = pl.BlockSpec((None, 1, D), lambda i, k: (layer, 0, 0))
    return _pcall(
        body, name=f"mlp_fwd_l{layer}", grid=(rows // tm, N_DEV),
        in_specs=[row_d, gain, gain,
                  pl.BlockSpec((None, None, D, COL_SHARD), lambda i, k: (k, layer, 0, 0)),
                  pl.BlockSpec((None, None, COL_SHARD, D), lambda i, k: (k, layer, 0, 0))],
        out_specs=[pl.BlockSpec((tm, COL_SHARD), lambda i, k: (i, k)),
                   pl.BlockSpec((tm, COL_SHARD), lambda i, k: (i, k)), row_d, row_d, row_d],
        out_shape=[SDS((rows, D_FF), F32), SDS((rows, D_FF), MXU_DTYPE), SDS((rows, D), MXU_DTYPE),
                   SDS((rows, D), F32), SDS((rows, D), F32)],
        scratch_shapes=[pltpu.VMEM((tm, D), MXU_DTYPE), pltpu.VMEM((tm, D), F32)],
        compiler_params=_cparams("parallel", "arbitrary"),
    )(hres, gain_pre3, gain_post3, w_up_g, w_down_g)


def _loss_and_grad(hres, target):
    rows = hres.shape[0]
    n_blk = rows // BLK

    def body(y_ref, t_ref, dy_ref, loss_ref):
        i = pl.program_id(0)

        @pl.when(i == 0)
        def _():
            dy_ref[...] = jnp.zeros_like(dy_ref)
            loss_ref[...] = jnp.zeros_like(loss_ref)

        @pl.when(i > 0)
        def _():
            err = y_ref[...] - t_ref[...]
            dy_ref[...] = err * (1.0 / D)
            loss_ref[...] += jnp.sum(err * err) * (0.5 / D)

    return _pcall(
        body, name="loss", grid=(n_blk,),
        in_specs=[pl.BlockSpec((BLK, D), lambda i: (i, 0)),
                  pl.BlockSpec((BLK, D), lambda i: (jnp.maximum(i - 1, 0), 0))],
        out_specs=[pl.BlockSpec((BLK, D), lambda i: (i, 0)), pl.BlockSpec((1, 128), lambda i: (0, 0))],
        out_shape=[SDS((rows, D), F32), SDS((1, 128), F32)],
        compiler_params=_cparams("arbitrary"),
    )(hres, target)


def _matmul_tn(a, b, name, dev_major_cols=None):
    rows, ka = a.shape
    n = b.shape[1]
    ta = min(ka, 1024)
    tn = 1024 if n % 1024 == 0 else 512
    tr = _row_tile(rows)
    n_r = rows // tr

    def body(a_ref, b_ref, o_ref, acc):
        r = pl.program_id(2)

        @pl.when(r == 0)
        def _():
            acc[...] = jnp.zeros_like(acc)

        acc[...] += _dot_tn(a_ref[...], b_ref[...])

        @pl.when(r == n_r - 1)
        def _():
            if dev_major_cols is None:
                o_ref[...] = acc[...].astype(XFER_DTYPE)
            else:
                for s in range(tn // dev_major_cols):
                    o_ref[s] = acc[:, s * dev_major_cols:(s + 1) * dev_major_cols].astype(XFER_DTYPE)

    if dev_major_cols is None:
        out_spec = pl.BlockSpec((ta, tn), lambda i, j, r: (i, j))
        out_shape = SDS((ka, n), XFER_DTYPE)
    else:
        w = dev_major_cols
        out_spec = pl.BlockSpec((tn // w, ta, w), lambda i, j, r: (j, i, 0))
        out_shape = SDS((n // w, ka, w), XFER_DTYPE)
    return _pcall(
        body, name=name, grid=(ka // ta, n // tn, n_r),
        in_specs=[pl.BlockSpec((tr, ta), lambda i, j, r: (r, i)), pl.BlockSpec((tr, tn), lambda i, j, r: (r, j))],
        out_specs=out_spec, out_shape=out_shape,
        scratch_shapes=[pltpu.VMEM((ta, tn), F32)],
        compiler_params=_cparams("parallel", "parallel", "arbitrary"),
    )(a, b)


def _mlp_bwd(dout, ff, up, hres_mid, gain_pre3, gain_post3, w_up_g, w_down_g, layer):
    rows = dout.shape[0]
    tm = _row_tile(rows)

    def body(do_ref, ff_ref, up_ref, x_ref, gp_ref, gq_ref, wu_ref, wd_ref,
             dff_ref, dup_ref, dx_ref, dgq_ref, dgp_ref, dff_scr, acc):
        i = pl.program_id(0)
        kf = pl.program_id(1)

        @pl.when((i == 0) & (kf == 0))
        def _():
            dgq_ref[...] = jnp.zeros_like(dgq_ref)
            dgp_ref[...] = jnp.zeros_like(dgp_ref)

        @pl.when(kf == 0)
        def _():
            dff, dg = _rms_bwd(ff_ref[...], gq_ref[...], do_ref[...])
            dgq_ref[...] += dg
            dffb = dff.astype(MXU_DTYPE)
            dff_scr[...] = dffb
            dff_ref[...] = dffb
            acc[...] = jnp.zeros_like(acc)

        dact = _dot_nt(dff_scr[...], wd_ref[...])
        dup = (dact * (2.0 * jnp.maximum(up_ref[...], 0.0))).astype(MXU_DTYPE)
        dup_ref[...] = dup
        acc[...] += _dot_nt(dup, wu_ref[...])

        @pl.when(kf == N_DEV - 1)
        def _():
            dx, dg = _rms_bwd(x_ref[...], gp_ref[...], acc[...])
            dgp_ref[...] += dg
            dx_ref[...] = do_ref[...] + dx

    row_d = pl.BlockSpec((tm, D), lambda i, k: (i, 0))
    gain = pl.BlockSpec((None, 1, D), lambda i, k: (layer, 0, 0))
    dgain = pl.BlockSpec((1, D), lambda i, k: (0, 0))
    return _pcall(
        body, name=f"mlp_bwd_l{layer}", grid=(rows // tm, N_DEV),
        in_specs=[row_d, row_d, pl.BlockSpec((tm, COL_SHARD), lambda i, k: (i, k)), row_d, gain, gain,
                  pl.BlockSpec((None, None, D, COL_SHARD), lambda i, k: (k, layer, 0, 0)),
                  pl.BlockSpec((None, None, COL_SHARD, D), lambda i, k: (k, layer, 0, 0))],
        out_specs=[row_d, pl.BlockSpec((tm, COL_SHARD), lambda i, k: (i, k)), row_d, dgain, dgain],
        out_shape=[SDS((rows, D), MXU_DTYPE), SDS((rows, D_FF), MXU_DTYPE), SDS((rows, D), F32),
                   SDS((1, D), F32), SDS((1, D), F32)],
        scratch_shapes=[pltpu.VMEM((tm, D), MXU_DTYPE), pltpu.VMEM((tm, D), F32)],
        compiler_params=_cparams("arbitrary", "arbitrary"),
    )(dout, ff, up, hres_mid, gain_pre3, gain_post3, w_up_g, w_down_g)


def _merge_bwd(dhm, mix, y_ssm, y_attn, proj, w_o_ssm, w_o_attn, w_out, gain3, layer):
    rows = dhm.shape[0]
    tm = _row_tile(rows, 320)

    def body(dh_ref, mix_ref, ys_ref, ya_ref, gs_ref, ga_ref, wos_ref, woa_ref, wout_ref, g_ref,
             dmix_ref, da1_ref, da2_ref, dgs_ref, dga_ref, dys_ref, dya_ref, dg_ref):
        @pl.when(pl.program_id(0) == 0)
        def _():
            dg_ref[...] = jnp.zeros_like(dg_ref)

        dmix, dg = _rms_bwd(mix_ref[...], g_ref[...], dh_ref[...])
        dg_ref[...] += dg
        dmixb = dmix.astype(MXU_DTYPE)
        dmix_ref[...] = dmixb
        dmerged = _dot_nt(dmixb, wout_ref[...])
        sg_s = _sigmoid(gs_ref[...])
        sg_a = _sigmoid(ga_ref[...])
        da1 = (dmerged * sg_s).astype(MXU_DTYPE)
        da2 = (dmerged * sg_a).astype(MXU_DTYPE)
        da1_ref[...] = da1
        da2_ref[...] = da2
        a1 = _dot(ys_ref[...], wos_ref[...])
        a2 = _dot(ya_ref[...], woa_ref[...])
        dgs_ref[...] = (dmerged * a1 * (sg_s * (1.0 - sg_s))).astype(MXU_DTYPE)
        dga_ref[...] = (dmerged * a2 * (sg_a * (1.0 - sg_a))).astype(MXU_DTYPE)
        dys_ref[...] = _dot_nt(da1, wos_ref[...])
        dya_ref[...] = _dot_nt(da2, woa_ref[...])

    row_d = pl.BlockSpec((tm, D), lambda i: (i, 0))
    full = lambda shape: pl.BlockSpec(shape, lambda i: (0,) * len(shape))
    return _pcall(
        body, name=f"merge_bwd_l{layer}", grid=(rows // tm,),
        in_specs=[row_d, row_d, pl.BlockSpec((tm, D_SSM), lambda i: (i, 0)), row_d,
                  pl.BlockSpec((tm, D), lambda i: (i, 2)), pl.BlockSpec((tm, D), lambda i: (i, 3)),
                  full((D_SSM, D)), full((D_ATTN, D)), full((D, D)),
                  pl.BlockSpec((None, 1, D), lambda i: (layer, 0, 0))],
        out_specs=[row_d, row_d, row_d, row_d, row_d, pl.BlockSpec((tm, D_SSM), lambda i: (i, 0)), row_d,
                   pl.BlockSpec((1, D), lambda i: (0, 0))],
        out_shape=[SDS((rows, D), MXU_DTYPE)] * 5 + [SDS((rows, D_SSM), F32), SDS((rows, D_ATTN), F32),
                                                      SDS((1, D), F32)],
        compiler_params=_cparams("arbitrary"),
    )(dhm, mix, y_ssm, y_attn, proj, proj, w_o_ssm, w_o_attn, w_out, gain3)


def _attn_bwd(q, k, v, d_out, sinks, layer):
    rows = q.shape[0]
    n_blk = rows // BLK
    last = n_blk - 1

    def body(sink_ref, q_ref, km_ref, kp_ref, kc_ref, vm_ref, vp_ref, vc_ref, do_ref,
             dq_ref, dk_ref, dv_ref, dkm_ref, dvm_ref, ds_ref, dk_carry, dv_carry):
        i = pl.program_id(0)

        @pl.when(i == 0)
        def _():
            dkm_ref[...] = jnp.zeros_like(dkm_ref)
            dvm_ref[...] = jnp.zeros_like(dvm_ref)
            ds_ref[...] = jnp.zeros_like(ds_ref)
            dk_carry[...] = jnp.zeros_like(dk_carry)
            dv_carry[...] = jnp.zeros_like(dv_carry)

        @pl.when(i <= last)
        def _():
            bias = _attn_mask(i)
            for kvh in range(N_KV_HEADS):
                lanes = _head_lanes(kvh)
                k3 = jnp.concatenate([km_ref[:, lanes], kp_ref[:, lanes], kc_ref[:, lanes]], axis=0)
                v3 = jnp.concatenate([vm_ref[:, lanes], vp_ref[:, lanes], vc_ref[:, lanes]], axis=0)
                q4 = _group_rows(q_ref, kvh)
                do4 = _group_rows(do_ref, kvh).astype(MXU_DTYPE)
                p = _attn_probs(q4, k3, _group_bias(bias, sink_ref, layer, kvh))
                dp = _dot_nt(do4, v3)
                dsf = p * (dp - jnp.sum(dp * p, axis=-1, keepdims=True))
                dsc = dsf.astype(MXU_DTYPE)
                dv3 = _dot_tn(p.astype(MXU_DTYPE), do4)
                dk3 = _dot_tn(dsc, q4)
                dq4 = _dot(dsc, k3)
                for g in range(Q_PER_KV):
                    h = kvh * Q_PER_KV + g
                    dq_ref[:, _head_lanes(h)] = dq4[g * BLK:(g + 1) * BLK]
                    ds_ref[h:h + 1, :] += jnp.sum(dsf[g * BLK:(g + 1) * BLK, 0:BLK], axis=0, keepdims=True)
                dkm_ref[:, lanes] += dk3[0:BLK]
                dvm_ref[:, lanes] += dv3[0:BLK]
                dk_ref[:, lanes] = dk_carry[:, lanes] + dk3[BLK:2 * BLK]
                dv_ref[:, lanes] = dv_carry[:, lanes] + dv3[BLK:2 * BLK]
                dk_carry[:, lanes] = dk3[2 * BLK:3 * BLK]
                dv_carry[:, lanes] = dv3[2 * BLK:3 * BLK]

        @pl.when(i == last + 1)
        def _():
            dk_ref[...] = dk_carry[...]
            dv_ref[...] = dv_carry[...]

    cur = lambda i: (jnp.minimum(i, last), 0)
    prev = lambda i: (jnp.clip(i - 1, 0, last), 0)
    kv_meta = pl.BlockSpec((BLK, D_KV), lambda i: (0, 0))
    kv_prev = pl.BlockSpec((BLK, D_KV), prev)
    kv_cur = pl.BlockSpec((BLK, D_KV), cur)
    return _pcall(
        body, name=f"attn_bwd_l{layer}", grid=(n_blk + 1,),
        in_specs=[pl.BlockSpec(memory_space=pltpu.SMEM),
                  pl.BlockSpec((BLK, D_ATTN), cur),
                  kv_meta, kv_prev, kv_cur, kv_meta, kv_prev, kv_cur,
                  pl.BlockSpec((BLK, D_ATTN), cur)],
        out_specs=[pl.BlockSpec((BLK, D_ATTN), cur), kv_prev, kv_prev, kv_meta, kv_meta,
                   pl.BlockSpec((N_Q_HEADS, 128), lambda i: (0, 0))],
        out_shape=[SDS((rows, D_ATTN), F32), SDS((rows, D_KV), F32), SDS((rows, D_KV), F32),
                   SDS((BLK, D_KV), F32), SDS((BLK, D_KV), F32), SDS((N_Q_HEADS, 128), F32)],
        scratch_shapes=[pltpu.VMEM((BLK, D_KV), F32), pltpu.VMEM((BLK, D_KV), F32)],
        compiler_params=_cparams("arbitrary"),
    )(sinks, q, k, k, k, v, v, v, d_out)


def _rope_bwd(dq, dk, dv, dk_meta, dv_meta, cos, sin_a, sin_b, layer):
    rows = dq.shape[0]
    tm = _row_tile(rows)

    def body(dq_ref, dk_ref, dv_ref, dkm_ref, dvm_ref, c_ref, a_ref, b_ref, o_ref):
        c, a, b = c_ref[...], -a_ref[...], -b_ref[...]
        for t in range(8):
            x = dq_ref[:, t * 128:(t + 1) * 128]
            o_ref[:, t * 128:(t + 1) * 128] = (_rope_lanes(x, c, a, b) * ATTN_SCALE).astype(MXU_DTYPE)
        for t in range(2):
            x = dk_ref[:, t * 128:(t + 1) * 128]
            o_ref[:, D_ATTN + t * 128:D_ATTN + (t + 1) * 128] = _rope_lanes(x, c, a, b).astype(MXU_DTYPE)
        o_ref[:, D_ATTN + D_KV:] = dv_ref[...].astype(MXU_DTYPE)

        @pl.when(pl.program_id(0) == 0)
        def _():
            cb, ab, bb = c[0:BLK], a[0:BLK], b[0:BLK]
            is_meta = lax.broadcasted_iota(jnp.int32, (BLK, 128), 0) >= PAD_ROWS
            for t in range(2):
                x = dk_ref[0:BLK, t * 128:(t + 1) * 128] + jnp.where(is_meta, dkm_ref[:, t * 128:(t + 1) * 128], 0.0)
                o_ref[0:BLK, D_ATTN + t * 128:D_ATTN + (t + 1) * 128] = _rope_lanes(x, cb, ab, bb).astype(MXU_DTYPE)
                xv = dv_ref[0:BLK, t * 128:(t + 1) * 128] + jnp.where(is_meta, dvm_ref[:, t * 128:(t + 1) * 128], 0.0)
                o_ref[0:BLK, D_ATTN + D_KV + t * 128:D_ATTN + D_KV + (t + 1) * 128] = xv.astype(MXU_DTYPE)

    tab = pl.BlockSpec((tm, 128), lambda i: (i, 0))
    kv = pl.BlockSpec((tm, D_KV), lambda i: (i, 0))
    meta = pl.BlockSpec((BLK, D_KV), lambda i: (0, 0))
    return _pcall(
        body, name=f"rope_bwd_l{layer}", grid=(rows // tm,),
        in_specs=[pl.BlockSpec((tm, D_ATTN), lambda i: (i, 0)), kv, kv, meta, meta, tab, tab, tab],
        out_specs=pl.BlockSpec((tm, D_ATTN + 2 * D_KV), lambda i: (i, 0)),
        out_shape=SDS((rows, D_ATTN + 2 * D_KV), MXU_DTYPE),
        compiler_params=_cparams("parallel"),
    )(dq, dk, dv, dk_meta, dv_meta, cos, sin_a, sin_b)


def _s5_bwd(d_gated, y, proj, carry_in, ssm, w_glu, b_glu3, layer):
    rows = y.shape[0]
    n_chunks = rows // BLK
    b_mat, c_mat, t_re, t_im, d_skip = (ssm[k] for k in ("b_mat", "c_mat", "t_re", "t_im", "d_skip"))

    def body(dz_ref, y_ref, u_ref, cin_ref, bm_ref, cm_ref, tre_ref, tim_ref, d_ref, wg_ref, bg_ref,
             du_ref, dwg_ref, dbg_ref, dd_ref, dbm_ref, dcm_ref, dab_ref,
             lam_carry, bu_scr, s_scr, sp_scr, g_scr, lam_scr):
        step = pl.program_id(0)
        chunk = n_chunks - 1 - step

        @pl.when(step == 0)
        def _():
            for r in (dwg_ref, dbg_ref, dd_ref, dbm_ref, dcm_ref, dab_ref, lam_carry):
                r[...] = jnp.zeros_like(r)

        y = y_ref[...]
        u = u_ref[...]
        d_o = dz_ref[...]
        z, t = _gelu_parts(y)
        zb = z.astype(MXU_DTYPE)
        sg = _sigmoid(_dot(zb, wg_ref[...]) + bg_ref[...])
        dgl = d_o * z * (sg * (1.0 - sg))
        dglb = dgl.astype(MXU_DTYPE)
        dz = d_o * sg + _dot_nt(dglb, wg_ref[...])
        dwg_ref[...] += _dot_tn(zb, dglb)
        dbg_ref[...] += jnp.sum(dgl, axis=0, keepdims=True)
        dy = dz * _gelu_grad(y, t)
        dd_ref[...] += jnp.sum(dy * u, axis=0, keepdims=True)
        grow = lax.broadcasted_iota(jnp.int32, (BLK, 128), 0) + chunk * BLK
        for sb in range(N_SB):
            cols = slice(sb * 128, (sb + 1) * 128)
            u_sb = u[:, cols].astype(MXU_DTYPE)
            dy_sb = dy[:, cols]
            dyb = dy_sb.astype(MXU_DTYPE)
            bu_scr[sb] = _dot(u_sb, bm_ref[sb])
            _scan_tiles(bu_scr.at[sb], s_scr.at[sb], tre_ref, tim_ref, sb,
                        cin_ref[2 * sb:2 * sb + 1, :], cin_ref[2 * sb + 1:2 * sb + 2, :], False, prev_ref=sp_scr.at[sb])
            dcm_ref[sb] += _dot_tn(s_scr[sb].astype(MXU_DTYPE), dyb)
            g_scr[sb] = _dot_nt(dyb, cm_ref[sb])
            n_r, n_i = _scan_tiles(g_scr.at[sb], lam_scr.at[sb], tre_ref, tim_ref, sb,
                                   lam_carry[2 * sb:2 * sb + 1, :], lam_carry[2 * sb + 1:2 * sb + 2, :], True)
            lam_carry[2 * sb:2 * sb + 1, :] = n_r
            lam_carry[2 * sb + 1:2 * sb + 2, :] = n_i
            lr, li = lam_scr[sb, :, :SB_STATES], lam_scr[sb, :, SB_STATES:]
            spr, spi = sp_scr[sb, :, :SB_STATES], sp_scr[sb, :, SB_STATES:]
            dab_ref[2 * sb:2 * sb + 1, :] += jnp.sum(spr * lr + spi * li, axis=0, keepdims=True)
            dab_ref[2 * sb + 1:2 * sb + 2, :] += jnp.sum(spr * li - spi * lr, axis=0, keepdims=True)
            lam = lam_scr[sb].astype(MXU_DTYPE)
            dbm_ref[sb] += _dot_tn(u_sb, lam)
            du = _dot_nt(lam, bm_ref[sb]) + d_ref[:, cols] * dy_sb
            du_ref[:, cols] = jnp.where(grow >= PAD_ROWS, du, 0.0).astype(MXU_DTYPE)

    rev = lambda j: (n_chunks - 1 - j, 0)
    full = lambda shape: pl.BlockSpec(shape, lambda j: (0,) * len(shape))
    tables = [full((N_SB, 8, SCAN_TILE, SB_STATES))] * 2
    chunk_scratch = pltpu.VMEM((N_SB, BLK, 2 * SB_STATES), F32)
    return _pcall(
        body, name=f"s5_bwd_l{layer}", grid=(n_chunks,),
        in_specs=[pl.BlockSpec((BLK, D_SSM), rev), pl.BlockSpec((BLK, D_SSM), rev), pl.BlockSpec((BLK, D_SSM), rev),
                  pl.BlockSpec((None, 8, SB_STATES), lambda j: (n_chunks - 1 - j, 0, 0)),
                  full((N_SB, 128, 2 * SB_STATES)), full((N_SB, 2 * SB_STATES, 128))] + tables + [
                  full((1, D_SSM)), full((D_SSM, D_SSM)),
                  pl.BlockSpec((None, 1, D_SSM), lambda j: (layer, 0, 0))],
        out_specs=[pl.BlockSpec((BLK, D_SSM), rev), full((D_SSM, D_SSM)), full((1, D_SSM)), full((1, D_SSM)),
                   full((N_SB, 128, 2 * SB_STATES)), full((N_SB, 2 * SB_STATES, 128)), full((8, SB_STATES))],
        out_shape=[SDS((rows, D_SSM), MXU_DTYPE), SDS((D_SSM, D_SSM), F32), SDS((1, D_SSM), F32), SDS((1, D_SSM), F32),
                   SDS((N_SB, 128, 2 * SB_STATES), F32), SDS((N_SB, 2 * SB_STATES, 128), F32), SDS((8, SB_STATES), F32)],
        scratch_shapes=[pltpu.VMEM((8, SB_STATES), F32)] + [chunk_scratch] * 5,
        compiler_params=_cparams("arbitrary"),
    )(d_gated, y, proj, carry_in, b_mat, c_mat, t_re, t_im, d_skip, w_glu, b_glu3)


DPROJ_PIECES = ((0, 1), (1, 3), (4, 2), (6, 2))


def _in_bwd(dproj_pieces, dhm, hres, gain3, w_in_g, layer):
    rows = hres.shape[0]
    tm = _row_tile(rows)

    def body(du_ref, dqkv_ref, dgs_ref, dga_ref, dh_ref, x_ref, g_ref, w_ref, dx_ref, dg_ref, acc):
        i = pl.program_id(0)
        j = pl.program_id(1)

        @pl.when((i == 0) & (j == 0))
        def _():
            dg_ref[...] = jnp.zeros_like(dg_ref)

        @pl.when(j == 0)
        def _():
            acc[...] = jnp.zeros_like(acc)

        for piece_ref, (first, count) in zip((du_ref, dqkv_ref, dgs_ref, dga_ref), DPROJ_PIECES):
            @pl.when((j >= first) & (j < first + count))
            def _():
                acc[...] += _dot_nt(piece_ref[...], w_ref[...])

        @pl.when(j == N_DEV - 1)
        def _():
            dx, dg = _rms_bwd(x_ref[...], g_ref[...], acc[...])
            dg_ref[...] += dg
            dx_ref[...] = dh_ref[...] + dx

    row_d = pl.BlockSpec((tm, D), lambda i, j: (i, 0))

    def piece_spec(first, count):
        return pl.BlockSpec((tm, COL_SHARD), lambda i, j: (i, jnp.clip(j - first, 0, count - 1)))

    return _pcall(
        body, name=f"in_bwd_l{layer}", grid=(rows // tm, N_DEV),
        in_specs=[piece_spec(*p) for p in DPROJ_PIECES] + [
                  row_d, row_d,
                  pl.BlockSpec((None, 1, D), lambda i, j: (layer, 0, 0)),
                  pl.BlockSpec((None, None, D, COL_SHARD), lambda i, j: (j, layer, 0, 0))],
        out_specs=[row_d, pl.BlockSpec((1, D), lambda i, j: (0, 0))],
        out_shape=[SDS((rows, D), F32), SDS((1, D), F32)],
        scratch_shapes=[pltpu.VMEM((tm, D), F32)],
        compiler_params=_cparams("arbitrary", "arbitrary"),
    )(*dproj_pieces, dhm, hres, gain3, w_in_g)


_ADAM_C1 = 1.0 / (1.0 - ADAM_B1 ** ADAM_STEP)
_ADAM_C2 = 1.0 / (1.0 - ADAM_B2 ** ADAM_STEP)


def _adam_math(w, g, m, v):
    m = ADAM_B1 * m + (1.0 - ADAM_B1) * g
    v = ADAM_B2 * v + (1.0 - ADAM_B2) * (g * g)
    delta = -ADAM_LR * ((m * _ADAM_C1) / (jnp.sqrt(v * _ADAM_C2) + ADAM_EPS) + ADAM_WD * w)
    return delta, m, v


def _adamw_layers(parts0, parts1, w, m, v, name):
    _, rows, cols = w.shape
    tr = min(rows, (1 << 16) // cols)
    nt = rows // tr

    def body(p0_ref, p1_ref, w_ref, m_ref, v_ref, g_ref, d_ref, nm_ref, nv_ref):
        layer = pl.program_id(0)

        def run(p_ref):
            g = p_ref[0].astype(F32)
            for s in range(1, N_DEV):
                g = g + p_ref[s].astype(F32)
            delta, nm, nv = _adam_math(w_ref[...], g, m_ref[...], v_ref[...])
            g_ref[...] = g
            d_ref[...] = delta
            nm_ref[...] = nm
            nv_ref[...] = nv

        @pl.when(layer == 0)
        def _():
            run(p0_ref)

        @pl.when(layer == 1)
        def _():
            run(p1_ref)

    wspec = pl.BlockSpec((None, tr, cols), lambda l, i: (l, i, 0))
    return _pcall(
        body, name=name, grid=(2, nt),
        in_specs=[pl.BlockSpec((N_DEV, tr, cols), lambda l, i: (0, jnp.where(l == 0, i, nt - 1), 0)),
                  pl.BlockSpec((N_DEV, tr, cols), lambda l, i: (0, jnp.where(l == 1, i, 0), 0)),
                  wspec, wspec, wspec],
        out_specs=[wspec] * 4, out_shape=[SDS(w.shape, F32)] * 4,
        compiler_params=_cparams("arbitrary", "arbitrary"),
    )(parts0, parts1, w, m, v)


def _adamw_packed(g, w, m, v, name):
    def body(g_ref, w_ref, m_ref, v_ref, d_ref, nm_ref, nv_ref):
        delta, nm, nv = _adam_math(w_ref[...], g_ref[...], m_ref[...], v_ref[...])
        d_ref[...] = delta
        nm_ref[...] = nm
        nv_ref[...] = nv

    vmem = pl.BlockSpec(memory_space=pltpu.VMEM)
    return _pcall(body, name=name, out_shape=[SDS(g.shape, F32)] * 3, in_specs=[vmem] * 4, out_specs=[vmem] * 3,
                  compiler_params=_cparams())(g, w, m, v)


def _ssm_discretize(a_re, a_im, log_dt, b_re, b_im):
    dt = jnp.exp(log_dt)[:, None]
    mag = jnp.exp(a_re * dt)
    ang = a_im * dt
    ab_re, ab_im = mag * jnp.cos(ang), mag * jnp.sin(ang)
    xr, xi = ab_re - 1.0, ab_im
    den = a_re * a_re + a_im * a_im
    q_re = (xr * a_re + xi * a_im) / den
    q_im = (xi * a_re - xr * a_im) / den
    bb_re = q_re[..., None] * b_re - q_im[..., None] * b_im
    bb_im = q_re[..., None] * b_im + q_im[..., None] * b_re
    return ab_re, ab_im, bb_re, bb_im


def _block_diag_b(bb):
    m = jnp.einsum("sgnc,gh->sgchn", bb.reshape(N_SB, 8, N_STATE, GROUP_CH), jnp.eye(8, dtype=F32))
    return m.reshape(N_SB, 128, SB_STATES)


def _block_diag_b_t(dm):
    return jnp.einsum("sgchn,gh->sgnc", dm.reshape(N_SB, 8, GROUP_CH, 8, N_STATE),
                      jnp.eye(8, dtype=F32)).reshape(N_GROUPS, N_STATE, GROUP_CH)


def _block_diag_c(cc):
    m = jnp.einsum("sgcn,gh->sgnhc", cc.reshape(N_SB, 8, GROUP_CH, N_STATE), jnp.eye(8, dtype=F32))
    return m.reshape(N_SB, SB_STATES, 128)


def _block_diag_c_t(dm):
    return jnp.einsum("sgnhc,gh->sgcn", dm.reshape(N_SB, 8, N_STATE, 8, GROUP_CH),
                      jnp.eye(8, dtype=F32)).reshape(N_GROUPS, GROUP_CH, N_STATE)


def _ssm_tables(ab_re, ab_im, bb_re, bb_im, c_re, c_im, d_skip):
    pr, pi = ab_re.reshape(1, -1), ab_im.reshape(1, -1)
    cr, ci = pr, pi
    squares = []
    for _ in range(3):
        squares.append((cr, ci))
        pr, pi = (jnp.concatenate([pr, pr * cr - pi * ci], axis=0),
                  jnp.concatenate([pi, pr * ci + pi * cr], axis=0))
        cr, ci = cr * cr - ci * ci, 2.0 * cr * ci
    r = jnp.arange(SCAN_TILE)[:, None]
    fwd = [(jnp.where(r >= (1 << k), squares[k][0], 0.0), jnp.where(r >= (1 << k), squares[k][1], 0.0))
           for k in range(3)] + [(pr, pi)]
    rev = [(jnp.where(r < SCAN_TILE - (1 << k), squares[k][0], 0.0),
            jnp.where(r < SCAN_TILE - (1 << k), -squares[k][1], 0.0)) for k in range(3)] + [(pr[::-1], -pi[::-1])]
    table = lambda part: jnp.stack([e[part] for e in fwd + rev]).reshape(
        8, SCAN_TILE, N_SB, SB_STATES).transpose(2, 0, 1, 3)
    return dict(
        b_mat=jnp.concatenate([_block_diag_b(bb_re), _block_diag_b(bb_im)], axis=-1).astype(MXU_DTYPE),
        c_mat=jnp.concatenate([_block_diag_c(c_re), -_block_diag_c(c_im)], axis=1).astype(MXU_DTYPE),
        t_re=table(0), t_im=table(1),
        d_skip=d_skip.reshape(1, D_SSM))


def _rope_tables(rows):
    pos = (jnp.arange(rows, dtype=jnp.int32) - PAD_ROWS).astype(F32)
    inv_freq = 1.0 / (ROPE_THETA ** (jnp.arange(0, HEAD_DIM, 2, dtype=F32) / HEAD_DIM))
    ang = pos[:, None] * inv_freq[None, :]
    ang = jnp.concatenate([ang, ang, ang, ang], axis=-1)
    first_half = (jnp.arange(128) % HEAD_DIM) < HEAD_DIM // 2
    sin = jnp.sin(ang)
    return jnp.cos(ang), jnp.where(first_half, -sin, 0.0), jnp.where(first_half, 0.0, sin)


def _pack(arrays):
    flat = jnp.concatenate([a.reshape(-1).astype(F32) for a in arrays])
    pad = (-flat.shape[0]) % 1024
    return jnp.pad(flat, (0, pad)).reshape(-1, 128)


def _unpack(packed, like):
    flat = packed.reshape(-1)
    out, off = [], 0
    for a in like:
        n = math.prod(a.shape)
        out.append(flat[off:off + n].reshape(a.shape))
        off += n
    return out


BIG = ("w_in", "w_glu", "w_o_ssm", "w_o_attn", "w_out", "w_up", "w_down")
WEIGHTS = ("meta_tokens", "norm_mix_pre", "norm_mix_post", "norm_mlp_pre", "norm_mlp_post", "w_in",
           "ssm_a_re", "ssm_a_im", "ssm_log_dt", "ssm_b_re", "ssm_b_im", "ssm_c_re", "ssm_c_im", "ssm_d",
           "w_glu", "b_glu", "attn_sinks", "w_o_ssm", "w_o_attn", "w_out", "w_up", "w_down")
SMALL = tuple(n for n in WEIGHTS if n not in BIG)


def kernel(x, meta_tokens, norm_mix_pre, norm_mix_post, norm_mlp_pre, norm_mlp_post, w_in, ssm_a_re, ssm_a_im, ssm_log_dt, ssm_b_re, ssm_b_im, ssm_c_re, ssm_c_im, ssm_d, w_glu, b_glu, attn_sinks, w_o_ssm, w_o_attn, w_out, w_up, w_down, loss_target, m_meta_tokens, m_norm_mix_pre, m_norm_mix_post, m_norm_mlp_pre, m_norm_mlp_post, m_w_in, m_ssm_a_re, m_ssm_a_im, m_ssm_log_dt, m_ssm_b_re, m_ssm_b_im, m_ssm_c_re, m_ssm_c_im, m_ssm_d, m_w_glu, m_b_glu, m_attn_sinks, m_w_o_ssm, m_w_o_attn, m_w_out, m_w_up, m_w_down, v_meta_tokens, v_norm_mix_pre, v_norm_mix_post, v_norm_mlp_pre, v_norm_mlp_post, v_w_in, v_ssm_a_re, v_ssm_a_im, v_ssm_log_dt, v_ssm_b_re, v_ssm_b_im, v_ssm_c_re, v_ssm_c_im, v_ssm_d, v_w_glu, v_b_glu, v_attn_sinks, v_w_o_ssm, v_w_o_attn, v_w_out, v_w_up, v_w_down):
    args = locals()
    w = {n: args[n] for n in WEIGHTS}
    m = {n: args["m_" + n] for n in WEIGHTS}
    v = {n: args["v_" + n] for n in WEIGHTS}
    n_layers = w_in.shape[0]
    seq = x.shape[1]
    rows = seq + BLK
    my_slot = _slot(_mesh_pos())

    gathered = _all_gather([w[n].astype(MXU_DTYPE) for n in BIG] + [meta_tokens], "gather_weights")
    w_in_g, w_glu_g, w_o_ssm_g, w_o_attn_g, w_out_g, w_up_g, w_down_g, meta_g = gathered
    meta_full = meta_g.transpose(1, 0, 2).reshape(N_META, D)
    w_glu_f = w_glu_g.transpose(1, 0, 2, 3).reshape(n_layers, D_SSM, D_SSM)
    w_o_ssm_f = w_o_ssm_g.transpose(1, 2, 0, 3).reshape(n_layers, D_SSM, D)
    w_o_attn_f = w_o_attn_g.transpose(1, 0, 2, 3).reshape(n_layers, D_ATTN, D)
    w_out_f = w_out_g.transpose(1, 0, 2, 3).reshape(n_layers, D, D)

    gains = {n: w[n].reshape(n_layers, 1, D) for n in ("norm_mix_pre", "norm_mix_post", "norm_mlp_pre", "norm_mlp_post")}
    b_glu3 = b_glu.reshape(n_layers, 1, D_SSM)
    cos, sin_a, sin_b = _rope_tables(rows)

    def ssm_setup(l):
        disc, disc_vjp = jax.vjp(_ssm_discretize, ssm_a_re[l], ssm_a_im[l], ssm_log_dt[l], ssm_b_re[l], ssm_b_im[l])
        return _ssm_tables(*disc, ssm_c_re[l], ssm_c_im[l], ssm_d[l]), disc_vjp

    hres = jnp.concatenate([jnp.zeros((PAD_ROWS, D), F32), meta_full, x[0]], axis=0)

    saved = []
    for l in range(n_layers):
        ssm, disc_vjp = ssm_setup(l)
        proj, h = _in_proj(hres, gains["norm_mix_pre"], w_in_g, l)
        q, k, vv = _rope_fwd(proj, cos, sin_a, sin_b, l)
        y, y_ssm, carry_in = _s5_fwd(proj, ssm, w_glu_f[l], b_glu3, l)
        y_attn = _attn_fwd(q, k, vv, attn_sinks, l)
        merged, mix, hres_mid = _merge_fwd(y_ssm, y_attn, proj, hres, w_o_ssm_f[l], w_o_attn_f[l], w_out_f[l],
                                                   gains["norm_mix_post"], l)
        up, act, h2, ff, hres_out = _mlp_fwd(hres_mid, gains["norm_mlp_pre"], gains["norm_mlp_post"], w_up_g, w_down_g, l)
        saved.append(dict(ssm=ssm, disc_vjp=disc_vjp, hres=hres, proj=proj, h=h, q=q, k=k, v=vv, y=y, y_ssm=y_ssm,
                          carry_in=carry_in, y_attn=y_attn, merged=merged, mix=mix, hres_mid=hres_mid,
                          up=up, act=act, h2=h2, ff=ff))
        hres = hres_out

    dhres, loss_vec = _loss_and_grad(hres, loss_target[0])
    loss = lax.psum(loss_vec[0, 0], MESH_AXES)

    small_grads = {n: [None] * n_layers for n in SMALL if n != "meta_tokens"}
    recv = [None] * n_layers
    for l in reversed(range(n_layers)):
        s = saved[l]
        dff, dup, dhm, dg_mlp_post, dg_mlp_pre = _mlp_bwd(dhres, s["ff"], s["up"], s["hres_mid"], gains["norm_mlp_pre"],
                                                          gains["norm_mlp_post"], w_up_g, w_down_g, l)
        dw_down = _matmul_tn(s["act"], dff, f"dw_down_l{l}").reshape(N_DEV, COL_SHARD, D)
        dw_up = _matmul_tn(s["h2"], dup, f"dw_up_l{l}", dev_major_cols=COL_SHARD)
        dmix, da1, da2, dgs, dga, dy_ssm, dy_attn, dg_mix_post = _merge_bwd(
            dhm, s["mix"], s["y_ssm"], s["y_attn"], s["proj"], w_o_ssm_f[l], w_o_attn_f[l], w_out_f[l], gains["norm_mix_post"], l)
        dw_out = _matmul_tn(s["merged"], dmix, f"dw_out_l{l}").reshape(N_DEV, D // N_DEV, D)
        dw_o_attn = _matmul_tn(s["y_attn"], da2, f"dw_o_attn_l{l}").reshape(N_DEV, D_ATTN // N_DEV, D)
        dw_o_ssm = _matmul_tn(s["y_ssm"], da1, f"dw_o_ssm_l{l}", dev_major_cols=D // N_DEV)
        dq, dk, dv, dk_meta, dv_meta, dsink = _attn_bwd(s["q"], s["k"], s["v"], dy_attn, attn_sinks, l)
        dqkv = _rope_bwd(dq, dk, dv, dk_meta, dv_meta, cos, sin_a, sin_b, l)
        du, dw_glu, db_glu, dd_skip, db_mat, dc_mat, dab = _s5_bwd(dy_ssm, s["y"], s["proj"], s["carry_in"], s["ssm"],
                                                                    w_glu_f[l], b_glu3, l)
        dproj = (du, dqkv, dgs, dga)
        dw_in = jnp.concatenate([_matmul_tn(s["h"], piece, f"dw_in{k}_l{l}", dev_major_cols=COL_SHARD)
                                 for k, piece in enumerate(dproj)], axis=0)
        dhres, dg_mix_pre = _in_bwd(dproj, dhm, s["hres"], gains["norm_mix_pre"], w_in_g, l)

        recv[l] = _exchange_slots([dw_in, dw_glu.astype(XFER_DTYPE).reshape(N_DEV, D_SSM // N_DEV, D_SSM), dw_o_ssm,
                                   dw_o_attn, dw_out, dw_up, dw_down], f"scatter_grads_l{l}")

        dab = dab.reshape(N_SB, 2, SB_STATES)
        da_re, da_im, dlog_dt, db_re, db_im = s["disc_vjp"]((
            dab[:, 0].reshape(N_GROUPS, N_STATE), dab[:, 1].reshape(N_GROUPS, N_STATE),
            _block_diag_b_t(db_mat[..., :SB_STATES]), _block_diag_b_t(db_mat[..., SB_STATES:])))
        for name, val in (("norm_mix_pre", dg_mix_pre[0]), ("norm_mix_post", dg_mix_post[0]),
                          ("norm_mlp_pre", dg_mlp_pre[0]), ("norm_mlp_post", dg_mlp_post[0]),
                          ("ssm_a_re", da_re), ("ssm_a_im", da_im), ("ssm_log_dt", dlog_dt),
                          ("ssm_b_re", db_re), ("ssm_b_im", db_im),
                          ("ssm_c_re", _block_diag_c_t(dc_mat[:, :SB_STATES])),
                          ("ssm_c_im", -_block_diag_c_t(dc_mat[:, SB_STATES:])),
                          ("ssm_d", dd_skip.reshape(N_GROUPS, GROUP_CH)), ("b_glu", db_glu[0]),
                          ("attn_sinks", dsink[:, 0])):
            small_grads[name][l] = val

    grad_x = dhres[BLK:][None]
    small_names = [n for n in SMALL if n != "meta_tokens"]
    partial_small = [dhres[PAD_ROWS:BLK]] + [jnp.stack(small_grads[n]) for n in small_names]
    summed = _unpack(_all_reduce_small(_pack(partial_small), "reduce_small_grads"), partial_small)
    grads = dict(zip(small_names, summed[1:]))
    grads["meta_tokens"] = lax.dynamic_slice_in_dim(summed[0], my_slot * (D // N_DEV), D // N_DEV, axis=1)

    delta, new_m, new_v = {}, {}, {}
    for idx, n in enumerate(BIG):
        grads[n], delta[n], new_m[n], new_v[n] = _adamw_layers(recv[0][idx], recv[1][idx], w[n], m[n], v[n], f"adamw_{n}")
    like = [w[n] for n in SMALL]
    d_s, m_s, v_s = _adamw_packed(_pack([grads[n] for n in SMALL]), _pack(like), _pack([m[n] for n in SMALL]),
                                  _pack([v[n] for n in SMALL]), "adamw_small")
    for n, dd, mm, vs in zip(SMALL, _unpack(d_s, like), _unpack(m_s, like), _unpack(v_s, like)):
        delta[n], new_m[n], new_v[n] = dd, mm, vs

    return (loss, grad_x, *[grads[n] for n in WEIGHTS], *[delta[n] for n in WEIGHTS],
            *[new_m[n] for n in WEIGHTS], *[new_v[n] for n in WEIGHTS])
```

```python
import functools
import math

import jax
import jax.numpy as jnp
from jax import lax
from jax.experimental import pallas as pl
from jax.experimental.pallas import tpu as pltpu

F32 = jnp.float32
MXU_DTYPE = jnp.bfloat16
XFER_DTYPE = MXU_DTYPE
_pcall = pl.pallas_call
SDS = jax.ShapeDtypeStruct

D = 1024
D_SSM = 512
D_ATTN = 1024
D_KV = 256
D_FF = 4096
D_IN = 4096
HEAD_DIM = 64
N_Q_HEADS = 16
N_KV_HEADS = 4
Q_PER_KV = 4
N_META = 16
BLK = 128
PAD_ROWS = BLK - N_META
N_GROUPS = 32
N_STATE = 64
GROUP_CH = 16
N_SB = 4
SB_STATES = 512
ROPE_THETA = 10000.0
ATTN_SCALE = HEAD_DIM ** -0.5
NEG_INF = -1e30
RMS_EPS = 1e-6
N_DEV = 8
COL_SHARD = 512

ADAM_LR = 0.001
ADAM_B1 = 0.9
ADAM_B2 = 0.999
ADAM_EPS = 1e-08
ADAM_WD = 0.01
ADAM_STEP = 10

VMEM_LIMIT = 56 * 1024 * 1024
MESH_AXES = ("x", "y", "c")

_NT = (((1,), (1,)), ((), ()))
_TN = (((0,), (0,)), ((), ()))


def _cparams(*sem):
    return pltpu.CompilerParams(dimension_semantics=tuple(sem) if sem else None,
                                vmem_limit_bytes=VMEM_LIMIT)


def _row_tile(rows, cap=640):
    for t in (640, 512, 320, 256, 128):
        if t <= cap and rows % t == 0:
            return t
    raise ValueError(f"unsupported row count {rows}")


def _dot(a, b):
    return jnp.dot(a, b, preferred_element_type=F32)


def _dot_nt(a, b):
    return lax.dot_general(a, b, _NT, preferred_element_type=F32)


def _dot_tn(a, b):
    return lax.dot_general(a, b, _TN, preferred_element_type=F32)


def _sigmoid(x):
    return 1.0 / (1.0 + jnp.exp(-x))


_GELU_C = math.sqrt(2.0 / math.pi)


def _gelu_parts(y):
    t = jnp.tanh(_GELU_C * (y + 0.044715 * (y * y * y)))
    return 0.5 * y * (1.0 + t), t


def _gelu_grad(y, t):
    return 0.5 * (1.0 + t) + 0.5 * y * (1.0 - t * t) * (_GELU_C * (1.0 + 0.134145 * (y * y)))


def _rms_fwd(x, gain):
    r = lax.rsqrt(jnp.mean(x * x, axis=-1, keepdims=True) + RMS_EPS)
    return (x * r) * gain


def _rms_bwd(x, gain, dout):
    r = lax.rsqrt(jnp.mean(x * x, axis=-1, keepdims=True) + RMS_EPS)
    xh = x * r
    dxh = dout * gain
    dx = r * (dxh - xh * jnp.mean(dxh * xh, axis=-1, keepdims=True))
    return dx, jnp.sum(dout * xh, axis=0, keepdims=True)


def _mesh_pos():
    return lax.axis_index("x"), lax.axis_index("y"), lax.axis_index("c")


def _peer(pos, d):
    x, y, c = pos
    return (1 - x if d & 4 else x, 1 - y if d & 2 else y, 1 - c if d & 1 else c)


def _slot(pos):
    return 4 * pos[0] + 2 * pos[1] + pos[2]


def _all_gather(shards, name):
    n = len(shards)

    def body(*refs):
        ins, outs = refs[:n], refs[n:2 * n]
        send_sems, recv_sems, local_sems = refs[2 * n:]
        me = _mesh_pos()
        my_slot = _slot(me)
        local = [pltpu.make_async_copy(ins[k], outs[k].at[my_slot], local_sems.at[k]) for k in range(n)]
        for cp in local:
            cp.start()
        sends = []
        for k in range(n):
            for d in range(1, N_DEV):
                sends.append(pltpu.make_async_remote_copy(
                    src_ref=ins[k], dst_ref=outs[k].at[my_slot],
                    send_sem=send_sems.at[k, d - 1], recv_sem=recv_sems.at[k, d - 1],
                    device_id=_peer(me, d), device_id_type=pl.DeviceIdType.MESH))
        for cp in sends:
            cp.start()
        for k in range(n):
            for d in range(1, N_DEV):
                pltpu.make_async_remote_copy(
                    src_ref=ins[k], dst_ref=outs[k].at[_slot(_peer(me, d))],
                    send_sem=send_sems.at[k, d - 1], recv_sem=recv_sems.at[k, d - 1],
                    device_id=_peer(me, d), device_id_type=pl.DeviceIdType.MESH).wait_recv()
        for cp in sends:
            cp.wait_send()
        for cp in local:
            cp.wait()

    any_spec = pl.BlockSpec(memory_space=pl.ANY)
    return _pcall(
        body, name=name,
        out_shape=[SDS((N_DEV,) + s.shape, s.dtype) for s in shards],
        in_specs=[any_spec] * n, out_specs=[any_spec] * n,
        scratch_shapes=[pltpu.SemaphoreType.DMA((n, N_DEV - 1)), pltpu.SemaphoreType.DMA((n, N_DEV - 1)),
                        pltpu.SemaphoreType.DMA((n,))],
    )(*shards)


def _exchange_slots(parts, name):
    n = len(parts)

    def body(*refs):
        ins, outs = refs[:n], refs[n:2 * n]
        send_sems, recv_sems, local_sems = refs[2 * n:]
        me = _mesh_pos()
        my_slot = _slot(me)
        local = [pltpu.make_async_copy(ins[k].at[my_slot], outs[k].at[my_slot], local_sems.at[k]) for k in range(n)]
        for cp in local:
            cp.start()
        sends = []
        for k in range(n):
            for d in range(1, N_DEV):
                sends.append(pltpu.make_async_remote_copy(
                    src_ref=ins[k].at[_slot(_peer(me, d))], dst_ref=outs[k].at[my_slot],
                    send_sem=send_sems.at[k, d - 1], recv_sem=recv_sems.at[k, d - 1],
                    device_id=_peer(me, d), device_id_type=pl.DeviceIdType.MESH))
        for cp in sends:
            cp.start()
        for k in range(n):
            for d in range(1, N_DEV):
                pltpu.make_async_remote_copy(
                    src_ref=ins[k].at[my_slot], dst_ref=outs[k].at[_slot(_peer(me, d))],
                    send_sem=send_sems.at[k, d - 1], recv_sem=recv_sems.at[k, d - 1],
                    device_id=_peer(me, d), device_id_type=pl.DeviceIdType.MESH).wait_recv()
        for cp in sends:
            cp.wait_send()
        for cp in local:
            cp.wait()

    any_spec = pl.BlockSpec(memory_space=pl.ANY)
    return _pcall(
        body, name=name,
        out_shape=[SDS(p.shape, p.dtype) for p in parts],
        in_specs=[any_spec] * n, out_specs=[any_spec] * n,
        scratch_shapes=[pltpu.SemaphoreType.DMA((n, N_DEV - 1)), pltpu.SemaphoreType.DMA((n, N_DEV - 1)),
                        pltpu.SemaphoreType.DMA((n,))],
    )(*parts)


def _all_reduce_small(packed, name):
    rows = packed.shape[0]

    def body(x_ref, out_ref, gath, send_sems, recv_sems):
        me = _mesh_pos()
        my_slot = _slot(me)
        gath[my_slot] = x_ref[...]
        sends = [pltpu.make_async_remote_copy(
            src_ref=x_ref, dst_ref=gath.at[my_slot],
            send_sem=send_sems.at[d - 1], recv_sem=recv_sems.at[d - 1],
            device_id=_peer(me, d), device_id_type=pl.DeviceIdType.MESH) for d in range(1, N_DEV)]
        for cp in sends:
            cp.start()
        for d in range(1, N_DEV):
            pltpu.make_async_remote_copy(
                src_ref=x_ref, dst_ref=gath.at[_slot(_peer(me, d))],
                send_sem=send_sems.at[d - 1], recv_sem=recv_sems.at[d - 1],
                device_id=_peer(me, d), device_id_type=pl.DeviceIdType.MESH).wait_recv()
        for cp in sends:
            cp.wait_send()
        acc = gath[0]
        for s in range(1, N_DEV):
            acc = acc + gath[s]
        out_ref[...] = acc

    vmem = pl.BlockSpec(memory_space=pltpu.VMEM)
    return _pcall(
        body, name=name, out_shape=SDS(packed.shape, F32), in_specs=[vmem], out_specs=vmem,
        scratch_shapes=[pltpu.VMEM((N_DEV, rows, 128), F32),
                        pltpu.SemaphoreType.DMA((N_DEV - 1,)), pltpu.SemaphoreType.DMA((N_DEV - 1,))],
        compiler_params=_cparams(),
    )(packed)


def _load_resident(w_hbm, w_scr, sems, layer, first_step):
    @pl.when(first_step)
    def _():
        copies = [pltpu.make_async_copy(w_hbm.at[s, layer], w_scr.at[s], sems.at[s]) for s in range(N_DEV)]
        for cp in copies:
            cp.start()
        for cp in copies:
            cp.wait()


def _in_proj(hres, gain3, w_in_g, layer):
    rows = hres.shape[0]
    tm = _row_tile(rows)

    def body(x_ref, g_ref, w_hbm, proj_ref, h_ref, h_scr, w_scr, w_sem):
        j = pl.program_id(1)
        _load_resident(w_hbm, w_scr, w_sem, layer, (pl.program_id(0) == 0) & (j == 0))

        @pl.when(j == 0)
        def _():
            hn = _rms_fwd(x_ref[...], g_ref[...]).astype(MXU_DTYPE)
            h_scr[...] = hn
            h_ref[...] = hn

        proj_ref[...] = _dot(h_scr[...], w_scr[j])

    return _pcall(
        body, name=f"in_proj_l{layer}", grid=(rows // tm, N_DEV),
        in_specs=[pl.BlockSpec((tm, D), lambda i, j: (i, 0)),
                  pl.BlockSpec((None, 1, D), lambda i, j: (layer, 0, 0)),
                  pl.BlockSpec(memory_space=pl.ANY)],
        out_specs=[pl.BlockSpec((tm, COL_SHARD), lambda i, j: (i, j)),
                   pl.BlockSpec((tm, D), lambda i, j: (i, 0))],
        out_shape=[SDS((rows, D_IN), F32), SDS((rows, D), MXU_DTYPE)],
        scratch_shapes=[pltpu.VMEM((tm, D), MXU_DTYPE), pltpu.VMEM((N_DEV, D, COL_SHARD), MXU_DTYPE),
                        pltpu.SemaphoreType.DMA((N_DEV,))],
        compiler_params=_cparams("arbitrary", "arbitrary"),
    )(hres, gain3, w_in_g)


def _rope_lanes(t, cos, sin_a, sin_b):
    return t * cos + pltpu.roll(t, 96, 1) * sin_a + pltpu.roll(t, 32, 1) * sin_b


def _rope_fwd(proj, cos, sin_a, sin_b, layer):
    rows = proj.shape[0]
    tm = _row_tile(rows)

    def body(q0_ref, q1_ref, kv_ref, c_ref, a_ref, b_ref, qo_ref, ko_ref, vo_ref):
        c, a, b = c_ref[...], a_ref[...], b_ref[...]
        for half, q_ref in enumerate((q0_ref, q1_ref)):
            for t in range(4):
                x = q_ref[:, t * 128:(t + 1) * 128]
                lo = half * 512 + t * 128
                qo_ref[:, lo:lo + 128] = (_rope_lanes(x, c, a, b) * ATTN_SCALE).astype(MXU_DTYPE)
        for t in range(2):
            x = kv_ref[:, t * 128:(t + 1) * 128]
            ko_ref[:, t * 128:(t + 1) * 128] = _rope_lanes(x, c, a, b).astype(MXU_DTYPE)
        vo_ref[...] = kv_ref[:, D_KV:2 * D_KV].astype(MXU_DTYPE)

    tab = pl.BlockSpec((tm, 128), lambda i: (i, 0))
    return _pcall(
        body, name=f"rope_fwd_l{layer}", grid=(rows // tm,),
        in_specs=[pl.BlockSpec((tm, 512), lambda i: (i, 1)), pl.BlockSpec((tm, 512), lambda i: (i, 2)),
                  pl.BlockSpec((tm, 512), lambda i: (i, 3)), tab, tab, tab],
        out_specs=[pl.BlockSpec((tm, D_ATTN), lambda i: (i, 0)), pl.BlockSpec((tm, D_KV), lambda i: (i, 0)),
                   pl.BlockSpec((tm, D_KV), lambda i: (i, 0))],
        out_shape=[SDS((rows, D_ATTN), MXU_DTYPE), SDS((rows, D_KV), MXU_DTYPE), SDS((rows, D_KV), MXU_DTYPE)],
        compiler_params=_cparams("parallel"),
    )(proj, proj, proj, cos, sin_a, sin_b)


SCAN_TILE = 8


def _scan_tiles(x_ref, out_ref, tre_ref, tim_ref, sb, t_r, t_i, reverse, prev_ref=None):
    base = 4 if reverse else 0
    n_tiles = BLK // SCAN_TILE
    row = lax.broadcasted_iota(jnp.int32, (SCAN_TILE, SB_STATES), 0)
    for j in (range(n_tiles - 1, -1, -1) if reverse else range(n_tiles)):
        rows = slice(SCAN_TILE * j, SCAN_TILE * (j + 1))
        xr = x_ref[rows, :SB_STATES]
        xi = x_ref[rows, SB_STATES:]
        for k in range(3):
            shift = SCAN_TILE - (1 << k) if reverse else (1 << k)
            rr = pltpu.roll(xr, shift, 0)
            ri = pltpu.roll(xi, shift, 0)
            ar = tre_ref[sb, base + k]
            ai = tim_ref[sb, base + k]
            xr, xi = xr + (ar * rr - ai * ri), xi + (ar * ri + ai * rr)
        pr = tre_ref[sb, base + 3]
        pi = tim_ref[sb, base + 3]
        xr, xi = xr + (pr * t_r - pi * t_i), xi + (pr * t_i + pi * t_r)
        out_ref[rows, :SB_STATES] = xr
        out_ref[rows, SB_STATES:] = xi
        if prev_ref is not None:
            prev_ref[rows, :SB_STATES] = jnp.where(row == 0, t_r, pltpu.roll(xr, 1, 0))
            prev_ref[rows, SB_STATES:] = jnp.where(row == 0, t_i, pltpu.roll(xi, 1, 0))
        edge = slice(0, 1) if reverse else slice(SCAN_TILE - 1, SCAN_TILE)
        t_r, t_i = xr[edge], xi[edge]
    return t_r, t_i


def _s5_fwd(proj, ssm, w_glu, b_glu3, layer):
    rows = proj.shape[0]
    n_chunks = rows // BLK
    b_mat, c_mat, t_re, t_im, d_skip = (ssm[k] for k in ("b_mat", "c_mat", "t_re", "t_im", "d_skip"))

    def body(u_ref, bm_ref, cm_ref, tre_ref, tim_ref, d_ref, wg_ref, bg_ref,
             y_ref, ys_ref, cin_ref, carry, bu_scr, s_scr):
        @pl.when(pl.program_id(0) == 0)
        def _():
            carry[...] = jnp.zeros_like(carry)

        cin_ref[...] = carry[...]
        u = u_ref[...]
        for sb in range(N_SB):
            cols = slice(sb * 128, (sb + 1) * 128)
            u_sb = u[:, cols]
            bu_scr[sb] = _dot(u_sb.astype(MXU_DTYPE), bm_ref[sb])
            t_r, t_i = _scan_tiles(bu_scr.at[sb], s_scr.at[sb], tre_ref, tim_ref, sb,
                                   carry[2 * sb:2 * sb + 1, :], carry[2 * sb + 1:2 * sb + 2, :], False)
            carry[2 * sb:2 * sb + 1, :] = t_r
            carry[2 * sb + 1:2 * sb + 2, :] = t_i
            y_ref[:, cols] = _dot(s_scr[sb].astype(MXU_DTYPE), cm_ref[sb]) + d_ref[:, cols] * u_sb
        z, _ = _gelu_parts(y_ref[...])
        gl = _dot(z.astype(MXU_DTYPE), wg_ref[...]) + bg_ref[...]
        ys_ref[...] = (z * _sigmoid(gl)).astype(MXU_DTYPE)

    full = lambda shape: pl.BlockSpec(shape, lambda j: (0,) * len(shape))
    return _pcall(
        body, name=f"s5_fwd_l{layer}", grid=(n_chunks,),
        in_specs=[pl.BlockSpec((BLK, D_SSM), lambda j: (j, 0)),
                  full((N_SB, 128, 2 * SB_STATES)), full((N_SB, 2 * SB_STATES, 128)),
                  full((N_SB, 8, SCAN_TILE, SB_STATES)), full((N_SB, 8, SCAN_TILE, SB_STATES)),
                  full((1, D_SSM)), full((D_SSM, D_SSM)),
                  pl.BlockSpec((None, 1, D_SSM), lambda j: (layer, 0, 0))],
        out_specs=[pl.BlockSpec((BLK, D_SSM), lambda j: (j, 0)), pl.BlockSpec((BLK, D_SSM), lambda j: (j, 0)),
                   pl.BlockSpec((None, 8, SB_STATES), lambda j: (j, 0, 0))],
        out_shape=[SDS((rows, D_SSM), F32), SDS((rows, D_SSM), MXU_DTYPE), SDS((n_chunks, 8, SB_STATES), F32)],
        scratch_shapes=[pltpu.VMEM((8, SB_STATES), F32), pltpu.VMEM((N_SB, BLK, 2 * SB_STATES), F32),
                        pltpu.VMEM((N_SB, BLK, 2 * SB_STATES), F32)],
        compiler_params=_cparams("arbitrary"),
    )(proj, b_mat, c_mat, t_re, t_im, d_skip, w_glu, b_glu3)


def _attn_mask(i):
    row = lax.broadcasted_iota(jnp.int32, (BLK, 3 * BLK), 0) + i * BLK
    col = lax.broadcasted_iota(jnp.int32, (BLK, 3 * BLK), 1)
    seg = jnp.right_shift(col, 7)
    c = jnp.bitwise_and(col, BLK - 1)
    kidx = c + (i + seg - 2) * BLK
    ok_meta = (seg == 0) & (c >= PAD_ROWS) & (row - c >= BLK)
    ok_win = (seg > 0) & (kidx >= PAD_ROWS) & (kidx <= row) & (row - kidx < BLK)
    return jnp.where(ok_meta | ok_win, 0.0, NEG_INF)


def _head_lanes(h):
    return slice(h * HEAD_DIM, (h + 1) * HEAD_DIM)


def _group_rows(ref, kvh):
    return jnp.concatenate([ref[:, _head_lanes(kvh * Q_PER_KV + g)] for g in range(Q_PER_KV)], axis=0)


def _group_bias(bias, sink_ref, layer, kvh):
    first_col = lax.broadcasted_iota(jnp.int32, (BLK, BLK), 1) == 0
    slabs = []
    for g in range(Q_PER_KV):
        first = jnp.where(first_col, sink_ref[layer, kvh * Q_PER_KV + g], bias[:, :BLK])
        slabs.append(jnp.concatenate([first, bias[:, BLK:]], axis=1))
    return jnp.concatenate(slabs, axis=0)


def _attn_probs(q4, k3, bias4):
    s = _dot_nt(q4, k3) + bias4
    e = jnp.exp(s - jnp.max(s, axis=-1, keepdims=True))
    return e * (1.0 / jnp.sum(e, axis=-1, keepdims=True))


def _attn_fwd(q, k, v, sinks, layer):
    rows = q.shape[0]
    n_blk = rows // BLK

    def body(sink_ref, q_ref, km_ref, kp_ref, kc_ref, vm_ref, vp_ref, vc_ref, o_ref):
        bias = _attn_mask(pl.program_id(0))
        for kvh in range(N_KV_HEADS):
            lanes = _head_lanes(kvh)
            k3 = jnp.concatenate([km_ref[:, lanes], kp_ref[:, lanes], kc_ref[:, lanes]], axis=0)
            v3 = jnp.concatenate([vm_ref[:, lanes], vp_ref[:, lanes], vc_ref[:, lanes]], axis=0)
            p = _attn_probs(_group_rows(q_ref, kvh), k3, _group_bias(bias, sink_ref, layer, kvh))
            o4 = _dot(p.astype(MXU_DTYPE), v3).astype(MXU_DTYPE)
            for g in range(Q_PER_KV):
                o_ref[:, _head_lanes(kvh * Q_PER_KV + g)] = o4[g * BLK:(g + 1) * BLK]

    kv_meta = pl.BlockSpec((BLK, D_KV), lambda i: (0, 0))
    kv_prev = pl.BlockSpec((BLK, D_KV), lambda i: (jnp.maximum(i - 1, 0), 0))
    kv_cur = pl.BlockSpec((BLK, D_KV), lambda i: (i, 0))
    return _pcall(
        body, name=f"attn_fwd_l{layer}", grid=(n_blk,),
        in_specs=[pl.BlockSpec(memory_space=pltpu.SMEM),
                  pl.BlockSpec((BLK, D_ATTN), lambda i: (i, 0)),
                  kv_meta, kv_prev, kv_cur, kv_meta, kv_prev, kv_cur],
        out_specs=pl.BlockSpec((BLK, D_ATTN), lambda i: (i, 0)),
        out_shape=SDS((rows, D_ATTN), MXU_DTYPE),
        compiler_params=_cparams("parallel"),
    )(sinks, q, k, k, k, v, v, v)


def _merge_fwd(y_ssm, y_attn, proj, hres, w_o_ssm, w_o_attn, w_out, gain3, layer):
    rows = hres.shape[0]
    tm = _row_tile(rows, 320)

    def body(ys_ref, ya_ref, gs_ref, ga_ref, x_ref, wos_ref, woa_ref, wout_ref, g_ref,
             mg_ref, mix_ref, out_ref):
        a1 = _dot(ys_ref[...], wos_ref[...])
        a2 = _dot(ya_ref[...], woa_ref[...])
        merged = (_sigmoid(gs_ref[...]) * a1 + _sigmoid(ga_ref[...]) * a2).astype(MXU_DTYPE)
        mg_ref[...] = merged
        mix = _dot(merged, wout_ref[...])
        mix_ref[...] = mix
        out_ref[...] = x_ref[...] + _rms_fwd(mix, g_ref[...])

    row_d = pl.BlockSpec((tm, D), lambda i: (i, 0))
    full = lambda shape: pl.BlockSpec(shape, lambda i: (0,) * len(shape))
    return _pcall(
        body, name=f"merge_fwd_l{layer}", grid=(rows // tm,),
        in_specs=[pl.BlockSpec((tm, D_SSM), lambda i: (i, 0)), row_d,
                  pl.BlockSpec((tm, D), lambda i: (i, 2)), pl.BlockSpec((tm, D), lambda i: (i, 3)), row_d,
                  full((D_SSM, D)), full((D_ATTN, D)), full((D, D)),
                  pl.BlockSpec((None, 1, D), lambda i: (layer, 0, 0))],
        out_specs=[row_d, row_d, row_d],
        out_shape=[SDS((rows, D), MXU_DTYPE), SDS((rows, D), F32), SDS((rows, D), F32)],
        compiler_params=_cparams("parallel"),
    )(y_ssm, y_attn, proj, proj, hres, w_o_ssm, w_o_attn, w_out, gain3)


def _mlp_fwd(hres, gain_pre3, gain_post3, w_up_g, w_down_g, layer):
    rows = hres.shape[0]
    tm = _row_tile(rows)

    def body(x_ref, gp_ref, gq_ref, wu_hbm, wd_hbm, up_ref, h_ref, ff_ref, out_ref,
             h_scr, acc, wu_scr, wd_scr, wu_sem, wd_sem):
        kf = pl.program_id(1)
        first = (pl.program_id(0) == 0) & (kf == 0)
        _load_resident(wu_hbm, wu_scr, wu_sem, layer, first)
        _load_resident(wd_hbm, wd_scr, wd_sem, layer, first)

        @pl.when(kf == 0)
        def _():
            hn = _rms_fwd(x_ref[...], gp_ref[...]).astype(MXU_DTYPE)
            h_scr[...] = hn
            h_ref[...] = hn
            acc[...] = jnp.zeros_like(acc)

        up = _dot(h_scr[...], wu_scr[kf])
        up_ref[...] = up.astype(MXU_DTYPE)
        r = jnp.maximum(up, 0.0)
        acc[...] += _dot((r * r).astype(MXU_DTYPE), wd_scr[kf])

        @pl.when(kf == N_DEV - 1)
        def _():
            ff = acc[...]
            ff_ref[...] = ff
            out_ref[...] = x_ref[...] + _rms_fwd(ff, gq_ref[...])

    row_d = pl.BlockSpec((tm, D), lambda i, k: (i, 0))
    gain = pl.BlockSpec((None, 1, D), lambda i, k: (layer, 0, 0))
    return _pcall(
        body, name=f"mlp_fwd_l{layer}", grid=(rows // tm, N_DEV),
        in_specs=[row_d, gain, gain, pl.BlockSpec(memory_space=pl.ANY), pl.BlockSpec(memory_space=pl.ANY)],
        out_specs=[pl.BlockSpec((tm, COL_SHARD), lambda i, k: (i, k)), row_d, row_d, row_d],
        out_shape=[SDS((rows, D_FF), MXU_DTYPE), SDS((rows, D), MXU_DTYPE), SDS((rows, D), F32), SDS((rows, D), F32)],
        scratch_shapes=[pltpu.VMEM((tm, D), MXU_DTYPE), pltpu.VMEM((tm, D), F32),
                        pltpu.VMEM((N_DEV, D, COL_SHARD), MXU_DTYPE), pltpu.VMEM((N_DEV, COL_SHARD, D), MXU_DTYPE),
                        pltpu.SemaphoreType.DMA((N_DEV,)), pltpu.SemaphoreType.DMA((N_DEV,))],
        compiler_params=_cparams("arbitrary", "arbitrary"),
    )(hres, gain_pre3, gain_post3, w_up_g, w_down_g)


def _loss_and_grad(hres, target):
    rows = hres.shape[0]
    n_blk = rows // BLK

    def body(y_ref, t_ref, dy_ref, loss_ref):
        i = pl.program_id(0)

        @pl.when(i == 0)
        def _():
            dy_ref[...] = jnp.zeros_like(dy_ref)
            loss_ref[...] = jnp.zeros_like(loss_ref)

        @pl.when(i > 0)
        def _():
            err = y_ref[...] - t_ref[...]
            dy_ref[...] = err * (1.0 / D)
            loss_ref[...] += jnp.sum(err * err) * (0.5 / D)

    return _pcall(
        body, name="loss", grid=(n_blk,),
        in_specs=[pl.BlockSpec((BLK, D), lambda i: (i, 0)),
                  pl.BlockSpec((BLK, D), lambda i: (jnp.maximum(i - 1, 0), 0))],
        out_specs=[pl.BlockSpec((BLK, D), lambda i: (i, 0)), pl.BlockSpec((1, 128), lambda i: (0, 0))],
        out_shape=[SDS((rows, D), F32), SDS((1, 128), F32)],
        compiler_params=_cparams("arbitrary"),
    )(hres, target)


def _relu_squared(up):
    r = jnp.maximum(up.astype(F32), 0.0)
    return (r * r).astype(MXU_DTYPE)


def _matmul_tn(a, b, name, dev_major_cols=None, a_fn=None):
    rows, ka = a.shape
    n = b.shape[1]
    ta = min(ka, 1024)
    tn = 1024 if n % 1024 == 0 else 512
    tr = _row_tile(rows)
    n_r = rows // tr

    def body(a_ref, b_ref, o_ref, acc):
        r = pl.program_id(2)

        @pl.when(r == 0)
        def _():
            acc[...] = jnp.zeros_like(acc)

        a_blk = a_ref[...] if a_fn is None else a_fn(a_ref[...])
        acc[...] += _dot_tn(a_blk, b_ref[...])

        @pl.when(r == n_r - 1)
        def _():
            if dev_major_cols is None:
                o_ref[...] = acc[...].astype(XFER_DTYPE)
            else:
                for s in range(tn // dev_major_cols):
                    o_ref[s] = acc[:, s * dev_major_cols:(s + 1) * dev_major_cols].astype(XFER_DTYPE)

    if dev_major_cols is None:
        out_spec = pl.BlockSpec((ta, tn), lambda i, j, r: (i, j))
        out_shape = SDS((ka, n), XFER_DTYPE)
    else:
        w = dev_major_cols
        out_spec = pl.BlockSpec((tn // w, ta, w), lambda i, j, r: (j, i, 0))
        out_shape = SDS((n // w, ka, w), XFER_DTYPE)
    return _pcall(
        body, name=name, grid=(ka // ta, n // tn, n_r),
        in_specs=[pl.BlockSpec((tr, ta), lambda i, j, r: (r, i)), pl.BlockSpec((tr, tn), lambda i, j, r: (r, j))],
        out_specs=out_spec, out_shape=out_shape,
        scratch_shapes=[pltpu.VMEM((ta, tn), F32)],
        compiler_params=_cparams("parallel", "parallel", "arbitrary"),
    )(a, b)


def _mlp_bwd(dout, ff, up, hres_mid, gain_pre3, gain_post3, w_up_g, w_down_g, layer):
    rows = dout.shape[0]
    tm = _row_tile(rows)

    def body(do_ref, ff_ref, up_ref, x_ref, gp_ref, gq_ref, wu_hbm, wd_hbm,
             dff_ref, dup_ref, dx_ref, dgq_ref, dgp_ref, dff_scr, acc, wu_scr, wd_scr, wu_sem, wd_sem):
        i = pl.program_id(0)
        kf = pl.program_id(1)
        _load_resident(wu_hbm, wu_scr, wu_sem, layer, (i == 0) & (kf == 0))
        _load_resident(wd_hbm, wd_scr, wd_sem, layer, (i == 0) & (kf == 0))

        @pl.when((i == 0) & (kf == 0))
        def _():
            dgq_ref[...] = jnp.zeros_like(dgq_ref)
            dgp_ref[...] = jnp.zeros_like(dgp_ref)

        @pl.when(kf == 0)
        def _():
            dff, dg = _rms_bwd(ff_ref[...], gq_ref[...], do_ref[...])
            dgq_ref[...] += dg
            dffb = dff.astype(MXU_DTYPE)
            dff_scr[...] = dffb
            dff_ref[...] = dffb
            acc[...] = jnp.zeros_like(acc)

        dact = _dot_nt(dff_scr[...], wd_scr[kf])
        dup = (dact * (2.0 * jnp.maximum(up_ref[...].astype(F32), 0.0))).astype(MXU_DTYPE)
        dup_ref[...] = dup
        acc[...] += _dot_nt(dup, wu_scr[kf])

        @pl.when(kf == N_DEV - 1)
        def _():
            dx, dg = _rms_bwd(x_ref[...], gp_ref[...], acc[...])
            dgp_ref[...] += dg
            dx_ref[...] = do_ref[...] + dx

    row_d = pl.BlockSpec((tm, D), lambda i, k: (i, 0))
    gain = pl.BlockSpec((None, 1, D), lambda i, k: (layer, 0, 0))
    dgain = pl.BlockSpec((1, D), lambda i, k: (0, 0))
    return _pcall(
        body, name=f"mlp_bwd_l{layer}", grid=(rows // tm, N_DEV),
        in_specs=[row_d, row_d, pl.BlockSpec((tm, COL_SHARD), lambda i, k: (i, k)), row_d, gain, gain,
                  pl.BlockSpec(memory_space=pl.ANY), pl.BlockSpec(memory_space=pl.ANY)],
        out_specs=[row_d, pl.BlockSpec((tm, COL_SHARD), lambda i, k: (i, k)), row_d, dgain, dgain],
        out_shape=[SDS((rows, D), MXU_DTYPE), SDS((rows, D_FF), MXU_DTYPE), SDS((rows, D), F32),
                   SDS((1, D), F32), SDS((1, D), F32)],
        scratch_shapes=[pltpu.VMEM((tm, D), MXU_DTYPE), pltpu.VMEM((tm, D), F32),
                        pltpu.VMEM((N_DEV, D, COL_SHARD), MXU_DTYPE), pltpu.VMEM((N_DEV, COL_SHARD, D), MXU_DTYPE),
                        pltpu.SemaphoreType.DMA((N_DEV,)), pltpu.SemaphoreType.DMA((N_DEV,))],
        compiler_params=_cparams("arbitrary", "arbitrary"),
    )(dout, ff, up, hres_mid, gain_pre3, gain_post3, w_up_g, w_down_g)


def _merge_bwd(dhm, mix, y_ssm, y_attn, proj, w_o_ssm, w_o_attn, w_out, gain3, layer):
    rows = dhm.shape[0]
    tm = _row_tile(rows, 320)

    def body(dh_ref, mix_ref, ys_ref, ya_ref, gs_ref, ga_ref, wos_ref, woa_ref, wout_ref, g_ref,
             dmix_ref, da1_ref, da2_ref, dgs_ref, dga_ref, dys_ref, dya_ref, dg_ref):
        @pl.when(pl.program_id(0) == 0)
        def _():
            dg_ref[...] = jnp.zeros_like(dg_ref)

        dmix, dg = _rms_bwd(mix_ref[...], g_ref[...], dh_ref[...])
        dg_ref[...] += dg
        dmixb = dmix.astype(MXU_DTYPE)
        dmix_ref[...] = dmixb
        dmerged = _dot_nt(dmixb, wout_ref[...])
        sg_s = _sigmoid(gs_ref[...])
        sg_a = _sigmoid(ga_ref[...])
        da1 = (dmerged * sg_s).astype(MXU_DTYPE)
        da2 = (dmerged * sg_a).astype(MXU_DTYPE)
        da1_ref[...] = da1
        da2_ref[...] = da2
        a1 = _dot(ys_ref[...], wos_ref[...])
        a2 = _dot(ya_ref[...], woa_ref[...])
        dgs_ref[...] = (dmerged * a1 * (sg_s * (1.0 - sg_s))).astype(MXU_DTYPE)
        dga_ref[...] = (dmerged * a2 * (sg_a * (1.0 - sg_a))).astype(MXU_DTYPE)
        dys_ref[...] = _dot_nt(da1, wos_ref[...])
        dya_ref[...] = _dot_nt(da2, woa_ref[...])

    row_d = pl.BlockSpec((tm, D), lambda i: (i, 0))
    full = lambda shape: pl.BlockSpec(shape, lambda i: (0,) * len(shape))
    return _pcall(
        body, name=f"merge_bwd_l{layer}", grid=(rows // tm,),
        in_specs=[row_d, row_d, pl.BlockSpec((tm, D_SSM), lambda i: (i, 0)), row_d,
                  pl.BlockSpec((tm, D), lambda i: (i, 2)), pl.BlockSpec((tm, D), lambda i: (i, 3)),
                  full((D_SSM, D)), full((D_ATTN, D)), full((D, D)),
                  pl.BlockSpec((None, 1, D), lambda i: (layer, 0, 0))],
        out_specs=[row_d, row_d, row_d, row_d, row_d, pl.BlockSpec((tm, D_SSM), lambda i: (i, 0)), row_d,
                   pl.BlockSpec((1, D), lambda i: (0, 0))],
        out_shape=[SDS((rows, D), MXU_DTYPE)] * 5 + [SDS((rows, D_SSM), F32), SDS((rows, D_ATTN), F32),
                                                      SDS((1, D), F32)],
        compiler_params=_cparams("arbitrary"),
    )(dhm, mix, y_ssm, y_attn, proj, proj, w_o_ssm, w_o_attn, w_out, gain3)


def _attn_bwd(q, k, v, d_out, sinks, layer):
    rows = q.shape[0]
    n_blk = rows // BLK
    last = n_blk - 1

    def body(sink_ref, q_ref, km_ref, kp_ref, kc_ref, vm_ref, vp_ref, vc_ref, do_ref,
             dq_ref, dk_ref, dv_ref, dkm_ref, dvm_ref, ds_ref, dk_carry, dv_carry):
        i = pl.program_id(0)

        @pl.when(i == 0)
        def _():
            dkm_ref[...] = jnp.zeros_like(dkm_ref)
            dvm_ref[...] = jnp.zeros_like(dvm_ref)
            ds_ref[...] = jnp.zeros_like(ds_ref)
            dk_carry[...] = jnp.zeros_like(dk_carry)
            dv_carry[...] = jnp.zeros_like(dv_carry)

        @pl.when(i <= last)
        def _():
            bias = _attn_mask(i)
            for kvh in range(N_KV_HEADS):
                lanes = _head_lanes(kvh)
                k3 = jnp.concatenate([km_ref[:, lanes], kp_ref[:, lanes], kc_ref[:, lanes]], axis=0)
                v3 = jnp.concatenate([vm_ref[:, lanes], vp_ref[:, lanes], vc_ref[:, lanes]], axis=0)
                q4 = _group_rows(q_ref, kvh)
                do4 = _group_rows(do_ref, kvh).astype(MXU_DTYPE)
                p = _attn_probs(q4, k3, _group_bias(bias, sink_ref, layer, kvh))
                dp = _dot_nt(do4, v3)
                dsf = p * (dp - jnp.sum(dp * p, axis=-1, keepdims=True))
                dsc = dsf.astype(MXU_DTYPE)
                dv3 = _dot_tn(p.astype(MXU_DTYPE), do4)
                dk3 = _dot_tn(dsc, q4)
                dq4 = _dot(dsc, k3)
                for g in range(Q_PER_KV):
                    h = kvh * Q_PER_KV + g
                    dq_ref[:, _head_lanes(h)] = dq4[g * BLK:(g + 1) * BLK]
                    ds_ref[h:h + 1, :] += jnp.sum(dsf[g * BLK:(g + 1) * BLK, 0:BLK], axis=0, keepdims=True)
                dkm_ref[:, lanes] += dk3[0:BLK]
                dvm_ref[:, lanes] += dv3[0:BLK]
                dk_ref[:, lanes] = dk_carry[:, lanes] + dk3[BLK:2 * BLK]
                dv_ref[:, lanes] = dv_carry[:, lanes] + dv3[BLK:2 * BLK]
                dk_carry[:, lanes] = dk3[2 * BLK:3 * BLK]
                dv_carry[:, lanes] = dv3[2 * BLK:3 * BLK]

        @pl.when(i == last + 1)
        def _():
            dk_ref[...] = dk_carry[...]
            dv_ref[...] = dv_carry[...]

    cur = lambda i: (jnp.minimum(i, last), 0)
    prev = lambda i: (jnp.clip(i - 1, 0, last), 0)
    kv_meta = pl.BlockSpec((BLK, D_KV), lambda i: (0, 0))
    kv_prev = pl.BlockSpec((BLK, D_KV), prev)
    kv_cur = pl.BlockSpec((BLK, D_KV), cur)
    return _pcall(
        body, name=f"attn_bwd_l{layer}", grid=(n_blk + 1,),
        in_specs=[pl.BlockSpec(memory_space=pltpu.SMEM),
                  pl.BlockSpec((BLK, D_ATTN), cur),
                  kv_meta, kv_prev, kv_cur, kv_meta, kv_prev, kv_cur,
                  pl.BlockSpec((BLK, D_ATTN), cur)],
        out_specs=[pl.BlockSpec((BLK, D_ATTN), cur), kv_prev, kv_prev, kv_meta, kv_meta,
                   pl.BlockSpec((N_Q_HEADS, 128), lambda i: (0, 0))],
        out_shape=[SDS((rows, D_ATTN), F32), SDS((rows, D_KV), F32), SDS((rows, D_KV), F32),
                   SDS((BLK, D_KV), F32), SDS((BLK, D_KV), F32), SDS((N_Q_HEADS, 128), F32)],
        scratch_shapes=[pltpu.VMEM((BLK, D_KV), F32), pltpu.VMEM((BLK, D_KV), F32)],
        compiler_params=_cparams("arbitrary"),
    )(sinks, q, k, k, k, v, v, v, d_out)


def _rope_bwd(dq, dk, dv, dk_meta, dv_meta, cos, sin_a, sin_b, layer):
    rows = dq.shape[0]
    tm = _row_tile(rows)

    def body(dq_ref, dk_ref, dv_ref, dkm_ref, dvm_ref, c_ref, a_ref, b_ref, o_ref):
        c, a, b = c_ref[...], -a_ref[...], -b_ref[...]
        for t in range(8):
            x = dq_ref[:, t * 128:(t + 1) * 128]
            o_ref[:, t * 128:(t + 1) * 128] = (_rope_lanes(x, c, a, b) * ATTN_SCALE).astype(MXU_DTYPE)
        for t in range(2):
            x = dk_ref[:, t * 128:(t + 1) * 128]
            o_ref[:, D_ATTN + t * 128:D_ATTN + (t + 1) * 128] = _rope_lanes(x, c, a, b).astype(MXU_DTYPE)
        o_ref[:, D_ATTN + D_KV:] = dv_ref[...].astype(MXU_DTYPE)

        @pl.when(pl.program_id(0) == 0)
        def _():
            cb, ab, bb = c[0:BLK], a[0:BLK], b[0:BLK]
            is_meta = lax.broadcasted_iota(jnp.int32, (BLK, 128), 0) >= PAD_ROWS
            for t in range(2):
                x = dk_ref[0:BLK, t * 128:(t + 1) * 128] + jnp.where(is_meta, dkm_ref[:, t * 128:(t + 1) * 128], 0.0)
                o_ref[0:BLK, D_ATTN + t * 128:D_ATTN + (t + 1) * 128] = _rope_lanes(x, cb, ab, bb).astype(MXU_DTYPE)
                xv = dv_ref[0:BLK, t * 128:(t + 1) * 128] + jnp.where(is_meta, dvm_ref[:, t * 128:(t + 1) * 128], 0.0)
                o_ref[0:BLK, D_ATTN + D_KV + t * 128:D_ATTN + D_KV + (t + 1) * 128] = xv.astype(MXU_DTYPE)

    tab = pl.BlockSpec((tm, 128), lambda i: (i, 0))
    kv = pl.BlockSpec((tm, D_KV), lambda i: (i, 0))
    meta = pl.BlockSpec((BLK, D_KV), lambda i: (0, 0))
    return _pcall(
        body, name=f"rope_bwd_l{layer}", grid=(rows // tm,),
        in_specs=[pl.BlockSpec((tm, D_ATTN), lambda i: (i, 0)), kv, kv, meta, meta, tab, tab, tab],
        out_specs=pl.BlockSpec((tm, D_ATTN + 2 * D_KV), lambda i: (i, 0)),
        out_shape=SDS((rows, D_ATTN + 2 * D_KV), MXU_DTYPE),
        compiler_params=_cparams("parallel"),
    )(dq, dk, dv, dk_meta, dv_meta, cos, sin_a, sin_b)


def _s5_bwd(d_gated, y, proj, carry_in, ssm, w_glu, b_glu3, layer):
    rows = y.shape[0]
    n_chunks = rows // BLK
    b_mat, c_mat, t_re, t_im, d_skip = (ssm[k] for k in ("b_mat", "c_mat", "t_re", "t_im", "d_skip"))

    def body(dz_ref, y_ref, u_ref, cin_ref, bm_ref, cm_ref, tre_ref, tim_ref, d_ref, wg_ref, bg_ref,
             du_ref, dwg_ref, dbg_ref, dd_ref, dbm_ref, dcm_ref, dab_ref,
             lam_carry, bu_scr, s_scr, sp_scr, g_scr, lam_scr):
        step = pl.program_id(0)
        chunk = n_chunks - 1 - step

        @pl.when(step == 0)
        def _():
            for r in (dwg_ref, dbg_ref, dd_ref, dbm_ref, dcm_ref, dab_ref, lam_carry):
                r[...] = jnp.zeros_like(r)

        y = y_ref[...]
        u = u_ref[...]
        d_o = dz_ref[...]
        z, t = _gelu_parts(y)
        zb = z.astype(MXU_DTYPE)
        sg = _sigmoid(_dot(zb, wg_ref[...]) + bg_ref[...])
        dgl = d_o * z * (sg * (1.0 - sg))
        dglb = dgl.astype(MXU_DTYPE)
        dz = d_o * sg + _dot_nt(dglb, wg_ref[...])
        dwg_ref[...] += _dot_tn(zb, dglb)
        dbg_ref[...] += jnp.sum(dgl, axis=0, keepdims=True)
        dy = dz * _gelu_grad(y, t)
        dd_ref[...] += jnp.sum(dy * u, axis=0, keepdims=True)
        grow = lax.broadcasted_iota(jnp.int32, (BLK, 128), 0) + chunk * BLK
        for sb in range(N_SB):
            cols = slice(sb * 128, (sb + 1) * 128)
            u_sb = u[:, cols].astype(MXU_DTYPE)
            dy_sb = dy[:, cols]
            dyb = dy_sb.astype(MXU_DTYPE)
            bu_scr[sb] = _dot(u_sb, bm_ref[sb])
            _scan_tiles(bu_scr.at[sb], s_scr.at[sb], tre_ref, tim_ref, sb,
                        cin_ref[2 * sb:2 * sb + 1, :], cin_ref[2 * sb + 1:2 * sb + 2, :], False, prev_ref=sp_scr.at[sb])
            dcm_ref[sb] += _dot_tn(s_scr[sb].astype(MXU_DTYPE), dyb)
            g_scr[sb] = _dot_nt(dyb, cm_ref[sb])
            n_r, n_i = _scan_tiles(g_scr.at[sb], lam_scr.at[sb], tre_ref, tim_ref, sb,
                                   lam_carry[2 * sb:2 * sb + 1, :], lam_carry[2 * sb + 1:2 * sb + 2, :], True)
            lam_carry[2 * sb:2 * sb + 1, :] = n_r
            lam_carry[2 * sb + 1:2 * sb + 2, :] = n_i
            lr, li = lam_scr[sb, :, :SB_STATES], lam_scr[sb, :, SB_STATES:]
            spr, spi = sp_scr[sb, :, :SB_STATES], sp_scr[sb, :, SB_STATES:]
            dab_ref[2 * sb:2 * sb + 1, :] += jnp.sum(spr * lr + spi * li, axis=0, keepdims=True)
            dab_ref[2 * sb + 1:2 * sb + 2, :] += jnp.sum(spr * li - spi * lr, axis=0, keepdims=True)
            lam = lam_scr[sb].astype(MXU_DTYPE)
            dbm_ref[sb] += _dot_tn(u_sb, lam)
            du = _dot_nt(lam, bm_ref[sb]) + d_ref[:, cols] * dy_sb
            du_ref[:, cols] = jnp.where(grow >= PAD_ROWS, du, 0.0).astype(MXU_DTYPE)

    rev = lambda j: (n_chunks - 1 - j, 0)
    full = lambda shape: pl.BlockSpec(shape, lambda j: (0,) * len(shape))
    tables = [full((N_SB, 8, SCAN_TILE, SB_STATES))] * 2
    chunk_scratch = pltpu.VMEM((N_SB, BLK, 2 * SB_STATES), F32)
    return _pcall(
        body, name=f"s5_bwd_l{layer}", grid=(n_chunks,),
        in_specs=[pl.BlockSpec((BLK, D_SSM), rev), pl.BlockSpec((BLK, D_SSM), rev), pl.BlockSpec((BLK, D_SSM), rev),
                  pl.BlockSpec((None, 8, SB_STATES), lambda j: (n_chunks - 1 - j, 0, 0)),
                  full((N_SB, 128, 2 * SB_STATES)), full((N_SB, 2 * SB_STATES, 128))] + tables + [
                  full((1, D_SSM)), full((D_SSM, D_SSM)),
                  pl.BlockSpec((None, 1, D_SSM), lambda j: (layer, 0, 0))],
        out_specs=[pl.BlockSpec((BLK, D_SSM), rev), full((D_SSM, D_SSM)), full((1, D_SSM)), full((1, D_SSM)),
                   full((N_SB, 128, 2 * SB_STATES)), full((N_SB, 2 * SB_STATES, 128)), full((8, SB_STATES))],
        out_shape=[SDS((rows, D_SSM), MXU_DTYPE), SDS((D_SSM, D_SSM), F32), SDS((1, D_SSM), F32), SDS((1, D_SSM), F32),
                   SDS((N_SB, 128, 2 * SB_STATES), F32), SDS((N_SB, 2 * SB_STATES, 128), F32), SDS((8, SB_STATES), F32)],
        scratch_shapes=[pltpu.VMEM((8, SB_STATES), F32)] + [chunk_scratch] * 5,
        compiler_params=_cparams("arbitrary"),
    )(d_gated, y, proj, carry_in, b_mat, c_mat, t_re, t_im, d_skip, w_glu, b_glu3)


DPROJ_PIECES = ((0, 1), (1, 3), (4, 2), (6, 2))


def _in_bwd(dproj_pieces, dhm, hres, gain3, w_in_g, layer):
    rows = hres.shape[0]
    tm = _row_tile(rows)

    def body(du_ref, dqkv_ref, dgs_ref, dga_ref, dh_ref, x_ref, g_ref, w_hbm, dx_ref, dg_ref, acc, w_scr, w_sem):
        i = pl.program_id(0)
        j = pl.program_id(1)
        _load_resident(w_hbm, w_scr, w_sem, layer, (i == 0) & (j == 0))

        @pl.when((i == 0) & (j == 0))
        def _():
            dg_ref[...] = jnp.zeros_like(dg_ref)

        @pl.when(j == 0)
        def _():
            acc[...] = jnp.zeros_like(acc)

        for piece_ref, (first, count) in zip((du_ref, dqkv_ref, dgs_ref, dga_ref), DPROJ_PIECES):
            @pl.when((j >= first) & (j < first + count))
            def _():
                acc[...] += _dot_nt(piece_ref[...], w_scr[j])

        @pl.when(j == N_DEV - 1)
        def _():
            dx, dg = _rms_bwd(x_ref[...], g_ref[...], acc[...])
            dg_ref[...] += dg
            dx_ref[...] = dh_ref[...] + dx

    row_d = pl.BlockSpec((tm, D), lambda i, j: (i, 0))

    def piece_spec(first, count):
        return pl.BlockSpec((tm, COL_SHARD), lambda i, j: (i, jnp.clip(j - first, 0, count - 1)))

    return _pcall(
        body, name=f"in_bwd_l{layer}", grid=(rows // tm, N_DEV),
        in_specs=[piece_spec(*p) for p in DPROJ_PIECES] + [
                  row_d, row_d,
                  pl.BlockSpec((None, 1, D), lambda i, j: (layer, 0, 0)),
                  pl.BlockSpec(memory_space=pl.ANY)],
        out_specs=[row_d, pl.BlockSpec((1, D), lambda i, j: (0, 0))],
        out_shape=[SDS((rows, D), F32), SDS((1, D), F32)],
        scratch_shapes=[pltpu.VMEM((tm, D), F32), pltpu.VMEM((N_DEV, D, COL_SHARD), MXU_DTYPE),
                        pltpu.SemaphoreType.DMA((N_DEV,))],
        compiler_params=_cparams("arbitrary", "arbitrary"),
    )(*dproj_pieces, dhm, hres, gain3, w_in_g)


_ADAM_C1 = 1.0 / (1.0 - ADAM_B1 ** ADAM_STEP)
_ADAM_C2 = 1.0 / (1.0 - ADAM_B2 ** ADAM_STEP)


def _adam_math(w, g, m, v):
    m = ADAM_B1 * m + (1.0 - ADAM_B1) * g
    v = ADAM_B2 * v + (1.0 - ADAM_B2) * (g * g)
    delta = -ADAM_LR * ((m * _ADAM_C1) / (jnp.sqrt(v * _ADAM_C2) + ADAM_EPS) + ADAM_WD * w)
    return delta, m, v


def _adamw_layers(parts0, parts1, w, m, v, name):
    _, rows, cols = w.shape
    tr = min(rows, (1 << 16) // cols)
    nt = rows // tr

    def body(p0_ref, p1_ref, w_ref, m_ref, v_ref, g_ref, d_ref, nm_ref, nv_ref):
        layer = pl.program_id(0)

        def run(p_ref):
            g = p_ref[0].astype(F32)
            for s in range(1, N_DEV):
                g = g + p_ref[s].astype(F32)
            delta, nm, nv = _adam_math(w_ref[...], g, m_ref[...], v_ref[...])
            g_ref[...] = g
            d_ref[...] = delta
            nm_ref[...] = nm
            nv_ref[...] = nv

        @pl.when(layer == 0)
        def _():
            run(p0_ref)

        @pl.when(layer == 1)
        def _():
            run(p1_ref)

    wspec = pl.BlockSpec((None, tr, cols), lambda l, i: (l, i, 0))
    return _pcall(
        body, name=name, grid=(2, nt),
        in_specs=[pl.BlockSpec((N_DEV, tr, cols), lambda l, i: (0, jnp.where(l == 0, i, nt - 1), 0)),
                  pl.BlockSpec((N_DEV, tr, cols), lambda l, i: (0, jnp.where(l == 1, i, 0), 0)),
                  wspec, wspec, wspec],
        out_specs=[wspec] * 4, out_shape=[SDS(w.shape, F32)] * 4,
        compiler_params=_cparams("arbitrary", "arbitrary"),
    )(parts0, parts1, w, m, v)


def _adamw_packed(g, w, m, v, name):
    def body(g_ref, w_ref, m_ref, v_ref, d_ref, nm_ref, nv_ref):
        delta, nm, nv = _adam_math(w_ref[...], g_ref[...], m_ref[...], v_ref[...])
        d_ref[...] = delta
        nm_ref[...] = nm
        nv_ref[...] = nv

    vmem = pl.BlockSpec(memory_space=pltpu.VMEM)
    return _pcall(body, name=name, out_shape=[SDS(g.shape, F32)] * 3, in_specs=[vmem] * 4, out_specs=[vmem] * 3,
                  compiler_params=_cparams())(g, w, m, v)


def _ssm_discretize(a_re, a_im, log_dt, b_re, b_im):
    dt = jnp.exp(log_dt)[:, None]
    mag = jnp.exp(a_re * dt)
    ang = a_im * dt
    ab_re, ab_im = mag * jnp.cos(ang), mag * jnp.sin(ang)
    xr, xi = ab_re - 1.0, ab_im
    den = a_re * a_re + a_im * a_im
    q_re = (xr * a_re + xi * a_im) / den
    q_im = (xi * a_re - xr * a_im) / den
    bb_re = q_re[..., None] * b_re - q_im[..., None] * b_im
    bb_im = q_re[..., None] * b_im + q_im[..., None] * b_re
    return ab_re, ab_im, bb_re, bb_im


def _block_diag_b(bb):
    m = jnp.einsum("sgnc,gh->sgchn", bb.reshape(N_SB, 8, N_STATE, GROUP_CH), jnp.eye(8, dtype=F32))
    return m.reshape(N_SB, 128, SB_STATES)


def _block_diag_b_t(dm):
    return jnp.einsum("sgchn,gh->sgnc", dm.reshape(N_SB, 8, GROUP_CH, 8, N_STATE),
                      jnp.eye(8, dtype=F32)).reshape(N_GROUPS, N_STATE, GROUP_CH)


def _block_diag_c(cc):
    m = jnp.einsum("sgcn,gh->sgnhc", cc.reshape(N_SB, 8, GROUP_CH, N_STATE), jnp.eye(8, dtype=F32))
    return m.reshape(N_SB, SB_STATES, 128)


def _block_diag_c_t(dm):
    return jnp.einsum("sgnhc,gh->sgcn", dm.reshape(N_SB, 8, N_STATE, 8, GROUP_CH),
                      jnp.eye(8, dtype=F32)).reshape(N_GROUPS, GROUP_CH, N_STATE)


def _ssm_tables(ab_re, ab_im, bb_re, bb_im, c_re, c_im, d_skip):
    pr, pi = ab_re.reshape(1, -1), ab_im.reshape(1, -1)
    cr, ci = pr, pi
    squares = []
    for _ in range(3):
        squares.append((cr, ci))
        pr, pi = (jnp.concatenate([pr, pr * cr - pi * ci], axis=0),
                  jnp.concatenate([pi, pr * ci + pi * cr], axis=0))
        cr, ci = cr * cr - ci * ci, 2.0 * cr * ci
    r = jnp.arange(SCAN_TILE)[:, None]
    fwd = [(jnp.where(r >= (1 << k), squares[k][0], 0.0), jnp.where(r >= (1 << k), squares[k][1], 0.0))
           for k in range(3)] + [(pr, pi)]
    rev = [(jnp.where(r < SCAN_TILE - (1 << k), squares[k][0], 0.0),
            jnp.where(r < SCAN_TILE - (1 << k), -squares[k][1], 0.0)) for k in range(3)] + [(pr[::-1], -pi[::-1])]
    table = lambda part: jnp.stack([e[part] for e in fwd + rev]).reshape(
        8, SCAN_TILE, N_SB, SB_STATES).transpose(2, 0, 1, 3)
    return dict(
        b_mat=jnp.concatenate([_block_diag_b(bb_re), _block_diag_b(bb_im)], axis=-1).astype(MXU_DTYPE),
        c_mat=jnp.concatenate([_block_diag_c(c_re), -_block_diag_c(c_im)], axis=1).astype(MXU_DTYPE),
        t_re=table(0), t_im=table(1),
        d_skip=d_skip.reshape(1, D_SSM))


def _rope_tables(rows):
    pos = (jnp.arange(rows, dtype=jnp.int32) - PAD_ROWS).astype(F32)
    inv_freq = 1.0 / (ROPE_THETA ** (jnp.arange(0, HEAD_DIM, 2, dtype=F32) / HEAD_DIM))
    ang = pos[:, None] * inv_freq[None, :]
    ang = jnp.concatenate([ang, ang, ang, ang], axis=-1)
    first_half = (jnp.arange(128) % HEAD_DIM) < HEAD_DIM // 2
    sin = jnp.sin(ang)
    return jnp.cos(ang), jnp.where(first_half, -sin, 0.0), jnp.where(first_half, 0.0, sin)


def _pack(arrays):
    flat = jnp.concatenate([a.reshape(-1).astype(F32) for a in arrays])
    pad = (-flat.shape[0]) % 1024
    return jnp.pad(flat, (0, pad)).reshape(-1, 128)


def _unpack(packed, like):
    flat = packed.reshape(-1)
    out, off = [], 0
    for a in like:
        n = math.prod(a.shape)
        out.append(flat[off:off + n].reshape(a.shape))
        off += n
    return out


BIG = ("w_in", "w_glu", "w_o_ssm", "w_o_attn", "w_out", "w_up", "w_down")
WEIGHTS = ("meta_tokens", "norm_mix_pre", "norm_mix_post", "norm_mlp_pre", "norm_mlp_post", "w_in",
           "ssm_a_re", "ssm_a_im", "ssm_log_dt", "ssm_b_re", "ssm_b_im", "ssm_c_re", "ssm_c_im", "ssm_d",
           "w_glu", "b_glu", "attn_sinks", "w_o_ssm", "w_o_attn", "w_out", "w_up", "w_down")
SMALL = tuple(n for n in WEIGHTS if n not in BIG)


def kernel(x, meta_tokens, norm_mix_pre, norm_mix_post, norm_mlp_pre, norm_mlp_post, w_in, ssm_a_re, ssm_a_im, ssm_log_dt, ssm_b_re, ssm_b_im, ssm_c_re, ssm_c_im, ssm_d, w_glu, b_glu, attn_sinks, w_o_ssm, w_o_attn, w_out, w_up, w_down, loss_target, m_meta_tokens, m_norm_mix_pre, m_norm_mix_post, m_norm_mlp_pre, m_norm_mlp_post, m_w_in, m_ssm_a_re, m_ssm_a_im, m_ssm_log_dt, m_ssm_b_re, m_ssm_b_im, m_ssm_c_re, m_ssm_c_im, m_ssm_d, m_w_glu, m_b_glu, m_attn_sinks, m_w_o_ssm, m_w_o_attn, m_w_out, m_w_up, m_w_down, v_meta_tokens, v_norm_mix_pre, v_norm_mix_post, v_norm_mlp_pre, v_norm_mlp_post, v_w_in, v_ssm_a_re, v_ssm_a_im, v_ssm_log_dt, v_ssm_b_re, v_ssm_b_im, v_ssm_c_re, v_ssm_c_im, v_ssm_d, v_w_glu, v_b_glu, v_attn_sinks, v_w_o_ssm, v_w_o_attn, v_w_out, v_w_up, v_w_down):
    args = locals()
    w = {n: args[n] for n in WEIGHTS}
    m = {n: args["m_" + n] for n in WEIGHTS}
    v = {n: args["v_" + n] for n in WEIGHTS}
    n_layers = w_in.shape[0]
    seq = x.shape[1]
    rows = seq + BLK
    my_slot = _slot(_mesh_pos())

    gathered = _all_gather([w[n].astype(MXU_DTYPE) for n in BIG] + [meta_tokens], "gather_weights")
    w_in_g, w_glu_g, w_o_ssm_g, w_o_attn_g, w_out_g, w_up_g, w_down_g, meta_g = gathered
    meta_full = meta_g.transpose(1, 0, 2).reshape(N_META, D)
    w_glu_f = w_glu_g.transpose(1, 0, 2, 3).reshape(n_layers, D_SSM, D_SSM)
    w_o_ssm_f = w_o_ssm_g.transpose(1, 2, 0, 3).reshape(n_layers, D_SSM, D)
    w_o_attn_f = w_o_attn_g.transpose(1, 0, 2, 3).reshape(n_layers, D_ATTN, D)
    w_out_f = w_out_g.transpose(1, 0, 2, 3).reshape(n_layers, D, D)

    gains = {n: w[n].reshape(n_layers, 1, D) for n in ("norm_mix_pre", "norm_mix_post", "norm_mlp_pre", "norm_mlp_post")}
    b_glu3 = b_glu.reshape(n_layers, 1, D_SSM)
    cos, sin_a, sin_b = _rope_tables(rows)

    def ssm_setup(l):
        disc, disc_vjp = jax.vjp(_ssm_discretize, ssm_a_re[l], ssm_a_im[l], ssm_log_dt[l], ssm_b_re[l], ssm_b_im[l])
        return _ssm_tables(*disc, ssm_c_re[l], ssm_c_im[l], ssm_d[l]), disc_vjp

    hres = jnp.concatenate([jnp.zeros((PAD_ROWS, D), F32), meta_full, x[0]], axis=0)

    saved = []
    for l in range(n_layers):
        ssm, disc_vjp = ssm_setup(l)
        proj, h = _in_proj(hres, gains["norm_mix_pre"], w_in_g, l)
        q, k, vv = _rope_fwd(proj, cos, sin_a, sin_b, l)
        y, y_ssm, carry_in = _s5_fwd(proj, ssm, w_glu_f[l], b_glu3, l)
        y_attn = _attn_fwd(q, k, vv, attn_sinks, l)
        merged, mix, hres_mid = _merge_fwd(y_ssm, y_attn, proj, hres, w_o_ssm_f[l], w_o_attn_f[l], w_out_f[l],
                                                   gains["norm_mix_post"], l)
        up, h2, ff, hres_out = _mlp_fwd(hres_mid, gains["norm_mlp_pre"], gains["norm_mlp_post"], w_up_g, w_down_g, l)
        saved.append(dict(ssm=ssm, disc_vjp=disc_vjp, hres=hres, proj=proj, h=h, q=q, k=k, v=vv, y=y, y_ssm=y_ssm,
                          carry_in=carry_in, y_attn=y_attn, merged=merged, mix=mix, hres_mid=hres_mid,
                          up=up, h2=h2, ff=ff))
        hres = hres_out

    dhres, loss_vec = _loss_and_grad(hres, loss_target[0])
    loss = lax.psum(loss_vec[0, 0], MESH_AXES)

    small_grads = {n: [None] * n_layers for n in SMALL if n != "meta_tokens"}
    recv = [None] * n_layers
    for l in reversed(range(n_layers)):
        s = saved[l]
        dff, dup, dhm, dg_mlp_post, dg_mlp_pre = _mlp_bwd(dhres, s["ff"], s["up"], s["hres_mid"], gains["norm_mlp_pre"],
                                                          gains["norm_mlp_post"], w_up_g, w_down_g, l)
        dw_down = _matmul_tn(s["up"], dff, f"dw_down_l{l}", a_fn=_relu_squared).reshape(N_DEV, COL_SHARD, D)
        dw_up = _matmul_tn(s["h2"], dup, f"dw_up_l{l}", dev_major_cols=COL_SHARD)
        dmix, da1, da2, dgs, dga, dy_ssm, dy_attn, dg_mix_post = _merge_bwd(
            dhm, s["mix"], s["y_ssm"], s["y_attn"], s["proj"], w_o_ssm_f[l], w_o_attn_f[l], w_out_f[l], gains["norm_mix_post"], l)
        dw_out = _matmul_tn(s["merged"], dmix, f"dw_out_l{l}").reshape(N_DEV, D // N_DEV, D)
        dw_o_attn = _matmul_tn(s["y_attn"], da2, f"dw_o_attn_l{l}").reshape(N_DEV, D_ATTN // N_DEV, D)
        dw_o_ssm = _matmul_tn(s["y_ssm"], da1, f"dw_o_ssm_l{l}", dev_major_cols=D // N_DEV)
        dq, dk, dv, dk_meta, dv_meta, dsink = _attn_bwd(s["q"], s["k"], s["v"], dy_attn, attn_sinks, l)
        dqkv = _rope_bwd(dq, dk, dv, dk_meta, dv_meta, cos, sin_a, sin_b, l)
        du, dw_glu, db_glu, dd_skip, db_mat, dc_mat, dab = _s5_bwd(dy_ssm, s["y"], s["proj"], s["carry_in"], s["ssm"],
                                                                    w_glu_f[l], b_glu3, l)
        dproj = (du, dqkv, dgs, dga)
        dw_in = jnp.concatenate([_matmul_tn(s["h"], piece, f"dw_in{k}_l{l}", dev_major_cols=COL_SHARD)
                                 for k, piece in enumerate(dproj)], axis=0)
        dhres, dg_mix_pre = _in_bwd(dproj, dhm, s["hres"], gains["norm_mix_pre"], w_in_g, l)

        recv[l] = _exchange_slots([dw_in, dw_glu.astype(XFER_DTYPE).reshape(N_DEV, D_SSM // N_DEV, D_SSM), dw_o_ssm,
                                   dw_o_attn, dw_out, dw_up, dw_down], f"scatter_grads_l{l}")

        dab = dab.reshape(N_SB, 2, SB_STATES)
        da_re, da_im, dlog_dt, db_re, db_im = s["disc_vjp"]((
            dab[:, 0].reshape(N_GROUPS, N_STATE), dab[:, 1].reshape(N_GROUPS, N_STATE),
            _block_diag_b_t(db_mat[..., :SB_STATES]), _block_diag_b_t(db_mat[..., SB_STATES:])))
        for name, val in (("norm_mix_pre", dg_mix_pre[0]), ("norm_mix_post", dg_mix_post[0]),
                          ("norm_mlp_pre", dg_mlp_pre[0]), ("norm_mlp_post", dg_mlp_post[0]),
                          ("ssm_a_re", da_re), ("ssm_a_im", da_im), ("ssm_log_dt", dlog_dt),
                          ("ssm_b_re", db_re), ("ssm_b_im", db_im),
                          ("ssm_c_re", _block_diag_c_t(dc_mat[:, :SB_STATES])),
                          ("ssm_c_im", -_block_diag_c_t(dc_mat[:, SB_STATES:])),
                          ("ssm_d", dd_skip.reshape(N_GROUPS, GROUP_CH)), ("b_glu", db_glu[0]),
                          ("attn_sinks", dsink[:, 0])):
            small_grads[name][l] = val

    grad_x = dhres[BLK:][None]
    small_names = [n for n in SMALL if n != "meta_tokens"]
    partial_small = [dhres[PAD_ROWS:BLK]] + [jnp.stack(small_grads[n]) for n in small_names]
    summed = _unpack(_all_reduce_small(_pack(partial_small), "reduce_small_grads"), partial_small)
    grads = dict(zip(small_names, summed[1:]))
    grads["meta_tokens"] = lax.dynamic_slice_in_dim(summed[0], my_slot * (D // N_DEV), D // N_DEV, axis=1)

    delta, new_m, new_v = {}, {}, {}
    for idx, n in enumerate(BIG):
        grads[n], delta[n], new_m[n], new_v[n] = _adamw_layers(recv[0][idx], recv[1][idx], w[n], m[n], v[n], f"adamw_{n}")
    like = [w[n] for n in SMALL]
    d_s, m_s, v_s = _adamw_packed(_pack([grads[n] for n in SMALL]), _pack(like), _pack([m[n] for n in SMALL]),
                                  _pack([v[n] for n in SMALL]), "adamw_small")
    for n, dd, mm, vs in zip(SMALL, _unpack(d_s, like), _unpack(m_s, like), _unpack(v_s, like)):
        delta[n], new_m[n], new_v[n] = dd, mm, vs

    return (loss, grad_x, *[grads[n] for n in WEIGHTS], *[delta[n] for n in WEIGHTS],
            *[new_m[n] for n in WEIGHTS], *[new_v[n] for n in WEIGHTS])
```

```python
import functools
import math

import jax
import jax.numpy as jnp
from jax import lax
from jax.experimental import pallas as pl
from jax.experimental.pallas import tpu as pltpu

F32 = jnp.float32
MXU_DTYPE = jnp.bfloat16
XFER_DTYPE = MXU_DTYPE
_pcall = pl.pallas_call
SDS = jax.ShapeDtypeStruct

D = 1024
D_SSM = 512
D_ATTN = 1024
D_KV = 256
D_FF = 4096
D_IN = 4096
HEAD_DIM = 64
N_Q_HEADS = 16
N_KV_HEADS = 4
Q_PER_KV = 4
N_META = 16
BLK = 128
PAD_ROWS = BLK - N_META
N_GROUPS = 32
N_STATE = 64
GROUP_CH = 16
N_SB = 4
SB_STATES = 512
ROPE_THETA = 10000.0
ATTN_SCALE = HEAD_DIM ** -0.5
NEG_INF = -1e30
RMS_EPS = 1e-6
N_DEV = 8
COL_SHARD = 512

ADAM_LR = 0.001
ADAM_B1 = 0.9
ADAM_B2 = 0.999
ADAM_EPS = 1e-08
ADAM_WD = 0.01
ADAM_STEP = 10

VMEM_LIMIT = 56 * 1024 * 1024
MESH_AXES = ("x", "y", "c")

_NT = (((1,), (1,)), ((), ()))
_TN = (((0,), (0,)), ((), ()))


def _cparams(*sem):
    return pltpu.CompilerParams(dimension_semantics=tuple(sem) if sem else None,
                                vmem_limit_bytes=VMEM_LIMIT)


def _row_tile(rows, cap=640):
    for t in (640, 512, 320, 256, 128):
        if t <= cap and rows % t == 0:
            return t
    raise ValueError(f"unsupported row count {rows}")


def _dot(a, b):
    return jnp.dot(a, b, preferred_element_type=F32)


def _dot_nt(a, b):
    return lax.dot_general(a, b, _NT, preferred_element_type=F32)


def _dot_tn(a, b):
    return lax.dot_general(a, b, _TN, preferred_element_type=F32)


def _sigmoid(x):
    return 1.0 / (1.0 + jnp.exp(-x))


_GELU_C = math.sqrt(2.0 / math.pi)


def _gelu_parts(y):
    t = jnp.tanh(_GELU_C * (y + 0.044715 * (y * y * y)))
    return 0.5 * y * (1.0 + t), t


def _gelu_grad(y, t):
    return 0.5 * (1.0 + t) + 0.5 * y * (1.0 - t * t) * (_GELU_C * (1.0 + 0.134145 * (y * y)))


def _rms_fwd(x, gain):
    r = lax.rsqrt(jnp.mean(x * x, axis=-1, keepdims=True) + RMS_EPS)
    return (x * r) * gain


def _rms_bwd(x, gain, dout):
    r = lax.rsqrt(jnp.mean(x * x, axis=-1, keepdims=True) + RMS_EPS)
    xh = x * r
    dxh = dout * gain
    dx = r * (dxh - xh * jnp.mean(dxh * xh, axis=-1, keepdims=True))
    return dx, jnp.sum(dout * xh, axis=0, keepdims=True)


def _mesh_pos():
    return lax.axis_index("x"), lax.axis_index("y"), lax.axis_index("c")


def _peer(pos, d):
    x, y, c = pos
    return (1 - x if d & 4 else x, 1 - y if d & 2 else y, 1 - c if d & 1 else c)


def _slot(pos):
    return 4 * pos[0] + 2 * pos[1] + pos[2]


_HBM = pl.BlockSpec(memory_space=pltpu.HBM)
_SEM = pl.BlockSpec(memory_space=pltpu.SEMAPHORE)
_DATAFLOW = pltpu.SideEffectType.DATAFLOW_SIDE_EFFECTING


def _exchange_copy(gather, src_ref, land_ref, sems, k, d, me, send_side):
    peer = _peer(me, d)
    sender = me if send_side else peer
    src = src_ref if gather else src_ref.at[_slot(peer) if send_side else _slot(me)]
    return pltpu.make_async_remote_copy(
        src_ref=src, dst_ref=land_ref.at[_slot(sender)],
        send_sem=sems[0].at[k * (N_DEV - 1) + d - 1], recv_sem=sems[1].at[k * (N_DEV - 1) + d - 1],
        device_id=peer, device_id_type=pl.DeviceIdType.MESH)


def _exchange_start(srcs, gather, name):
    n = len(srcs)
    lands = [lax.empty(((N_DEV,) + s.shape) if gather else s.shape, s.dtype) for s in srcs]

    def body(*refs):
        src_refs, land_refs = refs[:n], refs[n:2 * n]
        sems = refs[2 * n:2 * n + 2]
        token, local_sems = refs[4 * n + 2], refs[4 * n + 3]
        me = _mesh_pos()
        for k in range(n):
            for d in range(1, N_DEV):
                _exchange_copy(gather, src_refs[k], land_refs[k], sems, k, d, me, True).start()
        own = [pltpu.make_async_copy(src_refs[k] if gather else src_refs[k].at[_slot(me)],
                                     land_refs[k].at[_slot(me)], local_sems.at[k]) for k in range(n)]
        for cp in own:
            cp.start()
        for cp in own:
            cp.wait()
        token[...] = jnp.zeros_like(token)

    sem_type = pltpu.SemaphoreType.DMA((n * (N_DEV - 1),))
    outs = _pcall(
        body, name=name,
        out_shape=(sem_type, sem_type, *[pltpu.HBM(a.shape, a.dtype) for a in list(srcs) + lands], SDS((8, 128), F32)),
        in_specs=[_HBM] * (2 * n),
        out_specs=(_SEM, _SEM, *[_HBM] * (2 * n), pl.BlockSpec(memory_space=pltpu.VMEM)),
        input_output_aliases={k: 2 + k for k in range(2 * n)},
        scratch_shapes=[pltpu.SemaphoreType.DMA((n,))],
        compiler_params=pltpu.CompilerParams(has_side_effects=_DATAFLOW),
    )(*[pltpu.with_memory_space_constraint(a, pltpu.HBM) for a in list(srcs) + lands])
    return dict(sems=outs[:2], srcs=outs[2:2 + n], lands=outs[2 + n:2 + 2 * n], token=outs[-1], gather=gather)


def _exchange_wait(started, after, name):
    n = len(started["srcs"])
    gather = started["gather"]

    def body(*refs):
        src_refs, land_refs = refs[:n], refs[n:2 * n]
        sems = refs[2 * n:2 * n + 2]
        me = _mesh_pos()
        for k in range(n):
            for d in range(1, N_DEV):
                _exchange_copy(gather, src_refs[k], land_refs[k], sems, k, d, me, True).wait_send()
        for k in range(n):
            for d in range(1, N_DEV):
                _exchange_copy(gather, src_refs[k], land_refs[k], sems, k, d, me, False).wait_recv()

    arrays = list(started["srcs"]) + list(started["lands"])
    outs = _pcall(
        body, name=name,
        out_shape=tuple(pltpu.HBM(a.shape, a.dtype) for a in arrays),
        in_specs=[_HBM] * (2 * n) + [_SEM, _SEM] + [pl.BlockSpec(memory_space=pl.ANY)] * len(after),
        out_specs=tuple([_HBM] * (2 * n)),
        input_output_aliases={k: k for k in range(2 * n)},
        compiler_params=pltpu.CompilerParams(has_side_effects=_DATAFLOW),
    )(*arrays, *started["sems"], *after)
    return list(outs[n:])


def _all_reduce_small(packed, name):
    rows = packed.shape[0]

    def body(x_ref, out_ref, gath, send_sems, recv_sems):
        me = _mesh_pos()
        my_slot = _slot(me)
        gath[my_slot] = x_ref[...]
        sends = [pltpu.make_async_remote_copy(
            src_ref=x_ref, dst_ref=gath.at[my_slot],
            send_sem=send_sems.at[d - 1], recv_sem=recv_sems.at[d - 1],
            device_id=_peer(me, d), device_id_type=pl.DeviceIdType.MESH) for d in range(1, N_DEV)]
        for cp in sends:
            cp.start()
        for d in range(1, N_DEV):
            pltpu.make_async_remote_copy(
                src_ref=x_ref, dst_ref=gath.at[_slot(_peer(me, d))],
                send_sem=send_sems.at[d - 1], recv_sem=recv_sems.at[d - 1],
                device_id=_peer(me, d), device_id_type=pl.DeviceIdType.MESH).wait_recv()
        for cp in sends:
            cp.wait_send()
        acc = gath[0]
        for s in range(1, N_DEV):
            acc = acc + gath[s]
        out_ref[...] = acc

    vmem = pl.BlockSpec(memory_space=pltpu.VMEM)
    return _pcall(
        body, name=name, out_shape=SDS(packed.shape, F32), in_specs=[vmem], out_specs=vmem,
        scratch_shapes=[pltpu.VMEM((N_DEV, rows, 128), F32),
                        pltpu.SemaphoreType.DMA((N_DEV - 1,)), pltpu.SemaphoreType.DMA((N_DEV - 1,))],
        compiler_params=_cparams(),
    )(packed)


def _load_resident(w_hbm, w_scr, sems, first_step):
    @pl.when(first_step)
    def _():
        copies = [pltpu.make_async_copy(w_hbm.at[s], w_scr.at[s], sems.at[s]) for s in range(N_DEV)]
        for cp in copies:
            cp.start()
        for cp in copies:
            cp.wait()


def _in_proj(hres, gain3, w_in_g, layer):
    rows = hres.shape[0]
    tm = _row_tile(rows)

    def body(x_ref, g_ref, w_hbm, proj_ref, h_ref, h_scr, w_scr, w_sem):
        j = pl.program_id(1)
        _load_resident(w_hbm, w_scr, w_sem, (pl.program_id(0) == 0) & (j == 0))

        @pl.when(j == 0)
        def _():
            hn = _rms_fwd(x_ref[...], g_ref[...]).astype(MXU_DTYPE)
            h_scr[...] = hn
            h_ref[...] = hn

        proj_ref[...] = _dot(h_scr[...], w_scr[j])

    return _pcall(
        body, name=f"in_proj_l{layer}", grid=(rows // tm, N_DEV),
        in_specs=[pl.BlockSpec((tm, D), lambda i, j: (i, 0)),
                  pl.BlockSpec((None, 1, D), lambda i, j: (layer, 0, 0)),
                  pl.BlockSpec(memory_space=pl.ANY)],
        out_specs=[pl.BlockSpec((tm, COL_SHARD), lambda i, j: (i, j)),
                   pl.BlockSpec((tm, D), lambda i, j: (i, 0))],
        out_shape=[SDS((rows, D_IN), F32), SDS((rows, D), MXU_DTYPE)],
        scratch_shapes=[pltpu.VMEM((tm, D), MXU_DTYPE), pltpu.VMEM((N_DEV, D, COL_SHARD), MXU_DTYPE),
                        pltpu.SemaphoreType.DMA((N_DEV,))],
        compiler_params=_cparams("arbitrary", "arbitrary"),
    )(hres, gain3, w_in_g)


def _rope_lanes(t, cos, sin_a, sin_b):
    return t * cos + pltpu.roll(t, 96, 1) * sin_a + pltpu.roll(t, 32, 1) * sin_b


def _rope_fwd(proj, cos, sin_a, sin_b, layer):
    rows = proj.shape[0]
    tm = _row_tile(rows)

    def body(q0_ref, q1_ref, kv_ref, c_ref, a_ref, b_ref, qo_ref, ko_ref, vo_ref):
        c, a, b = c_ref[...], a_ref[...], b_ref[...]
        for half, q_ref in enumerate((q0_ref, q1_ref)):
            for t in range(4):
                x = q_ref[:, t * 128:(t + 1) * 128]
                lo = half * 512 + t * 128
                qo_ref[:, lo:lo + 128] = (_rope_lanes(x, c, a, b) * ATTN_SCALE).astype(MXU_DTYPE)
        for t in range(2):
            x = kv_ref[:, t * 128:(t + 1) * 128]
            ko_ref[:, t * 128:(t + 1) * 128] = _rope_lanes(x, c, a, b).astype(MXU_DTYPE)
        vo_ref[...] = kv_ref[:, D_KV:2 * D_KV].astype(MXU_DTYPE)

    tab = pl.BlockSpec((tm, 128), lambda i: (i, 0))
    return _pcall(
        body, name=f"rope_fwd_l{layer}", grid=(rows // tm,),
        in_specs=[pl.BlockSpec((tm, 512), lambda i: (i, 1)), pl.BlockSpec((tm, 512), lambda i: (i, 2)),
                  pl.BlockSpec((tm, 512), lambda i: (i, 3)), tab, tab, tab],
        out_specs=[pl.BlockSpec((tm, D_ATTN), lambda i: (i, 0)), pl.BlockSpec((tm, D_KV), lambda i: (i, 0)),
                   pl.BlockSpec((tm, D_KV), lambda i: (i, 0))],
        out_shape=[SDS((rows, D_ATTN), MXU_DTYPE), SDS((rows, D_KV), MXU_DTYPE), SDS((rows, D_KV), MXU_DTYPE)],
        compiler_params=_cparams("parallel"),
    )(proj, proj, proj, cos, sin_a, sin_b)


SCAN_TILE = 8


def _scan_tiles(x_ref, out_ref, tre_ref, tim_ref, sb, t_r, t_i, reverse, prev_ref=None):
    base = 4 if reverse else 0
    n_tiles = BLK // SCAN_TILE
    row = lax.broadcasted_iota(jnp.int32, (SCAN_TILE, SB_STATES), 0)
    for j in (range(n_tiles - 1, -1, -1) if reverse else range(n_tiles)):
        rows = slice(SCAN_TILE * j, SCAN_TILE * (j + 1))
        xr = x_ref[rows, :SB_STATES]
        xi = x_ref[rows, SB_STATES:]
        for k in range(3):
            shift = SCAN_TILE - (1 << k) if reverse else (1 << k)
            rr = pltpu.roll(xr, shift, 0)
            ri = pltpu.roll(xi, shift, 0)
            ar = tre_ref[sb, base + k]
            ai = tim_ref[sb, base + k]
            xr, xi = xr + (ar * rr - ai * ri), xi + (ar * ri + ai * rr)
        pr = tre_ref[sb, base + 3]
        pi = tim_ref[sb, base + 3]
        xr, xi = xr + (pr * t_r - pi * t_i), xi + (pr * t_i + pi * t_r)
        out_ref[rows, :SB_STATES] = xr
        out_ref[rows, SB_STATES:] = xi
        if prev_ref is not None:
            prev_ref[rows, :SB_STATES] = jnp.where(row == 0, t_r, pltpu.roll(xr, 1, 0))
            prev_ref[rows, SB_STATES:] = jnp.where(row == 0, t_i, pltpu.roll(xi, 1, 0))
        edge = slice(0, 1) if reverse else slice(SCAN_TILE - 1, SCAN_TILE)
        t_r, t_i = xr[edge], xi[edge]
    return t_r, t_i


def _s5_fwd(proj, ssm, w_glu, b_glu3, layer):
    rows = proj.shape[0]
    n_chunks = rows // BLK
    b_mat, c_mat, t_re, t_im, d_skip = (ssm[k] for k in ("b_mat", "c_mat", "t_re", "t_im", "d_skip"))

    def body(u_ref, bm_ref, cm_ref, tre_ref, tim_ref, d_ref, wg_ref, bg_ref,
             y_ref, ys_ref, cin_ref, carry, bu_scr, s_scr):
        @pl.when(pl.program_id(0) == 0)
        def _():
            carry[...] = jnp.zeros_like(carry)

        cin_ref[...] = carry[...]
        u = u_ref[...]
        for sb in range(N_SB):
            cols = slice(sb * 128, (sb + 1) * 128)
            u_sb = u[:, cols]
            bu_scr[sb] = _dot(u_sb.astype(MXU_DTYPE), bm_ref[sb])
            t_r, t_i = _scan_tiles(bu_scr.at[sb], s_scr.at[sb], tre_ref, tim_ref, sb,
                                   carry[2 * sb:2 * sb + 1, :], carry[2 * sb + 1:2 * sb + 2, :], False)
            carry[2 * sb:2 * sb + 1, :] = t_r
            carry[2 * sb + 1:2 * sb + 2, :] = t_i
            y_ref[:, cols] = _dot(s_scr[sb].astype(MXU_DTYPE), cm_ref[sb]) + d_ref[:, cols] * u_sb
        z, _ = _gelu_parts(y_ref[...])
        gl = _dot(z.astype(MXU_DTYPE), wg_ref[...]) + bg_ref[...]
        ys_ref[...] = (z * _sigmoid(gl)).astype(MXU_DTYPE)

    full = lambda shape: pl.BlockSpec(shape, lambda j: (0,) * len(shape))
    return _pcall(
        body, name=f"s5_fwd_l{layer}", grid=(n_chunks,),
        in_specs=[pl.BlockSpec((BLK, D_SSM), lambda j: (j, 0)),
                  full((N_SB, 128, 2 * SB_STATES)), full((N_SB, 2 * SB_STATES, 128)),
                  full((N_SB, 8, SCAN_TILE, SB_STATES)), full((N_SB, 8, SCAN_TILE, SB_STATES)),
                  full((1, D_SSM)), full((D_SSM, D_SSM)),
                  pl.BlockSpec((None, 1, D_SSM), lambda j: (layer, 0, 0))],
        out_specs=[pl.BlockSpec((BLK, D_SSM), lambda j: (j, 0)), pl.BlockSpec((BLK, D_SSM), lambda j: (j, 0)),
                   pl.BlockSpec((None, 8, SB_STATES), lambda j: (j, 0, 0))],
        out_shape=[SDS((rows, D_SSM), F32), SDS((rows, D_SSM), MXU_DTYPE), SDS((n_chunks, 8, SB_STATES), F32)],
        scratch_shapes=[pltpu.VMEM((8, SB_STATES), F32), pltpu.VMEM((N_SB, BLK, 2 * SB_STATES), F32),
                        pltpu.VMEM((N_SB, BLK, 2 * SB_STATES), F32)],
        compiler_params=_cparams("arbitrary"),
    )(proj, b_mat, c_mat, t_re, t_im, d_skip, w_glu, b_glu3)


def _attn_mask(i):
    row = lax.broadcasted_iota(jnp.int32, (BLK, 3 * BLK), 0) + i * BLK
    col = lax.broadcasted_iota(jnp.int32, (BLK, 3 * BLK), 1)
    seg = jnp.right_shift(col, 7)
    c = jnp.bitwise_and(col, BLK - 1)
    kidx = c + (i + seg - 2) * BLK
    ok_meta = (seg == 0) & (c >= PAD_ROWS) & (row - c >= BLK)
    ok_win = (seg > 0) & (kidx >= PAD_ROWS) & (kidx <= row) & (row - kidx < BLK)
    return jnp.where(ok_meta | ok_win, 0.0, NEG_INF)


def _head_lanes(h):
    return slice(h * HEAD_DIM, (h + 1) * HEAD_DIM)


def _group_rows(ref, kvh):
    return jnp.concatenate([ref[:, _head_lanes(kvh * Q_PER_KV + g)] for g in range(Q_PER_KV)], axis=0)


def _group_bias(bias, sink_ref, layer, kvh):
    first_col = lax.broadcasted_iota(jnp.int32, (BLK, BLK), 1) == 0
    slabs = []
    for g in range(Q_PER_KV):
        first = jnp.where(first_col, sink_ref[layer, kvh * Q_PER_KV + g], bias[:, :BLK])
        slabs.append(jnp.concatenate([first, bias[:, BLK:]], axis=1))
    return jnp.concatenate(slabs, axis=0)


def _attn_probs(q4, k3, bias4):
    s = _dot_nt(q4, k3) + bias4
    e = jnp.exp(s - jnp.max(s, axis=-1, keepdims=True))
    return e * (1.0 / jnp.sum(e, axis=-1, keepdims=True))


def _attn_fwd(q, k, v, sinks, layer):
    rows = q.shape[0]
    n_blk = rows // BLK

    def body(sink_ref, q_ref, km_ref, kp_ref, kc_ref, vm_ref, vp_ref, vc_ref, o_ref):
        bias = _attn_mask(pl.program_id(0))
        for kvh in range(N_KV_HEADS):
            lanes = _head_lanes(kvh)
            k3 = jnp.concatenate([km_ref[:, lanes], kp_ref[:, lanes], kc_ref[:, lanes]], axis=0)
            v3 = jnp.concatenate([vm_ref[:, lanes], vp_ref[:, lanes], vc_ref[:, lanes]], axis=0)
            p = _attn_probs(_group_rows(q_ref, kvh), k3, _group_bias(bias, sink_ref, layer, kvh))
            o4 = _dot(p.astype(MXU_DTYPE), v3).astype(MXU_DTYPE)
            for g in range(Q_PER_KV):
                o_ref[:, _head_lanes(kvh * Q_PER_KV + g)] = o4[g * BLK:(g + 1) * BLK]

    kv_meta = pl.BlockSpec((BLK, D_KV), lambda i: (0, 0))
    kv_prev = pl.BlockSpec((BLK, D_KV), lambda i: (jnp.maximum(i - 1, 0), 0))
    kv_cur = pl.BlockSpec((BLK, D_KV), lambda i: (i, 0))
    return _pcall(
        body, name=f"attn_fwd_l{layer}", grid=(n_blk,),
        in_specs=[pl.BlockSpec(memory_space=pltpu.SMEM),
                  pl.BlockSpec((BLK, D_ATTN), lambda i: (i, 0)),
                  kv_meta, kv_prev, kv_cur, kv_meta, kv_prev, kv_cur],
        out_specs=pl.BlockSpec((BLK, D_ATTN), lambda i: (i, 0)),
        out_shape=SDS((rows, D_ATTN), MXU_DTYPE),
        compiler_params=_cparams("parallel"),
    )(sinks, q, k, k, k, v, v, v)


def _merge_fwd(y_ssm, y_attn, proj, hres, w_o_ssm, w_o_attn, w_out, gain3, layer):
    rows = hres.shape[0]
    tm = _row_tile(rows, 320)

    def body(ys_ref, ya_ref, gs_ref, ga_ref, x_ref, wos_ref, woa_ref, wout_ref, g_ref,
             mg_ref, mix_ref, out_ref):
        a1 = _dot(ys_ref[...], wos_ref[...])
        a2 = _dot(ya_ref[...], woa_ref[...])
        merged = (_sigmoid(gs_ref[...]) * a1 + _sigmoid(ga_ref[...]) * a2).astype(MXU_DTYPE)
        mg_ref[...] = merged
        mix = _dot(merged, wout_ref[...])
        mix_ref[...] = mix
        out_ref[...] = x_ref[...] + _rms_fwd(mix, g_ref[...])

    row_d = pl.BlockSpec((tm, D), lambda i: (i, 0))
    full = lambda shape: pl.BlockSpec(shape, lambda i: (0,) * len(shape))
    return _pcall(
        body, name=f"merge_fwd_l{layer}", grid=(rows // tm,),
        in_specs=[pl.BlockSpec((tm, D_SSM), lambda i: (i, 0)), row_d,
                  pl.BlockSpec((tm, D), lambda i: (i, 2)), pl.BlockSpec((tm, D), lambda i: (i, 3)), row_d,
                  full((D_SSM, D)), full((D_ATTN, D)), full((D, D)),
                  pl.BlockSpec((None, 1, D), lambda i: (layer, 0, 0))],
        out_specs=[row_d, row_d, row_d],
        out_shape=[SDS((rows, D), MXU_DTYPE), SDS((rows, D), F32), SDS((rows, D), F32)],
        compiler_params=_cparams("parallel"),
    )(y_ssm, y_attn, proj, proj, hres, w_o_ssm, w_o_attn, w_out, gain3)


def _mlp_fwd(hres, gain_pre3, gain_post3, w_up_g, w_down_g, layer):
    rows = hres.shape[0]
    tm = _row_tile(rows)

    def body(x_ref, gp_ref, gq_ref, wu_hbm, wd_hbm, up_ref, h_ref, ff_ref, out_ref,
             h_scr, acc, wu_scr, wd_scr, wu_sem, wd_sem):
        kf = pl.program_id(1)
        first = (pl.program_id(0) == 0) & (kf == 0)
        _load_resident(wu_hbm, wu_scr, wu_sem, first)
        _load_resident(wd_hbm, wd_scr, wd_sem, first)

        @pl.when(kf == 0)
        def _():
            hn = _rms_fwd(x_ref[...], gp_ref[...]).astype(MXU_DTYPE)
            h_scr[...] = hn
            h_ref[...] = hn
            acc[...] = jnp.zeros_like(acc)

        up = _dot(h_scr[...], wu_scr[kf])
        up_ref[...] = up.astype(MXU_DTYPE)
        r = jnp.maximum(up, 0.0)
        acc[...] += _dot((r * r).astype(MXU_DTYPE), wd_scr[kf])

        @pl.when(kf == N_DEV - 1)
        def _():
            ff = acc[...]
            ff_ref[...] = ff
            out_ref[...] = x_ref[...] + _rms_fwd(ff, gq_ref[...])

    row_d = pl.BlockSpec((tm, D), lambda i, k: (i, 0))
    gain = pl.BlockSpec((None, 1, D), lambda i, k: (layer, 0, 0))
    return _pcall(
        body, name=f"mlp_fwd_l{layer}", grid=(rows // tm, N_DEV),
        in_specs=[row_d, gain, gain, pl.BlockSpec(memory_space=pl.ANY), pl.BlockSpec(memory_space=pl.ANY)],
        out_specs=[pl.BlockSpec((tm, COL_SHARD), lambda i, k: (i, k)), row_d, row_d, row_d],
        out_shape=[SDS((rows, D_FF), MXU_DTYPE), SDS((rows, D), MXU_DTYPE), SDS((rows, D), F32), SDS((rows, D), F32)],
        scratch_shapes=[pltpu.VMEM((tm, D), MXU_DTYPE), pltpu.VMEM((tm, D), F32),
                        pltpu.VMEM((N_DEV, D, COL_SHARD), MXU_DTYPE), pltpu.VMEM((N_DEV, COL_SHARD, D), MXU_DTYPE),
                        pltpu.SemaphoreType.DMA((N_DEV,)), pltpu.SemaphoreType.DMA((N_DEV,))],
        compiler_params=_cparams("arbitrary", "arbitrary"),
    )(hres, gain_pre3, gain_post3, w_up_g, w_down_g)


def _loss_and_grad(hres, target):
    rows = hres.shape[0]
    n_blk = rows // BLK

    def body(y_ref, t_ref, dy_ref, loss_ref):
        i = pl.program_id(0)

        @pl.when(i == 0)
        def _():
            dy_ref[...] = jnp.zeros_like(dy_ref)
            loss_ref[...] = jnp.zeros_like(loss_ref)

        @pl.when(i > 0)
        def _():
            err = y_ref[...] - t_ref[...]
            dy_ref[...] = err * (1.0 / D)
            loss_ref[...] += jnp.sum(err * err) * (0.5 / D)

    return _pcall(
        body, name="loss", grid=(n_blk,),
        in_specs=[pl.BlockSpec((BLK, D), lambda i: (i, 0)),
                  pl.BlockSpec((BLK, D), lambda i: (jnp.maximum(i - 1, 0), 0))],
        out_specs=[pl.BlockSpec((BLK, D), lambda i: (i, 0)), pl.BlockSpec((1, 128), lambda i: (0, 0))],
        out_shape=[SDS((rows, D), F32), SDS((1, 128), F32)],
        compiler_params=_cparams("arbitrary"),
    )(hres, target)


def _relu_squared(up):
    r = jnp.maximum(up.astype(F32), 0.0)
    return (r * r).astype(MXU_DTYPE)


def _matmul_tn(a, b, name, dev_major_cols=None, a_fn=None):
    rows, ka = a.shape
    n = b.shape[1]
    ta = min(ka, 1024)
    tn = 1024 if n % 1024 == 0 else 512
    tr = _row_tile(rows)
    n_r = rows // tr

    def body(a_ref, b_ref, o_ref, acc):
        r = pl.program_id(2)

        @pl.when(r == 0)
        def _():
            acc[...] = jnp.zeros_like(acc)

        a_blk = a_ref[...] if a_fn is None else a_fn(a_ref[...])
        acc[...] += _dot_tn(a_blk, b_ref[...])

        @pl.when(r == n_r - 1)
        def _():
            if dev_major_cols is None:
                o_ref[...] = acc[...].astype(XFER_DTYPE)
            else:
                for s in range(tn // dev_major_cols):
                    o_ref[s] = acc[:, s * dev_major_cols:(s + 1) * dev_major_cols].astype(XFER_DTYPE)

    if dev_major_cols is None:
        out_spec = pl.BlockSpec((ta, tn), lambda i, j, r: (i, j))
        out_shape = SDS((ka, n), XFER_DTYPE)
    else:
        w = dev_major_cols
        out_spec = pl.BlockSpec((tn // w, ta, w), lambda i, j, r: (j, i, 0))
        out_shape = SDS((n // w, ka, w), XFER_DTYPE)
    return _pcall(
        body, name=name, grid=(ka // ta, n // tn, n_r),
        in_specs=[pl.BlockSpec((tr, ta), lambda i, j, r: (r, i)), pl.BlockSpec((tr, tn), lambda i, j, r: (r, j))],
        out_specs=out_spec, out_shape=out_shape,
        scratch_shapes=[pltpu.VMEM((ta, tn), F32)],
        compiler_params=_cparams("parallel", "parallel", "arbitrary"),
    )(a, b)


def _mlp_bwd(dout, ff, up, hres_mid, gain_pre3, gain_post3, w_up_g, w_down_g, layer):
    rows = dout.shape[0]
    tm = _row_tile(rows)

    def body(do_ref, ff_ref, up_ref, x_ref, gp_ref, gq_ref, wu_hbm, wd_hbm,
             dff_ref, dup_ref, dx_ref, dgq_ref, dgp_ref, dff_scr, acc, wu_scr, wd_scr, wu_sem, wd_sem):
        i = pl.program_id(0)
        kf = pl.program_id(1)
        _load_resident(wu_hbm, wu_scr, wu_sem, (i == 0) & (kf == 0))
        _load_resident(wd_hbm, wd_scr, wd_sem, (i == 0) & (kf == 0))

        @pl.when((i == 0) & (kf == 0))
        def _():
            dgq_ref[...] = jnp.zeros_like(dgq_ref)
            dgp_ref[...] = jnp.zeros_like(dgp_ref)

        @pl.when(kf == 0)
        def _():
            dff, dg = _rms_bwd(ff_ref[...], gq_ref[...], do_ref[...])
            dgq_ref[...] += dg
            dffb = dff.astype(MXU_DTYPE)
            dff_scr[...] = dffb
            dff_ref[...] = dffb
            acc[...] = jnp.zeros_like(acc)

        dact = _dot_nt(dff_scr[...], wd_scr[kf])
        dup = (dact * (2.0 * jnp.maximum(up_ref[...].astype(F32), 0.0))).astype(MXU_DTYPE)
        dup_ref[...] = dup
        acc[...] += _dot_nt(dup, wu_scr[kf])

        @pl.when(kf == N_DEV - 1)
        def _():
            dx, dg = _rms_bwd(x_ref[...], gp_ref[...], acc[...])
            dgp_ref[...] += dg
            dx_ref[...] = do_ref[...] + dx

    row_d = pl.BlockSpec((tm, D), lambda i, k: (i, 0))
    gain = pl.BlockSpec((None, 1, D), lambda i, k: (layer, 0, 0))
    dgain = pl.BlockSpec((1, D), lambda i, k: (0, 0))
    return _pcall(
        body, name=f"mlp_bwd_l{layer}", grid=(rows // tm, N_DEV),
        in_specs=[row_d, row_d, pl.BlockSpec((tm, COL_SHARD), lambda i, k: (i, k)), row_d, gain, gain,
                  pl.BlockSpec(memory_space=pl.ANY), pl.BlockSpec(memory_space=pl.ANY)],
        out_specs=[row_d, pl.BlockSpec((tm, COL_SHARD), lambda i, k: (i, k)), row_d, dgain, dgain],
        out_shape=[SDS((rows, D), MXU_DTYPE), SDS((rows, D_FF), MXU_DTYPE), SDS((rows, D), F32),
                   SDS((1, D), F32), SDS((1, D), F32)],
        scratch_shapes=[pltpu.VMEM((tm, D), MXU_DTYPE), pltpu.VMEM((tm, D), F32),
                        pltpu.VMEM((N_DEV, D, COL_SHARD), MXU_DTYPE), pltpu.VMEM((N_DEV, COL_SHARD, D), MXU_DTYPE),
                        pltpu.SemaphoreType.DMA((N_DEV,)), pltpu.SemaphoreType.DMA((N_DEV,))],
        compiler_params=_cparams("arbitrary", "arbitrary"),
    )(dout, ff, up, hres_mid, gain_pre3, gain_post3, w_up_g, w_down_g)


def _merge_bwd(dhm, mix, y_ssm, y_attn, proj, w_o_ssm, w_o_attn, w_out, gain3, layer):
    rows = dhm.shape[0]
    tm = _row_tile(rows, 320)

    def body(dh_ref, mix_ref, ys_ref, ya_ref, gs_ref, ga_ref, wos_ref, woa_ref, wout_ref, g_ref,
             dmix_ref, da1_ref, da2_ref, dgs_ref, dga_ref, dys_ref, dya_ref, dg_ref):
        @pl.when(pl.program_id(0) == 0)
        def _():
            dg_ref[...] = jnp.zeros_like(dg_ref)

        dmix, dg = _rms_bwd(mix_ref[...], g_ref[...], dh_ref[...])
        dg_ref[...] += dg
        dmixb = dmix.astype(MXU_DTYPE)
        dmix_ref[...] = dmixb
        dmerged = _dot_nt(dmixb, wout_ref[...])
        sg_s = _sigmoid(gs_ref[...])
        sg_a = _sigmoid(ga_ref[...])
        da1 = (dmerged * sg_s).astype(MXU_DTYPE)
        da2 = (dmerged * sg_a).astype(MXU_DTYPE)
        da1_ref[...] = da1
        da2_ref[...] = da2
        a1 = _dot(ys_ref[...], wos_ref[...])
        a2 = _dot(ya_ref[...], woa_ref[...])
        dgs_ref[...] = (dmerged * a1 * (sg_s * (1.0 - sg_s))).astype(MXU_DTYPE)
        dga_ref[...] = (dmerged * a2 * (sg_a * (1.0 - sg_a))).astype(MXU_DTYPE)
        dys_ref[...] = _dot_nt(da1, wos_ref[...])
        dya_ref[...] = _dot_nt(da2, woa_ref[...])

    row_d = pl.BlockSpec((tm, D), lambda i: (i, 0))
    full = lambda shape: pl.BlockSpec(shape, lambda i: (0,) * len(shape))
    return _pcall(
        body, name=f"merge_bwd_l{layer}", grid=(rows // tm,),
        in_specs=[row_d, row_d, pl.BlockSpec((tm, D_SSM), lambda i: (i, 0)), row_d,
                  pl.BlockSpec((tm, D), lambda i: (i, 2)), pl.BlockSpec((tm, D), lambda i: (i, 3)),
                  full((D_SSM, D)), full((D_ATTN, D)), full((D, D)),
                  pl.BlockSpec((None, 1, D), lambda i: (layer, 0, 0))],
        out_specs=[row_d, row_d, row_d, row_d, row_d, pl.BlockSpec((tm, D_SSM), lambda i: (i, 0)), row_d,
                   pl.BlockSpec((1, D), lambda i: (0, 0))],
        out_shape=[SDS((rows, D), MXU_DTYPE)] * 5 + [SDS((rows, D_SSM), F32), SDS((rows, D_ATTN), F32),
                                                      SDS((1, D), F32)],
        compiler_params=_cparams("arbitrary"),
    )(dhm, mix, y_ssm, y_attn, proj, proj, w_o_ssm, w_o_attn, w_out, gain3)


def _attn_bwd(q, k, v, d_out, sinks, layer):
    rows = q.shape[0]
    n_blk = rows // BLK
    last = n_blk - 1

    def body(sink_ref, q_ref, km_ref, kp_ref, kc_ref, vm_ref, vp_ref, vc_ref, do_ref,
             dq_ref, dk_ref, dv_ref, dkm_ref, dvm_ref, ds_ref, dk_carry, dv_carry):
        i = pl.program_id(0)

        @pl.when(i == 0)
        def _():
            dkm_ref[...] = jnp.zeros_like(dkm_ref)
            dvm_ref[...] = jnp.zeros_like(dvm_ref)
            ds_ref[...] = jnp.zeros_like(ds_ref)
            dk_carry[...] = jnp.zeros_like(dk_carry)
            dv_carry[...] = jnp.zeros_like(dv_carry)

        @pl.when(i <= last)
        def _():
            bias = _attn_mask(i)
            for kvh in range(N_KV_HEADS):
                lanes = _head_lanes(kvh)
                k3 = jnp.concatenate([km_ref[:, lanes], kp_ref[:, lanes], kc_ref[:, lanes]], axis=0)
                v3 = jnp.concatenate([vm_ref[:, lanes], vp_ref[:, lanes], vc_ref[:, lanes]], axis=0)
                q4 = _group_rows(q_ref, kvh)
                do4 = _group_rows(do_ref, kvh).astype(MXU_DTYPE)
                p = _attn_probs(q4, k3, _group_bias(bias, sink_ref, layer, kvh))
                dp = _dot_nt(do4, v3)
                dsf = p * (dp - jnp.sum(dp * p, axis=-1, keepdims=True))
                dsc = dsf.astype(MXU_DTYPE)
                dv3 = _dot_tn(p.astype(MXU_DTYPE), do4)
                dk3 = _dot_tn(dsc, q4)
                dq4 = _dot(dsc, k3)
                for g in range(Q_PER_KV):
                    h = kvh * Q_PER_KV + g
                    dq_ref[:, _head_lanes(h)] = dq4[g * BLK:(g + 1) * BLK]
                    ds_ref[h:h + 1, :] += jnp.sum(dsf[g * BLK:(g + 1) * BLK, 0:BLK], axis=0, keepdims=True)
                dkm_ref[:, lanes] += dk3[0:BLK]
                dvm_ref[:, lanes] += dv3[0:BLK]
                dk_ref[:, lanes] = dk_carry[:, lanes] + dk3[BLK:2 * BLK]
                dv_ref[:, lanes] = dv_carry[:, lanes] + dv3[BLK:2 * BLK]
                dk_carry[:, lanes] = dk3[2 * BLK:3 * BLK]
                dv_carry[:, lanes] = dv3[2 * BLK:3 * BLK]

        @pl.when(i == last + 1)
        def _():
            dk_ref[...] = dk_carry[...]
            dv_ref[...] = dv_carry[...]

    cur = lambda i: (jnp.minimum(i, last), 0)
    prev = lambda i: (jnp.clip(i - 1, 0, last), 0)
    kv_meta = pl.BlockSpec((BLK, D_KV), lambda i: (0, 0))
    kv_prev = pl.BlockSpec((BLK, D_KV), prev)
    kv_cur = pl.BlockSpec((BLK, D_KV), cur)
    return _pcall(
        body, name=f"attn_bwd_l{layer}", grid=(n_blk + 1,),
        in_specs=[pl.BlockSpec(memory_space=pltpu.SMEM),
                  pl.BlockSpec((BLK, D_ATTN), cur),
                  kv_meta, kv_prev, kv_cur, kv_meta, kv_prev, kv_cur,
                  pl.BlockSpec((BLK, D_ATTN), cur)],
        out_specs=[pl.BlockSpec((BLK, D_ATTN), cur), kv_prev, kv_prev, kv_meta, kv_meta,
                   pl.BlockSpec((N_Q_HEADS, 128), lambda i: (0, 0))],
        out_shape=[SDS((rows, D_ATTN), F32), SDS((rows, D_KV), F32), SDS((rows, D_KV), F32),
                   SDS((BLK, D_KV), F32), SDS((BLK, D_KV), F32), SDS((N_Q_HEADS, 128), F32)],
        scratch_shapes=[pltpu.VMEM((BLK, D_KV), F32), pltpu.VMEM((BLK, D_KV), F32)],
        compiler_params=_cparams("arbitrary"),
    )(sinks, q, k, k, k, v, v, v, d_out)


def _rope_bwd(dq, dk, dv, dk_meta, dv_meta, cos, sin_a, sin_b, layer):
    rows = dq.shape[0]
    tm = _row_tile(rows)

    def body(dq_ref, dk_ref, dv_ref, dkm_ref, dvm_ref, c_ref, a_ref, b_ref, o_ref):
        c, a, b = c_ref[...], -a_ref[...], -b_ref[...]
        for t in range(8):
            x = dq_ref[:, t * 128:(t + 1) * 128]
            o_ref[:, t * 128:(t + 1) * 128] = (_rope_lanes(x, c, a, b) * ATTN_SCALE).astype(MXU_DTYPE)
        for t in range(2):
            x = dk_ref[:, t * 128:(t + 1) * 128]
            o_ref[:, D_ATTN + t * 128:D_ATTN + (t + 1) * 128] = _rope_lanes(x, c, a, b).astype(MXU_DTYPE)
        o_ref[:, D_ATTN + D_KV:] = dv_ref[...].astype(MXU_DTYPE)

        @pl.when(pl.program_id(0) == 0)
        def _():
            cb, ab, bb = c[0:BLK], a[0:BLK], b[0:BLK]
            is_meta = lax.broadcasted_iota(jnp.int32, (BLK, 128), 0) >= PAD_ROWS
            for t in range(2):
                x = dk_ref[0:BLK, t * 128:(t + 1) * 128] + jnp.where(is_meta, dkm_ref[:, t * 128:(t + 1) * 128], 0.0)
                o_ref[0:BLK, D_ATTN + t * 128:D_ATTN + (t + 1) * 128] = _rope_lanes(x, cb, ab, bb).astype(MXU_DTYPE)
                xv = dv_ref[0:BLK, t * 128:(t + 1) * 128] + jnp.where(is_meta, dvm_ref[:, t * 128:(t + 1) * 128], 0.0)
                o_ref[0:BLK, D_ATTN + D_KV + t * 128:D_ATTN + D_KV + (t + 1) * 128] = xv.astype(MXU_DTYPE)

    tab = pl.BlockSpec((tm, 128), lambda i: (i, 0))
    kv = pl.BlockSpec((tm, D_KV), lambda i: (i, 0))
    meta = pl.BlockSpec((BLK, D_KV), lambda i: (0, 0))
    return _pcall(
        body, name=f"rope_bwd_l{layer}", grid=(rows // tm,),
        in_specs=[pl.BlockSpec((tm, D_ATTN), lambda i: (i, 0)), kv, kv, meta, meta, tab, tab, tab],
        out_specs=pl.BlockSpec((tm, D_ATTN + 2 * D_KV), lambda i: (i, 0)),
        out_shape=SDS((rows, D_ATTN + 2 * D_KV), MXU_DTYPE),
        compiler_params=_cparams("parallel"),
    )(dq, dk, dv, dk_meta, dv_meta, cos, sin_a, sin_b)


def _s5_bwd(d_gated, y, proj, carry_in, ssm, w_glu, b_glu3, layer):
    rows = y.shape[0]
    n_chunks = rows // BLK
    b_mat, c_mat, t_re, t_im, d_skip = (ssm[k] for k in ("b_mat", "c_mat", "t_re", "t_im", "d_skip"))

    def body(dz_ref, y_ref, u_ref, cin_ref, bm_ref, cm_ref, tre_ref, tim_ref, d_ref, wg_ref, bg_ref,
             du_ref, dwg_ref, dbg_ref, dd_ref, dbm_ref, dcm_ref, dab_ref,
             lam_carry, bu_scr, s_scr, sp_scr, g_scr, lam_scr):
        step = pl.program_id(0)
        chunk = n_chunks - 1 - step

        @pl.when(step == 0)
        def _():
            for r in (dwg_ref, dbg_ref, dd_ref, dbm_ref, dcm_ref, dab_ref, lam_carry):
                r[...] = jnp.zeros_like(r)

        y = y_ref[...]
        u = u_ref[...]
        d_o = dz_ref[...]
        z, t = _gelu_parts(y)
        zb = z.astype(MXU_DTYPE)
        sg = _sigmoid(_dot(zb, wg_ref[...]) + bg_ref[...])
        dgl = d_o * z * (sg * (1.0 - sg))
        dglb = dgl.astype(MXU_DTYPE)
        dz = d_o * sg + _dot_nt(dglb, wg_ref[...])
        dwg_ref[...] += _dot_tn(zb, dglb)
        dbg_ref[...] += jnp.sum(dgl, axis=0, keepdims=True)
        dy = dz * _gelu_grad(y, t)
        dd_ref[...] += jnp.sum(dy * u, axis=0, keepdims=True)
        grow = lax.broadcasted_iota(jnp.int32, (BLK, 128), 0) + chunk * BLK
        for sb in range(N_SB):
            cols = slice(sb * 128, (sb + 1) * 128)
            u_sb = u[:, cols].astype(MXU_DTYPE)
            dy_sb = dy[:, cols]
            dyb = dy_sb.astype(MXU_DTYPE)
            bu_scr[sb] = _dot(u_sb, bm_ref[sb])
            _scan_tiles(bu_scr.at[sb], s_scr.at[sb], tre_ref, tim_ref, sb,
                        cin_ref[2 * sb:2 * sb + 1, :], cin_ref[2 * sb + 1:2 * sb + 2, :], False, prev_ref=sp_scr.at[sb])
            dcm_ref[sb] += _dot_tn(s_scr[sb].astype(MXU_DTYPE), dyb)
            g_scr[sb] = _dot_nt(dyb, cm_ref[sb])
            n_r, n_i = _scan_tiles(g_scr.at[sb], lam_scr.at[sb], tre_ref, tim_ref, sb,
                                   lam_carry[2 * sb:2 * sb + 1, :], lam_carry[2 * sb + 1:2 * sb + 2, :], True)
            lam_carry[2 * sb:2 * sb + 1, :] = n_r
            lam_carry[2 * sb + 1:2 * sb + 2, :] = n_i
            lr, li = lam_scr[sb, :, :SB_STATES], lam_scr[sb, :, SB_STATES:]
            spr, spi = sp_scr[sb, :, :SB_STATES], sp_scr[sb, :, SB_STATES:]
            dab_ref[2 * sb:2 * sb + 1, :] += jnp.sum(spr * lr + spi * li, axis=0, keepdims=True)
            dab_ref[2 * sb + 1:2 * sb + 2, :] += jnp.sum(spr * li - spi * lr, axis=0, keepdims=True)
            lam = lam_scr[sb].astype(MXU_DTYPE)
            dbm_ref[sb] += _dot_tn(u_sb, lam)
            du = _dot_nt(lam, bm_ref[sb]) + d_ref[:, cols] * dy_sb
            du_ref[:, cols] = jnp.where(grow >= PAD_ROWS, du, 0.0).astype(MXU_DTYPE)

    rev = lambda j: (n_chunks - 1 - j, 0)
    full = lambda shape: pl.BlockSpec(shape, lambda j: (0,) * len(shape))
    tables = [full((N_SB, 8, SCAN_TILE, SB_STATES))] * 2
    chunk_scratch = pltpu.VMEM((N_SB, BLK, 2 * SB_STATES), F32)
    return _pcall(
        body, name=f"s5_bwd_l{layer}", grid=(n_chunks,),
        in_specs=[pl.BlockSpec((BLK, D_SSM), rev), pl.BlockSpec((BLK, D_SSM), rev), pl.BlockSpec((BLK, D_SSM), rev),
                  pl.BlockSpec((None, 8, SB_STATES), lambda j: (n_chunks - 1 - j, 0, 0)),
                  full((N_SB, 128, 2 * SB_STATES)), full((N_SB, 2 * SB_STATES, 128))] + tables + [
                  full((1, D_SSM)), full((D_SSM, D_SSM)),
                  pl.BlockSpec((None, 1, D_SSM), lambda j: (layer, 0, 0))],
        out_specs=[pl.BlockSpec((BLK, D_SSM), rev), full((D_SSM, D_SSM)), full((1, D_SSM)), full((1, D_SSM)),
                   full((N_SB, 128, 2 * SB_STATES)), full((N_SB, 2 * SB_STATES, 128)), full((8, SB_STATES))],
        out_shape=[SDS((rows, D_SSM), MXU_DTYPE), SDS((D_SSM, D_SSM), F32), SDS((1, D_SSM), F32), SDS((1, D_SSM), F32),
                   SDS((N_SB, 128, 2 * SB_STATES), F32), SDS((N_SB, 2 * SB_STATES, 128), F32), SDS((8, SB_STATES), F32)],
        scratch_shapes=[pltpu.VMEM((8, SB_STATES), F32)] + [chunk_scratch] * 5,
        compiler_params=_cparams("arbitrary"),
    )(d_gated, y, proj, carry_in, b_mat, c_mat, t_re, t_im, d_skip, w_glu, b_glu3)


DPROJ_PIECES = ((0, 1), (1, 3), (4, 2), (6, 2))


def _in_bwd(dproj_pieces, dhm, hres, gain3, w_in_g, layer):
    rows = hres.shape[0]
    tm = _row_tile(rows)

    def body(du_ref, dqkv_ref, dgs_ref, dga_ref, dh_ref, x_ref, g_ref, w_hbm, dx_ref, dg_ref, acc, w_scr, w_sem):
        i = pl.program_id(0)
        j = pl.program_id(1)
        _load_resident(w_hbm, w_scr, w_sem, (i == 0) & (j == 0))

        @pl.when((i == 0) & (j == 0))
        def _():
            dg_ref[...] = jnp.zeros_like(dg_ref)

        @pl.when(j == 0)
        def _():
            acc[...] = jnp.zeros_like(acc)

        for piece_ref, (first, count) in zip((du_ref, dqkv_ref, dgs_ref, dga_ref), DPROJ_PIECES):
            @pl.when((j >= first) & (j < first + count))
            def _():
                acc[...] += _dot_nt(piece_ref[...], w_scr[j])

        @pl.when(j == N_DEV - 1)
        def _():
            dx, dg = _rms_bwd(x_ref[...], g_ref[...], acc[...])
            dg_ref[...] += dg
            dx_ref[...] = dh_ref[...] + dx

    row_d = pl.BlockSpec((tm, D), lambda i, j: (i, 0))

    def piece_spec(first, count):
        return pl.BlockSpec((tm, COL_SHARD), lambda i, j: (i, jnp.clip(j - first, 0, count - 1)))

    return _pcall(
        body, name=f"in_bwd_l{layer}", grid=(rows // tm, N_DEV),
        in_specs=[piece_spec(*p) for p in DPROJ_PIECES] + [
                  row_d, row_d,
                  pl.BlockSpec((None, 1, D), lambda i, j: (layer, 0, 0)),
                  pl.BlockSpec(memory_space=pl.ANY)],
        out_specs=[row_d, pl.BlockSpec((1, D), lambda i, j: (0, 0))],
        out_shape=[SDS((rows, D), F32), SDS((1, D), F32)],
        scratch_shapes=[pltpu.VMEM((tm, D), F32), pltpu.VMEM((N_DEV, D, COL_SHARD), MXU_DTYPE),
                        pltpu.SemaphoreType.DMA((N_DEV,))],
        compiler_params=_cparams("arbitrary", "arbitrary"),
    )(*dproj_pieces, dhm, hres, gain3, w_in_g)


_ADAM_C1 = 1.0 / (1.0 - ADAM_B1 ** ADAM_STEP)
_ADAM_C2 = 1.0 / (1.0 - ADAM_B2 ** ADAM_STEP)


def _adam_math(w, g, m, v):
    m = ADAM_B1 * m + (1.0 - ADAM_B1) * g
    v = ADAM_B2 * v + (1.0 - ADAM_B2) * (g * g)
    delta = -ADAM_LR * ((m * _ADAM_C1) / (jnp.sqrt(v * _ADAM_C2) + ADAM_EPS) + ADAM_WD * w)
    return delta, m, v


def _adamw_layers(parts0, parts1, w, m, v, name):
    _, rows, cols = w.shape
    tr = min(rows, (1 << 16) // cols)
    nt = rows // tr

    def body(p0_ref, p1_ref, w_ref, m_ref, v_ref, g_ref, d_ref, nm_ref, nv_ref):
        layer = pl.program_id(0)

        def run(p_ref):
            g = p_ref[0].astype(F32)
            for s in range(1, N_DEV):
                g = g + p_ref[s].astype(F32)
            delta, nm, nv = _adam_math(w_ref[...], g, m_ref[...], v_ref[...])
            g_ref[...] = g
            d_ref[...] = delta
            nm_ref[...] = nm
            nv_ref[...] = nv

        @pl.when(layer == 0)
        def _():
            run(p0_ref)

        @pl.when(layer == 1)
        def _():
            run(p1_ref)

    wspec = pl.BlockSpec((None, tr, cols), lambda l, i: (l, i, 0))
    return _pcall(
        body, name=name, grid=(2, nt),
        in_specs=[pl.BlockSpec((N_DEV, tr, cols), lambda l, i: (0, jnp.where(l == 0, i, nt - 1), 0)),
                  pl.BlockSpec((N_DEV, tr, cols), lambda l, i: (0, jnp.where(l == 1, i, 0), 0)),
                  wspec, wspec, wspec],
        out_specs=[wspec] * 4, out_shape=[SDS(w.shape, F32)] * 4,
        compiler_params=_cparams("arbitrary", "arbitrary"),
    )(parts0, parts1, w, m, v)


def _adamw_packed(g, w, m, v, name):
    def body(g_ref, w_ref, m_ref, v_ref, d_ref, nm_ref, nv_ref):
        delta, nm, nv = _adam_math(w_ref[...], g_ref[...], m_ref[...], v_ref[...])
        d_ref[...] = delta
        nm_ref[...] = nm
        nv_ref[...] = nv

    vmem = pl.BlockSpec(memory_space=pltpu.VMEM)
    return _pcall(body, name=name, out_shape=[SDS(g.shape, F32)] * 3, in_specs=[vmem] * 4, out_specs=[vmem] * 3,
                  compiler_params=_cparams())(g, w, m, v)


def _ssm_discretize(a_re, a_im, log_dt, b_re, b_im):
    dt = jnp.exp(log_dt)[:, None]
    mag = jnp.exp(a_re * dt)
    ang = a_im * dt
    ab_re, ab_im = mag * jnp.cos(ang), mag * jnp.sin(ang)
    xr, xi = ab_re - 1.0, ab_im
    den = a_re * a_re + a_im * a_im
    q_re = (xr * a_re + xi * a_im) / den
    q_im = (xi * a_re - xr * a_im) / den
    bb_re = q_re[..., None] * b_re - q_im[..., None] * b_im
    bb_im = q_re[..., None] * b_im + q_im[..., None] * b_re
    return ab_re, ab_im, bb_re, bb_im


def _block_diag_b(bb):
    m = jnp.einsum("sgnc,gh->sgchn", bb.reshape(N_SB, 8, N_STATE, GROUP_CH), jnp.eye(8, dtype=F32))
    return m.reshape(N_SB, 128, SB_STATES)


def _block_diag_b_t(dm):
    return jnp.einsum("sgchn,gh->sgnc", dm.reshape(N_SB, 8, GROUP_CH, 8, N_STATE),
                      jnp.eye(8, dtype=F32)).reshape(N_GROUPS, N_STATE, GROUP_CH)


def _block_diag_c(cc):
    m = jnp.einsum("sgcn,gh->sgnhc", cc.reshape(N_SB, 8, GROUP_CH, N_STATE), jnp.eye(8, dtype=F32))
    return m.reshape(N_SB, SB_STATES, 128)


def _block_diag_c_t(dm):
    return jnp.einsum("sgnhc,gh->sgcn", dm.reshape(N_SB, 8, N_STATE, 8, GROUP_CH),
                      jnp.eye(8, dtype=F32)).reshape(N_GROUPS, GROUP_CH, N_STATE)


def _ssm_tables(ab_re, ab_im, bb_re, bb_im, c_re, c_im, d_skip):
    pr, pi = ab_re.reshape(1, -1), ab_im.reshape(1, -1)
    cr, ci = pr, pi
    squares = []
    for _ in range(3):
        squares.append((cr, ci))
        pr, pi = (jnp.concatenate([pr, pr * cr - pi * ci], axis=0),
                  jnp.concatenate([pi, pr * ci + pi * cr], axis=0))
        cr, ci = cr * cr - ci * ci, 2.0 * cr * ci
    r = jnp.arange(SCAN_TILE)[:, None]
    fwd = [(jnp.where(r >= (1 << k), squares[k][0], 0.0), jnp.where(r >= (1 << k), squares[k][1], 0.0))
           for k in range(3)] + [(pr, pi)]
    rev = [(jnp.where(r < SCAN_TILE - (1 << k), squares[k][0], 0.0),
            jnp.where(r < SCAN_TILE - (1 << k), -squares[k][1], 0.0)) for k in range(3)] + [(pr[::-1], -pi[::-1])]
    table = lambda part: jnp.stack([e[part] for e in fwd + rev]).reshape(
        8, SCAN_TILE, N_SB, SB_STATES).transpose(2, 0, 1, 3)
    return dict(
        b_mat=jnp.concatenate([_block_diag_b(bb_re), _block_diag_b(bb_im)], axis=-1).astype(MXU_DTYPE),
        c_mat=jnp.concatenate([_block_diag_c(c_re), -_block_diag_c(c_im)], axis=1).astype(MXU_DTYPE),
        t_re=table(0), t_im=table(1),
        d_skip=d_skip.reshape(1, D_SSM))


def _rope_tables(rows):
    pos = (jnp.arange(rows, dtype=jnp.int32) - PAD_ROWS).astype(F32)
    inv_freq = 1.0 / (ROPE_THETA ** (jnp.arange(0, HEAD_DIM, 2, dtype=F32) / HEAD_DIM))
    ang = pos[:, None] * inv_freq[None, :]
    ang = jnp.concatenate([ang, ang, ang, ang], axis=-1)
    first_half = (jnp.arange(128) % HEAD_DIM) < HEAD_DIM // 2
    sin = jnp.sin(ang)
    return jnp.cos(ang), jnp.where(first_half, -sin, 0.0), jnp.where(first_half, 0.0, sin)


def _pack(arrays):
    flat = jnp.concatenate([a.reshape(-1).astype(F32) for a in arrays])
    pad = (-flat.shape[0]) % 1024
    return jnp.pad(flat, (0, pad)).reshape(-1, 128)


def _unpack(packed, like):
    flat = packed.reshape(-1)
    out, off = [], 0
    for a in like:
        n = math.prod(a.shape)
        out.append(flat[off:off + n].reshape(a.shape))
        off += n
    return out


BIG = ("w_in", "w_glu", "w_o_ssm", "w_o_attn", "w_out", "w_up", "w_down")
WEIGHTS = ("meta_tokens", "norm_mix_pre", "norm_mix_post", "norm_mlp_pre", "norm_mlp_post", "w_in",
           "ssm_a_re", "ssm_a_im", "ssm_log_dt", "ssm_b_re", "ssm_b_im", "ssm_c_re", "ssm_c_im", "ssm_d",
           "w_glu", "b_glu", "attn_sinks", "w_o_ssm", "w_o_attn", "w_out", "w_up", "w_down")
SMALL = tuple(n for n in WEIGHTS if n not in BIG)


def kernel(x, meta_tokens, norm_mix_pre, norm_mix_post, norm_mlp_pre, norm_mlp_post, w_in, ssm_a_re, ssm_a_im, ssm_log_dt, ssm_b_re, ssm_b_im, ssm_c_re, ssm_c_im, ssm_d, w_glu, b_glu, attn_sinks, w_o_ssm, w_o_attn, w_out, w_up, w_down, loss_target, m_meta_tokens, m_norm_mix_pre, m_norm_mix_post, m_norm_mlp_pre, m_norm_mlp_post, m_w_in, m_ssm_a_re, m_ssm_a_im, m_ssm_log_dt, m_ssm_b_re, m_ssm_b_im, m_ssm_c_re, m_ssm_c_im, m_ssm_d, m_w_glu, m_b_glu, m_attn_sinks, m_w_o_ssm, m_w_o_attn, m_w_out, m_w_up, m_w_down, v_meta_tokens, v_norm_mix_pre, v_norm_mix_post, v_norm_mlp_pre, v_norm_mlp_post, v_w_in, v_ssm_a_re, v_ssm_a_im, v_ssm_log_dt, v_ssm_b_re, v_ssm_b_im, v_ssm_c_re, v_ssm_c_im, v_ssm_d, v_w_glu, v_b_glu, v_attn_sinks, v_w_o_ssm, v_w_o_attn, v_w_out, v_w_up, v_w_down):
    args = locals()
    w = {n: args[n] for n in WEIGHTS}
    m = {n: args["m_" + n] for n in WEIGHTS}
    v = {n: args["v_" + n] for n in WEIGHTS}
    n_layers = w_in.shape[0]
    seq = x.shape[1]
    rows = seq + BLK
    my_slot = _slot(_mesh_pos())

    assert n_layers == 2
    xfer = {n: w[n].astype(XFER_DTYPE) for n in BIG}
    mixer_small = ("w_glu", "w_o_ssm", "w_o_attn", "w_out")
    gather_in0 = _exchange_start([meta_tokens, xfer["w_in"][0]], True, "gather_in0_start")
    gather_mix0 = _exchange_start([xfer[n][0] for n in mixer_small], True, "gather_mix0_start")
    gather_mlp0 = _exchange_start([xfer["w_up"][0], xfer["w_down"][0]], True, "gather_mlp0_start")
    gather_l1 = _exchange_start([xfer[n][1] for n in ("w_in",) + mixer_small + ("w_up", "w_down")], True,
                                "gather_l1_start")
    meta_g, w_in_g0 = _exchange_wait(gather_in0, [gather_mix0["token"], gather_mlp0["token"], gather_l1["token"]],
                                     "gather_in0_wait")
    meta_full = meta_g.transpose(1, 0, 2).reshape(N_META, D)

    def mixer_weights(w_glu_g, w_o_ssm_g, w_o_attn_g, w_out_g):
        return dict(w_glu=w_glu_g.reshape(D_SSM, D_SSM), w_o_ssm=w_o_ssm_g.transpose(1, 0, 2).reshape(D_SSM, D),
                    w_o_attn=w_o_attn_g.reshape(D_ATTN, D), w_out=w_out_g.reshape(D, D))

    gathered = [dict(w_in=w_in_g0), {}]

    gains = {n: w[n].reshape(n_layers, 1, D) for n in ("norm_mix_pre", "norm_mix_post", "norm_mlp_pre", "norm_mlp_post")}
    b_glu3 = b_glu.reshape(n_layers, 1, D_SSM)
    cos, sin_a, sin_b = _rope_tables(rows)

    def ssm_setup(l):
        disc, disc_vjp = jax.vjp(_ssm_discretize, ssm_a_re[l], ssm_a_im[l], ssm_log_dt[l], ssm_b_re[l], ssm_b_im[l])
        return _ssm_tables(*disc, ssm_c_re[l], ssm_c_im[l], ssm_d[l]), disc_vjp

    hres = jnp.concatenate([jnp.zeros((PAD_ROWS, D), F32), meta_full, x[0]], axis=0)

    saved = []
    for l in range(n_layers):
        ssm, disc_vjp = ssm_setup(l)
        wl = gathered[l]
        proj, h = _in_proj(hres, gains["norm_mix_pre"], wl["w_in"], l)
        q, k, vv = _rope_fwd(proj, cos, sin_a, sin_b, l)
        if l == 0:
            wl.update(mixer_weights(*_exchange_wait(gather_mix0, [q], "gather_mix0_wait")))
        y, y_ssm, carry_in = _s5_fwd(proj, ssm, wl["w_glu"], b_glu3, l)
        y_attn = _attn_fwd(q, k, vv, attn_sinks, l)
        merged, mix, hres_mid = _merge_fwd(y_ssm, y_attn, proj, hres, wl["w_o_ssm"], wl["w_o_attn"], wl["w_out"],
                                           gains["norm_mix_post"], l)
        if l == 0:
            wl["w_up"], wl["w_down"] = _exchange_wait(gather_mlp0, [hres_mid], "gather_mlp0_wait")
        up, h2, ff, hres_out = _mlp_fwd(hres_mid, gains["norm_mlp_pre"], gains["norm_mlp_post"], wl["w_up"],
                                        wl["w_down"], l)
        if l == 0:
            got = _exchange_wait(gather_l1, [hres_out], "gather_l1_wait")
            gathered[1] = dict(w_in=got[0], w_up=got[5], w_down=got[6], **mixer_weights(*got[1:5]))
        saved.append(dict(ssm=ssm, disc_vjp=disc_vjp, hres=hres, proj=proj, h=h, q=q, k=k, v=vv, y=y, y_ssm=y_ssm,
                          carry_in=carry_in, y_attn=y_attn, merged=merged, mix=mix, hres_mid=hres_mid,
                          up=up, h2=h2, ff=ff))
        hres = hres_out

    dhres, loss_vec = _loss_and_grad(hres, loss_target[0])
    loss = lax.psum(loss_vec[0, 0], MESH_AXES)

    small_grads = {n: [None] * n_layers for n in SMALL if n != "meta_tokens"}
    scatter_mlp, scatter_mix = [None] * n_layers, [None] * n_layers
    order_token = jnp.zeros((), F32)
    for l in reversed(range(n_layers)):
        s = saved[l]
        wl = gathered[l]
        dff, dup, dhm, dg_mlp_post, dg_mlp_pre = _mlp_bwd(dhres, s["ff"], s["up"], s["hres_mid"],
                                                          gains["norm_mlp_pre"] + order_token,
                                                          gains["norm_mlp_post"], wl["w_up"], wl["w_down"], l)
        dw_down = _matmul_tn(s["up"], dff, f"dw_down_l{l}", a_fn=_relu_squared).reshape(N_DEV, COL_SHARD, D)
        dw_up = _matmul_tn(s["h2"], dup, f"dw_up_l{l}", dev_major_cols=COL_SHARD)
        scatter_mlp[l] = _exchange_start([dw_up, dw_down], False, f"scatter_mlp{l}_start")
        dmix, da1, da2, dgs, dga, dy_ssm, dy_attn, dg_mix_post = _merge_bwd(
            dhm, s["mix"], s["y_ssm"], s["y_attn"], s["proj"], wl["w_o_ssm"], wl["w_o_attn"], wl["w_out"],
            gains["norm_mix_post"] + scatter_mlp[l]["token"][0, 0], l)
        dw_out = _matmul_tn(s["merged"], dmix, f"dw_out_l{l}").reshape(N_DEV, D // N_DEV, D)
        dw_o_attn = _matmul_tn(s["y_attn"], da2, f"dw_o_attn_l{l}").reshape(N_DEV, D_ATTN // N_DEV, D)
        dw_o_ssm = _matmul_tn(s["y_ssm"], da1, f"dw_o_ssm_l{l}", dev_major_cols=D // N_DEV)
        dq, dk, dv, dk_meta, dv_meta, dsink = _attn_bwd(s["q"], s["k"], s["v"], dy_attn, attn_sinks, l)
        dqkv = _rope_bwd(dq, dk, dv, dk_meta, dv_meta, cos, sin_a, sin_b, l)
        du, dw_glu, db_glu, dd_skip, db_mat, dc_mat, dab = _s5_bwd(dy_ssm, s["y"], s["proj"], s["carry_in"], s["ssm"],
                                                                    wl["w_glu"], b_glu3, l)
        dproj = (du, dqkv, dgs, dga)
        dw_in = jnp.concatenate([_matmul_tn(s["h"], piece, f"dw_in{k}_l{l}", dev_major_cols=COL_SHARD)
                                 for k, piece in enumerate(dproj)], axis=0)
        dhres, dg_mix_pre = _in_bwd(dproj, dhm, s["hres"], gains["norm_mix_pre"], wl["w_in"], l)
        scatter_mix[l] = _exchange_start([dw_in, dw_glu.astype(XFER_DTYPE).reshape(N_DEV, D_SSM // N_DEV, D_SSM),
                                          dw_o_ssm, dw_o_attn, dw_out], False, f"scatter_mix{l}_start")
        order_token = scatter_mix[l]["token"][0, 0]

        dab = dab.reshape(N_SB, 2, SB_STATES)
        da_re, da_im, dlog_dt, db_re, db_im = s["disc_vjp"]((
            dab[:, 0].reshape(N_GROUPS, N_STATE), dab[:, 1].reshape(N_GROUPS, N_STATE),
            _block_diag_b_t(db_mat[..., :SB_STATES]), _block_diag_b_t(db_mat[..., SB_STATES:])))
        for name, val in (("norm_mix_pre", dg_mix_pre[0]), ("norm_mix_post", dg_mix_post[0]),
                          ("norm_mlp_pre", dg_mlp_pre[0]), ("norm_mlp_post", dg_mlp_post[0]),
                          ("ssm_a_re", da_re), ("ssm_a_im", da_im), ("ssm_log_dt", dlog_dt),
                          ("ssm_b_re", db_re), ("ssm_b_im", db_im),
                          ("ssm_c_re", _block_diag_c_t(dc_mat[:, :SB_STATES])),
                          ("ssm_c_im", -_block_diag_c_t(dc_mat[:, SB_STATES:])),
                          ("ssm_d", dd_skip.reshape(N_GROUPS, GROUP_CH)), ("b_glu", db_glu[0]),
                          ("attn_sinks", dsink[:, 0])):
            small_grads[name][l] = val

    grad_x = dhres[BLK:][None]
    small_names = [n for n in SMALL if n != "meta_tokens"]
    partial_small = [dhres[PAD_ROWS:BLK]] + [jnp.stack(small_grads[n]) for n in small_names]
    summed_packed = _all_reduce_small(_pack(partial_small) + order_token, "reduce_small_grads")
    summed = _unpack(summed_packed, partial_small)
    grads = dict(zip(small_names, summed[1:]))
    grads["meta_tokens"] = lax.dynamic_slice_in_dim(summed[0], my_slot * (D // N_DEV), D // N_DEV, axis=1)

    delta, new_m, new_v = {}, {}, {}

    def adamw_big(names, recv0, recv1):
        for n, p0, p1 in zip(names, recv0, recv1):
            grads[n], delta[n], new_m[n], new_v[n] = _adamw_layers(p0, p1, w[n], m[n], v[n], f"adamw_{n}")

    recv_mlp1 = _exchange_wait(scatter_mlp[1], [summed_packed], "scatter_mlp1_wait")
    recv_mix1 = _exchange_wait(scatter_mix[1], [recv_mlp1[0]], "scatter_mix1_wait")
    recv_mlp0 = _exchange_wait(scatter_mlp[0], [recv_mix1[0]], "scatter_mlp0_wait")
    adamw_big(("w_up", "w_down"), recv_mlp0, recv_mlp1)
    recv_mix0 = _exchange_wait(scatter_mix[0], [delta["w_down"]], "scatter_mix0_wait")
    adamw_big(("w_in",) + mixer_small, recv_mix0, recv_mix1)
    like = [w[n] for n in SMALL]
    d_s, m_s, v_s = _adamw_packed(_pack([grads[n] for n in SMALL]), _pack(like), _pack([m[n] for n in SMALL]),
                                  _pack([v[n] for n in SMALL]), "adamw_small")
    for n, dd, mm, vs in zip(SMALL, _unpack(d_s, like), _unpack(m_s, like), _unpack(v_s, like)):
        delta[n], new_m[n], new_v[n] = dd, mm, vs

    return (loss, grad_x, *[grads[n] for n in WEIGHTS], *[delta[n] for n in WEIGHTS],
            *[new_m[n] for n in WEIGHTS], *[new_v[n] for n in WEIGHTS])
```

```python
import functools
import math

import jax
import jax.numpy as jnp
from jax import lax
from jax.experimental import pallas as pl
from jax.experimental.pallas import tpu as pltpu

F32 = jnp.float32
MXU_DTYPE = jnp.bfloat16
XFER_DTYPE = MXU_DTYPE
_pcall = pl.pallas_call
SDS = jax.ShapeDtypeStruct

D = 1024
D_SSM = 512
D_ATTN = 1024
D_KV = 256
D_FF = 4096
D_IN = 4096
HEAD_DIM = 64
N_Q_HEADS = 16
N_KV_HEADS = 4
Q_PER_KV = 4
N_META = 16
BLK = 128
PAD_ROWS = BLK - N_META
N_GROUPS = 32
N_STATE = 64
GROUP_CH = 16
N_SB = 4
SB_STATES = 512
ROPE_THETA = 10000.0
ATTN_SCALE = HEAD_DIM ** -0.5
NEG_INF = -1e30
RMS_EPS = 1e-6
N_DEV = 8
COL_SHARD = 512

ADAM_LR = 0.001
ADAM_B1 = 0.9
ADAM_B2 = 0.999
ADAM_EPS = 1e-08
ADAM_WD = 0.01
ADAM_STEP = 10

VMEM_LIMIT = 56 * 1024 * 1024
MESH_AXES = ("x", "y", "c")

_NT = (((1,), (1,)), ((), ()))
_TN = (((0,), (0,)), ((), ()))


def _cparams(*sem):
    return pltpu.CompilerParams(dimension_semantics=tuple(sem) if sem else None,
                                vmem_limit_bytes=VMEM_LIMIT)


def _row_tile(rows, cap=640):
    for t in (640, 512, 320, 256, 128):
        if t <= cap and rows % t == 0:
            return t
    raise ValueError(f"unsupported row count {rows}")


def _dot(a, b):
    return jnp.dot(a, b, preferred_element_type=F32)


def _dot_nt(a, b):
    return lax.dot_general(a, b, _NT, preferred_element_type=F32)


def _dot_tn(a, b):
    return lax.dot_general(a, b, _TN, preferred_element_type=F32)


def _sigmoid(x):
    return 1.0 / (1.0 + jnp.exp(-x))


_GELU_C = math.sqrt(2.0 / math.pi)


def _gelu_parts(y):
    t = jnp.tanh(_GELU_C * (y + 0.044715 * (y * y * y)))
    return 0.5 * y * (1.0 + t), t


def _gelu_grad(y, t):
    return 0.5 * (1.0 + t) + 0.5 * y * (1.0 - t * t) * (_GELU_C * (1.0 + 0.134145 * (y * y)))


def _rms_fwd(x, gain):
    r = lax.rsqrt(jnp.mean(x * x, axis=-1, keepdims=True) + RMS_EPS)
    return (x * r) * gain


def _rms_bwd(x, gain, dout):
    r = lax.rsqrt(jnp.mean(x * x, axis=-1, keepdims=True) + RMS_EPS)
    xh = x * r
    dxh = dout * gain
    dx = r * (dxh - xh * jnp.mean(dxh * xh, axis=-1, keepdims=True))
    return dx, jnp.sum(dout * xh, axis=0, keepdims=True)


def _mesh_pos():
    return lax.axis_index("x"), lax.axis_index("y"), lax.axis_index("c")


def _peer(pos, d):
    x, y, c = pos
    return (1 - x if d & 4 else x, 1 - y if d & 2 else y, 1 - c if d & 1 else c)


def _slot(pos):
    return 4 * pos[0] + 2 * pos[1] + pos[2]


_HBM = pl.BlockSpec(memory_space=pltpu.HBM)
_SEM = pl.BlockSpec(memory_space=pltpu.SEMAPHORE)
_DATAFLOW = pltpu.SideEffectType.DATAFLOW_SIDE_EFFECTING


def _exchange_copy(gather, src_ref, land_ref, sems, k, d, me, send_side):
    peer = _peer(me, d)
    sender = me if send_side else peer
    src = src_ref if gather else src_ref.at[_slot(peer) if send_side else _slot(me)]
    return pltpu.make_async_remote_copy(
        src_ref=src, dst_ref=land_ref.at[_slot(sender)],
        send_sem=sems[0].at[k * (N_DEV - 1) + d - 1], recv_sem=sems[1].at[k * (N_DEV - 1) + d - 1],
        device_id=peer, device_id_type=pl.DeviceIdType.MESH)


def _exchange_start(srcs, gather, name):
    n = len(srcs)
    lands = [lax.empty(((N_DEV,) + s.shape) if gather else s.shape, s.dtype) for s in srcs]

    def body(*refs):
        src_refs, land_refs = refs[:n], refs[n:2 * n]
        sems = refs[2 * n:2 * n + 2]
        token, local_sems = refs[4 * n + 2], refs[4 * n + 3]
        me = _mesh_pos()
        own = [pltpu.make_async_copy(src_refs[k] if gather else src_refs[k].at[_slot(me)],
                                     land_refs[k].at[_slot(me)], local_sems.at[k]) for k in range(n)]
        for cp in own:
            cp.start()
        for cp in own:
            cp.wait()
        for k in range(n):
            for d in range(1, N_DEV):
                _exchange_copy(gather, src_refs[k], land_refs[k], sems, k, d, me, True).start()
        token[...] = jnp.zeros_like(token)

    sem_type = pltpu.SemaphoreType.DMA((n * (N_DEV - 1),))
    outs = _pcall(
        body, name=name,
        out_shape=(sem_type, sem_type, *[pltpu.HBM(a.shape, a.dtype) for a in list(srcs) + lands], SDS((8, 128), F32)),
        in_specs=[_HBM] * (2 * n),
        out_specs=(_SEM, _SEM, *[_HBM] * (2 * n), pl.BlockSpec(memory_space=pltpu.VMEM)),
        input_output_aliases={k: 2 + k for k in range(2 * n)},
        scratch_shapes=[pltpu.SemaphoreType.DMA((n,))],
        compiler_params=pltpu.CompilerParams(has_side_effects=_DATAFLOW),
    )(*[pltpu.with_memory_space_constraint(a, pltpu.HBM) for a in list(srcs) + lands])
    return dict(sems=outs[:2], srcs=outs[2:2 + n], lands=outs[2 + n:2 + 2 * n], token=outs[-1], gather=gather)


def _exchange_wait(started, after, name):
    n = len(started["srcs"])
    gather = started["gather"]

    def body(*refs):
        src_refs, land_refs = refs[:n], refs[n:2 * n]
        sems = refs[2 * n:2 * n + 2]
        me = _mesh_pos()
        for k in range(n):
            for d in range(1, N_DEV):
                _exchange_copy(gather, src_refs[k], land_refs[k], sems, k, d, me, True).wait_send()
        for k in range(n):
            for d in range(1, N_DEV):
                _exchange_copy(gather, src_refs[k], land_refs[k], sems, k, d, me, False).wait_recv()

    arrays = list(started["srcs"]) + list(started["lands"])
    outs = _pcall(
        body, name=name,
        out_shape=tuple(pltpu.HBM(a.shape, a.dtype) for a in arrays),
        in_specs=[_HBM] * (2 * n) + [_SEM, _SEM] + [pl.BlockSpec(memory_space=pl.ANY)] * len(after),
        out_specs=tuple([_HBM] * (2 * n)),
        input_output_aliases={k: k for k in range(2 * n)},
        compiler_params=pltpu.CompilerParams(has_side_effects=_DATAFLOW),
    )(*arrays, *started["sems"], *after)
    return list(outs[n:])


def _all_reduce_small(packed, name):
    rows = packed.shape[0]

    def body(x_ref, out_ref, gath, send_sems, recv_sems):
        me = _mesh_pos()
        my_slot = _slot(me)
        gath[my_slot] = x_ref[...]
        sends = [pltpu.make_async_remote_copy(
            src_ref=x_ref, dst_ref=gath.at[my_slot],
            send_sem=send_sems.at[d - 1], recv_sem=recv_sems.at[d - 1],
            device_id=_peer(me, d), device_id_type=pl.DeviceIdType.MESH) for d in range(1, N_DEV)]
        for cp in sends:
            cp.start()
        for d in range(1, N_DEV):
            pltpu.make_async_remote_copy(
                src_ref=x_ref, dst_ref=gath.at[_slot(_peer(me, d))],
                send_sem=send_sems.at[d - 1], recv_sem=recv_sems.at[d - 1],
                device_id=_peer(me, d), device_id_type=pl.DeviceIdType.MESH).wait_recv()
        for cp in sends:
            cp.wait_send()
        acc = gath[0]
        for s in range(1, N_DEV):
            acc = acc + gath[s]
        out_ref[...] = acc

    vmem = pl.BlockSpec(memory_space=pltpu.VMEM)
    return _pcall(
        body, name=name, out_shape=SDS(packed.shape, F32), in_specs=[vmem], out_specs=vmem,
        scratch_shapes=[pltpu.VMEM((N_DEV, rows, 128), F32),
                        pltpu.SemaphoreType.DMA((N_DEV - 1,)), pltpu.SemaphoreType.DMA((N_DEV - 1,))],
        compiler_params=_cparams(),
    )(packed)


def _load_resident(w_hbm, w_scr, sems, first_step):
    @pl.when(first_step)
    def _():
        copies = [pltpu.make_async_copy(w_hbm.at[s], w_scr.at[s], sems.at[s]) for s in range(N_DEV)]
        for cp in copies:
            cp.start()
        for cp in copies:
            cp.wait()


def _in_proj(hres, gain3, w_in_g, layer):
    rows = hres.shape[0]
    tm = _row_tile(rows)

    def body(x_ref, g_ref, w_hbm, proj_ref, h_ref, h_scr, w_scr, w_sem):
        j = pl.program_id(1)
        _load_resident(w_hbm, w_scr, w_sem, (pl.program_id(0) == 0) & (j == 0))

        @pl.when(j == 0)
        def _():
            hn = _rms_fwd(x_ref[...], g_ref[...]).astype(MXU_DTYPE)
            h_scr[...] = hn
            h_ref[...] = hn

        proj_ref[...] = _dot(h_scr[...], w_scr[j])

    return _pcall(
        body, name=f"in_proj_l{layer}", grid=(rows // tm, N_DEV),
        in_specs=[pl.BlockSpec((tm, D), lambda i, j: (i, 0)),
                  pl.BlockSpec((None, 1, D), lambda i, j: (layer, 0, 0)),
                  pl.BlockSpec(memory_space=pl.ANY)],
        out_specs=[pl.BlockSpec((tm, COL_SHARD), lambda i, j: (i, j)),
                   pl.BlockSpec((tm, D), lambda i, j: (i, 0))],
        out_shape=[SDS((rows, D_IN), F32), SDS((rows, D), MXU_DTYPE)],
        scratch_shapes=[pltpu.VMEM((tm, D), MXU_DTYPE), pltpu.VMEM((N_DEV, D, COL_SHARD), MXU_DTYPE),
                        pltpu.SemaphoreType.DMA((N_DEV,))],
        compiler_params=_cparams("arbitrary", "arbitrary"),
    )(hres, gain3, w_in_g)


def _rope_lanes(t, cos, sin_a, sin_b):
    return t * cos + pltpu.roll(t, 96, 1) * sin_a + pltpu.roll(t, 32, 1) * sin_b


def _rope_fwd(proj, cos, sin_a, sin_b, layer):
    rows = proj.shape[0]
    tm = _row_tile(rows)

    def body(q0_ref, q1_ref, kv_ref, c_ref, a_ref, b_ref, qo_ref, ko_ref, vo_ref):
        c, a, b = c_ref[...], a_ref[...], b_ref[...]
        for half, q_ref in enumerate((q0_ref, q1_ref)):
            for t in range(4):
                x = q_ref[:, t * 128:(t + 1) * 128]
                lo = half * 512 + t * 128
                qo_ref[:, lo:lo + 128] = (_rope_lanes(x, c, a, b) * ATTN_SCALE).astype(MXU_DTYPE)
        for t in range(2):
            x = kv_ref[:, t * 128:(t + 1) * 128]
            ko_ref[:, t * 128:(t + 1) * 128] = _rope_lanes(x, c, a, b).astype(MXU_DTYPE)
        vo_ref[...] = kv_ref[:, D_KV:2 * D_KV].astype(MXU_DTYPE)

    tab = pl.BlockSpec((tm, 128), lambda i: (i, 0))
    return _pcall(
        body, name=f"rope_fwd_l{layer}", grid=(rows // tm,),
        in_specs=[pl.BlockSpec((tm, 512), lambda i: (i, 1)), pl.BlockSpec((tm, 512), lambda i: (i, 2)),
                  pl.BlockSpec((tm, 512), lambda i: (i, 3)), tab, tab, tab],
        out_specs=[pl.BlockSpec((tm, D_ATTN), lambda i: (i, 0)), pl.BlockSpec((tm, D_KV), lambda i: (i, 0)),
                   pl.BlockSpec((tm, D_KV), lambda i: (i, 0))],
        out_shape=[SDS((rows, D_ATTN), MXU_DTYPE), SDS((rows, D_KV), MXU_DTYPE), SDS((rows, D_KV), MXU_DTYPE)],
        compiler_params=_cparams("parallel"),
    )(proj, proj, proj, cos, sin_a, sin_b)


SCAN_TILE = 8


def _scan_tiles(x_ref, out_ref, tre_ref, tim_ref, sb, t_r, t_i, reverse, prev_ref=None):
    base = 4 if reverse else 0
    n_tiles = BLK // SCAN_TILE
    row = lax.broadcasted_iota(jnp.int32, (SCAN_TILE, SB_STATES), 0)
    for j in (range(n_tiles - 1, -1, -1) if reverse else range(n_tiles)):
        rows = slice(SCAN_TILE * j, SCAN_TILE * (j + 1))
        xr = x_ref[rows, :SB_STATES]
        xi = x_ref[rows, SB_STATES:]
        for k in range(3):
            shift = SCAN_TILE - (1 << k) if reverse else (1 << k)
            rr = pltpu.roll(xr, shift, 0)
            ri = pltpu.roll(xi, shift, 0)
            ar = tre_ref[sb, base + k]
            ai = tim_ref[sb, base + k]
            xr, xi = xr + (ar * rr - ai * ri), xi + (ar * ri + ai * rr)
        pr = tre_ref[sb, base + 3]
        pi = tim_ref[sb, base + 3]
        xr, xi = xr + (pr * t_r - pi * t_i), xi + (pr * t_i + pi * t_r)
        out_ref[rows, :SB_STATES] = xr
        out_ref[rows, SB_STATES:] = xi
        if prev_ref is not None:
            prev_ref[rows, :SB_STATES] = jnp.where(row == 0, t_r, pltpu.roll(xr, 1, 0))
            prev_ref[rows, SB_STATES:] = jnp.where(row == 0, t_i, pltpu.roll(xi, 1, 0))
        edge = slice(0, 1) if reverse else slice(SCAN_TILE - 1, SCAN_TILE)
        t_r, t_i = xr[edge], xi[edge]
    return t_r, t_i


def _s5_fwd(proj, ssm, w_glu, b_glu3, layer):
    rows = proj.shape[0]
    n_chunks = rows // BLK
    b_mat, c_mat, t_re, t_im, d_skip = (ssm[k] for k in ("b_mat", "c_mat", "t_re", "t_im", "d_skip"))

    def body(u_ref, bm_ref, cm_ref, tre_ref, tim_ref, d_ref, wg_ref, bg_ref,
             y_ref, ys_ref, cin_ref, carry, bu_scr, s_scr):
        @pl.when(pl.program_id(0) == 0)
        def _():
            carry[...] = jnp.zeros_like(carry)

        cin_ref[...] = carry[...]
        u = u_ref[...]
        for sb in range(N_SB):
            cols = slice(sb * 128, (sb + 1) * 128)
            u_sb = u[:, cols]
            bu_scr[sb] = _dot(u_sb.astype(MXU_DTYPE), bm_ref[sb])
            t_r, t_i = _scan_tiles(bu_scr.at[sb], s_scr.at[sb], tre_ref, tim_ref, sb,
                                   carry[2 * sb:2 * sb + 1, :], carry[2 * sb + 1:2 * sb + 2, :], False)
            carry[2 * sb:2 * sb + 1, :] = t_r
            carry[2 * sb + 1:2 * sb + 2, :] = t_i
            y_ref[:, cols] = _dot(s_scr[sb].astype(MXU_DTYPE), cm_ref[sb]) + d_ref[:, cols] * u_sb
        z, _ = _gelu_parts(y_ref[...])
        gl = _dot(z.astype(MXU_DTYPE), wg_ref[...]) + bg_ref[...]
        ys_ref[...] = (z * _sigmoid(gl)).astype(MXU_DTYPE)

    full = lambda shape: pl.BlockSpec(shape, lambda j: (0,) * len(shape))
    return _pcall(
        body, name=f"s5_fwd_l{layer}", grid=(n_chunks,),
        in_specs=[pl.BlockSpec((BLK, D_SSM), lambda j: (j, 0)),
                  full((N_SB, 128, 2 * SB_STATES)), full((N_SB, 2 * SB_STATES, 128)),
                  full((N_SB, 8, SCAN_TILE, SB_STATES)), full((N_SB, 8, SCAN_TILE, SB_STATES)),
                  full((1, D_SSM)), full((D_SSM, D_SSM)),
                  pl.BlockSpec((None, 1, D_SSM), lambda j: (layer, 0, 0))],
        out_specs=[pl.BlockSpec((BLK, D_SSM), lambda j: (j, 0)), pl.BlockSpec((BLK, D_SSM), lambda j: (j, 0)),
                   pl.BlockSpec((None, 8, SB_STATES), lambda j: (j, 0, 0))],
        out_shape=[SDS((rows, D_SSM), F32), SDS((rows, D_SSM), MXU_DTYPE), SDS((n_chunks, 8, SB_STATES), F32)],
        scratch_shapes=[pltpu.VMEM((8, SB_STATES), F32), pltpu.VMEM((N_SB, BLK, 2 * SB_STATES), F32),
                        pltpu.VMEM((N_SB, BLK, 2 * SB_STATES), F32)],
        compiler_params=_cparams("arbitrary"),
    )(proj, b_mat, c_mat, t_re, t_im, d_skip, w_glu, b_glu3)


def _attn_mask(i):
    row = lax.broadcasted_iota(jnp.int32, (BLK, 3 * BLK), 0) + i * BLK
    col = lax.broadcasted_iota(jnp.int32, (BLK, 3 * BLK), 1)
    seg = jnp.right_shift(col, 7)
    c = jnp.bitwise_and(col, BLK - 1)
    kidx = c + (i + seg - 2) * BLK
    ok_meta = (seg == 0) & (c >= PAD_ROWS) & (row - c >= BLK)
    ok_win = (seg > 0) & (kidx >= PAD_ROWS) & (kidx <= row) & (row - kidx < BLK)
    return jnp.where(ok_meta | ok_win, 0.0, NEG_INF)


def _head_lanes(h):
    return slice(h * HEAD_DIM, (h + 1) * HEAD_DIM)


def _group_rows(ref, kvh):
    return jnp.concatenate([ref[:, _head_lanes(kvh * Q_PER_KV + g)] for g in range(Q_PER_KV)], axis=0)


def _group_bias(bias, sink_ref, layer, kvh):
    first_col = lax.broadcasted_iota(jnp.int32, (BLK, BLK), 1) == 0
    slabs = []
    for g in range(Q_PER_KV):
        first = jnp.where(first_col, sink_ref[layer, kvh * Q_PER_KV + g], bias[:, :BLK])
        slabs.append(jnp.concatenate([first, bias[:, BLK:]], axis=1))
    return jnp.concatenate(slabs, axis=0)


def _attn_probs(q4, k3, bias4):
    s = _dot_nt(q4, k3) + bias4
    e = jnp.exp(s - jnp.max(s, axis=-1, keepdims=True))
    return e * (1.0 / jnp.sum(e, axis=-1, keepdims=True))


def _attn_fwd(q, k, v, sinks, layer):
    rows = q.shape[0]
    n_blk = rows // BLK

    def body(sink_ref, q_ref, km_ref, kp_ref, kc_ref, vm_ref, vp_ref, vc_ref, o_ref):
        bias = _attn_mask(pl.program_id(0))
        for kvh in range(N_KV_HEADS):
            lanes = _head_lanes(kvh)
            k3 = jnp.concatenate([km_ref[:, lanes], kp_ref[:, lanes], kc_ref[:, lanes]], axis=0)
            v3 = jnp.concatenate([vm_ref[:, lanes], vp_ref[:, lanes], vc_ref[:, lanes]], axis=0)
            p = _attn_probs(_group_rows(q_ref, kvh), k3, _group_bias(bias, sink_ref, layer, kvh))
            o4 = _dot(p.astype(MXU_DTYPE), v3).astype(MXU_DTYPE)
            for g in range(Q_PER_KV):
                o_ref[:, _head_lanes(kvh * Q_PER_KV + g)] = o4[g * BLK:(g + 1) * BLK]

    kv_meta = pl.BlockSpec((BLK, D_KV), lambda i: (0, 0))
    kv_prev = pl.BlockSpec((BLK, D_KV), lambda i: (jnp.maximum(i - 1, 0), 0))
    kv_cur = pl.BlockSpec((BLK, D_KV), lambda i: (i, 0))
    return _pcall(
        body, name=f"attn_fwd_l{layer}", grid=(n_blk,),
        in_specs=[pl.BlockSpec(memory_space=pltpu.SMEM),
                  pl.BlockSpec((BLK, D_ATTN), lambda i: (i, 0)),
                  kv_meta, kv_prev, kv_cur, kv_meta, kv_prev, kv_cur],
        out_specs=pl.BlockSpec((BLK, D_ATTN), lambda i: (i, 0)),
        out_shape=SDS((rows, D_ATTN), MXU_DTYPE),
        compiler_params=_cparams("parallel"),
    )(sinks, q, k, k, k, v, v, v)


def _merge_fwd(y_ssm, y_attn, proj, hres, w_o_ssm, w_o_attn, w_out, gain3, layer):
    rows = hres.shape[0]
    tm = _row_tile(rows, 320)

    def body(ys_ref, ya_ref, gs_ref, ga_ref, x_ref, wos_ref, woa_ref, wout_ref, g_ref,
             mg_ref, mix_ref, out_ref):
        a1 = _dot(ys_ref[...], wos_ref[...])
        a2 = _dot(ya_ref[...], woa_ref[...])
        merged = (_sigmoid(gs_ref[...]) * a1 + _sigmoid(ga_ref[...]) * a2).astype(MXU_DTYPE)
        mg_ref[...] = merged
        mix = _dot(merged, wout_ref[...])
        mix_ref[...] = mix
        out_ref[...] = x_ref[...] + _rms_fwd(mix, g_ref[...])

    row_d = pl.BlockSpec((tm, D), lambda i: (i, 0))
    full = lambda shape: pl.BlockSpec(shape, lambda i: (0,) * len(shape))
    return _pcall(
        body, name=f"merge_fwd_l{layer}", grid=(rows // tm,),
        in_specs=[pl.BlockSpec((tm, D_SSM), lambda i: (i, 0)), row_d,
                  pl.BlockSpec((tm, D), lambda i: (i, 2)), pl.BlockSpec((tm, D), lambda i: (i, 3)), row_d,
                  full((D_SSM, D)), full((D_ATTN, D)), full((D, D)),
                  pl.BlockSpec((None, 1, D), lambda i: (layer, 0, 0))],
        out_specs=[row_d, row_d, row_d],
        out_shape=[SDS((rows, D), MXU_DTYPE), SDS((rows, D), F32), SDS((rows, D), F32)],
        compiler_params=_cparams("parallel"),
    )(y_ssm, y_attn, proj, proj, hres, w_o_ssm, w_o_attn, w_out, gain3)


def _mlp_fwd(hres, gain_pre3, gain_post3, w_up_g, w_down_g, layer):
    rows = hres.shape[0]
    tm = _row_tile(rows)

    def body(x_ref, gp_ref, gq_ref, wu_hbm, wd_hbm, up_ref, h_ref, ff_ref, out_ref,
             h_scr, acc, wu_scr, wd_scr, wu_sem, wd_sem):
        kf = pl.program_id(1)
        first = (pl.program_id(0) == 0) & (kf == 0)
        _load_resident(wu_hbm, wu_scr, wu_sem, first)
        _load_resident(wd_hbm, wd_scr, wd_sem, first)

        @pl.when(kf == 0)
        def _():
            hn = _rms_fwd(x_ref[...], gp_ref[...]).astype(MXU_DTYPE)
            h_scr[...] = hn
            h_ref[...] = hn
            acc[...] = jnp.zeros_like(acc)

        up = _dot(h_scr[...], wu_scr[kf])
        up_ref[...] = up.astype(MXU_DTYPE)
        r = jnp.maximum(up, 0.0)
        acc[...] += _dot((r * r).astype(MXU_DTYPE), wd_scr[kf])

        @pl.when(kf == N_DEV - 1)
        def _():
            ff = acc[...]
            ff_ref[...] = ff
            out_ref[...] = x_ref[...] + _rms_fwd(ff, gq_ref[...])

    row_d = pl.BlockSpec((tm, D), lambda i, k: (i, 0))
    gain = pl.BlockSpec((None, 1, D), lambda i, k: (layer, 0, 0))
    return _pcall(
        body, name=f"mlp_fwd_l{layer}", grid=(rows // tm, N_DEV),
        in_specs=[row_d, gain, gain, pl.BlockSpec(memory_space=pl.ANY), pl.BlockSpec(memory_space=pl.ANY)],
        out_specs=[pl.BlockSpec((tm, COL_SHARD), lambda i, k: (i, k)), row_d, row_d, row_d],
        out_shape=[SDS((rows, D_FF), MXU_DTYPE), SDS((rows, D), MXU_DTYPE), SDS((rows, D), F32), SDS((rows, D), F32)],
        scratch_shapes=[pltpu.VMEM((tm, D), MXU_DTYPE), pltpu.VMEM((tm, D), F32),
                        pltpu.VMEM((N_DEV, D, COL_SHARD), MXU_DTYPE), pltpu.VMEM((N_DEV, COL_SHARD, D), MXU_DTYPE),
                        pltpu.SemaphoreType.DMA((N_DEV,)), pltpu.SemaphoreType.DMA((N_DEV,))],
        compiler_params=_cparams("arbitrary", "arbitrary"),
    )(hres, gain_pre3, gain_post3, w_up_g, w_down_g)


def _loss_and_grad(hres, target):
    rows = hres.shape[0]
    n_blk = rows // BLK

    def body(y_ref, t_ref, dy_ref, loss_ref):
        i = pl.program_id(0)

        @pl.when(i == 0)
        def _():
            dy_ref[...] = jnp.zeros_like(dy_ref)
            loss_ref[...] = jnp.zeros_like(loss_ref)

        @pl.when(i > 0)
        def _():
            err = y_ref[...] - t_ref[...]
            dy_ref[...] = err * (1.0 / D)
            loss_ref[...] += jnp.sum(err * err) * (0.5 / D)

    return _pcall(
        body, name="loss", grid=(n_blk,),
        in_specs=[pl.BlockSpec((BLK, D), lambda i: (i, 0)),
                  pl.BlockSpec((BLK, D), lambda i: (jnp.maximum(i - 1, 0), 0))],
        out_specs=[pl.BlockSpec((BLK, D), lambda i: (i, 0)), pl.BlockSpec((1, 128), lambda i: (0, 0))],
        out_shape=[SDS((rows, D), F32), SDS((1, 128), F32)],
        compiler_params=_cparams("arbitrary"),
    )(hres, target)


def _relu_squared(up):
    r = jnp.maximum(up.astype(F32), 0.0)
    return (r * r).astype(MXU_DTYPE)


def _matmul_tn(a, b, name, dev_major_cols=None, a_fn=None):
    rows, ka = a.shape
    n = b.shape[1]
    ta = min(ka, 1024)
    tn = 1024 if n % 1024 == 0 else 512
    tr = _row_tile(rows)
    n_r = rows // tr

    def body(a_ref, b_ref, o_ref, acc):
        r = pl.program_id(2)

        @pl.when(r == 0)
        def _():
            acc[...] = jnp.zeros_like(acc)

        a_blk = a_ref[...] if a_fn is None else a_fn(a_ref[...])
        acc[...] += _dot_tn(a_blk, b_ref[...])

        @pl.when(r == n_r - 1)
        def _():
            if dev_major_cols is None:
                o_ref[...] = acc[...].astype(XFER_DTYPE)
            else:
                for s in range(tn // dev_major_cols):
                    o_ref[s] = acc[:, s * dev_major_cols:(s + 1) * dev_major_cols].astype(XFER_DTYPE)

    if dev_major_cols is None:
        out_spec = pl.BlockSpec((ta, tn), lambda i, j, r: (i, j))
        out_shape = SDS((ka, n), XFER_DTYPE)
    else:
        w = dev_major_cols
        out_spec = pl.BlockSpec((tn // w, ta, w), lambda i, j, r: (j, i, 0))
        out_shape = SDS((n // w, ka, w), XFER_DTYPE)
    return _pcall(
        body, name=name, grid=(ka // ta, n // tn, n_r),
        in_specs=[pl.BlockSpec((tr, ta), lambda i, j, r: (r, i)), pl.BlockSpec((tr, tn), lambda i, j, r: (r, j))],
        out_specs=out_spec, out_shape=out_shape,
        scratch_shapes=[pltpu.VMEM((ta, tn), F32)],
        compiler_params=_cparams("parallel", "parallel", "arbitrary"),
    )(a, b)


def _mlp_bwd(dout, ff, up, hres_mid, gain_pre3, gain_post3, w_up_g, w_down_g, layer):
    rows = dout.shape[0]
    tm = _row_tile(rows)

    def body(do_ref, ff_ref, up_ref, x_ref, gp_ref, gq_ref, wu_hbm, wd_hbm,
             dff_ref, dup_ref, dx_ref, dgq_ref, dgp_ref, dff_scr, acc, wu_scr, wd_scr, wu_sem, wd_sem):
        i = pl.program_id(0)
        kf = pl.program_id(1)
        _load_resident(wu_hbm, wu_scr, wu_sem, (i == 0) & (kf == 0))
        _load_resident(wd_hbm, wd_scr, wd_sem, (i == 0) & (kf == 0))

        @pl.when((i == 0) & (kf == 0))
        def _():
            dgq_ref[...] = jnp.zeros_like(dgq_ref)
            dgp_ref[...] = jnp.zeros_like(dgp_ref)

        @pl.when(kf == 0)
        def _():
            dff, dg = _rms_bwd(ff_ref[...], gq_ref[...], do_ref[...])
            dgq_ref[...] += dg
            dffb = dff.astype(MXU_DTYPE)
            dff_scr[...] = dffb
            dff_ref[...] = dffb
            acc[...] = jnp.zeros_like(acc)

        dact = _dot_nt(dff_scr[...], wd_scr[kf])
        dup = (dact * (2.0 * jnp.maximum(up_ref[...].astype(F32), 0.0))).astype(MXU_DTYPE)
        dup_ref[...] = dup
        acc[...] += _dot_nt(dup, wu_scr[kf])

        @pl.when(kf == N_DEV - 1)
        def _():
            dx, dg = _rms_bwd(x_ref[...], gp_ref[...], acc[...])
            dgp_ref[...] += dg
            dx_ref[...] = do_ref[...] + dx

    row_d = pl.BlockSpec((tm, D), lambda i, k: (i, 0))
    gain = pl.BlockSpec((None, 1, D), lambda i, k: (layer, 0, 0))
    dgain = pl.BlockSpec((1, D), lambda i, k: (0, 0))
    return _pcall(
        body, name=f"mlp_bwd_l{layer}", grid=(rows // tm, N_DEV),
        in_specs=[row_d, row_d, pl.BlockSpec((tm, COL_SHARD), lambda i, k: (i, k)), row_d, gain, gain,
                  pl.BlockSpec(memory_space=pl.ANY), pl.BlockSpec(memory_space=pl.ANY)],
        out_specs=[row_d, pl.BlockSpec((tm, COL_SHARD), lambda i, k: (i, k)), row_d, dgain, dgain],
        out_shape=[SDS((rows, D), MXU_DTYPE), SDS((rows, D_FF), MXU_DTYPE), SDS((rows, D), F32),
                   SDS((1, D), F32), SDS((1, D), F32)],
        scratch_shapes=[pltpu.VMEM((tm, D), MXU_DTYPE), pltpu.VMEM((tm, D), F32),
                        pltpu.VMEM((N_DEV, D, COL_SHARD), MXU_DTYPE), pltpu.VMEM((N_DEV, COL_SHARD, D), MXU_DTYPE),
                        pltpu.SemaphoreType.DMA((N_DEV,)), pltpu.SemaphoreType.DMA((N_DEV,))],
        compiler_params=_cparams("arbitrary", "arbitrary"),
    )(dout, ff, up, hres_mid, gain_pre3, gain_post3, w_up_g, w_down_g)


def _merge_bwd(dhm, mix, y_ssm, y_attn, proj, w_o_ssm, w_o_attn, w_out, gain3, layer):
    rows = dhm.shape[0]
    tm = _row_tile(rows, 320)

    def body(dh_ref, mix_ref, ys_ref, ya_ref, gs_ref, ga_ref, wos_ref, woa_ref, wout_ref, g_ref,
             dmix_ref, da1_ref, da2_ref, dgs_ref, dga_ref, dys_ref, dya_ref, dg_ref):
        @pl.when(pl.program_id(0) == 0)
        def _():
            dg_ref[...] = jnp.zeros_like(dg_ref)

        dmix, dg = _rms_bwd(mix_ref[...], g_ref[...], dh_ref[...])
        dg_ref[...] += dg
        dmixb = dmix.astype(MXU_DTYPE)
        dmix_ref[...] = dmixb
        dmerged = _dot_nt(dmixb, wout_ref[...])
        sg_s = _sigmoid(gs_ref[...])
        sg_a = _sigmoid(ga_ref[...])
        da1 = (dmerged * sg_s).astype(MXU_DTYPE)
        da2 = (dmerged * sg_a).astype(MXU_DTYPE)
        da1_ref[...] = da1
        da2_ref[...] = da2
        a1 = _dot(ys_ref[...], wos_ref[...])
        a2 = _dot(ya_ref[...], woa_ref[...])
        dgs_ref[...] = (dmerged * a1 * (sg_s * (1.0 - sg_s))).astype(MXU_DTYPE)
        dga_ref[...] = (dmerged * a2 * (sg_a * (1.0 - sg_a))).astype(MXU_DTYPE)
        dys_ref[...] = _dot_nt(da1, wos_ref[...])
        dya_ref[...] = _dot_nt(da2, woa_ref[...])

    row_d = pl.BlockSpec((tm, D), lambda i: (i, 0))
    full = lambda shape: pl.BlockSpec(shape, lambda i: (0,) * len(shape))
    return _pcall(
        body, name=f"merge_bwd_l{layer}", grid=(rows // tm,),
        in_specs=[row_d, row_d, pl.BlockSpec((tm, D_SSM), lambda i: (i, 0)), row_d,
                  pl.BlockSpec((tm, D), lambda i: (i, 2)), pl.BlockSpec((tm, D), lambda i: (i, 3)),
                  full((D_SSM, D)), full((D_ATTN, D)), full((D, D)),
                  pl.BlockSpec((None, 1, D), lambda i: (layer, 0, 0))],
        out_specs=[row_d, row_d, row_d, row_d, row_d, pl.BlockSpec((tm, D_SSM), lambda i: (i, 0)), row_d,
                   pl.BlockSpec((1, D), lambda i: (0, 0))],
        out_shape=[SDS((rows, D), MXU_DTYPE)] * 5 + [SDS((rows, D_SSM), F32), SDS((rows, D_ATTN), F32),
                                                      SDS((1, D), F32)],
        compiler_params=_cparams("arbitrary"),
    )(dhm, mix, y_ssm, y_attn, proj, proj, w_o_ssm, w_o_attn, w_out, gain3)


def _attn_bwd(q, k, v, d_out, sinks, layer):
    rows = q.shape[0]
    n_blk = rows // BLK
    last = n_blk - 1

    def body(sink_ref, q_ref, km_ref, kp_ref, kc_ref, vm_ref, vp_ref, vc_ref, do_ref,
             dq_ref, dk_ref, dv_ref, dkm_ref, dvm_ref, ds_ref, dk_carry, dv_carry):
        i = pl.program_id(0)

        @pl.when(i == 0)
        def _():
            dkm_ref[...] = jnp.zeros_like(dkm_ref)
            dvm_ref[...] = jnp.zeros_like(dvm_ref)
            ds_ref[...] = jnp.zeros_like(ds_ref)
            dk_carry[...] = jnp.zeros_like(dk_carry)
            dv_carry[...] = jnp.zeros_like(dv_carry)

        @pl.when(i <= last)
        def _():
            bias = _attn_mask(i)
            for kvh in range(N_KV_HEADS):
                lanes = _head_lanes(kvh)
                k3 = jnp.concatenate([km_ref[:, lanes], kp_ref[:, lanes], kc_ref[:, lanes]], axis=0)
                v3 = jnp.concatenate([vm_ref[:, lanes], vp_ref[:, lanes], vc_ref[:, lanes]], axis=0)
                q4 = _group_rows(q_ref, kvh)
                do4 = _group_rows(do_ref, kvh).astype(MXU_DTYPE)
                p = _attn_probs(q4, k3, _group_bias(bias, sink_ref, layer, kvh))
                dp = _dot_nt(do4, v3)
                dsf = p * (dp - jnp.sum(dp * p, axis=-1, keepdims=True))
                dsc = dsf.astype(MXU_DTYPE)
                dv3 = _dot_tn(p.astype(MXU_DTYPE), do4)
                dk3 = _dot_tn(dsc, q4)
                dq4 = _dot(dsc, k3)
                for g in range(Q_PER_KV):
                    h = kvh * Q_PER_KV + g
                    dq_ref[:, _head_lanes(h)] = dq4[g * BLK:(g + 1) * BLK]
                    ds_ref[h:h + 1, :] += jnp.sum(dsf[g * BLK:(g + 1) * BLK, 0:BLK], axis=0, keepdims=True)
                dkm_ref[:, lanes] += dk3[0:BLK]
                dvm_ref[:, lanes] += dv3[0:BLK]
                dk_ref[:, lanes] = dk_carry[:, lanes] + dk3[BLK:2 * BLK]
                dv_ref[:, lanes] = dv_carry[:, lanes] + dv3[BLK:2 * BLK]
                dk_carry[:, lanes] = dk3[2 * BLK:3 * BLK]
                dv_carry[:, lanes] = dv3[2 * BLK:3 * BLK]

        @pl.when(i == last + 1)
        def _():
            dk_ref[...] = dk_carry[...]
            dv_ref[...] = dv_carry[...]

    cur = lambda i: (jnp.minimum(i, last), 0)
    prev = lambda i: (jnp.clip(i - 1, 0, last), 0)
    kv_meta = pl.BlockSpec((BLK, D_KV), lambda i: (0, 0))
    kv_prev = pl.BlockSpec((BLK, D_KV), prev)
    kv_cur = pl.BlockSpec((BLK, D_KV), cur)
    return _pcall(
        body, name=f"attn_bwd_l{layer}", grid=(n_blk + 1,),
        in_specs=[pl.BlockSpec(memory_space=pltpu.SMEM),
                  pl.BlockSpec((BLK, D_ATTN), cur),
                  kv_meta, kv_prev, kv_cur, kv_meta, kv_prev, kv_cur,
                  pl.BlockSpec((BLK, D_ATTN), cur)],
        out_specs=[pl.BlockSpec((BLK, D_ATTN), cur), kv_prev, kv_prev, kv_meta, kv_meta,
                   pl.BlockSpec((N_Q_HEADS, 128), lambda i: (0, 0))],
        out_shape=[SDS((rows, D_ATTN), F32), SDS((rows, D_KV), F32), SDS((rows, D_KV), F32),
                   SDS((BLK, D_KV), F32), SDS((BLK, D_KV), F32), SDS((N_Q_HEADS, 128), F32)],
        scratch_shapes=[pltpu.VMEM((BLK, D_KV), F32), pltpu.VMEM((BLK, D_KV), F32)],
        compiler_params=_cparams("arbitrary"),
    )(sinks, q, k, k, k, v, v, v, d_out)


def _rope_bwd(dq, dk, dv, dk_meta, dv_meta, cos, sin_a, sin_b, layer):
    rows = dq.shape[0]
    tm = _row_tile(rows)

    def body(dq_ref, dk_ref, dv_ref, dkm_ref, dvm_ref, c_ref, a_ref, b_ref, o_ref):
        c, a, b = c_ref[...], -a_ref[...], -b_ref[...]
        for t in range(8):
            x = dq_ref[:, t * 128:(t + 1) * 128]
            o_ref[:, t * 128:(t + 1) * 128] = (_rope_lanes(x, c, a, b) * ATTN_SCALE).astype(MXU_DTYPE)
        for t in range(2):
            x = dk_ref[:, t * 128:(t + 1) * 128]
            o_ref[:, D_ATTN + t * 128:D_ATTN + (t + 1) * 128] = _rope_lanes(x, c, a, b).astype(MXU_DTYPE)
        o_ref[:, D_ATTN + D_KV:] = dv_ref[...].astype(MXU_DTYPE)

        @pl.when(pl.program_id(0) == 0)
        def _():
            cb, ab, bb = c[0:BLK], a[0:BLK], b[0:BLK]
            is_meta = lax.broadcasted_iota(jnp.int32, (BLK, 128), 0) >= PAD_ROWS
            for t in range(2):
                x = dk_ref[0:BLK, t * 128:(t + 1) * 128] + jnp.where(is_meta, dkm_ref[:, t * 128:(t + 1) * 128], 0.0)
                o_ref[0:BLK, D_ATTN + t * 128:D_ATTN + (t + 1) * 128] = _rope_lanes(x, cb, ab, bb).astype(MXU_DTYPE)
                xv = dv_ref[0:BLK, t * 128:(t + 1) * 128] + jnp.where(is_meta, dvm_ref[:, t * 128:(t + 1) * 128], 0.0)
                o_ref[0:BLK, D_ATTN + D_KV + t * 128:D_ATTN + D_KV + (t + 1) * 128] = xv.astype(MXU_DTYPE)

    tab = pl.BlockSpec((tm, 128), lambda i: (i, 0))
    kv = pl.BlockSpec((tm, D_KV), lambda i: (i, 0))
    meta = pl.BlockSpec((BLK, D_KV), lambda i: (0, 0))
    return _pcall(
        body, name=f"rope_bwd_l{layer}", grid=(rows // tm,),
        in_specs=[pl.BlockSpec((tm, D_ATTN), lambda i: (i, 0)), kv, kv, meta, meta, tab, tab, tab],
        out_specs=pl.BlockSpec((tm, D_ATTN + 2 * D_KV), lambda i: (i, 0)),
        out_shape=SDS((rows, D_ATTN + 2 * D_KV), MXU_DTYPE),
        compiler_params=_cparams("parallel"),
    )(dq, dk, dv, dk_meta, dv_meta, cos, sin_a, sin_b)


def _s5_bwd(d_gated, y, proj, carry_in, ssm, w_glu, b_glu3, layer):
    rows = y.shape[0]
    n_chunks = rows // BLK
    b_mat, c_mat, t_re, t_im, d_skip = (ssm[k] for k in ("b_mat", "c_mat", "t_re", "t_im", "d_skip"))

    def body(dz_ref, y_ref, u_ref, cin_ref, bm_ref, cm_ref, tre_ref, tim_ref, d_ref, wg_ref, bg_ref,
             du_ref, dwg_ref, dbg_ref, dd_ref, dbm_ref, dcm_ref, dab_ref,
             lam_carry, bu_scr, s_scr, sp_scr, g_scr, lam_scr):
        step = pl.program_id(0)
        chunk = n_chunks - 1 - step

        @pl.when(step == 0)
        def _():
            for r in (dwg_ref, dbg_ref, dd_ref, dbm_ref, dcm_ref, dab_ref, lam_carry):
                r[...] = jnp.zeros_like(r)

        y = y_ref[...]
        u = u_ref[...]
        d_o = dz_ref[...]
        z, t = _gelu_parts(y)
        zb = z.astype(MXU_DTYPE)
        sg = _sigmoid(_dot(zb, wg_ref[...]) + bg_ref[...])
        dgl = d_o * z * (sg * (1.0 - sg))
        dglb = dgl.astype(MXU_DTYPE)
        dz = d_o * sg + _dot_nt(dglb, wg_ref[...])
        dwg_ref[...] += _dot_tn(zb, dglb)
        dbg_ref[...] += jnp.sum(dgl, axis=0, keepdims=True)
        dy = dz * _gelu_grad(y, t)
        dd_ref[...] += jnp.sum(dy * u, axis=0, keepdims=True)
        grow = lax.broadcasted_iota(jnp.int32, (BLK, 128), 0) + chunk * BLK
        for sb in range(N_SB):
            cols = slice(sb * 128, (sb + 1) * 128)
            u_sb = u[:, cols].astype(MXU_DTYPE)
            dy_sb = dy[:, cols]
            dyb = dy_sb.astype(MXU_DTYPE)
            bu_scr[sb] = _dot(u_sb, bm_ref[sb])
            _scan_tiles(bu_scr.at[sb], s_scr.at[sb], tre_ref, tim_ref, sb,
                        cin_ref[2 * sb:2 * sb + 1, :], cin_ref[2 * sb + 1:2 * sb + 2, :], False, prev_ref=sp_scr.at[sb])
            dcm_ref[sb] += _dot_tn(s_scr[sb].astype(MXU_DTYPE), dyb)
            g_scr[sb] = _dot_nt(dyb, cm_ref[sb])
            n_r, n_i = _scan_tiles(g_scr.at[sb], lam_scr.at[sb], tre_ref, tim_ref, sb,
                                   lam_carry[2 * sb:2 * sb + 1, :], lam_carry[2 * sb + 1:2 * sb + 2, :], True)
            lam_carry[2 * sb:2 * sb + 1, :] = n_r
            lam_carry[2 * sb + 1:2 * sb + 2, :] = n_i
            lr, li = lam_scr[sb, :, :SB_STATES], lam_scr[sb, :, SB_STATES:]
            spr, spi = sp_scr[sb, :, :SB_STATES], sp_scr[sb, :, SB_STATES:]
            dab_ref[2 * sb:2 * sb + 1, :] += jnp.sum(spr * lr + spi * li, axis=0, keepdims=True)
            dab_ref[2 * sb + 1:2 * sb + 2, :] += jnp.sum(spr * li - spi * lr, axis=0, keepdims=True)
            lam = lam_scr[sb].astype(MXU_DTYPE)
            dbm_ref[sb] += _dot_tn(u_sb, lam)
            du = _dot_nt(lam, bm_ref[sb]) + d_ref[:, cols] * dy_sb
            du_ref[:, cols] = jnp.where(grow >= PAD_ROWS, du, 0.0).astype(MXU_DTYPE)

    rev = lambda j: (n_chunks - 1 - j, 0)
    full = lambda shape: pl.BlockSpec(shape, lambda j: (0,) * len(shape))
    tables = [full((N_SB, 8, SCAN_TILE, SB_STATES))] * 2
    chunk_scratch = pltpu.VMEM((N_SB, BLK, 2 * SB_STATES), F32)
    return _pcall(
        body, name=f"s5_bwd_l{layer}", grid=(n_chunks,),
        in_specs=[pl.BlockSpec((BLK, D_SSM), rev), pl.BlockSpec((BLK, D_SSM), rev), pl.BlockSpec((BLK, D_SSM), rev),
                  pl.BlockSpec((None, 8, SB_STATES), lambda j: (n_chunks - 1 - j, 0, 0)),
                  full((N_SB, 128, 2 * SB_STATES)), full((N_SB, 2 * SB_STATES, 128))] + tables + [
                  full((1, D_SSM)), full((D_SSM, D_SSM)),
                  pl.BlockSpec((None, 1, D_SSM), lambda j: (layer, 0, 0))],
        out_specs=[pl.BlockSpec((BLK, D_SSM), rev), full((D_SSM, D_SSM)), full((1, D_SSM)), full((1, D_SSM)),
                   full((N_SB, 128, 2 * SB_STATES)), full((N_SB, 2 * SB_STATES, 128)), full((8, SB_STATES))],
        out_shape=[SDS((rows, D_SSM), MXU_DTYPE), SDS((D_SSM, D_SSM), F32), SDS((1, D_SSM), F32), SDS((1, D_SSM), F32),
                   SDS((N_SB, 128, 2 * SB_STATES), F32), SDS((N_SB, 2 * SB_STATES, 128), F32), SDS((8, SB_STATES), F32)],
        scratch_shapes=[pltpu.VMEM((8, SB_STATES), F32)] + [chunk_scratch] * 5,
        compiler_params=_cparams("arbitrary"),
    )(d_gated, y, proj, carry_in, b_mat, c_mat, t_re, t_im, d_skip, w_glu, b_glu3)


DPROJ_PIECES = ((0, 1), (1, 3), (4, 2), (6, 2))


def _in_bwd(dproj_pieces, dhm, hres, gain3, w_in_g, layer):
    rows = hres.shape[0]
    tm = _row_tile(rows)

    def body(du_ref, dqkv_ref, dgs_ref, dga_ref, dh_ref, x_ref, g_ref, w_hbm, dx_ref, dg_ref, acc, w_scr, w_sem):
        i = pl.program_id(0)
        j = pl.program_id(1)
        _load_resident(w_hbm, w_scr, w_sem, (i == 0) & (j == 0))

        @pl.when((i == 0) & (j == 0))
        def _():
            dg_ref[...] = jnp.zeros_like(dg_ref)

        @pl.when(j == 0)
        def _():
            acc[...] = jnp.zeros_like(acc)

        for piece_ref, (first, count) in zip((du_ref, dqkv_ref, dgs_ref, dga_ref), DPROJ_PIECES):
            @pl.when((j >= first) & (j < first + count))
            def _():
                acc[...] += _dot_nt(piece_ref[...], w_scr[j])

        @pl.when(j == N_DEV - 1)
        def _():
            dx, dg = _rms_bwd(x_ref[...], g_ref[...], acc[...])
            dg_ref[...] += dg
            dx_ref[...] = dh_ref[...] + dx

    row_d = pl.BlockSpec((tm, D), lambda i, j: (i, 0))

    def piece_spec(first, count):
        return pl.BlockSpec((tm, COL_SHARD), lambda i, j: (i, jnp.clip(j - first, 0, count - 1)))

    return _pcall(
        body, name=f"in_bwd_l{layer}", grid=(rows // tm, N_DEV),
        in_specs=[piece_spec(*p) for p in DPROJ_PIECES] + [
                  row_d, row_d,
                  pl.BlockSpec((None, 1, D), lambda i, j: (layer, 0, 0)),
                  pl.BlockSpec(memory_space=pl.ANY)],
        out_specs=[row_d, pl.BlockSpec((1, D), lambda i, j: (0, 0))],
        out_shape=[SDS((rows, D), F32), SDS((1, D), F32)],
        scratch_shapes=[pltpu.VMEM((tm, D), F32), pltpu.VMEM((N_DEV, D, COL_SHARD), MXU_DTYPE),
                        pltpu.SemaphoreType.DMA((N_DEV,))],
        compiler_params=_cparams("arbitrary", "arbitrary"),
    )(*dproj_pieces, dhm, hres, gain3, w_in_g)


_ADAM_C1 = 1.0 / (1.0 - ADAM_B1 ** ADAM_STEP)
_ADAM_C2 = 1.0 / (1.0 - ADAM_B2 ** ADAM_STEP)


def _adam_math(w, g, m, v):
    m = ADAM_B1 * m + (1.0 - ADAM_B1) * g
    v = ADAM_B2 * v + (1.0 - ADAM_B2) * (g * g)
    delta = -ADAM_LR * ((m * _ADAM_C1) / (jnp.sqrt(v * _ADAM_C2) + ADAM_EPS) + ADAM_WD * w)
    return delta, m, v


def _adamw_layers(parts0, parts1, w, m, v, name):
    _, rows, cols = w.shape
    tr = min(rows, (1 << 16) // cols)
    nt = rows // tr

    def body(p0_ref, p1_ref, w_ref, m_ref, v_ref, g_ref, d_ref, nm_ref, nv_ref):
        layer = pl.program_id(0)

        def run(p_ref):
            g = p_ref[0].astype(F32)
            for s in range(1, N_DEV):
                g = g + p_ref[s].astype(F32)
            delta, nm, nv = _adam_math(w_ref[...], g, m_ref[...], v_ref[...])
            g_ref[...] = g
            d_ref[...] = delta
            nm_ref[...] = nm
            nv_ref[...] = nv

        @pl.when(layer == 0)
        def _():
            run(p0_ref)

        @pl.when(layer == 1)
        def _():
            run(p1_ref)

    wspec = pl.BlockSpec((None, tr, cols), lambda l, i: (l, i, 0))
    return _pcall(
        body, name=name, grid=(2, nt),
        in_specs=[pl.BlockSpec((N_DEV, tr, cols), lambda l, i: (0, jnp.where(l == 0, i, nt - 1), 0)),
                  pl.BlockSpec((N_DEV, tr, cols), lambda l, i: (0, jnp.where(l == 1, i, 0), 0)),
                  wspec, wspec, wspec],
        out_specs=[wspec] * 4, out_shape=[SDS(w.shape, F32)] * 4,
        compiler_params=_cparams("arbitrary", "arbitrary"),
    )(parts0, parts1, w, m, v)


def _adamw_packed(g, w, m, v, name):
    def body(g_ref, w_ref, m_ref, v_ref, d_ref, nm_ref, nv_ref):
        delta, nm, nv = _adam_math(w_ref[...], g_ref[...], m_ref[...], v_ref[...])
        d_ref[...] = delta
        nm_ref[...] = nm
        nv_ref[...] = nv

    vmem = pl.BlockSpec(memory_space=pltpu.VMEM)
    return _pcall(body, name=name, out_shape=[SDS(g.shape, F32)] * 3, in_specs=[vmem] * 4, out_specs=[vmem] * 3,
                  compiler_params=_cparams())(g, w, m, v)


def _ssm_discretize(a_re, a_im, log_dt, b_re, b_im):
    dt = jnp.exp(log_dt)[:, None]
    mag = jnp.exp(a_re * dt)
    ang = a_im * dt
    ab_re, ab_im = mag * jnp.cos(ang), mag * jnp.sin(ang)
    xr, xi = ab_re - 1.0, ab_im
    den = a_re * a_re + a_im * a_im
    q_re = (xr * a_re + xi * a_im) / den
    q_im = (xi * a_re - xr * a_im) / den
    bb_re = q_re[..., None] * b_re - q_im[..., None] * b_im
    bb_im = q_re[..., None] * b_im + q_im[..., None] * b_re
    return ab_re, ab_im, bb_re, bb_im


def _block_diag_b(bb):
    m = jnp.einsum("sgnc,gh->sgchn", bb.reshape(N_SB, 8, N_STATE, GROUP_CH), jnp.eye(8, dtype=F32))
    return m.reshape(N_SB, 128, SB_STATES)


def _block_diag_b_t(dm):
    return jnp.einsum("sgchn,gh->sgnc", dm.reshape(N_SB, 8, GROUP_CH, 8, N_STATE),
                      jnp.eye(8, dtype=F32)).reshape(N_GROUPS, N_STATE, GROUP_CH)


def _block_diag_c(cc):
    m = jnp.einsum("sgcn,gh->sgnhc", cc.reshape(N_SB, 8, GROUP_CH, N_STATE), jnp.eye(8, dtype=F32))
    return m.reshape(N_SB, SB_STATES, 128)


def _block_diag_c_t(dm):
    return jnp.einsum("sgnhc,gh->sgcn", dm.reshape(N_SB, 8, N_STATE, 8, GROUP_CH),
                      jnp.eye(8, dtype=F32)).reshape(N_GROUPS, GROUP_CH, N_STATE)


def _ssm_tables(ab_re, ab_im, bb_re, bb_im, c_re, c_im, d_skip):
    pr, pi = ab_re.reshape(1, -1), ab_im.reshape(1, -1)
    cr, ci = pr, pi
    squares = []
    for _ in range(3):
        squares.append((cr, ci))
        pr, pi = (jnp.concatenate([pr, pr * cr - pi * ci], axis=0),
                  jnp.concatenate([pi, pr * ci + pi * cr], axis=0))
        cr, ci = cr * cr - ci * ci, 2.0 * cr * ci
    r = jnp.arange(SCAN_TILE)[:, None]
    fwd = [(jnp.where(r >= (1 << k), squares[k][0], 0.0), jnp.where(r >= (1 << k), squares[k][1], 0.0))
           for k in range(3)] + [(pr, pi)]
    rev = [(jnp.where(r < SCAN_TILE - (1 << k), squares[k][0], 0.0),
            jnp.where(r < SCAN_TILE - (1 << k), -squares[k][1], 0.0)) for k in range(3)] + [(pr[::-1], -pi[::-1])]
    table = lambda part: jnp.stack([e[part] for e in fwd + rev]).reshape(
        8, SCAN_TILE, N_SB, SB_STATES).transpose(2, 0, 1, 3)
    return dict(
        b_mat=jnp.concatenate([_block_diag_b(bb_re), _block_diag_b(bb_im)], axis=-1).astype(MXU_DTYPE),
        c_mat=jnp.concatenate([_block_diag_c(c_re), -_block_diag_c(c_im)], axis=1).astype(MXU_DTYPE),
        t_re=table(0), t_im=table(1),
        d_skip=d_skip.reshape(1, D_SSM))


def _rope_tables(rows):
    pos = (jnp.arange(rows, dtype=jnp.int32) - PAD_ROWS).astype(F32)
    inv_freq = 1.0 / (ROPE_THETA ** (jnp.arange(0, HEAD_DIM, 2, dtype=F32) / HEAD_DIM))
    ang = pos[:, None] * inv_freq[None, :]
    ang = jnp.concatenate([ang, ang, ang, ang], axis=-1)
    first_half = (jnp.arange(128) % HEAD_DIM) < HEAD_DIM // 2
    sin = jnp.sin(ang)
    return jnp.cos(ang), jnp.where(first_half, -sin, 0.0), jnp.where(first_half, 0.0, sin)


def _pack(arrays):
    flat = jnp.concatenate([a.reshape(-1).astype(F32) for a in arrays])
    pad = (-flat.shape[0]) % 1024
    return jnp.pad(flat, (0, pad)).reshape(-1, 128)


def _unpack(packed, like):
    flat = packed.reshape(-1)
    out, off = [], 0
    for a in like:
        n = math.prod(a.shape)
        out.append(flat[off:off + n].reshape(a.shape))
        off += n
    return out


BIG = ("w_in", "w_glu", "w_o_ssm", "w_o_attn", "w_out", "w_up", "w_down")
WEIGHTS = ("meta_tokens", "norm_mix_pre", "norm_mix_post", "norm_mlp_pre", "norm_mlp_post", "w_in",
           "ssm_a_re", "ssm_a_im", "ssm_log_dt", "ssm_b_re", "ssm_b_im", "ssm_c_re", "ssm_c_im", "ssm_d",
           "w_glu", "b_glu", "attn_sinks", "w_o_ssm", "w_o_attn", "w_out", "w_up", "w_down")
SMALL = tuple(n for n in WEIGHTS if n not in BIG)


def kernel(x, meta_tokens, norm_mix_pre, norm_mix_post, norm_mlp_pre, norm_mlp_post, w_in, ssm_a_re, ssm_a_im, ssm_log_dt, ssm_b_re, ssm_b_im, ssm_c_re, ssm_c_im, ssm_d, w_glu, b_glu, attn_sinks, w_o_ssm, w_o_attn, w_out, w_up, w_down, loss_target, m_meta_tokens, m_norm_mix_pre, m_norm_mix_post, m_norm_mlp_pre, m_norm_mlp_post, m_w_in, m_ssm_a_re, m_ssm_a_im, m_ssm_log_dt, m_ssm_b_re, m_ssm_b_im, m_ssm_c_re, m_ssm_c_im, m_ssm_d, m_w_glu, m_b_glu, m_attn_sinks, m_w_o_ssm, m_w_o_attn, m_w_out, m_w_up, m_w_down, v_meta_tokens, v_norm_mix_pre, v_norm_mix_post, v_norm_mlp_pre, v_norm_mlp_post, v_w_in, v_ssm_a_re, v_ssm_a_im, v_ssm_log_dt, v_ssm_b_re, v_ssm_b_im, v_ssm_c_re, v_ssm_c_im, v_ssm_d, v_w_glu, v_b_glu, v_attn_sinks, v_w_o_ssm, v_w_o_attn, v_w_out, v_w_up, v_w_down):
    args = locals()
    w = {n: args[n] for n in WEIGHTS}
    m = {n: args["m_" + n] for n in WEIGHTS}
    v = {n: args["v_" + n] for n in WEIGHTS}
    n_layers = w_in.shape[0]
    seq = x.shape[1]
    rows = seq + BLK
    my_slot = _slot(_mesh_pos())

    assert n_layers == 2
    xfer = {n: w[n].astype(XFER_DTYPE) for n in BIG}
    mixer_small = ("w_glu", "w_o_ssm", "w_o_attn", "w_out")
    gather_in0 = _exchange_start([meta_tokens, xfer["w_in"][0]], True, "gather_in0_start")
    gather_mix0 = _exchange_start([xfer[n][0] for n in mixer_small], True, "gather_mix0_start")
    gather_mlp0 = _exchange_start([xfer["w_up"][0], xfer["w_down"][0]], True, "gather_mlp0_start")
    gather_l1 = _exchange_start([xfer[n][1] for n in ("w_in",) + mixer_small + ("w_up", "w_down")], True,
                                "gather_l1_start")
    meta_g, w_in_g0 = _exchange_wait(gather_in0, [gather_mix0["token"], gather_mlp0["token"], gather_l1["token"]],
                                     "gather_in0_wait")
    meta_full = meta_g.transpose(1, 0, 2).reshape(N_META, D)

    def mixer_weights(w_glu_g, w_o_ssm_g, w_o_attn_g, w_out_g):
        return dict(w_glu=w_glu_g.reshape(D_SSM, D_SSM), w_o_ssm=w_o_ssm_g.transpose(1, 0, 2).reshape(D_SSM, D),
                    w_o_attn=w_o_attn_g.reshape(D_ATTN, D), w_out=w_out_g.reshape(D, D))

    gathered = [dict(w_in=w_in_g0), {}]

    gains = {n: w[n].reshape(n_layers, 1, D) for n in ("norm_mix_pre", "norm_mix_post", "norm_mlp_pre", "norm_mlp_post")}
    b_glu3 = b_glu.reshape(n_layers, 1, D_SSM)
    cos, sin_a, sin_b = _rope_tables(rows)

    def ssm_setup(l):
        disc, disc_vjp = jax.vjp(_ssm_discretize, ssm_a_re[l], ssm_a_im[l], ssm_log_dt[l], ssm_b_re[l], ssm_b_im[l])
        return _ssm_tables(*disc, ssm_c_re[l], ssm_c_im[l], ssm_d[l]), disc_vjp

    hres = jnp.concatenate([jnp.zeros((PAD_ROWS, D), F32), meta_full, x[0]], axis=0)

    saved = []
    for l in range(n_layers):
        ssm, disc_vjp = ssm_setup(l)
        wl = gathered[l]
        proj, h = _in_proj(hres, gains["norm_mix_pre"], wl["w_in"], l)
        q, k, vv = _rope_fwd(proj, cos, sin_a, sin_b, l)
        if l == 0:
            wl.update(mixer_weights(*_exchange_wait(gather_mix0, [q], "gather_mix0_wait")))
        y, y_ssm, carry_in = _s5_fwd(proj, ssm, wl["w_glu"], b_glu3, l)
        y_attn = _attn_fwd(q, k, vv, attn_sinks, l)
        merged, mix, hres_mid = _merge_fwd(y_ssm, y_attn, proj, hres, wl["w_o_ssm"], wl["w_o_attn"], wl["w_out"],
                                           gains["norm_mix_post"], l)
        if l == 0:
            wl["w_up"], wl["w_down"] = _exchange_wait(gather_mlp0, [hres_mid], "gather_mlp0_wait")
        up, h2, ff, hres_out = _mlp_fwd(hres_mid, gains["norm_mlp_pre"], gains["norm_mlp_post"], wl["w_up"],
                                        wl["w_down"], l)
        if l == 0:
            got = _exchange_wait(gather_l1, [hres_out], "gather_l1_wait")
            gathered[1] = dict(w_in=got[0], w_up=got[5], w_down=got[6], **mixer_weights(*got[1:5]))
        saved.append(dict(ssm=ssm, disc_vjp=disc_vjp, hres=hres, proj=proj, h=h, q=q, k=k, v=vv, y=y, y_ssm=y_ssm,
                          carry_in=carry_in, y_attn=y_attn, merged=merged, mix=mix, hres_mid=hres_mid,
                          up=up, h2=h2, ff=ff))
        hres = hres_out

    dhres, loss_vec = _loss_and_grad(hres, loss_target[0])
    loss = lax.psum(loss_vec[0, 0], MESH_AXES)

    small_grads = {n: [None] * n_layers for n in SMALL if n != "meta_tokens"}
    scatter_mlp, scatter_mix = [None] * n_layers, [None] * n_layers
    order_token = jnp.zeros((), F32)
    for l in reversed(range(n_layers)):
        s = saved[l]
        wl = gathered[l]
        dff, dup, dhm, dg_mlp_post, dg_mlp_pre = _mlp_bwd(dhres, s["ff"], s["up"], s["hres_mid"],
                                                          gains["norm_mlp_pre"] + order_token,
                                                          gains["norm_mlp_post"], wl["w_up"], wl["w_down"], l)
        dw_down = _matmul_tn(s["up"], dff, f"dw_down_l{l}", a_fn=_relu_squared).reshape(N_DEV, COL_SHARD, D)
        dw_up = _matmul_tn(s["h2"], dup, f"dw_up_l{l}", dev_major_cols=COL_SHARD)
        scatter_mlp[l] = _exchange_start([dw_up, dw_down], False, f"scatter_mlp{l}_start")
        dmix, da1, da2, dgs, dga, dy_ssm, dy_attn, dg_mix_post = _merge_bwd(
            dhm, s["mix"], s["y_ssm"], s["y_attn"], s["proj"], wl["w_o_ssm"], wl["w_o_attn"], wl["w_out"],
            gains["norm_mix_post"] + scatter_mlp[l]["token"][0, 0], l)
        dw_out = _matmul_tn(s["merged"], dmix, f"dw_out_l{l}").reshape(N_DEV, D // N_DEV, D)
        dw_o_attn = _matmul_tn(s["y_attn"], da2, f"dw_o_attn_l{l}").reshape(N_DEV, D_ATTN // N_DEV, D)
        dw_o_ssm = _matmul_tn(s["y_ssm"], da1, f"dw_o_ssm_l{l}", dev_major_cols=D // N_DEV)
        dq, dk, dv, dk_meta, dv_meta, dsink = _attn_bwd(s["q"], s["k"], s["v"], dy_attn, attn_sinks, l)
        dqkv = _rope_bwd(dq, dk, dv, dk_meta, dv_meta, cos, sin_a, sin_b, l)
        du, dw_glu, db_glu, dd_skip, db_mat, dc_mat, dab = _s5_bwd(dy_ssm, s["y"], s["proj"], s["carry_in"], s["ssm"],
                                                                    wl["w_glu"], b_glu3, l)
        dproj = (du, dqkv, dgs, dga)
        dw_in = jnp.concatenate([_matmul_tn(s["h"], piece, f"dw_in{k}_l{l}", dev_major_cols=COL_SHARD)
                                 for k, piece in enumerate(dproj)], axis=0)
        dhres, dg_mix_pre = _in_bwd(dproj, dhm, s["hres"], gains["norm_mix_pre"], wl["w_in"], l)
        scatter_mix[l] = _exchange_start([dw_in, dw_glu.astype(XFER_DTYPE).reshape(N_DEV, D_SSM // N_DEV, D_SSM),
                                          dw_o_ssm, dw_o_attn, dw_out], False, f"scatter_mix{l}_start")
        order_token = scatter_mix[l]["token"][0, 0]

        dab = dab.reshape(N_SB, 2, SB_STATES)
        da_re, da_im, dlog_dt, db_re, db_im = s["disc_vjp"]((
            dab[:, 0].reshape(N_GROUPS, N_STATE), dab[:, 1].reshape(N_GROUPS, N_STATE),
            _block_diag_b_t(db_mat[..., :SB_STATES]), _block_diag_b_t(db_mat[..., SB_STATES:])))
        for name, val in (("norm_mix_pre", dg_mix_pre[0]), ("norm_mix_post", dg_mix_post[0]),
                          ("norm_mlp_pre", dg_mlp_pre[0]), ("norm_mlp_post", dg_mlp_post[0]),
                          ("ssm_a_re", da_re), ("ssm_a_im", da_im), ("ssm_log_dt", dlog_dt),
                          ("ssm_b_re", db_re), ("ssm_b_im", db_im),
                          ("ssm_c_re", _block_diag_c_t(dc_mat[:, :SB_STATES])),
                          ("ssm_c_im", -_block_diag_c_t(dc_mat[:, SB_STATES:])),
                          ("ssm_d", dd_skip.reshape(N_GROUPS, GROUP_CH)), ("b_glu", db_glu[0]),
                          ("attn_sinks", dsink[:, 0])):
            small_grads[name][l] = val

    grad_x = dhres[BLK:][None]
    small_names = [n for n in SMALL if n != "meta_tokens"]
    partial_small = [dhres[PAD_ROWS:BLK]] + [jnp.stack(small_grads[n]) for n in small_names]
    summed_packed = _all_reduce_small(_pack(partial_small) + order_token, "reduce_small_grads")
    summed = _unpack(summed_packed, partial_small)
    grads = dict(zip(small_names, summed[1:]))
    grads["meta_tokens"] = lax.dynamic_slice_in_dim(summed[0], my_slot * (D // N_DEV), D // N_DEV, axis=1)

    delta, new_m, new_v = {}, {}, {}

    def adamw_big(names, recv0, recv1):
        for n, p0, p1 in zip(names, recv0, recv1):
            grads[n], delta[n], new_m[n], new_v[n] = _adamw_layers(p0, p1, w[n], m[n], v[n], f"adamw_{n}")

    recv_mlp1 = _exchange_wait(scatter_mlp[1], [summed_packed], "scatter_mlp1_wait")
    recv_mix1 = _exchange_wait(scatter_mix[1], [recv_mlp1[0]], "scatter_mix1_wait")
    recv_mlp0 = _exchange_wait(scatter_mlp[0], [recv_mix1[0]], "scatter_mlp0_wait")
    adamw_big(("w_up", "w_down"), recv_mlp0, recv_mlp1)
    recv_mix0 = _exchange_wait(scatter_mix[0], [delta["w_down"]], "scatter_mix0_wait")
    adamw_big(("w_in",) + mixer_small, recv_mix0, recv_mix1)
    like = [w[n] for n in SMALL]
    d_s, m_s, v_s = _adamw_packed(_pack([grads[n] for n in SMALL]), _pack(like), _pack([m[n] for n in SMALL]),
                                  _pack([v[n] for n in SMALL]), "adamw_small")
    for n, dd, mm, vs in zip(SMALL, _unpack(d_s, like), _unpack(m_s, like), _unpack(v_s, like)):
        delta[n], new_m[n], new_v[n] = dd, mm, vs

    return (loss, grad_x, *[grads[n] for n in WEIGHTS], *[delta[n] for n in WEIGHTS],
            *[new_m[n] for n in WEIGHTS], *[new_v[n] for n in WEIGHTS])
```

```python
import functools
import math

import jax
import jax.numpy as jnp
from jax import lax
from jax.experimental import pallas as pl
from jax.experimental.pallas import tpu as pltpu

F32 = jnp.float32
MXU_DTYPE = jnp.bfloat16
XFER_DTYPE = MXU_DTYPE
_pcall = pl.pallas_call
SDS = jax.ShapeDtypeStruct

D = 1024
D_SSM = 512
D_ATTN = 1024
D_KV = 256
D_FF = 4096
D_IN = 4096
HEAD_DIM = 64
N_Q_HEADS = 16
N_KV_HEADS = 4
Q_PER_KV = 4
N_META = 16
BLK = 128
PAD_ROWS = BLK - N_META
N_GROUPS = 32
N_STATE = 64
GROUP_CH = 16
N_SB = 4
SB_STATES = 512
ROPE_THETA = 10000.0
ATTN_SCALE = HEAD_DIM ** -0.5
NEG_INF = -1e30
RMS_EPS = 1e-6
N_DEV = 8
COL_SHARD = 512

ADAM_LR = 0.001
ADAM_B1 = 0.9
ADAM_B2 = 0.999
ADAM_EPS = 1e-08
ADAM_WD = 0.01
ADAM_STEP = 10

VMEM_LIMIT = 56 * 1024 * 1024
MESH_AXES = ("x", "y", "c")

_NT = (((1,), (1,)), ((), ()))
_TN = (((0,), (0,)), ((), ()))


def _cparams(*sem):
    return pltpu.CompilerParams(dimension_semantics=tuple(sem) if sem else None,
                                vmem_limit_bytes=VMEM_LIMIT)


def _row_tile(rows, cap=640):
    for t in (640, 512, 320, 256, 128):
        if t <= cap and rows % t == 0:
            return t
    raise ValueError(f"unsupported row count {rows}")


def _dot(a, b):
    return jnp.dot(a, b, preferred_element_type=F32)


def _dot_nt(a, b):
    return lax.dot_general(a, b, _NT, preferred_element_type=F32)


def _dot_tn(a, b):
    return lax.dot_general(a, b, _TN, preferred_element_type=F32)


def _sigmoid(x):
    return 1.0 / (1.0 + jnp.exp(-x))


_GELU_C = math.sqrt(2.0 / math.pi)


def _gelu_parts(y):
    t = jnp.tanh(_GELU_C * (y + 0.044715 * (y * y * y)))
    return 0.5 * y * (1.0 + t), t


def _gelu_grad(y, t):
    return 0.5 * (1.0 + t) + 0.5 * y * (1.0 - t * t) * (_GELU_C * (1.0 + 0.134145 * (y * y)))


def _rms_fwd(x, gain):
    r = lax.rsqrt(jnp.mean(x * x, axis=-1, keepdims=True) + RMS_EPS)
    return (x * r) * gain


def _rms_bwd(x, gain, dout):
    r = lax.rsqrt(jnp.mean(x * x, axis=-1, keepdims=True) + RMS_EPS)
    xh = x * r
    dxh = dout * gain
    dx = r * (dxh - xh * jnp.mean(dxh * xh, axis=-1, keepdims=True))
    return dx, jnp.sum(dout * xh, axis=0, keepdims=True)


def _mesh_pos():
    return lax.axis_index("x"), lax.axis_index("y"), lax.axis_index("c")


def _peer(pos, d):
    x, y, c = pos
    return (1 - x if d & 4 else x, 1 - y if d & 2 else y, 1 - c if d & 1 else c)


def _slot(pos):
    return 4 * pos[0] + 2 * pos[1] + pos[2]


_HBM = pl.BlockSpec(memory_space=pltpu.HBM)
_SEM = pl.BlockSpec(memory_space=pltpu.SEMAPHORE)
_DATAFLOW = pltpu.SideEffectType.DATAFLOW_SIDE_EFFECTING


def _exchange_copy(gather, src_ref, land_ref, sems, k, d, me, send_side):
    peer = _peer(me, d)
    sender = me if send_side else peer
    src = src_ref if gather else src_ref.at[_slot(peer) if send_side else _slot(me)]
    return pltpu.make_async_remote_copy(
        src_ref=src, dst_ref=land_ref.at[_slot(sender)],
        send_sem=sems[0].at[k * (N_DEV - 1) + d - 1], recv_sem=sems[1].at[k * (N_DEV - 1) + d - 1],
        device_id=peer, device_id_type=pl.DeviceIdType.MESH)


def _exchange_start(srcs, gather, name):
    n = len(srcs)
    lands = [lax.empty(((N_DEV,) + s.shape) if gather else s.shape, s.dtype) for s in srcs]

    def body(*refs):
        src_refs, land_refs = refs[:n], refs[n:2 * n]
        sems = refs[2 * n:2 * n + 2]
        token, local_sems = refs[4 * n + 2], refs[4 * n + 3]
        me = _mesh_pos()
        own = [pltpu.make_async_copy(src_refs[k] if gather else src_refs[k].at[_slot(me)],
                                     land_refs[k].at[_slot(me)], local_sems.at[k]) for k in range(n)]
        for cp in own:
            cp.start()
        for cp in own:
            cp.wait()
        for k in range(n):
            for d in range(1, N_DEV):
                _exchange_copy(gather, src_refs[k], land_refs[k], sems, k, d, me, True).start()
        token[...] = jnp.zeros_like(token)

    sem_type = pltpu.SemaphoreType.DMA((n * (N_DEV - 1),))
    outs = _pcall(
        body, name=name,
        out_shape=(sem_type, sem_type, *[pltpu.HBM(a.shape, a.dtype) for a in list(srcs) + lands], SDS((8, 128), F32)),
        in_specs=[_HBM] * (2 * n),
        out_specs=(_SEM, _SEM, *[_HBM] * (2 * n), pl.BlockSpec(memory_space=pltpu.VMEM)),
        input_output_aliases={k: 2 + k for k in range(2 * n)},
        scratch_shapes=[pltpu.SemaphoreType.DMA((n,))],
        compiler_params=pltpu.CompilerParams(has_side_effects=_DATAFLOW),
    )(*[pltpu.with_memory_space_constraint(a, pltpu.HBM) for a in list(srcs) + lands])
    return dict(sems=outs[:2], srcs=outs[2:2 + n], lands=outs[2 + n:2 + 2 * n], token=outs[-1], gather=gather)


def _exchange_wait(started, after, name):
    n = len(started["srcs"])
    gather = started["gather"]

    def body(*refs):
        src_refs, land_refs = refs[:n], refs[n:2 * n]
        sems = refs[2 * n:2 * n + 2]
        me = _mesh_pos()
        for k in range(n):
            for d in range(1, N_DEV):
                _exchange_copy(gather, src_refs[k], land_refs[k], sems, k, d, me, True).wait_send()
        for k in range(n):
            for d in range(1, N_DEV):
                _exchange_copy(gather, src_refs[k], land_refs[k], sems, k, d, me, False).wait_recv()

    arrays = list(started["srcs"]) + list(started["lands"])
    outs = _pcall(
        body, name=name,
        out_shape=tuple(pltpu.HBM(a.shape, a.dtype) for a in arrays),
        in_specs=[_HBM] * (2 * n) + [_SEM, _SEM] + [pl.BlockSpec(memory_space=pl.ANY)] * len(after),
        out_specs=tuple([_HBM] * (2 * n)),
        input_output_aliases={k: k for k in range(2 * n)},
        compiler_params=pltpu.CompilerParams(has_side_effects=_DATAFLOW),
    )(*arrays, *started["sems"], *after)
    return list(outs[n:])


def _load_resident(w_hbm, w_scr, sems, first_step):
    @pl.when(first_step)
    def _():
        copies = [pltpu.make_async_copy(w_hbm.at[s], w_scr.at[s], sems.at[s]) for s in range(N_DEV)]
        for cp in copies:
            cp.start()
        for cp in copies:
            cp.wait()


def _in_proj(hres, gain3, w_in_g, layer):
    rows = hres.shape[0]
    tm = _row_tile(rows)

    def body(x_ref, g_ref, w_hbm, proj_ref, h_ref, h_scr, w_scr, w_sem):
        j = pl.program_id(1)
        _load_resident(w_hbm, w_scr, w_sem, (pl.program_id(0) == 0) & (j == 0))

        @pl.when(j == 0)
        def _():
            hn = _rms_fwd(x_ref[...], g_ref[...]).astype(MXU_DTYPE)
            h_scr[...] = hn
            h_ref[...] = hn

        proj_ref[...] = _dot(h_scr[...], w_scr[j])

    return _pcall(
        body, name=f"in_proj_l{layer}", grid=(rows // tm, N_DEV),
        in_specs=[pl.BlockSpec((tm, D), lambda i, j: (i, 0)),
                  pl.BlockSpec((None, 1, D), lambda i, j: (layer, 0, 0)),
                  pl.BlockSpec(memory_space=pl.ANY)],
        out_specs=[pl.BlockSpec((tm, COL_SHARD), lambda i, j: (i, j)),
                   pl.BlockSpec((tm, D), lambda i, j: (i, 0))],
        out_shape=[SDS((rows, D_IN), F32), SDS((rows, D), MXU_DTYPE)],
        scratch_shapes=[pltpu.VMEM((tm, D), MXU_DTYPE), pltpu.VMEM((N_DEV, D, COL_SHARD), MXU_DTYPE),
                        pltpu.SemaphoreType.DMA((N_DEV,))],
        compiler_params=_cparams("arbitrary", "arbitrary"),
    )(hres, gain3, w_in_g)


def _rope_lanes(t, cos, sin_a, sin_b):
    return t * cos + pltpu.roll(t, 96, 1) * sin_a + pltpu.roll(t, 32, 1) * sin_b


def _rope_fwd(proj, cos, sin_a, sin_b, layer):
    rows = proj.shape[0]
    tm = _row_tile(rows)

    def body(q0_ref, q1_ref, kv_ref, c_ref, a_ref, b_ref, qo_ref, ko_ref, vo_ref):
        c, a, b = c_ref[...], a_ref[...], b_ref[...]
        for half, q_ref in enumerate((q0_ref, q1_ref)):
            for t in range(4):
                x = q_ref[:, t * 128:(t + 1) * 128]
                lo = half * 512 + t * 128
                qo_ref[:, lo:lo + 128] = (_rope_lanes(x, c, a, b) * ATTN_SCALE).astype(MXU_DTYPE)
        for t in range(2):
            x = kv_ref[:, t * 128:(t + 1) * 128]
            ko_ref[:, t * 128:(t + 1) * 128] = _rope_lanes(x, c, a, b).astype(MXU_DTYPE)
        vo_ref[...] = kv_ref[:, D_KV:2 * D_KV].astype(MXU_DTYPE)

    tab = pl.BlockSpec((tm, 128), lambda i: (i, 0))
    return _pcall(
        body, name=f"rope_fwd_l{layer}", grid=(rows // tm,),
        in_specs=[pl.BlockSpec((tm, 512), lambda i: (i, 1)), pl.BlockSpec((tm, 512), lambda i: (i, 2)),
                  pl.BlockSpec((tm, 512), lambda i: (i, 3)), tab, tab, tab],
        out_specs=[pl.BlockSpec((tm, D_ATTN), lambda i: (i, 0)), pl.BlockSpec((tm, D_KV), lambda i: (i, 0)),
                   pl.BlockSpec((tm, D_KV), lambda i: (i, 0))],
        out_shape=[SDS((rows, D_ATTN), MXU_DTYPE), SDS((rows, D_KV), MXU_DTYPE), SDS((rows, D_KV), MXU_DTYPE)],
        compiler_params=_cparams("parallel"),
    )(proj, proj, proj, cos, sin_a, sin_b)


SCAN_TILE = 8


def _scan_tiles(x_ref, out_ref, tre_ref, tim_ref, sb, t_r, t_i, reverse, prev_ref=None):
    base = 4 if reverse else 0
    n_tiles = BLK // SCAN_TILE
    row = lax.broadcasted_iota(jnp.int32, (SCAN_TILE, SB_STATES), 0)
    for j in (range(n_tiles - 1, -1, -1) if reverse else range(n_tiles)):
        rows = slice(SCAN_TILE * j, SCAN_TILE * (j + 1))
        xr = x_ref[rows, :SB_STATES]
        xi = x_ref[rows, SB_STATES:]
        for k in range(3):
            shift = SCAN_TILE - (1 << k) if reverse else (1 << k)
            rr = pltpu.roll(xr, shift, 0)
            ri = pltpu.roll(xi, shift, 0)
            ar = tre_ref[sb, base + k]
            ai = tim_ref[sb, base + k]
            xr, xi = xr + (ar * rr - ai * ri), xi + (ar * ri + ai * rr)
        pr = tre_ref[sb, base + 3]
        pi = tim_ref[sb, base + 3]
        xr, xi = xr + (pr * t_r - pi * t_i), xi + (pr * t_i + pi * t_r)
        out_ref[rows, :SB_STATES] = xr
        out_ref[rows, SB_STATES:] = xi
        if prev_ref is not None:
            prev_ref[rows, :SB_STATES] = jnp.where(row == 0, t_r, pltpu.roll(xr, 1, 0))
            prev_ref[rows, SB_STATES:] = jnp.where(row == 0, t_i, pltpu.roll(xi, 1, 0))
        edge = slice(0, 1) if reverse else slice(SCAN_TILE - 1, SCAN_TILE)
        t_r, t_i = xr[edge], xi[edge]
    return t_r, t_i


def _s5_fwd(proj, ssm, w_glu, b_glu3, layer):
    rows = proj.shape[0]
    n_chunks = rows // BLK
    b_mat, c_mat, t_re, t_im, d_skip = (ssm[k] for k in ("b_mat", "c_mat", "t_re", "t_im", "d_skip"))

    def body(u_ref, bm_ref, cm_ref, tre_ref, tim_ref, d_ref, wg_ref, bg_ref,
             y_ref, ys_ref, cin_ref, carry, bu_scr, s_scr):
        @pl.when(pl.program_id(0) == 0)
        def _():
            carry[...] = jnp.zeros_like(carry)

        cin_ref[...] = carry[...]
        u = u_ref[...]
        for sb in range(N_SB):
            cols = slice(sb * 128, (sb + 1) * 128)
            u_sb = u[:, cols]
            bu_scr[sb] = _dot(u_sb.astype(MXU_DTYPE), bm_ref[sb])
            t_r, t_i = _scan_tiles(bu_scr.at[sb], s_scr.at[sb], tre_ref, tim_ref, sb,
                                   carry[2 * sb:2 * sb + 1, :], carry[2 * sb + 1:2 * sb + 2, :], False)
            carry[2 * sb:2 * sb + 1, :] = t_r
            carry[2 * sb + 1:2 * sb + 2, :] = t_i
            y_ref[:, cols] = _dot(s_scr[sb].astype(MXU_DTYPE), cm_ref[sb]) + d_ref[:, cols] * u_sb
        z, _ = _gelu_parts(y_ref[...])
        gl = _dot(z.astype(MXU_DTYPE), wg_ref[...]) + bg_ref[...]
        ys_ref[...] = (z * _sigmoid(gl)).astype(MXU_DTYPE)

    full = lambda shape: pl.BlockSpec(shape, lambda j: (0,) * len(shape))
    return _pcall(
        body, name=f"s5_fwd_l{layer}", grid=(n_chunks,),
        in_specs=[pl.BlockSpec((BLK, D_SSM), lambda j: (j, 0)),
                  full((N_SB, 128, 2 * SB_STATES)), full((N_SB, 2 * SB_STATES, 128)),
                  full((N_SB, 8, SCAN_TILE, SB_STATES)), full((N_SB, 8, SCAN_TILE, SB_STATES)),
                  full((1, D_SSM)), full((D_SSM, D_SSM)),
                  pl.BlockSpec((None, 1, D_SSM), lambda j: (layer, 0, 0))],
        out_specs=[pl.BlockSpec((BLK, D_SSM), lambda j: (j, 0)), pl.BlockSpec((BLK, D_SSM), lambda j: (j, 0)),
                   pl.BlockSpec((None, 8, SB_STATES), lambda j: (j, 0, 0))],
        out_shape=[SDS((rows, D_SSM), F32), SDS((rows, D_SSM), MXU_DTYPE), SDS((n_chunks, 8, SB_STATES), F32)],
        scratch_shapes=[pltpu.VMEM((8, SB_STATES), F32), pltpu.VMEM((N_SB, BLK, 2 * SB_STATES), F32),
                        pltpu.VMEM((N_SB, BLK, 2 * SB_STATES), F32)],
        compiler_params=_cparams("arbitrary"),
    )(proj, b_mat, c_mat, t_re, t_im, d_skip, w_glu, b_glu3)


def _attn_mask(i):
    row = lax.broadcasted_iota(jnp.int32, (BLK, 3 * BLK), 0) + i * BLK
    col = lax.broadcasted_iota(jnp.int32, (BLK, 3 * BLK), 1)
    seg = jnp.right_shift(col, 7)
    c = jnp.bitwise_and(col, BLK - 1)
    kidx = c + (i + seg - 2) * BLK
    ok_meta = (seg == 0) & (c >= PAD_ROWS) & (row - c >= BLK)
    ok_win = (seg > 0) & (kidx >= PAD_ROWS) & (kidx <= row) & (row - kidx < BLK)
    return jnp.where(ok_meta | ok_win, 0.0, NEG_INF)


def _head_lanes(h):
    return slice(h * HEAD_DIM, (h + 1) * HEAD_DIM)


def _group_rows(ref, kvh):
    return jnp.concatenate([ref[:, _head_lanes(kvh * Q_PER_KV + g)] for g in range(Q_PER_KV)], axis=0)


def _group_bias(bias, sink_ref, layer, kvh):
    first_col = lax.broadcasted_iota(jnp.int32, (BLK, BLK), 1) == 0
    slabs = []
    for g in range(Q_PER_KV):
        first = jnp.where(first_col, sink_ref[layer, kvh * Q_PER_KV + g], bias[:, :BLK])
        slabs.append(jnp.concatenate([first, bias[:, BLK:]], axis=1))
    return jnp.concatenate(slabs, axis=0)


def _attn_probs(q4, k3, bias4):
    s = _dot_nt(q4, k3) + bias4
    e = jnp.exp(s - jnp.max(s, axis=-1, keepdims=True))
    return e * (1.0 / jnp.sum(e, axis=-1, keepdims=True))


def _attn_fwd(q, k, v, sinks, layer):
    rows = q.shape[0]
    n_blk = rows // BLK

    def body(sink_ref, q_ref, km_ref, kp_ref, kc_ref, vm_ref, vp_ref, vc_ref, o_ref):
        bias = _attn_mask(pl.program_id(0))
        for kvh in range(N_KV_HEADS):
            lanes = _head_lanes(kvh)
            k3 = jnp.concatenate([km_ref[:, lanes], kp_ref[:, lanes], kc_ref[:, lanes]], axis=0)
            v3 = jnp.concatenate([vm_ref[:, lanes], vp_ref[:, lanes], vc_ref[:, lanes]], axis=0)
            p = _attn_probs(_group_rows(q_ref, kvh), k3, _group_bias(bias, sink_ref, layer, kvh))
            o4 = _dot(p.astype(MXU_DTYPE), v3).astype(MXU_DTYPE)
            for g in range(Q_PER_KV):
                o_ref[:, _head_lanes(kvh * Q_PER_KV + g)] = o4[g * BLK:(g + 1) * BLK]

    kv_meta = pl.BlockSpec((BLK, D_KV), lambda i: (0, 0))
    kv_prev = pl.BlockSpec((BLK, D_KV), lambda i: (jnp.maximum(i - 1, 0), 0))
    kv_cur = pl.BlockSpec((BLK, D_KV), lambda i: (i, 0))
    return _pcall(
        body, name=f"attn_fwd_l{layer}", grid=(n_blk,),
        in_specs=[pl.BlockSpec(memory_space=pltpu.SMEM),
                  pl.BlockSpec((BLK, D_ATTN), lambda i: (i, 0)),
                  kv_meta, kv_prev, kv_cur, kv_meta, kv_prev, kv_cur],
        out_specs=pl.BlockSpec((BLK, D_ATTN), lambda i: (i, 0)),
        out_shape=SDS((rows, D_ATTN), MXU_DTYPE),
        compiler_params=_cparams("parallel"),
    )(sinks, q, k, k, k, v, v, v)


def _merge_fwd(y_ssm, y_attn, proj, hres, w_o_ssm, w_o_attn, w_out, gain3, layer):
    rows = hres.shape[0]
    tm = _row_tile(rows, 320)

    def body(ys_ref, ya_ref, gs_ref, ga_ref, x_ref, wos_ref, woa_ref, wout_ref, g_ref,
             mg_ref, mix_ref, out_ref):
        a1 = _dot(ys_ref[...], wos_ref[...])
        a2 = _dot(ya_ref[...], woa_ref[...])
        merged = (_sigmoid(gs_ref[...]) * a1 + _sigmoid(ga_ref[...]) * a2).astype(MXU_DTYPE)
        mg_ref[...] = merged
        mix = _dot(merged, wout_ref[...])
        mix_ref[...] = mix
        out_ref[...] = x_ref[...] + _rms_fwd(mix, g_ref[...])

    row_d = pl.BlockSpec((tm, D), lambda i: (i, 0))
    full = lambda shape: pl.BlockSpec(shape, lambda i: (0,) * len(shape))
    return _pcall(
        body, name=f"merge_fwd_l{layer}", grid=(rows // tm,),
        in_specs=[pl.BlockSpec((tm, D_SSM), lambda i: (i, 0)), row_d,
                  pl.BlockSpec((tm, D), lambda i: (i, 2)), pl.BlockSpec((tm, D), lambda i: (i, 3)), row_d,
                  full((D_SSM, D)), full((D_ATTN, D)), full((D, D)),
                  pl.BlockSpec((None, 1, D), lambda i: (layer, 0, 0))],
        out_specs=[row_d, row_d, row_d],
        out_shape=[SDS((rows, D), MXU_DTYPE), SDS((rows, D), F32), SDS((rows, D), F32)],
        compiler_params=_cparams("parallel"),
    )(y_ssm, y_attn, proj, proj, hres, w_o_ssm, w_o_attn, w_out, gain3)


def _mlp_fwd(hres, gain_pre3, gain_post3, w_up_g, w_down_g, layer):
    rows = hres.shape[0]
    tm = _row_tile(rows)

    def body(x_ref, gp_ref, gq_ref, wu_hbm, wd_hbm, up_ref, h_ref, ff_ref, out_ref,
             h_scr, acc, wu_scr, wd_scr, wu_sem, wd_sem):
        kf = pl.program_id(1)
        first = (pl.program_id(0) == 0) & (kf == 0)
        _load_resident(wu_hbm, wu_scr, wu_sem, first)
        _load_resident(wd_hbm, wd_scr, wd_sem, first)

        @pl.when(kf == 0)
        def _():
            hn = _rms_fwd(x_ref[...], gp_ref[...]).astype(MXU_DTYPE)
            h_scr[...] = hn
            h_ref[...] = hn
            acc[...] = jnp.zeros_like(acc)

        up = _dot(h_scr[...], wu_scr[kf])
        up_ref[...] = up.astype(MXU_DTYPE)
        r = jnp.maximum(up, 0.0)
        acc[...] += _dot((r * r).astype(MXU_DTYPE), wd_scr[kf])

        @pl.when(kf == N_DEV - 1)
        def _():
            ff = acc[...]
            ff_ref[...] = ff
            out_ref[...] = x_ref[...] + _rms_fwd(ff, gq_ref[...])

    row_d = pl.BlockSpec((tm, D), lambda i, k: (i, 0))
    gain = pl.BlockSpec((None, 1, D), lambda i, k: (layer, 0, 0))
    return _pcall(
        body, name=f"mlp_fwd_l{layer}", grid=(rows // tm, N_DEV),
        in_specs=[row_d, gain, gain, pl.BlockSpec(memory_space=pl.ANY), pl.BlockSpec(memory_space=pl.ANY)],
        out_specs=[pl.BlockSpec((tm, COL_SHARD), lambda i, k: (i, k)), row_d, row_d, row_d],
        out_shape=[SDS((rows, D_FF), MXU_DTYPE), SDS((rows, D), MXU_DTYPE), SDS((rows, D), F32), SDS((rows, D), F32)],
        scratch_shapes=[pltpu.VMEM((tm, D), MXU_DTYPE), pltpu.VMEM((tm, D), F32),
                        pltpu.VMEM((N_DEV, D, COL_SHARD), MXU_DTYPE), pltpu.VMEM((N_DEV, COL_SHARD, D), MXU_DTYPE),
                        pltpu.SemaphoreType.DMA((N_DEV,)), pltpu.SemaphoreType.DMA((N_DEV,))],
        compiler_params=_cparams("arbitrary", "arbitrary"),
    )(hres, gain_pre3, gain_post3, w_up_g, w_down_g)


def _loss_and_grad(hres, target):
    rows = hres.shape[0]
    n_blk = rows // BLK

    def body(y_ref, t_ref, dy_ref, loss_ref):
        i = pl.program_id(0)

        @pl.when(i == 0)
        def _():
            dy_ref[...] = jnp.zeros_like(dy_ref)
            loss_ref[...] = jnp.zeros_like(loss_ref)

        @pl.when(i > 0)
        def _():
            err = y_ref[...] - t_ref[...]
            dy_ref[...] = err * (1.0 / D)
            loss_ref[...] += jnp.sum(err * err) * (0.5 / D)

    return _pcall(
        body, name="loss", grid=(n_blk,),
        in_specs=[pl.BlockSpec((BLK, D), lambda i: (i, 0)),
                  pl.BlockSpec((BLK, D), lambda i: (jnp.maximum(i - 1, 0), 0))],
        out_specs=[pl.BlockSpec((BLK, D), lambda i: (i, 0)), pl.BlockSpec((1, 128), lambda i: (0, 0))],
        out_shape=[SDS((rows, D), F32), SDS((1, 128), F32)],
        compiler_params=_cparams("arbitrary"),
    )(hres, target)


def _relu_squared(up):
    r = jnp.maximum(up.astype(F32), 0.0)
    return (r * r).astype(MXU_DTYPE)


def _matmul_tn(a, b, name, dev_major_cols=None, a_fn=None):
    rows, ka = a.shape
    n = b.shape[1]
    ta = min(ka, 1024)
    tn = 1024 if n % 1024 == 0 else 512
    tr = _row_tile(rows)
    n_r = rows // tr

    def body(a_ref, b_ref, o_ref, acc):
        r = pl.program_id(2)

        @pl.when(r == 0)
        def _():
            acc[...] = jnp.zeros_like(acc)

        a_blk = a_ref[...] if a_fn is None else a_fn(a_ref[...])
        acc[...] += _dot_tn(a_blk, b_ref[...])

        @pl.when(r == n_r - 1)
        def _():
            if dev_major_cols is None:
                o_ref[...] = acc[...].astype(XFER_DTYPE)
            else:
                for s in range(tn // dev_major_cols):
                    o_ref[s] = acc[:, s * dev_major_cols:(s + 1) * dev_major_cols].astype(XFER_DTYPE)

    if dev_major_cols is None:
        out_spec = pl.BlockSpec((ta, tn), lambda i, j, r: (i, j))
        out_shape = SDS((ka, n), XFER_DTYPE)
    else:
        w = dev_major_cols
        out_spec = pl.BlockSpec((tn // w, ta, w), lambda i, j, r: (j, i, 0))
        out_shape = SDS((n // w, ka, w), XFER_DTYPE)
    return _pcall(
        body, name=name, grid=(ka // ta, n // tn, n_r),
        in_specs=[pl.BlockSpec((tr, ta), lambda i, j, r: (r, i)), pl.BlockSpec((tr, tn), lambda i, j, r: (r, j))],
        out_specs=out_spec, out_shape=out_shape,
        scratch_shapes=[pltpu.VMEM((ta, tn), F32)],
        compiler_params=_cparams("parallel", "parallel", "arbitrary"),
    )(a, b)


def _mlp_bwd(dout, ff, up, hres_mid, gain_pre3, gain_post3, w_up_g, w_down_g, layer):
    rows = dout.shape[0]
    tm = _row_tile(rows)

    def body(do_ref, ff_ref, up_ref, x_ref, gp_ref, gq_ref, wu_hbm, wd_hbm,
             dff_ref, dup_ref, dx_ref, dgq_ref, dgp_ref, dff_scr, acc, wu_scr, wd_scr, wu_sem, wd_sem):
        i = pl.program_id(0)
        kf = pl.program_id(1)
        _load_resident(wu_hbm, wu_scr, wu_sem, (i == 0) & (kf == 0))
        _load_resident(wd_hbm, wd_scr, wd_sem, (i == 0) & (kf == 0))

        @pl.when((i == 0) & (kf == 0))
        def _():
            dgq_ref[...] = jnp.zeros_like(dgq_ref)
            dgp_ref[...] = jnp.zeros_like(dgp_ref)

        @pl.when(kf == 0)
        def _():
            dff, dg = _rms_bwd(ff_ref[...], gq_ref[...], do_ref[...])
            dgq_ref[...] += dg
            dffb = dff.astype(MXU_DTYPE)
            dff_scr[...] = dffb
            dff_ref[...] = dffb
            acc[...] = jnp.zeros_like(acc)

        dact = _dot_nt(dff_scr[...], wd_scr[kf])
        dup = (dact * (2.0 * jnp.maximum(up_ref[...].astype(F32), 0.0))).astype(MXU_DTYPE)
        dup_ref[...] = dup
        acc[...] += _dot_nt(dup, wu_scr[kf])

        @pl.when(kf == N_DEV - 1)
        def _():
            dx, dg = _rms_bwd(x_ref[...], gp_ref[...], acc[...])
            dgp_ref[...] += dg
            dx_ref[...] = do_ref[...] + dx

    row_d = pl.BlockSpec((tm, D), lambda i, k: (i, 0))
    gain = pl.BlockSpec((None, 1, D), lambda i, k: (layer, 0, 0))
    dgain = pl.BlockSpec((1, D), lambda i, k: (0, 0))
    return _pcall(
        body, name=f"mlp_bwd_l{layer}", grid=(rows // tm, N_DEV),
        in_specs=[row_d, row_d, pl.BlockSpec((tm, COL_SHARD), lambda i, k: (i, k)), row_d, gain, gain,
                  pl.BlockSpec(memory_space=pl.ANY), pl.BlockSpec(memory_space=pl.ANY)],
        out_specs=[row_d, pl.BlockSpec((tm, COL_SHARD), lambda i, k: (i, k)), row_d, dgain, dgain],
        out_shape=[SDS((rows, D), MXU_DTYPE), SDS((rows, D_FF), MXU_DTYPE), SDS((rows, D), F32),
                   SDS((1, D), F32), SDS((1, D), F32)],
        scratch_shapes=[pltpu.VMEM((tm, D), MXU_DTYPE), pltpu.VMEM((tm, D), F32),
                        pltpu.VMEM((N_DEV, D, COL_SHARD), MXU_DTYPE), pltpu.VMEM((N_DEV, COL_SHARD, D), MXU_DTYPE),
                        pltpu.SemaphoreType.DMA((N_DEV,)), pltpu.SemaphoreType.DMA((N_DEV,))],
        compiler_params=_cparams("arbitrary", "arbitrary"),
    )(dout, ff, up, hres_mid, gain_pre3, gain_post3, w_up_g, w_down_g)


def _merge_bwd(dhm, mix, y_ssm, y_attn, proj, w_o_ssm, w_o_attn, w_out, gain3, layer):
    rows = dhm.shape[0]
    tm = _row_tile(rows, 320)

    def body(dh_ref, mix_ref, ys_ref, ya_ref, gs_ref, ga_ref, wos_ref, woa_ref, wout_ref, g_ref,
             dmix_ref, da1_ref, da2_ref, dgs_ref, dga_ref, dys_ref, dya_ref, dg_ref):
        @pl.when(pl.program_id(0) == 0)
        def _():
            dg_ref[...] = jnp.zeros_like(dg_ref)

        dmix, dg = _rms_bwd(mix_ref[...], g_ref[...], dh_ref[...])
        dg_ref[...] += dg
        dmixb = dmix.astype(MXU_DTYPE)
        dmix_ref[...] = dmixb
        dmerged = _dot_nt(dmixb, wout_ref[...])
        sg_s = _sigmoid(gs_ref[...])
        sg_a = _sigmoid(ga_ref[...])
        da1 = (dmerged * sg_s).astype(MXU_DTYPE)
        da2 = (dmerged * sg_a).astype(MXU_DTYPE)
        da1_ref[...] = da1
        da2_ref[...] = da2
        a1 = _dot(ys_ref[...], wos_ref[...])
        a2 = _dot(ya_ref[...], woa_ref[...])
        dgs_ref[...] = (dmerged * a1 * (sg_s * (1.0 - sg_s))).astype(MXU_DTYPE)
        dga_ref[...] = (dmerged * a2 * (sg_a * (1.0 - sg_a))).astype(MXU_DTYPE)
        dys_ref[...] = _dot_nt(da1, wos_ref[...])
        dya_ref[...] = _dot_nt(da2, woa_ref[...])

    row_d = pl.BlockSpec((tm, D), lambda i: (i, 0))
    full = lambda shape: pl.BlockSpec(shape, lambda i: (0,) * len(shape))
    return _pcall(
        body, name=f"merge_bwd_l{layer}", grid=(rows // tm,),
        in_specs=[row_d, row_d, pl.BlockSpec((tm, D_SSM), lambda i: (i, 0)), row_d,
                  pl.BlockSpec((tm, D), lambda i: (i, 2)), pl.BlockSpec((tm, D), lambda i: (i, 3)),
                  full((D_SSM, D)), full((D_ATTN, D)), full((D, D)),
                  pl.BlockSpec((None, 1, D), lambda i: (layer, 0, 0))],
        out_specs=[row_d, row_d, row_d, row_d, row_d, pl.BlockSpec((tm, D_SSM), lambda i: (i, 0)), row_d,
                   pl.BlockSpec((1, D), lambda i: (0, 0))],
        out_shape=[SDS((rows, D), MXU_DTYPE)] * 5 + [SDS((rows, D_SSM), F32), SDS((rows, D_ATTN), F32),
                                                      SDS((1, D), F32)],
        compiler_params=_cparams("arbitrary"),
    )(dhm, mix, y_ssm, y_attn, proj, proj, w_o_ssm, w_o_attn, w_out, gain3)


def _attn_bwd(q, k, v, d_out, sinks, layer):
    rows = q.shape[0]
    n_blk = rows // BLK
    last = n_blk - 1

    def body(sink_ref, q_ref, km_ref, kp_ref, kc_ref, vm_ref, vp_ref, vc_ref, do_ref,
             dq_ref, dk_ref, dv_ref, dkm_ref, dvm_ref, ds_ref, dk_carry, dv_carry):
        i = pl.program_id(0)

        @pl.when(i == 0)
        def _():
            dkm_ref[...] = jnp.zeros_like(dkm_ref)
            dvm_ref[...] = jnp.zeros_like(dvm_ref)
            ds_ref[...] = jnp.zeros_like(ds_ref)
            dk_carry[...] = jnp.zeros_like(dk_carry)
            dv_carry[...] = jnp.zeros_like(dv_carry)

        @pl.when(i <= last)
        def _():
            bias = _attn_mask(i)
            for kvh in range(N_KV_HEADS):
                lanes = _head_lanes(kvh)
                k3 = jnp.concatenate([km_ref[:, lanes], kp_ref[:, lanes], kc_ref[:, lanes]], axis=0)
                v3 = jnp.concatenate([vm_ref[:, lanes], vp_ref[:, lanes], vc_ref[:, lanes]], axis=0)
                q4 = _group_rows(q_ref, kvh)
                do4 = _group_rows(do_ref, kvh).astype(MXU_DTYPE)
                p = _attn_probs(q4, k3, _group_bias(bias, sink_ref, layer, kvh))
                dp = _dot_nt(do4, v3)
                dsf = p * (dp - jnp.sum(dp * p, axis=-1, keepdims=True))
                dsc = dsf.astype(MXU_DTYPE)
                dv3 = _dot_tn(p.astype(MXU_DTYPE), do4)
                dk3 = _dot_tn(dsc, q4)
                dq4 = _dot(dsc, k3)
                for g in range(Q_PER_KV):
                    h = kvh * Q_PER_KV + g
                    dq_ref[:, _head_lanes(h)] = dq4[g * BLK:(g + 1) * BLK]
                    ds_ref[h:h + 1, :] += jnp.sum(dsf[g * BLK:(g + 1) * BLK, 0:BLK], axis=0, keepdims=True)
                dkm_ref[:, lanes] += dk3[0:BLK]
                dvm_ref[:, lanes] += dv3[0:BLK]
                dk_ref[:, lanes] = dk_carry[:, lanes] + dk3[BLK:2 * BLK]
                dv_ref[:, lanes] = dv_carry[:, lanes] + dv3[BLK:2 * BLK]
                dk_carry[:, lanes] = dk3[2 * BLK:3 * BLK]
                dv_carry[:, lanes] = dv3[2 * BLK:3 * BLK]

        @pl.when(i == last + 1)
        def _():
            dk_ref[...] = dk_carry[...]
            dv_ref[...] = dv_carry[...]

    cur = lambda i: (jnp.minimum(i, last), 0)
    prev = lambda i: (jnp.clip(i - 1, 0, last), 0)
    kv_meta = pl.BlockSpec((BLK, D_KV), lambda i: (0, 0))
    kv_prev = pl.BlockSpec((BLK, D_KV), prev)
    kv_cur = pl.BlockSpec((BLK, D_KV), cur)
    return _pcall(
        body, name=f"attn_bwd_l{layer}", grid=(n_blk + 1,),
        in_specs=[pl.BlockSpec(memory_space=pltpu.SMEM),
                  pl.BlockSpec((BLK, D_ATTN), cur),
                  kv_meta, kv_prev, kv_cur, kv_meta, kv_prev, kv_cur,
                  pl.BlockSpec((BLK, D_ATTN), cur)],
        out_specs=[pl.BlockSpec((BLK, D_ATTN), cur), kv_prev, kv_prev, kv_meta, kv_meta,
                   pl.BlockSpec((N_Q_HEADS, 128), lambda i: (0, 0))],
        out_shape=[SDS((rows, D_ATTN), F32), SDS((rows, D_KV), F32), SDS((rows, D_KV), F32),
                   SDS((BLK, D_KV), F32), SDS((BLK, D_KV), F32), SDS((N_Q_HEADS, 128), F32)],
        scratch_shapes=[pltpu.VMEM((BLK, D_KV), F32), pltpu.VMEM((BLK, D_KV), F32)],
        compiler_params=_cparams("arbitrary"),
    )(sinks, q, k, k, k, v, v, v, d_out)


def _rope_bwd(dq, dk, dv, dk_meta, dv_meta, cos, sin_a, sin_b, layer):
    rows = dq.shape[0]
    tm = _row_tile(rows)

    def body(dq_ref, dk_ref, dv_ref, dkm_ref, dvm_ref, c_ref, a_ref, b_ref, o_ref):
        c, a, b = c_ref[...], -a_ref[...], -b_ref[...]
        for t in range(8):
            x = dq_ref[:, t * 128:(t + 1) * 128]
            o_ref[:, t * 128:(t + 1) * 128] = (_rope_lanes(x, c, a, b) * ATTN_SCALE).astype(MXU_DTYPE)
        for t in range(2):
            x = dk_ref[:, t * 128:(t + 1) * 128]
            o_ref[:, D_ATTN + t * 128:D_ATTN + (t + 1) * 128] = _rope_lanes(x, c, a, b).astype(MXU_DTYPE)
        o_ref[:, D_ATTN + D_KV:] = dv_ref[...].astype(MXU_DTYPE)

        @pl.when(pl.program_id(0) == 0)
        def _():
            cb, ab, bb = c[0:BLK], a[0:BLK], b[0:BLK]
            is_meta = lax.broadcasted_iota(jnp.int32, (BLK, 128), 0) >= PAD_ROWS
            for t in range(2):
                x = dk_ref[0:BLK, t * 128:(t + 1) * 128] + jnp.where(is_meta, dkm_ref[:, t * 128:(t + 1) * 128], 0.0)
                o_ref[0:BLK, D_ATTN + t * 128:D_ATTN + (t + 1) * 128] = _rope_lanes(x, cb, ab, bb).astype(MXU_DTYPE)
                xv = dv_ref[0:BLK, t * 128:(t + 1) * 128] + jnp.where(is_meta, dvm_ref[:, t * 128:(t + 1) * 128], 0.0)
                o_ref[0:BLK, D_ATTN + D_KV + t * 128:D_ATTN + D_KV + (t + 1) * 128] = xv.astype(MXU_DTYPE)

    tab = pl.BlockSpec((tm, 128), lambda i: (i, 0))
    kv = pl.BlockSpec((tm, D_KV), lambda i: (i, 0))
    meta = pl.BlockSpec((BLK, D_KV), lambda i: (0, 0))
    return _pcall(
        body, name=f"rope_bwd_l{layer}", grid=(rows // tm,),
        in_specs=[pl.BlockSpec((tm, D_ATTN), lambda i: (i, 0)), kv, kv, meta, meta, tab, tab, tab],
        out_specs=pl.BlockSpec((tm, D_ATTN + 2 * D_KV), lambda i: (i, 0)),
        out_shape=SDS((rows, D_ATTN + 2 * D_KV), MXU_DTYPE),
        compiler_params=_cparams("parallel"),
    )(dq, dk, dv, dk_meta, dv_meta, cos, sin_a, sin_b)


def _s5_bwd(d_gated, y, proj, carry_in, ssm, w_glu, b_glu3, layer):
    rows = y.shape[0]
    n_chunks = rows // BLK
    b_mat, c_mat, t_re, t_im, d_skip = (ssm[k] for k in ("b_mat", "c_mat", "t_re", "t_im", "d_skip"))

    def body(dz_ref, y_ref, u_ref, cin_ref, bm_ref, cm_ref, tre_ref, tim_ref, d_ref, wg_ref, bg_ref,
             du_ref, dwg_ref, dbg_ref, dd_ref, dbm_ref, dcm_ref, dab_ref,
             lam_carry, bu_scr, s_scr, sp_scr, g_scr, lam_scr):
        step = pl.program_id(0)
        chunk = n_chunks - 1 - step

        @pl.when(step == 0)
        def _():
            for r in (dwg_ref, dbg_ref, dd_ref, dbm_ref, dcm_ref, dab_ref, lam_carry):
                r[...] = jnp.zeros_like(r)

        y = y_ref[...]
        u = u_ref[...]
        d_o = dz_ref[...]
        z, t = _gelu_parts(y)
        zb = z.astype(MXU_DTYPE)
        sg = _sigmoid(_dot(zb, wg_ref[...]) + bg_ref[...])
        dgl = d_o * z * (sg * (1.0 - sg))
        dglb = dgl.astype(MXU_DTYPE)
        dz = d_o * sg + _dot_nt(dglb, wg_ref[...])
        dwg_ref[...] += _dot_tn(zb, dglb)
        dbg_ref[...] += jnp.sum(dgl, axis=0, keepdims=True)
        dy = dz * _gelu_grad(y, t)
        dd_ref[...] += jnp.sum(dy * u, axis=0, keepdims=True)
        grow = lax.broadcasted_iota(jnp.int32, (BLK, 128), 0) + chunk * BLK
        for sb in range(N_SB):
            cols = slice(sb * 128, (sb + 1) * 128)
            u_sb = u[:, cols].astype(MXU_DTYPE)
            dy_sb = dy[:, cols]
            dyb = dy_sb.astype(MXU_DTYPE)
            bu_scr[sb] = _dot(u_sb, bm_ref[sb])
            _scan_tiles(bu_scr.at[sb], s_scr.at[sb], tre_ref, tim_ref, sb,
                        cin_ref[2 * sb:2 * sb + 1, :], cin_ref[2 * sb + 1:2 * sb + 2, :], False, prev_ref=sp_scr.at[sb])
            dcm_ref[sb] += _dot_tn(s_scr[sb].astype(MXU_DTYPE), dyb)
            g_scr[sb] = _dot_nt(dyb, cm_ref[sb])
            n_r, n_i = _scan_tiles(g_scr.at[sb], lam_scr.at[sb], tre_ref, tim_ref, sb,
                                   lam_carry[2 * sb:2 * sb + 1, :], lam_carry[2 * sb + 1:2 * sb + 2, :], True)
            lam_carry[2 * sb:2 * sb + 1, :] = n_r
            lam_carry[2 * sb + 1:2 * sb + 2, :] = n_i
            lr, li = lam_scr[sb, :, :SB_STATES], lam_scr[sb, :, SB_STATES:]
            spr, spi = sp_scr[sb, :, :SB_STATES], sp_scr[sb, :, SB_STATES:]
            dab_ref[2 * sb:2 * sb + 1, :] += jnp.sum(spr * lr + spi * li, axis=0, keepdims=True)
            dab_ref[2 * sb + 1:2 * sb + 2, :] += jnp.sum(spr * li - spi * lr, axis=0, keepdims=True)
            lam = lam_scr[sb].astype(MXU_DTYPE)
            dbm_ref[sb] += _dot_tn(u_sb, lam)
            du = _dot_nt(lam, bm_ref[sb]) + d_ref[:, cols] * dy_sb
            du_ref[:, cols] = jnp.where(grow >= PAD_ROWS, du, 0.0).astype(MXU_DTYPE)

    rev = lambda j: (n_chunks - 1 - j, 0)
    full = lambda shape: pl.BlockSpec(shape, lambda j: (0,) * len(shape))
    tables = [full((N_SB, 8, SCAN_TILE, SB_STATES))] * 2
    chunk_scratch = pltpu.VMEM((N_SB, BLK, 2 * SB_STATES), F32)
    return _pcall(
        body, name=f"s5_bwd_l{layer}", grid=(n_chunks,),
        in_specs=[pl.BlockSpec((BLK, D_SSM), rev), pl.BlockSpec((BLK, D_SSM), rev), pl.BlockSpec((BLK, D_SSM), rev),
                  pl.BlockSpec((None, 8, SB_STATES), lambda j: (n_chunks - 1 - j, 0, 0)),
                  full((N_SB, 128, 2 * SB_STATES)), full((N_SB, 2 * SB_STATES, 128))] + tables + [
                  full((1, D_SSM)), full((D_SSM, D_SSM)),
                  pl.BlockSpec((None, 1, D_SSM), lambda j: (layer, 0, 0))],
        out_specs=[pl.BlockSpec((BLK, D_SSM), rev), full((D_SSM, D_SSM)), full((1, D_SSM)), full((1, D_SSM)),
                   full((N_SB, 128, 2 * SB_STATES)), full((N_SB, 2 * SB_STATES, 128)), full((8, SB_STATES))],
        out_shape=[SDS((rows, D_SSM), MXU_DTYPE), SDS((D_SSM, D_SSM), F32), SDS((1, D_SSM), F32), SDS((1, D_SSM), F32),
                   SDS((N_SB, 128, 2 * SB_STATES), F32), SDS((N_SB, 2 * SB_STATES, 128), F32), SDS((8, SB_STATES), F32)],
        scratch_shapes=[pltpu.VMEM((8, SB_STATES), F32)] + [chunk_scratch] * 5,
        compiler_params=_cparams("arbitrary"),
    )(d_gated, y, proj, carry_in, b_mat, c_mat, t_re, t_im, d_skip, w_glu, b_glu3)


DPROJ_PIECES = ((0, 1), (1, 3), (4, 2), (6, 2))


def _in_bwd(dproj_pieces, dhm, hres, gain3, w_in_g, layer):
    rows = hres.shape[0]
    tm = _row_tile(rows)

    def body(du_ref, dqkv_ref, dgs_ref, dga_ref, dh_ref, x_ref, g_ref, w_hbm, dx_ref, dg_ref, acc, w_scr, w_sem):
        i = pl.program_id(0)
        j = pl.program_id(1)
        _load_resident(w_hbm, w_scr, w_sem, (i == 0) & (j == 0))

        @pl.when((i == 0) & (j == 0))
        def _():
            dg_ref[...] = jnp.zeros_like(dg_ref)

        @pl.when(j == 0)
        def _():
            acc[...] = jnp.zeros_like(acc)

        for piece_ref, (first, count) in zip((du_ref, dqkv_ref, dgs_ref, dga_ref), DPROJ_PIECES):
            @pl.when((j >= first) & (j < first + count))
            def _():
                acc[...] += _dot_nt(piece_ref[...], w_scr[j])

        @pl.when(j == N_DEV - 1)
        def _():
            dx, dg = _rms_bwd(x_ref[...], g_ref[...], acc[...])
            dg_ref[...] += dg
            dx_ref[...] = dh_ref[...] + dx

    row_d = pl.BlockSpec((tm, D), lambda i, j: (i, 0))

    def piece_spec(first, count):
        return pl.BlockSpec((tm, COL_SHARD), lambda i, j: (i, jnp.clip(j - first, 0, count - 1)))

    return _pcall(
        body, name=f"in_bwd_l{layer}", grid=(rows // tm, N_DEV),
        in_specs=[piece_spec(*p) for p in DPROJ_PIECES] + [
                  row_d, row_d,
                  pl.BlockSpec((None, 1, D), lambda i, j: (layer, 0, 0)),
                  pl.BlockSpec(memory_space=pl.ANY)],
        out_specs=[row_d, pl.BlockSpec((1, D), lambda i, j: (0, 0))],
        out_shape=[SDS((rows, D), F32), SDS((1, D), F32)],
        scratch_shapes=[pltpu.VMEM((tm, D), F32), pltpu.VMEM((N_DEV, D, COL_SHARD), MXU_DTYPE),
                        pltpu.SemaphoreType.DMA((N_DEV,))],
        compiler_params=_cparams("arbitrary", "arbitrary"),
    )(*dproj_pieces, dhm, hres, gain3, w_in_g)


_ADAM_C1 = 1.0 / (1.0 - ADAM_B1 ** ADAM_STEP)
_ADAM_C2 = 1.0 / (1.0 - ADAM_B2 ** ADAM_STEP)


def _adam_math(w, g, m, v):
    m = ADAM_B1 * m + (1.0 - ADAM_B1) * g
    v = ADAM_B2 * v + (1.0 - ADAM_B2) * (g * g)
    delta = -ADAM_LR * ((m * _ADAM_C1) / (jnp.sqrt(v * _ADAM_C2) + ADAM_EPS) + ADAM_WD * w)
    return delta, m, v


def _adamw_layers(parts0, parts1, w, m, v, name):
    _, rows, cols = w.shape
    tr = min(rows, (1 << 16) // cols)
    nt = rows // tr

    def body(p0_ref, p1_ref, w_ref, m_ref, v_ref, g_ref, d_ref, nm_ref, nv_ref):
        layer = pl.program_id(0)

        def run(p_ref):
            g = p_ref[0].astype(F32)
            for s in range(1, N_DEV):
                g = g + p_ref[s].astype(F32)
            delta, nm, nv = _adam_math(w_ref[...], g, m_ref[...], v_ref[...])
            g_ref[...] = g
            d_ref[...] = delta
            nm_ref[...] = nm
            nv_ref[...] = nv

        @pl.when(layer == 0)
        def _():
            run(p0_ref)

        @pl.when(layer == 1)
        def _():
            run(p1_ref)

    wspec = pl.BlockSpec((None, tr, cols), lambda l, i: (l, i, 0))
    return _pcall(
        body, name=name, grid=(2, nt),
        in_specs=[pl.BlockSpec((N_DEV, tr, cols), lambda l, i: (0, jnp.where(l == 0, i, nt - 1), 0)),
                  pl.BlockSpec((N_DEV, tr, cols), lambda l, i: (0, jnp.where(l == 1, i, 0), 0)),
                  wspec, wspec, wspec],
        out_specs=[wspec] * 4, out_shape=[SDS(w.shape, F32)] * 4,
        compiler_params=_cparams("arbitrary", "arbitrary"),
    )(parts0, parts1, w, m, v)


def _sum_slots(parts, name):
    def body(p_ref, o_ref):
        acc = p_ref[0]
        for s in range(1, N_DEV):
            acc = acc + p_ref[s]
        o_ref[...] = acc

    vmem = pl.BlockSpec(memory_space=pltpu.VMEM)
    return _pcall(body, name=name, out_shape=SDS(parts.shape[1:], F32), in_specs=[vmem], out_specs=vmem,
                  compiler_params=_cparams())(parts)


def _adamw_packed(g, w, m, v, name):
    def body(g_ref, w_ref, m_ref, v_ref, d_ref, nm_ref, nv_ref):
        delta, nm, nv = _adam_math(w_ref[...], g_ref[...], m_ref[...], v_ref[...])
        d_ref[...] = delta
        nm_ref[...] = nm
        nv_ref[...] = nv

    vmem = pl.BlockSpec(memory_space=pltpu.VMEM)
    return _pcall(body, name=name, out_shape=[SDS(g.shape, F32)] * 3, in_specs=[vmem] * 4, out_specs=[vmem] * 3,
                  compiler_params=_cparams())(g, w, m, v)


def _ssm_discretize(a_re, a_im, log_dt, b_re, b_im):
    dt = jnp.exp(log_dt)[:, None]
    mag = jnp.exp(a_re * dt)
    ang = a_im * dt
    ab_re, ab_im = mag * jnp.cos(ang), mag * jnp.sin(ang)
    xr, xi = ab_re - 1.0, ab_im
    den = a_re * a_re + a_im * a_im
    q_re = (xr * a_re + xi * a_im) / den
    q_im = (xi * a_re - xr * a_im) / den
    bb_re = q_re[..., None] * b_re - q_im[..., None] * b_im
    bb_im = q_re[..., None] * b_im + q_im[..., None] * b_re
    return ab_re, ab_im, bb_re, bb_im


def _block_diag_b(bb):
    m = jnp.einsum("sgnc,gh->sgchn", bb.reshape(N_SB, 8, N_STATE, GROUP_CH), jnp.eye(8, dtype=F32))
    return m.reshape(N_SB, 128, SB_STATES)


def _block_diag_b_t(dm):
    return jnp.einsum("sgchn,gh->sgnc", dm.reshape(N_SB, 8, GROUP_CH, 8, N_STATE),
                      jnp.eye(8, dtype=F32)).reshape(N_GROUPS, N_STATE, GROUP_CH)


def _block_diag_c(cc):
    m = jnp.einsum("sgcn,gh->sgnhc", cc.reshape(N_SB, 8, GROUP_CH, N_STATE), jnp.eye(8, dtype=F32))
    return m.reshape(N_SB, SB_STATES, 128)


def _block_diag_c_t(dm):
    return jnp.einsum("sgnhc,gh->sgcn", dm.reshape(N_SB, 8, N_STATE, 8, GROUP_CH),
                      jnp.eye(8, dtype=F32)).reshape(N_GROUPS, GROUP_CH, N_STATE)


def _ssm_tables(ab_re, ab_im, bb_re, bb_im, c_re, c_im, d_skip):
    pr, pi = ab_re.reshape(1, -1), ab_im.reshape(1, -1)
    cr, ci = pr, pi
    squares = []
    for _ in range(3):
        squares.append((cr, ci))
        pr, pi = (jnp.concatenate([pr, pr * cr - pi * ci], axis=0),
                  jnp.concatenate([pi, pr * ci + pi * cr], axis=0))
        cr, ci = cr * cr - ci * ci, 2.0 * cr * ci
    r = jnp.arange(SCAN_TILE)[:, None]
    fwd = [(jnp.where(r >= (1 << k), squares[k][0], 0.0), jnp.where(r >= (1 << k), squares[k][1], 0.0))
           for k in range(3)] + [(pr, pi)]
    rev = [(jnp.where(r < SCAN_TILE - (1 << k), squares[k][0], 0.0),
            jnp.where(r < SCAN_TILE - (1 << k), -squares[k][1], 0.0)) for k in range(3)] + [(pr[::-1], -pi[::-1])]
    table = lambda part: jnp.stack([e[part] for e in fwd + rev]).reshape(
        8, SCAN_TILE, N_SB, SB_STATES).transpose(2, 0, 1, 3)
    return dict(
        b_mat=jnp.concatenate([_block_diag_b(bb_re), _block_diag_b(bb_im)], axis=-1).astype(MXU_DTYPE),
        c_mat=jnp.concatenate([_block_diag_c(c_re), -_block_diag_c(c_im)], axis=1).astype(MXU_DTYPE),
        t_re=table(0), t_im=table(1),
        d_skip=d_skip.reshape(1, D_SSM))


def _rope_tables(rows):
    pos = (jnp.arange(rows, dtype=jnp.int32) - PAD_ROWS).astype(F32)
    inv_freq = 1.0 / (ROPE_THETA ** (jnp.arange(0, HEAD_DIM, 2, dtype=F32) / HEAD_DIM))
    ang = pos[:, None] * inv_freq[None, :]
    ang = jnp.concatenate([ang, ang, ang, ang], axis=-1)
    first_half = (jnp.arange(128) % HEAD_DIM) < HEAD_DIM // 2
    sin = jnp.sin(ang)
    return jnp.cos(ang), jnp.where(first_half, -sin, 0.0), jnp.where(first_half, 0.0, sin)


def _pack(arrays):
    flat = jnp.concatenate([a.reshape(-1).astype(F32) for a in arrays])
    pad = (-flat.shape[0]) % 1024
    return jnp.pad(flat, (0, pad)).reshape(-1, 128)


def _unpack(packed, like):
    flat = packed.reshape(-1)
    out, off = [], 0
    for a in like:
        n = math.prod(a.shape)
        out.append(flat[off:off + n].reshape(a.shape))
        off += n
    return out


BIG = ("w_in", "w_glu", "w_o_ssm", "w_o_attn", "w_out", "w_up", "w_down")
WEIGHTS = ("meta_tokens", "norm_mix_pre", "norm_mix_post", "norm_mlp_pre", "norm_mlp_post", "w_in",
           "ssm_a_re", "ssm_a_im", "ssm_log_dt", "ssm_b_re", "ssm_b_im", "ssm_c_re", "ssm_c_im", "ssm_d",
           "w_glu", "b_glu", "attn_sinks", "w_o_ssm", "w_o_attn", "w_out", "w_up", "w_down")
SMALL = tuple(n for n in WEIGHTS if n not in BIG)


def kernel(x, meta_tokens, norm_mix_pre, norm_mix_post, norm_mlp_pre, norm_mlp_post, w_in, ssm_a_re, ssm_a_im, ssm_log_dt, ssm_b_re, ssm_b_im, ssm_c_re, ssm_c_im, ssm_d, w_glu, b_glu, attn_sinks, w_o_ssm, w_o_attn, w_out, w_up, w_down, loss_target, m_meta_tokens, m_norm_mix_pre, m_norm_mix_post, m_norm_mlp_pre, m_norm_mlp_post, m_w_in, m_ssm_a_re, m_ssm_a_im, m_ssm_log_dt, m_ssm_b_re, m_ssm_b_im, m_ssm_c_re, m_ssm_c_im, m_ssm_d, m_w_glu, m_b_glu, m_attn_sinks, m_w_o_ssm, m_w_o_attn, m_w_out, m_w_up, m_w_down, v_meta_tokens, v_norm_mix_pre, v_norm_mix_post, v_norm_mlp_pre, v_norm_mlp_post, v_w_in, v_ssm_a_re, v_ssm_a_im, v_ssm_log_dt, v_ssm_b_re, v_ssm_b_im, v_ssm_c_re, v_ssm_c_im, v_ssm_d, v_w_glu, v_b_glu, v_attn_sinks, v_w_o_ssm, v_w_o_attn, v_w_out, v_w_up, v_w_down):
    args = locals()
    w = {n: args[n] for n in WEIGHTS}
    m = {n: args["m_" + n] for n in WEIGHTS}
    v = {n: args["v_" + n] for n in WEIGHTS}
    n_layers = w_in.shape[0]
    seq = x.shape[1]
    rows = seq + BLK
    my_slot = _slot(_mesh_pos())

    assert n_layers == 2
    xfer = {n: w[n].astype(XFER_DTYPE) for n in BIG}
    mixer_small = ("w_glu", "w_o_ssm", "w_o_attn", "w_out")
    gather_in0 = _exchange_start([meta_tokens, xfer["w_in"][0]], True, "gather_in0_start")
    gather_mix0 = _exchange_start([xfer[n][0] for n in mixer_small], True, "gather_mix0_start")
    gather_mlp0 = _exchange_start([xfer["w_up"][0], xfer["w_down"][0]], True, "gather_mlp0_start")
    gather_l1 = _exchange_start([xfer[n][1] for n in ("w_in",) + mixer_small + ("w_up", "w_down")], True,
                                "gather_l1_start")
    meta_g, w_in_g0 = _exchange_wait(gather_in0, [gather_mix0["token"], gather_mlp0["token"], gather_l1["token"]],
                                     "gather_in0_wait")
    meta_full = meta_g.transpose(1, 0, 2).reshape(N_META, D)

    def mixer_weights(w_glu_g, w_o_ssm_g, w_o_attn_g, w_out_g):
        return dict(w_glu=w_glu_g.reshape(D_SSM, D_SSM), w_o_ssm=w_o_ssm_g.transpose(1, 0, 2).reshape(D_SSM, D),
                    w_o_attn=w_o_attn_g.reshape(D_ATTN, D), w_out=w_out_g.reshape(D, D))

    gathered = [dict(w_in=w_in_g0), {}]

    gains = {n: w[n].reshape(n_layers, 1, D) for n in ("norm_mix_pre", "norm_mix_post", "norm_mlp_pre", "norm_mlp_post")}
    b_glu3 = b_glu.reshape(n_layers, 1, D_SSM)
    cos, sin_a, sin_b = _rope_tables(rows)

    def ssm_setup(l):
        disc, disc_vjp = jax.vjp(_ssm_discretize, ssm_a_re[l], ssm_a_im[l], ssm_log_dt[l], ssm_b_re[l], ssm_b_im[l])
        return _ssm_tables(*disc, ssm_c_re[l], ssm_c_im[l], ssm_d[l]), disc_vjp

    hres = jnp.concatenate([jnp.zeros((PAD_ROWS, D), F32), meta_full, x[0]], axis=0)

    saved = []
    for l in range(n_layers):
        ssm, disc_vjp = ssm_setup(l)
        wl = gathered[l]
        proj, h = _in_proj(hres, gains["norm_mix_pre"], wl["w_in"], l)
        q, k, vv = _rope_fwd(proj, cos, sin_a, sin_b, l)
        if l == 0:
            wl.update(mixer_weights(*_exchange_wait(gather_mix0, [q], "gather_mix0_wait")))
        y, y_ssm, carry_in = _s5_fwd(proj, ssm, wl["w_glu"], b_glu3, l)
        y_attn = _attn_fwd(q, k, vv, attn_sinks, l)
        merged, mix, hres_mid = _merge_fwd(y_ssm, y_attn, proj, hres, wl["w_o_ssm"], wl["w_o_attn"], wl["w_out"],
                                           gains["norm_mix_post"], l)
        if l == 0:
            wl["w_up"], wl["w_down"] = _exchange_wait(gather_mlp0, [hres_mid], "gather_mlp0_wait")
        up, h2, ff, hres_out = _mlp_fwd(hres_mid, gains["norm_mlp_pre"], gains["norm_mlp_post"], wl["w_up"],
                                        wl["w_down"], l)
        if l == 0:
            got = _exchange_wait(gather_l1, [hres_out], "gather_l1_wait")
            gathered[1] = dict(w_in=got[0], w_up=got[5], w_down=got[6], **mixer_weights(*got[1:5]))
        saved.append(dict(ssm=ssm, disc_vjp=disc_vjp, hres=hres, proj=proj, h=h, q=q, k=k, v=vv, y=y, y_ssm=y_ssm,
                          carry_in=carry_in, y_attn=y_attn, merged=merged, mix=mix, hres_mid=hres_mid,
                          up=up, h2=h2, ff=ff))
        hres = hres_out

    dhres, loss_vec = _loss_and_grad(hres, loss_target[0])
    loss = lax.psum(loss_vec[0, 0], MESH_AXES)

    small_grads = {n: [None] * n_layers for n in SMALL if n != "meta_tokens"}
    scatter_mlp, scatter_mix = [None] * n_layers, [None] * n_layers
    order_token = jnp.zeros((), F32)
    for l in reversed(range(n_layers)):
        s = saved[l]
        wl = gathered[l]
        dff, dup, dhm, dg_mlp_post, dg_mlp_pre = _mlp_bwd(dhres, s["ff"], s["up"], s["hres_mid"],
                                                          gains["norm_mlp_pre"] + order_token,
                                                          gains["norm_mlp_post"], wl["w_up"], wl["w_down"], l)
        dw_down = _matmul_tn(s["up"], dff, f"dw_down_l{l}", a_fn=_relu_squared).reshape(N_DEV, COL_SHARD, D)
        dw_up = _matmul_tn(s["h2"], dup, f"dw_up_l{l}", dev_major_cols=COL_SHARD)
        scatter_mlp[l] = _exchange_start([dw_up, dw_down], False, f"scatter_mlp{l}_start")
        dmix, da1, da2, dgs, dga, dy_ssm, dy_attn, dg_mix_post = _merge_bwd(
            dhm, s["mix"], s["y_ssm"], s["y_attn"], s["proj"], wl["w_o_ssm"], wl["w_o_attn"], wl["w_out"],
            gains["norm_mix_post"] + scatter_mlp[l]["token"][0, 0], l)
        dw_out = _matmul_tn(s["merged"], dmix, f"dw_out_l{l}").reshape(N_DEV, D // N_DEV, D)
        dw_o_attn = _matmul_tn(s["y_attn"], da2, f"dw_o_attn_l{l}").reshape(N_DEV, D_ATTN // N_DEV, D)
        dw_o_ssm = _matmul_tn(s["y_ssm"], da1, f"dw_o_ssm_l{l}", dev_major_cols=D // N_DEV)
        dq, dk, dv, dk_meta, dv_meta, dsink = _attn_bwd(s["q"], s["k"], s["v"], dy_attn, attn_sinks, l)
        dqkv = _rope_bwd(dq, dk, dv, dk_meta, dv_meta, cos, sin_a, sin_b, l)
        du, dw_glu, db_glu, dd_skip, db_mat, dc_mat, dab = _s5_bwd(dy_ssm, s["y"], s["proj"], s["carry_in"], s["ssm"],
                                                                    wl["w_glu"], b_glu3, l)
        dproj = (du, dqkv, dgs, dga)
        dw_in = jnp.concatenate([_matmul_tn(s["h"], piece, f"dw_in{k}_l{l}", dev_major_cols=COL_SHARD)
                                 for k, piece in enumerate(dproj)], axis=0)
        dhres, dg_mix_pre = _in_bwd(dproj, dhm, s["hres"], gains["norm_mix_pre"], wl["w_in"], l)
        mix_parts = [dw_in, dw_glu.astype(XFER_DTYPE).reshape(N_DEV, D_SSM // N_DEV, D_SSM), dw_o_ssm, dw_o_attn, dw_out]
        if l > 0:
            scatter_mix[l] = _exchange_start(mix_parts, False, f"scatter_mix{l}_start")
            order_token = scatter_mix[l]["token"][0, 0]

        dab = dab.reshape(N_SB, 2, SB_STATES)
        da_re, da_im, dlog_dt, db_re, db_im = s["disc_vjp"]((
            dab[:, 0].reshape(N_GROUPS, N_STATE), dab[:, 1].reshape(N_GROUPS, N_STATE),
            _block_diag_b_t(db_mat[..., :SB_STATES]), _block_diag_b_t(db_mat[..., SB_STATES:])))
        for name, val in (("norm_mix_pre", dg_mix_pre[0]), ("norm_mix_post", dg_mix_post[0]),
                          ("norm_mlp_pre", dg_mlp_pre[0]), ("norm_mlp_post", dg_mlp_post[0]),
                          ("ssm_a_re", da_re), ("ssm_a_im", da_im), ("ssm_log_dt", dlog_dt),
                          ("ssm_b_re", db_re), ("ssm_b_im", db_im),
                          ("ssm_c_re", _block_diag_c_t(dc_mat[:, :SB_STATES])),
                          ("ssm_c_im", -_block_diag_c_t(dc_mat[:, SB_STATES:])),
                          ("ssm_d", dd_skip.reshape(N_GROUPS, GROUP_CH)), ("b_glu", db_glu[0]),
                          ("attn_sinks", dsink[:, 0])):
            small_grads[name][l] = val

    grad_x = dhres[BLK:][None]
    small_names = [n for n in SMALL if n != "meta_tokens"]
    partial_small = [dhres[PAD_ROWS:BLK]] + [jnp.stack(small_grads[n]) for n in small_names]
    gather_small = _exchange_start([_pack(partial_small)], True, "gather_small_start")
    mix_parts[1] = mix_parts[1] + gather_small["token"][0, 0].astype(XFER_DTYPE)
    scatter_mix[0] = _exchange_start(mix_parts, False, "scatter_mix0_start")

    delta, new_m, new_v = {}, {}, {}

    def adamw_big(names, recv0, recv1):
        for n, p0, p1 in zip(names, recv0, recv1):
            grads[n], delta[n], new_m[n], new_v[n] = _adamw_layers(p0, p1, w[n], m[n], v[n], f"adamw_{n}")

    recv_mlp1 = _exchange_wait(scatter_mlp[1], [scatter_mix[0]["token"]], "scatter_mlp1_wait")
    recv_mix1 = _exchange_wait(scatter_mix[1], [recv_mlp1[0]], "scatter_mix1_wait")
    recv_mlp0 = _exchange_wait(scatter_mlp[0], [recv_mix1[0]], "scatter_mlp0_wait")
    grads = {}
    adamw_big(("w_up", "w_down"), recv_mlp0, recv_mlp1)
    small_parts, = _exchange_wait(gather_small, [delta["w_down"]], "gather_small_wait")
    summed = _unpack(_sum_slots(small_parts, "sum_small_grads"), partial_small)
    grads.update(zip(small_names, summed[1:]))
    grads["meta_tokens"] = lax.dynamic_slice_in_dim(summed[0], my_slot * (D // N_DEV), D // N_DEV, axis=1)
    like = [w[n] for n in SMALL]
    d_s, m_s, v_s = _adamw_packed(_pack([grads[n] for n in SMALL]), _pack(like), _pack([m[n] for n in SMALL]),
                                  _pack([v[n] for n in SMALL]), "adamw_small")
    recv_mix0 = _exchange_wait(scatter_mix[0], [d_s], "scatter_mix0_wait")
    adamw_big(("w_in",) + mixer_small, recv_mix0, recv_mix1)
    for n, dd, mm, vs in zip(SMALL, _unpack(d_s, like), _unpack(m_s, like), _unpack(v_s, like)):
        delta[n], new_m[n], new_v[n] = dd, mm, vs

    return (loss, grad_x, *[grads[n] for n in WEIGHTS], *[delta[n] for n in WEIGHTS],
            *[new_m[n] for n in WEIGHTS], *[new_v[n] for n in WEIGHTS])
```

```python
import functools
import math

import jax
import jax.numpy as jnp
from jax import lax
from jax.experimental import pallas as pl
from jax.experimental.pallas import tpu as pltpu

F32 = jnp.float32
MXU_DTYPE = jnp.bfloat16
XFER_DTYPE = MXU_DTYPE
_pcall = pl.pallas_call
SDS = jax.ShapeDtypeStruct

D = 1024
D_SSM = 512
D_ATTN = 1024
D_KV = 256
D_FF = 4096
D_IN = 4096
HEAD_DIM = 64
N_Q_HEADS = 16
N_KV_HEADS = 4
Q_PER_KV = 4
N_META = 16
BLK = 128
PAD_ROWS = BLK - N_META
N_GROUPS = 32
N_STATE = 64
GROUP_CH = 16
N_SB = 4
SB_STATES = 512
ROPE_THETA = 10000.0
ATTN_SCALE = HEAD_DIM ** -0.5
NEG_INF = -1e30
RMS_EPS = 1e-6
N_DEV = 8
COL_SHARD = 512

ADAM_LR = 0.001
ADAM_B1 = 0.9
ADAM_B2 = 0.999
ADAM_EPS = 1e-08
ADAM_WD = 0.01
ADAM_STEP = 10

VMEM_LIMIT = 56 * 1024 * 1024
MESH_AXES = ("x", "y", "c")

_NT = (((1,), (1,)), ((), ()))
_TN = (((0,), (0,)), ((), ()))


def _cparams(*sem):
    return pltpu.CompilerParams(dimension_semantics=tuple(sem) if sem else None,
                                vmem_limit_bytes=VMEM_LIMIT)


def _row_tile(rows, cap=640):
    for t in (640, 512, 320, 256, 128):
        if t <= cap and rows % t == 0:
            return t
    raise ValueError(f"unsupported row count {rows}")


def _dot(a, b):
    return jnp.dot(a, b, preferred_element_type=F32)


def _dot_nt(a, b):
    return lax.dot_general(a, b, _NT, preferred_element_type=F32)


def _dot_tn(a, b):
    return lax.dot_general(a, b, _TN, preferred_element_type=F32)


def _sigmoid(x):
    return 1.0 / (1.0 + jnp.exp(-x))


_GELU_C = math.sqrt(2.0 / math.pi)


def _gelu_parts(y):
    t = jnp.tanh(_GELU_C * (y + 0.044715 * (y * y * y)))
    return 0.5 * y * (1.0 + t), t


def _gelu_grad(y, t):
    return 0.5 * (1.0 + t) + 0.5 * y * (1.0 - t * t) * (_GELU_C * (1.0 + 0.134145 * (y * y)))


def _rms_fwd(x, gain):
    r = lax.rsqrt(jnp.mean(x * x, axis=-1, keepdims=True) + RMS_EPS)
    return (x * r) * gain


def _rms_bwd(x, gain, dout):
    r = lax.rsqrt(jnp.mean(x * x, axis=-1, keepdims=True) + RMS_EPS)
    xh = x * r
    dxh = dout * gain
    dx = r * (dxh - xh * jnp.mean(dxh * xh, axis=-1, keepdims=True))
    return dx, jnp.sum(dout * xh, axis=0, keepdims=True)


def _mesh_pos():
    return lax.axis_index("x"), lax.axis_index("y"), lax.axis_index("c")


def _peer(pos, d):
    x, y, c = pos
    return (1 - x if d & 4 else x, 1 - y if d & 2 else y, 1 - c if d & 1 else c)


def _slot(pos):
    return 4 * pos[0] + 2 * pos[1] + pos[2]


_HBM = pl.BlockSpec(memory_space=pltpu.HBM)
_SEM = pl.BlockSpec(memory_space=pltpu.SEMAPHORE)
_DATAFLOW = pltpu.SideEffectType.DATAFLOW_SIDE_EFFECTING


def _exchange_copy(gather, src_ref, land_ref, sems, k, d, me, send_side):
    peer = _peer(me, d)
    sender = me if send_side else peer
    src = src_ref if gather else src_ref.at[_slot(peer) if send_side else _slot(me)]
    return pltpu.make_async_remote_copy(
        src_ref=src, dst_ref=land_ref.at[_slot(sender)],
        send_sem=sems[0].at[k * (N_DEV - 1) + d - 1], recv_sem=sems[1].at[k * (N_DEV - 1) + d - 1],
        device_id=peer, device_id_type=pl.DeviceIdType.MESH)


def _exchange_start(srcs, gather, name):
    n = len(srcs)
    lands = [lax.empty(((N_DEV,) + s.shape) if gather else s.shape, s.dtype) for s in srcs]

    def body(*refs):
        src_refs, land_refs = refs[:n], refs[n:2 * n]
        sems = refs[2 * n:2 * n + 2]
        token, local_sems = refs[4 * n + 2], refs[4 * n + 3]
        me = _mesh_pos()
        own = [pltpu.make_async_copy(src_refs[k] if gather else src_refs[k].at[_slot(me)],
                                     land_refs[k].at[_slot(me)], local_sems.at[k]) for k in range(n)]
        for cp in own:
            cp.start()
        for cp in own:
            cp.wait()
        for k in range(n):
            for d in range(1, N_DEV):
                _exchange_copy(gather, src_refs[k], land_refs[k], sems, k, d, me, True).start()
        token[...] = jnp.zeros_like(token)

    sem_type = pltpu.SemaphoreType.DMA((n * (N_DEV - 1),))
    outs = _pcall(
        body, name=name,
        out_shape=(sem_type, sem_type, *[pltpu.HBM(a.shape, a.dtype) for a in list(srcs) + lands], SDS((8, 128), F32)),
        in_specs=[_HBM] * (2 * n),
        out_specs=(_SEM, _SEM, *[_HBM] * (2 * n), pl.BlockSpec(memory_space=pltpu.VMEM)),
        input_output_aliases={k: 2 + k for k in range(2 * n)},
        scratch_shapes=[pltpu.SemaphoreType.DMA((n,))],
        compiler_params=pltpu.CompilerParams(has_side_effects=_DATAFLOW),
    )(*[pltpu.with_memory_space_constraint(a, pltpu.HBM) for a in list(srcs) + lands])
    return dict(sems=outs[:2], srcs=outs[2:2 + n], lands=outs[2 + n:2 + 2 * n], token=outs[-1], gather=gather)


def _exchange_wait(started, after, name):
    n = len(started["srcs"])
    gather = started["gather"]

    def body(*refs):
        src_refs, land_refs = refs[:n], refs[n:2 * n]
        sems = refs[2 * n:2 * n + 2]
        me = _mesh_pos()
        for k in range(n):
            for d in range(1, N_DEV):
                _exchange_copy(gather, src_refs[k], land_refs[k], sems, k, d, me, True).wait_send()
        for k in range(n):
            for d in range(1, N_DEV):
                _exchange_copy(gather, src_refs[k], land_refs[k], sems, k, d, me, False).wait_recv()

    arrays = list(started["srcs"]) + list(started["lands"])
    outs = _pcall(
        body, name=name,
        out_shape=tuple(pltpu.HBM(a.shape, a.dtype) for a in arrays),
        in_specs=[_HBM] * (2 * n) + [_SEM, _SEM] + [pl.BlockSpec(memory_space=pl.ANY)] * len(after),
        out_specs=tuple([_HBM] * (2 * n)),
        input_output_aliases={k: k for k in range(2 * n)},
        compiler_params=pltpu.CompilerParams(has_side_effects=_DATAFLOW),
    )(*arrays, *started["sems"], *after)
    return list(outs[n:])


def _load_resident(w_hbm, w_scr, sems, first_step):
    @pl.when(first_step)
    def _():
        copies = [pltpu.make_async_copy(w_hbm.at[s], w_scr.at[s], sems.at[s]) for s in range(N_DEV)]
        for cp in copies:
            cp.start()
        for cp in copies:
            cp.wait()


def _rope_lanes(t, cos, sin_a, sin_b):
    return t * cos + pltpu.roll(t, 96, 1) * sin_a + pltpu.roll(t, 32, 1) * sin_b


def _in_proj(hres, gain3, w_in_g, cos, sin_a, sin_b, layer):
    rows = hres.shape[0]
    tm = _row_tile(rows)

    def body(x_ref, g_ref, w_hbm, c_ref, a_ref, b_ref, u_ref, gate_ref, q_ref, k_ref, v_ref, h_ref,
             h_scr, w_scr, w_sem):
        j = pl.program_id(1)
        _load_resident(w_hbm, w_scr, w_sem, (pl.program_id(0) == 0) & (j == 0))

        @pl.when(j == 0)
        def _():
            hn = _rms_fwd(x_ref[...], g_ref[...]).astype(MXU_DTYPE)
            h_scr[...] = hn
            h_ref[...] = hn
            u_ref[...] = _dot(hn, w_scr[0])

        @pl.when((j == 1) | (j == 2))
        def _():
            res = _dot(h_scr[...], w_scr[j])
            c, a, b = c_ref[...], a_ref[...], b_ref[...]
            for t in range(4):
                lanes = slice(t * 128, (t + 1) * 128)
                q_ref[:, lanes] = (_rope_lanes(res[:, lanes], c, a, b) * ATTN_SCALE).astype(MXU_DTYPE)

        @pl.when(j == 3)
        def _():
            res = _dot(h_scr[...], w_scr[3])
            c, a, b = c_ref[...], a_ref[...], b_ref[...]
            for t in range(2):
                lanes = slice(t * 128, (t + 1) * 128)
                k_ref[:, lanes] = _rope_lanes(res[:, lanes], c, a, b).astype(MXU_DTYPE)
            v_ref[...] = res[:, D_KV:].astype(MXU_DTYPE)

        @pl.when(j >= 4)
        def _():
            gate_ref[...] = _dot(h_scr[...], w_scr[j])

    tab = pl.BlockSpec((tm, 128), lambda i, j: (i, 0))
    kv = pl.BlockSpec((tm, D_KV), lambda i, j: (i, 0))
    return _pcall(
        body, name=f"in_proj_l{layer}", grid=(rows // tm, N_DEV),
        in_specs=[pl.BlockSpec((tm, D), lambda i, j: (i, 0)),
                  pl.BlockSpec((None, 1, D), lambda i, j: (layer, 0, 0)),
                  pl.BlockSpec(memory_space=pl.ANY), tab, tab, tab],
        out_specs=[pl.BlockSpec((tm, COL_SHARD), lambda i, j: (i, 0)),
                   pl.BlockSpec((tm, COL_SHARD), lambda i, j: (i, jnp.clip(j - 4, 0, 3))),
                   pl.BlockSpec((tm, COL_SHARD), lambda i, j: (i, jnp.clip(j - 1, 0, 1))),
                   kv, kv, pl.BlockSpec((tm, D), lambda i, j: (i, 0))],
        out_shape=[SDS((rows, D_SSM), F32), SDS((rows, 2 * D), F32), SDS((rows, D_ATTN), MXU_DTYPE),
                   SDS((rows, D_KV), MXU_DTYPE), SDS((rows, D_KV), MXU_DTYPE), SDS((rows, D), MXU_DTYPE)],
        scratch_shapes=[pltpu.VMEM((tm, D), MXU_DTYPE), pltpu.VMEM((N_DEV, D, COL_SHARD), MXU_DTYPE),
                        pltpu.SemaphoreType.DMA((N_DEV,))],
        compiler_params=_cparams("arbitrary", "arbitrary"),
    )(hres, gain3, w_in_g, cos, sin_a, sin_b)


SCAN_TILE = 8


def _scan_tiles(x_ref, out_ref, tre_ref, tim_ref, sb, t_r, t_i, reverse, prev_ref=None):
    base = 4 if reverse else 0
    n_tiles = BLK // SCAN_TILE
    row = lax.broadcasted_iota(jnp.int32, (SCAN_TILE, SB_STATES), 0)
    for j in (range(n_tiles - 1, -1, -1) if reverse else range(n_tiles)):
        rows = slice(SCAN_TILE * j, SCAN_TILE * (j + 1))
        xr = x_ref[rows, :SB_STATES]
        xi = x_ref[rows, SB_STATES:]
        for k in range(3):
            shift = SCAN_TILE - (1 << k) if reverse else (1 << k)
            rr = pltpu.roll(xr, shift, 0)
            ri = pltpu.roll(xi, shift, 0)
            ar = tre_ref[sb, base + k]
            ai = tim_ref[sb, base + k]
            xr, xi = xr + (ar * rr - ai * ri), xi + (ar * ri + ai * rr)
        pr = tre_ref[sb, base + 3]
        pi = tim_ref[sb, base + 3]
        xr, xi = xr + (pr * t_r - pi * t_i), xi + (pr * t_i + pi * t_r)
        out_ref[rows, :SB_STATES] = xr
        out_ref[rows, SB_STATES:] = xi
        if prev_ref is not None:
            prev_ref[rows, :SB_STATES] = jnp.where(row == 0, t_r, pltpu.roll(xr, 1, 0))
            prev_ref[rows, SB_STATES:] = jnp.where(row == 0, t_i, pltpu.roll(xi, 1, 0))
        edge = slice(0, 1) if reverse else slice(SCAN_TILE - 1, SCAN_TILE)
        t_r, t_i = xr[edge], xi[edge]
    return t_r, t_i


def _s5_fwd(u, ssm, w_glu, b_glu3, layer):
    rows = u.shape[0]
    n_chunks = rows // BLK
    b_mat, c_mat, t_re, t_im, d_skip = (ssm[k] for k in ("b_mat", "c_mat", "t_re", "t_im", "d_skip"))

    def body(u_ref, bm_ref, cm_ref, tre_ref, tim_ref, d_ref, wg_ref, bg_ref,
             y_ref, ys_ref, cin_ref, carry, bu_scr, s_scr):
        @pl.when(pl.program_id(0) == 0)
        def _():
            carry[...] = jnp.zeros_like(carry)

        cin_ref[...] = carry[...]
        u = u_ref[...]
        for sb in range(N_SB):
            cols = slice(sb * 128, (sb + 1) * 128)
            u_sb = u[:, cols]
            bu_scr[sb] = _dot(u_sb.astype(MXU_DTYPE), bm_ref[sb])
            t_r, t_i = _scan_tiles(bu_scr.at[sb], s_scr.at[sb], tre_ref, tim_ref, sb,
                                   carry[2 * sb:2 * sb + 1, :], carry[2 * sb + 1:2 * sb + 2, :], False)
            carry[2 * sb:2 * sb + 1, :] = t_r
            carry[2 * sb + 1:2 * sb + 2, :] = t_i
            y_ref[:, cols] = _dot(s_scr[sb].astype(MXU_DTYPE), cm_ref[sb]) + d_ref[:, cols] * u_sb
        z, _ = _gelu_parts(y_ref[...])
        gl = _dot(z.astype(MXU_DTYPE), wg_ref[...]) + bg_ref[...]
        ys_ref[...] = (z * _sigmoid(gl)).astype(MXU_DTYPE)

    full = lambda shape: pl.BlockSpec(shape, lambda j: (0,) * len(shape))
    return _pcall(
        body, name=f"s5_fwd_l{layer}", grid=(n_chunks,),
        in_specs=[pl.BlockSpec((BLK, D_SSM), lambda j: (j, 0)),
                  full((N_SB, 128, 2 * SB_STATES)), full((N_SB, 2 * SB_STATES, 128)),
                  full((N_SB, 8, SCAN_TILE, SB_STATES)), full((N_SB, 8, SCAN_TILE, SB_STATES)),
                  full((1, D_SSM)), full((D_SSM, D_SSM)),
                  pl.BlockSpec((None, 1, D_SSM), lambda j: (layer, 0, 0))],
        out_specs=[pl.BlockSpec((BLK, D_SSM), lambda j: (j, 0)), pl.BlockSpec((BLK, D_SSM), lambda j: (j, 0)),
                   pl.BlockSpec((None, 8, SB_STATES), lambda j: (j, 0, 0))],
        out_shape=[SDS((rows, D_SSM), F32), SDS((rows, D_SSM), MXU_DTYPE), SDS((n_chunks, 8, SB_STATES), F32)],
        scratch_shapes=[pltpu.VMEM((8, SB_STATES), F32), pltpu.VMEM((N_SB, BLK, 2 * SB_STATES), F32),
                        pltpu.VMEM((N_SB, BLK, 2 * SB_STATES), F32)],
        compiler_params=_cparams("arbitrary"),
    )(u, b_mat, c_mat, t_re, t_im, d_skip, w_glu, b_glu3)


def _attn_mask(i):
    row = lax.broadcasted_iota(jnp.int32, (BLK, 3 * BLK), 0) + i * BLK
    col = lax.broadcasted_iota(jnp.int32, (BLK, 3 * BLK), 1)
    seg = jnp.right_shift(col, 7)
    c = jnp.bitwise_and(col, BLK - 1)
    kidx = c + (i + seg - 2) * BLK
    ok_meta = (seg == 0) & (c >= PAD_ROWS) & (row - c >= BLK)
    ok_win = (seg > 0) & (kidx >= PAD_ROWS) & (kidx <= row) & (row - kidx < BLK)
    return jnp.where(ok_meta | ok_win, 0.0, NEG_INF)


def _head_lanes(h):
    return slice(h * HEAD_DIM, (h + 1) * HEAD_DIM)


def _group_rows(ref, kvh):
    return jnp.concatenate([ref[:, _head_lanes(kvh * Q_PER_KV + g)] for g in range(Q_PER_KV)], axis=0)


def _group_bias(bias, sink_ref, layer, kvh):
    first_col = lax.broadcasted_iota(jnp.int32, (BLK, BLK), 1) == 0
    slabs = []
    for g in range(Q_PER_KV):
        first = jnp.where(first_col, sink_ref[layer, kvh * Q_PER_KV + g], bias[:, :BLK])
        slabs.append(jnp.concatenate([first, bias[:, BLK:]], axis=1))
    return jnp.concatenate(slabs, axis=0)


def _attn_probs(q4, k3, bias4):
    s = _dot_nt(q4, k3) + bias4
    e = jnp.exp(s - jnp.max(s, axis=-1, keepdims=True))
    return e * (1.0 / jnp.sum(e, axis=-1, keepdims=True))


def _attn_fwd(q, k, v, sinks, layer):
    rows = q.shape[0]
    n_blk = rows // BLK

    def body(sink_ref, q_ref, km_ref, kp_ref, kc_ref, vm_ref, vp_ref, vc_ref, o_ref):
        bias = _attn_mask(pl.program_id(0))
        for kvh in range(N_KV_HEADS):
            lanes = _head_lanes(kvh)
            k3 = jnp.concatenate([km_ref[:, lanes], kp_ref[:, lanes], kc_ref[:, lanes]], axis=0)
            v3 = jnp.concatenate([vm_ref[:, lanes], vp_ref[:, lanes], vc_ref[:, lanes]], axis=0)
            p = _attn_probs(_group_rows(q_ref, kvh), k3, _group_bias(bias, sink_ref, layer, kvh))
            o4 = _dot(p.astype(MXU_DTYPE), v3).astype(MXU_DTYPE)
            for g in range(Q_PER_KV):
                o_ref[:, _head_lanes(kvh * Q_PER_KV + g)] = o4[g * BLK:(g + 1) * BLK]

    kv_meta = pl.BlockSpec((BLK, D_KV), lambda i: (0, 0))
    kv_prev = pl.BlockSpec((BLK, D_KV), lambda i: (jnp.maximum(i - 1, 0), 0))
    kv_cur = pl.BlockSpec((BLK, D_KV), lambda i: (i, 0))
    return _pcall(
        body, name=f"attn_fwd_l{layer}", grid=(n_blk,),
        in_specs=[pl.BlockSpec(memory_space=pltpu.SMEM),
                  pl.BlockSpec((BLK, D_ATTN), lambda i: (i, 0)),
                  kv_meta, kv_prev, kv_cur, kv_meta, kv_prev, kv_cur],
        out_specs=pl.BlockSpec((BLK, D_ATTN), lambda i: (i, 0)),
        out_shape=SDS((rows, D_ATTN), MXU_DTYPE),
        compiler_params=_cparams("parallel"),
    )(sinks, q, k, k, k, v, v, v)


def _merge_fwd(y_ssm, y_attn, gates, hres, w_o_ssm, w_o_attn, w_out, gain3, layer):
    rows = hres.shape[0]
    tm = _row_tile(rows, 320)

    def body(ys_ref, ya_ref, gs_ref, ga_ref, x_ref, wos_ref, woa_ref, wout_ref, g_ref,
             mg_ref, mix_ref, out_ref):
        a1 = _dot(ys_ref[...], wos_ref[...])
        a2 = _dot(ya_ref[...], woa_ref[...])
        merged = (_sigmoid(gs_ref[...]) * a1 + _sigmoid(ga_ref[...]) * a2).astype(MXU_DTYPE)
        mg_ref[...] = merged
        mix = _dot(merged, wout_ref[...])
        mix_ref[...] = mix
        out_ref[...] = x_ref[...] + _rms_fwd(mix, g_ref[...])

    row_d = pl.BlockSpec((tm, D), lambda i: (i, 0))
    full = lambda shape: pl.BlockSpec(shape, lambda i: (0,) * len(shape))
    return _pcall(
        body, name=f"merge_fwd_l{layer}", grid=(rows // tm,),
        in_specs=[pl.BlockSpec((tm, D_SSM), lambda i: (i, 0)), row_d,
                  row_d, pl.BlockSpec((tm, D), lambda i: (i, 1)), row_d,
                  full((D_SSM, D)), full((D_ATTN, D)), full((D, D)),
                  pl.BlockSpec((None, 1, D), lambda i: (layer, 0, 0))],
        out_specs=[row_d, row_d, row_d],
        out_shape=[SDS((rows, D), MXU_DTYPE), SDS((rows, D), F32), SDS((rows, D), F32)],
        compiler_params=_cparams("parallel"),
    )(y_ssm, y_attn, gates, gates, hres, w_o_ssm, w_o_attn, w_out, gain3)


def _mlp_fwd(hres, gain_pre3, gain_post3, w_up_g, w_down_g, layer):
    rows = hres.shape[0]
    tm = _row_tile(rows)

    def body(x_ref, gp_ref, gq_ref, wu_hbm, wd_hbm, up_ref, h_ref, ff_ref, out_ref,
             h_scr, acc, wu_scr, wd_scr, wu_sem, wd_sem):
        kf = pl.program_id(1)
        first = (pl.program_id(0) == 0) & (kf == 0)
        _load_resident(wu_hbm, wu_scr, wu_sem, first)
        _load_resident(wd_hbm, wd_scr, wd_sem, first)

        @pl.when(kf == 0)
        def _():
            hn = _rms_fwd(x_ref[...], gp_ref[...]).astype(MXU_DTYPE)
            h_scr[...] = hn
            h_ref[...] = hn
            acc[...] = jnp.zeros_like(acc)

        up = _dot(h_scr[...], wu_scr[kf])
        up_ref[...] = up.astype(MXU_DTYPE)
        r = jnp.maximum(up, 0.0)
        acc[...] += _dot((r * r).astype(MXU_DTYPE), wd_scr[kf])

        @pl.when(kf == N_DEV - 1)
        def _():
            ff = acc[...]
            ff_ref[...] = ff
            out_ref[...] = x_ref[...] + _rms_fwd(ff, gq_ref[...])

    row_d = pl.BlockSpec((tm, D), lambda i, k: (i, 0))
    gain = pl.BlockSpec((None, 1, D), lambda i, k: (layer, 0, 0))
    return _pcall(
        body, name=f"mlp_fwd_l{layer}", grid=(rows // tm, N_DEV),
        in_specs=[row_d, gain, gain, pl.BlockSpec(memory_space=pl.ANY), pl.BlockSpec(memory_space=pl.ANY)],
        out_specs=[pl.BlockSpec((tm, COL_SHARD), lambda i, k: (i, k)), row_d, row_d, row_d],
        out_shape=[SDS((rows, D_FF), MXU_DTYPE), SDS((rows, D), MXU_DTYPE), SDS((rows, D), F32), SDS((rows, D), F32)],
        scratch_shapes=[pltpu.VMEM((tm, D), MXU_DTYPE), pltpu.VMEM((tm, D), F32),
                        pltpu.VMEM((N_DEV, D, COL_SHARD), MXU_DTYPE), pltpu.VMEM((N_DEV, COL_SHARD, D), MXU_DTYPE),
                        pltpu.SemaphoreType.DMA((N_DEV,)), pltpu.SemaphoreType.DMA((N_DEV,))],
        compiler_params=_cparams("arbitrary", "arbitrary"),
    )(hres, gain_pre3, gain_post3, w_up_g, w_down_g)


def _loss_and_grad(hres, target):
    rows = hres.shape[0]
    n_blk = rows // BLK

    def body(y_ref, t_ref, dy_ref, loss_ref):
        i = pl.program_id(0)

        @pl.when(i == 0)
        def _():
            dy_ref[...] = jnp.zeros_like(dy_ref)
            loss_ref[...] = jnp.zeros_like(loss_ref)

        @pl.when(i > 0)
        def _():
            err = y_ref[...] - t_ref[...]
            dy_ref[...] = err * (1.0 / D)
            loss_ref[...] += jnp.sum(err * err) * (0.5 / D)

    return _pcall(
        body, name="loss", grid=(n_blk,),
        in_specs=[pl.BlockSpec((BLK, D), lambda i: (i, 0)),
                  pl.BlockSpec((BLK, D), lambda i: (jnp.maximum(i - 1, 0), 0))],
        out_specs=[pl.BlockSpec((BLK, D), lambda i: (i, 0)), pl.BlockSpec((1, 128), lambda i: (0, 0))],
        out_shape=[SDS((rows, D), F32), SDS((1, 128), F32)],
        compiler_params=_cparams("arbitrary"),
    )(hres, target)


def _relu_squared(up):
    r = jnp.maximum(up.astype(F32), 0.0)
    return (r * r).astype(MXU_DTYPE)


def _matmul_tn(a, b, name, dev_major_cols=None, a_fn=None):
    rows, ka = a.shape
    n = b.shape[1]
    ta = min(ka, 1024)
    tn = 1024 if n % 1024 == 0 else 512
    tr = _row_tile(rows)
    n_r = rows // tr

    def body(a_ref, b_ref, o_ref, acc):
        r = pl.program_id(2)

        @pl.when(r == 0)
        def _():
            acc[...] = jnp.zeros_like(acc)

        a_blk = a_ref[...] if a_fn is None else a_fn(a_ref[...])
        acc[...] += _dot_tn(a_blk, b_ref[...])

        @pl.when(r == n_r - 1)
        def _():
            if dev_major_cols is None:
                o_ref[...] = acc[...].astype(XFER_DTYPE)
            else:
                for s in range(tn // dev_major_cols):
                    o_ref[s] = acc[:, s * dev_major_cols:(s + 1) * dev_major_cols].astype(XFER_DTYPE)

    if dev_major_cols is None:
        out_spec = pl.BlockSpec((ta, tn), lambda i, j, r: (i, j))
        out_shape = SDS((ka, n), XFER_DTYPE)
    else:
        w = dev_major_cols
        out_spec = pl.BlockSpec((tn // w, ta, w), lambda i, j, r: (j, i, 0))
        out_shape = SDS((n // w, ka, w), XFER_DTYPE)
    return _pcall(
        body, name=name, grid=(ka // ta, n // tn, n_r),
        in_specs=[pl.BlockSpec((tr, ta), lambda i, j, r: (r, i)), pl.BlockSpec((tr, tn), lambda i, j, r: (r, j))],
        out_specs=out_spec, out_shape=out_shape,
        scratch_shapes=[pltpu.VMEM((ta, tn), F32)],
        compiler_params=_cparams("parallel", "parallel", "arbitrary"),
    )(a, b)


def _mlp_bwd(dout, ff, up, hres_mid, gain_pre3, gain_post3, w_up_g, w_down_g, layer):
    rows = dout.shape[0]
    tm = _row_tile(rows)

    def body(do_ref, ff_ref, up_ref, x_ref, gp_ref, gq_ref, wu_hbm, wd_hbm,
             dff_ref, dup_ref, dx_ref, dgq_ref, dgp_ref, dff_scr, acc, wu_scr, wd_scr, wu_sem, wd_sem):
        i = pl.program_id(0)
        kf = pl.program_id(1)
        _load_resident(wu_hbm, wu_scr, wu_sem, (i == 0) & (kf == 0))
        _load_resident(wd_hbm, wd_scr, wd_sem, (i == 0) & (kf == 0))

        @pl.when((i == 0) & (kf == 0))
        def _():
            dgq_ref[...] = jnp.zeros_like(dgq_ref)
            dgp_ref[...] = jnp.zeros_like(dgp_ref)

        @pl.when(kf == 0)
        def _():
            dff, dg = _rms_bwd(ff_ref[...], gq_ref[...], do_ref[...])
            dgq_ref[...] += dg
            dffb = dff.astype(MXU_DTYPE)
            dff_scr[...] = dffb
            dff_ref[...] = dffb
            acc[...] = jnp.zeros_like(acc)

        dact = _dot_nt(dff_scr[...], wd_scr[kf])
        dup = (dact * (2.0 * jnp.maximum(up_ref[...].astype(F32), 0.0))).astype(MXU_DTYPE)
        dup_ref[...] = dup
        acc[...] += _dot_nt(dup, wu_scr[kf])

        @pl.when(kf == N_DEV - 1)
        def _():
            dx, dg = _rms_bwd(x_ref[...], gp_ref[...], acc[...])
            dgp_ref[...] += dg
            dx_ref[...] = do_ref[...] + dx

    row_d = pl.BlockSpec((tm, D), lambda i, k: (i, 0))
    gain = pl.BlockSpec((None, 1, D), lambda i, k: (layer, 0, 0))
    dgain = pl.BlockSpec((1, D), lambda i, k: (0, 0))
    return _pcall(
        body, name=f"mlp_bwd_l{layer}", grid=(rows // tm, N_DEV),
        in_specs=[row_d, row_d, pl.BlockSpec((tm, COL_SHARD), lambda i, k: (i, k)), row_d, gain, gain,
                  pl.BlockSpec(memory_space=pl.ANY), pl.BlockSpec(memory_space=pl.ANY)],
        out_specs=[row_d, pl.BlockSpec((tm, COL_SHARD), lambda i, k: (i, k)), row_d, dgain, dgain],
        out_shape=[SDS((rows, D), MXU_DTYPE), SDS((rows, D_FF), MXU_DTYPE), SDS((rows, D), F32),
                   SDS((1, D), F32), SDS((1, D), F32)],
        scratch_shapes=[pltpu.VMEM((tm, D), MXU_DTYPE), pltpu.VMEM((tm, D), F32),
                        pltpu.VMEM((N_DEV, D, COL_SHARD), MXU_DTYPE), pltpu.VMEM((N_DEV, COL_SHARD, D), MXU_DTYPE),
                        pltpu.SemaphoreType.DMA((N_DEV,)), pltpu.SemaphoreType.DMA((N_DEV,))],
        compiler_params=_cparams("arbitrary", "arbitrary"),
    )(dout, ff, up, hres_mid, gain_pre3, gain_post3, w_up_g, w_down_g)


def _merge_bwd(dhm, mix, y_ssm, y_attn, gates, w_o_ssm, w_o_attn, w_out, gain3, layer):
    rows = dhm.shape[0]
    tm = _row_tile(rows, 320)

    def body(dh_ref, mix_ref, ys_ref, ya_ref, gs_ref, ga_ref, wos_ref, woa_ref, wout_ref, g_ref,
             dmix_ref, da1_ref, da2_ref, dgs_ref, dga_ref, dys_ref, dya_ref, dg_ref):
        @pl.when(pl.program_id(0) == 0)
        def _():
            dg_ref[...] = jnp.zeros_like(dg_ref)

        dmix, dg = _rms_bwd(mix_ref[...], g_ref[...], dh_ref[...])
        dg_ref[...] += dg
        dmixb = dmix.astype(MXU_DTYPE)
        dmix_ref[...] = dmixb
        dmerged = _dot_nt(dmixb, wout_ref[...])
        sg_s = _sigmoid(gs_ref[...])
        sg_a = _sigmoid(ga_ref[...])
        da1 = (dmerged * sg_s).astype(MXU_DTYPE)
        da2 = (dmerged * sg_a).astype(MXU_DTYPE)
        da1_ref[...] = da1
        da2_ref[...] = da2
        a1 = _dot(ys_ref[...], wos_ref[...])
        a2 = _dot(ya_ref[...], woa_ref[...])
        dgs_ref[...] = (dmerged * a1 * (sg_s * (1.0 - sg_s))).astype(MXU_DTYPE)
        dga_ref[...] = (dmerged * a2 * (sg_a * (1.0 - sg_a))).astype(MXU_DTYPE)
        dys_ref[...] = _dot_nt(da1, wos_ref[...])
        dya_ref[...] = _dot_nt(da2, woa_ref[...])

    row_d = pl.BlockSpec((tm, D), lambda i: (i, 0))
    full = lambda shape: pl.BlockSpec(shape, lambda i: (0,) * len(shape))
    return _pcall(
        body, name=f"merge_bwd_l{layer}", grid=(rows // tm,),
        in_specs=[row_d, row_d, pl.BlockSpec((tm, D_SSM), lambda i: (i, 0)), row_d,
                  row_d, pl.BlockSpec((tm, D), lambda i: (i, 1)),
                  full((D_SSM, D)), full((D_ATTN, D)), full((D, D)),
                  pl.BlockSpec((None, 1, D), lambda i: (layer, 0, 0))],
        out_specs=[row_d, row_d, row_d, row_d, row_d, pl.BlockSpec((tm, D_SSM), lambda i: (i, 0)), row_d,
                   pl.BlockSpec((1, D), lambda i: (0, 0))],
        out_shape=[SDS((rows, D), MXU_DTYPE)] * 5 + [SDS((rows, D_SSM), F32), SDS((rows, D_ATTN), F32),
                                                      SDS((1, D), F32)],
        compiler_params=_cparams("arbitrary"),
    )(dhm, mix, y_ssm, y_attn, gates, gates, w_o_ssm, w_o_attn, w_out, gain3)


def _attn_bwd(q, k, v, d_out, sinks, layer):
    rows = q.shape[0]
    n_blk = rows // BLK
    last = n_blk - 1

    def body(sink_ref, q_ref, km_ref, kp_ref, kc_ref, vm_ref, vp_ref, vc_ref, do_ref,
             dq_ref, dk_ref, dv_ref, dkm_ref, dvm_ref, ds_ref, dk_carry, dv_carry):
        i = pl.program_id(0)

        @pl.when(i == 0)
        def _():
            dkm_ref[...] = jnp.zeros_like(dkm_ref)
            dvm_ref[...] = jnp.zeros_like(dvm_ref)
            ds_ref[...] = jnp.zeros_like(ds_ref)
            dk_carry[...] = jnp.zeros_like(dk_carry)
            dv_carry[...] = jnp.zeros_like(dv_carry)

        @pl.when(i <= last)
        def _():
            bias = _attn_mask(i)
            for kvh in range(N_KV_HEADS):
                lanes = _head_lanes(kvh)
                k3 = jnp.concatenate([km_ref[:, lanes], kp_ref[:, lanes], kc_ref[:, lanes]], axis=0)
                v3 = jnp.concatenate([vm_ref[:, lanes], vp_ref[:, lanes], vc_ref[:, lanes]], axis=0)
                q4 = _group_rows(q_ref, kvh)
                do4 = _group_rows(do_ref, kvh).astype(MXU_DTYPE)
                p = _attn_probs(q4, k3, _group_bias(bias, sink_ref, layer, kvh))
                dp = _dot_nt(do4, v3)
                dsf = p * (dp - jnp.sum(dp * p, axis=-1, keepdims=True))
                dsc = dsf.astype(MXU_DTYPE)
                dv3 = _dot_tn(p.astype(MXU_DTYPE), do4)
                dk3 = _dot_tn(dsc, q4)
                dq4 = _dot(dsc, k3)
                for g in range(Q_PER_KV):
                    h = kvh * Q_PER_KV + g
                    dq_ref[:, _head_lanes(h)] = dq4[g * BLK:(g + 1) * BLK]
                    ds_ref[h:h + 1, :] += jnp.sum(dsf[g * BLK:(g + 1) * BLK, 0:BLK], axis=0, keepdims=True)
                dkm_ref[:, lanes] += dk3[0:BLK]
                dvm_ref[:, lanes] += dv3[0:BLK]
                dk_ref[:, lanes] = dk_carry[:, lanes] + dk3[BLK:2 * BLK]
                dv_ref[:, lanes] = dv_carry[:, lanes] + dv3[BLK:2 * BLK]
                dk_carry[:, lanes] = dk3[2 * BLK:3 * BLK]
                dv_carry[:, lanes] = dv3[2 * BLK:3 * BLK]

        @pl.when(i == last + 1)
        def _():
            dk_ref[...] = dk_carry[...]
            dv_ref[...] = dv_carry[...]

    cur = lambda i: (jnp.minimum(i, last), 0)
    prev = lambda i: (jnp.clip(i - 1, 0, last), 0)
    kv_meta = pl.BlockSpec((BLK, D_KV), lambda i: (0, 0))
    kv_prev = pl.BlockSpec((BLK, D_KV), prev)
    kv_cur = pl.BlockSpec((BLK, D_KV), cur)
    return _pcall(
        body, name=f"attn_bwd_l{layer}", grid=(n_blk + 1,),
        in_specs=[pl.BlockSpec(memory_space=pltpu.SMEM),
                  pl.BlockSpec((BLK, D_ATTN), cur),
                  kv_meta, kv_prev, kv_cur, kv_meta, kv_prev, kv_cur,
                  pl.BlockSpec((BLK, D_ATTN), cur)],
        out_specs=[pl.BlockSpec((BLK, D_ATTN), cur), kv_prev, kv_prev, kv_meta, kv_meta,
                   pl.BlockSpec((N_Q_HEADS, 128), lambda i: (0, 0))],
        out_shape=[SDS((rows, D_ATTN), F32), SDS((rows, D_KV), F32), SDS((rows, D_KV), F32),
                   SDS((BLK, D_KV), F32), SDS((BLK, D_KV), F32), SDS((N_Q_HEADS, 128), F32)],
        scratch_shapes=[pltpu.VMEM((BLK, D_KV), F32), pltpu.VMEM((BLK, D_KV), F32)],
        compiler_params=_cparams("arbitrary"),
    )(sinks, q, k, k, k, v, v, v, d_out)


def _rope_bwd(dq, dk, dv, dk_meta, dv_meta, cos, sin_a, sin_b, layer):
    rows = dq.shape[0]
    tm = _row_tile(rows)

    def body(dq_ref, dk_ref, dv_ref, dkm_ref, dvm_ref, c_ref, a_ref, b_ref, o_ref):
        c, a, b = c_ref[...], -a_ref[...], -b_ref[...]
        for t in range(8):
            x = dq_ref[:, t * 128:(t + 1) * 128]
            o_ref[:, t * 128:(t + 1) * 128] = (_rope_lanes(x, c, a, b) * ATTN_SCALE).astype(MXU_DTYPE)
        for t in range(2):
            x = dk_ref[:, t * 128:(t + 1) * 128]
            o_ref[:, D_ATTN + t * 128:D_ATTN + (t + 1) * 128] = _rope_lanes(x, c, a, b).astype(MXU_DTYPE)
        o_ref[:, D_ATTN + D_KV:] = dv_ref[...].astype(MXU_DTYPE)

        @pl.when(pl.program_id(0) == 0)
        def _():
            cb, ab, bb = c[0:BLK], a[0:BLK], b[0:BLK]
            is_meta = lax.broadcasted_iota(jnp.int32, (BLK, 128), 0) >= PAD_ROWS
            for t in range(2):
                x = dk_ref[0:BLK, t * 128:(t + 1) * 128] + jnp.where(is_meta, dkm_ref[:, t * 128:(t + 1) * 128], 0.0)
                o_ref[0:BLK, D_ATTN + t * 128:D_ATTN + (t + 1) * 128] = _rope_lanes(x, cb, ab, bb).astype(MXU_DTYPE)
                xv = dv_ref[0:BLK, t * 128:(t + 1) * 128] + jnp.where(is_meta, dvm_ref[:, t * 128:(t + 1) * 128], 0.0)
                o_ref[0:BLK, D_ATTN + D_KV + t * 128:D_ATTN + D_KV + (t + 1) * 128] = xv.astype(MXU_DTYPE)

    tab = pl.BlockSpec((tm, 128), lambda i: (i, 0))
    kv = pl.BlockSpec((tm, D_KV), lambda i: (i, 0))
    meta = pl.BlockSpec((BLK, D_KV), lambda i: (0, 0))
    return _pcall(
        body, name=f"rope_bwd_l{layer}", grid=(rows // tm,),
        in_specs=[pl.BlockSpec((tm, D_ATTN), lambda i: (i, 0)), kv, kv, meta, meta, tab, tab, tab],
        out_specs=pl.BlockSpec((tm, D_ATTN + 2 * D_KV), lambda i: (i, 0)),
        out_shape=SDS((rows, D_ATTN + 2 * D_KV), MXU_DTYPE),
        compiler_params=_cparams("parallel"),
    )(dq, dk, dv, dk_meta, dv_meta, cos, sin_a, sin_b)


def _s5_bwd(d_gated, y, u, carry_in, ssm, w_glu, b_glu3, layer):
    rows = y.shape[0]
    n_chunks = rows // BLK
    b_mat, c_mat, t_re, t_im, d_skip = (ssm[k] for k in ("b_mat", "c_mat", "t_re", "t_im", "d_skip"))

    def body(dz_ref, y_ref, u_ref, cin_ref, bm_ref, cm_ref, tre_ref, tim_ref, d_ref, wg_ref, bg_ref,
             du_ref, dwg_ref, dbg_ref, dd_ref, dbm_ref, dcm_ref, dab_ref,
             lam_carry, bu_scr, s_scr, sp_scr, g_scr, lam_scr):
        step = pl.program_id(0)
        chunk = n_chunks - 1 - step

        @pl.when(step == 0)
        def _():
            for r in (dwg_ref, dbg_ref, dd_ref, dbm_ref, dcm_ref, dab_ref, lam_carry):
                r[...] = jnp.zeros_like(r)

        y = y_ref[...]
        u = u_ref[...]
        d_o = dz_ref[...]
        z, t = _gelu_parts(y)
        zb = z.astype(MXU_DTYPE)
        sg = _sigmoid(_dot(zb, wg_ref[...]) + bg_ref[...])
        dgl = d_o * z * (sg * (1.0 - sg))
        dglb = dgl.astype(MXU_DTYPE)
        dz = d_o * sg + _dot_nt(dglb, wg_ref[...])
        dwg_ref[...] += _dot_tn(zb, dglb)
        dbg_ref[...] += jnp.sum(dgl, axis=0, keepdims=True)
        dy = dz * _gelu_grad(y, t)
        dd_ref[...] += jnp.sum(dy * u, axis=0, keepdims=True)
        grow = lax.broadcasted_iota(jnp.int32, (BLK, 128), 0) + chunk * BLK
        for sb in range(N_SB):
            cols = slice(sb * 128, (sb + 1) * 128)
            u_sb = u[:, cols].astype(MXU_DTYPE)
            dy_sb = dy[:, cols]
            dyb = dy_sb.astype(MXU_DTYPE)
            bu_scr[sb] = _dot(u_sb, bm_ref[sb])
            _scan_tiles(bu_scr.at[sb], s_scr.at[sb], tre_ref, tim_ref, sb,
                        cin_ref[2 * sb:2 * sb + 1, :], cin_ref[2 * sb + 1:2 * sb + 2, :], False, prev_ref=sp_scr.at[sb])
            dcm_ref[sb] += _dot_tn(s_scr[sb].astype(MXU_DTYPE), dyb)
            g_scr[sb] = _dot_nt(dyb, cm_ref[sb])
            n_r, n_i = _scan_tiles(g_scr.at[sb], lam_scr.at[sb], tre_ref, tim_ref, sb,
                                   lam_carry[2 * sb:2 * sb + 1, :], lam_carry[2 * sb + 1:2 * sb + 2, :], True)
            lam_carry[2 * sb:2 * sb + 1, :] = n_r
            lam_carry[2 * sb + 1:2 * sb + 2, :] = n_i
            lr, li = lam_scr[sb, :, :SB_STATES], lam_scr[sb, :, SB_STATES:]
            spr, spi = sp_scr[sb, :, :SB_STATES], sp_scr[sb, :, SB_STATES:]
            dab_ref[2 * sb:2 * sb + 1, :] += jnp.sum(spr * lr + spi * li, axis=0, keepdims=True)
            dab_ref[2 * sb + 1:2 * sb + 2, :] += jnp.sum(spr * li - spi * lr, axis=0, keepdims=True)
            lam = lam_scr[sb].astype(MXU_DTYPE)
            dbm_ref[sb] += _dot_tn(u_sb, lam)
            du = _dot_nt(lam, bm_ref[sb]) + d_ref[:, cols] * dy_sb
            du_ref[:, cols] = jnp.where(grow >= PAD_ROWS, du, 0.0).astype(MXU_DTYPE)

    rev = lambda j: (n_chunks - 1 - j, 0)
    full = lambda shape: pl.BlockSpec(shape, lambda j: (0,) * len(shape))
    tables = [full((N_SB, 8, SCAN_TILE, SB_STATES))] * 2
    chunk_scratch = pltpu.VMEM((N_SB, BLK, 2 * SB_STATES), F32)
    return _pcall(
        body, name=f"s5_bwd_l{layer}", grid=(n_chunks,),
        in_specs=[pl.BlockSpec((BLK, D_SSM), rev), pl.BlockSpec((BLK, D_SSM), rev), pl.BlockSpec((BLK, D_SSM), rev),
                  pl.BlockSpec((None, 8, SB_STATES), lambda j: (n_chunks - 1 - j, 0, 0)),
                  full((N_SB, 128, 2 * SB_STATES)), full((N_SB, 2 * SB_STATES, 128))] + tables + [
                  full((1, D_SSM)), full((D_SSM, D_SSM)),
                  pl.BlockSpec((None, 1, D_SSM), lambda j: (layer, 0, 0))],
        out_specs=[pl.BlockSpec((BLK, D_SSM), rev), full((D_SSM, D_SSM)), full((1, D_SSM)), full((1, D_SSM)),
                   full((N_SB, 128, 2 * SB_STATES)), full((N_SB, 2 * SB_STATES, 128)), full((8, SB_STATES))],
        out_shape=[SDS((rows, D_SSM), MXU_DTYPE), SDS((D_SSM, D_SSM), F32), SDS((1, D_SSM), F32), SDS((1, D_SSM), F32),
                   SDS((N_SB, 128, 2 * SB_STATES), F32), SDS((N_SB, 2 * SB_STATES, 128), F32), SDS((8, SB_STATES), F32)],
        scratch_shapes=[pltpu.VMEM((8, SB_STATES), F32)] + [chunk_scratch] * 5,
        compiler_params=_cparams("arbitrary"),
    )(d_gated, y, u, carry_in, b_mat, c_mat, t_re, t_im, d_skip, w_glu, b_glu3)


DPROJ_PIECES = ((0, 1), (1, 3), (4, 2), (6, 2))


def _in_bwd(dproj_pieces, dhm, hres, gain3, w_in_g, layer):
    rows = hres.shape[0]
    tm = _row_tile(rows)

    def body(du_ref, dqkv_ref, dgs_ref, dga_ref, dh_ref, x_ref, g_ref, w_hbm, dx_ref, dg_ref, acc, w_scr, w_sem):
        i = pl.program_id(0)
        j = pl.program_id(1)
        _load_resident(w_hbm, w_scr, w_sem, (i == 0) & (j == 0))

        @pl.when((i == 0) & (j == 0))
        def _():
            dg_ref[...] = jnp.zeros_like(dg_ref)

        @pl.when(j == 0)
        def _():
            acc[...] = jnp.zeros_like(acc)

        for piece_ref, (first, count) in zip((du_ref, dqkv_ref, dgs_ref, dga_ref), DPROJ_PIECES):
            @pl.when((j >= first) & (j < first + count))
            def _():
                acc[...] += _dot_nt(piece_ref[...], w_scr[j])

        @pl.when(j == N_DEV - 1)
        def _():
            dx, dg = _rms_bwd(x_ref[...], g_ref[...], acc[...])
            dg_ref[...] += dg
            dx_ref[...] = dh_ref[...] + dx

    row_d = pl.BlockSpec((tm, D), lambda i, j: (i, 0))

    def piece_spec(first, count):
        return pl.BlockSpec((tm, COL_SHARD), lambda i, j: (i, jnp.clip(j - first, 0, count - 1)))

    return _pcall(
        body, name=f"in_bwd_l{layer}", grid=(rows // tm, N_DEV),
        in_specs=[piece_spec(*p) for p in DPROJ_PIECES] + [
                  row_d, row_d,
                  pl.BlockSpec((None, 1, D), lambda i, j: (layer, 0, 0)),
                  pl.BlockSpec(memory_space=pl.ANY)],
        out_specs=[row_d, pl.BlockSpec((1, D), lambda i, j: (0, 0))],
        out_shape=[SDS((rows, D), F32), SDS((1, D), F32)],
        scratch_shapes=[pltpu.VMEM((tm, D), F32), pltpu.VMEM((N_DEV, D, COL_SHARD), MXU_DTYPE),
                        pltpu.SemaphoreType.DMA((N_DEV,))],
        compiler_params=_cparams("arbitrary", "arbitrary"),
    )(*dproj_pieces, dhm, hres, gain3, w_in_g)


_ADAM_C1 = 1.0 / (1.0 - ADAM_B1 ** ADAM_STEP)
_ADAM_C2 = 1.0 / (1.0 - ADAM_B2 ** ADAM_STEP)


def _adam_math(w, g, m, v):
    m = ADAM_B1 * m + (1.0 - ADAM_B1) * g
    v = ADAM_B2 * v + (1.0 - ADAM_B2) * (g * g)
    delta = -ADAM_LR * ((m * _ADAM_C1) / (jnp.sqrt(v * _ADAM_C2) + ADAM_EPS) + ADAM_WD * w)
    return delta, m, v


def _adamw_layers(parts0, parts1, w, m, v, name):
    _, rows, cols = w.shape
    tr = min(rows, (1 << 16) // cols)
    nt = rows // tr

    def body(p0_ref, p1_ref, w_ref, m_ref, v_ref, g_ref, d_ref, nm_ref, nv_ref):
        layer = pl.program_id(0)

        def run(p_ref):
            g = p_ref[0].astype(F32)
            for s in range(1, N_DEV):
                g = g + p_ref[s].astype(F32)
            delta, nm, nv = _adam_math(w_ref[...], g, m_ref[...], v_ref[...])
            g_ref[...] = g
            d_ref[...] = delta
            nm_ref[...] = nm
            nv_ref[...] = nv

        @pl.when(layer == 0)
        def _():
            run(p0_ref)

        @pl.when(layer == 1)
        def _():
            run(p1_ref)

    wspec = pl.BlockSpec((None, tr, cols), lambda l, i: (l, i, 0))
    return _pcall(
        body, name=name, grid=(2, nt),
        in_specs=[pl.BlockSpec((N_DEV, tr, cols), lambda l, i: (0, jnp.where(l == 0, i, nt - 1), 0)),
                  pl.BlockSpec((N_DEV, tr, cols), lambda l, i: (0, jnp.where(l == 1, i, 0), 0)),
                  wspec, wspec, wspec],
        out_specs=[wspec] * 4, out_shape=[SDS(w.shape, F32)] * 4,
        compiler_params=_cparams("arbitrary", "arbitrary"),
    )(parts0, parts1, w, m, v)


def _sum_slots(parts, name):
    def body(p_ref, o_ref):
        acc = p_ref[0]
        for s in range(1, N_DEV):
            acc = acc + p_ref[s]
        o_ref[...] = acc

    vmem = pl.BlockSpec(memory_space=pltpu.VMEM)
    return _pcall(body, name=name, out_shape=SDS(parts.shape[1:], F32), in_specs=[vmem], out_specs=vmem,
                  compiler_params=_cparams())(parts)


def _adamw_packed(g, w, m, v, name):
    def body(g_ref, w_ref, m_ref, v_ref, d_ref, nm_ref, nv_ref):
        delta, nm, nv = _adam_math(w_ref[...], g_ref[...], m_ref[...], v_ref[...])
        d_ref[...] = delta
        nm_ref[...] = nm
        nv_ref[...] = nv

    vmem = pl.BlockSpec(memory_space=pltpu.VMEM)
    return _pcall(body, name=name, out_shape=[SDS(g.shape, F32)] * 3, in_specs=[vmem] * 4, out_specs=[vmem] * 3,
                  compiler_params=_cparams())(g, w, m, v)


def _ssm_discretize(a_re, a_im, log_dt, b_re, b_im):
    dt = jnp.exp(log_dt)[:, None]
    mag = jnp.exp(a_re * dt)
    ang = a_im * dt
    ab_re, ab_im = mag * jnp.cos(ang), mag * jnp.sin(ang)
    xr, xi = ab_re - 1.0, ab_im
    den = a_re * a_re + a_im * a_im
    q_re = (xr * a_re + xi * a_im) / den
    q_im = (xi * a_re - xr * a_im) / den
    bb_re = q_re[..., None] * b_re - q_im[..., None] * b_im
    bb_im = q_re[..., None] * b_im + q_im[..., None] * b_re
    return ab_re, ab_im, bb_re, bb_im


def _block_diag_b(bb):
    m = jnp.einsum("sgnc,gh->sgchn", bb.reshape(N_SB, 8, N_STATE, GROUP_CH), jnp.eye(8, dtype=F32))
    return m.reshape(N_SB, 128, SB_STATES)


def _block_diag_b_t(dm):
    return jnp.einsum("sgchn,gh->sgnc", dm.reshape(N_SB, 8, GROUP_CH, 8, N_STATE),
                      jnp.eye(8, dtype=F32)).reshape(N_GROUPS, N_STATE, GROUP_CH)


def _block_diag_c(cc):
    m = jnp.einsum("sgcn,gh->sgnhc", cc.reshape(N_SB, 8, GROUP_CH, N_STATE), jnp.eye(8, dtype=F32))
    return m.reshape(N_SB, SB_STATES, 128)


def _block_diag_c_t(dm):
    return jnp.einsum("sgnhc,gh->sgcn", dm.reshape(N_SB, 8, N_STATE, 8, GROUP_CH),
                      jnp.eye(8, dtype=F32)).reshape(N_GROUPS, GROUP_CH, N_STATE)


def _ssm_tables(ab_re, ab_im, bb_re, bb_im, c_re, c_im, d_skip):
    pr, pi = ab_re.reshape(1, -1), ab_im.reshape(1, -1)
    cr, ci = pr, pi
    squares = []
    for _ in range(3):
        squares.append((cr, ci))
        pr, pi = (jnp.concatenate([pr, pr * cr - pi * ci], axis=0),
                  jnp.concatenate([pi, pr * ci + pi * cr], axis=0))
        cr, ci = cr * cr - ci * ci, 2.0 * cr * ci
    r = jnp.arange(SCAN_TILE)[:, None]
    fwd = [(jnp.where(r >= (1 << k), squares[k][0], 0.0), jnp.where(r >= (1 << k), squares[k][1], 0.0))
           for k in range(3)] + [(pr, pi)]
    rev = [(jnp.where(r < SCAN_TILE - (1 << k), squares[k][0], 0.0),
            jnp.where(r < SCAN_TILE - (1 << k), -squares[k][1], 0.0)) for k in range(3)] + [(pr[::-1], -pi[::-1])]
    table = lambda part: jnp.stack([e[part] for e in fwd + rev]).reshape(
        8, SCAN_TILE, N_SB, SB_STATES).transpose(2, 0, 1, 3)
    return dict(
        b_mat=jnp.concatenate([_block_diag_b(bb_re), _block_diag_b(bb_im)], axis=-1).astype(MXU_DTYPE),
        c_mat=jnp.concatenate([_block_diag_c(c_re), -_block_diag_c(c_im)], axis=1).astype(MXU_DTYPE),
        t_re=table(0), t_im=table(1),
        d_skip=d_skip.reshape(1, D_SSM))


def _rope_tables(rows):
    pos = (jnp.arange(rows, dtype=jnp.int32) - PAD_ROWS).astype(F32)
    inv_freq = 1.0 / (ROPE_THETA ** (jnp.arange(0, HEAD_DIM, 2, dtype=F32) / HEAD_DIM))
    ang = pos[:, None] * inv_freq[None, :]
    ang = jnp.concatenate([ang, ang, ang, ang], axis=-1)
    first_half = (jnp.arange(128) % HEAD_DIM) < HEAD_DIM // 2
    sin = jnp.sin(ang)
    return jnp.cos(ang), jnp.where(first_half, -sin, 0.0), jnp.where(first_half, 0.0, sin)


def _pack(arrays):
    flat = jnp.concatenate([a.reshape(-1).astype(F32) for a in arrays])
    pad = (-flat.shape[0]) % 1024
    return jnp.pad(flat, (0, pad)).reshape(-1, 128)


def _unpack(packed, like):
    flat = packed.reshape(-1)
    out, off = [], 0
    for a in like:
        n = math.prod(a.shape)
        out.append(flat[off:off + n].reshape(a.shape))
        off += n
    return out


BIG = ("w_in", "w_glu", "w_o_ssm", "w_o_attn", "w_out", "w_up", "w_down")
WEIGHTS = ("meta_tokens", "norm_mix_pre", "norm_mix_post", "norm_mlp_pre", "norm_mlp_post", "w_in",
           "ssm_a_re", "ssm_a_im", "ssm_log_dt", "ssm_b_re", "ssm_b_im", "ssm_c_re", "ssm_c_im", "ssm_d",
           "w_glu", "b_glu", "attn_sinks", "w_o_ssm", "w_o_attn", "w_out", "w_up", "w_down")
SMALL = tuple(n for n in WEIGHTS if n not in BIG)


def kernel(x, meta_tokens, norm_mix_pre, norm_mix_post, norm_mlp_pre, norm_mlp_post, w_in, ssm_a_re, ssm_a_im, ssm_log_dt, ssm_b_re, ssm_b_im, ssm_c_re, ssm_c_im, ssm_d, w_glu, b_glu, attn_sinks, w_o_ssm, w_o_attn, w_out, w_up, w_down, loss_target, m_meta_tokens, m_norm_mix_pre, m_norm_mix_post, m_norm_mlp_pre, m_norm_mlp_post, m_w_in, m_ssm_a_re, m_ssm_a_im, m_ssm_log_dt, m_ssm_b_re, m_ssm_b_im, m_ssm_c_re, m_ssm_c_im, m_ssm_d, m_w_glu, m_b_glu, m_attn_sinks, m_w_o_ssm, m_w_o_attn, m_w_out, m_w_up, m_w_down, v_meta_tokens, v_norm_mix_pre, v_norm_mix_post, v_norm_mlp_pre, v_norm_mlp_post, v_w_in, v_ssm_a_re, v_ssm_a_im, v_ssm_log_dt, v_ssm_b_re, v_ssm_b_im, v_ssm_c_re, v_ssm_c_im, v_ssm_d, v_w_glu, v_b_glu, v_attn_sinks, v_w_o_ssm, v_w_o_attn, v_w_out, v_w_up, v_w_down):
    args = locals()
    w = {n: args[n] for n in WEIGHTS}
    m = {n: args["m_" + n] for n in WEIGHTS}
    v = {n: args["v_" + n] for n in WEIGHTS}
    n_layers = w_in.shape[0]
    seq = x.shape[1]
    rows = seq + BLK
    my_slot = _slot(_mesh_pos())

    assert n_layers == 2
    xfer = {n: w[n].astype(XFER_DTYPE) for n in BIG}
    mixer_small = ("w_glu", "w_o_ssm", "w_o_attn", "w_out")
    gather_in0 = _exchange_start([meta_tokens, xfer["w_in"][0]], True, "gather_in0_start")
    gather_mix0 = _exchange_start([xfer[n][0] for n in mixer_small], True, "gather_mix0_start")
    gather_mlp0 = _exchange_start([xfer["w_up"][0], xfer["w_down"][0]], True, "gather_mlp0_start")
    gather_l1 = _exchange_start([xfer[n][1] for n in ("w_in",) + mixer_small + ("w_up", "w_down")], True,
                                "gather_l1_start")
    meta_g, w_in_g0 = _exchange_wait(gather_in0, [gather_mix0["token"], gather_mlp0["token"], gather_l1["token"]],
                                     "gather_in0_wait")
    meta_full = meta_g.transpose(1, 0, 2).reshape(N_META, D)

    def mixer_weights(w_glu_g, w_o_ssm_g, w_o_attn_g, w_out_g):
        return dict(w_glu=w_glu_g.reshape(D_SSM, D_SSM), w_o_ssm=w_o_ssm_g.transpose(1, 0, 2).reshape(D_SSM, D),
                    w_o_attn=w_o_attn_g.reshape(D_ATTN, D), w_out=w_out_g.reshape(D, D))

    gathered = [dict(w_in=w_in_g0), {}]

    gains = {n: w[n].reshape(n_layers, 1, D) for n in ("norm_mix_pre", "norm_mix_post", "norm_mlp_pre", "norm_mlp_post")}
    b_glu3 = b_glu.reshape(n_layers, 1, D_SSM)
    cos, sin_a, sin_b = _rope_tables(rows)

    def ssm_setup(l):
        disc, disc_vjp = jax.vjp(_ssm_discretize, ssm_a_re[l], ssm_a_im[l], ssm_log_dt[l], ssm_b_re[l], ssm_b_im[l])
        return _ssm_tables(*disc, ssm_c_re[l], ssm_c_im[l], ssm_d[l]), disc_vjp

    hres = jnp.concatenate([jnp.zeros((PAD_ROWS, D), F32), meta_full, x[0]], axis=0)

    saved = []
    for l in range(n_layers):
        ssm, disc_vjp = ssm_setup(l)
        wl = gathered[l]
        u, gates, q, k, vv, h = _in_proj(hres, gains["norm_mix_pre"], wl["w_in"], cos, sin_a, sin_b, l)
        if l == 0:
            wl.update(mixer_weights(*_exchange_wait(gather_mix0, [q], "gather_mix0_wait")))
        y, y_ssm, carry_in = _s5_fwd(u, ssm, wl["w_glu"], b_glu3, l)
        y_attn = _attn_fwd(q, k, vv, attn_sinks, l)
        merged, mix, hres_mid = _merge_fwd(y_ssm, y_attn, gates, hres, wl["w_o_ssm"], wl["w_o_attn"], wl["w_out"],
                                           gains["norm_mix_post"], l)
        if l == 0:
            wl["w_up"], wl["w_down"] = _exchange_wait(gather_mlp0, [hres_mid], "gather_mlp0_wait")
        up, h2, ff, hres_out = _mlp_fwd(hres_mid, gains["norm_mlp_pre"], gains["norm_mlp_post"], wl["w_up"],
                                        wl["w_down"], l)
        if l == 0:
            got = _exchange_wait(gather_l1, [hres_out], "gather_l1_wait")
            gathered[1] = dict(w_in=got[0], w_up=got[5], w_down=got[6], **mixer_weights(*got[1:5]))
        saved.append(dict(ssm=ssm, disc_vjp=disc_vjp, hres=hres, u=u, gates=gates, h=h, q=q, k=k, v=vv, y=y, y_ssm=y_ssm,
                          carry_in=carry_in, y_attn=y_attn, merged=merged, mix=mix, hres_mid=hres_mid,
                          up=up, h2=h2, ff=ff))
        hres = hres_out

    dhres, loss_vec = _loss_and_grad(hres, loss_target[0])
    loss = lax.psum(loss_vec[0, 0], MESH_AXES)

    small_grads = {n: [None] * n_layers for n in SMALL if n != "meta_tokens"}
    scatter_up, scatter_down, scatter_mix = [None] * n_layers, [None] * n_layers, [None] * n_layers
    order_token = jnp.zeros((), F32)
    for l in reversed(range(n_layers)):
        s = saved[l]
        wl = gathered[l]
        dff, dup, dhm, dg_mlp_post, dg_mlp_pre = _mlp_bwd(dhres, s["ff"], s["up"], s["hres_mid"],
                                                          gains["norm_mlp_pre"] + order_token,
                                                          gains["norm_mlp_post"], wl["w_up"], wl["w_down"], l)
        dw_up = _matmul_tn(s["h2"], dup, f"dw_up_l{l}", dev_major_cols=COL_SHARD)
        scatter_up[l] = _exchange_start([dw_up], False, f"scatter_up{l}_start")
        dw_down = _matmul_tn(s["up"], dff, f"dw_down_l{l}", a_fn=_relu_squared).reshape(N_DEV, COL_SHARD, D)
        scatter_down[l] = _exchange_start([dw_down], False, f"scatter_down{l}_start")
        dmix, da1, da2, dgs, dga, dy_ssm, dy_attn, dg_mix_post = _merge_bwd(
            dhm, s["mix"], s["y_ssm"], s["y_attn"], s["gates"], wl["w_o_ssm"], wl["w_o_attn"], wl["w_out"],
            gains["norm_mix_post"] + (scatter_up[l]["token"][0, 0] + scatter_down[l]["token"][0, 0]), l)
        dw_out = _matmul_tn(s["merged"], dmix, f"dw_out_l{l}").reshape(N_DEV, D // N_DEV, D)
        dw_o_attn = _matmul_tn(s["y_attn"], da2, f"dw_o_attn_l{l}").reshape(N_DEV, D_ATTN // N_DEV, D)
        dw_o_ssm = _matmul_tn(s["y_ssm"], da1, f"dw_o_ssm_l{l}", dev_major_cols=D // N_DEV)
        dq, dk, dv, dk_meta, dv_meta, dsink = _attn_bwd(s["q"], s["k"], s["v"], dy_attn, attn_sinks, l)
        dqkv = _rope_bwd(dq, dk, dv, dk_meta, dv_meta, cos, sin_a, sin_b, l)
        du, dw_glu, db_glu, dd_skip, db_mat, dc_mat, dab = _s5_bwd(dy_ssm, s["y"], s["u"], s["carry_in"], s["ssm"],
                                                                    wl["w_glu"], b_glu3, l)
        dproj = (du, dqkv, dgs, dga)
        dw_in = jnp.concatenate([_matmul_tn(s["h"], piece, f"dw_in{k}_l{l}", dev_major_cols=COL_SHARD)
                                 for k, piece in enumerate(dproj)], axis=0)
        dhres, dg_mix_pre = _in_bwd(dproj, dhm, s["hres"], gains["norm_mix_pre"], wl["w_in"], l)
        mix_parts = [dw_in, dw_glu.astype(XFER_DTYPE).reshape(N_DEV, D_SSM // N_DEV, D_SSM), dw_o_ssm, dw_o_attn, dw_out]
        if l > 0:
            scatter_mix[l] = _exchange_start(mix_parts, False, f"scatter_mix{l}_start")
            order_token = scatter_mix[l]["token"][0, 0]

        dab = dab.reshape(N_SB, 2, SB_STATES)
        da_re, da_im, dlog_dt, db_re, db_im = s["disc_vjp"]((
            dab[:, 0].reshape(N_GROUPS, N_STATE), dab[:, 1].reshape(N_GROUPS, N_STATE),
            _block_diag_b_t(db_mat[..., :SB_STATES]), _block_diag_b_t(db_mat[..., SB_STATES:])))
        for name, val in (("norm_mix_pre", dg_mix_pre[0]), ("norm_mix_post", dg_mix_post[0]),
                          ("norm_mlp_pre", dg_mlp_pre[0]), ("norm_mlp_post", dg_mlp_post[0]),
                          ("ssm_a_re", da_re), ("ssm_a_im", da_im), ("ssm_log_dt", dlog_dt),
                          ("ssm_b_re", db_re), ("ssm_b_im", db_im),
                          ("ssm_c_re", _block_diag_c_t(dc_mat[:, :SB_STATES])),
                          ("ssm_c_im", -_block_diag_c_t(dc_mat[:, SB_STATES:])),
                          ("ssm_d", dd_skip.reshape(N_GROUPS, GROUP_CH)), ("b_glu", db_glu[0]),
                          ("attn_sinks", dsink[:, 0])):
            small_grads[name][l] = val

    grad_x = dhres[BLK:][None]
    small_names = [n for n in SMALL if n != "meta_tokens"]
    partial_small = [dhres[PAD_ROWS:BLK]] + [jnp.stack(small_grads[n]) for n in small_names]
    gather_small = _exchange_start([_pack(partial_small)], True, "gather_small_start")
    mix_parts[1] = mix_parts[1] + gather_small["token"][0, 0].astype(XFER_DTYPE)
    scatter_mix[0] = _exchange_start(mix_parts, False, "scatter_mix0_start")

    delta, new_m, new_v = {}, {}, {}

    def adamw_big(names, recv0, recv1):
        for n, p0, p1 in zip(names, recv0, recv1):
            grads[n], delta[n], new_m[n], new_v[n] = _adamw_layers(p0, p1, w[n], m[n], v[n], f"adamw_{n}")

    recv_up1 = _exchange_wait(scatter_up[1], [scatter_mix[0]["token"]], "scatter_up1_wait")
    recv_down1 = _exchange_wait(scatter_down[1], [recv_up1[0]], "scatter_down1_wait")
    recv_mix1 = _exchange_wait(scatter_mix[1], [recv_down1[0]], "scatter_mix1_wait")
    recv_up0 = _exchange_wait(scatter_up[0], [recv_mix1[0]], "scatter_up0_wait")
    recv_down0 = _exchange_wait(scatter_down[0], [recv_up0[0]], "scatter_down0_wait")
    grads = {}
    adamw_big(("w_up", "w_down"), recv_up0 + recv_down0, recv_up1 + recv_down1)
    small_parts, = _exchange_wait(gather_small, [delta["w_down"]], "gather_small_wait")
    summed = _unpack(_sum_slots(small_parts, "sum_small_grads"), partial_small)
    grads.update(zip(small_names, summed[1:]))
    grads["meta_tokens"] = lax.dynamic_slice_in_dim(summed[0], my_slot * (D // N_DEV), D // N_DEV, axis=1)
    like = [w[n] for n in SMALL]
    d_s, m_s, v_s = _adamw_packed(_pack([grads[n] for n in SMALL]), _pack(like), _pack([m[n] for n in SMALL]),
                                  _pack([v[n] for n in SMALL]), "adamw_small")
    recv_mix0 = _exchange_wait(scatter_mix[0], [d_s], "scatter_mix0_wait")
    adamw_big(("w_in",) + mixer_small, recv_mix0, recv_mix1)
    for n, dd, mm, vs in zip(SMALL, _unpack(d_s, like), _unpack(m_s, like), _unpack(v_s, like)):
        delta[n], new_m[n], new_v[n] = dd, mm, vs

    return (loss, grad_x, *[grads[n] for n in WEIGHTS], *[delta[n] for n in WEIGHTS],
            *[new_m[n] for n in WEIGHTS], *[new_v[n] for n in WEIGHTS])
```

```python
import functools
import math

import jax
import jax.numpy as jnp
from jax import lax
from jax.experimental import pallas as pl
from jax.experimental.pallas import tpu as pltpu
from jax.experimental.pallas import tpu_sc as plsc

F32 = jnp.float32
MXU_DTYPE = jnp.bfloat16
XFER_DTYPE = MXU_DTYPE
_pcall = pl.pallas_call
SDS = jax.ShapeDtypeStruct

D = 1024
D_SSM = 512
D_ATTN = 1024
D_KV = 256
D_FF = 4096
D_IN = 4096
HEAD_DIM = 64
N_Q_HEADS = 16
N_KV_HEADS = 4
Q_PER_KV = 4
N_META = 16
BLK = 128
PAD_ROWS = BLK - N_META
N_GROUPS = 32
N_STATE = 64
GROUP_CH = 16
N_SB = 4
SB_STATES = 512
ROPE_THETA = 10000.0
ATTN_SCALE = HEAD_DIM ** -0.5
NEG_INF = -1e30
RMS_EPS = 1e-6
N_DEV = 8
COL_SHARD = 512

ADAM_LR = 0.001
ADAM_B1 = 0.9
ADAM_B2 = 0.999
ADAM_EPS = 1e-08
ADAM_WD = 0.01
ADAM_STEP = 10

VMEM_LIMIT = 56 * 1024 * 1024
MESH_AXES = ("x", "y", "c")

_NT = (((1,), (1,)), ((), ()))
_TN = (((0,), (0,)), ((), ()))


def _cparams(*sem):
    return pltpu.CompilerParams(dimension_semantics=tuple(sem) if sem else None,
                                vmem_limit_bytes=VMEM_LIMIT)


def _row_tile(rows, cap=640):
    for t in (640, 512, 320, 256, 128):
        if t <= cap and rows % t == 0:
            return t
    raise ValueError(f"unsupported row count {rows}")


def _dot(a, b):
    return jnp.dot(a, b, preferred_element_type=F32)


def _dot_nt(a, b):
    return lax.dot_general(a, b, _NT, preferred_element_type=F32)


def _dot_tn(a, b):
    return lax.dot_general(a, b, _TN, preferred_element_type=F32)


def _sigmoid(x):
    return 1.0 / (1.0 + jnp.exp(-x))


_GELU_C = math.sqrt(2.0 / math.pi)


def _gelu_parts(y):
    t = jnp.tanh(_GELU_C * (y + 0.044715 * (y * y * y)))
    return 0.5 * y * (1.0 + t), t


def _gelu_grad(y, t):
    return 0.5 * (1.0 + t) + 0.5 * y * (1.0 - t * t) * (_GELU_C * (1.0 + 0.134145 * (y * y)))


def _rms_fwd(x, gain):
    r = lax.rsqrt(jnp.mean(x * x, axis=-1, keepdims=True) + RMS_EPS)
    return (x * r) * gain


def _rms_bwd(x, gain, dout):
    r = lax.rsqrt(jnp.mean(x * x, axis=-1, keepdims=True) + RMS_EPS)
    xh = x * r
    dxh = dout * gain
    dx = r * (dxh - xh * jnp.mean(dxh * xh, axis=-1, keepdims=True))
    return dx, jnp.sum(dout * xh, axis=0, keepdims=True)


def _mesh_pos():
    return lax.axis_index("x"), lax.axis_index("y"), lax.axis_index("c")


def _peer(pos, d):
    x, y, c = pos
    return (1 - x if d & 4 else x, 1 - y if d & 2 else y, 1 - c if d & 1 else c)


def _slot(pos):
    return 4 * pos[0] + 2 * pos[1] + pos[2]


def _exchange_copy(gather, src_ref, land_ref, sems, k, d, me, send_side):
    peer = _peer(me, d)
    sender = me if send_side else peer
    src = src_ref if gather else src_ref.at[_slot(peer) if send_side else _slot(me)]
    return pltpu.make_async_remote_copy(
        src_ref=src, dst_ref=land_ref.at[_slot(sender)],
        send_sem=sems[0].at[k * (N_DEV - 1) + d - 1], recv_sem=sems[1].at[k * (N_DEV - 1) + d - 1],
        device_id=peer, device_id_type=pl.DeviceIdType.MESH)


def _exchange_by_sequencer(srcs, gather, collective_id, name, after=()):
    n = len(srcs)
    flags = [gather] * n if isinstance(gather, bool) else list(gather)
    land_types = [SDS(((N_DEV,) + s.shape) if g else s.shape, s.dtype) for s, g in zip(srcs, flags)]

    def body(*refs):
        src_refs = refs[:n]
        land_refs = refs[n + len(after):2 * n + len(after)]
        sems = refs[2 * n + len(after):2 * n + len(after) + 2]
        local_sems = refs[2 * n + len(after) + 2]
        me = _mesh_pos()
        barrier = pltpu.get_barrier_semaphore()
        for d in range(1, N_DEV):
            pl.semaphore_signal(barrier, inc=1, device_id=_peer(me, d), device_id_type=pl.DeviceIdType.MESH)
        pl.semaphore_wait(barrier, N_DEV - 1)
        own = [pltpu.make_async_copy(src_refs[k] if flags[k] else src_refs[k].at[_slot(me)],
                                     land_refs[k].at[_slot(me)], local_sems.at[k]) for k in range(n)]
        for cp in own:
            cp.start()
        for k in range(n):
            for d in range(1, N_DEV):
                _exchange_copy(flags[k], src_refs[k], land_refs[k], sems, k, d, me, True).start()
        for cp in own:
            cp.wait()
        for k in range(n):
            for d in range(1, N_DEV):
                _exchange_copy(flags[k], src_refs[k], land_refs[k], sems, k, d, me, True).wait_send()
        for k in range(n):
            for d in range(1, N_DEV):
                _exchange_copy(flags[k], src_refs[k], land_refs[k], sems, k, d, me, False).wait_recv()

    sem_type = pltpu.SemaphoreType.DMA((n * (N_DEV - 1),))
    return pl.kernel(
        body, out_type=land_types, mesh=plsc.ScalarSubcoreMesh(axis_name="sequencer", num_cores=1), name=name,
        scratch_types=(sem_type, sem_type, pltpu.SemaphoreType.DMA((n,))),
        compiler_params=pltpu.CompilerParams(collective_id=collective_id),
    )(*srcs, *after)


def _load_resident(w_hbm, w_scr, sems, first_step):
    @pl.when(first_step)
    def _():
        copies = [pltpu.make_async_copy(w_hbm.at[s], w_scr.at[s], sems.at[s]) for s in range(N_DEV)]
        for cp in copies:
            cp.start()
        for cp in copies:
            cp.wait()


def _rope_lanes(t, cos, sin_a, sin_b):
    return t * cos + pltpu.roll(t, 96, 1) * sin_a + pltpu.roll(t, 32, 1) * sin_b


def _in_proj(hres, gain3, w_in_g, cos, sin_a, sin_b, layer):
    rows = hres.shape[0]
    tm = _row_tile(rows)

    def body(x_ref, g_ref, w_hbm, c_ref, a_ref, b_ref, u_ref, gate_ref, q_ref, k_ref, v_ref, h_ref,
             h_scr, w_scr, w_sem):
        j = pl.program_id(1)
        _load_resident(w_hbm, w_scr, w_sem, (pl.program_id(0) == 0) & (j == 0))

        @pl.when(j == 0)
        def _():
            hn = _rms_fwd(x_ref[...], g_ref[...]).astype(MXU_DTYPE)
            h_scr[...] = hn
            h_ref[...] = hn
            u_ref[...] = _dot(hn, w_scr[0])

        @pl.when((j == 1) | (j == 2))
        def _():
            res = _dot(h_scr[...], w_scr[j])
            c, a, b = c_ref[...], a_ref[...], b_ref[...]
            for t in range(4):
                lanes = slice(t * 128, (t + 1) * 128)
                q_ref[:, lanes] = (_rope_lanes(res[:, lanes], c, a, b) * ATTN_SCALE).astype(MXU_DTYPE)

        @pl.when(j == 3)
        def _():
            res = _dot(h_scr[...], w_scr[3])
            c, a, b = c_ref[...], a_ref[...], b_ref[...]
            for t in range(2):
                lanes = slice(t * 128, (t + 1) * 128)
                k_ref[:, lanes] = _rope_lanes(res[:, lanes], c, a, b).astype(MXU_DTYPE)
            v_ref[...] = res[:, D_KV:].astype(MXU_DTYPE)

        @pl.when(j >= 4)
        def _():
            gate_ref[...] = _dot(h_scr[...], w_scr[j])

    tab = pl.BlockSpec((tm, 128), lambda i, j: (i, 0))
    kv = pl.BlockSpec((tm, D_KV), lambda i, j: (i, 0))
    return _pcall(
        body, name=f"in_proj_l{layer}", grid=(rows // tm, N_DEV),
        in_specs=[pl.BlockSpec((tm, D), lambda i, j: (i, 0)),
                  pl.BlockSpec((None, 1, D), lambda i, j: (layer, 0, 0)),
                  pl.BlockSpec(memory_space=pl.ANY), tab, tab, tab],
        out_specs=[pl.BlockSpec((tm, COL_SHARD), lambda i, j: (i, 0)),
                   pl.BlockSpec((tm, COL_SHARD), lambda i, j: (i, jnp.clip(j - 4, 0, 3))),
                   pl.BlockSpec((tm, COL_SHARD), lambda i, j: (i, jnp.clip(j - 1, 0, 1))),
                   kv, kv, pl.BlockSpec((tm, D), lambda i, j: (i, 0))],
        out_shape=[SDS((rows, D_SSM), F32), SDS((rows, 2 * D), F32), SDS((rows, D_ATTN), MXU_DTYPE),
                   SDS((rows, D_KV), MXU_DTYPE), SDS((rows, D_KV), MXU_DTYPE), SDS((rows, D), MXU_DTYPE)],
        scratch_shapes=[pltpu.VMEM((tm, D), MXU_DTYPE), pltpu.VMEM((N_DEV, D, COL_SHARD), MXU_DTYPE),
                        pltpu.SemaphoreType.DMA((N_DEV,))],
        compiler_params=_cparams("arbitrary", "arbitrary"),
    )(hres, gain3, w_in_g, cos, sin_a, sin_b)


SCAN_TILE = 8


def _scan_tiles(x_ref, out_ref, tre_ref, tim_ref, sb, t_r, t_i, reverse, prev_ref=None):
    base = 4 if reverse else 0
    n_tiles = BLK // SCAN_TILE
    row = lax.broadcasted_iota(jnp.int32, (SCAN_TILE, SB_STATES), 0)
    for j in (range(n_tiles - 1, -1, -1) if reverse else range(n_tiles)):
        rows = slice(SCAN_TILE * j, SCAN_TILE * (j + 1))
        xr = x_ref[rows, :SB_STATES]
        xi = x_ref[rows, SB_STATES:]
        for k in range(3):
            shift = SCAN_TILE - (1 << k) if reverse else (1 << k)
            rr = pltpu.roll(xr, shift, 0)
            ri = pltpu.roll(xi, shift, 0)
            ar = tre_ref[sb, base + k]
            ai = tim_ref[sb, base + k]
            xr, xi = xr + (ar * rr - ai * ri), xi + (ar * ri + ai * rr)
        pr = tre_ref[sb, base + 3]
        pi = tim_ref[sb, base + 3]
        xr, xi = xr + (pr * t_r - pi * t_i), xi + (pr * t_i + pi * t_r)
        out_ref[rows, :SB_STATES] = xr
        out_ref[rows, SB_STATES:] = xi
        if prev_ref is not None:
            prev_ref[rows, :SB_STATES] = jnp.where(row == 0, t_r, pltpu.roll(xr, 1, 0))
            prev_ref[rows, SB_STATES:] = jnp.where(row == 0, t_i, pltpu.roll(xi, 1, 0))
        edge = slice(0, 1) if reverse else slice(SCAN_TILE - 1, SCAN_TILE)
        t_r, t_i = xr[edge], xi[edge]
    return t_r, t_i


def _s5_fwd(u, ssm, w_glu, b_glu3, layer):
    rows = u.shape[0]
    n_chunks = rows // BLK
    b_mat, c_mat, t_re, t_im, d_skip = (ssm[k] for k in ("b_mat", "c_mat", "t_re", "t_im", "d_skip"))

    def body(u_ref, bm_ref, cm_ref, tre_ref, tim_ref, d_ref, wg_ref, bg_ref,
             y_ref, ys_ref, cin_ref, carry, bu_scr, s_scr):
        @pl.when(pl.program_id(0) == 0)
        def _():
            carry[...] = jnp.zeros_like(carry)

        cin_ref[...] = carry[...]
        u = u_ref[...]
        for sb in range(N_SB):
            cols = slice(sb * 128, (sb + 1) * 128)
            u_sb = u[:, cols]
            bu_scr[sb] = _dot(u_sb.astype(MXU_DTYPE), bm_ref[sb])
            t_r, t_i = _scan_tiles(bu_scr.at[sb], s_scr.at[sb], tre_ref, tim_ref, sb,
                                   carry[2 * sb:2 * sb + 1, :], carry[2 * sb + 1:2 * sb + 2, :], False)
            carry[2 * sb:2 * sb + 1, :] = t_r
            carry[2 * sb + 1:2 * sb + 2, :] = t_i
            y_ref[:, cols] = _dot(s_scr[sb].astype(MXU_DTYPE), cm_ref[sb]) + d_ref[:, cols] * u_sb
        z, _ = _gelu_parts(y_ref[...])
        gl = _dot(z.astype(MXU_DTYPE), wg_ref[...]) + bg_ref[...]
        ys_ref[...] = (z * _sigmoid(gl)).astype(MXU_DTYPE)

    full = lambda shape: pl.BlockSpec(shape, lambda j: (0,) * len(shape))
    return _pcall(
        body, name=f"s5_fwd_l{layer}", grid=(n_chunks,),
        in_specs=[pl.BlockSpec((BLK, D_SSM), lambda j: (j, 0)),
                  full((N_SB, 128, 2 * SB_STATES)), full((N_SB, 2 * SB_STATES, 128)),
                  full((N_SB, 8, SCAN_TILE, SB_STATES)), full((N_SB, 8, SCAN_TILE, SB_STATES)),
                  full((1, D_SSM)), full((D_SSM, D_SSM)),
                  pl.BlockSpec((None, 1, D_SSM), lambda j: (layer, 0, 0))],
        out_specs=[pl.BlockSpec((BLK, D_SSM), lambda j: (j, 0)), pl.BlockSpec((BLK, D_SSM), lambda j: (j, 0)),
                   pl.BlockSpec((None, 8, SB_STATES), lambda j: (j, 0, 0))],
        out_shape=[SDS((rows, D_SSM), F32), SDS((rows, D_SSM), MXU_DTYPE), SDS((n_chunks, 8, SB_STATES), F32)],
        scratch_shapes=[pltpu.VMEM((8, SB_STATES), F32), pltpu.VMEM((N_SB, BLK, 2 * SB_STATES), F32),
                        pltpu.VMEM((N_SB, BLK, 2 * SB_STATES), F32)],
        compiler_params=_cparams("arbitrary"),
    )(u, b_mat, c_mat, t_re, t_im, d_skip, w_glu, b_glu3)


def _attn_mask(i):
    row = lax.broadcasted_iota(jnp.int32, (BLK, 3 * BLK), 0) + i * BLK
    col = lax.broadcasted_iota(jnp.int32, (BLK, 3 * BLK), 1)
    seg = jnp.right_shift(col, 7)
    c = jnp.bitwise_and(col, BLK - 1)
    kidx = c + (i + seg - 2) * BLK
    ok_meta = (seg == 0) & (c >= PAD_ROWS) & (row - c >= BLK)
    ok_win = (seg > 0) & (kidx >= PAD_ROWS) & (kidx <= row) & (row - kidx < BLK)
    return jnp.where(ok_meta | ok_win, 0.0, NEG_INF)


def _head_lanes(h):
    return slice(h * HEAD_DIM, (h + 1) * HEAD_DIM)


def _group_rows(ref, kvh):
    return jnp.concatenate([ref[:, _head_lanes(kvh * Q_PER_KV + g)] for g in range(Q_PER_KV)], axis=0)


def _group_bias(bias, sink_ref, layer, kvh):
    first_col = lax.broadcasted_iota(jnp.int32, (BLK, BLK), 1) == 0
    slabs = []
    for g in range(Q_PER_KV):
        first = jnp.where(first_col, sink_ref[layer, kvh * Q_PER_KV + g], bias[:, :BLK])
        slabs.append(jnp.concatenate([first, bias[:, BLK:]], axis=1))
    return jnp.concatenate(slabs, axis=0)


def _attn_probs(q4, k3, bias4):
    s = _dot_nt(q4, k3) + bias4
    e = jnp.exp(s - jnp.max(s, axis=-1, keepdims=True))
    return e * (1.0 / jnp.sum(e, axis=-1, keepdims=True))


def _attn_fwd(q, k, v, sinks, layer):
    rows = q.shape[0]
    n_blk = rows // BLK

    def body(sink_ref, q_ref, km_ref, kp_ref, kc_ref, vm_ref, vp_ref, vc_ref, o_ref):
        bias = _attn_mask(pl.program_id(0))
        for kvh in range(N_KV_HEADS):
            lanes = _head_lanes(kvh)
            k3 = jnp.concatenate([km_ref[:, lanes], kp_ref[:, lanes], kc_ref[:, lanes]], axis=0)
            v3 = jnp.concatenate([vm_ref[:, lanes], vp_ref[:, lanes], vc_ref[:, lanes]], axis=0)
            p = _attn_probs(_group_rows(q_ref, kvh), k3, _group_bias(bias, sink_ref, layer, kvh))
            o4 = _dot(p.astype(MXU_DTYPE), v3).astype(MXU_DTYPE)
            for g in range(Q_PER_KV):
                o_ref[:, _head_lanes(kvh * Q_PER_KV + g)] = o4[g * BLK:(g + 1) * BLK]

    kv_meta = pl.BlockSpec((BLK, D_KV), lambda i: (0, 0))
    kv_prev = pl.BlockSpec((BLK, D_KV), lambda i: (jnp.maximum(i - 1, 0), 0))
    kv_cur = pl.BlockSpec((BLK, D_KV), lambda i: (i, 0))
    return _pcall(
        body, name=f"attn_fwd_l{layer}", grid=(n_blk,),
        in_specs=[pl.BlockSpec(memory_space=pltpu.SMEM),
                  pl.BlockSpec((BLK, D_ATTN), lambda i: (i, 0)),
                  kv_meta, kv_prev, kv_cur, kv_meta, kv_prev, kv_cur],
        out_specs=pl.BlockSpec((BLK, D_ATTN), lambda i: (i, 0)),
        out_shape=SDS((rows, D_ATTN), MXU_DTYPE),
        compiler_params=_cparams("parallel"),
    )(sinks, q, k, k, k, v, v, v)


def _merge_fwd(y_ssm, y_attn, gates, hres, w_o_ssm, w_o_attn, w_out, gain3, layer):
    rows = hres.shape[0]
    tm = _row_tile(rows, 320)

    def body(ys_ref, ya_ref, gs_ref, ga_ref, x_ref, wos_ref, woa_ref, wout_ref, g_ref,
             mg_ref, mix_ref, out_ref):
        a1 = _dot(ys_ref[...], wos_ref[...])
        a2 = _dot(ya_ref[...], woa_ref[...])
        merged = (_sigmoid(gs_ref[...]) * a1 + _sigmoid(ga_ref[...]) * a2).astype(MXU_DTYPE)
        mg_ref[...] = merged
        mix = _dot(merged, wout_ref[...])
        mix_ref[...] = mix
        out_ref[...] = x_ref[...] + _rms_fwd(mix, g_ref[...])

    row_d = pl.BlockSpec((tm, D), lambda i: (i, 0))
    full = lambda shape: pl.BlockSpec(shape, lambda i: (0,) * len(shape))
    return _pcall(
        body, name=f"merge_fwd_l{layer}", grid=(rows // tm,),
        in_specs=[pl.BlockSpec((tm, D_SSM), lambda i: (i, 0)), row_d,
                  row_d, pl.BlockSpec((tm, D), lambda i: (i, 1)), row_d,
                  full((D_SSM, D)), full((D_ATTN, D)), full((D, D)),
                  pl.BlockSpec((None, 1, D), lambda i: (layer, 0, 0))],
        out_specs=[row_d, row_d, row_d],
        out_shape=[SDS((rows, D), MXU_DTYPE), SDS((rows, D), F32), SDS((rows, D), F32)],
        compiler_params=_cparams("parallel"),
    )(y_ssm, y_attn, gates, gates, hres, w_o_ssm, w_o_attn, w_out, gain3)


def _mlp_fwd(hres, gain_pre3, gain_post3, w_up_g, w_down_g, layer):
    rows = hres.shape[0]
    tm = _row_tile(rows)

    def body(x_ref, gp_ref, gq_ref, wu_hbm, wd_hbm, up_ref, h_ref, ff_ref, out_ref,
             h_scr, acc, wu_scr, wd_scr, wu_sem, wd_sem):
        kf = pl.program_id(1)
        first = (pl.program_id(0) == 0) & (kf == 0)
        _load_resident(wu_hbm, wu_scr, wu_sem, first)
        _load_resident(wd_hbm, wd_scr, wd_sem, first)

        @pl.when(kf == 0)
        def _():
            hn = _rms_fwd(x_ref[...], gp_ref[...]).astype(MXU_DTYPE)
            h_scr[...] = hn
            h_ref[...] = hn
            acc[...] = jnp.zeros_like(acc)

        up = _dot(h_scr[...], wu_scr[kf])
        up_ref[...] = up.astype(MXU_DTYPE)
        r = jnp.maximum(up, 0.0)
        acc[...] += _dot((r * r).astype(MXU_DTYPE), wd_scr[kf])

        @pl.when(kf == N_DEV - 1)
        def _():
            ff = acc[...]
            ff_ref[...] = ff
            out_ref[...] = x_ref[...] + _rms_fwd(ff, gq_ref[...])

    row_d = pl.BlockSpec((tm, D), lambda i, k: (i, 0))
    gain = pl.BlockSpec((None, 1, D), lambda i, k: (layer, 0, 0))
    return _pcall(
        body, name=f"mlp_fwd_l{layer}", grid=(rows // tm, N_DEV),
        in_specs=[row_d, gain, gain, pl.BlockSpec(memory_space=pl.ANY), pl.BlockSpec(memory_space=pl.ANY)],
        out_specs=[pl.BlockSpec((tm, COL_SHARD), lambda i, k: (i, k)), row_d, row_d, row_d],
        out_shape=[SDS((rows, D_FF), MXU_DTYPE), SDS((rows, D), MXU_DTYPE), SDS((rows, D), F32), SDS((rows, D), F32)],
        scratch_shapes=[pltpu.VMEM((tm, D), MXU_DTYPE), pltpu.VMEM((tm, D), F32),
                        pltpu.VMEM((N_DEV, D, COL_SHARD), MXU_DTYPE), pltpu.VMEM((N_DEV, COL_SHARD, D), MXU_DTYPE),
                        pltpu.SemaphoreType.DMA((N_DEV,)), pltpu.SemaphoreType.DMA((N_DEV,))],
        compiler_params=_cparams("arbitrary", "arbitrary"),
    )(hres, gain_pre3, gain_post3, w_up_g, w_down_g)


def _loss_and_grad(hres, target):
    rows = hres.shape[0]
    n_blk = rows // BLK

    def body(y_ref, t_ref, dy_ref, loss_ref):
        i = pl.program_id(0)

        @pl.when(i == 0)
        def _():
            dy_ref[...] = jnp.zeros_like(dy_ref)
            loss_ref[...] = jnp.zeros_like(loss_ref)

        @pl.when(i > 0)
        def _():
            err = y_ref[...] - t_ref[...]
            dy_ref[...] = err * (1.0 / D)
            loss_ref[...] += jnp.sum(err * err) * (0.5 / D)

    return _pcall(
        body, name="loss", grid=(n_blk,),
        in_specs=[pl.BlockSpec((BLK, D), lambda i: (i, 0)),
                  pl.BlockSpec((BLK, D), lambda i: (jnp.maximum(i - 1, 0), 0))],
        out_specs=[pl.BlockSpec((BLK, D), lambda i: (i, 0)), pl.BlockSpec((1, 128), lambda i: (0, 0))],
        out_shape=[SDS((rows, D), F32), SDS((1, 128), F32)],
        compiler_params=_cparams("arbitrary"),
    )(hres, target)


def _relu_squared(up):
    r = jnp.maximum(up.astype(F32), 0.0)
    return (r * r).astype(MXU_DTYPE)


def _matmul_tn(a, b, name, dev_major_cols=None, a_fn=None):
    rows, ka = a.shape
    n = b.shape[1]
    ta = min(ka, 1024)
    tn = 1024 if n % 1024 == 0 else 512
    tr = _row_tile(rows)
    n_r = rows // tr

    def body(a_ref, b_ref, o_ref, acc):
        r = pl.program_id(2)

        @pl.when(r == 0)
        def _():
            acc[...] = jnp.zeros_like(acc)

        a_blk = a_ref[...] if a_fn is None else a_fn(a_ref[...])
        acc[...] += _dot_tn(a_blk, b_ref[...])

        @pl.when(r == n_r - 1)
        def _():
            if dev_major_cols is None:
                o_ref[...] = acc[...].astype(XFER_DTYPE)
            else:
                for s in range(tn // dev_major_cols):
                    o_ref[s] = acc[:, s * dev_major_cols:(s + 1) * dev_major_cols].astype(XFER_DTYPE)

    if dev_major_cols is None:
        out_spec = pl.BlockSpec((ta, tn), lambda i, j, r: (i, j))
        out_shape = SDS((ka, n), XFER_DTYPE)
    else:
        w = dev_major_cols
        out_spec = pl.BlockSpec((tn // w, ta, w), lambda i, j, r: (j, i, 0))
        out_shape = SDS((n // w, ka, w), XFER_DTYPE)
    return _pcall(
        body, name=name, grid=(ka // ta, n // tn, n_r),
        in_specs=[pl.BlockSpec((tr, ta), lambda i, j, r: (r, i)), pl.BlockSpec((tr, tn), lambda i, j, r: (r, j))],
        out_specs=out_spec, out_shape=out_shape,
        scratch_shapes=[pltpu.VMEM((ta, tn), F32)],
        compiler_params=_cparams("parallel", "parallel", "arbitrary"),
    )(a, b)


def _mlp_bwd(dout, ff, up, hres_mid, gain_pre3, gain_post3, w_up_g, w_down_g, layer):
    rows = dout.shape[0]
    tm = _row_tile(rows)

    def body(do_ref, ff_ref, up_ref, x_ref, gp_ref, gq_ref, wu_hbm, wd_hbm,
             dff_ref, dup_ref, dx_ref, dgq_ref, dgp_ref, dff_scr, acc, wu_scr, wd_scr, wu_sem, wd_sem):
        i = pl.program_id(0)
        kf = pl.program_id(1)
        _load_resident(wu_hbm, wu_scr, wu_sem, (i == 0) & (kf == 0))
        _load_resident(wd_hbm, wd_scr, wd_sem, (i == 0) & (kf == 0))

        @pl.when((i == 0) & (kf == 0))
        def _():
            dgq_ref[...] = jnp.zeros_like(dgq_ref)
            dgp_ref[...] = jnp.zeros_like(dgp_ref)

        @pl.when(kf == 0)
        def _():
            dff, dg = _rms_bwd(ff_ref[...], gq_ref[...], do_ref[...])
            dgq_ref[...] += dg
            dffb = dff.astype(MXU_DTYPE)
            dff_scr[...] = dffb
            dff_ref[...] = dffb
            acc[...] = jnp.zeros_like(acc)

        dact = _dot_nt(dff_scr[...], wd_scr[kf])
        dup = (dact * (2.0 * jnp.maximum(up_ref[...].astype(F32), 0.0))).astype(MXU_DTYPE)
        dup_ref[...] = dup
        acc[...] += _dot_nt(dup, wu_scr[kf])

        @pl.when(kf == N_DEV - 1)
        def _():
            dx, dg = _rms_bwd(x_ref[...], gp_ref[...], acc[...])
            dgp_ref[...] += dg
            dx_ref[...] = do_ref[...] + dx

    row_d = pl.BlockSpec((tm, D), lambda i, k: (i, 0))
    gain = pl.BlockSpec((None, 1, D), lambda i, k: (layer, 0, 0))
    dgain = pl.BlockSpec((1, D), lambda i, k: (0, 0))
    return _pcall(
        body, name=f"mlp_bwd_l{layer}", grid=(rows // tm, N_DEV),
        in_specs=[row_d, row_d, pl.BlockSpec((tm, COL_SHARD), lambda i, k: (i, k)), row_d, gain, gain,
                  pl.BlockSpec(memory_space=pl.ANY), pl.BlockSpec(memory_space=pl.ANY)],
        out_specs=[row_d, pl.BlockSpec((tm, COL_SHARD), lambda i, k: (i, k)), row_d, dgain, dgain],
        out_shape=[SDS((rows, D), MXU_DTYPE), SDS((rows, D_FF), MXU_DTYPE), SDS((rows, D), F32),
                   SDS((1, D), F32), SDS((1, D), F32)],
        scratch_shapes=[pltpu.VMEM((tm, D), MXU_DTYPE), pltpu.VMEM((tm, D), F32),
                        pltpu.VMEM((N_DEV, D, COL_SHARD), MXU_DTYPE), pltpu.VMEM((N_DEV, COL_SHARD, D), MXU_DTYPE),
                        pltpu.SemaphoreType.DMA((N_DEV,)), pltpu.SemaphoreType.DMA((N_DEV,))],
        compiler_params=_cparams("arbitrary", "arbitrary"),
    )(dout, ff, up, hres_mid, gain_pre3, gain_post3, w_up_g, w_down_g)


def _merge_bwd(dhm, mix, y_ssm, y_attn, gates, w_o_ssm, w_o_attn, w_out, gain3, layer):
    rows = dhm.shape[0]
    tm = _row_tile(rows, 320)

    def body(dh_ref, mix_ref, ys_ref, ya_ref, gs_ref, ga_ref, wos_ref, woa_ref, wout_ref, g_ref,
             dmix_ref, da1_ref, da2_ref, dgs_ref, dga_ref, dys_ref, dya_ref, dg_ref):
        @pl.when(pl.program_id(0) == 0)
        def _():
            dg_ref[...] = jnp.zeros_like(dg_ref)

        dmix, dg = _rms_bwd(mix_ref[...], g_ref[...], dh_ref[...])
        dg_ref[...] += dg
        dmixb = dmix.astype(MXU_DTYPE)
        dmix_ref[...] = dmixb
        dmerged = _dot_nt(dmixb, wout_ref[...])
        sg_s = _sigmoid(gs_ref[...])
        sg_a = _sigmoid(ga_ref[...])
        da1 = (dmerged * sg_s).astype(MXU_DTYPE)
        da2 = (dmerged * sg_a).astype(MXU_DTYPE)
        da1_ref[...] = da1
        da2_ref[...] = da2
        a1 = _dot(ys_ref[...], wos_ref[...])
        a2 = _dot(ya_ref[...], woa_ref[...])
        dgs_ref[...] = (dmerged * a1 * (sg_s * (1.0 - sg_s))).astype(MXU_DTYPE)
        dga_ref[...] = (dmerged * a2 * (sg_a * (1.0 - sg_a))).astype(MXU_DTYPE)
        dys_ref[...] = _dot_nt(da1, wos_ref[...])
        dya_ref[...] = _dot_nt(da2, woa_ref[...])

    row_d = pl.BlockSpec((tm, D), lambda i: (i, 0))
    full = lambda shape: pl.BlockSpec(shape, lambda i: (0,) * len(shape))
    return _pcall(
        body, name=f"merge_bwd_l{layer}", grid=(rows // tm,),
        in_specs=[row_d, row_d, pl.BlockSpec((tm, D_SSM), lambda i: (i, 0)), row_d,
                  row_d, pl.BlockSpec((tm, D), lambda i: (i, 1)),
                  full((D_SSM, D)), full((D_ATTN, D)), full((D, D)),
                  pl.BlockSpec((None, 1, D), lambda i: (layer, 0, 0))],
        out_specs=[row_d, row_d, row_d, row_d, row_d, pl.BlockSpec((tm, D_SSM), lambda i: (i, 0)), row_d,
                   pl.BlockSpec((1, D), lambda i: (0, 0))],
        out_shape=[SDS((rows, D), MXU_DTYPE)] * 5 + [SDS((rows, D_SSM), F32), SDS((rows, D_ATTN), F32),
                                                      SDS((1, D), F32)],
        compiler_params=_cparams("arbitrary"),
    )(dhm, mix, y_ssm, y_attn, gates, gates, w_o_ssm, w_o_attn, w_out, gain3)


def _attn_bwd(q, k, v, d_out, sinks, layer):
    rows = q.shape[0]
    n_blk = rows // BLK
    last = n_blk - 1

    def body(sink_ref, q_ref, km_ref, kp_ref, kc_ref, vm_ref, vp_ref, vc_ref, do_ref,
             dq_ref, dk_ref, dv_ref, dkm_ref, dvm_ref, ds_ref, dk_carry, dv_carry):
        i = pl.program_id(0)

        @pl.when(i == 0)
        def _():
            dkm_ref[...] = jnp.zeros_like(dkm_ref)
            dvm_ref[...] = jnp.zeros_like(dvm_ref)
            ds_ref[...] = jnp.zeros_like(ds_ref)
            dk_carry[...] = jnp.zeros_like(dk_carry)
            dv_carry[...] = jnp.zeros_like(dv_carry)

        @pl.when(i <= last)
        def _():
            bias = _attn_mask(i)
            for kvh in range(N_KV_HEADS):
                lanes = _head_lanes(kvh)
                k3 = jnp.concatenate([km_ref[:, lanes], kp_ref[:, lanes], kc_ref[:, lanes]], axis=0)
                v3 = jnp.concatenate([vm_ref[:, lanes], vp_ref[:, lanes], vc_ref[:, lanes]], axis=0)
                q4 = _group_rows(q_ref, kvh)
                do4 = _group_rows(do_ref, kvh).astype(MXU_DTYPE)
                p = _attn_probs(q4, k3, _group_bias(bias, sink_ref, layer, kvh))
                dp = _dot_nt(do4, v3)
                dsf = p * (dp - jnp.sum(dp * p, axis=-1, keepdims=True))
                dsc = dsf.astype(MXU_DTYPE)
                dv3 = _dot_tn(p.astype(MXU_DTYPE), do4)
                dk3 = _dot_tn(dsc, q4)
                dq4 = _dot(dsc, k3)
                for g in range(Q_PER_KV):
                    h = kvh * Q_PER_KV + g
                    dq_ref[:, _head_lanes(h)] = dq4[g * BLK:(g + 1) * BLK]
                    ds_ref[h:h + 1, :] += jnp.sum(dsf[g * BLK:(g + 1) * BLK, 0:BLK], axis=0, keepdims=True)
                dkm_ref[:, lanes] += dk3[0:BLK]
                dvm_ref[:, lanes] += dv3[0:BLK]
                dk_ref[:, lanes] = dk_carry[:, lanes] + dk3[BLK:2 * BLK]
                dv_ref[:, lanes] = dv_carry[:, lanes] + dv3[BLK:2 * BLK]
                dk_carry[:, lanes] = dk3[2 * BLK:3 * BLK]
                dv_carry[:, lanes] = dv3[2 * BLK:3 * BLK]

        @pl.when(i == last + 1)
        def _():
            dk_ref[...] = dk_carry[...]
            dv_ref[...] = dv_carry[...]

    cur = lambda i: (jnp.minimum(i, last), 0)
    prev = lambda i: (jnp.clip(i - 1, 0, last), 0)
    kv_meta = pl.BlockSpec((BLK, D_KV), lambda i: (0, 0))
    kv_prev = pl.BlockSpec((BLK, D_KV), prev)
    kv_cur = pl.BlockSpec((BLK, D_KV), cur)
    return _pcall(
        body, name=f"attn_bwd_l{layer}", grid=(n_blk + 1,),
        in_specs=[pl.BlockSpec(memory_space=pltpu.SMEM),
                  pl.BlockSpec((BLK, D_ATTN), cur),
                  kv_meta, kv_prev, kv_cur, kv_meta, kv_prev, kv_cur,
                  pl.BlockSpec((BLK, D_ATTN), cur)],
        out_specs=[pl.BlockSpec((BLK, D_ATTN), cur), kv_prev, kv_prev, kv_meta, kv_meta,
                   pl.BlockSpec((N_Q_HEADS, 128), lambda i: (0, 0))],
        out_shape=[SDS((rows, D_ATTN), F32), SDS((rows, D_KV), F32), SDS((rows, D_KV), F32),
                   SDS((BLK, D_KV), F32), SDS((BLK, D_KV), F32), SDS((N_Q_HEADS, 128), F32)],
        scratch_shapes=[pltpu.VMEM((BLK, D_KV), F32), pltpu.VMEM((BLK, D_KV), F32)],
        compiler_params=_cparams("arbitrary"),
    )(sinks, q, k, k, k, v, v, v, d_out)


def _rope_bwd(dq, dk, dv, dk_meta, dv_meta, cos, sin_a, sin_b, layer):
    rows = dq.shape[0]
    tm = _row_tile(rows)

    def body(dq_ref, dk_ref, dv_ref, dkm_ref, dvm_ref, c_ref, a_ref, b_ref, o_ref):
        c, a, b = c_ref[...], -a_ref[...], -b_ref[...]
        for t in range(8):
            x = dq_ref[:, t * 128:(t + 1) * 128]
            o_ref[:, t * 128:(t + 1) * 128] = (_rope_lanes(x, c, a, b) * ATTN_SCALE).astype(MXU_DTYPE)
        for t in range(2):
            x = dk_ref[:, t * 128:(t + 1) * 128]
            o_ref[:, D_ATTN + t * 128:D_ATTN + (t + 1) * 128] = _rope_lanes(x, c, a, b).astype(MXU_DTYPE)
        o_ref[:, D_ATTN + D_KV:] = dv_ref[...].astype(MXU_DTYPE)

        @pl.when(pl.program_id(0) == 0)
        def _():
            cb, ab, bb = c[0:BLK], a[0:BLK], b[0:BLK]
            is_meta = lax.broadcasted_iota(jnp.int32, (BLK, 128), 0) >= PAD_ROWS
            for t in range(2):
                x = dk_ref[0:BLK, t * 128:(t + 1) * 128] + jnp.where(is_meta, dkm_ref[:, t * 128:(t + 1) * 128], 0.0)
                o_ref[0:BLK, D_ATTN + t * 128:D_ATTN + (t + 1) * 128] = _rope_lanes(x, cb, ab, bb).astype(MXU_DTYPE)
                xv = dv_ref[0:BLK, t * 128:(t + 1) * 128] + jnp.where(is_meta, dvm_ref[:, t * 128:(t + 1) * 128], 0.0)
                o_ref[0:BLK, D_ATTN + D_KV + t * 128:D_ATTN + D_KV + (t + 1) * 128] = xv.astype(MXU_DTYPE)

    tab = pl.BlockSpec((tm, 128), lambda i: (i, 0))
    kv = pl.BlockSpec((tm, D_KV), lambda i: (i, 0))
    meta = pl.BlockSpec((BLK, D_KV), lambda i: (0, 0))
    return _pcall(
        body, name=f"rope_bwd_l{layer}", grid=(rows // tm,),
        in_specs=[pl.BlockSpec((tm, D_ATTN), lambda i: (i, 0)), kv, kv, meta, meta, tab, tab, tab],
        out_specs=pl.BlockSpec((tm, D_ATTN + 2 * D_KV), lambda i: (i, 0)),
        out_shape=SDS((rows, D_ATTN + 2 * D_KV), MXU_DTYPE),
        compiler_params=_cparams("parallel"),
    )(dq, dk, dv, dk_meta, dv_meta, cos, sin_a, sin_b)


def _s5_bwd(d_gated, y, u, carry_in, ssm, w_glu, b_glu3, layer):
    rows = y.shape[0]
    n_chunks = rows // BLK
    b_mat, c_mat, t_re, t_im, d_skip = (ssm[k] for k in ("b_mat", "c_mat", "t_re", "t_im", "d_skip"))

    def body(dz_ref, y_ref, u_ref, cin_ref, bm_ref, cm_ref, tre_ref, tim_ref, d_ref, wg_ref, bg_ref,
             du_ref, dwg_ref, dbg_ref, dd_ref, dbm_ref, dcm_ref, dab_ref,
             lam_carry, bu_scr, s_scr, sp_scr, g_scr, lam_scr):
        step = pl.program_id(0)
        chunk = n_chunks - 1 - step

        @pl.when(step == 0)
        def _():
            for r in (dwg_ref, dbg_ref, dd_ref, dbm_ref, dcm_ref, dab_ref, lam_carry):
                r[...] = jnp.zeros_like(r)

        y = y_ref[...]
        u = u_ref[...]
        d_o = dz_ref[...]
        z, t = _gelu_parts(y)
        zb = z.astype(MXU_DTYPE)
        sg = _sigmoid(_dot(zb, wg_ref[...]) + bg_ref[...])
        dgl = d_o * z * (sg * (1.0 - sg))
        dglb = dgl.astype(MXU_DTYPE)
        dz = d_o * sg + _dot_nt(dglb, wg_ref[...])
        dwg_ref[...] += _dot_tn(zb, dglb)
        dbg_ref[...] += jnp.sum(dgl, axis=0, keepdims=True)
        dy = dz * _gelu_grad(y, t)
        dd_ref[...] += jnp.sum(dy * u, axis=0, keepdims=True)
        grow = lax.broadcasted_iota(jnp.int32, (BLK, 128), 0) + chunk * BLK
        for sb in range(N_SB):
            cols = slice(sb * 128, (sb + 1) * 128)
            u_sb = u[:, cols].astype(MXU_DTYPE)
            dy_sb = dy[:, cols]
            dyb = dy_sb.astype(MXU_DTYPE)
            bu_scr[sb] = _dot(u_sb, bm_ref[sb])
            _scan_tiles(bu_scr.at[sb], s_scr.at[sb], tre_ref, tim_ref, sb,
                        cin_ref[2 * sb:2 * sb + 1, :], cin_ref[2 * sb + 1:2 * sb + 2, :], False, prev_ref=sp_scr.at[sb])
            dcm_ref[sb] += _dot_tn(s_scr[sb].astype(MXU_DTYPE), dyb)
            g_scr[sb] = _dot_nt(dyb, cm_ref[sb])
            n_r, n_i = _scan_tiles(g_scr.at[sb], lam_scr.at[sb], tre_ref, tim_ref, sb,
                                   lam_carry[2 * sb:2 * sb + 1, :], lam_carry[2 * sb + 1:2 * sb + 2, :], True)
            lam_carry[2 * sb:2 * sb + 1, :] = n_r
            lam_carry[2 * sb + 1:2 * sb + 2, :] = n_i
            lr, li = lam_scr[sb, :, :SB_STATES], lam_scr[sb, :, SB_STATES:]
            spr, spi = sp_scr[sb, :, :SB_STATES], sp_scr[sb, :, SB_STATES:]
            dab_ref[2 * sb:2 * sb + 1, :] += jnp.sum(spr * lr + spi * li, axis=0, keepdims=True)
            dab_ref[2 * sb + 1:2 * sb + 2, :] += jnp.sum(spr * li - spi * lr, axis=0, keepdims=True)
            lam = lam_scr[sb].astype(MXU_DTYPE)
            dbm_ref[sb] += _dot_tn(u_sb, lam)
            du = _dot_nt(lam, bm_ref[sb]) + d_ref[:, cols] * dy_sb
            du_ref[:, cols] = jnp.where(grow >= PAD_ROWS, du, 0.0).astype(MXU_DTYPE)

    rev = lambda j: (n_chunks - 1 - j, 0)
    full = lambda shape: pl.BlockSpec(shape, lambda j: (0,) * len(shape))
    tables = [full((N_SB, 8, SCAN_TILE, SB_STATES))] * 2
    chunk_scratch = pltpu.VMEM((N_SB, BLK, 2 * SB_STATES), F32)
    return _pcall(
        body, name=f"s5_bwd_l{layer}", grid=(n_chunks,),
        in_specs=[pl.BlockSpec((BLK, D_SSM), rev), pl.BlockSpec((BLK, D_SSM), rev), pl.BlockSpec((BLK, D_SSM), rev),
                  pl.BlockSpec((None, 8, SB_STATES), lambda j: (n_chunks - 1 - j, 0, 0)),
                  full((N_SB, 128, 2 * SB_STATES)), full((N_SB, 2 * SB_STATES, 128))] + tables + [
                  full((1, D_SSM)), full((D_SSM, D_SSM)),
                  pl.BlockSpec((None, 1, D_SSM), lambda j: (layer, 0, 0))],
        out_specs=[pl.BlockSpec((BLK, D_SSM), rev), full((D_SSM, D_SSM)), full((1, D_SSM)), full((1, D_SSM)),
                   full((N_SB, 128, 2 * SB_STATES)), full((N_SB, 2 * SB_STATES, 128)), full((8, SB_STATES))],
        out_shape=[SDS((rows, D_SSM), MXU_DTYPE), SDS((D_SSM, D_SSM), F32), SDS((1, D_SSM), F32), SDS((1, D_SSM), F32),
                   SDS((N_SB, 128, 2 * SB_STATES), F32), SDS((N_SB, 2 * SB_STATES, 128), F32), SDS((8, SB_STATES), F32)],
        scratch_shapes=[pltpu.VMEM((8, SB_STATES), F32)] + [chunk_scratch] * 5,
        compiler_params=_cparams("arbitrary"),
    )(d_gated, y, u, carry_in, b_mat, c_mat, t_re, t_im, d_skip, w_glu, b_glu3)


DPROJ_PIECES = ((0, 1), (1, 3), (4, 2), (6, 2))


def _in_bwd(dproj_pieces, dhm, hres, gain3, w_in_g, layer):
    rows = hres.shape[0]
    tm = _row_tile(rows)

    def body(du_ref, dqkv_ref, dgs_ref, dga_ref, dh_ref, x_ref, g_ref, w_hbm, dx_ref, dg_ref, acc, w_scr, w_sem):
        i = pl.program_id(0)
        j = pl.program_id(1)
        _load_resident(w_hbm, w_scr, w_sem, (i == 0) & (j == 0))

        @pl.when((i == 0) & (j == 0))
        def _():
            dg_ref[...] = jnp.zeros_like(dg_ref)

        @pl.when(j == 0)
        def _():
            acc[...] = jnp.zeros_like(acc)

        for piece_ref, (first, count) in zip((du_ref, dqkv_ref, dgs_ref, dga_ref), DPROJ_PIECES):
            @pl.when((j >= first) & (j < first + count))
            def _():
                acc[...] += _dot_nt(piece_ref[...], w_scr[j])

        @pl.when(j == N_DEV - 1)
        def _():
            dx, dg = _rms_bwd(x_ref[...], g_ref[...], acc[...])
            dg_ref[...] += dg
            dx_ref[...] = dh_ref[...] + dx

    row_d = pl.BlockSpec((tm, D), lambda i, j: (i, 0))

    def piece_spec(first, count):
        return pl.BlockSpec((tm, COL_SHARD), lambda i, j: (i, jnp.clip(j - first, 0, count - 1)))

    return _pcall(
        body, name=f"in_bwd_l{layer}", grid=(rows // tm, N_DEV),
        in_specs=[piece_spec(*p) for p in DPROJ_PIECES] + [
                  row_d, row_d,
                  pl.BlockSpec((None, 1, D), lambda i, j: (layer, 0, 0)),
                  pl.BlockSpec(memory_space=pl.ANY)],
        out_specs=[row_d, pl.BlockSpec((1, D), lambda i, j: (0, 0))],
        out_shape=[SDS((rows, D), F32), SDS((1, D), F32)],
        scratch_shapes=[pltpu.VMEM((tm, D), F32), pltpu.VMEM((N_DEV, D, COL_SHARD), MXU_DTYPE),
                        pltpu.SemaphoreType.DMA((N_DEV,))],
        compiler_params=_cparams("arbitrary", "arbitrary"),
    )(*dproj_pieces, dhm, hres, gain3, w_in_g)


_ADAM_C1 = 1.0 / (1.0 - ADAM_B1 ** ADAM_STEP)
_ADAM_C2 = 1.0 / (1.0 - ADAM_B2 ** ADAM_STEP)


def _adam_math(w, g, m, v):
    m = ADAM_B1 * m + (1.0 - ADAM_B1) * g
    v = ADAM_B2 * v + (1.0 - ADAM_B2) * (g * g)
    delta = -ADAM_LR * ((m * _ADAM_C1) / (jnp.sqrt(v * _ADAM_C2) + ADAM_EPS) + ADAM_WD * w)
    return delta, m, v


def _adamw_layers(parts0, parts1, w, m, v, name):
    _, rows, cols = w.shape
    tr = min(rows, (1 << 16) // cols)
    nt = rows // tr

    def body(p0_ref, p1_ref, w_ref, m_ref, v_ref, g_ref, d_ref, nm_ref, nv_ref):
        layer = pl.program_id(0)

        def run(p_ref):
            g = p_ref[0].astype(F32)
            for s in range(1, N_DEV):
                g = g + p_ref[s].astype(F32)
            delta, nm, nv = _adam_math(w_ref[...], g, m_ref[...], v_ref[...])
            g_ref[...] = g
            d_ref[...] = delta
            nm_ref[...] = nm
            nv_ref[...] = nv

        @pl.when(layer == 0)
        def _():
            run(p0_ref)

        @pl.when(layer == 1)
        def _():
            run(p1_ref)

    wspec = pl.BlockSpec((None, tr, cols), lambda l, i: (l, i, 0))
    return _pcall(
        body, name=name, grid=(2, nt),
        in_specs=[pl.BlockSpec((N_DEV, tr, cols), lambda l, i: (0, jnp.where(l == 0, i, nt - 1), 0)),
                  pl.BlockSpec((N_DEV, tr, cols), lambda l, i: (0, jnp.where(l == 1, i, 0), 0)),
                  wspec, wspec, wspec],
        out_specs=[wspec] * 4, out_shape=[SDS(w.shape, F32)] * 4,
        compiler_params=_cparams("arbitrary", "arbitrary"),
    )(parts0, parts1, w, m, v)


def _sum_slots(parts, name):
    def body(p_ref, o_ref):
        acc = p_ref[0]
        for s in range(1, N_DEV):
            acc = acc + p_ref[s]
        o_ref[...] = acc

    vmem = pl.BlockSpec(memory_space=pltpu.VMEM)
    return _pcall(body, name=name, out_shape=SDS(parts.shape[1:], F32), in_specs=[vmem], out_specs=vmem,
                  compiler_params=_cparams())(parts)


def _adamw_packed(g, w, m, v, name):
    def body(g_ref, w_ref, m_ref, v_ref, d_ref, nm_ref, nv_ref):
        delta, nm, nv = _adam_math(w_ref[...], g_ref[...], m_ref[...], v_ref[...])
        d_ref[...] = delta
        nm_ref[...] = nm
        nv_ref[...] = nv

    vmem = pl.BlockSpec(memory_space=pltpu.VMEM)
    return _pcall(body, name=name, out_shape=[SDS(g.shape, F32)] * 3, in_specs=[vmem] * 4, out_specs=[vmem] * 3,
                  compiler_params=_cparams())(g, w, m, v)


def _ssm_discretize(a_re, a_im, log_dt, b_re, b_im):
    dt = jnp.exp(log_dt)[:, None]
    mag = jnp.exp(a_re * dt)
    ang = a_im * dt
    ab_re, ab_im = mag * jnp.cos(ang), mag * jnp.sin(ang)
    xr, xi = ab_re - 1.0, ab_im
    den = a_re * a_re + a_im * a_im
    q_re = (xr * a_re + xi * a_im) / den
    q_im = (xi * a_re - xr * a_im) / den
    bb_re = q_re[..., None] * b_re - q_im[..., None] * b_im
    bb_im = q_re[..., None] * b_im + q_im[..., None] * b_re
    return ab_re, ab_im, bb_re, bb_im


def _block_diag_b(bb):
    m = jnp.einsum("sgnc,gh->sgchn", bb.reshape(N_SB, 8, N_STATE, GROUP_CH), jnp.eye(8, dtype=F32))
    return m.reshape(N_SB, 128, SB_STATES)


def _block_diag_b_t(dm):
    return jnp.einsum("sgchn,gh->sgnc", dm.reshape(N_SB, 8, GROUP_CH, 8, N_STATE),
                      jnp.eye(8, dtype=F32)).reshape(N_GROUPS, N_STATE, GROUP_CH)


def _block_diag_c(cc):
    m = jnp.einsum("sgcn,gh->sgnhc", cc.reshape(N_SB, 8, GROUP_CH, N_STATE), jnp.eye(8, dtype=F32))
    return m.reshape(N_SB, SB_STATES, 128)


def _block_diag_c_t(dm):
    return jnp.einsum("sgnhc,gh->sgcn", dm.reshape(N_SB, 8, N_STATE, 8, GROUP_CH),
                      jnp.eye(8, dtype=F32)).reshape(N_GROUPS, GROUP_CH, N_STATE)


def _ssm_tables(ab_re, ab_im, bb_re, bb_im, c_re, c_im, d_skip):
    pr, pi = ab_re.reshape(1, -1), ab_im.reshape(1, -1)
    cr, ci = pr, pi
    squares = []
    for _ in range(3):
        squares.append((cr, ci))
        pr, pi = (jnp.concatenate([pr, pr * cr - pi * ci], axis=0),
                  jnp.concatenate([pi, pr * ci + pi * cr], axis=0))
        cr, ci = cr * cr - ci * ci, 2.0 * cr * ci
    r = jnp.arange(SCAN_TILE)[:, None]
    fwd = [(jnp.where(r >= (1 << k), squares[k][0], 0.0), jnp.where(r >= (1 << k), squares[k][1], 0.0))
           for k in range(3)] + [(pr, pi)]
    rev = [(jnp.where(r < SCAN_TILE - (1 << k), squares[k][0], 0.0),
            jnp.where(r < SCAN_TILE - (1 << k), -squares[k][1], 0.0)) for k in range(3)] + [(pr[::-1], -pi[::-1])]
    table = lambda part: jnp.stack([e[part] for e in fwd + rev]).reshape(
        8, SCAN_TILE, N_SB, SB_STATES).transpose(2, 0, 1, 3)
    return dict(
        b_mat=jnp.concatenate([_block_diag_b(bb_re), _block_diag_b(bb_im)], axis=-1).astype(MXU_DTYPE),
        c_mat=jnp.concatenate([_block_diag_c(c_re), -_block_diag_c(c_im)], axis=1).astype(MXU_DTYPE),
        t_re=table(0), t_im=table(1),
        d_skip=d_skip.reshape(1, D_SSM))


def _rope_tables(rows):
    pos = (jnp.arange(rows, dtype=jnp.int32) - PAD_ROWS).astype(F32)
    inv_freq = 1.0 / (ROPE_THETA ** (jnp.arange(0, HEAD_DIM, 2, dtype=F32) / HEAD_DIM))
    ang = pos[:, None] * inv_freq[None, :]
    ang = jnp.concatenate([ang, ang, ang, ang], axis=-1)
    first_half = (jnp.arange(128) % HEAD_DIM) < HEAD_DIM // 2
    sin = jnp.sin(ang)
    return jnp.cos(ang), jnp.where(first_half, -sin, 0.0), jnp.where(first_half, 0.0, sin)


def _pack(arrays):
    flat = jnp.concatenate([a.reshape(-1).astype(F32) for a in arrays])
    pad = (-flat.shape[0]) % 1024
    return jnp.pad(flat, (0, pad)).reshape(-1, 128)


def _unpack(packed, like):
    flat = packed.reshape(-1)
    out, off = [], 0
    for a in like:
        n = math.prod(a.shape)
        out.append(flat[off:off + n].reshape(a.shape))
        off += n
    return out


BIG = ("w_in", "w_glu", "w_o_ssm", "w_o_attn", "w_out", "w_up", "w_down")
WEIGHTS = ("meta_tokens", "norm_mix_pre", "norm_mix_post", "norm_mlp_pre", "norm_mlp_post", "w_in",
           "ssm_a_re", "ssm_a_im", "ssm_log_dt", "ssm_b_re", "ssm_b_im", "ssm_c_re", "ssm_c_im", "ssm_d",
           "w_glu", "b_glu", "attn_sinks", "w_o_ssm", "w_o_attn", "w_out", "w_up", "w_down")
SMALL = tuple(n for n in WEIGHTS if n not in BIG)


def kernel(x, meta_tokens, norm_mix_pre, norm_mix_post, norm_mlp_pre, norm_mlp_post, w_in, ssm_a_re, ssm_a_im, ssm_log_dt, ssm_b_re, ssm_b_im, ssm_c_re, ssm_c_im, ssm_d, w_glu, b_glu, attn_sinks, w_o_ssm, w_o_attn, w_out, w_up, w_down, loss_target, m_meta_tokens, m_norm_mix_pre, m_norm_mix_post, m_norm_mlp_pre, m_norm_mlp_post, m_w_in, m_ssm_a_re, m_ssm_a_im, m_ssm_log_dt, m_ssm_b_re, m_ssm_b_im, m_ssm_c_re, m_ssm_c_im, m_ssm_d, m_w_glu, m_b_glu, m_attn_sinks, m_w_o_ssm, m_w_o_attn, m_w_out, m_w_up, m_w_down, v_meta_tokens, v_norm_mix_pre, v_norm_mix_post, v_norm_mlp_pre, v_norm_mlp_post, v_w_in, v_ssm_a_re, v_ssm_a_im, v_ssm_log_dt, v_ssm_b_re, v_ssm_b_im, v_ssm_c_re, v_ssm_c_im, v_ssm_d, v_w_glu, v_b_glu, v_attn_sinks, v_w_o_ssm, v_w_o_attn, v_w_out, v_w_up, v_w_down):
    args = locals()
    w = {n: args[n] for n in WEIGHTS}
    m = {n: args["m_" + n] for n in WEIGHTS}
    v = {n: args["v_" + n] for n in WEIGHTS}
    n_layers = w_in.shape[0]
    seq = x.shape[1]
    rows = seq + BLK
    my_slot = _slot(_mesh_pos())

    assert n_layers == 2
    xfer = {n: w[n].astype(XFER_DTYPE) for n in BIG}
    mixer_small = ("w_glu", "w_o_ssm", "w_o_attn", "w_out")
    meta_g, w_in_g0 = _exchange_by_sequencer([meta_tokens, xfer["w_in"][0]], True, 0, "gather_in0")
    mix0_g = _exchange_by_sequencer([xfer[n][0] for n in mixer_small], True, 1, "gather_mix0", after=[meta_g])
    mlp0_g = _exchange_by_sequencer([xfer["w_up"][0], xfer["w_down"][0]], True, 2, "gather_mlp0", after=[meta_g])
    l1_g = _exchange_by_sequencer([xfer[n][1] for n in ("w_in",) + mixer_small + ("w_up", "w_down")], True, 3,
                                  "gather_l1", after=mix0_g[:1])
    last_exchange = l1_g[:1]
    meta_full = meta_g.transpose(1, 0, 2).reshape(N_META, D)

    def mixer_weights(w_glu_g, w_o_ssm_g, w_o_attn_g, w_out_g):
        return dict(w_glu=w_glu_g.reshape(D_SSM, D_SSM), w_o_ssm=w_o_ssm_g.transpose(1, 0, 2).reshape(D_SSM, D),
                    w_o_attn=w_o_attn_g.reshape(D_ATTN, D), w_out=w_out_g.reshape(D, D))

    gathered = [dict(w_in=w_in_g0, w_up=mlp0_g[0], w_down=mlp0_g[1], **mixer_weights(*mix0_g)),
                dict(w_in=l1_g[0], w_up=l1_g[5], w_down=l1_g[6], **mixer_weights(*l1_g[1:5]))]

    gains = {n: w[n].reshape(n_layers, 1, D) for n in ("norm_mix_pre", "norm_mix_post", "norm_mlp_pre", "norm_mlp_post")}
    b_glu3 = b_glu.reshape(n_layers, 1, D_SSM)
    cos, sin_a, sin_b = _rope_tables(rows)

    def ssm_setup(l):
        disc, disc_vjp = jax.vjp(_ssm_discretize, ssm_a_re[l], ssm_a_im[l], ssm_log_dt[l], ssm_b_re[l], ssm_b_im[l])
        return _ssm_tables(*disc, ssm_c_re[l], ssm_c_im[l], ssm_d[l]), disc_vjp

    hres = jnp.concatenate([jnp.zeros((PAD_ROWS, D), F32), meta_full, x[0]], axis=0)

    saved = []
    for l in range(n_layers):
        ssm, disc_vjp = ssm_setup(l)
        wl = gathered[l]
        u, gates, q, k, vv, h = _in_proj(hres, gains["norm_mix_pre"], wl["w_in"], cos, sin_a, sin_b, l)
        y, y_ssm, carry_in = _s5_fwd(u, ssm, wl["w_glu"], b_glu3, l)
        y_attn = _attn_fwd(q, k, vv, attn_sinks, l)
        merged, mix, hres_mid = _merge_fwd(y_ssm, y_attn, gates, hres, wl["w_o_ssm"], wl["w_o_attn"], wl["w_out"],
                                           gains["norm_mix_post"], l)
        up, h2, ff, hres_out = _mlp_fwd(hres_mid, gains["norm_mlp_pre"], gains["norm_mlp_post"], wl["w_up"],
                                        wl["w_down"], l)
        saved.append(dict(ssm=ssm, disc_vjp=disc_vjp, hres=hres, u=u, gates=gates, h=h, q=q, k=k, v=vv, y=y, y_ssm=y_ssm,
                          carry_in=carry_in, y_attn=y_attn, merged=merged, mix=mix, hres_mid=hres_mid,
                          up=up, h2=h2, ff=ff))
        hres = hres_out

    dhres, loss_vec = _loss_and_grad(hres, loss_target[0])
    loss = lax.psum(loss_vec[0, 0], MESH_AXES)

    small_grads = {n: [None] * n_layers for n in SMALL if n != "meta_tokens"}
    recv_up, recv_down, recv_mix = [None] * n_layers, [None] * n_layers, [None] * n_layers
    for l in reversed(range(n_layers)):
        s = saved[l]
        wl = gathered[l]
        dff, dup, dhm, dg_mlp_post, dg_mlp_pre = _mlp_bwd(dhres, s["ff"], s["up"], s["hres_mid"], gains["norm_mlp_pre"],
                                                          gains["norm_mlp_post"], wl["w_up"], wl["w_down"], l)
        dw_up = _matmul_tn(s["h2"], dup, f"dw_up_l{l}", dev_major_cols=COL_SHARD)
        recv_up[l] = _exchange_by_sequencer([dw_up], False, 4 + 3 * l, f"scatter_up{l}", after=last_exchange)
        dw_down = _matmul_tn(s["up"], dff, f"dw_down_l{l}", a_fn=_relu_squared).reshape(N_DEV, COL_SHARD, D)
        recv_down[l] = _exchange_by_sequencer([dw_down], False, 5 + 3 * l, f"scatter_down{l}", after=recv_up[l])
        last_exchange = recv_down[l]
        dmix, da1, da2, dgs, dga, dy_ssm, dy_attn, dg_mix_post = _merge_bwd(
            dhm, s["mix"], s["y_ssm"], s["y_attn"], s["gates"], wl["w_o_ssm"], wl["w_o_attn"], wl["w_out"],
            gains["norm_mix_post"], l)
        dw_out = _matmul_tn(s["merged"], dmix, f"dw_out_l{l}").reshape(N_DEV, D // N_DEV, D)
        dw_o_attn = _matmul_tn(s["y_attn"], da2, f"dw_o_attn_l{l}").reshape(N_DEV, D_ATTN // N_DEV, D)
        dw_o_ssm = _matmul_tn(s["y_ssm"], da1, f"dw_o_ssm_l{l}", dev_major_cols=D // N_DEV)
        dq, dk, dv, dk_meta, dv_meta, dsink = _attn_bwd(s["q"], s["k"], s["v"], dy_attn, attn_sinks, l)
        dqkv = _rope_bwd(dq, dk, dv, dk_meta, dv_meta, cos, sin_a, sin_b, l)
        du, dw_glu, db_glu, dd_skip, db_mat, dc_mat, dab = _s5_bwd(dy_ssm, s["y"], s["u"], s["carry_in"], s["ssm"],
                                                                    wl["w_glu"], b_glu3, l)
        dproj = (du, dqkv, dgs, dga)
        dw_in = jnp.concatenate([_matmul_tn(s["h"], piece, f"dw_in{k}_l{l}", dev_major_cols=COL_SHARD)
                                 for k, piece in enumerate(dproj)], axis=0)
        dhres, dg_mix_pre = _in_bwd(dproj, dhm, s["hres"], gains["norm_mix_pre"], wl["w_in"], l)
        mix_parts = [dw_in, dw_glu.astype(XFER_DTYPE).reshape(N_DEV, D_SSM // N_DEV, D_SSM), dw_o_ssm, dw_o_attn, dw_out]
        if l > 0:
            recv_mix[l] = _exchange_by_sequencer(mix_parts, False, 6 + 3 * l, f"scatter_mix{l}", after=last_exchange)
            last_exchange = recv_mix[l][:1]

        dab = dab.reshape(N_SB, 2, SB_STATES)
        da_re, da_im, dlog_dt, db_re, db_im = s["disc_vjp"]((
            dab[:, 0].reshape(N_GROUPS, N_STATE), dab[:, 1].reshape(N_GROUPS, N_STATE),
            _block_diag_b_t(db_mat[..., :SB_STATES]), _block_diag_b_t(db_mat[..., SB_STATES:])))
        for name, val in (("norm_mix_pre", dg_mix_pre[0]), ("norm_mix_post", dg_mix_post[0]),
                          ("norm_mlp_pre", dg_mlp_pre[0]), ("norm_mlp_post", dg_mlp_post[0]),
                          ("ssm_a_re", da_re), ("ssm_a_im", da_im), ("ssm_log_dt", dlog_dt),
                          ("ssm_b_re", db_re), ("ssm_b_im", db_im),
                          ("ssm_c_re", _block_diag_c_t(dc_mat[:, :SB_STATES])),
                          ("ssm_c_im", -_block_diag_c_t(dc_mat[:, SB_STATES:])),
                          ("ssm_d", dd_skip.reshape(N_GROUPS, GROUP_CH)), ("b_glu", db_glu[0]),
                          ("attn_sinks", dsink[:, 0])):
            small_grads[name][l] = val

    grad_x = dhres[BLK:][None]
    small_names = [n for n in SMALL if n != "meta_tokens"]
    partial_small = [dhres[PAD_ROWS:BLK]] + [jnp.stack(small_grads[n]) for n in small_names]
    *recv_mix[0], small_parts = _exchange_by_sequencer(
        mix_parts + [_pack(partial_small)], [False] * len(mix_parts) + [True], 6, "scatter_mix0", after=last_exchange)

    grads, delta, new_m, new_v = {}, {}, {}, {}

    def adamw_big(names, recv0, recv1):
        for n, p0, p1 in zip(names, recv0, recv1):
            grads[n], delta[n], new_m[n], new_v[n] = _adamw_layers(p0, p1, w[n], m[n], v[n], f"adamw_{n}")

    adamw_big(("w_up", "w_down"), recv_up[0] + recv_down[0], recv_up[1] + recv_down[1])
    summed = _unpack(_sum_slots(small_parts, "sum_small_grads"), partial_small)
    grads.update(zip(small_names, summed[1:]))
    grads["meta_tokens"] = lax.dynamic_slice_in_dim(summed[0], my_slot * (D // N_DEV), D // N_DEV, axis=1)
    like = [w[n] for n in SMALL]
    d_s, m_s, v_s = _adamw_packed(_pack([grads[n] for n in SMALL]), _pack(like), _pack([m[n] for n in SMALL]),
                                  _pack([v[n] for n in SMALL]), "adamw_small")
    adamw_big(("w_in",) + mixer_small, recv_mix[0], recv_mix[1])
    for n, dd, mm, vs in zip(SMALL, _unpack(d_s, like), _unpack(m_s, like), _unpack(v_s, like)):
        delta[n], new_m[n], new_v[n] = dd, mm, vs

    return (loss, grad_x, *[grads[n] for n in WEIGHTS], *[delta[n] for n in WEIGHTS],
            *[new_m[n] for n in WEIGHTS], *[new_v[n] for n in WEIGHTS])
```

```python
import functools
import math

import jax
import jax.numpy as jnp
from jax import lax
from jax.experimental import pallas as pl
from jax.experimental.pallas import tpu as pltpu
from jax.experimental.pallas import tpu_sc as plsc

F32 = jnp.float32
MXU_DTYPE = jnp.bfloat16
XFER_DTYPE = MXU_DTYPE
_pcall = pl.pallas_call
SDS = jax.ShapeDtypeStruct

D = 1024
D_SSM = 512
D_ATTN = 1024
D_KV = 256
D_FF = 4096
D_IN = 4096
HEAD_DIM = 64
N_Q_HEADS = 16
N_KV_HEADS = 4
Q_PER_KV = 4
N_META = 16
BLK = 128
PAD_ROWS = BLK - N_META
N_GROUPS = 32
N_STATE = 64
GROUP_CH = 16
N_SB = 4
SB_STATES = 512
ROPE_THETA = 10000.0
ATTN_SCALE = HEAD_DIM ** -0.5
NEG_INF = -1e30
RMS_EPS = 1e-6
N_DEV = 8
COL_SHARD = 512

ADAM_LR = 0.001
ADAM_B1 = 0.9
ADAM_B2 = 0.999
ADAM_EPS = 1e-08
ADAM_WD = 0.01
ADAM_STEP = 10

VMEM_LIMIT = 56 * 1024 * 1024
MESH_AXES = ("x", "y", "c")

_NT = (((1,), (1,)), ((), ()))
_TN = (((0,), (0,)), ((), ()))


def _cparams(*sem):
    return pltpu.CompilerParams(dimension_semantics=tuple(sem) if sem else None,
                                vmem_limit_bytes=VMEM_LIMIT)


def _row_tile(rows, cap=640):
    for t in (640, 512, 320, 256, 128):
        if t <= cap and rows % t == 0:
            return t
    raise ValueError(f"unsupported row count {rows}")


def _dot(a, b):
    return jnp.dot(a, b, preferred_element_type=F32)


def _dot_nt(a, b):
    return lax.dot_general(a, b, _NT, preferred_element_type=F32)


def _dot_tn(a, b):
    return lax.dot_general(a, b, _TN, preferred_element_type=F32)


def _sigmoid(x):
    return 1.0 / (1.0 + jnp.exp(-x))


_GELU_C = math.sqrt(2.0 / math.pi)


def _gelu_parts(y):
    t = jnp.tanh(_GELU_C * (y + 0.044715 * (y * y * y)))
    return 0.5 * y * (1.0 + t), t


def _gelu_grad(y, t):
    return 0.5 * (1.0 + t) + 0.5 * y * (1.0 - t * t) * (_GELU_C * (1.0 + 0.134145 * (y * y)))


def _rms_fwd(x, gain):
    r = lax.rsqrt(jnp.mean(x * x, axis=-1, keepdims=True) + RMS_EPS)
    return (x * r) * gain


def _rms_bwd(x, gain, dout):
    r = lax.rsqrt(jnp.mean(x * x, axis=-1, keepdims=True) + RMS_EPS)
    xh = x * r
    dxh = dout * gain
    dx = r * (dxh - xh * jnp.mean(dxh * xh, axis=-1, keepdims=True))
    return dx, jnp.sum(dout * xh, axis=0, keepdims=True)


def _mesh_pos():
    return lax.axis_index("x"), lax.axis_index("y"), lax.axis_index("c")


def _peer(pos, d):
    x, y, c = pos
    return (1 - x if d & 4 else x, 1 - y if d & 2 else y, 1 - c if d & 1 else c)


def _slot(pos):
    return 4 * pos[0] + 2 * pos[1] + pos[2]


def _exchange_copy(gather, src_ref, land_ref, sems, k, d, me, send_side):
    peer = _peer(me, d)
    sender = me if send_side else peer
    src = src_ref if gather else src_ref.at[_slot(peer) if send_side else _slot(me)]
    return pltpu.make_async_remote_copy(
        src_ref=src, dst_ref=land_ref.at[_slot(sender)],
        send_sem=sems[0].at[k * (N_DEV - 1) + d - 1], recv_sem=sems[1].at[k * (N_DEV - 1) + d - 1],
        device_id=peer, device_id_type=pl.DeviceIdType.MESH)


def _exchange_by_sequencer(srcs, gather, collective_id, name, after=()):
    n = len(srcs)
    flags = [gather] * n if isinstance(gather, bool) else list(gather)
    land_types = [SDS(((N_DEV,) + s.shape) if g else s.shape, s.dtype) for s, g in zip(srcs, flags)]

    def body(*refs):
        src_refs = refs[:n]
        land_refs = refs[n + len(after):2 * n + len(after)]
        sems = refs[2 * n + len(after):2 * n + len(after) + 2]
        local_sems = refs[2 * n + len(after) + 2]
        me = _mesh_pos()
        barrier = pltpu.get_barrier_semaphore()
        for d in range(1, N_DEV):
            pl.semaphore_signal(barrier, inc=1, device_id=_peer(me, d), device_id_type=pl.DeviceIdType.MESH)
        pl.semaphore_wait(barrier, N_DEV - 1)
        own = [pltpu.make_async_copy(src_refs[k] if flags[k] else src_refs[k].at[_slot(me)],
                                     land_refs[k].at[_slot(me)], local_sems.at[k]) for k in range(n)]
        for cp in own:
            cp.start()
        for k in range(n):
            for d in range(1, N_DEV):
                _exchange_copy(flags[k], src_refs[k], land_refs[k], sems, k, d, me, True).start()
        for cp in own:
            cp.wait()
        for k in range(n):
            for d in range(1, N_DEV):
                _exchange_copy(flags[k], src_refs[k], land_refs[k], sems, k, d, me, True).wait_send()
        for k in range(n):
            for d in range(1, N_DEV):
                _exchange_copy(flags[k], src_refs[k], land_refs[k], sems, k, d, me, False).wait_recv()

    sem_type = pltpu.SemaphoreType.DMA((n * (N_DEV - 1),))
    return pl.kernel(
        body, out_type=land_types, mesh=plsc.ScalarSubcoreMesh(axis_name="sequencer", num_cores=1), name=name,
        scratch_types=(sem_type, sem_type, pltpu.SemaphoreType.DMA((n,))),
        compiler_params=pltpu.CompilerParams(collective_id=collective_id),
    )(*srcs, *after)


def _load_resident(w_hbm, w_scr, sems, first_step):
    @pl.when(first_step)
    def _():
        copies = [pltpu.make_async_copy(w_hbm.at[s], w_scr.at[s], sems.at[s]) for s in range(N_DEV)]
        for cp in copies:
            cp.start()
        for cp in copies:
            cp.wait()


def _rope_lanes(t, cos, sin_a, sin_b):
    return t * cos + pltpu.roll(t, 96, 1) * sin_a + pltpu.roll(t, 32, 1) * sin_b


def _in_proj(hres, gain3, w_in_g, cos, sin_a, sin_b, layer):
    rows = hres.shape[0]
    tm = _row_tile(rows)

    def body(x_ref, g_ref, w_hbm, c_ref, a_ref, b_ref, u_ref, gate_ref, q_ref, k_ref, v_ref, h_ref,
             h_scr, w_scr, w_sem):
        j = pl.program_id(1)
        _load_resident(w_hbm, w_scr, w_sem, (pl.program_id(0) == 0) & (j == 0))

        @pl.when(j == 0)
        def _():
            hn = _rms_fwd(x_ref[...], g_ref[...]).astype(MXU_DTYPE)
            h_scr[...] = hn
            h_ref[...] = hn
            u_ref[...] = _dot(hn, w_scr[0])

        @pl.when((j == 1) | (j == 2))
        def _():
            res = _dot(h_scr[...], w_scr[j])
            c, a, b = c_ref[...], a_ref[...], b_ref[...]
            for t in range(4):
                lanes = slice(t * 128, (t + 1) * 128)
                q_ref[:, lanes] = (_rope_lanes(res[:, lanes], c, a, b) * ATTN_SCALE).astype(MXU_DTYPE)

        @pl.when(j == 3)
        def _():
            res = _dot(h_scr[...], w_scr[3])
            c, a, b = c_ref[...], a_ref[...], b_ref[...]
            for t in range(2):
                lanes = slice(t * 128, (t + 1) * 128)
                k_ref[:, lanes] = _rope_lanes(res[:, lanes], c, a, b).astype(MXU_DTYPE)
            v_ref[...] = res[:, D_KV:].astype(MXU_DTYPE)

        @pl.when(j >= 4)
        def _():
            gate_ref[...] = _dot(h_scr[...], w_scr[j])

    tab = pl.BlockSpec((tm, 128), lambda i, j: (i, 0))
    kv = pl.BlockSpec((tm, D_KV), lambda i, j: (i, 0))
    return _pcall(
        body, name=f"in_proj_l{layer}", grid=(rows // tm, N_DEV),
        in_specs=[pl.BlockSpec((tm, D), lambda i, j: (i, 0)),
                  pl.BlockSpec((None, 1, D), lambda i, j: (layer, 0, 0)),
                  pl.BlockSpec(memory_space=pl.ANY), tab, tab, tab],
        out_specs=[pl.BlockSpec((tm, COL_SHARD), lambda i, j: (i, 0)),
                   pl.BlockSpec((tm, COL_SHARD), lambda i, j: (i, jnp.clip(j - 4, 0, 3))),
                   pl.BlockSpec((tm, COL_SHARD), lambda i, j: (i, jnp.clip(j - 1, 0, 1))),
                   kv, kv, pl.BlockSpec((tm, D), lambda i, j: (i, 0))],
        out_shape=[SDS((rows, D_SSM), F32), SDS((rows, 2 * D), F32), SDS((rows, D_ATTN), MXU_DTYPE),
                   SDS((rows, D_KV), MXU_DTYPE), SDS((rows, D_KV), MXU_DTYPE), SDS((rows, D), MXU_DTYPE)],
        scratch_shapes=[pltpu.VMEM((tm, D), MXU_DTYPE), pltpu.VMEM((N_DEV, D, COL_SHARD), MXU_DTYPE),
                        pltpu.SemaphoreType.DMA((N_DEV,))],
        compiler_params=_cparams("arbitrary", "arbitrary"),
    )(hres, gain3, w_in_g, cos, sin_a, sin_b)


SCAN_TILE = 8


def _scan_tiles(x_ref, out_ref, tre_ref, tim_ref, sb, t_r, t_i, reverse, prev_ref=None):
    base = 4 if reverse else 0
    n_tiles = BLK // SCAN_TILE
    row = lax.broadcasted_iota(jnp.int32, (SCAN_TILE, SB_STATES), 0)
    for j in (range(n_tiles - 1, -1, -1) if reverse else range(n_tiles)):
        rows = slice(SCAN_TILE * j, SCAN_TILE * (j + 1))
        xr = x_ref[rows, :SB_STATES]
        xi = x_ref[rows, SB_STATES:]
        for k in range(3):
            shift = SCAN_TILE - (1 << k) if reverse else (1 << k)
            rr = pltpu.roll(xr, shift, 0)
            ri = pltpu.roll(xi, shift, 0)
            ar = tre_ref[sb, base + k]
            ai = tim_ref[sb, base + k]
            xr, xi = xr + (ar * rr - ai * ri), xi + (ar * ri + ai * rr)
        pr = tre_ref[sb, base + 3]
        pi = tim_ref[sb, base + 3]
        xr, xi = xr + (pr * t_r - pi * t_i), xi + (pr * t_i + pi * t_r)
        out_ref[rows, :SB_STATES] = xr
        out_ref[rows, SB_STATES:] = xi
        if prev_ref is not None:
            prev_ref[rows, :SB_STATES] = jnp.where(row == 0, t_r, pltpu.roll(xr, 1, 0))
            prev_ref[rows, SB_STATES:] = jnp.where(row == 0, t_i, pltpu.roll(xi, 1, 0))
        edge = slice(0, 1) if reverse else slice(SCAN_TILE - 1, SCAN_TILE)
        t_r, t_i = xr[edge], xi[edge]
    return t_r, t_i


def _s5_fwd(u, ssm, w_glu, b_glu3, layer):
    rows = u.shape[0]
    n_chunks = rows // BLK
    b_mat, c_mat, t_re, t_im, d_skip = (ssm[k] for k in ("b_mat", "c_mat", "t_re", "t_im", "d_skip"))

    def body(u_ref, bm_ref, cm_ref, tre_ref, tim_ref, d_ref, wg_ref, bg_ref,
             y_ref, ys_ref, cin_ref, carry, bu_scr, s_scr):
        @pl.when(pl.program_id(0) == 0)
        def _():
            carry[...] = jnp.zeros_like(carry)

        cin_ref[...] = carry[...]
        u = u_ref[...]
        for sb in range(N_SB):
            cols = slice(sb * 128, (sb + 1) * 128)
            u_sb = u[:, cols]
            bu_scr[sb] = _dot(u_sb.astype(MXU_DTYPE), bm_ref[sb])
            t_r, t_i = _scan_tiles(bu_scr.at[sb], s_scr.at[sb], tre_ref, tim_ref, sb,
                                   carry[2 * sb:2 * sb + 1, :], carry[2 * sb + 1:2 * sb + 2, :], False)
            carry[2 * sb:2 * sb + 1, :] = t_r
            carry[2 * sb + 1:2 * sb + 2, :] = t_i
            y_ref[:, cols] = _dot(s_scr[sb].astype(MXU_DTYPE), cm_ref[sb]) + d_ref[:, cols] * u_sb
        z, _ = _gelu_parts(y_ref[...])
        gl = _dot(z.astype(MXU_DTYPE), wg_ref[...]) + bg_ref[...]
        ys_ref[...] = (z * _sigmoid(gl)).astype(MXU_DTYPE)

    full = lambda shape: pl.BlockSpec(shape, lambda j: (0,) * len(shape))
    return _pcall(
        body, name=f"s5_fwd_l{layer}", grid=(n_chunks,),
        in_specs=[pl.BlockSpec((BLK, D_SSM), lambda j: (j, 0)),
                  full((N_SB, 128, 2 * SB_STATES)), full((N_SB, 2 * SB_STATES, 128)),
                  full((N_SB, 8, SCAN_TILE, SB_STATES)), full((N_SB, 8, SCAN_TILE, SB_STATES)),
                  full((1, D_SSM)), full((D_SSM, D_SSM)),
                  pl.BlockSpec((None, 1, D_SSM), lambda j: (layer, 0, 0))],
        out_specs=[pl.BlockSpec((BLK, D_SSM), lambda j: (j, 0)), pl.BlockSpec((BLK, D_SSM), lambda j: (j, 0)),
                   pl.BlockSpec((None, 8, SB_STATES), lambda j: (j, 0, 0))],
        out_shape=[SDS((rows, D_SSM), F32), SDS((rows, D_SSM), MXU_DTYPE), SDS((n_chunks, 8, SB_STATES), F32)],
        scratch_shapes=[pltpu.VMEM((8, SB_STATES), F32), pltpu.VMEM((N_SB, BLK, 2 * SB_STATES), F32),
                        pltpu.VMEM((N_SB, BLK, 2 * SB_STATES), F32)],
        compiler_params=_cparams("arbitrary"),
    )(u, b_mat, c_mat, t_re, t_im, d_skip, w_glu, b_glu3)


def _attn_mask(i):
    row = lax.broadcasted_iota(jnp.int32, (BLK, 3 * BLK), 0) + i * BLK
    col = lax.broadcasted_iota(jnp.int32, (BLK, 3 * BLK), 1)
    seg = jnp.right_shift(col, 7)
    c = jnp.bitwise_and(col, BLK - 1)
    kidx = c + (i + seg - 2) * BLK
    ok_meta = (seg == 0) & (c >= PAD_ROWS) & (row - c >= BLK)
    ok_win = (seg > 0) & (kidx >= PAD_ROWS) & (kidx <= row) & (row - kidx < BLK)
    return jnp.where(ok_meta | ok_win, 0.0, NEG_INF)


def _head_lanes(h):
    return slice(h * HEAD_DIM, (h + 1) * HEAD_DIM)


def _group_rows(ref, kvh):
    return jnp.concatenate([ref[:, _head_lanes(kvh * Q_PER_KV + g)] for g in range(Q_PER_KV)], axis=0)


def _group_bias(bias, sink_ref, layer, kvh):
    first_col = lax.broadcasted_iota(jnp.int32, (BLK, BLK), 1) == 0
    slabs = []
    for g in range(Q_PER_KV):
        first = jnp.where(first_col, sink_ref[layer, kvh * Q_PER_KV + g], bias[:, :BLK])
        slabs.append(jnp.concatenate([first, bias[:, BLK:]], axis=1))
    return jnp.concatenate(slabs, axis=0)


def _attn_probs(q4, k3, bias4):
    s = _dot_nt(q4, k3) + bias4
    e = jnp.exp(s - jnp.max(s, axis=-1, keepdims=True))
    return e * (1.0 / jnp.sum(e, axis=-1, keepdims=True))


def _attn_fwd(q, k, v, sinks, layer):
    rows = q.shape[0]
    n_blk = rows // BLK

    def body(sink_ref, q_ref, km_ref, kp_ref, kc_ref, vm_ref, vp_ref, vc_ref, o_ref):
        bias = _attn_mask(pl.program_id(0))
        for kvh in range(N_KV_HEADS):
            lanes = _head_lanes(kvh)
            k3 = jnp.concatenate([km_ref[:, lanes], kp_ref[:, lanes], kc_ref[:, lanes]], axis=0)
            v3 = jnp.concatenate([vm_ref[:, lanes], vp_ref[:, lanes], vc_ref[:, lanes]], axis=0)
            p = _attn_probs(_group_rows(q_ref, kvh), k3, _group_bias(bias, sink_ref, layer, kvh))
            o4 = _dot(p.astype(MXU_DTYPE), v3).astype(MXU_DTYPE)
            for g in range(Q_PER_KV):
                o_ref[:, _head_lanes(kvh * Q_PER_KV + g)] = o4[g * BLK:(g + 1) * BLK]

    kv_meta = pl.BlockSpec((BLK, D_KV), lambda i: (0, 0))
    kv_prev = pl.BlockSpec((BLK, D_KV), lambda i: (jnp.maximum(i - 1, 0), 0))
    kv_cur = pl.BlockSpec((BLK, D_KV), lambda i: (i, 0))
    return _pcall(
        body, name=f"attn_fwd_l{layer}", grid=(n_blk,),
        in_specs=[pl.BlockSpec(memory_space=pltpu.SMEM),
                  pl.BlockSpec((BLK, D_ATTN), lambda i: (i, 0)),
                  kv_meta, kv_prev, kv_cur, kv_meta, kv_prev, kv_cur],
        out_specs=pl.BlockSpec((BLK, D_ATTN), lambda i: (i, 0)),
        out_shape=SDS((rows, D_ATTN), MXU_DTYPE),
        compiler_params=_cparams("parallel"),
    )(sinks, q, k, k, k, v, v, v)


def _merge_fwd(y_ssm, y_attn, gates, hres, w_o_ssm, w_o_attn, w_out, gain3, layer):
    rows = hres.shape[0]
    tm = _row_tile(rows, 320)

    def body(ys_ref, ya_ref, gs_ref, ga_ref, x_ref, wos_ref, woa_ref, wout_ref, g_ref,
             mg_ref, mix_ref, out_ref):
        a1 = _dot(ys_ref[...], wos_ref[...])
        a2 = _dot(ya_ref[...], woa_ref[...])
        merged = (_sigmoid(gs_ref[...]) * a1 + _sigmoid(ga_ref[...]) * a2).astype(MXU_DTYPE)
        mg_ref[...] = merged
        mix = _dot(merged, wout_ref[...])
        mix_ref[...] = mix
        out_ref[...] = x_ref[...] + _rms_fwd(mix, g_ref[...])

    row_d = pl.BlockSpec((tm, D), lambda i: (i, 0))
    full = lambda shape: pl.BlockSpec(shape, lambda i: (0,) * len(shape))
    return _pcall(
        body, name=f"merge_fwd_l{layer}", grid=(rows // tm,),
        in_specs=[pl.BlockSpec((tm, D_SSM), lambda i: (i, 0)), row_d,
                  row_d, pl.BlockSpec((tm, D), lambda i: (i, 1)), row_d,
                  full((D_SSM, D)), full((D_ATTN, D)), full((D, D)),
                  pl.BlockSpec((None, 1, D), lambda i: (layer, 0, 0))],
        out_specs=[row_d, row_d, row_d],
        out_shape=[SDS((rows, D), MXU_DTYPE), SDS((rows, D), F32), SDS((rows, D), F32)],
        compiler_params=_cparams("parallel"),
    )(y_ssm, y_attn, gates, gates, hres, w_o_ssm, w_o_attn, w_out, gain3)


def _mlp_fwd(hres, gain_pre3, gain_post3, w_up_g, w_down_g, layer):
    rows = hres.shape[0]
    tm = _row_tile(rows)

    def body(x_ref, gp_ref, gq_ref, wu_hbm, wd_hbm, up_ref, h_ref, ff_ref, out_ref,
             h_scr, acc, wu_scr, wd_scr, wu_sem, wd_sem):
        kf = pl.program_id(1)
        first = (pl.program_id(0) == 0) & (kf == 0)
        _load_resident(wu_hbm, wu_scr, wu_sem, first)
        _load_resident(wd_hbm, wd_scr, wd_sem, first)

        @pl.when(kf == 0)
        def _():
            hn = _rms_fwd(x_ref[...], gp_ref[...]).astype(MXU_DTYPE)
            h_scr[...] = hn
            h_ref[...] = hn
            acc[...] = jnp.zeros_like(acc)

        up = _dot(h_scr[...], wu_scr[kf])
        up_ref[...] = up.astype(MXU_DTYPE)
        r = jnp.maximum(up, 0.0)
        acc[...] += _dot((r * r).astype(MXU_DTYPE), wd_scr[kf])

        @pl.when(kf == N_DEV - 1)
        def _():
            ff = acc[...]
            ff_ref[...] = ff
            out_ref[...] = x_ref[...] + _rms_fwd(ff, gq_ref[...])

    row_d = pl.BlockSpec((tm, D), lambda i, k: (i, 0))
    gain = pl.BlockSpec((None, 1, D), lambda i, k: (layer, 0, 0))
    return _pcall(
        body, name=f"mlp_fwd_l{layer}", grid=(rows // tm, N_DEV),
        in_specs=[row_d, gain, gain, pl.BlockSpec(memory_space=pl.ANY), pl.BlockSpec(memory_space=pl.ANY)],
        out_specs=[pl.BlockSpec((tm, COL_SHARD), lambda i, k: (i, k)), row_d, row_d, row_d],
        out_shape=[SDS((rows, D_FF), MXU_DTYPE), SDS((rows, D), MXU_DTYPE), SDS((rows, D), F32), SDS((rows, D), F32)],
        scratch_shapes=[pltpu.VMEM((tm, D), MXU_DTYPE), pltpu.VMEM((tm, D), F32),
                        pltpu.VMEM((N_DEV, D, COL_SHARD), MXU_DTYPE), pltpu.VMEM((N_DEV, COL_SHARD, D), MXU_DTYPE),
                        pltpu.SemaphoreType.DMA((N_DEV,)), pltpu.SemaphoreType.DMA((N_DEV,))],
        compiler_params=_cparams("arbitrary", "arbitrary"),
    )(hres, gain_pre3, gain_post3, w_up_g, w_down_g)


def _loss_and_grad(hres, target):
    rows = hres.shape[0]
    n_blk = rows // BLK

    def body(y_ref, t_ref, dy_ref, loss_ref):
        i = pl.program_id(0)

        @pl.when(i == 0)
        def _():
            dy_ref[...] = jnp.zeros_like(dy_ref)
            loss_ref[...] = jnp.zeros_like(loss_ref)

        @pl.when(i > 0)
        def _():
            err = y_ref[...] - t_ref[...]
            dy_ref[...] = err * (1.0 / D)
            loss_ref[...] += jnp.sum(err * err) * (0.5 / D)

    return _pcall(
        body, name="loss", grid=(n_blk,),
        in_specs=[pl.BlockSpec((BLK, D), lambda i: (i, 0)),
                  pl.BlockSpec((BLK, D), lambda i: (jnp.maximum(i - 1, 0), 0))],
        out_specs=[pl.BlockSpec((BLK, D), lambda i: (i, 0)), pl.BlockSpec((1, 128), lambda i: (0, 0))],
        out_shape=[SDS((rows, D), F32), SDS((1, 128), F32)],
        compiler_params=_cparams("arbitrary"),
    )(hres, target)


def _relu_squared(up):
    r = jnp.maximum(up.astype(F32), 0.0)
    return (r * r).astype(MXU_DTYPE)


def _matmul_tn(a, b, name, dev_major_cols=None, a_fn=None):
    rows, ka = a.shape
    n = b.shape[1]
    ta = min(ka, 1024)
    tn = 1024 if n % 1024 == 0 else 512
    tr = _row_tile(rows)
    n_r = rows // tr

    def body(a_ref, b_ref, o_ref, acc):
        r = pl.program_id(2)

        @pl.when(r == 0)
        def _():
            acc[...] = jnp.zeros_like(acc)

        a_blk = a_ref[...] if a_fn is None else a_fn(a_ref[...])
        acc[...] += _dot_tn(a_blk, b_ref[...])

        @pl.when(r == n_r - 1)
        def _():
            if dev_major_cols is None:
                o_ref[...] = acc[...].astype(XFER_DTYPE)
            else:
                for s in range(tn // dev_major_cols):
                    o_ref[s] = acc[:, s * dev_major_cols:(s + 1) * dev_major_cols].astype(XFER_DTYPE)

    if dev_major_cols is None:
        out_spec = pl.BlockSpec((ta, tn), lambda i, j, r: (i, j))
        out_shape = SDS((ka, n), XFER_DTYPE)
    else:
        w = dev_major_cols
        out_spec = pl.BlockSpec((tn // w, ta, w), lambda i, j, r: (j, i, 0))
        out_shape = SDS((n // w, ka, w), XFER_DTYPE)
    return _pcall(
        body, name=name, grid=(ka // ta, n // tn, n_r),
        in_specs=[pl.BlockSpec((tr, ta), lambda i, j, r: (r, i)), pl.BlockSpec((tr, tn), lambda i, j, r: (r, j))],
        out_specs=out_spec, out_shape=out_shape,
        scratch_shapes=[pltpu.VMEM((ta, tn), F32)],
        compiler_params=_cparams("parallel", "parallel", "arbitrary"),
    )(a, b)


def _mlp_bwd(dout, ff, up, hres_mid, gain_pre3, gain_post3, w_up_g, w_down_g, layer):
    rows = dout.shape[0]
    tm = _row_tile(rows)

    def body(do_ref, ff_ref, up_ref, x_ref, gp_ref, gq_ref, wu_hbm, wd_hbm,
             dff_ref, dup_ref, dx_ref, dgq_ref, dgp_ref, dff_scr, acc, wu_scr, wd_scr, wu_sem, wd_sem):
        i = pl.program_id(0)
        kf = pl.program_id(1)
        _load_resident(wu_hbm, wu_scr, wu_sem, (i == 0) & (kf == 0))
        _load_resident(wd_hbm, wd_scr, wd_sem, (i == 0) & (kf == 0))

        @pl.when((i == 0) & (kf == 0))
        def _():
            dgq_ref[...] = jnp.zeros_like(dgq_ref)
            dgp_ref[...] = jnp.zeros_like(dgp_ref)

        @pl.when(kf == 0)
        def _():
            dff, dg = _rms_bwd(ff_ref[...], gq_ref[...], do_ref[...])
            dgq_ref[...] += dg
            dffb = dff.astype(MXU_DTYPE)
            dff_scr[...] = dffb
            dff_ref[...] = dffb
            acc[...] = jnp.zeros_like(acc)

        dact = _dot_nt(dff_scr[...], wd_scr[kf])
        dup = (dact * (2.0 * jnp.maximum(up_ref[...].astype(F32), 0.0))).astype(MXU_DTYPE)
        dup_ref[...] = dup
        acc[...] += _dot_nt(dup, wu_scr[kf])

        @pl.when(kf == N_DEV - 1)
        def _():
            dx, dg = _rms_bwd(x_ref[...], gp_ref[...], acc[...])
            dgp_ref[...] += dg
            dx_ref[...] = do_ref[...] + dx

    row_d = pl.BlockSpec((tm, D), lambda i, k: (i, 0))
    gain = pl.BlockSpec((None, 1, D), lambda i, k: (layer, 0, 0))
    dgain = pl.BlockSpec((1, D), lambda i, k: (0, 0))
    return _pcall(
        body, name=f"mlp_bwd_l{layer}", grid=(rows // tm, N_DEV),
        in_specs=[row_d, row_d, pl.BlockSpec((tm, COL_SHARD), lambda i, k: (i, k)), row_d, gain, gain,
                  pl.BlockSpec(memory_space=pl.ANY), pl.BlockSpec(memory_space=pl.ANY)],
        out_specs=[row_d, pl.BlockSpec((tm, COL_SHARD), lambda i, k: (i, k)), row_d, dgain, dgain],
        out_shape=[SDS((rows, D), MXU_DTYPE), SDS((rows, D_FF), MXU_DTYPE), SDS((rows, D), F32),
                   SDS((1, D), F32), SDS((1, D), F32)],
        scratch_shapes=[pltpu.VMEM((tm, D), MXU_DTYPE), pltpu.VMEM((tm, D), F32),
                        pltpu.VMEM((N_DEV, D, COL_SHARD), MXU_DTYPE), pltpu.VMEM((N_DEV, COL_SHARD, D), MXU_DTYPE),
                        pltpu.SemaphoreType.DMA((N_DEV,)), pltpu.SemaphoreType.DMA((N_DEV,))],
        compiler_params=_cparams("arbitrary", "arbitrary"),
    )(dout, ff, up, hres_mid, gain_pre3, gain_post3, w_up_g, w_down_g)


def _merge_bwd(dhm, mix, y_ssm, y_attn, gates, w_o_ssm, w_o_attn, w_out, gain3, layer):
    rows = dhm.shape[0]
    tm = _row_tile(rows, 320)

    def body(dh_ref, mix_ref, ys_ref, ya_ref, gs_ref, ga_ref, wos_ref, woa_ref, wout_ref, g_ref,
             dmix_ref, da1_ref, da2_ref, dgs_ref, dga_ref, dys_ref, dya_ref, dg_ref):
        @pl.when(pl.program_id(0) == 0)
        def _():
            dg_ref[...] = jnp.zeros_like(dg_ref)

        dmix, dg = _rms_bwd(mix_ref[...], g_ref[...], dh_ref[...])
        dg_ref[...] += dg
        dmixb = dmix.astype(MXU_DTYPE)
        dmix_ref[...] = dmixb
        dmerged = _dot_nt(dmixb, wout_ref[...])
        sg_s = _sigmoid(gs_ref[...])
        sg_a = _sigmoid(ga_ref[...])
        da1 = (dmerged * sg_s).astype(MXU_DTYPE)
        da2 = (dmerged * sg_a).astype(MXU_DTYPE)
        da1_ref[...] = da1
        da2_ref[...] = da2
        a1 = _dot(ys_ref[...], wos_ref[...])
        a2 = _dot(ya_ref[...], woa_ref[...])
        dgs_ref[...] = (dmerged * a1 * (sg_s * (1.0 - sg_s))).astype(MXU_DTYPE)
        dga_ref[...] = (dmerged * a2 * (sg_a * (1.0 - sg_a))).astype(MXU_DTYPE)
        dys_ref[...] = _dot_nt(da1, wos_ref[...])
        dya_ref[...] = _dot_nt(da2, woa_ref[...])

    row_d = pl.BlockSpec((tm, D), lambda i: (i, 0))
    full = lambda shape: pl.BlockSpec(shape, lambda i: (0,) * len(shape))
    return _pcall(
        body, name=f"merge_bwd_l{layer}", grid=(rows // tm,),
        in_specs=[row_d, row_d, pl.BlockSpec((tm, D_SSM), lambda i: (i, 0)), row_d,
                  row_d, pl.BlockSpec((tm, D), lambda i: (i, 1)),
                  full((D_SSM, D)), full((D_ATTN, D)), full((D, D)),
                  pl.BlockSpec((None, 1, D), lambda i: (layer, 0, 0))],
        out_specs=[row_d, row_d, row_d, row_d, row_d, pl.BlockSpec((tm, D_SSM), lambda i: (i, 0)), row_d,
                   pl.BlockSpec((1, D), lambda i: (0, 0))],
        out_shape=[SDS((rows, D), MXU_DTYPE)] * 5 + [SDS((rows, D_SSM), F32), SDS((rows, D_ATTN), F32),
                                                      SDS((1, D), F32)],
        compiler_params=_cparams("arbitrary"),
    )(dhm, mix, y_ssm, y_attn, gates, gates, w_o_ssm, w_o_attn, w_out, gain3)


def _attn_bwd(q, k, v, d_out, sinks, layer):
    rows = q.shape[0]
    n_blk = rows // BLK
    last = n_blk - 1

    def body(sink_ref, q_ref, km_ref, kp_ref, kc_ref, vm_ref, vp_ref, vc_ref, do_ref,
             dq_ref, dk_ref, dv_ref, dkm_ref, dvm_ref, ds_ref, dk_carry, dv_carry):
        i = pl.program_id(0)

        @pl.when(i == 0)
        def _():
            dkm_ref[...] = jnp.zeros_like(dkm_ref)
            dvm_ref[...] = jnp.zeros_like(dvm_ref)
            ds_ref[...] = jnp.zeros_like(ds_ref)
            dk_carry[...] = jnp.zeros_like(dk_carry)
            dv_carry[...] = jnp.zeros_like(dv_carry)

        @pl.when(i <= last)
        def _():
            bias = _attn_mask(i)
            for kvh in range(N_KV_HEADS):
                lanes = _head_lanes(kvh)
                k3 = jnp.concatenate([km_ref[:, lanes], kp_ref[:, lanes], kc_ref[:, lanes]], axis=0)
                v3 = jnp.concatenate([vm_ref[:, lanes], vp_ref[:, lanes], vc_ref[:, lanes]], axis=0)
                q4 = _group_rows(q_ref, kvh)
                do4 = _group_rows(do_ref, kvh).astype(MXU_DTYPE)
                p = _attn_probs(q4, k3, _group_bias(bias, sink_ref, layer, kvh))
                dp = _dot_nt(do4, v3)
                dsf = p * (dp - jnp.sum(dp * p, axis=-1, keepdims=True))
                dsc = dsf.astype(MXU_DTYPE)
                dv3 = _dot_tn(p.astype(MXU_DTYPE), do4)
                dk3 = _dot_tn(dsc, q4)
                dq4 = _dot(dsc, k3)
                for g in range(Q_PER_KV):
                    h = kvh * Q_PER_KV + g
                    dq_ref[:, _head_lanes(h)] = dq4[g * BLK:(g + 1) * BLK]
                    ds_ref[h:h + 1, :] += jnp.sum(dsf[g * BLK:(g + 1) * BLK, 0:BLK], axis=0, keepdims=True)
                dkm_ref[:, lanes] += dk3[0:BLK]
                dvm_ref[:, lanes] += dv3[0:BLK]
                dk_ref[:, lanes] = dk_carry[:, lanes] + dk3[BLK:2 * BLK]
                dv_ref[:, lanes] = dv_carry[:, lanes] + dv3[BLK:2 * BLK]
                dk_carry[:, lanes] = dk3[2 * BLK:3 * BLK]
                dv_carry[:, lanes] = dv3[2 * BLK:3 * BLK]

        @pl.when(i == last + 1)
        def _():
            dk_ref[...] = dk_carry[...]
            dv_ref[...] = dv_carry[...]

    cur = lambda i: (jnp.minimum(i, last), 0)
    prev = lambda i: (jnp.clip(i - 1, 0, last), 0)
    kv_meta = pl.BlockSpec((BLK, D_KV), lambda i: (0, 0))
    kv_prev = pl.BlockSpec((BLK, D_KV), prev)
    kv_cur = pl.BlockSpec((BLK, D_KV), cur)
    return _pcall(
        body, name=f"attn_bwd_l{layer}", grid=(n_blk + 1,),
        in_specs=[pl.BlockSpec(memory_space=pltpu.SMEM),
                  pl.BlockSpec((BLK, D_ATTN), cur),
                  kv_meta, kv_prev, kv_cur, kv_meta, kv_prev, kv_cur,
                  pl.BlockSpec((BLK, D_ATTN), cur)],
        out_specs=[pl.BlockSpec((BLK, D_ATTN), cur), kv_prev, kv_prev, kv_meta, kv_meta,
                   pl.BlockSpec((N_Q_HEADS, 128), lambda i: (0, 0))],
        out_shape=[SDS((rows, D_ATTN), F32), SDS((rows, D_KV), F32), SDS((rows, D_KV), F32),
                   SDS((BLK, D_KV), F32), SDS((BLK, D_KV), F32), SDS((N_Q_HEADS, 128), F32)],
        scratch_shapes=[pltpu.VMEM((BLK, D_KV), F32), pltpu.VMEM((BLK, D_KV), F32)],
        compiler_params=_cparams("arbitrary"),
    )(sinks, q, k, k, k, v, v, v, d_out)


def _rope_bwd(dq, dk, dv, dk_meta, dv_meta, cos, sin_a, sin_b, layer):
    rows = dq.shape[0]
    tm = _row_tile(rows)

    def body(dq_ref, dk_ref, dv_ref, dkm_ref, dvm_ref, c_ref, a_ref, b_ref, o_ref):
        c, a, b = c_ref[...], -a_ref[...], -b_ref[...]
        for t in range(8):
            x = dq_ref[:, t * 128:(t + 1) * 128]
            o_ref[:, t * 128:(t + 1) * 128] = (_rope_lanes(x, c, a, b) * ATTN_SCALE).astype(MXU_DTYPE)
        for t in range(2):
            x = dk_ref[:, t * 128:(t + 1) * 128]
            o_ref[:, D_ATTN + t * 128:D_ATTN + (t + 1) * 128] = _rope_lanes(x, c, a, b).astype(MXU_DTYPE)
        o_ref[:, D_ATTN + D_KV:] = dv_ref[...].astype(MXU_DTYPE)

        @pl.when(pl.program_id(0) == 0)
        def _():
            cb, ab, bb = c[0:BLK], a[0:BLK], b[0:BLK]
            is_meta = lax.broadcasted_iota(jnp.int32, (BLK, 128), 0) >= PAD_ROWS
            for t in range(2):
                x = dk_ref[0:BLK, t * 128:(t + 1) * 128] + jnp.where(is_meta, dkm_ref[:, t * 128:(t + 1) * 128], 0.0)
                o_ref[0:BLK, D_ATTN + t * 128:D_ATTN + (t + 1) * 128] = _rope_lanes(x, cb, ab, bb).astype(MXU_DTYPE)
                xv = dv_ref[0:BLK, t * 128:(t + 1) * 128] + jnp.where(is_meta, dvm_ref[:, t * 128:(t + 1) * 128], 0.0)
                o_ref[0:BLK, D_ATTN + D_KV + t * 128:D_ATTN + D_KV + (t + 1) * 128] = xv.astype(MXU_DTYPE)

    tab = pl.BlockSpec((tm, 128), lambda i: (i, 0))
    kv = pl.BlockSpec((tm, D_KV), lambda i: (i, 0))
    meta = pl.BlockSpec((BLK, D_KV), lambda i: (0, 0))
    return _pcall(
        body, name=f"rope_bwd_l{layer}", grid=(rows // tm,),
        in_specs=[pl.BlockSpec((tm, D_ATTN), lambda i: (i, 0)), kv, kv, meta, meta, tab, tab, tab],
        out_specs=pl.BlockSpec((tm, D_ATTN + 2 * D_KV), lambda i: (i, 0)),
        out_shape=SDS((rows, D_ATTN + 2 * D_KV), MXU_DTYPE),
        compiler_params=_cparams("parallel"),
    )(dq, dk, dv, dk_meta, dv_meta, cos, sin_a, sin_b)


def _s5_bwd(d_gated, y, u, carry_in, ssm, w_glu, b_glu3, layer):
    rows = y.shape[0]
    n_chunks = rows // BLK
    b_mat, c_mat, t_re, t_im, d_skip = (ssm[k] for k in ("b_mat", "c_mat", "t_re", "t_im", "d_skip"))

    def body(dz_ref, y_ref, u_ref, cin_ref, bm_ref, cm_ref, tre_ref, tim_ref, d_ref, wg_ref, bg_ref,
             du_ref, dwg_ref, dbg_ref, dd_ref, dbm_ref, dcm_ref, dab_ref,
             lam_carry, bu_scr, s_scr, sp_scr, g_scr, lam_scr):
        step = pl.program_id(0)
        chunk = n_chunks - 1 - step

        @pl.when(step == 0)
        def _():
            for r in (dwg_ref, dbg_ref, dd_ref, dbm_ref, dcm_ref, dab_ref, lam_carry):
                r[...] = jnp.zeros_like(r)

        y = y_ref[...]
        u = u_ref[...]
        d_o = dz_ref[...]
        z, t = _gelu_parts(y)
        zb = z.astype(MXU_DTYPE)
        sg = _sigmoid(_dot(zb, wg_ref[...]) + bg_ref[...])
        dgl = d_o * z * (sg * (1.0 - sg))
        dglb = dgl.astype(MXU_DTYPE)
        dz = d_o * sg + _dot_nt(dglb, wg_ref[...])
        dwg_ref[...] += _dot_tn(zb, dglb)
        dbg_ref[...] += jnp.sum(dgl, axis=0, keepdims=True)
        dy = dz * _gelu_grad(y, t)
        dd_ref[...] += jnp.sum(dy * u, axis=0, keepdims=True)
        grow = lax.broadcasted_iota(jnp.int32, (BLK, 128), 0) + chunk * BLK
        for sb in range(N_SB):
            cols = slice(sb * 128, (sb + 1) * 128)
            u_sb = u[:, cols].astype(MXU_DTYPE)
            dy_sb = dy[:, cols]
            dyb = dy_sb.astype(MXU_DTYPE)
            bu_scr[sb] = _dot(u_sb, bm_ref[sb])
            _scan_tiles(bu_scr.at[sb], s_scr.at[sb], tre_ref, tim_ref, sb,
                        cin_ref[2 * sb:2 * sb + 1, :], cin_ref[2 * sb + 1:2 * sb + 2, :], False, prev_ref=sp_scr.at[sb])
            dcm_ref[sb] += _dot_tn(s_scr[sb].astype(MXU_DTYPE), dyb)
            g_scr[sb] = _dot_nt(dyb, cm_ref[sb])
            n_r, n_i = _scan_tiles(g_scr.at[sb], lam_scr.at[sb], tre_ref, tim_ref, sb,
                                   lam_carry[2 * sb:2 * sb + 1, :], lam_carry[2 * sb + 1:2 * sb + 2, :], True)
            lam_carry[2 * sb:2 * sb + 1, :] = n_r
            lam_carry[2 * sb + 1:2 * sb + 2, :] = n_i
            lr, li = lam_scr[sb, :, :SB_STATES], lam_scr[sb, :, SB_STATES:]
            spr, spi = sp_scr[sb, :, :SB_STATES], sp_scr[sb, :, SB_STATES:]
            dab_ref[2 * sb:2 * sb + 1, :] += jnp.sum(spr * lr + spi * li, axis=0, keepdims=True)
            dab_ref[2 * sb + 1:2 * sb + 2, :] += jnp.sum(spr * li - spi * lr, axis=0, keepdims=True)
            lam = lam_scr[sb].astype(MXU_DTYPE)
            dbm_ref[sb] += _dot_tn(u_sb, lam)
            du = _dot_nt(lam, bm_ref[sb]) + d_ref[:, cols] * dy_sb
            du_ref[:, cols] = jnp.where(grow >= PAD_ROWS, du, 0.0).astype(MXU_DTYPE)

    rev = lambda j: (n_chunks - 1 - j, 0)
    full = lambda shape: pl.BlockSpec(shape, lambda j: (0,) * len(shape))
    tables = [full((N_SB, 8, SCAN_TILE, SB_STATES))] * 2
    chunk_scratch = pltpu.VMEM((N_SB, BLK, 2 * SB_STATES), F32)
    return _pcall(
        body, name=f"s5_bwd_l{layer}", grid=(n_chunks,),
        in_specs=[pl.BlockSpec((BLK, D_SSM), rev), pl.BlockSpec((BLK, D_SSM), rev), pl.BlockSpec((BLK, D_SSM), rev),
                  pl.BlockSpec((None, 8, SB_STATES), lambda j: (n_chunks - 1 - j, 0, 0)),
                  full((N_SB, 128, 2 * SB_STATES)), full((N_SB, 2 * SB_STATES, 128))] + tables + [
                  full((1, D_SSM)), full((D_SSM, D_SSM)),
                  pl.BlockSpec((None, 1, D_SSM), lambda j: (layer, 0, 0))],
        out_specs=[pl.BlockSpec((BLK, D_SSM), rev), full((D_SSM, D_SSM)), full((1, D_SSM)), full((1, D_SSM)),
                   full((N_SB, 128, 2 * SB_STATES)), full((N_SB, 2 * SB_STATES, 128)), full((8, SB_STATES))],
        out_shape=[SDS((rows, D_SSM), MXU_DTYPE), SDS((D_SSM, D_SSM), F32), SDS((1, D_SSM), F32), SDS((1, D_SSM), F32),
                   SDS((N_SB, 128, 2 * SB_STATES), F32), SDS((N_SB, 2 * SB_STATES, 128), F32), SDS((8, SB_STATES), F32)],
        scratch_shapes=[pltpu.VMEM((8, SB_STATES), F32)] + [chunk_scratch] * 5,
        compiler_params=_cparams("arbitrary"),
    )(d_gated, y, u, carry_in, b_mat, c_mat, t_re, t_im, d_skip, w_glu, b_glu3)


DPROJ_PIECES = ((0, 1), (1, 3), (4, 2), (6, 2))


def _in_bwd(dproj_pieces, dhm, hres, gain3, w_in_g, layer):
    rows = hres.shape[0]
    tm = _row_tile(rows)

    def body(du_ref, dqkv_ref, dgs_ref, dga_ref, dh_ref, x_ref, g_ref, w_hbm, dx_ref, dg_ref, acc, w_scr, w_sem):
        i = pl.program_id(0)
        j = pl.program_id(1)
        _load_resident(w_hbm, w_scr, w_sem, (i == 0) & (j == 0))

        @pl.when((i == 0) & (j == 0))
        def _():
            dg_ref[...] = jnp.zeros_like(dg_ref)

        @pl.when(j == 0)
        def _():
            acc[...] = jnp.zeros_like(acc)

        for piece_ref, (first, count) in zip((du_ref, dqkv_ref, dgs_ref, dga_ref), DPROJ_PIECES):
            @pl.when((j >= first) & (j < first + count))
            def _():
                acc[...] += _dot_nt(piece_ref[...], w_scr[j])

        @pl.when(j == N_DEV - 1)
        def _():
            dx, dg = _rms_bwd(x_ref[...], g_ref[...], acc[...])
            dg_ref[...] += dg
            dx_ref[...] = dh_ref[...] + dx

    row_d = pl.BlockSpec((tm, D), lambda i, j: (i, 0))

    def piece_spec(first, count):
        return pl.BlockSpec((tm, COL_SHARD), lambda i, j: (i, jnp.clip(j - first, 0, count - 1)))

    return _pcall(
        body, name=f"in_bwd_l{layer}", grid=(rows // tm, N_DEV),
        in_specs=[piece_spec(*p) for p in DPROJ_PIECES] + [
                  row_d, row_d,
                  pl.BlockSpec((None, 1, D), lambda i, j: (layer, 0, 0)),
                  pl.BlockSpec(memory_space=pl.ANY)],
        out_specs=[row_d, pl.BlockSpec((1, D), lambda i, j: (0, 0))],
        out_shape=[SDS((rows, D), F32), SDS((1, D), F32)],
        scratch_shapes=[pltpu.VMEM((tm, D), F32), pltpu.VMEM((N_DEV, D, COL_SHARD), MXU_DTYPE),
                        pltpu.SemaphoreType.DMA((N_DEV,))],
        compiler_params=_cparams("arbitrary", "arbitrary"),
    )(*dproj_pieces, dhm, hres, gain3, w_in_g)


_ADAM_C1 = 1.0 / (1.0 - ADAM_B1 ** ADAM_STEP)
_ADAM_C2 = 1.0 / (1.0 - ADAM_B2 ** ADAM_STEP)


def _adam_math(w, g, m, v):
    m = ADAM_B1 * m + (1.0 - ADAM_B1) * g
    v = ADAM_B2 * v + (1.0 - ADAM_B2) * (g * g)
    delta = -ADAM_LR * ((m * _ADAM_C1) / (jnp.sqrt(v * _ADAM_C2) + ADAM_EPS) + ADAM_WD * w)
    return delta, m, v


def _adamw_layers(parts0, parts1, w, m, v, name):
    _, rows, cols = w.shape
    tr = min(rows, (1 << 16) // cols)
    nt = rows // tr

    def body(p0_ref, p1_ref, w_ref, m_ref, v_ref, g_ref, d_ref, nm_ref, nv_ref):
        layer = pl.program_id(0)

        def run(p_ref):
            g = p_ref[0].astype(F32)
            for s in range(1, N_DEV):
                g = g + p_ref[s].astype(F32)
            delta, nm, nv = _adam_math(w_ref[...], g, m_ref[...], v_ref[...])
            g_ref[...] = g
            d_ref[...] = delta
            nm_ref[...] = nm
            nv_ref[...] = nv

        @pl.when(layer == 0)
        def _():
            run(p0_ref)

        @pl.when(layer == 1)
        def _():
            run(p1_ref)

    wspec = pl.BlockSpec((None, tr, cols), lambda l, i: (l, i, 0))
    return _pcall(
        body, name=name, grid=(2, nt),
        in_specs=[pl.BlockSpec((N_DEV, tr, cols), lambda l, i: (0, jnp.where(l == 0, i, nt - 1), 0)),
                  pl.BlockSpec((N_DEV, tr, cols), lambda l, i: (0, jnp.where(l == 1, i, 0), 0)),
                  wspec, wspec, wspec],
        out_specs=[wspec] * 4, out_shape=[SDS(w.shape, F32)] * 4,
        compiler_params=_cparams("arbitrary", "arbitrary"),
    )(parts0, parts1, w, m, v)


def _sum_slots(parts, name):
    def body(p_ref, o_ref):
        acc = p_ref[0]
        for s in range(1, N_DEV):
            acc = acc + p_ref[s]
        o_ref[...] = acc

    vmem = pl.BlockSpec(memory_space=pltpu.VMEM)
    return _pcall(body, name=name, out_shape=SDS(parts.shape[1:], F32), in_specs=[vmem], out_specs=vmem,
                  compiler_params=_cparams())(parts)


def _adamw_packed(g, w, m, v, name):
    def body(g_ref, w_ref, m_ref, v_ref, d_ref, nm_ref, nv_ref):
        delta, nm, nv = _adam_math(w_ref[...], g_ref[...], m_ref[...], v_ref[...])
        d_ref[...] = delta
        nm_ref[...] = nm
        nv_ref[...] = nv

    vmem = pl.BlockSpec(memory_space=pltpu.VMEM)
    return _pcall(body, name=name, out_shape=[SDS(g.shape, F32)] * 3, in_specs=[vmem] * 4, out_specs=[vmem] * 3,
                  compiler_params=_cparams())(g, w, m, v)


def _ssm_discretize(a_re, a_im, log_dt, b_re, b_im):
    dt = jnp.exp(log_dt)[:, None]
    mag = jnp.exp(a_re * dt)
    ang = a_im * dt
    ab_re, ab_im = mag * jnp.cos(ang), mag * jnp.sin(ang)
    xr, xi = ab_re - 1.0, ab_im
    den = a_re * a_re + a_im * a_im
    q_re = (xr * a_re + xi * a_im) / den
    q_im = (xi * a_re - xr * a_im) / den
    bb_re = q_re[..., None] * b_re - q_im[..., None] * b_im
    bb_im = q_re[..., None] * b_im + q_im[..., None] * b_re
    return ab_re, ab_im, bb_re, bb_im


def _block_diag_b(bb):
    m = jnp.einsum("sgnc,gh->sgchn", bb.reshape(N_SB, 8, N_STATE, GROUP_CH), jnp.eye(8, dtype=F32))
    return m.reshape(N_SB, 128, SB_STATES)


def _block_diag_b_t(dm):
    return jnp.einsum("sgchn,gh->sgnc", dm.reshape(N_SB, 8, GROUP_CH, 8, N_STATE),
                      jnp.eye(8, dtype=F32)).reshape(N_GROUPS, N_STATE, GROUP_CH)


def _block_diag_c(cc):
    m = jnp.einsum("sgcn,gh->sgnhc", cc.reshape(N_SB, 8, GROUP_CH, N_STATE), jnp.eye(8, dtype=F32))
    return m.reshape(N_SB, SB_STATES, 128)


def _block_diag_c_t(dm):
    return jnp.einsum("sgnhc,gh->sgcn", dm.reshape(N_SB, 8, N_STATE, 8, GROUP_CH),
                      jnp.eye(8, dtype=F32)).reshape(N_GROUPS, GROUP_CH, N_STATE)


def _ssm_tables(ab_re, ab_im, bb_re, bb_im, c_re, c_im, d_skip):
    pr, pi = ab_re.reshape(1, -1), ab_im.reshape(1, -1)
    cr, ci = pr, pi
    squares = []
    for _ in range(3):
        squares.append((cr, ci))
        pr, pi = (jnp.concatenate([pr, pr * cr - pi * ci], axis=0),
                  jnp.concatenate([pi, pr * ci + pi * cr], axis=0))
        cr, ci = cr * cr - ci * ci, 2.0 * cr * ci
    r = jnp.arange(SCAN_TILE)[:, None]
    fwd = [(jnp.where(r >= (1 << k), squares[k][0], 0.0), jnp.where(r >= (1 << k), squares[k][1], 0.0))
           for k in range(3)] + [(pr, pi)]
    rev = [(jnp.where(r < SCAN_TILE - (1 << k), squares[k][0], 0.0),
            jnp.where(r < SCAN_TILE - (1 << k), -squares[k][1], 0.0)) for k in range(3)] + [(pr[::-1], -pi[::-1])]
    table = lambda part: jnp.stack([e[part] for e in fwd + rev]).reshape(
        8, SCAN_TILE, N_SB, SB_STATES).transpose(2, 0, 1, 3)
    return dict(
        b_mat=jnp.concatenate([_block_diag_b(bb_re), _block_diag_b(bb_im)], axis=-1).astype(MXU_DTYPE),
        c_mat=jnp.concatenate([_block_diag_c(c_re), -_block_diag_c(c_im)], axis=1).astype(MXU_DTYPE),
        t_re=table(0), t_im=table(1),
        d_skip=d_skip.reshape(1, D_SSM))


def _rope_tables(rows):
    pos = (jnp.arange(rows, dtype=jnp.int32) - PAD_ROWS).astype(F32)
    inv_freq = 1.0 / (ROPE_THETA ** (jnp.arange(0, HEAD_DIM, 2, dtype=F32) / HEAD_DIM))
    ang = pos[:, None] * inv_freq[None, :]
    ang = jnp.concatenate([ang, ang, ang, ang], axis=-1)
    first_half = (jnp.arange(128) % HEAD_DIM) < HEAD_DIM // 2
    sin = jnp.sin(ang)
    return jnp.cos(ang), jnp.where(first_half, -sin, 0.0), jnp.where(first_half, 0.0, sin)


def _pack(arrays):
    flat = jnp.concatenate([a.reshape(-1).astype(F32) for a in arrays])
    pad = (-flat.shape[0]) % 1024
    return jnp.pad(flat, (0, pad)).reshape(-1, 128)


def _unpack(packed, like):
    flat = packed.reshape(-1)
    out, off = [], 0
    for a in like:
        n = math.prod(a.shape)
        out.append(flat[off:off + n].reshape(a.shape))
        off += n
    return out


BIG = ("w_in", "w_glu", "w_o_ssm", "w_o_attn", "w_out", "w_up", "w_down")
WEIGHTS = ("meta_tokens", "norm_mix_pre", "norm_mix_post", "norm_mlp_pre", "norm_mlp_post", "w_in",
           "ssm_a_re", "ssm_a_im", "ssm_log_dt", "ssm_b_re", "ssm_b_im", "ssm_c_re", "ssm_c_im", "ssm_d",
           "w_glu", "b_glu", "attn_sinks", "w_o_ssm", "w_o_attn", "w_out", "w_up", "w_down")
SMALL = tuple(n for n in WEIGHTS if n not in BIG)


def kernel(x, meta_tokens, norm_mix_pre, norm_mix_post, norm_mlp_pre, norm_mlp_post, w_in, ssm_a_re, ssm_a_im, ssm_log_dt, ssm_b_re, ssm_b_im, ssm_c_re, ssm_c_im, ssm_d, w_glu, b_glu, attn_sinks, w_o_ssm, w_o_attn, w_out, w_up, w_down, loss_target, m_meta_tokens, m_norm_mix_pre, m_norm_mix_post, m_norm_mlp_pre, m_norm_mlp_post, m_w_in, m_ssm_a_re, m_ssm_a_im, m_ssm_log_dt, m_ssm_b_re, m_ssm_b_im, m_ssm_c_re, m_ssm_c_im, m_ssm_d, m_w_glu, m_b_glu, m_attn_sinks, m_w_o_ssm, m_w_o_attn, m_w_out, m_w_up, m_w_down, v_meta_tokens, v_norm_mix_pre, v_norm_mix_post, v_norm_mlp_pre, v_norm_mlp_post, v_w_in, v_ssm_a_re, v_ssm_a_im, v_ssm_log_dt, v_ssm_b_re, v_ssm_b_im, v_ssm_c_re, v_ssm_c_im, v_ssm_d, v_w_glu, v_b_glu, v_attn_sinks, v_w_o_ssm, v_w_o_attn, v_w_out, v_w_up, v_w_down):
    args = locals()
    w = {n: args[n] for n in WEIGHTS}
    m = {n: args["m_" + n] for n in WEIGHTS}
    v = {n: args["v_" + n] for n in WEIGHTS}
    n_layers = w_in.shape[0]
    seq = x.shape[1]
    rows = seq + BLK
    my_slot = _slot(_mesh_pos())

    assert n_layers == 2
    xfer = {n: w[n].astype(XFER_DTYPE) for n in BIG}
    mixer_small = ("w_glu", "w_o_ssm", "w_o_attn", "w_out")
    meta_g, w_in_g0 = _exchange_by_sequencer([meta_tokens, xfer["w_in"][0]], True, 0, "gather_in0")
    mix0_g = _exchange_by_sequencer([xfer[n][0] for n in mixer_small], True, 1, "gather_mix0", after=[meta_g])
    meta_full = meta_g.transpose(1, 0, 2).reshape(N_META, D)

    def mixer_weights(w_glu_g, w_o_ssm_g, w_o_attn_g, w_out_g):
        return dict(w_glu=w_glu_g.reshape(D_SSM, D_SSM), w_o_ssm=w_o_ssm_g.transpose(1, 0, 2).reshape(D_SSM, D),
                    w_o_attn=w_o_attn_g.reshape(D_ATTN, D), w_out=w_out_g.reshape(D, D))

    gathered = [dict(w_in=w_in_g0, **mixer_weights(*mix0_g)), {}]

    gains = {n: w[n].reshape(n_layers, 1, D) for n in ("norm_mix_pre", "norm_mix_post", "norm_mlp_pre", "norm_mlp_post")}
    b_glu3 = b_glu.reshape(n_layers, 1, D_SSM)
    cos, sin_a, sin_b = _rope_tables(rows)

    def ssm_setup(l):
        disc, disc_vjp = jax.vjp(_ssm_discretize, ssm_a_re[l], ssm_a_im[l], ssm_log_dt[l], ssm_b_re[l], ssm_b_im[l])
        return _ssm_tables(*disc, ssm_c_re[l], ssm_c_im[l], ssm_d[l]), disc_vjp

    hres = jnp.concatenate([jnp.zeros((PAD_ROWS, D), F32), meta_full, x[0]], axis=0)

    saved = []
    for l in range(n_layers):
        ssm, disc_vjp = ssm_setup(l)
        wl = gathered[l]
        u, gates, q, k, vv, h = _in_proj(hres, gains["norm_mix_pre"], wl["w_in"], cos, sin_a, sin_b, l)
        if l == 0:
            wl["w_up"], wl["w_down"] = _exchange_by_sequencer([xfer["w_up"][0], xfer["w_down"][0]], True, 2,
                                                              "gather_mlp0", after=[h])
        y, y_ssm, carry_in = _s5_fwd(u, ssm, wl["w_glu"], b_glu3, l)
        if l == 0:
            l1_g = _exchange_by_sequencer([xfer[n][1] for n in ("w_in",) + mixer_small + ("w_up", "w_down")], True, 3,
                                          "gather_l1", after=[y])
            gathered[1] = dict(w_in=l1_g[0], w_up=l1_g[5], w_down=l1_g[6], **mixer_weights(*l1_g[1:5]))
            last_exchange = l1_g[:1]
        y_attn = _attn_fwd(q, k, vv, attn_sinks, l)
        merged, mix, hres_mid = _merge_fwd(y_ssm, y_attn, gates, hres, wl["w_o_ssm"], wl["w_o_attn"], wl["w_out"],
                                           gains["norm_mix_post"], l)
        up, h2, ff, hres_out = _mlp_fwd(hres_mid, gains["norm_mlp_pre"], gains["norm_mlp_post"], wl["w_up"],
                                        wl["w_down"], l)
        saved.append(dict(ssm=ssm, disc_vjp=disc_vjp, hres=hres, u=u, gates=gates, h=h, q=q, k=k, v=vv, y=y, y_ssm=y_ssm,
                          carry_in=carry_in, y_attn=y_attn, merged=merged, mix=mix, hres_mid=hres_mid,
                          up=up, h2=h2, ff=ff))
        hres = hres_out

    dhres, loss_vec = _loss_and_grad(hres, loss_target[0])
    loss = lax.psum(loss_vec[0, 0], MESH_AXES)

    small_grads = {n: [None] * n_layers for n in SMALL if n != "meta_tokens"}
    recv_up, recv_down, recv_mix = [None] * n_layers, [None] * n_layers, [None] * n_layers
    for l in reversed(range(n_layers)):
        s = saved[l]
        wl = gathered[l]
        dff, dup, dhm, dg_mlp_post, dg_mlp_pre = _mlp_bwd(dhres, s["ff"], s["up"], s["hres_mid"], gains["norm_mlp_pre"],
                                                          gains["norm_mlp_post"], wl["w_up"], wl["w_down"], l)
        dw_up = _matmul_tn(s["h2"], dup, f"dw_up_l{l}", dev_major_cols=COL_SHARD)
        recv_up[l] = _exchange_by_sequencer([dw_up], False, 4 + 3 * l, f"scatter_up{l}", after=last_exchange)
        dw_down = _matmul_tn(s["up"], dff, f"dw_down_l{l}", a_fn=_relu_squared).reshape(N_DEV, COL_SHARD, D)
        recv_down[l] = _exchange_by_sequencer([dw_down], False, 5 + 3 * l, f"scatter_down{l}", after=recv_up[l])
        last_exchange = recv_down[l]
        dmix, da1, da2, dgs, dga, dy_ssm, dy_attn, dg_mix_post = _merge_bwd(
            dhm, s["mix"], s["y_ssm"], s["y_attn"], s["gates"], wl["w_o_ssm"], wl["w_o_attn"], wl["w_out"],
            gains["norm_mix_post"], l)
        dw_out = _matmul_tn(s["merged"], dmix, f"dw_out_l{l}").reshape(N_DEV, D // N_DEV, D)
        dw_o_attn = _matmul_tn(s["y_attn"], da2, f"dw_o_attn_l{l}").reshape(N_DEV, D_ATTN // N_DEV, D)
        dw_o_ssm = _matmul_tn(s["y_ssm"], da1, f"dw_o_ssm_l{l}", dev_major_cols=D // N_DEV)
        if l == 0:
            recv_out0 = _exchange_by_sequencer([dw_o_ssm, dw_o_attn, dw_out], False, 11, "scatter_out0",
                                               after=last_exchange)
            last_exchange = recv_out0[:1]
        dq, dk, dv, dk_meta, dv_meta, dsink = _attn_bwd(s["q"], s["k"], s["v"], dy_attn, attn_sinks, l)
        dqkv = _rope_bwd(dq, dk, dv, dk_meta, dv_meta, cos, sin_a, sin_b, l)
        du, dw_glu, db_glu, dd_skip, db_mat, dc_mat, dab = _s5_bwd(dy_ssm, s["y"], s["u"], s["carry_in"], s["ssm"],
                                                                    wl["w_glu"], b_glu3, l)
        dproj = (du, dqkv, dgs, dga)
        dw_in = jnp.concatenate([_matmul_tn(s["h"], piece, f"dw_in{k}_l{l}", dev_major_cols=COL_SHARD)
                                 for k, piece in enumerate(dproj)], axis=0)
        dhres, dg_mix_pre = _in_bwd(dproj, dhm, s["hres"], gains["norm_mix_pre"], wl["w_in"], l)
        mix_parts = [dw_in, dw_glu.astype(XFER_DTYPE).reshape(N_DEV, D_SSM // N_DEV, D_SSM), dw_o_ssm, dw_o_attn, dw_out]
        if l > 0:
            recv_mix[l] = _exchange_by_sequencer(mix_parts, False, 6 + 3 * l, f"scatter_mix{l}", after=last_exchange)
            last_exchange = recv_mix[l][:1]

        dab = dab.reshape(N_SB, 2, SB_STATES)
        da_re, da_im, dlog_dt, db_re, db_im = s["disc_vjp"]((
            dab[:, 0].reshape(N_GROUPS, N_STATE), dab[:, 1].reshape(N_GROUPS, N_STATE),
            _block_diag_b_t(db_mat[..., :SB_STATES]), _block_diag_b_t(db_mat[..., SB_STATES:])))
        for name, val in (("norm_mix_pre", dg_mix_pre[0]), ("norm_mix_post", dg_mix_post[0]),
                          ("norm_mlp_pre", dg_mlp_pre[0]), ("norm_mlp_post", dg_mlp_post[0]),
                          ("ssm_a_re", da_re), ("ssm_a_im", da_im), ("ssm_log_dt", dlog_dt),
                          ("ssm_b_re", db_re), ("ssm_b_im", db_im),
                          ("ssm_c_re", _block_diag_c_t(dc_mat[:, :SB_STATES])),
                          ("ssm_c_im", -_block_diag_c_t(dc_mat[:, SB_STATES:])),
                          ("ssm_d", dd_skip.reshape(N_GROUPS, GROUP_CH)), ("b_glu", db_glu[0]),
                          ("attn_sinks", dsink[:, 0])):
            small_grads[name][l] = val

    grad_x = dhres[BLK:][None]
    small_names = [n for n in SMALL if n != "meta_tokens"]
    partial_small = [dhres[PAD_ROWS:BLK]] + [jnp.stack(small_grads[n]) for n in small_names]
    recv_in0, recv_glu0, small_parts = _exchange_by_sequencer(
        mix_parts[:2] + [_pack(partial_small)], [False, False, True], 6, "scatter_in0", after=last_exchange)
    recv_mix[0] = [recv_in0, recv_glu0] + recv_out0

    grads, delta, new_m, new_v = {}, {}, {}, {}

    def adamw_big(names, recv0, recv1):
        for n, p0, p1 in zip(names, recv0, recv1):
            grads[n], delta[n], new_m[n], new_v[n] = _adamw_layers(p0, p1, w[n], m[n], v[n], f"adamw_{n}")

    adamw_big(("w_up", "w_down"), recv_up[0] + recv_down[0], recv_up[1] + recv_down[1])
    summed = _unpack(_sum_slots(small_parts, "sum_small_grads"), partial_small)
    grads.update(zip(small_names, summed[1:]))
    grads["meta_tokens"] = lax.dynamic_slice_in_dim(summed[0], my_slot * (D // N_DEV), D // N_DEV, axis=1)
    like = [w[n] for n in SMALL]
    d_s, m_s, v_s = _adamw_packed(_pack([grads[n] for n in SMALL]), _pack(like), _pack([m[n] for n in SMALL]),
                                  _pack([v[n] for n in SMALL]), "adamw_small")
    adamw_big(("w_in",) + mixer_small, recv_mix[0], recv_mix[1])
    for n, dd, mm, vs in zip(SMALL, _unpack(d_s, like), _unpack(m_s, like), _unpack(v_s, like)):
        delta[n], new_m[n], new_v[n] = dd, mm, vs

    return (loss, grad_x, *[grads[n] for n in WEIGHTS], *[delta[n] for n in WEIGHTS],
            *[new_m[n] for n in WEIGHTS], *[new_v[n] for n in WEIGHTS])
```

```python
import functools
import math

import jax
import jax.numpy as jnp
from jax import lax
from jax.experimental import pallas as pl
from jax.experimental.pallas import tpu as pltpu
from jax.experimental.pallas import tpu_sc as plsc

F32 = jnp.float32
MXU_DTYPE = jnp.bfloat16
XFER_DTYPE = MXU_DTYPE
_pcall = pl.pallas_call
SDS = jax.ShapeDtypeStruct

D = 1024
D_SSM = 512
D_ATTN = 1024
D_KV = 256
D_FF = 4096
D_IN = 4096
HEAD_DIM = 64
N_Q_HEADS = 16
N_KV_HEADS = 4
Q_PER_KV = 4
N_META = 16
BLK = 128
PAD_ROWS = BLK - N_META
N_GROUPS = 32
N_STATE = 64
GROUP_CH = 16
N_SB = 4
SB_STATES = 512
ROPE_THETA = 10000.0
ATTN_SCALE = HEAD_DIM ** -0.5
NEG_INF = -1e30
RMS_EPS = 1e-6
N_DEV = 8
COL_SHARD = 512

ADAM_LR = 0.001
ADAM_B1 = 0.9
ADAM_B2 = 0.999
ADAM_EPS = 1e-08
ADAM_WD = 0.01
ADAM_STEP = 10

VMEM_LIMIT = 56 * 1024 * 1024
MESH_AXES = ("x", "y", "c")

_NT = (((1,), (1,)), ((), ()))
_TN = (((0,), (0,)), ((), ()))


def _cparams(*sem):
    return pltpu.CompilerParams(dimension_semantics=tuple(sem) if sem else None,
                                vmem_limit_bytes=VMEM_LIMIT)


def _row_tile(rows, cap=640):
    for t in (640, 512, 320, 256, 128):
        if t <= cap and rows % t == 0:
            return t
    raise ValueError(f"unsupported row count {rows}")


def _dot(a, b):
    return jnp.dot(a, b, preferred_element_type=F32)


def _dot_nt(a, b):
    return lax.dot_general(a, b, _NT, preferred_element_type=F32)


def _dot_tn(a, b):
    return lax.dot_general(a, b, _TN, preferred_element_type=F32)


def _sigmoid(x):
    return 1.0 / (1.0 + jnp.exp(-x))


_GELU_C = math.sqrt(2.0 / math.pi)


def _gelu_parts(y):
    t = jnp.tanh(_GELU_C * (y + 0.044715 * (y * y * y)))
    return 0.5 * y * (1.0 + t), t


def _gelu_grad(y, t):
    return 0.5 * (1.0 + t) + 0.5 * y * (1.0 - t * t) * (_GELU_C * (1.0 + 0.134145 * (y * y)))


def _rms_fwd(x, gain):
    r = lax.rsqrt(jnp.mean(x * x, axis=-1, keepdims=True) + RMS_EPS)
    return (x * r) * gain


def _rms_bwd(x, gain, dout):
    r = lax.rsqrt(jnp.mean(x * x, axis=-1, keepdims=True) + RMS_EPS)
    xh = x * r
    dxh = dout * gain
    dx = r * (dxh - xh * jnp.mean(dxh * xh, axis=-1, keepdims=True))
    return dx, jnp.sum(dout * xh, axis=0, keepdims=True)


def _mesh_pos():
    return lax.axis_index("x"), lax.axis_index("y"), lax.axis_index("c")


def _peer(pos, d):
    x, y, c = pos
    return (1 - x if d & 4 else x, 1 - y if d & 2 else y, 1 - c if d & 1 else c)


def _slot(pos):
    return 4 * pos[0] + 2 * pos[1] + pos[2]


def _exchange_copy(gather, src_ref, land_ref, sems, k, d, me, send_side):
    peer = _peer(me, d)
    sender = me if send_side else peer
    src = src_ref if gather else src_ref.at[_slot(peer) if send_side else _slot(me)]
    return pltpu.make_async_remote_copy(
        src_ref=src, dst_ref=land_ref.at[_slot(sender)],
        send_sem=sems[0].at[k * (N_DEV - 1) + d - 1], recv_sem=sems[1].at[k * (N_DEV - 1) + d - 1],
        device_id=peer, device_id_type=pl.DeviceIdType.MESH)


def _exchange_by_sequencer(srcs, gather, collective_id, name, after=()):
    n = len(srcs)
    flags = [gather] * n if isinstance(gather, bool) else list(gather)
    land_types = [SDS(((N_DEV,) + s.shape) if g else s.shape, s.dtype) for s, g in zip(srcs, flags)]

    def body(*refs):
        src_refs = refs[:n]
        land_refs = refs[n + len(after):2 * n + len(after)]
        sems = refs[2 * n + len(after):2 * n + len(after) + 2]
        local_sems = refs[2 * n + len(after) + 2]
        me = _mesh_pos()
        barrier = pltpu.get_barrier_semaphore()
        for d in range(1, N_DEV):
            pl.semaphore_signal(barrier, inc=1, device_id=_peer(me, d), device_id_type=pl.DeviceIdType.MESH)
        pl.semaphore_wait(barrier, N_DEV - 1)
        own = [pltpu.make_async_copy(src_refs[k] if flags[k] else src_refs[k].at[_slot(me)],
                                     land_refs[k].at[_slot(me)], local_sems.at[k]) for k in range(n)]
        for cp in own:
            cp.start()
        for k in range(n):
            for d in range(1, N_DEV):
                _exchange_copy(flags[k], src_refs[k], land_refs[k], sems, k, d, me, True).start()
        for cp in own:
            cp.wait()
        for k in range(n):
            for d in range(1, N_DEV):
                _exchange_copy(flags[k], src_refs[k], land_refs[k], sems, k, d, me, True).wait_send()
        for k in range(n):
            for d in range(1, N_DEV):
                _exchange_copy(flags[k], src_refs[k], land_refs[k], sems, k, d, me, False).wait_recv()

    sem_type = pltpu.SemaphoreType.DMA((n * (N_DEV - 1),))
    return pl.kernel(
        body, out_type=land_types, mesh=plsc.ScalarSubcoreMesh(axis_name="sequencer", num_cores=1), name=name,
        scratch_types=(sem_type, sem_type, pltpu.SemaphoreType.DMA((n,))),
        compiler_params=pltpu.CompilerParams(collective_id=collective_id),
    )(*srcs, *after)


def _load_resident(w_hbm, w_scr, sems, first_step):
    @pl.when(first_step)
    def _():
        copies = [pltpu.make_async_copy(w_hbm.at[s], w_scr.at[s], sems.at[s]) for s in range(N_DEV)]
        for cp in copies:
            cp.start()
        for cp in copies:
            cp.wait()


def _load_resident_transposed(w_hbm, w_scr, stage, sems, first_step):
    @pl.when(first_step)
    def _():
        copies = [pltpu.make_async_copy(w_hbm.at[s], stage.at[s % 2], sems.at[s % 2]) for s in range(N_DEV)]
        copies[0].start()
        for s in range(N_DEV):
            if s + 1 < N_DEV:
                copies[s + 1].start()
            copies[s].wait()
            w_scr[s] = stage[s % 2].T


def _rope_lanes(t, cos, sin_a, sin_b):
    return t * cos + pltpu.roll(t, 96, 1) * sin_a + pltpu.roll(t, 32, 1) * sin_b


def _in_proj(hres, gain3, w_in_g, cos, sin_a, sin_b, layer):
    rows = hres.shape[0]
    tm = _row_tile(rows)

    def body(x_ref, g_ref, w_hbm, c_ref, a_ref, b_ref, u_ref, gate_ref, q_ref, k_ref, v_ref, h_ref,
             h_scr, w_scr, w_sem):
        j = pl.program_id(1)
        _load_resident(w_hbm, w_scr, w_sem, (pl.program_id(0) == 0) & (j == 0))

        @pl.when(j == 0)
        def _():
            hn = _rms_fwd(x_ref[...], g_ref[...]).astype(MXU_DTYPE)
            h_scr[...] = hn
            h_ref[...] = hn
            u_ref[...] = _dot(hn, w_scr[0])

        @pl.when((j == 1) | (j == 2))
        def _():
            res = _dot(h_scr[...], w_scr[j])
            c, a, b = c_ref[...], a_ref[...], b_ref[...]
            for t in range(4):
                lanes = slice(t * 128, (t + 1) * 128)
                q_ref[:, lanes] = (_rope_lanes(res[:, lanes], c, a, b) * ATTN_SCALE).astype(MXU_DTYPE)

        @pl.when(j == 3)
        def _():
            res = _dot(h_scr[...], w_scr[3])
            c, a, b = c_ref[...], a_ref[...], b_ref[...]
            for t in range(2):
                lanes = slice(t * 128, (t + 1) * 128)
                k_ref[:, lanes] = _rope_lanes(res[:, lanes], c, a, b).astype(MXU_DTYPE)
            v_ref[...] = res[:, D_KV:].astype(MXU_DTYPE)

        @pl.when(j >= 4)
        def _():
            gate_ref[...] = _dot(h_scr[...], w_scr[j])

    tab = pl.BlockSpec((tm, 128), lambda i, j: (i, 0))
    kv = pl.BlockSpec((tm, D_KV), lambda i, j: (i, 0))
    return _pcall(
        body, name=f"in_proj_l{layer}", grid=(rows // tm, N_DEV),
        in_specs=[pl.BlockSpec((tm, D), lambda i, j: (i, 0)),
                  pl.BlockSpec((None, 1, D), lambda i, j: (layer, 0, 0)),
                  pl.BlockSpec(memory_space=pl.ANY), tab, tab, tab],
        out_specs=[pl.BlockSpec((tm, COL_SHARD), lambda i, j: (i, 0)),
                   pl.BlockSpec((tm, COL_SHARD), lambda i, j: (i, jnp.clip(j - 4, 0, 3))),
                   pl.BlockSpec((tm, COL_SHARD), lambda i, j: (i, jnp.clip(j - 1, 0, 1))),
                   kv, kv, pl.BlockSpec((tm, D), lambda i, j: (i, 0))],
        out_shape=[SDS((rows, D_SSM), F32), SDS((rows, 2 * D), F32), SDS((rows, D_ATTN), MXU_DTYPE),
                   SDS((rows, D_KV), MXU_DTYPE), SDS((rows, D_KV), MXU_DTYPE), SDS((rows, D), MXU_DTYPE)],
        scratch_shapes=[pltpu.VMEM((tm, D), MXU_DTYPE), pltpu.VMEM((N_DEV, D, COL_SHARD), MXU_DTYPE),
                        pltpu.SemaphoreType.DMA((N_DEV,))],
        compiler_params=_cparams("arbitrary", "arbitrary"),
    )(hres, gain3, w_in_g, cos, sin_a, sin_b)


SCAN_TILE = 8


def _scan_tiles(x_ref, out_ref, tre_ref, tim_ref, sb, t_r, t_i, reverse, prev_ref=None):
    base = 4 if reverse else 0
    n_tiles = BLK // SCAN_TILE
    row = lax.broadcasted_iota(jnp.int32, (SCAN_TILE, SB_STATES), 0)
    for j in (range(n_tiles - 1, -1, -1) if reverse else range(n_tiles)):
        rows = slice(SCAN_TILE * j, SCAN_TILE * (j + 1))
        xr = x_ref[rows, :SB_STATES]
        xi = x_ref[rows, SB_STATES:]
        for k in range(3):
            shift = SCAN_TILE - (1 << k) if reverse else (1 << k)
            rr = pltpu.roll(xr, shift, 0)
            ri = pltpu.roll(xi, shift, 0)
            ar = tre_ref[sb, base + k]
            ai = tim_ref[sb, base + k]
            xr, xi = xr + (ar * rr - ai * ri), xi + (ar * ri + ai * rr)
        pr = tre_ref[sb, base + 3]
        pi = tim_ref[sb, base + 3]
        xr, xi = xr + (pr * t_r - pi * t_i), xi + (pr * t_i + pi * t_r)
        out_ref[rows, :SB_STATES] = xr
        out_ref[rows, SB_STATES:] = xi
        if prev_ref is not None:
            prev_ref[rows, :SB_STATES] = jnp.where(row == 0, t_r, pltpu.roll(xr, 1, 0))
            prev_ref[rows, SB_STATES:] = jnp.where(row == 0, t_i, pltpu.roll(xi, 1, 0))
        edge = slice(0, 1) if reverse else slice(SCAN_TILE - 1, SCAN_TILE)
        t_r, t_i = xr[edge], xi[edge]
    return t_r, t_i


def _s5_fwd(u, ssm, w_glu, b_glu3, layer):
    rows = u.shape[0]
    n_chunks = rows // BLK
    b_mat, c_mat, t_re, t_im, d_skip = (ssm[k] for k in ("b_mat", "c_mat", "t_re", "t_im", "d_skip"))

    def body(u_ref, bm_ref, cm_ref, tre_ref, tim_ref, d_ref, wg_ref, bg_ref,
             y_ref, ys_ref, cin_ref, carry, bu_scr, s_scr):
        @pl.when(pl.program_id(0) == 0)
        def _():
            carry[...] = jnp.zeros_like(carry)

        cin_ref[...] = carry[...]
        u = u_ref[...]
        for sb in range(N_SB):
            cols = slice(sb * 128, (sb + 1) * 128)
            u_sb = u[:, cols]
            bu_scr[sb] = _dot(u_sb.astype(MXU_DTYPE), bm_ref[sb])
            t_r, t_i = _scan_tiles(bu_scr.at[sb], s_scr.at[sb], tre_ref, tim_ref, sb,
                                   carry[2 * sb:2 * sb + 1, :], carry[2 * sb + 1:2 * sb + 2, :], False)
            carry[2 * sb:2 * sb + 1, :] = t_r
            carry[2 * sb + 1:2 * sb + 2, :] = t_i
            y_ref[:, cols] = _dot(s_scr[sb].astype(MXU_DTYPE), cm_ref[sb]) + d_ref[:, cols] * u_sb
        z, _ = _gelu_parts(y_ref[...])
        gl = _dot(z.astype(MXU_DTYPE), wg_ref[...]) + bg_ref[...]
        ys_ref[...] = (z * _sigmoid(gl)).astype(MXU_DTYPE)

    full = lambda shape: pl.BlockSpec(shape, lambda j: (0,) * len(shape))
    return _pcall(
        body, name=f"s5_fwd_l{layer}", grid=(n_chunks,),
        in_specs=[pl.BlockSpec((BLK, D_SSM), lambda j: (j, 0)),
                  full((N_SB, 128, 2 * SB_STATES)), full((N_SB, 2 * SB_STATES, 128)),
                  full((N_SB, 8, SCAN_TILE, SB_STATES)), full((N_SB, 8, SCAN_TILE, SB_STATES)),
                  full((1, D_SSM)), full((D_SSM, D_SSM)),
                  pl.BlockSpec((None, 1, D_SSM), lambda j: (layer, 0, 0))],
        out_specs=[pl.BlockSpec((BLK, D_SSM), lambda j: (j, 0)), pl.BlockSpec((BLK, D_SSM), lambda j: (j, 0)),
                   pl.BlockSpec((None, 8, SB_STATES), lambda j: (j, 0, 0))],
        out_shape=[SDS((rows, D_SSM), F32), SDS((rows, D_SSM), MXU_DTYPE), SDS((n_chunks, 8, SB_STATES), F32)],
        scratch_shapes=[pltpu.VMEM((8, SB_STATES), F32), pltpu.VMEM((N_SB, BLK, 2 * SB_STATES), F32),
                        pltpu.VMEM((N_SB, BLK, 2 * SB_STATES), F32)],
        compiler_params=_cparams("arbitrary"),
    )(u, b_mat, c_mat, t_re, t_im, d_skip, w_glu, b_glu3)


def _attn_mask(i):
    row = lax.broadcasted_iota(jnp.int32, (BLK, 3 * BLK), 0) + i * BLK
    col = lax.broadcasted_iota(jnp.int32, (BLK, 3 * BLK), 1)
    seg = jnp.right_shift(col, 7)
    c = jnp.bitwise_and(col, BLK - 1)
    kidx = c + (i + seg - 2) * BLK
    ok_meta = (seg == 0) & (c >= PAD_ROWS) & (row - c >= BLK)
    ok_win = (seg > 0) & (kidx >= PAD_ROWS) & (kidx <= row) & (row - kidx < BLK)
    return jnp.where(ok_meta | ok_win, 0.0, NEG_INF)


def _head_lanes(h):
    return slice(h * HEAD_DIM, (h + 1) * HEAD_DIM)


def _group_rows(ref, kvh):
    return jnp.concatenate([ref[:, _head_lanes(kvh * Q_PER_KV + g)] for g in range(Q_PER_KV)], axis=0)


def _group_bias(bias, sink_ref, layer, kvh):
    first_col = lax.broadcasted_iota(jnp.int32, (BLK, BLK), 1) == 0
    slabs = []
    for g in range(Q_PER_KV):
        first = jnp.where(first_col, sink_ref[layer, kvh * Q_PER_KV + g], bias[:, :BLK])
        slabs.append(jnp.concatenate([first, bias[:, BLK:]], axis=1))
    return jnp.concatenate(slabs, axis=0)


def _attn_probs(q4, k3, bias4):
    s = _dot_nt(q4, k3) + bias4
    e = jnp.exp(s - jnp.max(s, axis=-1, keepdims=True))
    return e * (1.0 / jnp.sum(e, axis=-1, keepdims=True))


def _attn_fwd(q, k, v, sinks, layer):
    rows = q.shape[0]
    n_blk = rows // BLK

    def body(sink_ref, q_ref, km_ref, kp_ref, kc_ref, vm_ref, vp_ref, vc_ref, o_ref):
        bias = _attn_mask(pl.program_id(0))
        for kvh in range(N_KV_HEADS):
            lanes = _head_lanes(kvh)
            k3 = jnp.concatenate([km_ref[:, lanes], kp_ref[:, lanes], kc_ref[:, lanes]], axis=0)
            v3 = jnp.concatenate([vm_ref[:, lanes], vp_ref[:, lanes], vc_ref[:, lanes]], axis=0)
            p = _attn_probs(_group_rows(q_ref, kvh), k3, _group_bias(bias, sink_ref, layer, kvh))
            o4 = _dot(p.astype(MXU_DTYPE), v3).astype(MXU_DTYPE)
            for g in range(Q_PER_KV):
                o_ref[:, _head_lanes(kvh * Q_PER_KV + g)] = o4[g * BLK:(g + 1) * BLK]

    kv_meta = pl.BlockSpec((BLK, D_KV), lambda i: (0, 0))
    kv_prev = pl.BlockSpec((BLK, D_KV), lambda i: (jnp.maximum(i - 1, 0), 0))
    kv_cur = pl.BlockSpec((BLK, D_KV), lambda i: (i, 0))
    return _pcall(
        body, name=f"attn_fwd_l{layer}", grid=(n_blk,),
        in_specs=[pl.BlockSpec(memory_space=pltpu.SMEM),
                  pl.BlockSpec((BLK, D_ATTN), lambda i: (i, 0)),
                  kv_meta, kv_prev, kv_cur, kv_meta, kv_prev, kv_cur],
        out_specs=pl.BlockSpec((BLK, D_ATTN), lambda i: (i, 0)),
        out_shape=SDS((rows, D_ATTN), MXU_DTYPE),
        compiler_params=_cparams("parallel"),
    )(sinks, q, k, k, k, v, v, v)


def _merge_fwd(y_ssm, y_attn, gates, hres, w_o_ssm, w_o_attn, w_out, gain3, layer):
    rows = hres.shape[0]
    tm = _row_tile(rows, 320)

    def body(ys_ref, ya_ref, gs_ref, ga_ref, x_ref, wos_ref, woa_ref, wout_ref, g_ref,
             mg_ref, mix_ref, out_ref):
        a1 = _dot(ys_ref[...], wos_ref[...])
        a2 = _dot(ya_ref[...], woa_ref[...])
        merged = (_sigmoid(gs_ref[...]) * a1 + _sigmoid(ga_ref[...]) * a2).astype(MXU_DTYPE)
        mg_ref[...] = merged
        mix = _dot(merged, wout_ref[...])
        mix_ref[...] = mix
        out_ref[...] = x_ref[...] + _rms_fwd(mix, g_ref[...])

    row_d = pl.BlockSpec((tm, D), lambda i: (i, 0))
    full = lambda shape: pl.BlockSpec(shape, lambda i: (0,) * len(shape))
    return _pcall(
        body, name=f"merge_fwd_l{layer}", grid=(rows // tm,),
        in_specs=[pl.BlockSpec((tm, D_SSM), lambda i: (i, 0)), row_d,
                  row_d, pl.BlockSpec((tm, D), lambda i: (i, 1)), row_d,
                  full((D_SSM, D)), full((D_ATTN, D)), full((D, D)),
                  pl.BlockSpec((None, 1, D), lambda i: (layer, 0, 0))],
        out_specs=[row_d, row_d, row_d],
        out_shape=[SDS((rows, D), MXU_DTYPE), SDS((rows, D), F32), SDS((rows, D), F32)],
        compiler_params=_cparams("parallel"),
    )(y_ssm, y_attn, gates, gates, hres, w_o_ssm, w_o_attn, w_out, gain3)


def _mlp_fwd(hres, gain_pre3, gain_post3, w_up_g, w_down_g, layer):
    rows = hres.shape[0]
    tm = _row_tile(rows)

    def body(x_ref, gp_ref, gq_ref, wu_hbm, wd_hbm, up_ref, h_ref, ff_ref, out_ref,
             h_scr, acc, wu_scr, wd_scr, wu_sem, wd_sem):
        kf = pl.program_id(1)
        first = (pl.program_id(0) == 0) & (kf == 0)
        _load_resident(wu_hbm, wu_scr, wu_sem, first)
        _load_resident(wd_hbm, wd_scr, wd_sem, first)

        @pl.when(kf == 0)
        def _():
            hn = _rms_fwd(x_ref[...], gp_ref[...]).astype(MXU_DTYPE)
            h_scr[...] = hn
            h_ref[...] = hn
            acc[...] = jnp.zeros_like(acc)

        up = _dot(h_scr[...], wu_scr[kf])
        up_ref[...] = up.astype(MXU_DTYPE)
        r = jnp.maximum(up, 0.0)
        acc[...] += _dot((r * r).astype(MXU_DTYPE), wd_scr[kf])

        @pl.when(kf == N_DEV - 1)
        def _():
            ff = acc[...]
            ff_ref[...] = ff
            out_ref[...] = x_ref[...] + _rms_fwd(ff, gq_ref[...])

    row_d = pl.BlockSpec((tm, D), lambda i, k: (i, 0))
    gain = pl.BlockSpec((None, 1, D), lambda i, k: (layer, 0, 0))
    return _pcall(
        body, name=f"mlp_fwd_l{layer}", grid=(rows // tm, N_DEV),
        in_specs=[row_d, gain, gain, pl.BlockSpec(memory_space=pl.ANY), pl.BlockSpec(memory_space=pl.ANY)],
        out_specs=[pl.BlockSpec((tm, COL_SHARD), lambda i, k: (i, k)), row_d, row_d, row_d],
        out_shape=[SDS((rows, D_FF), MXU_DTYPE), SDS((rows, D), MXU_DTYPE), SDS((rows, D), F32), SDS((rows, D), F32)],
        scratch_shapes=[pltpu.VMEM((tm, D), MXU_DTYPE), pltpu.VMEM((tm, D), F32),
                        pltpu.VMEM((N_DEV, D, COL_SHARD), MXU_DTYPE), pltpu.VMEM((N_DEV, COL_SHARD, D), MXU_DTYPE),
                        pltpu.SemaphoreType.DMA((N_DEV,)), pltpu.SemaphoreType.DMA((N_DEV,))],
        compiler_params=_cparams("arbitrary", "arbitrary"),
    )(hres, gain_pre3, gain_post3, w_up_g, w_down_g)


def _loss_and_grad(hres, target):
    rows = hres.shape[0]
    n_blk = rows // BLK

    def body(y_ref, t_ref, dy_ref, loss_ref):
        i = pl.program_id(0)

        @pl.when(i == 0)
        def _():
            dy_ref[...] = jnp.zeros_like(dy_ref)
            loss_ref[...] = jnp.zeros_like(loss_ref)

        @pl.when(i > 0)
        def _():
            err = y_ref[...] - t_ref[...]
            dy_ref[...] = err * (1.0 / D)
            loss_ref[...] += jnp.sum(err * err) * (0.5 / D)

    return _pcall(
        body, name="loss", grid=(n_blk,),
        in_specs=[pl.BlockSpec((BLK, D), lambda i: (i, 0)),
                  pl.BlockSpec((BLK, D), lambda i: (jnp.maximum(i - 1, 0), 0))],
        out_specs=[pl.BlockSpec((BLK, D), lambda i: (i, 0)), pl.BlockSpec((1, 128), lambda i: (0, 0))],
        out_shape=[SDS((rows, D), F32), SDS((1, 128), F32)],
        compiler_params=_cparams("arbitrary"),
    )(hres, target)


def _relu_squared(up):
    r = jnp.maximum(up.astype(F32), 0.0)
    return (r * r).astype(MXU_DTYPE)


def _matmul_tn(a, b, name, dev_major_cols=None, a_fn=None):
    rows, ka = a.shape
    n = b.shape[1]
    ta = min(ka, 1024)
    tn = 1024 if n % 1024 == 0 else 512
    tr = _row_tile(rows)
    n_r = rows // tr

    def body(a_ref, b_ref, o_ref, acc):
        r = pl.program_id(2)

        @pl.when(r == 0)
        def _():
            acc[...] = jnp.zeros_like(acc)

        a_blk = a_ref[...] if a_fn is None else a_fn(a_ref[...])
        acc[...] += _dot_tn(a_blk, b_ref[...])

        @pl.when(r == n_r - 1)
        def _():
            if dev_major_cols is None:
                o_ref[...] = acc[...].astype(XFER_DTYPE)
            else:
                for s in range(tn // dev_major_cols):
                    o_ref[s] = acc[:, s * dev_major_cols:(s + 1) * dev_major_cols].astype(XFER_DTYPE)

    if dev_major_cols is None:
        out_spec = pl.BlockSpec((ta, tn), lambda i, j, r: (i, j))
        out_shape = SDS((ka, n), XFER_DTYPE)
    else:
        w = dev_major_cols
        out_spec = pl.BlockSpec((tn // w, ta, w), lambda i, j, r: (j, i, 0))
        out_shape = SDS((n // w, ka, w), XFER_DTYPE)
    return _pcall(
        body, name=name, grid=(ka // ta, n // tn, n_r),
        in_specs=[pl.BlockSpec((tr, ta), lambda i, j, r: (r, i)), pl.BlockSpec((tr, tn), lambda i, j, r: (r, j))],
        out_specs=out_spec, out_shape=out_shape,
        scratch_shapes=[pltpu.VMEM((ta, tn), F32)],
        compiler_params=_cparams("parallel", "parallel", "arbitrary"),
    )(a, b)


def _dw_in(h, dproj_pieces, layer):
    rows = h.shape[0]
    tr = _row_tile(rows)
    n_r = rows // tr

    def body(h_ref, du_ref, dqkv_ref, dgs_ref, dga_ref, o_ref, acc):
        j = pl.program_id(0)
        r = pl.program_id(1)

        @pl.when(r == 0)
        def _():
            acc[...] = jnp.zeros_like(acc)

        for piece_ref, (first, count) in zip((du_ref, dqkv_ref, dgs_ref, dga_ref), DPROJ_PIECES):
            @pl.when((j >= first) & (j < first + count))
            def _():
                acc[...] += _dot_tn(h_ref[...], piece_ref[...])

        @pl.when(r == n_r - 1)
        def _():
            o_ref[...] = acc[...].astype(XFER_DTYPE)

    def piece_spec(first, count):
        def index(j, r):
            mine = (j >= first) & (j < first + count)
            return jnp.where(mine, r, 0), jnp.clip(j - first, 0, count - 1)
        return pl.BlockSpec((tr, COL_SHARD), index)

    return _pcall(
        body, name=f"dw_in_l{layer}", grid=(N_DEV, n_r),
        in_specs=[pl.BlockSpec((tr, D), lambda j, r: (r, 0))] + [piece_spec(*p) for p in DPROJ_PIECES],
        out_specs=pl.BlockSpec((None, D, COL_SHARD), lambda j, r: (j, 0, 0)),
        out_shape=SDS((N_DEV, D, COL_SHARD), XFER_DTYPE),
        scratch_shapes=[pltpu.VMEM((D, COL_SHARD), F32)],
        compiler_params=_cparams("arbitrary", "arbitrary"),
    )(h, *dproj_pieces)


def _mlp_bwd(dout, ff, up, hres_mid, gain_pre3, gain_post3, w_up_g, w_down_g, layer):
    rows = dout.shape[0]
    tm = _row_tile(rows)

    def body(do_ref, ff_ref, up_ref, x_ref, gp_ref, gq_ref, wu_hbm, wd_hbm,
             dff_ref, dup_ref, dx_ref, dgq_ref, dgp_ref, dff_scr, acc, wut_scr, wdt_scr, wu_stage, wd_stage,
             wu_sem, wd_sem):
        i = pl.program_id(0)
        kf = pl.program_id(1)
        _load_resident_transposed(wu_hbm, wut_scr, wu_stage, wu_sem, (i == 0) & (kf == 0))
        _load_resident_transposed(wd_hbm, wdt_scr, wd_stage, wd_sem, (i == 0) & (kf == 0))

        @pl.when((i == 0) & (kf == 0))
        def _():
            dgq_ref[...] = jnp.zeros_like(dgq_ref)
            dgp_ref[...] = jnp.zeros_like(dgp_ref)

        @pl.when(kf == 0)
        def _():
            dff, dg = _rms_bwd(ff_ref[...], gq_ref[...], do_ref[...])
            dgq_ref[...] += dg
            dffb = dff.astype(MXU_DTYPE)
            dff_scr[...] = dffb
            dff_ref[...] = dffb
            acc[...] = jnp.zeros_like(acc)

        dact = _dot(dff_scr[...], wdt_scr[kf])
        dup = (dact * (2.0 * jnp.maximum(up_ref[...].astype(F32), 0.0))).astype(MXU_DTYPE)
        dup_ref[...] = dup
        acc[...] += _dot(dup, wut_scr[kf])

        @pl.when(kf == N_DEV - 1)
        def _():
            dx, dg = _rms_bwd(x_ref[...], gp_ref[...], acc[...])
            dgp_ref[...] += dg
            dx_ref[...] = do_ref[...] + dx

    row_d = pl.BlockSpec((tm, D), lambda i, k: (i, 0))
    gain = pl.BlockSpec((None, 1, D), lambda i, k: (layer, 0, 0))
    dgain = pl.BlockSpec((1, D), lambda i, k: (0, 0))
    return _pcall(
        body, name=f"mlp_bwd_l{layer}", grid=(rows // tm, N_DEV),
        in_specs=[row_d, row_d, pl.BlockSpec((tm, COL_SHARD), lambda i, k: (i, k)), row_d, gain, gain,
                  pl.BlockSpec(memory_space=pl.ANY), pl.BlockSpec(memory_space=pl.ANY)],
        out_specs=[row_d, pl.BlockSpec((tm, COL_SHARD), lambda i, k: (i, k)), row_d, dgain, dgain],
        out_shape=[SDS((rows, D), MXU_DTYPE), SDS((rows, D_FF), MXU_DTYPE), SDS((rows, D), F32),
                   SDS((1, D), F32), SDS((1, D), F32)],
        scratch_shapes=[pltpu.VMEM((tm, D), MXU_DTYPE), pltpu.VMEM((tm, D), F32),
                        pltpu.VMEM((N_DEV, COL_SHARD, D), MXU_DTYPE), pltpu.VMEM((N_DEV, D, COL_SHARD), MXU_DTYPE),
                        pltpu.VMEM((2, D, COL_SHARD), MXU_DTYPE), pltpu.VMEM((2, COL_SHARD, D), MXU_DTYPE),
                        pltpu.SemaphoreType.DMA((2,)), pltpu.SemaphoreType.DMA((2,))],
        compiler_params=_cparams("arbitrary", "arbitrary"),
    )(dout, ff, up, hres_mid, gain_pre3, gain_post3, w_up_g, w_down_g)


def _merge_bwd(dhm, mix, y_ssm, y_attn, gates, w_o_ssm, w_o_attn, w_o_ssm_t, w_o_attn_t, w_out_t, gain3, layer):
    rows = dhm.shape[0]
    tm = _row_tile(rows, 320)

    def body(dh_ref, mix_ref, ys_ref, ya_ref, gs_ref, ga_ref, wos_ref, woa_ref, wost_ref, woat_ref, woutt_ref, g_ref,
             dmix_ref, da1_ref, da2_ref, dgs_ref, dga_ref, dys_ref, dya_ref, dg_ref):
        @pl.when(pl.program_id(0) == 0)
        def _():
            dg_ref[...] = jnp.zeros_like(dg_ref)

        dmix, dg = _rms_bwd(mix_ref[...], g_ref[...], dh_ref[...])
        dg_ref[...] += dg
        dmixb = dmix.astype(MXU_DTYPE)
        dmix_ref[...] = dmixb
        dmerged = _dot(dmixb, woutt_ref[...])
        sg_s = _sigmoid(gs_ref[...])
        sg_a = _sigmoid(ga_ref[...])
        da1 = (dmerged * sg_s).astype(MXU_DTYPE)
        da2 = (dmerged * sg_a).astype(MXU_DTYPE)
        da1_ref[...] = da1
        da2_ref[...] = da2
        a1 = _dot(ys_ref[...], wos_ref[...])
        a2 = _dot(ya_ref[...], woa_ref[...])
        dgs_ref[...] = (dmerged * a1 * (sg_s * (1.0 - sg_s))).astype(MXU_DTYPE)
        dga_ref[...] = (dmerged * a2 * (sg_a * (1.0 - sg_a))).astype(MXU_DTYPE)
        dys_ref[...] = _dot(da1, wost_ref[...])
        dya_ref[...] = _dot(da2, woat_ref[...])

    row_d = pl.BlockSpec((tm, D), lambda i: (i, 0))
    full = lambda shape: pl.BlockSpec(shape, lambda i: (0,) * len(shape))
    return _pcall(
        body, name=f"merge_bwd_l{layer}", grid=(rows // tm,),
        in_specs=[row_d, row_d, pl.BlockSpec((tm, D_SSM), lambda i: (i, 0)), row_d,
                  row_d, pl.BlockSpec((tm, D), lambda i: (i, 1)),
                  full((D_SSM, D)), full((D_ATTN, D)), full((D, D_SSM)), full((D, D_ATTN)), full((D, D)),
                  pl.BlockSpec((None, 1, D), lambda i: (layer, 0, 0))],
        out_specs=[row_d, row_d, row_d, row_d, row_d, pl.BlockSpec((tm, D_SSM), lambda i: (i, 0)), row_d,
                   pl.BlockSpec((1, D), lambda i: (0, 0))],
        out_shape=[SDS((rows, D), MXU_DTYPE)] * 5 + [SDS((rows, D_SSM), F32), SDS((rows, D_ATTN), F32),
                                                      SDS((1, D), F32)],
        compiler_params=_cparams("arbitrary"),
    )(dhm, mix, y_ssm, y_attn, gates, gates, w_o_ssm, w_o_attn, w_o_ssm_t, w_o_attn_t, w_out_t, gain3)


def _attn_bwd(q, k, v, d_out, sinks, layer):
    rows = q.shape[0]
    n_blk = rows // BLK
    last = n_blk - 1

    def body(sink_ref, q_ref, km_ref, kp_ref, kc_ref, vm_ref, vp_ref, vc_ref, do_ref,
             dq_ref, dk_ref, dv_ref, dkm_ref, dvm_ref, ds_ref, dk_carry, dv_carry):
        i = pl.program_id(0)

        @pl.when(i == 0)
        def _():
            dkm_ref[...] = jnp.zeros_like(dkm_ref)
            dvm_ref[...] = jnp.zeros_like(dvm_ref)
            ds_ref[...] = jnp.zeros_like(ds_ref)
            dk_carry[...] = jnp.zeros_like(dk_carry)
            dv_carry[...] = jnp.zeros_like(dv_carry)

        @pl.when(i <= last)
        def _():
            bias = _attn_mask(i)
            for kvh in range(N_KV_HEADS):
                lanes = _head_lanes(kvh)
                k3 = jnp.concatenate([km_ref[:, lanes], kp_ref[:, lanes], kc_ref[:, lanes]], axis=0)
                v3 = jnp.concatenate([vm_ref[:, lanes], vp_ref[:, lanes], vc_ref[:, lanes]], axis=0)
                q4 = _group_rows(q_ref, kvh)
                do4 = _group_rows(do_ref, kvh).astype(MXU_DTYPE)
                p = _attn_probs(q4, k3, _group_bias(bias, sink_ref, layer, kvh))
                dp = _dot_nt(do4, v3)
                dsf = p * (dp - jnp.sum(dp * p, axis=-1, keepdims=True))
                dsc = dsf.astype(MXU_DTYPE)
                dv3 = _dot_tn(p.astype(MXU_DTYPE), do4)
                dk3 = _dot_tn(dsc, q4)
                dq4 = _dot(dsc, k3)
                for g in range(Q_PER_KV):
                    h = kvh * Q_PER_KV + g
                    dq_ref[:, _head_lanes(h)] = dq4[g * BLK:(g + 1) * BLK]
                    ds_ref[h:h + 1, :] += jnp.sum(dsf[g * BLK:(g + 1) * BLK, 0:BLK], axis=0, keepdims=True)
                dkm_ref[:, lanes] += dk3[0:BLK]
                dvm_ref[:, lanes] += dv3[0:BLK]
                dk_ref[:, lanes] = dk_carry[:, lanes] + dk3[BLK:2 * BLK]
                dv_ref[:, lanes] = dv_carry[:, lanes] + dv3[BLK:2 * BLK]
                dk_carry[:, lanes] = dk3[2 * BLK:3 * BLK]
                dv_carry[:, lanes] = dv3[2 * BLK:3 * BLK]

        @pl.when(i == last + 1)
        def _():
            dk_ref[...] = dk_carry[...]
            dv_ref[...] = dv_carry[...]

    cur = lambda i: (jnp.minimum(i, last), 0)
    prev = lambda i: (jnp.clip(i - 1, 0, last), 0)
    kv_meta = pl.BlockSpec((BLK, D_KV), lambda i: (0, 0))
    kv_prev = pl.BlockSpec((BLK, D_KV), prev)
    kv_cur = pl.BlockSpec((BLK, D_KV), cur)
    return _pcall(
        body, name=f"attn_bwd_l{layer}", grid=(n_blk + 1,),
        in_specs=[pl.BlockSpec(memory_space=pltpu.SMEM),
                  pl.BlockSpec((BLK, D_ATTN), cur),
                  kv_meta, kv_prev, kv_cur, kv_meta, kv_prev, kv_cur,
                  pl.BlockSpec((BLK, D_ATTN), cur)],
        out_specs=[pl.BlockSpec((BLK, D_ATTN), cur), kv_prev, kv_prev, kv_meta, kv_meta,
                   pl.BlockSpec((N_Q_HEADS, 128), lambda i: (0, 0))],
        out_shape=[SDS((rows, D_ATTN), F32), SDS((rows, D_KV), F32), SDS((rows, D_KV), F32),
                   SDS((BLK, D_KV), F32), SDS((BLK, D_KV), F32), SDS((N_Q_HEADS, 128), F32)],
        scratch_shapes=[pltpu.VMEM((BLK, D_KV), F32), pltpu.VMEM((BLK, D_KV), F32)],
        compiler_params=_cparams("arbitrary"),
    )(sinks, q, k, k, k, v, v, v, d_out)


def _rope_bwd(dq, dk, dv, dk_meta, dv_meta, cos, sin_a, sin_b, layer):
    rows = dq.shape[0]
    tm = _row_tile(rows)

    def body(dq_ref, dk_ref, dv_ref, dkm_ref, dvm_ref, c_ref, a_ref, b_ref, o_ref):
        c, a, b = c_ref[...], -a_ref[...], -b_ref[...]
        for t in range(8):
            x = dq_ref[:, t * 128:(t + 1) * 128]
            o_ref[:, t * 128:(t + 1) * 128] = (_rope_lanes(x, c, a, b) * ATTN_SCALE).astype(MXU_DTYPE)
        for t in range(2):
            x = dk_ref[:, t * 128:(t + 1) * 128]
            o_ref[:, D_ATTN + t * 128:D_ATTN + (t + 1) * 128] = _rope_lanes(x, c, a, b).astype(MXU_DTYPE)
        o_ref[:, D_ATTN + D_KV:] = dv_ref[...].astype(MXU_DTYPE)

        @pl.when(pl.program_id(0) == 0)
        def _():
            cb, ab, bb = c[0:BLK], a[0:BLK], b[0:BLK]
            is_meta = lax.broadcasted_iota(jnp.int32, (BLK, 128), 0) >= PAD_ROWS
            for t in range(2):
                x = dk_ref[0:BLK, t * 128:(t + 1) * 128] + jnp.where(is_meta, dkm_ref[:, t * 128:(t + 1) * 128], 0.0)
                o_ref[0:BLK, D_ATTN + t * 128:D_ATTN + (t + 1) * 128] = _rope_lanes(x, cb, ab, bb).astype(MXU_DTYPE)
                xv = dv_ref[0:BLK, t * 128:(t + 1) * 128] + jnp.where(is_meta, dvm_ref[:, t * 128:(t + 1) * 128], 0.0)
                o_ref[0:BLK, D_ATTN + D_KV + t * 128:D_ATTN + D_KV + (t + 1) * 128] = xv.astype(MXU_DTYPE)

    tab = pl.BlockSpec((tm, 128), lambda i: (i, 0))
    kv = pl.BlockSpec((tm, D_KV), lambda i: (i, 0))
    meta = pl.BlockSpec((BLK, D_KV), lambda i: (0, 0))
    return _pcall(
        body, name=f"rope_bwd_l{layer}", grid=(rows // tm,),
        in_specs=[pl.BlockSpec((tm, D_ATTN), lambda i: (i, 0)), kv, kv, meta, meta, tab, tab, tab],
        out_specs=pl.BlockSpec((tm, D_ATTN + 2 * D_KV), lambda i: (i, 0)),
        out_shape=SDS((rows, D_ATTN + 2 * D_KV), MXU_DTYPE),
        compiler_params=_cparams("parallel"),
    )(dq, dk, dv, dk_meta, dv_meta, cos, sin_a, sin_b)


def _s5_bwd(d_gated, y, u, carry_in, ssm, w_glu, b_glu3, layer):
    rows = y.shape[0]
    n_chunks = rows // BLK
    b_mat, c_mat, t_re, t_im, d_skip = (ssm[k] for k in ("b_mat", "c_mat", "t_re", "t_im", "d_skip"))

    def body(dz_ref, y_ref, u_ref, cin_ref, bm_ref, cm_ref, tre_ref, tim_ref, d_ref, wg_ref, bg_ref,
             du_ref, dwg_ref, dbg_ref, dd_ref, dbm_ref, dcm_ref, dab_ref,
             lam_carry, bu_scr, s_scr, sp_scr, g_scr, lam_scr):
        step = pl.program_id(0)
        chunk = n_chunks - 1 - step

        @pl.when(step == 0)
        def _():
            for r in (dwg_ref, dbg_ref, dd_ref, dbm_ref, dcm_ref, dab_ref, lam_carry):
                r[...] = jnp.zeros_like(r)

        y = y_ref[...]
        u = u_ref[...]
        d_o = dz_ref[...]
        z, t = _gelu_parts(y)
        zb = z.astype(MXU_DTYPE)
        sg = _sigmoid(_dot(zb, wg_ref[...]) + bg_ref[...])
        dgl = d_o * z * (sg * (1.0 - sg))
        dglb = dgl.astype(MXU_DTYPE)
        dz = d_o * sg + _dot_nt(dglb, wg_ref[...])
        dwg_ref[...] += _dot_tn(zb, dglb)
        dbg_ref[...] += jnp.sum(dgl, axis=0, keepdims=True)
        dy = dz * _gelu_grad(y, t)
        dd_ref[...] += jnp.sum(dy * u, axis=0, keepdims=True)
        grow = lax.broadcasted_iota(jnp.int32, (BLK, 128), 0) + chunk * BLK
        for sb in range(N_SB):
            cols = slice(sb * 128, (sb + 1) * 128)
            u_sb = u[:, cols].astype(MXU_DTYPE)
            dy_sb = dy[:, cols]
            dyb = dy_sb.astype(MXU_DTYPE)
            bu_scr[sb] = _dot(u_sb, bm_ref[sb])
            _scan_tiles(bu_scr.at[sb], s_scr.at[sb], tre_ref, tim_ref, sb,
                        cin_ref[2 * sb:2 * sb + 1, :], cin_ref[2 * sb + 1:2 * sb + 2, :], False, prev_ref=sp_scr.at[sb])
            dcm_ref[sb] += _dot_tn(s_scr[sb].astype(MXU_DTYPE), dyb)
            g_scr[sb] = _dot_nt(dyb, cm_ref[sb])
            n_r, n_i = _scan_tiles(g_scr.at[sb], lam_scr.at[sb], tre_ref, tim_ref, sb,
                                   lam_carry[2 * sb:2 * sb + 1, :], lam_carry[2 * sb + 1:2 * sb + 2, :], True)
            lam_carry[2 * sb:2 * sb + 1, :] = n_r
            lam_carry[2 * sb + 1:2 * sb + 2, :] = n_i
            lr, li = lam_scr[sb, :, :SB_STATES], lam_scr[sb, :, SB_STATES:]
            spr, spi = sp_scr[sb, :, :SB_STATES], sp_scr[sb, :, SB_STATES:]
            dab_ref[2 * sb:2 * sb + 1, :] += jnp.sum(spr * lr + spi * li, axis=0, keepdims=True)
            dab_ref[2 * sb + 1:2 * sb + 2, :] += jnp.sum(spr * li - spi * lr, axis=0, keepdims=True)
            lam = lam_scr[sb].astype(MXU_DTYPE)
            dbm_ref[sb] += _dot_tn(u_sb, lam)
            du = _dot_nt(lam, bm_ref[sb]) + d_ref[:, cols] * dy_sb
            du_ref[:, cols] = jnp.where(grow >= PAD_ROWS, du, 0.0).astype(MXU_DTYPE)

    rev = lambda j: (n_chunks - 1 - j, 0)
    full = lambda shape: pl.BlockSpec(shape, lambda j: (0,) * len(shape))
    tables = [full((N_SB, 8, SCAN_TILE, SB_STATES))] * 2
    chunk_scratch = pltpu.VMEM((N_SB, BLK, 2 * SB_STATES), F32)
    return _pcall(
        body, name=f"s5_bwd_l{layer}", grid=(n_chunks,),
        in_specs=[pl.BlockSpec((BLK, D_SSM), rev), pl.BlockSpec((BLK, D_SSM), rev), pl.BlockSpec((BLK, D_SSM), rev),
                  pl.BlockSpec((None, 8, SB_STATES), lambda j: (n_chunks - 1 - j, 0, 0)),
                  full((N_SB, 128, 2 * SB_STATES)), full((N_SB, 2 * SB_STATES, 128))] + tables + [
                  full((1, D_SSM)), full((D_SSM, D_SSM)),
                  pl.BlockSpec((None, 1, D_SSM), lambda j: (layer, 0, 0))],
        out_specs=[pl.BlockSpec((BLK, D_SSM), rev), full((D_SSM, D_SSM)), full((1, D_SSM)), full((1, D_SSM)),
                   full((N_SB, 128, 2 * SB_STATES)), full((N_SB, 2 * SB_STATES, 128)), full((8, SB_STATES))],
        out_shape=[SDS((rows, D_SSM), MXU_DTYPE), SDS((D_SSM, D_SSM), F32), SDS((1, D_SSM), F32), SDS((1, D_SSM), F32),
                   SDS((N_SB, 128, 2 * SB_STATES), F32), SDS((N_SB, 2 * SB_STATES, 128), F32), SDS((8, SB_STATES), F32)],
        scratch_shapes=[pltpu.VMEM((8, SB_STATES), F32)] + [chunk_scratch] * 5,
        compiler_params=_cparams("arbitrary"),
    )(d_gated, y, u, carry_in, b_mat, c_mat, t_re, t_im, d_skip, w_glu, b_glu3)


DPROJ_PIECES = ((0, 1), (1, 3), (4, 2), (6, 2))


def _in_bwd(dproj_pieces, dhm, hres, gain3, w_in_g, layer):
    rows = hres.shape[0]
    tm = _row_tile(rows)

    def body(du_ref, dqkv_ref, dgs_ref, dga_ref, dh_ref, x_ref, g_ref, w_hbm, dx_ref, dg_ref,
             acc, wt_scr, w_stage, w_sem):
        i = pl.program_id(0)
        j = pl.program_id(1)
        _load_resident_transposed(w_hbm, wt_scr, w_stage, w_sem, (i == 0) & (j == 0))

        @pl.when((i == 0) & (j == 0))
        def _():
            dg_ref[...] = jnp.zeros_like(dg_ref)

        @pl.when(j == 0)
        def _():
            acc[...] = jnp.zeros_like(acc)

        for piece_ref, (first, count) in zip((du_ref, dqkv_ref, dgs_ref, dga_ref), DPROJ_PIECES):
            @pl.when((j >= first) & (j < first + count))
            def _():
                acc[...] += _dot(piece_ref[...], wt_scr[j])

        @pl.when(j == N_DEV - 1)
        def _():
            dx, dg = _rms_bwd(x_ref[...], g_ref[...], acc[...])
            dg_ref[...] += dg
            dx_ref[...] = dh_ref[...] + dx

    row_d = pl.BlockSpec((tm, D), lambda i, j: (i, 0))

    def piece_spec(first, count):
        return pl.BlockSpec((tm, COL_SHARD), lambda i, j: (i, jnp.clip(j - first, 0, count - 1)))

    return _pcall(
        body, name=f"in_bwd_l{layer}", grid=(rows // tm, N_DEV),
        in_specs=[piece_spec(*p) for p in DPROJ_PIECES] + [
                  row_d, row_d,
                  pl.BlockSpec((None, 1, D), lambda i, j: (layer, 0, 0)),
                  pl.BlockSpec(memory_space=pl.ANY)],
        out_specs=[row_d, pl.BlockSpec((1, D), lambda i, j: (0, 0))],
        out_shape=[SDS((rows, D), F32), SDS((1, D), F32)],
        scratch_shapes=[pltpu.VMEM((tm, D), F32), pltpu.VMEM((N_DEV, COL_SHARD, D), MXU_DTYPE),
                        pltpu.VMEM((2, D, COL_SHARD), MXU_DTYPE), pltpu.SemaphoreType.DMA((2,))],
        compiler_params=_cparams("arbitrary", "arbitrary"),
    )(*dproj_pieces, dhm, hres, gain3, w_in_g)


_ADAM_C1 = 1.0 / (1.0 - ADAM_B1 ** ADAM_STEP)
_ADAM_C2 = 1.0 / (1.0 - ADAM_B2 ** ADAM_STEP)


def _adam_math(w, g, m, v):
    m = ADAM_B1 * m + (1.0 - ADAM_B1) * g
    v = ADAM_B2 * v + (1.0 - ADAM_B2) * (g * g)
    delta = -ADAM_LR * ((m * _ADAM_C1) / (jnp.sqrt(v * _ADAM_C2) + ADAM_EPS) + ADAM_WD * w)
    return delta, m, v


def _adamw_layers(parts0, parts1, w, m, v, name):
    _, rows, cols = w.shape
    tr = min(rows, (1 << 16) // cols)
    nt = rows // tr

    def body(p0_ref, p1_ref, w_ref, m_ref, v_ref, g_ref, d_ref, nm_ref, nv_ref):
        layer = pl.program_id(0)

        def run(p_ref):
            g = p_ref[0].astype(F32)
            for s in range(1, N_DEV):
                g = g + p_ref[s].astype(F32)
            delta, nm, nv = _adam_math(w_ref[...], g, m_ref[...], v_ref[...])
            g_ref[...] = g
            d_ref[...] = delta
            nm_ref[...] = nm
            nv_ref[...] = nv

        @pl.when(layer == 0)
        def _():
            run(p0_ref)

        @pl.when(layer == 1)
        def _():
            run(p1_ref)

    wspec = pl.BlockSpec((None, tr, cols), lambda l, i: (l, i, 0))
    return _pcall(
        body, name=name, grid=(2, nt),
        in_specs=[pl.BlockSpec((N_DEV, tr, cols), lambda l, i: (0, jnp.where(l == 0, i, nt - 1), 0)),
                  pl.BlockSpec((N_DEV, tr, cols), lambda l, i: (0, jnp.where(l == 1, i, 0), 0)),
                  wspec, wspec, wspec],
        out_specs=[wspec] * 4, out_shape=[SDS(w.shape, F32)] * 4,
        compiler_params=_cparams("arbitrary", "arbitrary"),
    )(parts0, parts1, w, m, v)


def _sum_slots(parts, name):
    def body(p_ref, o_ref):
        acc = p_ref[0]
        for s in range(1, N_DEV):
            acc = acc + p_ref[s]
        o_ref[...] = acc

    vmem = pl.BlockSpec(memory_space=pltpu.VMEM)
    return _pcall(body, name=name, out_shape=SDS(parts.shape[1:], F32), in_specs=[vmem], out_specs=vmem,
                  compiler_params=_cparams())(parts)


def _adamw_packed(g, w, m, v, name):
    def body(g_ref, w_ref, m_ref, v_ref, d_ref, nm_ref, nv_ref):
        delta, nm, nv = _adam_math(w_ref[...], g_ref[...], m_ref[...], v_ref[...])
        d_ref[...] = delta
        nm_ref[...] = nm
        nv_ref[...] = nv

    vmem = pl.BlockSpec(memory_space=pltpu.VMEM)
    return _pcall(body, name=name, out_shape=[SDS(g.shape, F32)] * 3, in_specs=[vmem] * 4, out_specs=[vmem] * 3,
                  compiler_params=_cparams())(g, w, m, v)


def _ssm_discretize(a_re, a_im, log_dt, b_re, b_im):
    dt = jnp.exp(log_dt)[:, None]
    mag = jnp.exp(a_re * dt)
    ang = a_im * dt
    ab_re, ab_im = mag * jnp.cos(ang), mag * jnp.sin(ang)
    xr, xi = ab_re - 1.0, ab_im
    den = a_re * a_re + a_im * a_im
    q_re = (xr * a_re + xi * a_im) / den
    q_im = (xi * a_re - xr * a_im) / den
    bb_re = q_re[..., None] * b_re - q_im[..., None] * b_im
    bb_im = q_re[..., None] * b_im + q_im[..., None] * b_re
    return ab_re, ab_im, bb_re, bb_im


def _block_diag_b(bb):
    m = jnp.einsum("sgnc,gh->sgchn", bb.reshape(N_SB, 8, N_STATE, GROUP_CH), jnp.eye(8, dtype=F32))
    return m.reshape(N_SB, 128, SB_STATES)


def _block_diag_b_t(dm):
    return jnp.einsum("sgchn,gh->sgnc", dm.reshape(N_SB, 8, GROUP_CH, 8, N_STATE),
                      jnp.eye(8, dtype=F32)).reshape(N_GROUPS, N_STATE, GROUP_CH)


def _block_diag_c(cc):
    m = jnp.einsum("sgcn,gh->sgnhc", cc.reshape(N_SB, 8, GROUP_CH, N_STATE), jnp.eye(8, dtype=F32))
    return m.reshape(N_SB, SB_STATES, 128)


def _block_diag_c_t(dm):
    return jnp.einsum("sgnhc,gh->sgcn", dm.reshape(N_SB, 8, N_STATE, 8, GROUP_CH),
                      jnp.eye(8, dtype=F32)).reshape(N_GROUPS, GROUP_CH, N_STATE)


def _ssm_tables(ab_re, ab_im, bb_re, bb_im, c_re, c_im, d_skip):
    pr, pi = ab_re.reshape(1, -1), ab_im.reshape(1, -1)
    cr, ci = pr, pi
    squares = []
    for _ in range(3):
        squares.append((cr, ci))
        pr, pi = (jnp.concatenate([pr, pr * cr - pi * ci], axis=0),
                  jnp.concatenate([pi, pr * ci + pi * cr], axis=0))
        cr, ci = cr * cr - ci * ci, 2.0 * cr * ci
    r = jnp.arange(SCAN_TILE)[:, None]
    fwd = [(jnp.where(r >= (1 << k), squares[k][0], 0.0), jnp.where(r >= (1 << k), squares[k][1], 0.0))
           for k in range(3)] + [(pr, pi)]
    rev = [(jnp.where(r < SCAN_TILE - (1 << k), squares[k][0], 0.0),
            jnp.where(r < SCAN_TILE - (1 << k), -squares[k][1], 0.0)) for k in range(3)] + [(pr[::-1], -pi[::-1])]
    table = lambda part: jnp.stack([e[part] for e in fwd + rev]).reshape(
        8, SCAN_TILE, N_SB, SB_STATES).transpose(2, 0, 1, 3)
    return dict(
        b_mat=jnp.concatenate([_block_diag_b(bb_re), _block_diag_b(bb_im)], axis=-1).astype(MXU_DTYPE),
        c_mat=jnp.concatenate([_block_diag_c(c_re), -_block_diag_c(c_im)], axis=1).astype(MXU_DTYPE),
        t_re=table(0), t_im=table(1),
        d_skip=d_skip.reshape(1, D_SSM))


def _rope_tables(rows):
    pos = (jnp.arange(rows, dtype=jnp.int32) - PAD_ROWS).astype(F32)
    inv_freq = 1.0 / (ROPE_THETA ** (jnp.arange(0, HEAD_DIM, 2, dtype=F32) / HEAD_DIM))
    ang = pos[:, None] * inv_freq[None, :]
    ang = jnp.concatenate([ang, ang, ang, ang], axis=-1)
    first_half = (jnp.arange(128) % HEAD_DIM) < HEAD_DIM // 2
    sin = jnp.sin(ang)
    return jnp.cos(ang), jnp.where(first_half, -sin, 0.0), jnp.where(first_half, 0.0, sin)


def _pack(arrays):
    flat = jnp.concatenate([a.reshape(-1).astype(F32) for a in arrays])
    pad = (-flat.shape[0]) % 1024
    return jnp.pad(flat, (0, pad)).reshape(-1, 128)


def _unpack(packed, like):
    flat = packed.reshape(-1)
    out, off = [], 0
    for a in like:
        n = math.prod(a.shape)
        out.append(flat[off:off + n].reshape(a.shape))
        off += n
    return out


BIG = ("w_in", "w_glu", "w_o_ssm", "w_o_attn", "w_out", "w_up", "w_down")
WEIGHTS = ("meta_tokens", "norm_mix_pre", "norm_mix_post", "norm_mlp_pre", "norm_mlp_post", "w_in",
           "ssm_a_re", "ssm_a_im", "ssm_log_dt", "ssm_b_re", "ssm_b_im", "ssm_c_re", "ssm_c_im", "ssm_d",
           "w_glu", "b_glu", "attn_sinks", "w_o_ssm", "w_o_attn", "w_out", "w_up", "w_down")
SMALL = tuple(n for n in WEIGHTS if n not in BIG)


def kernel(x, meta_tokens, norm_mix_pre, norm_mix_post, norm_mlp_pre, norm_mlp_post, w_in, ssm_a_re, ssm_a_im, ssm_log_dt, ssm_b_re, ssm_b_im, ssm_c_re, ssm_c_im, ssm_d, w_glu, b_glu, attn_sinks, w_o_ssm, w_o_attn, w_out, w_up, w_down, loss_target, m_meta_tokens, m_norm_mix_pre, m_norm_mix_post, m_norm_mlp_pre, m_norm_mlp_post, m_w_in, m_ssm_a_re, m_ssm_a_im, m_ssm_log_dt, m_ssm_b_re, m_ssm_b_im, m_ssm_c_re, m_ssm_c_im, m_ssm_d, m_w_glu, m_b_glu, m_attn_sinks, m_w_o_ssm, m_w_o_attn, m_w_out, m_w_up, m_w_down, v_meta_tokens, v_norm_mix_pre, v_norm_mix_post, v_norm_mlp_pre, v_norm_mlp_post, v_w_in, v_ssm_a_re, v_ssm_a_im, v_ssm_log_dt, v_ssm_b_re, v_ssm_b_im, v_ssm_c_re, v_ssm_c_im, v_ssm_d, v_w_glu, v_b_glu, v_attn_sinks, v_w_o_ssm, v_w_o_attn, v_w_out, v_w_up, v_w_down):
    args = locals()
    w = {n: args[n] for n in WEIGHTS}
    m = {n: args["m_" + n] for n in WEIGHTS}
    v = {n: args["v_" + n] for n in WEIGHTS}
    n_layers = w_in.shape[0]
    seq = x.shape[1]
    rows = seq + BLK
    my_slot = _slot(_mesh_pos())

    assert n_layers == 2
    xfer = {n: [w[n][l].astype(XFER_DTYPE) for l in range(n_layers)] for n in BIG}
    mixer_small = ("w_glu", "w_o_ssm", "w_o_attn", "w_out")
    meta_g, w_in_g0 = _exchange_by_sequencer([meta_tokens, xfer["w_in"][0]], True, 0, "gather_in0")
    mix0_g = _exchange_by_sequencer([xfer[n][0] for n in mixer_small], True, 1, "gather_mix0", after=[meta_g])
    meta_full = meta_g.transpose(1, 0, 2).reshape(N_META, D)

    def mixer_weights(w_glu_g, w_o_ssm_g, w_o_attn_g, w_out_g):
        return dict(w_glu=w_glu_g.reshape(D_SSM, D_SSM), w_o_ssm=w_o_ssm_g.transpose(1, 0, 2).reshape(D_SSM, D),
                    w_o_attn=w_o_attn_g.reshape(D_ATTN, D), w_out=w_out_g.reshape(D, D),
                    w_o_ssm_t=w_o_ssm_g.transpose(0, 2, 1).reshape(D, D_SSM),
                    w_o_attn_t=w_o_attn_g.reshape(D_ATTN, D).T, w_out_t=w_out_g.reshape(D, D).T)

    gathered = [dict(w_in=w_in_g0, **mixer_weights(*mix0_g)), {}]

    gains = {n: w[n].reshape(n_layers, 1, D) for n in ("norm_mix_pre", "norm_mix_post", "norm_mlp_pre", "norm_mlp_post")}
    b_glu3 = b_glu.reshape(n_layers, 1, D_SSM)
    cos, sin_a, sin_b = _rope_tables(rows)

    def ssm_setup(l):
        disc, disc_vjp = jax.vjp(_ssm_discretize, ssm_a_re[l], ssm_a_im[l], ssm_log_dt[l], ssm_b_re[l], ssm_b_im[l])
        return _ssm_tables(*disc, ssm_c_re[l], ssm_c_im[l], ssm_d[l]), disc_vjp

    hres = jnp.concatenate([jnp.zeros((PAD_ROWS, D), F32), meta_full, x[0]], axis=0)

    saved = []
    for l in range(n_layers):
        ssm, disc_vjp = ssm_setup(l)
        wl = gathered[l]
        u, gates, q, k, vv, h = _in_proj(hres, gains["norm_mix_pre"], wl["w_in"], cos, sin_a, sin_b, l)
        if l == 0:
            wl["w_up"], wl["w_down"] = _exchange_by_sequencer([xfer["w_up"][0], xfer["w_down"][0]], True, 2,
                                                              "gather_mlp0", after=[h])
        y, y_ssm, carry_in = _s5_fwd(u, ssm, wl["w_glu"], b_glu3, l)
        if l == 0:
            l1_g = _exchange_by_sequencer([xfer[n][1] for n in ("w_in",) + mixer_small + ("w_up", "w_down")], True, 3,
                                          "gather_l1", after=[y])
            gathered[1] = dict(w_in=l1_g[0], w_up=l1_g[5], w_down=l1_g[6], **mixer_weights(*l1_g[1:5]))
            last_exchange = l1_g[:1]
        y_attn = _attn_fwd(q, k, vv, attn_sinks, l)
        merged, mix, hres_mid = _merge_fwd(y_ssm, y_attn, gates, hres, wl["w_o_ssm"], wl["w_o_attn"], wl["w_out"],
                                           gains["norm_mix_post"], l)
        up, h2, ff, hres_out = _mlp_fwd(hres_mid, gains["norm_mlp_pre"], gains["norm_mlp_post"], wl["w_up"],
                                        wl["w_down"], l)
        saved.append(dict(ssm=ssm, disc_vjp=disc_vjp, hres=hres, u=u, gates=gates, h=h, q=q, k=k, v=vv, y=y, y_ssm=y_ssm,
                          carry_in=carry_in, y_attn=y_attn, merged=merged, mix=mix, hres_mid=hres_mid,
                          up=up, h2=h2, ff=ff))
        hres = hres_out

    dhres, loss_vec = _loss_and_grad(hres, loss_target[0])

    small_grads = {n: [None] * n_layers for n in SMALL if n != "meta_tokens"}
    recv_up, recv_down, recv_mix = [None] * n_layers, [None] * n_layers, [None] * n_layers
    for l in reversed(range(n_layers)):
        s = saved[l]
        wl = gathered[l]
        dff, dup, dhm, dg_mlp_post, dg_mlp_pre = _mlp_bwd(dhres, s["ff"], s["up"], s["hres_mid"], gains["norm_mlp_pre"],
                                                          gains["norm_mlp_post"], wl["w_up"], wl["w_down"], l)
        dw_up = _matmul_tn(s["h2"], dup, f"dw_up_l{l}", dev_major_cols=COL_SHARD)
        recv_up[l] = _exchange_by_sequencer([dw_up], False, 4 + 3 * l, f"scatter_up{l}", after=last_exchange)
        dw_down = _matmul_tn(s["up"], dff, f"dw_down_l{l}", a_fn=_relu_squared).reshape(N_DEV, COL_SHARD, D)
        recv_down[l] = _exchange_by_sequencer([dw_down], False, 5 + 3 * l, f"scatter_down{l}", after=recv_up[l])
        last_exchange = recv_down[l]
        dmix, da1, da2, dgs, dga, dy_ssm, dy_attn, dg_mix_post = _merge_bwd(
            dhm, s["mix"], s["y_ssm"], s["y_attn"], s["gates"], wl["w_o_ssm"], wl["w_o_attn"], wl["w_o_ssm_t"],
            wl["w_o_attn_t"], wl["w_out_t"], gains["norm_mix_post"], l)
        dw_out = _matmul_tn(s["merged"], dmix, f"dw_out_l{l}").reshape(N_DEV, D // N_DEV, D)
        dw_o_attn = _matmul_tn(s["y_attn"], da2, f"dw_o_attn_l{l}").reshape(N_DEV, D_ATTN // N_DEV, D)
        dw_o_ssm = _matmul_tn(s["y_ssm"], da1, f"dw_o_ssm_l{l}", dev_major_cols=D // N_DEV)
        if l == 0:
            recv_out0 = _exchange_by_sequencer([dw_o_ssm, dw_o_attn, dw_out], False, 11, "scatter_out0",
                                               after=last_exchange)
            last_exchange = recv_out0[:1]
        dq, dk, dv, dk_meta, dv_meta, dsink = _attn_bwd(s["q"], s["k"], s["v"], dy_attn, attn_sinks, l)
        dqkv = _rope_bwd(dq, dk, dv, dk_meta, dv_meta, cos, sin_a, sin_b, l)
        du, dw_glu, db_glu, dd_skip, db_mat, dc_mat, dab = _s5_bwd(dy_ssm, s["y"], s["u"], s["carry_in"], s["ssm"],
                                                                    wl["w_glu"], b_glu3, l)
        dproj = (du, dqkv, dgs, dga)
        dw_in = _dw_in(s["h"], dproj, l)
        dhres, dg_mix_pre = _in_bwd(dproj, dhm, s["hres"], gains["norm_mix_pre"], wl["w_in"], l)
        mix_parts = [dw_in, dw_glu.astype(XFER_DTYPE).reshape(N_DEV, D_SSM // N_DEV, D_SSM), dw_o_ssm, dw_o_attn, dw_out]
        if l > 0:
            recv_mix[l] = _exchange_by_sequencer(mix_parts, False, 6 + 3 * l, f"scatter_mix{l}", after=last_exchange)
            last_exchange = recv_mix[l][:1]

        dab = dab.reshape(N_SB, 2, SB_STATES)
        da_re, da_im, dlog_dt, db_re, db_im = s["disc_vjp"]((
            dab[:, 0].reshape(N_GROUPS, N_STATE), dab[:, 1].reshape(N_GROUPS, N_STATE),
            _block_diag_b_t(db_mat[..., :SB_STATES]), _block_diag_b_t(db_mat[..., SB_STATES:])))
        for name, val in (("norm_mix_pre", dg_mix_pre[0]), ("norm_mix_post", dg_mix_post[0]),
                          ("norm_mlp_pre", dg_mlp_pre[0]), ("norm_mlp_post", dg_mlp_post[0]),
                          ("ssm_a_re", da_re), ("ssm_a_im", da_im), ("ssm_log_dt", dlog_dt),
                          ("ssm_b_re", db_re), ("ssm_b_im", db_im),
                          ("ssm_c_re", _block_diag_c_t(dc_mat[:, :SB_STATES])),
                          ("ssm_c_im", -_block_diag_c_t(dc_mat[:, SB_STATES:])),
                          ("ssm_d", dd_skip.reshape(N_GROUPS, GROUP_CH)), ("b_glu", db_glu[0]),
                          ("attn_sinks", dsink[:, 0])):
            small_grads[name][l] = val

    grad_x = dhres[BLK:][None]
    small_names = [n for n in SMALL if n != "meta_tokens"]
    partial_small = [dhres[PAD_ROWS:BLK]] + [jnp.stack(small_grads[n]) for n in small_names] + [loss_vec[0, :1]]
    recv_in0, recv_glu0, small_parts = _exchange_by_sequencer(
        mix_parts[:2] + [_pack(partial_small)], [False, False, True], 6, "scatter_in0", after=last_exchange)
    recv_mix[0] = [recv_in0, recv_glu0] + recv_out0

    grads, delta, new_m, new_v = {}, {}, {}, {}

    def adamw_big(names, recv0, recv1):
        for n, p0, p1 in zip(names, recv0, recv1):
            grads[n], delta[n], new_m[n], new_v[n] = _adamw_layers(p0, p1, w[n], m[n], v[n], f"adamw_{n}")

    adamw_big(("w_up", "w_down"), recv_up[0] + recv_down[0], recv_up[1] + recv_down[1])
    summed = _unpack(_sum_slots(small_parts, "sum_small_grads"), partial_small)
    loss = summed[-1][0]
    grads.update(zip(small_names, summed[1:-1]))
    grads["meta_tokens"] = lax.dynamic_slice_in_dim(summed[0], my_slot * (D // N_DEV), D // N_DEV, axis=1)
    like = [w[n] for n in SMALL]
    d_s, m_s, v_s = _adamw_packed(_pack([grads[n] for n in SMALL]), _pack(like), _pack([m[n] for n in SMALL]),
                                  _pack([v[n] for n in SMALL]), "adamw_small")
    adamw_big(("w_in",) + mixer_small, recv_mix[0], recv_mix[1])
    for n, dd, mm, vs in zip(SMALL, _unpack(d_s, like), _unpack(m_s, like), _unpack(v_s, like)):
        delta[n], new_m[n], new_v[n] = dd, mm, vs

    return (loss, grad_x, *[grads[n] for n in WEIGHTS], *[delta[n] for n in WEIGHTS],
            *[new_m[n] for n in WEIGHTS], *[new_v[n] for n in WEIGHTS])
```

```python
import functools
import math

import jax
import jax.numpy as jnp
from jax import lax
from jax.experimental import pallas as pl
from jax.experimental.pallas import tpu as pltpu
from jax.experimental.pallas import tpu_sc as plsc

F32 = jnp.float32
MXU_DTYPE = jnp.bfloat16
XFER_DTYPE = MXU_DTYPE
_pcall = pl.pallas_call
SDS = jax.ShapeDtypeStruct

D = 1024
D_SSM = 512
D_ATTN = 1024
D_KV = 256
D_FF = 4096
D_IN = 4096
HEAD_DIM = 64
N_Q_HEADS = 16
N_KV_HEADS = 4
Q_PER_KV = 4
N_META = 16
BLK = 128
PAD_ROWS = BLK - N_META
N_GROUPS = 32
N_STATE = 64
GROUP_CH = 16
N_SB = 4
SB_STATES = 512
ROPE_THETA = 10000.0
ATTN_SCALE = HEAD_DIM ** -0.5
NEG_INF = -1e30
RMS_EPS = 1e-6
N_DEV = 8
COL_SHARD = 512

ADAM_LR = 0.001
ADAM_B1 = 0.9
ADAM_B2 = 0.999
ADAM_EPS = 1e-08
ADAM_WD = 0.01
ADAM_STEP = 10

VMEM_LIMIT = 56 * 1024 * 1024
MESH_AXES = ("x", "y", "c")

_NT = (((1,), (1,)), ((), ()))
_TN = (((0,), (0,)), ((), ()))


def _cparams(*sem):
    return pltpu.CompilerParams(dimension_semantics=tuple(sem) if sem else None,
                                vmem_limit_bytes=VMEM_LIMIT)


def _row_tile(rows, cap=640):
    for t in (1664, 640, 512, 320, 256, 128):
        if t <= cap and rows % t == 0:
            return t
    raise ValueError(f"unsupported row count {rows}")


def _dot(a, b):
    return jnp.dot(a, b, preferred_element_type=F32)


def _dot_nt(a, b):
    return lax.dot_general(a, b, _NT, preferred_element_type=F32)


def _dot_tn(a, b):
    return lax.dot_general(a, b, _TN, preferred_element_type=F32)


def _sigmoid(x):
    return 1.0 / (1.0 + jnp.exp(-x))


_GELU_C = math.sqrt(2.0 / math.pi)


def _gelu_parts(y):
    t = jnp.tanh(_GELU_C * (y + 0.044715 * (y * y * y)))
    return 0.5 * y * (1.0 + t), t


def _gelu_grad(y, t):
    return 0.5 * (1.0 + t) + 0.5 * y * (1.0 - t * t) * (_GELU_C * (1.0 + 0.134145 * (y * y)))


def _rms_fwd(x, gain):
    r = lax.rsqrt(jnp.mean(x * x, axis=-1, keepdims=True) + RMS_EPS)
    return (x * r) * gain


def _rms_bwd(x, gain, dout):
    r = lax.rsqrt(jnp.mean(x * x, axis=-1, keepdims=True) + RMS_EPS)
    xh = x * r
    dxh = dout * gain
    dx = r * (dxh - xh * jnp.mean(dxh * xh, axis=-1, keepdims=True))
    return dx, jnp.sum(dout * xh, axis=0, keepdims=True)


def _mesh_pos():
    return lax.axis_index("x"), lax.axis_index("y"), lax.axis_index("c")


def _peer(pos, d):
    x, y, c = pos
    return (1 - x if d & 4 else x, 1 - y if d & 2 else y, 1 - c if d & 1 else c)


def _slot(pos):
    return 4 * pos[0] + 2 * pos[1] + pos[2]


def _exchange_copy(gather, src_ref, land_ref, sems, k, d, me, send_side):
    peer = _peer(me, d)
    sender = me if send_side else peer
    src = src_ref if gather else src_ref.at[_slot(peer) if send_side else _slot(me)]
    return pltpu.make_async_remote_copy(
        src_ref=src, dst_ref=land_ref.at[_slot(sender)],
        send_sem=sems[0].at[k * (N_DEV - 1) + d - 1], recv_sem=sems[1].at[k * (N_DEV - 1) + d - 1],
        device_id=peer, device_id_type=pl.DeviceIdType.MESH)


def _exchange_by_sequencer(srcs, gather, collective_id, name, after=()):
    n = len(srcs)
    flags = [gather] * n if isinstance(gather, bool) else list(gather)
    land_types = [SDS(((N_DEV,) + s.shape) if g else s.shape, s.dtype) for s, g in zip(srcs, flags)]

    def body(*refs):
        src_refs = refs[:n]
        land_refs = refs[n + len(after):2 * n + len(after)]
        sems = refs[2 * n + len(after):2 * n + len(after) + 2]
        local_sems = refs[2 * n + len(after) + 2]
        me = _mesh_pos()
        barrier = pltpu.get_barrier_semaphore()
        for d in range(1, N_DEV):
            pl.semaphore_signal(barrier, inc=1, device_id=_peer(me, d), device_id_type=pl.DeviceIdType.MESH)
        pl.semaphore_wait(barrier, N_DEV - 1)
        own = [pltpu.make_async_copy(src_refs[k] if flags[k] else src_refs[k].at[_slot(me)],
                                     land_refs[k].at[_slot(me)], local_sems.at[k]) for k in range(n)]
        for cp in own:
            cp.start()
        for k in range(n):
            for d in range(1, N_DEV):
                _exchange_copy(flags[k], src_refs[k], land_refs[k], sems, k, d, me, True).start()
        for cp in own:
            cp.wait()
        for k in range(n):
            for d in range(1, N_DEV):
                _exchange_copy(flags[k], src_refs[k], land_refs[k], sems, k, d, me, True).wait_send()
        for k in range(n):
            for d in range(1, N_DEV):
                _exchange_copy(flags[k], src_refs[k], land_refs[k], sems, k, d, me, False).wait_recv()

    sem_type = pltpu.SemaphoreType.DMA((n * (N_DEV - 1),))
    return pl.kernel(
        body, out_type=land_types, mesh=plsc.ScalarSubcoreMesh(axis_name="sequencer", num_cores=1), name=name,
        scratch_types=(sem_type, sem_type, pltpu.SemaphoreType.DMA((n,))),
        compiler_params=pltpu.CompilerParams(collective_id=collective_id),
    )(*srcs, *after)


def _load_resident(w_hbm, w_scr, sems, first_step):
    @pl.when(first_step)
    def _():
        copies = [pltpu.make_async_copy(w_hbm.at[s], w_scr.at[s], sems.at[s]) for s in range(N_DEV)]
        for cp in copies:
            cp.start()
        for cp in copies:
            cp.wait()


def _load_resident_transposed(w_hbm, w_scr, stage, sems, first_step):
    @pl.when(first_step)
    def _():
        copies = [pltpu.make_async_copy(w_hbm.at[s], stage.at[s % 2], sems.at[s % 2]) for s in range(N_DEV)]
        copies[0].start()
        for s in range(N_DEV):
            if s + 1 < N_DEV:
                copies[s + 1].start()
            copies[s].wait()
            w_scr[s] = stage[s % 2].T


def _rope_lanes(t, cos, sin_a, sin_b):
    return t * cos + pltpu.roll(t, 96, 1) * sin_a + pltpu.roll(t, 32, 1) * sin_b


def _in_proj(hres, gain3, w_in_g, cos, sin_a, sin_b, layer):
    rows = hres.shape[0]
    tm = _row_tile(rows, 320)

    def body(x_ref, g_ref, w_hbm, c_ref, a_ref, b_ref, u_ref, gate_ref, q_ref, k_ref, v_ref, h_ref, w_scr, w_sem):
        _load_resident(w_hbm, w_scr, w_sem, pl.program_id(0) == 0)
        hn = _rms_fwd(x_ref[...], g_ref[...]).astype(MXU_DTYPE)
        h_ref[...] = hn
        c, a, b = c_ref[...], a_ref[...], b_ref[...]
        u_ref[...] = _dot(hn, w_scr[0])
        for shard in (1, 2):
            res = _dot(hn, w_scr[shard])
            for t in range(4):
                lanes = slice(t * 128, (t + 1) * 128)
                out = slice((shard - 1) * COL_SHARD + t * 128, (shard - 1) * COL_SHARD + (t + 1) * 128)
                q_ref[:, out] = (_rope_lanes(res[:, lanes], c, a, b) * ATTN_SCALE).astype(MXU_DTYPE)
        res = _dot(hn, w_scr[3])
        for t in range(2):
            lanes = slice(t * 128, (t + 1) * 128)
            k_ref[:, lanes] = _rope_lanes(res[:, lanes], c, a, b).astype(MXU_DTYPE)
        v_ref[...] = res[:, D_KV:].astype(MXU_DTYPE)
        for shard in range(4, N_DEV):
            gate_ref[:, (shard - 4) * COL_SHARD:(shard - 3) * COL_SHARD] = _dot(hn, w_scr[shard])

    tab = pl.BlockSpec((tm, 128), lambda i: (i, 0))
    kv = pl.BlockSpec((tm, D_KV), lambda i: (i, 0))
    row_d = pl.BlockSpec((tm, D), lambda i: (i, 0))
    return _pcall(
        body, name=f"in_proj_l{layer}", grid=(rows // tm,),
        in_specs=[row_d, pl.BlockSpec((None, 1, D), lambda i: (layer, 0, 0)),
                  pl.BlockSpec(memory_space=pl.ANY), tab, tab, tab],
        out_specs=[pl.BlockSpec((tm, D_SSM), lambda i: (i, 0)), pl.BlockSpec((tm, 2 * D), lambda i: (i, 0)),
                   row_d, kv, kv, row_d],
        out_shape=[SDS((rows, D_SSM), F32), SDS((rows, 2 * D), F32), SDS((rows, D_ATTN), MXU_DTYPE),
                   SDS((rows, D_KV), MXU_DTYPE), SDS((rows, D_KV), MXU_DTYPE), SDS((rows, D), MXU_DTYPE)],
        scratch_shapes=[pltpu.VMEM((N_DEV, D, COL_SHARD), MXU_DTYPE), pltpu.SemaphoreType.DMA((N_DEV,))],
        compiler_params=_cparams("arbitrary"),
    )(hres, gain3, w_in_g, cos, sin_a, sin_b)


SCAN_TILE = 8


def _scan_tiles(x_ref, out_ref, tre_ref, tim_ref, sb, t_r, t_i, reverse, prev_ref=None):
    base = 4 if reverse else 0
    n_tiles = BLK // SCAN_TILE
    row = lax.broadcasted_iota(jnp.int32, (SCAN_TILE, SB_STATES), 0)
    for j in (range(n_tiles - 1, -1, -1) if reverse else range(n_tiles)):
        rows = slice(SCAN_TILE * j, SCAN_TILE * (j + 1))
        xr = x_ref[rows, :SB_STATES]
        xi = x_ref[rows, SB_STATES:]
        for k in range(3):
            shift = SCAN_TILE - (1 << k) if reverse else (1 << k)
            rr = pltpu.roll(xr, shift, 0)
            ri = pltpu.roll(xi, shift, 0)
            ar = tre_ref[sb, base + k]
            ai = tim_ref[sb, base + k]
            xr, xi = xr + (ar * rr - ai * ri), xi + (ar * ri + ai * rr)
        pr = tre_ref[sb, base + 3]
        pi = tim_ref[sb, base + 3]
        xr, xi = xr + (pr * t_r - pi * t_i), xi + (pr * t_i + pi * t_r)
        out_ref[rows, :SB_STATES] = xr
        out_ref[rows, SB_STATES:] = xi
        if prev_ref is not None:
            prev_ref[rows, :SB_STATES] = jnp.where(row == 0, t_r, pltpu.roll(xr, 1, 0))
            prev_ref[rows, SB_STATES:] = jnp.where(row == 0, t_i, pltpu.roll(xi, 1, 0))
        edge = slice(0, 1) if reverse else slice(SCAN_TILE - 1, SCAN_TILE)
        t_r, t_i = xr[edge], xi[edge]
    return t_r, t_i


def _s5_fwd(u, ssm, w_glu, b_glu3, layer):
    rows = u.shape[0]
    n_chunks = rows // BLK
    b_mat, c_mat, t_re, t_im, d_skip = (ssm[k] for k in ("b_mat", "c_mat", "t_re", "t_im", "d_skip"))

    def body(u_ref, bm_ref, cm_ref, tre_ref, tim_ref, d_ref, wg_ref, bg_ref,
             y_ref, ys_ref, cin_ref, carry, bu_scr, s_scr):
        @pl.when(pl.program_id(0) == 0)
        def _():
            carry[...] = jnp.zeros_like(carry)

        cin_ref[...] = carry[...]
        u = u_ref[...]
        for sb in range(N_SB):
            cols = slice(sb * 128, (sb + 1) * 128)
            u_sb = u[:, cols]
            bu_scr[sb] = _dot(u_sb.astype(MXU_DTYPE), bm_ref[sb])
            t_r, t_i = _scan_tiles(bu_scr.at[sb], s_scr.at[sb], tre_ref, tim_ref, sb,
                                   carry[2 * sb:2 * sb + 1, :], carry[2 * sb + 1:2 * sb + 2, :], False)
            carry[2 * sb:2 * sb + 1, :] = t_r
            carry[2 * sb + 1:2 * sb + 2, :] = t_i
            y_ref[:, cols] = _dot(s_scr[sb].astype(MXU_DTYPE), cm_ref[sb]) + d_ref[:, cols] * u_sb
        z, _ = _gelu_parts(y_ref[...])
        gl = _dot(z.astype(MXU_DTYPE), wg_ref[...]) + bg_ref[...]
        ys_ref[...] = (z * _sigmoid(gl)).astype(MXU_DTYPE)

    full = lambda shape: pl.BlockSpec(shape, lambda j: (0,) * len(shape))
    return _pcall(
        body, name=f"s5_fwd_l{layer}", grid=(n_chunks,),
        in_specs=[pl.BlockSpec((BLK, D_SSM), lambda j: (j, 0)),
                  full((N_SB, 128, 2 * SB_STATES)), full((N_SB, 2 * SB_STATES, 128)),
                  full((N_SB, 8, SCAN_TILE, SB_STATES)), full((N_SB, 8, SCAN_TILE, SB_STATES)),
                  full((1, D_SSM)), full((D_SSM, D_SSM)),
                  pl.BlockSpec((None, 1, D_SSM), lambda j: (layer, 0, 0))],
        out_specs=[pl.BlockSpec((BLK, D_SSM), lambda j: (j, 0)), pl.BlockSpec((BLK, D_SSM), lambda j: (j, 0)),
                   pl.BlockSpec((None, 8, SB_STATES), lambda j: (j, 0, 0))],
        out_shape=[SDS((rows, D_SSM), F32), SDS((rows, D_SSM), MXU_DTYPE), SDS((n_chunks, 8, SB_STATES), F32)],
        scratch_shapes=[pltpu.VMEM((8, SB_STATES), F32), pltpu.VMEM((N_SB, BLK, 2 * SB_STATES), F32),
                        pltpu.VMEM((N_SB, BLK, 2 * SB_STATES), F32)],
        compiler_params=_cparams("arbitrary"),
    )(u, b_mat, c_mat, t_re, t_im, d_skip, w_glu, b_glu3)


def _attn_mask(i):
    row = lax.broadcasted_iota(jnp.int32, (BLK, 3 * BLK), 0) + i * BLK
    col = lax.broadcasted_iota(jnp.int32, (BLK, 3 * BLK), 1)
    seg = jnp.right_shift(col, 7)
    c = jnp.bitwise_and(col, BLK - 1)
    kidx = c + (i + seg - 2) * BLK
    ok_meta = (seg == 0) & (c >= PAD_ROWS) & (row - c >= BLK)
    ok_win = (seg > 0) & (kidx >= PAD_ROWS) & (kidx <= row) & (row - kidx < BLK)
    return jnp.where(ok_meta | ok_win, 0.0, NEG_INF)


def _head_lanes(h):
    return slice(h * HEAD_DIM, (h + 1) * HEAD_DIM)


def _group_rows(ref, kvh):
    return jnp.concatenate([ref[:, _head_lanes(kvh * Q_PER_KV + g)] for g in range(Q_PER_KV)], axis=0)


def _group_bias(bias, sink_ref, layer, kvh):
    first_col = lax.broadcasted_iota(jnp.int32, (BLK, BLK), 1) == 0
    slabs = []
    for g in range(Q_PER_KV):
        first = jnp.where(first_col, sink_ref[layer, kvh * Q_PER_KV + g], bias[:, :BLK])
        slabs.append(jnp.concatenate([first, bias[:, BLK:]], axis=1))
    return jnp.concatenate(slabs, axis=0)


def _attn_probs(q4, k3, bias4):
    s = _dot_nt(q4, k3) + bias4
    e = jnp.exp(s - jnp.max(s, axis=-1, keepdims=True))
    return e * (1.0 / jnp.sum(e, axis=-1, keepdims=True))


def _attn_fwd(q, k, v, sinks, layer):
    rows = q.shape[0]
    n_blk = rows // BLK

    def body(sink_ref, q_ref, km_ref, kp_ref, kc_ref, vm_ref, vp_ref, vc_ref, o_ref):
        bias = _attn_mask(pl.program_id(0))
        for kvh in range(N_KV_HEADS):
            lanes = _head_lanes(kvh)
            k3 = jnp.concatenate([km_ref[:, lanes], kp_ref[:, lanes], kc_ref[:, lanes]], axis=0)
            v3 = jnp.concatenate([vm_ref[:, lanes], vp_ref[:, lanes], vc_ref[:, lanes]], axis=0)
            p = _attn_probs(_group_rows(q_ref, kvh), k3, _group_bias(bias, sink_ref, layer, kvh))
            o4 = _dot(p.astype(MXU_DTYPE), v3).astype(MXU_DTYPE)
            for g in range(Q_PER_KV):
                o_ref[:, _head_lanes(kvh * Q_PER_KV + g)] = o4[g * BLK:(g + 1) * BLK]

    kv_meta = pl.BlockSpec((BLK, D_KV), lambda i: (0, 0))
    kv_prev = pl.BlockSpec((BLK, D_KV), lambda i: (jnp.maximum(i - 1, 0), 0))
    kv_cur = pl.BlockSpec((BLK, D_KV), lambda i: (i, 0))
    return _pcall(
        body, name=f"attn_fwd_l{layer}", grid=(n_blk,),
        in_specs=[pl.BlockSpec(memory_space=pltpu.SMEM),
                  pl.BlockSpec((BLK, D_ATTN), lambda i: (i, 0)),
                  kv_meta, kv_prev, kv_cur, kv_meta, kv_prev, kv_cur],
        out_specs=pl.BlockSpec((BLK, D_ATTN), lambda i: (i, 0)),
        out_shape=SDS((rows, D_ATTN), MXU_DTYPE),
        compiler_params=_cparams("parallel"),
    )(sinks, q, k, k, k, v, v, v)


def _merge_fwd(y_ssm, y_attn, gates, hres, w_o_ssm, w_o_attn, w_out, gain3, layer):
    rows = hres.shape[0]
    tm = _row_tile(rows, 320)

    def body(ys_ref, ya_ref, gs_ref, ga_ref, x_ref, wos_ref, woa_ref, wout_ref, g_ref,
             mg_ref, mix_ref, out_ref):
        a1 = _dot(ys_ref[...], wos_ref[...])
        a2 = _dot(ya_ref[...], woa_ref[...])
        merged = (_sigmoid(gs_ref[...]) * a1 + _sigmoid(ga_ref[...]) * a2).astype(MXU_DTYPE)
        mg_ref[...] = merged
        mix = _dot(merged, wout_ref[...])
        mix_ref[...] = mix
        out_ref[...] = x_ref[...] + _rms_fwd(mix, g_ref[...])

    row_d = pl.BlockSpec((tm, D), lambda i: (i, 0))
    full = lambda shape: pl.BlockSpec(shape, lambda i: (0,) * len(shape))
    return _pcall(
        body, name=f"merge_fwd_l{layer}", grid=(rows // tm,),
        in_specs=[pl.BlockSpec((tm, D_SSM), lambda i: (i, 0)), row_d,
                  row_d, pl.BlockSpec((tm, D), lambda i: (i, 1)), row_d,
                  full((D_SSM, D)), full((D_ATTN, D)), full((D, D)),
                  pl.BlockSpec((None, 1, D), lambda i: (layer, 0, 0))],
        out_specs=[row_d, row_d, row_d],
        out_shape=[SDS((rows, D), MXU_DTYPE), SDS((rows, D), F32), SDS((rows, D), F32)],
        compiler_params=_cparams("parallel"),
    )(y_ssm, y_attn, gates, gates, hres, w_o_ssm, w_o_attn, w_out, gain3)


def _mlp_fwd(hres, gain_pre3, gain_post3, w_up_g, w_down_g, layer):
    rows = hres.shape[0]
    tm = _row_tile(rows, 320)

    def body(x_ref, gp_ref, gq_ref, wu_hbm, wd_hbm, up_ref, h_ref, ff_ref, out_ref,
             act_scr, wu_scr, wd_scr, wu_sem, wd_sem):
        first = pl.program_id(0) == 0
        _load_resident(wu_hbm, wu_scr, wu_sem, first)
        _load_resident(wd_hbm, wd_scr, wd_sem, first)
        hn = _rms_fwd(x_ref[...], gp_ref[...]).astype(MXU_DTYPE)
        h_ref[...] = hn
        for kf in range(N_DEV):
            cols = slice(kf * COL_SHARD, (kf + 1) * COL_SHARD)
            up = _dot(hn, wu_scr[kf])
            up_ref[:, cols] = up.astype(MXU_DTYPE)
            r = jnp.maximum(up, 0.0)
            act_scr[:, cols] = (r * r).astype(MXU_DTYPE)
        ff = _dot(act_scr[...], wd_scr[...].reshape(D_FF, D))
        ff_ref[...] = ff
        out_ref[...] = x_ref[...] + _rms_fwd(ff, gq_ref[...])

    row_d = pl.BlockSpec((tm, D), lambda i: (i, 0))
    gain = pl.BlockSpec((None, 1, D), lambda i: (layer, 0, 0))
    return _pcall(
        body, name=f"mlp_fwd_l{layer}", grid=(rows // tm,),
        in_specs=[row_d, gain, gain, pl.BlockSpec(memory_space=pl.ANY), pl.BlockSpec(memory_space=pl.ANY)],
        out_specs=[pl.BlockSpec((tm, D_FF), lambda i: (i, 0)), row_d, row_d, row_d],
        out_shape=[SDS((rows, D_FF), MXU_DTYPE), SDS((rows, D), MXU_DTYPE), SDS((rows, D), F32), SDS((rows, D), F32)],
        scratch_shapes=[pltpu.VMEM((tm, D_FF), MXU_DTYPE),
                        pltpu.VMEM((N_DEV, D, COL_SHARD), MXU_DTYPE), pltpu.VMEM((N_DEV, COL_SHARD, D), MXU_DTYPE),
                        pltpu.SemaphoreType.DMA((N_DEV,)), pltpu.SemaphoreType.DMA((N_DEV,))],
        compiler_params=_cparams("arbitrary"),
    )(hres, gain_pre3, gain_post3, w_up_g, w_down_g)


def _loss_and_grad(hres, target):
    rows = hres.shape[0]
    n_blk = rows // BLK

    def body(y_ref, t_ref, dy_ref, loss_ref):
        i = pl.program_id(0)

        @pl.when(i == 0)
        def _():
            dy_ref[...] = jnp.zeros_like(dy_ref)
            loss_ref[...] = jnp.zeros_like(loss_ref)

        @pl.when(i > 0)
        def _():
            err = y_ref[...] - t_ref[...]
            dy_ref[...] = err * (1.0 / D)
            loss_ref[...] += jnp.sum(err * err) * (0.5 / D)

    return _pcall(
        body, name="loss", grid=(n_blk,),
        in_specs=[pl.BlockSpec((BLK, D), lambda i: (i, 0)),
                  pl.BlockSpec((BLK, D), lambda i: (jnp.maximum(i - 1, 0), 0))],
        out_specs=[pl.BlockSpec((BLK, D), lambda i: (i, 0)), pl.BlockSpec((1, 128), lambda i: (0, 0))],
        out_shape=[SDS((rows, D), F32), SDS((1, 128), F32)],
        compiler_params=_cparams("arbitrary"),
    )(hres, target)


def _relu_squared(up):
    r = jnp.maximum(up.astype(F32), 0.0)
    return (r * r).astype(MXU_DTYPE)


def _matmul_tn(a, b, name, dev_major_cols=None, a_fn=None):
    rows, ka = a.shape
    n = b.shape[1]
    ta = min(ka, 1024)
    tn = 1024 if n % 1024 == 0 else 512
    tr = _row_tile(rows, 1664)
    n_r = rows // tr

    def body(a_ref, b_ref, o_ref, acc):
        r = pl.program_id(2)

        @pl.when(r == 0)
        def _():
            acc[...] = jnp.zeros_like(acc)

        a_blk = a_ref[...] if a_fn is None else a_fn(a_ref[...])
        acc[...] += _dot_tn(a_blk, b_ref[...])

        @pl.when(r == n_r - 1)
        def _():
            if dev_major_cols is None:
                o_ref[...] = acc[...].astype(XFER_DTYPE)
            else:
                for s in range(tn // dev_major_cols):
                    o_ref[s] = acc[:, s * dev_major_cols:(s + 1) * dev_major_cols].astype(XFER_DTYPE)

    if dev_major_cols is None:
        out_spec = pl.BlockSpec((ta, tn), lambda i, j, r: (i, j))
        out_shape = SDS((ka, n), XFER_DTYPE)
    else:
        w = dev_major_cols
        out_spec = pl.BlockSpec((tn // w, ta, w), lambda i, j, r: (j, i, 0))
        out_shape = SDS((n // w, ka, w), XFER_DTYPE)
    return _pcall(
        body, name=name, grid=(ka // ta, n // tn, n_r),
        in_specs=[pl.BlockSpec((tr, ta), lambda i, j, r: (r, i)), pl.BlockSpec((tr, tn), lambda i, j, r: (r, j))],
        out_specs=out_spec, out_shape=out_shape,
        scratch_shapes=[pltpu.VMEM((ta, tn), F32)],
        compiler_params=_cparams("parallel", "parallel", "arbitrary"),
    )(a, b)


def _dw_in(h, dproj_pieces, layer):
    rows = h.shape[0]
    tr = _row_tile(rows, 1664)
    n_r = rows // tr

    def body(h_ref, du_ref, dqkv_ref, dgs_ref, dga_ref, o_ref, acc):
        j = pl.program_id(0)
        r = pl.program_id(1)

        @pl.when(r == 0)
        def _():
            acc[...] = jnp.zeros_like(acc)

        for piece_ref, (first, count) in zip((du_ref, dqkv_ref, dgs_ref, dga_ref), DPROJ_PIECES):
            @pl.when((j >= first) & (j < first + count))
            def _():
                acc[...] += _dot_tn(h_ref[...], piece_ref[...])

        @pl.when(r == n_r - 1)
        def _():
            o_ref[...] = acc[...].astype(XFER_DTYPE)

    def piece_spec(first, count):
        def index(j, r):
            mine = (j >= first) & (j < first + count)
            return jnp.where(mine, r, 0), jnp.clip(j - first, 0, count - 1)
        return pl.BlockSpec((tr, COL_SHARD), index)

    return _pcall(
        body, name=f"dw_in_l{layer}", grid=(N_DEV, n_r),
        in_specs=[pl.BlockSpec((tr, D), lambda j, r: (r, 0))] + [piece_spec(*p) for p in DPROJ_PIECES],
        out_specs=pl.BlockSpec((None, D, COL_SHARD), lambda j, r: (j, 0, 0)),
        out_shape=SDS((N_DEV, D, COL_SHARD), XFER_DTYPE),
        scratch_shapes=[pltpu.VMEM((D, COL_SHARD), F32)],
        compiler_params=_cparams("arbitrary", "arbitrary"),
    )(h, *dproj_pieces)


def _mlp_bwd(dout, ff, up, hres_mid, gain_pre3, gain_post3, w_up_g, w_down_g, layer):
    rows = dout.shape[0]
    tm = _row_tile(rows, 320)

    def body(do_ref, ff_ref, up_ref, x_ref, gp_ref, gq_ref, wu_hbm, wd_hbm,
             dff_ref, dup_ref, dx_ref, dgq_ref, dgp_ref, wut_scr, wdt_scr, wu_stage, wd_stage, wu_sem, wd_sem):
        i = pl.program_id(0)
        _load_resident_transposed(wu_hbm, wut_scr, wu_stage, wu_sem, i == 0)
        _load_resident_transposed(wd_hbm, wdt_scr, wd_stage, wd_sem, i == 0)

        @pl.when(i == 0)
        def _():
            dgq_ref[...] = jnp.zeros_like(dgq_ref)
            dgp_ref[...] = jnp.zeros_like(dgp_ref)

        dff, dg = _rms_bwd(ff_ref[...], gq_ref[...], do_ref[...])
        dgq_ref[...] += dg
        dffb = dff.astype(MXU_DTYPE)
        dff_ref[...] = dffb
        for kf in range(N_DEV):
            cols = slice(kf * COL_SHARD, (kf + 1) * COL_SHARD)
            dact = _dot(dffb, wdt_scr[kf])
            dup_ref[:, cols] = (dact * (2.0 * jnp.maximum(up_ref[:, cols].astype(F32), 0.0))).astype(MXU_DTYPE)
        dh = _dot(dup_ref[...], wut_scr[...].reshape(D_FF, D))
        dx, dg = _rms_bwd(x_ref[...], gp_ref[...], dh)
        dgp_ref[...] += dg
        dx_ref[...] = do_ref[...] + dx

    row_d = pl.BlockSpec((tm, D), lambda i: (i, 0))
    row_ff = pl.BlockSpec((tm, D_FF), lambda i: (i, 0))
    gain = pl.BlockSpec((None, 1, D), lambda i: (layer, 0, 0))
    dgain = pl.BlockSpec((1, D), lambda i: (0, 0))
    return _pcall(
        body, name=f"mlp_bwd_l{layer}", grid=(rows // tm,),
        in_specs=[row_d, row_d, row_ff, row_d, gain, gain,
                  pl.BlockSpec(memory_space=pl.ANY), pl.BlockSpec(memory_space=pl.ANY)],
        out_specs=[row_d, row_ff, row_d, dgain, dgain],
        out_shape=[SDS((rows, D), MXU_DTYPE), SDS((rows, D_FF), MXU_DTYPE), SDS((rows, D), F32),
                   SDS((1, D), F32), SDS((1, D), F32)],
        scratch_shapes=[pltpu.VMEM((N_DEV, COL_SHARD, D), MXU_DTYPE), pltpu.VMEM((N_DEV, D, COL_SHARD), MXU_DTYPE),
                        pltpu.VMEM((2, D, COL_SHARD), MXU_DTYPE), pltpu.VMEM((2, COL_SHARD, D), MXU_DTYPE),
                        pltpu.SemaphoreType.DMA((2,)), pltpu.SemaphoreType.DMA((2,))],
        compiler_params=_cparams("arbitrary"),
    )(dout, ff, up, hres_mid, gain_pre3, gain_post3, w_up_g, w_down_g)


def _merge_bwd(dhm, mix, y_ssm, y_attn, gates, w_o_ssm, w_o_attn, w_o_ssm_t, w_o_attn_t, w_out_t, gain3, layer):
    rows = dhm.shape[0]
    tm = _row_tile(rows, 320)

    def body(dh_ref, mix_ref, ys_ref, ya_ref, gs_ref, ga_ref, wos_ref, woa_ref, wost_ref, woat_ref, woutt_ref, g_ref,
             dmix_ref, da1_ref, da2_ref, dgs_ref, dga_ref, dys_ref, dya_ref, dg_ref):
        @pl.when(pl.program_id(0) == 0)
        def _():
            dg_ref[...] = jnp.zeros_like(dg_ref)

        dmix, dg = _rms_bwd(mix_ref[...], g_ref[...], dh_ref[...])
        dg_ref[...] += dg
        dmixb = dmix.astype(MXU_DTYPE)
        dmix_ref[...] = dmixb
        dmerged = _dot(dmixb, woutt_ref[...])
        sg_s = _sigmoid(gs_ref[...])
        sg_a = _sigmoid(ga_ref[...])
        da1 = (dmerged * sg_s).astype(MXU_DTYPE)
        da2 = (dmerged * sg_a).astype(MXU_DTYPE)
        da1_ref[...] = da1
        da2_ref[...] = da2
        a1 = _dot(ys_ref[...], wos_ref[...])
        a2 = _dot(ya_ref[...], woa_ref[...])
        dgs_ref[...] = (dmerged * a1 * (sg_s * (1.0 - sg_s))).astype(MXU_DTYPE)
        dga_ref[...] = (dmerged * a2 * (sg_a * (1.0 - sg_a))).astype(MXU_DTYPE)
        dys_ref[...] = _dot(da1, wost_ref[...])
        dya_ref[...] = _dot(da2, woat_ref[...])

    row_d = pl.BlockSpec((tm, D), lambda i: (i, 0))
    full = lambda shape: pl.BlockSpec(shape, lambda i: (0,) * len(shape))
    return _pcall(
        body, name=f"merge_bwd_l{layer}", grid=(rows // tm,),
        in_specs=[row_d, row_d, pl.BlockSpec((tm, D_SSM), lambda i: (i, 0)), row_d,
                  row_d, pl.BlockSpec((tm, D), lambda i: (i, 1)),
                  full((D_SSM, D)), full((D_ATTN, D)), full((D, D_SSM)), full((D, D_ATTN)), full((D, D)),
                  pl.BlockSpec((None, 1, D), lambda i: (layer, 0, 0))],
        out_specs=[row_d, row_d, row_d, row_d, row_d, pl.BlockSpec((tm, D_SSM), lambda i: (i, 0)), row_d,
                   pl.BlockSpec((1, D), lambda i: (0, 0))],
        out_shape=[SDS((rows, D), MXU_DTYPE)] * 5 + [SDS((rows, D_SSM), F32), SDS((rows, D_ATTN), F32),
                                                      SDS((1, D), F32)],
        compiler_params=_cparams("arbitrary"),
    )(dhm, mix, y_ssm, y_attn, gates, gates, w_o_ssm, w_o_attn, w_o_ssm_t, w_o_attn_t, w_out_t, gain3)


def _attn_bwd(q, k, v, d_out, sinks, layer):
    rows = q.shape[0]
    n_blk = rows // BLK
    last = n_blk - 1

    def body(sink_ref, q_ref, km_ref, kp_ref, kc_ref, vm_ref, vp_ref, vc_ref, do_ref,
             dq_ref, dk_ref, dv_ref, dkm_ref, dvm_ref, ds_ref, dk_carry, dv_carry):
        i = pl.program_id(0)

        @pl.when(i == 0)
        def _():
            dkm_ref[...] = jnp.zeros_like(dkm_ref)
            dvm_ref[...] = jnp.zeros_like(dvm_ref)
            ds_ref[...] = jnp.zeros_like(ds_ref)
            dk_carry[...] = jnp.zeros_like(dk_carry)
            dv_carry[...] = jnp.zeros_like(dv_carry)

        @pl.when(i <= last)
        def _():
            bias = _attn_mask(i)
            for kvh in range(N_KV_HEADS):
                lanes = _head_lanes(kvh)
                k3 = jnp.concatenate([km_ref[:, lanes], kp_ref[:, lanes], kc_ref[:, lanes]], axis=0)
                v3 = jnp.concatenate([vm_ref[:, lanes], vp_ref[:, lanes], vc_ref[:, lanes]], axis=0)
                q4 = _group_rows(q_ref, kvh)
                do4 = _group_rows(do_ref, kvh).astype(MXU_DTYPE)
                p = _attn_probs(q4, k3, _group_bias(bias, sink_ref, layer, kvh))
                dp = _dot_nt(do4, v3)
                dsf = p * (dp - jnp.sum(dp * p, axis=-1, keepdims=True))
                dsc = dsf.astype(MXU_DTYPE)
                dv3 = _dot_tn(p.astype(MXU_DTYPE), do4)
                dk3 = _dot_tn(dsc, q4)
                dq4 = _dot(dsc, k3)
                for g in range(Q_PER_KV):
                    h = kvh * Q_PER_KV + g
                    dq_ref[:, _head_lanes(h)] = dq4[g * BLK:(g + 1) * BLK]
                    ds_ref[h:h + 1, :] += jnp.sum(dsf[g * BLK:(g + 1) * BLK, 0:BLK], axis=0, keepdims=True)
                dkm_ref[:, lanes] += dk3[0:BLK]
                dvm_ref[:, lanes] += dv3[0:BLK]
                dk_ref[:, lanes] = dk_carry[:, lanes] + dk3[BLK:2 * BLK]
                dv_ref[:, lanes] = dv_carry[:, lanes] + dv3[BLK:2 * BLK]
                dk_carry[:, lanes] = dk3[2 * BLK:3 * BLK]
                dv_carry[:, lanes] = dv3[2 * BLK:3 * BLK]

        @pl.when(i == last + 1)
        def _():
            dk_ref[...] = dk_carry[...]
            dv_ref[...] = dv_carry[...]

    cur = lambda i: (jnp.minimum(i, last), 0)
    prev = lambda i: (jnp.clip(i - 1, 0, last), 0)
    kv_meta = pl.BlockSpec((BLK, D_KV), lambda i: (0, 0))
    kv_prev = pl.BlockSpec((BLK, D_KV), prev)
    kv_cur = pl.BlockSpec((BLK, D_KV), cur)
    return _pcall(
        body, name=f"attn_bwd_l{layer}", grid=(n_blk + 1,),
        in_specs=[pl.BlockSpec(memory_space=pltpu.SMEM),
                  pl.BlockSpec((BLK, D_ATTN), cur),
                  kv_meta, kv_prev, kv_cur, kv_meta, kv_prev, kv_cur,
                  pl.BlockSpec((BLK, D_ATTN), cur)],
        out_specs=[pl.BlockSpec((BLK, D_ATTN), cur), kv_prev, kv_prev, kv_meta, kv_meta,
                   pl.BlockSpec((N_Q_HEADS, 128), lambda i: (0, 0))],
        out_shape=[SDS((rows, D_ATTN), F32), SDS((rows, D_KV), F32), SDS((rows, D_KV), F32),
                   SDS((BLK, D_KV), F32), SDS((BLK, D_KV), F32), SDS((N_Q_HEADS, 128), F32)],
        scratch_shapes=[pltpu.VMEM((BLK, D_KV), F32), pltpu.VMEM((BLK, D_KV), F32)],
        compiler_params=_cparams("arbitrary"),
    )(sinks, q, k, k, k, v, v, v, d_out)


def _rope_bwd(dq, dk, dv, dk_meta, dv_meta, cos, sin_a, sin_b, layer):
    rows = dq.shape[0]
    tm = _row_tile(rows)

    def body(dq_ref, dk_ref, dv_ref, dkm_ref, dvm_ref, c_ref, a_ref, b_ref, o_ref):
        c, a, b = c_ref[...], -a_ref[...], -b_ref[...]
        for t in range(8):
            x = dq_ref[:, t * 128:(t + 1) * 128]
            o_ref[:, t * 128:(t + 1) * 128] = (_rope_lanes(x, c, a, b) * ATTN_SCALE).astype(MXU_DTYPE)
        for t in range(2):
            x = dk_ref[:, t * 128:(t + 1) * 128]
            o_ref[:, D_ATTN + t * 128:D_ATTN + (t + 1) * 128] = _rope_lanes(x, c, a, b).astype(MXU_DTYPE)
        o_ref[:, D_ATTN + D_KV:] = dv_ref[...].astype(MXU_DTYPE)

        @pl.when(pl.program_id(0) == 0)
        def _():
            cb, ab, bb = c[0:BLK], a[0:BLK], b[0:BLK]
            is_meta = lax.broadcasted_iota(jnp.int32, (BLK, 128), 0) >= PAD_ROWS
            for t in range(2):
                x = dk_ref[0:BLK, t * 128:(t + 1) * 128] + jnp.where(is_meta, dkm_ref[:, t * 128:(t + 1) * 128], 0.0)
                o_ref[0:BLK, D_ATTN + t * 128:D_ATTN + (t + 1) * 128] = _rope_lanes(x, cb, ab, bb).astype(MXU_DTYPE)
                xv = dv_ref[0:BLK, t * 128:(t + 1) * 128] + jnp.where(is_meta, dvm_ref[:, t * 128:(t + 1) * 128], 0.0)
                o_ref[0:BLK, D_ATTN + D_KV + t * 128:D_ATTN + D_KV + (t + 1) * 128] = xv.astype(MXU_DTYPE)

    tab = pl.BlockSpec((tm, 128), lambda i: (i, 0))
    kv = pl.BlockSpec((tm, D_KV), lambda i: (i, 0))
    meta = pl.BlockSpec((BLK, D_KV), lambda i: (0, 0))
    return _pcall(
        body, name=f"rope_bwd_l{layer}", grid=(rows // tm,),
        in_specs=[pl.BlockSpec((tm, D_ATTN), lambda i: (i, 0)), kv, kv, meta, meta, tab, tab, tab],
        out_specs=pl.BlockSpec((tm, D_ATTN + 2 * D_KV), lambda i: (i, 0)),
        out_shape=SDS((rows, D_ATTN + 2 * D_KV), MXU_DTYPE),
        compiler_params=_cparams("parallel"),
    )(dq, dk, dv, dk_meta, dv_meta, cos, sin_a, sin_b)


def _s5_bwd(d_gated, y, u, carry_in, ssm, w_glu, b_glu3, layer):
    rows = y.shape[0]
    n_chunks = rows // BLK
    b_mat, c_mat, t_re, t_im, d_skip = (ssm[k] for k in ("b_mat", "c_mat", "t_re", "t_im", "d_skip"))

    def body(dz_ref, y_ref, u_ref, cin_ref, bm_ref, cm_ref, tre_ref, tim_ref, d_ref, wg_ref, bg_ref,
             du_ref, dwg_ref, dbg_ref, dd_ref, dbm_ref, dcm_ref, dab_ref,
             lam_carry, bu_scr, s_scr, sp_scr, g_scr, lam_scr):
        step = pl.program_id(0)
        chunk = n_chunks - 1 - step

        @pl.when(step == 0)
        def _():
            for r in (dwg_ref, dbg_ref, dd_ref, dbm_ref, dcm_ref, dab_ref, lam_carry):
                r[...] = jnp.zeros_like(r)

        y = y_ref[...]
        u = u_ref[...]
        d_o = dz_ref[...]
        z, t = _gelu_parts(y)
        zb = z.astype(MXU_DTYPE)
        sg = _sigmoid(_dot(zb, wg_ref[...]) + bg_ref[...])
        dgl = d_o * z * (sg * (1.0 - sg))
        dglb = dgl.astype(MXU_DTYPE)
        dz = d_o * sg + _dot_nt(dglb, wg_ref[...])
        dwg_ref[...] += _dot_tn(zb, dglb)
        dbg_ref[...] += jnp.sum(dgl, axis=0, keepdims=True)
        dy = dz * _gelu_grad(y, t)
        dd_ref[...] += jnp.sum(dy * u, axis=0, keepdims=True)
        grow = lax.broadcasted_iota(jnp.int32, (BLK, 128), 0) + chunk * BLK
        for sb in range(N_SB):
            cols = slice(sb * 128, (sb + 1) * 128)
            u_sb = u[:, cols].astype(MXU_DTYPE)
            dy_sb = dy[:, cols]
            dyb = dy_sb.astype(MXU_DTYPE)
            bu_scr[sb] = _dot(u_sb, bm_ref[sb])
            _scan_tiles(bu_scr.at[sb], s_scr.at[sb], tre_ref, tim_ref, sb,
                        cin_ref[2 * sb:2 * sb + 1, :], cin_ref[2 * sb + 1:2 * sb + 2, :], False, prev_ref=sp_scr.at[sb])
            dcm_ref[sb] += _dot_tn(s_scr[sb].astype(MXU_DTYPE), dyb)
            g_scr[sb] = _dot_nt(dyb, cm_ref[sb])
            n_r, n_i = _scan_tiles(g_scr.at[sb], lam_scr.at[sb], tre_ref, tim_ref, sb,
                                   lam_carry[2 * sb:2 * sb + 1, :], lam_carry[2 * sb + 1:2 * sb + 2, :], True)
            lam_carry[2 * sb:2 * sb + 1, :] = n_r
            lam_carry[2 * sb + 1:2 * sb + 2, :] = n_i
            lr, li = lam_scr[sb, :, :SB_STATES], lam_scr[sb, :, SB_STATES:]
            spr, spi = sp_scr[sb, :, :SB_STATES], sp_scr[sb, :, SB_STATES:]
            dab_ref[2 * sb:2 * sb + 1, :] += jnp.sum(spr * lr + spi * li, axis=0, keepdims=True)
            dab_ref[2 * sb + 1:2 * sb + 2, :] += jnp.sum(spr * li - spi * lr, axis=0, keepdims=True)
            lam = lam_scr[sb].astype(MXU_DTYPE)
            dbm_ref[sb] += _dot_tn(u_sb, lam)
            du = _dot_nt(lam, bm_ref[sb]) + d_ref[:, cols] * dy_sb
            du_ref[:, cols] = jnp.where(grow >= PAD_ROWS, du, 0.0).astype(MXU_DTYPE)

    rev = lambda j: (n_chunks - 1 - j, 0)
    full = lambda shape: pl.BlockSpec(shape, lambda j: (0,) * len(shape))
    tables = [full((N_SB, 8, SCAN_TILE, SB_STATES))] * 2
    chunk_scratch = pltpu.VMEM((N_SB, BLK, 2 * SB_STATES), F32)
    return _pcall(
        body, name=f"s5_bwd_l{layer}", grid=(n_chunks,),
        in_specs=[pl.BlockSpec((BLK, D_SSM), rev), pl.BlockSpec((BLK, D_SSM), rev), pl.BlockSpec((BLK, D_SSM), rev),
                  pl.BlockSpec((None, 8, SB_STATES), lambda j: (n_chunks - 1 - j, 0, 0)),
                  full((N_SB, 128, 2 * SB_STATES)), full((N_SB, 2 * SB_STATES, 128))] + tables + [
                  full((1, D_SSM)), full((D_SSM, D_SSM)),
                  pl.BlockSpec((None, 1, D_SSM), lambda j: (layer, 0, 0))],
        out_specs=[pl.BlockSpec((BLK, D_SSM), rev), full((D_SSM, D_SSM)), full((1, D_SSM)), full((1, D_SSM)),
                   full((N_SB, 128, 2 * SB_STATES)), full((N_SB, 2 * SB_STATES, 128)), full((8, SB_STATES))],
        out_shape=[SDS((rows, D_SSM), MXU_DTYPE), SDS((D_SSM, D_SSM), F32), SDS((1, D_SSM), F32), SDS((1, D_SSM), F32),
                   SDS((N_SB, 128, 2 * SB_STATES), F32), SDS((N_SB, 2 * SB_STATES, 128), F32), SDS((8, SB_STATES), F32)],
        scratch_shapes=[pltpu.VMEM((8, SB_STATES), F32)] + [chunk_scratch] * 5,
        compiler_params=_cparams("arbitrary"),
    )(d_gated, y, u, carry_in, b_mat, c_mat, t_re, t_im, d_skip, w_glu, b_glu3)


DPROJ_PIECES = ((0, 1), (1, 3), (4, 2), (6, 2))


def _in_bwd(dproj_pieces, dhm, hres, gain3, w_in_g, layer):
    rows = hres.shape[0]
    tm = _row_tile(rows)

    def body(du_ref, dqkv_ref, dgs_ref, dga_ref, dh_ref, x_ref, g_ref, w_hbm, dx_ref, dg_ref,
             wt_scr, w_stage, w_sem):
        i = pl.program_id(0)
        _load_resident_transposed(w_hbm, wt_scr, w_stage, w_sem, i == 0)

        @pl.when(i == 0)
        def _():
            dg_ref[...] = jnp.zeros_like(dg_ref)

        dh = None
        for piece_ref, (first, count) in zip((du_ref, dqkv_ref, dgs_ref, dga_ref), DPROJ_PIECES):
            wt = wt_scr[first:first + count].reshape(count * COL_SHARD, D)
            part = _dot(piece_ref[...], wt)
            dh = part if dh is None else dh + part
        dx, dg = _rms_bwd(x_ref[...], g_ref[...], dh)
        dg_ref[...] += dg
        dx_ref[...] = dh_ref[...] + dx

    row_d = pl.BlockSpec((tm, D), lambda i: (i, 0))
    return _pcall(
        body, name=f"in_bwd_l{layer}", grid=(rows // tm,),
        in_specs=[pl.BlockSpec((tm, count * COL_SHARD), lambda i: (i, 0)) for _, count in DPROJ_PIECES] + [
                  row_d, row_d,
                  pl.BlockSpec((None, 1, D), lambda i: (layer, 0, 0)),
                  pl.BlockSpec(memory_space=pl.ANY)],
        out_specs=[row_d, pl.BlockSpec((1, D), lambda i: (0, 0))],
        out_shape=[SDS((rows, D), F32), SDS((1, D), F32)],
        scratch_shapes=[pltpu.VMEM((N_DEV, COL_SHARD, D), MXU_DTYPE),
                        pltpu.VMEM((2, D, COL_SHARD), MXU_DTYPE), pltpu.SemaphoreType.DMA((2,))],
        compiler_params=_cparams("arbitrary"),
    )(*dproj_pieces, dhm, hres, gain3, w_in_g)


_ADAM_C1 = 1.0 / (1.0 - ADAM_B1 ** ADAM_STEP)
_ADAM_C2 = 1.0 / (1.0 - ADAM_B2 ** ADAM_STEP)


def _adam_math(w, g, m, v):
    m = ADAM_B1 * m + (1.0 - ADAM_B1) * g
    v = ADAM_B2 * v + (1.0 - ADAM_B2) * (g * g)
    delta = -ADAM_LR * ((m * _ADAM_C1) / (jnp.sqrt(v * _ADAM_C2) + ADAM_EPS) + ADAM_WD * w)
    return delta, m, v


def _adamw_layers(parts0, parts1, w, m, v, name):
    _, rows, cols = w.shape
    tr = min(rows, (1 << 16) // cols)
    nt = rows // tr

    def body(p0_ref, p1_ref, w_ref, m_ref, v_ref, g_ref, d_ref, nm_ref, nv_ref):
        layer = pl.program_id(0)

        def run(p_ref):
            g = p_ref[0].astype(F32)
            for s in range(1, N_DEV):
                g = g + p_ref[s].astype(F32)
            delta, nm, nv = _adam_math(w_ref[...], g, m_ref[...], v_ref[...])
            g_ref[...] = g
            d_ref[...] = delta
            nm_ref[...] = nm
            nv_ref[...] = nv

        @pl.when(layer == 0)
        def _():
            run(p0_ref)

        @pl.when(layer == 1)
        def _():
            run(p1_ref)

    wspec = pl.BlockSpec((None, tr, cols), lambda l, i: (l, i, 0))
    return _pcall(
        body, name=name, grid=(2, nt),
        in_specs=[pl.BlockSpec((N_DEV, tr, cols), lambda l, i: (0, jnp.where(l == 0, i, nt - 1), 0)),
                  pl.BlockSpec((N_DEV, tr, cols), lambda l, i: (0, jnp.where(l == 1, i, 0), 0)),
                  wspec, wspec, wspec],
        out_specs=[wspec] * 4, out_shape=[SDS(w.shape, F32)] * 4,
        compiler_params=_cparams("arbitrary", "arbitrary"),
    )(parts0, parts1, w, m, v)


def _sum_slots(parts, name):
    def body(p_ref, o_ref):
        acc = p_ref[0]
        for s in range(1, N_DEV):
            acc = acc + p_ref[s]
        o_ref[...] = acc

    vmem = pl.BlockSpec(memory_space=pltpu.VMEM)
    return _pcall(body, name=name, out_shape=SDS(parts.shape[1:], F32), in_specs=[vmem], out_specs=vmem,
                  compiler_params=_cparams())(parts)


def _adamw_packed(g, w, m, v, name):
    def body(g_ref, w_ref, m_ref, v_ref, d_ref, nm_ref, nv_ref):
        delta, nm, nv = _adam_math(w_ref[...], g_ref[...], m_ref[...], v_ref[...])
        d_ref[...] = delta
        nm_ref[...] = nm
        nv_ref[...] = nv

    vmem = pl.BlockSpec(memory_space=pltpu.VMEM)
    return _pcall(body, name=name, out_shape=[SDS(g.shape, F32)] * 3, in_specs=[vmem] * 4, out_specs=[vmem] * 3,
                  compiler_params=_cparams())(g, w, m, v)


def _ssm_discretize(a_re, a_im, log_dt, b_re, b_im):
    dt = jnp.exp(log_dt)[:, None]
    mag = jnp.exp(a_re * dt)
    ang = a_im * dt
    ab_re, ab_im = mag * jnp.cos(ang), mag * jnp.sin(ang)
    xr, xi = ab_re - 1.0, ab_im
    den = a_re * a_re + a_im * a_im
    q_re = (xr * a_re + xi * a_im) / den
    q_im = (xi * a_re - xr * a_im) / den
    bb_re = q_re[..., None] * b_re - q_im[..., None] * b_im
    bb_im = q_re[..., None] * b_im + q_im[..., None] * b_re
    return ab_re, ab_im, bb_re, bb_im


def _block_diag_b(bb):
    m = jnp.einsum("sgnc,gh->sgchn", bb.reshape(N_SB, 8, N_STATE, GROUP_CH), jnp.eye(8, dtype=F32))
    return m.reshape(N_SB, 128, SB_STATES)


def _block_diag_b_t(dm):
    return jnp.einsum("sgchn,gh->sgnc", dm.reshape(N_SB, 8, GROUP_CH, 8, N_STATE),
                      jnp.eye(8, dtype=F32)).reshape(N_GROUPS, N_STATE, GROUP_CH)


def _block_diag_c(cc):
    m = jnp.einsum("sgcn,gh->sgnhc", cc.reshape(N_SB, 8, GROUP_CH, N_STATE), jnp.eye(8, dtype=F32))
    return m.reshape(N_SB, SB_STATES, 128)


def _block_diag_c_t(dm):
    return jnp.einsum("sgnhc,gh->sgcn", dm.reshape(N_SB, 8, N_STATE, 8, GROUP_CH),
                      jnp.eye(8, dtype=F32)).reshape(N_GROUPS, GROUP_CH, N_STATE)


def _ssm_tables(ab_re, ab_im, bb_re, bb_im, c_re, c_im, d_skip):
    pr, pi = ab_re.reshape(1, -1), ab_im.reshape(1, -1)
    cr, ci = pr, pi
    squares = []
    for _ in range(3):
        squares.append((cr, ci))
        pr, pi = (jnp.concatenate([pr, pr * cr - pi * ci], axis=0),
                  jnp.concatenate([pi, pr * ci + pi * cr], axis=0))
        cr, ci = cr * cr - ci * ci, 2.0 * cr * ci
    r = jnp.arange(SCAN_TILE)[:, None]
    fwd = [(jnp.where(r >= (1 << k), squares[k][0], 0.0), jnp.where(r >= (1 << k), squares[k][1], 0.0))
           for k in range(3)] + [(pr, pi)]
    rev = [(jnp.where(r < SCAN_TILE - (1 << k), squares[k][0], 0.0),
            jnp.where(r < SCAN_TILE - (1 << k), -squares[k][1], 0.0)) for k in range(3)] + [(pr[::-1], -pi[::-1])]
    table = lambda part: jnp.stack([e[part] for e in fwd + rev]).reshape(
        8, SCAN_TILE, N_SB, SB_STATES).transpose(2, 0, 1, 3)
    return dict(
        b_mat=jnp.concatenate([_block_diag_b(bb_re), _block_diag_b(bb_im)], axis=-1).astype(MXU_DTYPE),
        c_mat=jnp.concatenate([_block_diag_c(c_re), -_block_diag_c(c_im)], axis=1).astype(MXU_DTYPE),
        t_re=table(0), t_im=table(1),
        d_skip=d_skip.reshape(1, D_SSM))


def _rope_tables(rows):
    pos = (jnp.arange(rows, dtype=jnp.int32) - PAD_ROWS).astype(F32)
    inv_freq = 1.0 / (ROPE_THETA ** (jnp.arange(0, HEAD_DIM, 2, dtype=F32) / HEAD_DIM))
    ang = pos[:, None] * inv_freq[None, :]
    ang = jnp.concatenate([ang, ang, ang, ang], axis=-1)
    first_half = (jnp.arange(128) % HEAD_DIM) < HEAD_DIM // 2
    sin = jnp.sin(ang)
    return jnp.cos(ang), jnp.where(first_half, -sin, 0.0), jnp.where(first_half, 0.0, sin)


def _pack(arrays):
    flat = jnp.concatenate([a.reshape(-1).astype(F32) for a in arrays])
    pad = (-flat.shape[0]) % 1024
    return jnp.pad(flat, (0, pad)).reshape(-1, 128)


def _unpack(packed, like):
    flat = packed.reshape(-1)
    out, off = [], 0
    for a in like:
        n = math.prod(a.shape)
        out.append(flat[off:off + n].reshape(a.shape))
        off += n
    return out


BIG = ("w_in", "w_glu", "w_o_ssm", "w_o_attn", "w_out", "w_up", "w_down")
WEIGHTS = ("meta_tokens", "norm_mix_pre", "norm_mix_post", "norm_mlp_pre", "norm_mlp_post", "w_in",
           "ssm_a_re", "ssm_a_im", "ssm_log_dt", "ssm_b_re", "ssm_b_im", "ssm_c_re", "ssm_c_im", "ssm_d",
           "w_glu", "b_glu", "attn_sinks", "w_o_ssm", "w_o_attn", "w_out", "w_up", "w_down")
SMALL = tuple(n for n in WEIGHTS if n not in BIG)


def kernel(x, meta_tokens, norm_mix_pre, norm_mix_post, norm_mlp_pre, norm_mlp_post, w_in, ssm_a_re, ssm_a_im, ssm_log_dt, ssm_b_re, ssm_b_im, ssm_c_re, ssm_c_im, ssm_d, w_glu, b_glu, attn_sinks, w_o_ssm, w_o_attn, w_out, w_up, w_down, loss_target, m_meta_tokens, m_norm_mix_pre, m_norm_mix_post, m_norm_mlp_pre, m_norm_mlp_post, m_w_in, m_ssm_a_re, m_ssm_a_im, m_ssm_log_dt, m_ssm_b_re, m_ssm_b_im, m_ssm_c_re, m_ssm_c_im, m_ssm_d, m_w_glu, m_b_glu, m_attn_sinks, m_w_o_ssm, m_w_o_attn, m_w_out, m_w_up, m_w_down, v_meta_tokens, v_norm_mix_pre, v_norm_mix_post, v_norm_mlp_pre, v_norm_mlp_post, v_w_in, v_ssm_a_re, v_ssm_a_im, v_ssm_log_dt, v_ssm_b_re, v_ssm_b_im, v_ssm_c_re, v_ssm_c_im, v_ssm_d, v_w_glu, v_b_glu, v_attn_sinks, v_w_o_ssm, v_w_o_attn, v_w_out, v_w_up, v_w_down):
    args = locals()
    w = {n: args[n] for n in WEIGHTS}
    m = {n: args["m_" + n] for n in WEIGHTS}
    v = {n: args["v_" + n] for n in WEIGHTS}
    n_layers = w_in.shape[0]
    seq = x.shape[1]
    rows = seq + BLK
    my_slot = _slot(_mesh_pos())

    assert n_layers == 2
    xfer = {n: [w[n][l].astype(XFER_DTYPE) for l in range(n_layers)] for n in BIG}
    mixer_small = ("w_glu", "w_o_ssm", "w_o_attn", "w_out")
    meta_g, w_in_g0 = _exchange_by_sequencer([meta_tokens, xfer["w_in"][0]], True, 0, "gather_in0")
    mix0_g = _exchange_by_sequencer([xfer[n][0] for n in mixer_small], True, 1, "gather_mix0", after=[meta_g])
    meta_full = meta_g.transpose(1, 0, 2).reshape(N_META, D)

    def mixer_weights(w_glu_g, w_o_ssm_g, w_o_attn_g, w_out_g):
        return dict(w_glu=w_glu_g.reshape(D_SSM, D_SSM), w_o_ssm=w_o_ssm_g.transpose(1, 0, 2).reshape(D_SSM, D),
                    w_o_attn=w_o_attn_g.reshape(D_ATTN, D), w_out=w_out_g.reshape(D, D),
                    w_o_ssm_t=w_o_ssm_g.transpose(0, 2, 1).reshape(D, D_SSM),
                    w_o_attn_t=w_o_attn_g.reshape(D_ATTN, D).T, w_out_t=w_out_g.reshape(D, D).T)

    gathered = [dict(w_in=w_in_g0, **mixer_weights(*mix0_g)), {}]

    gains = {n: w[n].reshape(n_layers, 1, D) for n in ("norm_mix_pre", "norm_mix_post", "norm_mlp_pre", "norm_mlp_post")}
    b_glu3 = b_glu.reshape(n_layers, 1, D_SSM)
    cos, sin_a, sin_b = _rope_tables(rows)

    def ssm_setup(l):
        disc, disc_vjp = jax.vjp(_ssm_discretize, ssm_a_re[l], ssm_a_im[l], ssm_log_dt[l], ssm_b_re[l], ssm_b_im[l])
        return _ssm_tables(*disc, ssm_c_re[l], ssm_c_im[l], ssm_d[l]), disc_vjp

    hres = jnp.concatenate([jnp.zeros((PAD_ROWS, D), F32), meta_full, x[0]], axis=0)

    saved = []
    for l in range(n_layers):
        ssm, disc_vjp = ssm_setup(l)
        wl = gathered[l]
        u, gates, q, k, vv, h = _in_proj(hres, gains["norm_mix_pre"], wl["w_in"], cos, sin_a, sin_b, l)
        if l == 0:
            wl["w_up"], wl["w_down"] = _exchange_by_sequencer([xfer["w_up"][0], xfer["w_down"][0]], True, 2,
                                                              "gather_mlp0", after=[h])
        y, y_ssm, carry_in = _s5_fwd(u, ssm, wl["w_glu"], b_glu3, l)
        if l == 0:
            l1_g = _exchange_by_sequencer([xfer[n][1] for n in ("w_in",) + mixer_small + ("w_up", "w_down")], True, 3,
                                          "gather_l1", after=[y])
            gathered[1] = dict(w_in=l1_g[0], w_up=l1_g[5], w_down=l1_g[6], **mixer_weights(*l1_g[1:5]))
            last_exchange = l1_g[:1]
        y_attn = _attn_fwd(q, k, vv, attn_sinks, l)
        merged, mix, hres_mid = _merge_fwd(y_ssm, y_attn, gates, hres, wl["w_o_ssm"], wl["w_o_attn"], wl["w_out"],
                                           gains["norm_mix_post"], l)
        up, h2, ff, hres_out = _mlp_fwd(hres_mid, gains["norm_mlp_pre"], gains["norm_mlp_post"], wl["w_up"],
                                        wl["w_down"], l)
        saved.append(dict(ssm=ssm, disc_vjp=disc_vjp, hres=hres, u=u, gates=gates, h=h, q=q, k=k, v=vv, y=y, y_ssm=y_ssm,
                          carry_in=carry_in, y_attn=y_attn, merged=merged, mix=mix, hres_mid=hres_mid,
                          up=up, h2=h2, ff=ff))
        hres = hres_out

    dhres, loss_vec = _loss_and_grad(hres, loss_target[0])

    small_grads = {n: [None] * n_layers for n in SMALL if n != "meta_tokens"}
    recv_up, recv_down, recv_mix = [None] * n_layers, [None] * n_layers, [None] * n_layers
    for l in reversed(range(n_layers)):
        s = saved[l]
        wl = gathered[l]
        dff, dup, dhm, dg_mlp_post, dg_mlp_pre = _mlp_bwd(dhres, s["ff"], s["up"], s["hres_mid"], gains["norm_mlp_pre"],
                                                          gains["norm_mlp_post"], wl["w_up"], wl["w_down"], l)
        dw_up = _matmul_tn(s["h2"], dup, f"dw_up_l{l}", dev_major_cols=COL_SHARD)
        recv_up[l] = _exchange_by_sequencer([dw_up], False, 4 + 3 * l, f"scatter_up{l}", after=last_exchange)
        dw_down = _matmul_tn(s["up"], dff, f"dw_down_l{l}", a_fn=_relu_squared).reshape(N_DEV, COL_SHARD, D)
        recv_down[l] = _exchange_by_sequencer([dw_down], False, 5 + 3 * l, f"scatter_down{l}", after=recv_up[l])
        last_exchange = recv_down[l]
        dmix, da1, da2, dgs, dga, dy_ssm, dy_attn, dg_mix_post = _merge_bwd(
            dhm, s["mix"], s["y_ssm"], s["y_attn"], s["gates"], wl["w_o_ssm"], wl["w_o_attn"], wl["w_o_ssm_t"],
            wl["w_o_attn_t"], wl["w_out_t"], gains["norm_mix_post"], l)
        dw_out = _matmul_tn(s["merged"], dmix, f"dw_out_l{l}").reshape(N_DEV, D // N_DEV, D)
        dw_o_attn = _matmul_tn(s["y_attn"], da2, f"dw_o_attn_l{l}").reshape(N_DEV, D_ATTN // N_DEV, D)
        dw_o_ssm = _matmul_tn(s["y_ssm"], da1, f"dw_o_ssm_l{l}", dev_major_cols=D // N_DEV)
        if l == 0:
            recv_out0 = _exchange_by_sequencer([dw_o_ssm, dw_o_attn, dw_out], False, 11, "scatter_out0",
                                               after=last_exchange)
            last_exchange = recv_out0[:1]
        dq, dk, dv, dk_meta, dv_meta, dsink = _attn_bwd(s["q"], s["k"], s["v"], dy_attn, attn_sinks, l)
        dqkv = _rope_bwd(dq, dk, dv, dk_meta, dv_meta, cos, sin_a, sin_b, l)
        du, dw_glu, db_glu, dd_skip, db_mat, dc_mat, dab = _s5_bwd(dy_ssm, s["y"], s["u"], s["carry_in"], s["ssm"],
                                                                    wl["w_glu"], b_glu3, l)
        dproj = (du, dqkv, dgs, dga)
        dw_in = _dw_in(s["h"], dproj, l)
        dhres, dg_mix_pre = _in_bwd(dproj, dhm, s["hres"], gains["norm_mix_pre"], wl["w_in"], l)
        mix_parts = [dw_in, dw_glu.astype(XFER_DTYPE).reshape(N_DEV, D_SSM // N_DEV, D_SSM), dw_o_ssm, dw_o_attn, dw_out]
        if l > 0:
            recv_mix[l] = _exchange_by_sequencer(mix_parts, False, 6 + 3 * l, f"scatter_mix{l}", after=last_exchange)
            last_exchange = recv_mix[l][:1]

        dab = dab.reshape(N_SB, 2, SB_STATES)
        da_re, da_im, dlog_dt, db_re, db_im = s["disc_vjp"]((
            dab[:, 0].reshape(N_GROUPS, N_STATE), dab[:, 1].reshape(N_GROUPS, N_STATE),
            _block_diag_b_t(db_mat[..., :SB_STATES]), _block_diag_b_t(db_mat[..., SB_STATES:])))
        for name, val in (("norm_mix_pre", dg_mix_pre[0]), ("norm_mix_post", dg_mix_post[0]),
                          ("norm_mlp_pre", dg_mlp_pre[0]), ("norm_mlp_post", dg_mlp_post[0]),
                          ("ssm_a_re", da_re), ("ssm_a_im", da_im), ("ssm_log_dt", dlog_dt),
                          ("ssm_b_re", db_re), ("ssm_b_im", db_im),
                          ("ssm_c_re", _block_diag_c_t(dc_mat[:, :SB_STATES])),
                          ("ssm_c_im", -_block_diag_c_t(dc_mat[:, SB_STATES:])),
                          ("ssm_d", dd_skip.reshape(N_GROUPS, GROUP_CH)), ("b_glu", db_glu[0]),
                          ("attn_sinks", dsink[:, 0])):
            small_grads[name][l] = val

    grad_x = dhres[BLK:][None]
    small_names = [n for n in SMALL if n != "meta_tokens"]
    partial_small = [dhres[PAD_ROWS:BLK]] + [jnp.stack(small_grads[n]) for n in small_names] + [loss_vec[0, :1]]
    recv_in0, recv_glu0, small_parts = _exchange_by_sequencer(
        mix_parts[:2] + [_pack(partial_small)], [False, False, True], 6, "scatter_in0", after=last_exchange)
    recv_mix[0] = [recv_in0, recv_glu0] + recv_out0

    grads, delta, new_m, new_v = {}, {}, {}, {}

    def adamw_big(names, recv0, recv1):
        for n, p0, p1 in zip(names, recv0, recv1):
            grads[n], delta[n], new_m[n], new_v[n] = _adamw_layers(p0, p1, w[n], m[n], v[n], f"adamw_{n}")

    adamw_big(("w_up", "w_down"), recv_up[0] + recv_down[0], recv_up[1] + recv_down[1])
    summed = _unpack(_sum_slots(small_parts, "sum_small_grads"), partial_small)
    loss = summed[-1][0]
    grads.update(zip(small_names, summed[1:-1]))
    grads["meta_tokens"] = lax.dynamic_slice_in_dim(summed[0], my_slot * (D // N_DEV), D // N_DEV, axis=1)
    like = [w[n] for n in SMALL]
    d_s, m_s, v_s = _adamw_packed(_pack([grads[n] for n in SMALL]), _pack(like), _pack([m[n] for n in SMALL]),
                                  _pack([v[n] for n in SMALL]), "adamw_small")
    adamw_big(("w_in",) + mixer_small, recv_mix[0], recv_mix[1])
    for n, dd, mm, vs in zip(SMALL, _unpack(d_s, like), _unpack(m_s, like), _unpack(v_s, like)):
        delta[n], new_m[n], new_v[n] = dd, mm, vs

    return (loss, grad_x, *[grads[n] for n in WEIGHTS], *[delta[n] for n in WEIGHTS],
            *[new_m[n] for n in WEIGHTS], *[new_v[n] for n in WEIGHTS])
```

```python
import functools
import math

import jax
import jax.numpy as jnp
from jax import lax
from jax.experimental import pallas as pl
from jax.experimental.pallas import tpu as pltpu
from jax.experimental.pallas import tpu_sc as plsc

F32 = jnp.float32
MXU_DTYPE = jnp.bfloat16
XFER_DTYPE = MXU_DTYPE
_pcall = pl.pallas_call
SDS = jax.ShapeDtypeStruct

D = 1024
D_SSM = 512
D_ATTN = 1024
D_KV = 256
D_FF = 4096
D_IN = 4096
HEAD_DIM = 64
N_Q_HEADS = 16
N_KV_HEADS = 4
Q_PER_KV = 4
N_META = 16
BLK = 128
PAD_ROWS = BLK - N_META
N_GROUPS = 32
N_STATE = 64
GROUP_CH = 16
N_SB = 4
SB_STATES = 512
ROPE_THETA = 10000.0
ATTN_SCALE = HEAD_DIM ** -0.5
NEG_INF = -1e30
RMS_EPS = 1e-6
N_DEV = 8
COL_SHARD = 512

ADAM_LR = 0.001
ADAM_B1 = 0.9
ADAM_B2 = 0.999
ADAM_EPS = 1e-08
ADAM_WD = 0.01
ADAM_STEP = 10

VMEM_LIMIT = 56 * 1024 * 1024
MESH_AXES = ("x", "y", "c")

_NT = (((1,), (1,)), ((), ()))
_TN = (((0,), (0,)), ((), ()))


def _cparams(*sem):
    return pltpu.CompilerParams(dimension_semantics=tuple(sem) if sem else None,
                                vmem_limit_bytes=VMEM_LIMIT)


def _row_tile(rows, cap=640):
    for t in (1664, 640, 512, 320, 256, 128):
        if t <= cap and rows % t == 0:
            return t
    raise ValueError(f"unsupported row count {rows}")


def _dot(a, b):
    return jnp.dot(a, b, preferred_element_type=F32)


def _dot_nt(a, b):
    return lax.dot_general(a, b, _NT, preferred_element_type=F32)


def _dot_tn(a, b):
    return lax.dot_general(a, b, _TN, preferred_element_type=F32)


def _sigmoid(x):
    return 1.0 / (1.0 + jnp.exp(-x))


_GELU_C = math.sqrt(2.0 / math.pi)


def _gelu_parts(y):
    t = jnp.tanh(_GELU_C * (y + 0.044715 * (y * y * y)))
    return 0.5 * y * (1.0 + t), t


def _gelu_grad(y, t):
    return 0.5 * (1.0 + t) + 0.5 * y * (1.0 - t * t) * (_GELU_C * (1.0 + 0.134145 * (y * y)))


def _rms_fwd(x, gain):
    r = lax.rsqrt(jnp.mean(x * x, axis=-1, keepdims=True) + RMS_EPS)
    return (x * r) * gain


def _rms_bwd(x, gain, dout):
    r = lax.rsqrt(jnp.mean(x * x, axis=-1, keepdims=True) + RMS_EPS)
    xh = x * r
    dxh = dout * gain
    dx = r * (dxh - xh * jnp.mean(dxh * xh, axis=-1, keepdims=True))
    return dx, jnp.sum(dout * xh, axis=0, keepdims=True)


def _mesh_pos():
    return lax.axis_index("x"), lax.axis_index("y"), lax.axis_index("c")


def _peer(pos, d):
    x, y, c = pos
    return (1 - x if d & 4 else x, 1 - y if d & 2 else y, 1 - c if d & 1 else c)


def _slot(pos):
    return 4 * pos[0] + 2 * pos[1] + pos[2]


def _exchange_copy(gather, src_ref, land_ref, sems, k, d, me, send_side):
    peer = _peer(me, d)
    sender = me if send_side else peer
    src = src_ref if gather else src_ref.at[_slot(peer) if send_side else _slot(me)]
    return pltpu.make_async_remote_copy(
        src_ref=src, dst_ref=land_ref.at[_slot(sender)],
        send_sem=sems[0].at[k * (N_DEV - 1) + d - 1], recv_sem=sems[1].at[k * (N_DEV - 1) + d - 1],
        device_id=peer, device_id_type=pl.DeviceIdType.MESH)


def _exchange_by_sequencer(srcs, gather, collective_id, name, after=()):
    n = len(srcs)
    flags = [gather] * n if isinstance(gather, bool) else list(gather)
    land_types = [SDS(((N_DEV,) + s.shape) if g else s.shape, s.dtype) for s, g in zip(srcs, flags)]

    def body(*refs):
        src_refs = refs[:n]
        land_refs = refs[n + len(after):2 * n + len(after)]
        sems = refs[2 * n + len(after):2 * n + len(after) + 2]
        local_sems = refs[2 * n + len(after) + 2]
        me = _mesh_pos()
        barrier = pltpu.get_barrier_semaphore()
        for d in range(1, N_DEV):
            pl.semaphore_signal(barrier, inc=1, device_id=_peer(me, d), device_id_type=pl.DeviceIdType.MESH)
        pl.semaphore_wait(barrier, N_DEV - 1)
        own = [pltpu.make_async_copy(src_refs[k] if flags[k] else src_refs[k].at[_slot(me)],
                                     land_refs[k].at[_slot(me)], local_sems.at[k]) for k in range(n)]
        for cp in own:
            cp.start()
        for k in range(n):
            for d in range(1, N_DEV):
                _exchange_copy(flags[k], src_refs[k], land_refs[k], sems, k, d, me, True).start()
        for cp in own:
            cp.wait()
        for k in range(n):
            for d in range(1, N_DEV):
                _exchange_copy(flags[k], src_refs[k], land_refs[k], sems, k, d, me, True).wait_send()
        for k in range(n):
            for d in range(1, N_DEV):
                _exchange_copy(flags[k], src_refs[k], land_refs[k], sems, k, d, me, False).wait_recv()

    sem_type = pltpu.SemaphoreType.DMA((n * (N_DEV - 1),))
    return pl.kernel(
        body, out_type=land_types, mesh=plsc.ScalarSubcoreMesh(axis_name="sequencer", num_cores=1), name=name,
        scratch_types=(sem_type, sem_type, pltpu.SemaphoreType.DMA((n,))),
        compiler_params=pltpu.CompilerParams(collective_id=collective_id),
    )(*srcs, *after)


def _load_resident(w_hbm, w_scr, sems, first_step):
    @pl.when(first_step)
    def _():
        copies = [pltpu.make_async_copy(w_hbm.at[s], w_scr.at[s], sems.at[s]) for s in range(N_DEV)]
        for cp in copies:
            cp.start()
        for cp in copies:
            cp.wait()


def _load_resident_transposed(w_hbm, w_scr, stage, sems, first_step):
    @pl.when(first_step)
    def _():
        copies = [pltpu.make_async_copy(w_hbm.at[s], stage.at[s % 2], sems.at[s % 2]) for s in range(N_DEV)]
        copies[0].start()
        for s in range(N_DEV):
            if s + 1 < N_DEV:
                copies[s + 1].start()
            copies[s].wait()
            w_scr[s] = stage[s % 2].T


def _rope_lanes(t, cos, sin_a, sin_b):
    return t * cos + pltpu.roll(t, 96, 1) * sin_a + pltpu.roll(t, 32, 1) * sin_b


def _in_proj(hres, gain3, w_in_g, cos, sin_a, sin_b, layer):
    rows = hres.shape[0]
    tm = _row_tile(rows, 320)

    def body(x_ref, g_ref, w_hbm, c_ref, a_ref, b_ref, u_ref, gate_ref, q_ref, k_ref, v_ref, h_ref, w_scr, w_sem):
        _load_resident(w_hbm, w_scr, w_sem, pl.program_id(0) == 0)
        hn = _rms_fwd(x_ref[...], g_ref[...]).astype(MXU_DTYPE)
        h_ref[...] = hn
        c, a, b = c_ref[...], a_ref[...], b_ref[...]
        u_ref[...] = _dot(hn, w_scr[0])
        for shard in (1, 2):
            res = _dot(hn, w_scr[shard])
            for t in range(4):
                lanes = slice(t * 128, (t + 1) * 128)
                out = slice((shard - 1) * COL_SHARD + t * 128, (shard - 1) * COL_SHARD + (t + 1) * 128)
                q_ref[:, out] = (_rope_lanes(res[:, lanes], c, a, b) * ATTN_SCALE).astype(MXU_DTYPE)
        res = _dot(hn, w_scr[3])
        for t in range(2):
            lanes = slice(t * 128, (t + 1) * 128)
            k_ref[:, lanes] = _rope_lanes(res[:, lanes], c, a, b).astype(MXU_DTYPE)
        v_ref[...] = res[:, D_KV:].astype(MXU_DTYPE)
        for shard in range(4, N_DEV):
            gate_ref[:, (shard - 4) * COL_SHARD:(shard - 3) * COL_SHARD] = _dot(hn, w_scr[shard])

    tab = pl.BlockSpec((tm, 128), lambda i: (i, 0))
    kv = pl.BlockSpec((tm, D_KV), lambda i: (i, 0))
    row_d = pl.BlockSpec((tm, D), lambda i: (i, 0))
    return _pcall(
        body, name=f"in_proj_l{layer}", grid=(rows // tm,),
        in_specs=[row_d, pl.BlockSpec((None, 1, D), lambda i: (layer, 0, 0)),
                  pl.BlockSpec(memory_space=pl.ANY), tab, tab, tab],
        out_specs=[pl.BlockSpec((tm, D_SSM), lambda i: (i, 0)), pl.BlockSpec((tm, 2 * D), lambda i: (i, 0)),
                   row_d, kv, kv, row_d],
        out_shape=[SDS((rows, D_SSM), F32), SDS((rows, 2 * D), F32), SDS((rows, D_ATTN), MXU_DTYPE),
                   SDS((rows, D_KV), MXU_DTYPE), SDS((rows, D_KV), MXU_DTYPE), SDS((rows, D), MXU_DTYPE)],
        scratch_shapes=[pltpu.VMEM((N_DEV, D, COL_SHARD), MXU_DTYPE), pltpu.SemaphoreType.DMA((N_DEV,))],
        compiler_params=_cparams("arbitrary"),
    )(hres, gain3, w_in_g, cos, sin_a, sin_b)


SCAN_TILE = 8


def _scan_tiles(x_ref, out_ref, tre_ref, tim_ref, sb, t_r, t_i, reverse, prev_ref=None):
    base = 4 if reverse else 0
    n_tiles = BLK // SCAN_TILE
    row = lax.broadcasted_iota(jnp.int32, (SCAN_TILE, SB_STATES), 0)
    for j in (range(n_tiles - 1, -1, -1) if reverse else range(n_tiles)):
        rows = slice(SCAN_TILE * j, SCAN_TILE * (j + 1))
        xr = x_ref[rows, :SB_STATES]
        xi = x_ref[rows, SB_STATES:]
        for k in range(3):
            shift = SCAN_TILE - (1 << k) if reverse else (1 << k)
            rr = pltpu.roll(xr, shift, 0)
            ri = pltpu.roll(xi, shift, 0)
            ar = tre_ref[sb, base + k]
            ai = tim_ref[sb, base + k]
            xr, xi = xr + (ar * rr - ai * ri), xi + (ar * ri + ai * rr)
        pr = tre_ref[sb, base + 3]
        pi = tim_ref[sb, base + 3]
        xr, xi = xr + (pr * t_r - pi * t_i), xi + (pr * t_i + pi * t_r)
        out_ref[rows, :SB_STATES] = xr
        out_ref[rows, SB_STATES:] = xi
        if prev_ref is not None:
            prev_ref[rows, :SB_STATES] = jnp.where(row == 0, t_r, pltpu.roll(xr, 1, 0))
            prev_ref[rows, SB_STATES:] = jnp.where(row == 0, t_i, pltpu.roll(xi, 1, 0))
        edge = slice(0, 1) if reverse else slice(SCAN_TILE - 1, SCAN_TILE)
        t_r, t_i = xr[edge], xi[edge]
    return t_r, t_i


def _s5_fwd(u, ssm, w_glu, b_glu3, layer):
    rows = u.shape[0]
    n_chunks = rows // BLK
    b_mat, c_mat, t_re, t_im, d_skip = (ssm[k] for k in ("b_mat", "c_mat", "t_re", "t_im", "d_skip"))

    def body(u_ref, bm_ref, cm_ref, tre_ref, tim_ref, d_ref, wg_ref, bg_ref,
             y_ref, ys_ref, cin_ref, carry, bu_scr, s_scr):
        @pl.when(pl.program_id(0) == 0)
        def _():
            carry[...] = jnp.zeros_like(carry)

        cin_ref[...] = carry[...]
        u = u_ref[...]
        for sb in range(N_SB):
            cols = slice(sb * 128, (sb + 1) * 128)
            u_sb = u[:, cols]
            bu_scr[sb] = _dot(u_sb.astype(MXU_DTYPE), bm_ref[sb])
            t_r, t_i = _scan_tiles(bu_scr.at[sb], s_scr.at[sb], tre_ref, tim_ref, sb,
                                   carry[2 * sb:2 * sb + 1, :], carry[2 * sb + 1:2 * sb + 2, :], False)
            carry[2 * sb:2 * sb + 1, :] = t_r
            carry[2 * sb + 1:2 * sb + 2, :] = t_i
            y_ref[:, cols] = _dot(s_scr[sb].astype(MXU_DTYPE), cm_ref[sb]) + d_ref[:, cols] * u_sb
        z, _ = _gelu_parts(y_ref[...])
        gl = _dot(z.astype(MXU_DTYPE), wg_ref[...]) + bg_ref[...]
        ys_ref[...] = (z * _sigmoid(gl)).astype(MXU_DTYPE)

    full = lambda shape: pl.BlockSpec(shape, lambda j: (0,) * len(shape))
    of_layer = lambda shape: pl.BlockSpec((None,) + shape, lambda j: (layer,) + (0,) * len(shape))
    return _pcall(
        body, name=f"s5_fwd_l{layer}", grid=(n_chunks,),
        in_specs=[pl.BlockSpec((BLK, D_SSM), lambda j: (j, 0)),
                  of_layer((N_SB, 128, 2 * SB_STATES)), of_layer((N_SB, 2 * SB_STATES, 128)),
                  of_layer((N_SB, 8, SCAN_TILE, SB_STATES)), of_layer((N_SB, 8, SCAN_TILE, SB_STATES)),
                  of_layer((1, D_SSM)), full((D_SSM, D_SSM)),
                  pl.BlockSpec((None, 1, D_SSM), lambda j: (layer, 0, 0))],
        out_specs=[pl.BlockSpec((BLK, D_SSM), lambda j: (j, 0)), pl.BlockSpec((BLK, D_SSM), lambda j: (j, 0)),
                   pl.BlockSpec((None, 8, SB_STATES), lambda j: (j, 0, 0))],
        out_shape=[SDS((rows, D_SSM), F32), SDS((rows, D_SSM), MXU_DTYPE), SDS((n_chunks, 8, SB_STATES), F32)],
        scratch_shapes=[pltpu.VMEM((8, SB_STATES), F32), pltpu.VMEM((N_SB, BLK, 2 * SB_STATES), F32),
                        pltpu.VMEM((N_SB, BLK, 2 * SB_STATES), F32)],
        compiler_params=_cparams("arbitrary"),
    )(u, b_mat, c_mat, t_re, t_im, d_skip, w_glu, b_glu3)


def _attn_mask(i):
    row = lax.broadcasted_iota(jnp.int32, (BLK, 3 * BLK), 0) + i * BLK
    col = lax.broadcasted_iota(jnp.int32, (BLK, 3 * BLK), 1)
    seg = jnp.right_shift(col, 7)
    c = jnp.bitwise_and(col, BLK - 1)
    kidx = c + (i + seg - 2) * BLK
    ok_meta = (seg == 0) & (c >= PAD_ROWS) & (row - c >= BLK)
    ok_win = (seg > 0) & (kidx >= PAD_ROWS) & (kidx <= row) & (row - kidx < BLK)
    return jnp.where(ok_meta | ok_win, 0.0, NEG_INF)


def _head_lanes(h):
    return slice(h * HEAD_DIM, (h + 1) * HEAD_DIM)


def _group_rows(ref, kvh):
    return jnp.concatenate([ref[:, _head_lanes(kvh * Q_PER_KV + g)] for g in range(Q_PER_KV)], axis=0)


def _group_bias(bias, sink_ref, layer, kvh):
    first_col = lax.broadcasted_iota(jnp.int32, (BLK, BLK), 1) == 0
    slabs = []
    for g in range(Q_PER_KV):
        first = jnp.where(first_col, sink_ref[layer, kvh * Q_PER_KV + g], bias[:, :BLK])
        slabs.append(jnp.concatenate([first, bias[:, BLK:]], axis=1))
    return jnp.concatenate(slabs, axis=0)


def _attn_probs(q4, k3, bias4):
    s = _dot_nt(q4, k3) + bias4
    e = jnp.exp(s - jnp.max(s, axis=-1, keepdims=True))
    return e * (1.0 / jnp.sum(e, axis=-1, keepdims=True))


def _attn_fwd(q, k, v, sinks, layer):
    rows = q.shape[0]
    n_blk = rows // BLK

    def body(sink_ref, q_ref, km_ref, kp_ref, kc_ref, vm_ref, vp_ref, vc_ref, o_ref):
        bias = _attn_mask(pl.program_id(0))
        for kvh in range(N_KV_HEADS):
            lanes = _head_lanes(kvh)
            k3 = jnp.concatenate([km_ref[:, lanes], kp_ref[:, lanes], kc_ref[:, lanes]], axis=0)
            v3 = jnp.concatenate([vm_ref[:, lanes], vp_ref[:, lanes], vc_ref[:, lanes]], axis=0)
            p = _attn_probs(_group_rows(q_ref, kvh), k3, _group_bias(bias, sink_ref, layer, kvh))
            o4 = _dot(p.astype(MXU_DTYPE), v3).astype(MXU_DTYPE)
            for g in range(Q_PER_KV):
                o_ref[:, _head_lanes(kvh * Q_PER_KV + g)] = o4[g * BLK:(g + 1) * BLK]

    kv_meta = pl.BlockSpec((BLK, D_KV), lambda i: (0, 0))
    kv_prev = pl.BlockSpec((BLK, D_KV), lambda i: (jnp.maximum(i - 1, 0), 0))
    kv_cur = pl.BlockSpec((BLK, D_KV), lambda i: (i, 0))
    return _pcall(
        body, name=f"attn_fwd_l{layer}", grid=(n_blk,),
        in_specs=[pl.BlockSpec(memory_space=pltpu.SMEM),
                  pl.BlockSpec((BLK, D_ATTN), lambda i: (i, 0)),
                  kv_meta, kv_prev, kv_cur, kv_meta, kv_prev, kv_cur],
        out_specs=pl.BlockSpec((BLK, D_ATTN), lambda i: (i, 0)),
        out_shape=SDS((rows, D_ATTN), MXU_DTYPE),
        compiler_params=_cparams("parallel"),
    )(sinks, q, k, k, k, v, v, v)


def _merge_fwd(y_ssm, y_attn, gates, hres, w_o_ssm, w_o_attn, w_out, gain3, layer):
    rows = hres.shape[0]
    tm = _row_tile(rows, 320)

    def body(ys_ref, ya_ref, gs_ref, ga_ref, x_ref, wos_ref, woa_ref, wout_ref, g_ref,
             mg_ref, mix_ref, out_ref):
        a1 = _dot(ys_ref[...], wos_ref[...])
        a2 = _dot(ya_ref[...], woa_ref[...])
        merged = (_sigmoid(gs_ref[...]) * a1 + _sigmoid(ga_ref[...]) * a2).astype(MXU_DTYPE)
        mg_ref[...] = merged
        mix = _dot(merged, wout_ref[...])
        mix_ref[...] = mix
        out_ref[...] = x_ref[...] + _rms_fwd(mix, g_ref[...])

    row_d = pl.BlockSpec((tm, D), lambda i: (i, 0))
    full = lambda shape: pl.BlockSpec(shape, lambda i: (0,) * len(shape))
    return _pcall(
        body, name=f"merge_fwd_l{layer}", grid=(rows // tm,),
        in_specs=[pl.BlockSpec((tm, D_SSM), lambda i: (i, 0)), row_d,
                  row_d, pl.BlockSpec((tm, D), lambda i: (i, 1)), row_d,
                  full((D_SSM, D)), full((D_ATTN, D)), full((D, D)),
                  pl.BlockSpec((None, 1, D), lambda i: (layer, 0, 0))],
        out_specs=[row_d, row_d, row_d],
        out_shape=[SDS((rows, D), MXU_DTYPE), SDS((rows, D), F32), SDS((rows, D), F32)],
        compiler_params=_cparams("parallel"),
    )(y_ssm, y_attn, gates, gates, hres, w_o_ssm, w_o_attn, w_out, gain3)


def _mlp_fwd(hres, gain_pre3, gain_post3, w_up_g, w_down_g, layer):
    rows = hres.shape[0]
    tm = _row_tile(rows, 320)

    def body(x_ref, gp_ref, gq_ref, wu_hbm, wd_hbm, up_ref, h_ref, ff_ref, out_ref,
             act_scr, wu_scr, wd_scr, wu_sem, wd_sem):
        first = pl.program_id(0) == 0
        _load_resident(wu_hbm, wu_scr, wu_sem, first)
        _load_resident(wd_hbm, wd_scr, wd_sem, first)
        hn = _rms_fwd(x_ref[...], gp_ref[...]).astype(MXU_DTYPE)
        h_ref[...] = hn
        for kf in range(N_DEV):
            cols = slice(kf * COL_SHARD, (kf + 1) * COL_SHARD)
            up = _dot(hn, wu_scr[kf])
            up_ref[:, cols] = up.astype(MXU_DTYPE)
            r = jnp.maximum(up, 0.0)
            act_scr[:, cols] = (r * r).astype(MXU_DTYPE)
        ff = _dot(act_scr[...], wd_scr[...].reshape(D_FF, D))
        ff_ref[...] = ff
        out_ref[...] = x_ref[...] + _rms_fwd(ff, gq_ref[...])

    row_d = pl.BlockSpec((tm, D), lambda i: (i, 0))
    gain = pl.BlockSpec((None, 1, D), lambda i: (layer, 0, 0))
    return _pcall(
        body, name=f"mlp_fwd_l{layer}", grid=(rows // tm,),
        in_specs=[row_d, gain, gain, pl.BlockSpec(memory_space=pl.ANY), pl.BlockSpec(memory_space=pl.ANY)],
        out_specs=[pl.BlockSpec((tm, D_FF), lambda i: (i, 0)), row_d, row_d, row_d],
        out_shape=[SDS((rows, D_FF), MXU_DTYPE), SDS((rows, D), MXU_DTYPE), SDS((rows, D), F32), SDS((rows, D), F32)],
        scratch_shapes=[pltpu.VMEM((tm, D_FF), MXU_DTYPE),
                        pltpu.VMEM((N_DEV, D, COL_SHARD), MXU_DTYPE), pltpu.VMEM((N_DEV, COL_SHARD, D), MXU_DTYPE),
                        pltpu.SemaphoreType.DMA((N_DEV,)), pltpu.SemaphoreType.DMA((N_DEV,))],
        compiler_params=_cparams("arbitrary"),
    )(hres, gain_pre3, gain_post3, w_up_g, w_down_g)


def _loss_and_grad(hres, target):
    rows = hres.shape[0]
    n_blk = rows // BLK

    def body(y_ref, t_ref, dy_ref, loss_ref):
        i = pl.program_id(0)

        @pl.when(i == 0)
        def _():
            dy_ref[...] = jnp.zeros_like(dy_ref)
            loss_ref[...] = jnp.zeros_like(loss_ref)

        @pl.when(i > 0)
        def _():
            err = y_ref[...] - t_ref[...]
            dy_ref[...] = err * (1.0 / D)
            loss_ref[...] += jnp.sum(err * err) * (0.5 / D)

    return _pcall(
        body, name="loss", grid=(n_blk,),
        in_specs=[pl.BlockSpec((BLK, D), lambda i: (i, 0)),
                  pl.BlockSpec((BLK, D), lambda i: (jnp.maximum(i - 1, 0), 0))],
        out_specs=[pl.BlockSpec((BLK, D), lambda i: (i, 0)), pl.BlockSpec((1, 128), lambda i: (0, 0))],
        out_shape=[SDS((rows, D), F32), SDS((1, 128), F32)],
        compiler_params=_cparams("arbitrary"),
    )(hres, target)


def _relu_squared(up):
    r = jnp.maximum(up.astype(F32), 0.0)
    return (r * r).astype(MXU_DTYPE)


def _matmul_tn(a, b, name, dev_major_cols=None, a_fn=None):
    rows, ka = a.shape
    n = b.shape[1]
    ta = min(ka, 1024)
    tn = 1024 if n % 1024 == 0 else 512
    tr = _row_tile(rows, 1664)
    n_r = rows // tr

    def body(a_ref, b_ref, o_ref, acc):
        r = pl.program_id(2)

        @pl.when(r == 0)
        def _():
            acc[...] = jnp.zeros_like(acc)

        a_blk = a_ref[...] if a_fn is None else a_fn(a_ref[...])
        acc[...] += _dot_tn(a_blk, b_ref[...])

        @pl.when(r == n_r - 1)
        def _():
            if dev_major_cols is None:
                o_ref[...] = acc[...].astype(XFER_DTYPE)
            else:
                for s in range(tn // dev_major_cols):
                    o_ref[s] = acc[:, s * dev_major_cols:(s + 1) * dev_major_cols].astype(XFER_DTYPE)

    if dev_major_cols is None:
        out_spec = pl.BlockSpec((ta, tn), lambda i, j, r: (i, j))
        out_shape = SDS((ka, n), XFER_DTYPE)
    else:
        w = dev_major_cols
        out_spec = pl.BlockSpec((tn // w, ta, w), lambda i, j, r: (j, i, 0))
        out_shape = SDS((n // w, ka, w), XFER_DTYPE)
    return _pcall(
        body, name=name, grid=(ka // ta, n // tn, n_r),
        in_specs=[pl.BlockSpec((tr, ta), lambda i, j, r: (r, i)), pl.BlockSpec((tr, tn), lambda i, j, r: (r, j))],
        out_specs=out_spec, out_shape=out_shape,
        scratch_shapes=[pltpu.VMEM((ta, tn), F32)],
        compiler_params=_cparams("parallel", "parallel", "arbitrary"),
    )(a, b)


def _dw_in(h, dproj_pieces, layer):
    rows = h.shape[0]
    tr = _row_tile(rows, 1664)
    n_r = rows // tr

    def body(h_ref, du_ref, dqkv_ref, dgs_ref, dga_ref, o_ref, acc):
        j = pl.program_id(0)
        r = pl.program_id(1)

        @pl.when(r == 0)
        def _():
            acc[...] = jnp.zeros_like(acc)

        for piece_ref, (first, count) in zip((du_ref, dqkv_ref, dgs_ref, dga_ref), DPROJ_PIECES):
            @pl.when((j >= first) & (j < first + count))
            def _():
                acc[...] += _dot_tn(h_ref[...], piece_ref[...])

        @pl.when(r == n_r - 1)
        def _():
            o_ref[...] = acc[...].astype(XFER_DTYPE)

    def piece_spec(first, count):
        def index(j, r):
            mine = (j >= first) & (j < first + count)
            return jnp.where(mine, r, 0), jnp.clip(j - first, 0, count - 1)
        return pl.BlockSpec((tr, COL_SHARD), index)

    return _pcall(
        body, name=f"dw_in_l{layer}", grid=(N_DEV, n_r),
        in_specs=[pl.BlockSpec((tr, D), lambda j, r: (r, 0))] + [piece_spec(*p) for p in DPROJ_PIECES],
        out_specs=pl.BlockSpec((None, D, COL_SHARD), lambda j, r: (j, 0, 0)),
        out_shape=SDS((N_DEV, D, COL_SHARD), XFER_DTYPE),
        scratch_shapes=[pltpu.VMEM((D, COL_SHARD), F32)],
        compiler_params=_cparams("arbitrary", "arbitrary"),
    )(h, *dproj_pieces)


def _mlp_bwd(dout, ff, up, hres_mid, gain_pre3, gain_post3, w_up_g, w_down_g, layer):
    rows = dout.shape[0]
    tm = _row_tile(rows, 320)

    def body(do_ref, ff_ref, up_ref, x_ref, gp_ref, gq_ref, wu_hbm, wd_hbm,
             dff_ref, dup_ref, dx_ref, dgq_ref, dgp_ref, wut_scr, wdt_scr, wu_stage, wd_stage, wu_sem, wd_sem):
        i = pl.program_id(0)
        _load_resident_transposed(wu_hbm, wut_scr, wu_stage, wu_sem, i == 0)
        _load_resident_transposed(wd_hbm, wdt_scr, wd_stage, wd_sem, i == 0)

        @pl.when(i == 0)
        def _():
            dgq_ref[...] = jnp.zeros_like(dgq_ref)
            dgp_ref[...] = jnp.zeros_like(dgp_ref)

        dff, dg = _rms_bwd(ff_ref[...], gq_ref[...], do_ref[...])
        dgq_ref[...] += dg
        dffb = dff.astype(MXU_DTYPE)
        dff_ref[...] = dffb
        for kf in range(N_DEV):
            cols = slice(kf * COL_SHARD, (kf + 1) * COL_SHARD)
            dact = _dot(dffb, wdt_scr[kf])
            dup_ref[:, cols] = (dact * (2.0 * jnp.maximum(up_ref[:, cols].astype(F32), 0.0))).astype(MXU_DTYPE)
        dh = _dot(dup_ref[...], wut_scr[...].reshape(D_FF, D))
        dx, dg = _rms_bwd(x_ref[...], gp_ref[...], dh)
        dgp_ref[...] += dg
        dx_ref[...] = do_ref[...] + dx

    row_d = pl.BlockSpec((tm, D), lambda i: (i, 0))
    row_ff = pl.BlockSpec((tm, D_FF), lambda i: (i, 0))
    gain = pl.BlockSpec((None, 1, D), lambda i: (layer, 0, 0))
    dgain = pl.BlockSpec((1, D), lambda i: (0, 0))
    return _pcall(
        body, name=f"mlp_bwd_l{layer}", grid=(rows // tm,),
        in_specs=[row_d, row_d, row_ff, row_d, gain, gain,
                  pl.BlockSpec(memory_space=pl.ANY), pl.BlockSpec(memory_space=pl.ANY)],
        out_specs=[row_d, row_ff, row_d, dgain, dgain],
        out_shape=[SDS((rows, D), MXU_DTYPE), SDS((rows, D_FF), MXU_DTYPE), SDS((rows, D), F32),
                   SDS((1, D), F32), SDS((1, D), F32)],
        scratch_shapes=[pltpu.VMEM((N_DEV, COL_SHARD, D), MXU_DTYPE), pltpu.VMEM((N_DEV, D, COL_SHARD), MXU_DTYPE),
                        pltpu.VMEM((2, D, COL_SHARD), MXU_DTYPE), pltpu.VMEM((2, COL_SHARD, D), MXU_DTYPE),
                        pltpu.SemaphoreType.DMA((2,)), pltpu.SemaphoreType.DMA((2,))],
        compiler_params=_cparams("arbitrary"),
    )(dout, ff, up, hres_mid, gain_pre3, gain_post3, w_up_g, w_down_g)


def _merge_bwd(dhm, mix, y_ssm, y_attn, gates, w_o_ssm, w_o_attn, w_o_ssm_t, w_o_attn_t, w_out_t, gain3, layer):
    rows = dhm.shape[0]
    tm = _row_tile(rows, 320)

    def body(dh_ref, mix_ref, ys_ref, ya_ref, gs_ref, ga_ref, wos_ref, woa_ref, wost_ref, woat_ref, woutt_ref, g_ref,
             dmix_ref, da1_ref, da2_ref, dgs_ref, dga_ref, dys_ref, dya_ref, dg_ref):
        @pl.when(pl.program_id(0) == 0)
        def _():
            dg_ref[...] = jnp.zeros_like(dg_ref)

        dmix, dg = _rms_bwd(mix_ref[...], g_ref[...], dh_ref[...])
        dg_ref[...] += dg
        dmixb = dmix.astype(MXU_DTYPE)
        dmix_ref[...] = dmixb
        dmerged = _dot(dmixb, woutt_ref[...])
        sg_s = _sigmoid(gs_ref[...])
        sg_a = _sigmoid(ga_ref[...])
        da1 = (dmerged * sg_s).astype(MXU_DTYPE)
        da2 = (dmerged * sg_a).astype(MXU_DTYPE)
        da1_ref[...] = da1
        da2_ref[...] = da2
        a1 = _dot(ys_ref[...], wos_ref[...])
        a2 = _dot(ya_ref[...], woa_ref[...])
        dgs_ref[...] = (dmerged * a1 * (sg_s * (1.0 - sg_s))).astype(MXU_DTYPE)
        dga_ref[...] = (dmerged * a2 * (sg_a * (1.0 - sg_a))).astype(MXU_DTYPE)
        dys_ref[...] = _dot(da1, wost_ref[...])
        dya_ref[...] = _dot(da2, woat_ref[...])

    row_d = pl.BlockSpec((tm, D), lambda i: (i, 0))
    full = lambda shape: pl.BlockSpec(shape, lambda i: (0,) * len(shape))
    return _pcall(
        body, name=f"merge_bwd_l{layer}", grid=(rows // tm,),
        in_specs=[row_d, row_d, pl.BlockSpec((tm, D_SSM), lambda i: (i, 0)), row_d,
                  row_d, pl.BlockSpec((tm, D), lambda i: (i, 1)),
                  full((D_SSM, D)), full((D_ATTN, D)), full((D, D_SSM)), full((D, D_ATTN)), full((D, D)),
                  pl.BlockSpec((None, 1, D), lambda i: (layer, 0, 0))],
        out_specs=[row_d, row_d, row_d, row_d, row_d, pl.BlockSpec((tm, D_SSM), lambda i: (i, 0)), row_d,
                   pl.BlockSpec((1, D), lambda i: (0, 0))],
        out_shape=[SDS((rows, D), MXU_DTYPE)] * 5 + [SDS((rows, D_SSM), F32), SDS((rows, D_ATTN), F32),
                                                      SDS((1, D), F32)],
        compiler_params=_cparams("arbitrary"),
    )(dhm, mix, y_ssm, y_attn, gates, gates, w_o_ssm, w_o_attn, w_o_ssm_t, w_o_attn_t, w_out_t, gain3)


def _attn_bwd(q, k, v, d_out, sinks, layer):
    rows = q.shape[0]
    n_blk = rows // BLK
    last = n_blk - 1

    def body(sink_ref, q_ref, km_ref, kp_ref, kc_ref, vm_ref, vp_ref, vc_ref, do_ref,
             dq_ref, dk_ref, dv_ref, dkm_ref, dvm_ref, ds_ref, dk_carry, dv_carry):
        i = pl.program_id(0)

        @pl.when(i == 0)
        def _():
            dkm_ref[...] = jnp.zeros_like(dkm_ref)
            dvm_ref[...] = jnp.zeros_like(dvm_ref)
            ds_ref[...] = jnp.zeros_like(ds_ref)
            dk_carry[...] = jnp.zeros_like(dk_carry)
            dv_carry[...] = jnp.zeros_like(dv_carry)

        @pl.when(i <= last)
        def _():
            bias = _attn_mask(i)
            for kvh in range(N_KV_HEADS):
                lanes = _head_lanes(kvh)
                k3 = jnp.concatenate([km_ref[:, lanes], kp_ref[:, lanes], kc_ref[:, lanes]], axis=0)
                v3 = jnp.concatenate([vm_ref[:, lanes], vp_ref[:, lanes], vc_ref[:, lanes]], axis=0)
                q4 = _group_rows(q_ref, kvh)
                do4 = _group_rows(do_ref, kvh).astype(MXU_DTYPE)
                p = _attn_probs(q4, k3, _group_bias(bias, sink_ref, layer, kvh))
                dp = _dot_nt(do4, v3)
                dsf = p * (dp - jnp.sum(dp * p, axis=-1, keepdims=True))
                dsc = dsf.astype(MXU_DTYPE)
                dv3 = _dot_tn(p.astype(MXU_DTYPE), do4)
                dk3 = _dot_tn(dsc, q4)
                dq4 = _dot(dsc, k3)
                for g in range(Q_PER_KV):
                    h = kvh * Q_PER_KV + g
                    dq_ref[:, _head_lanes(h)] = dq4[g * BLK:(g + 1) * BLK]
                    ds_ref[h:h + 1, :] += jnp.sum(dsf[g * BLK:(g + 1) * BLK, 0:BLK], axis=0, keepdims=True)
                dkm_ref[:, lanes] += dk3[0:BLK]
                dvm_ref[:, lanes] += dv3[0:BLK]
                dk_ref[:, lanes] = dk_carry[:, lanes] + dk3[BLK:2 * BLK]
                dv_ref[:, lanes] = dv_carry[:, lanes] + dv3[BLK:2 * BLK]
                dk_carry[:, lanes] = dk3[2 * BLK:3 * BLK]
                dv_carry[:, lanes] = dv3[2 * BLK:3 * BLK]

        @pl.when(i == last + 1)
        def _():
            dk_ref[...] = dk_carry[...]
            dv_ref[...] = dv_carry[...]

    cur = lambda i: (jnp.minimum(i, last), 0)
    prev = lambda i: (jnp.clip(i - 1, 0, last), 0)
    kv_meta = pl.BlockSpec((BLK, D_KV), lambda i: (0, 0))
    kv_prev = pl.BlockSpec((BLK, D_KV), prev)
    kv_cur = pl.BlockSpec((BLK, D_KV), cur)
    return _pcall(
        body, name=f"attn_bwd_l{layer}", grid=(n_blk + 1,),
        in_specs=[pl.BlockSpec(memory_space=pltpu.SMEM),
                  pl.BlockSpec((BLK, D_ATTN), cur),
                  kv_meta, kv_prev, kv_cur, kv_meta, kv_prev, kv_cur,
                  pl.BlockSpec((BLK, D_ATTN), cur)],
        out_specs=[pl.BlockSpec((BLK, D_ATTN), cur), kv_prev, kv_prev, kv_meta, kv_meta,
                   pl.BlockSpec((N_Q_HEADS, 128), lambda i: (0, 0))],
        out_shape=[SDS((rows, D_ATTN), F32), SDS((rows, D_KV), F32), SDS((rows, D_KV), F32),
                   SDS((BLK, D_KV), F32), SDS((BLK, D_KV), F32), SDS((N_Q_HEADS, 128), F32)],
        scratch_shapes=[pltpu.VMEM((BLK, D_KV), F32), pltpu.VMEM((BLK, D_KV), F32)],
        compiler_params=_cparams("arbitrary"),
    )(sinks, q, k, k, k, v, v, v, d_out)


def _rope_bwd(dq, dk, dv, dk_meta, dv_meta, cos, sin_a, sin_b, layer):
    rows = dq.shape[0]
    tm = _row_tile(rows)

    def body(dq_ref, dk_ref, dv_ref, dkm_ref, dvm_ref, c_ref, a_ref, b_ref, o_ref):
        c, a, b = c_ref[...], -a_ref[...], -b_ref[...]
        for t in range(8):
            x = dq_ref[:, t * 128:(t + 1) * 128]
            o_ref[:, t * 128:(t + 1) * 128] = (_rope_lanes(x, c, a, b) * ATTN_SCALE).astype(MXU_DTYPE)
        for t in range(2):
            x = dk_ref[:, t * 128:(t + 1) * 128]
            o_ref[:, D_ATTN + t * 128:D_ATTN + (t + 1) * 128] = _rope_lanes(x, c, a, b).astype(MXU_DTYPE)
        o_ref[:, D_ATTN + D_KV:] = dv_ref[...].astype(MXU_DTYPE)

        @pl.when(pl.program_id(0) == 0)
        def _():
            cb, ab, bb = c[0:BLK], a[0:BLK], b[0:BLK]
            is_meta = lax.broadcasted_iota(jnp.int32, (BLK, 128), 0) >= PAD_ROWS
            for t in range(2):
                x = dk_ref[0:BLK, t * 128:(t + 1) * 128] + jnp.where(is_meta, dkm_ref[:, t * 128:(t + 1) * 128], 0.0)
                o_ref[0:BLK, D_ATTN + t * 128:D_ATTN + (t + 1) * 128] = _rope_lanes(x, cb, ab, bb).astype(MXU_DTYPE)
                xv = dv_ref[0:BLK, t * 128:(t + 1) * 128] + jnp.where(is_meta, dvm_ref[:, t * 128:(t + 1) * 128], 0.0)
                o_ref[0:BLK, D_ATTN + D_KV + t * 128:D_ATTN + D_KV + (t + 1) * 128] = xv.astype(MXU_DTYPE)

    tab = pl.BlockSpec((tm, 128), lambda i: (i, 0))
    kv = pl.BlockSpec((tm, D_KV), lambda i: (i, 0))
    meta = pl.BlockSpec((BLK, D_KV), lambda i: (0, 0))
    return _pcall(
        body, name=f"rope_bwd_l{layer}", grid=(rows // tm,),
        in_specs=[pl.BlockSpec((tm, D_ATTN), lambda i: (i, 0)), kv, kv, meta, meta, tab, tab, tab],
        out_specs=pl.BlockSpec((tm, D_ATTN + 2 * D_KV), lambda i: (i, 0)),
        out_shape=SDS((rows, D_ATTN + 2 * D_KV), MXU_DTYPE),
        compiler_params=_cparams("parallel"),
    )(dq, dk, dv, dk_meta, dv_meta, cos, sin_a, sin_b)


def _s5_bwd(d_gated, y, u, carry_in, ssm, w_glu, b_glu3, layer):
    rows = y.shape[0]
    n_chunks = rows // BLK
    b_mat, c_mat, t_re, t_im, d_skip = (ssm[k] for k in ("b_mat", "c_mat", "t_re", "t_im", "d_skip"))

    def body(dz_ref, y_ref, u_ref, cin_ref, bm_ref, cm_ref, tre_ref, tim_ref, d_ref, wg_ref, bg_ref,
             du_ref, dwg_ref, dbg_ref, dd_ref, dbm_ref, dcm_ref, dab_ref,
             lam_carry, bu_scr, s_scr, sp_scr, g_scr, lam_scr):
        step = pl.program_id(0)
        chunk = n_chunks - 1 - step

        @pl.when(step == 0)
        def _():
            for r in (dwg_ref, dbg_ref, dd_ref, dbm_ref, dcm_ref, dab_ref, lam_carry):
                r[...] = jnp.zeros_like(r)

        y = y_ref[...]
        u = u_ref[...]
        d_o = dz_ref[...]
        z, t = _gelu_parts(y)
        zb = z.astype(MXU_DTYPE)
        sg = _sigmoid(_dot(zb, wg_ref[...]) + bg_ref[...])
        dgl = d_o * z * (sg * (1.0 - sg))
        dglb = dgl.astype(MXU_DTYPE)
        dz = d_o * sg + _dot_nt(dglb, wg_ref[...])
        dwg_ref[...] += _dot_tn(zb, dglb)
        dbg_ref[...] += jnp.sum(dgl, axis=0, keepdims=True)
        dy = dz * _gelu_grad(y, t)
        dd_ref[...] += jnp.sum(dy * u, axis=0, keepdims=True)
        grow = lax.broadcasted_iota(jnp.int32, (BLK, 128), 0) + chunk * BLK
        for sb in range(N_SB):
            cols = slice(sb * 128, (sb + 1) * 128)
            u_sb = u[:, cols].astype(MXU_DTYPE)
            dy_sb = dy[:, cols]
            dyb = dy_sb.astype(MXU_DTYPE)
            bu_scr[sb] = _dot(u_sb, bm_ref[sb])
            _scan_tiles(bu_scr.at[sb], s_scr.at[sb], tre_ref, tim_ref, sb,
                        cin_ref[2 * sb:2 * sb + 1, :], cin_ref[2 * sb + 1:2 * sb + 2, :], False, prev_ref=sp_scr.at[sb])
            dcm_ref[sb] += _dot_tn(s_scr[sb].astype(MXU_DTYPE), dyb)
            g_scr[sb] = _dot_nt(dyb, cm_ref[sb])
            n_r, n_i = _scan_tiles(g_scr.at[sb], lam_scr.at[sb], tre_ref, tim_ref, sb,
                                   lam_carry[2 * sb:2 * sb + 1, :], lam_carry[2 * sb + 1:2 * sb + 2, :], True)
            lam_carry[2 * sb:2 * sb + 1, :] = n_r
            lam_carry[2 * sb + 1:2 * sb + 2, :] = n_i
            lr, li = lam_scr[sb, :, :SB_STATES], lam_scr[sb, :, SB_STATES:]
            spr, spi = sp_scr[sb, :, :SB_STATES], sp_scr[sb, :, SB_STATES:]
            dab_ref[2 * sb:2 * sb + 1, :] += jnp.sum(spr * lr + spi * li, axis=0, keepdims=True)
            dab_ref[2 * sb + 1:2 * sb + 2, :] += jnp.sum(spr * li - spi * lr, axis=0, keepdims=True)
            lam = lam_scr[sb].astype(MXU_DTYPE)
            dbm_ref[sb] += _dot_tn(u_sb, lam)
            du = _dot_nt(lam, bm_ref[sb]) + d_ref[:, cols] * dy_sb
            du_ref[:, cols] = jnp.where(grow >= PAD_ROWS, du, 0.0).astype(MXU_DTYPE)

    rev = lambda j: (n_chunks - 1 - j, 0)
    full = lambda shape: pl.BlockSpec(shape, lambda j: (0,) * len(shape))
    of_layer = lambda shape: pl.BlockSpec((None,) + shape, lambda j: (layer,) + (0,) * len(shape))
    tables = [of_layer((N_SB, 8, SCAN_TILE, SB_STATES))] * 2
    chunk_scratch = pltpu.VMEM((N_SB, BLK, 2 * SB_STATES), F32)
    return _pcall(
        body, name=f"s5_bwd_l{layer}", grid=(n_chunks,),
        in_specs=[pl.BlockSpec((BLK, D_SSM), rev), pl.BlockSpec((BLK, D_SSM), rev), pl.BlockSpec((BLK, D_SSM), rev),
                  pl.BlockSpec((None, 8, SB_STATES), lambda j: (n_chunks - 1 - j, 0, 0)),
                  of_layer((N_SB, 128, 2 * SB_STATES)), of_layer((N_SB, 2 * SB_STATES, 128))] + tables + [
                  of_layer((1, D_SSM)), full((D_SSM, D_SSM)),
                  pl.BlockSpec((None, 1, D_SSM), lambda j: (layer, 0, 0))],
        out_specs=[pl.BlockSpec((BLK, D_SSM), rev), full((D_SSM, D_SSM)), full((1, D_SSM)), full((1, D_SSM)),
                   full((N_SB, 128, 2 * SB_STATES)), full((N_SB, 2 * SB_STATES, 128)), full((8, SB_STATES))],
        out_shape=[SDS((rows, D_SSM), MXU_DTYPE), SDS((D_SSM, D_SSM), F32), SDS((1, D_SSM), F32), SDS((1, D_SSM), F32),
                   SDS((N_SB, 128, 2 * SB_STATES), F32), SDS((N_SB, 2 * SB_STATES, 128), F32), SDS((8, SB_STATES), F32)],
        scratch_shapes=[pltpu.VMEM((8, SB_STATES), F32)] + [chunk_scratch] * 5,
        compiler_params=_cparams("arbitrary"),
    )(d_gated, y, u, carry_in, b_mat, c_mat, t_re, t_im, d_skip, w_glu, b_glu3)


DPROJ_PIECES = ((0, 1), (1, 3), (4, 2), (6, 2))


def _in_bwd(dproj_pieces, dhm, hres, gain3, w_in_g, layer):
    rows = hres.shape[0]
    tm = _row_tile(rows)

    def body(du_ref, dqkv_ref, dgs_ref, dga_ref, dh_ref, x_ref, g_ref, w_hbm, dx_ref, dg_ref,
             wt_scr, w_stage, w_sem):
        i = pl.program_id(0)
        _load_resident_transposed(w_hbm, wt_scr, w_stage, w_sem, i == 0)

        @pl.when(i == 0)
        def _():
            dg_ref[...] = jnp.zeros_like(dg_ref)

        dh = None
        for piece_ref, (first, count) in zip((du_ref, dqkv_ref, dgs_ref, dga_ref), DPROJ_PIECES):
            wt = wt_scr[first:first + count].reshape(count * COL_SHARD, D)
            part = _dot(piece_ref[...], wt)
            dh = part if dh is None else dh + part
        dx, dg = _rms_bwd(x_ref[...], g_ref[...], dh)
        dg_ref[...] += dg
        dx_ref[...] = dh_ref[...] + dx

    row_d = pl.BlockSpec((tm, D), lambda i: (i, 0))
    return _pcall(
        body, name=f"in_bwd_l{layer}", grid=(rows // tm,),
        in_specs=[pl.BlockSpec((tm, count * COL_SHARD), lambda i: (i, 0)) for _, count in DPROJ_PIECES] + [
                  row_d, row_d,
                  pl.BlockSpec((None, 1, D), lambda i: (layer, 0, 0)),
                  pl.BlockSpec(memory_space=pl.ANY)],
        out_specs=[row_d, pl.BlockSpec((1, D), lambda i: (0, 0))],
        out_shape=[SDS((rows, D), F32), SDS((1, D), F32)],
        scratch_shapes=[pltpu.VMEM((N_DEV, COL_SHARD, D), MXU_DTYPE),
                        pltpu.VMEM((2, D, COL_SHARD), MXU_DTYPE), pltpu.SemaphoreType.DMA((2,))],
        compiler_params=_cparams("arbitrary"),
    )(*dproj_pieces, dhm, hres, gain3, w_in_g)


_ADAM_C1 = 1.0 / (1.0 - ADAM_B1 ** ADAM_STEP)
_ADAM_C2 = 1.0 / (1.0 - ADAM_B2 ** ADAM_STEP)


def _adam_math(w, g, m, v):
    m = ADAM_B1 * m + (1.0 - ADAM_B1) * g
    v = ADAM_B2 * v + (1.0 - ADAM_B2) * (g * g)
    delta = -ADAM_LR * ((m * _ADAM_C1) / (jnp.sqrt(v * _ADAM_C2) + ADAM_EPS) + ADAM_WD * w)
    return delta, m, v


def _adamw_layers(parts0, parts1, w, m, v, name):
    _, rows, cols = w.shape
    tr = min(rows, (1 << 16) // cols)
    nt = rows // tr

    def body(p0_ref, p1_ref, w_ref, m_ref, v_ref, g_ref, d_ref, nm_ref, nv_ref):
        layer = pl.program_id(0)

        def run(p_ref):
            g = p_ref[0].astype(F32)
            for s in range(1, N_DEV):
                g = g + p_ref[s].astype(F32)
            delta, nm, nv = _adam_math(w_ref[...], g, m_ref[...], v_ref[...])
            g_ref[...] = g
            d_ref[...] = delta
            nm_ref[...] = nm
            nv_ref[...] = nv

        @pl.when(layer == 0)
        def _():
            run(p0_ref)

        @pl.when(layer == 1)
        def _():
            run(p1_ref)

    wspec = pl.BlockSpec((None, tr, cols), lambda l, i: (l, i, 0))
    return _pcall(
        body, name=name, grid=(2, nt),
        in_specs=[pl.BlockSpec((N_DEV, tr, cols), lambda l, i: (0, jnp.where(l == 0, i, nt - 1), 0)),
                  pl.BlockSpec((N_DEV, tr, cols), lambda l, i: (0, jnp.where(l == 1, i, 0), 0)),
                  wspec, wspec, wspec],
        out_specs=[wspec] * 4, out_shape=[SDS(w.shape, F32)] * 4,
        compiler_params=_cparams("arbitrary", "arbitrary"),
    )(parts0, parts1, w, m, v)


def _sum_slots(parts, name):
    def body(p_ref, o_ref):
        acc = p_ref[0]
        for s in range(1, N_DEV):
            acc = acc + p_ref[s]
        o_ref[...] = acc

    vmem = pl.BlockSpec(memory_space=pltpu.VMEM)
    return _pcall(body, name=name, out_shape=SDS(parts.shape[1:], F32), in_specs=[vmem], out_specs=vmem,
                  compiler_params=_cparams())(parts)


def _adamw_packed(g, w, m, v, name):
    def body(g_ref, w_ref, m_ref, v_ref, d_ref, nm_ref, nv_ref):
        delta, nm, nv = _adam_math(w_ref[...], g_ref[...], m_ref[...], v_ref[...])
        d_ref[...] = delta
        nm_ref[...] = nm
        nv_ref[...] = nv

    vmem = pl.BlockSpec(memory_space=pltpu.VMEM)
    return _pcall(body, name=name, out_shape=[SDS(g.shape, F32)] * 3, in_specs=[vmem] * 4, out_specs=[vmem] * 3,
                  compiler_params=_cparams())(g, w, m, v)


def _ssm_discretize(a_re, a_im, log_dt, b_re, b_im):
    dt = jnp.exp(log_dt)[:, None]
    mag = jnp.exp(a_re * dt)
    ang = a_im * dt
    ab_re, ab_im = mag * jnp.cos(ang), mag * jnp.sin(ang)
    xr, xi = ab_re - 1.0, ab_im
    den = a_re * a_re + a_im * a_im
    q_re = (xr * a_re + xi * a_im) / den
    q_im = (xi * a_re - xr * a_im) / den
    bb_re = q_re[..., None] * b_re - q_im[..., None] * b_im
    bb_im = q_re[..., None] * b_im + q_im[..., None] * b_re
    return ab_re, ab_im, bb_re, bb_im


def _block_diag_b(bb):
    m = jnp.einsum("sgnc,gh->sgchn", bb.reshape(N_SB, 8, N_STATE, GROUP_CH), jnp.eye(8, dtype=F32))
    return m.reshape(N_SB, 128, SB_STATES)


def _block_diag_b_t(dm):
    return jnp.einsum("sgchn,gh->sgnc", dm.reshape(N_SB, 8, GROUP_CH, 8, N_STATE),
                      jnp.eye(8, dtype=F32)).reshape(N_GROUPS, N_STATE, GROUP_CH)


def _block_diag_c(cc):
    m = jnp.einsum("sgcn,gh->sgnhc", cc.reshape(N_SB, 8, GROUP_CH, N_STATE), jnp.eye(8, dtype=F32))
    return m.reshape(N_SB, SB_STATES, 128)


def _block_diag_c_t(dm):
    return jnp.einsum("sgnhc,gh->sgcn", dm.reshape(N_SB, 8, N_STATE, 8, GROUP_CH),
                      jnp.eye(8, dtype=F32)).reshape(N_GROUPS, GROUP_CH, N_STATE)


def _ssm_tables(ab_re, ab_im, bb_re, bb_im, c_re, c_im, d_skip):
    pr, pi = ab_re.reshape(1, -1), ab_im.reshape(1, -1)
    cr, ci = pr, pi
    squares = []
    for _ in range(3):
        squares.append((cr, ci))
        pr, pi = (jnp.concatenate([pr, pr * cr - pi * ci], axis=0),
                  jnp.concatenate([pi, pr * ci + pi * cr], axis=0))
        cr, ci = cr * cr - ci * ci, 2.0 * cr * ci
    r = jnp.arange(SCAN_TILE)[:, None]
    fwd = [(jnp.where(r >= (1 << k), squares[k][0], 0.0), jnp.where(r >= (1 << k), squares[k][1], 0.0))
           for k in range(3)] + [(pr, pi)]
    rev = [(jnp.where(r < SCAN_TILE - (1 << k), squares[k][0], 0.0),
            jnp.where(r < SCAN_TILE - (1 << k), -squares[k][1], 0.0)) for k in range(3)] + [(pr[::-1], -pi[::-1])]
    table = lambda part: jnp.stack([e[part] for e in fwd + rev]).reshape(
        8, SCAN_TILE, N_SB, SB_STATES).transpose(2, 0, 1, 3)
    return dict(
        b_mat=jnp.concatenate([_block_diag_b(bb_re), _block_diag_b(bb_im)], axis=-1).astype(MXU_DTYPE),
        c_mat=jnp.concatenate([_block_diag_c(c_re), -_block_diag_c(c_im)], axis=1).astype(MXU_DTYPE),
        t_re=table(0), t_im=table(1),
        d_skip=d_skip.reshape(1, D_SSM))


def _rope_tables(rows):
    pos = (jnp.arange(rows, dtype=jnp.int32) - PAD_ROWS).astype(F32)
    inv_freq = 1.0 / (ROPE_THETA ** (jnp.arange(0, HEAD_DIM, 2, dtype=F32) / HEAD_DIM))
    ang = pos[:, None] * inv_freq[None, :]
    ang = jnp.concatenate([ang, ang, ang, ang], axis=-1)
    first_half = (jnp.arange(128) % HEAD_DIM) < HEAD_DIM // 2
    sin = jnp.sin(ang)
    return jnp.cos(ang), jnp.where(first_half, -sin, 0.0), jnp.where(first_half, 0.0, sin)


def _pack(arrays):
    flat = jnp.concatenate([a.reshape(-1).astype(F32) for a in arrays])
    pad = (-flat.shape[0]) % 1024
    return jnp.pad(flat, (0, pad)).reshape(-1, 128)


def _unpack(packed, like):
    flat = packed.reshape(-1)
    out, off = [], 0
    for a in like:
        n = math.prod(a.shape)
        out.append(flat[off:off + n].reshape(a.shape))
        off += n
    return out


BIG = ("w_in", "w_glu", "w_o_ssm", "w_o_attn", "w_out", "w_up", "w_down")
WEIGHTS = ("meta_tokens", "norm_mix_pre", "norm_mix_post", "norm_mlp_pre", "norm_mlp_post", "w_in",
           "ssm_a_re", "ssm_a_im", "ssm_log_dt", "ssm_b_re", "ssm_b_im", "ssm_c_re", "ssm_c_im", "ssm_d",
           "w_glu", "b_glu", "attn_sinks", "w_o_ssm", "w_o_attn", "w_out", "w_up", "w_down")
SMALL = tuple(n for n in WEIGHTS if n not in BIG)


def kernel(x, meta_tokens, norm_mix_pre, norm_mix_post, norm_mlp_pre, norm_mlp_post, w_in, ssm_a_re, ssm_a_im, ssm_log_dt, ssm_b_re, ssm_b_im, ssm_c_re, ssm_c_im, ssm_d, w_glu, b_glu, attn_sinks, w_o_ssm, w_o_attn, w_out, w_up, w_down, loss_target, m_meta_tokens, m_norm_mix_pre, m_norm_mix_post, m_norm_mlp_pre, m_norm_mlp_post, m_w_in, m_ssm_a_re, m_ssm_a_im, m_ssm_log_dt, m_ssm_b_re, m_ssm_b_im, m_ssm_c_re, m_ssm_c_im, m_ssm_d, m_w_glu, m_b_glu, m_attn_sinks, m_w_o_ssm, m_w_o_attn, m_w_out, m_w_up, m_w_down, v_meta_tokens, v_norm_mix_pre, v_norm_mix_post, v_norm_mlp_pre, v_norm_mlp_post, v_w_in, v_ssm_a_re, v_ssm_a_im, v_ssm_log_dt, v_ssm_b_re, v_ssm_b_im, v_ssm_c_re, v_ssm_c_im, v_ssm_d, v_w_glu, v_b_glu, v_attn_sinks, v_w_o_ssm, v_w_o_attn, v_w_out, v_w_up, v_w_down):
    args = locals()
    w = {n: args[n] for n in WEIGHTS}
    m = {n: args["m_" + n] for n in WEIGHTS}
    v = {n: args["v_" + n] for n in WEIGHTS}
    n_layers = w_in.shape[0]
    seq = x.shape[1]
    rows = seq + BLK
    my_slot = _slot(_mesh_pos())

    assert n_layers == 2
    xfer = {n: [w[n][l].astype(XFER_DTYPE) for l in range(n_layers)] for n in BIG}
    mixer_small = ("w_glu", "w_o_ssm", "w_o_attn", "w_out")
    meta_g, w_in_g0 = _exchange_by_sequencer([meta_tokens, xfer["w_in"][0]], True, 0, "gather_in0")
    mix0_g = _exchange_by_sequencer([xfer[n][0] for n in mixer_small], True, 1, "gather_mix0", after=[meta_g])
    meta_full = meta_g.transpose(1, 0, 2).reshape(N_META, D)

    def mixer_weights(w_glu_g, w_o_ssm_g, w_o_attn_g, w_out_g):
        return dict(w_glu=w_glu_g.reshape(D_SSM, D_SSM), w_o_ssm=w_o_ssm_g.transpose(1, 0, 2).reshape(D_SSM, D),
                    w_o_attn=w_o_attn_g.reshape(D_ATTN, D), w_out=w_out_g.reshape(D, D),
                    w_o_ssm_t=w_o_ssm_g.transpose(0, 2, 1).reshape(D, D_SSM),
                    w_o_attn_t=w_o_attn_g.reshape(D_ATTN, D).T, w_out_t=w_out_g.reshape(D, D).T)

    gathered = [dict(w_in=w_in_g0, **mixer_weights(*mix0_g)), {}]

    gains = {n: w[n].reshape(n_layers, 1, D) for n in ("norm_mix_pre", "norm_mix_post", "norm_mlp_pre", "norm_mlp_post")}
    b_glu3 = b_glu.reshape(n_layers, 1, D_SSM)
    cos, sin_a, sin_b = _rope_tables(rows)

    disc, disc_vjp = jax.vjp(jax.vmap(_ssm_discretize), ssm_a_re, ssm_a_im, ssm_log_dt, ssm_b_re, ssm_b_im)
    ssm = jax.vmap(_ssm_tables)(*disc, ssm_c_re, ssm_c_im, ssm_d)

    hres = jnp.concatenate([jnp.zeros((PAD_ROWS, D), F32), meta_full, x[0]], axis=0)

    saved = []
    for l in range(n_layers):
        wl = gathered[l]
        u, gates, q, k, vv, h = _in_proj(hres, gains["norm_mix_pre"], wl["w_in"], cos, sin_a, sin_b, l)
        if l == 0:
            wl["w_up"], wl["w_down"] = _exchange_by_sequencer([xfer["w_up"][0], xfer["w_down"][0]], True, 2,
                                                              "gather_mlp0", after=[h])
        y, y_ssm, carry_in = _s5_fwd(u, ssm, wl["w_glu"], b_glu3, l)
        if l == 0:
            l1_g = _exchange_by_sequencer([xfer[n][1] for n in ("w_in",) + mixer_small + ("w_up", "w_down")], True, 3,
                                          "gather_l1", after=[y])
            gathered[1] = dict(w_in=l1_g[0], w_up=l1_g[5], w_down=l1_g[6], **mixer_weights(*l1_g[1:5]))
            last_exchange = l1_g[:1]
        y_attn = _attn_fwd(q, k, vv, attn_sinks, l)
        merged, mix, hres_mid = _merge_fwd(y_ssm, y_attn, gates, hres, wl["w_o_ssm"], wl["w_o_attn"], wl["w_out"],
                                           gains["norm_mix_post"], l)
        up, h2, ff, hres_out = _mlp_fwd(hres_mid, gains["norm_mlp_pre"], gains["norm_mlp_post"], wl["w_up"],
                                        wl["w_down"], l)
        saved.append(dict(hres=hres, u=u, gates=gates, h=h, q=q, k=k, v=vv, y=y, y_ssm=y_ssm,
                          carry_in=carry_in, y_attn=y_attn, merged=merged, mix=mix, hres_mid=hres_mid,
                          up=up, h2=h2, ff=ff))
        hres = hres_out

    dhres, loss_vec = _loss_and_grad(hres, loss_target[0])

    small_grads = {}
    recv_up, recv_down, recv_mix = [None] * n_layers, [None] * n_layers, [None] * n_layers
    for l in reversed(range(n_layers)):
        s = saved[l]
        wl = gathered[l]
        dff, dup, dhm, dg_mlp_post, dg_mlp_pre = _mlp_bwd(dhres, s["ff"], s["up"], s["hres_mid"], gains["norm_mlp_pre"],
                                                          gains["norm_mlp_post"], wl["w_up"], wl["w_down"], l)
        dw_up = _matmul_tn(s["h2"], dup, f"dw_up_l{l}", dev_major_cols=COL_SHARD)
        recv_up[l] = _exchange_by_sequencer([dw_up], False, 4 + 3 * l, f"scatter_up{l}", after=last_exchange)
        dw_down = _matmul_tn(s["up"], dff, f"dw_down_l{l}", a_fn=_relu_squared).reshape(N_DEV, COL_SHARD, D)
        recv_down[l] = _exchange_by_sequencer([dw_down], False, 5 + 3 * l, f"scatter_down{l}", after=recv_up[l])
        last_exchange = recv_down[l]
        dmix, da1, da2, dgs, dga, dy_ssm, dy_attn, dg_mix_post = _merge_bwd(
            dhm, s["mix"], s["y_ssm"], s["y_attn"], s["gates"], wl["w_o_ssm"], wl["w_o_attn"], wl["w_o_ssm_t"],
            wl["w_o_attn_t"], wl["w_out_t"], gains["norm_mix_post"], l)
        dw_out = _matmul_tn(s["merged"], dmix, f"dw_out_l{l}").reshape(N_DEV, D // N_DEV, D)
        dw_o_attn = _matmul_tn(s["y_attn"], da2, f"dw_o_attn_l{l}").reshape(N_DEV, D_ATTN // N_DEV, D)
        dw_o_ssm = _matmul_tn(s["y_ssm"], da1, f"dw_o_ssm_l{l}", dev_major_cols=D // N_DEV)
        if l == 0:
            recv_out0 = _exchange_by_sequencer([dw_o_ssm, dw_o_attn, dw_out], False, 11, "scatter_out0",
                                               after=last_exchange)
            last_exchange = recv_out0[:1]
        dq, dk, dv, dk_meta, dv_meta, dsink = _attn_bwd(s["q"], s["k"], s["v"], dy_attn, attn_sinks, l)
        dqkv = _rope_bwd(dq, dk, dv, dk_meta, dv_meta, cos, sin_a, sin_b, l)
        du, dw_glu, db_glu, dd_skip, db_mat, dc_mat, dab = _s5_bwd(dy_ssm, s["y"], s["u"], s["carry_in"], ssm,
                                                                    wl["w_glu"], b_glu3, l)
        dproj = (du, dqkv, dgs, dga)
        dw_in = _dw_in(s["h"], dproj, l)
        dhres, dg_mix_pre = _in_bwd(dproj, dhm, s["hres"], gains["norm_mix_pre"], wl["w_in"], l)
        mix_parts = [dw_in, dw_glu.astype(XFER_DTYPE).reshape(N_DEV, D_SSM // N_DEV, D_SSM), dw_o_ssm, dw_o_attn, dw_out]
        if l > 0:
            recv_mix[l] = _exchange_by_sequencer(mix_parts, False, 6 + 3 * l, f"scatter_mix{l}", after=last_exchange)
            last_exchange = recv_mix[l][:1]

        for name, val in (("norm_mix_pre", dg_mix_pre[0]), ("norm_mix_post", dg_mix_post[0]),
                          ("norm_mlp_pre", dg_mlp_pre[0]), ("norm_mlp_post", dg_mlp_post[0]),
                          ("dab", dab), ("db_mat", db_mat), ("dc_mat", dc_mat),
                          ("ssm_d", dd_skip.reshape(N_GROUPS, GROUP_CH)), ("b_glu", db_glu[0]),
                          ("attn_sinks", dsink[:, 0])):
            small_grads.setdefault(name, [None] * n_layers)[l] = val

    grad_x = dhres[BLK:][None]
    stacked = {n: jnp.stack(v) for n, v in small_grads.items()}
    dab = stacked["dab"].reshape(n_layers, N_SB, 2, SB_STATES)
    db_mat, dc_mat = stacked["db_mat"], stacked["dc_mat"]
    b_t, c_t = jax.vmap(_block_diag_b_t), jax.vmap(_block_diag_c_t)
    (stacked["ssm_a_re"], stacked["ssm_a_im"], stacked["ssm_log_dt"], stacked["ssm_b_re"],
     stacked["ssm_b_im"]) = disc_vjp((dab[:, :, 0].reshape(n_layers, N_GROUPS, N_STATE),
                                      dab[:, :, 1].reshape(n_layers, N_GROUPS, N_STATE),
                                      b_t(db_mat[..., :SB_STATES]), b_t(db_mat[..., SB_STATES:])))
    stacked["ssm_c_re"] = c_t(dc_mat[:, :, :SB_STATES])
    stacked["ssm_c_im"] = -c_t(dc_mat[:, :, SB_STATES:])
    small_names = [n for n in SMALL if n != "meta_tokens"]
    partial_small = [dhres[PAD_ROWS:BLK]] + [stacked[n] for n in small_names] + [loss_vec[0, :1]]
    recv_in0, recv_glu0, small_parts = _exchange_by_sequencer(
        mix_parts[:2] + [_pack(partial_small)], [False, False, True], 6, "scatter_in0", after=last_exchange)
    recv_mix[0] = [recv_in0, recv_glu0] + recv_out0

    grads, delta, new_m, new_v = {}, {}, {}, {}

    def adamw_big(names, recv0, recv1):
        for n, p0, p1 in zip(names, recv0, recv1):
            grads[n], delta[n], new_m[n], new_v[n] = _adamw_layers(p0, p1, w[n], m[n], v[n], f"adamw_{n}")

    adamw_big(("w_up", "w_down"), recv_up[0] + recv_down[0], recv_up[1] + recv_down[1])
    summed = _unpack(_sum_slots(small_parts, "sum_small_grads"), partial_small)
    loss = summed[-1][0]
    grads.update(zip(small_names, summed[1:-1]))
    grads["meta_tokens"] = lax.dynamic_slice_in_dim(summed[0], my_slot * (D // N_DEV), D // N_DEV, axis=1)
    like = [w[n] for n in SMALL]
    d_s, m_s, v_s = _adamw_packed(_pack([grads[n] for n in SMALL]), _pack(like), _pack([m[n] for n in SMALL]),
                                  _pack([v[n] for n in SMALL]), "adamw_small")
    adamw_big(("w_in",) + mixer_small, recv_mix[0], recv_mix[1])
    for n, dd, mm, vs in zip(SMALL, _unpack(d_s, like), _unpack(m_s, like), _unpack(v_s, like)):
        delta[n], new_m[n], new_v[n] = dd, mm, vs

    return (loss, grad_x, *[grads[n] for n in WEIGHTS], *[delta[n] for n in WEIGHTS],
            *[new_m[n] for n in WEIGHTS], *[new_v[n] for n in WEIGHTS])
```

```python
import functools
import math

import jax
import jax.numpy as jnp
from jax import lax
from jax.experimental import pallas as pl
from jax.experimental.pallas import tpu as pltpu
from jax.experimental.pallas import tpu_sc as plsc

F32 = jnp.float32
MXU_DTYPE = jnp.bfloat16
XFER_DTYPE = MXU_DTYPE
_pcall = pl.pallas_call
SDS = jax.ShapeDtypeStruct

D = 1024
D_SSM = 512
D_ATTN = 1024
D_KV = 256
D_FF = 4096
D_IN = 4096
HEAD_DIM = 64
N_Q_HEADS = 16
N_KV_HEADS = 4
Q_PER_KV = 4
N_META = 16
BLK = 128
PAD_ROWS = BLK - N_META
N_GROUPS = 32
N_STATE = 64
GROUP_CH = 16
N_SB = 4
SB_STATES = 512
ROPE_THETA = 10000.0
ATTN_SCALE = HEAD_DIM ** -0.5
NEG_INF = -1e30
RMS_EPS = 1e-6
N_DEV = 8
COL_SHARD = 512

ADAM_LR = 0.001
ADAM_B1 = 0.9
ADAM_B2 = 0.999
ADAM_EPS = 1e-08
ADAM_WD = 0.01
ADAM_STEP = 10

VMEM_LIMIT = 56 * 1024 * 1024
MESH_AXES = ("x", "y", "c")

_NT = (((1,), (1,)), ((), ()))
_TN = (((0,), (0,)), ((), ()))


def _cparams(*sem):
    return pltpu.CompilerParams(dimension_semantics=tuple(sem) if sem else None,
                                vmem_limit_bytes=VMEM_LIMIT)


def _row_tile(rows, cap=640):
    for t in (1664, 640, 512, 320, 256, 128):
        if t <= cap and rows % t == 0:
            return t
    raise ValueError(f"unsupported row count {rows}")


def _dot(a, b):
    return jnp.dot(a, b, preferred_element_type=F32)


def _dot_nt(a, b):
    return lax.dot_general(a, b, _NT, preferred_element_type=F32)


def _dot_tn(a, b):
    return lax.dot_general(a, b, _TN, preferred_element_type=F32)


def _sigmoid(x):
    return 1.0 / (1.0 + jnp.exp(-x))


_GELU_C = math.sqrt(2.0 / math.pi)


def _gelu_parts(y):
    t = jnp.tanh(_GELU_C * (y + 0.044715 * (y * y * y)))
    return 0.5 * y * (1.0 + t), t


def _gelu_grad(y, t):
    return 0.5 * (1.0 + t) + 0.5 * y * (1.0 - t * t) * (_GELU_C * (1.0 + 0.134145 * (y * y)))


def _rms_fwd(x, gain):
    r = lax.rsqrt(jnp.mean(x * x, axis=-1, keepdims=True) + RMS_EPS)
    return (x * r) * gain


def _rms_bwd(x, gain, dout):
    r = lax.rsqrt(jnp.mean(x * x, axis=-1, keepdims=True) + RMS_EPS)
    xh = x * r
    dxh = dout * gain
    dx = r * (dxh - xh * jnp.mean(dxh * xh, axis=-1, keepdims=True))
    return dx, jnp.sum(dout * xh, axis=0, keepdims=True)


def _mesh_pos():
    return lax.axis_index("x"), lax.axis_index("y"), lax.axis_index("c")


def _peer(pos, d):
    x, y, c = pos
    return (1 - x if d & 4 else x, 1 - y if d & 2 else y, 1 - c if d & 1 else c)


def _slot(pos):
    return 4 * pos[0] + 2 * pos[1] + pos[2]


def _exchange_copy(gather, src_ref, land_ref, sems, k, d, me, send_side):
    peer = _peer(me, d)
    sender = me if send_side else peer
    src = src_ref if gather else src_ref.at[_slot(peer) if send_side else _slot(me)]
    return pltpu.make_async_remote_copy(
        src_ref=src, dst_ref=land_ref.at[_slot(sender)],
        send_sem=sems[0].at[k * (N_DEV - 1) + d - 1], recv_sem=sems[1].at[k * (N_DEV - 1) + d - 1],
        device_id=peer, device_id_type=pl.DeviceIdType.MESH)


def _exchange_by_sequencer(srcs, gather, collective_id, name, after=()):
    n = len(srcs)
    flags = [gather] * n if isinstance(gather, bool) else list(gather)
    land_types = [SDS(((N_DEV,) + s.shape) if g else s.shape, s.dtype) for s, g in zip(srcs, flags)]

    def body(*refs):
        src_refs = refs[:n]
        land_refs = refs[n + len(after):2 * n + len(after)]
        sems = refs[2 * n + len(after):2 * n + len(after) + 2]
        local_sems = refs[2 * n + len(after) + 2]
        me = _mesh_pos()
        barrier = pltpu.get_barrier_semaphore()
        for d in range(1, N_DEV):
            pl.semaphore_signal(barrier, inc=1, device_id=_peer(me, d), device_id_type=pl.DeviceIdType.MESH)
        pl.semaphore_wait(barrier, N_DEV - 1)
        own = [pltpu.make_async_copy(src_refs[k] if flags[k] else src_refs[k].at[_slot(me)],
                                     land_refs[k].at[_slot(me)], local_sems.at[k]) for k in range(n)]
        for cp in own:
            cp.start()
        for k in range(n):
            for d in range(1, N_DEV):
                _exchange_copy(flags[k], src_refs[k], land_refs[k], sems, k, d, me, True).start()
        for cp in own:
            cp.wait()
        for k in range(n):
            for d in range(1, N_DEV):
                _exchange_copy(flags[k], src_refs[k], land_refs[k], sems, k, d, me, True).wait_send()
        for k in range(n):
            for d in range(1, N_DEV):
                _exchange_copy(flags[k], src_refs[k], land_refs[k], sems, k, d, me, False).wait_recv()

    sem_type = pltpu.SemaphoreType.DMA((n * (N_DEV - 1),))
    return pl.kernel(
        body, out_type=land_types, mesh=plsc.ScalarSubcoreMesh(axis_name="sequencer", num_cores=1), name=name,
        scratch_types=(sem_type, sem_type, pltpu.SemaphoreType.DMA((n,))),
        compiler_params=pltpu.CompilerParams(collective_id=collective_id),
    )(*srcs, *after)


def _load_resident(w_hbm, w_scr, sems, first_step):
    @pl.when(first_step)
    def _():
        copies = [pltpu.make_async_copy(w_hbm.at[s], w_scr.at[s], sems.at[s]) for s in range(N_DEV)]
        for cp in copies:
            cp.start()
        for cp in copies:
            cp.wait()


def _load_resident_transposed(w_hbm, w_scr, stage, sems, first_step):
    @pl.when(first_step)
    def _():
        copies = [pltpu.make_async_copy(w_hbm.at[s], stage.at[s % 2], sems.at[s % 2]) for s in range(N_DEV)]
        copies[0].start()
        for s in range(N_DEV):
            if s + 1 < N_DEV:
                copies[s + 1].start()
            copies[s].wait()
            w_scr[s] = stage[s % 2].T


def _rope_lanes(t, cos, sin_a, sin_b):
    return t * cos + pltpu.roll(t, 96, 1) * sin_a + pltpu.roll(t, 32, 1) * sin_b


def _in_proj(hres, gain3, w_in_g, cos, sin_a, sin_b, layer):
    rows = hres.shape[0]
    tm = _row_tile(rows, 320)

    def body(x_ref, g_ref, w_hbm, c_ref, a_ref, b_ref, u_ref, gate_ref, q_ref, k_ref, v_ref, h_ref, w_scr, w_sem):
        _load_resident(w_hbm, w_scr, w_sem, pl.program_id(0) == 0)
        hn = _rms_fwd(x_ref[...], g_ref[...]).astype(MXU_DTYPE)
        h_ref[...] = hn
        c, a, b = c_ref[...], a_ref[...], b_ref[...]
        u_ref[...] = _dot(hn, w_scr[0])
        for shard in (1, 2):
            res = _dot(hn, w_scr[shard])
            for t in range(4):
                lanes = slice(t * 128, (t + 1) * 128)
                out = slice((shard - 1) * COL_SHARD + t * 128, (shard - 1) * COL_SHARD + (t + 1) * 128)
                q_ref[:, out] = (_rope_lanes(res[:, lanes], c, a, b) * ATTN_SCALE).astype(MXU_DTYPE)
        res = _dot(hn, w_scr[3])
        for t in range(2):
            lanes = slice(t * 128, (t + 1) * 128)
            k_ref[:, lanes] = _rope_lanes(res[:, lanes], c, a, b).astype(MXU_DTYPE)
        v_ref[...] = res[:, D_KV:].astype(MXU_DTYPE)
        for shard in range(4, N_DEV):
            gate_ref[:, (shard - 4) * COL_SHARD:(shard - 3) * COL_SHARD] = _dot(hn, w_scr[shard])

    tab = pl.BlockSpec((tm, 128), lambda i: (i, 0))
    kv = pl.BlockSpec((tm, D_KV), lambda i: (i, 0))
    row_d = pl.BlockSpec((tm, D), lambda i: (i, 0))
    return _pcall(
        body, name=f"in_proj_l{layer}", grid=(rows // tm,),
        in_specs=[row_d, pl.BlockSpec((None, 1, D), lambda i: (layer, 0, 0)),
                  pl.BlockSpec(memory_space=pl.ANY), tab, tab, tab],
        out_specs=[pl.BlockSpec((tm, D_SSM), lambda i: (i, 0)), pl.BlockSpec((tm, 2 * D), lambda i: (i, 0)),
                   row_d, kv, kv, row_d],
        out_shape=[SDS((rows, D_SSM), F32), SDS((rows, 2 * D), F32), SDS((rows, D_ATTN), MXU_DTYPE),
                   SDS((rows, D_KV), MXU_DTYPE), SDS((rows, D_KV), MXU_DTYPE), SDS((rows, D), MXU_DTYPE)],
        scratch_shapes=[pltpu.VMEM((N_DEV, D, COL_SHARD), MXU_DTYPE), pltpu.SemaphoreType.DMA((N_DEV,))],
        compiler_params=_cparams("arbitrary"),
    )(hres, gain3, w_in_g, cos, sin_a, sin_b)


SCAN_TILE = 8


def _scan_tiles(x_ref, out_ref, tre_ref, tim_ref, sb, t_r, t_i, reverse, prev_ref=None):
    base = 4 if reverse else 0
    n_tiles = BLK // SCAN_TILE
    row = lax.broadcasted_iota(jnp.int32, (SCAN_TILE, SB_STATES), 0)
    for j in (range(n_tiles - 1, -1, -1) if reverse else range(n_tiles)):
        rows = slice(SCAN_TILE * j, SCAN_TILE * (j + 1))
        xr = x_ref[rows, :SB_STATES]
        xi = x_ref[rows, SB_STATES:]
        for k in range(3):
            shift = SCAN_TILE - (1 << k) if reverse else (1 << k)
            rr = pltpu.roll(xr, shift, 0)
            ri = pltpu.roll(xi, shift, 0)
            ar = tre_ref[sb, base + k]
            ai = tim_ref[sb, base + k]
            xr, xi = xr + (ar * rr - ai * ri), xi + (ar * ri + ai * rr)
        pr = tre_ref[sb, base + 3]
        pi = tim_ref[sb, base + 3]
        xr, xi = xr + (pr * t_r - pi * t_i), xi + (pr * t_i + pi * t_r)
        out_ref[rows, :SB_STATES] = xr
        out_ref[rows, SB_STATES:] = xi
        if prev_ref is not None:
            prev_ref[rows, :SB_STATES] = jnp.where(row == 0, t_r, pltpu.roll(xr, 1, 0))
            prev_ref[rows, SB_STATES:] = jnp.where(row == 0, t_i, pltpu.roll(xi, 1, 0))
        edge = slice(0, 1) if reverse else slice(SCAN_TILE - 1, SCAN_TILE)
        t_r, t_i = xr[edge], xi[edge]
    return t_r, t_i


def _s5_fwd(u, ssm, w_glu, b_glu3, layer):
    rows = u.shape[0]
    n_chunks = rows // BLK
    b_mat, c_mat, t_re, t_im, d_skip = (ssm[k] for k in ("b_mat", "c_mat", "t_re", "t_im", "d_skip"))

    def body(u_ref, bm_ref, cm_ref, tre_ref, tim_ref, d_ref, wg_ref, bg_ref,
             y_ref, ys_ref, cin_ref, carry, bu_scr, s_scr):
        @pl.when(pl.program_id(0) == 0)
        def _():
            carry[...] = jnp.zeros_like(carry)

        cin_ref[...] = carry[...]
        u = u_ref[...]
        for sb in range(N_SB):
            cols = slice(sb * 128, (sb + 1) * 128)
            u_sb = u[:, cols]
            bu_scr[sb] = _dot(u_sb.astype(MXU_DTYPE), bm_ref[sb])
            t_r, t_i = _scan_tiles(bu_scr.at[sb], s_scr.at[sb], tre_ref, tim_ref, sb,
                                   carry[2 * sb:2 * sb + 1, :], carry[2 * sb + 1:2 * sb + 2, :], False)
            carry[2 * sb:2 * sb + 1, :] = t_r
            carry[2 * sb + 1:2 * sb + 2, :] = t_i
            y_ref[:, cols] = _dot(s_scr[sb].astype(MXU_DTYPE), cm_ref[sb]) + d_ref[:, cols] * u_sb
        z, _ = _gelu_parts(y_ref[...])
        gl = _dot(z.astype(MXU_DTYPE), wg_ref[...]) + bg_ref[...]
        ys_ref[...] = (z * _sigmoid(gl)).astype(MXU_DTYPE)

    full = lambda shape: pl.BlockSpec(shape, lambda j: (0,) * len(shape))
    of_layer = lambda shape: pl.BlockSpec((None,) + shape, lambda j: (layer,) + (0,) * len(shape))
    return _pcall(
        body, name=f"s5_fwd_l{layer}", grid=(n_chunks,),
        in_specs=[pl.BlockSpec((BLK, D_SSM), lambda j: (j, 0)),
                  of_layer((N_SB, 128, 2 * SB_STATES)), of_layer((N_SB, 2 * SB_STATES, 128)),
                  of_layer((N_SB, 8, SCAN_TILE, SB_STATES)), of_layer((N_SB, 8, SCAN_TILE, SB_STATES)),
                  of_layer((1, D_SSM)), full((D_SSM, D_SSM)),
                  pl.BlockSpec((None, 1, D_SSM), lambda j: (layer, 0, 0))],
        out_specs=[pl.BlockSpec((BLK, D_SSM), lambda j: (j, 0)), pl.BlockSpec((BLK, D_SSM), lambda j: (j, 0)),
                   pl.BlockSpec((None, 8, SB_STATES), lambda j: (j, 0, 0))],
        out_shape=[SDS((rows, D_SSM), F32), SDS((rows, D_SSM), MXU_DTYPE), SDS((n_chunks, 8, SB_STATES), F32)],
        scratch_shapes=[pltpu.VMEM((8, SB_STATES), F32), pltpu.VMEM((N_SB, BLK, 2 * SB_STATES), F32),
                        pltpu.VMEM((N_SB, BLK, 2 * SB_STATES), F32)],
        compiler_params=_cparams("arbitrary"),
    )(u, b_mat, c_mat, t_re, t_im, d_skip, w_glu, b_glu3)


def _attn_mask(i):
    row = lax.broadcasted_iota(jnp.int32, (BLK, 3 * BLK), 0) + i * BLK
    col = lax.broadcasted_iota(jnp.int32, (BLK, 3 * BLK), 1)
    seg = jnp.right_shift(col, 7)
    c = jnp.bitwise_and(col, BLK - 1)
    kidx = c + (i + seg - 2) * BLK
    ok_meta = (seg == 0) & (c >= PAD_ROWS) & (row - c >= BLK)
    ok_win = (seg > 0) & (kidx >= PAD_ROWS) & (kidx <= row) & (row - kidx < BLK)
    return jnp.where(ok_meta | ok_win, 0.0, NEG_INF)


def _head_lanes(h):
    return slice(h * HEAD_DIM, (h + 1) * HEAD_DIM)


def _group_rows(ref, kvh):
    return jnp.concatenate([ref[:, _head_lanes(kvh * Q_PER_KV + g)] for g in range(Q_PER_KV)], axis=0)


def _group_bias(bias, sink_ref, layer, kvh):
    first_col = lax.broadcasted_iota(jnp.int32, (BLK, BLK), 1) == 0
    slabs = []
    for g in range(Q_PER_KV):
        first = jnp.where(first_col, sink_ref[layer, kvh * Q_PER_KV + g], bias[:, :BLK])
        slabs.append(jnp.concatenate([first, bias[:, BLK:]], axis=1))
    return jnp.concatenate(slabs, axis=0)


def _attn_probs(q4, k3, bias4):
    s = _dot_nt(q4, k3) + bias4
    e = jnp.exp(s - jnp.max(s, axis=-1, keepdims=True))
    return e * (1.0 / jnp.sum(e, axis=-1, keepdims=True))


def _attn_fwd(q, k, v, sinks, layer):
    rows = q.shape[0]
    n_blk = rows // BLK

    def body(sink_ref, q_ref, km_ref, kp_ref, kc_ref, vm_ref, vp_ref, vc_ref, o_ref):
        bias = _attn_mask(pl.program_id(0))
        for kvh in range(N_KV_HEADS):
            lanes = _head_lanes(kvh)
            k3 = jnp.concatenate([km_ref[:, lanes], kp_ref[:, lanes], kc_ref[:, lanes]], axis=0)
            v3 = jnp.concatenate([vm_ref[:, lanes], vp_ref[:, lanes], vc_ref[:, lanes]], axis=0)
            p = _attn_probs(_group_rows(q_ref, kvh), k3, _group_bias(bias, sink_ref, layer, kvh))
            o4 = _dot(p.astype(MXU_DTYPE), v3).astype(MXU_DTYPE)
            for g in range(Q_PER_KV):
                o_ref[:, _head_lanes(kvh * Q_PER_KV + g)] = o4[g * BLK:(g + 1) * BLK]

    kv_meta = pl.BlockSpec((BLK, D_KV), lambda i: (0, 0))
    kv_prev = pl.BlockSpec((BLK, D_KV), lambda i: (jnp.maximum(i - 1, 0), 0))
    kv_cur = pl.BlockSpec((BLK, D_KV), lambda i: (i, 0))
    return _pcall(
        body, name=f"attn_fwd_l{layer}", grid=(n_blk,),
        in_specs=[pl.BlockSpec(memory_space=pltpu.SMEM),
                  pl.BlockSpec((BLK, D_ATTN), lambda i: (i, 0)),
                  kv_meta, kv_prev, kv_cur, kv_meta, kv_prev, kv_cur],
        out_specs=pl.BlockSpec((BLK, D_ATTN), lambda i: (i, 0)),
        out_shape=SDS((rows, D_ATTN), MXU_DTYPE),
        compiler_params=_cparams("parallel"),
    )(sinks, q, k, k, k, v, v, v)


def _merge_fwd(y_ssm, y_attn, gates, hres, w_o_ssm, w_o_attn, w_out, gain3, layer):
    rows = hres.shape[0]
    tm = _row_tile(rows, 320)

    def body(ys_ref, ya_ref, gs_ref, ga_ref, x_ref, wos_ref, woa_ref, wout_ref, g_ref,
             mg_ref, mix_ref, out_ref):
        a1 = _dot(ys_ref[...], wos_ref[...])
        a2 = _dot(ya_ref[...], woa_ref[...])
        merged = (_sigmoid(gs_ref[...]) * a1 + _sigmoid(ga_ref[...]) * a2).astype(MXU_DTYPE)
        mg_ref[...] = merged
        mix = _dot(merged, wout_ref[...])
        mix_ref[...] = mix
        out_ref[...] = x_ref[...] + _rms_fwd(mix, g_ref[...])

    row_d = pl.BlockSpec((tm, D), lambda i: (i, 0))
    full = lambda shape: pl.BlockSpec(shape, lambda i: (0,) * len(shape))
    return _pcall(
        body, name=f"merge_fwd_l{layer}", grid=(rows // tm,),
        in_specs=[pl.BlockSpec((tm, D_SSM), lambda i: (i, 0)), row_d,
                  row_d, pl.BlockSpec((tm, D), lambda i: (i, 1)), row_d,
                  full((D_SSM, D)), full((D_ATTN, D)), full((D, D)),
                  pl.BlockSpec((None, 1, D), lambda i: (layer, 0, 0))],
        out_specs=[row_d, row_d, row_d],
        out_shape=[SDS((rows, D), MXU_DTYPE), SDS((rows, D), F32), SDS((rows, D), F32)],
        compiler_params=_cparams("parallel"),
    )(y_ssm, y_attn, gates, gates, hres, w_o_ssm, w_o_attn, w_out, gain3)


def _mlp_fwd(hres, gain_pre3, gain_post3, w_up_g, w_down_g, layer, target=None):
    rows = hres.shape[0]
    tm = _row_tile(rows, 320)

    def body(x_ref, gp_ref, gq_ref, wu_hbm, wd_hbm, *refs):
        if target is None:
            up_ref, h_ref, ff_ref, out_ref, act_scr, wu_scr, wd_scr, wu_sem, wd_sem = refs
        else:
            t_ref, up_ref, h_ref, ff_ref, out_ref, loss_ref, act_scr, wu_scr, wd_scr, wu_sem, wd_sem = refs
        first = pl.program_id(0) == 0
        _load_resident(wu_hbm, wu_scr, wu_sem, first)
        _load_resident(wd_hbm, wd_scr, wd_sem, first)
        hn = _rms_fwd(x_ref[...], gp_ref[...]).astype(MXU_DTYPE)
        h_ref[...] = hn
        for kf in range(N_DEV):
            cols = slice(kf * COL_SHARD, (kf + 1) * COL_SHARD)
            up = _dot(hn, wu_scr[kf])
            up_ref[:, cols] = up.astype(MXU_DTYPE)
            r = jnp.maximum(up, 0.0)
            act_scr[:, cols] = (r * r).astype(MXU_DTYPE)
        ff = _dot(act_scr[...], wd_scr[...].reshape(D_FF, D))
        ff_ref[...] = ff
        out = x_ref[...] + _rms_fwd(ff, gq_ref[...])
        if target is None:
            out_ref[...] = out
        else:
            @pl.when(first)
            def _():
                loss_ref[...] = jnp.zeros_like(loss_ref)

            row = lax.broadcasted_iota(jnp.int32, (tm, D), 0) + pl.program_id(0) * tm
            err = jnp.where(row >= BLK, out - t_ref[...], 0.0)
            out_ref[...] = err * (1.0 / D)
            loss_ref[...] += jnp.sum(err * err) * (0.5 / D)

    row_d = pl.BlockSpec((tm, D), lambda i: (i, 0))
    gain = pl.BlockSpec((None, 1, D), lambda i: (layer, 0, 0))
    with_loss = target is not None
    return _pcall(
        body, name=f"mlp_fwd_l{layer}", grid=(rows // tm,),
        in_specs=[row_d, gain, gain, pl.BlockSpec(memory_space=pl.ANY), pl.BlockSpec(memory_space=pl.ANY)]
        + [row_d] * with_loss,
        out_specs=[pl.BlockSpec((tm, D_FF), lambda i: (i, 0)), row_d, row_d, row_d]
        + [pl.BlockSpec((1, 128), lambda i: (0, 0))] * with_loss,
        out_shape=[SDS((rows, D_FF), MXU_DTYPE), SDS((rows, D), MXU_DTYPE), SDS((rows, D), F32), SDS((rows, D), F32)]
        + [SDS((1, 128), F32)] * with_loss,
        scratch_shapes=[pltpu.VMEM((tm, D_FF), MXU_DTYPE),
                        pltpu.VMEM((N_DEV, D, COL_SHARD), MXU_DTYPE), pltpu.VMEM((N_DEV, COL_SHARD, D), MXU_DTYPE),
                        pltpu.SemaphoreType.DMA((N_DEV,)), pltpu.SemaphoreType.DMA((N_DEV,))],
        compiler_params=_cparams("arbitrary"),
    )(hres, gain_pre3, gain_post3, w_up_g, w_down_g, *([target] if with_loss else []))


def _relu_squared(up):
    r = jnp.maximum(up.astype(F32), 0.0)
    return (r * r).astype(MXU_DTYPE)


def _matmul_tn(a, b, name, dev_major_cols=None, a_fn=None):
    rows, ka = a.shape
    n = b.shape[1]
    ta = min(ka, 1024)
    tn = 1024 if n % 1024 == 0 else 512
    tr = _row_tile(rows, 1664)
    n_r = rows // tr

    def body(a_ref, b_ref, o_ref, acc):
        r = pl.program_id(2)

        @pl.when(r == 0)
        def _():
            acc[...] = jnp.zeros_like(acc)

        a_blk = a_ref[...] if a_fn is None else a_fn(a_ref[...])
        acc[...] += _dot_tn(a_blk, b_ref[...])

        @pl.when(r == n_r - 1)
        def _():
            if dev_major_cols is None:
                o_ref[...] = acc[...].astype(XFER_DTYPE)
            else:
                for s in range(tn // dev_major_cols):
                    o_ref[s] = acc[:, s * dev_major_cols:(s + 1) * dev_major_cols].astype(XFER_DTYPE)

    if dev_major_cols is None:
        out_spec = pl.BlockSpec((ta, tn), lambda i, j, r: (i, j))
        out_shape = SDS((ka, n), XFER_DTYPE)
    else:
        w = dev_major_cols
        out_spec = pl.BlockSpec((tn // w, ta, w), lambda i, j, r: (j, i, 0))
        out_shape = SDS((n // w, ka, w), XFER_DTYPE)
    return _pcall(
        body, name=name, grid=(ka // ta, n // tn, n_r),
        in_specs=[pl.BlockSpec((tr, ta), lambda i, j, r: (r, i)), pl.BlockSpec((tr, tn), lambda i, j, r: (r, j))],
        out_specs=out_spec, out_shape=out_shape,
        scratch_shapes=[pltpu.VMEM((ta, tn), F32)],
        compiler_params=_cparams("parallel", "parallel", "arbitrary"),
    )(a, b)


def _dw_in(h, dproj_pieces, layer):
    rows = h.shape[0]
    tr = _row_tile(rows, 1664)
    n_r = rows // tr

    def body(h_ref, du_ref, dqkv_ref, dgs_ref, dga_ref, o_ref, acc):
        j = pl.program_id(0)
        r = pl.program_id(1)

        @pl.when(r == 0)
        def _():
            acc[...] = jnp.zeros_like(acc)

        for piece_ref, (first, count) in zip((du_ref, dqkv_ref, dgs_ref, dga_ref), DPROJ_PIECES):
            @pl.when((j >= first) & (j < first + count))
            def _():
                acc[...] += _dot_tn(h_ref[...], piece_ref[...])

        @pl.when(r == n_r - 1)
        def _():
            o_ref[...] = acc[...].astype(XFER_DTYPE)

    def piece_spec(first, count):
        def index(j, r):
            mine = (j >= first) & (j < first + count)
            return jnp.where(mine, r, 0), jnp.clip(j - first, 0, count - 1)
        return pl.BlockSpec((tr, COL_SHARD), index)

    return _pcall(
        body, name=f"dw_in_l{layer}", grid=(N_DEV, n_r),
        in_specs=[pl.BlockSpec((tr, D), lambda j, r: (r, 0))] + [piece_spec(*p) for p in DPROJ_PIECES],
        out_specs=pl.BlockSpec((None, D, COL_SHARD), lambda j, r: (j, 0, 0)),
        out_shape=SDS((N_DEV, D, COL_SHARD), XFER_DTYPE),
        scratch_shapes=[pltpu.VMEM((D, COL_SHARD), F32)],
        compiler_params=_cparams("arbitrary", "arbitrary"),
    )(h, *dproj_pieces)


def _mlp_bwd(dout, ff, up, hres_mid, gain_pre3, gain_post3, w_up_g, w_down_g, layer):
    rows = dout.shape[0]
    tm = _row_tile(rows, 320)

    def body(do_ref, ff_ref, up_ref, x_ref, gp_ref, gq_ref, wu_hbm, wd_hbm,
             dff_ref, dup_ref, dx_ref, dgq_ref, dgp_ref, wut_scr, wdt_scr, wu_stage, wd_stage, wu_sem, wd_sem):
        i = pl.program_id(0)
        _load_resident_transposed(wu_hbm, wut_scr, wu_stage, wu_sem, i == 0)
        _load_resident_transposed(wd_hbm, wdt_scr, wd_stage, wd_sem, i == 0)

        @pl.when(i == 0)
        def _():
            dgq_ref[...] = jnp.zeros_like(dgq_ref)
            dgp_ref[...] = jnp.zeros_like(dgp_ref)

        dff, dg = _rms_bwd(ff_ref[...], gq_ref[...], do_ref[...])
        dgq_ref[...] += dg
        dffb = dff.astype(MXU_DTYPE)
        dff_ref[...] = dffb
        for kf in range(N_DEV):
            cols = slice(kf * COL_SHARD, (kf + 1) * COL_SHARD)
            dact = _dot(dffb, wdt_scr[kf])
            dup_ref[:, cols] = (dact * (2.0 * jnp.maximum(up_ref[:, cols].astype(F32), 0.0))).astype(MXU_DTYPE)
        dh = _dot(dup_ref[...], wut_scr[...].reshape(D_FF, D))
        dx, dg = _rms_bwd(x_ref[...], gp_ref[...], dh)
        dgp_ref[...] += dg
        dx_ref[...] = do_ref[...] + dx

    row_d = pl.BlockSpec((tm, D), lambda i: (i, 0))
    row_ff = pl.BlockSpec((tm, D_FF), lambda i: (i, 0))
    gain = pl.BlockSpec((None, 1, D), lambda i: (layer, 0, 0))
    dgain = pl.BlockSpec((1, D), lambda i: (0, 0))
    return _pcall(
        body, name=f"mlp_bwd_l{layer}", grid=(rows // tm,),
        in_specs=[row_d, row_d, row_ff, row_d, gain, gain,
                  pl.BlockSpec(memory_space=pl.ANY), pl.BlockSpec(memory_space=pl.ANY)],
        out_specs=[row_d, row_ff, row_d, dgain, dgain],
        out_shape=[SDS((rows, D), MXU_DTYPE), SDS((rows, D_FF), MXU_DTYPE), SDS((rows, D), F32),
                   SDS((1, D), F32), SDS((1, D), F32)],
        scratch_shapes=[pltpu.VMEM((N_DEV, COL_SHARD, D), MXU_DTYPE), pltpu.VMEM((N_DEV, D, COL_SHARD), MXU_DTYPE),
                        pltpu.VMEM((2, D, COL_SHARD), MXU_DTYPE), pltpu.VMEM((2, COL_SHARD, D), MXU_DTYPE),
                        pltpu.SemaphoreType.DMA((2,)), pltpu.SemaphoreType.DMA((2,))],
        compiler_params=_cparams("arbitrary"),
    )(dout, ff, up, hres_mid, gain_pre3, gain_post3, w_up_g, w_down_g)


def _merge_bwd(dhm, mix, y_ssm, y_attn, gates, w_o_ssm, w_o_attn, w_o_ssm_t, w_o_attn_t, w_out_t, gain3, layer):
    rows = dhm.shape[0]
    tm = _row_tile(rows, 320)

    def body(dh_ref, mix_ref, ys_ref, ya_ref, gs_ref, ga_ref, wos_ref, woa_ref, wost_ref, woat_ref, woutt_ref, g_ref,
             dmix_ref, da1_ref, da2_ref, dgs_ref, dga_ref, dys_ref, dya_ref, dg_ref):
        @pl.when(pl.program_id(0) == 0)
        def _():
            dg_ref[...] = jnp.zeros_like(dg_ref)

        dmix, dg = _rms_bwd(mix_ref[...], g_ref[...], dh_ref[...])
        dg_ref[...] += dg
        dmixb = dmix.astype(MXU_DTYPE)
        dmix_ref[...] = dmixb
        dmerged = _dot(dmixb, woutt_ref[...])
        sg_s = _sigmoid(gs_ref[...])
        sg_a = _sigmoid(ga_ref[...])
        da1 = (dmerged * sg_s).astype(MXU_DTYPE)
        da2 = (dmerged * sg_a).astype(MXU_DTYPE)
        da1_ref[...] = da1
        da2_ref[...] = da2
        a1 = _dot(ys_ref[...], wos_ref[...])
        a2 = _dot(ya_ref[...], woa_ref[...])
        dgs_ref[...] = (dmerged * a1 * (sg_s * (1.0 - sg_s))).astype(MXU_DTYPE)
        dga_ref[...] = (dmerged * a2 * (sg_a * (1.0 - sg_a))).astype(MXU_DTYPE)
        dys_ref[...] = _dot(da1, wost_ref[...])
        dya_ref[...] = _dot(da2, woat_ref[...])

    row_d = pl.BlockSpec((tm, D), lambda i: (i, 0))
    full = lambda shape: pl.BlockSpec(shape, lambda i: (0,) * len(shape))
    return _pcall(
        body, name=f"merge_bwd_l{layer}", grid=(rows // tm,),
        in_specs=[row_d, row_d, pl.BlockSpec((tm, D_SSM), lambda i: (i, 0)), row_d,
                  row_d, pl.BlockSpec((tm, D), lambda i: (i, 1)),
                  full((D_SSM, D)), full((D_ATTN, D)), full((D, D_SSM)), full((D, D_ATTN)), full((D, D)),
                  pl.BlockSpec((None, 1, D), lambda i: (layer, 0, 0))],
        out_specs=[row_d, row_d, row_d, row_d, row_d, pl.BlockSpec((tm, D_SSM), lambda i: (i, 0)), row_d,
                   pl.BlockSpec((1, D), lambda i: (0, 0))],
        out_shape=[SDS((rows, D), MXU_DTYPE)] * 5 + [SDS((rows, D_SSM), F32), SDS((rows, D_ATTN), F32),
                                                      SDS((1, D), F32)],
        compiler_params=_cparams("arbitrary"),
    )(dhm, mix, y_ssm, y_attn, gates, gates, w_o_ssm, w_o_attn, w_o_ssm_t, w_o_attn_t, w_out_t, gain3)


def _attn_bwd(q, k, v, d_out, sinks, layer):
    rows = q.shape[0]
    n_blk = rows // BLK
    last = n_blk - 1

    def body(sink_ref, q_ref, km_ref, kp_ref, kc_ref, vm_ref, vp_ref, vc_ref, do_ref,
             dq_ref, dk_ref, dv_ref, dkm_ref, dvm_ref, ds_ref, dk_carry, dv_carry):
        i = pl.program_id(0)

        @pl.when(i == 0)
        def _():
            dkm_ref[...] = jnp.zeros_like(dkm_ref)
            dvm_ref[...] = jnp.zeros_like(dvm_ref)
            ds_ref[...] = jnp.zeros_like(ds_ref)
            dk_carry[...] = jnp.zeros_like(dk_carry)
            dv_carry[...] = jnp.zeros_like(dv_carry)

        @pl.when(i <= last)
        def _():
            bias = _attn_mask(i)
            for kvh in range(N_KV_HEADS):
                lanes = _head_lanes(kvh)
                k3 = jnp.concatenate([km_ref[:, lanes], kp_ref[:, lanes], kc_ref[:, lanes]], axis=0)
                v3 = jnp.concatenate([vm_ref[:, lanes], vp_ref[:, lanes], vc_ref[:, lanes]], axis=0)
                q4 = _group_rows(q_ref, kvh)
                do4 = _group_rows(do_ref, kvh).astype(MXU_DTYPE)
                p = _attn_probs(q4, k3, _group_bias(bias, sink_ref, layer, kvh))
                dp = _dot_nt(do4, v3)
                dsf = p * (dp - jnp.sum(dp * p, axis=-1, keepdims=True))
                dsc = dsf.astype(MXU_DTYPE)
                dv3 = _dot_tn(p.astype(MXU_DTYPE), do4)
                dk3 = _dot_tn(dsc, q4)
                dq4 = _dot(dsc, k3)
                for g in range(Q_PER_KV):
                    h = kvh * Q_PER_KV + g
                    dq_ref[:, _head_lanes(h)] = dq4[g * BLK:(g + 1) * BLK]
                    ds_ref[h:h + 1, :] += jnp.sum(dsf[g * BLK:(g + 1) * BLK, 0:BLK], axis=0, keepdims=True)
                dkm_ref[:, lanes] += dk3[0:BLK]
                dvm_ref[:, lanes] += dv3[0:BLK]
                dk_ref[:, lanes] = dk_carry[:, lanes] + dk3[BLK:2 * BLK]
                dv_ref[:, lanes] = dv_carry[:, lanes] + dv3[BLK:2 * BLK]
                dk_carry[:, lanes] = dk3[2 * BLK:3 * BLK]
                dv_carry[:, lanes] = dv3[2 * BLK:3 * BLK]

        @pl.when(i == last + 1)
        def _():
            dk_ref[...] = dk_carry[...]
            dv_ref[...] = dv_carry[...]

    cur = lambda i: (jnp.minimum(i, last), 0)
    prev = lambda i: (jnp.clip(i - 1, 0, last), 0)
    kv_meta = pl.BlockSpec((BLK, D_KV), lambda i: (0, 0))
    kv_prev = pl.BlockSpec((BLK, D_KV), prev)
    kv_cur = pl.BlockSpec((BLK, D_KV), cur)
    return _pcall(
        body, name=f"attn_bwd_l{layer}", grid=(n_blk + 1,),
        in_specs=[pl.BlockSpec(memory_space=pltpu.SMEM),
                  pl.BlockSpec((BLK, D_ATTN), cur),
                  kv_meta, kv_prev, kv_cur, kv_meta, kv_prev, kv_cur,
                  pl.BlockSpec((BLK, D_ATTN), cur)],
        out_specs=[pl.BlockSpec((BLK, D_ATTN), cur), kv_prev, kv_prev, kv_meta, kv_meta,
                   pl.BlockSpec((N_Q_HEADS, 128), lambda i: (0, 0))],
        out_shape=[SDS((rows, D_ATTN), F32), SDS((rows, D_KV), F32), SDS((rows, D_KV), F32),
                   SDS((BLK, D_KV), F32), SDS((BLK, D_KV), F32), SDS((N_Q_HEADS, 128), F32)],
        scratch_shapes=[pltpu.VMEM((BLK, D_KV), F32), pltpu.VMEM((BLK, D_KV), F32)],
        compiler_params=_cparams("arbitrary"),
    )(sinks, q, k, k, k, v, v, v, d_out)


def _rope_bwd(dq, dk, dv, dk_meta, dv_meta, cos, sin_a, sin_b, layer):
    rows = dq.shape[0]
    tm = _row_tile(rows)

    def body(dq_ref, dk_ref, dv_ref, dkm_ref, dvm_ref, c_ref, a_ref, b_ref, o_ref):
        c, a, b = c_ref[...], -a_ref[...], -b_ref[...]
        for t in range(8):
            x = dq_ref[:, t * 128:(t + 1) * 128]
            o_ref[:, t * 128:(t + 1) * 128] = (_rope_lanes(x, c, a, b) * ATTN_SCALE).astype(MXU_DTYPE)
        for t in range(2):
            x = dk_ref[:, t * 128:(t + 1) * 128]
            o_ref[:, D_ATTN + t * 128:D_ATTN + (t + 1) * 128] = _rope_lanes(x, c, a, b).astype(MXU_DTYPE)
        o_ref[:, D_ATTN + D_KV:] = dv_ref[...].astype(MXU_DTYPE)

        @pl.when(pl.program_id(0) == 0)
        def _():
            cb, ab, bb = c[0:BLK], a[0:BLK], b[0:BLK]
            is_meta = lax.broadcasted_iota(jnp.int32, (BLK, 128), 0) >= PAD_ROWS
            for t in range(2):
                x = dk_ref[0:BLK, t * 128:(t + 1) * 128] + jnp.where(is_meta, dkm_ref[:, t * 128:(t + 1) * 128], 0.0)
                o_ref[0:BLK, D_ATTN + t * 128:D_ATTN + (t + 1) * 128] = _rope_lanes(x, cb, ab, bb).astype(MXU_DTYPE)
                xv = dv_ref[0:BLK, t * 128:(t + 1) * 128] + jnp.where(is_meta, dvm_ref[:, t * 128:(t + 1) * 128], 0.0)
                o_ref[0:BLK, D_ATTN + D_KV + t * 128:D_ATTN + D_KV + (t + 1) * 128] = xv.astype(MXU_DTYPE)

    tab = pl.BlockSpec((tm, 128), lambda i: (i, 0))
    kv = pl.BlockSpec((tm, D_KV), lambda i: (i, 0))
    meta = pl.BlockSpec((BLK, D_KV), lambda i: (0, 0))
    return _pcall(
        body, name=f"rope_bwd_l{layer}", grid=(rows // tm,),
        in_specs=[pl.BlockSpec((tm, D_ATTN), lambda i: (i, 0)), kv, kv, meta, meta, tab, tab, tab],
        out_specs=pl.BlockSpec((tm, D_ATTN + 2 * D_KV), lambda i: (i, 0)),
        out_shape=SDS((rows, D_ATTN + 2 * D_KV), MXU_DTYPE),
        compiler_params=_cparams("parallel"),
    )(dq, dk, dv, dk_meta, dv_meta, cos, sin_a, sin_b)


def _s5_bwd(d_gated, y, u, carry_in, ssm, w_glu, b_glu3, layer):
    rows = y.shape[0]
    n_chunks = rows // BLK
    b_mat, c_mat, t_re, t_im, d_skip = (ssm[k] for k in ("b_mat", "c_mat", "t_re", "t_im", "d_skip"))

    def body(dz_ref, y_ref, u_ref, cin_ref, bm_ref, cm_ref, tre_ref, tim_ref, d_ref, wg_ref, bg_ref,
             du_ref, dwg_ref, dbg_ref, dd_ref, dbm_ref, dcm_ref, dab_ref,
             lam_carry, bu_scr, s_scr, sp_scr, g_scr, lam_scr):
        step = pl.program_id(0)
        chunk = n_chunks - 1 - step

        @pl.when(step == 0)
        def _():
            for r in (dwg_ref, dbg_ref, dd_ref, dbm_ref, dcm_ref, dab_ref, lam_carry):
                r[...] = jnp.zeros_like(r)

        y = y_ref[...]
        u = u_ref[...]
        d_o = dz_ref[...]
        z, t = _gelu_parts(y)
        zb = z.astype(MXU_DTYPE)
        sg = _sigmoid(_dot(zb, wg_ref[...]) + bg_ref[...])
        dgl = d_o * z * (sg * (1.0 - sg))
        dglb = dgl.astype(MXU_DTYPE)
        dz = d_o * sg + _dot_nt(dglb, wg_ref[...])
        dwg_ref[...] += _dot_tn(zb, dglb)
        dbg_ref[...] += jnp.sum(dgl, axis=0, keepdims=True)
        dy = dz * _gelu_grad(y, t)
        dd_ref[...] += jnp.sum(dy * u, axis=0, keepdims=True)
        grow = lax.broadcasted_iota(jnp.int32, (BLK, 128), 0) + chunk * BLK
        for sb in range(N_SB):
            cols = slice(sb * 128, (sb + 1) * 128)
            u_sb = u[:, cols].astype(MXU_DTYPE)
            dy_sb = dy[:, cols]
            dyb = dy_sb.astype(MXU_DTYPE)
            bu_scr[sb] = _dot(u_sb, bm_ref[sb])
            _scan_tiles(bu_scr.at[sb], s_scr.at[sb], tre_ref, tim_ref, sb,
                        cin_ref[2 * sb:2 * sb + 1, :], cin_ref[2 * sb + 1:2 * sb + 2, :], False, prev_ref=sp_scr.at[sb])
            dcm_ref[sb] += _dot_tn(s_scr[sb].astype(MXU_DTYPE), dyb)
            g_scr[sb] = _dot_nt(dyb, cm_ref[sb])
            n_r, n_i = _scan_tiles(g_scr.at[sb], lam_scr.at[sb], tre_ref, tim_ref, sb,
                                   lam_carry[2 * sb:2 * sb + 1, :], lam_carry[2 * sb + 1:2 * sb + 2, :], True)
            lam_carry[2 * sb:2 * sb + 1, :] = n_r
            lam_carry[2 * sb + 1:2 * sb + 2, :] = n_i
            lr, li = lam_scr[sb, :, :SB_STATES], lam_scr[sb, :, SB_STATES:]
            spr, spi = sp_scr[sb, :, :SB_STATES], sp_scr[sb, :, SB_STATES:]
            dab_ref[2 * sb:2 * sb + 1, :] += jnp.sum(spr * lr + spi * li, axis=0, keepdims=True)
            dab_ref[2 * sb + 1:2 * sb + 2, :] += jnp.sum(spr * li - spi * lr, axis=0, keepdims=True)
            lam = lam_scr[sb].astype(MXU_DTYPE)
            dbm_ref[sb] += _dot_tn(u_sb, lam)
            du = _dot_nt(lam, bm_ref[sb]) + d_ref[:, cols] * dy_sb
            du_ref[:, cols] = jnp.where(grow >= PAD_ROWS, du, 0.0).astype(MXU_DTYPE)

    rev = lambda j: (n_chunks - 1 - j, 0)
    full = lambda shape: pl.BlockSpec(shape, lambda j: (0,) * len(shape))
    of_layer = lambda shape: pl.BlockSpec((None,) + shape, lambda j: (layer,) + (0,) * len(shape))
    tables = [of_layer((N_SB, 8, SCAN_TILE, SB_STATES))] * 2
    chunk_scratch = pltpu.VMEM((N_SB, BLK, 2 * SB_STATES), F32)
    return _pcall(
        body, name=f"s5_bwd_l{layer}", grid=(n_chunks,),
        in_specs=[pl.BlockSpec((BLK, D_SSM), rev), pl.BlockSpec((BLK, D_SSM), rev), pl.BlockSpec((BLK, D_SSM), rev),
                  pl.BlockSpec((None, 8, SB_STATES), lambda j: (n_chunks - 1 - j, 0, 0)),
                  of_layer((N_SB, 128, 2 * SB_STATES)), of_layer((N_SB, 2 * SB_STATES, 128))] + tables + [
                  of_layer((1, D_SSM)), full((D_SSM, D_SSM)),
                  pl.BlockSpec((None, 1, D_SSM), lambda j: (layer, 0, 0))],
        out_specs=[pl.BlockSpec((BLK, D_SSM), rev), full((D_SSM, D_SSM)), full((1, D_SSM)), full((1, D_SSM)),
                   full((N_SB, 128, 2 * SB_STATES)), full((N_SB, 2 * SB_STATES, 128)), full((8, SB_STATES))],
        out_shape=[SDS((rows, D_SSM), MXU_DTYPE), SDS((D_SSM, D_SSM), F32), SDS((1, D_SSM), F32), SDS((1, D_SSM), F32),
                   SDS((N_SB, 128, 2 * SB_STATES), F32), SDS((N_SB, 2 * SB_STATES, 128), F32), SDS((8, SB_STATES), F32)],
        scratch_shapes=[pltpu.VMEM((8, SB_STATES), F32)] + [chunk_scratch] * 5,
        compiler_params=_cparams("arbitrary"),
    )(d_gated, y, u, carry_in, b_mat, c_mat, t_re, t_im, d_skip, w_glu, b_glu3)


DPROJ_PIECES = ((0, 1), (1, 3), (4, 2), (6, 2))


def _in_bwd(dproj_pieces, dhm, hres, gain3, w_in_g, layer):
    rows = hres.shape[0]
    tm = _row_tile(rows)

    def body(du_ref, dqkv_ref, dgs_ref, dga_ref, dh_ref, x_ref, g_ref, w_hbm, dx_ref, dg_ref,
             wt_scr, w_stage, w_sem):
        i = pl.program_id(0)
        _load_resident_transposed(w_hbm, wt_scr, w_stage, w_sem, i == 0)

        @pl.when(i == 0)
        def _():
            dg_ref[...] = jnp.zeros_like(dg_ref)

        dh = None
        for piece_ref, (first, count) in zip((du_ref, dqkv_ref, dgs_ref, dga_ref), DPROJ_PIECES):
            wt = wt_scr[first:first + count].reshape(count * COL_SHARD, D)
            part = _dot(piece_ref[...], wt)
            dh = part if dh is None else dh + part
        dx, dg = _rms_bwd(x_ref[...], g_ref[...], dh)
        dg_ref[...] += dg
        dx_ref[...] = dh_ref[...] + dx

    row_d = pl.BlockSpec((tm, D), lambda i: (i, 0))
    return _pcall(
        body, name=f"in_bwd_l{layer}", grid=(rows // tm,),
        in_specs=[pl.BlockSpec((tm, count * COL_SHARD), lambda i: (i, 0)) for _, count in DPROJ_PIECES] + [
                  row_d, row_d,
                  pl.BlockSpec((None, 1, D), lambda i: (layer, 0, 0)),
                  pl.BlockSpec(memory_space=pl.ANY)],
        out_specs=[row_d, pl.BlockSpec((1, D), lambda i: (0, 0))],
        out_shape=[SDS((rows, D), F32), SDS((1, D), F32)],
        scratch_shapes=[pltpu.VMEM((N_DEV, COL_SHARD, D), MXU_DTYPE),
                        pltpu.VMEM((2, D, COL_SHARD), MXU_DTYPE), pltpu.SemaphoreType.DMA((2,))],
        compiler_params=_cparams("arbitrary"),
    )(*dproj_pieces, dhm, hres, gain3, w_in_g)


_ADAM_C1 = 1.0 / (1.0 - ADAM_B1 ** ADAM_STEP)
_ADAM_C2 = 1.0 / (1.0 - ADAM_B2 ** ADAM_STEP)


def _adam_math(w, g, m, v):
    m = ADAM_B1 * m + (1.0 - ADAM_B1) * g
    v = ADAM_B2 * v + (1.0 - ADAM_B2) * (g * g)
    delta = -ADAM_LR * ((m * _ADAM_C1) / (jnp.sqrt(v * _ADAM_C2) + ADAM_EPS) + ADAM_WD * w)
    return delta, m, v


def _adamw_layers(parts0, parts1, w, m, v, name):
    _, rows, cols = w.shape
    tr = min(rows, (1 << 16) // cols)
    nt = rows // tr

    def body(p0_ref, p1_ref, w_ref, m_ref, v_ref, g_ref, d_ref, nm_ref, nv_ref):
        layer = pl.program_id(0)

        def run(p_ref):
            g = p_ref[0].astype(F32)
            for s in range(1, N_DEV):
                g = g + p_ref[s].astype(F32)
            delta, nm, nv = _adam_math(w_ref[...], g, m_ref[...], v_ref[...])
            g_ref[...] = g
            d_ref[...] = delta
            nm_ref[...] = nm
            nv_ref[...] = nv

        @pl.when(layer == 0)
        def _():
            run(p0_ref)

        @pl.when(layer == 1)
        def _():
            run(p1_ref)

    wspec = pl.BlockSpec((None, tr, cols), lambda l, i: (l, i, 0))
    return _pcall(
        body, name=name, grid=(2, nt),
        in_specs=[pl.BlockSpec((N_DEV, tr, cols), lambda l, i: (0, jnp.where(l == 0, i, nt - 1), 0)),
                  pl.BlockSpec((N_DEV, tr, cols), lambda l, i: (0, jnp.where(l == 1, i, 0), 0)),
                  wspec, wspec, wspec],
        out_specs=[wspec] * 4, out_shape=[SDS(w.shape, F32)] * 4,
        compiler_params=_cparams("arbitrary", "arbitrary"),
    )(parts0, parts1, w, m, v)


def _sum_slots(parts, name):
    def body(p_ref, o_ref):
        acc = p_ref[0]
        for s in range(1, N_DEV):
            acc = acc + p_ref[s]
        o_ref[...] = acc

    vmem = pl.BlockSpec(memory_space=pltpu.VMEM)
    return _pcall(body, name=name, out_shape=SDS(parts.shape[1:], F32), in_specs=[vmem], out_specs=vmem,
                  compiler_params=_cparams())(parts)


def _adamw_packed(g, w, m, v, name):
    def body(g_ref, w_ref, m_ref, v_ref, d_ref, nm_ref, nv_ref):
        delta, nm, nv = _adam_math(w_ref[...], g_ref[...], m_ref[...], v_ref[...])
        d_ref[...] = delta
        nm_ref[...] = nm
        nv_ref[...] = nv

    vmem = pl.BlockSpec(memory_space=pltpu.VMEM)
    return _pcall(body, name=name, out_shape=[SDS(g.shape, F32)] * 3, in_specs=[vmem] * 4, out_specs=[vmem] * 3,
                  compiler_params=_cparams())(g, w, m, v)


def _ssm_discretize(a_re, a_im, log_dt, b_re, b_im):
    dt = jnp.exp(log_dt)[:, None]
    mag = jnp.exp(a_re * dt)
    ang = a_im * dt
    ab_re, ab_im = mag * jnp.cos(ang), mag * jnp.sin(ang)
    xr, xi = ab_re - 1.0, ab_im
    den = a_re * a_re + a_im * a_im
    q_re = (xr * a_re + xi * a_im) / den
    q_im = (xi * a_re - xr * a_im) / den
    bb_re = q_re[..., None] * b_re - q_im[..., None] * b_im
    bb_im = q_re[..., None] * b_im + q_im[..., None] * b_re
    return ab_re, ab_im, bb_re, bb_im


def _block_diag_b(bb):
    m = jnp.einsum("sgnc,gh->sgchn", bb.reshape(N_SB, 8, N_STATE, GROUP_CH), jnp.eye(8, dtype=F32))
    return m.reshape(N_SB, 128, SB_STATES)


def _block_diag_b_t(dm):
    return jnp.einsum("sgchn,gh->sgnc", dm.reshape(N_SB, 8, GROUP_CH, 8, N_STATE),
                      jnp.eye(8, dtype=F32)).reshape(N_GROUPS, N_STATE, GROUP_CH)


def _block_diag_c(cc):
    m = jnp.einsum("sgcn,gh->sgnhc", cc.reshape(N_SB, 8, GROUP_CH, N_STATE), jnp.eye(8, dtype=F32))
    return m.reshape(N_SB, SB_STATES, 128)


def _block_diag_c_t(dm):
    return jnp.einsum("sgnhc,gh->sgcn", dm.reshape(N_SB, 8, N_STATE, 8, GROUP_CH),
                      jnp.eye(8, dtype=F32)).reshape(N_GROUPS, GROUP_CH, N_STATE)


def _ssm_tables(ab_re, ab_im, bb_re, bb_im, c_re, c_im, d_skip):
    pr, pi = ab_re.reshape(1, -1), ab_im.reshape(1, -1)
    cr, ci = pr, pi
    squares = []
    for _ in range(3):
        squares.append((cr, ci))
        pr, pi = (jnp.concatenate([pr, pr * cr - pi * ci], axis=0),
                  jnp.concatenate([pi, pr * ci + pi * cr], axis=0))
        cr, ci = cr * cr - ci * ci, 2.0 * cr * ci
    r = jnp.arange(SCAN_TILE)[:, None]
    fwd = [(jnp.where(r >= (1 << k), squares[k][0], 0.0), jnp.where(r >= (1 << k), squares[k][1], 0.0))
           for k in range(3)] + [(pr, pi)]
    rev = [(jnp.where(r < SCAN_TILE - (1 << k), squares[k][0], 0.0),
            jnp.where(r < SCAN_TILE - (1 << k), -squares[k][1], 0.0)) for k in range(3)] + [(pr[::-1], -pi[::-1])]
    table = lambda part: jnp.stack([e[part] for e in fwd + rev]).reshape(
        8, SCAN_TILE, N_SB, SB_STATES).transpose(2, 0, 1, 3)
    return dict(
        b_mat=jnp.concatenate([_block_diag_b(bb_re), _block_diag_b(bb_im)], axis=-1).astype(MXU_DTYPE),
        c_mat=jnp.concatenate([_block_diag_c(c_re), -_block_diag_c(c_im)], axis=1).astype(MXU_DTYPE),
        t_re=table(0), t_im=table(1),
        d_skip=d_skip.reshape(1, D_SSM))


def _rope_tables(rows):
    pos = (jnp.arange(rows, dtype=jnp.int32) - PAD_ROWS).astype(F32)
    inv_freq = 1.0 / (ROPE_THETA ** (jnp.arange(0, HEAD_DIM, 2, dtype=F32) / HEAD_DIM))
    ang = pos[:, None] * inv_freq[None, :]
    ang = jnp.concatenate([ang, ang, ang, ang], axis=-1)
    first_half = (jnp.arange(128) % HEAD_DIM) < HEAD_DIM // 2
    sin = jnp.sin(ang)
    return jnp.cos(ang), jnp.where(first_half, -sin, 0.0), jnp.where(first_half, 0.0, sin)


def _pack(arrays):
    flat = jnp.concatenate([a.reshape(-1).astype(F32) for a in arrays])
    pad = (-flat.shape[0]) % 1024
    return jnp.pad(flat, (0, pad)).reshape(-1, 128)


def _unpack(packed, like):
    flat = packed.reshape(-1)
    out, off = [], 0
    for a in like:
        n = math.prod(a.shape)
        out.append(flat[off:off + n].reshape(a.shape))
        off += n
    return out


BIG = ("w_in", "w_glu", "w_o_ssm", "w_o_attn", "w_out", "w_up", "w_down")
WEIGHTS = ("meta_tokens", "norm_mix_pre", "norm_mix_post", "norm_mlp_pre", "norm_mlp_post", "w_in",
           "ssm_a_re", "ssm_a_im", "ssm_log_dt", "ssm_b_re", "ssm_b_im", "ssm_c_re", "ssm_c_im", "ssm_d",
           "w_glu", "b_glu", "attn_sinks", "w_o_ssm", "w_o_attn", "w_out", "w_up", "w_down")
SMALL = tuple(n for n in WEIGHTS if n not in BIG)


def kernel(x, meta_tokens, norm_mix_pre, norm_mix_post, norm_mlp_pre, norm_mlp_post, w_in, ssm_a_re, ssm_a_im, ssm_log_dt, ssm_b_re, ssm_b_im, ssm_c_re, ssm_c_im, ssm_d, w_glu, b_glu, attn_sinks, w_o_ssm, w_o_attn, w_out, w_up, w_down, loss_target, m_meta_tokens, m_norm_mix_pre, m_norm_mix_post, m_norm_mlp_pre, m_norm_mlp_post, m_w_in, m_ssm_a_re, m_ssm_a_im, m_ssm_log_dt, m_ssm_b_re, m_ssm_b_im, m_ssm_c_re, m_ssm_c_im, m_ssm_d, m_w_glu, m_b_glu, m_attn_sinks, m_w_o_ssm, m_w_o_attn, m_w_out, m_w_up, m_w_down, v_meta_tokens, v_norm_mix_pre, v_norm_mix_post, v_norm_mlp_pre, v_norm_mlp_post, v_w_in, v_ssm_a_re, v_ssm_a_im, v_ssm_log_dt, v_ssm_b_re, v_ssm_b_im, v_ssm_c_re, v_ssm_c_im, v_ssm_d, v_w_glu, v_b_glu, v_attn_sinks, v_w_o_ssm, v_w_o_attn, v_w_out, v_w_up, v_w_down):
    args = locals()
    w = {n: args[n] for n in WEIGHTS}
    m = {n: args["m_" + n] for n in WEIGHTS}
    v = {n: args["v_" + n] for n in WEIGHTS}
    n_layers = w_in.shape[0]
    seq = x.shape[1]
    rows = seq + BLK
    my_slot = _slot(_mesh_pos())

    assert n_layers == 2
    xfer = {n: [w[n][l].astype(XFER_DTYPE) for l in range(n_layers)] for n in BIG}
    mixer_small = ("w_glu", "w_o_ssm", "w_o_attn", "w_out")
    meta_g, w_in_g0 = _exchange_by_sequencer([meta_tokens, xfer["w_in"][0]], True, 0, "gather_in0")
    mix0_g = _exchange_by_sequencer([xfer[n][0] for n in mixer_small], True, 1, "gather_mix0")
    meta_full = meta_g.transpose(1, 0, 2).reshape(N_META, D)

    def mixer_weights(w_glu_g, w_o_ssm_g, w_o_attn_g, w_out_g):
        return dict(w_glu=w_glu_g.reshape(D_SSM, D_SSM), w_o_ssm=w_o_ssm_g.transpose(1, 0, 2).reshape(D_SSM, D),
                    w_o_attn=w_o_attn_g.reshape(D_ATTN, D), w_out=w_out_g.reshape(D, D),
                    w_o_ssm_t=w_o_ssm_g.transpose(0, 2, 1).reshape(D, D_SSM),
                    w_o_attn_t=w_o_attn_g.reshape(D_ATTN, D).T, w_out_t=w_out_g.reshape(D, D).T)

    gathered = [dict(w_in=w_in_g0, **mixer_weights(*mix0_g)), {}]

    gains = {n: w[n].reshape(n_layers, 1, D) for n in ("norm_mix_pre", "norm_mix_post", "norm_mlp_pre", "norm_mlp_post")}
    b_glu3 = b_glu.reshape(n_layers, 1, D_SSM)
    cos, sin_a, sin_b = _rope_tables(rows)

    disc, disc_vjp = jax.vjp(jax.vmap(_ssm_discretize), ssm_a_re, ssm_a_im, ssm_log_dt, ssm_b_re, ssm_b_im)
    ssm = jax.vmap(_ssm_tables)(*disc, ssm_c_re, ssm_c_im, ssm_d)

    hres = jnp.concatenate([jnp.zeros((PAD_ROWS, D), F32), meta_full, x[0]], axis=0)

    saved = []
    for l in range(n_layers):
        wl = gathered[l]
        u, gates, q, k, vv, h = _in_proj(hres, gains["norm_mix_pre"], wl["w_in"], cos, sin_a, sin_b, l)
        if l == 0:
            wl["w_up"], wl["w_down"] = _exchange_by_sequencer([xfer["w_up"][0], xfer["w_down"][0]], True, 2,
                                                              "gather_mlp0", after=[h])
        y, y_ssm, carry_in = _s5_fwd(u, ssm, wl["w_glu"], b_glu3, l)
        if l == 0:
            l1_g = _exchange_by_sequencer([xfer[n][1] for n in ("w_in",) + mixer_small + ("w_up", "w_down")], True, 3,
                                          "gather_l1", after=[y, wl["w_up"]])
            gathered[1] = dict(w_in=l1_g[0], w_up=l1_g[5], w_down=l1_g[6], **mixer_weights(*l1_g[1:5]))
            last_exchange = l1_g[:1]
        y_attn = _attn_fwd(q, k, vv, attn_sinks, l)
        merged, mix, hres_mid = _merge_fwd(y_ssm, y_attn, gates, hres, wl["w_o_ssm"], wl["w_o_attn"], wl["w_out"],
                                           gains["norm_mix_post"], l)
        if l + 1 < n_layers:
            up, h2, ff, hres_out = _mlp_fwd(hres_mid, gains["norm_mlp_pre"], gains["norm_mlp_post"], wl["w_up"],
                                            wl["w_down"], l)
        else:
            target = jnp.concatenate([jnp.zeros((BLK, D), F32), loss_target[0]], axis=0)
            up, h2, ff, dhres, loss_vec = _mlp_fwd(hres_mid, gains["norm_mlp_pre"], gains["norm_mlp_post"], wl["w_up"],
                                                   wl["w_down"], l, target=target)
        saved.append(dict(hres=hres, u=u, gates=gates, h=h, q=q, k=k, v=vv, y=y, y_ssm=y_ssm,
                          carry_in=carry_in, y_attn=y_attn, merged=merged, mix=mix, hres_mid=hres_mid,
                          up=up, h2=h2, ff=ff))
        hres = hres_out

    small_grads = {}
    recv_up, recv_down, recv_mix = [None] * n_layers, [None] * n_layers, [None] * n_layers
    for l in reversed(range(n_layers)):
        s = saved[l]
        wl = gathered[l]
        dff, dup, dhm, dg_mlp_post, dg_mlp_pre = _mlp_bwd(dhres, s["ff"], s["up"], s["hres_mid"], gains["norm_mlp_pre"],
                                                          gains["norm_mlp_post"], wl["w_up"], wl["w_down"], l)
        dw_up = _matmul_tn(s["h2"], dup, f"dw_up_l{l}", dev_major_cols=COL_SHARD)
        recv_up[l] = _exchange_by_sequencer([dw_up], False, 4 + 3 * l, f"scatter_up{l}", after=last_exchange)
        dw_down = _matmul_tn(s["up"], dff, f"dw_down_l{l}", a_fn=_relu_squared).reshape(N_DEV, COL_SHARD, D)
        recv_down[l] = _exchange_by_sequencer([dw_down], False, 5 + 3 * l, f"scatter_down{l}", after=recv_up[l])
        last_exchange = recv_down[l]
        dmix, da1, da2, dgs, dga, dy_ssm, dy_attn, dg_mix_post = _merge_bwd(
            dhm, s["mix"], s["y_ssm"], s["y_attn"], s["gates"], wl["w_o_ssm"], wl["w_o_attn"], wl["w_o_ssm_t"],
            wl["w_o_attn_t"], wl["w_out_t"], gains["norm_mix_post"], l)
        dw_out = _matmul_tn(s["merged"], dmix, f"dw_out_l{l}").reshape(N_DEV, D // N_DEV, D)
        dw_o_attn = _matmul_tn(s["y_attn"], da2, f"dw_o_attn_l{l}").reshape(N_DEV, D_ATTN // N_DEV, D)
        dw_o_ssm = _matmul_tn(s["y_ssm"], da1, f"dw_o_ssm_l{l}", dev_major_cols=D // N_DEV)
        if l == 0:
            recv_out0 = _exchange_by_sequencer([dw_o_ssm, dw_o_attn, dw_out], False, 11, "scatter_out0",
                                               after=last_exchange)
            last_exchange = recv_out0[:1]
        dq, dk, dv, dk_meta, dv_meta, dsink = _attn_bwd(s["q"], s["k"], s["v"], dy_attn, attn_sinks, l)
        dqkv = _rope_bwd(dq, dk, dv, dk_meta, dv_meta, cos, sin_a, sin_b, l)
        du, dw_glu, db_glu, dd_skip, db_mat, dc_mat, dab = _s5_bwd(dy_ssm, s["y"], s["u"], s["carry_in"], ssm,
                                                                    wl["w_glu"], b_glu3, l)
        dproj = (du, dqkv, dgs, dga)
        dw_in = _dw_in(s["h"], dproj, l)
        dhres, dg_mix_pre = _in_bwd(dproj, dhm, s["hres"], gains["norm_mix_pre"], wl["w_in"], l)
        mix_parts = [dw_in, dw_glu.astype(XFER_DTYPE).reshape(N_DEV, D_SSM // N_DEV, D_SSM), dw_o_ssm, dw_o_attn, dw_out]
        if l > 0:
            recv_mix[l] = _exchange_by_sequencer(mix_parts, False, 6 + 3 * l, f"scatter_mix{l}", after=last_exchange)
            last_exchange = recv_mix[l][:1]

        for name, val in (("norm_mix_pre", dg_mix_pre[0]), ("norm_mix_post", dg_mix_post[0]),
                          ("norm_mlp_pre", dg_mlp_pre[0]), ("norm_mlp_post", dg_mlp_post[0]),
                          ("dab", dab), ("db_mat", db_mat), ("dc_mat", dc_mat),
                          ("ssm_d", dd_skip.reshape(N_GROUPS, GROUP_CH)), ("b_glu", db_glu[0]),
                          ("attn_sinks", dsink[:, 0])):
            small_grads.setdefault(name, [None] * n_layers)[l] = val

    grad_x = dhres[BLK:][None]
    stacked = {n: jnp.stack(v) for n, v in small_grads.items()}
    dab = stacked["dab"].reshape(n_layers, N_SB, 2, SB_STATES)
    db_mat, dc_mat = stacked["db_mat"], stacked["dc_mat"]
    b_t, c_t = jax.vmap(_block_diag_b_t), jax.vmap(_block_diag_c_t)
    (stacked["ssm_a_re"], stacked["ssm_a_im"], stacked["ssm_log_dt"], stacked["ssm_b_re"],
     stacked["ssm_b_im"]) = disc_vjp((dab[:, :, 0].reshape(n_layers, N_GROUPS, N_STATE),
                                      dab[:, :, 1].reshape(n_layers, N_GROUPS, N_STATE),
                                      b_t(db_mat[..., :SB_STATES]), b_t(db_mat[..., SB_STATES:])))
    stacked["ssm_c_re"] = c_t(dc_mat[:, :, :SB_STATES])
    stacked["ssm_c_im"] = -c_t(dc_mat[:, :, SB_STATES:])
    small_names = [n for n in SMALL if n != "meta_tokens"]
    partial_small = [dhres[PAD_ROWS:BLK]] + [stacked[n] for n in small_names] + [loss_vec[0, :1]]
    recv_in0, recv_glu0, small_parts = _exchange_by_sequencer(
        mix_parts[:2] + [_pack(partial_small)], [False, False, True], 6, "scatter_in0", after=last_exchange)
    recv_mix[0] = [recv_in0, recv_glu0] + recv_out0

    grads, delta, new_m, new_v = {}, {}, {}, {}

    def adamw_big(names, recv0, recv1):
        for n, p0, p1 in zip(names, recv0, recv1):
            grads[n], delta[n], new_m[n], new_v[n] = _adamw_layers(p0, p1, w[n], m[n], v[n], f"adamw_{n}")

    adamw_big(("w_up", "w_down"), recv_up[0] + recv_down[0], recv_up[1] + recv_down[1])
    summed = _unpack(_sum_slots(small_parts, "sum_small_grads"), partial_small)
    loss = summed[-1][0]
    grads.update(zip(small_names, summed[1:-1]))
    grads["meta_tokens"] = lax.dynamic_slice_in_dim(summed[0], my_slot * (D // N_DEV), D // N_DEV, axis=1)
    like = [w[n] for n in SMALL]
    d_s, m_s, v_s = _adamw_packed(_pack([grads[n] for n in SMALL]), _pack(like), _pack([m[n] for n in SMALL]),
                                  _pack([v[n] for n in SMALL]), "adamw_small")
    adamw_big(("w_in",) + mixer_small, recv_mix[0], recv_mix[1])
    for n, dd, mm, vs in zip(SMALL, _unpack(d_s, like), _unpack(m_s, like), _unpack(v_s, like)):
        delta[n], new_m[n], new_v[n] = dd, mm, vs

    return (loss, grad_x, *[grads[n] for n in WEIGHTS], *[delta[n] for n in WEIGHTS],
            *[new_m[n] for n in WEIGHTS], *[new_v[n] for n in WEIGHTS])
```

```python
import functools
import math

import jax
import jax.numpy as jnp
from jax import lax
from jax.experimental import pallas as pl
from jax.experimental.pallas import tpu as pltpu
from jax.experimental.pallas import tpu_sc as plsc

F32 = jnp.float32
MXU_DTYPE = jnp.bfloat16
XFER_DTYPE = MXU_DTYPE
_pcall = pl.pallas_call
SDS = jax.ShapeDtypeStruct

D = 1024
D_SSM = 512
D_ATTN = 1024
D_KV = 256
D_FF = 4096
D_IN = 4096
HEAD_DIM = 64
N_Q_HEADS = 16
N_KV_HEADS = 4
Q_PER_KV = 4
N_META = 16
BLK = 128
PAD_ROWS = BLK - N_META
N_GROUPS = 32
N_STATE = 64
GROUP_CH = 16
N_SB = 4
SB_STATES = 512
ROPE_THETA = 10000.0
ATTN_SCALE = HEAD_DIM ** -0.5
NEG_INF = -1e30
RMS_EPS = 1e-6
N_DEV = 8
COL_SHARD = 512

ADAM_LR = 0.001
ADAM_B1 = 0.9
ADAM_B2 = 0.999
ADAM_EPS = 1e-08
ADAM_WD = 0.01
ADAM_STEP = 10

VMEM_LIMIT = 56 * 1024 * 1024
MESH_AXES = ("x", "y", "c")

_NT = (((1,), (1,)), ((), ()))
_TN = (((0,), (0,)), ((), ()))


def _cparams(*sem):
    return pltpu.CompilerParams(dimension_semantics=tuple(sem) if sem else None,
                                vmem_limit_bytes=VMEM_LIMIT)


def _row_tile(rows, cap=640):
    for t in (1664, 640, 512, 320, 256, 128):
        if t <= cap and rows % t == 0:
            return t
    raise ValueError(f"unsupported row count {rows}")


def _dot(a, b):
    return jnp.dot(a, b, preferred_element_type=F32)


def _dot_nt(a, b):
    return lax.dot_general(a, b, _NT, preferred_element_type=F32)


def _dot_tn(a, b):
    return lax.dot_general(a, b, _TN, preferred_element_type=F32)


def _sigmoid(x):
    return 1.0 / (1.0 + jnp.exp(-x))


_GELU_C = math.sqrt(2.0 / math.pi)


def _gelu_parts(y):
    t = jnp.tanh(_GELU_C * (y + 0.044715 * (y * y * y)))
    return 0.5 * y * (1.0 + t), t


def _gelu_grad(y, t):
    return 0.5 * (1.0 + t) + 0.5 * y * (1.0 - t * t) * (_GELU_C * (1.0 + 0.134145 * (y * y)))


def _rms_fwd(x, gain):
    r = lax.rsqrt(jnp.mean(x * x, axis=-1, keepdims=True) + RMS_EPS)
    return (x * r) * gain


def _rms_bwd(x, gain, dout):
    r = lax.rsqrt(jnp.mean(x * x, axis=-1, keepdims=True) + RMS_EPS)
    xh = x * r
    dxh = dout * gain
    dx = r * (dxh - xh * jnp.mean(dxh * xh, axis=-1, keepdims=True))
    return dx, jnp.sum(dout * xh, axis=0, keepdims=True)


def _mesh_pos():
    return lax.axis_index("x"), lax.axis_index("y"), lax.axis_index("c")


def _peer(pos, d):
    x, y, c = pos
    return (1 - x if d & 4 else x, 1 - y if d & 2 else y, 1 - c if d & 1 else c)


def _slot(pos):
    return 4 * pos[0] + 2 * pos[1] + pos[2]


def _exchange_copy(gather, src_ref, land_ref, sems, k, d, me, send_side):
    peer = _peer(me, d)
    sender = me if send_side else peer
    src = src_ref if gather else src_ref.at[_slot(peer) if send_side else _slot(me)]
    return pltpu.make_async_remote_copy(
        src_ref=src, dst_ref=land_ref.at[_slot(sender)],
        send_sem=sems[0].at[k * (N_DEV - 1) + d - 1], recv_sem=sems[1].at[k * (N_DEV - 1) + d - 1],
        device_id=peer, device_id_type=pl.DeviceIdType.MESH)


def _exchange_by_sequencer(srcs, gather, collective_id, name, after=()):
    n = len(srcs)
    flags = [gather] * n if isinstance(gather, bool) else list(gather)
    land_types = [SDS(((N_DEV,) + s.shape) if g else s.shape, s.dtype) for s, g in zip(srcs, flags)]

    def body(*refs):
        src_refs = refs[:n]
        land_refs = refs[n + len(after):2 * n + len(after)]
        sems = refs[2 * n + len(after):2 * n + len(after) + 2]
        local_sems = refs[2 * n + len(after) + 2]
        me = _mesh_pos()
        barrier = pltpu.get_barrier_semaphore()
        for d in range(1, N_DEV):
            pl.semaphore_signal(barrier, inc=1, device_id=_peer(me, d), device_id_type=pl.DeviceIdType.MESH)
        pl.semaphore_wait(barrier, N_DEV - 1)
        own = [pltpu.make_async_copy(src_refs[k] if flags[k] else src_refs[k].at[_slot(me)],
                                     land_refs[k].at[_slot(me)], local_sems.at[k]) for k in range(n)]
        for cp in own:
            cp.start()
        for k in range(n):
            for d in range(1, N_DEV):
                _exchange_copy(flags[k], src_refs[k], land_refs[k], sems, k, d, me, True).start()
        for cp in own:
            cp.wait()
        for k in range(n):
            for d in range(1, N_DEV):
                _exchange_copy(flags[k], src_refs[k], land_refs[k], sems, k, d, me, True).wait_send()
        for k in range(n):
            for d in range(1, N_DEV):
                _exchange_copy(flags[k], src_refs[k], land_refs[k], sems, k, d, me, False).wait_recv()

    sem_type = pltpu.SemaphoreType.DMA((n * (N_DEV - 1),))
    return pl.kernel(
        body, out_type=land_types, mesh=plsc.ScalarSubcoreMesh(axis_name="sequencer", num_cores=1), name=name,
        scratch_types=(sem_type, sem_type, pltpu.SemaphoreType.DMA((n,))),
        compiler_params=pltpu.CompilerParams(collective_id=collective_id),
    )(*srcs, *after)


def _load_resident(w_hbm, w_scr, sems, first_step):
    @pl.when(first_step)
    def _():
        copies = [pltpu.make_async_copy(w_hbm.at[s], w_scr.at[s], sems.at[s]) for s in range(N_DEV)]
        for cp in copies:
            cp.start()
        for cp in copies:
            cp.wait()


def _load_resident_transposed(w_hbm, w_scr, stage, sems, first_step):
    @pl.when(first_step)
    def _():
        copies = [pltpu.make_async_copy(w_hbm.at[s], stage.at[s % 2], sems.at[s % 2]) for s in range(N_DEV)]
        copies[0].start()
        for s in range(N_DEV):
            if s + 1 < N_DEV:
                copies[s + 1].start()
            copies[s].wait()
            w_scr[s] = stage[s % 2].T


def _rope_lanes(t, cos, sin_a, sin_b):
    return t * cos + pltpu.roll(t, 96, 1) * sin_a + pltpu.roll(t, 32, 1) * sin_b


def _in_proj(hres, gain3, w_in_g, cos, sin_a, sin_b, layer):
    rows = hres.shape[0]
    tm = _row_tile(rows, 320)

    def body(x_ref, g_ref, w_hbm, c_ref, a_ref, b_ref, u_ref, gate_ref, q_ref, k_ref, v_ref, h_ref, w_scr, w_sem):
        _load_resident(w_hbm, w_scr, w_sem, pl.program_id(0) == 0)
        hn = _rms_fwd(x_ref[...], g_ref[...]).astype(MXU_DTYPE)
        h_ref[...] = hn
        c, a, b = c_ref[...], a_ref[...], b_ref[...]
        u_ref[...] = _dot(hn, w_scr[0])
        for shard in (1, 2):
            res = _dot(hn, w_scr[shard])
            for t in range(4):
                lanes = slice(t * 128, (t + 1) * 128)
                out = slice((shard - 1) * COL_SHARD + t * 128, (shard - 1) * COL_SHARD + (t + 1) * 128)
                q_ref[:, out] = (_rope_lanes(res[:, lanes], c, a, b) * ATTN_SCALE).astype(MXU_DTYPE)
        res = _dot(hn, w_scr[3])
        for t in range(2):
            lanes = slice(t * 128, (t + 1) * 128)
            k_ref[:, lanes] = _rope_lanes(res[:, lanes], c, a, b).astype(MXU_DTYPE)
        v_ref[...] = res[:, D_KV:].astype(MXU_DTYPE)
        for shard in range(4, N_DEV):
            gate_ref[:, (shard - 4) * COL_SHARD:(shard - 3) * COL_SHARD] = _dot(hn, w_scr[shard])

    tab = pl.BlockSpec((tm, 128), lambda i: (i, 0))
    kv = pl.BlockSpec((tm, D_KV), lambda i: (i, 0))
    row_d = pl.BlockSpec((tm, D), lambda i: (i, 0))
    return _pcall(
        body, name=f"in_proj_l{layer}", grid=(rows // tm,),
        in_specs=[row_d, pl.BlockSpec((None, 1, D), lambda i: (layer, 0, 0)),
                  pl.BlockSpec(memory_space=pl.ANY), tab, tab, tab],
        out_specs=[pl.BlockSpec((tm, D_SSM), lambda i: (i, 0)), pl.BlockSpec((tm, 2 * D), lambda i: (i, 0)),
                   row_d, kv, kv, row_d],
        out_shape=[SDS((rows, D_SSM), F32), SDS((rows, 2 * D), F32), SDS((rows, D_ATTN), MXU_DTYPE),
                   SDS((rows, D_KV), MXU_DTYPE), SDS((rows, D_KV), MXU_DTYPE), SDS((rows, D), MXU_DTYPE)],
        scratch_shapes=[pltpu.VMEM((N_DEV, D, COL_SHARD), MXU_DTYPE), pltpu.SemaphoreType.DMA((N_DEV,))],
        compiler_params=_cparams("arbitrary"),
    )(hres, gain3, w_in_g, cos, sin_a, sin_b)


SCAN_TILE = 8


def _scan_tiles(x_ref, out_ref, tre_ref, tim_ref, sb, t_r, t_i, reverse, prev_ref=None):
    base = 4 if reverse else 0
    n_tiles = BLK // SCAN_TILE
    row = lax.broadcasted_iota(jnp.int32, (SCAN_TILE, SB_STATES), 0)
    for j in (range(n_tiles - 1, -1, -1) if reverse else range(n_tiles)):
        rows = slice(SCAN_TILE * j, SCAN_TILE * (j + 1))
        xr = x_ref[rows, :SB_STATES]
        xi = x_ref[rows, SB_STATES:]
        for k in range(3):
            shift = SCAN_TILE - (1 << k) if reverse else (1 << k)
            rr = pltpu.roll(xr, shift, 0)
            ri = pltpu.roll(xi, shift, 0)
            ar = tre_ref[sb, base + k]
            ai = tim_ref[sb, base + k]
            xr, xi = xr + (ar * rr - ai * ri), xi + (ar * ri + ai * rr)
        pr = tre_ref[sb, base + 3]
        pi = tim_ref[sb, base + 3]
        xr, xi = xr + (pr * t_r - pi * t_i), xi + (pr * t_i + pi * t_r)
        out_ref[rows, :SB_STATES] = xr
        out_ref[rows, SB_STATES:] = xi
        if prev_ref is not None:
            prev_ref[rows, :SB_STATES] = jnp.where(row == 0, t_r, pltpu.roll(xr, 1, 0))
            prev_ref[rows, SB_STATES:] = jnp.where(row == 0, t_i, pltpu.roll(xi, 1, 0))
        edge = slice(0, 1) if reverse else slice(SCAN_TILE - 1, SCAN_TILE)
        t_r, t_i = xr[edge], xi[edge]
    return t_r, t_i


def _s5_fwd(u, ssm, w_glu, b_glu3, layer):
    rows = u.shape[0]
    n_chunks = rows // BLK
    b_mat, c_mat, t_re, t_im, d_skip = (ssm[k] for k in ("b_mat", "c_mat", "t_re", "t_im", "d_skip"))

    def body(u_ref, bm_ref, cm_ref, tre_ref, tim_ref, d_ref, wg_ref, bg_ref,
             y_ref, ys_ref, cin_ref, carry, bu_scr, s_scr):
        @pl.when(pl.program_id(0) == 0)
        def _():
            carry[...] = jnp.zeros_like(carry)

        cin_ref[...] = carry[...]
        u = u_ref[...]
        for sb in range(N_SB):
            cols = slice(sb * 128, (sb + 1) * 128)
            u_sb = u[:, cols]
            bu_scr[sb] = _dot(u_sb.astype(MXU_DTYPE), bm_ref[sb])
            t_r, t_i = _scan_tiles(bu_scr.at[sb], s_scr.at[sb], tre_ref, tim_ref, sb,
                                   carry[2 * sb:2 * sb + 1, :], carry[2 * sb + 1:2 * sb + 2, :], False)
            carry[2 * sb:2 * sb + 1, :] = t_r
            carry[2 * sb + 1:2 * sb + 2, :] = t_i
            y_ref[:, cols] = _dot(s_scr[sb].astype(MXU_DTYPE), cm_ref[sb]) + d_ref[:, cols] * u_sb
        z, _ = _gelu_parts(y_ref[...])
        gl = _dot(z.astype(MXU_DTYPE), wg_ref[...]) + bg_ref[...]
        ys_ref[...] = (z * _sigmoid(gl)).astype(MXU_DTYPE)

    full = lambda shape: pl.BlockSpec(shape, lambda j: (0,) * len(shape))
    of_layer = lambda shape: pl.BlockSpec((None,) + shape, lambda j: (layer,) + (0,) * len(shape))
    return _pcall(
        body, name=f"s5_fwd_l{layer}", grid=(n_chunks,),
        in_specs=[pl.BlockSpec((BLK, D_SSM), lambda j: (j, 0)),
                  of_layer((N_SB, 128, 2 * SB_STATES)), of_layer((N_SB, 2 * SB_STATES, 128)),
                  of_layer((N_SB, 8, SCAN_TILE, SB_STATES)), of_layer((N_SB, 8, SCAN_TILE, SB_STATES)),
                  of_layer((1, D_SSM)), full((D_SSM, D_SSM)),
                  pl.BlockSpec((None, 1, D_SSM), lambda j: (layer, 0, 0))],
        out_specs=[pl.BlockSpec((BLK, D_SSM), lambda j: (j, 0)), pl.BlockSpec((BLK, D_SSM), lambda j: (j, 0)),
                   pl.BlockSpec((None, 8, SB_STATES), lambda j: (j, 0, 0))],
        out_shape=[SDS((rows, D_SSM), F32), SDS((rows, D_SSM), MXU_DTYPE), SDS((n_chunks, 8, SB_STATES), F32)],
        scratch_shapes=[pltpu.VMEM((8, SB_STATES), F32), pltpu.VMEM((N_SB, BLK, 2 * SB_STATES), F32),
                        pltpu.VMEM((N_SB, BLK, 2 * SB_STATES), F32)],
        compiler_params=_cparams("arbitrary"),
    )(u, b_mat, c_mat, t_re, t_im, d_skip, w_glu, b_glu3)


def _attn_mask(i):
    row = lax.broadcasted_iota(jnp.int32, (BLK, 3 * BLK), 0) + i * BLK
    col = lax.broadcasted_iota(jnp.int32, (BLK, 3 * BLK), 1)
    seg = jnp.right_shift(col, 7)
    c = jnp.bitwise_and(col, BLK - 1)
    kidx = c + (i + seg - 2) * BLK
    ok_meta = (seg == 0) & (c >= PAD_ROWS) & (row - c >= BLK)
    ok_win = (seg > 0) & (kidx >= PAD_ROWS) & (kidx <= row) & (row - kidx < BLK)
    return jnp.where(ok_meta | ok_win, 0.0, NEG_INF)


def _head_lanes(h):
    return slice(h * HEAD_DIM, (h + 1) * HEAD_DIM)


def _group_rows(ref, kvh):
    return jnp.concatenate([ref[:, _head_lanes(kvh * Q_PER_KV + g)] for g in range(Q_PER_KV)], axis=0)


def _group_bias(bias, sink_ref, layer, kvh):
    first_col = lax.broadcasted_iota(jnp.int32, (BLK, BLK), 1) == 0
    slabs = []
    for g in range(Q_PER_KV):
        first = jnp.where(first_col, sink_ref[layer, kvh * Q_PER_KV + g], bias[:, :BLK])
        slabs.append(jnp.concatenate([first, bias[:, BLK:]], axis=1))
    return jnp.concatenate(slabs, axis=0)


def _attn_probs(q4, k3, bias4):
    s = _dot_nt(q4, k3) + bias4
    e = jnp.exp(s - jnp.max(s, axis=-1, keepdims=True))
    return e * (1.0 / jnp.sum(e, axis=-1, keepdims=True))


def _attn_fwd(q, k, v, sinks, layer):
    rows = q.shape[0]
    n_blk = rows // BLK

    def body(sink_ref, q_ref, km_ref, kp_ref, kc_ref, vm_ref, vp_ref, vc_ref, o_ref):
        bias = _attn_mask(pl.program_id(0))
        for kvh in range(N_KV_HEADS):
            lanes = _head_lanes(kvh)
            k3 = jnp.concatenate([km_ref[:, lanes], kp_ref[:, lanes], kc_ref[:, lanes]], axis=0)
            v3 = jnp.concatenate([vm_ref[:, lanes], vp_ref[:, lanes], vc_ref[:, lanes]], axis=0)
            p = _attn_probs(_group_rows(q_ref, kvh), k3, _group_bias(bias, sink_ref, layer, kvh))
            o4 = _dot(p.astype(MXU_DTYPE), v3).astype(MXU_DTYPE)
            for g in range(Q_PER_KV):
                o_ref[:, _head_lanes(kvh * Q_PER_KV + g)] = o4[g * BLK:(g + 1) * BLK]

    kv_meta = pl.BlockSpec((BLK, D_KV), lambda i: (0, 0))
    kv_prev = pl.BlockSpec((BLK, D_KV), lambda i: (jnp.maximum(i - 1, 0), 0))
    kv_cur = pl.BlockSpec((BLK, D_KV), lambda i: (i, 0))
    return _pcall(
        body, name=f"attn_fwd_l{layer}", grid=(n_blk,),
        in_specs=[pl.BlockSpec(memory_space=pltpu.SMEM),
                  pl.BlockSpec((BLK, D_ATTN), lambda i: (i, 0)),
                  kv_meta, kv_prev, kv_cur, kv_meta, kv_prev, kv_cur],
        out_specs=pl.BlockSpec((BLK, D_ATTN), lambda i: (i, 0)),
        out_shape=SDS((rows, D_ATTN), MXU_DTYPE),
        compiler_params=_cparams("parallel"),
    )(sinks, q, k, k, k, v, v, v)


def _merge_fwd(y_ssm, y_attn, gates, hres, w_o_ssm, w_o_attn, w_out, gain3, layer):
    rows = hres.shape[0]
    tm = _row_tile(rows, 320)

    def body(ys_ref, ya_ref, gs_ref, ga_ref, x_ref, wos_ref, woa_ref, wout_ref, g_ref,
             mg_ref, mix_ref, out_ref):
        a1 = _dot(ys_ref[...], wos_ref[...])
        a2 = _dot(ya_ref[...], woa_ref[...])
        merged = (_sigmoid(gs_ref[...]) * a1 + _sigmoid(ga_ref[...]) * a2).astype(MXU_DTYPE)
        mg_ref[...] = merged
        mix = _dot(merged, wout_ref[...])
        mix_ref[...] = mix
        out_ref[...] = x_ref[...] + _rms_fwd(mix, g_ref[...])

    row_d = pl.BlockSpec((tm, D), lambda i: (i, 0))
    full = lambda shape: pl.BlockSpec(shape, lambda i: (0,) * len(shape))
    return _pcall(
        body, name=f"merge_fwd_l{layer}", grid=(rows // tm,),
        in_specs=[pl.BlockSpec((tm, D_SSM), lambda i: (i, 0)), row_d,
                  row_d, pl.BlockSpec((tm, D), lambda i: (i, 1)), row_d,
                  full((D_SSM, D)), full((D_ATTN, D)), full((D, D)),
                  pl.BlockSpec((None, 1, D), lambda i: (layer, 0, 0))],
        out_specs=[row_d, row_d, row_d],
        out_shape=[SDS((rows, D), MXU_DTYPE), SDS((rows, D), F32), SDS((rows, D), F32)],
        compiler_params=_cparams("parallel"),
    )(y_ssm, y_attn, gates, gates, hres, w_o_ssm, w_o_attn, w_out, gain3)


def _mlp_fwd(hres, gain_pre3, gain_post3, w_up_g, w_down_g, layer, target=None):
    rows = hres.shape[0]
    tm = _row_tile(rows, 320)

    def body(x_ref, gp_ref, gq_ref, wu_hbm, wd_hbm, *refs):
        if target is None:
            up_ref, h_ref, ff_ref, out_ref, act_scr, wu_scr, wd_scr, wu_sem, wd_sem = refs
        else:
            t_ref, up_ref, h_ref, ff_ref, out_ref, loss_ref, act_scr, wu_scr, wd_scr, wu_sem, wd_sem = refs
        first = pl.program_id(0) == 0
        _load_resident(wu_hbm, wu_scr, wu_sem, first)
        _load_resident(wd_hbm, wd_scr, wd_sem, first)
        hn = _rms_fwd(x_ref[...], gp_ref[...]).astype(MXU_DTYPE)
        h_ref[...] = hn
        for kf in range(N_DEV):
            cols = slice(kf * COL_SHARD, (kf + 1) * COL_SHARD)
            up = _dot(hn, wu_scr[kf])
            up_ref[:, cols] = up.astype(MXU_DTYPE)
            r = jnp.maximum(up, 0.0)
            act_scr[:, cols] = (r * r).astype(MXU_DTYPE)
        ff = _dot(act_scr[...], wd_scr[...].reshape(D_FF, D))
        ff_ref[...] = ff
        out = x_ref[...] + _rms_fwd(ff, gq_ref[...])
        if target is None:
            out_ref[...] = out
        else:
            @pl.when(first)
            def _():
                loss_ref[...] = jnp.zeros_like(loss_ref)

            row = lax.broadcasted_iota(jnp.int32, (tm, D), 0) + pl.program_id(0) * tm
            err = jnp.where(row >= BLK, out - t_ref[...], 0.0)
            out_ref[...] = err * (1.0 / D)
            loss_ref[...] += jnp.sum(err * err) * (0.5 / D)

    row_d = pl.BlockSpec((tm, D), lambda i: (i, 0))
    gain = pl.BlockSpec((None, 1, D), lambda i: (layer, 0, 0))
    with_loss = target is not None
    return _pcall(
        body, name=f"mlp_fwd_l{layer}", grid=(rows // tm,),
        in_specs=[row_d, gain, gain, pl.BlockSpec(memory_space=pl.ANY), pl.BlockSpec(memory_space=pl.ANY)]
        + [row_d] * with_loss,
        out_specs=[pl.BlockSpec((tm, D_FF), lambda i: (i, 0)), row_d, row_d, row_d]
        + [pl.BlockSpec((1, 128), lambda i: (0, 0))] * with_loss,
        out_shape=[SDS((rows, D_FF), MXU_DTYPE), SDS((rows, D), MXU_DTYPE), SDS((rows, D), F32), SDS((rows, D), F32)]
        + [SDS((1, 128), F32)] * with_loss,
        scratch_shapes=[pltpu.VMEM((tm, D_FF), MXU_DTYPE),
                        pltpu.VMEM((N_DEV, D, COL_SHARD), MXU_DTYPE), pltpu.VMEM((N_DEV, COL_SHARD, D), MXU_DTYPE),
                        pltpu.SemaphoreType.DMA((N_DEV,)), pltpu.SemaphoreType.DMA((N_DEV,))],
        compiler_params=_cparams("arbitrary"),
    )(hres, gain_pre3, gain_post3, w_up_g, w_down_g, *([target] if with_loss else []))


def _relu_squared(up):
    r = jnp.maximum(up.astype(F32), 0.0)
    return (r * r).astype(MXU_DTYPE)


def _matmul_tn(a, b, name, dev_major_cols=None, a_fn=None):
    rows, ka = a.shape
    n = b.shape[1]
    ta = min(ka, 1024)
    tn = 1024 if n % 1024 == 0 else 512
    tr = _row_tile(rows, 1664)
    n_r = rows // tr

    def body(a_ref, b_ref, o_ref, acc):
        r = pl.program_id(2)

        @pl.when(r == 0)
        def _():
            acc[...] = jnp.zeros_like(acc)

        a_blk = a_ref[...] if a_fn is None else a_fn(a_ref[...])
        acc[...] += _dot_tn(a_blk, b_ref[...])

        @pl.when(r == n_r - 1)
        def _():
            if dev_major_cols is None:
                o_ref[...] = acc[...].astype(XFER_DTYPE)
            else:
                for s in range(tn // dev_major_cols):
                    o_ref[s] = acc[:, s * dev_major_cols:(s + 1) * dev_major_cols].astype(XFER_DTYPE)

    if dev_major_cols is None:
        out_spec = pl.BlockSpec((ta, tn), lambda i, j, r: (i, j))
        out_shape = SDS((ka, n), XFER_DTYPE)
    else:
        w = dev_major_cols
        out_spec = pl.BlockSpec((tn // w, ta, w), lambda i, j, r: (j, i, 0))
        out_shape = SDS((n // w, ka, w), XFER_DTYPE)
    return _pcall(
        body, name=name, grid=(ka // ta, n // tn, n_r),
        in_specs=[pl.BlockSpec((tr, ta), lambda i, j, r: (r, i)), pl.BlockSpec((tr, tn), lambda i, j, r: (r, j))],
        out_specs=out_spec, out_shape=out_shape,
        scratch_shapes=[pltpu.VMEM((ta, tn), F32)],
        compiler_params=_cparams("parallel", "parallel", "arbitrary"),
    )(a, b)


def _dw_in(h, dproj_pieces, layer):
    rows = h.shape[0]
    tr = _row_tile(rows, 1664)
    n_r = rows // tr

    def body(h_ref, du_ref, dqkv_ref, dgs_ref, dga_ref, o_ref, acc):
        j = pl.program_id(0)
        r = pl.program_id(1)

        @pl.when(r == 0)
        def _():
            acc[...] = jnp.zeros_like(acc)

        for piece_ref, (first, count) in zip((du_ref, dqkv_ref, dgs_ref, dga_ref), DPROJ_PIECES):
            @pl.when((j >= first) & (j < first + count))
            def _():
                acc[...] += _dot_tn(h_ref[...], piece_ref[...])

        @pl.when(r == n_r - 1)
        def _():
            o_ref[...] = acc[...].astype(XFER_DTYPE)

    def piece_spec(first, count):
        def index(j, r):
            mine = (j >= first) & (j < first + count)
            return jnp.where(mine, r, 0), jnp.clip(j - first, 0, count - 1)
        return pl.BlockSpec((tr, COL_SHARD), index)

    return _pcall(
        body, name=f"dw_in_l{layer}", grid=(N_DEV, n_r),
        in_specs=[pl.BlockSpec((tr, D), lambda j, r: (r, 0))] + [piece_spec(*p) for p in DPROJ_PIECES],
        out_specs=pl.BlockSpec((None, D, COL_SHARD), lambda j, r: (j, 0, 0)),
        out_shape=SDS((N_DEV, D, COL_SHARD), XFER_DTYPE),
        scratch_shapes=[pltpu.VMEM((D, COL_SHARD), F32)],
        compiler_params=_cparams("arbitrary", "arbitrary"),
    )(h, *dproj_pieces)


def _mlp_bwd(dout, ff, up, hres_mid, gain_pre3, gain_post3, w_up_g, w_down_g, layer):
    rows = dout.shape[0]
    tm = _row_tile(rows, 320)

    def body(do_ref, ff_ref, up_ref, x_ref, gp_ref, gq_ref, wu_hbm, wd_hbm,
             dff_ref, dup_ref, dx_ref, dgq_ref, dgp_ref, wut_scr, wdt_scr, wu_stage, wd_stage, wu_sem, wd_sem):
        i = pl.program_id(0)
        _load_resident_transposed(wu_hbm, wut_scr, wu_stage, wu_sem, i == 0)
        _load_resident_transposed(wd_hbm, wdt_scr, wd_stage, wd_sem, i == 0)

        @pl.when(i == 0)
        def _():
            dgq_ref[...] = jnp.zeros_like(dgq_ref)
            dgp_ref[...] = jnp.zeros_like(dgp_ref)

        dff, dg = _rms_bwd(ff_ref[...], gq_ref[...], do_ref[...])
        dgq_ref[...] += dg
        dffb = dff.astype(MXU_DTYPE)
        dff_ref[...] = dffb
        for kf in range(N_DEV):
            cols = slice(kf * COL_SHARD, (kf + 1) * COL_SHARD)
            dact = _dot(dffb, wdt_scr[kf])
            dup_ref[:, cols] = (dact * (2.0 * jnp.maximum(up_ref[:, cols].astype(F32), 0.0))).astype(MXU_DTYPE)
        dh = _dot(dup_ref[...], wut_scr[...].reshape(D_FF, D))
        dx, dg = _rms_bwd(x_ref[...], gp_ref[...], dh)
        dgp_ref[...] += dg
        dx_ref[...] = do_ref[...] + dx

    row_d = pl.BlockSpec((tm, D), lambda i: (i, 0))
    row_ff = pl.BlockSpec((tm, D_FF), lambda i: (i, 0))
    gain = pl.BlockSpec((None, 1, D), lambda i: (layer, 0, 0))
    dgain = pl.BlockSpec((1, D), lambda i: (0, 0))
    return _pcall(
        body, name=f"mlp_bwd_l{layer}", grid=(rows // tm,),
        in_specs=[row_d, row_d, row_ff, row_d, gain, gain,
                  pl.BlockSpec(memory_space=pl.ANY), pl.BlockSpec(memory_space=pl.ANY)],
        out_specs=[row_d, row_ff, row_d, dgain, dgain],
        out_shape=[SDS((rows, D), MXU_DTYPE), SDS((rows, D_FF), MXU_DTYPE), SDS((rows, D), F32),
                   SDS((1, D), F32), SDS((1, D), F32)],
        scratch_shapes=[pltpu.VMEM((N_DEV, COL_SHARD, D), MXU_DTYPE), pltpu.VMEM((N_DEV, D, COL_SHARD), MXU_DTYPE),
                        pltpu.VMEM((2, D, COL_SHARD), MXU_DTYPE), pltpu.VMEM((2, COL_SHARD, D), MXU_DTYPE),
                        pltpu.SemaphoreType.DMA((2,)), pltpu.SemaphoreType.DMA((2,))],
        compiler_params=_cparams("arbitrary"),
    )(dout, ff, up, hres_mid, gain_pre3, gain_post3, w_up_g, w_down_g)


def _merge_bwd(dhm, mix, y_ssm, y_attn, gates, w_o_ssm, w_o_attn, w_o_ssm_t, w_o_attn_t, w_out_t, gain3, layer):
    rows = dhm.shape[0]
    tm = _row_tile(rows, 320)

    def body(dh_ref, mix_ref, ys_ref, ya_ref, gs_ref, ga_ref, wos_ref, woa_ref, wost_ref, woat_ref, woutt_ref, g_ref,
             dmix_ref, da1_ref, da2_ref, dgs_ref, dga_ref, dys_ref, dya_ref, dg_ref):
        @pl.when(pl.program_id(0) == 0)
        def _():
            dg_ref[...] = jnp.zeros_like(dg_ref)

        dmix, dg = _rms_bwd(mix_ref[...], g_ref[...], dh_ref[...])
        dg_ref[...] += dg
        dmixb = dmix.astype(MXU_DTYPE)
        dmix_ref[...] = dmixb
        dmerged = _dot(dmixb, woutt_ref[...])
        sg_s = _sigmoid(gs_ref[...])
        sg_a = _sigmoid(ga_ref[...])
        da1 = (dmerged * sg_s).astype(MXU_DTYPE)
        da2 = (dmerged * sg_a).astype(MXU_DTYPE)
        da1_ref[...] = da1
        da2_ref[...] = da2
        a1 = _dot(ys_ref[...], wos_ref[...])
        a2 = _dot(ya_ref[...], woa_ref[...])
        dgs_ref[...] = (dmerged * a1 * (sg_s * (1.0 - sg_s))).astype(MXU_DTYPE)
        dga_ref[...] = (dmerged * a2 * (sg_a * (1.0 - sg_a))).astype(MXU_DTYPE)
        dys_ref[...] = _dot(da1, wost_ref[...])
        dya_ref[...] = _dot(da2, woat_ref[...])

    row_d = pl.BlockSpec((tm, D), lambda i: (i, 0))
    full = lambda shape: pl.BlockSpec(shape, lambda i: (0,) * len(shape))
    return _pcall(
        body, name=f"merge_bwd_l{layer}", grid=(rows // tm,),
        in_specs=[row_d, row_d, pl.BlockSpec((tm, D_SSM), lambda i: (i, 0)), row_d,
                  row_d, pl.BlockSpec((tm, D), lambda i: (i, 1)),
                  full((D_SSM, D)), full((D_ATTN, D)), full((D, D_SSM)), full((D, D_ATTN)), full((D, D)),
                  pl.BlockSpec((None, 1, D), lambda i: (layer, 0, 0))],
        out_specs=[row_d, row_d, row_d, row_d, row_d, pl.BlockSpec((tm, D_SSM), lambda i: (i, 0)), row_d,
                   pl.BlockSpec((1, D), lambda i: (0, 0))],
        out_shape=[SDS((rows, D), MXU_DTYPE)] * 5 + [SDS((rows, D_SSM), F32), SDS((rows, D_ATTN), F32),
                                                      SDS((1, D), F32)],
        compiler_params=_cparams("arbitrary"),
    )(dhm, mix, y_ssm, y_attn, gates, gates, w_o_ssm, w_o_attn, w_o_ssm_t, w_o_attn_t, w_out_t, gain3)


def _attn_bwd(q, k, v, d_out, sinks, layer):
    rows = q.shape[0]
    n_blk = rows // BLK
    last = n_blk - 1

    def body(sink_ref, q_ref, km_ref, kp_ref, kc_ref, vm_ref, vp_ref, vc_ref, do_ref,
             dq_ref, dk_ref, dv_ref, dkm_ref, dvm_ref, ds_ref, dk_carry, dv_carry):
        i = pl.program_id(0)

        @pl.when(i == 0)
        def _():
            dkm_ref[...] = jnp.zeros_like(dkm_ref)
            dvm_ref[...] = jnp.zeros_like(dvm_ref)
            ds_ref[...] = jnp.zeros_like(ds_ref)
            dk_carry[...] = jnp.zeros_like(dk_carry)
            dv_carry[...] = jnp.zeros_like(dv_carry)

        @pl.when(i <= last)
        def _():
            bias = _attn_mask(i)
            for kvh in range(N_KV_HEADS):
                lanes = _head_lanes(kvh)
                k3 = jnp.concatenate([km_ref[:, lanes], kp_ref[:, lanes], kc_ref[:, lanes]], axis=0)
                v3 = jnp.concatenate([vm_ref[:, lanes], vp_ref[:, lanes], vc_ref[:, lanes]], axis=0)
                q4 = _group_rows(q_ref, kvh)
                do4 = _group_rows(do_ref, kvh).astype(MXU_DTYPE)
                p = _attn_probs(q4, k3, _group_bias(bias, sink_ref, layer, kvh))
                dp = _dot_nt(do4, v3)
                dsf = p * (dp - jnp.sum(dp * p, axis=-1, keepdims=True))
                dsc = dsf.astype(MXU_DTYPE)
                dv3 = _dot_tn(p.astype(MXU_DTYPE), do4)
                dk3 = _dot_tn(dsc, q4)
                dq4 = _dot(dsc, k3)
                for g in range(Q_PER_KV):
                    h = kvh * Q_PER_KV + g
                    dq_ref[:, _head_lanes(h)] = dq4[g * BLK:(g + 1) * BLK]
                    ds_ref[h:h + 1, :] += jnp.sum(dsf[g * BLK:(g + 1) * BLK, 0:BLK], axis=0, keepdims=True)
                dkm_ref[:, lanes] += dk3[0:BLK]
                dvm_ref[:, lanes] += dv3[0:BLK]
                dk_ref[:, lanes] = dk_carry[:, lanes] + dk3[BLK:2 * BLK]
                dv_ref[:, lanes] = dv_carry[:, lanes] + dv3[BLK:2 * BLK]
                dk_carry[:, lanes] = dk3[2 * BLK:3 * BLK]
                dv_carry[:, lanes] = dv3[2 * BLK:3 * BLK]

        @pl.when(i == last + 1)
        def _():
            dk_ref[...] = dk_carry[...]
            dv_ref[...] = dv_carry[...]

    cur = lambda i: (jnp.minimum(i, last), 0)
    prev = lambda i: (jnp.clip(i - 1, 0, last), 0)
    kv_meta = pl.BlockSpec((BLK, D_KV), lambda i: (0, 0))
    kv_prev = pl.BlockSpec((BLK, D_KV), prev)
    kv_cur = pl.BlockSpec((BLK, D_KV), cur)
    return _pcall(
        body, name=f"attn_bwd_l{layer}", grid=(n_blk + 1,),
        in_specs=[pl.BlockSpec(memory_space=pltpu.SMEM),
                  pl.BlockSpec((BLK, D_ATTN), cur),
                  kv_meta, kv_prev, kv_cur, kv_meta, kv_prev, kv_cur,
                  pl.BlockSpec((BLK, D_ATTN), cur)],
        out_specs=[pl.BlockSpec((BLK, D_ATTN), cur), kv_prev, kv_prev, kv_meta, kv_meta,
                   pl.BlockSpec((N_Q_HEADS, 128), lambda i: (0, 0))],
        out_shape=[SDS((rows, D_ATTN), F32), SDS((rows, D_KV), F32), SDS((rows, D_KV), F32),
                   SDS((BLK, D_KV), F32), SDS((BLK, D_KV), F32), SDS((N_Q_HEADS, 128), F32)],
        scratch_shapes=[pltpu.VMEM((BLK, D_KV), F32), pltpu.VMEM((BLK, D_KV), F32)],
        compiler_params=_cparams("arbitrary"),
    )(sinks, q, k, k, k, v, v, v, d_out)


def _rope_bwd(dq, dk, dv, dk_meta, dv_meta, cos, sin_a, sin_b, layer):
    rows = dq.shape[0]
    tm = _row_tile(rows)

    def body(dq_ref, dk_ref, dv_ref, dkm_ref, dvm_ref, c_ref, a_ref, b_ref, o_ref):
        c, a, b = c_ref[...], -a_ref[...], -b_ref[...]
        for t in range(8):
            x = dq_ref[:, t * 128:(t + 1) * 128]
            o_ref[:, t * 128:(t + 1) * 128] = (_rope_lanes(x, c, a, b) * ATTN_SCALE).astype(MXU_DTYPE)
        for t in range(2):
            x = dk_ref[:, t * 128:(t + 1) * 128]
            o_ref[:, D_ATTN + t * 128:D_ATTN + (t + 1) * 128] = _rope_lanes(x, c, a, b).astype(MXU_DTYPE)
        o_ref[:, D_ATTN + D_KV:] = dv_ref[...].astype(MXU_DTYPE)

        @pl.when(pl.program_id(0) == 0)
        def _():
            cb, ab, bb = c[0:BLK], a[0:BLK], b[0:BLK]
            is_meta = lax.broadcasted_iota(jnp.int32, (BLK, 128), 0) >= PAD_ROWS
            for t in range(2):
                x = dk_ref[0:BLK, t * 128:(t + 1) * 128] + jnp.where(is_meta, dkm_ref[:, t * 128:(t + 1) * 128], 0.0)
                o_ref[0:BLK, D_ATTN + t * 128:D_ATTN + (t + 1) * 128] = _rope_lanes(x, cb, ab, bb).astype(MXU_DTYPE)
                xv = dv_ref[0:BLK, t * 128:(t + 1) * 128] + jnp.where(is_meta, dvm_ref[:, t * 128:(t + 1) * 128], 0.0)
                o_ref[0:BLK, D_ATTN + D_KV + t * 128:D_ATTN + D_KV + (t + 1) * 128] = xv.astype(MXU_DTYPE)

    tab = pl.BlockSpec((tm, 128), lambda i: (i, 0))
    kv = pl.BlockSpec((tm, D_KV), lambda i: (i, 0))
    meta = pl.BlockSpec((BLK, D_KV), lambda i: (0, 0))
    return _pcall(
        body, name=f"rope_bwd_l{layer}", grid=(rows // tm,),
        in_specs=[pl.BlockSpec((tm, D_ATTN), lambda i: (i, 0)), kv, kv, meta, meta, tab, tab, tab],
        out_specs=pl.BlockSpec((tm, D_ATTN + 2 * D_KV), lambda i: (i, 0)),
        out_shape=SDS((rows, D_ATTN + 2 * D_KV), MXU_DTYPE),
        compiler_params=_cparams("parallel"),
    )(dq, dk, dv, dk_meta, dv_meta, cos, sin_a, sin_b)


def _s5_bwd(d_gated, y, u, carry_in, ssm, w_glu, b_glu3, layer):
    rows = y.shape[0]
    n_chunks = rows // BLK
    b_mat, c_mat, t_re, t_im, d_skip = (ssm[k] for k in ("b_mat", "c_mat", "t_re", "t_im", "d_skip"))

    def body(dz_ref, y_ref, u_ref, cin_ref, bm_ref, cm_ref, tre_ref, tim_ref, d_ref, wg_ref, bg_ref,
             du_ref, dwg_ref, dbg_ref, dd_ref, dbm_ref, dcm_ref, dab_ref,
             lam_carry, bu_scr, s_scr, sp_scr, g_scr, lam_scr):
        step = pl.program_id(0)
        chunk = n_chunks - 1 - step

        @pl.when(step == 0)
        def _():
            for r in (dwg_ref, dbg_ref, dd_ref, dbm_ref, dcm_ref, dab_ref, lam_carry):
                r[...] = jnp.zeros_like(r)

        y = y_ref[...]
        u = u_ref[...]
        d_o = dz_ref[...]
        z, t = _gelu_parts(y)
        zb = z.astype(MXU_DTYPE)
        sg = _sigmoid(_dot(zb, wg_ref[...]) + bg_ref[...])
        dgl = d_o * z * (sg * (1.0 - sg))
        dglb = dgl.astype(MXU_DTYPE)
        dz = d_o * sg + _dot_nt(dglb, wg_ref[...])
        dwg_ref[...] += _dot_tn(zb, dglb)
        dbg_ref[...] += jnp.sum(dgl, axis=0, keepdims=True)
        dy = dz * _gelu_grad(y, t)
        dd_ref[...] += jnp.sum(dy * u, axis=0, keepdims=True)
        grow = lax.broadcasted_iota(jnp.int32, (BLK, 128), 0) + chunk * BLK
        for sb in range(N_SB):
            cols = slice(sb * 128, (sb + 1) * 128)
            u_sb = u[:, cols].astype(MXU_DTYPE)
            dy_sb = dy[:, cols]
            dyb = dy_sb.astype(MXU_DTYPE)
            bu_scr[sb] = _dot(u_sb, bm_ref[sb])
            _scan_tiles(bu_scr.at[sb], s_scr.at[sb], tre_ref, tim_ref, sb,
                        cin_ref[2 * sb:2 * sb + 1, :], cin_ref[2 * sb + 1:2 * sb + 2, :], False, prev_ref=sp_scr.at[sb])
            dcm_ref[sb] += _dot_tn(s_scr[sb].astype(MXU_DTYPE), dyb)
            g_scr[sb] = _dot_nt(dyb, cm_ref[sb])
            n_r, n_i = _scan_tiles(g_scr.at[sb], lam_scr.at[sb], tre_ref, tim_ref, sb,
                                   lam_carry[2 * sb:2 * sb + 1, :], lam_carry[2 * sb + 1:2 * sb + 2, :], True)
            lam_carry[2 * sb:2 * sb + 1, :] = n_r
            lam_carry[2 * sb + 1:2 * sb + 2, :] = n_i
            lr, li = lam_scr[sb, :, :SB_STATES], lam_scr[sb, :, SB_STATES:]
            spr, spi = sp_scr[sb, :, :SB_STATES], sp_scr[sb, :, SB_STATES:]
            dab_ref[2 * sb:2 * sb + 1, :] += jnp.sum(spr * lr + spi * li, axis=0, keepdims=True)
            dab_ref[2 * sb + 1:2 * sb + 2, :] += jnp.sum(spr * li - spi * lr, axis=0, keepdims=True)
            lam = lam_scr[sb].astype(MXU_DTYPE)
            dbm_ref[sb] += _dot_tn(u_sb, lam)
            du = _dot_nt(lam, bm_ref[sb]) + d_ref[:, cols] * dy_sb
            du_ref[:, cols] = jnp.where(grow >= PAD_ROWS, du, 0.0).astype(MXU_DTYPE)

    rev = lambda j: (n_chunks - 1 - j, 0)
    full = lambda shape: pl.BlockSpec(shape, lambda j: (0,) * len(shape))
    of_layer = lambda shape: pl.BlockSpec((None,) + shape, lambda j: (layer,) + (0,) * len(shape))
    tables = [of_layer((N_SB, 8, SCAN_TILE, SB_STATES))] * 2
    chunk_scratch = pltpu.VMEM((N_SB, BLK, 2 * SB_STATES), F32)
    return _pcall(
        body, name=f"s5_bwd_l{layer}", grid=(n_chunks,),
        in_specs=[pl.BlockSpec((BLK, D_SSM), rev), pl.BlockSpec((BLK, D_SSM), rev), pl.BlockSpec((BLK, D_SSM), rev),
                  pl.BlockSpec((None, 8, SB_STATES), lambda j: (n_chunks - 1 - j, 0, 0)),
                  of_layer((N_SB, 128, 2 * SB_STATES)), of_layer((N_SB, 2 * SB_STATES, 128))] + tables + [
                  of_layer((1, D_SSM)), full((D_SSM, D_SSM)),
                  pl.BlockSpec((None, 1, D_SSM), lambda j: (layer, 0, 0))],
        out_specs=[pl.BlockSpec((BLK, D_SSM), rev), full((D_SSM, D_SSM)), full((1, D_SSM)), full((1, D_SSM)),
                   full((N_SB, 128, 2 * SB_STATES)), full((N_SB, 2 * SB_STATES, 128)), full((8, SB_STATES))],
        out_shape=[SDS((rows, D_SSM), MXU_DTYPE), SDS((D_SSM, D_SSM), F32), SDS((1, D_SSM), F32), SDS((1, D_SSM), F32),
                   SDS((N_SB, 128, 2 * SB_STATES), F32), SDS((N_SB, 2 * SB_STATES, 128), F32), SDS((8, SB_STATES), F32)],
        scratch_shapes=[pltpu.VMEM((8, SB_STATES), F32)] + [chunk_scratch] * 5,
        compiler_params=_cparams("arbitrary"),
    )(d_gated, y, u, carry_in, b_mat, c_mat, t_re, t_im, d_skip, w_glu, b_glu3)


DPROJ_PIECES = ((0, 1), (1, 3), (4, 2), (6, 2))


def _in_bwd(dproj_pieces, dhm, hres, gain3, w_in_g, layer):
    rows = hres.shape[0]
    tm = _row_tile(rows)

    def body(du_ref, dqkv_ref, dgs_ref, dga_ref, dh_ref, x_ref, g_ref, w_hbm, dx_ref, dg_ref,
             wt_scr, w_stage, w_sem):
        i = pl.program_id(0)
        _load_resident_transposed(w_hbm, wt_scr, w_stage, w_sem, i == 0)

        @pl.when(i == 0)
        def _():
            dg_ref[...] = jnp.zeros_like(dg_ref)

        dh = None
        for piece_ref, (first, count) in zip((du_ref, dqkv_ref, dgs_ref, dga_ref), DPROJ_PIECES):
            wt = wt_scr[first:first + count].reshape(count * COL_SHARD, D)
            part = _dot(piece_ref[...], wt)
            dh = part if dh is None else dh + part
        dx, dg = _rms_bwd(x_ref[...], g_ref[...], dh)
        dg_ref[...] += dg
        dx_ref[...] = dh_ref[...] + dx

    row_d = pl.BlockSpec((tm, D), lambda i: (i, 0))
    return _pcall(
        body, name=f"in_bwd_l{layer}", grid=(rows // tm,),
        in_specs=[pl.BlockSpec((tm, count * COL_SHARD), lambda i: (i, 0)) for _, count in DPROJ_PIECES] + [
                  row_d, row_d,
                  pl.BlockSpec((None, 1, D), lambda i: (layer, 0, 0)),
                  pl.BlockSpec(memory_space=pl.ANY)],
        out_specs=[row_d, pl.BlockSpec((1, D), lambda i: (0, 0))],
        out_shape=[SDS((rows, D), F32), SDS((1, D), F32)],
        scratch_shapes=[pltpu.VMEM((N_DEV, COL_SHARD, D), MXU_DTYPE),
                        pltpu.VMEM((2, D, COL_SHARD), MXU_DTYPE), pltpu.SemaphoreType.DMA((2,))],
        compiler_params=_cparams("arbitrary"),
    )(*dproj_pieces, dhm, hres, gain3, w_in_g)


_ADAM_C1 = 1.0 / (1.0 - ADAM_B1 ** ADAM_STEP)
_ADAM_C2 = 1.0 / (1.0 - ADAM_B2 ** ADAM_STEP)


def _adam_math(w, g, m, v):
    m = ADAM_B1 * m + (1.0 - ADAM_B1) * g
    v = ADAM_B2 * v + (1.0 - ADAM_B2) * (g * g)
    delta = -ADAM_LR * ((m * _ADAM_C1) / (jnp.sqrt(v * _ADAM_C2) + ADAM_EPS) + ADAM_WD * w)
    return delta, m, v


def _adamw_layers(parts0, parts1, w, m, v, name):
    _, rows, cols = w.shape
    tr = min(rows, (1 << 16) // cols)
    nt = rows // tr

    def body(p0_ref, p1_ref, w_ref, m_ref, v_ref, g_ref, d_ref, nm_ref, nv_ref):
        layer = pl.program_id(0)

        def run(p_ref):
            g = p_ref[0].astype(F32)
            for s in range(1, N_DEV):
                g = g + p_ref[s].astype(F32)
            delta, nm, nv = _adam_math(w_ref[...], g, m_ref[...], v_ref[...])
            g_ref[...] = g
            d_ref[...] = delta
            nm_ref[...] = nm
            nv_ref[...] = nv

        @pl.when(layer == 0)
        def _():
            run(p0_ref)

        @pl.when(layer == 1)
        def _():
            run(p1_ref)

    wspec = pl.BlockSpec((None, tr, cols), lambda l, i: (l, i, 0))
    return _pcall(
        body, name=name, grid=(2, nt),
        in_specs=[pl.BlockSpec((N_DEV, tr, cols), lambda l, i: (0, jnp.where(l == 0, i, nt - 1), 0)),
                  pl.BlockSpec((N_DEV, tr, cols), lambda l, i: (0, jnp.where(l == 1, i, 0), 0)),
                  wspec, wspec, wspec],
        out_specs=[wspec] * 4, out_shape=[SDS(w.shape, F32)] * 4,
        compiler_params=_cparams("arbitrary", "arbitrary"),
    )(parts0, parts1, w, m, v)


def _sum_slots(parts, name):
    def body(p_ref, o_ref):
        acc = p_ref[0]
        for s in range(1, N_DEV):
            acc = acc + p_ref[s]
        o_ref[...] = acc

    vmem = pl.BlockSpec(memory_space=pltpu.VMEM)
    return _pcall(body, name=name, out_shape=SDS(parts.shape[1:], F32), in_specs=[vmem], out_specs=vmem,
                  compiler_params=_cparams())(parts)


def _adamw_packed(g, w, m, v, name):
    def body(g_ref, w_ref, m_ref, v_ref, d_ref, nm_ref, nv_ref):
        delta, nm, nv = _adam_math(w_ref[...], g_ref[...], m_ref[...], v_ref[...])
        d_ref[...] = delta
        nm_ref[...] = nm
        nv_ref[...] = nv

    vmem = pl.BlockSpec(memory_space=pltpu.VMEM)
    return _pcall(body, name=name, out_shape=[SDS(g.shape, F32)] * 3, in_specs=[vmem] * 4, out_specs=[vmem] * 3,
                  compiler_params=_cparams())(g, w, m, v)


def _ssm_discretize(a_re, a_im, log_dt, b_re, b_im):
    dt = jnp.exp(log_dt)[:, None]
    mag = jnp.exp(a_re * dt)
    ang = a_im * dt
    ab_re, ab_im = mag * jnp.cos(ang), mag * jnp.sin(ang)
    xr, xi = ab_re - 1.0, ab_im
    den = a_re * a_re + a_im * a_im
    q_re = (xr * a_re + xi * a_im) / den
    q_im = (xi * a_re - xr * a_im) / den
    bb_re = q_re[..., None] * b_re - q_im[..., None] * b_im
    bb_im = q_re[..., None] * b_im + q_im[..., None] * b_re
    return ab_re, ab_im, bb_re, bb_im


def _block_diag_b(bb):
    m = jnp.einsum("sgnc,gh->sgchn", bb.reshape(N_SB, 8, N_STATE, GROUP_CH), jnp.eye(8, dtype=F32))
    return m.reshape(N_SB, 128, SB_STATES)


def _block_diag_b_t(dm):
    return jnp.einsum("sgchn,gh->sgnc", dm.reshape(N_SB, 8, GROUP_CH, 8, N_STATE),
                      jnp.eye(8, dtype=F32)).reshape(N_GROUPS, N_STATE, GROUP_CH)


def _block_diag_c(cc):
    m = jnp.einsum("sgcn,gh->sgnhc", cc.reshape(N_SB, 8, GROUP_CH, N_STATE), jnp.eye(8, dtype=F32))
    return m.reshape(N_SB, SB_STATES, 128)


def _block_diag_c_t(dm):
    return jnp.einsum("sgnhc,gh->sgcn", dm.reshape(N_SB, 8, N_STATE, 8, GROUP_CH),
                      jnp.eye(8, dtype=F32)).reshape(N_GROUPS, GROUP_CH, N_STATE)


def _ssm_tables(ab_re, ab_im, bb_re, bb_im, c_re, c_im, d_skip):
    pr, pi = ab_re.reshape(1, -1), ab_im.reshape(1, -1)
    cr, ci = pr, pi
    squares = []
    for _ in range(3):
        squares.append((cr, ci))
        pr, pi = (jnp.concatenate([pr, pr * cr - pi * ci], axis=0),
                  jnp.concatenate([pi, pr * ci + pi * cr], axis=0))
        cr, ci = cr * cr - ci * ci, 2.0 * cr * ci
    r = jnp.arange(SCAN_TILE)[:, None]
    fwd = [(jnp.where(r >= (1 << k), squares[k][0], 0.0), jnp.where(r >= (1 << k), squares[k][1], 0.0))
           for k in range(3)] + [(pr, pi)]
    rev = [(jnp.where(r < SCAN_TILE - (1 << k), squares[k][0], 0.0),
            jnp.where(r < SCAN_TILE - (1 << k), -squares[k][1], 0.0)) for k in range(3)] + [(pr[::-1], -pi[::-1])]
    table = lambda part: jnp.stack([e[part] for e in fwd + rev]).reshape(
        8, SCAN_TILE, N_SB, SB_STATES).transpose(2, 0, 1, 3)
    return dict(
        b_mat=jnp.concatenate([_block_diag_b(bb_re), _block_diag_b(bb_im)], axis=-1).astype(MXU_DTYPE),
        c_mat=jnp.concatenate([_block_diag_c(c_re), -_block_diag_c(c_im)], axis=1).astype(MXU_DTYPE),
        t_re=table(0), t_im=table(1),
        d_skip=d_skip.reshape(1, D_SSM))


def _rope_tables(rows):
    pos = (jnp.arange(rows, dtype=jnp.int32) - PAD_ROWS).astype(F32)
    inv_freq = 1.0 / (ROPE_THETA ** (jnp.arange(0, HEAD_DIM, 2, dtype=F32) / HEAD_DIM))
    ang = pos[:, None] * inv_freq[None, :]
    ang = jnp.concatenate([ang, ang, ang, ang], axis=-1)
    first_half = (jnp.arange(128) % HEAD_DIM) < HEAD_DIM // 2
    sin = jnp.sin(ang)
    return jnp.cos(ang), jnp.where(first_half, -sin, 0.0), jnp.where(first_half, 0.0, sin)


def _pack(arrays):
    flat = jnp.concatenate([a.reshape(-1).astype(F32) for a in arrays])
    pad = (-flat.shape[0]) % 1024
    return jnp.pad(flat, (0, pad)).reshape(-1, 128)


def _unpack(packed, like):
    flat = packed.reshape(-1)
    out, off = [], 0
    for a in like:
        n = math.prod(a.shape)
        out.append(flat[off:off + n].reshape(a.shape))
        off += n
    return out


BIG = ("w_in", "w_glu", "w_o_ssm", "w_o_attn", "w_out", "w_up", "w_down")
WEIGHTS = ("meta_tokens", "norm_mix_pre", "norm_mix_post", "norm_mlp_pre", "norm_mlp_post", "w_in",
           "ssm_a_re", "ssm_a_im", "ssm_log_dt", "ssm_b_re", "ssm_b_im", "ssm_c_re", "ssm_c_im", "ssm_d",
           "w_glu", "b_glu", "attn_sinks", "w_o_ssm", "w_o_attn", "w_out", "w_up", "w_down")
SMALL = tuple(n for n in WEIGHTS if n not in BIG)


def kernel(x, meta_tokens, norm_mix_pre, norm_mix_post, norm_mlp_pre, norm_mlp_post, w_in, ssm_a_re, ssm_a_im, ssm_log_dt, ssm_b_re, ssm_b_im, ssm_c_re, ssm_c_im, ssm_d, w_glu, b_glu, attn_sinks, w_o_ssm, w_o_attn, w_out, w_up, w_down, loss_target, m_meta_tokens, m_norm_mix_pre, m_norm_mix_post, m_norm_mlp_pre, m_norm_mlp_post, m_w_in, m_ssm_a_re, m_ssm_a_im, m_ssm_log_dt, m_ssm_b_re, m_ssm_b_im, m_ssm_c_re, m_ssm_c_im, m_ssm_d, m_w_glu, m_b_glu, m_attn_sinks, m_w_o_ssm, m_w_o_attn, m_w_out, m_w_up, m_w_down, v_meta_tokens, v_norm_mix_pre, v_norm_mix_post, v_norm_mlp_pre, v_norm_mlp_post, v_w_in, v_ssm_a_re, v_ssm_a_im, v_ssm_log_dt, v_ssm_b_re, v_ssm_b_im, v_ssm_c_re, v_ssm_c_im, v_ssm_d, v_w_glu, v_b_glu, v_attn_sinks, v_w_o_ssm, v_w_o_attn, v_w_out, v_w_up, v_w_down):
    args = locals()
    w = {n: args[n] for n in WEIGHTS}
    m = {n: args["m_" + n] for n in WEIGHTS}
    v = {n: args["v_" + n] for n in WEIGHTS}
    n_layers = w_in.shape[0]
    seq = x.shape[1]
    rows = seq + BLK
    my_slot = _slot(_mesh_pos())

    assert n_layers == 2
    xfer = {n: [w[n][l].astype(XFER_DTYPE) for l in range(n_layers)] for n in BIG}
    mixer_small = ("w_glu", "w_o_ssm", "w_o_attn", "w_out")
    meta_g, w_in_g0 = _exchange_by_sequencer([meta_tokens, xfer["w_in"][0]], True, 0, "gather_in0")
    mix0_g = _exchange_by_sequencer([xfer[n][0] for n in mixer_small], True, 1, "gather_mix0")
    meta_full = meta_g.transpose(1, 0, 2).reshape(N_META, D)

    def mixer_weights(w_glu_g, w_o_ssm_g, w_o_attn_g, w_out_g):
        return dict(w_glu=w_glu_g.reshape(D_SSM, D_SSM), w_o_ssm=w_o_ssm_g.transpose(1, 0, 2).reshape(D_SSM, D),
                    w_o_attn=w_o_attn_g.reshape(D_ATTN, D), w_out=w_out_g.reshape(D, D),
                    w_o_ssm_t=w_o_ssm_g.transpose(0, 2, 1).reshape(D, D_SSM),
                    w_o_attn_t=w_o_attn_g.reshape(D_ATTN, D).T, w_out_t=w_out_g.reshape(D, D).T)

    gathered = [dict(w_in=w_in_g0, **mixer_weights(*mix0_g)), {}]

    gains = {n: w[n].reshape(n_layers, 1, D) for n in ("norm_mix_pre", "norm_mix_post", "norm_mlp_pre", "norm_mlp_post")}
    b_glu3 = b_glu.reshape(n_layers, 1, D_SSM)
    cos, sin_a, sin_b = _rope_tables(rows)

    disc, disc_vjp = jax.vjp(jax.vmap(_ssm_discretize), ssm_a_re, ssm_a_im, ssm_log_dt, ssm_b_re, ssm_b_im)
    ssm = jax.vmap(_ssm_tables)(*disc, ssm_c_re, ssm_c_im, ssm_d)

    hres = jnp.concatenate([jnp.zeros((PAD_ROWS, D), F32), meta_full, x[0]], axis=0)

    saved = []
    for l in range(n_layers):
        wl = gathered[l]
        u, gates, q, k, vv, h = _in_proj(hres, gains["norm_mix_pre"], wl["w_in"], cos, sin_a, sin_b, l)
        if l == 0:
            wl["w_up"], wl["w_down"] = _exchange_by_sequencer([xfer["w_up"][0], xfer["w_down"][0]], True, 2,
                                                              "gather_mlp0", after=[h])
        y, y_ssm, carry_in = _s5_fwd(u, ssm, wl["w_glu"], b_glu3, l)
        if l == 0:
            l1_g = _exchange_by_sequencer([xfer[n][1] for n in ("w_in",) + mixer_small + ("w_up", "w_down")], True, 3,
                                          "gather_l1", after=[y, wl["w_up"]])
            gathered[1] = dict(w_in=l1_g[0], w_up=l1_g[5], w_down=l1_g[6], **mixer_weights(*l1_g[1:5]))
            last_exchange = l1_g[:1]
        y_attn = _attn_fwd(q, k, vv, attn_sinks, l)
        merged, mix, hres_mid = _merge_fwd(y_ssm, y_attn, gates, hres, wl["w_o_ssm"], wl["w_o_attn"], wl["w_out"],
                                           gains["norm_mix_post"], l)
        if l + 1 < n_layers:
            up, h2, ff, hres_out = _mlp_fwd(hres_mid, gains["norm_mlp_pre"], gains["norm_mlp_post"], wl["w_up"],
                                            wl["w_down"], l)
        else:
            target = jnp.concatenate([jnp.zeros((BLK, D), F32), loss_target[0]], axis=0)
            up, h2, ff, dhres, loss_vec = _mlp_fwd(hres_mid, gains["norm_mlp_pre"], gains["norm_mlp_post"], wl["w_up"],
                                                   wl["w_down"], l, target=target)
        saved.append(dict(hres=hres, u=u, gates=gates, h=h, q=q, k=k, v=vv, y=y, y_ssm=y_ssm,
                          carry_in=carry_in, y_attn=y_attn, merged=merged, mix=mix, hres_mid=hres_mid,
                          up=up, h2=h2, ff=ff))
        hres = hres_out

    small_grads = {}
    recv_up, recv_down, recv_mix = [None] * n_layers, [None] * n_layers, [None] * n_layers
    for l in reversed(range(n_layers)):
        s = saved[l]
        wl = gathered[l]
        dff, dup, dhm, dg_mlp_post, dg_mlp_pre = _mlp_bwd(dhres, s["ff"], s["up"], s["hres_mid"], gains["norm_mlp_pre"],
                                                          gains["norm_mlp_post"], wl["w_up"], wl["w_down"], l)
        dw_up = _matmul_tn(s["h2"], dup, f"dw_up_l{l}", dev_major_cols=COL_SHARD)
        recv_up[l] = _exchange_by_sequencer([dw_up], False, 4 + 3 * l, f"scatter_up{l}", after=last_exchange)
        dw_down = _matmul_tn(s["up"], dff, f"dw_down_l{l}", a_fn=_relu_squared).reshape(N_DEV, COL_SHARD, D)
        recv_down[l] = _exchange_by_sequencer([dw_down], False, 5 + 3 * l, f"scatter_down{l}", after=recv_up[l])
        last_exchange = recv_down[l]
        dmix, da1, da2, dgs, dga, dy_ssm, dy_attn, dg_mix_post = _merge_bwd(
            dhm, s["mix"], s["y_ssm"], s["y_attn"], s["gates"], wl["w_o_ssm"], wl["w_o_attn"], wl["w_o_ssm_t"],
            wl["w_o_attn_t"], wl["w_out_t"], gains["norm_mix_post"], l)
        dw_out = _matmul_tn(s["merged"], dmix, f"dw_out_l{l}").reshape(N_DEV, D // N_DEV, D)
        dw_o_attn = _matmul_tn(s["y_attn"], da2, f"dw_o_attn_l{l}").reshape(N_DEV, D_ATTN // N_DEV, D)
        dw_o_ssm = _matmul_tn(s["y_ssm"], da1, f"dw_o_ssm_l{l}", dev_major_cols=D // N_DEV)
        if l == 0:
            recv_out0 = _exchange_by_sequencer([dw_o_ssm, dw_o_attn, dw_out], False, 11, "scatter_out0",
                                               after=last_exchange)
            last_exchange = recv_out0[:1]
        dq, dk, dv, dk_meta, dv_meta, dsink = _attn_bwd(s["q"], s["k"], s["v"], dy_attn, attn_sinks, l)
        dqkv = _rope_bwd(dq, dk, dv, dk_meta, dv_meta, cos, sin_a, sin_b, l)
        du, dw_glu, db_glu, dd_skip, db_mat, dc_mat, dab = _s5_bwd(dy_ssm, s["y"], s["u"], s["carry_in"], ssm,
                                                                    wl["w_glu"], b_glu3, l)
        dproj = (du, dqkv, dgs, dga)
        dw_in = _dw_in(s["h"], dproj, l)
        mix_parts = [dw_in, dw_glu.astype(XFER_DTYPE).reshape(N_DEV, D_SSM // N_DEV, D_SSM), dw_o_ssm, dw_o_attn, dw_out]
        if l > 0:
            recv_mix[l] = _exchange_by_sequencer(mix_parts, False, 6 + 3 * l, f"scatter_mix{l}", after=last_exchange)
            last_exchange = recv_mix[l][:1]
        else:
            recv_mix[0] = _exchange_by_sequencer(mix_parts[:2], False, 6, "scatter_in0", after=last_exchange) + recv_out0
            last_exchange = recv_mix[0][:1]
        dhres, dg_mix_pre = _in_bwd(dproj, dhm, s["hres"], gains["norm_mix_pre"], wl["w_in"], l)

        for name, val in (("norm_mix_pre", dg_mix_pre[0]), ("norm_mix_post", dg_mix_post[0]),
                          ("norm_mlp_pre", dg_mlp_pre[0]), ("norm_mlp_post", dg_mlp_post[0]),
                          ("dab", dab), ("db_mat", db_mat), ("dc_mat", dc_mat),
                          ("ssm_d", dd_skip.reshape(N_GROUPS, GROUP_CH)), ("b_glu", db_glu[0]),
                          ("attn_sinks", dsink[:, 0])):
            small_grads.setdefault(name, [None] * n_layers)[l] = val

    grad_x = dhres[BLK:][None]
    stacked = {n: jnp.stack(v) for n, v in small_grads.items()}
    dab = stacked["dab"].reshape(n_layers, N_SB, 2, SB_STATES)
    db_mat, dc_mat = stacked["db_mat"], stacked["dc_mat"]
    b_t, c_t = jax.vmap(_block_diag_b_t), jax.vmap(_block_diag_c_t)
    (stacked["ssm_a_re"], stacked["ssm_a_im"], stacked["ssm_log_dt"], stacked["ssm_b_re"],
     stacked["ssm_b_im"]) = disc_vjp((dab[:, :, 0].reshape(n_layers, N_GROUPS, N_STATE),
                                      dab[:, :, 1].reshape(n_layers, N_GROUPS, N_STATE),
                                      b_t(db_mat[..., :SB_STATES]), b_t(db_mat[..., SB_STATES:])))
    stacked["ssm_c_re"] = c_t(dc_mat[:, :, :SB_STATES])
    stacked["ssm_c_im"] = -c_t(dc_mat[:, :, SB_STATES:])
    small_names = [n for n in SMALL if n != "meta_tokens"]
    partial_small = [dhres[PAD_ROWS:BLK]] + [stacked[n] for n in small_names] + [loss_vec[0, :1]]
    small_parts, = _exchange_by_sequencer([_pack(partial_small)], True, 10, "gather_small", after=last_exchange)

    grads, delta, new_m, new_v = {}, {}, {}, {}

    def adamw_big(names, recv0, recv1):
        for n, p0, p1 in zip(names, recv0, recv1):
            grads[n], delta[n], new_m[n], new_v[n] = _adamw_layers(p0, p1, w[n], m[n], v[n], f"adamw_{n}")

    adamw_big(("w_up", "w_down"), recv_up[0] + recv_down[0], recv_up[1] + recv_down[1])
    summed = _unpack(_sum_slots(small_parts, "sum_small_grads"), partial_small)
    loss = summed[-1][0]
    grads.update(zip(small_names, summed[1:-1]))
    grads["meta_tokens"] = lax.dynamic_slice_in_dim(summed[0], my_slot * (D // N_DEV), D // N_DEV, axis=1)
    like = [w[n] for n in SMALL]
    d_s, m_s, v_s = _adamw_packed(_pack([grads[n] for n in SMALL]), _pack(like), _pack([m[n] for n in SMALL]),
                                  _pack([v[n] for n in SMALL]), "adamw_small")
    adamw_big(("w_in",) + mixer_small, recv_mix[0], recv_mix[1])
    for n, dd, mm, vs in zip(SMALL, _unpack(d_s, like), _unpack(m_s, like), _unpack(v_s, like)):
        delta[n], new_m[n], new_v[n] = dd, mm, vs

    return (loss, grad_x, *[grads[n] for n in WEIGHTS], *[delta[n] for n in WEIGHTS],
            *[new_m[n] for n in WEIGHTS], *[new_v[n] for n in WEIGHTS])
```

```python
import functools
import math

import jax
import jax.numpy as jnp
from jax import lax
from jax.experimental import pallas as pl
from jax.experimental.pallas import tpu as pltpu
from jax.experimental.pallas import tpu_sc as plsc

F32 = jnp.float32
MXU_DTYPE = jnp.bfloat16
XFER_DTYPE = MXU_DTYPE
_pcall = pl.pallas_call
SDS = jax.ShapeDtypeStruct

D = 1024
D_SSM = 512
D_ATTN = 1024
D_KV = 256
D_FF = 4096
D_IN = 4096
HEAD_DIM = 64
N_Q_HEADS = 16
N_KV_HEADS = 4
Q_PER_KV = 4
N_META = 16
BLK = 128
PAD_ROWS = BLK - N_META
N_GROUPS = 32
N_STATE = 64
GROUP_CH = 16
N_SB = 4
SB_STATES = 512
ROPE_THETA = 10000.0
ATTN_SCALE = HEAD_DIM ** -0.5
NEG_INF = -1e30
RMS_EPS = 1e-6
N_DEV = 8
COL_SHARD = 512

ADAM_LR = 0.001
ADAM_B1 = 0.9
ADAM_B2 = 0.999
ADAM_EPS = 1e-08
ADAM_WD = 0.01
ADAM_STEP = 10

VMEM_LIMIT = 56 * 1024 * 1024
MESH_AXES = ("x", "y", "c")

_NT = (((1,), (1,)), ((), ()))
_TN = (((0,), (0,)), ((), ()))


def _cparams(*sem):
    return pltpu.CompilerParams(dimension_semantics=tuple(sem) if sem else None,
                                vmem_limit_bytes=VMEM_LIMIT)


def _row_tile(rows, cap=640):
    for t in (1664, 640, 512, 320, 256, 128):
        if t <= cap and rows % t == 0:
            return t
    raise ValueError(f"unsupported row count {rows}")


def _dot(a, b):
    return jnp.dot(a, b, preferred_element_type=F32)


def _dot_nt(a, b):
    return lax.dot_general(a, b, _NT, preferred_element_type=F32)


def _dot_tn(a, b):
    return lax.dot_general(a, b, _TN, preferred_element_type=F32)


def _sigmoid(x):
    return 1.0 / (1.0 + jnp.exp(-x))


_GELU_C = math.sqrt(2.0 / math.pi)


def _gelu_parts(y):
    t = jnp.tanh(_GELU_C * (y + 0.044715 * (y * y * y)))
    return 0.5 * y * (1.0 + t), t


def _gelu_grad(y, t):
    return 0.5 * (1.0 + t) + 0.5 * y * (1.0 - t * t) * (_GELU_C * (1.0 + 0.134145 * (y * y)))


def _rms_fwd(x, gain):
    r = lax.rsqrt(jnp.mean(x * x, axis=-1, keepdims=True) + RMS_EPS)
    return (x * r) * gain


def _rms_bwd(x, gain, dout):
    r = lax.rsqrt(jnp.mean(x * x, axis=-1, keepdims=True) + RMS_EPS)
    xh = x * r
    dxh = dout * gain
    dx = r * (dxh - xh * jnp.mean(dxh * xh, axis=-1, keepdims=True))
    return dx, jnp.sum(dout * xh, axis=0, keepdims=True)


def _mesh_pos():
    return lax.axis_index("x"), lax.axis_index("y"), lax.axis_index("c")


def _peer(pos, d):
    x, y, c = pos
    return (1 - x if d & 4 else x, 1 - y if d & 2 else y, 1 - c if d & 1 else c)


def _slot(pos):
    return 4 * pos[0] + 2 * pos[1] + pos[2]


def _exchange_copy(gather, src_ref, land_ref, sems, k, d, me, send_side):
    peer = _peer(me, d)
    sender = me if send_side else peer
    src = src_ref if gather else src_ref.at[_slot(peer) if send_side else _slot(me)]
    return pltpu.make_async_remote_copy(
        src_ref=src, dst_ref=land_ref.at[_slot(sender)],
        send_sem=sems[0].at[k * (N_DEV - 1) + d - 1], recv_sem=sems[1].at[k * (N_DEV - 1) + d - 1],
        device_id=peer, device_id_type=pl.DeviceIdType.MESH)


def _exchange_by_sequencer(srcs, gather, collective_id, name, after=()):
    n = len(srcs)
    flags = [gather] * n if isinstance(gather, bool) else list(gather)
    land_types = [SDS(((N_DEV,) + s.shape) if g else s.shape, s.dtype) for s, g in zip(srcs, flags)]

    def body(*refs):
        src_refs = refs[:n]
        land_refs = refs[n + len(after):2 * n + len(after)]
        sems = refs[2 * n + len(after):2 * n + len(after) + 2]
        local_sems = refs[2 * n + len(after) + 2]
        me = _mesh_pos()
        barrier = pltpu.get_barrier_semaphore()
        for d in range(1, N_DEV):
            pl.semaphore_signal(barrier, inc=1, device_id=_peer(me, d), device_id_type=pl.DeviceIdType.MESH)
        pl.semaphore_wait(barrier, N_DEV - 1)
        own = [pltpu.make_async_copy(src_refs[k] if flags[k] else src_refs[k].at[_slot(me)],
                                     land_refs[k].at[_slot(me)], local_sems.at[k]) for k in range(n)]
        for cp in own:
            cp.start()
        for k in range(n):
            for d in range(1, N_DEV):
                _exchange_copy(flags[k], src_refs[k], land_refs[k], sems, k, d, me, True).start()
        for cp in own:
            cp.wait()
        for k in range(n):
            for d in range(1, N_DEV):
                _exchange_copy(flags[k], src_refs[k], land_refs[k], sems, k, d, me, True).wait_send()
        for k in range(n):
            for d in range(1, N_DEV):
                _exchange_copy(flags[k], src_refs[k], land_refs[k], sems, k, d, me, False).wait_recv()

    sem_type = pltpu.SemaphoreType.DMA((n * (N_DEV - 1),))
    return pl.kernel(
        body, out_type=land_types, mesh=plsc.ScalarSubcoreMesh(axis_name="sequencer", num_cores=1), name=name,
        scratch_types=(sem_type, sem_type, pltpu.SemaphoreType.DMA((n,))),
        compiler_params=pltpu.CompilerParams(collective_id=collective_id),
    )(*srcs, *after)


def _load_resident(w_hbm, w_scr, sems, first_step):
    @pl.when(first_step)
    def _():
        copies = [pltpu.make_async_copy(w_hbm.at[s], w_scr.at[s], sems.at[s]) for s in range(N_DEV)]
        for cp in copies:
            cp.start()
        for cp in copies:
            cp.wait()


def _load_resident_transposed(w_hbm, w_scr, stage, sems, first_step):
    @pl.when(first_step)
    def _():
        copies = [pltpu.make_async_copy(w_hbm.at[s], stage.at[s % 2], sems.at[s % 2]) for s in range(N_DEV)]
        copies[0].start()
        for s in range(N_DEV):
            if s + 1 < N_DEV:
                copies[s + 1].start()
            copies[s].wait()
            w_scr[s] = stage[s % 2].T


def _rope_lanes(t, cos, sin_a, sin_b):
    return t * cos + pltpu.roll(t, 96, 1) * sin_a + pltpu.roll(t, 32, 1) * sin_b


def _in_proj(hres, gain3, w_in_g, cos, sin_a, sin_b, layer, after=()):
    rows = hres.shape[0]
    tm = _row_tile(rows, 320)

    def body(x_ref, g_ref, w_hbm, c_ref, a_ref, b_ref, *refs):
        u_ref, gate_ref, q_ref, k_ref, v_ref, h_ref, w_scr, w_sem = refs[len(after):]
        _load_resident(w_hbm, w_scr, w_sem, pl.program_id(0) == 0)
        hn = _rms_fwd(x_ref[...], g_ref[...]).astype(MXU_DTYPE)
        h_ref[...] = hn
        c, a, b = c_ref[...], a_ref[...], b_ref[...]
        u_ref[...] = _dot(hn, w_scr[0])
        for shard in (1, 2):
            res = _dot(hn, w_scr[shard])
            for t in range(4):
                lanes = slice(t * 128, (t + 1) * 128)
                out = slice((shard - 1) * COL_SHARD + t * 128, (shard - 1) * COL_SHARD + (t + 1) * 128)
                q_ref[:, out] = (_rope_lanes(res[:, lanes], c, a, b) * ATTN_SCALE).astype(MXU_DTYPE)
        res = _dot(hn, w_scr[3])
        for t in range(2):
            lanes = slice(t * 128, (t + 1) * 128)
            k_ref[:, lanes] = _rope_lanes(res[:, lanes], c, a, b).astype(MXU_DTYPE)
        v_ref[...] = res[:, D_KV:].astype(MXU_DTYPE)
        for shard in range(4, N_DEV):
            gate_ref[:, (shard - 4) * COL_SHARD:(shard - 3) * COL_SHARD] = _dot(hn, w_scr[shard])

    tab = pl.BlockSpec((tm, 128), lambda i: (i, 0))
    kv = pl.BlockSpec((tm, D_KV), lambda i: (i, 0))
    row_d = pl.BlockSpec((tm, D), lambda i: (i, 0))
    return _pcall(
        body, name=f"in_proj_l{layer}", grid=(rows // tm,),
        in_specs=[row_d, pl.BlockSpec((None, 1, D), lambda i: (layer, 0, 0)),
                  pl.BlockSpec(memory_space=pl.ANY), tab, tab, tab] + [pl.BlockSpec(memory_space=pl.ANY)] * len(after),
        out_specs=[pl.BlockSpec((tm, D_SSM), lambda i: (i, 0)), pl.BlockSpec((tm, 2 * D), lambda i: (i, 0)),
                   row_d, kv, kv, row_d],
        out_shape=[SDS((rows, D_SSM), F32), SDS((rows, 2 * D), F32), SDS((rows, D_ATTN), MXU_DTYPE),
                   SDS((rows, D_KV), MXU_DTYPE), SDS((rows, D_KV), MXU_DTYPE), SDS((rows, D), MXU_DTYPE)],
        scratch_shapes=[pltpu.VMEM((N_DEV, D, COL_SHARD), MXU_DTYPE), pltpu.SemaphoreType.DMA((N_DEV,))],
        compiler_params=_cparams("arbitrary"),
    )(hres, gain3, w_in_g, cos, sin_a, sin_b, *after)


SCAN_TILE = 8


def _scan_tiles(x_ref, out_ref, tre_ref, tim_ref, sb, t_r, t_i, reverse, prev_ref=None):
    base = 4 if reverse else 0
    n_tiles = BLK // SCAN_TILE
    row = lax.broadcasted_iota(jnp.int32, (SCAN_TILE, SB_STATES), 0)
    for j in (range(n_tiles - 1, -1, -1) if reverse else range(n_tiles)):
        rows = slice(SCAN_TILE * j, SCAN_TILE * (j + 1))
        xr = x_ref[rows, :SB_STATES]
        xi = x_ref[rows, SB_STATES:]
        for k in range(3):
            shift = SCAN_TILE - (1 << k) if reverse else (1 << k)
            rr = pltpu.roll(xr, shift, 0)
            ri = pltpu.roll(xi, shift, 0)
            ar = tre_ref[sb, base + k]
            ai = tim_ref[sb, base + k]
            xr, xi = xr + (ar * rr - ai * ri), xi + (ar * ri + ai * rr)
        pr = tre_ref[sb, base + 3]
        pi = tim_ref[sb, base + 3]
        xr, xi = xr + (pr * t_r - pi * t_i), xi + (pr * t_i + pi * t_r)
        out_ref[rows, :SB_STATES] = xr
        out_ref[rows, SB_STATES:] = xi
        if prev_ref is not None:
            prev_ref[rows, :SB_STATES] = jnp.where(row == 0, t_r, pltpu.roll(xr, 1, 0))
            prev_ref[rows, SB_STATES:] = jnp.where(row == 0, t_i, pltpu.roll(xi, 1, 0))
        edge = slice(0, 1) if reverse else slice(SCAN_TILE - 1, SCAN_TILE)
        t_r, t_i = xr[edge], xi[edge]
    return t_r, t_i


def _s5_fwd(u, ssm, w_glu, b_glu3, layer):
    rows = u.shape[0]
    n_chunks = rows // BLK
    b_mat, c_mat, t_re, t_im, d_skip = (ssm[k] for k in ("b_mat", "c_mat", "t_re", "t_im", "d_skip"))

    def body(u_ref, bm_ref, cm_ref, tre_ref, tim_ref, d_ref, wg_ref, bg_ref,
             y_ref, ys_ref, cin_ref, carry, bu_scr, s_scr):
        @pl.when(pl.program_id(0) == 0)
        def _():
            carry[...] = jnp.zeros_like(carry)

        cin_ref[...] = carry[...]
        u = u_ref[...]
        for sb in range(N_SB):
            cols = slice(sb * 128, (sb + 1) * 128)
            u_sb = u[:, cols]
            bu_scr[sb] = _dot(u_sb.astype(MXU_DTYPE), bm_ref[sb])
            t_r, t_i = _scan_tiles(bu_scr.at[sb], s_scr.at[sb], tre_ref, tim_ref, sb,
                                   carry[2 * sb:2 * sb + 1, :], carry[2 * sb + 1:2 * sb + 2, :], False)
            carry[2 * sb:2 * sb + 1, :] = t_r
            carry[2 * sb + 1:2 * sb + 2, :] = t_i
            y_ref[:, cols] = _dot(s_scr[sb].astype(MXU_DTYPE), cm_ref[sb]) + d_ref[:, cols] * u_sb
        z, _ = _gelu_parts(y_ref[...])
        gl = _dot(z.astype(MXU_DTYPE), wg_ref[...]) + bg_ref[...]
        ys_ref[...] = (z * _sigmoid(gl)).astype(MXU_DTYPE)

    full = lambda shape: pl.BlockSpec(shape, lambda j: (0,) * len(shape))
    of_layer = lambda shape: pl.BlockSpec((None,) + shape, lambda j: (layer,) + (0,) * len(shape))
    return _pcall(
        body, name=f"s5_fwd_l{layer}", grid=(n_chunks,),
        in_specs=[pl.BlockSpec((BLK, D_SSM), lambda j: (j, 0)),
                  of_layer((N_SB, 128, 2 * SB_STATES)), of_layer((N_SB, 2 * SB_STATES, 128)),
                  of_layer((N_SB, 8, SCAN_TILE, SB_STATES)), of_layer((N_SB, 8, SCAN_TILE, SB_STATES)),
                  of_layer((1, D_SSM)), full((D_SSM, D_SSM)),
                  pl.BlockSpec((None, 1, D_SSM), lambda j: (layer, 0, 0))],
        out_specs=[pl.BlockSpec((BLK, D_SSM), lambda j: (j, 0)), pl.BlockSpec((BLK, D_SSM), lambda j: (j, 0)),
                   pl.BlockSpec((None, 8, SB_STATES), lambda j: (j, 0, 0))],
        out_shape=[SDS((rows, D_SSM), F32), SDS((rows, D_SSM), MXU_DTYPE), SDS((n_chunks, 8, SB_STATES), F32)],
        scratch_shapes=[pltpu.VMEM((8, SB_STATES), F32), pltpu.VMEM((N_SB, BLK, 2 * SB_STATES), F32),
                        pltpu.VMEM((N_SB, BLK, 2 * SB_STATES), F32)],
        compiler_params=_cparams("arbitrary"),
    )(u, b_mat, c_mat, t_re, t_im, d_skip, w_glu, b_glu3)


def _attn_mask(i):
    row = lax.broadcasted_iota(jnp.int32, (BLK, 3 * BLK), 0) + i * BLK
    col = lax.broadcasted_iota(jnp.int32, (BLK, 3 * BLK), 1)
    seg = jnp.right_shift(col, 7)
    c = jnp.bitwise_and(col, BLK - 1)
    kidx = c + (i + seg - 2) * BLK
    ok_meta = (seg == 0) & (c >= PAD_ROWS) & (row - c >= BLK)
    ok_win = (seg > 0) & (kidx >= PAD_ROWS) & (kidx <= row) & (row - kidx < BLK)
    return jnp.where(ok_meta | ok_win, 0.0, NEG_INF)


def _head_lanes(h):
    return slice(h * HEAD_DIM, (h + 1) * HEAD_DIM)


def _group_rows(ref, kvh):
    return jnp.concatenate([ref[:, _head_lanes(kvh * Q_PER_KV + g)] for g in range(Q_PER_KV)], axis=0)


def _group_bias(bias, sink_ref, layer, kvh):
    first_col = lax.broadcasted_iota(jnp.int32, (BLK, BLK), 1) == 0
    slabs = []
    for g in range(Q_PER_KV):
        first = jnp.where(first_col, sink_ref[layer, kvh * Q_PER_KV + g], bias[:, :BLK])
        slabs.append(jnp.concatenate([first, bias[:, BLK:]], axis=1))
    return jnp.concatenate(slabs, axis=0)


def _attn_probs(q4, k3, bias4):
    s = _dot_nt(q4, k3) + bias4
    e = jnp.exp(s - jnp.max(s, axis=-1, keepdims=True))
    return e * (1.0 / jnp.sum(e, axis=-1, keepdims=True))


def _attn_fwd(q, k, v, sinks, layer):
    rows = q.shape[0]
    n_blk = rows // BLK

    def body(sink_ref, q_ref, km_ref, kp_ref, kc_ref, vm_ref, vp_ref, vc_ref, o_ref):
        bias = _attn_mask(pl.program_id(0))
        for kvh in range(N_KV_HEADS):
            lanes = _head_lanes(kvh)
            k3 = jnp.concatenate([km_ref[:, lanes], kp_ref[:, lanes], kc_ref[:, lanes]], axis=0)
            v3 = jnp.concatenate([vm_ref[:, lanes], vp_ref[:, lanes], vc_ref[:, lanes]], axis=0)
            p = _attn_probs(_group_rows(q_ref, kvh), k3, _group_bias(bias, sink_ref, layer, kvh))
            o4 = _dot(p.astype(MXU_DTYPE), v3).astype(MXU_DTYPE)
            for g in range(Q_PER_KV):
                o_ref[:, _head_lanes(kvh * Q_PER_KV + g)] = o4[g * BLK:(g + 1) * BLK]

    kv_meta = pl.BlockSpec((BLK, D_KV), lambda i: (0, 0))
    kv_prev = pl.BlockSpec((BLK, D_KV), lambda i: (jnp.maximum(i - 1, 0), 0))
    kv_cur = pl.BlockSpec((BLK, D_KV), lambda i: (i, 0))
    return _pcall(
        body, name=f"attn_fwd_l{layer}", grid=(n_blk,),
        in_specs=[pl.BlockSpec(memory_space=pltpu.SMEM),
                  pl.BlockSpec((BLK, D_ATTN), lambda i: (i, 0)),
                  kv_meta, kv_prev, kv_cur, kv_meta, kv_prev, kv_cur],
        out_specs=pl.BlockSpec((BLK, D_ATTN), lambda i: (i, 0)),
        out_shape=SDS((rows, D_ATTN), MXU_DTYPE),
        compiler_params=_cparams("parallel"),
    )(sinks, q, k, k, k, v, v, v)


def _merge_fwd(y_ssm, y_attn, gates, hres, w_o_ssm, w_o_attn, w_out, gain3, layer):
    rows = hres.shape[0]
    tm = _row_tile(rows, 320)

    def body(ys_ref, ya_ref, gs_ref, ga_ref, x_ref, wos_ref, woa_ref, wout_ref, g_ref,
             mg_ref, mix_ref, out_ref):
        a1 = _dot(ys_ref[...], wos_ref[...])
        a2 = _dot(ya_ref[...], woa_ref[...])
        merged = (_sigmoid(gs_ref[...]) * a1 + _sigmoid(ga_ref[...]) * a2).astype(MXU_DTYPE)
        mg_ref[...] = merged
        mix = _dot(merged, wout_ref[...])
        mix_ref[...] = mix
        out_ref[...] = x_ref[...] + _rms_fwd(mix, g_ref[...])

    row_d = pl.BlockSpec((tm, D), lambda i: (i, 0))
    full = lambda shape: pl.BlockSpec(shape, lambda i: (0,) * len(shape))
    return _pcall(
        body, name=f"merge_fwd_l{layer}", grid=(rows // tm,),
        in_specs=[pl.BlockSpec((tm, D_SSM), lambda i: (i, 0)), row_d,
                  row_d, pl.BlockSpec((tm, D), lambda i: (i, 1)), row_d,
                  full((D_SSM, D)), full((D_ATTN, D)), full((D, D)),
                  pl.BlockSpec((None, 1, D), lambda i: (layer, 0, 0))],
        out_specs=[row_d, row_d, row_d],
        out_shape=[SDS((rows, D), MXU_DTYPE), SDS((rows, D), F32), SDS((rows, D), F32)],
        compiler_params=_cparams("parallel"),
    )(y_ssm, y_attn, gates, gates, hres, w_o_ssm, w_o_attn, w_out, gain3)


def _mlp_fwd(hres, gain_pre3, gain_post3, w_up_g, w_down_g, layer, target=None):
    rows = hres.shape[0]
    tm = _row_tile(rows, 320)

    def body(x_ref, gp_ref, gq_ref, wu_hbm, wd_hbm, *refs):
        if target is None:
            up_ref, h_ref, ff_ref, out_ref, act_scr, wu_scr, wd_scr, wu_sem, wd_sem = refs
        else:
            t_ref, up_ref, h_ref, ff_ref, out_ref, loss_ref, act_scr, wu_scr, wd_scr, wu_sem, wd_sem = refs
        first = pl.program_id(0) == 0
        _load_resident(wu_hbm, wu_scr, wu_sem, first)
        _load_resident(wd_hbm, wd_scr, wd_sem, first)
        hn = _rms_fwd(x_ref[...], gp_ref[...]).astype(MXU_DTYPE)
        h_ref[...] = hn
        for kf in range(N_DEV):
            cols = slice(kf * COL_SHARD, (kf + 1) * COL_SHARD)
            up = _dot(hn, wu_scr[kf])
            up_ref[:, cols] = up.astype(MXU_DTYPE)
            r = jnp.maximum(up, 0.0)
            act_scr[:, cols] = (r * r).astype(MXU_DTYPE)
        ff = _dot(act_scr[...], wd_scr[...].reshape(D_FF, D))
        ff_ref[...] = ff
        out = x_ref[...] + _rms_fwd(ff, gq_ref[...])
        if target is None:
            out_ref[...] = out
        else:
            @pl.when(first)
            def _():
                loss_ref[...] = jnp.zeros_like(loss_ref)

            row = lax.broadcasted_iota(jnp.int32, (tm, D), 0) + pl.program_id(0) * tm
            err = jnp.where(row >= BLK, out - t_ref[...], 0.0)
            out_ref[...] = err * (1.0 / D)
            loss_ref[...] += jnp.sum(err * err) * (0.5 / D)

    row_d = pl.BlockSpec((tm, D), lambda i: (i, 0))
    gain = pl.BlockSpec((None, 1, D), lambda i: (layer, 0, 0))
    with_loss = target is not None
    return _pcall(
        body, name=f"mlp_fwd_l{layer}", grid=(rows // tm,),
        in_specs=[row_d, gain, gain, pl.BlockSpec(memory_space=pl.ANY), pl.BlockSpec(memory_space=pl.ANY)]
        + [row_d] * with_loss,
        out_specs=[pl.BlockSpec((tm, D_FF), lambda i: (i, 0)), row_d, row_d, row_d]
        + [pl.BlockSpec((1, 128), lambda i: (0, 0))] * with_loss,
        out_shape=[SDS((rows, D_FF), MXU_DTYPE), SDS((rows, D), MXU_DTYPE), SDS((rows, D), F32), SDS((rows, D), F32)]
        + [SDS((1, 128), F32)] * with_loss,
        scratch_shapes=[pltpu.VMEM((tm, D_FF), MXU_DTYPE),
                        pltpu.VMEM((N_DEV, D, COL_SHARD), MXU_DTYPE), pltpu.VMEM((N_DEV, COL_SHARD, D), MXU_DTYPE),
                        pltpu.SemaphoreType.DMA((N_DEV,)), pltpu.SemaphoreType.DMA((N_DEV,))],
        compiler_params=_cparams("arbitrary"),
    )(hres, gain_pre3, gain_post3, w_up_g, w_down_g, *([target] if with_loss else []))


def _relu_squared(up):
    r = jnp.maximum(up.astype(F32), 0.0)
    return (r * r).astype(MXU_DTYPE)


def _matmul_tn(a, b, name, dev_major_cols=None, a_fn=None):
    rows, ka = a.shape
    n = b.shape[1]
    ta = min(ka, 1024)
    tn = 1024 if n % 1024 == 0 else 512
    tr = _row_tile(rows, 1664)
    n_r = rows // tr

    def body(a_ref, b_ref, o_ref, acc):
        r = pl.program_id(2)

        @pl.when(r == 0)
        def _():
            acc[...] = jnp.zeros_like(acc)

        a_blk = a_ref[...] if a_fn is None else a_fn(a_ref[...])
        acc[...] += _dot_tn(a_blk, b_ref[...])

        @pl.when(r == n_r - 1)
        def _():
            if dev_major_cols is None:
                o_ref[...] = acc[...].astype(XFER_DTYPE)
            else:
                for s in range(tn // dev_major_cols):
                    o_ref[s] = acc[:, s * dev_major_cols:(s + 1) * dev_major_cols].astype(XFER_DTYPE)

    if dev_major_cols is None:
        out_spec = pl.BlockSpec((ta, tn), lambda i, j, r: (i, j))
        out_shape = SDS((ka, n), XFER_DTYPE)
    else:
        w = dev_major_cols
        out_spec = pl.BlockSpec((tn // w, ta, w), lambda i, j, r: (j, i, 0))
        out_shape = SDS((n // w, ka, w), XFER_DTYPE)
    return _pcall(
        body, name=name, grid=(ka // ta, n // tn, n_r),
        in_specs=[pl.BlockSpec((tr, ta), lambda i, j, r: (r, i)), pl.BlockSpec((tr, tn), lambda i, j, r: (r, j))],
        out_specs=out_spec, out_shape=out_shape,
        scratch_shapes=[pltpu.VMEM((ta, tn), F32)],
        compiler_params=_cparams("parallel", "parallel", "arbitrary"),
    )(a, b)


def _dw_in(h, dproj_pieces, layer):
    rows = h.shape[0]
    tr = _row_tile(rows, 1664)
    n_r = rows // tr

    def body(h_ref, du_ref, dqkv_ref, dgs_ref, dga_ref, o_ref, acc):
        j = pl.program_id(0)
        r = pl.program_id(1)

        @pl.when(r == 0)
        def _():
            acc[...] = jnp.zeros_like(acc)

        for piece_ref, (first, count) in zip((du_ref, dqkv_ref, dgs_ref, dga_ref), DPROJ_PIECES):
            @pl.when((j >= first) & (j < first + count))
            def _():
                acc[...] += _dot_tn(h_ref[...], piece_ref[...])

        @pl.when(r == n_r - 1)
        def _():
            o_ref[...] = acc[...].astype(XFER_DTYPE)

    def piece_spec(first, count):
        def index(j, r):
            mine = (j >= first) & (j < first + count)
            return jnp.where(mine, r, 0), jnp.clip(j - first, 0, count - 1)
        return pl.BlockSpec((tr, COL_SHARD), index)

    return _pcall(
        body, name=f"dw_in_l{layer}", grid=(N_DEV, n_r),
        in_specs=[pl.BlockSpec((tr, D), lambda j, r: (r, 0))] + [piece_spec(*p) for p in DPROJ_PIECES],
        out_specs=pl.BlockSpec((None, D, COL_SHARD), lambda j, r: (j, 0, 0)),
        out_shape=SDS((N_DEV, D, COL_SHARD), XFER_DTYPE),
        scratch_shapes=[pltpu.VMEM((D, COL_SHARD), F32)],
        compiler_params=_cparams("arbitrary", "arbitrary"),
    )(h, *dproj_pieces)


def _mlp_bwd(dout, ff, up, hres_mid, gain_pre3, gain_post3, w_up_g, w_down_g, layer):
    rows = dout.shape[0]
    tm = _row_tile(rows, 320)

    def body(do_ref, ff_ref, up_ref, x_ref, gp_ref, gq_ref, wu_hbm, wd_hbm,
             dff_ref, dup_ref, dx_ref, dgq_ref, dgp_ref, wut_scr, wdt_scr, wu_stage, wd_stage, wu_sem, wd_sem):
        i = pl.program_id(0)
        _load_resident_transposed(wu_hbm, wut_scr, wu_stage, wu_sem, i == 0)
        _load_resident_transposed(wd_hbm, wdt_scr, wd_stage, wd_sem, i == 0)

        @pl.when(i == 0)
        def _():
            dgq_ref[...] = jnp.zeros_like(dgq_ref)
            dgp_ref[...] = jnp.zeros_like(dgp_ref)

        dff, dg = _rms_bwd(ff_ref[...], gq_ref[...], do_ref[...])
        dgq_ref[...] += dg
        dffb = dff.astype(MXU_DTYPE)
        dff_ref[...] = dffb
        for kf in range(N_DEV):
            cols = slice(kf * COL_SHARD, (kf + 1) * COL_SHARD)
            dact = _dot(dffb, wdt_scr[kf])
            dup_ref[:, cols] = (dact * (2.0 * jnp.maximum(up_ref[:, cols].astype(F32), 0.0))).astype(MXU_DTYPE)
        dh = _dot(dup_ref[...], wut_scr[...].reshape(D_FF, D))
        dx, dg = _rms_bwd(x_ref[...], gp_ref[...], dh)
        dgp_ref[...] += dg
        dx_ref[...] = do_ref[...] + dx

    row_d = pl.BlockSpec((tm, D), lambda i: (i, 0))
    row_ff = pl.BlockSpec((tm, D_FF), lambda i: (i, 0))
    gain = pl.BlockSpec((None, 1, D), lambda i: (layer, 0, 0))
    dgain = pl.BlockSpec((1, D), lambda i: (0, 0))
    return _pcall(
        body, name=f"mlp_bwd_l{layer}", grid=(rows // tm,),
        in_specs=[row_d, row_d, row_ff, row_d, gain, gain,
                  pl.BlockSpec(memory_space=pl.ANY), pl.BlockSpec(memory_space=pl.ANY)],
        out_specs=[row_d, row_ff, row_d, dgain, dgain],
        out_shape=[SDS((rows, D), MXU_DTYPE), SDS((rows, D_FF), MXU_DTYPE), SDS((rows, D), F32),
                   SDS((1, D), F32), SDS((1, D), F32)],
        scratch_shapes=[pltpu.VMEM((N_DEV, COL_SHARD, D), MXU_DTYPE), pltpu.VMEM((N_DEV, D, COL_SHARD), MXU_DTYPE),
                        pltpu.VMEM((2, D, COL_SHARD), MXU_DTYPE), pltpu.VMEM((2, COL_SHARD, D), MXU_DTYPE),
                        pltpu.SemaphoreType.DMA((2,)), pltpu.SemaphoreType.DMA((2,))],
        compiler_params=_cparams("arbitrary"),
    )(dout, ff, up, hres_mid, gain_pre3, gain_post3, w_up_g, w_down_g)


def _merge_bwd(dhm, mix, y_ssm, y_attn, gates, w_o_ssm, w_o_attn, w_o_ssm_t, w_o_attn_t, w_out_t, gain3, layer):
    rows = dhm.shape[0]
    tm = _row_tile(rows, 320)

    def body(dh_ref, mix_ref, ys_ref, ya_ref, gs_ref, ga_ref, wos_ref, woa_ref, wost_ref, woat_ref, woutt_ref, g_ref,
             dmix_ref, da1_ref, da2_ref, dgs_ref, dga_ref, dys_ref, dya_ref, dg_ref):
        @pl.when(pl.program_id(0) == 0)
        def _():
            dg_ref[...] = jnp.zeros_like(dg_ref)

        dmix, dg = _rms_bwd(mix_ref[...], g_ref[...], dh_ref[...])
        dg_ref[...] += dg
        dmixb = dmix.astype(MXU_DTYPE)
        dmix_ref[...] = dmixb
        dmerged = _dot(dmixb, woutt_ref[...])
        sg_s = _sigmoid(gs_ref[...])
        sg_a = _sigmoid(ga_ref[...])
        da1 = (dmerged * sg_s).astype(MXU_DTYPE)
        da2 = (dmerged * sg_a).astype(MXU_DTYPE)
        da1_ref[...] = da1
        da2_ref[...] = da2
        a1 = _dot(ys_ref[...], wos_ref[...])
        a2 = _dot(ya_ref[...], woa_ref[...])
        dgs_ref[...] = (dmerged * a1 * (sg_s * (1.0 - sg_s))).astype(MXU_DTYPE)
        dga_ref[...] = (dmerged * a2 * (sg_a * (1.0 - sg_a))).astype(MXU_DTYPE)
        dys_ref[...] = _dot(da1, wost_ref[...])
        dya_ref[...] = _dot(da2, woat_ref[...])

    row_d = pl.BlockSpec((tm, D), lambda i: (i, 0))
    full = lambda shape: pl.BlockSpec(shape, lambda i: (0,) * len(shape))
    return _pcall(
        body, name=f"merge_bwd_l{layer}", grid=(rows // tm,),
        in_specs=[row_d, row_d, pl.BlockSpec((tm, D_SSM), lambda i: (i, 0)), row_d,
                  row_d, pl.BlockSpec((tm, D), lambda i: (i, 1)),
                  full((D_SSM, D)), full((D_ATTN, D)), full((D, D_SSM)), full((D, D_ATTN)), full((D, D)),
                  pl.BlockSpec((None, 1, D), lambda i: (layer, 0, 0))],
        out_specs=[row_d, row_d, row_d, row_d, row_d, pl.BlockSpec((tm, D_SSM), lambda i: (i, 0)), row_d,
                   pl.BlockSpec((1, D), lambda i: (0, 0))],
        out_shape=[SDS((rows, D), MXU_DTYPE)] * 5 + [SDS((rows, D_SSM), F32), SDS((rows, D_ATTN), F32),
                                                      SDS((1, D), F32)],
        compiler_params=_cparams("arbitrary"),
    )(dhm, mix, y_ssm, y_attn, gates, gates, w_o_ssm, w_o_attn, w_o_ssm_t, w_o_attn_t, w_out_t, gain3)


def _attn_bwd(q, k, v, d_out, sinks, layer):
    rows = q.shape[0]
    n_blk = rows // BLK
    last = n_blk - 1

    def body(sink_ref, q_ref, km_ref, kp_ref, kc_ref, vm_ref, vp_ref, vc_ref, do_ref,
             dq_ref, dk_ref, dv_ref, dkm_ref, dvm_ref, ds_ref, dk_carry, dv_carry):
        i = pl.program_id(0)

        @pl.when(i == 0)
        def _():
            dkm_ref[...] = jnp.zeros_like(dkm_ref)
            dvm_ref[...] = jnp.zeros_like(dvm_ref)
            ds_ref[...] = jnp.zeros_like(ds_ref)
            dk_carry[...] = jnp.zeros_like(dk_carry)
            dv_carry[...] = jnp.zeros_like(dv_carry)

        @pl.when(i <= last)
        def _():
            bias = _attn_mask(i)
            for kvh in range(N_KV_HEADS):
                lanes = _head_lanes(kvh)
                k3 = jnp.concatenate([km_ref[:, lanes], kp_ref[:, lanes], kc_ref[:, lanes]], axis=0)
                v3 = jnp.concatenate([vm_ref[:, lanes], vp_ref[:, lanes], vc_ref[:, lanes]], axis=0)
                q4 = _group_rows(q_ref, kvh)
                do4 = _group_rows(do_ref, kvh).astype(MXU_DTYPE)
                p = _attn_probs(q4, k3, _group_bias(bias, sink_ref, layer, kvh))
                dp = _dot_nt(do4, v3)
                dsf = p * (dp - jnp.sum(dp * p, axis=-1, keepdims=True))
                dsc = dsf.astype(MXU_DTYPE)
                dv3 = _dot_tn(p.astype(MXU_DTYPE), do4)
                dk3 = _dot_tn(dsc, q4)
                dq4 = _dot(dsc, k3)
                for g in range(Q_PER_KV):
                    h = kvh * Q_PER_KV + g
                    dq_ref[:, _head_lanes(h)] = dq4[g * BLK:(g + 1) * BLK]
                    ds_ref[h:h + 1, :] += jnp.sum(dsf[g * BLK:(g + 1) * BLK, 0:BLK], axis=0, keepdims=True)
                dkm_ref[:, lanes] += dk3[0:BLK]
                dvm_ref[:, lanes] += dv3[0:BLK]
                dk_ref[:, lanes] = dk_carry[:, lanes] + dk3[BLK:2 * BLK]
                dv_ref[:, lanes] = dv_carry[:, lanes] + dv3[BLK:2 * BLK]
                dk_carry[:, lanes] = dk3[2 * BLK:3 * BLK]
                dv_carry[:, lanes] = dv3[2 * BLK:3 * BLK]

        @pl.when(i == last + 1)
        def _():
            dk_ref[...] = dk_carry[...]
            dv_ref[...] = dv_carry[...]

    cur = lambda i: (jnp.minimum(i, last), 0)
    prev = lambda i: (jnp.clip(i - 1, 0, last), 0)
    kv_meta = pl.BlockSpec((BLK, D_KV), lambda i: (0, 0))
    kv_prev = pl.BlockSpec((BLK, D_KV), prev)
    kv_cur = pl.BlockSpec((BLK, D_KV), cur)
    return _pcall(
        body, name=f"attn_bwd_l{layer}", grid=(n_blk + 1,),
        in_specs=[pl.BlockSpec(memory_space=pltpu.SMEM),
                  pl.BlockSpec((BLK, D_ATTN), cur),
                  kv_meta, kv_prev, kv_cur, kv_meta, kv_prev, kv_cur,
                  pl.BlockSpec((BLK, D_ATTN), cur)],
        out_specs=[pl.BlockSpec((BLK, D_ATTN), cur), kv_prev, kv_prev, kv_meta, kv_meta,
                   pl.BlockSpec((N_Q_HEADS, 128), lambda i: (0, 0))],
        out_shape=[SDS((rows, D_ATTN), F32), SDS((rows, D_KV), F32), SDS((rows, D_KV), F32),
                   SDS((BLK, D_KV), F32), SDS((BLK, D_KV), F32), SDS((N_Q_HEADS, 128), F32)],
        scratch_shapes=[pltpu.VMEM((BLK, D_KV), F32), pltpu.VMEM((BLK, D_KV), F32)],
        compiler_params=_cparams("arbitrary"),
    )(sinks, q, k, k, k, v, v, v, d_out)


def _rope_bwd(dq, dk, dv, dk_meta, dv_meta, cos, sin_a, sin_b, layer):
    rows = dq.shape[0]
    tm = _row_tile(rows)

    def body(dq_ref, dk_ref, dv_ref, dkm_ref, dvm_ref, c_ref, a_ref, b_ref, o_ref):
        c, a, b = c_ref[...], -a_ref[...], -b_ref[...]
        for t in range(8):
            x = dq_ref[:, t * 128:(t + 1) * 128]
            o_ref[:, t * 128:(t + 1) * 128] = (_rope_lanes(x, c, a, b) * ATTN_SCALE).astype(MXU_DTYPE)
        for t in range(2):
            x = dk_ref[:, t * 128:(t + 1) * 128]
            o_ref[:, D_ATTN + t * 128:D_ATTN + (t + 1) * 128] = _rope_lanes(x, c, a, b).astype(MXU_DTYPE)
        o_ref[:, D_ATTN + D_KV:] = dv_ref[...].astype(MXU_DTYPE)

        @pl.when(pl.program_id(0) == 0)
        def _():
            cb, ab, bb = c[0:BLK], a[0:BLK], b[0:BLK]
            is_meta = lax.broadcasted_iota(jnp.int32, (BLK, 128), 0) >= PAD_ROWS
            for t in range(2):
                x = dk_ref[0:BLK, t * 128:(t + 1) * 128] + jnp.where(is_meta, dkm_ref[:, t * 128:(t + 1) * 128], 0.0)
                o_ref[0:BLK, D_ATTN + t * 128:D_ATTN + (t + 1) * 128] = _rope_lanes(x, cb, ab, bb).astype(MXU_DTYPE)
                xv = dv_ref[0:BLK, t * 128:(t + 1) * 128] + jnp.where(is_meta, dvm_ref[:, t * 128:(t + 1) * 128], 0.0)
                o_ref[0:BLK, D_ATTN + D_KV + t * 128:D_ATTN + D_KV + (t + 1) * 128] = xv.astype(MXU_DTYPE)

    tab = pl.BlockSpec((tm, 128), lambda i: (i, 0))
    kv = pl.BlockSpec((tm, D_KV), lambda i: (i, 0))
    meta = pl.BlockSpec((BLK, D_KV), lambda i: (0, 0))
    return _pcall(
        body, name=f"rope_bwd_l{layer}", grid=(rows // tm,),
        in_specs=[pl.BlockSpec((tm, D_ATTN), lambda i: (i, 0)), kv, kv, meta, meta, tab, tab, tab],
        out_specs=pl.BlockSpec((tm, D_ATTN + 2 * D_KV), lambda i: (i, 0)),
        out_shape=SDS((rows, D_ATTN + 2 * D_KV), MXU_DTYPE),
        compiler_params=_cparams("parallel"),
    )(dq, dk, dv, dk_meta, dv_meta, cos, sin_a, sin_b)


def _s5_bwd(d_gated, y, u, carry_in, ssm, w_glu, b_glu3, layer):
    rows = y.shape[0]
    n_chunks = rows // BLK
    b_mat, c_mat, t_re, t_im, d_skip = (ssm[k] for k in ("b_mat", "c_mat", "t_re", "t_im", "d_skip"))

    def body(dz_ref, y_ref, u_ref, cin_ref, bm_ref, cm_ref, tre_ref, tim_ref, d_ref, wg_ref, bg_ref,
             du_ref, dwg_ref, dbg_ref, dd_ref, dbm_ref, dcm_ref, dab_ref,
             lam_carry, bu_scr, s_scr, sp_scr, g_scr, lam_scr):
        step = pl.program_id(0)
        chunk = n_chunks - 1 - step

        @pl.when(step == 0)
        def _():
            for r in (dwg_ref, dbg_ref, dd_ref, dbm_ref, dcm_ref, dab_ref, lam_carry):
                r[...] = jnp.zeros_like(r)

        y = y_ref[...]
        u = u_ref[...]
        d_o = dz_ref[...]
        z, t = _gelu_parts(y)
        zb = z.astype(MXU_DTYPE)
        sg = _sigmoid(_dot(zb, wg_ref[...]) + bg_ref[...])
        dgl = d_o * z * (sg * (1.0 - sg))
        dglb = dgl.astype(MXU_DTYPE)
        dz = d_o * sg + _dot_nt(dglb, wg_ref[...])
        dwg_ref[...] += _dot_tn(zb, dglb)
        dbg_ref[...] += jnp.sum(dgl, axis=0, keepdims=True)
        dy = dz * _gelu_grad(y, t)
        dd_ref[...] += jnp.sum(dy * u, axis=0, keepdims=True)
        grow = lax.broadcasted_iota(jnp.int32, (BLK, 128), 0) + chunk * BLK
        for sb in range(N_SB):
            cols = slice(sb * 128, (sb + 1) * 128)
            u_sb = u[:, cols].astype(MXU_DTYPE)
            dy_sb = dy[:, cols]
            dyb = dy_sb.astype(MXU_DTYPE)
            bu_scr[sb] = _dot(u_sb, bm_ref[sb])
            _scan_tiles(bu_scr.at[sb], s_scr.at[sb], tre_ref, tim_ref, sb,
                        cin_ref[2 * sb:2 * sb + 1, :], cin_ref[2 * sb + 1:2 * sb + 2, :], False, prev_ref=sp_scr.at[sb])
            dcm_ref[sb] += _dot_tn(s_scr[sb].astype(MXU_DTYPE), dyb)
            g_scr[sb] = _dot_nt(dyb, cm_ref[sb])
            n_r, n_i = _scan_tiles(g_scr.at[sb], lam_scr.at[sb], tre_ref, tim_ref, sb,
                                   lam_carry[2 * sb:2 * sb + 1, :], lam_carry[2 * sb + 1:2 * sb + 2, :], True)
            lam_carry[2 * sb:2 * sb + 1, :] = n_r
            lam_carry[2 * sb + 1:2 * sb + 2, :] = n_i
            lr, li = lam_scr[sb, :, :SB_STATES], lam_scr[sb, :, SB_STATES:]
            spr, spi = sp_scr[sb, :, :SB_STATES], sp_scr[sb, :, SB_STATES:]
            dab_ref[2 * sb:2 * sb + 1, :] += jnp.sum(spr * lr + spi * li, axis=0, keepdims=True)
            dab_ref[2 * sb + 1:2 * sb + 2, :] += jnp.sum(spr * li - spi * lr, axis=0, keepdims=True)
            lam = lam_scr[sb].astype(MXU_DTYPE)
            dbm_ref[sb] += _dot_tn(u_sb, lam)
            du = _dot_nt(lam, bm_ref[sb]) + d_ref[:, cols] * dy_sb
            du_ref[:, cols] = jnp.where(grow >= PAD_ROWS, du, 0.0).astype(MXU_DTYPE)

    rev = lambda j: (n_chunks - 1 - j, 0)
    full = lambda shape: pl.BlockSpec(shape, lambda j: (0,) * len(shape))
    of_layer = lambda shape: pl.BlockSpec((None,) + shape, lambda j: (layer,) + (0,) * len(shape))
    tables = [of_layer((N_SB, 8, SCAN_TILE, SB_STATES))] * 2
    chunk_scratch = pltpu.VMEM((N_SB, BLK, 2 * SB_STATES), F32)
    return _pcall(
        body, name=f"s5_bwd_l{layer}", grid=(n_chunks,),
        in_specs=[pl.BlockSpec((BLK, D_SSM), rev), pl.BlockSpec((BLK, D_SSM), rev), pl.BlockSpec((BLK, D_SSM), rev),
                  pl.BlockSpec((None, 8, SB_STATES), lambda j: (n_chunks - 1 - j, 0, 0)),
                  of_layer((N_SB, 128, 2 * SB_STATES)), of_layer((N_SB, 2 * SB_STATES, 128))] + tables + [
                  of_layer((1, D_SSM)), full((D_SSM, D_SSM)),
                  pl.BlockSpec((None, 1, D_SSM), lambda j: (layer, 0, 0))],
        out_specs=[pl.BlockSpec((BLK, D_SSM), rev), full((D_SSM, D_SSM)), full((1, D_SSM)), full((1, D_SSM)),
                   full((N_SB, 128, 2 * SB_STATES)), full((N_SB, 2 * SB_STATES, 128)), full((8, SB_STATES))],
        out_shape=[SDS((rows, D_SSM), MXU_DTYPE), SDS((D_SSM, D_SSM), F32), SDS((1, D_SSM), F32), SDS((1, D_SSM), F32),
                   SDS((N_SB, 128, 2 * SB_STATES), F32), SDS((N_SB, 2 * SB_STATES, 128), F32), SDS((8, SB_STATES), F32)],
        scratch_shapes=[pltpu.VMEM((8, SB_STATES), F32)] + [chunk_scratch] * 5,
        compiler_params=_cparams("arbitrary"),
    )(d_gated, y, u, carry_in, b_mat, c_mat, t_re, t_im, d_skip, w_glu, b_glu3)


DPROJ_PIECES = ((0, 1), (1, 3), (4, 2), (6, 2))


def _in_bwd(dproj_pieces, dhm, hres, gain3, w_in_g, layer):
    rows = hres.shape[0]
    tm = _row_tile(rows)

    def body(du_ref, dqkv_ref, dgs_ref, dga_ref, dh_ref, x_ref, g_ref, w_hbm, dx_ref, dg_ref,
             wt_scr, w_stage, w_sem):
        i = pl.program_id(0)
        _load_resident_transposed(w_hbm, wt_scr, w_stage, w_sem, i == 0)

        @pl.when(i == 0)
        def _():
            dg_ref[...] = jnp.zeros_like(dg_ref)

        dh = None
        for piece_ref, (first, count) in zip((du_ref, dqkv_ref, dgs_ref, dga_ref), DPROJ_PIECES):
            wt = wt_scr[first:first + count].reshape(count * COL_SHARD, D)
            part = _dot(piece_ref[...], wt)
            dh = part if dh is None else dh + part
        dx, dg = _rms_bwd(x_ref[...], g_ref[...], dh)
        dg_ref[...] += dg
        dx_ref[...] = dh_ref[...] + dx

    row_d = pl.BlockSpec((tm, D), lambda i: (i, 0))
    return _pcall(
        body, name=f"in_bwd_l{layer}", grid=(rows // tm,),
        in_specs=[pl.BlockSpec((tm, count * COL_SHARD), lambda i: (i, 0)) for _, count in DPROJ_PIECES] + [
                  row_d, row_d,
                  pl.BlockSpec((None, 1, D), lambda i: (layer, 0, 0)),
                  pl.BlockSpec(memory_space=pl.ANY)],
        out_specs=[row_d, pl.BlockSpec((1, D), lambda i: (0, 0))],
        out_shape=[SDS((rows, D), F32), SDS((1, D), F32)],
        scratch_shapes=[pltpu.VMEM((N_DEV, COL_SHARD, D), MXU_DTYPE),
                        pltpu.VMEM((2, D, COL_SHARD), MXU_DTYPE), pltpu.SemaphoreType.DMA((2,))],
        compiler_params=_cparams("arbitrary"),
    )(*dproj_pieces, dhm, hres, gain3, w_in_g)


_ADAM_C1 = 1.0 / (1.0 - ADAM_B1 ** ADAM_STEP)
_ADAM_C2 = 1.0 / (1.0 - ADAM_B2 ** ADAM_STEP)


def _adam_math(w, g, m, v):
    m = ADAM_B1 * m + (1.0 - ADAM_B1) * g
    v = ADAM_B2 * v + (1.0 - ADAM_B2) * (g * g)
    delta = -ADAM_LR * ((m * _ADAM_C1) / (jnp.sqrt(v * _ADAM_C2) + ADAM_EPS) + ADAM_WD * w)
    return delta, m, v


def _adamw_layers(parts0, parts1, w, m, v, name):
    _, rows, cols = w.shape
    tr = min(rows, (1 << 16) // cols)
    nt = rows // tr

    def body(p0_ref, p1_ref, w_ref, m_ref, v_ref, g_ref, d_ref, nm_ref, nv_ref):
        layer = pl.program_id(0)

        def run(p_ref):
            g = p_ref[0].astype(F32)
            for s in range(1, N_DEV):
                g = g + p_ref[s].astype(F32)
            delta, nm, nv = _adam_math(w_ref[...], g, m_ref[...], v_ref[...])
            g_ref[...] = g
            d_ref[...] = delta
            nm_ref[...] = nm
            nv_ref[...] = nv

        @pl.when(layer == 0)
        def _():
            run(p0_ref)

        @pl.when(layer == 1)
        def _():
            run(p1_ref)

    wspec = pl.BlockSpec((None, tr, cols), lambda l, i: (l, i, 0))
    return _pcall(
        body, name=name, grid=(2, nt),
        in_specs=[pl.BlockSpec((N_DEV, tr, cols), lambda l, i: (0, jnp.where(l == 0, i, nt - 1), 0)),
                  pl.BlockSpec((N_DEV, tr, cols), lambda l, i: (0, jnp.where(l == 1, i, 0), 0)),
                  wspec, wspec, wspec],
        out_specs=[wspec] * 4, out_shape=[SDS(w.shape, F32)] * 4,
        compiler_params=_cparams("arbitrary", "arbitrary"),
    )(parts0, parts1, w, m, v)


def _sum_slots(parts, name):
    def body(p_ref, o_ref):
        acc = p_ref[0]
        for s in range(1, N_DEV):
            acc = acc + p_ref[s]
        o_ref[...] = acc

    vmem = pl.BlockSpec(memory_space=pltpu.VMEM)
    return _pcall(body, name=name, out_shape=SDS(parts.shape[1:], F32), in_specs=[vmem], out_specs=vmem,
                  compiler_params=_cparams())(parts)


def _adamw_packed(g, w, m, v, name):
    def body(g_ref, w_ref, m_ref, v_ref, d_ref, nm_ref, nv_ref):
        delta, nm, nv = _adam_math(w_ref[...], g_ref[...], m_ref[...], v_ref[...])
        d_ref[...] = delta
        nm_ref[...] = nm
        nv_ref[...] = nv

    vmem = pl.BlockSpec(memory_space=pltpu.VMEM)
    return _pcall(body, name=name, out_shape=[SDS(g.shape, F32)] * 3, in_specs=[vmem] * 4, out_specs=[vmem] * 3,
                  compiler_params=_cparams())(g, w, m, v)


def _ssm_discretize(a_re, a_im, log_dt, b_re, b_im):
    dt = jnp.exp(log_dt)[:, None]
    mag = jnp.exp(a_re * dt)
    ang = a_im * dt
    ab_re, ab_im = mag * jnp.cos(ang), mag * jnp.sin(ang)
    xr, xi = ab_re - 1.0, ab_im
    den = a_re * a_re + a_im * a_im
    q_re = (xr * a_re + xi * a_im) / den
    q_im = (xi * a_re - xr * a_im) / den
    bb_re = q_re[..., None] * b_re - q_im[..., None] * b_im
    bb_im = q_re[..., None] * b_im + q_im[..., None] * b_re
    return ab_re, ab_im, bb_re, bb_im


def _block_diag_b(bb):
    m = jnp.einsum("sgnc,gh->sgchn", bb.reshape(N_SB, 8, N_STATE, GROUP_CH), jnp.eye(8, dtype=F32))
    return m.reshape(N_SB, 128, SB_STATES)


def _block_diag_b_t(dm):
    return jnp.einsum("sgchn,gh->sgnc", dm.reshape(N_SB, 8, GROUP_CH, 8, N_STATE),
                      jnp.eye(8, dtype=F32)).reshape(N_GROUPS, N_STATE, GROUP_CH)


def _block_diag_c(cc):
    m = jnp.einsum("sgcn,gh->sgnhc", cc.reshape(N_SB, 8, GROUP_CH, N_STATE), jnp.eye(8, dtype=F32))
    return m.reshape(N_SB, SB_STATES, 128)


def _block_diag_c_t(dm):
    return jnp.einsum("sgnhc,gh->sgcn", dm.reshape(N_SB, 8, N_STATE, 8, GROUP_CH),
                      jnp.eye(8, dtype=F32)).reshape(N_GROUPS, GROUP_CH, N_STATE)


def _ssm_tables(ab_re, ab_im, bb_re, bb_im, c_re, c_im, d_skip):
    pr, pi = ab_re.reshape(1, -1), ab_im.reshape(1, -1)
    cr, ci = pr, pi
    squares = []
    for _ in range(3):
        squares.append((cr, ci))
        pr, pi = (jnp.concatenate([pr, pr * cr - pi * ci], axis=0),
                  jnp.concatenate([pi, pr * ci + pi * cr], axis=0))
        cr, ci = cr * cr - ci * ci, 2.0 * cr * ci
    r = jnp.arange(SCAN_TILE)[:, None]
    fwd = [(jnp.where(r >= (1 << k), squares[k][0], 0.0), jnp.where(r >= (1 << k), squares[k][1], 0.0))
           for k in range(3)] + [(pr, pi)]
    rev = [(jnp.where(r < SCAN_TILE - (1 << k), squares[k][0], 0.0),
            jnp.where(r < SCAN_TILE - (1 << k), -squares[k][1], 0.0)) for k in range(3)] + [(pr[::-1], -pi[::-1])]
    table = lambda part: jnp.stack([e[part] for e in fwd + rev]).reshape(
        8, SCAN_TILE, N_SB, SB_STATES).transpose(2, 0, 1, 3)
    return dict(
        b_mat=jnp.concatenate([_block_diag_b(bb_re), _block_diag_b(bb_im)], axis=-1).astype(MXU_DTYPE),
        c_mat=jnp.concatenate([_block_diag_c(c_re), -_block_diag_c(c_im)], axis=1).astype(MXU_DTYPE),
        t_re=table(0), t_im=table(1),
        d_skip=d_skip.reshape(1, D_SSM))


def _rope_tables(rows):
    pos = (jnp.arange(rows, dtype=jnp.int32) - PAD_ROWS).astype(F32)
    inv_freq = 1.0 / (ROPE_THETA ** (jnp.arange(0, HEAD_DIM, 2, dtype=F32) / HEAD_DIM))
    ang = pos[:, None] * inv_freq[None, :]
    ang = jnp.concatenate([ang, ang, ang, ang], axis=-1)
    first_half = (jnp.arange(128) % HEAD_DIM) < HEAD_DIM // 2
    sin = jnp.sin(ang)
    return jnp.cos(ang), jnp.where(first_half, -sin, 0.0), jnp.where(first_half, 0.0, sin)


def _pack(arrays):
    flat = jnp.concatenate([a.reshape(-1).astype(F32) for a in arrays])
    pad = (-flat.shape[0]) % 1024
    return jnp.pad(flat, (0, pad)).reshape(-1, 128)


def _unpack(packed, like):
    flat = packed.reshape(-1)
    out, off = [], 0
    for a in like:
        n = math.prod(a.shape)
        out.append(flat[off:off + n].reshape(a.shape))
        off += n
    return out


BIG = ("w_in", "w_glu", "w_o_ssm", "w_o_attn", "w_out", "w_up", "w_down")
WEIGHTS = ("meta_tokens", "norm_mix_pre", "norm_mix_post", "norm_mlp_pre", "norm_mlp_post", "w_in",
           "ssm_a_re", "ssm_a_im", "ssm_log_dt", "ssm_b_re", "ssm_b_im", "ssm_c_re", "ssm_c_im", "ssm_d",
           "w_glu", "b_glu", "attn_sinks", "w_o_ssm", "w_o_attn", "w_out", "w_up", "w_down")
SMALL = tuple(n for n in WEIGHTS if n not in BIG)


def kernel(x, meta_tokens, norm_mix_pre, norm_mix_post, norm_mlp_pre, norm_mlp_post, w_in, ssm_a_re, ssm_a_im, ssm_log_dt, ssm_b_re, ssm_b_im, ssm_c_re, ssm_c_im, ssm_d, w_glu, b_glu, attn_sinks, w_o_ssm, w_o_attn, w_out, w_up, w_down, loss_target, m_meta_tokens, m_norm_mix_pre, m_norm_mix_post, m_norm_mlp_pre, m_norm_mlp_post, m_w_in, m_ssm_a_re, m_ssm_a_im, m_ssm_log_dt, m_ssm_b_re, m_ssm_b_im, m_ssm_c_re, m_ssm_c_im, m_ssm_d, m_w_glu, m_b_glu, m_attn_sinks, m_w_o_ssm, m_w_o_attn, m_w_out, m_w_up, m_w_down, v_meta_tokens, v_norm_mix_pre, v_norm_mix_post, v_norm_mlp_pre, v_norm_mlp_post, v_w_in, v_ssm_a_re, v_ssm_a_im, v_ssm_log_dt, v_ssm_b_re, v_ssm_b_im, v_ssm_c_re, v_ssm_c_im, v_ssm_d, v_w_glu, v_b_glu, v_attn_sinks, v_w_o_ssm, v_w_o_attn, v_w_out, v_w_up, v_w_down):
    args = locals()
    w = {n: args[n] for n in WEIGHTS}
    m = {n: args["m_" + n] for n in WEIGHTS}
    v = {n: args["v_" + n] for n in WEIGHTS}
    n_layers = w_in.shape[0]
    seq = x.shape[1]
    rows = seq + BLK
    my_slot = _slot(_mesh_pos())

    assert n_layers == 2
    xfer = {n: [w[n][l].astype(XFER_DTYPE) for l in range(n_layers)] for n in BIG}
    mixer_small = ("w_glu", "w_o_ssm", "w_o_attn", "w_out")
    meta_g, w_in_g0 = _exchange_by_sequencer([meta_tokens, xfer["w_in"][0]], True, 0, "gather_in0")
    mix0_g = _exchange_by_sequencer([xfer[n][0] for n in mixer_small], True, 1, "gather_mix0")
    meta_full = meta_g.transpose(1, 0, 2).reshape(N_META, D)

    def mixer_weights(w_glu_g, w_o_ssm_g, w_o_attn_g, w_out_g):
        return dict(w_glu=w_glu_g.reshape(D_SSM, D_SSM), w_o_ssm=w_o_ssm_g.transpose(1, 0, 2).reshape(D_SSM, D),
                    w_o_attn=w_o_attn_g.reshape(D_ATTN, D), w_out=w_out_g.reshape(D, D),
                    w_o_ssm_t=w_o_ssm_g.transpose(0, 2, 1).reshape(D, D_SSM),
                    w_o_attn_t=w_o_attn_g.reshape(D_ATTN, D).T, w_out_t=w_out_g.reshape(D, D).T)

    gathered = [dict(w_in=w_in_g0, **mixer_weights(*mix0_g)), {}]

    gains = {n: w[n].reshape(n_layers, 1, D) for n in ("norm_mix_pre", "norm_mix_post", "norm_mlp_pre", "norm_mlp_post")}
    b_glu3 = b_glu.reshape(n_layers, 1, D_SSM)
    cos, sin_a, sin_b = _rope_tables(rows)

    disc, disc_vjp = jax.vjp(jax.vmap(_ssm_discretize), ssm_a_re, ssm_a_im, ssm_log_dt, ssm_b_re, ssm_b_im)
    ssm = jax.vmap(_ssm_tables)(*disc, ssm_c_re, ssm_c_im, ssm_d)

    hres = jnp.concatenate([jnp.zeros((PAD_ROWS, D), F32), meta_full, x[0]], axis=0)

    saved = []
    for l in range(n_layers):
        wl = gathered[l]
        u, gates, q, k, vv, h = _in_proj(hres, gains["norm_mix_pre"], wl["w_in"], cos, sin_a, sin_b, l,
                                         after=[ssm["b_mat"], ssm["c_mat"], ssm["t_re"], ssm["t_im"]] if l == 0 else ())
        if l == 0:
            wl["w_up"], wl["w_down"] = _exchange_by_sequencer([xfer["w_up"][0], xfer["w_down"][0]], True, 2,
                                                              "gather_mlp0", after=[h])
        y, y_ssm, carry_in = _s5_fwd(u, ssm, wl["w_glu"], b_glu3, l)
        if l == 0:
            l1_g = _exchange_by_sequencer([xfer[n][1] for n in ("w_in",) + mixer_small + ("w_up", "w_down")], True, 3,
                                          "gather_l1", after=[y, wl["w_up"]])
            gathered[1] = dict(w_in=l1_g[0], w_up=l1_g[5], w_down=l1_g[6], **mixer_weights(*l1_g[1:5]))
            last_exchange = l1_g[:1]
        y_attn = _attn_fwd(q, k, vv, attn_sinks, l)
        merged, mix, hres_mid = _merge_fwd(y_ssm, y_attn, gates, hres, wl["w_o_ssm"], wl["w_o_attn"], wl["w_out"],
                                           gains["norm_mix_post"], l)
        if l + 1 < n_layers:
            up, h2, ff, hres_out = _mlp_fwd(hres_mid, gains["norm_mlp_pre"], gains["norm_mlp_post"], wl["w_up"],
                                            wl["w_down"], l)
        else:
            target = jnp.concatenate([jnp.zeros((BLK, D), F32), loss_target[0]], axis=0)
            up, h2, ff, dhres, loss_vec = _mlp_fwd(hres_mid, gains["norm_mlp_pre"], gains["norm_mlp_post"], wl["w_up"],
                                                   wl["w_down"], l, target=target)
        saved.append(dict(hres=hres, u=u, gates=gates, h=h, q=q, k=k, v=vv, y=y, y_ssm=y_ssm,
                          carry_in=carry_in, y_attn=y_attn, merged=merged, mix=mix, hres_mid=hres_mid,
                          up=up, h2=h2, ff=ff))
        hres = hres_out

    small_grads = {}
    recv_up, recv_down, recv_mix = [None] * n_layers, [None] * n_layers, [None] * n_layers
    for l in reversed(range(n_layers)):
        s = saved[l]
        wl = gathered[l]
        dff, dup, dhm, dg_mlp_post, dg_mlp_pre = _mlp_bwd(dhres, s["ff"], s["up"], s["hres_mid"], gains["norm_mlp_pre"],
                                                          gains["norm_mlp_post"], wl["w_up"], wl["w_down"], l)
        dw_up = _matmul_tn(s["h2"], dup, f"dw_up_l{l}", dev_major_cols=COL_SHARD)
        recv_up[l] = _exchange_by_sequencer([dw_up], False, 4 + 3 * l, f"scatter_up{l}", after=last_exchange)
        dw_down = _matmul_tn(s["up"], dff, f"dw_down_l{l}", a_fn=_relu_squared).reshape(N_DEV, COL_SHARD, D)
        recv_down[l] = _exchange_by_sequencer([dw_down], False, 5 + 3 * l, f"scatter_down{l}", after=recv_up[l])
        last_exchange = recv_down[l]
        dmix, da1, da2, dgs, dga, dy_ssm, dy_attn, dg_mix_post = _merge_bwd(
            dhm, s["mix"], s["y_ssm"], s["y_attn"], s["gates"], wl["w_o_ssm"], wl["w_o_attn"], wl["w_o_ssm_t"],
            wl["w_o_attn_t"], wl["w_out_t"], gains["norm_mix_post"], l)
        dw_out = _matmul_tn(s["merged"], dmix, f"dw_out_l{l}").reshape(N_DEV, D // N_DEV, D)
        dw_o_attn = _matmul_tn(s["y_attn"], da2, f"dw_o_attn_l{l}").reshape(N_DEV, D_ATTN // N_DEV, D)
        dw_o_ssm = _matmul_tn(s["y_ssm"], da1, f"dw_o_ssm_l{l}", dev_major_cols=D // N_DEV)
        if l == 0:
            recv_out0 = _exchange_by_sequencer([dw_o_ssm, dw_o_attn, dw_out], False, 11, "scatter_out0",
                                               after=last_exchange)
            last_exchange = recv_out0[:1]
        dq, dk, dv, dk_meta, dv_meta, dsink = _attn_bwd(s["q"], s["k"], s["v"], dy_attn, attn_sinks, l)
        dqkv = _rope_bwd(dq, dk, dv, dk_meta, dv_meta, cos, sin_a, sin_b, l)
        du, dw_glu, db_glu, dd_skip, db_mat, dc_mat, dab = _s5_bwd(dy_ssm, s["y"], s["u"], s["carry_in"], ssm,
                                                                    wl["w_glu"], b_glu3, l)
        dproj = (du, dqkv, dgs, dga)
        dw_in = _dw_in(s["h"], dproj, l)
        mix_parts = [dw_in, dw_glu.astype(XFER_DTYPE).reshape(N_DEV, D_SSM // N_DEV, D_SSM), dw_o_ssm, dw_o_attn, dw_out]
        if l > 0:
            recv_mix[l] = _exchange_by_sequencer(mix_parts, False, 6 + 3 * l, f"scatter_mix{l}", after=last_exchange)
            last_exchange = recv_mix[l][:1]
        else:
            recv_mix[0] = _exchange_by_sequencer(mix_parts[:2], False, 6, "scatter_in0", after=last_exchange) + recv_out0
            last_exchange = recv_mix[0][:1]
        dhres, dg_mix_pre = _in_bwd(dproj, dhm, s["hres"], gains["norm_mix_pre"], wl["w_in"], l)

        for name, val in (("norm_mix_pre", dg_mix_pre[0]), ("norm_mix_post", dg_mix_post[0]),
                          ("norm_mlp_pre", dg_mlp_pre[0]), ("norm_mlp_post", dg_mlp_post[0]),
                          ("dab", dab), ("db_mat", db_mat), ("dc_mat", dc_mat),
                          ("ssm_d", dd_skip.reshape(N_GROUPS, GROUP_CH)), ("b_glu", db_glu[0]),
                          ("attn_sinks", dsink[:, 0])):
            small_grads.setdefault(name, [None] * n_layers)[l] = val

    grad_x = dhres[BLK:][None]
    stacked = {n: jnp.stack(v) for n, v in small_grads.items()}
    dab = stacked["dab"].reshape(n_layers, N_SB, 2, SB_STATES)
    db_mat, dc_mat = stacked["db_mat"], stacked["dc_mat"]
    b_t, c_t = jax.vmap(_block_diag_b_t), jax.vmap(_block_diag_c_t)
    (stacked["ssm_a_re"], stacked["ssm_a_im"], stacked["ssm_log_dt"], stacked["ssm_b_re"],
     stacked["ssm_b_im"]) = disc_vjp((dab[:, :, 0].reshape(n_layers, N_GROUPS, N_STATE),
                                      dab[:, :, 1].reshape(n_layers, N_GROUPS, N_STATE),
                                      b_t(db_mat[..., :SB_STATES]), b_t(db_mat[..., SB_STATES:])))
    stacked["ssm_c_re"] = c_t(dc_mat[:, :, :SB_STATES])
    stacked["ssm_c_im"] = -c_t(dc_mat[:, :, SB_STATES:])
    small_names = [n for n in SMALL if n != "meta_tokens"]
    partial_small = [dhres[PAD_ROWS:BLK]] + [stacked[n] for n in small_names] + [loss_vec[0, :1]]
    small_parts, = _exchange_by_sequencer([_pack(partial_small)], True, 10, "gather_small", after=last_exchange)

    grads, delta, new_m, new_v = {}, {}, {}, {}

    def adamw_big(names, recv0, recv1):
        for n, p0, p1 in zip(names, recv0, recv1):
            grads[n], delta[n], new_m[n], new_v[n] = _adamw_layers(p0, p1, w[n], m[n], v[n], f"adamw_{n}")

    adamw_big(("w_up", "w_down"), recv_up[0] + recv_down[0], recv_up[1] + recv_down[1])
    summed = _unpack(_sum_slots(small_parts, "sum_small_grads"), partial_small)
    loss = summed[-1][0]
    grads.update(zip(small_names, summed[1:-1]))
    grads["meta_tokens"] = lax.dynamic_slice_in_dim(summed[0], my_slot * (D // N_DEV), D // N_DEV, axis=1)
    like = [w[n] for n in SMALL]
    d_s, m_s, v_s = _adamw_packed(_pack([grads[n] for n in SMALL]), _pack(like), _pack([m[n] for n in SMALL]),
                                  _pack([v[n] for n in SMALL]), "adamw_small")
    adamw_big(("w_in",) + mixer_small, recv_mix[0], recv_mix[1])
    for n, dd, mm, vs in zip(SMALL, _unpack(d_s, like), _unpack(m_s, like), _unpack(v_s, like)):
        delta[n], new_m[n], new_v[n] = dd, mm, vs

    return (loss, grad_x, *[grads[n] for n in WEIGHTS], *[delta[n] for n in WEIGHTS],
            *[new_m[n] for n in WEIGHTS], *[new_v[n] for n in WEIGHTS])
```

```python
import functools
import math

import jax
import jax.numpy as jnp
from jax import lax
from jax.experimental import pallas as pl
from jax.experimental.pallas import tpu as pltpu
from jax.experimental.pallas import tpu_sc as plsc

F32 = jnp.float32
MXU_DTYPE = jnp.bfloat16
XFER_DTYPE = MXU_DTYPE
_pcall = pl.pallas_call
SDS = jax.ShapeDtypeStruct

D = 1024
D_SSM = 512
D_ATTN = 1024
D_KV = 256
D_FF = 4096
D_IN = 4096
HEAD_DIM = 64
N_Q_HEADS = 16
N_KV_HEADS = 4
Q_PER_KV = 4
N_META = 16
BLK = 128
PAD_ROWS = BLK - N_META
N_GROUPS = 32
N_STATE = 64
GROUP_CH = 16
N_SB = 4
SB_STATES = 512
ROPE_THETA = 10000.0
ATTN_SCALE = HEAD_DIM ** -0.5
NEG_INF = -1e30
RMS_EPS = 1e-6
N_DEV = 8
COL_SHARD = 512

ADAM_LR = 0.001
ADAM_B1 = 0.9
ADAM_B2 = 0.999
ADAM_EPS = 1e-08
ADAM_WD = 0.01
ADAM_STEP = 10

VMEM_LIMIT = 56 * 1024 * 1024

_NT = (((1,), (1,)), ((), ()))
_TN = (((0,), (0,)), ((), ()))


def _cparams(*sem):
    return pltpu.CompilerParams(dimension_semantics=tuple(sem) if sem else None,
                                vmem_limit_bytes=VMEM_LIMIT)


def _row_tile(rows, cap=640):
    for t in (1664, 640, 512, 320, 256, 128):
        if t <= cap and rows % t == 0:
            return t
    raise ValueError(f"unsupported row count {rows}")


def _dot(a, b):
    return jnp.dot(a, b, preferred_element_type=F32)


def _dot_nt(a, b):
    return lax.dot_general(a, b, _NT, preferred_element_type=F32)


def _dot_tn(a, b):
    return lax.dot_general(a, b, _TN, preferred_element_type=F32)


def _sigmoid(x):
    return 1.0 / (1.0 + jnp.exp(-x))


_GELU_C = math.sqrt(2.0 / math.pi)


def _gelu_parts(y):
    t = jnp.tanh(_GELU_C * (y + 0.044715 * (y * y * y)))
    return 0.5 * y * (1.0 + t), t


def _gelu_grad(y, t):
    return 0.5 * (1.0 + t) + 0.5 * y * (1.0 - t * t) * (_GELU_C * (1.0 + 0.134145 * (y * y)))


def _rms_fwd(x, gain):
    r = lax.rsqrt(jnp.mean(x * x, axis=-1, keepdims=True) + RMS_EPS)
    return (x * r) * gain


def _rms_bwd(x, gain, dout):
    r = lax.rsqrt(jnp.mean(x * x, axis=-1, keepdims=True) + RMS_EPS)
    xh = x * r
    dxh = dout * gain
    dx = r * (dxh - xh * jnp.mean(dxh * xh, axis=-1, keepdims=True))
    return dx, jnp.sum(dout * xh, axis=0, keepdims=True)


def _mesh_pos():
    return lax.axis_index("x"), lax.axis_index("y"), lax.axis_index("c")


def _peer(pos, d):
    x, y, c = pos
    return (1 - x if d & 4 else x, 1 - y if d & 2 else y, 1 - c if d & 1 else c)


def _slot(pos):
    return 4 * pos[0] + 2 * pos[1] + pos[2]


def _exchange_copy(gather, src_ref, land_ref, sems, k, d, me, send_side):
    peer = _peer(me, d)
    sender = me if send_side else peer
    src = src_ref if gather else src_ref.at[_slot(peer) if send_side else _slot(me)]
    return pltpu.make_async_remote_copy(
        src_ref=src, dst_ref=land_ref.at[_slot(sender)],
        send_sem=sems[0].at[k * (N_DEV - 1) + d - 1], recv_sem=sems[1].at[k * (N_DEV - 1) + d - 1],
        device_id=peer, device_id_type=pl.DeviceIdType.MESH)


def _exchange_by_sequencer(srcs, gather, collective_id, name, after=()):
    n = len(srcs)
    flags = [gather] * n if isinstance(gather, bool) else list(gather)
    land_types = [SDS(((N_DEV,) + s.shape) if g else s.shape, s.dtype) for s, g in zip(srcs, flags)]

    def body(*refs):
        src_refs = refs[:n]
        land_refs = refs[n + len(after):2 * n + len(after)]
        sems = refs[2 * n + len(after):2 * n + len(after) + 2]
        local_sems = refs[2 * n + len(after) + 2]
        me = _mesh_pos()
        barrier = pltpu.get_barrier_semaphore()
        for d in range(1, N_DEV):
            pl.semaphore_signal(barrier, inc=1, device_id=_peer(me, d), device_id_type=pl.DeviceIdType.MESH)
        pl.semaphore_wait(barrier, N_DEV - 1)
        own = [pltpu.make_async_copy(src_refs[k] if flags[k] else src_refs[k].at[_slot(me)],
                                     land_refs[k].at[_slot(me)], local_sems.at[k]) for k in range(n)]
        for cp in own:
            cp.start()
        for k in range(n):
            for d in range(1, N_DEV):
                _exchange_copy(flags[k], src_refs[k], land_refs[k], sems, k, d, me, True).start()
        for cp in own:
            cp.wait()
        for k in range(n):
            for d in range(1, N_DEV):
                _exchange_copy(flags[k], src_refs[k], land_refs[k], sems, k, d, me, True).wait_send()
        for k in range(n):
            for d in range(1, N_DEV):
                _exchange_copy(flags[k], src_refs[k], land_refs[k], sems, k, d, me, False).wait_recv()

    sem_type = pltpu.SemaphoreType.DMA((n * (N_DEV - 1),))
    return pl.kernel(
        body, out_type=land_types, mesh=plsc.ScalarSubcoreMesh(axis_name="sequencer", num_cores=1), name=name,
        scratch_types=(sem_type, sem_type, pltpu.SemaphoreType.DMA((n,))),
        compiler_params=pltpu.CompilerParams(collective_id=collective_id),
    )(*srcs, *after)


def _load_resident(w_hbm, w_scr, sems, first_step):
    @pl.when(first_step)
    def _():
        copies = [pltpu.make_async_copy(w_hbm.at[s], w_scr.at[s], sems.at[s]) for s in range(N_DEV)]
        for cp in copies:
            cp.start()
        for cp in copies:
            cp.wait()


def _load_resident_transposed(w_hbm, w_scr, stage, sems, first_step):
    @pl.when(first_step)
    def _():
        copies = [pltpu.make_async_copy(w_hbm.at[s], stage.at[s % 2], sems.at[s % 2]) for s in range(N_DEV)]
        copies[0].start()
        for s in range(N_DEV):
            if s + 1 < N_DEV:
                copies[s + 1].start()
            copies[s].wait()
            w_scr[s] = stage[s % 2].T


def _rope_lanes(t, cos, sin_a, sin_b):
    return t * cos + pltpu.roll(t, 96, 1) * sin_a + pltpu.roll(t, 32, 1) * sin_b


def _in_proj(hres, gain3, w_in_g, cos, sin_a, sin_b, layer, after=()):
    rows = hres.shape[0]
    tm = _row_tile(rows, 320)

    def body(x_ref, g_ref, w_hbm, c_ref, a_ref, b_ref, *refs):
        u_ref, gate_ref, q_ref, k_ref, v_ref, h_ref, w_scr, w_sem = refs[len(after):]
        _load_resident(w_hbm, w_scr, w_sem, pl.program_id(0) == 0)
        hn = _rms_fwd(x_ref[...], g_ref[...]).astype(MXU_DTYPE)
        h_ref[...] = hn
        c, a, b = c_ref[...], a_ref[...], b_ref[...]
        u_ref[...] = _dot(hn, w_scr[0])
        for shard in (1, 2):
            res = _dot(hn, w_scr[shard])
            for t in range(4):
                lanes = slice(t * 128, (t + 1) * 128)
                out = slice((shard - 1) * COL_SHARD + t * 128, (shard - 1) * COL_SHARD + (t + 1) * 128)
                q_ref[:, out] = (_rope_lanes(res[:, lanes], c, a, b) * ATTN_SCALE).astype(MXU_DTYPE)
        res = _dot(hn, w_scr[3])
        for t in range(2):
            lanes = slice(t * 128, (t + 1) * 128)
            k_ref[:, lanes] = _rope_lanes(res[:, lanes], c, a, b).astype(MXU_DTYPE)
        v_ref[...] = res[:, D_KV:].astype(MXU_DTYPE)
        for shard in range(4, N_DEV):
            gate_ref[:, (shard - 4) * COL_SHARD:(shard - 3) * COL_SHARD] = _dot(hn, w_scr[shard])

    tab = pl.BlockSpec((tm, 128), lambda i: (i, 0))
    kv = pl.BlockSpec((tm, D_KV), lambda i: (i, 0))
    row_d = pl.BlockSpec((tm, D), lambda i: (i, 0))
    return _pcall(
        body, name=f"in_proj_l{layer}", grid=(rows // tm,),
        in_specs=[row_d, pl.BlockSpec((None, 1, D), lambda i: (layer, 0, 0)),
                  pl.BlockSpec(memory_space=pl.ANY), tab, tab, tab] + [pl.BlockSpec(memory_space=pl.ANY)] * len(after),
        out_specs=[pl.BlockSpec((tm, D_SSM), lambda i: (i, 0)), pl.BlockSpec((tm, 2 * D), lambda i: (i, 0)),
                   row_d, kv, kv, row_d],
        out_shape=[SDS((rows, D_SSM), F32), SDS((rows, 2 * D), F32), SDS((rows, D_ATTN), MXU_DTYPE),
                   SDS((rows, D_KV), MXU_DTYPE), SDS((rows, D_KV), MXU_DTYPE), SDS((rows, D), MXU_DTYPE)],
        scratch_shapes=[pltpu.VMEM((N_DEV, D, COL_SHARD), MXU_DTYPE), pltpu.SemaphoreType.DMA((N_DEV,))],
        compiler_params=_cparams("arbitrary"),
    )(hres, gain3, w_in_g, cos, sin_a, sin_b, *after)


SCAN_TILE = 8


def _scan_tiles(x_ref, out_ref, tre_ref, tim_ref, sb, t_r, t_i, reverse, prev_ref=None):
    base = 4 if reverse else 0
    n_tiles = BLK // SCAN_TILE
    row = lax.broadcasted_iota(jnp.int32, (SCAN_TILE, SB_STATES), 0)
    for j in (range(n_tiles - 1, -1, -1) if reverse else range(n_tiles)):
        rows = slice(SCAN_TILE * j, SCAN_TILE * (j + 1))
        xr = x_ref[rows, :SB_STATES]
        xi = x_ref[rows, SB_STATES:]
        for k in range(3):
            shift = SCAN_TILE - (1 << k) if reverse else (1 << k)
            rr = pltpu.roll(xr, shift, 0)
            ri = pltpu.roll(xi, shift, 0)
            ar = tre_ref[sb, base + k]
            ai = tim_ref[sb, base + k]
            xr, xi = xr + (ar * rr - ai * ri), xi + (ar * ri + ai * rr)
        pr = tre_ref[sb, base + 3]
        pi = tim_ref[sb, base + 3]
        xr, xi = xr + (pr * t_r - pi * t_i), xi + (pr * t_i + pi * t_r)
        out_ref[rows, :SB_STATES] = xr
        out_ref[rows, SB_STATES:] = xi
        if prev_ref is not None:
            prev_ref[rows, :SB_STATES] = jnp.where(row == 0, t_r, pltpu.roll(xr, 1, 0))
            prev_ref[rows, SB_STATES:] = jnp.where(row == 0, t_i, pltpu.roll(xi, 1, 0))
        edge = slice(0, 1) if reverse else slice(SCAN_TILE - 1, SCAN_TILE)
        t_r, t_i = xr[edge], xi[edge]
    return t_r, t_i


def _s5_fwd(u, ssm, w_glu, b_glu3, layer):
    rows = u.shape[0]
    n_chunks = rows // BLK
    b_mat, c_mat, t_re, t_im, d_skip = (ssm[k] for k in ("b_mat", "c_mat", "t_re", "t_im", "d_skip"))

    def body(u_ref, bm_ref, cm_ref, tre_ref, tim_ref, d_ref, wg_ref, bg_ref,
             y_ref, ys_ref, cin_ref, carry, bu_scr, s_scr):
        @pl.when(pl.program_id(0) == 0)
        def _():
            carry[...] = jnp.zeros_like(carry)

        cin_ref[...] = carry[...]
        u = u_ref[...]
        for sb in range(N_SB):
            cols = slice(sb * 128, (sb + 1) * 128)
            u_sb = u[:, cols]
            bu_scr[sb] = _dot(u_sb.astype(MXU_DTYPE), bm_ref[sb])
            t_r, t_i = _scan_tiles(bu_scr.at[sb], s_scr.at[sb], tre_ref, tim_ref, sb,
                                   carry[2 * sb:2 * sb + 1, :], carry[2 * sb + 1:2 * sb + 2, :], False)
            carry[2 * sb:2 * sb + 1, :] = t_r
            carry[2 * sb + 1:2 * sb + 2, :] = t_i
            y_ref[:, cols] = _dot(s_scr[sb].astype(MXU_DTYPE), cm_ref[sb]) + d_ref[:, cols] * u_sb
        z, _ = _gelu_parts(y_ref[...])
        gl = _dot(z.astype(MXU_DTYPE), wg_ref[...]) + bg_ref[...]
        ys_ref[...] = (z * _sigmoid(gl)).astype(MXU_DTYPE)

    full = lambda shape: pl.BlockSpec(shape, lambda j: (0,) * len(shape))
    of_layer = lambda shape: pl.BlockSpec((None,) + shape, lambda j: (layer,) + (0,) * len(shape))
    return _pcall(
        body, name=f"s5_fwd_l{layer}", grid=(n_chunks,),
        in_specs=[pl.BlockSpec((BLK, D_SSM), lambda j: (j, 0)),
                  of_layer((N_SB, 128, 2 * SB_STATES)), of_layer((N_SB, 2 * SB_STATES, 128)),
                  of_layer((N_SB, 8, SCAN_TILE, SB_STATES)), of_layer((N_SB, 8, SCAN_TILE, SB_STATES)),
                  of_layer((1, D_SSM)), full((D_SSM, D_SSM)),
                  pl.BlockSpec((None, 1, D_SSM), lambda j: (layer, 0, 0))],
        out_specs=[pl.BlockSpec((BLK, D_SSM), lambda j: (j, 0)), pl.BlockSpec((BLK, D_SSM), lambda j: (j, 0)),
                   pl.BlockSpec((None, 8, SB_STATES), lambda j: (j, 0, 0))],
        out_shape=[SDS((rows, D_SSM), F32), SDS((rows, D_SSM), MXU_DTYPE), SDS((n_chunks, 8, SB_STATES), F32)],
        scratch_shapes=[pltpu.VMEM((8, SB_STATES), F32), pltpu.VMEM((N_SB, BLK, 2 * SB_STATES), F32),
                        pltpu.VMEM((N_SB, BLK, 2 * SB_STATES), F32)],
        compiler_params=_cparams("arbitrary"),
    )(u, b_mat, c_mat, t_re, t_im, d_skip, w_glu, b_glu3)


def _attn_mask(i):
    row = lax.broadcasted_iota(jnp.int32, (BLK, 3 * BLK), 0) + i * BLK
    col = lax.broadcasted_iota(jnp.int32, (BLK, 3 * BLK), 1)
    seg = jnp.right_shift(col, 7)
    c = jnp.bitwise_and(col, BLK - 1)
    kidx = c + (i + seg - 2) * BLK
    ok_meta = (seg == 0) & (c >= PAD_ROWS) & (row - c >= BLK)
    ok_win = (seg > 0) & (kidx >= PAD_ROWS) & (kidx <= row) & (row - kidx < BLK)
    return jnp.where(ok_meta | ok_win, 0.0, NEG_INF)


def _head_lanes(h):
    return slice(h * HEAD_DIM, (h + 1) * HEAD_DIM)


def _group_rows(ref, kvh):
    return jnp.concatenate([ref[:, _head_lanes(kvh * Q_PER_KV + g)] for g in range(Q_PER_KV)], axis=0)


def _group_bias(bias, sink_ref, layer, kvh):
    first_col = lax.broadcasted_iota(jnp.int32, (BLK, BLK), 1) == 0
    slabs = []
    for g in range(Q_PER_KV):
        first = jnp.where(first_col, sink_ref[layer, kvh * Q_PER_KV + g], bias[:, :BLK])
        slabs.append(jnp.concatenate([first, bias[:, BLK:]], axis=1))
    return jnp.concatenate(slabs, axis=0)


def _attn_probs(q4, k3, bias4):
    s = _dot_nt(q4, k3) + bias4
    e = jnp.exp(s - jnp.max(s, axis=-1, keepdims=True))
    return e * (1.0 / jnp.sum(e, axis=-1, keepdims=True))


def _attn_fwd(q, k, v, sinks, layer):
    rows = q.shape[0]
    n_blk = rows // BLK

    def body(sink_ref, q_ref, km_ref, kp_ref, kc_ref, vm_ref, vp_ref, vc_ref, o_ref):
        bias = _attn_mask(pl.program_id(0))
        for kvh in range(N_KV_HEADS):
            lanes = _head_lanes(kvh)
            k3 = jnp.concatenate([km_ref[:, lanes], kp_ref[:, lanes], kc_ref[:, lanes]], axis=0)
            v3 = jnp.concatenate([vm_ref[:, lanes], vp_ref[:, lanes], vc_ref[:, lanes]], axis=0)
            p = _attn_probs(_group_rows(q_ref, kvh), k3, _group_bias(bias, sink_ref, layer, kvh))
            o4 = _dot(p.astype(MXU_DTYPE), v3).astype(MXU_DTYPE)
            for g in range(Q_PER_KV):
                o_ref[:, _head_lanes(kvh * Q_PER_KV + g)] = o4[g * BLK:(g + 1) * BLK]

    kv_meta = pl.BlockSpec((BLK, D_KV), lambda i: (0, 0))
    kv_prev = pl.BlockSpec((BLK, D_KV), lambda i: (jnp.maximum(i - 1, 0), 0))
    kv_cur = pl.BlockSpec((BLK, D_KV), lambda i: (i, 0))
    return _pcall(
        body, name=f"attn_fwd_l{layer}", grid=(n_blk,),
        in_specs=[pl.BlockSpec(memory_space=pltpu.SMEM),
                  pl.BlockSpec((BLK, D_ATTN), lambda i: (i, 0)),
                  kv_meta, kv_prev, kv_cur, kv_meta, kv_prev, kv_cur],
        out_specs=pl.BlockSpec((BLK, D_ATTN), lambda i: (i, 0)),
        out_shape=SDS((rows, D_ATTN), MXU_DTYPE),
        compiler_params=_cparams("parallel"),
    )(sinks, q, k, k, k, v, v, v)


def _merge_fwd(y_ssm, y_attn, gates, hres, w_o_ssm, w_o_attn, w_out, gain3, layer):
    rows = hres.shape[0]
    tm = _row_tile(rows, 320)

    def body(ys_ref, ya_ref, gs_ref, ga_ref, x_ref, wos_ref, woa_ref, wout_ref, g_ref,
             mg_ref, mix_ref, out_ref):
        a1 = _dot(ys_ref[...], wos_ref[...])
        a2 = _dot(ya_ref[...], woa_ref[...])
        merged = (_sigmoid(gs_ref[...]) * a1 + _sigmoid(ga_ref[...]) * a2).astype(MXU_DTYPE)
        mg_ref[...] = merged
        mix = _dot(merged, wout_ref[...])
        mix_ref[...] = mix
        out_ref[...] = x_ref[...] + _rms_fwd(mix, g_ref[...])

    row_d = pl.BlockSpec((tm, D), lambda i: (i, 0))
    full = lambda shape: pl.BlockSpec(shape, lambda i: (0,) * len(shape))
    return _pcall(
        body, name=f"merge_fwd_l{layer}", grid=(rows // tm,),
        in_specs=[pl.BlockSpec((tm, D_SSM), lambda i: (i, 0)), row_d,
                  row_d, pl.BlockSpec((tm, D), lambda i: (i, 1)), row_d,
                  full((D_SSM, D)), full((D_ATTN, D)), full((D, D)),
                  pl.BlockSpec((None, 1, D), lambda i: (layer, 0, 0))],
        out_specs=[row_d, row_d, row_d],
        out_shape=[SDS((rows, D), MXU_DTYPE), SDS((rows, D), F32), SDS((rows, D), F32)],
        compiler_params=_cparams("parallel"),
    )(y_ssm, y_attn, gates, gates, hres, w_o_ssm, w_o_attn, w_out, gain3)


def _mlp_fwd(hres, gain_pre3, gain_post3, w_up_g, w_down_g, layer, target=None):
    rows = hres.shape[0]
    tm = _row_tile(rows, 320)

    def body(x_ref, gp_ref, gq_ref, wu_hbm, wd_hbm, *refs):
        if target is None:
            up_ref, h_ref, ff_ref, out_ref, act_scr, wu_scr, wd_scr, wu_sem, wd_sem = refs
        else:
            t_ref, up_ref, h_ref, ff_ref, out_ref, loss_ref, act_scr, wu_scr, wd_scr, wu_sem, wd_sem = refs
        first = pl.program_id(0) == 0
        _load_resident(wu_hbm, wu_scr, wu_sem, first)
        _load_resident(wd_hbm, wd_scr, wd_sem, first)
        hn = _rms_fwd(x_ref[...], gp_ref[...]).astype(MXU_DTYPE)
        h_ref[...] = hn
        for kf in range(N_DEV):
            cols = slice(kf * COL_SHARD, (kf + 1) * COL_SHARD)
            up = _dot(hn, wu_scr[kf])
            up_ref[:, cols] = up.astype(MXU_DTYPE)
            r = jnp.maximum(up, 0.0)
            act_scr[:, cols] = (r * r).astype(MXU_DTYPE)
        ff = _dot(act_scr[...], wd_scr[...].reshape(D_FF, D))
        ff_ref[...] = ff
        out = x_ref[...] + _rms_fwd(ff, gq_ref[...])
        if target is None:
            out_ref[...] = out
        else:
            @pl.when(first)
            def _():
                loss_ref[...] = jnp.zeros_like(loss_ref)

            row = lax.broadcasted_iota(jnp.int32, (tm, D), 0) + pl.program_id(0) * tm
            err = jnp.where(row >= BLK, out - t_ref[...], 0.0)
            out_ref[...] = err * (1.0 / D)
            loss_ref[...] += jnp.sum(err * err) * (0.5 / D)

    row_d = pl.BlockSpec((tm, D), lambda i: (i, 0))
    gain = pl.BlockSpec((None, 1, D), lambda i: (layer, 0, 0))
    with_loss = target is not None
    return _pcall(
        body, name=f"mlp_fwd_l{layer}", grid=(rows // tm,),
        in_specs=[row_d, gain, gain, pl.BlockSpec(memory_space=pl.ANY), pl.BlockSpec(memory_space=pl.ANY)]
        + [row_d] * with_loss,
        out_specs=[pl.BlockSpec((tm, D_FF), lambda i: (i, 0)), row_d, row_d, row_d]
        + [pl.BlockSpec((1, 128), lambda i: (0, 0))] * with_loss,
        out_shape=[SDS((rows, D_FF), MXU_DTYPE), SDS((rows, D), MXU_DTYPE), SDS((rows, D), F32), SDS((rows, D), F32)]
        + [SDS((1, 128), F32)] * with_loss,
        scratch_shapes=[pltpu.VMEM((tm, D_FF), MXU_DTYPE),
                        pltpu.VMEM((N_DEV, D, COL_SHARD), MXU_DTYPE), pltpu.VMEM((N_DEV, COL_SHARD, D), MXU_DTYPE),
                        pltpu.SemaphoreType.DMA((N_DEV,)), pltpu.SemaphoreType.DMA((N_DEV,))],
        compiler_params=_cparams("arbitrary"),
    )(hres, gain_pre3, gain_post3, w_up_g, w_down_g, *([target] if with_loss else []))


def _relu_squared(up):
    r = jnp.maximum(up.astype(F32), 0.0)
    return (r * r).astype(MXU_DTYPE)


def _matmul_tn(a, b, name, dev_major_cols=None, a_fn=None):
    rows, ka = a.shape
    n = b.shape[1]
    ta = min(ka, 1024)
    tn = 1024 if n % 1024 == 0 else 512
    tr = _row_tile(rows, 1664)
    n_r = rows // tr

    def body(a_ref, b_ref, o_ref, acc):
        r = pl.program_id(2)

        @pl.when(r == 0)
        def _():
            acc[...] = jnp.zeros_like(acc)

        a_blk = a_ref[...] if a_fn is None else a_fn(a_ref[...])
        acc[...] += _dot_tn(a_blk, b_ref[...])

        @pl.when(r == n_r - 1)
        def _():
            if dev_major_cols is None:
                o_ref[...] = acc[...].astype(XFER_DTYPE)
            else:
                for s in range(tn // dev_major_cols):
                    o_ref[s] = acc[:, s * dev_major_cols:(s + 1) * dev_major_cols].astype(XFER_DTYPE)

    if dev_major_cols is None:
        out_spec = pl.BlockSpec((ta, tn), lambda i, j, r: (i, j))
        out_shape = SDS((ka, n), XFER_DTYPE)
    else:
        w = dev_major_cols
        out_spec = pl.BlockSpec((tn // w, ta, w), lambda i, j, r: (j, i, 0))
        out_shape = SDS((n // w, ka, w), XFER_DTYPE)
    return _pcall(
        body, name=name, grid=(ka // ta, n // tn, n_r),
        in_specs=[pl.BlockSpec((tr, ta), lambda i, j, r: (r, i)), pl.BlockSpec((tr, tn), lambda i, j, r: (r, j))],
        out_specs=out_spec, out_shape=out_shape,
        scratch_shapes=[pltpu.VMEM((ta, tn), F32)],
        compiler_params=_cparams("parallel", "parallel", "arbitrary"),
    )(a, b)


def _dw_in(h, dproj_pieces, layer):
    rows = h.shape[0]
    tr = _row_tile(rows, 1664)
    n_r = rows // tr

    def body(h_ref, *refs):
        piece_refs, (o_ref, acc) = refs[:len(DPROJ_PIECES)], refs[len(DPROJ_PIECES):]
        j = pl.program_id(0)
        r = pl.program_id(1)

        @pl.when(r == 0)
        def _():
            acc[...] = jnp.zeros_like(acc)

        for piece_ref, (first, count) in zip(piece_refs, DPROJ_PIECES):
            @pl.when((j >= first) & (j < first + count))
            def _():
                acc[...] += _dot_tn(h_ref[...], piece_ref[...])

        @pl.when(r == n_r - 1)
        def _():
            o_ref[...] = acc[...].astype(XFER_DTYPE)

    def piece_spec(first, count):
        def index(j, r):
            mine = (j >= first) & (j < first + count)
            return jnp.where(mine, r, 0), jnp.clip(j - first, 0, count - 1)
        return pl.BlockSpec((tr, COL_SHARD), index)

    return _pcall(
        body, name=f"dw_in_l{layer}", grid=(N_DEV, n_r),
        in_specs=[pl.BlockSpec((tr, D), lambda j, r: (r, 0))] + [piece_spec(*p) for p in DPROJ_PIECES],
        out_specs=pl.BlockSpec((None, D, COL_SHARD), lambda j, r: (j, 0, 0)),
        out_shape=SDS((N_DEV, D, COL_SHARD), XFER_DTYPE),
        scratch_shapes=[pltpu.VMEM((D, COL_SHARD), F32)],
        compiler_params=_cparams("arbitrary", "arbitrary"),
    )(h, *dproj_pieces)


def _mlp_bwd(dout, ff, up, hres_mid, gain_pre3, gain_post3, w_up_g, w_down_g, layer):
    rows = dout.shape[0]
    tm = _row_tile(rows, 320)

    def body(do_ref, ff_ref, up_ref, x_ref, gp_ref, gq_ref, wu_hbm, wd_hbm,
             dff_ref, dup_ref, dx_ref, dgq_ref, dgp_ref, wut_scr, wdt_scr, wu_stage, wd_stage, wu_sem, wd_sem):
        i = pl.program_id(0)
        _load_resident_transposed(wu_hbm, wut_scr, wu_stage, wu_sem, i == 0)
        _load_resident_transposed(wd_hbm, wdt_scr, wd_stage, wd_sem, i == 0)

        @pl.when(i == 0)
        def _():
            dgq_ref[...] = jnp.zeros_like(dgq_ref)
            dgp_ref[...] = jnp.zeros_like(dgp_ref)

        dff, dg = _rms_bwd(ff_ref[...], gq_ref[...], do_ref[...])
        dgq_ref[...] += dg
        dffb = dff.astype(MXU_DTYPE)
        dff_ref[...] = dffb
        for kf in range(N_DEV):
            cols = slice(kf * COL_SHARD, (kf + 1) * COL_SHARD)
            dact = _dot(dffb, wdt_scr[kf])
            dup_ref[:, cols] = (dact * (2.0 * jnp.maximum(up_ref[:, cols].astype(F32), 0.0))).astype(MXU_DTYPE)
        dh = _dot(dup_ref[...], wut_scr[...].reshape(D_FF, D))
        dx, dg = _rms_bwd(x_ref[...], gp_ref[...], dh)
        dgp_ref[...] += dg
        dx_ref[...] = do_ref[...] + dx

    row_d = pl.BlockSpec((tm, D), lambda i: (i, 0))
    row_ff = pl.BlockSpec((tm, D_FF), lambda i: (i, 0))
    gain = pl.BlockSpec((None, 1, D), lambda i: (layer, 0, 0))
    dgain = pl.BlockSpec((1, D), lambda i: (0, 0))
    return _pcall(
        body, name=f"mlp_bwd_l{layer}", grid=(rows // tm,),
        in_specs=[row_d, row_d, row_ff, row_d, gain, gain,
                  pl.BlockSpec(memory_space=pl.ANY), pl.BlockSpec(memory_space=pl.ANY)],
        out_specs=[row_d, row_ff, row_d, dgain, dgain],
        out_shape=[SDS((rows, D), MXU_DTYPE), SDS((rows, D_FF), MXU_DTYPE), SDS((rows, D), F32),
                   SDS((1, D), F32), SDS((1, D), F32)],
        scratch_shapes=[pltpu.VMEM((N_DEV, COL_SHARD, D), MXU_DTYPE), pltpu.VMEM((N_DEV, D, COL_SHARD), MXU_DTYPE),
                        pltpu.VMEM((2, D, COL_SHARD), MXU_DTYPE), pltpu.VMEM((2, COL_SHARD, D), MXU_DTYPE),
                        pltpu.SemaphoreType.DMA((2,)), pltpu.SemaphoreType.DMA((2,))],
        compiler_params=_cparams("arbitrary"),
    )(dout, ff, up, hres_mid, gain_pre3, gain_post3, w_up_g, w_down_g)


def _merge_bwd(dhm, mix, y_ssm, y_attn, gates, w_o_ssm, w_o_attn, w_o_ssm_t, w_o_attn_t, w_out_t, gain3, layer):
    rows = dhm.shape[0]
    tm = _row_tile(rows, 320)

    def body(dh_ref, mix_ref, ys_ref, ya_ref, gs_ref, ga_ref, wos_ref, woa_ref, wost_ref, woat_ref, woutt_ref, g_ref,
             dmix_ref, da1_ref, da2_ref, dgs_ref, dga_ref, dys_ref, dya_ref, dg_ref):
        @pl.when(pl.program_id(0) == 0)
        def _():
            dg_ref[...] = jnp.zeros_like(dg_ref)

        dmix, dg = _rms_bwd(mix_ref[...], g_ref[...], dh_ref[...])
        dg_ref[...] += dg
        dmixb = dmix.astype(MXU_DTYPE)
        dmix_ref[...] = dmixb
        dmerged = _dot(dmixb, woutt_ref[...])
        sg_s = _sigmoid(gs_ref[...])
        sg_a = _sigmoid(ga_ref[...])
        da1 = (dmerged * sg_s).astype(MXU_DTYPE)
        da2 = (dmerged * sg_a).astype(MXU_DTYPE)
        da1_ref[...] = da1
        da2_ref[...] = da2
        a1 = _dot(ys_ref[...], wos_ref[...])
        a2 = _dot(ya_ref[...], woa_ref[...])
        dgs_ref[...] = (dmerged * a1 * (sg_s * (1.0 - sg_s))).astype(MXU_DTYPE)
        dga_ref[...] = (dmerged * a2 * (sg_a * (1.0 - sg_a))).astype(MXU_DTYPE)
        dys_ref[...] = _dot(da1, wost_ref[...])
        dya_ref[...] = _dot(da2, woat_ref[...])

    row_d = pl.BlockSpec((tm, D), lambda i: (i, 0))
    full = lambda shape: pl.BlockSpec(shape, lambda i: (0,) * len(shape))
    return _pcall(
        body, name=f"merge_bwd_l{layer}", grid=(rows // tm,),
        in_specs=[row_d, row_d, pl.BlockSpec((tm, D_SSM), lambda i: (i, 0)), row_d,
                  row_d, pl.BlockSpec((tm, D), lambda i: (i, 1)),
                  full((D_SSM, D)), full((D_ATTN, D)), full((D, D_SSM)), full((D, D_ATTN)), full((D, D)),
                  pl.BlockSpec((None, 1, D), lambda i: (layer, 0, 0))],
        out_specs=[row_d, row_d, row_d, row_d, row_d, pl.BlockSpec((tm, D_SSM), lambda i: (i, 0)), row_d,
                   pl.BlockSpec((1, D), lambda i: (0, 0))],
        out_shape=[SDS((rows, D), MXU_DTYPE)] * 5 + [SDS((rows, D_SSM), F32), SDS((rows, D_ATTN), F32),
                                                      SDS((1, D), F32)],
        compiler_params=_cparams("arbitrary"),
    )(dhm, mix, y_ssm, y_attn, gates, gates, w_o_ssm, w_o_attn, w_o_ssm_t, w_o_attn_t, w_out_t, gain3)


def _attn_bwd(q, k, v, d_out, sinks, cos, sin_a, sin_b, layer):
    rows = q.shape[0]
    n_blk = rows // BLK
    last = n_blk - 1

    def body(sink_ref, q_ref, km_ref, kp_ref, kc_ref, vm_ref, vp_ref, vc_ref, do_ref, c_ref, a_ref, b_ref,
             dqb_ref, dk_ref, dv_ref, dkm_ref, dvm_ref, ds_ref, dk_carry, dv_carry, dq_ref):
        i = pl.program_id(0)

        @pl.when(i == 0)
        def _():
            dkm_ref[...] = jnp.zeros_like(dkm_ref)
            dvm_ref[...] = jnp.zeros_like(dvm_ref)
            ds_ref[...] = jnp.zeros_like(ds_ref)
            dk_carry[...] = jnp.zeros_like(dk_carry)
            dv_carry[...] = jnp.zeros_like(dv_carry)

        @pl.when(i <= last)
        def _():
            bias = _attn_mask(i)
            for kvh in range(N_KV_HEADS):
                lanes = _head_lanes(kvh)
                k3 = jnp.concatenate([km_ref[:, lanes], kp_ref[:, lanes], kc_ref[:, lanes]], axis=0)
                v3 = jnp.concatenate([vm_ref[:, lanes], vp_ref[:, lanes], vc_ref[:, lanes]], axis=0)
                q4 = _group_rows(q_ref, kvh)
                do4 = _group_rows(do_ref, kvh).astype(MXU_DTYPE)
                p = _attn_probs(q4, k3, _group_bias(bias, sink_ref, layer, kvh))
                dp = _dot_nt(do4, v3)
                dsf = p * (dp - jnp.sum(dp * p, axis=-1, keepdims=True))
                dsc = dsf.astype(MXU_DTYPE)
                dv3 = _dot_tn(p.astype(MXU_DTYPE), do4)
                dk3 = _dot_tn(dsc, q4)
                dq4 = _dot(dsc, k3)
                for g in range(Q_PER_KV):
                    h = kvh * Q_PER_KV + g
                    dq_ref[:, _head_lanes(h)] = dq4[g * BLK:(g + 1) * BLK]
                    ds_ref[h:h + 1, :] += jnp.sum(dsf[g * BLK:(g + 1) * BLK, 0:BLK], axis=0, keepdims=True)
                dkm_ref[:, lanes] += dk3[0:BLK]
                dvm_ref[:, lanes] += dv3[0:BLK]
                dk_ref[:, lanes] = dk_carry[:, lanes] + dk3[BLK:2 * BLK]
                dv_ref[:, lanes] = dv_carry[:, lanes] + dv3[BLK:2 * BLK]
                dk_carry[:, lanes] = dk3[2 * BLK:3 * BLK]
                dv_carry[:, lanes] = dv3[2 * BLK:3 * BLK]
            c, a, b = c_ref[...], -a_ref[...], -b_ref[...]
            for t in range(D_ATTN // 128):
                lanes = slice(t * 128, (t + 1) * 128)
                dqb_ref[:, lanes] = (_rope_lanes(dq_ref[:, lanes], c, a, b) * ATTN_SCALE).astype(MXU_DTYPE)

        @pl.when(i == last + 1)
        def _():
            dk_ref[...] = dk_carry[...]
            dv_ref[...] = dv_carry[...]

    cur = lambda i: (jnp.minimum(i, last), 0)
    prev = lambda i: (jnp.clip(i - 1, 0, last), 0)
    kv_meta = pl.BlockSpec((BLK, D_KV), lambda i: (0, 0))
    kv_prev = pl.BlockSpec((BLK, D_KV), prev)
    kv_cur = pl.BlockSpec((BLK, D_KV), cur)
    tab = pl.BlockSpec((BLK, 128), cur)
    return _pcall(
        body, name=f"attn_bwd_l{layer}", grid=(n_blk + 1,),
        in_specs=[pl.BlockSpec(memory_space=pltpu.SMEM),
                  pl.BlockSpec((BLK, D_ATTN), cur),
                  kv_meta, kv_prev, kv_cur, kv_meta, kv_prev, kv_cur,
                  pl.BlockSpec((BLK, D_ATTN), cur), tab, tab, tab],
        out_specs=[pl.BlockSpec((BLK, D_ATTN), cur), kv_prev, kv_prev, kv_meta, kv_meta,
                   pl.BlockSpec((N_Q_HEADS, 128), lambda i: (0, 0))],
        out_shape=[SDS((rows, D_ATTN), MXU_DTYPE), SDS((rows, D_KV), F32), SDS((rows, D_KV), F32),
                   SDS((BLK, D_KV), F32), SDS((BLK, D_KV), F32), SDS((N_Q_HEADS, 128), F32)],
        scratch_shapes=[pltpu.VMEM((BLK, D_KV), F32), pltpu.VMEM((BLK, D_KV), F32), pltpu.VMEM((BLK, D_ATTN), F32)],
        compiler_params=_cparams("arbitrary"),
    )(sinks, q, k, k, k, v, v, v, d_out, cos, sin_a, sin_b)


def _rope_bwd(dk, dv, dk_meta, dv_meta, cos, sin_a, sin_b, layer):
    rows = dk.shape[0]
    tm = _row_tile(rows)

    def body(dk_ref, dv_ref, dkm_ref, dvm_ref, c_ref, a_ref, b_ref, o_ref):
        c, a, b = c_ref[...], -a_ref[...], -b_ref[...]
        for t in range(2):
            x = dk_ref[:, t * 128:(t + 1) * 128]
            o_ref[:, t * 128:(t + 1) * 128] = _rope_lanes(x, c, a, b).astype(MXU_DTYPE)
        o_ref[:, D_KV:] = dv_ref[...].astype(MXU_DTYPE)

        @pl.when(pl.program_id(0) == 0)
        def _():
            cb, ab, bb = c[0:BLK], a[0:BLK], b[0:BLK]
            is_meta = lax.broadcasted_iota(jnp.int32, (BLK, 128), 0) >= PAD_ROWS
            for t in range(2):
                x = dk_ref[0:BLK, t * 128:(t + 1) * 128] + jnp.where(is_meta, dkm_ref[:, t * 128:(t + 1) * 128], 0.0)
                o_ref[0:BLK, t * 128:(t + 1) * 128] = _rope_lanes(x, cb, ab, bb).astype(MXU_DTYPE)
                xv = dv_ref[0:BLK, t * 128:(t + 1) * 128] + jnp.where(is_meta, dvm_ref[:, t * 128:(t + 1) * 128], 0.0)
                o_ref[0:BLK, D_KV + t * 128:D_KV + (t + 1) * 128] = xv.astype(MXU_DTYPE)

    tab = pl.BlockSpec((tm, 128), lambda i: (i, 0))
    kv = pl.BlockSpec((tm, D_KV), lambda i: (i, 0))
    meta = pl.BlockSpec((BLK, D_KV), lambda i: (0, 0))
    return _pcall(
        body, name=f"rope_bwd_l{layer}", grid=(rows // tm,),
        in_specs=[kv, kv, meta, meta, tab, tab, tab],
        out_specs=pl.BlockSpec((tm, 2 * D_KV), lambda i: (i, 0)),
        out_shape=SDS((rows, 2 * D_KV), MXU_DTYPE),
        compiler_params=_cparams("parallel"),
    )(dk, dv, dk_meta, dv_meta, cos, sin_a, sin_b)


def _s5_bwd(d_gated, y, u, carry_in, ssm, w_glu, b_glu3, layer):
    rows = y.shape[0]
    n_chunks = rows // BLK
    b_mat, c_mat, t_re, t_im, d_skip = (ssm[k] for k in ("b_mat", "c_mat", "t_re", "t_im", "d_skip"))

    def body(dz_ref, y_ref, u_ref, cin_ref, bm_ref, cm_ref, tre_ref, tim_ref, d_ref, wg_ref, bg_ref,
             du_ref, dwg_ref, dbg_ref, dd_ref, dbm_ref, dcm_ref, dab_ref,
             lam_carry, bu_scr, s_scr, sp_scr, g_scr, lam_scr):
        step = pl.program_id(0)
        chunk = n_chunks - 1 - step

        @pl.when(step == 0)
        def _():
            for r in (dwg_ref, dbg_ref, dd_ref, dbm_ref, dcm_ref, dab_ref, lam_carry):
                r[...] = jnp.zeros_like(r)

        y = y_ref[...]
        u = u_ref[...]
        d_o = dz_ref[...]
        z, t = _gelu_parts(y)
        zb = z.astype(MXU_DTYPE)
        sg = _sigmoid(_dot(zb, wg_ref[...]) + bg_ref[...])
        dgl = d_o * z * (sg * (1.0 - sg))
        dglb = dgl.astype(MXU_DTYPE)
        dz = d_o * sg + _dot_nt(dglb, wg_ref[...])
        dwg_ref[...] += _dot_tn(zb, dglb)
        dbg_ref[...] += jnp.sum(dgl, axis=0, keepdims=True)
        dy = dz * _gelu_grad(y, t)
        dd_ref[...] += jnp.sum(dy * u, axis=0, keepdims=True)
        grow = lax.broadcasted_iota(jnp.int32, (BLK, 128), 0) + chunk * BLK
        for sb in range(N_SB):
            cols = slice(sb * 128, (sb + 1) * 128)
            u_sb = u[:, cols].astype(MXU_DTYPE)
            dy_sb = dy[:, cols]
            dyb = dy_sb.astype(MXU_DTYPE)
            bu_scr[sb] = _dot(u_sb, bm_ref[sb])
            _scan_tiles(bu_scr.at[sb], s_scr.at[sb], tre_ref, tim_ref, sb,
                        cin_ref[2 * sb:2 * sb + 1, :], cin_ref[2 * sb + 1:2 * sb + 2, :], False, prev_ref=sp_scr.at[sb])
            dcm_ref[sb] += _dot_tn(s_scr[sb].astype(MXU_DTYPE), dyb)
            g_scr[sb] = _dot_nt(dyb, cm_ref[sb])
            n_r, n_i = _scan_tiles(g_scr.at[sb], lam_scr.at[sb], tre_ref, tim_ref, sb,
                                   lam_carry[2 * sb:2 * sb + 1, :], lam_carry[2 * sb + 1:2 * sb + 2, :], True)
            lam_carry[2 * sb:2 * sb + 1, :] = n_r
            lam_carry[2 * sb + 1:2 * sb + 2, :] = n_i
            lr, li = lam_scr[sb, :, :SB_STATES], lam_scr[sb, :, SB_STATES:]
            spr, spi = sp_scr[sb, :, :SB_STATES], sp_scr[sb, :, SB_STATES:]
            dab_ref[2 * sb:2 * sb + 1, :] += jnp.sum(spr * lr + spi * li, axis=0, keepdims=True)
            dab_ref[2 * sb + 1:2 * sb + 2, :] += jnp.sum(spr * li - spi * lr, axis=0, keepdims=True)
            lam = lam_scr[sb].astype(MXU_DTYPE)
            dbm_ref[sb] += _dot_tn(u_sb, lam)
            du = _dot_nt(lam, bm_ref[sb]) + d_ref[:, cols] * dy_sb
            du_ref[:, cols] = jnp.where(grow >= PAD_ROWS, du, 0.0).astype(MXU_DTYPE)

    rev = lambda j: (n_chunks - 1 - j, 0)
    full = lambda shape: pl.BlockSpec(shape, lambda j: (0,) * len(shape))
    of_layer = lambda shape: pl.BlockSpec((None,) + shape, lambda j: (layer,) + (0,) * len(shape))
    tables = [of_layer((N_SB, 8, SCAN_TILE, SB_STATES))] * 2
    chunk_scratch = pltpu.VMEM((N_SB, BLK, 2 * SB_STATES), F32)
    return _pcall(
        body, name=f"s5_bwd_l{layer}", grid=(n_chunks,),
        in_specs=[pl.BlockSpec((BLK, D_SSM), rev), pl.BlockSpec((BLK, D_SSM), rev), pl.BlockSpec((BLK, D_SSM), rev),
                  pl.BlockSpec((None, 8, SB_STATES), lambda j: (n_chunks - 1 - j, 0, 0)),
                  of_layer((N_SB, 128, 2 * SB_STATES)), of_layer((N_SB, 2 * SB_STATES, 128))] + tables + [
                  of_layer((1, D_SSM)), full((D_SSM, D_SSM)),
                  pl.BlockSpec((None, 1, D_SSM), lambda j: (layer, 0, 0))],
        out_specs=[pl.BlockSpec((BLK, D_SSM), rev), full((D_SSM, D_SSM)), full((1, D_SSM)), full((1, D_SSM)),
                   full((N_SB, 128, 2 * SB_STATES)), full((N_SB, 2 * SB_STATES, 128)), full((8, SB_STATES))],
        out_shape=[SDS((rows, D_SSM), MXU_DTYPE), SDS((D_SSM, D_SSM), F32), SDS((1, D_SSM), F32), SDS((1, D_SSM), F32),
                   SDS((N_SB, 128, 2 * SB_STATES), F32), SDS((N_SB, 2 * SB_STATES, 128), F32), SDS((8, SB_STATES), F32)],
        scratch_shapes=[pltpu.VMEM((8, SB_STATES), F32)] + [chunk_scratch] * 5,
        compiler_params=_cparams("arbitrary"),
    )(d_gated, y, u, carry_in, b_mat, c_mat, t_re, t_im, d_skip, w_glu, b_glu3)


DPROJ_PIECES = ((0, 1), (1, 2), (3, 1), (4, 2), (6, 2))


def _in_bwd(dproj_pieces, dhm, hres, gain3, w_in_g, layer):
    rows = hres.shape[0]
    tm = _row_tile(rows)

    def body(*refs):
        piece_refs = refs[:len(DPROJ_PIECES)]
        dh_ref, x_ref, g_ref, w_hbm, dx_ref, dg_ref, wt_scr, w_stage, w_sem = refs[len(DPROJ_PIECES):]
        i = pl.program_id(0)
        _load_resident_transposed(w_hbm, wt_scr, w_stage, w_sem, i == 0)

        @pl.when(i == 0)
        def _():
            dg_ref[...] = jnp.zeros_like(dg_ref)

        dh = None
        for piece_ref, (first, count) in zip(piece_refs, DPROJ_PIECES):
            wt = wt_scr[first:first + count].reshape(count * COL_SHARD, D)
            part = _dot(piece_ref[...], wt)
            dh = part if dh is None else dh + part
        dx, dg = _rms_bwd(x_ref[...], g_ref[...], dh)
        dg_ref[...] += dg
        dx_ref[...] = dh_ref[...] + dx

    row_d = pl.BlockSpec((tm, D), lambda i: (i, 0))
    return _pcall(
        body, name=f"in_bwd_l{layer}", grid=(rows // tm,),
        in_specs=[pl.BlockSpec((tm, count * COL_SHARD), lambda i: (i, 0)) for _, count in DPROJ_PIECES] + [
                  row_d, row_d,
                  pl.BlockSpec((None, 1, D), lambda i: (layer, 0, 0)),
                  pl.BlockSpec(memory_space=pl.ANY)],
        out_specs=[row_d, pl.BlockSpec((1, D), lambda i: (0, 0))],
        out_shape=[SDS((rows, D), F32), SDS((1, D), F32)],
        scratch_shapes=[pltpu.VMEM((N_DEV, COL_SHARD, D), MXU_DTYPE),
                        pltpu.VMEM((2, D, COL_SHARD), MXU_DTYPE), pltpu.SemaphoreType.DMA((2,))],
        compiler_params=_cparams("arbitrary"),
    )(*dproj_pieces, dhm, hres, gain3, w_in_g)


_ADAM_C1 = 1.0 / (1.0 - ADAM_B1 ** ADAM_STEP)
_ADAM_C2 = 1.0 / (1.0 - ADAM_B2 ** ADAM_STEP)


def _adam_math(w, g, m, v):
    m = ADAM_B1 * m + (1.0 - ADAM_B1) * g
    v = ADAM_B2 * v + (1.0 - ADAM_B2) * (g * g)
    delta = -ADAM_LR * ((m * _ADAM_C1) / (jnp.sqrt(v * _ADAM_C2) + ADAM_EPS) + ADAM_WD * w)
    return delta, m, v


def _adamw_layers(parts0, parts1, w, m, v, name):
    _, rows, cols = w.shape
    tr = min(rows, (1 << 16) // cols)
    nt = rows // tr

    def body(p0_ref, p1_ref, w_ref, m_ref, v_ref, g_ref, d_ref, nm_ref, nv_ref):
        layer = pl.program_id(0)

        def run(p_ref):
            g = p_ref[0].astype(F32)
            for s in range(1, N_DEV):
                g = g + p_ref[s].astype(F32)
            delta, nm, nv = _adam_math(w_ref[...], g, m_ref[...], v_ref[...])
            g_ref[...] = g
            d_ref[...] = delta
            nm_ref[...] = nm
            nv_ref[...] = nv

        @pl.when(layer == 0)
        def _():
            run(p0_ref)

        @pl.when(layer == 1)
        def _():
            run(p1_ref)

    wspec = pl.BlockSpec((None, tr, cols), lambda l, i: (l, i, 0))
    return _pcall(
        body, name=name, grid=(2, nt),
        in_specs=[pl.BlockSpec((N_DEV, tr, cols), lambda l, i: (0, jnp.where(l == 0, i, nt - 1), 0)),
                  pl.BlockSpec((N_DEV, tr, cols), lambda l, i: (0, jnp.where(l == 1, i, 0), 0)),
                  wspec, wspec, wspec],
        out_specs=[wspec] * 4, out_shape=[SDS(w.shape, F32)] * 4,
        compiler_params=_cparams("arbitrary", "arbitrary"),
    )(parts0, parts1, w, m, v)


def _sum_slots(parts, name):
    def body(p_ref, o_ref):
        acc = p_ref[0]
        for s in range(1, N_DEV):
            acc = acc + p_ref[s]
        o_ref[...] = acc

    vmem = pl.BlockSpec(memory_space=pltpu.VMEM)
    return _pcall(body, name=name, out_shape=SDS(parts.shape[1:], F32), in_specs=[vmem], out_specs=vmem,
                  compiler_params=_cparams())(parts)


def _adamw_packed(g, w, m, v, name):
    def body(g_ref, w_ref, m_ref, v_ref, d_ref, nm_ref, nv_ref):
        delta, nm, nv = _adam_math(w_ref[...], g_ref[...], m_ref[...], v_ref[...])
        d_ref[...] = delta
        nm_ref[...] = nm
        nv_ref[...] = nv

    vmem = pl.BlockSpec(memory_space=pltpu.VMEM)
    return _pcall(body, name=name, out_shape=[SDS(g.shape, F32)] * 3, in_specs=[vmem] * 4, out_specs=[vmem] * 3,
                  compiler_params=_cparams())(g, w, m, v)


def _ssm_discretize(a_re, a_im, log_dt, b_re, b_im):
    dt = jnp.exp(log_dt)[:, None]
    mag = jnp.exp(a_re * dt)
    ang = a_im * dt
    ab_re, ab_im = mag * jnp.cos(ang), mag * jnp.sin(ang)
    xr, xi = ab_re - 1.0, ab_im
    den = a_re * a_re + a_im * a_im
    q_re = (xr * a_re + xi * a_im) / den
    q_im = (xi * a_re - xr * a_im) / den
    bb_re = q_re[..., None] * b_re - q_im[..., None] * b_im
    bb_im = q_re[..., None] * b_im + q_im[..., None] * b_re
    return ab_re, ab_im, bb_re, bb_im


def _block_diag_b(bb):
    m = jnp.einsum("sgnc,gh->sgchn", bb.reshape(N_SB, 8, N_STATE, GROUP_CH), jnp.eye(8, dtype=F32))
    return m.reshape(N_SB, 128, SB_STATES)


def _block_diag_b_t(dm):
    return jnp.einsum("sgchn,gh->sgnc", dm.reshape(N_SB, 8, GROUP_CH, 8, N_STATE),
                      jnp.eye(8, dtype=F32)).reshape(N_GROUPS, N_STATE, GROUP_CH)


def _block_diag_c(cc):
    m = jnp.einsum("sgcn,gh->sgnhc", cc.reshape(N_SB, 8, GROUP_CH, N_STATE), jnp.eye(8, dtype=F32))
    return m.reshape(N_SB, SB_STATES, 128)


def _block_diag_c_t(dm):
    return jnp.einsum("sgnhc,gh->sgcn", dm.reshape(N_SB, 8, N_STATE, 8, GROUP_CH),
                      jnp.eye(8, dtype=F32)).reshape(N_GROUPS, GROUP_CH, N_STATE)


def _ssm_tables(ab_re, ab_im, bb_re, bb_im, c_re, c_im, d_skip):
    pr, pi = ab_re.reshape(1, -1), ab_im.reshape(1, -1)
    cr, ci = pr, pi
    squares = []
    for _ in range(3):
        squares.append((cr, ci))
        pr, pi = (jnp.concatenate([pr, pr * cr - pi * ci], axis=0),
                  jnp.concatenate([pi, pr * ci + pi * cr], axis=0))
        cr, ci = cr * cr - ci * ci, 2.0 * cr * ci
    r = jnp.arange(SCAN_TILE)[:, None]
    fwd = [(jnp.where(r >= (1 << k), squares[k][0], 0.0), jnp.where(r >= (1 << k), squares[k][1], 0.0))
           for k in range(3)] + [(pr, pi)]
    rev = [(jnp.where(r < SCAN_TILE - (1 << k), squares[k][0], 0.0),
            jnp.where(r < SCAN_TILE - (1 << k), -squares[k][1], 0.0)) for k in range(3)] + [(pr[::-1], -pi[::-1])]
    table = lambda part: jnp.stack([e[part] for e in fwd + rev]).reshape(
        8, SCAN_TILE, N_SB, SB_STATES).transpose(2, 0, 1, 3)
    return dict(
        b_mat=jnp.concatenate([_block_diag_b(bb_re), _block_diag_b(bb_im)], axis=-1).astype(MXU_DTYPE),
        c_mat=jnp.concatenate([_block_diag_c(c_re), -_block_diag_c(c_im)], axis=1).astype(MXU_DTYPE),
        t_re=table(0), t_im=table(1),
        d_skip=d_skip.reshape(1, D_SSM))


def _rope_tables(rows):
    pos = (jnp.arange(rows, dtype=jnp.int32) - PAD_ROWS).astype(F32)
    inv_freq = 1.0 / (ROPE_THETA ** (jnp.arange(0, HEAD_DIM, 2, dtype=F32) / HEAD_DIM))
    ang = pos[:, None] * inv_freq[None, :]
    ang = jnp.concatenate([ang, ang, ang, ang], axis=-1)
    first_half = (jnp.arange(128) % HEAD_DIM) < HEAD_DIM // 2
    sin = jnp.sin(ang)
    return jnp.cos(ang), jnp.where(first_half, -sin, 0.0), jnp.where(first_half, 0.0, sin)


def _pack(arrays):
    flat = jnp.concatenate([a.reshape(-1).astype(F32) for a in arrays])
    pad = (-flat.shape[0]) % 1024
    return jnp.pad(flat, (0, pad)).reshape(-1, 128)


def _unpack(packed, like):
    flat = packed.reshape(-1)
    out, off = [], 0
    for a in like:
        n = math.prod(a.shape)
        out.append(flat[off:off + n].reshape(a.shape))
        off += n
    return out


BIG = ("w_in", "w_glu", "w_o_ssm", "w_o_attn", "w_out", "w_up", "w_down")
WEIGHTS = ("meta_tokens", "norm_mix_pre", "norm_mix_post", "norm_mlp_pre", "norm_mlp_post", "w_in",
           "ssm_a_re", "ssm_a_im", "ssm_log_dt", "ssm_b_re", "ssm_b_im", "ssm_c_re", "ssm_c_im", "ssm_d",
           "w_glu", "b_glu", "attn_sinks", "w_o_ssm", "w_o_attn", "w_out", "w_up", "w_down")
SMALL = tuple(n for n in WEIGHTS if n not in BIG)


def kernel(x, meta_tokens, norm_mix_pre, norm_mix_post, norm_mlp_pre, norm_mlp_post, w_in, ssm_a_re, ssm_a_im, ssm_log_dt, ssm_b_re, ssm_b_im, ssm_c_re, ssm_c_im, ssm_d, w_glu, b_glu, attn_sinks, w_o_ssm, w_o_attn, w_out, w_up, w_down, loss_target, m_meta_tokens, m_norm_mix_pre, m_norm_mix_post, m_norm_mlp_pre, m_norm_mlp_post, m_w_in, m_ssm_a_re, m_ssm_a_im, m_ssm_log_dt, m_ssm_b_re, m_ssm_b_im, m_ssm_c_re, m_ssm_c_im, m_ssm_d, m_w_glu, m_b_glu, m_attn_sinks, m_w_o_ssm, m_w_o_attn, m_w_out, m_w_up, m_w_down, v_meta_tokens, v_norm_mix_pre, v_norm_mix_post, v_norm_mlp_pre, v_norm_mlp_post, v_w_in, v_ssm_a_re, v_ssm_a_im, v_ssm_log_dt, v_ssm_b_re, v_ssm_b_im, v_ssm_c_re, v_ssm_c_im, v_ssm_d, v_w_glu, v_b_glu, v_attn_sinks, v_w_o_ssm, v_w_o_attn, v_w_out, v_w_up, v_w_down):
    args = locals()
    w = {n: args[n] for n in WEIGHTS}
    m = {n: args["m_" + n] for n in WEIGHTS}
    v = {n: args["v_" + n] for n in WEIGHTS}
    n_layers = w_in.shape[0]
    seq = x.shape[1]
    rows = seq + BLK
    my_slot = _slot(_mesh_pos())

    assert n_layers == 2
    xfer = {n: [w[n][l].astype(XFER_DTYPE) for l in range(n_layers)] for n in BIG}
    mixer_small = ("w_glu", "w_o_ssm", "w_o_attn", "w_out")
    meta_g, w_in_g0 = _exchange_by_sequencer([meta_tokens, xfer["w_in"][0]], True, 0, "gather_in0")
    mix0_g = _exchange_by_sequencer([xfer[n][0] for n in mixer_small], True, 1, "gather_mix0")
    meta_full = meta_g.transpose(1, 0, 2).reshape(N_META, D)

    def mixer_weights(w_glu_g, w_o_ssm_g, w_o_attn_g, w_out_g):
        return dict(w_glu=w_glu_g.reshape(D_SSM, D_SSM), w_o_ssm=w_o_ssm_g.transpose(1, 0, 2).reshape(D_SSM, D),
                    w_o_attn=w_o_attn_g.reshape(D_ATTN, D), w_out=w_out_g.reshape(D, D),
                    w_o_ssm_t=w_o_ssm_g.transpose(0, 2, 1).reshape(D, D_SSM),
                    w_o_attn_t=w_o_attn_g.reshape(D_ATTN, D).T, w_out_t=w_out_g.reshape(D, D).T)

    gathered = [dict(w_in=w_in_g0, **mixer_weights(*mix0_g)), {}]

    gains = {n: w[n].reshape(n_layers, 1, D) for n in ("norm_mix_pre", "norm_mix_post", "norm_mlp_pre", "norm_mlp_post")}
    b_glu3 = b_glu.reshape(n_layers, 1, D_SSM)
    cos, sin_a, sin_b = _rope_tables(rows)

    disc, disc_vjp = jax.vjp(jax.vmap(_ssm_discretize), ssm_a_re, ssm_a_im, ssm_log_dt, ssm_b_re, ssm_b_im)
    ssm = jax.vmap(_ssm_tables)(*disc, ssm_c_re, ssm_c_im, ssm_d)

    hres = jnp.concatenate([jnp.zeros((PAD_ROWS, D), F32), meta_full, x[0]], axis=0)

    saved = []
    for l in range(n_layers):
        wl = gathered[l]
        u, gates, q, k, vv, h = _in_proj(hres, gains["norm_mix_pre"], wl["w_in"], cos, sin_a, sin_b, l,
                                         after=[ssm["b_mat"], ssm["c_mat"], ssm["t_re"], ssm["t_im"]] if l == 0 else ())
        if l == 0:
            wl["w_up"], wl["w_down"] = _exchange_by_sequencer([xfer["w_up"][0], xfer["w_down"][0]], True, 2,
                                                              "gather_mlp0", after=[h])
        y, y_ssm, carry_in = _s5_fwd(u, ssm, wl["w_glu"], b_glu3, l)
        if l == 0:
            l1_g = _exchange_by_sequencer([xfer[n][1] for n in ("w_in",) + mixer_small + ("w_up", "w_down")], True, 3,
                                          "gather_l1", after=[y, wl["w_up"]])
            gathered[1] = dict(w_in=l1_g[0], w_up=l1_g[5], w_down=l1_g[6], **mixer_weights(*l1_g[1:5]))
            last_exchange = l1_g[:1]
        y_attn = _attn_fwd(q, k, vv, attn_sinks, l)
        merged, mix, hres_mid = _merge_fwd(y_ssm, y_attn, gates, hres, wl["w_o_ssm"], wl["w_o_attn"], wl["w_out"],
                                           gains["norm_mix_post"], l)
        hres_in = hres
        if l + 1 < n_layers:
            up, h2, ff, hres = _mlp_fwd(hres_mid, gains["norm_mlp_pre"], gains["norm_mlp_post"], wl["w_up"],
                                        wl["w_down"], l)
        else:
            target = jnp.concatenate([jnp.zeros((BLK, D), F32), loss_target[0]], axis=0)
            up, h2, ff, dhres, loss_vec = _mlp_fwd(hres_mid, gains["norm_mlp_pre"], gains["norm_mlp_post"], wl["w_up"],
                                                   wl["w_down"], l, target=target)
        saved.append(dict(hres=hres_in, u=u, gates=gates, h=h, q=q, k=k, v=vv, y=y, y_ssm=y_ssm,
                          carry_in=carry_in, y_attn=y_attn, merged=merged, mix=mix, hres_mid=hres_mid,
                          up=up, h2=h2, ff=ff))

    small_grads = {}
    recv_up, recv_down, recv_mix = [None] * n_layers, [None] * n_layers, [None] * n_layers
    for l in reversed(range(n_layers)):
        s = saved[l]
        wl = gathered[l]
        dff, dup, dhm, dg_mlp_post, dg_mlp_pre = _mlp_bwd(dhres, s["ff"], s["up"], s["hres_mid"], gains["norm_mlp_pre"],
                                                          gains["norm_mlp_post"], wl["w_up"], wl["w_down"], l)
        dw_up = _matmul_tn(s["h2"], dup, f"dw_up_l{l}", dev_major_cols=COL_SHARD)
        recv_up[l] = _exchange_by_sequencer([dw_up], False, 4 + 3 * l, f"scatter_up{l}", after=last_exchange)
        dw_down = _matmul_tn(s["up"], dff, f"dw_down_l{l}", a_fn=_relu_squared).reshape(N_DEV, COL_SHARD, D)
        recv_down[l] = _exchange_by_sequencer([dw_down], False, 5 + 3 * l, f"scatter_down{l}", after=recv_up[l])
        last_exchange = recv_down[l]
        dmix, da1, da2, dgs, dga, dy_ssm, dy_attn, dg_mix_post = _merge_bwd(
            dhm, s["mix"], s["y_ssm"], s["y_attn"], s["gates"], wl["w_o_ssm"], wl["w_o_attn"], wl["w_o_ssm_t"],
            wl["w_o_attn_t"], wl["w_out_t"], gains["norm_mix_post"], l)
        dw_out = _matmul_tn(s["merged"], dmix, f"dw_out_l{l}").reshape(N_DEV, D // N_DEV, D)
        dw_o_attn = _matmul_tn(s["y_attn"], da2, f"dw_o_attn_l{l}").reshape(N_DEV, D_ATTN // N_DEV, D)
        dw_o_ssm = _matmul_tn(s["y_ssm"], da1, f"dw_o_ssm_l{l}", dev_major_cols=D // N_DEV)
        if l == 0:
            recv_out0 = _exchange_by_sequencer([dw_o_ssm, dw_o_attn, dw_out], False, 11, "scatter_out0",
                                               after=last_exchange)
            last_exchange = recv_out0[:1]
        dq, dk, dv, dk_meta, dv_meta, dsink = _attn_bwd(s["q"], s["k"], s["v"], dy_attn, attn_sinks, cos, sin_a, sin_b, l)
        dkv = _rope_bwd(dk, dv, dk_meta, dv_meta, cos, sin_a, sin_b, l)
        du, dw_glu, db_glu, dd_skip, db_mat, dc_mat, dab = _s5_bwd(dy_ssm, s["y"], s["u"], s["carry_in"], ssm,
                                                                    wl["w_glu"], b_glu3, l)
        dproj = (du, dq, dkv, dgs, dga)
        dw_in = _dw_in(s["h"], dproj, l)
        mix_parts = [dw_in, dw_glu.astype(XFER_DTYPE).reshape(N_DEV, D_SSM // N_DEV, D_SSM), dw_o_ssm, dw_o_attn, dw_out]
        if l > 0:
            recv_mix[l] = _exchange_by_sequencer(mix_parts, False, 6 + 3 * l, f"scatter_mix{l}", after=last_exchange)
            last_exchange = recv_mix[l][:1]
        else:
            recv_mix[0] = _exchange_by_sequencer(mix_parts[:2], False, 6, "scatter_in0", after=last_exchange) + recv_out0
            last_exchange = recv_mix[0][:1]
        dhres, dg_mix_pre = _in_bwd(dproj, dhm, s["hres"], gains["norm_mix_pre"], wl["w_in"], l)

        for name, val in (("norm_mix_pre", dg_mix_pre[0]), ("norm_mix_post", dg_mix_post[0]),
                          ("norm_mlp_pre", dg_mlp_pre[0]), ("norm_mlp_post", dg_mlp_post[0]),
                          ("dab", dab), ("db_mat", db_mat), ("dc_mat", dc_mat),
                          ("ssm_d", dd_skip.reshape(N_GROUPS, GROUP_CH)), ("b_glu", db_glu[0]),
                          ("attn_sinks", dsink[:, 0])):
            small_grads.setdefault(name, [None] * n_layers)[l] = val

    grad_x = dhres[BLK:][None]
    stacked = {n: jnp.stack(v) for n, v in small_grads.items()}
    dab = stacked["dab"].reshape(n_layers, N_SB, 2, SB_STATES)
    db_mat, dc_mat = stacked["db_mat"], stacked["dc_mat"]
    b_t, c_t = jax.vmap(_block_diag_b_t), jax.vmap(_block_diag_c_t)
    (stacked["ssm_a_re"], stacked["ssm_a_im"], stacked["ssm_log_dt"], stacked["ssm_b_re"],
     stacked["ssm_b_im"]) = disc_vjp((dab[:, :, 0].reshape(n_layers, N_GROUPS, N_STATE),
                                      dab[:, :, 1].reshape(n_layers, N_GROUPS, N_STATE),
                                      b_t(db_mat[..., :SB_STATES]), b_t(db_mat[..., SB_STATES:])))
    stacked["ssm_c_re"] = c_t(dc_mat[:, :, :SB_STATES])
    stacked["ssm_c_im"] = -c_t(dc_mat[:, :, SB_STATES:])
    small_names = [n for n in SMALL if n != "meta_tokens"]
    partial_small = [dhres[PAD_ROWS:BLK]] + [stacked[n] for n in small_names] + [loss_vec[0, :1]]
    small_parts, = _exchange_by_sequencer([_pack(partial_small)], True, 10, "gather_small", after=last_exchange)

    grads, delta, new_m, new_v = {}, {}, {}, {}

    def adamw_big(names, recv0, recv1):
        for n, p0, p1 in zip(names, recv0, recv1):
            grads[n], delta[n], new_m[n], new_v[n] = _adamw_layers(p0, p1, w[n], m[n], v[n], f"adamw_{n}")

    adamw_big(("w_up", "w_down"), recv_up[0] + recv_down[0], recv_up[1] + recv_down[1])
    summed = _unpack(_sum_slots(small_parts, "sum_small_grads"), partial_small)
    loss = summed[-1][0]
    grads.update(zip(small_names, summed[1:-1]))
    grads["meta_tokens"] = lax.dynamic_slice_in_dim(summed[0], my_slot * (D // N_DEV), D // N_DEV, axis=1)
    like = [w[n] for n in SMALL]
    d_s, m_s, v_s = _adamw_packed(_pack([grads[n] for n in SMALL]), _pack(like), _pack([m[n] for n in SMALL]),
                                  _pack([v[n] for n in SMALL]), "adamw_small")
    adamw_big(("w_in",) + mixer_small, recv_mix[0], recv_mix[1])
    for n, dd, mm, vs in zip(SMALL, _unpack(d_s, like), _unpack(m_s, like), _unpack(v_s, like)):
        delta[n], new_m[n], new_v[n] = dd, mm, vs

    return (loss, grad_x, *[grads[n] for n in WEIGHTS], *[delta[n] for n in WEIGHTS],
            *[new_m[n] for n in WEIGHTS], *[new_v[n] for n in WEIGHTS])
```

```python
import functools
import math

import jax
import jax.numpy as jnp
from jax import lax
from jax.experimental import pallas as pl
from jax.experimental.pallas import tpu as pltpu
from jax.experimental.pallas import tpu_sc as plsc

F32 = jnp.float32
MXU_DTYPE = jnp.bfloat16
XFER_DTYPE = MXU_DTYPE
_pcall = pl.pallas_call
SDS = jax.ShapeDtypeStruct

D = 1024
D_SSM = 512
D_ATTN = 1024
D_KV = 256
D_FF = 4096
D_IN = 4096
HEAD_DIM = 64
N_Q_HEADS = 16
N_KV_HEADS = 4
Q_PER_KV = 4
N_META = 16
BLK = 128
PAD_ROWS = BLK - N_META
N_GROUPS = 32
N_STATE = 64
GROUP_CH = 16
N_SB = 4
SB_STATES = 512
ROPE_THETA = 10000.0
ATTN_SCALE = HEAD_DIM ** -0.5
NEG_INF = -1e30
RMS_EPS = 1e-6
N_DEV = 8
COL_SHARD = 512

ADAM_LR = 0.001
ADAM_B1 = 0.9
ADAM_B2 = 0.999
ADAM_EPS = 1e-08
ADAM_WD = 0.01
ADAM_STEP = 10

VMEM_LIMIT = 56 * 1024 * 1024

_NT = (((1,), (1,)), ((), ()))
_TN = (((0,), (0,)), ((), ()))


def _cparams(*sem):
    return pltpu.CompilerParams(dimension_semantics=tuple(sem) if sem else None,
                                vmem_limit_bytes=VMEM_LIMIT)


def _row_tile(rows, cap=640):
    for t in (1664, 640, 512, 320, 256, 128):
        if t <= cap and rows % t == 0:
            return t
    raise ValueError(f"unsupported row count {rows}")


def _dot(a, b):
    return jnp.dot(a, b, preferred_element_type=F32)


def _dot_nt(a, b):
    return lax.dot_general(a, b, _NT, preferred_element_type=F32)


def _dot_tn(a, b):
    return lax.dot_general(a, b, _TN, preferred_element_type=F32)


def _sigmoid(x):
    return 1.0 / (1.0 + jnp.exp(-x))


_GELU_C = math.sqrt(2.0 / math.pi)


def _gelu_parts(y):
    t = jnp.tanh(_GELU_C * (y + 0.044715 * (y * y * y)))
    return 0.5 * y * (1.0 + t), t


def _gelu_grad(y, t):
    return 0.5 * (1.0 + t) + 0.5 * y * (1.0 - t * t) * (_GELU_C * (1.0 + 0.134145 * (y * y)))


def _rms_fwd(x, gain):
    r = lax.rsqrt(jnp.mean(x * x, axis=-1, keepdims=True) + RMS_EPS)
    return (x * r) * gain


def _rms_bwd(x, gain, dout):
    r = lax.rsqrt(jnp.mean(x * x, axis=-1, keepdims=True) + RMS_EPS)
    xh = x * r
    dxh = dout * gain
    dx = r * (dxh - xh * jnp.mean(dxh * xh, axis=-1, keepdims=True))
    return dx, jnp.sum(dout * xh, axis=0, keepdims=True)


def _mesh_pos():
    return lax.axis_index("x"), lax.axis_index("y"), lax.axis_index("c")


def _peer(pos, d):
    x, y, c = pos
    return (1 - x if d & 4 else x, 1 - y if d & 2 else y, 1 - c if d & 1 else c)


def _slot(pos):
    return 4 * pos[0] + 2 * pos[1] + pos[2]


def _exchange_copy(gather, src_ref, land_ref, sems, k, d, me, send_side):
    peer = _peer(me, d)
    sender = me if send_side else peer
    src = src_ref if gather else src_ref.at[_slot(peer) if send_side else _slot(me)]
    return pltpu.make_async_remote_copy(
        src_ref=src, dst_ref=land_ref.at[_slot(sender)],
        send_sem=sems[0].at[k * (N_DEV - 1) + d - 1], recv_sem=sems[1].at[k * (N_DEV - 1) + d - 1],
        device_id=peer, device_id_type=pl.DeviceIdType.MESH)


def _exchange_by_sequencer(srcs, gather, collective_id, name, after=()):
    n = len(srcs)
    flags = [gather] * n if isinstance(gather, bool) else list(gather)
    land_types = [SDS(((N_DEV,) + s.shape) if g else s.shape, s.dtype) for s, g in zip(srcs, flags)]

    def body(*refs):
        src_refs = refs[:n]
        land_refs = refs[n + len(after):2 * n + len(after)]
        sems = refs[2 * n + len(after):2 * n + len(after) + 2]
        local_sems = refs[2 * n + len(after) + 2]
        me = _mesh_pos()
        barrier = pltpu.get_barrier_semaphore()
        for d in range(1, N_DEV):
            pl.semaphore_signal(barrier, inc=1, device_id=_peer(me, d), device_id_type=pl.DeviceIdType.MESH)
        pl.semaphore_wait(barrier, N_DEV - 1)
        own = [pltpu.make_async_copy(src_refs[k] if flags[k] else src_refs[k].at[_slot(me)],
                                     land_refs[k].at[_slot(me)], local_sems.at[k]) for k in range(n)]
        for cp in own:
            cp.start()
        for k in range(n):
            for d in range(1, N_DEV):
                _exchange_copy(flags[k], src_refs[k], land_refs[k], sems, k, d, me, True).start()
        for cp in own:
            cp.wait()
        for k in range(n):
            for d in range(1, N_DEV):
                _exchange_copy(flags[k], src_refs[k], land_refs[k], sems, k, d, me, True).wait_send()
        for k in range(n):
            for d in range(1, N_DEV):
                _exchange_copy(flags[k], src_refs[k], land_refs[k], sems, k, d, me, False).wait_recv()

    sem_type = pltpu.SemaphoreType.DMA((n * (N_DEV - 1),))
    return pl.kernel(
        body, out_type=land_types, mesh=plsc.ScalarSubcoreMesh(axis_name="sequencer", num_cores=1), name=name,
        scratch_types=(sem_type, sem_type, pltpu.SemaphoreType.DMA((n,))),
        compiler_params=pltpu.CompilerParams(collective_id=collective_id),
    )(*srcs, *after)


def _load_resident(w_hbm, w_scr, sems, first_step):
    @pl.when(first_step)
    def _():
        copies = [pltpu.make_async_copy(w_hbm.at[s], w_scr.at[s], sems.at[s]) for s in range(N_DEV)]
        for cp in copies:
            cp.start()
        for cp in copies:
            cp.wait()


def _load_resident_transposed(w_hbm, w_scr, stage, sems, first_step):
    @pl.when(first_step)
    def _():
        copies = [pltpu.make_async_copy(w_hbm.at[s], stage.at[s % 2], sems.at[s % 2]) for s in range(N_DEV)]
        copies[0].start()
        for s in range(N_DEV):
            if s + 1 < N_DEV:
                copies[s + 1].start()
            copies[s].wait()
            w_scr[s] = stage[s % 2].T


def _rope_lanes(t, cos, sin_a, sin_b):
    return t * cos + pltpu.roll(t, 96, 1) * sin_a + pltpu.roll(t, 32, 1) * sin_b


def _in_proj(hres, gain3, w_in_g, cos, sin_a, sin_b, layer, after=()):
    rows = hres.shape[0]
    tm = _row_tile(rows, 320)

    def body(x_ref, g_ref, w_hbm, c_ref, a_ref, b_ref, *refs):
        u_ref, gate_ref, q_ref, k_ref, v_ref, h_ref, w_scr, w_sem = refs[len(after):]
        _load_resident(w_hbm, w_scr, w_sem, pl.program_id(0) == 0)
        hn = _rms_fwd(x_ref[...], g_ref[...]).astype(MXU_DTYPE)
        h_ref[...] = hn
        c, a, b = c_ref[...], a_ref[...], b_ref[...]
        u_ref[...] = _dot(hn, w_scr[0])
        for shard in (1, 2):
            res = _dot(hn, w_scr[shard])
            for t in range(4):
                lanes = slice(t * 128, (t + 1) * 128)
                out = slice((shard - 1) * COL_SHARD + t * 128, (shard - 1) * COL_SHARD + (t + 1) * 128)
                q_ref[:, out] = (_rope_lanes(res[:, lanes], c, a, b) * ATTN_SCALE).astype(MXU_DTYPE)
        res = _dot(hn, w_scr[3])
        for t in range(2):
            lanes = slice(t * 128, (t + 1) * 128)
            k_ref[:, lanes] = _rope_lanes(res[:, lanes], c, a, b).astype(MXU_DTYPE)
        v_ref[...] = res[:, D_KV:].astype(MXU_DTYPE)
        for shard in range(4, N_DEV):
            gate_ref[:, (shard - 4) * COL_SHARD:(shard - 3) * COL_SHARD] = _dot(hn, w_scr[shard])

    tab = pl.BlockSpec((tm, 128), lambda i: (i, 0))
    kv = pl.BlockSpec((tm, D_KV), lambda i: (i, 0))
    row_d = pl.BlockSpec((tm, D), lambda i: (i, 0))
    return _pcall(
        body, name=f"in_proj_l{layer}", grid=(rows // tm,),
        in_specs=[row_d, pl.BlockSpec((None, 1, D), lambda i: (layer, 0, 0)),
                  pl.BlockSpec(memory_space=pl.ANY), tab, tab, tab] + [pl.BlockSpec(memory_space=pl.ANY)] * len(after),
        out_specs=[pl.BlockSpec((tm, D_SSM), lambda i: (i, 0)), pl.BlockSpec((tm, 2 * D), lambda i: (i, 0)),
                   row_d, kv, kv, row_d],
        out_shape=[SDS((rows, D_SSM), F32), SDS((rows, 2 * D), F32), SDS((rows, D_ATTN), MXU_DTYPE),
                   SDS((rows, D_KV), MXU_DTYPE), SDS((rows, D_KV), MXU_DTYPE), SDS((rows, D), MXU_DTYPE)],
        scratch_shapes=[pltpu.VMEM((N_DEV, D, COL_SHARD), MXU_DTYPE), pltpu.SemaphoreType.DMA((N_DEV,))],
        compiler_params=_cparams("arbitrary"),
    )(hres, gain3, w_in_g, cos, sin_a, sin_b, *after)


SCAN_TILE = 8


def _scan_tiles(x_scr, out_scr, tre_ref, tim_ref, carries, reverse, prev_scr=None):
    base = 4 if reverse else 0
    n_tiles = BLK // SCAN_TILE
    row = lax.broadcasted_iota(jnp.int32, (SCAN_TILE, SB_STATES), 0)
    carries = list(carries)
    for j in (range(n_tiles - 1, -1, -1) if reverse else range(n_tiles)):
        rows = slice(SCAN_TILE * j, SCAN_TILE * (j + 1))
        for sb in range(N_SB):
            t_r, t_i = carries[sb]
            xr = x_scr[sb, rows, :SB_STATES]
            xi = x_scr[sb, rows, SB_STATES:]
            for k in range(3):
                shift = SCAN_TILE - (1 << k) if reverse else (1 << k)
                rr = pltpu.roll(xr, shift, 0)
                ri = pltpu.roll(xi, shift, 0)
                ar = tre_ref[sb, base + k]
                ai = tim_ref[sb, base + k]
                xr, xi = xr + (ar * rr - ai * ri), xi + (ar * ri + ai * rr)
            pr = tre_ref[sb, base + 3]
            pi = tim_ref[sb, base + 3]
            xr, xi = xr + (pr * t_r - pi * t_i), xi + (pr * t_i + pi * t_r)
            out_scr[sb, rows, :SB_STATES] = xr
            out_scr[sb, rows, SB_STATES:] = xi
            if prev_scr is not None:
                prev_scr[sb, rows, :SB_STATES] = jnp.where(row == 0, t_r, pltpu.roll(xr, 1, 0))
                prev_scr[sb, rows, SB_STATES:] = jnp.where(row == 0, t_i, pltpu.roll(xi, 1, 0))
            edge = slice(0, 1) if reverse else slice(SCAN_TILE - 1, SCAN_TILE)
            carries[sb] = (xr[edge], xi[edge])
    return carries


def _s5_fwd(u, ssm, w_glu, b_glu3, layer):
    rows = u.shape[0]
    n_chunks = rows // BLK
    b_mat, c_mat, t_re, t_im, d_skip = (ssm[k] for k in ("b_mat", "c_mat", "t_re", "t_im", "d_skip"))

    def body(u_ref, bm_ref, cm_ref, tre_ref, tim_ref, d_ref, wg_ref, bg_ref,
             y_ref, ys_ref, cin_ref, carry, bu_scr, s_scr):
        @pl.when(pl.program_id(0) == 0)
        def _():
            carry[...] = jnp.zeros_like(carry)

        cin_ref[...] = carry[...]
        u = u_ref[...]
        for sb in range(N_SB):
            bu_scr[sb] = _dot(u[:, sb * 128:(sb + 1) * 128].astype(MXU_DTYPE), bm_ref[sb])
        entering = [(carry[2 * sb:2 * sb + 1, :], carry[2 * sb + 1:2 * sb + 2, :]) for sb in range(N_SB)]
        leaving = _scan_tiles(bu_scr, s_scr, tre_ref, tim_ref, entering, False)
        for sb in range(N_SB):
            cols = slice(sb * 128, (sb + 1) * 128)
            carry[2 * sb:2 * sb + 1, :], carry[2 * sb + 1:2 * sb + 2, :] = leaving[sb]
            y_ref[:, cols] = _dot(s_scr[sb].astype(MXU_DTYPE), cm_ref[sb]) + d_ref[:, cols] * u[:, cols]
        z, _ = _gelu_parts(y_ref[...])
        gl = _dot(z.astype(MXU_DTYPE), wg_ref[...]) + bg_ref[...]
        ys_ref[...] = (z * _sigmoid(gl)).astype(MXU_DTYPE)

    full = lambda shape: pl.BlockSpec(shape, lambda j: (0,) * len(shape))
    of_layer = lambda shape: pl.BlockSpec((None,) + shape, lambda j: (layer,) + (0,) * len(shape))
    return _pcall(
        body, name=f"s5_fwd_l{layer}", grid=(n_chunks,),
        in_specs=[pl.BlockSpec((BLK, D_SSM), lambda j: (j, 0)),
                  of_layer((N_SB, 128, 2 * SB_STATES)), of_layer((N_SB, 2 * SB_STATES, 128)),
                  of_layer((N_SB, 8, SCAN_TILE, SB_STATES)), of_layer((N_SB, 8, SCAN_TILE, SB_STATES)),
                  of_layer((1, D_SSM)), full((D_SSM, D_SSM)),
                  pl.BlockSpec((None, 1, D_SSM), lambda j: (layer, 0, 0))],
        out_specs=[pl.BlockSpec((BLK, D_SSM), lambda j: (j, 0)), pl.BlockSpec((BLK, D_SSM), lambda j: (j, 0)),
                   pl.BlockSpec((None, 8, SB_STATES), lambda j: (j, 0, 0))],
        out_shape=[SDS((rows, D_SSM), F32), SDS((rows, D_SSM), MXU_DTYPE), SDS((n_chunks, 8, SB_STATES), F32)],
        scratch_shapes=[pltpu.VMEM((8, SB_STATES), F32), pltpu.VMEM((N_SB, BLK, 2 * SB_STATES), F32),
                        pltpu.VMEM((N_SB, BLK, 2 * SB_STATES), F32)],
        compiler_params=_cparams("arbitrary"),
    )(u, b_mat, c_mat, t_re, t_im, d_skip, w_glu, b_glu3)


def _attn_mask(i):
    row = lax.broadcasted_iota(jnp.int32, (BLK, 3 * BLK), 0) + i * BLK
    col = lax.broadcasted_iota(jnp.int32, (BLK, 3 * BLK), 1)
    seg = jnp.right_shift(col, 7)
    c = jnp.bitwise_and(col, BLK - 1)
    kidx = c + (i + seg - 2) * BLK
    ok_meta = (seg == 0) & (c >= PAD_ROWS) & (row - c >= BLK)
    ok_win = (seg > 0) & (kidx >= PAD_ROWS) & (kidx <= row) & (row - kidx < BLK)
    return jnp.where(ok_meta | ok_win, 0.0, NEG_INF)


def _head_lanes(h):
    return slice(h * HEAD_DIM, (h + 1) * HEAD_DIM)


def _group_rows(ref, kvh):
    return jnp.concatenate([ref[:, _head_lanes(kvh * Q_PER_KV + g)] for g in range(Q_PER_KV)], axis=0)


def _group_bias(bias, sink_ref, layer, kvh):
    first_col = lax.broadcasted_iota(jnp.int32, (BLK, BLK), 1) == 0
    slabs = []
    for g in range(Q_PER_KV):
        first = jnp.where(first_col, sink_ref[layer, kvh * Q_PER_KV + g], bias[:, :BLK])
        slabs.append(jnp.concatenate([first, bias[:, BLK:]], axis=1))
    return jnp.concatenate(slabs, axis=0)


def _attn_probs(q4, k3, bias4):
    s = _dot_nt(q4, k3) + bias4
    e = jnp.exp(s - jnp.max(s, axis=-1, keepdims=True))
    return e * (1.0 / jnp.sum(e, axis=-1, keepdims=True))


def _attn_fwd(q, k, v, sinks, layer):
    rows = q.shape[0]
    n_blk = rows // BLK

    def body(sink_ref, q_ref, km_ref, kp_ref, kc_ref, vm_ref, vp_ref, vc_ref, o_ref):
        bias = _attn_mask(pl.program_id(0))
        for kvh in range(N_KV_HEADS):
            lanes = _head_lanes(kvh)
            k3 = jnp.concatenate([km_ref[:, lanes], kp_ref[:, lanes], kc_ref[:, lanes]], axis=0)
            v3 = jnp.concatenate([vm_ref[:, lanes], vp_ref[:, lanes], vc_ref[:, lanes]], axis=0)
            p = _attn_probs(_group_rows(q_ref, kvh), k3, _group_bias(bias, sink_ref, layer, kvh))
            o4 = _dot(p.astype(MXU_DTYPE), v3).astype(MXU_DTYPE)
            for g in range(Q_PER_KV):
                o_ref[:, _head_lanes(kvh * Q_PER_KV + g)] = o4[g * BLK:(g + 1) * BLK]

    kv_meta = pl.BlockSpec((BLK, D_KV), lambda i: (0, 0))
    kv_prev = pl.BlockSpec((BLK, D_KV), lambda i: (jnp.maximum(i - 1, 0), 0))
    kv_cur = pl.BlockSpec((BLK, D_KV), lambda i: (i, 0))
    return _pcall(
        body, name=f"attn_fwd_l{layer}", grid=(n_blk,),
        in_specs=[pl.BlockSpec(memory_space=pltpu.SMEM),
                  pl.BlockSpec((BLK, D_ATTN), lambda i: (i, 0)),
                  kv_meta, kv_prev, kv_cur, kv_meta, kv_prev, kv_cur],
        out_specs=pl.BlockSpec((BLK, D_ATTN), lambda i: (i, 0)),
        out_shape=SDS((rows, D_ATTN), MXU_DTYPE),
        compiler_params=_cparams("parallel"),
    )(sinks, q, k, k, k, v, v, v)


def _merge_fwd(y_ssm, y_attn, gates, hres, w_o_ssm, w_o_attn, w_out, gain3, layer):
    rows = hres.shape[0]
    tm = _row_tile(rows, 320)

    def body(ys_ref, ya_ref, gs_ref, ga_ref, x_ref, wos_ref, woa_ref, wout_ref, g_ref,
             mg_ref, mix_ref, out_ref):
        a1 = _dot(ys_ref[...], wos_ref[...])
        a2 = _dot(ya_ref[...], woa_ref[...])
        merged = (_sigmoid(gs_ref[...]) * a1 + _sigmoid(ga_ref[...]) * a2).astype(MXU_DTYPE)
        mg_ref[...] = merged
        mix = _dot(merged, wout_ref[...])
        mix_ref[...] = mix
        out_ref[...] = x_ref[...] + _rms_fwd(mix, g_ref[...])

    row_d = pl.BlockSpec((tm, D), lambda i: (i, 0))
    full = lambda shape: pl.BlockSpec(shape, lambda i: (0,) * len(shape))
    return _pcall(
        body, name=f"merge_fwd_l{layer}", grid=(rows // tm,),
        in_specs=[pl.BlockSpec((tm, D_SSM), lambda i: (i, 0)), row_d,
                  row_d, pl.BlockSpec((tm, D), lambda i: (i, 1)), row_d,
                  full((D_SSM, D)), full((D_ATTN, D)), full((D, D)),
                  pl.BlockSpec((None, 1, D), lambda i: (layer, 0, 0))],
        out_specs=[row_d, row_d, row_d],
        out_shape=[SDS((rows, D), MXU_DTYPE), SDS((rows, D), F32), SDS((rows, D), F32)],
        compiler_params=_cparams("parallel"),
    )(y_ssm, y_attn, gates, gates, hres, w_o_ssm, w_o_attn, w_out, gain3)


def _mlp_fwd(hres, gain_pre3, gain_post3, w_up_g, w_down_g, layer, target=None):
    rows = hres.shape[0]
    tm = _row_tile(rows, 320)

    def body(x_ref, gp_ref, gq_ref, wu_hbm, wd_hbm, *refs):
        if target is None:
            up_ref, h_ref, ff_ref, out_ref, act_scr, wu_scr, wd_scr, wu_sem, wd_sem = refs
        else:
            t_ref, up_ref, h_ref, ff_ref, out_ref, loss_ref, act_scr, wu_scr, wd_scr, wu_sem, wd_sem = refs
        first = pl.program_id(0) == 0
        _load_resident(wu_hbm, wu_scr, wu_sem, first)
        _load_resident(wd_hbm, wd_scr, wd_sem, first)
        hn = _rms_fwd(x_ref[...], gp_ref[...]).astype(MXU_DTYPE)
        h_ref[...] = hn
        for kf in range(N_DEV):
            cols = slice(kf * COL_SHARD, (kf + 1) * COL_SHARD)
            up = _dot(hn, wu_scr[kf])
            up_ref[:, cols] = up.astype(MXU_DTYPE)
            r = jnp.maximum(up, 0.0)
            act_scr[:, cols] = (r * r).astype(MXU_DTYPE)
        ff = _dot(act_scr[...], wd_scr[...].reshape(D_FF, D))
        ff_ref[...] = ff
        out = x_ref[...] + _rms_fwd(ff, gq_ref[...])
        if target is None:
            out_ref[...] = out
        else:
            @pl.when(first)
            def _():
                loss_ref[...] = jnp.zeros_like(loss_ref)

            row = lax.broadcasted_iota(jnp.int32, (tm, D), 0) + pl.program_id(0) * tm
            err = jnp.where(row >= BLK, out - t_ref[...], 0.0)
            out_ref[...] = err * (1.0 / D)
            loss_ref[...] += jnp.sum(err * err) * (0.5 / D)

    row_d = pl.BlockSpec((tm, D), lambda i: (i, 0))
    gain = pl.BlockSpec((None, 1, D), lambda i: (layer, 0, 0))
    with_loss = target is not None
    return _pcall(
        body, name=f"mlp_fwd_l{layer}", grid=(rows // tm,),
        in_specs=[row_d, gain, gain, pl.BlockSpec(memory_space=pl.ANY), pl.BlockSpec(memory_space=pl.ANY)]
        + [row_d] * with_loss,
        out_specs=[pl.BlockSpec((tm, D_FF), lambda i: (i, 0)), row_d, row_d, row_d]
        + [pl.BlockSpec((1, 128), lambda i: (0, 0))] * with_loss,
        out_shape=[SDS((rows, D_FF), MXU_DTYPE), SDS((rows, D), MXU_DTYPE), SDS((rows, D), F32), SDS((rows, D), F32)]
        + [SDS((1, 128), F32)] * with_loss,
        scratch_shapes=[pltpu.VMEM((tm, D_FF), MXU_DTYPE),
                        pltpu.VMEM((N_DEV, D, COL_SHARD), MXU_DTYPE), pltpu.VMEM((N_DEV, COL_SHARD, D), MXU_DTYPE),
                        pltpu.SemaphoreType.DMA((N_DEV,)), pltpu.SemaphoreType.DMA((N_DEV,))],
        compiler_params=_cparams("arbitrary"),
    )(hres, gain_pre3, gain_post3, w_up_g, w_down_g, *([target] if with_loss else []))


def _relu_squared(up):
    r = jnp.maximum(up.astype(F32), 0.0)
    return (r * r).astype(MXU_DTYPE)


def _matmul_tn(a, b, name, dev_major_cols=None, a_fn=None):
    rows, ka = a.shape
    n = b.shape[1]
    ta = min(ka, 1024)
    tn = 1024 if n % 1024 == 0 else 512
    tr = _row_tile(rows, 1664)
    n_r = rows // tr

    def body(a_ref, b_ref, o_ref, acc):
        r = pl.program_id(2)

        @pl.when(r == 0)
        def _():
            acc[...] = jnp.zeros_like(acc)

        a_blk = a_ref[...] if a_fn is None else a_fn(a_ref[...])
        acc[...] += _dot_tn(a_blk, b_ref[...])

        @pl.when(r == n_r - 1)
        def _():
            if dev_major_cols is None:
                o_ref[...] = acc[...].astype(XFER_DTYPE)
            else:
                for s in range(tn // dev_major_cols):
                    o_ref[s] = acc[:, s * dev_major_cols:(s + 1) * dev_major_cols].astype(XFER_DTYPE)

    if dev_major_cols is None:
        out_spec = pl.BlockSpec((ta, tn), lambda i, j, r: (i, j))
        out_shape = SDS((ka, n), XFER_DTYPE)
    else:
        w = dev_major_cols
        out_spec = pl.BlockSpec((tn // w, ta, w), lambda i, j, r: (j, i, 0))
        out_shape = SDS((n // w, ka, w), XFER_DTYPE)
    return _pcall(
        body, name=name, grid=(ka // ta, n // tn, n_r),
        in_specs=[pl.BlockSpec((tr, ta), lambda i, j, r: (r, i)), pl.BlockSpec((tr, tn), lambda i, j, r: (r, j))],
        out_specs=out_spec, out_shape=out_shape,
        scratch_shapes=[pltpu.VMEM((ta, tn), F32)],
        compiler_params=_cparams("parallel", "parallel", "arbitrary"),
    )(a, b)


def _dw_in(h, dproj_pieces, layer):
    rows = h.shape[0]
    tr = _row_tile(rows, 1664)
    n_r = rows // tr

    def body(h_ref, *refs):
        piece_refs, (o_ref, acc) = refs[:len(DPROJ_PIECES)], refs[len(DPROJ_PIECES):]
        j = pl.program_id(0)
        r = pl.program_id(1)

        @pl.when(r == 0)
        def _():
            acc[...] = jnp.zeros_like(acc)

        for piece_ref, (first, count) in zip(piece_refs, DPROJ_PIECES):
            @pl.when((j >= first) & (j < first + count))
            def _():
                acc[...] += _dot_tn(h_ref[...], piece_ref[...])

        @pl.when(r == n_r - 1)
        def _():
            o_ref[...] = acc[...].astype(XFER_DTYPE)

    def piece_spec(first, count):
        def index(j, r):
            mine = (j >= first) & (j < first + count)
            return jnp.where(mine, r, 0), jnp.clip(j - first, 0, count - 1)
        return pl.BlockSpec((tr, COL_SHARD), index)

    return _pcall(
        body, name=f"dw_in_l{layer}", grid=(N_DEV, n_r),
        in_specs=[pl.BlockSpec((tr, D), lambda j, r: (r, 0))] + [piece_spec(*p) for p in DPROJ_PIECES],
        out_specs=pl.BlockSpec((None, D, COL_SHARD), lambda j, r: (j, 0, 0)),
        out_shape=SDS((N_DEV, D, COL_SHARD), XFER_DTYPE),
        scratch_shapes=[pltpu.VMEM((D, COL_SHARD), F32)],
        compiler_params=_cparams("arbitrary", "arbitrary"),
    )(h, *dproj_pieces)


def _mlp_bwd(dout, ff, up, hres_mid, gain_pre3, gain_post3, w_up_g, w_down_g, layer):
    rows = dout.shape[0]
    tm = _row_tile(rows, 320)

    def body(do_ref, ff_ref, up_ref, x_ref, gp_ref, gq_ref, wu_hbm, wd_hbm,
             dff_ref, dup_ref, dx_ref, dgq_ref, dgp_ref, wut_scr, wdt_scr, wu_stage, wd_stage, wu_sem, wd_sem):
        i = pl.program_id(0)
        _load_resident_transposed(wu_hbm, wut_scr, wu_stage, wu_sem, i == 0)
        _load_resident_transposed(wd_hbm, wdt_scr, wd_stage, wd_sem, i == 0)

        @pl.when(i == 0)
        def _():
            dgq_ref[...] = jnp.zeros_like(dgq_ref)
            dgp_ref[...] = jnp.zeros_like(dgp_ref)

        dff, dg = _rms_bwd(ff_ref[...], gq_ref[...], do_ref[...])
        dgq_ref[...] += dg
        dffb = dff.astype(MXU_DTYPE)
        dff_ref[...] = dffb
        for kf in range(N_DEV):
            cols = slice(kf * COL_SHARD, (kf + 1) * COL_SHARD)
            dact = _dot(dffb, wdt_scr[kf])
            dup_ref[:, cols] = (dact * (2.0 * jnp.maximum(up_ref[:, cols].astype(F32), 0.0))).astype(MXU_DTYPE)
        dh = _dot(dup_ref[...], wut_scr[...].reshape(D_FF, D))
        dx, dg = _rms_bwd(x_ref[...], gp_ref[...], dh)
        dgp_ref[...] += dg
        dx_ref[...] = do_ref[...] + dx

    row_d = pl.BlockSpec((tm, D), lambda i: (i, 0))
    row_ff = pl.BlockSpec((tm, D_FF), lambda i: (i, 0))
    gain = pl.BlockSpec((None, 1, D), lambda i: (layer, 0, 0))
    dgain = pl.BlockSpec((1, D), lambda i: (0, 0))
    return _pcall(
        body, name=f"mlp_bwd_l{layer}", grid=(rows // tm,),
        in_specs=[row_d, row_d, row_ff, row_d, gain, gain,
                  pl.BlockSpec(memory_space=pl.ANY), pl.BlockSpec(memory_space=pl.ANY)],
        out_specs=[row_d, row_ff, row_d, dgain, dgain],
        out_shape=[SDS((rows, D), MXU_DTYPE), SDS((rows, D_FF), MXU_DTYPE), SDS((rows, D), F32),
                   SDS((1, D), F32), SDS((1, D), F32)],
        scratch_shapes=[pltpu.VMEM((N_DEV, COL_SHARD, D), MXU_DTYPE), pltpu.VMEM((N_DEV, D, COL_SHARD), MXU_DTYPE),
                        pltpu.VMEM((2, D, COL_SHARD), MXU_DTYPE), pltpu.VMEM((2, COL_SHARD, D), MXU_DTYPE),
                        pltpu.SemaphoreType.DMA((2,)), pltpu.SemaphoreType.DMA((2,))],
        compiler_params=_cparams("arbitrary"),
    )(dout, ff, up, hres_mid, gain_pre3, gain_post3, w_up_g, w_down_g)


def _merge_bwd(dhm, mix, y_ssm, y_attn, gates, w_o_ssm, w_o_attn, w_o_ssm_t, w_o_attn_t, w_out_t, gain3, layer):
    rows = dhm.shape[0]
    tm = _row_tile(rows, 320)

    def body(dh_ref, mix_ref, ys_ref, ya_ref, gs_ref, ga_ref, wos_ref, woa_ref, wost_ref, woat_ref, woutt_ref, g_ref,
             dmix_ref, da1_ref, da2_ref, dgs_ref, dga_ref, dys_ref, dya_ref, dg_ref):
        @pl.when(pl.program_id(0) == 0)
        def _():
            dg_ref[...] = jnp.zeros_like(dg_ref)

        dmix, dg = _rms_bwd(mix_ref[...], g_ref[...], dh_ref[...])
        dg_ref[...] += dg
        dmixb = dmix.astype(MXU_DTYPE)
        dmix_ref[...] = dmixb
        dmerged = _dot(dmixb, woutt_ref[...])
        sg_s = _sigmoid(gs_ref[...])
        sg_a = _sigmoid(ga_ref[...])
        da1 = (dmerged * sg_s).astype(MXU_DTYPE)
        da2 = (dmerged * sg_a).astype(MXU_DTYPE)
        da1_ref[...] = da1
        da2_ref[...] = da2
        a1 = _dot(ys_ref[...], wos_ref[...])
        a2 = _dot(ya_ref[...], woa_ref[...])
        dgs_ref[...] = (dmerged * a1 * (sg_s * (1.0 - sg_s))).astype(MXU_DTYPE)
        dga_ref[...] = (dmerged * a2 * (sg_a * (1.0 - sg_a))).astype(MXU_DTYPE)
        dys_ref[...] = _dot(da1, wost_ref[...])
        dya_ref[...] = _dot(da2, woat_ref[...])

    row_d = pl.BlockSpec((tm, D), lambda i: (i, 0))
    full = lambda shape: pl.BlockSpec(shape, lambda i: (0,) * len(shape))
    return _pcall(
        body, name=f"merge_bwd_l{layer}", grid=(rows // tm,),
        in_specs=[row_d, row_d, pl.BlockSpec((tm, D_SSM), lambda i: (i, 0)), row_d,
                  row_d, pl.BlockSpec((tm, D), lambda i: (i, 1)),
                  full((D_SSM, D)), full((D_ATTN, D)), full((D, D_SSM)), full((D, D_ATTN)), full((D, D)),
                  pl.BlockSpec((None, 1, D), lambda i: (layer, 0, 0))],
        out_specs=[row_d, row_d, row_d, row_d, row_d, pl.BlockSpec((tm, D_SSM), lambda i: (i, 0)), row_d,
                   pl.BlockSpec((1, D), lambda i: (0, 0))],
        out_shape=[SDS((rows, D), MXU_DTYPE)] * 5 + [SDS((rows, D_SSM), F32), SDS((rows, D_ATTN), F32),
                                                      SDS((1, D), F32)],
        compiler_params=_cparams("arbitrary"),
    )(dhm, mix, y_ssm, y_attn, gates, gates, w_o_ssm, w_o_attn, w_o_ssm_t, w_o_attn_t, w_out_t, gain3)


def _attn_bwd(q, k, v, d_out, sinks, cos, sin_a, sin_b, layer):
    rows = q.shape[0]
    n_blk = rows // BLK
    last = n_blk - 1

    def body(sink_ref, q_ref, km_ref, kp_ref, kc_ref, vm_ref, vp_ref, vc_ref, do_ref, c_ref, a_ref, b_ref,
             dqb_ref, dk_ref, dv_ref, dkm_ref, dvm_ref, ds_ref, dk_carry, dv_carry, dq_ref):
        i = pl.program_id(0)

        @pl.when(i == 0)
        def _():
            dkm_ref[...] = jnp.zeros_like(dkm_ref)
            dvm_ref[...] = jnp.zeros_like(dvm_ref)
            ds_ref[...] = jnp.zeros_like(ds_ref)
            dk_carry[...] = jnp.zeros_like(dk_carry)
            dv_carry[...] = jnp.zeros_like(dv_carry)

        @pl.when(i <= last)
        def _():
            bias = _attn_mask(i)
            for kvh in range(N_KV_HEADS):
                lanes = _head_lanes(kvh)
                k3 = jnp.concatenate([km_ref[:, lanes], kp_ref[:, lanes], kc_ref[:, lanes]], axis=0)
                v3 = jnp.concatenate([vm_ref[:, lanes], vp_ref[:, lanes], vc_ref[:, lanes]], axis=0)
                q4 = _group_rows(q_ref, kvh)
                do4 = _group_rows(do_ref, kvh).astype(MXU_DTYPE)
                p = _attn_probs(q4, k3, _group_bias(bias, sink_ref, layer, kvh))
                dp = _dot_nt(do4, v3)
                dsf = p * (dp - jnp.sum(dp * p, axis=-1, keepdims=True))
                dsc = dsf.astype(MXU_DTYPE)
                dv3 = _dot_tn(p.astype(MXU_DTYPE), do4)
                dk3 = _dot_tn(dsc, q4)
                dq4 = _dot(dsc, k3)
                for g in range(Q_PER_KV):
                    h = kvh * Q_PER_KV + g
                    dq_ref[:, _head_lanes(h)] = dq4[g * BLK:(g + 1) * BLK]
                    ds_ref[h:h + 1, :] += jnp.sum(dsf[g * BLK:(g + 1) * BLK, 0:BLK], axis=0, keepdims=True)
                dkm_ref[:, lanes] += dk3[0:BLK]
                dvm_ref[:, lanes] += dv3[0:BLK]
                dk_ref[:, lanes] = dk_carry[:, lanes] + dk3[BLK:2 * BLK]
                dv_ref[:, lanes] = dv_carry[:, lanes] + dv3[BLK:2 * BLK]
                dk_carry[:, lanes] = dk3[2 * BLK:3 * BLK]
                dv_carry[:, lanes] = dv3[2 * BLK:3 * BLK]
            c, a, b = c_ref[...], -a_ref[...], -b_ref[...]
            for t in range(D_ATTN // 128):
                lanes = slice(t * 128, (t + 1) * 128)
                dqb_ref[:, lanes] = (_rope_lanes(dq_ref[:, lanes], c, a, b) * ATTN_SCALE).astype(MXU_DTYPE)

        @pl.when(i == last + 1)
        def _():
            dk_ref[...] = dk_carry[...]
            dv_ref[...] = dv_carry[...]

    cur = lambda i: (jnp.minimum(i, last), 0)
    prev = lambda i: (jnp.clip(i - 1, 0, last), 0)
    kv_meta = pl.BlockSpec((BLK, D_KV), lambda i: (0, 0))
    kv_prev = pl.BlockSpec((BLK, D_KV), prev)
    kv_cur = pl.BlockSpec((BLK, D_KV), cur)
    tab = pl.BlockSpec((BLK, 128), cur)
    return _pcall(
        body, name=f"attn_bwd_l{layer}", grid=(n_blk + 1,),
        in_specs=[pl.BlockSpec(memory_space=pltpu.SMEM),
                  pl.BlockSpec((BLK, D_ATTN), cur),
                  kv_meta, kv_prev, kv_cur, kv_meta, kv_prev, kv_cur,
                  pl.BlockSpec((BLK, D_ATTN), cur), tab, tab, tab],
        out_specs=[pl.BlockSpec((BLK, D_ATTN), cur), kv_prev, kv_prev, kv_meta, kv_meta,
                   pl.BlockSpec((N_Q_HEADS, 128), lambda i: (0, 0))],
        out_shape=[SDS((rows, D_ATTN), MXU_DTYPE), SDS((rows, D_KV), F32), SDS((rows, D_KV), F32),
                   SDS((BLK, D_KV), F32), SDS((BLK, D_KV), F32), SDS((N_Q_HEADS, 128), F32)],
        scratch_shapes=[pltpu.VMEM((BLK, D_KV), F32), pltpu.VMEM((BLK, D_KV), F32), pltpu.VMEM((BLK, D_ATTN), F32)],
        compiler_params=_cparams("arbitrary"),
    )(sinks, q, k, k, k, v, v, v, d_out, cos, sin_a, sin_b)


def _rope_bwd(dk, dv, dk_meta, dv_meta, cos, sin_a, sin_b, layer):
    rows = dk.shape[0]
    tm = _row_tile(rows)

    def body(dk_ref, dv_ref, dkm_ref, dvm_ref, c_ref, a_ref, b_ref, o_ref):
        c, a, b = c_ref[...], -a_ref[...], -b_ref[...]
        for t in range(2):
            x = dk_ref[:, t * 128:(t + 1) * 128]
            o_ref[:, t * 128:(t + 1) * 128] = _rope_lanes(x, c, a, b).astype(MXU_DTYPE)
        o_ref[:, D_KV:] = dv_ref[...].astype(MXU_DTYPE)

        @pl.when(pl.program_id(0) == 0)
        def _():
            cb, ab, bb = c[0:BLK], a[0:BLK], b[0:BLK]
            is_meta = lax.broadcasted_iota(jnp.int32, (BLK, 128), 0) >= PAD_ROWS
            for t in range(2):
                x = dk_ref[0:BLK, t * 128:(t + 1) * 128] + jnp.where(is_meta, dkm_ref[:, t * 128:(t + 1) * 128], 0.0)
                o_ref[0:BLK, t * 128:(t + 1) * 128] = _rope_lanes(x, cb, ab, bb).astype(MXU_DTYPE)
                xv = dv_ref[0:BLK, t * 128:(t + 1) * 128] + jnp.where(is_meta, dvm_ref[:, t * 128:(t + 1) * 128], 0.0)
                o_ref[0:BLK, D_KV + t * 128:D_KV + (t + 1) * 128] = xv.astype(MXU_DTYPE)

    tab = pl.BlockSpec((tm, 128), lambda i: (i, 0))
    kv = pl.BlockSpec((tm, D_KV), lambda i: (i, 0))
    meta = pl.BlockSpec((BLK, D_KV), lambda i: (0, 0))
    return _pcall(
        body, name=f"rope_bwd_l{layer}", grid=(rows // tm,),
        in_specs=[kv, kv, meta, meta, tab, tab, tab],
        out_specs=pl.BlockSpec((tm, 2 * D_KV), lambda i: (i, 0)),
        out_shape=SDS((rows, 2 * D_KV), MXU_DTYPE),
        compiler_params=_cparams("parallel"),
    )(dk, dv, dk_meta, dv_meta, cos, sin_a, sin_b)


def _s5_bwd(d_gated, y, u, carry_in, ssm, w_glu, b_glu3, layer):
    rows = y.shape[0]
    n_chunks = rows // BLK
    b_mat, c_mat, t_re, t_im, d_skip = (ssm[k] for k in ("b_mat", "c_mat", "t_re", "t_im", "d_skip"))

    def body(dz_ref, y_ref, u_ref, cin_ref, bm_ref, cm_ref, tre_ref, tim_ref, d_ref, wg_ref, bg_ref,
             du_ref, dwg_ref, dbg_ref, dd_ref, dbm_ref, dcm_ref, dab_ref,
             lam_carry, bu_scr, s_scr, sp_scr, g_scr, lam_scr):
        step = pl.program_id(0)
        chunk = n_chunks - 1 - step

        @pl.when(step == 0)
        def _():
            for r in (dwg_ref, dbg_ref, dd_ref, dbm_ref, dcm_ref, dab_ref, lam_carry):
                r[...] = jnp.zeros_like(r)

        y = y_ref[...]
        u = u_ref[...]
        d_o = dz_ref[...]
        z, t = _gelu_parts(y)
        zb = z.astype(MXU_DTYPE)
        sg = _sigmoid(_dot(zb, wg_ref[...]) + bg_ref[...])
        dgl = d_o * z * (sg * (1.0 - sg))
        dglb = dgl.astype(MXU_DTYPE)
        dz = d_o * sg + _dot_nt(dglb, wg_ref[...])
        dwg_ref[...] += _dot_tn(zb, dglb)
        dbg_ref[...] += jnp.sum(dgl, axis=0, keepdims=True)
        dy = dz * _gelu_grad(y, t)
        dd_ref[...] += jnp.sum(dy * u, axis=0, keepdims=True)
        grow = lax.broadcasted_iota(jnp.int32, (BLK, 128), 0) + chunk * BLK
        ub = u.astype(MXU_DTYPE)
        dyb_all = dy.astype(MXU_DTYPE)
        for sb in range(N_SB):
            cols = slice(sb * 128, (sb + 1) * 128)
            bu_scr[sb] = _dot(ub[:, cols], bm_ref[sb])
            g_scr[sb] = _dot_nt(dyb_all[:, cols], cm_ref[sb])
        _scan_tiles(bu_scr, s_scr, tre_ref, tim_ref,
                    [(cin_ref[2 * sb:2 * sb + 1, :], cin_ref[2 * sb + 1:2 * sb + 2, :]) for sb in range(N_SB)],
                    False, prev_scr=sp_scr)
        leaving = _scan_tiles(g_scr, lam_scr, tre_ref, tim_ref,
                              [(lam_carry[2 * sb:2 * sb + 1, :], lam_carry[2 * sb + 1:2 * sb + 2, :])
                               for sb in range(N_SB)], True)
        for sb in range(N_SB):
            cols = slice(sb * 128, (sb + 1) * 128)
            u_sb = ub[:, cols]
            dy_sb = dy[:, cols]
            dyb = dyb_all[:, cols]
            lam_carry[2 * sb:2 * sb + 1, :], lam_carry[2 * sb + 1:2 * sb + 2, :] = leaving[sb]
            dcm_ref[sb] += _dot_tn(s_scr[sb].astype(MXU_DTYPE), dyb)
            lr, li = lam_scr[sb, :, :SB_STATES], lam_scr[sb, :, SB_STATES:]
            spr, spi = sp_scr[sb, :, :SB_STATES], sp_scr[sb, :, SB_STATES:]
            dab_ref[2 * sb:2 * sb + 1, :] += jnp.sum(spr * lr + spi * li, axis=0, keepdims=True)
            dab_ref[2 * sb + 1:2 * sb + 2, :] += jnp.sum(spr * li - spi * lr, axis=0, keepdims=True)
            lam = lam_scr[sb].astype(MXU_DTYPE)
            dbm_ref[sb] += _dot_tn(u_sb, lam)
            du = _dot_nt(lam, bm_ref[sb]) + d_ref[:, cols] * dy_sb
            du_ref[:, cols] = jnp.where(grow >= PAD_ROWS, du, 0.0).astype(MXU_DTYPE)

    rev = lambda j: (n_chunks - 1 - j, 0)
    full = lambda shape: pl.BlockSpec(shape, lambda j: (0,) * len(shape))
    of_layer = lambda shape: pl.BlockSpec((None,) + shape, lambda j: (layer,) + (0,) * len(shape))
    tables = [of_layer((N_SB, 8, SCAN_TILE, SB_STATES))] * 2
    chunk_scratch = pltpu.VMEM((N_SB, BLK, 2 * SB_STATES), F32)
    return _pcall(
        body, name=f"s5_bwd_l{layer}", grid=(n_chunks,),
        in_specs=[pl.BlockSpec((BLK, D_SSM), rev), pl.BlockSpec((BLK, D_SSM), rev), pl.BlockSpec((BLK, D_SSM), rev),
                  pl.BlockSpec((None, 8, SB_STATES), lambda j: (n_chunks - 1 - j, 0, 0)),
                  of_layer((N_SB, 128, 2 * SB_STATES)), of_layer((N_SB, 2 * SB_STATES, 128))] + tables + [
                  of_layer((1, D_SSM)), full((D_SSM, D_SSM)),
                  pl.BlockSpec((None, 1, D_SSM), lambda j: (layer, 0, 0))],
        out_specs=[pl.BlockSpec((BLK, D_SSM), rev), full((D_SSM, D_SSM)), full((1, D_SSM)), full((1, D_SSM)),
                   full((N_SB, 128, 2 * SB_STATES)), full((N_SB, 2 * SB_STATES, 128)), full((8, SB_STATES))],
        out_shape=[SDS((rows, D_SSM), MXU_DTYPE), SDS((D_SSM, D_SSM), F32), SDS((1, D_SSM), F32), SDS((1, D_SSM), F32),
                   SDS((N_SB, 128, 2 * SB_STATES), F32), SDS((N_SB, 2 * SB_STATES, 128), F32), SDS((8, SB_STATES), F32)],
        scratch_shapes=[pltpu.VMEM((8, SB_STATES), F32)] + [chunk_scratch] * 5,
        compiler_params=_cparams("arbitrary"),
    )(d_gated, y, u, carry_in, b_mat, c_mat, t_re, t_im, d_skip, w_glu, b_glu3)


DPROJ_PIECES = ((0, 1), (1, 2), (3, 1), (4, 2), (6, 2))


def _in_bwd(dproj_pieces, dhm, hres, gain3, w_in_g, layer):
    rows = hres.shape[0]
    tm = _row_tile(rows)

    def body(*refs):
        piece_refs = refs[:len(DPROJ_PIECES)]
        dh_ref, x_ref, g_ref, w_hbm, dx_ref, dg_ref, wt_scr, w_stage, w_sem = refs[len(DPROJ_PIECES):]
        i = pl.program_id(0)
        _load_resident_transposed(w_hbm, wt_scr, w_stage, w_sem, i == 0)

        @pl.when(i == 0)
        def _():
            dg_ref[...] = jnp.zeros_like(dg_ref)

        dh = None
        for piece_ref, (first, count) in zip(piece_refs, DPROJ_PIECES):
            wt = wt_scr[first:first + count].reshape(count * COL_SHARD, D)
            part = _dot(piece_ref[...], wt)
            dh = part if dh is None else dh + part
        dx, dg = _rms_bwd(x_ref[...], g_ref[...], dh)
        dg_ref[...] += dg
        dx_ref[...] = dh_ref[...] + dx

    row_d = pl.BlockSpec((tm, D), lambda i: (i, 0))
    return _pcall(
        body, name=f"in_bwd_l{layer}", grid=(rows // tm,),
        in_specs=[pl.BlockSpec((tm, count * COL_SHARD), lambda i: (i, 0)) for _, count in DPROJ_PIECES] + [
                  row_d, row_d,
                  pl.BlockSpec((None, 1, D), lambda i: (layer, 0, 0)),
                  pl.BlockSpec(memory_space=pl.ANY)],
        out_specs=[row_d, pl.BlockSpec((1, D), lambda i: (0, 0))],
        out_shape=[SDS((rows, D), F32), SDS((1, D), F32)],
        scratch_shapes=[pltpu.VMEM((N_DEV, COL_SHARD, D), MXU_DTYPE),
                        pltpu.VMEM((2, D, COL_SHARD), MXU_DTYPE), pltpu.SemaphoreType.DMA((2,))],
        compiler_params=_cparams("arbitrary"),
    )(*dproj_pieces, dhm, hres, gain3, w_in_g)


_ADAM_C1 = 1.0 / (1.0 - ADAM_B1 ** ADAM_STEP)
_ADAM_C2 = 1.0 / (1.0 - ADAM_B2 ** ADAM_STEP)


def _adam_math(w, g, m, v):
    m = ADAM_B1 * m + (1.0 - ADAM_B1) * g
    v = ADAM_B2 * v + (1.0 - ADAM_B2) * (g * g)
    delta = -ADAM_LR * ((m * _ADAM_C1) / (jnp.sqrt(v * _ADAM_C2) + ADAM_EPS) + ADAM_WD * w)
    return delta, m, v


def _adamw_layers(parts0, parts1, w, m, v, name):
    _, rows, cols = w.shape
    tr = min(rows, (1 << 16) // cols)
    nt = rows // tr

    def body(p0_ref, p1_ref, w_ref, m_ref, v_ref, g_ref, d_ref, nm_ref, nv_ref):
        layer = pl.program_id(0)

        def run(p_ref):
            g = p_ref[0].astype(F32)
            for s in range(1, N_DEV):
                g = g + p_ref[s].astype(F32)
            delta, nm, nv = _adam_math(w_ref[...], g, m_ref[...], v_ref[...])
            g_ref[...] = g
            d_ref[...] = delta
            nm_ref[...] = nm
            nv_ref[...] = nv

        @pl.when(layer == 0)
        def _():
            run(p0_ref)

        @pl.when(layer == 1)
        def _():
            run(p1_ref)

    wspec = pl.BlockSpec((None, tr, cols), lambda l, i: (l, i, 0))
    return _pcall(
        body, name=name, grid=(2, nt),
        in_specs=[pl.BlockSpec((N_DEV, tr, cols), lambda l, i: (0, jnp.where(l == 0, i, nt - 1), 0)),
                  pl.BlockSpec((N_DEV, tr, cols), lambda l, i: (0, jnp.where(l == 1, i, 0), 0)),
                  wspec, wspec, wspec],
        out_specs=[wspec] * 4, out_shape=[SDS(w.shape, F32)] * 4,
        compiler_params=_cparams("arbitrary", "arbitrary"),
    )(parts0, parts1, w, m, v)


def _sum_slots(parts, name):
    def body(p_ref, o_ref):
        acc = p_ref[0]
        for s in range(1, N_DEV):
            acc = acc + p_ref[s]
        o_ref[...] = acc

    vmem = pl.BlockSpec(memory_space=pltpu.VMEM)
    return _pcall(body, name=name, out_shape=SDS(parts.shape[1:], F32), in_specs=[vmem], out_specs=vmem,
                  compiler_params=_cparams())(parts)


def _adamw_packed(g, w, m, v, name):
    def body(g_ref, w_ref, m_ref, v_ref, d_ref, nm_ref, nv_ref):
        delta, nm, nv = _adam_math(w_ref[...], g_ref[...], m_ref[...], v_ref[...])
        d_ref[...] = delta
        nm_ref[...] = nm
        nv_ref[...] = nv

    vmem = pl.BlockSpec(memory_space=pltpu.VMEM)
    return _pcall(body, name=name, out_shape=[SDS(g.shape, F32)] * 3, in_specs=[vmem] * 4, out_specs=[vmem] * 3,
                  compiler_params=_cparams())(g, w, m, v)


def _ssm_discretize(a_re, a_im, log_dt, b_re, b_im):
    dt = jnp.exp(log_dt)[:, None]
    mag = jnp.exp(a_re * dt)
    ang = a_im * dt
    ab_re, ab_im = mag * jnp.cos(ang), mag * jnp.sin(ang)
    xr, xi = ab_re - 1.0, ab_im
    den = a_re * a_re + a_im * a_im
    q_re = (xr * a_re + xi * a_im) / den
    q_im = (xi * a_re - xr * a_im) / den
    bb_re = q_re[..., None] * b_re - q_im[..., None] * b_im
    bb_im = q_re[..., None] * b_im + q_im[..., None] * b_re
    return ab_re, ab_im, bb_re, bb_im


def _block_diag_b(bb):
    m = jnp.einsum("sgnc,gh->sgchn", bb.reshape(N_SB, 8, N_STATE, GROUP_CH), jnp.eye(8, dtype=F32))
    return m.reshape(N_SB, 128, SB_STATES)


def _block_diag_b_t(dm):
    return jnp.einsum("sgchn,gh->sgnc", dm.reshape(N_SB, 8, GROUP_CH, 8, N_STATE),
                      jnp.eye(8, dtype=F32)).reshape(N_GROUPS, N_STATE, GROUP_CH)


def _block_diag_c(cc):
    m = jnp.einsum("sgcn,gh->sgnhc", cc.reshape(N_SB, 8, GROUP_CH, N_STATE), jnp.eye(8, dtype=F32))
    return m.reshape(N_SB, SB_STATES, 128)


def _block_diag_c_t(dm):
    return jnp.einsum("sgnhc,gh->sgcn", dm.reshape(N_SB, 8, N_STATE, 8, GROUP_CH),
                      jnp.eye(8, dtype=F32)).reshape(N_GROUPS, GROUP_CH, N_STATE)


def _ssm_tables(ab_re, ab_im, bb_re, bb_im, c_re, c_im, d_skip):
    pr, pi = ab_re.reshape(1, -1), ab_im.reshape(1, -1)
    cr, ci = pr, pi
    squares = []
    for _ in range(3):
        squares.append((cr, ci))
        pr, pi = (jnp.concatenate([pr, pr * cr - pi * ci], axis=0),
                  jnp.concatenate([pi, pr * ci + pi * cr], axis=0))
        cr, ci = cr * cr - ci * ci, 2.0 * cr * ci
    r = jnp.arange(SCAN_TILE)[:, None]
    fwd = [(jnp.where(r >= (1 << k), squares[k][0], 0.0), jnp.where(r >= (1 << k), squares[k][1], 0.0))
           for k in range(3)] + [(pr, pi)]
    rev = [(jnp.where(r < SCAN_TILE - (1 << k), squares[k][0], 0.0),
            jnp.where(r < SCAN_TILE - (1 << k), -squares[k][1], 0.0)) for k in range(3)] + [(pr[::-1], -pi[::-1])]
    table = lambda part: jnp.stack([e[part] for e in fwd + rev]).reshape(
        8, SCAN_TILE, N_SB, SB_STATES).transpose(2, 0, 1, 3)
    return dict(
        b_mat=jnp.concatenate([_block_diag_b(bb_re), _block_diag_b(bb_im)], axis=-1).astype(MXU_DTYPE),
        c_mat=jnp.concatenate([_block_diag_c(c_re), -_block_diag_c(c_im)], axis=1).astype(MXU_DTYPE),
        t_re=table(0), t_im=table(1),
        d_skip=d_skip.reshape(1, D_SSM))


def _rope_tables(rows):
    pos = (jnp.arange(rows, dtype=jnp.int32) - PAD_ROWS).astype(F32)
    inv_freq = 1.0 / (ROPE_THETA ** (jnp.arange(0, HEAD_DIM, 2, dtype=F32) / HEAD_DIM))
    ang = pos[:, None] * inv_freq[None, :]
    ang = jnp.concatenate([ang, ang, ang, ang], axis=-1)
    first_half = (jnp.arange(128) % HEAD_DIM) < HEAD_DIM // 2
    sin = jnp.sin(ang)
    return jnp.cos(ang), jnp.where(first_half, -sin, 0.0), jnp.where(first_half, 0.0, sin)


def _pack(arrays):
    flat = jnp.concatenate([a.reshape(-1).astype(F32) for a in arrays])
    pad = (-flat.shape[0]) % 1024
    return jnp.pad(flat, (0, pad)).reshape(-1, 128)


def _unpack(packed, like):
    flat = packed.reshape(-1)
    out, off = [], 0
    for a in like:
        n = math.prod(a.shape)
        out.append(flat[off:off + n].reshape(a.shape))
        off += n
    return out


BIG = ("w_in", "w_glu", "w_o_ssm", "w_o_attn", "w_out", "w_up", "w_down")
WEIGHTS = ("meta_tokens", "norm_mix_pre", "norm_mix_post", "norm_mlp_pre", "norm_mlp_post", "w_in",
           "ssm_a_re", "ssm_a_im", "ssm_log_dt", "ssm_b_re", "ssm_b_im", "ssm_c_re", "ssm_c_im", "ssm_d",
           "w_glu", "b_glu", "attn_sinks", "w_o_ssm", "w_o_attn", "w_out", "w_up", "w_down")
SMALL = tuple(n for n in WEIGHTS if n not in BIG)


def kernel(x, meta_tokens, norm_mix_pre, norm_mix_post, norm_mlp_pre, norm_mlp_post, w_in, ssm_a_re, ssm_a_im, ssm_log_dt, ssm_b_re, ssm_b_im, ssm_c_re, ssm_c_im, ssm_d, w_glu, b_glu, attn_sinks, w_o_ssm, w_o_attn, w_out, w_up, w_down, loss_target, m_meta_tokens, m_norm_mix_pre, m_norm_mix_post, m_norm_mlp_pre, m_norm_mlp_post, m_w_in, m_ssm_a_re, m_ssm_a_im, m_ssm_log_dt, m_ssm_b_re, m_ssm_b_im, m_ssm_c_re, m_ssm_c_im, m_ssm_d, m_w_glu, m_b_glu, m_attn_sinks, m_w_o_ssm, m_w_o_attn, m_w_out, m_w_up, m_w_down, v_meta_tokens, v_norm_mix_pre, v_norm_mix_post, v_norm_mlp_pre, v_norm_mlp_post, v_w_in, v_ssm_a_re, v_ssm_a_im, v_ssm_log_dt, v_ssm_b_re, v_ssm_b_im, v_ssm_c_re, v_ssm_c_im, v_ssm_d, v_w_glu, v_b_glu, v_attn_sinks, v_w_o_ssm, v_w_o_attn, v_w_out, v_w_up, v_w_down):
    args = locals()
    w = {n: args[n] for n in WEIGHTS}
    m = {n: args["m_" + n] for n in WEIGHTS}
    v = {n: args["v_" + n] for n in WEIGHTS}
    n_layers = w_in.shape[0]
    seq = x.shape[1]
    rows = seq + BLK
    my_slot = _slot(_mesh_pos())

    assert n_layers == 2
    xfer = {n: [w[n][l].astype(XFER_DTYPE) for l in range(n_layers)] for n in BIG}
    mixer_small = ("w_glu", "w_o_ssm", "w_o_attn", "w_out")
    meta_g, w_in_g0 = _exchange_by_sequencer([meta_tokens, xfer["w_in"][0]], True, 0, "gather_in0")
    mix0_g = _exchange_by_sequencer([xfer[n][0] for n in mixer_small], True, 1, "gather_mix0")
    meta_full = meta_g.transpose(1, 0, 2).reshape(N_META, D)

    def mixer_weights(w_glu_g, w_o_ssm_g, w_o_attn_g, w_out_g):
        return dict(w_glu=w_glu_g.reshape(D_SSM, D_SSM), w_o_ssm=w_o_ssm_g.transpose(1, 0, 2).reshape(D_SSM, D),
                    w_o_attn=w_o_attn_g.reshape(D_ATTN, D), w_out=w_out_g.reshape(D, D),
                    w_o_ssm_t=w_o_ssm_g.transpose(0, 2, 1).reshape(D, D_SSM),
                    w_o_attn_t=w_o_attn_g.reshape(D_ATTN, D).T, w_out_t=w_out_g.reshape(D, D).T)

    gathered = [dict(w_in=w_in_g0, **mixer_weights(*mix0_g)), {}]

    gains = {n: w[n].reshape(n_layers, 1, D) for n in ("norm_mix_pre", "norm_mix_post", "norm_mlp_pre", "norm_mlp_post")}
    b_glu3 = b_glu.reshape(n_layers, 1, D_SSM)
    cos, sin_a, sin_b = _rope_tables(rows)

    disc, disc_vjp = jax.vjp(jax.vmap(_ssm_discretize), ssm_a_re, ssm_a_im, ssm_log_dt, ssm_b_re, ssm_b_im)
    ssm = jax.vmap(_ssm_tables)(*disc, ssm_c_re, ssm_c_im, ssm_d)

    hres = jnp.concatenate([jnp.zeros((PAD_ROWS, D), F32), meta_full, x[0]], axis=0)

    saved = []
    for l in range(n_layers):
        wl = gathered[l]
        u, gates, q, k, vv, h = _in_proj(hres, gains["norm_mix_pre"], wl["w_in"], cos, sin_a, sin_b, l,
                                         after=[ssm["b_mat"], ssm["c_mat"], ssm["t_re"], ssm["t_im"]] if l == 0 else ())
        if l == 0:
            wl["w_up"], wl["w_down"] = _exchange_by_sequencer([xfer["w_up"][0], xfer["w_down"][0]], True, 2,
                                                              "gather_mlp0", after=[h])
        y, y_ssm, carry_in = _s5_fwd(u, ssm, wl["w_glu"], b_glu3, l)
        if l == 0:
            l1_g = _exchange_by_sequencer([xfer[n][1] for n in ("w_in",) + mixer_small + ("w_up", "w_down")], True, 3,
                                          "gather_l1", after=[y, wl["w_up"]])
            gathered[1] = dict(w_in=l1_g[0], w_up=l1_g[5], w_down=l1_g[6], **mixer_weights(*l1_g[1:5]))
            last_exchange = l1_g[:1]
        y_attn = _attn_fwd(q, k, vv, attn_sinks, l)
        merged, mix, hres_mid = _merge_fwd(y_ssm, y_attn, gates, hres, wl["w_o_ssm"], wl["w_o_attn"], wl["w_out"],
                                           gains["norm_mix_post"], l)
        hres_in = hres
        if l + 1 < n_layers:
            up, h2, ff, hres = _mlp_fwd(hres_mid, gains["norm_mlp_pre"], gains["norm_mlp_post"], wl["w_up"],
                                        wl["w_down"], l)
        else:
            target = jnp.concatenate([jnp.zeros((BLK, D), F32), loss_target[0]], axis=0)
            up, h2, ff, dhres, loss_vec = _mlp_fwd(hres_mid, gains["norm_mlp_pre"], gains["norm_mlp_post"], wl["w_up"],
                                                   wl["w_down"], l, target=target)
        saved.append(dict(hres=hres_in, u=u, gates=gates, h=h, q=q, k=k, v=vv, y=y, y_ssm=y_ssm,
                          carry_in=carry_in, y_attn=y_attn, merged=merged, mix=mix, hres_mid=hres_mid,
                          up=up, h2=h2, ff=ff))

    small_grads = {}
    recv_up, recv_down, recv_mix = [None] * n_layers, [None] * n_layers, [None] * n_layers
    for l in reversed(range(n_layers)):
        s = saved[l]
        wl = gathered[l]
        dff, dup, dhm, dg_mlp_post, dg_mlp_pre = _mlp_bwd(dhres, s["ff"], s["up"], s["hres_mid"], gains["norm_mlp_pre"],
                                                          gains["norm_mlp_post"], wl["w_up"], wl["w_down"], l)
        dw_up = _matmul_tn(s["h2"], dup, f"dw_up_l{l}", dev_major_cols=COL_SHARD)
        recv_up[l] = _exchange_by_sequencer([dw_up], False, 4 + 3 * l, f"scatter_up{l}", after=last_exchange)
        dw_down = _matmul_tn(s["up"], dff, f"dw_down_l{l}", a_fn=_relu_squared).reshape(N_DEV, COL_SHARD, D)
        recv_down[l] = _exchange_by_sequencer([dw_down], False, 5 + 3 * l, f"scatter_down{l}", after=recv_up[l])
        last_exchange = recv_down[l]
        dmix, da1, da2, dgs, dga, dy_ssm, dy_attn, dg_mix_post = _merge_bwd(
            dhm, s["mix"], s["y_ssm"], s["y_attn"], s["gates"], wl["w_o_ssm"], wl["w_o_attn"], wl["w_o_ssm_t"],
            wl["w_o_attn_t"], wl["w_out_t"], gains["norm_mix_post"], l)
        dw_out = _matmul_tn(s["merged"], dmix, f"dw_out_l{l}").reshape(N_DEV, D // N_DEV, D)
        dw_o_attn = _matmul_tn(s["y_attn"], da2, f"dw_o_attn_l{l}").reshape(N_DEV, D_ATTN // N_DEV, D)
        dw_o_ssm = _matmul_tn(s["y_ssm"], da1, f"dw_o_ssm_l{l}", dev_major_cols=D // N_DEV)
        if l == 0:
            recv_out0 = _exchange_by_sequencer([dw_o_ssm, dw_o_attn, dw_out], False, 11, "scatter_out0",
                                               after=last_exchange)
            last_exchange = recv_out0[:1]
        dq, dk, dv, dk_meta, dv_meta, dsink = _attn_bwd(s["q"], s["k"], s["v"], dy_attn, attn_sinks, cos, sin_a, sin_b, l)
        dkv = _rope_bwd(dk, dv, dk_meta, dv_meta, cos, sin_a, sin_b, l)
        du, dw_glu, db_glu, dd_skip, db_mat, dc_mat, dab = _s5_bwd(dy_ssm, s["y"], s["u"], s["carry_in"], ssm,
                                                                    wl["w_glu"], b_glu3, l)
        dproj = (du, dq, dkv, dgs, dga)
        dw_in = _dw_in(s["h"], dproj, l)
        mix_parts = [dw_in, dw_glu.astype(XFER_DTYPE).reshape(N_DEV, D_SSM // N_DEV, D_SSM), dw_o_ssm, dw_o_attn, dw_out]
        if l > 0:
            recv_mix[l] = _exchange_by_sequencer(mix_parts, False, 6 + 3 * l, f"scatter_mix{l}", after=last_exchange)
            last_exchange = recv_mix[l][:1]
        else:
            recv_mix[0] = _exchange_by_sequencer(mix_parts[:2], False, 6, "scatter_in0", after=last_exchange) + recv_out0
            last_exchange = recv_mix[0][:1]
        dhres, dg_mix_pre = _in_bwd(dproj, dhm, s["hres"], gains["norm_mix_pre"], wl["w_in"], l)

        for name, val in (("norm_mix_pre", dg_mix_pre[0]), ("norm_mix_post", dg_mix_post[0]),
                          ("norm_mlp_pre", dg_mlp_pre[0]), ("norm_mlp_post", dg_mlp_post[0]),
                          ("dab", dab), ("db_mat", db_mat), ("dc_mat", dc_mat),
                          ("ssm_d", dd_skip.reshape(N_GROUPS, GROUP_CH)), ("b_glu", db_glu[0]),
                          ("attn_sinks", dsink[:, 0])):
            small_grads.setdefault(name, [None] * n_layers)[l] = val

    grad_x = dhres[BLK:][None]
    stacked = {n: jnp.stack(v) for n, v in small_grads.items()}
    dab = stacked["dab"].reshape(n_layers, N_SB, 2, SB_STATES)
    db_mat, dc_mat = stacked["db_mat"], stacked["dc_mat"]
    b_t, c_t = jax.vmap(_block_diag_b_t), jax.vmap(_block_diag_c_t)
    (stacked["ssm_a_re"], stacked["ssm_a_im"], stacked["ssm_log_dt"], stacked["ssm_b_re"],
     stacked["ssm_b_im"]) = disc_vjp((dab[:, :, 0].reshape(n_layers, N_GROUPS, N_STATE),
                                      dab[:, :, 1].reshape(n_layers, N_GROUPS, N_STATE),
                                      b_t(db_mat[..., :SB_STATES]), b_t(db_mat[..., SB_STATES:])))
    stacked["ssm_c_re"] = c_t(dc_mat[:, :, :SB_STATES])
    stacked["ssm_c_im"] = -c_t(dc_mat[:, :, SB_STATES:])
    small_names = [n for n in SMALL if n != "meta_tokens"]
    partial_small = [dhres[PAD_ROWS:BLK]] + [stacked[n] for n in small_names] + [loss_vec[0, :1]]
    small_parts, = _exchange_by_sequencer([_pack(partial_small)], True, 10, "gather_small", after=last_exchange)

    grads, delta, new_m, new_v = {}, {}, {}, {}

    def adamw_big(names, recv0, recv1):
        for n, p0, p1 in zip(names, recv0, recv1):
            grads[n], delta[n], new_m[n], new_v[n] = _adamw_layers(p0, p1, w[n], m[n], v[n], f"adamw_{n}")

    adamw_big(("w_up", "w_down"), recv_up[0] + recv_down[0], recv_up[1] + recv_down[1])
    summed = _unpack(_sum_slots(small_parts, "sum_small_grads"), partial_small)
    loss = summed[-1][0]
    grads.update(zip(small_names, summed[1:-1]))
    grads["meta_tokens"] = lax.dynamic_slice_in_dim(summed[0], my_slot * (D // N_DEV), D // N_DEV, axis=1)
    like = [w[n] for n in SMALL]
    d_s, m_s, v_s = _adamw_packed(_pack([grads[n] for n in SMALL]), _pack(like), _pack([m[n] for n in SMALL]),
                                  _pack([v[n] for n in SMALL]), "adamw_small")
    adamw_big(("w_in",) + mixer_small, recv_mix[0], recv_mix[1])
    for n, dd, mm, vs in zip(SMALL, _unpack(d_s, like), _unpack(m_s, like), _unpack(v_s, like)):
        delta[n], new_m[n], new_v[n] = dd, mm, vs

    return (loss, grad_x, *[grads[n] for n in WEIGHTS], *[delta[n] for n in WEIGHTS],
            *[new_m[n] for n in WEIGHTS], *[new_v[n] for n in WEIGHTS])
```

```python
import functools
import math

import jax
import jax.numpy as jnp
from jax import lax
from jax.experimental import pallas as pl
from jax.experimental.pallas import tpu as pltpu
from jax.experimental.pallas import tpu_sc as plsc

F32 = jnp.float32
MXU_DTYPE = jnp.bfloat16
XFER_DTYPE = MXU_DTYPE
_pcall = pl.pallas_call
SDS = jax.ShapeDtypeStruct

D = 1024
D_SSM = 512
D_ATTN = 1024
D_KV = 256
D_FF = 4096
D_IN = 4096
HEAD_DIM = 64
N_Q_HEADS = 16
N_KV_HEADS = 4
Q_PER_KV = 4
N_META = 16
BLK = 128
PAD_ROWS = BLK - N_META
N_GROUPS = 32
N_STATE = 64
GROUP_CH = 16
N_SB = 4
SB_STATES = 512
ROPE_THETA = 10000.0
ATTN_SCALE = HEAD_DIM ** -0.5
NEG_INF = -1e30
RMS_EPS = 1e-6
N_DEV = 8
COL_SHARD = 512

ADAM_LR = 0.001
ADAM_B1 = 0.9
ADAM_B2 = 0.999
ADAM_EPS = 1e-08
ADAM_WD = 0.01
ADAM_STEP = 10

VMEM_LIMIT = 56 * 1024 * 1024

_NT = (((1,), (1,)), ((), ()))
_TN = (((0,), (0,)), ((), ()))


def _cparams(*sem):
    return pltpu.CompilerParams(dimension_semantics=tuple(sem) if sem else None,
                                vmem_limit_bytes=VMEM_LIMIT)


def _row_tile(rows, cap=640):
    for t in (1664, 640, 512, 320, 256, 128):
        if t <= cap and rows % t == 0:
            return t
    raise ValueError(f"unsupported row count {rows}")


def _dot(a, b):
    return jnp.dot(a, b, preferred_element_type=F32)


def _dot_nt(a, b):
    return lax.dot_general(a, b, _NT, preferred_element_type=F32)


def _dot_tn(a, b):
    return lax.dot_general(a, b, _TN, preferred_element_type=F32)


def _sigmoid(x):
    return 1.0 / (1.0 + jnp.exp(-x))


_GELU_C = math.sqrt(2.0 / math.pi)


def _gelu_parts(y):
    t = jnp.tanh(_GELU_C * (y + 0.044715 * (y * y * y)))
    return 0.5 * y * (1.0 + t), t


def _gelu_grad(y, t):
    return 0.5 * (1.0 + t) + 0.5 * y * (1.0 - t * t) * (_GELU_C * (1.0 + 0.134145 * (y * y)))


def _rms_fwd(x, gain):
    r = lax.rsqrt(jnp.mean(x * x, axis=-1, keepdims=True) + RMS_EPS)
    return (x * r) * gain


def _rms_bwd(x, gain, dout):
    r = lax.rsqrt(jnp.mean(x * x, axis=-1, keepdims=True) + RMS_EPS)
    xh = x * r
    dxh = dout * gain
    dx = r * (dxh - xh * jnp.mean(dxh * xh, axis=-1, keepdims=True))
    return dx, jnp.sum(dout * xh, axis=0, keepdims=True)


def _mesh_pos():
    return lax.axis_index("x"), lax.axis_index("y"), lax.axis_index("c")


def _peer(pos, d):
    x, y, c = pos
    return (1 - x if d & 4 else x, 1 - y if d & 2 else y, 1 - c if d & 1 else c)


def _slot(pos):
    return 4 * pos[0] + 2 * pos[1] + pos[2]


def _exchange_copy(gather, src_ref, land_ref, sems, k, d, me, send_side):
    peer = _peer(me, d)
    sender = me if send_side else peer
    src = src_ref if gather else src_ref.at[_slot(peer) if send_side else _slot(me)]
    return pltpu.make_async_remote_copy(
        src_ref=src, dst_ref=land_ref.at[_slot(sender)],
        send_sem=sems[0].at[k * (N_DEV - 1) + d - 1], recv_sem=sems[1].at[k * (N_DEV - 1) + d - 1],
        device_id=peer, device_id_type=pl.DeviceIdType.MESH)


def _exchange_by_sequencer(srcs, gather, collective_id, name, after=()):
    n = len(srcs)
    flags = [gather] * n if isinstance(gather, bool) else list(gather)
    land_types = [SDS(((N_DEV,) + s.shape) if g else s.shape, s.dtype) for s, g in zip(srcs, flags)]

    def body(*refs):
        src_refs = refs[:n]
        land_refs = refs[n + len(after):2 * n + len(after)]
        sems = refs[2 * n + len(after):2 * n + len(after) + 2]
        local_sems = refs[2 * n + len(after) + 2]
        me = _mesh_pos()
        barrier = pltpu.get_barrier_semaphore()
        for d in range(1, N_DEV):
            pl.semaphore_signal(barrier, inc=1, device_id=_peer(me, d), device_id_type=pl.DeviceIdType.MESH)
        pl.semaphore_wait(barrier, N_DEV - 1)
        own = [pltpu.make_async_copy(src_refs[k] if flags[k] else src_refs[k].at[_slot(me)],
                                     land_refs[k].at[_slot(me)], local_sems.at[k]) for k in range(n)]
        for cp in own:
            cp.start()
        for k in range(n):
            for d in range(1, N_DEV):
                _exchange_copy(flags[k], src_refs[k], land_refs[k], sems, k, d, me, True).start()
        for cp in own:
            cp.wait()
        for k in range(n):
            for d in range(1, N_DEV):
                _exchange_copy(flags[k], src_refs[k], land_refs[k], sems, k, d, me, True).wait_send()
        for k in range(n):
            for d in range(1, N_DEV):
                _exchange_copy(flags[k], src_refs[k], land_refs[k], sems, k, d, me, False).wait_recv()

    sem_type = pltpu.SemaphoreType.DMA((n * (N_DEV - 1),))
    return pl.kernel(
        body, out_type=land_types, mesh=plsc.ScalarSubcoreMesh(axis_name="sequencer", num_cores=1), name=name,
        scratch_types=(sem_type, sem_type, pltpu.SemaphoreType.DMA((n,))),
        compiler_params=pltpu.CompilerParams(collective_id=collective_id),
    )(*srcs, *after)


def _load_resident(w_hbm, w_scr, sems, first_step):
    @pl.when(first_step)
    def _():
        copies = [pltpu.make_async_copy(w_hbm.at[s], w_scr.at[s], sems.at[s]) for s in range(N_DEV)]
        for cp in copies:
            cp.start()
        for cp in copies:
            cp.wait()


def _load_resident_transposed(w_hbm, w_scr, stage, sems, first_step):
    @pl.when(first_step)
    def _():
        copies = [pltpu.make_async_copy(w_hbm.at[s], stage.at[s % 2], sems.at[s % 2]) for s in range(N_DEV)]
        copies[0].start()
        for s in range(N_DEV):
            if s + 1 < N_DEV:
                copies[s + 1].start()
            copies[s].wait()
            w_scr[s] = stage[s % 2].T


def _rope_lanes(t, cos, sin_a, sin_b):
    return t * cos + pltpu.roll(t, 96, 1) * sin_a + pltpu.roll(t, 32, 1) * sin_b


def _in_proj(hres, gain3, w_in_g, cos, sin_a, sin_b, layer, after=()):
    rows = hres.shape[0]
    tm = _row_tile(rows, 320)

    def body(x_ref, g_ref, w_hbm, c_ref, a_ref, b_ref, *refs):
        u_ref, gate_ref, q_ref, k_ref, v_ref, h_ref, w_scr, w_sem = refs[len(after):]
        _load_resident(w_hbm, w_scr, w_sem, pl.program_id(0) == 0)
        hn = _rms_fwd(x_ref[...], g_ref[...]).astype(MXU_DTYPE)
        h_ref[...] = hn
        c, a, b = c_ref[...], a_ref[...], b_ref[...]
        u_ref[...] = _dot(hn, w_scr[0])
        for shard in (1, 2):
            res = _dot(hn, w_scr[shard])
            for t in range(4):
                lanes = slice(t * 128, (t + 1) * 128)
                out = slice((shard - 1) * COL_SHARD + t * 128, (shard - 1) * COL_SHARD + (t + 1) * 128)
                q_ref[:, out] = (_rope_lanes(res[:, lanes], c, a, b) * ATTN_SCALE).astype(MXU_DTYPE)
        res = _dot(hn, w_scr[3])
        for t in range(2):
            lanes = slice(t * 128, (t + 1) * 128)
            k_ref[:, lanes] = _rope_lanes(res[:, lanes], c, a, b).astype(MXU_DTYPE)
        v_ref[...] = res[:, D_KV:].astype(MXU_DTYPE)
        for shard in range(4, N_DEV):
            gate_ref[:, (shard - 4) * COL_SHARD:(shard - 3) * COL_SHARD] = _dot(hn, w_scr[shard])

    tab = pl.BlockSpec((tm, 128), lambda i: (i, 0))
    kv = pl.BlockSpec((tm, D_KV), lambda i: (i, 0))
    row_d = pl.BlockSpec((tm, D), lambda i: (i, 0))
    return _pcall(
        body, name=f"in_proj_l{layer}", grid=(rows // tm,),
        in_specs=[row_d, pl.BlockSpec((None, 1, D), lambda i: (layer, 0, 0)),
                  pl.BlockSpec(memory_space=pl.ANY), tab, tab, tab] + [pl.BlockSpec(memory_space=pl.ANY)] * len(after),
        out_specs=[pl.BlockSpec((tm, D_SSM), lambda i: (i, 0)), pl.BlockSpec((tm, 2 * D), lambda i: (i, 0)),
                   row_d, kv, kv, row_d],
        out_shape=[SDS((rows, D_SSM), F32), SDS((rows, 2 * D), F32), SDS((rows, D_ATTN), MXU_DTYPE),
                   SDS((rows, D_KV), MXU_DTYPE), SDS((rows, D_KV), MXU_DTYPE), SDS((rows, D), MXU_DTYPE)],
        scratch_shapes=[pltpu.VMEM((N_DEV, D, COL_SHARD), MXU_DTYPE), pltpu.SemaphoreType.DMA((N_DEV,))],
        compiler_params=_cparams("arbitrary"),
    )(hres, gain3, w_in_g, cos, sin_a, sin_b, *after)


SCAN_TILE = 8


def _scan_tiles(tre_ref, tim_ref, *scans):
    n_tiles = BLK // SCAN_TILE
    row = lax.broadcasted_iota(jnp.int32, (SCAN_TILE, SB_STATES), 0)
    leaving = [list(scan[2]) for scan in scans]
    for step in range(n_tiles):
        for n, (x_scr, out_scr, _, reverse, prev_scr) in enumerate(scans):
            base = 4 if reverse else 0
            j = n_tiles - 1 - step if reverse else step
            rows = slice(SCAN_TILE * j, SCAN_TILE * (j + 1))
            for sb in range(N_SB):
                t_r, t_i = leaving[n][sb]
                xr = x_scr[sb, rows, :SB_STATES]
                xi = x_scr[sb, rows, SB_STATES:]
                for k in range(3):
                    shift = SCAN_TILE - (1 << k) if reverse else (1 << k)
                    rr = pltpu.roll(xr, shift, 0)
                    ri = pltpu.roll(xi, shift, 0)
                    ar = tre_ref[sb, base + k]
                    ai = tim_ref[sb, base + k]
                    xr, xi = xr + (ar * rr - ai * ri), xi + (ar * ri + ai * rr)
                pr = tre_ref[sb, base + 3]
                pi = tim_ref[sb, base + 3]
                xr, xi = xr + (pr * t_r - pi * t_i), xi + (pr * t_i + pi * t_r)
                out_scr[sb, rows, :SB_STATES] = xr
                out_scr[sb, rows, SB_STATES:] = xi
                if prev_scr is not None:
                    prev_scr[sb, rows, :SB_STATES] = jnp.where(row == 0, t_r, pltpu.roll(xr, 1, 0))
                    prev_scr[sb, rows, SB_STATES:] = jnp.where(row == 0, t_i, pltpu.roll(xi, 1, 0))
                edge = slice(0, 1) if reverse else slice(SCAN_TILE - 1, SCAN_TILE)
                leaving[n][sb] = (xr[edge], xi[edge])
    return leaving


def _s5_fwd(u, ssm, w_glu, b_glu3, layer):
    rows = u.shape[0]
    n_chunks = rows // BLK
    b_mat, c_mat, t_re, t_im, d_skip = (ssm[k] for k in ("b_mat", "c_mat", "t_re", "t_im", "d_skip"))

    def body(u_ref, bm_ref, cm_ref, tre_ref, tim_ref, d_ref, wg_ref, bg_ref,
             y_ref, ys_ref, cin_ref, carry, bu_scr, s_scr):
        @pl.when(pl.program_id(0) == 0)
        def _():
            carry[...] = jnp.zeros_like(carry)

        cin_ref[...] = carry[...]
        u = u_ref[...]
        for sb in range(N_SB):
            bu_scr[sb] = _dot(u[:, sb * 128:(sb + 1) * 128].astype(MXU_DTYPE), bm_ref[sb])
        entering = [(carry[2 * sb:2 * sb + 1, :], carry[2 * sb + 1:2 * sb + 2, :]) for sb in range(N_SB)]
        leaving, = _scan_tiles(tre_ref, tim_ref, (bu_scr, s_scr, entering, False, None))
        for sb in range(N_SB):
            cols = slice(sb * 128, (sb + 1) * 128)
            carry[2 * sb:2 * sb + 1, :], carry[2 * sb + 1:2 * sb + 2, :] = leaving[sb]
            y_ref[:, cols] = _dot(s_scr[sb].astype(MXU_DTYPE), cm_ref[sb]) + d_ref[:, cols] * u[:, cols]
        z, _ = _gelu_parts(y_ref[...])
        gl = _dot(z.astype(MXU_DTYPE), wg_ref[...]) + bg_ref[...]
        ys_ref[...] = (z * _sigmoid(gl)).astype(MXU_DTYPE)

    full = lambda shape: pl.BlockSpec(shape, lambda j: (0,) * len(shape))
    of_layer = lambda shape: pl.BlockSpec((None,) + shape, lambda j: (layer,) + (0,) * len(shape))
    return _pcall(
        body, name=f"s5_fwd_l{layer}", grid=(n_chunks,),
        in_specs=[pl.BlockSpec((BLK, D_SSM), lambda j: (j, 0)),
                  of_layer((N_SB, 128, 2 * SB_STATES)), of_layer((N_SB, 2 * SB_STATES, 128)),
                  of_layer((N_SB, 8, SCAN_TILE, SB_STATES)), of_layer((N_SB, 8, SCAN_TILE, SB_STATES)),
                  of_layer((1, D_SSM)), full((D_SSM, D_SSM)),
                  pl.BlockSpec((None, 1, D_SSM), lambda j: (layer, 0, 0))],
        out_specs=[pl.BlockSpec((BLK, D_SSM), lambda j: (j, 0)), pl.BlockSpec((BLK, D_SSM), lambda j: (j, 0)),
                   pl.BlockSpec((None, 8, SB_STATES), lambda j: (j, 0, 0))],
        out_shape=[SDS((rows, D_SSM), F32), SDS((rows, D_SSM), MXU_DTYPE), SDS((n_chunks, 8, SB_STATES), F32)],
        scratch_shapes=[pltpu.VMEM((8, SB_STATES), F32), pltpu.VMEM((N_SB, BLK, 2 * SB_STATES), F32),
                        pltpu.VMEM((N_SB, BLK, 2 * SB_STATES), F32)],
        compiler_params=_cparams("arbitrary"),
    )(u, b_mat, c_mat, t_re, t_im, d_skip, w_glu, b_glu3)


def _attn_mask(i):
    row = lax.broadcasted_iota(jnp.int32, (BLK, 3 * BLK), 0) + i * BLK
    col = lax.broadcasted_iota(jnp.int32, (BLK, 3 * BLK), 1)
    seg = jnp.right_shift(col, 7)
    c = jnp.bitwise_and(col, BLK - 1)
    kidx = c + (i + seg - 2) * BLK
    ok_meta = (seg == 0) & (c >= PAD_ROWS) & (row - c >= BLK)
    ok_win = (seg > 0) & (kidx >= PAD_ROWS) & (kidx <= row) & (row - kidx < BLK)
    return jnp.where(ok_meta | ok_win, 0.0, NEG_INF)


def _head_lanes(h):
    return slice(h * HEAD_DIM, (h + 1) * HEAD_DIM)


def _group_rows(ref, kvh):
    return jnp.concatenate([ref[:, _head_lanes(kvh * Q_PER_KV + g)] for g in range(Q_PER_KV)], axis=0)


def _group_bias(bias, sink_ref, layer, kvh):
    first_col = lax.broadcasted_iota(jnp.int32, (BLK, BLK), 1) == 0
    slabs = []
    for g in range(Q_PER_KV):
        first = jnp.where(first_col, sink_ref[layer, kvh * Q_PER_KV + g], bias[:, :BLK])
        slabs.append(jnp.concatenate([first, bias[:, BLK:]], axis=1))
    return jnp.concatenate(slabs, axis=0)


def _attn_probs(q4, k3, bias4):
    s = _dot_nt(q4, k3) + bias4
    e = jnp.exp(s - jnp.max(s, axis=-1, keepdims=True))
    return e * (1.0 / jnp.sum(e, axis=-1, keepdims=True))


def _attn_fwd(q, k, v, sinks, layer):
    rows = q.shape[0]
    n_blk = rows // BLK

    def body(sink_ref, q_ref, km_ref, kp_ref, kc_ref, vm_ref, vp_ref, vc_ref, o_ref):
        bias = _attn_mask(pl.program_id(0))
        for kvh in range(N_KV_HEADS):
            lanes = _head_lanes(kvh)
            k3 = jnp.concatenate([km_ref[:, lanes], kp_ref[:, lanes], kc_ref[:, lanes]], axis=0)
            v3 = jnp.concatenate([vm_ref[:, lanes], vp_ref[:, lanes], vc_ref[:, lanes]], axis=0)
            p = _attn_probs(_group_rows(q_ref, kvh), k3, _group_bias(bias, sink_ref, layer, kvh))
            o4 = _dot(p.astype(MXU_DTYPE), v3).astype(MXU_DTYPE)
            for g in range(Q_PER_KV):
                o_ref[:, _head_lanes(kvh * Q_PER_KV + g)] = o4[g * BLK:(g + 1) * BLK]

    kv_meta = pl.BlockSpec((BLK, D_KV), lambda i: (0, 0))
    kv_prev = pl.BlockSpec((BLK, D_KV), lambda i: (jnp.maximum(i - 1, 0), 0))
    kv_cur = pl.BlockSpec((BLK, D_KV), lambda i: (i, 0))
    return _pcall(
        body, name=f"attn_fwd_l{layer}", grid=(n_blk,),
        in_specs=[pl.BlockSpec(memory_space=pltpu.SMEM),
                  pl.BlockSpec((BLK, D_ATTN), lambda i: (i, 0)),
                  kv_meta, kv_prev, kv_cur, kv_meta, kv_prev, kv_cur],
        out_specs=pl.BlockSpec((BLK, D_ATTN), lambda i: (i, 0)),
        out_shape=SDS((rows, D_ATTN), MXU_DTYPE),
        compiler_params=_cparams("parallel"),
    )(sinks, q, k, k, k, v, v, v)


def _merge_fwd(y_ssm, y_attn, gates, hres, w_o_ssm, w_o_attn, w_out, gain3, layer):
    rows = hres.shape[0]
    tm = _row_tile(rows, 320)

    def body(ys_ref, ya_ref, gs_ref, ga_ref, x_ref, wos_ref, woa_ref, wout_ref, g_ref,
             mg_ref, mix_ref, out_ref):
        a1 = _dot(ys_ref[...], wos_ref[...])
        a2 = _dot(ya_ref[...], woa_ref[...])
        merged = (_sigmoid(gs_ref[...]) * a1 + _sigmoid(ga_ref[...]) * a2).astype(MXU_DTYPE)
        mg_ref[...] = merged
        mix = _dot(merged, wout_ref[...])
        mix_ref[...] = mix
        out_ref[...] = x_ref[...] + _rms_fwd(mix, g_ref[...])

    row_d = pl.BlockSpec((tm, D), lambda i: (i, 0))
    full = lambda shape: pl.BlockSpec(shape, lambda i: (0,) * len(shape))
    return _pcall(
        body, name=f"merge_fwd_l{layer}", grid=(rows // tm,),
        in_specs=[pl.BlockSpec((tm, D_SSM), lambda i: (i, 0)), row_d,
                  row_d, pl.BlockSpec((tm, D), lambda i: (i, 1)), row_d,
                  full((D_SSM, D)), full((D_ATTN, D)), full((D, D)),
                  pl.BlockSpec((None, 1, D), lambda i: (layer, 0, 0))],
        out_specs=[row_d, row_d, row_d],
        out_shape=[SDS((rows, D), MXU_DTYPE), SDS((rows, D), F32), SDS((rows, D), F32)],
        compiler_params=_cparams("parallel"),
    )(y_ssm, y_attn, gates, gates, hres, w_o_ssm, w_o_attn, w_out, gain3)


def _mlp_fwd(hres, gain_pre3, gain_post3, w_up_g, w_down_g, layer, target=None):
    rows = hres.shape[0]
    tm = _row_tile(rows, 320)

    def body(x_ref, gp_ref, gq_ref, wu_hbm, wd_hbm, *refs):
        if target is None:
            up_ref, h_ref, ff_ref, out_ref, act_scr, wu_scr, wd_scr, wu_sem, wd_sem = refs
        else:
            t_ref, up_ref, h_ref, ff_ref, out_ref, loss_ref, act_scr, wu_scr, wd_scr, wu_sem, wd_sem = refs
        first = pl.program_id(0) == 0
        _load_resident(wu_hbm, wu_scr, wu_sem, first)
        _load_resident(wd_hbm, wd_scr, wd_sem, first)
        hn = _rms_fwd(x_ref[...], gp_ref[...]).astype(MXU_DTYPE)
        h_ref[...] = hn
        for kf in range(N_DEV):
            cols = slice(kf * COL_SHARD, (kf + 1) * COL_SHARD)
            up = _dot(hn, wu_scr[kf])
            up_ref[:, cols] = up.astype(MXU_DTYPE)
            r = jnp.maximum(up, 0.0)
            act_scr[:, cols] = (r * r).astype(MXU_DTYPE)
        ff = _dot(act_scr[...], wd_scr[...].reshape(D_FF, D))
        ff_ref[...] = ff
        out = x_ref[...] + _rms_fwd(ff, gq_ref[...])
        if target is None:
            out_ref[...] = out
        else:
            @pl.when(first)
            def _():
                loss_ref[...] = jnp.zeros_like(loss_ref)

            row = lax.broadcasted_iota(jnp.int32, (tm, D), 0) + pl.program_id(0) * tm
            err = jnp.where(row >= BLK, out - t_ref[...], 0.0)
            out_ref[...] = err * (1.0 / D)
            loss_ref[...] += jnp.sum(err * err) * (0.5 / D)

    row_d = pl.BlockSpec((tm, D), lambda i: (i, 0))
    gain = pl.BlockSpec((None, 1, D), lambda i: (layer, 0, 0))
    with_loss = target is not None
    return _pcall(
        body, name=f"mlp_fwd_l{layer}", grid=(rows // tm,),
        in_specs=[row_d, gain, gain, pl.BlockSpec(memory_space=pl.ANY), pl.BlockSpec(memory_space=pl.ANY)]
        + [row_d] * with_loss,
        out_specs=[pl.BlockSpec((tm, D_FF), lambda i: (i, 0)), row_d, row_d, row_d]
        + [pl.BlockSpec((1, 128), lambda i: (0, 0))] * with_loss,
        out_shape=[SDS((rows, D_FF), MXU_DTYPE), SDS((rows, D), MXU_DTYPE), SDS((rows, D), F32), SDS((rows, D), F32)]
        + [SDS((1, 128), F32)] * with_loss,
        scratch_shapes=[pltpu.VMEM((tm, D_FF), MXU_DTYPE),
                        pltpu.VMEM((N_DEV, D, COL_SHARD), MXU_DTYPE), pltpu.VMEM((N_DEV, COL_SHARD, D), MXU_DTYPE),
                        pltpu.SemaphoreType.DMA((N_DEV,)), pltpu.SemaphoreType.DMA((N_DEV,))],
        compiler_params=_cparams("arbitrary"),
    )(hres, gain_pre3, gain_post3, w_up_g, w_down_g, *([target] if with_loss else []))


def _relu_squared(up):
    r = jnp.maximum(up.astype(F32), 0.0)
    return (r * r).astype(MXU_DTYPE)


def _matmul_tn(a, b, name, dev_major_cols=None, a_fn=None):
    rows, ka = a.shape
    n = b.shape[1]
    ta = min(ka, 1024)
    tn = 1024 if n % 1024 == 0 else 512
    tr = _row_tile(rows, 1664)
    n_r = rows // tr

    def body(a_ref, b_ref, o_ref, acc):
        r = pl.program_id(2)

        @pl.when(r == 0)
        def _():
            acc[...] = jnp.zeros_like(acc)

        a_blk = a_ref[...] if a_fn is None else a_fn(a_ref[...])
        acc[...] += _dot_tn(a_blk, b_ref[...])

        @pl.when(r == n_r - 1)
        def _():
            if dev_major_cols is None:
                o_ref[...] = acc[...].astype(XFER_DTYPE)
            else:
                for s in range(tn // dev_major_cols):
                    o_ref[s] = acc[:, s * dev_major_cols:(s + 1) * dev_major_cols].astype(XFER_DTYPE)

    if dev_major_cols is None:
        out_spec = pl.BlockSpec((ta, tn), lambda i, j, r: (i, j))
        out_shape = SDS((ka, n), XFER_DTYPE)
    else:
        w = dev_major_cols
        out_spec = pl.BlockSpec((tn // w, ta, w), lambda i, j, r: (j, i, 0))
        out_shape = SDS((n // w, ka, w), XFER_DTYPE)
    return _pcall(
        body, name=name, grid=(ka // ta, n // tn, n_r),
        in_specs=[pl.BlockSpec((tr, ta), lambda i, j, r: (r, i)), pl.BlockSpec((tr, tn), lambda i, j, r: (r, j))],
        out_specs=out_spec, out_shape=out_shape,
        scratch_shapes=[pltpu.VMEM((ta, tn), F32)],
        compiler_params=_cparams("parallel", "parallel", "arbitrary"),
    )(a, b)


def _dw_in(h, dproj_pieces, layer):
    rows = h.shape[0]
    tr = _row_tile(rows, 1664)
    n_r = rows // tr

    def body(h_ref, *refs):
        piece_refs, (o_ref, acc) = refs[:len(DPROJ_PIECES)], refs[len(DPROJ_PIECES):]
        j = pl.program_id(0)
        r = pl.program_id(1)

        @pl.when(r == 0)
        def _():
            acc[...] = jnp.zeros_like(acc)

        for piece_ref, (first, count) in zip(piece_refs, DPROJ_PIECES):
            @pl.when((j >= first) & (j < first + count))
            def _():
                acc[...] += _dot_tn(h_ref[...], piece_ref[...])

        @pl.when(r == n_r - 1)
        def _():
            o_ref[...] = acc[...].astype(XFER_DTYPE)

    def piece_spec(first, count):
        def index(j, r):
            mine = (j >= first) & (j < first + count)
            return jnp.where(mine, r, 0), jnp.clip(j - first, 0, count - 1)
        return pl.BlockSpec((tr, COL_SHARD), index)

    return _pcall(
        body, name=f"dw_in_l{layer}", grid=(N_DEV, n_r),
        in_specs=[pl.BlockSpec((tr, D), lambda j, r: (r, 0))] + [piece_spec(*p) for p in DPROJ_PIECES],
        out_specs=pl.BlockSpec((None, D, COL_SHARD), lambda j, r: (j, 0, 0)),
        out_shape=SDS((N_DEV, D, COL_SHARD), XFER_DTYPE),
        scratch_shapes=[pltpu.VMEM((D, COL_SHARD), F32)],
        compiler_params=_cparams("arbitrary", "arbitrary"),
    )(h, *dproj_pieces)


def _mlp_bwd(dout, ff, up, hres_mid, gain_pre3, gain_post3, w_up_g, w_down_g, layer):
    rows = dout.shape[0]
    tm = _row_tile(rows, 320)

    def body(do_ref, ff_ref, up_ref, x_ref, gp_ref, gq_ref, wu_hbm, wd_hbm,
             dff_ref, dup_ref, dx_ref, dgq_ref, dgp_ref, wut_scr, wdt_scr, wu_stage, wd_stage, wu_sem, wd_sem):
        i = pl.program_id(0)
        _load_resident_transposed(wu_hbm, wut_scr, wu_stage, wu_sem, i == 0)
        _load_resident_transposed(wd_hbm, wdt_scr, wd_stage, wd_sem, i == 0)

        @pl.when(i == 0)
        def _():
            dgq_ref[...] = jnp.zeros_like(dgq_ref)
            dgp_ref[...] = jnp.zeros_like(dgp_ref)

        dff, dg = _rms_bwd(ff_ref[...], gq_ref[...], do_ref[...])
        dgq_ref[...] += dg
        dffb = dff.astype(MXU_DTYPE)
        dff_ref[...] = dffb
        for kf in range(N_DEV):
            cols = slice(kf * COL_SHARD, (kf + 1) * COL_SHARD)
            dact = _dot(dffb, wdt_scr[kf])
            dup_ref[:, cols] = (dact * (2.0 * jnp.maximum(up_ref[:, cols].astype(F32), 0.0))).astype(MXU_DTYPE)
        dh = _dot(dup_ref[...], wut_scr[...].reshape(D_FF, D))
        dx, dg = _rms_bwd(x_ref[...], gp_ref[...], dh)
        dgp_ref[...] += dg
        dx_ref[...] = do_ref[...] + dx

    row_d = pl.BlockSpec((tm, D), lambda i: (i, 0))
    row_ff = pl.BlockSpec((tm, D_FF), lambda i: (i, 0))
    gain = pl.BlockSpec((None, 1, D), lambda i: (layer, 0, 0))
    dgain = pl.BlockSpec((1, D), lambda i: (0, 0))
    return _pcall(
        body, name=f"mlp_bwd_l{layer}", grid=(rows // tm,),
        in_specs=[row_d, row_d, row_ff, row_d, gain, gain,
                  pl.BlockSpec(memory_space=pl.ANY), pl.BlockSpec(memory_space=pl.ANY)],
        out_specs=[row_d, row_ff, row_d, dgain, dgain],
        out_shape=[SDS((rows, D), MXU_DTYPE), SDS((rows, D_FF), MXU_DTYPE), SDS((rows, D), F32),
                   SDS((1, D), F32), SDS((1, D), F32)],
        scratch_shapes=[pltpu.VMEM((N_DEV, COL_SHARD, D), MXU_DTYPE), pltpu.VMEM((N_DEV, D, COL_SHARD), MXU_DTYPE),
                        pltpu.VMEM((2, D, COL_SHARD), MXU_DTYPE), pltpu.VMEM((2, COL_SHARD, D), MXU_DTYPE),
                        pltpu.SemaphoreType.DMA((2,)), pltpu.SemaphoreType.DMA((2,))],
        compiler_params=_cparams("arbitrary"),
    )(dout, ff, up, hres_mid, gain_pre3, gain_post3, w_up_g, w_down_g)


def _merge_bwd(dhm, mix, y_ssm, y_attn, gates, w_o_ssm, w_o_attn, w_o_ssm_t, w_o_attn_t, w_out_t, gain3, layer):
    rows = dhm.shape[0]
    tm = _row_tile(rows, 320)

    def body(dh_ref, mix_ref, ys_ref, ya_ref, gs_ref, ga_ref, wos_ref, woa_ref, wost_ref, woat_ref, woutt_ref, g_ref,
             dmix_ref, da1_ref, da2_ref, dgs_ref, dga_ref, dys_ref, dya_ref, dg_ref):
        @pl.when(pl.program_id(0) == 0)
        def _():
            dg_ref[...] = jnp.zeros_like(dg_ref)

        dmix, dg = _rms_bwd(mix_ref[...], g_ref[...], dh_ref[...])
        dg_ref[...] += dg
        dmixb = dmix.astype(MXU_DTYPE)
        dmix_ref[...] = dmixb
        dmerged = _dot(dmixb, woutt_ref[...])
        sg_s = _sigmoid(gs_ref[...])
        sg_a = _sigmoid(ga_ref[...])
        da1 = (dmerged * sg_s).astype(MXU_DTYPE)
        da2 = (dmerged * sg_a).astype(MXU_DTYPE)
        da1_ref[...] = da1
        da2_ref[...] = da2
        a1 = _dot(ys_ref[...], wos_ref[...])
        a2 = _dot(ya_ref[...], woa_ref[...])
        dgs_ref[...] = (dmerged * a1 * (sg_s * (1.0 - sg_s))).astype(MXU_DTYPE)
        dga_ref[...] = (dmerged * a2 * (sg_a * (1.0 - sg_a))).astype(MXU_DTYPE)
        dys_ref[...] = _dot(da1, wost_ref[...])
        dya_ref[...] = _dot(da2, woat_ref[...])

    row_d = pl.BlockSpec((tm, D), lambda i: (i, 0))
    full = lambda shape: pl.BlockSpec(shape, lambda i: (0,) * len(shape))
    return _pcall(
        body, name=f"merge_bwd_l{layer}", grid=(rows // tm,),
        in_specs=[row_d, row_d, pl.BlockSpec((tm, D_SSM), lambda i: (i, 0)), row_d,
                  row_d, pl.BlockSpec((tm, D), lambda i: (i, 1)),
                  full((D_SSM, D)), full((D_ATTN, D)), full((D, D_SSM)), full((D, D_ATTN)), full((D, D)),
                  pl.BlockSpec((None, 1, D), lambda i: (layer, 0, 0))],
        out_specs=[row_d, row_d, row_d, row_d, row_d, pl.BlockSpec((tm, D_SSM), lambda i: (i, 0)), row_d,
                   pl.BlockSpec((1, D), lambda i: (0, 0))],
        out_shape=[SDS((rows, D), MXU_DTYPE)] * 5 + [SDS((rows, D_SSM), F32), SDS((rows, D_ATTN), F32),
                                                      SDS((1, D), F32)],
        compiler_params=_cparams("arbitrary"),
    )(dhm, mix, y_ssm, y_attn, gates, gates, w_o_ssm, w_o_attn, w_o_ssm_t, w_o_attn_t, w_out_t, gain3)


def _attn_bwd(q, k, v, d_out, sinks, cos, sin_a, sin_b, layer):
    rows = q.shape[0]
    n_blk = rows // BLK
    last = n_blk - 1

    def body(sink_ref, q_ref, km_ref, kp_ref, kc_ref, vm_ref, vp_ref, vc_ref, do_ref, c_ref, a_ref, b_ref,
             dqb_ref, dk_ref, dv_ref, dkm_ref, dvm_ref, ds_ref, dk_carry, dv_carry, dq_ref):
        i = pl.program_id(0)

        @pl.when(i == 0)
        def _():
            dkm_ref[...] = jnp.zeros_like(dkm_ref)
            dvm_ref[...] = jnp.zeros_like(dvm_ref)
            ds_ref[...] = jnp.zeros_like(ds_ref)
            dk_carry[...] = jnp.zeros_like(dk_carry)
            dv_carry[...] = jnp.zeros_like(dv_carry)

        @pl.when(i <= last)
        def _():
            bias = _attn_mask(i)
            for kvh in range(N_KV_HEADS):
                lanes = _head_lanes(kvh)
                k3 = jnp.concatenate([km_ref[:, lanes], kp_ref[:, lanes], kc_ref[:, lanes]], axis=0)
                v3 = jnp.concatenate([vm_ref[:, lanes], vp_ref[:, lanes], vc_ref[:, lanes]], axis=0)
                q4 = _group_rows(q_ref, kvh)
                do4 = _group_rows(do_ref, kvh).astype(MXU_DTYPE)
                p = _attn_probs(q4, k3, _group_bias(bias, sink_ref, layer, kvh))
                dp = _dot_nt(do4, v3)
                dsf = p * (dp - jnp.sum(dp * p, axis=-1, keepdims=True))
                dsc = dsf.astype(MXU_DTYPE)
                dv3 = _dot_tn(p.astype(MXU_DTYPE), do4)
                dk3 = _dot_tn(dsc, q4)
                dq4 = _dot(dsc, k3)
                for g in range(Q_PER_KV):
                    h = kvh * Q_PER_KV + g
                    dq_ref[:, _head_lanes(h)] = dq4[g * BLK:(g + 1) * BLK]
                    ds_ref[h:h + 1, :] += jnp.sum(dsf[g * BLK:(g + 1) * BLK, 0:BLK], axis=0, keepdims=True)
                dkm_ref[:, lanes] += dk3[0:BLK]
                dvm_ref[:, lanes] += dv3[0:BLK]
                dk_ref[:, lanes] = dk_carry[:, lanes] + dk3[BLK:2 * BLK]
                dv_ref[:, lanes] = dv_carry[:, lanes] + dv3[BLK:2 * BLK]
                dk_carry[:, lanes] = dk3[2 * BLK:3 * BLK]
                dv_carry[:, lanes] = dv3[2 * BLK:3 * BLK]
            c, a, b = c_ref[...], -a_ref[...], -b_ref[...]
            for t in range(D_ATTN // 128):
                lanes = slice(t * 128, (t + 1) * 128)
                dqb_ref[:, lanes] = (_rope_lanes(dq_ref[:, lanes], c, a, b) * ATTN_SCALE).astype(MXU_DTYPE)

        @pl.when(i == last + 1)
        def _():
            dk_ref[...] = dk_carry[...]
            dv_ref[...] = dv_carry[...]

    cur = lambda i: (jnp.minimum(i, last), 0)
    prev = lambda i: (jnp.clip(i - 1, 0, last), 0)
    kv_meta = pl.BlockSpec((BLK, D_KV), lambda i: (0, 0))
    kv_prev = pl.BlockSpec((BLK, D_KV), prev)
    kv_cur = pl.BlockSpec((BLK, D_KV), cur)
    tab = pl.BlockSpec((BLK, 128), cur)
    return _pcall(
        body, name=f"attn_bwd_l{layer}", grid=(n_blk + 1,),
        in_specs=[pl.BlockSpec(memory_space=pltpu.SMEM),
                  pl.BlockSpec((BLK, D_ATTN), cur),
                  kv_meta, kv_prev, kv_cur, kv_meta, kv_prev, kv_cur,
                  pl.BlockSpec((BLK, D_ATTN), cur), tab, tab, tab],
        out_specs=[pl.BlockSpec((BLK, D_ATTN), cur), kv_prev, kv_prev, kv_meta, kv_meta,
                   pl.BlockSpec((N_Q_HEADS, 128), lambda i: (0, 0))],
        out_shape=[SDS((rows, D_ATTN), MXU_DTYPE), SDS((rows, D_KV), F32), SDS((rows, D_KV), F32),
                   SDS((BLK, D_KV), F32), SDS((BLK, D_KV), F32), SDS((N_Q_HEADS, 128), F32)],
        scratch_shapes=[pltpu.VMEM((BLK, D_KV), F32), pltpu.VMEM((BLK, D_KV), F32), pltpu.VMEM((BLK, D_ATTN), F32)],
        compiler_params=_cparams("arbitrary"),
    )(sinks, q, k, k, k, v, v, v, d_out, cos, sin_a, sin_b)


def _rope_bwd(dk, dv, dk_meta, dv_meta, cos, sin_a, sin_b, layer):
    rows = dk.shape[0]
    tm = _row_tile(rows)

    def body(dk_ref, dv_ref, dkm_ref, dvm_ref, c_ref, a_ref, b_ref, o_ref):
        c, a, b = c_ref[...], -a_ref[...], -b_ref[...]
        for t in range(2):
            x = dk_ref[:, t * 128:(t + 1) * 128]
            o_ref[:, t * 128:(t + 1) * 128] = _rope_lanes(x, c, a, b).astype(MXU_DTYPE)
        o_ref[:, D_KV:] = dv_ref[...].astype(MXU_DTYPE)

        @pl.when(pl.program_id(0) == 0)
        def _():
            cb, ab, bb = c[0:BLK], a[0:BLK], b[0:BLK]
            is_meta = lax.broadcasted_iota(jnp.int32, (BLK, 128), 0) >= PAD_ROWS
            for t in range(2):
                x = dk_ref[0:BLK, t * 128:(t + 1) * 128] + jnp.where(is_meta, dkm_ref[:, t * 128:(t + 1) * 128], 0.0)
                o_ref[0:BLK, t * 128:(t + 1) * 128] = _rope_lanes(x, cb, ab, bb).astype(MXU_DTYPE)
                xv = dv_ref[0:BLK, t * 128:(t + 1) * 128] + jnp.where(is_meta, dvm_ref[:, t * 128:(t + 1) * 128], 0.0)
                o_ref[0:BLK, D_KV + t * 128:D_KV + (t + 1) * 128] = xv.astype(MXU_DTYPE)

    tab = pl.BlockSpec((tm, 128), lambda i: (i, 0))
    kv = pl.BlockSpec((tm, D_KV), lambda i: (i, 0))
    meta = pl.BlockSpec((BLK, D_KV), lambda i: (0, 0))
    return _pcall(
        body, name=f"rope_bwd_l{layer}", grid=(rows // tm,),
        in_specs=[kv, kv, meta, meta, tab, tab, tab],
        out_specs=pl.BlockSpec((tm, 2 * D_KV), lambda i: (i, 0)),
        out_shape=SDS((rows, 2 * D_KV), MXU_DTYPE),
        compiler_params=_cparams("parallel"),
    )(dk, dv, dk_meta, dv_meta, cos, sin_a, sin_b)


def _s5_bwd(d_gated, y, u, carry_in, ssm, w_glu, b_glu3, layer):
    rows = y.shape[0]
    n_chunks = rows // BLK
    b_mat, c_mat, t_re, t_im, d_skip = (ssm[k] for k in ("b_mat", "c_mat", "t_re", "t_im", "d_skip"))

    def body(dz_ref, y_ref, u_ref, cin_ref, bm_ref, cm_ref, tre_ref, tim_ref, d_ref, wg_ref, bg_ref,
             du_ref, dwg_ref, dbg_ref, dd_ref, dbm_ref, dcm_ref, dab_ref,
             lam_carry, bu_scr, s_scr, sp_scr, g_scr, lam_scr):
        step = pl.program_id(0)
        chunk = n_chunks - 1 - step

        @pl.when(step == 0)
        def _():
            for r in (dwg_ref, dbg_ref, dd_ref, dbm_ref, dcm_ref, dab_ref, lam_carry):
                r[...] = jnp.zeros_like(r)

        y = y_ref[...]
        u = u_ref[...]
        d_o = dz_ref[...]
        z, t = _gelu_parts(y)
        zb = z.astype(MXU_DTYPE)
        sg = _sigmoid(_dot(zb, wg_ref[...]) + bg_ref[...])
        dgl = d_o * z * (sg * (1.0 - sg))
        dglb = dgl.astype(MXU_DTYPE)
        dz = d_o * sg + _dot_nt(dglb, wg_ref[...])
        dwg_ref[...] += _dot_tn(zb, dglb)
        dbg_ref[...] += jnp.sum(dgl, axis=0, keepdims=True)
        dy = dz * _gelu_grad(y, t)
        dd_ref[...] += jnp.sum(dy * u, axis=0, keepdims=True)
        grow = lax.broadcasted_iota(jnp.int32, (BLK, 128), 0) + chunk * BLK
        ub = u.astype(MXU_DTYPE)
        dyb_all = dy.astype(MXU_DTYPE)
        for sb in range(N_SB):
            cols = slice(sb * 128, (sb + 1) * 128)
            bu_scr[sb] = _dot(ub[:, cols], bm_ref[sb])
            g_scr[sb] = _dot_nt(dyb_all[:, cols], cm_ref[sb])
        entering_s = [(cin_ref[2 * sb:2 * sb + 1, :], cin_ref[2 * sb + 1:2 * sb + 2, :]) for sb in range(N_SB)]
        entering_lam = [(lam_carry[2 * sb:2 * sb + 1, :], lam_carry[2 * sb + 1:2 * sb + 2, :]) for sb in range(N_SB)]
        _, leaving = _scan_tiles(tre_ref, tim_ref, (bu_scr, s_scr, entering_s, False, sp_scr),
                                 (g_scr, lam_scr, entering_lam, True, None))
        for sb in range(N_SB):
            cols = slice(sb * 128, (sb + 1) * 128)
            u_sb = ub[:, cols]
            dy_sb = dy[:, cols]
            dyb = dyb_all[:, cols]
            lam_carry[2 * sb:2 * sb + 1, :], lam_carry[2 * sb + 1:2 * sb + 2, :] = leaving[sb]
            dcm_ref[sb] += _dot_tn(s_scr[sb].astype(MXU_DTYPE), dyb)
            lr, li = lam_scr[sb, :, :SB_STATES], lam_scr[sb, :, SB_STATES:]
            spr, spi = sp_scr[sb, :, :SB_STATES], sp_scr[sb, :, SB_STATES:]
            dab_ref[2 * sb:2 * sb + 1, :] += jnp.sum(spr * lr + spi * li, axis=0, keepdims=True)
            dab_ref[2 * sb + 1:2 * sb + 2, :] += jnp.sum(spr * li - spi * lr, axis=0, keepdims=True)
            lam = lam_scr[sb].astype(MXU_DTYPE)
            dbm_ref[sb] += _dot_tn(u_sb, lam)
            du = _dot_nt(lam, bm_ref[sb]) + d_ref[:, cols] * dy_sb
            du_ref[:, cols] = jnp.where(grow >= PAD_ROWS, du, 0.0).astype(MXU_DTYPE)

    rev = lambda j: (n_chunks - 1 - j, 0)
    full = lambda shape: pl.BlockSpec(shape, lambda j: (0,) * len(shape))
    of_layer = lambda shape: pl.BlockSpec((None,) + shape, lambda j: (layer,) + (0,) * len(shape))
    tables = [of_layer((N_SB, 8, SCAN_TILE, SB_STATES))] * 2
    chunk_scratch = pltpu.VMEM((N_SB, BLK, 2 * SB_STATES), F32)
    return _pcall(
        body, name=f"s5_bwd_l{layer}", grid=(n_chunks,),
        in_specs=[pl.BlockSpec((BLK, D_SSM), rev), pl.BlockSpec((BLK, D_SSM), rev), pl.BlockSpec((BLK, D_SSM), rev),
                  pl.BlockSpec((None, 8, SB_STATES), lambda j: (n_chunks - 1 - j, 0, 0)),
                  of_layer((N_SB, 128, 2 * SB_STATES)), of_layer((N_SB, 2 * SB_STATES, 128))] + tables + [
                  of_layer((1, D_SSM)), full((D_SSM, D_SSM)),
                  pl.BlockSpec((None, 1, D_SSM), lambda j: (layer, 0, 0))],
        out_specs=[pl.BlockSpec((BLK, D_SSM), rev), full((D_SSM, D_SSM)), full((1, D_SSM)), full((1, D_SSM)),
                   full((N_SB, 128, 2 * SB_STATES)), full((N_SB, 2 * SB_STATES, 128)), full((8, SB_STATES))],
        out_shape=[SDS((rows, D_SSM), MXU_DTYPE), SDS((D_SSM, D_SSM), F32), SDS((1, D_SSM), F32), SDS((1, D_SSM), F32),
                   SDS((N_SB, 128, 2 * SB_STATES), F32), SDS((N_SB, 2 * SB_STATES, 128), F32), SDS((8, SB_STATES), F32)],
        scratch_shapes=[pltpu.VMEM((8, SB_STATES), F32)] + [chunk_scratch] * 5,
        compiler_params=_cparams("arbitrary"),
    )(d_gated, y, u, carry_in, b_mat, c_mat, t_re, t_im, d_skip, w_glu, b_glu3)


DPROJ_PIECES = ((0, 1), (1, 2), (3, 1), (4, 2), (6, 2))


def _in_bwd(dproj_pieces, dhm, hres, gain3, w_in_g, layer):
    rows = hres.shape[0]
    tm = _row_tile(rows)

    def body(*refs):
        piece_refs = refs[:len(DPROJ_PIECES)]
        dh_ref, x_ref, g_ref, w_hbm, dx_ref, dg_ref, wt_scr, w_stage, w_sem = refs[len(DPROJ_PIECES):]
        i = pl.program_id(0)
        _load_resident_transposed(w_hbm, wt_scr, w_stage, w_sem, i == 0)

        @pl.when(i == 0)
        def _():
            dg_ref[...] = jnp.zeros_like(dg_ref)

        dh = None
        for piece_ref, (first, count) in zip(piece_refs, DPROJ_PIECES):
            wt = wt_scr[first:first + count].reshape(count * COL_SHARD, D)
            part = _dot(piece_ref[...], wt)
            dh = part if dh is None else dh + part
        dx, dg = _rms_bwd(x_ref[...], g_ref[...], dh)
        dg_ref[...] += dg
        dx_ref[...] = dh_ref[...] + dx

    row_d = pl.BlockSpec((tm, D), lambda i: (i, 0))
    return _pcall(
        body, name=f"in_bwd_l{layer}", grid=(rows // tm,),
        in_specs=[pl.BlockSpec((tm, count * COL_SHARD), lambda i: (i, 0)) for _, count in DPROJ_PIECES] + [
                  row_d, row_d,
                  pl.BlockSpec((None, 1, D), lambda i: (layer, 0, 0)),
                  pl.BlockSpec(memory_space=pl.ANY)],
        out_specs=[row_d, pl.BlockSpec((1, D), lambda i: (0, 0))],
        out_shape=[SDS((rows, D), F32), SDS((1, D), F32)],
        scratch_shapes=[pltpu.VMEM((N_DEV, COL_SHARD, D), MXU_DTYPE),
                        pltpu.VMEM((2, D, COL_SHARD), MXU_DTYPE), pltpu.SemaphoreType.DMA((2,))],
        compiler_params=_cparams("arbitrary"),
    )(*dproj_pieces, dhm, hres, gain3, w_in_g)


_ADAM_C1 = 1.0 / (1.0 - ADAM_B1 ** ADAM_STEP)
_ADAM_C2 = 1.0 / (1.0 - ADAM_B2 ** ADAM_STEP)


def _adam_math(w, g, m, v):
    m = ADAM_B1 * m + (1.0 - ADAM_B1) * g
    v = ADAM_B2 * v + (1.0 - ADAM_B2) * (g * g)
    delta = -ADAM_LR * ((m * _ADAM_C1) / (jnp.sqrt(v * _ADAM_C2) + ADAM_EPS) + ADAM_WD * w)
    return delta, m, v


def _adamw_layers(parts0, parts1, w, m, v, name):
    _, rows, cols = w.shape
    tr = min(rows, (1 << 17) // cols)
    nt = rows // tr

    def body(p0_ref, p1_ref, w_ref, m_ref, v_ref, g_ref, d_ref, nm_ref, nv_ref):
        layer = pl.program_id(0)

        def run(p_ref):
            g = p_ref[0].astype(F32)
            for s in range(1, N_DEV):
                g = g + p_ref[s].astype(F32)
            delta, nm, nv = _adam_math(w_ref[...], g, m_ref[...], v_ref[...])
            g_ref[...] = g
            d_ref[...] = delta
            nm_ref[...] = nm
            nv_ref[...] = nv

        @pl.when(layer == 0)
        def _():
            run(p0_ref)

        @pl.when(layer == 1)
        def _():
            run(p1_ref)

    wspec = pl.BlockSpec((None, tr, cols), lambda l, i: (l, i, 0))
    return _pcall(
        body, name=name, grid=(2, nt),
        in_specs=[pl.BlockSpec((N_DEV, tr, cols), lambda l, i: (0, jnp.where(l == 0, i, nt - 1), 0)),
                  pl.BlockSpec((N_DEV, tr, cols), lambda l, i: (0, jnp.where(l == 1, i, 0), 0)),
                  wspec, wspec, wspec],
        out_specs=[wspec] * 4, out_shape=[SDS(w.shape, F32)] * 4,
        compiler_params=_cparams("arbitrary", "arbitrary"),
    )(parts0, parts1, w, m, v)


def _sum_slots(parts, name):
    def body(p_ref, o_ref):
        acc = p_ref[0]
        for s in range(1, N_DEV):
            acc = acc + p_ref[s]
        o_ref[...] = acc

    vmem = pl.BlockSpec(memory_space=pltpu.VMEM)
    return _pcall(body, name=name, out_shape=SDS(parts.shape[1:], F32), in_specs=[vmem], out_specs=vmem,
                  compiler_params=_cparams())(parts)


def _adamw_packed(g, w, m, v, name):
    def body(g_ref, w_ref, m_ref, v_ref, d_ref, nm_ref, nv_ref):
        delta, nm, nv = _adam_math(w_ref[...], g_ref[...], m_ref[...], v_ref[...])
        d_ref[...] = delta
        nm_ref[...] = nm
        nv_ref[...] = nv

    vmem = pl.BlockSpec(memory_space=pltpu.VMEM)
    return _pcall(body, name=name, out_shape=[SDS(g.shape, F32)] * 3, in_specs=[vmem] * 4, out_specs=[vmem] * 3,
                  compiler_params=_cparams())(g, w, m, v)


def _ssm_discretize(a_re, a_im, log_dt, b_re, b_im):
    dt = jnp.exp(log_dt)[:, None]
    mag = jnp.exp(a_re * dt)
    ang = a_im * dt
    ab_re, ab_im = mag * jnp.cos(ang), mag * jnp.sin(ang)
    xr, xi = ab_re - 1.0, ab_im
    den = a_re * a_re + a_im * a_im
    q_re = (xr * a_re + xi * a_im) / den
    q_im = (xi * a_re - xr * a_im) / den
    bb_re = q_re[..., None] * b_re - q_im[..., None] * b_im
    bb_im = q_re[..., None] * b_im + q_im[..., None] * b_re
    return ab_re, ab_im, bb_re, bb_im


def _block_diag_b(bb):
    m = jnp.einsum("sgnc,gh->sgchn", bb.reshape(N_SB, 8, N_STATE, GROUP_CH), jnp.eye(8, dtype=F32))
    return m.reshape(N_SB, 128, SB_STATES)


def _block_diag_b_t(dm):
    return jnp.einsum("sgchn,gh->sgnc", dm.reshape(N_SB, 8, GROUP_CH, 8, N_STATE),
                      jnp.eye(8, dtype=F32)).reshape(N_GROUPS, N_STATE, GROUP_CH)


def _block_diag_c(cc):
    m = jnp.einsum("sgcn,gh->sgnhc", cc.reshape(N_SB, 8, GROUP_CH, N_STATE), jnp.eye(8, dtype=F32))
    return m.reshape(N_SB, SB_STATES, 128)


def _block_diag_c_t(dm):
    return jnp.einsum("sgnhc,gh->sgcn", dm.reshape(N_SB, 8, N_STATE, 8, GROUP_CH),
                      jnp.eye(8, dtype=F32)).reshape(N_GROUPS, GROUP_CH, N_STATE)


def _ssm_tables(ab_re, ab_im, bb_re, bb_im, c_re, c_im, d_skip):
    pr, pi = ab_re.reshape(1, -1), ab_im.reshape(1, -1)
    cr, ci = pr, pi
    squares = []
    for _ in range(3):
        squares.append((cr, ci))
        pr, pi = (jnp.concatenate([pr, pr * cr - pi * ci], axis=0),
                  jnp.concatenate([pi, pr * ci + pi * cr], axis=0))
        cr, ci = cr * cr - ci * ci, 2.0 * cr * ci
    r = jnp.arange(SCAN_TILE)[:, None]
    fwd = [(jnp.where(r >= (1 << k), squares[k][0], 0.0), jnp.where(r >= (1 << k), squares[k][1], 0.0))
           for k in range(3)] + [(pr, pi)]
    rev = [(jnp.where(r < SCAN_TILE - (1 << k), squares[k][0], 0.0),
            jnp.where(r < SCAN_TILE - (1 << k), -squares[k][1], 0.0)) for k in range(3)] + [(pr[::-1], -pi[::-1])]
    table = lambda part: jnp.stack([e[part] for e in fwd + rev]).reshape(
        8, SCAN_TILE, N_SB, SB_STATES).transpose(2, 0, 1, 3)
    return dict(
        b_mat=jnp.concatenate([_block_diag_b(bb_re), _block_diag_b(bb_im)], axis=-1).astype(MXU_DTYPE),
        c_mat=jnp.concatenate([_block_diag_c(c_re), -_block_diag_c(c_im)], axis=1).astype(MXU_DTYPE),
        t_re=table(0), t_im=table(1),
        d_skip=d_skip.reshape(1, D_SSM))


def _rope_tables(rows):
    pos = (jnp.arange(rows, dtype=jnp.int32) - PAD_ROWS).astype(F32)
    inv_freq = 1.0 / (ROPE_THETA ** (jnp.arange(0, HEAD_DIM, 2, dtype=F32) / HEAD_DIM))
    ang = pos[:, None] * inv_freq[None, :]
    ang = jnp.concatenate([ang, ang, ang, ang], axis=-1)
    first_half = (jnp.arange(128) % HEAD_DIM) < HEAD_DIM // 2
    sin = jnp.sin(ang)
    return jnp.cos(ang), jnp.where(first_half, -sin, 0.0), jnp.where(first_half, 0.0, sin)


def _pack(arrays):
    flat = jnp.concatenate([a.reshape(-1).astype(F32) for a in arrays])
    pad = (-flat.shape[0]) % 1024
    return jnp.pad(flat, (0, pad)).reshape(-1, 128)


def _unpack(packed, like):
    flat = packed.reshape(-1)
    out, off = [], 0
    for a in like:
        n = math.prod(a.shape)
        out.append(flat[off:off + n].reshape(a.shape))
        off += n
    return out


BIG = ("w_in", "w_glu", "w_o_ssm", "w_o_attn", "w_out", "w_up", "w_down")
WEIGHTS = ("meta_tokens", "norm_mix_pre", "norm_mix_post", "norm_mlp_pre", "norm_mlp_post", "w_in",
           "ssm_a_re", "ssm_a_im", "ssm_log_dt", "ssm_b_re", "ssm_b_im", "ssm_c_re", "ssm_c_im", "ssm_d",
           "w_glu", "b_glu", "attn_sinks", "w_o_ssm", "w_o_attn", "w_out", "w_up", "w_down")
SMALL = tuple(n for n in WEIGHTS if n not in BIG)


def kernel(x, meta_tokens, norm_mix_pre, norm_mix_post, norm_mlp_pre, norm_mlp_post, w_in, ssm_a_re, ssm_a_im, ssm_log_dt, ssm_b_re, ssm_b_im, ssm_c_re, ssm_c_im, ssm_d, w_glu, b_glu, attn_sinks, w_o_ssm, w_o_attn, w_out, w_up, w_down, loss_target, m_meta_tokens, m_norm_mix_pre, m_norm_mix_post, m_norm_mlp_pre, m_norm_mlp_post, m_w_in, m_ssm_a_re, m_ssm_a_im, m_ssm_log_dt, m_ssm_b_re, m_ssm_b_im, m_ssm_c_re, m_ssm_c_im, m_ssm_d, m_w_glu, m_b_glu, m_attn_sinks, m_w_o_ssm, m_w_o_attn, m_w_out, m_w_up, m_w_down, v_meta_tokens, v_norm_mix_pre, v_norm_mix_post, v_norm_mlp_pre, v_norm_mlp_post, v_w_in, v_ssm_a_re, v_ssm_a_im, v_ssm_log_dt, v_ssm_b_re, v_ssm_b_im, v_ssm_c_re, v_ssm_c_im, v_ssm_d, v_w_glu, v_b_glu, v_attn_sinks, v_w_o_ssm, v_w_o_attn, v_w_out, v_w_up, v_w_down):
    args = locals()
    w = {n: args[n] for n in WEIGHTS}
    m = {n: args["m_" + n] for n in WEIGHTS}
    v = {n: args["v_" + n] for n in WEIGHTS}
    n_layers = w_in.shape[0]
    seq = x.shape[1]
    rows = seq + BLK
    my_slot = _slot(_mesh_pos())

    assert n_layers == 2
    xfer = {n: [w[n][l].astype(XFER_DTYPE) for l in range(n_layers)] for n in BIG}
    mixer_small = ("w_glu", "w_o_ssm", "w_o_attn", "w_out")
    meta_g, w_in_g0 = _exchange_by_sequencer([meta_tokens, xfer["w_in"][0]], True, 0, "gather_in0")
    mix0_g = _exchange_by_sequencer([xfer[n][0] for n in mixer_small], True, 1, "gather_mix0")
    meta_full = meta_g.transpose(1, 0, 2).reshape(N_META, D)

    def mixer_weights(w_glu_g, w_o_ssm_g, w_o_attn_g, w_out_g):
        return dict(w_glu=w_glu_g.reshape(D_SSM, D_SSM), w_o_ssm=w_o_ssm_g.transpose(1, 0, 2).reshape(D_SSM, D),
                    w_o_attn=w_o_attn_g.reshape(D_ATTN, D), w_out=w_out_g.reshape(D, D),
                    w_o_ssm_t=w_o_ssm_g.transpose(0, 2, 1).reshape(D, D_SSM),
                    w_o_attn_t=w_o_attn_g.reshape(D_ATTN, D).T, w_out_t=w_out_g.reshape(D, D).T)

    gathered = [dict(w_in=w_in_g0, **mixer_weights(*mix0_g)), {}]

    gains = {n: w[n].reshape(n_layers, 1, D) for n in ("norm_mix_pre", "norm_mix_post", "norm_mlp_pre", "norm_mlp_post")}
    b_glu3 = b_glu.reshape(n_layers, 1, D_SSM)
    cos, sin_a, sin_b = _rope_tables(rows)

    disc, disc_vjp = jax.vjp(jax.vmap(_ssm_discretize), ssm_a_re, ssm_a_im, ssm_log_dt, ssm_b_re, ssm_b_im)
    ssm = jax.vmap(_ssm_tables)(*disc, ssm_c_re, ssm_c_im, ssm_d)

    hres = jnp.concatenate([jnp.zeros((PAD_ROWS, D), F32), meta_full, x[0]], axis=0)

    saved = []
    for l in range(n_layers):
        wl = gathered[l]
        u, gates, q, k, vv, h = _in_proj(hres, gains["norm_mix_pre"], wl["w_in"], cos, sin_a, sin_b, l,
                                         after=[ssm["b_mat"], ssm["c_mat"], ssm["t_re"], ssm["t_im"]] if l == 0 else ())
        if l == 0:
            wl["w_up"], wl["w_down"] = _exchange_by_sequencer([xfer["w_up"][0], xfer["w_down"][0]], True, 2,
                                                              "gather_mlp0", after=[h])
        y, y_ssm, carry_in = _s5_fwd(u, ssm, wl["w_glu"], b_glu3, l)
        if l == 0:
            l1_g = _exchange_by_sequencer([xfer[n][1] for n in ("w_in",) + mixer_small + ("w_up", "w_down")], True, 3,
                                          "gather_l1", after=[y, wl["w_up"]])
            gathered[1] = dict(w_in=l1_g[0], w_up=l1_g[5], w_down=l1_g[6], **mixer_weights(*l1_g[1:5]))
            last_exchange = l1_g[:1]
        y_attn = _attn_fwd(q, k, vv, attn_sinks, l)
        merged, mix, hres_mid = _merge_fwd(y_ssm, y_attn, gates, hres, wl["w_o_ssm"], wl["w_o_attn"], wl["w_out"],
                                           gains["norm_mix_post"], l)
        hres_in = hres
        if l + 1 < n_layers:
            up, h2, ff, hres = _mlp_fwd(hres_mid, gains["norm_mlp_pre"], gains["norm_mlp_post"], wl["w_up"],
                                        wl["w_down"], l)
        else:
            target = jnp.concatenate([jnp.zeros((BLK, D), F32), loss_target[0]], axis=0)
            up, h2, ff, dhres, loss_vec = _mlp_fwd(hres_mid, gains["norm_mlp_pre"], gains["norm_mlp_post"], wl["w_up"],
                                                   wl["w_down"], l, target=target)
        saved.append(dict(hres=hres_in, u=u, gates=gates, h=h, q=q, k=k, v=vv, y=y, y_ssm=y_ssm,
                          carry_in=carry_in, y_attn=y_attn, merged=merged, mix=mix, hres_mid=hres_mid,
                          up=up, h2=h2, ff=ff))

    small_grads = {}
    recv_up, recv_down, recv_mix = [None] * n_layers, [None] * n_layers, [None] * n_layers
    for l in reversed(range(n_layers)):
        s = saved[l]
        wl = gathered[l]
        dff, dup, dhm, dg_mlp_post, dg_mlp_pre = _mlp_bwd(dhres, s["ff"], s["up"], s["hres_mid"], gains["norm_mlp_pre"],
                                                          gains["norm_mlp_post"], wl["w_up"], wl["w_down"], l)
        dw_up = _matmul_tn(s["h2"], dup, f"dw_up_l{l}", dev_major_cols=COL_SHARD)
        recv_up[l] = _exchange_by_sequencer([dw_up], False, 4 + 3 * l, f"scatter_up{l}", after=last_exchange)
        dw_down = _matmul_tn(s["up"], dff, f"dw_down_l{l}", a_fn=_relu_squared).reshape(N_DEV, COL_SHARD, D)
        recv_down[l] = _exchange_by_sequencer([dw_down], False, 5 + 3 * l, f"scatter_down{l}", after=recv_up[l])
        last_exchange = recv_down[l]
        dmix, da1, da2, dgs, dga, dy_ssm, dy_attn, dg_mix_post = _merge_bwd(
            dhm, s["mix"], s["y_ssm"], s["y_attn"], s["gates"], wl["w_o_ssm"], wl["w_o_attn"], wl["w_o_ssm_t"],
            wl["w_o_attn_t"], wl["w_out_t"], gains["norm_mix_post"], l)
        dw_out = _matmul_tn(s["merged"], dmix, f"dw_out_l{l}").reshape(N_DEV, D // N_DEV, D)
        dw_o_attn = _matmul_tn(s["y_attn"], da2, f"dw_o_attn_l{l}").reshape(N_DEV, D_ATTN // N_DEV, D)
        dw_o_ssm = _matmul_tn(s["y_ssm"], da1, f"dw_o_ssm_l{l}", dev_major_cols=D // N_DEV)
        if l == 0:
            recv_out0 = _exchange_by_sequencer([dw_o_ssm, dw_o_attn, dw_out], False, 11, "scatter_out0",
                                               after=last_exchange)
            last_exchange = recv_out0[:1]
        dq, dk, dv, dk_meta, dv_meta, dsink = _attn_bwd(s["q"], s["k"], s["v"], dy_attn, attn_sinks, cos, sin_a, sin_b, l)
        dkv = _rope_bwd(dk, dv, dk_meta, dv_meta, cos, sin_a, sin_b, l)
        du, dw_glu, db_glu, dd_skip, db_mat, dc_mat, dab = _s5_bwd(dy_ssm, s["y"], s["u"], s["carry_in"], ssm,
                                                                    wl["w_glu"], b_glu3, l)
        dproj = (du, dq, dkv, dgs, dga)
        dw_in = _dw_in(s["h"], dproj, l)
        mix_parts = [dw_in, dw_glu.astype(XFER_DTYPE).reshape(N_DEV, D_SSM // N_DEV, D_SSM), dw_o_ssm, dw_o_attn, dw_out]
        if l > 0:
            recv_mix[l] = _exchange_by_sequencer(mix_parts, False, 6 + 3 * l, f"scatter_mix{l}", after=last_exchange)
            last_exchange = recv_mix[l][:1]
        else:
            recv_mix[0] = _exchange_by_sequencer(mix_parts[:2], False, 6, "scatter_in0", after=last_exchange) + recv_out0
            last_exchange = recv_mix[0][:1]
        dhres, dg_mix_pre = _in_bwd(dproj, dhm, s["hres"], gains["norm_mix_pre"], wl["w_in"], l)

        for name, val in (("norm_mix_pre", dg_mix_pre[0]), ("norm_mix_post", dg_mix_post[0]),
                          ("norm_mlp_pre", dg_mlp_pre[0]), ("norm_mlp_post", dg_mlp_post[0]),
                          ("dab", dab), ("db_mat", db_mat), ("dc_mat", dc_mat),
                          ("ssm_d", dd_skip.reshape(N_GROUPS, GROUP_CH)), ("b_glu", db_glu[0]),
                          ("attn_sinks", dsink[:, 0])):
            small_grads.setdefault(name, [None] * n_layers)[l] = val

    grad_x = dhres[BLK:][None]
    stacked = {n: jnp.stack(v) for n, v in small_grads.items()}
    dab = stacked["dab"].reshape(n_layers, N_SB, 2, SB_STATES)
    db_mat, dc_mat = stacked["db_mat"], stacked["dc_mat"]
    b_t, c_t = jax.vmap(_block_diag_b_t), jax.vmap(_block_diag_c_t)
    (stacked["ssm_a_re"], stacked["ssm_a_im"], stacked["ssm_log_dt"], stacked["ssm_b_re"],
     stacked["ssm_b_im"]) = disc_vjp((dab[:, :, 0].reshape(n_layers, N_GROUPS, N_STATE),
                                      dab[:, :, 1].reshape(n_layers, N_GROUPS, N_STATE),
                                      b_t(db_mat[..., :SB_STATES]), b_t(db_mat[..., SB_STATES:])))
    stacked["ssm_c_re"] = c_t(dc_mat[:, :, :SB_STATES])
    stacked["ssm_c_im"] = -c_t(dc_mat[:, :, SB_STATES:])
    small_names = [n for n in SMALL if n != "meta_tokens"]
    partial_small = [dhres[PAD_ROWS:BLK]] + [stacked[n] for n in small_names] + [loss_vec[0, :1]]
    small_parts, = _exchange_by_sequencer([_pack(partial_small)], True, 10, "gather_small", after=last_exchange)

    grads, delta, new_m, new_v = {}, {}, {}, {}

    def adamw_big(names, recv0, recv1):
        for n, p0, p1 in zip(names, recv0, recv1):
            grads[n], delta[n], new_m[n], new_v[n] = _adamw_layers(p0, p1, w[n], m[n], v[n], f"adamw_{n}")

    adamw_big(("w_up", "w_down"), recv_up[0] + recv_down[0], recv_up[1] + recv_down[1])
    summed = _unpack(_sum_slots(small_parts, "sum_small_grads"), partial_small)
    loss = summed[-1][0]
    grads.update(zip(small_names, summed[1:-1]))
    grads["meta_tokens"] = lax.dynamic_slice_in_dim(summed[0], my_slot * (D // N_DEV), D // N_DEV, axis=1)
    like = [w[n] for n in SMALL]
    d_s, m_s, v_s = _adamw_packed(_pack([grads[n] for n in SMALL]), _pack(like), _pack([m[n] for n in SMALL]),
                                  _pack([v[n] for n in SMALL]), "adamw_small")
    adamw_big(("w_in",) + mixer_small, recv_mix[0], recv_mix[1])
    for n, dd, mm, vs in zip(SMALL, _unpack(d_s, like), _unpack(m_s, like), _unpack(v_s, like)):
        delta[n], new_m[n], new_v[n] = dd, mm, vs

    return (loss, grad_x, *[grads[n] for n in WEIGHTS], *[delta[n] for n in WEIGHTS],
            *[new_m[n] for n in WEIGHTS], *[new_v[n] for n in WEIGHTS])
```

```python
import functools
import math

import jax
import jax.numpy as jnp
from jax import lax
from jax.experimental import pallas as pl
from jax.experimental.pallas import tpu as pltpu
from jax.experimental.pallas import tpu_sc as plsc

F32 = jnp.float32
MXU_DTYPE = jnp.bfloat16
XFER_DTYPE = MXU_DTYPE
_pcall = pl.pallas_call
SDS = jax.ShapeDtypeStruct

D = 1024
D_SSM = 512
D_ATTN = 1024
D_KV = 256
D_FF = 4096
D_IN = 4096
HEAD_DIM = 64
N_Q_HEADS = 16
N_KV_HEADS = 4
Q_PER_KV = 4
N_META = 16
BLK = 128
PAD_ROWS = BLK - N_META
N_GROUPS = 32
N_STATE = 64
GROUP_CH = 16
N_SB = 4
SB_STATES = 512
ROPE_THETA = 10000.0
ATTN_SCALE = HEAD_DIM ** -0.5
NEG_INF = -1e30
RMS_EPS = 1e-6
N_DEV = 8
COL_SHARD = 512

ADAM_LR = 0.001
ADAM_B1 = 0.9
ADAM_B2 = 0.999
ADAM_EPS = 1e-08
ADAM_WD = 0.01
ADAM_STEP = 10

VMEM_LIMIT = 56 * 1024 * 1024

_NT = (((1,), (1,)), ((), ()))
_TN = (((0,), (0,)), ((), ()))


def _cparams(*sem):
    return pltpu.CompilerParams(dimension_semantics=tuple(sem) if sem else None,
                                vmem_limit_bytes=VMEM_LIMIT)


def _row_tile(rows, cap=640):
    for t in (1664, 640, 512, 320, 256, 128):
        if t <= cap and rows % t == 0:
            return t
    raise ValueError(f"unsupported row count {rows}")


def _dot(a, b):
    return jnp.dot(a, b, preferred_element_type=F32)


def _dot_nt(a, b):
    return lax.dot_general(a, b, _NT, preferred_element_type=F32)


def _dot_tn(a, b):
    return lax.dot_general(a, b, _TN, preferred_element_type=F32)


def _sigmoid(x):
    return 1.0 / (1.0 + jnp.exp(-x))


_GELU_C = math.sqrt(2.0 / math.pi)


def _gelu_parts(y):
    t = jnp.tanh(_GELU_C * (y + 0.044715 * (y * y * y)))
    return 0.5 * y * (1.0 + t), t


def _gelu_grad(y, t):
    return 0.5 * (1.0 + t) + 0.5 * y * (1.0 - t * t) * (_GELU_C * (1.0 + 0.134145 * (y * y)))


def _rms_fwd(x, gain):
    r = lax.rsqrt(jnp.mean(x * x, axis=-1, keepdims=True) + RMS_EPS)
    return (x * r) * gain


def _rms_bwd(x, gain, dout):
    r = lax.rsqrt(jnp.mean(x * x, axis=-1, keepdims=True) + RMS_EPS)
    xh = x * r
    dxh = dout * gain
    dx = r * (dxh - xh * jnp.mean(dxh * xh, axis=-1, keepdims=True))
    return dx, jnp.sum(dout * xh, axis=0, keepdims=True)


def _mesh_pos():
    return lax.axis_index("x"), lax.axis_index("y"), lax.axis_index("c")


def _peer(pos, d):
    x, y, c = pos
    return (1 - x if d & 4 else x, 1 - y if d & 2 else y, 1 - c if d & 1 else c)


def _slot(pos):
    return 4 * pos[0] + 2 * pos[1] + pos[2]


def _exchange_copy(gather, src_ref, land_ref, sems, k, d, me, send_side):
    peer = _peer(me, d)
    sender = me if send_side else peer
    src = src_ref if gather else src_ref.at[_slot(peer) if send_side else _slot(me)]
    return pltpu.make_async_remote_copy(
        src_ref=src, dst_ref=land_ref.at[_slot(sender)],
        send_sem=sems[0].at[k * (N_DEV - 1) + d - 1], recv_sem=sems[1].at[k * (N_DEV - 1) + d - 1],
        device_id=peer, device_id_type=pl.DeviceIdType.MESH)


def _exchange_by_sequencer(srcs, gather, collective_id, name, after=()):
    n = len(srcs)
    flags = [gather] * n if isinstance(gather, bool) else list(gather)
    land_types = [SDS(((N_DEV,) + s.shape) if g else s.shape, s.dtype) for s, g in zip(srcs, flags)]

    def body(*refs):
        src_refs = refs[:n]
        land_refs = refs[n + len(after):2 * n + len(after)]
        sems = refs[2 * n + len(after):2 * n + len(after) + 2]
        local_sems = refs[2 * n + len(after) + 2]
        me = _mesh_pos()
        barrier = pltpu.get_barrier_semaphore()
        for d in range(1, N_DEV):
            pl.semaphore_signal(barrier, inc=1, device_id=_peer(me, d), device_id_type=pl.DeviceIdType.MESH)
        pl.semaphore_wait(barrier, N_DEV - 1)
        own = [pltpu.make_async_copy(src_refs[k] if flags[k] else src_refs[k].at[_slot(me)],
                                     land_refs[k].at[_slot(me)], local_sems.at[k]) for k in range(n)]
        for cp in own:
            cp.start()
        for k in range(n):
            for d in range(1, N_DEV):
                _exchange_copy(flags[k], src_refs[k], land_refs[k], sems, k, d, me, True).start()
        for cp in own:
            cp.wait()
        for k in range(n):
            for d in range(1, N_DEV):
                _exchange_copy(flags[k], src_refs[k], land_refs[k], sems, k, d, me, True).wait_send()
        for k in range(n):
            for d in range(1, N_DEV):
                _exchange_copy(flags[k], src_refs[k], land_refs[k], sems, k, d, me, False).wait_recv()

    sem_type = pltpu.SemaphoreType.DMA((n * (N_DEV - 1),))
    return pl.kernel(
        body, out_type=land_types, mesh=plsc.ScalarSubcoreMesh(axis_name="sequencer", num_cores=1), name=name,
        scratch_types=(sem_type, sem_type, pltpu.SemaphoreType.DMA((n,))),
        compiler_params=pltpu.CompilerParams(collective_id=collective_id),
    )(*srcs, *after)


def _load_resident(w_hbm, w_scr, sems, first_step):
    @pl.when(first_step)
    def _():
        copies = [pltpu.make_async_copy(w_hbm.at[s], w_scr.at[s], sems.at[s]) for s in range(N_DEV)]
        for cp in copies:
            cp.start()
        for cp in copies:
            cp.wait()


def _load_resident_transposed(w_hbm, w_scr, stage, sems, first_step):
    @pl.when(first_step)
    def _():
        copies = [pltpu.make_async_copy(w_hbm.at[s], stage.at[s % 2], sems.at[s % 2]) for s in range(N_DEV)]
        copies[0].start()
        for s in range(N_DEV):
            if s + 1 < N_DEV:
                copies[s + 1].start()
            copies[s].wait()
            w_scr[s] = stage[s % 2].T


def _rope_lanes(t, cos, sin_a, sin_b):
    return t * cos + pltpu.roll(t, 96, 1) * sin_a + pltpu.roll(t, 32, 1) * sin_b


def _in_proj(hres, gain3, w_in_g, cos, sin_a, sin_b, layer, after=()):
    rows = hres.shape[0]
    tm = _row_tile(rows, 320)

    def body(x_ref, g_ref, w_hbm, c_ref, a_ref, b_ref, *refs):
        u_ref, gate_ref, q_ref, k_ref, v_ref, h_ref, w_scr, w_sem = refs[len(after):]
        _load_resident(w_hbm, w_scr, w_sem, pl.program_id(0) == 0)
        hn = _rms_fwd(x_ref[...], g_ref[...]).astype(MXU_DTYPE)
        h_ref[...] = hn
        c, a, b = c_ref[...], a_ref[...], b_ref[...]
        u_ref[...] = _dot(hn, w_scr[0])
        for shard in (1, 2):
            res = _dot(hn, w_scr[shard])
            for t in range(4):
                lanes = slice(t * 128, (t + 1) * 128)
                out = slice((shard - 1) * COL_SHARD + t * 128, (shard - 1) * COL_SHARD + (t + 1) * 128)
                q_ref[:, out] = (_rope_lanes(res[:, lanes], c, a, b) * ATTN_SCALE).astype(MXU_DTYPE)
        res = _dot(hn, w_scr[3])
        for t in range(2):
            lanes = slice(t * 128, (t + 1) * 128)
            k_ref[:, lanes] = _rope_lanes(res[:, lanes], c, a, b).astype(MXU_DTYPE)
        v_ref[...] = res[:, D_KV:].astype(MXU_DTYPE)
        for shard in range(4, N_DEV):
            gate_ref[:, (shard - 4) * COL_SHARD:(shard - 3) * COL_SHARD] = _dot(hn, w_scr[shard])

    tab = pl.BlockSpec((tm, 128), lambda i: (i, 0))
    kv = pl.BlockSpec((tm, D_KV), lambda i: (i, 0))
    row_d = pl.BlockSpec((tm, D), lambda i: (i, 0))
    return _pcall(
        body, name=f"in_proj_l{layer}", grid=(rows // tm,),
        in_specs=[row_d, pl.BlockSpec((None, 1, D), lambda i: (layer, 0, 0)),
                  pl.BlockSpec(memory_space=pl.ANY), tab, tab, tab] + [pl.BlockSpec(memory_space=pl.ANY)] * len(after),
        out_specs=[pl.BlockSpec((tm, D_SSM), lambda i: (i, 0)), pl.BlockSpec((tm, 2 * D), lambda i: (i, 0)),
                   row_d, kv, kv, row_d],
        out_shape=[SDS((rows, D_SSM), F32), SDS((rows, 2 * D), F32), SDS((rows, D_ATTN), MXU_DTYPE),
                   SDS((rows, D_KV), MXU_DTYPE), SDS((rows, D_KV), MXU_DTYPE), SDS((rows, D), MXU_DTYPE)],
        scratch_shapes=[pltpu.VMEM((N_DEV, D, COL_SHARD), MXU_DTYPE), pltpu.SemaphoreType.DMA((N_DEV,))],
        compiler_params=_cparams("arbitrary"),
    )(hres, gain3, w_in_g, cos, sin_a, sin_b, *after)


SCAN_TILE = 8


def _scan_tiles(tre_ref, tim_ref, *scans):
    n_tiles = BLK // SCAN_TILE
    row = lax.broadcasted_iota(jnp.int32, (SCAN_TILE, SB_STATES), 0)
    leaving = [list(scan[2]) for scan in scans]
    for step in range(n_tiles):
        for n, (x_scr, out_scr, _, reverse, prev_scr) in enumerate(scans):
            base = 4 if reverse else 0
            j = n_tiles - 1 - step if reverse else step
            rows = slice(SCAN_TILE * j, SCAN_TILE * (j + 1))
            for sb in range(N_SB):
                t_r, t_i = leaving[n][sb]
                xr = x_scr[sb, rows, :SB_STATES]
                xi = x_scr[sb, rows, SB_STATES:]
                for k in range(3):
                    shift = SCAN_TILE - (1 << k) if reverse else (1 << k)
                    rr = pltpu.roll(xr, shift, 0)
                    ri = pltpu.roll(xi, shift, 0)
                    ar = tre_ref[sb, base + k]
                    ai = tim_ref[sb, base + k]
                    xr, xi = xr + (ar * rr - ai * ri), xi + (ar * ri + ai * rr)
                pr = tre_ref[sb, base + 3]
                pi = tim_ref[sb, base + 3]
                xr, xi = xr + (pr * t_r - pi * t_i), xi + (pr * t_i + pi * t_r)
                out_scr[sb, rows, :SB_STATES] = xr
                out_scr[sb, rows, SB_STATES:] = xi
                if prev_scr is not None:
                    prev_scr[sb, rows, :SB_STATES] = jnp.where(row == 0, t_r, pltpu.roll(xr, 1, 0))
                    prev_scr[sb, rows, SB_STATES:] = jnp.where(row == 0, t_i, pltpu.roll(xi, 1, 0))
                edge = slice(0, 1) if reverse else slice(SCAN_TILE - 1, SCAN_TILE)
                leaving[n][sb] = (xr[edge], xi[edge])
    return leaving


def _s5_fwd(u, ssm, w_glu, b_glu3, layer):
    rows = u.shape[0]
    n_chunks = rows // BLK
    b_mat, c_mat, t_re, t_im, d_skip = (ssm[k] for k in ("b_mat", "c_mat", "t_re", "t_im", "d_skip"))

    def body(u_ref, bm_ref, cm_ref, tre_ref, tim_ref, d_ref, wg_ref, bg_ref,
             y_ref, ys_ref, cin_ref, carry, bu_scr, s_scr):
        @pl.when(pl.program_id(0) == 0)
        def _():
            carry[...] = jnp.zeros_like(carry)

        cin_ref[...] = carry[...]
        u = u_ref[...]
        for sb in range(N_SB):
            bu_scr[sb] = _dot(u[:, sb * 128:(sb + 1) * 128].astype(MXU_DTYPE), bm_ref[sb])
        entering = [(carry[2 * sb:2 * sb + 1, :], carry[2 * sb + 1:2 * sb + 2, :]) for sb in range(N_SB)]
        leaving, = _scan_tiles(tre_ref, tim_ref, (bu_scr, s_scr, entering, False, None))
        for sb in range(N_SB):
            cols = slice(sb * 128, (sb + 1) * 128)
            carry[2 * sb:2 * sb + 1, :], carry[2 * sb + 1:2 * sb + 2, :] = leaving[sb]
            y_ref[:, cols] = _dot(s_scr[sb].astype(MXU_DTYPE), cm_ref[sb]) + d_ref[:, cols] * u[:, cols]
        z, _ = _gelu_parts(y_ref[...])
        gl = _dot(z.astype(MXU_DTYPE), wg_ref[...]) + bg_ref[...]
        ys_ref[...] = (z * _sigmoid(gl)).astype(MXU_DTYPE)

    full = lambda shape: pl.BlockSpec(shape, lambda j: (0,) * len(shape))
    of_layer = lambda shape: pl.BlockSpec((None,) + shape, lambda j: (layer,) + (0,) * len(shape))
    return _pcall(
        body, name=f"s5_fwd_l{layer}", grid=(n_chunks,),
        in_specs=[pl.BlockSpec((BLK, D_SSM), lambda j: (j, 0)),
                  of_layer((N_SB, 128, 2 * SB_STATES)), of_layer((N_SB, 2 * SB_STATES, 128)),
                  of_layer((N_SB, 8, SCAN_TILE, SB_STATES)), of_layer((N_SB, 8, SCAN_TILE, SB_STATES)),
                  of_layer((1, D_SSM)), full((D_SSM, D_SSM)),
                  pl.BlockSpec((None, 1, D_SSM), lambda j: (layer, 0, 0))],
        out_specs=[pl.BlockSpec((BLK, D_SSM), lambda j: (j, 0)), pl.BlockSpec((BLK, D_SSM), lambda j: (j, 0)),
                   pl.BlockSpec((None, 8, SB_STATES), lambda j: (j, 0, 0))],
        out_shape=[SDS((rows, D_SSM), F32), SDS((rows, D_SSM), MXU_DTYPE), SDS((n_chunks, 8, SB_STATES), F32)],
        scratch_shapes=[pltpu.VMEM((8, SB_STATES), F32), pltpu.VMEM((N_SB, BLK, 2 * SB_STATES), F32),
                        pltpu.VMEM((N_SB, BLK, 2 * SB_STATES), F32)],
        compiler_params=_cparams("arbitrary"),
    )(u, b_mat, c_mat, t_re, t_im, d_skip, w_glu, b_glu3)


def _attn_mask(i):
    row = lax.broadcasted_iota(jnp.int32, (BLK, 3 * BLK), 0) + i * BLK
    col = lax.broadcasted_iota(jnp.int32, (BLK, 3 * BLK), 1)
    seg = jnp.right_shift(col, 7)
    c = jnp.bitwise_and(col, BLK - 1)
    kidx = c + (i + seg - 2) * BLK
    ok_meta = (seg == 0) & (c >= PAD_ROWS) & (row - c >= BLK)
    ok_win = (seg > 0) & (kidx >= PAD_ROWS) & (kidx <= row) & (row - kidx < BLK)
    return jnp.where(ok_meta | ok_win, 0.0, NEG_INF)


def _head_lanes(h):
    return slice(h * HEAD_DIM, (h + 1) * HEAD_DIM)


def _group_rows(ref, kvh):
    return jnp.concatenate([ref[:, _head_lanes(kvh * Q_PER_KV + g)] for g in range(Q_PER_KV)], axis=0)


def _group_bias(bias, sink_ref, layer, kvh):
    first_col = lax.broadcasted_iota(jnp.int32, (BLK, BLK), 1) == 0
    slabs = []
    for g in range(Q_PER_KV):
        first = jnp.where(first_col, sink_ref[layer, kvh * Q_PER_KV + g], bias[:, :BLK])
        slabs.append(jnp.concatenate([first, bias[:, BLK:]], axis=1))
    return jnp.concatenate(slabs, axis=0)


def _attn_probs(q4, k3, bias4):
    s = _dot_nt(q4, k3) + bias4
    e = jnp.exp(s - jnp.max(s, axis=-1, keepdims=True))
    return e * (1.0 / jnp.sum(e, axis=-1, keepdims=True))


def _attn_fwd(q, k, v, sinks, layer):
    rows = q.shape[0]
    n_blk = rows // BLK

    def body(sink_ref, q_ref, km_ref, kp_ref, kc_ref, vm_ref, vp_ref, vc_ref, o_ref):
        bias = _attn_mask(pl.program_id(0))
        for kvh in range(N_KV_HEADS):
            lanes = _head_lanes(kvh)
            k3 = jnp.concatenate([km_ref[:, lanes], kp_ref[:, lanes], kc_ref[:, lanes]], axis=0)
            v3 = jnp.concatenate([vm_ref[:, lanes], vp_ref[:, lanes], vc_ref[:, lanes]], axis=0)
            p = _attn_probs(_group_rows(q_ref, kvh), k3, _group_bias(bias, sink_ref, layer, kvh))
            o4 = _dot(p.astype(MXU_DTYPE), v3).astype(MXU_DTYPE)
            for g in range(Q_PER_KV):
                o_ref[:, _head_lanes(kvh * Q_PER_KV + g)] = o4[g * BLK:(g + 1) * BLK]

    kv_meta = pl.BlockSpec((BLK, D_KV), lambda i: (0, 0))
    kv_prev = pl.BlockSpec((BLK, D_KV), lambda i: (jnp.maximum(i - 1, 0), 0))
    kv_cur = pl.BlockSpec((BLK, D_KV), lambda i: (i, 0))
    return _pcall(
        body, name=f"attn_fwd_l{layer}", grid=(n_blk,),
        in_specs=[pl.BlockSpec(memory_space=pltpu.SMEM),
                  pl.BlockSpec((BLK, D_ATTN), lambda i: (i, 0)),
                  kv_meta, kv_prev, kv_cur, kv_meta, kv_prev, kv_cur],
        out_specs=pl.BlockSpec((BLK, D_ATTN), lambda i: (i, 0)),
        out_shape=SDS((rows, D_ATTN), MXU_DTYPE),
        compiler_params=_cparams("parallel"),
    )(sinks, q, k, k, k, v, v, v)


def _merge_fwd(y_ssm, y_attn, gates, hres, w_o_ssm, w_o_attn, w_out, gain3, layer):
    rows = hres.shape[0]
    tm = _row_tile(rows, 320)

    def body(ys_ref, ya_ref, gs_ref, ga_ref, x_ref, wos_ref, woa_ref, wout_ref, g_ref,
             mg_ref, mix_ref, out_ref):
        a1 = _dot(ys_ref[...], wos_ref[...])
        a2 = _dot(ya_ref[...], woa_ref[...])
        merged = (_sigmoid(gs_ref[...]) * a1 + _sigmoid(ga_ref[...]) * a2).astype(MXU_DTYPE)
        mg_ref[...] = merged
        mix = _dot(merged, wout_ref[...])
        mix_ref[...] = mix
        out_ref[...] = x_ref[...] + _rms_fwd(mix, g_ref[...])

    row_d = pl.BlockSpec((tm, D), lambda i: (i, 0))
    full = lambda shape: pl.BlockSpec(shape, lambda i: (0,) * len(shape))
    return _pcall(
        body, name=f"merge_fwd_l{layer}", grid=(rows // tm,),
        in_specs=[pl.BlockSpec((tm, D_SSM), lambda i: (i, 0)), row_d,
                  row_d, pl.BlockSpec((tm, D), lambda i: (i, 1)), row_d,
                  full((D_SSM, D)), full((D_ATTN, D)), full((D, D)),
                  pl.BlockSpec((None, 1, D), lambda i: (layer, 0, 0))],
        out_specs=[row_d, row_d, row_d],
        out_shape=[SDS((rows, D), MXU_DTYPE), SDS((rows, D), F32), SDS((rows, D), F32)],
        compiler_params=_cparams("parallel"),
    )(y_ssm, y_attn, gates, gates, hres, w_o_ssm, w_o_attn, w_out, gain3)


def _mlp_fwd(hres, gain_pre3, gain_post3, w_up_g, w_down_g, layer, target=None):
    rows = hres.shape[0]
    tm = _row_tile(rows, 320)

    def body(x_ref, gp_ref, gq_ref, wu_hbm, wd_hbm, *refs):
        if target is None:
            up_ref, h_ref, ff_ref, out_ref, act_scr, wu_scr, wd_scr, wu_sem, wd_sem = refs
        else:
            t_ref, up_ref, h_ref, ff_ref, out_ref, loss_ref, act_scr, wu_scr, wd_scr, wu_sem, wd_sem = refs
        first = pl.program_id(0) == 0
        _load_resident(wu_hbm, wu_scr, wu_sem, first)
        _load_resident(wd_hbm, wd_scr, wd_sem, first)
        hn = _rms_fwd(x_ref[...], gp_ref[...]).astype(MXU_DTYPE)
        h_ref[...] = hn
        for kf in range(N_DEV):
            cols = slice(kf * COL_SHARD, (kf + 1) * COL_SHARD)
            up = _dot(hn, wu_scr[kf])
            up_ref[:, cols] = up.astype(MXU_DTYPE)
            r = jnp.maximum(up, 0.0)
            act_scr[:, cols] = (r * r).astype(MXU_DTYPE)
        ff = _dot(act_scr[...], wd_scr[...].reshape(D_FF, D))
        ff_ref[...] = ff
        out = x_ref[...] + _rms_fwd(ff, gq_ref[...])
        if target is None:
            out_ref[...] = out
        else:
            @pl.when(first)
            def _():
                loss_ref[...] = jnp.zeros_like(loss_ref)

            row = lax.broadcasted_iota(jnp.int32, (tm, D), 0) + pl.program_id(0) * tm
            err = jnp.where(row >= BLK, out - t_ref[...], 0.0)
            out_ref[...] = err * (1.0 / D)
            loss_ref[...] += jnp.sum(err * err) * (0.5 / D)

    row_d = pl.BlockSpec((tm, D), lambda i: (i, 0))
    gain = pl.BlockSpec((None, 1, D), lambda i: (layer, 0, 0))
    with_loss = target is not None
    return _pcall(
        body, name=f"mlp_fwd_l{layer}", grid=(rows // tm,),
        in_specs=[row_d, gain, gain, pl.BlockSpec(memory_space=pl.ANY), pl.BlockSpec(memory_space=pl.ANY)]
        + [row_d] * with_loss,
        out_specs=[pl.BlockSpec((tm, D_FF), lambda i: (i, 0)), row_d, row_d, row_d]
        + [pl.BlockSpec((1, 128), lambda i: (0, 0))] * with_loss,
        out_shape=[SDS((rows, D_FF), MXU_DTYPE), SDS((rows, D), MXU_DTYPE), SDS((rows, D), F32), SDS((rows, D), F32)]
        + [SDS((1, 128), F32)] * with_loss,
        scratch_shapes=[pltpu.VMEM((tm, D_FF), MXU_DTYPE),
                        pltpu.VMEM((N_DEV, D, COL_SHARD), MXU_DTYPE), pltpu.VMEM((N_DEV, COL_SHARD, D), MXU_DTYPE),
                        pltpu.SemaphoreType.DMA((N_DEV,)), pltpu.SemaphoreType.DMA((N_DEV,))],
        compiler_params=_cparams("arbitrary"),
    )(hres, gain_pre3, gain_post3, w_up_g, w_down_g, *([target] if with_loss else []))


def _relu_squared(up):
    r = jnp.maximum(up.astype(F32), 0.0)
    return (r * r).astype(MXU_DTYPE)


def _matmul_tn(a, b, name, dev_major_cols=None, a_fn=None):
    rows, ka = a.shape
    n = b.shape[1]
    ta = min(ka, 1024)
    tn = 1024 if n % 1024 == 0 else 512
    tr = _row_tile(rows, 1664)
    n_r = rows // tr

    def body(a_ref, b_ref, o_ref, acc):
        r = pl.program_id(2)

        @pl.when(r == 0)
        def _():
            acc[...] = jnp.zeros_like(acc)

        a_blk = a_ref[...] if a_fn is None else a_fn(a_ref[...])
        acc[...] += _dot_tn(a_blk, b_ref[...])

        @pl.when(r == n_r - 1)
        def _():
            if dev_major_cols is None:
                o_ref[...] = acc[...].astype(XFER_DTYPE)
            else:
                for s in range(tn // dev_major_cols):
                    o_ref[s] = acc[:, s * dev_major_cols:(s + 1) * dev_major_cols].astype(XFER_DTYPE)

    if dev_major_cols is None:
        out_spec = pl.BlockSpec((ta, tn), lambda i, j, r: (i, j))
        out_shape = SDS((ka, n), XFER_DTYPE)
    else:
        w = dev_major_cols
        out_spec = pl.BlockSpec((tn // w, ta, w), lambda i, j, r: (j, i, 0))
        out_shape = SDS((n // w, ka, w), XFER_DTYPE)
    return _pcall(
        body, name=name, grid=(ka // ta, n // tn, n_r),
        in_specs=[pl.BlockSpec((tr, ta), lambda i, j, r: (r, i)), pl.BlockSpec((tr, tn), lambda i, j, r: (r, j))],
        out_specs=out_spec, out_shape=out_shape,
        scratch_shapes=[pltpu.VMEM((ta, tn), F32)],
        compiler_params=_cparams("parallel", "parallel", "arbitrary"),
    )(a, b)


def _dw_in(h, dproj_pieces, layer):
    rows = h.shape[0]
    tr = _row_tile(rows, 1664)
    n_r = rows // tr

    def body(h_ref, *refs):
        piece_refs, (o_ref, acc) = refs[:len(DPROJ_PIECES)], refs[len(DPROJ_PIECES):]
        j = pl.program_id(0)
        r = pl.program_id(1)

        @pl.when(r == 0)
        def _():
            acc[...] = jnp.zeros_like(acc)

        for piece_ref, (first, count) in zip(piece_refs, DPROJ_PIECES):
            @pl.when((j >= first) & (j < first + count))
            def _():
                acc[...] += _dot_tn(h_ref[...], piece_ref[...])

        @pl.when(r == n_r - 1)
        def _():
            o_ref[...] = acc[...].astype(XFER_DTYPE)

    def piece_spec(first, count):
        def index(j, r):
            mine = (j >= first) & (j < first + count)
            return jnp.where(mine, r, 0), jnp.clip(j - first, 0, count - 1)
        return pl.BlockSpec((tr, COL_SHARD), index)

    return _pcall(
        body, name=f"dw_in_l{layer}", grid=(N_DEV, n_r),
        in_specs=[pl.BlockSpec((tr, D), lambda j, r: (r, 0))] + [piece_spec(*p) for p in DPROJ_PIECES],
        out_specs=pl.BlockSpec((None, D, COL_SHARD), lambda j, r: (j, 0, 0)),
        out_shape=SDS((N_DEV, D, COL_SHARD), XFER_DTYPE),
        scratch_shapes=[pltpu.VMEM((D, COL_SHARD), F32)],
        compiler_params=_cparams("arbitrary", "arbitrary"),
    )(h, *dproj_pieces)


def _mlp_bwd(dout, ff, up, hres_mid, gain_pre3, gain_post3, w_up_g, w_down_g, layer):
    rows = dout.shape[0]
    tm = _row_tile(rows, 320)

    def body(do_ref, ff_ref, up_ref, x_ref, gp_ref, gq_ref, wu_hbm, wd_hbm,
             dff_ref, dup_ref, dx_ref, dgq_ref, dgp_ref, wut_scr, wdt_scr, wu_stage, wd_stage, wu_sem, wd_sem):
        i = pl.program_id(0)
        _load_resident_transposed(wu_hbm, wut_scr, wu_stage, wu_sem, i == 0)
        _load_resident_transposed(wd_hbm, wdt_scr, wd_stage, wd_sem, i == 0)

        @pl.when(i == 0)
        def _():
            dgq_ref[...] = jnp.zeros_like(dgq_ref)
            dgp_ref[...] = jnp.zeros_like(dgp_ref)

        dff, dg = _rms_bwd(ff_ref[...], gq_ref[...], do_ref[...])
        dgq_ref[...] += dg
        dffb = dff.astype(MXU_DTYPE)
        dff_ref[...] = dffb
        for kf in range(N_DEV):
            cols = slice(kf * COL_SHARD, (kf + 1) * COL_SHARD)
            dact = _dot(dffb, wdt_scr[kf])
            dup_ref[:, cols] = (dact * (2.0 * jnp.maximum(up_ref[:, cols].astype(F32), 0.0))).astype(MXU_DTYPE)
        dh = _dot(dup_ref[...], wut_scr[...].reshape(D_FF, D))
        dx, dg = _rms_bwd(x_ref[...], gp_ref[...], dh)
        dgp_ref[...] += dg
        dx_ref[...] = do_ref[...] + dx

    row_d = pl.BlockSpec((tm, D), lambda i: (i, 0))
    row_ff = pl.BlockSpec((tm, D_FF), lambda i: (i, 0))
    gain = pl.BlockSpec((None, 1, D), lambda i: (layer, 0, 0))
    dgain = pl.BlockSpec((1, D), lambda i: (0, 0))
    return _pcall(
        body, name=f"mlp_bwd_l{layer}", grid=(rows // tm,),
        in_specs=[row_d, row_d, row_ff, row_d, gain, gain,
                  pl.BlockSpec(memory_space=pl.ANY), pl.BlockSpec(memory_space=pl.ANY)],
        out_specs=[row_d, row_ff, row_d, dgain, dgain],
        out_shape=[SDS((rows, D), MXU_DTYPE), SDS((rows, D_FF), MXU_DTYPE), SDS((rows, D), F32),
                   SDS((1, D), F32), SDS((1, D), F32)],
        scratch_shapes=[pltpu.VMEM((N_DEV, COL_SHARD, D), MXU_DTYPE), pltpu.VMEM((N_DEV, D, COL_SHARD), MXU_DTYPE),
                        pltpu.VMEM((2, D, COL_SHARD), MXU_DTYPE), pltpu.VMEM((2, COL_SHARD, D), MXU_DTYPE),
                        pltpu.SemaphoreType.DMA((2,)), pltpu.SemaphoreType.DMA((2,))],
        compiler_params=_cparams("arbitrary"),
    )(dout, ff, up, hres_mid, gain_pre3, gain_post3, w_up_g, w_down_g)


def _merge_bwd(dhm, mix, y_ssm, y_attn, gates, w_o_ssm, w_o_attn, w_o_ssm_t, w_o_attn_t, w_out_t, gain3, layer):
    rows = dhm.shape[0]
    tm = _row_tile(rows, 320)

    def body(dh_ref, mix_ref, ys_ref, ya_ref, gs_ref, ga_ref, wos_ref, woa_ref, wost_ref, woat_ref, woutt_ref, g_ref,
             dmix_ref, da1_ref, da2_ref, dgs_ref, dga_ref, dys_ref, dya_ref, dg_ref):
        @pl.when(pl.program_id(0) == 0)
        def _():
            dg_ref[...] = jnp.zeros_like(dg_ref)

        dmix, dg = _rms_bwd(mix_ref[...], g_ref[...], dh_ref[...])
        dg_ref[...] += dg
        dmixb = dmix.astype(MXU_DTYPE)
        dmix_ref[...] = dmixb
        dmerged = _dot(dmixb, woutt_ref[...])
        sg_s = _sigmoid(gs_ref[...])
        sg_a = _sigmoid(ga_ref[...])
        da1 = (dmerged * sg_s).astype(MXU_DTYPE)
        da2 = (dmerged * sg_a).astype(MXU_DTYPE)
        da1_ref[...] = da1
        da2_ref[...] = da2
        a1 = _dot(ys_ref[...], wos_ref[...])
        a2 = _dot(ya_ref[...], woa_ref[...])
        dgs_ref[...] = (dmerged * a1 * (sg_s * (1.0 - sg_s))).astype(MXU_DTYPE)
        dga_ref[...] = (dmerged * a2 * (sg_a * (1.0 - sg_a))).astype(MXU_DTYPE)
        dys_ref[...] = _dot(da1, wost_ref[...])
        dya_ref[...] = _dot(da2, woat_ref[...])

    row_d = pl.BlockSpec((tm, D), lambda i: (i, 0))
    full = lambda shape: pl.BlockSpec(shape, lambda i: (0,) * len(shape))
    return _pcall(
        body, name=f"merge_bwd_l{layer}", grid=(rows // tm,),
        in_specs=[row_d, row_d, pl.BlockSpec((tm, D_SSM), lambda i: (i, 0)), row_d,
                  row_d, pl.BlockSpec((tm, D), lambda i: (i, 1)),
                  full((D_SSM, D)), full((D_ATTN, D)), full((D, D_SSM)), full((D, D_ATTN)), full((D, D)),
                  pl.BlockSpec((None, 1, D), lambda i: (layer, 0, 0))],
        out_specs=[row_d, row_d, row_d, row_d, row_d, pl.BlockSpec((tm, D_SSM), lambda i: (i, 0)), row_d,
                   pl.BlockSpec((1, D), lambda i: (0, 0))],
        out_shape=[SDS((rows, D), MXU_DTYPE)] * 5 + [SDS((rows, D_SSM), F32), SDS((rows, D_ATTN), F32),
                                                      SDS((1, D), F32)],
        compiler_params=_cparams("arbitrary"),
    )(dhm, mix, y_ssm, y_attn, gates, gates, w_o_ssm, w_o_attn, w_o_ssm_t, w_o_attn_t, w_out_t, gain3)


def _attn_bwd(q, k, v, d_out, sinks, cos, sin_a, sin_b, layer):
    rows = q.shape[0]
    n_blk = rows // BLK
    last = n_blk - 1

    def body(sink_ref, q_ref, km_ref, kp_ref, kc_ref, vm_ref, vp_ref, vc_ref, do_ref, c_ref, a_ref, b_ref,
             dqb_ref, dk_ref, dv_ref, dkm_ref, dvm_ref, ds_ref, dk_carry, dv_carry, dq_ref):
        i = pl.program_id(0)

        @pl.when(i == 0)
        def _():
            dkm_ref[...] = jnp.zeros_like(dkm_ref)
            dvm_ref[...] = jnp.zeros_like(dvm_ref)
            ds_ref[...] = jnp.zeros_like(ds_ref)
            dk_carry[...] = jnp.zeros_like(dk_carry)
            dv_carry[...] = jnp.zeros_like(dv_carry)

        @pl.when(i <= last)
        def _():
            bias = _attn_mask(i)
            for kvh in range(N_KV_HEADS):
                lanes = _head_lanes(kvh)
                k3 = jnp.concatenate([km_ref[:, lanes], kp_ref[:, lanes], kc_ref[:, lanes]], axis=0)
                v3 = jnp.concatenate([vm_ref[:, lanes], vp_ref[:, lanes], vc_ref[:, lanes]], axis=0)
                q4 = _group_rows(q_ref, kvh)
                do4 = _group_rows(do_ref, kvh).astype(MXU_DTYPE)
                p = _attn_probs(q4, k3, _group_bias(bias, sink_ref, layer, kvh))
                dp = _dot_nt(do4, v3)
                dsf = p * (dp - jnp.sum(dp * p, axis=-1, keepdims=True))
                dsc = dsf.astype(MXU_DTYPE)
                dv3 = _dot_tn(p.astype(MXU_DTYPE), do4)
                dk3 = _dot_tn(dsc, q4)
                dq4 = _dot(dsc, k3)
                for g in range(Q_PER_KV):
                    h = kvh * Q_PER_KV + g
                    dq_ref[:, _head_lanes(h)] = dq4[g * BLK:(g + 1) * BLK]
                    ds_ref[h:h + 1, :] += jnp.sum(dsf[g * BLK:(g + 1) * BLK, 0:BLK], axis=0, keepdims=True)
                dkm_ref[:, lanes] += dk3[0:BLK]
                dvm_ref[:, lanes] += dv3[0:BLK]
                dk_ref[:, lanes] = dk_carry[:, lanes] + dk3[BLK:2 * BLK]
                dv_ref[:, lanes] = dv_carry[:, lanes] + dv3[BLK:2 * BLK]
                dk_carry[:, lanes] = dk3[2 * BLK:3 * BLK]
                dv_carry[:, lanes] = dv3[2 * BLK:3 * BLK]
            c, a, b = c_ref[...], -a_ref[...], -b_ref[...]
            for t in range(D_ATTN // 128):
                lanes = slice(t * 128, (t + 1) * 128)
                dqb_ref[:, lanes] = (_rope_lanes(dq_ref[:, lanes], c, a, b) * ATTN_SCALE).astype(MXU_DTYPE)

        @pl.when(i == last + 1)
        def _():
            dk_ref[...] = dk_carry[...]
            dv_ref[...] = dv_carry[...]

    cur = lambda i: (jnp.minimum(i, last), 0)
    prev = lambda i: (jnp.clip(i - 1, 0, last), 0)
    kv_meta = pl.BlockSpec((BLK, D_KV), lambda i: (0, 0))
    kv_prev = pl.BlockSpec((BLK, D_KV), prev)
    kv_cur = pl.BlockSpec((BLK, D_KV), cur)
    tab = pl.BlockSpec((BLK, 128), cur)
    return _pcall(
        body, name=f"attn_bwd_l{layer}", grid=(n_blk + 1,),
        in_specs=[pl.BlockSpec(memory_space=pltpu.SMEM),
                  pl.BlockSpec((BLK, D_ATTN), cur),
                  kv_meta, kv_prev, kv_cur, kv_meta, kv_prev, kv_cur,
                  pl.BlockSpec((BLK, D_ATTN), cur), tab, tab, tab],
        out_specs=[pl.BlockSpec((BLK, D_ATTN), cur), kv_prev, kv_prev, kv_meta, kv_meta,
                   pl.BlockSpec((N_Q_HEADS, 128), lambda i: (0, 0))],
        out_shape=[SDS((rows, D_ATTN), MXU_DTYPE), SDS((rows, D_KV), F32), SDS((rows, D_KV), F32),
                   SDS((BLK, D_KV), F32), SDS((BLK, D_KV), F32), SDS((N_Q_HEADS, 128), F32)],
        scratch_shapes=[pltpu.VMEM((BLK, D_KV), F32), pltpu.VMEM((BLK, D_KV), F32), pltpu.VMEM((BLK, D_ATTN), F32)],
        compiler_params=_cparams("arbitrary"),
    )(sinks, q, k, k, k, v, v, v, d_out, cos, sin_a, sin_b)


def _rope_bwd(dk, dv, dk_meta, dv_meta, cos, sin_a, sin_b, layer):
    rows = dk.shape[0]
    tm = _row_tile(rows)

    def body(dk_ref, dv_ref, dkm_ref, dvm_ref, c_ref, a_ref, b_ref, o_ref):
        c, a, b = c_ref[...], -a_ref[...], -b_ref[...]
        for t in range(2):
            x = dk_ref[:, t * 128:(t + 1) * 128]
            o_ref[:, t * 128:(t + 1) * 128] = _rope_lanes(x, c, a, b).astype(MXU_DTYPE)
        o_ref[:, D_KV:] = dv_ref[...].astype(MXU_DTYPE)

        @pl.when(pl.program_id(0) == 0)
        def _():
            cb, ab, bb = c[0:BLK], a[0:BLK], b[0:BLK]
            is_meta = lax.broadcasted_iota(jnp.int32, (BLK, 128), 0) >= PAD_ROWS
            for t in range(2):
                x = dk_ref[0:BLK, t * 128:(t + 1) * 128] + jnp.where(is_meta, dkm_ref[:, t * 128:(t + 1) * 128], 0.0)
                o_ref[0:BLK, t * 128:(t + 1) * 128] = _rope_lanes(x, cb, ab, bb).astype(MXU_DTYPE)
                xv = dv_ref[0:BLK, t * 128:(t + 1) * 128] + jnp.where(is_meta, dvm_ref[:, t * 128:(t + 1) * 128], 0.0)
                o_ref[0:BLK, D_KV + t * 128:D_KV + (t + 1) * 128] = xv.astype(MXU_DTYPE)

    tab = pl.BlockSpec((tm, 128), lambda i: (i, 0))
    kv = pl.BlockSpec((tm, D_KV), lambda i: (i, 0))
    meta = pl.BlockSpec((BLK, D_KV), lambda i: (0, 0))
    return _pcall(
        body, name=f"rope_bwd_l{layer}", grid=(rows // tm,),
        in_specs=[kv, kv, meta, meta, tab, tab, tab],
        out_specs=pl.BlockSpec((tm, 2 * D_KV), lambda i: (i, 0)),
        out_shape=SDS((rows, 2 * D_KV), MXU_DTYPE),
        compiler_params=_cparams("parallel"),
    )(dk, dv, dk_meta, dv_meta, cos, sin_a, sin_b)


def _s5_bwd(d_gated, y, u, carry_in, ssm, w_glu, b_glu3, layer):
    rows = y.shape[0]
    n_chunks = rows // BLK
    b_mat, c_mat, t_re, t_im, d_skip = (ssm[k] for k in ("b_mat", "c_mat", "t_re", "t_im", "d_skip"))

    def body(dz_ref, y_ref, u_ref, cin_ref, bm_ref, cm_ref, tre_ref, tim_ref, d_ref, wg_ref, bg_ref,
             du_ref, dwg_ref, dbg_ref, dd_ref, dbm_ref, dcm_ref, dab_ref,
             lam_carry, bu_scr, s_scr, sp_scr, g_scr, lam_scr):
        step = pl.program_id(0)
        chunk = n_chunks - 1 - step

        @pl.when(step == 0)
        def _():
            for r in (dwg_ref, dbg_ref, dd_ref, dbm_ref, dcm_ref, dab_ref, lam_carry):
                r[...] = jnp.zeros_like(r)

        y = y_ref[...]
        u = u_ref[...]
        d_o = dz_ref[...]
        z, t = _gelu_parts(y)
        zb = z.astype(MXU_DTYPE)
        sg = _sigmoid(_dot(zb, wg_ref[...]) + bg_ref[...])
        dgl = d_o * z * (sg * (1.0 - sg))
        dglb = dgl.astype(MXU_DTYPE)
        dz = d_o * sg + _dot_nt(dglb, wg_ref[...])
        dwg_ref[...] += _dot_tn(zb, dglb)
        dbg_ref[...] += jnp.sum(dgl, axis=0, keepdims=True)
        dy = dz * _gelu_grad(y, t)
        dd_ref[...] += jnp.sum(dy * u, axis=0, keepdims=True)
        grow = lax.broadcasted_iota(jnp.int32, (BLK, 128), 0) + chunk * BLK
        ub = u.astype(MXU_DTYPE)
        dyb_all = dy.astype(MXU_DTYPE)
        for sb in range(N_SB):
            cols = slice(sb * 128, (sb + 1) * 128)
            bu_scr[sb] = _dot(ub[:, cols], bm_ref[sb])
            g_scr[sb] = _dot_nt(dyb_all[:, cols], cm_ref[sb])
        entering_s = [(cin_ref[2 * sb:2 * sb + 1, :], cin_ref[2 * sb + 1:2 * sb + 2, :]) for sb in range(N_SB)]
        entering_lam = [(lam_carry[2 * sb:2 * sb + 1, :], lam_carry[2 * sb + 1:2 * sb + 2, :]) for sb in range(N_SB)]
        _, leaving = _scan_tiles(tre_ref, tim_ref, (bu_scr, s_scr, entering_s, False, sp_scr),
                                 (g_scr, lam_scr, entering_lam, True, None))
        for sb in range(N_SB):
            cols = slice(sb * 128, (sb + 1) * 128)
            u_sb = ub[:, cols]
            dy_sb = dy[:, cols]
            dyb = dyb_all[:, cols]
            lam_carry[2 * sb:2 * sb + 1, :], lam_carry[2 * sb + 1:2 * sb + 2, :] = leaving[sb]
            dcm_ref[sb] += _dot_tn(s_scr[sb].astype(MXU_DTYPE), dyb)
            lr, li = lam_scr[sb, :, :SB_STATES], lam_scr[sb, :, SB_STATES:]
            spr, spi = sp_scr[sb, :, :SB_STATES], sp_scr[sb, :, SB_STATES:]
            dab_ref[2 * sb:2 * sb + 1, :] += jnp.sum(spr * lr + spi * li, axis=0, keepdims=True)
            dab_ref[2 * sb + 1:2 * sb + 2, :] += jnp.sum(spr * li - spi * lr, axis=0, keepdims=True)
            lam = lam_scr[sb].astype(MXU_DTYPE)
            dbm_ref[sb] += _dot_tn(u_sb, lam)
            du = _dot_nt(lam, bm_ref[sb]) + d_ref[:, cols] * dy_sb
            du_ref[:, cols] = jnp.where(grow >= PAD_ROWS, du, 0.0).astype(MXU_DTYPE)

    rev = lambda j: (n_chunks - 1 - j, 0)
    full = lambda shape: pl.BlockSpec(shape, lambda j: (0,) * len(shape))
    of_layer = lambda shape: pl.BlockSpec((None,) + shape, lambda j: (layer,) + (0,) * len(shape))
    tables = [of_layer((N_SB, 8, SCAN_TILE, SB_STATES))] * 2
    chunk_scratch = pltpu.VMEM((N_SB, BLK, 2 * SB_STATES), F32)
    return _pcall(
        body, name=f"s5_bwd_l{layer}", grid=(n_chunks,),
        in_specs=[pl.BlockSpec((BLK, D_SSM), rev), pl.BlockSpec((BLK, D_SSM), rev), pl.BlockSpec((BLK, D_SSM), rev),
                  pl.BlockSpec((None, 8, SB_STATES), lambda j: (n_chunks - 1 - j, 0, 0)),
                  of_layer((N_SB, 128, 2 * SB_STATES)), of_layer((N_SB, 2 * SB_STATES, 128))] + tables + [
                  of_layer((1, D_SSM)), full((D_SSM, D_SSM)),
                  pl.BlockSpec((None, 1, D_SSM), lambda j: (layer, 0, 0))],
        out_specs=[pl.BlockSpec((BLK, D_SSM), rev), full((D_SSM, D_SSM)), full((1, D_SSM)), full((1, D_SSM)),
                   full((N_SB, 128, 2 * SB_STATES)), full((N_SB, 2 * SB_STATES, 128)), full((8, SB_STATES))],
        out_shape=[SDS((rows, D_SSM), MXU_DTYPE), SDS((D_SSM, D_SSM), F32), SDS((1, D_SSM), F32), SDS((1, D_SSM), F32),
                   SDS((N_SB, 128, 2 * SB_STATES), F32), SDS((N_SB, 2 * SB_STATES, 128), F32), SDS((8, SB_STATES), F32)],
        scratch_shapes=[pltpu.VMEM((8, SB_STATES), F32)] + [chunk_scratch] * 5,
        compiler_params=_cparams("arbitrary"),
    )(d_gated, y, u, carry_in, b_mat, c_mat, t_re, t_im, d_skip, w_glu, b_glu3)


DPROJ_PIECES = ((0, 1), (1, 2), (3, 1), (4, 2), (6, 2))


def _in_bwd(dproj_pieces, dhm, hres, gain3, w_in_g, layer):
    rows = hres.shape[0]
    tm = _row_tile(rows)

    def body(*refs):
        piece_refs = refs[:len(DPROJ_PIECES)]
        dh_ref, x_ref, g_ref, w_hbm, dx_ref, dg_ref, wt_scr, w_stage, w_sem = refs[len(DPROJ_PIECES):]
        i = pl.program_id(0)
        _load_resident_transposed(w_hbm, wt_scr, w_stage, w_sem, i == 0)

        @pl.when(i == 0)
        def _():
            dg_ref[...] = jnp.zeros_like(dg_ref)

        dh = None
        for piece_ref, (first, count) in zip(piece_refs, DPROJ_PIECES):
            wt = wt_scr[first:first + count].reshape(count * COL_SHARD, D)
            part = _dot(piece_ref[...], wt)
            dh = part if dh is None else dh + part
        dx, dg = _rms_bwd(x_ref[...], g_ref[...], dh)
        dg_ref[...] += dg
        dx_ref[...] = dh_ref[...] + dx

    row_d = pl.BlockSpec((tm, D), lambda i: (i, 0))
    return _pcall(
        body, name=f"in_bwd_l{layer}", grid=(rows // tm,),
        in_specs=[pl.BlockSpec((tm, count * COL_SHARD), lambda i: (i, 0)) for _, count in DPROJ_PIECES] + [
                  row_d, row_d,
                  pl.BlockSpec((None, 1, D), lambda i: (layer, 0, 0)),
                  pl.BlockSpec(memory_space=pl.ANY)],
        out_specs=[row_d, pl.BlockSpec((1, D), lambda i: (0, 0))],
        out_shape=[SDS((rows, D), F32), SDS((1, D), F32)],
        scratch_shapes=[pltpu.VMEM((N_DEV, COL_SHARD, D), MXU_DTYPE),
                        pltpu.VMEM((2, D, COL_SHARD), MXU_DTYPE), pltpu.SemaphoreType.DMA((2,))],
        compiler_params=_cparams("arbitrary"),
    )(*dproj_pieces, dhm, hres, gain3, w_in_g)


_ADAM_C1 = 1.0 / (1.0 - ADAM_B1 ** ADAM_STEP)
_ADAM_C2 = 1.0 / (1.0 - ADAM_B2 ** ADAM_STEP)


def _adam_math(w, g, m, v):
    m = ADAM_B1 * m + (1.0 - ADAM_B1) * g
    v = ADAM_B2 * v + (1.0 - ADAM_B2) * (g * g)
    delta = -ADAM_LR * ((m * _ADAM_C1) / (jnp.sqrt(v * _ADAM_C2) + ADAM_EPS) + ADAM_WD * w)
    return delta, m, v


def _adamw_layers(parts0, parts1, w, m, v, name):
    _, rows, cols = w.shape
    tr = min(rows, (1 << 17) // cols)
    nt = rows // tr

    def body(p0_ref, p1_ref, w_ref, m_ref, v_ref, g_ref, d_ref, nm_ref, nv_ref):
        layer = pl.program_id(0)

        def run(p_ref):
            g = p_ref[0].astype(F32)
            for s in range(1, N_DEV):
                g = g + p_ref[s].astype(F32)
            delta, nm, nv = _adam_math(w_ref[...], g, m_ref[...], v_ref[...])
            g_ref[...] = g
            d_ref[...] = delta
            nm_ref[...] = nm
            nv_ref[...] = nv

        @pl.when(layer == 0)
        def _():
            run(p0_ref)

        @pl.when(layer == 1)
        def _():
            run(p1_ref)

    wspec = pl.BlockSpec((None, tr, cols), lambda l, i: (l, i, 0))
    return _pcall(
        body, name=name, grid=(2, nt),
        in_specs=[pl.BlockSpec((N_DEV, tr, cols), lambda l, i: (0, jnp.where(l == 0, i, nt - 1), 0)),
                  pl.BlockSpec((N_DEV, tr, cols), lambda l, i: (0, jnp.where(l == 1, i, 0), 0)),
                  wspec, wspec, wspec],
        out_specs=[wspec] * 4, out_shape=[SDS(w.shape, F32)] * 4,
        compiler_params=_cparams("arbitrary", "arbitrary"),
    )(parts0, parts1, w, m, v)


def _sum_slots(parts, name):
    def body(p_ref, o_ref):
        acc = p_ref[0]
        for s in range(1, N_DEV):
            acc = acc + p_ref[s]
        o_ref[...] = acc

    vmem = pl.BlockSpec(memory_space=pltpu.VMEM)
    return _pcall(body, name=name, out_shape=SDS(parts.shape[1:], F32), in_specs=[vmem], out_specs=vmem,
                  compiler_params=_cparams())(parts)


def _adamw_packed(g, w, m, v, name):
    def body(g_ref, w_ref, m_ref, v_ref, d_ref, nm_ref, nv_ref):
        delta, nm, nv = _adam_math(w_ref[...], g_ref[...], m_ref[...], v_ref[...])
        d_ref[...] = delta
        nm_ref[...] = nm
        nv_ref[...] = nv

    vmem = pl.BlockSpec(memory_space=pltpu.VMEM)
    return _pcall(body, name=name, out_shape=[SDS(g.shape, F32)] * 3, in_specs=[vmem] * 4, out_specs=[vmem] * 3,
                  compiler_params=_cparams())(g, w, m, v)


def _ssm_discretize(a_re, a_im, log_dt, b_re, b_im):
    dt = jnp.exp(log_dt)[:, None]
    mag = jnp.exp(a_re * dt)
    ang = a_im * dt
    ab_re, ab_im = mag * jnp.cos(ang), mag * jnp.sin(ang)
    xr, xi = ab_re - 1.0, ab_im
    den = a_re * a_re + a_im * a_im
    q_re = (xr * a_re + xi * a_im) / den
    q_im = (xi * a_re - xr * a_im) / den
    bb_re = q_re[..., None] * b_re - q_im[..., None] * b_im
    bb_im = q_re[..., None] * b_im + q_im[..., None] * b_re
    return ab_re, ab_im, bb_re, bb_im


def _block_diag_b(bb):
    m = jnp.einsum("sgnc,gh->sgchn", bb.reshape(N_SB, 8, N_STATE, GROUP_CH), jnp.eye(8, dtype=F32))
    return m.reshape(N_SB, 128, SB_STATES)


def _block_diag_b_t(dm):
    return jnp.einsum("sgchn,gh->sgnc", dm.reshape(N_SB, 8, GROUP_CH, 8, N_STATE),
                      jnp.eye(8, dtype=F32)).reshape(N_GROUPS, N_STATE, GROUP_CH)


def _block_diag_c(cc):
    m = jnp.einsum("sgcn,gh->sgnhc", cc.reshape(N_SB, 8, GROUP_CH, N_STATE), jnp.eye(8, dtype=F32))
    return m.reshape(N_SB, SB_STATES, 128)


def _block_diag_c_t(dm):
    return jnp.einsum("sgnhc,gh->sgcn", dm.reshape(N_SB, 8, N_STATE, 8, GROUP_CH),
                      jnp.eye(8, dtype=F32)).reshape(N_GROUPS, GROUP_CH, N_STATE)


def _ssm_tables(ab_re, ab_im, bb_re, bb_im, c_re, c_im, d_skip):
    pr, pi = ab_re.reshape(1, -1), ab_im.reshape(1, -1)
    cr, ci = pr, pi
    squares = []
    for _ in range(3):
        squares.append((cr, ci))
        pr, pi = (jnp.concatenate([pr, pr * cr - pi * ci], axis=0),
                  jnp.concatenate([pi, pr * ci + pi * cr], axis=0))
        cr, ci = cr * cr - ci * ci, 2.0 * cr * ci
    r = jnp.arange(SCAN_TILE)[:, None]
    fwd = [(jnp.where(r >= (1 << k), squares[k][0], 0.0), jnp.where(r >= (1 << k), squares[k][1], 0.0))
           for k in range(3)] + [(pr, pi)]
    rev = [(jnp.where(r < SCAN_TILE - (1 << k), squares[k][0], 0.0),
            jnp.where(r < SCAN_TILE - (1 << k), -squares[k][1], 0.0)) for k in range(3)] + [(pr[::-1], -pi[::-1])]
    table = lambda part: jnp.stack([e[part] for e in fwd + rev]).reshape(
        8, SCAN_TILE, N_SB, SB_STATES).transpose(2, 0, 1, 3)
    return dict(
        b_mat=jnp.concatenate([_block_diag_b(bb_re), _block_diag_b(bb_im)], axis=-1).astype(MXU_DTYPE),
        c_mat=jnp.concatenate([_block_diag_c(c_re), -_block_diag_c(c_im)], axis=1).astype(MXU_DTYPE),
        t_re=table(0), t_im=table(1),
        d_skip=d_skip.reshape(1, D_SSM))


def _rope_tables(rows):
    pos = (jnp.arange(rows, dtype=jnp.int32) - PAD_ROWS).astype(F32)
    inv_freq = 1.0 / (ROPE_THETA ** (jnp.arange(0, HEAD_DIM, 2, dtype=F32) / HEAD_DIM))
    ang = pos[:, None] * inv_freq[None, :]
    ang = jnp.concatenate([ang, ang, ang, ang], axis=-1)
    first_half = (jnp.arange(128) % HEAD_DIM) < HEAD_DIM // 2
    sin = jnp.sin(ang)
    return jnp.cos(ang), jnp.where(first_half, -sin, 0.0), jnp.where(first_half, 0.0, sin)


def _pack(arrays):
    flat = jnp.concatenate([a.reshape(-1).astype(F32) for a in arrays])
    pad = (-flat.shape[0]) % 1024
    return jnp.pad(flat, (0, pad)).reshape(-1, 128)


def _unpack(packed, like):
    flat = packed.reshape(-1)
    out, off = [], 0
    for a in like:
        n = math.prod(a.shape)
        out.append(flat[off:off + n].reshape(a.shape))
        off += n
    return out


BIG = ("w_in", "w_glu", "w_o_ssm", "w_o_attn", "w_out", "w_up", "w_down")
WEIGHTS = ("meta_tokens", "norm_mix_pre", "norm_mix_post", "norm_mlp_pre", "norm_mlp_post", "w_in",
           "ssm_a_re", "ssm_a_im", "ssm_log_dt", "ssm_b_re", "ssm_b_im", "ssm_c_re", "ssm_c_im", "ssm_d",
           "w_glu", "b_glu", "attn_sinks", "w_o_ssm", "w_o_attn", "w_out", "w_up", "w_down")
SMALL = tuple(n for n in WEIGHTS if n not in BIG)


def kernel(x, meta_tokens, norm_mix_pre, norm_mix_post, norm_mlp_pre, norm_mlp_post, w_in, ssm_a_re, ssm_a_im, ssm_log_dt, ssm_b_re, ssm_b_im, ssm_c_re, ssm_c_im, ssm_d, w_glu, b_glu, attn_sinks, w_o_ssm, w_o_attn, w_out, w_up, w_down, loss_target, m_meta_tokens, m_norm_mix_pre, m_norm_mix_post, m_norm_mlp_pre, m_norm_mlp_post, m_w_in, m_ssm_a_re, m_ssm_a_im, m_ssm_log_dt, m_ssm_b_re, m_ssm_b_im, m_ssm_c_re, m_ssm_c_im, m_ssm_d, m_w_glu, m_b_glu, m_attn_sinks, m_w_o_ssm, m_w_o_attn, m_w_out, m_w_up, m_w_down, v_meta_tokens, v_norm_mix_pre, v_norm_mix_post, v_norm_mlp_pre, v_norm_mlp_post, v_w_in, v_ssm_a_re, v_ssm_a_im, v_ssm_log_dt, v_ssm_b_re, v_ssm_b_im, v_ssm_c_re, v_ssm_c_im, v_ssm_d, v_w_glu, v_b_glu, v_attn_sinks, v_w_o_ssm, v_w_o_attn, v_w_out, v_w_up, v_w_down):
    args = locals()
    w = {n: args[n] for n in WEIGHTS}
    m = {n: args["m_" + n] for n in WEIGHTS}
    v = {n: args["v_" + n] for n in WEIGHTS}
    n_layers = w_in.shape[0]
    seq = x.shape[1]
    rows = seq + BLK
    my_slot = _slot(_mesh_pos())

    assert n_layers == 2
    xfer = {n: [w[n][l].astype(XFER_DTYPE) for l in range(n_layers)] for n in BIG}
    mixer_small = ("w_glu", "w_o_ssm", "w_o_attn", "w_out")
    meta_g, w_in_g0 = _exchange_by_sequencer([meta_tokens, xfer["w_in"][0]], True, 0, "gather_in0")
    mix0_g = _exchange_by_sequencer([xfer[n][0] for n in mixer_small], True, 1, "gather_mix0")
    meta_full = meta_g.transpose(1, 0, 2).reshape(N_META, D)

    def mixer_weights(w_glu_g, w_o_ssm_g, w_o_attn_g, w_out_g):
        return dict(w_glu=w_glu_g.reshape(D_SSM, D_SSM), w_o_ssm=w_o_ssm_g.transpose(1, 0, 2).reshape(D_SSM, D),
                    w_o_attn=w_o_attn_g.reshape(D_ATTN, D), w_out=w_out_g.reshape(D, D),
                    w_o_ssm_t=w_o_ssm_g.transpose(0, 2, 1).reshape(D, D_SSM),
                    w_o_attn_t=w_o_attn_g.reshape(D_ATTN, D).T, w_out_t=w_out_g.reshape(D, D).T)

    gathered = [dict(w_in=w_in_g0, **mixer_weights(*mix0_g)), {}]

    gains = {n: w[n].reshape(n_layers, 1, D) for n in ("norm_mix_pre", "norm_mix_post", "norm_mlp_pre", "norm_mlp_post")}
    b_glu3 = b_glu.reshape(n_layers, 1, D_SSM)
    cos, sin_a, sin_b = _rope_tables(rows)

    disc, disc_vjp = jax.vjp(jax.vmap(_ssm_discretize), ssm_a_re, ssm_a_im, ssm_log_dt, ssm_b_re, ssm_b_im)
    ssm = jax.vmap(_ssm_tables)(*disc, ssm_c_re, ssm_c_im, ssm_d)
    like = [w[n] for n in SMALL]
    packed_small_state = [_pack(like), _pack([m[n] for n in SMALL]), _pack([v[n] for n in SMALL])]
    early_setup = [ssm["b_mat"], ssm["c_mat"], ssm["t_re"], ssm["t_im"]] + packed_small_state

    hres = jnp.concatenate([jnp.zeros((PAD_ROWS, D), F32), meta_full, x[0]], axis=0)

    saved = []
    for l in range(n_layers):
        wl = gathered[l]
        u, gates, q, k, vv, h = _in_proj(hres, gains["norm_mix_pre"], wl["w_in"], cos, sin_a, sin_b, l,
                                         after=early_setup if l == 0 else ())
        if l == 0:
            wl["w_up"], wl["w_down"] = _exchange_by_sequencer([xfer["w_up"][0], xfer["w_down"][0]], True, 2,
                                                              "gather_mlp0", after=[h])
        y, y_ssm, carry_in = _s5_fwd(u, ssm, wl["w_glu"], b_glu3, l)
        if l == 0:
            l1_g = _exchange_by_sequencer([xfer[n][1] for n in ("w_in",) + mixer_small + ("w_up", "w_down")], True, 3,
                                          "gather_l1", after=[y, wl["w_up"]])
            gathered[1] = dict(w_in=l1_g[0], w_up=l1_g[5], w_down=l1_g[6], **mixer_weights(*l1_g[1:5]))
            last_exchange = l1_g[:1]
        y_attn = _attn_fwd(q, k, vv, attn_sinks, l)
        merged, mix, hres_mid = _merge_fwd(y_ssm, y_attn, gates, hres, wl["w_o_ssm"], wl["w_o_attn"], wl["w_out"],
                                           gains["norm_mix_post"], l)
        hres_in = hres
        if l + 1 < n_layers:
            up, h2, ff, hres = _mlp_fwd(hres_mid, gains["norm_mlp_pre"], gains["norm_mlp_post"], wl["w_up"],
                                        wl["w_down"], l)
        else:
            target = jnp.concatenate([jnp.zeros((BLK, D), F32), loss_target[0]], axis=0)
            up, h2, ff, dhres, loss_vec = _mlp_fwd(hres_mid, gains["norm_mlp_pre"], gains["norm_mlp_post"], wl["w_up"],
                                                   wl["w_down"], l, target=target)
        saved.append(dict(hres=hres_in, u=u, gates=gates, h=h, q=q, k=k, v=vv, y=y, y_ssm=y_ssm,
                          carry_in=carry_in, y_attn=y_attn, merged=merged, mix=mix, hres_mid=hres_mid,
                          up=up, h2=h2, ff=ff))

    small_grads = {}
    recv_up, recv_down, recv_mix = [None] * n_layers, [None] * n_layers, [None] * n_layers
    for l in reversed(range(n_layers)):
        s = saved[l]
        wl = gathered[l]
        dff, dup, dhm, dg_mlp_post, dg_mlp_pre = _mlp_bwd(dhres, s["ff"], s["up"], s["hres_mid"], gains["norm_mlp_pre"],
                                                          gains["norm_mlp_post"], wl["w_up"], wl["w_down"], l)
        dw_up = _matmul_tn(s["h2"], dup, f"dw_up_l{l}", dev_major_cols=COL_SHARD)
        recv_up[l] = _exchange_by_sequencer([dw_up], False, 4 + 3 * l, f"scatter_up{l}", after=last_exchange)
        dw_down = _matmul_tn(s["up"], dff, f"dw_down_l{l}", a_fn=_relu_squared).reshape(N_DEV, COL_SHARD, D)
        recv_down[l] = _exchange_by_sequencer([dw_down], False, 5 + 3 * l, f"scatter_down{l}", after=recv_up[l])
        last_exchange = recv_down[l]
        dmix, da1, da2, dgs, dga, dy_ssm, dy_attn, dg_mix_post = _merge_bwd(
            dhm, s["mix"], s["y_ssm"], s["y_attn"], s["gates"], wl["w_o_ssm"], wl["w_o_attn"], wl["w_o_ssm_t"],
            wl["w_o_attn_t"], wl["w_out_t"], gains["norm_mix_post"], l)
        dw_out = _matmul_tn(s["merged"], dmix, f"dw_out_l{l}").reshape(N_DEV, D // N_DEV, D)
        dw_o_attn = _matmul_tn(s["y_attn"], da2, f"dw_o_attn_l{l}").reshape(N_DEV, D_ATTN // N_DEV, D)
        dw_o_ssm = _matmul_tn(s["y_ssm"], da1, f"dw_o_ssm_l{l}", dev_major_cols=D // N_DEV)
        if l == 0:
            recv_out0 = _exchange_by_sequencer([dw_o_ssm, dw_o_attn, dw_out], False, 11, "scatter_out0",
                                               after=last_exchange)
            last_exchange = recv_out0[:1]
        dq, dk, dv, dk_meta, dv_meta, dsink = _attn_bwd(s["q"], s["k"], s["v"], dy_attn, attn_sinks, cos, sin_a, sin_b, l)
        dkv = _rope_bwd(dk, dv, dk_meta, dv_meta, cos, sin_a, sin_b, l)
        du, dw_glu, db_glu, dd_skip, db_mat, dc_mat, dab = _s5_bwd(dy_ssm, s["y"], s["u"], s["carry_in"], ssm,
                                                                    wl["w_glu"], b_glu3, l)
        dproj = (du, dq, dkv, dgs, dga)
        dw_in = _dw_in(s["h"], dproj, l)
        mix_parts = [dw_in, dw_glu.astype(XFER_DTYPE).reshape(N_DEV, D_SSM // N_DEV, D_SSM), dw_o_ssm, dw_o_attn, dw_out]
        if l > 0:
            recv_mix[l] = _exchange_by_sequencer(mix_parts, False, 6 + 3 * l, f"scatter_mix{l}", after=last_exchange)
            last_exchange = recv_mix[l][:1]
        else:
            recv_mix[0] = _exchange_by_sequencer(mix_parts[:2], False, 6, "scatter_in0", after=last_exchange) + recv_out0
            last_exchange = recv_mix[0][:1]
        dhres, dg_mix_pre = _in_bwd(dproj, dhm, s["hres"], gains["norm_mix_pre"], wl["w_in"], l)

        for name, val in (("norm_mix_pre", dg_mix_pre[0]), ("norm_mix_post", dg_mix_post[0]),
                          ("norm_mlp_pre", dg_mlp_pre[0]), ("norm_mlp_post", dg_mlp_post[0]),
                          ("dab", dab), ("db_mat", db_mat), ("dc_mat", dc_mat),
                          ("ssm_d", dd_skip.reshape(N_GROUPS, GROUP_CH)), ("b_glu", db_glu[0]),
                          ("attn_sinks", dsink[:, 0])):
            small_grads.setdefault(name, [None] * n_layers)[l] = val

    grad_x = dhres[BLK:][None]
    stacked = {n: jnp.stack(v) for n, v in small_grads.items()}
    dab = stacked["dab"].reshape(n_layers, N_SB, 2, SB_STATES)
    db_mat, dc_mat = stacked["db_mat"], stacked["dc_mat"]
    b_t, c_t = jax.vmap(_block_diag_b_t), jax.vmap(_block_diag_c_t)
    (stacked["ssm_a_re"], stacked["ssm_a_im"], stacked["ssm_log_dt"], stacked["ssm_b_re"],
     stacked["ssm_b_im"]) = disc_vjp((dab[:, :, 0].reshape(n_layers, N_GROUPS, N_STATE),
                                      dab[:, :, 1].reshape(n_layers, N_GROUPS, N_STATE),
                                      b_t(db_mat[..., :SB_STATES]), b_t(db_mat[..., SB_STATES:])))
    stacked["ssm_c_re"] = c_t(dc_mat[:, :, :SB_STATES])
    stacked["ssm_c_im"] = -c_t(dc_mat[:, :, SB_STATES:])
    small_names = [n for n in SMALL if n != "meta_tokens"]
    partial_small = [dhres[PAD_ROWS:BLK]] + [stacked[n] for n in small_names] + [loss_vec[0, :1]]
    small_parts, = _exchange_by_sequencer([_pack(partial_small)], True, 10, "gather_small", after=last_exchange)

    grads, delta, new_m, new_v = {}, {}, {}, {}

    def adamw_big(names, recv0, recv1):
        for n, p0, p1 in zip(names, recv0, recv1):
            grads[n], delta[n], new_m[n], new_v[n] = _adamw_layers(p0, p1, w[n], m[n], v[n], f"adamw_{n}")

    adamw_big(("w_up", "w_down"), recv_up[0] + recv_down[0], recv_up[1] + recv_down[1])
    summed = _unpack(_sum_slots(small_parts, "sum_small_grads"), partial_small)
    loss = summed[-1][0]
    grads.update(zip(small_names, summed[1:-1]))
    grads["meta_tokens"] = lax.dynamic_slice_in_dim(summed[0], my_slot * (D // N_DEV), D // N_DEV, axis=1)
    d_s, m_s, v_s = _adamw_packed(_pack([grads[n] for n in SMALL]), *packed_small_state, "adamw_small")
    adamw_big(("w_in",) + mixer_small, recv_mix[0], recv_mix[1])
    for n, dd, mm, vs in zip(SMALL, _unpack(d_s, like), _unpack(m_s, like), _unpack(v_s, like)):
        delta[n], new_m[n], new_v[n] = dd, mm, vs

    return (loss, grad_x, *[grads[n] for n in WEIGHTS], *[delta[n] for n in WEIGHTS],
            *[new_m[n] for n in WEIGHTS], *[new_v[n] for n in WEIGHTS])
```

```python
import functools
import math

import jax
import jax.numpy as jnp
from jax import lax
from jax.experimental import pallas as pl
from jax.experimental.pallas import tpu as pltpu
from jax.experimental.pallas import tpu_sc as plsc

F32 = jnp.float32
MXU_DTYPE = jnp.bfloat16
XFER_DTYPE = MXU_DTYPE
_pcall = pl.pallas_call
SDS = jax.ShapeDtypeStruct

D = 1024
D_SSM = 512
D_ATTN = 1024
D_KV = 256
D_FF = 4096
D_IN = 4096
HEAD_DIM = 64
N_Q_HEADS = 16
N_KV_HEADS = 4
Q_PER_KV = 4
N_META = 16
BLK = 128
PAD_ROWS = BLK - N_META
N_GROUPS = 32
N_STATE = 64
GROUP_CH = 16
N_SB = 4
SB_STATES = 512
ROPE_THETA = 10000.0
ATTN_SCALE = HEAD_DIM ** -0.5
NEG_INF = -1e30
RMS_EPS = 1e-6
N_DEV = 8
COL_SHARD = 512

ADAM_LR = 0.001
ADAM_B1 = 0.9
ADAM_B2 = 0.999
ADAM_EPS = 1e-08
ADAM_WD = 0.01
ADAM_STEP = 10

VMEM_LIMIT = 56 * 1024 * 1024

_NT = (((1,), (1,)), ((), ()))
_TN = (((0,), (0,)), ((), ()))


def _cparams(*sem):
    return pltpu.CompilerParams(dimension_semantics=tuple(sem) if sem else None,
                                vmem_limit_bytes=VMEM_LIMIT)


def _row_tile(rows, cap=640):
    for t in (1664, 640, 512, 320, 256, 128):
        if t <= cap and rows % t == 0:
            return t
    raise ValueError(f"unsupported row count {rows}")


def _dot(a, b):
    return jnp.dot(a, b, preferred_element_type=F32)


def _dot_nt(a, b):
    return lax.dot_general(a, b, _NT, preferred_element_type=F32)


def _dot_tn(a, b):
    return lax.dot_general(a, b, _TN, preferred_element_type=F32)


def _sigmoid(x):
    return 1.0 / (1.0 + jnp.exp(-x))


_GELU_C = math.sqrt(2.0 / math.pi)


def _gelu_parts(y):
    t = jnp.tanh(_GELU_C * (y + 0.044715 * (y * y * y)))
    return 0.5 * y * (1.0 + t), t


def _gelu_grad(y, t):
    return 0.5 * (1.0 + t) + 0.5 * y * (1.0 - t * t) * (_GELU_C * (1.0 + 0.134145 * (y * y)))


def _rms_fwd(x, gain):
    r = lax.rsqrt(jnp.mean(x * x, axis=-1, keepdims=True) + RMS_EPS)
    return (x * r) * gain


def _rms_bwd(x, gain, dout):
    r = lax.rsqrt(jnp.mean(x * x, axis=-1, keepdims=True) + RMS_EPS)
    xh = x * r
    dxh = dout * gain
    dx = r * (dxh - xh * jnp.mean(dxh * xh, axis=-1, keepdims=True))
    return dx, jnp.sum(dout * xh, axis=0, keepdims=True)


def _mesh_pos():
    return lax.axis_index("x"), lax.axis_index("y"), lax.axis_index("c")


def _peer(pos, d):
    x, y, c = pos
    return (1 - x if d & 4 else x, 1 - y if d & 2 else y, 1 - c if d & 1 else c)


def _slot(pos):
    return 4 * pos[0] + 2 * pos[1] + pos[2]


def _exchange_copy(gather, src_ref, land_ref, sems, k, d, me, send_side):
    peer = _peer(me, d)
    sender = me if send_side else peer
    src = src_ref if gather else src_ref.at[_slot(peer) if send_side else _slot(me)]
    return pltpu.make_async_remote_copy(
        src_ref=src, dst_ref=land_ref.at[_slot(sender)],
        send_sem=sems[0].at[k * (N_DEV - 1) + d - 1], recv_sem=sems[1].at[k * (N_DEV - 1) + d - 1],
        device_id=peer, device_id_type=pl.DeviceIdType.MESH)


def _exchange_by_sequencer(srcs, gather, collective_id, name, after=()):
    n = len(srcs)
    flags = [gather] * n if isinstance(gather, bool) else list(gather)
    land_types = [SDS(((N_DEV,) + s.shape) if g else s.shape, s.dtype) for s, g in zip(srcs, flags)]

    def body(*refs):
        src_refs = refs[:n]
        land_refs = refs[n + len(after):2 * n + len(after)]
        sems = refs[2 * n + len(after):2 * n + len(after) + 2]
        local_sems = refs[2 * n + len(after) + 2]
        me = _mesh_pos()
        barrier = pltpu.get_barrier_semaphore()
        for d in range(1, N_DEV):
            pl.semaphore_signal(barrier, inc=1, device_id=_peer(me, d), device_id_type=pl.DeviceIdType.MESH)
        pl.semaphore_wait(barrier, N_DEV - 1)
        own = [pltpu.make_async_copy(src_refs[k] if flags[k] else src_refs[k].at[_slot(me)],
                                     land_refs[k].at[_slot(me)], local_sems.at[k]) for k in range(n)]
        for cp in own:
            cp.start()
        for k in range(n):
            for d in range(1, N_DEV):
                _exchange_copy(flags[k], src_refs[k], land_refs[k], sems, k, d, me, True).start()
        for cp in own:
            cp.wait()
        for k in range(n):
            for d in range(1, N_DEV):
                _exchange_copy(flags[k], src_refs[k], land_refs[k], sems, k, d, me, True).wait_send()
        for k in range(n):
            for d in range(1, N_DEV):
                _exchange_copy(flags[k], src_refs[k], land_refs[k], sems, k, d, me, False).wait_recv()

    sem_type = pltpu.SemaphoreType.DMA((n * (N_DEV - 1),))
    return pl.kernel(
        body, out_type=land_types, mesh=plsc.ScalarSubcoreMesh(axis_name="sequencer", num_cores=1), name=name,
        scratch_types=(sem_type, sem_type, pltpu.SemaphoreType.DMA((n,))),
        compiler_params=pltpu.CompilerParams(collective_id=collective_id),
    )(*srcs, *after)


def _load_resident(w_hbm, w_scr, sems, first_step):
    @pl.when(first_step)
    def _():
        copies = [pltpu.make_async_copy(w_hbm.at[s], w_scr.at[s], sems.at[s]) for s in range(N_DEV)]
        for cp in copies:
            cp.start()
        for cp in copies:
            cp.wait()


def _load_resident_transposed(w_hbm, w_scr, stage, sems, first_step):
    @pl.when(first_step)
    def _():
        copies = [pltpu.make_async_copy(w_hbm.at[s], stage.at[s % 2], sems.at[s % 2]) for s in range(N_DEV)]
        copies[0].start()
        for s in range(N_DEV):
            if s + 1 < N_DEV:
                copies[s + 1].start()
            copies[s].wait()
            w_scr[s] = stage[s % 2].T


def _rope_lanes(t, cos, sin_a, sin_b):
    return t * cos + pltpu.roll(t, 96, 1) * sin_a + pltpu.roll(t, 32, 1) * sin_b


def _in_proj(hres, gain3, w_in_g, cos, sin_a, sin_b, layer, after=()):
    rows = hres.shape[0]
    tm = _row_tile(rows, 320)

    def body(x_ref, g_ref, w_hbm, c_ref, a_ref, b_ref, *refs):
        u_ref, gate_ref, q_ref, k_ref, v_ref, h_ref, w_scr, w_sem = refs[len(after):]
        _load_resident(w_hbm, w_scr, w_sem, pl.program_id(0) == 0)
        hn = _rms_fwd(x_ref[...], g_ref[...]).astype(MXU_DTYPE)
        h_ref[...] = hn
        c, a, b = c_ref[...], a_ref[...], b_ref[...]
        u_ref[...] = _dot(hn, w_scr[0])
        for shard in (1, 2):
            res = _dot(hn, w_scr[shard])
            for t in range(4):
                lanes = slice(t * 128, (t + 1) * 128)
                out = slice((shard - 1) * COL_SHARD + t * 128, (shard - 1) * COL_SHARD + (t + 1) * 128)
                q_ref[:, out] = (_rope_lanes(res[:, lanes], c, a, b) * ATTN_SCALE).astype(MXU_DTYPE)
        res = _dot(hn, w_scr[3])
        for t in range(2):
            lanes = slice(t * 128, (t + 1) * 128)
            k_ref[:, lanes] = _rope_lanes(res[:, lanes], c, a, b).astype(MXU_DTYPE)
        v_ref[...] = res[:, D_KV:].astype(MXU_DTYPE)
        for shard in range(4, N_DEV):
            gate_ref[:, (shard - 4) * COL_SHARD:(shard - 3) * COL_SHARD] = _dot(hn, w_scr[shard])

    tab = pl.BlockSpec((tm, 128), lambda i: (i, 0))
    kv = pl.BlockSpec((tm, D_KV), lambda i: (i, 0))
    row_d = pl.BlockSpec((tm, D), lambda i: (i, 0))
    return _pcall(
        body, name=f"in_proj_l{layer}", grid=(rows // tm,),
        in_specs=[row_d, pl.BlockSpec((None, 1, D), lambda i: (layer, 0, 0)),
                  pl.BlockSpec(memory_space=pl.ANY), tab, tab, tab] + [pl.BlockSpec(memory_space=pl.ANY)] * len(after),
        out_specs=[pl.BlockSpec((tm, D_SSM), lambda i: (i, 0)), pl.BlockSpec((tm, 2 * D), lambda i: (i, 0)),
                   row_d, kv, kv, row_d],
        out_shape=[SDS((rows, D_SSM), F32), SDS((rows, 2 * D), F32), SDS((rows, D_ATTN), MXU_DTYPE),
                   SDS((rows, D_KV), MXU_DTYPE), SDS((rows, D_KV), MXU_DTYPE), SDS((rows, D), MXU_DTYPE)],
        scratch_shapes=[pltpu.VMEM((N_DEV, D, COL_SHARD), MXU_DTYPE), pltpu.SemaphoreType.DMA((N_DEV,))],
        compiler_params=_cparams("arbitrary"),
    )(hres, gain3, w_in_g, cos, sin_a, sin_b, *after)


SCAN_TILE = 8


def _scan_tiles(tre_ref, tim_ref, *scans):
    n_tiles = BLK // SCAN_TILE
    row = lax.broadcasted_iota(jnp.int32, (SCAN_TILE, SB_STATES), 0)
    leaving = [list(scan[2]) for scan in scans]
    for step in range(n_tiles):
        for n, (x_scr, out_scr, _, reverse, prev_scr) in enumerate(scans):
            base = 4 if reverse else 0
            j = n_tiles - 1 - step if reverse else step
            rows = slice(SCAN_TILE * j, SCAN_TILE * (j + 1))
            for sb in range(N_SB):
                t_r, t_i = leaving[n][sb]
                xr = x_scr[sb, rows, :SB_STATES]
                xi = x_scr[sb, rows, SB_STATES:]
                for k in range(3):
                    shift = SCAN_TILE - (1 << k) if reverse else (1 << k)
                    rr = pltpu.roll(xr, shift, 0)
                    ri = pltpu.roll(xi, shift, 0)
                    ar = tre_ref[sb, base + k]
                    ai = tim_ref[sb, base + k]
                    xr, xi = xr + (ar * rr - ai * ri), xi + (ar * ri + ai * rr)
                pr = tre_ref[sb, base + 3]
                pi = tim_ref[sb, base + 3]
                xr, xi = xr + (pr * t_r - pi * t_i), xi + (pr * t_i + pi * t_r)
                out_scr[sb, rows, :SB_STATES] = xr
                out_scr[sb, rows, SB_STATES:] = xi
                if prev_scr is not None:
                    prev_scr[sb, rows, :SB_STATES] = jnp.where(row == 0, t_r, pltpu.roll(xr, 1, 0))
                    prev_scr[sb, rows, SB_STATES:] = jnp.where(row == 0, t_i, pltpu.roll(xi, 1, 0))
                edge = slice(0, 1) if reverse else slice(SCAN_TILE - 1, SCAN_TILE)
                leaving[n][sb] = (xr[edge], xi[edge])
    return leaving


def _s5_fwd(u, ssm, w_glu, b_glu3, layer):
    rows = u.shape[0]
    n_chunks = rows // BLK
    b_mat, c_mat, t_re, t_im, d_skip = (ssm[k] for k in ("b_mat", "c_mat", "t_re", "t_im", "d_skip"))

    def body(u_ref, bm_ref, cm_ref, tre_ref, tim_ref, d_ref, wg_ref, bg_ref,
             y_ref, ys_ref, cin_ref, carry, bu_scr, s_scr):
        @pl.when(pl.program_id(0) == 0)
        def _():
            carry[...] = jnp.zeros_like(carry)

        cin_ref[...] = carry[...]
        u = u_ref[...]
        for sb in range(N_SB):
            bu_scr[sb] = _dot(u[:, sb * 128:(sb + 1) * 128].astype(MXU_DTYPE), bm_ref[sb])
        entering = [(carry[2 * sb:2 * sb + 1, :], carry[2 * sb + 1:2 * sb + 2, :]) for sb in range(N_SB)]
        leaving, = _scan_tiles(tre_ref, tim_ref, (bu_scr, s_scr, entering, False, None))
        for sb in range(N_SB):
            cols = slice(sb * 128, (sb + 1) * 128)
            carry[2 * sb:2 * sb + 1, :], carry[2 * sb + 1:2 * sb + 2, :] = leaving[sb]
            y_ref[:, cols] = _dot(s_scr[sb].astype(MXU_DTYPE), cm_ref[sb]) + d_ref[:, cols] * u[:, cols]
        z, _ = _gelu_parts(y_ref[...])
        gl = _dot(z.astype(MXU_DTYPE), wg_ref[...]) + bg_ref[...]
        ys_ref[...] = (z * _sigmoid(gl)).astype(MXU_DTYPE)

    full = lambda shape: pl.BlockSpec(shape, lambda j: (0,) * len(shape))
    of_layer = lambda shape: pl.BlockSpec((None,) + shape, lambda j: (layer,) + (0,) * len(shape))
    return _pcall(
        body, name=f"s5_fwd_l{layer}", grid=(n_chunks,),
        in_specs=[pl.BlockSpec((BLK, D_SSM), lambda j: (j, 0)),
                  of_layer((N_SB, 128, 2 * SB_STATES)), of_layer((N_SB, 2 * SB_STATES, 128)),
                  of_layer((N_SB, 8, SCAN_TILE, SB_STATES)), of_layer((N_SB, 8, SCAN_TILE, SB_STATES)),
                  of_layer((1, D_SSM)), full((D_SSM, D_SSM)),
                  pl.BlockSpec((None, 1, D_SSM), lambda j: (layer, 0, 0))],
        out_specs=[pl.BlockSpec((BLK, D_SSM), lambda j: (j, 0)), pl.BlockSpec((BLK, D_SSM), lambda j: (j, 0)),
                   pl.BlockSpec((None, 8, SB_STATES), lambda j: (j, 0, 0))],
        out_shape=[SDS((rows, D_SSM), F32), SDS((rows, D_SSM), MXU_DTYPE), SDS((n_chunks, 8, SB_STATES), F32)],
        scratch_shapes=[pltpu.VMEM((8, SB_STATES), F32), pltpu.VMEM((N_SB, BLK, 2 * SB_STATES), F32),
                        pltpu.VMEM((N_SB, BLK, 2 * SB_STATES), F32)],
        compiler_params=_cparams("arbitrary"),
    )(u, b_mat, c_mat, t_re, t_im, d_skip, w_glu, b_glu3)


def _attn_mask(i):
    row = lax.broadcasted_iota(jnp.int32, (BLK, 3 * BLK), 0) + i * BLK
    col = lax.broadcasted_iota(jnp.int32, (BLK, 3 * BLK), 1)
    seg = jnp.right_shift(col, 7)
    c = jnp.bitwise_and(col, BLK - 1)
    kidx = c + (i + seg - 2) * BLK
    ok_meta = (seg == 0) & (c >= PAD_ROWS) & (row - c >= BLK)
    ok_win = (seg > 0) & (kidx >= PAD_ROWS) & (kidx <= row) & (row - kidx < BLK)
    return jnp.where(ok_meta | ok_win, 0.0, NEG_INF)


def _head_lanes(h):
    return slice(h * HEAD_DIM, (h + 1) * HEAD_DIM)


def _group_rows(ref, kvh):
    return jnp.concatenate([ref[:, _head_lanes(kvh * Q_PER_KV + g)] for g in range(Q_PER_KV)], axis=0)


def _group_bias(bias, sink_ref, layer, kvh):
    first_col = lax.broadcasted_iota(jnp.int32, (BLK, BLK), 1) == 0
    slabs = []
    for g in range(Q_PER_KV):
        first = jnp.where(first_col, sink_ref[layer, kvh * Q_PER_KV + g], bias[:, :BLK])
        slabs.append(jnp.concatenate([first, bias[:, BLK:]], axis=1))
    return jnp.concatenate(slabs, axis=0)


def _attn_probs(q4, k3, bias4):
    s = _dot_nt(q4, k3) + bias4
    e = jnp.exp(s - jnp.max(s, axis=-1, keepdims=True))
    return e * (1.0 / jnp.sum(e, axis=-1, keepdims=True))


def _attn_fwd(q, k, v, sinks, layer):
    rows = q.shape[0]
    n_blk = rows // BLK

    def body(sink_ref, q_ref, km_ref, kp_ref, kc_ref, vm_ref, vp_ref, vc_ref, o_ref):
        bias = _attn_mask(pl.program_id(0))
        for kvh in range(N_KV_HEADS):
            lanes = _head_lanes(kvh)
            k3 = jnp.concatenate([km_ref[:, lanes], kp_ref[:, lanes], kc_ref[:, lanes]], axis=0)
            v3 = jnp.concatenate([vm_ref[:, lanes], vp_ref[:, lanes], vc_ref[:, lanes]], axis=0)
            p = _attn_probs(_group_rows(q_ref, kvh), k3, _group_bias(bias, sink_ref, layer, kvh))
            o4 = _dot(p.astype(MXU_DTYPE), v3).astype(MXU_DTYPE)
            for g in range(Q_PER_KV):
                o_ref[:, _head_lanes(kvh * Q_PER_KV + g)] = o4[g * BLK:(g + 1) * BLK]

    kv_meta = pl.BlockSpec((BLK, D_KV), lambda i: (0, 0))
    kv_prev = pl.BlockSpec((BLK, D_KV), lambda i: (jnp.maximum(i - 1, 0), 0))
    kv_cur = pl.BlockSpec((BLK, D_KV), lambda i: (i, 0))
    return _pcall(
        body, name=f"attn_fwd_l{layer}", grid=(n_blk,),
        in_specs=[pl.BlockSpec(memory_space=pltpu.SMEM),
                  pl.BlockSpec((BLK, D_ATTN), lambda i: (i, 0)),
                  kv_meta, kv_prev, kv_cur, kv_meta, kv_prev, kv_cur],
        out_specs=pl.BlockSpec((BLK, D_ATTN), lambda i: (i, 0)),
        out_shape=SDS((rows, D_ATTN), MXU_DTYPE),
        compiler_params=_cparams("parallel"),
    )(sinks, q, k, k, k, v, v, v)


def _merge_fwd(y_ssm, y_attn, gates, hres, w_o_ssm, w_o_attn, w_out, gain3, layer):
    rows = hres.shape[0]
    tm = _row_tile(rows, 320)

    def body(ys_ref, ya_ref, gs_ref, ga_ref, x_ref, wos_ref, woa_ref, wout_ref, g_ref, mg_ref, out_ref):
        a1 = _dot(ys_ref[...], wos_ref[...])
        a2 = _dot(ya_ref[...], woa_ref[...])
        merged = (_sigmoid(gs_ref[...]) * a1 + _sigmoid(ga_ref[...]) * a2).astype(MXU_DTYPE)
        mg_ref[...] = merged
        out_ref[...] = x_ref[...] + _rms_fwd(_dot(merged, wout_ref[...]), g_ref[...])

    row_d = pl.BlockSpec((tm, D), lambda i: (i, 0))
    full = lambda shape: pl.BlockSpec(shape, lambda i: (0,) * len(shape))
    return _pcall(
        body, name=f"merge_fwd_l{layer}", grid=(rows // tm,),
        in_specs=[pl.BlockSpec((tm, D_SSM), lambda i: (i, 0)), row_d,
                  row_d, pl.BlockSpec((tm, D), lambda i: (i, 1)), row_d,
                  full((D_SSM, D)), full((D_ATTN, D)), full((D, D)),
                  pl.BlockSpec((None, 1, D), lambda i: (layer, 0, 0))],
        out_specs=[row_d, row_d],
        out_shape=[SDS((rows, D), MXU_DTYPE), SDS((rows, D), F32)],
        compiler_params=_cparams("parallel"),
    )(y_ssm, y_attn, gates, gates, hres, w_o_ssm, w_o_attn, w_out, gain3)


def _mlp_fwd(hres, gain_pre3, gain_post3, w_up_g, w_down_g, layer, target=None):
    rows = hres.shape[0]
    tm = _row_tile(rows, 320)

    def body(x_ref, gp_ref, gq_ref, wu_hbm, wd_hbm, *refs):
        if target is None:
            up_ref, h_ref, ff_ref, out_ref, act_scr, wu_scr, wd_scr, wu_sem, wd_sem = refs
        else:
            t_ref, up_ref, h_ref, ff_ref, out_ref, loss_ref, act_scr, wu_scr, wd_scr, wu_sem, wd_sem = refs
        first = pl.program_id(0) == 0
        _load_resident(wu_hbm, wu_scr, wu_sem, first)
        _load_resident(wd_hbm, wd_scr, wd_sem, first)
        hn = _rms_fwd(x_ref[...], gp_ref[...]).astype(MXU_DTYPE)
        h_ref[...] = hn
        for kf in range(N_DEV):
            cols = slice(kf * COL_SHARD, (kf + 1) * COL_SHARD)
            up = _dot(hn, wu_scr[kf])
            up_ref[:, cols] = up.astype(MXU_DTYPE)
            r = jnp.maximum(up, 0.0)
            act_scr[:, cols] = (r * r).astype(MXU_DTYPE)
        ff = _dot(act_scr[...], wd_scr[...].reshape(D_FF, D))
        ff_ref[...] = ff
        out = x_ref[...] + _rms_fwd(ff, gq_ref[...])
        if target is None:
            out_ref[...] = out
        else:
            @pl.when(first)
            def _():
                loss_ref[...] = jnp.zeros_like(loss_ref)

            row = lax.broadcasted_iota(jnp.int32, (tm, D), 0) + pl.program_id(0) * tm
            err = jnp.where(row >= BLK, out - t_ref[...], 0.0)
            out_ref[...] = err * (1.0 / D)
            loss_ref[...] += jnp.sum(err * err) * (0.5 / D)

    row_d = pl.BlockSpec((tm, D), lambda i: (i, 0))
    gain = pl.BlockSpec((None, 1, D), lambda i: (layer, 0, 0))
    with_loss = target is not None
    return _pcall(
        body, name=f"mlp_fwd_l{layer}", grid=(rows // tm,),
        in_specs=[row_d, gain, gain, pl.BlockSpec(memory_space=pl.ANY), pl.BlockSpec(memory_space=pl.ANY)]
        + [row_d] * with_loss,
        out_specs=[pl.BlockSpec((tm, D_FF), lambda i: (i, 0)), row_d, row_d, row_d]
        + [pl.BlockSpec((1, 128), lambda i: (0, 0))] * with_loss,
        out_shape=[SDS((rows, D_FF), MXU_DTYPE), SDS((rows, D), MXU_DTYPE), SDS((rows, D), F32), SDS((rows, D), F32)]
        + [SDS((1, 128), F32)] * with_loss,
        scratch_shapes=[pltpu.VMEM((tm, D_FF), MXU_DTYPE),
                        pltpu.VMEM((N_DEV, D, COL_SHARD), MXU_DTYPE), pltpu.VMEM((N_DEV, COL_SHARD, D), MXU_DTYPE),
                        pltpu.SemaphoreType.DMA((N_DEV,)), pltpu.SemaphoreType.DMA((N_DEV,))],
        compiler_params=_cparams("arbitrary"),
    )(hres, gain_pre3, gain_post3, w_up_g, w_down_g, *([target] if with_loss else []))


def _relu_squared(up):
    r = jnp.maximum(up.astype(F32), 0.0)
    return (r * r).astype(MXU_DTYPE)


def _matmul_tn(a, b, name, dev_major_cols=None, a_fn=None):
    rows, ka = a.shape
    n = b.shape[1]
    ta = min(ka, 1024)
    tn = 1024 if n % 1024 == 0 else 512
    tr = _row_tile(rows, 1664)
    n_r = rows // tr

    def body(a_ref, b_ref, o_ref, acc):
        r = pl.program_id(2)

        @pl.when(r == 0)
        def _():
            acc[...] = jnp.zeros_like(acc)

        a_blk = a_ref[...] if a_fn is None else a_fn(a_ref[...])
        acc[...] += _dot_tn(a_blk, b_ref[...])

        @pl.when(r == n_r - 1)
        def _():
            if dev_major_cols is None:
                o_ref[...] = acc[...].astype(XFER_DTYPE)
            else:
                for s in range(tn // dev_major_cols):
                    o_ref[s] = acc[:, s * dev_major_cols:(s + 1) * dev_major_cols].astype(XFER_DTYPE)

    if dev_major_cols is None:
        out_spec = pl.BlockSpec((ta, tn), lambda i, j, r: (i, j))
        out_shape = SDS((ka, n), XFER_DTYPE)
    else:
        w = dev_major_cols
        out_spec = pl.BlockSpec((tn // w, ta, w), lambda i, j, r: (j, i, 0))
        out_shape = SDS((n // w, ka, w), XFER_DTYPE)
    return _pcall(
        body, name=name, grid=(ka // ta, n // tn, n_r),
        in_specs=[pl.BlockSpec((tr, ta), lambda i, j, r: (r, i)), pl.BlockSpec((tr, tn), lambda i, j, r: (r, j))],
        out_specs=out_spec, out_shape=out_shape,
        scratch_shapes=[pltpu.VMEM((ta, tn), F32)],
        compiler_params=_cparams("parallel", "parallel", "arbitrary"),
    )(a, b)


def _dw_in(h, dproj_pieces, layer):
    rows = h.shape[0]
    tr = _row_tile(rows, 1664)
    n_r = rows // tr

    def body(h_ref, *refs):
        piece_refs, (o_ref, acc) = refs[:len(DPROJ_PIECES)], refs[len(DPROJ_PIECES):]
        j = pl.program_id(0)
        r = pl.program_id(1)

        @pl.when(r == 0)
        def _():
            acc[...] = jnp.zeros_like(acc)

        for piece_ref, (first, count) in zip(piece_refs, DPROJ_PIECES):
            @pl.when((j >= first) & (j < first + count))
            def _():
                acc[...] += _dot_tn(h_ref[...], piece_ref[...])

        @pl.when(r == n_r - 1)
        def _():
            o_ref[...] = acc[...].astype(XFER_DTYPE)

    def piece_spec(first, count):
        def index(j, r):
            mine = (j >= first) & (j < first + count)
            return jnp.where(mine, r, 0), jnp.clip(j - first, 0, count - 1)
        return pl.BlockSpec((tr, COL_SHARD), index)

    return _pcall(
        body, name=f"dw_in_l{layer}", grid=(N_DEV, n_r),
        in_specs=[pl.BlockSpec((tr, D), lambda j, r: (r, 0))] + [piece_spec(*p) for p in DPROJ_PIECES],
        out_specs=pl.BlockSpec((None, D, COL_SHARD), lambda j, r: (j, 0, 0)),
        out_shape=SDS((N_DEV, D, COL_SHARD), XFER_DTYPE),
        scratch_shapes=[pltpu.VMEM((D, COL_SHARD), F32)],
        compiler_params=_cparams("arbitrary", "arbitrary"),
    )(h, *dproj_pieces)


def _mlp_bwd(dout, ff, up, hres_mid, gain_pre3, gain_post3, w_up_g, w_down_g, layer):
    rows = dout.shape[0]
    tm = _row_tile(rows, 320)

    def body(do_ref, ff_ref, up_ref, x_ref, gp_ref, gq_ref, wu_hbm, wd_hbm,
             dff_ref, dup_ref, dx_ref, dgq_ref, dgp_ref, wut_scr, wdt_scr, wu_stage, wd_stage, wu_sem, wd_sem):
        i = pl.program_id(0)
        _load_resident_transposed(wu_hbm, wut_scr, wu_stage, wu_sem, i == 0)
        _load_resident_transposed(wd_hbm, wdt_scr, wd_stage, wd_sem, i == 0)

        @pl.when(i == 0)
        def _():
            dgq_ref[...] = jnp.zeros_like(dgq_ref)
            dgp_ref[...] = jnp.zeros_like(dgp_ref)

        dff, dg = _rms_bwd(ff_ref[...], gq_ref[...], do_ref[...])
        dgq_ref[...] += dg
        dffb = dff.astype(MXU_DTYPE)
        dff_ref[...] = dffb
        for kf in range(N_DEV):
            cols = slice(kf * COL_SHARD, (kf + 1) * COL_SHARD)
            dact = _dot(dffb, wdt_scr[kf])
            dup_ref[:, cols] = (dact * (2.0 * jnp.maximum(up_ref[:, cols].astype(F32), 0.0))).astype(MXU_DTYPE)
        dh = _dot(dup_ref[...], wut_scr[...].reshape(D_FF, D))
        dx, dg = _rms_bwd(x_ref[...], gp_ref[...], dh)
        dgp_ref[...] += dg
        dx_ref[...] = do_ref[...] + dx

    row_d = pl.BlockSpec((tm, D), lambda i: (i, 0))
    row_ff = pl.BlockSpec((tm, D_FF), lambda i: (i, 0))
    gain = pl.BlockSpec((None, 1, D), lambda i: (layer, 0, 0))
    dgain = pl.BlockSpec((1, D), lambda i: (0, 0))
    return _pcall(
        body, name=f"mlp_bwd_l{layer}", grid=(rows // tm,),
        in_specs=[row_d, row_d, row_ff, row_d, gain, gain,
                  pl.BlockSpec(memory_space=pl.ANY), pl.BlockSpec(memory_space=pl.ANY)],
        out_specs=[row_d, row_ff, row_d, dgain, dgain],
        out_shape=[SDS((rows, D), MXU_DTYPE), SDS((rows, D_FF), MXU_DTYPE), SDS((rows, D), F32),
                   SDS((1, D), F32), SDS((1, D), F32)],
        scratch_shapes=[pltpu.VMEM((N_DEV, COL_SHARD, D), MXU_DTYPE), pltpu.VMEM((N_DEV, D, COL_SHARD), MXU_DTYPE),
                        pltpu.VMEM((2, D, COL_SHARD), MXU_DTYPE), pltpu.VMEM((2, COL_SHARD, D), MXU_DTYPE),
                        pltpu.SemaphoreType.DMA((2,)), pltpu.SemaphoreType.DMA((2,))],
        compiler_params=_cparams("arbitrary"),
    )(dout, ff, up, hres_mid, gain_pre3, gain_post3, w_up_g, w_down_g)


def _merge_bwd(dhm, merged, y_ssm, y_attn, gates, w_o_ssm, w_o_attn, w_out, w_o_ssm_t, w_o_attn_t, w_out_t, gain3,
               layer):
    rows = dhm.shape[0]
    tm = _row_tile(rows, 320)

    def body(dh_ref, mg_ref, ys_ref, ya_ref, gs_ref, ga_ref, wos_ref, woa_ref, wout_ref, wost_ref, woat_ref, woutt_ref,
             g_ref, dmix_ref, da1_ref, da2_ref, dgs_ref, dga_ref, dys_ref, dya_ref, dg_ref):
        @pl.when(pl.program_id(0) == 0)
        def _():
            dg_ref[...] = jnp.zeros_like(dg_ref)

        dmix, dg = _rms_bwd(_dot(mg_ref[...], wout_ref[...]), g_ref[...], dh_ref[...])
        dg_ref[...] += dg
        dmixb = dmix.astype(MXU_DTYPE)
        dmix_ref[...] = dmixb
        dmerged = _dot(dmixb, woutt_ref[...])
        sg_s = _sigmoid(gs_ref[...])
        sg_a = _sigmoid(ga_ref[...])
        da1 = (dmerged * sg_s).astype(MXU_DTYPE)
        da2 = (dmerged * sg_a).astype(MXU_DTYPE)
        da1_ref[...] = da1
        da2_ref[...] = da2
        a1 = _dot(ys_ref[...], wos_ref[...])
        a2 = _dot(ya_ref[...], woa_ref[...])
        dgs_ref[...] = (dmerged * a1 * (sg_s * (1.0 - sg_s))).astype(MXU_DTYPE)
        dga_ref[...] = (dmerged * a2 * (sg_a * (1.0 - sg_a))).astype(MXU_DTYPE)
        dys_ref[...] = _dot(da1, wost_ref[...])
        dya_ref[...] = _dot(da2, woat_ref[...])

    row_d = pl.BlockSpec((tm, D), lambda i: (i, 0))
    full = lambda shape: pl.BlockSpec(shape, lambda i: (0,) * len(shape))
    return _pcall(
        body, name=f"merge_bwd_l{layer}", grid=(rows // tm,),
        in_specs=[row_d, row_d, pl.BlockSpec((tm, D_SSM), lambda i: (i, 0)), row_d,
                  row_d, pl.BlockSpec((tm, D), lambda i: (i, 1)),
                  full((D_SSM, D)), full((D_ATTN, D)), full((D, D)), full((D, D_SSM)), full((D, D_ATTN)), full((D, D)),
                  pl.BlockSpec((None, 1, D), lambda i: (layer, 0, 0))],
        out_specs=[row_d, row_d, row_d, row_d, row_d, pl.BlockSpec((tm, D_SSM), lambda i: (i, 0)), row_d,
                   pl.BlockSpec((1, D), lambda i: (0, 0))],
        out_shape=[SDS((rows, D), MXU_DTYPE)] * 5 + [SDS((rows, D_SSM), F32), SDS((rows, D_ATTN), F32),
                                                      SDS((1, D), F32)],
        compiler_params=_cparams("arbitrary"),
    )(dhm, merged, y_ssm, y_attn, gates, gates, w_o_ssm, w_o_attn, w_out, w_o_ssm_t, w_o_attn_t, w_out_t, gain3)


def _attn_bwd(q, k, v, d_out, sinks, cos, sin_a, sin_b, layer):
    rows = q.shape[0]
    n_blk = rows // BLK
    last = n_blk - 1

    def body(sink_ref, q_ref, km_ref, kp_ref, kc_ref, vm_ref, vp_ref, vc_ref, do_ref, c_ref, a_ref, b_ref,
             dqb_ref, dk_ref, dv_ref, dkm_ref, dvm_ref, ds_ref, dk_carry, dv_carry, dq_ref):
        i = pl.program_id(0)

        @pl.when(i == 0)
        def _():
            dkm_ref[...] = jnp.zeros_like(dkm_ref)
            dvm_ref[...] = jnp.zeros_like(dvm_ref)
            ds_ref[...] = jnp.zeros_like(ds_ref)
            dk_carry[...] = jnp.zeros_like(dk_carry)
            dv_carry[...] = jnp.zeros_like(dv_carry)

        @pl.when(i <= last)
        def _():
            bias = _attn_mask(i)
            for kvh in range(N_KV_HEADS):
                lanes = _head_lanes(kvh)
                k3 = jnp.concatenate([km_ref[:, lanes], kp_ref[:, lanes], kc_ref[:, lanes]], axis=0)
                v3 = jnp.concatenate([vm_ref[:, lanes], vp_ref[:, lanes], vc_ref[:, lanes]], axis=0)
                q4 = _group_rows(q_ref, kvh)
                do4 = _group_rows(do_ref, kvh).astype(MXU_DTYPE)
                p = _attn_probs(q4, k3, _group_bias(bias, sink_ref, layer, kvh))
                dp = _dot_nt(do4, v3)
                dsf = p * (dp - jnp.sum(dp * p, axis=-1, keepdims=True))
                dsc = dsf.astype(MXU_DTYPE)
                dv3 = _dot_tn(p.astype(MXU_DTYPE), do4)
                dk3 = _dot_tn(dsc, q4)
                dq4 = _dot(dsc, k3)
                for g in range(Q_PER_KV):
                    h = kvh * Q_PER_KV + g
                    dq_ref[:, _head_lanes(h)] = dq4[g * BLK:(g + 1) * BLK]
                    ds_ref[h:h + 1, :] += jnp.sum(dsf[g * BLK:(g + 1) * BLK, 0:BLK], axis=0, keepdims=True)
                dkm_ref[:, lanes] += dk3[0:BLK]
                dvm_ref[:, lanes] += dv3[0:BLK]
                dk_ref[:, lanes] = dk_carry[:, lanes] + dk3[BLK:2 * BLK]
                dv_ref[:, lanes] = dv_carry[:, lanes] + dv3[BLK:2 * BLK]
                dk_carry[:, lanes] = dk3[2 * BLK:3 * BLK]
                dv_carry[:, lanes] = dv3[2 * BLK:3 * BLK]
            c, a, b = c_ref[...], -a_ref[...], -b_ref[...]
            for t in range(D_ATTN // 128):
                lanes = slice(t * 128, (t + 1) * 128)
                dqb_ref[:, lanes] = (_rope_lanes(dq_ref[:, lanes], c, a, b) * ATTN_SCALE).astype(MXU_DTYPE)

        @pl.when(i == last + 1)
        def _():
            dk_ref[...] = dk_carry[...]
            dv_ref[...] = dv_carry[...]

    cur = lambda i: (jnp.minimum(i, last), 0)
    prev = lambda i: (jnp.clip(i - 1, 0, last), 0)
    kv_meta = pl.BlockSpec((BLK, D_KV), lambda i: (0, 0))
    kv_prev = pl.BlockSpec((BLK, D_KV), prev)
    kv_cur = pl.BlockSpec((BLK, D_KV), cur)
    tab = pl.BlockSpec((BLK, 128), cur)
    return _pcall(
        body, name=f"attn_bwd_l{layer}", grid=(n_blk + 1,),
        in_specs=[pl.BlockSpec(memory_space=pltpu.SMEM),
                  pl.BlockSpec((BLK, D_ATTN), cur),
                  kv_meta, kv_prev, kv_cur, kv_meta, kv_prev, kv_cur,
                  pl.BlockSpec((BLK, D_ATTN), cur), tab, tab, tab],
        out_specs=[pl.BlockSpec((BLK, D_ATTN), cur), kv_prev, kv_prev, kv_meta, kv_meta,
                   pl.BlockSpec((N_Q_HEADS, 128), lambda i: (0, 0))],
        out_shape=[SDS((rows, D_ATTN), MXU_DTYPE), SDS((rows, D_KV), F32), SDS((rows, D_KV), F32),
                   SDS((BLK, D_KV), F32), SDS((BLK, D_KV), F32), SDS((N_Q_HEADS, 128), F32)],
        scratch_shapes=[pltpu.VMEM((BLK, D_KV), F32), pltpu.VMEM((BLK, D_KV), F32), pltpu.VMEM((BLK, D_ATTN), F32)],
        compiler_params=_cparams("arbitrary"),
    )(sinks, q, k, k, k, v, v, v, d_out, cos, sin_a, sin_b)


def _rope_bwd(dk, dv, dk_meta, dv_meta, cos, sin_a, sin_b, layer):
    rows = dk.shape[0]
    tm = _row_tile(rows)

    def body(dk_ref, dv_ref, dkm_ref, dvm_ref, c_ref, a_ref, b_ref, o_ref):
        c, a, b = c_ref[...], -a_ref[...], -b_ref[...]
        for t in range(2):
            x = dk_ref[:, t * 128:(t + 1) * 128]
            o_ref[:, t * 128:(t + 1) * 128] = _rope_lanes(x, c, a, b).astype(MXU_DTYPE)
        o_ref[:, D_KV:] = dv_ref[...].astype(MXU_DTYPE)

        @pl.when(pl.program_id(0) == 0)
        def _():
            cb, ab, bb = c[0:BLK], a[0:BLK], b[0:BLK]
            is_meta = lax.broadcasted_iota(jnp.int32, (BLK, 128), 0) >= PAD_ROWS
            for t in range(2):
                x = dk_ref[0:BLK, t * 128:(t + 1) * 128] + jnp.where(is_meta, dkm_ref[:, t * 128:(t + 1) * 128], 0.0)
                o_ref[0:BLK, t * 128:(t + 1) * 128] = _rope_lanes(x, cb, ab, bb).astype(MXU_DTYPE)
                xv = dv_ref[0:BLK, t * 128:(t + 1) * 128] + jnp.where(is_meta, dvm_ref[:, t * 128:(t + 1) * 128], 0.0)
                o_ref[0:BLK, D_KV + t * 128:D_KV + (t + 1) * 128] = xv.astype(MXU_DTYPE)

    tab = pl.BlockSpec((tm, 128), lambda i: (i, 0))
    kv = pl.BlockSpec((tm, D_KV), lambda i: (i, 0))
    meta = pl.BlockSpec((BLK, D_KV), lambda i: (0, 0))
    return _pcall(
        body, name=f"rope_bwd_l{layer}", grid=(rows // tm,),
        in_specs=[kv, kv, meta, meta, tab, tab, tab],
        out_specs=pl.BlockSpec((tm, 2 * D_KV), lambda i: (i, 0)),
        out_shape=SDS((rows, 2 * D_KV), MXU_DTYPE),
        compiler_params=_cparams("parallel"),
    )(dk, dv, dk_meta, dv_meta, cos, sin_a, sin_b)


def _s5_bwd(d_gated, y, u, carry_in, ssm, w_glu, b_glu3, layer):
    rows = y.shape[0]
    n_chunks = rows // BLK
    b_mat, c_mat, t_re, t_im, d_skip = (ssm[k] for k in ("b_mat", "c_mat", "t_re", "t_im", "d_skip"))

    def body(dz_ref, y_ref, u_ref, cin_ref, bm_ref, cm_ref, tre_ref, tim_ref, d_ref, wg_ref, bg_ref,
             du_ref, dwg_ref, dbg_ref, dd_ref, dbm_ref, dcm_ref, dab_ref,
             lam_carry, bu_scr, s_scr, sp_scr, g_scr, lam_scr):
        step = pl.program_id(0)
        chunk = n_chunks - 1 - step

        @pl.when(step == 0)
        def _():
            for r in (dwg_ref, dbg_ref, dd_ref, dbm_ref, dcm_ref, dab_ref, lam_carry):
                r[...] = jnp.zeros_like(r)

        y = y_ref[...]
        u = u_ref[...]
        d_o = dz_ref[...]
        z, t = _gelu_parts(y)
        zb = z.astype(MXU_DTYPE)
        sg = _sigmoid(_dot(zb, wg_ref[...]) + bg_ref[...])
        dgl = d_o * z * (sg * (1.0 - sg))
        dglb = dgl.astype(MXU_DTYPE)
        dz = d_o * sg + _dot_nt(dglb, wg_ref[...])
        dwg_ref[...] += _dot_tn(zb, dglb)
        dbg_ref[...] += jnp.sum(dgl, axis=0, keepdims=True)
        dy = dz * _gelu_grad(y, t)
        dd_ref[...] += jnp.sum(dy * u, axis=0, keepdims=True)
        grow = lax.broadcasted_iota(jnp.int32, (BLK, 128), 0) + chunk * BLK
        ub = u.astype(MXU_DTYPE)
        dyb_all = dy.astype(MXU_DTYPE)
        for sb in range(N_SB):
            cols = slice(sb * 128, (sb + 1) * 128)
            bu_scr[sb] = _dot(ub[:, cols], bm_ref[sb])
            g_scr[sb] = _dot_nt(dyb_all[:, cols], cm_ref[sb])
        entering_s = [(cin_ref[2 * sb:2 * sb + 1, :], cin_ref[2 * sb + 1:2 * sb + 2, :]) for sb in range(N_SB)]
        entering_lam = [(lam_carry[2 * sb:2 * sb + 1, :], lam_carry[2 * sb + 1:2 * sb + 2, :]) for sb in range(N_SB)]
        _, leaving = _scan_tiles(tre_ref, tim_ref, (bu_scr, s_scr, entering_s, False, sp_scr),
                                 (g_scr, lam_scr, entering_lam, True, None))
        for sb in range(N_SB):
            cols = slice(sb * 128, (sb + 1) * 128)
            u_sb = ub[:, cols]
            dy_sb = dy[:, cols]
            dyb = dyb_all[:, cols]
            lam_carry[2 * sb:2 * sb + 1, :], lam_carry[2 * sb + 1:2 * sb + 2, :] = leaving[sb]
            dcm_ref[sb] += _dot_tn(s_scr[sb].astype(MXU_DTYPE), dyb)
            lr, li = lam_scr[sb, :, :SB_STATES], lam_scr[sb, :, SB_STATES:]
            spr, spi = sp_scr[sb, :, :SB_STATES], sp_scr[sb, :, SB_STATES:]
            dab_ref[2 * sb:2 * sb + 1, :] += jnp.sum(spr * lr + spi * li, axis=0, keepdims=True)
            dab_ref[2 * sb + 1:2 * sb + 2, :] += jnp.sum(spr * li - spi * lr, axis=0, keepdims=True)
            lam = lam_scr[sb].astype(MXU_DTYPE)
            dbm_ref[sb] += _dot_tn(u_sb, lam)
            du = _dot_nt(lam, bm_ref[sb]) + d_ref[:, cols] * dy_sb
            du_ref[:, cols] = jnp.where(grow >= PAD_ROWS, du, 0.0).astype(MXU_DTYPE)

    rev = lambda j: (n_chunks - 1 - j, 0)
    full = lambda shape: pl.BlockSpec(shape, lambda j: (0,) * len(shape))
    of_layer = lambda shape: pl.BlockSpec((None,) + shape, lambda j: (layer,) + (0,) * len(shape))
    tables = [of_layer((N_SB, 8, SCAN_TILE, SB_STATES))] * 2
    chunk_scratch = pltpu.VMEM((N_SB, BLK, 2 * SB_STATES), F32)
    return _pcall(
        body, name=f"s5_bwd_l{layer}", grid=(n_chunks,),
        in_specs=[pl.BlockSpec((BLK, D_SSM), rev), pl.BlockSpec((BLK, D_SSM), rev), pl.BlockSpec((BLK, D_SSM), rev),
                  pl.BlockSpec((None, 8, SB_STATES), lambda j: (n_chunks - 1 - j, 0, 0)),
                  of_layer((N_SB, 128, 2 * SB_STATES)), of_layer((N_SB, 2 * SB_STATES, 128))] + tables + [
                  of_layer((1, D_SSM)), full((D_SSM, D_SSM)),
                  pl.BlockSpec((None, 1, D_SSM), lambda j: (layer, 0, 0))],
        out_specs=[pl.BlockSpec((BLK, D_SSM), rev), full((D_SSM, D_SSM)), full((1, D_SSM)), full((1, D_SSM)),
                   full((N_SB, 128, 2 * SB_STATES)), full((N_SB, 2 * SB_STATES, 128)), full((8, SB_STATES))],
        out_shape=[SDS((rows, D_SSM), MXU_DTYPE), SDS((D_SSM, D_SSM), F32), SDS((1, D_SSM), F32), SDS((1, D_SSM), F32),
                   SDS((N_SB, 128, 2 * SB_STATES), F32), SDS((N_SB, 2 * SB_STATES, 128), F32), SDS((8, SB_STATES), F32)],
        scratch_shapes=[pltpu.VMEM((8, SB_STATES), F32)] + [chunk_scratch] * 5,
        compiler_params=_cparams("arbitrary"),
    )(d_gated, y, u, carry_in, b_mat, c_mat, t_re, t_im, d_skip, w_glu, b_glu3)


DPROJ_PIECES = ((0, 1), (1, 2), (3, 1), (4, 2), (6, 2))


def _in_bwd(dproj_pieces, dhm, hres, gain3, w_in_g, layer):
    rows = hres.shape[0]
    tm = _row_tile(rows)

    def body(*refs):
        piece_refs = refs[:len(DPROJ_PIECES)]
        dh_ref, x_ref, g_ref, w_hbm, dx_ref, dg_ref, wt_scr, w_stage, w_sem = refs[len(DPROJ_PIECES):]
        i = pl.program_id(0)
        _load_resident_transposed(w_hbm, wt_scr, w_stage, w_sem, i == 0)

        @pl.when(i == 0)
        def _():
            dg_ref[...] = jnp.zeros_like(dg_ref)

        dh = None
        for piece_ref, (first, count) in zip(piece_refs, DPROJ_PIECES):
            wt = wt_scr[first:first + count].reshape(count * COL_SHARD, D)
            part = _dot(piece_ref[...], wt)
            dh = part if dh is None else dh + part
        dx, dg = _rms_bwd(x_ref[...], g_ref[...], dh)
        dg_ref[...] += dg
        dx_ref[...] = dh_ref[...] + dx

    row_d = pl.BlockSpec((tm, D), lambda i: (i, 0))
    return _pcall(
        body, name=f"in_bwd_l{layer}", grid=(rows // tm,),
        in_specs=[pl.BlockSpec((tm, count * COL_SHARD), lambda i: (i, 0)) for _, count in DPROJ_PIECES] + [
                  row_d, row_d,
                  pl.BlockSpec((None, 1, D), lambda i: (layer, 0, 0)),
                  pl.BlockSpec(memory_space=pl.ANY)],
        out_specs=[row_d, pl.BlockSpec((1, D), lambda i: (0, 0))],
        out_shape=[SDS((rows, D), F32), SDS((1, D), F32)],
        scratch_shapes=[pltpu.VMEM((N_DEV, COL_SHARD, D), MXU_DTYPE),
                        pltpu.VMEM((2, D, COL_SHARD), MXU_DTYPE), pltpu.SemaphoreType.DMA((2,))],
        compiler_params=_cparams("arbitrary"),
    )(*dproj_pieces, dhm, hres, gain3, w_in_g)


_ADAM_C1 = 1.0 / (1.0 - ADAM_B1 ** ADAM_STEP)
_ADAM_C2 = 1.0 / (1.0 - ADAM_B2 ** ADAM_STEP)


def _adam_math(w, g, m, v):
    m = ADAM_B1 * m + (1.0 - ADAM_B1) * g
    v = ADAM_B2 * v + (1.0 - ADAM_B2) * (g * g)
    delta = -ADAM_LR * ((m * _ADAM_C1) / (jnp.sqrt(v * _ADAM_C2) + ADAM_EPS) + ADAM_WD * w)
    return delta, m, v


def _adamw_layers(parts0, parts1, w, m, v, name):
    _, rows, cols = w.shape
    tr = min(rows, (1 << 17) // cols)
    nt = rows // tr

    def body(p0_ref, p1_ref, w_ref, m_ref, v_ref, g_ref, d_ref, nm_ref, nv_ref):
        layer = pl.program_id(0)

        def run(p_ref):
            g = p_ref[0].astype(F32)
            for s in range(1, N_DEV):
                g = g + p_ref[s].astype(F32)
            delta, nm, nv = _adam_math(w_ref[...], g, m_ref[...], v_ref[...])
            g_ref[...] = g
            d_ref[...] = delta
            nm_ref[...] = nm
            nv_ref[...] = nv

        @pl.when(layer == 0)
        def _():
            run(p0_ref)

        @pl.when(layer == 1)
        def _():
            run(p1_ref)

    wspec = pl.BlockSpec((None, tr, cols), lambda l, i: (l, i, 0))
    return _pcall(
        body, name=name, grid=(2, nt),
        in_specs=[pl.BlockSpec((N_DEV, tr, cols), lambda l, i: (0, jnp.where(l == 0, i, nt - 1), 0)),
                  pl.BlockSpec((N_DEV, tr, cols), lambda l, i: (0, jnp.where(l == 1, i, 0), 0)),
                  wspec, wspec, wspec],
        out_specs=[wspec] * 4, out_shape=[SDS(w.shape, F32)] * 4,
        compiler_params=_cparams("arbitrary", "arbitrary"),
    )(parts0, parts1, w, m, v)


def _sum_slots(parts, name):
    def body(p_ref, o_ref):
        acc = p_ref[0]
        for s in range(1, N_DEV):
            acc = acc + p_ref[s]
        o_ref[...] = acc

    vmem = pl.BlockSpec(memory_space=pltpu.VMEM)
    return _pcall(body, name=name, out_shape=SDS(parts.shape[1:], F32), in_specs=[vmem], out_specs=vmem,
                  compiler_params=_cparams())(parts)


def _adamw_packed(g, w, m, v, name):
    def body(g_ref, w_ref, m_ref, v_ref, d_ref, nm_ref, nv_ref):
        delta, nm, nv = _adam_math(w_ref[...], g_ref[...], m_ref[...], v_ref[...])
        d_ref[...] = delta
        nm_ref[...] = nm
        nv_ref[...] = nv

    vmem = pl.BlockSpec(memory_space=pltpu.VMEM)
    return _pcall(body, name=name, out_shape=[SDS(g.shape, F32)] * 3, in_specs=[vmem] * 4, out_specs=[vmem] * 3,
                  compiler_params=_cparams())(g, w, m, v)


def _ssm_discretize(a_re, a_im, log_dt, b_re, b_im):
    dt = jnp.exp(log_dt)[:, None]
    mag = jnp.exp(a_re * dt)
    ang = a_im * dt
    ab_re, ab_im = mag * jnp.cos(ang), mag * jnp.sin(ang)
    xr, xi = ab_re - 1.0, ab_im
    den = a_re * a_re + a_im * a_im
    q_re = (xr * a_re + xi * a_im) / den
    q_im = (xi * a_re - xr * a_im) / den
    bb_re = q_re[..., None] * b_re - q_im[..., None] * b_im
    bb_im = q_re[..., None] * b_im + q_im[..., None] * b_re
    return ab_re, ab_im, bb_re, bb_im


def _block_diag_b(bb):
    m = jnp.einsum("sgnc,gh->sgchn", bb.reshape(N_SB, 8, N_STATE, GROUP_CH), jnp.eye(8, dtype=F32))
    return m.reshape(N_SB, 128, SB_STATES)


def _block_diag_b_t(dm):
    return jnp.einsum("sgchn,gh->sgnc", dm.reshape(N_SB, 8, GROUP_CH, 8, N_STATE),
                      jnp.eye(8, dtype=F32)).reshape(N_GROUPS, N_STATE, GROUP_CH)


def _block_diag_c(cc):
    m = jnp.einsum("sgcn,gh->sgnhc", cc.reshape(N_SB, 8, GROUP_CH, N_STATE), jnp.eye(8, dtype=F32))
    return m.reshape(N_SB, SB_STATES, 128)


def _block_diag_c_t(dm):
    return jnp.einsum("sgnhc,gh->sgcn", dm.reshape(N_SB, 8, N_STATE, 8, GROUP_CH),
                      jnp.eye(8, dtype=F32)).reshape(N_GROUPS, GROUP_CH, N_STATE)


def _ssm_tables(ab_re, ab_im, bb_re, bb_im, c_re, c_im, d_skip):
    pr, pi = ab_re.reshape(1, -1), ab_im.reshape(1, -1)
    cr, ci = pr, pi
    squares = []
    for _ in range(3):
        squares.append((cr, ci))
        pr, pi = (jnp.concatenate([pr, pr * cr - pi * ci], axis=0),
                  jnp.concatenate([pi, pr * ci + pi * cr], axis=0))
        cr, ci = cr * cr - ci * ci, 2.0 * cr * ci
    r = jnp.arange(SCAN_TILE)[:, None]
    fwd = [(jnp.where(r >= (1 << k), squares[k][0], 0.0), jnp.where(r >= (1 << k), squares[k][1], 0.0))
           for k in range(3)] + [(pr, pi)]
    rev = [(jnp.where(r < SCAN_TILE - (1 << k), squares[k][0], 0.0),
            jnp.where(r < SCAN_TILE - (1 << k), -squares[k][1], 0.0)) for k in range(3)] + [(pr[::-1], -pi[::-1])]
    table = lambda part: jnp.stack([e[part] for e in fwd + rev]).reshape(
        8, SCAN_TILE, N_SB, SB_STATES).transpose(2, 0, 1, 3)
    return dict(
        b_mat=jnp.concatenate([_block_diag_b(bb_re), _block_diag_b(bb_im)], axis=-1).astype(MXU_DTYPE),
        c_mat=jnp.concatenate([_block_diag_c(c_re), -_block_diag_c(c_im)], axis=1).astype(MXU_DTYPE),
        t_re=table(0), t_im=table(1),
        d_skip=d_skip.reshape(1, D_SSM))


def _rope_tables(rows):
    pos = (jnp.arange(rows, dtype=jnp.int32) - PAD_ROWS).astype(F32)
    inv_freq = 1.0 / (ROPE_THETA ** (jnp.arange(0, HEAD_DIM, 2, dtype=F32) / HEAD_DIM))
    ang = pos[:, None] * inv_freq[None, :]
    ang = jnp.concatenate([ang, ang, ang, ang], axis=-1)
    first_half = (jnp.arange(128) % HEAD_DIM) < HEAD_DIM // 2
    sin = jnp.sin(ang)
    return jnp.cos(ang), jnp.where(first_half, -sin, 0.0), jnp.where(first_half, 0.0, sin)


def _pack(arrays):
    flat = jnp.concatenate([a.reshape(-1).astype(F32) for a in arrays])
    pad = (-flat.shape[0]) % 1024
    return jnp.pad(flat, (0, pad)).reshape(-1, 128)


def _unpack(packed, like):
    flat = packed.reshape(-1)
    out, off = [], 0
    for a in like:
        n = math.prod(a.shape)
        out.append(flat[off:off + n].reshape(a.shape))
        off += n
    return out


BIG = ("w_in", "w_glu", "w_o_ssm", "w_o_attn", "w_out", "w_up", "w_down")
WEIGHTS = ("meta_tokens", "norm_mix_pre", "norm_mix_post", "norm_mlp_pre", "norm_mlp_post", "w_in",
           "ssm_a_re", "ssm_a_im", "ssm_log_dt", "ssm_b_re", "ssm_b_im", "ssm_c_re", "ssm_c_im", "ssm_d",
           "w_glu", "b_glu", "attn_sinks", "w_o_ssm", "w_o_attn", "w_out", "w_up", "w_down")
SMALL = tuple(n for n in WEIGHTS if n not in BIG)


def kernel(x, meta_tokens, norm_mix_pre, norm_mix_post, norm_mlp_pre, norm_mlp_post, w_in, ssm_a_re, ssm_a_im, ssm_log_dt, ssm_b_re, ssm_b_im, ssm_c_re, ssm_c_im, ssm_d, w_glu, b_glu, attn_sinks, w_o_ssm, w_o_attn, w_out, w_up, w_down, loss_target, m_meta_tokens, m_norm_mix_pre, m_norm_mix_post, m_norm_mlp_pre, m_norm_mlp_post, m_w_in, m_ssm_a_re, m_ssm_a_im, m_ssm_log_dt, m_ssm_b_re, m_ssm_b_im, m_ssm_c_re, m_ssm_c_im, m_ssm_d, m_w_glu, m_b_glu, m_attn_sinks, m_w_o_ssm, m_w_o_attn, m_w_out, m_w_up, m_w_down, v_meta_tokens, v_norm_mix_pre, v_norm_mix_post, v_norm_mlp_pre, v_norm_mlp_post, v_w_in, v_ssm_a_re, v_ssm_a_im, v_ssm_log_dt, v_ssm_b_re, v_ssm_b_im, v_ssm_c_re, v_ssm_c_im, v_ssm_d, v_w_glu, v_b_glu, v_attn_sinks, v_w_o_ssm, v_w_o_attn, v_w_out, v_w_up, v_w_down):
    args = locals()
    w = {n: args[n] for n in WEIGHTS}
    m = {n: args["m_" + n] for n in WEIGHTS}
    v = {n: args["v_" + n] for n in WEIGHTS}
    n_layers = w_in.shape[0]
    seq = x.shape[1]
    rows = seq + BLK
    my_slot = _slot(_mesh_pos())

    assert n_layers == 2
    xfer = {n: [w[n][l].astype(XFER_DTYPE) for l in range(n_layers)] for n in BIG}
    mixer_small = ("w_glu", "w_o_ssm", "w_o_attn", "w_out")
    meta_g, w_in_g0 = _exchange_by_sequencer([meta_tokens, xfer["w_in"][0]], True, 0, "gather_in0")
    mix0_g = _exchange_by_sequencer([xfer[n][0] for n in mixer_small], True, 1, "gather_mix0")
    meta_full = meta_g.transpose(1, 0, 2).reshape(N_META, D)

    def mixer_weights(w_glu_g, w_o_ssm_g, w_o_attn_g, w_out_g):
        return dict(w_glu=w_glu_g.reshape(D_SSM, D_SSM), w_o_ssm=w_o_ssm_g.transpose(1, 0, 2).reshape(D_SSM, D),
                    w_o_attn=w_o_attn_g.reshape(D_ATTN, D), w_out=w_out_g.reshape(D, D),
                    w_o_ssm_t=w_o_ssm_g.transpose(0, 2, 1).reshape(D, D_SSM),
                    w_o_attn_t=w_o_attn_g.reshape(D_ATTN, D).T, w_out_t=w_out_g.reshape(D, D).T)

    gathered = [dict(w_in=w_in_g0, **mixer_weights(*mix0_g)), {}]

    gains = {n: w[n].reshape(n_layers, 1, D) for n in ("norm_mix_pre", "norm_mix_post", "norm_mlp_pre", "norm_mlp_post")}
    b_glu3 = b_glu.reshape(n_layers, 1, D_SSM)
    cos, sin_a, sin_b = _rope_tables(rows)

    disc, disc_vjp = jax.vjp(jax.vmap(_ssm_discretize), ssm_a_re, ssm_a_im, ssm_log_dt, ssm_b_re, ssm_b_im)
    ssm = jax.vmap(_ssm_tables)(*disc, ssm_c_re, ssm_c_im, ssm_d)

    hres = jnp.concatenate([jnp.zeros((PAD_ROWS, D), F32), meta_full, x[0]], axis=0)

    saved = []
    for l in range(n_layers):
        wl = gathered[l]
        u, gates, q, k, vv, h = _in_proj(hres, gains["norm_mix_pre"], wl["w_in"], cos, sin_a, sin_b, l,
                                         after=[ssm["b_mat"], ssm["c_mat"], ssm["t_re"], ssm["t_im"]] if l == 0 else ())
        if l == 0:
            wl["w_up"], wl["w_down"] = _exchange_by_sequencer([xfer["w_up"][0], xfer["w_down"][0]], True, 2,
                                                              "gather_mlp0", after=[h])
        y, y_ssm, carry_in = _s5_fwd(u, ssm, wl["w_glu"], b_glu3, l)
        if l == 0:
            l1_g = _exchange_by_sequencer([xfer[n][1] for n in ("w_in",) + mixer_small + ("w_up", "w_down")], True, 3,
                                          "gather_l1", after=[y, wl["w_up"]])
            gathered[1] = dict(w_in=l1_g[0], w_up=l1_g[5], w_down=l1_g[6], **mixer_weights(*l1_g[1:5]))
            last_exchange = l1_g[:1]
        y_attn = _attn_fwd(q, k, vv, attn_sinks, l)
        merged, hres_mid = _merge_fwd(y_ssm, y_attn, gates, hres, wl["w_o_ssm"], wl["w_o_attn"], wl["w_out"],
                                      gains["norm_mix_post"], l)
        hres_in = hres
        if l + 1 < n_layers:
            up, h2, ff, hres = _mlp_fwd(hres_mid, gains["norm_mlp_pre"], gains["norm_mlp_post"], wl["w_up"],
                                        wl["w_down"], l)
        else:
            target = jnp.concatenate([jnp.zeros((BLK, D), F32), loss_target[0]], axis=0)
            up, h2, ff, dhres, loss_vec = _mlp_fwd(hres_mid, gains["norm_mlp_pre"], gains["norm_mlp_post"], wl["w_up"],
                                                   wl["w_down"], l, target=target)
        saved.append(dict(hres=hres_in, u=u, gates=gates, h=h, q=q, k=k, v=vv, y=y, y_ssm=y_ssm,
                          carry_in=carry_in, y_attn=y_attn, merged=merged, hres_mid=hres_mid,
                          up=up, h2=h2, ff=ff))

    small_grads = {}
    recv_up, recv_down, recv_mix = [None] * n_layers, [None] * n_layers, [None] * n_layers
    for l in reversed(range(n_layers)):
        s = saved[l]
        wl = gathered[l]
        dff, dup, dhm, dg_mlp_post, dg_mlp_pre = _mlp_bwd(dhres, s["ff"], s["up"], s["hres_mid"], gains["norm_mlp_pre"],
                                                          gains["norm_mlp_post"], wl["w_up"], wl["w_down"], l)
        dw_up = _matmul_tn(s["h2"], dup, f"dw_up_l{l}", dev_major_cols=COL_SHARD)
        recv_up[l] = _exchange_by_sequencer([dw_up], False, 4 + 3 * l, f"scatter_up{l}", after=last_exchange)
        dw_down = _matmul_tn(s["up"], dff, f"dw_down_l{l}", a_fn=_relu_squared).reshape(N_DEV, COL_SHARD, D)
        recv_down[l] = _exchange_by_sequencer([dw_down], False, 5 + 3 * l, f"scatter_down{l}", after=recv_up[l])
        last_exchange = recv_down[l]
        dmix, da1, da2, dgs, dga, dy_ssm, dy_attn, dg_mix_post = _merge_bwd(
            dhm, s["merged"], s["y_ssm"], s["y_attn"], s["gates"], wl["w_o_ssm"], wl["w_o_attn"], wl["w_out"],
            wl["w_o_ssm_t"], wl["w_o_attn_t"], wl["w_out_t"], gains["norm_mix_post"], l)
        dw_out = _matmul_tn(s["merged"], dmix, f"dw_out_l{l}").reshape(N_DEV, D // N_DEV, D)
        dw_o_attn = _matmul_tn(s["y_attn"], da2, f"dw_o_attn_l{l}").reshape(N_DEV, D_ATTN // N_DEV, D)
        dw_o_ssm = _matmul_tn(s["y_ssm"], da1, f"dw_o_ssm_l{l}", dev_major_cols=D // N_DEV)
        if l == 0:
            recv_out0 = _exchange_by_sequencer([dw_o_ssm, dw_o_attn, dw_out], False, 11, "scatter_out0",
                                               after=last_exchange)
            last_exchange = recv_out0[:1]
        dq, dk, dv, dk_meta, dv_meta, dsink = _attn_bwd(s["q"], s["k"], s["v"], dy_attn, attn_sinks, cos, sin_a, sin_b, l)
        dkv = _rope_bwd(dk, dv, dk_meta, dv_meta, cos, sin_a, sin_b, l)
        du, dw_glu, db_glu, dd_skip, db_mat, dc_mat, dab = _s5_bwd(dy_ssm, s["y"], s["u"], s["carry_in"], ssm,
                                                                    wl["w_glu"], b_glu3, l)
        dproj = (du, dq, dkv, dgs, dga)
        dw_in = _dw_in(s["h"], dproj, l)
        mix_parts = [dw_in, dw_glu.astype(XFER_DTYPE).reshape(N_DEV, D_SSM // N_DEV, D_SSM), dw_o_ssm, dw_o_attn, dw_out]
        if l > 0:
            recv_mix[l] = _exchange_by_sequencer(mix_parts, False, 6 + 3 * l, f"scatter_mix{l}", after=last_exchange)
            last_exchange = recv_mix[l][:1]
        else:
            recv_mix[0] = _exchange_by_sequencer(mix_parts[:2], False, 6, "scatter_in0", after=last_exchange) + recv_out0
            last_exchange = recv_mix[0][:1]
        dhres, dg_mix_pre = _in_bwd(dproj, dhm, s["hres"], gains["norm_mix_pre"], wl["w_in"], l)

        for name, val in (("norm_mix_pre", dg_mix_pre[0]), ("norm_mix_post", dg_mix_post[0]),
                          ("norm_mlp_pre", dg_mlp_pre[0]), ("norm_mlp_post", dg_mlp_post[0]),
                          ("dab", dab), ("db_mat", db_mat), ("dc_mat", dc_mat),
                          ("ssm_d", dd_skip.reshape(N_GROUPS, GROUP_CH)), ("b_glu", db_glu[0]),
                          ("attn_sinks", dsink[:, 0])):
            small_grads.setdefault(name, [None] * n_layers)[l] = val

    grad_x = dhres[BLK:][None]
    stacked = {n: jnp.stack(v) for n, v in small_grads.items()}
    dab = stacked["dab"].reshape(n_layers, N_SB, 2, SB_STATES)
    db_mat, dc_mat = stacked["db_mat"], stacked["dc_mat"]
    b_t, c_t = jax.vmap(_block_diag_b_t), jax.vmap(_block_diag_c_t)
    (stacked["ssm_a_re"], stacked["ssm_a_im"], stacked["ssm_log_dt"], stacked["ssm_b_re"],
     stacked["ssm_b_im"]) = disc_vjp((dab[:, :, 0].reshape(n_layers, N_GROUPS, N_STATE),
                                      dab[:, :, 1].reshape(n_layers, N_GROUPS, N_STATE),
                                      b_t(db_mat[..., :SB_STATES]), b_t(db_mat[..., SB_STATES:])))
    stacked["ssm_c_re"] = c_t(dc_mat[:, :, :SB_STATES])
    stacked["ssm_c_im"] = -c_t(dc_mat[:, :, SB_STATES:])
    small_names = [n for n in SMALL if n != "meta_tokens"]
    partial_small = [dhres[PAD_ROWS:BLK]] + [stacked[n] for n in small_names] + [loss_vec[0, :1]]
    small_parts, = _exchange_by_sequencer([_pack(partial_small)], True, 10, "gather_small", after=last_exchange)

    grads, delta, new_m, new_v = {}, {}, {}, {}

    def adamw_big(names, recv0, recv1):
        for n, p0, p1 in zip(names, recv0, recv1):
            grads[n], delta[n], new_m[n], new_v[n] = _adamw_layers(p0, p1, w[n], m[n], v[n], f"adamw_{n}")

    adamw_big(("w_up", "w_down"), recv_up[0] + recv_down[0], recv_up[1] + recv_down[1])
    summed = _unpack(_sum_slots(small_parts, "sum_small_grads"), partial_small)
    loss = summed[-1][0]
    grads.update(zip(small_names, summed[1:-1]))
    grads["meta_tokens"] = lax.dynamic_slice_in_dim(summed[0], my_slot * (D // N_DEV), D // N_DEV, axis=1)
    like = [w[n] for n in SMALL]
    d_s, m_s, v_s = _adamw_packed(_pack([grads[n] for n in SMALL]), _pack(like), _pack([m[n] for n in SMALL]),
                                  _pack([v[n] for n in SMALL]), "adamw_small")
    adamw_big(("w_in",) + mixer_small, recv_mix[0], recv_mix[1])
    for n, dd, mm, vs in zip(SMALL, _unpack(d_s, like), _unpack(m_s, like), _unpack(v_s, like)):
        delta[n], new_m[n], new_v[n] = dd, mm, vs

    return (loss, grad_x, *[grads[n] for n in WEIGHTS], *[delta[n] for n in WEIGHTS],
            *[new_m[n] for n in WEIGHTS], *[new_v[n] for n in WEIGHTS])
```

```python
import functools
import math

import jax
import jax.numpy as jnp
from jax import lax
from jax.experimental import pallas as pl
from jax.experimental.pallas import tpu as pltpu
from jax.experimental.pallas import tpu_sc as plsc

F32 = jnp.float32
MXU_DTYPE = jnp.bfloat16
XFER_DTYPE = MXU_DTYPE
_pcall = pl.pallas_call
SDS = jax.ShapeDtypeStruct

D = 1024
D_SSM = 512
D_ATTN = 1024
D_KV = 256
D_FF = 4096
D_IN = 4096
HEAD_DIM = 64
N_Q_HEADS = 16
N_KV_HEADS = 4
Q_PER_KV = 4
N_META = 16
BLK = 128
PAD_ROWS = BLK - N_META
N_GROUPS = 32
N_STATE = 64
GROUP_CH = 16
N_SB = 4
SB_STATES = 512
ROPE_THETA = 10000.0
ATTN_SCALE = HEAD_DIM ** -0.5
NEG_INF = -1e30
RMS_EPS = 1e-6
N_DEV = 8
COL_SHARD = 512

ADAM_LR = 0.001
ADAM_B1 = 0.9
ADAM_B2 = 0.999
ADAM_EPS = 1e-08
ADAM_WD = 0.01
ADAM_STEP = 10

VMEM_LIMIT = 56 * 1024 * 1024

_NT = (((1,), (1,)), ((), ()))
_TN = (((0,), (0,)), ((), ()))


def _cparams(*sem):
    return pltpu.CompilerParams(dimension_semantics=tuple(sem) if sem else None,
                                vmem_limit_bytes=VMEM_LIMIT)


def _row_tile(rows, cap=640):
    for t in (1664, 640, 512, 320, 256, 128):
        if t <= cap and rows % t == 0:
            return t
    raise ValueError(f"unsupported row count {rows}")


def _dot(a, b):
    return jnp.dot(a, b, preferred_element_type=F32)


def _dot_nt(a, b):
    return lax.dot_general(a, b, _NT, preferred_element_type=F32)


def _dot_tn(a, b):
    return lax.dot_general(a, b, _TN, preferred_element_type=F32)


def _sigmoid(x):
    return 1.0 / (1.0 + jnp.exp(-x))


_GELU_C = math.sqrt(2.0 / math.pi)


def _gelu_parts(y):
    t = jnp.tanh(_GELU_C * (y + 0.044715 * (y * y * y)))
    return 0.5 * y * (1.0 + t), t


def _gelu_grad(y, t):
    return 0.5 * (1.0 + t) + 0.5 * y * (1.0 - t * t) * (_GELU_C * (1.0 + 0.134145 * (y * y)))


def _rms_fwd(x, gain):
    r = lax.rsqrt(jnp.mean(x * x, axis=-1, keepdims=True) + RMS_EPS)
    return (x * r) * gain


def _rms_bwd(x, gain, dout):
    r = lax.rsqrt(jnp.mean(x * x, axis=-1, keepdims=True) + RMS_EPS)
    xh = x * r
    dxh = dout * gain
    dx = r * (dxh - xh * jnp.mean(dxh * xh, axis=-1, keepdims=True))
    return dx, jnp.sum(dout * xh, axis=0, keepdims=True)


def _mesh_pos():
    return lax.axis_index("x"), lax.axis_index("y"), lax.axis_index("c")


def _peer(pos, d):
    x, y, c = pos
    return (1 - x if d & 4 else x, 1 - y if d & 2 else y, 1 - c if d & 1 else c)


def _slot(pos):
    return 4 * pos[0] + 2 * pos[1] + pos[2]


def _exchange_copy(gather, src_ref, land_ref, sems, k, d, me, send_side):
    peer = _peer(me, d)
    sender = me if send_side else peer
    src = src_ref if gather else src_ref.at[_slot(peer) if send_side else _slot(me)]
    return pltpu.make_async_remote_copy(
        src_ref=src, dst_ref=land_ref.at[_slot(sender)],
        send_sem=sems[0].at[k * (N_DEV - 1) + d - 1], recv_sem=sems[1].at[k * (N_DEV - 1) + d - 1],
        device_id=peer, device_id_type=pl.DeviceIdType.MESH)


def _exchange_by_sequencer(srcs, gather, collective_id, name, after=()):
    n = len(srcs)
    flags = [gather] * n if isinstance(gather, bool) else list(gather)
    land_types = [SDS(((N_DEV,) + s.shape) if g else s.shape, s.dtype) for s, g in zip(srcs, flags)]

    def body(*refs):
        src_refs = refs[:n]
        land_refs = refs[n + len(after):2 * n + len(after)]
        sems = refs[2 * n + len(after):2 * n + len(after) + 2]
        local_sems = refs[2 * n + len(after) + 2]
        me = _mesh_pos()
        barrier = pltpu.get_barrier_semaphore()
        for d in range(1, N_DEV):
            pl.semaphore_signal(barrier, inc=1, device_id=_peer(me, d), device_id_type=pl.DeviceIdType.MESH)
        pl.semaphore_wait(barrier, N_DEV - 1)
        own = [pltpu.make_async_copy(src_refs[k] if flags[k] else src_refs[k].at[_slot(me)],
                                     land_refs[k].at[_slot(me)], local_sems.at[k]) for k in range(n)]
        for cp in own:
            cp.start()
        for k in range(n):
            for d in range(1, N_DEV):
                _exchange_copy(flags[k], src_refs[k], land_refs[k], sems, k, d, me, True).start()
        for cp in own:
            cp.wait()
        for k in range(n):
            for d in range(1, N_DEV):
                _exchange_copy(flags[k], src_refs[k], land_refs[k], sems, k, d, me, True).wait_send()
        for k in range(n):
            for d in range(1, N_DEV):
                _exchange_copy(flags[k], src_refs[k], land_refs[k], sems, k, d, me, False).wait_recv()

    sem_type = pltpu.SemaphoreType.DMA((n * (N_DEV - 1),))
    return pl.kernel(
        body, out_type=land_types, mesh=plsc.ScalarSubcoreMesh(axis_name="sequencer", num_cores=1), name=name,
        scratch_types=(sem_type, sem_type, pltpu.SemaphoreType.DMA((n,))),
        compiler_params=pltpu.CompilerParams(collective_id=collective_id),
    )(*srcs, *after)


def _load_resident(w_hbm, w_scr, sems, first_step):
    @pl.when(first_step)
    def _():
        copies = [pltpu.make_async_copy(w_hbm.at[s], w_scr.at[s], sems.at[s]) for s in range(N_DEV)]
        for cp in copies:
            cp.start()
        for cp in copies:
            cp.wait()


def _load_resident_transposed(w_hbm, w_scr, stage, sems, first_step):
    @pl.when(first_step)
    def _():
        copies = [pltpu.make_async_copy(w_hbm.at[s], stage.at[s % 2], sems.at[s % 2]) for s in range(N_DEV)]
        copies[0].start()
        for s in range(N_DEV):
            if s + 1 < N_DEV:
                copies[s + 1].start()
            copies[s].wait()
            w_scr[s] = stage[s % 2].T


def _rope_lanes(t, cos, sin_a, sin_b):
    return t * cos + pltpu.roll(t, 96, 1) * sin_a + pltpu.roll(t, 32, 1) * sin_b


def _in_proj(hres, gain3, w_in_g, cos, sin_a, sin_b, layer, after=()):
    rows = hres.shape[0]
    tm = _row_tile(rows, 320)

    def body(x_ref, g_ref, w_hbm, c_ref, a_ref, b_ref, *refs):
        u_ref, gate_ref, q_ref, k_ref, v_ref, h_ref, w_scr, w_sem = refs[len(after):]
        _load_resident(w_hbm, w_scr, w_sem, pl.program_id(0) == 0)
        hn = _rms_fwd(x_ref[...], g_ref[...]).astype(MXU_DTYPE)
        h_ref[...] = hn
        c, a, b = c_ref[...], a_ref[...], b_ref[...]
        u_ref[...] = _dot(hn, w_scr[0])
        for shard in (1, 2):
            res = _dot(hn, w_scr[shard])
            for t in range(4):
                lanes = slice(t * 128, (t + 1) * 128)
                out = slice((shard - 1) * COL_SHARD + t * 128, (shard - 1) * COL_SHARD + (t + 1) * 128)
                q_ref[:, out] = (_rope_lanes(res[:, lanes], c, a, b) * ATTN_SCALE).astype(MXU_DTYPE)
        res = _dot(hn, w_scr[3])
        for t in range(2):
            lanes = slice(t * 128, (t + 1) * 128)
            k_ref[:, lanes] = _rope_lanes(res[:, lanes], c, a, b).astype(MXU_DTYPE)
        v_ref[...] = res[:, D_KV:].astype(MXU_DTYPE)
        for shard in range(4, N_DEV):
            gate_ref[:, (shard - 4) * COL_SHARD:(shard - 3) * COL_SHARD] = _dot(hn, w_scr[shard])

    tab = pl.BlockSpec((tm, 128), lambda i: (i, 0))
    kv = pl.BlockSpec((tm, D_KV), lambda i: (i, 0))
    row_d = pl.BlockSpec((tm, D), lambda i: (i, 0))
    return _pcall(
        body, name=f"in_proj_l{layer}", grid=(rows // tm,),
        in_specs=[row_d, pl.BlockSpec((None, 1, D), lambda i: (layer, 0, 0)),
                  pl.BlockSpec(memory_space=pl.ANY), tab, tab, tab] + [pl.BlockSpec(memory_space=pl.ANY)] * len(after),
        out_specs=[pl.BlockSpec((tm, D_SSM), lambda i: (i, 0)), pl.BlockSpec((tm, 2 * D), lambda i: (i, 0)),
                   row_d, kv, kv, row_d],
        out_shape=[SDS((rows, D_SSM), F32), SDS((rows, 2 * D), F32), SDS((rows, D_ATTN), MXU_DTYPE),
                   SDS((rows, D_KV), MXU_DTYPE), SDS((rows, D_KV), MXU_DTYPE), SDS((rows, D), MXU_DTYPE)],
        scratch_shapes=[pltpu.VMEM((N_DEV, D, COL_SHARD), MXU_DTYPE), pltpu.SemaphoreType.DMA((N_DEV,))],
        compiler_params=_cparams("arbitrary"),
    )(hres, gain3, w_in_g, cos, sin_a, sin_b, *after)


SCAN_TILE = 8


def _scan_tiles(tre_ref, tim_ref, *scans):
    n_tiles = BLK // SCAN_TILE
    row = lax.broadcasted_iota(jnp.int32, (SCAN_TILE, SB_STATES), 0)
    leaving = [list(scan[2]) for scan in scans]
    for step in range(n_tiles):
        for n, (x_scr, out_scr, _, reverse, prev_scr) in enumerate(scans):
            base = 4 if reverse else 0
            j = n_tiles - 1 - step if reverse else step
            rows = slice(SCAN_TILE * j, SCAN_TILE * (j + 1))
            for sb in range(N_SB):
                t_r, t_i = leaving[n][sb]
                xr = x_scr[sb, rows, :SB_STATES]
                xi = x_scr[sb, rows, SB_STATES:]
                for k in range(3):
                    shift = SCAN_TILE - (1 << k) if reverse else (1 << k)
                    rr = pltpu.roll(xr, shift, 0)
                    ri = pltpu.roll(xi, shift, 0)
                    ar = tre_ref[sb, base + k]
                    ai = tim_ref[sb, base + k]
                    xr, xi = xr + (ar * rr - ai * ri), xi + (ar * ri + ai * rr)
                pr = tre_ref[sb, base + 3]
                pi = tim_ref[sb, base + 3]
                xr, xi = xr + (pr * t_r - pi * t_i), xi + (pr * t_i + pi * t_r)
                out_scr[sb, rows, :SB_STATES] = xr
                out_scr[sb, rows, SB_STATES:] = xi
                if prev_scr is not None:
                    prev_scr[sb, rows, :SB_STATES] = jnp.where(row == 0, t_r, pltpu.roll(xr, 1, 0))
                    prev_scr[sb, rows, SB_STATES:] = jnp.where(row == 0, t_i, pltpu.roll(xi, 1, 0))
                edge = slice(0, 1) if reverse else slice(SCAN_TILE - 1, SCAN_TILE)
                leaving[n][sb] = (xr[edge], xi[edge])
    return leaving


def _s5_fwd(u, ssm, w_glu, b_glu3, layer):
    rows = u.shape[0]
    n_chunks = rows // BLK
    b_mat, c_mat, t_re, t_im, d_skip = (ssm[k] for k in ("b_mat", "c_mat", "t_re", "t_im", "d_skip"))

    def body(u_ref, bm_ref, cm_ref, tre_ref, tim_ref, d_ref, wg_ref, bg_ref,
             y_ref, ys_ref, cin_ref, carry, bu_scr, s_scr):
        @pl.when(pl.program_id(0) == 0)
        def _():
            carry[...] = jnp.zeros_like(carry)

        cin_ref[...] = carry[...]
        u = u_ref[...]
        for sb in range(N_SB):
            bu_scr[sb] = _dot(u[:, sb * 128:(sb + 1) * 128].astype(MXU_DTYPE), bm_ref[sb])
        entering = [(carry[2 * sb:2 * sb + 1, :], carry[2 * sb + 1:2 * sb + 2, :]) for sb in range(N_SB)]
        leaving, = _scan_tiles(tre_ref, tim_ref, (bu_scr, s_scr, entering, False, None))
        for sb in range(N_SB):
            cols = slice(sb * 128, (sb + 1) * 128)
            carry[2 * sb:2 * sb + 1, :], carry[2 * sb + 1:2 * sb + 2, :] = leaving[sb]
            y_ref[:, cols] = _dot(s_scr[sb].astype(MXU_DTYPE), cm_ref[sb]) + d_ref[:, cols] * u[:, cols]
        z, _ = _gelu_parts(y_ref[...])
        gl = _dot(z.astype(MXU_DTYPE), wg_ref[...]) + bg_ref[...]
        ys_ref[...] = (z * _sigmoid(gl)).astype(MXU_DTYPE)

    full = lambda shape: pl.BlockSpec(shape, lambda j: (0,) * len(shape))
    of_layer = lambda shape: pl.BlockSpec((None,) + shape, lambda j: (layer,) + (0,) * len(shape))
    return _pcall(
        body, name=f"s5_fwd_l{layer}", grid=(n_chunks,),
        in_specs=[pl.BlockSpec((BLK, D_SSM), lambda j: (j, 0)),
                  of_layer((N_SB, 128, 2 * SB_STATES)), of_layer((N_SB, 2 * SB_STATES, 128)),
                  of_layer((N_SB, 8, SCAN_TILE, SB_STATES)), of_layer((N_SB, 8, SCAN_TILE, SB_STATES)),
                  of_layer((1, D_SSM)), full((D_SSM, D_SSM)),
                  pl.BlockSpec((None, 1, D_SSM), lambda j: (layer, 0, 0))],
        out_specs=[pl.BlockSpec((BLK, D_SSM), lambda j: (j, 0)), pl.BlockSpec((BLK, D_SSM), lambda j: (j, 0)),
                   pl.BlockSpec((None, 8, SB_STATES), lambda j: (j, 0, 0))],
        out_shape=[SDS((rows, D_SSM), F32), SDS((rows, D_SSM), MXU_DTYPE), SDS((n_chunks, 8, SB_STATES), F32)],
        scratch_shapes=[pltpu.VMEM((8, SB_STATES), F32), pltpu.VMEM((N_SB, BLK, 2 * SB_STATES), F32),
                        pltpu.VMEM((N_SB, BLK, 2 * SB_STATES), F32)],
        compiler_params=_cparams("arbitrary"),
    )(u, b_mat, c_mat, t_re, t_im, d_skip, w_glu, b_glu3)


def _attn_mask(i):
    row = lax.broadcasted_iota(jnp.int32, (BLK, 3 * BLK), 0) + i * BLK
    col = lax.broadcasted_iota(jnp.int32, (BLK, 3 * BLK), 1)
    seg = jnp.right_shift(col, 7)
    c = jnp.bitwise_and(col, BLK - 1)
    kidx = c + (i + seg - 2) * BLK
    ok_meta = (seg == 0) & (c >= PAD_ROWS) & (row - c >= BLK)
    ok_win = (seg > 0) & (kidx >= PAD_ROWS) & (kidx <= row) & (row - kidx < BLK)
    return jnp.where(ok_meta | ok_win, 0.0, NEG_INF)


def _head_lanes(h):
    return slice(h * HEAD_DIM, (h + 1) * HEAD_DIM)


def _group_rows(ref, kvh):
    return jnp.concatenate([ref[:, _head_lanes(kvh * Q_PER_KV + g)] for g in range(Q_PER_KV)], axis=0)


def _group_bias(bias, sink_ref, layer, kvh):
    first_col = lax.broadcasted_iota(jnp.int32, (BLK, BLK), 1) == 0
    slabs = []
    for g in range(Q_PER_KV):
        first = jnp.where(first_col, sink_ref[layer, kvh * Q_PER_KV + g], bias[:, :BLK])
        slabs.append(jnp.concatenate([first, bias[:, BLK:]], axis=1))
    return jnp.concatenate(slabs, axis=0)


def _attn_probs(q4, k3, bias4):
    s = _dot_nt(q4, k3) + bias4
    e = jnp.exp(s - jnp.max(s, axis=-1, keepdims=True))
    return e * (1.0 / jnp.sum(e, axis=-1, keepdims=True))


def _attn_fwd(q, k, v, sinks, layer):
    rows = q.shape[0]
    n_blk = rows // BLK

    def body(sink_ref, q_ref, km_ref, kp_ref, kc_ref, vm_ref, vp_ref, vc_ref, o_ref):
        bias = _attn_mask(pl.program_id(0))
        for kvh in range(N_KV_HEADS):
            lanes = _head_lanes(kvh)
            k3 = jnp.concatenate([km_ref[:, lanes], kp_ref[:, lanes], kc_ref[:, lanes]], axis=0)
            v3 = jnp.concatenate([vm_ref[:, lanes], vp_ref[:, lanes], vc_ref[:, lanes]], axis=0)
            p = _attn_probs(_group_rows(q_ref, kvh), k3, _group_bias(bias, sink_ref, layer, kvh))
            o4 = _dot(p.astype(MXU_DTYPE), v3).astype(MXU_DTYPE)
            for g in range(Q_PER_KV):
                o_ref[:, _head_lanes(kvh * Q_PER_KV + g)] = o4[g * BLK:(g + 1) * BLK]

    kv_meta = pl.BlockSpec((BLK, D_KV), lambda i: (0, 0))
    kv_prev = pl.BlockSpec((BLK, D_KV), lambda i: (jnp.maximum(i - 1, 0), 0))
    kv_cur = pl.BlockSpec((BLK, D_KV), lambda i: (i, 0))
    return _pcall(
        body, name=f"attn_fwd_l{layer}", grid=(n_blk,),
        in_specs=[pl.BlockSpec(memory_space=pltpu.SMEM),
                  pl.BlockSpec((BLK, D_ATTN), lambda i: (i, 0)),
                  kv_meta, kv_prev, kv_cur, kv_meta, kv_prev, kv_cur],
        out_specs=pl.BlockSpec((BLK, D_ATTN), lambda i: (i, 0)),
        out_shape=SDS((rows, D_ATTN), MXU_DTYPE),
        compiler_params=_cparams("parallel"),
    )(sinks, q, k, k, k, v, v, v)


def _merge_fwd(y_ssm, y_attn, gates, hres, w_o_ssm, w_o_attn, w_out, gain3, layer):
    rows = hres.shape[0]
    tm = _row_tile(rows, 320)

    def body(ys_ref, ya_ref, gs_ref, ga_ref, x_ref, wos_ref, woa_ref, wout_ref, g_ref,
             mg_ref, mix_ref, out_ref):
        a1 = _dot(ys_ref[...], wos_ref[...])
        a2 = _dot(ya_ref[...], woa_ref[...])
        merged = (_sigmoid(gs_ref[...]) * a1 + _sigmoid(ga_ref[...]) * a2).astype(MXU_DTYPE)
        mg_ref[...] = merged
        mix = _dot(merged, wout_ref[...])
        mix_ref[...] = mix
        out_ref[...] = x_ref[...] + _rms_fwd(mix, g_ref[...])

    row_d = pl.BlockSpec((tm, D), lambda i: (i, 0))
    full = lambda shape: pl.BlockSpec(shape, lambda i: (0,) * len(shape))
    return _pcall(
        body, name=f"merge_fwd_l{layer}", grid=(rows // tm,),
        in_specs=[pl.BlockSpec((tm, D_SSM), lambda i: (i, 0)), row_d,
                  row_d, pl.BlockSpec((tm, D), lambda i: (i, 1)), row_d,
                  full((D_SSM, D)), full((D_ATTN, D)), full((D, D)),
                  pl.BlockSpec((None, 1, D), lambda i: (layer, 0, 0))],
        out_specs=[row_d, row_d, row_d],
        out_shape=[SDS((rows, D), MXU_DTYPE), SDS((rows, D), F32), SDS((rows, D), F32)],
        compiler_params=_cparams("parallel"),
    )(y_ssm, y_attn, gates, gates, hres, w_o_ssm, w_o_attn, w_out, gain3)


def _mlp_fwd(hres, gain_pre3, gain_post3, w_up_g, w_down_g, layer, target=None):
    rows = hres.shape[0]
    tm = _row_tile(rows, 320)

    def body(x_ref, gp_ref, gq_ref, wu_hbm, wd_hbm, *refs):
        if target is None:
            up_ref, h_ref, ff_ref, out_ref, act_scr, wu_scr, wd_scr, wu_sem, wd_sem = refs
        else:
            t_ref, up_ref, h_ref, ff_ref, out_ref, loss_ref, act_scr, wu_scr, wd_scr, wu_sem, wd_sem = refs
        first = pl.program_id(0) == 0
        _load_resident(wu_hbm, wu_scr, wu_sem, first)
        _load_resident(wd_hbm, wd_scr, wd_sem, first)
        hn = _rms_fwd(x_ref[...], gp_ref[...]).astype(MXU_DTYPE)
        h_ref[...] = hn
        for kf in range(N_DEV):
            cols = slice(kf * COL_SHARD, (kf + 1) * COL_SHARD)
            up = _dot(hn, wu_scr[kf])
            up_ref[:, cols] = up.astype(MXU_DTYPE)
            r = jnp.maximum(up, 0.0)
            act_scr[:, cols] = (r * r).astype(MXU_DTYPE)
        ff = _dot(act_scr[...], wd_scr[...].reshape(D_FF, D))
        ff_ref[...] = ff
        out = x_ref[...] + _rms_fwd(ff, gq_ref[...])
        if target is None:
            out_ref[...] = out
        else:
            @pl.when(first)
            def _():
                loss_ref[...] = jnp.zeros_like(loss_ref)

            row = lax.broadcasted_iota(jnp.int32, (tm, D), 0) + pl.program_id(0) * tm
            err = jnp.where(row >= BLK, out - t_ref[...], 0.0)
            out_ref[...] = err * (1.0 / D)
            loss_ref[...] += jnp.sum(err * err) * (0.5 / D)

    row_d = pl.BlockSpec((tm, D), lambda i: (i, 0))
    gain = pl.BlockSpec((None, 1, D), lambda i: (layer, 0, 0))
    with_loss = target is not None
    return _pcall(
        body, name=f"mlp_fwd_l{layer}", grid=(rows // tm,),
        in_specs=[row_d, gain, gain, pl.BlockSpec(memory_space=pl.ANY), pl.BlockSpec(memory_space=pl.ANY)]
        + [row_d] * with_loss,
        out_specs=[pl.BlockSpec((tm, D_FF), lambda i: (i, 0)), row_d, row_d, row_d]
        + [pl.BlockSpec((1, 128), lambda i: (0, 0))] * with_loss,
        out_shape=[SDS((rows, D_FF), MXU_DTYPE), SDS((rows, D), MXU_DTYPE), SDS((rows, D), F32), SDS((rows, D), F32)]
        + [SDS((1, 128), F32)] * with_loss,
        scratch_shapes=[pltpu.VMEM((tm, D_FF), MXU_DTYPE),
                        pltpu.VMEM((N_DEV, D, COL_SHARD), MXU_DTYPE), pltpu.VMEM((N_DEV, COL_SHARD, D), MXU_DTYPE),
                        pltpu.SemaphoreType.DMA((N_DEV,)), pltpu.SemaphoreType.DMA((N_DEV,))],
        compiler_params=_cparams("arbitrary"),
    )(hres, gain_pre3, gain_post3, w_up_g, w_down_g, *([target] if with_loss else []))


def _relu_squared(up):
    r = jnp.maximum(up.astype(F32), 0.0)
    return (r * r).astype(MXU_DTYPE)


def _matmul_tn(a, b, name, dev_major_cols=None, a_fn=None):
    rows, ka = a.shape
    n = b.shape[1]
    ta = min(ka, 1024)
    tn = 1024 if n % 1024 == 0 else 512
    tr = _row_tile(rows, 1664)
    n_r = rows // tr

    def body(a_ref, b_ref, o_ref, acc):
        r = pl.program_id(2)

        @pl.when(r == 0)
        def _():
            acc[...] = jnp.zeros_like(acc)

        a_blk = a_ref[...] if a_fn is None else a_fn(a_ref[...])
        acc[...] += _dot_tn(a_blk, b_ref[...])

        @pl.when(r == n_r - 1)
        def _():
            if dev_major_cols is None:
                o_ref[...] = acc[...].astype(XFER_DTYPE)
            else:
                for s in range(tn // dev_major_cols):
                    o_ref[s] = acc[:, s * dev_major_cols:(s + 1) * dev_major_cols].astype(XFER_DTYPE)

    if dev_major_cols is None:
        out_spec = pl.BlockSpec((ta, tn), lambda i, j, r: (i, j))
        out_shape = SDS((ka, n), XFER_DTYPE)
    else:
        w = dev_major_cols
        out_spec = pl.BlockSpec((tn // w, ta, w), lambda i, j, r: (j, i, 0))
        out_shape = SDS((n // w, ka, w), XFER_DTYPE)
    return _pcall(
        body, name=name, grid=(ka // ta, n // tn, n_r),
        in_specs=[pl.BlockSpec((tr, ta), lambda i, j, r: (r, i)), pl.BlockSpec((tr, tn), lambda i, j, r: (r, j))],
        out_specs=out_spec, out_shape=out_shape,
        scratch_shapes=[pltpu.VMEM((ta, tn), F32)],
        compiler_params=_cparams("parallel", "parallel", "arbitrary"),
    )(a, b)


def _dw_in(h, dproj_pieces, layer):
    rows = h.shape[0]
    tr = _row_tile(rows, 1664)
    n_r = rows // tr

    def body(h_ref, *refs):
        piece_refs, (o_ref, acc) = refs[:len(DPROJ_PIECES)], refs[len(DPROJ_PIECES):]
        j = pl.program_id(0)
        r = pl.program_id(1)

        @pl.when(r == 0)
        def _():
            acc[...] = jnp.zeros_like(acc)

        for piece_ref, (first, count) in zip(piece_refs, DPROJ_PIECES):
            @pl.when((j >= first) & (j < first + count))
            def _():
                acc[...] += _dot_tn(h_ref[...], piece_ref[...])

        @pl.when(r == n_r - 1)
        def _():
            o_ref[...] = acc[...].astype(XFER_DTYPE)

    def piece_spec(first, count):
        def index(j, r):
            mine = (j >= first) & (j < first + count)
            return jnp.where(mine, r, 0), jnp.clip(j - first, 0, count - 1)
        return pl.BlockSpec((tr, COL_SHARD), index)

    return _pcall(
        body, name=f"dw_in_l{layer}", grid=(N_DEV, n_r),
        in_specs=[pl.BlockSpec((tr, D), lambda j, r: (r, 0))] + [piece_spec(*p) for p in DPROJ_PIECES],
        out_specs=pl.BlockSpec((None, D, COL_SHARD), lambda j, r: (j, 0, 0)),
        out_shape=SDS((N_DEV, D, COL_SHARD), XFER_DTYPE),
        scratch_shapes=[pltpu.VMEM((D, COL_SHARD), F32)],
        compiler_params=_cparams("arbitrary", "arbitrary"),
    )(h, *dproj_pieces)


def _mlp_bwd(dout, ff, up, hres_mid, gain_pre3, gain_post3, w_up_g, w_down_g, layer):
    rows = dout.shape[0]
    tm = _row_tile(rows, 320)

    def body(do_ref, ff_ref, up_ref, x_ref, gp_ref, gq_ref, wu_hbm, wd_hbm,
             dff_ref, dup_ref, dx_ref, dgq_ref, dgp_ref, wut_scr, wdt_scr, wu_stage, wd_stage, wu_sem, wd_sem):
        i = pl.program_id(0)
        _load_resident_transposed(wu_hbm, wut_scr, wu_stage, wu_sem, i == 0)
        _load_resident_transposed(wd_hbm, wdt_scr, wd_stage, wd_sem, i == 0)

        @pl.when(i == 0)
        def _():
            dgq_ref[...] = jnp.zeros_like(dgq_ref)
            dgp_ref[...] = jnp.zeros_like(dgp_ref)

        dff, dg = _rms_bwd(ff_ref[...], gq_ref[...], do_ref[...])
        dgq_ref[...] += dg
        dffb = dff.astype(MXU_DTYPE)
        dff_ref[...] = dffb
        for kf in range(N_DEV):
            cols = slice(kf * COL_SHARD, (kf + 1) * COL_SHARD)
            dact = _dot(dffb, wdt_scr[kf])
            dup_ref[:, cols] = (dact * (2.0 * jnp.maximum(up_ref[:, cols].astype(F32), 0.0))).astype(MXU_DTYPE)
        dh = _dot(dup_ref[...], wut_scr[...].reshape(D_FF, D))
        dx, dg = _rms_bwd(x_ref[...], gp_ref[...], dh)
        dgp_ref[...] += dg
        dx_ref[...] = do_ref[...] + dx

    row_d = pl.BlockSpec((tm, D), lambda i: (i, 0))
    row_ff = pl.BlockSpec((tm, D_FF), lambda i: (i, 0))
    gain = pl.BlockSpec((None, 1, D), lambda i: (layer, 0, 0))
    dgain = pl.BlockSpec((1, D), lambda i: (0, 0))
    return _pcall(
        body, name=f"mlp_bwd_l{layer}", grid=(rows // tm,),
        in_specs=[row_d, row_d, row_ff, row_d, gain, gain,
                  pl.BlockSpec(memory_space=pl.ANY), pl.BlockSpec(memory_space=pl.ANY)],
        out_specs=[row_d, row_ff, row_d, dgain, dgain],
        out_shape=[SDS((rows, D), MXU_DTYPE), SDS((rows, D_FF), MXU_DTYPE), SDS((rows, D), F32),
                   SDS((1, D), F32), SDS((1, D), F32)],
        scratch_shapes=[pltpu.VMEM((N_DEV, COL_SHARD, D), MXU_DTYPE), pltpu.VMEM((N_DEV, D, COL_SHARD), MXU_DTYPE),
                        pltpu.VMEM((2, D, COL_SHARD), MXU_DTYPE), pltpu.VMEM((2, COL_SHARD, D), MXU_DTYPE),
                        pltpu.SemaphoreType.DMA((2,)), pltpu.SemaphoreType.DMA((2,))],
        compiler_params=_cparams("arbitrary"),
    )(dout, ff, up, hres_mid, gain_pre3, gain_post3, w_up_g, w_down_g)


def _merge_bwd(dhm, mix, y_ssm, y_attn, gates, w_o_ssm, w_o_attn, w_o_ssm_t, w_o_attn_t, w_out_t, gain3, layer):
    rows = dhm.shape[0]
    tm = _row_tile(rows, 320)

    def body(dh_ref, mix_ref, ys_ref, ya_ref, gs_ref, ga_ref, wos_ref, woa_ref, wost_ref, woat_ref, woutt_ref, g_ref,
             dmix_ref, da1_ref, da2_ref, dgs_ref, dga_ref, dys_ref, dya_ref, dg_ref):
        @pl.when(pl.program_id(0) == 0)
        def _():
            dg_ref[...] = jnp.zeros_like(dg_ref)

        dmix, dg = _rms_bwd(mix_ref[...], g_ref[...], dh_ref[...])
        dg_ref[...] += dg
        dmixb = dmix.astype(MXU_DTYPE)
        dmix_ref[...] = dmixb
        dmerged = _dot(dmixb, woutt_ref[...])
        sg_s = _sigmoid(gs_ref[...])
        sg_a = _sigmoid(ga_ref[...])
        da1 = (dmerged * sg_s).astype(MXU_DTYPE)
        da2 = (dmerged * sg_a).astype(MXU_DTYPE)
        da1_ref[...] = da1
        da2_ref[...] = da2
        a1 = _dot(ys_ref[...], wos_ref[...])
        a2 = _dot(ya_ref[...], woa_ref[...])
        dgs_ref[...] = (dmerged * a1 * (sg_s * (1.0 - sg_s))).astype(MXU_DTYPE)
        dga_ref[...] = (dmerged * a2 * (sg_a * (1.0 - sg_a))).astype(MXU_DTYPE)
        dys_ref[...] = _dot(da1, wost_ref[...])
        dya_ref[...] = _dot(da2, woat_ref[...])

    row_d = pl.BlockSpec((tm, D), lambda i: (i, 0))
    full = lambda shape: pl.BlockSpec(shape, lambda i: (0,) * len(shape))
    return _pcall(
        body, name=f"merge_bwd_l{layer}", grid=(rows // tm,),
        in_specs=[row_d, row_d, pl.BlockSpec((tm, D_SSM), lambda i: (i, 0)), row_d,
                  row_d, pl.BlockSpec((tm, D), lambda i: (i, 1)),
                  full((D_SSM, D)), full((D_ATTN, D)), full((D, D_SSM)), full((D, D_ATTN)), full((D, D)),
                  pl.BlockSpec((None, 1, D), lambda i: (layer, 0, 0))],
        out_specs=[row_d, row_d, row_d, row_d, row_d, pl.BlockSpec((tm, D_SSM), lambda i: (i, 0)), row_d,
                   pl.BlockSpec((1, D), lambda i: (0, 0))],
        out_shape=[SDS((rows, D), MXU_DTYPE)] * 5 + [SDS((rows, D_SSM), F32), SDS((rows, D_ATTN), F32),
                                                      SDS((1, D), F32)],
        compiler_params=_cparams("arbitrary"),
    )(dhm, mix, y_ssm, y_attn, gates, gates, w_o_ssm, w_o_attn, w_o_ssm_t, w_o_attn_t, w_out_t, gain3)


def _attn_bwd(q, k, v, d_out, sinks, cos, sin_a, sin_b, layer):
    rows = q.shape[0]
    n_blk = rows // BLK
    last = n_blk - 1

    def body(sink_ref, q_ref, km_ref, kp_ref, kc_ref, vm_ref, vp_ref, vc_ref, do_ref, c_ref, a_ref, b_ref,
             dqb_ref, dk_ref, dv_ref, dkm_ref, dvm_ref, ds_ref, dk_carry, dv_carry, dq_ref):
        i = pl.program_id(0)

        @pl.when(i == 0)
        def _():
            dkm_ref[...] = jnp.zeros_like(dkm_ref)
            dvm_ref[...] = jnp.zeros_like(dvm_ref)
            ds_ref[...] = jnp.zeros_like(ds_ref)
            dk_carry[...] = jnp.zeros_like(dk_carry)
            dv_carry[...] = jnp.zeros_like(dv_carry)

        @pl.when(i <= last)
        def _():
            bias = _attn_mask(i)
            for kvh in range(N_KV_HEADS):
                lanes = _head_lanes(kvh)
                k3 = jnp.concatenate([km_ref[:, lanes], kp_ref[:, lanes], kc_ref[:, lanes]], axis=0)
                v3 = jnp.concatenate([vm_ref[:, lanes], vp_ref[:, lanes], vc_ref[:, lanes]], axis=0)
                q4 = _group_rows(q_ref, kvh)
                do4 = _group_rows(do_ref, kvh).astype(MXU_DTYPE)
                p = _attn_probs(q4, k3, _group_bias(bias, sink_ref, layer, kvh))
                dp = _dot_nt(do4, v3)
                dsf = p * (dp - jnp.sum(dp * p, axis=-1, keepdims=True))
                dsc = dsf.astype(MXU_DTYPE)
                dv3 = _dot_tn(p.astype(MXU_DTYPE), do4)
                dk3 = _dot_tn(dsc, q4)
                dq4 = _dot(dsc, k3)
                for g in range(Q_PER_KV):
                    h = kvh * Q_PER_KV + g
                    dq_ref[:, _head_lanes(h)] = dq4[g * BLK:(g + 1) * BLK]
                    ds_ref[h:h + 1, :] += jnp.sum(dsf[g * BLK:(g + 1) * BLK, 0:BLK], axis=0, keepdims=True)
                dkm_ref[:, lanes] += dk3[0:BLK]
                dvm_ref[:, lanes] += dv3[0:BLK]
                dk_ref[:, lanes] = dk_carry[:, lanes] + dk3[BLK:2 * BLK]
                dv_ref[:, lanes] = dv_carry[:, lanes] + dv3[BLK:2 * BLK]
                dk_carry[:, lanes] = dk3[2 * BLK:3 * BLK]
                dv_carry[:, lanes] = dv3[2 * BLK:3 * BLK]
            c, a, b = c_ref[...], -a_ref[...], -b_ref[...]
            for t in range(D_ATTN // 128):
                lanes = slice(t * 128, (t + 1) * 128)
                dqb_ref[:, lanes] = (_rope_lanes(dq_ref[:, lanes], c, a, b) * ATTN_SCALE).astype(MXU_DTYPE)

        @pl.when(i == last + 1)
        def _():
            dk_ref[...] = dk_carry[...]
            dv_ref[...] = dv_carry[...]

    cur = lambda i: (jnp.minimum(i, last), 0)
    prev = lambda i: (jnp.clip(i - 1, 0, last), 0)
    kv_meta = pl.BlockSpec((BLK, D_KV), lambda i: (0, 0))
    kv_prev = pl.BlockSpec((BLK, D_KV), prev)
    kv_cur = pl.BlockSpec((BLK, D_KV), cur)
    tab = pl.BlockSpec((BLK, 128), cur)
    return _pcall(
        body, name=f"attn_bwd_l{layer}", grid=(n_blk + 1,),
        in_specs=[pl.BlockSpec(memory_space=pltpu.SMEM),
                  pl.BlockSpec((BLK, D_ATTN), cur),
                  kv_meta, kv_prev, kv_cur, kv_meta, kv_prev, kv_cur,
                  pl.BlockSpec((BLK, D_ATTN), cur), tab, tab, tab],
        out_specs=[pl.BlockSpec((BLK, D_ATTN), cur), kv_prev, kv_prev, kv_meta, kv_meta,
                   pl.BlockSpec((N_Q_HEADS, 128), lambda i: (0, 0))],
        out_shape=[SDS((rows, D_ATTN), MXU_DTYPE), SDS((rows, D_KV), F32), SDS((rows, D_KV), F32),
                   SDS((BLK, D_KV), F32), SDS((BLK, D_KV), F32), SDS((N_Q_HEADS, 128), F32)],
        scratch_shapes=[pltpu.VMEM((BLK, D_KV), F32), pltpu.VMEM((BLK, D_KV), F32), pltpu.VMEM((BLK, D_ATTN), F32)],
        compiler_params=_cparams("arbitrary"),
    )(sinks, q, k, k, k, v, v, v, d_out, cos, sin_a, sin_b)


def _rope_bwd(dk, dv, dk_meta, dv_meta, cos, sin_a, sin_b, layer):
    rows = dk.shape[0]
    tm = _row_tile(rows)

    def body(dk_ref, dv_ref, dkm_ref, dvm_ref, c_ref, a_ref, b_ref, o_ref):
        c, a, b = c_ref[...], -a_ref[...], -b_ref[...]
        for t in range(2):
            x = dk_ref[:, t * 128:(t + 1) * 128]
            o_ref[:, t * 128:(t + 1) * 128] = _rope_lanes(x, c, a, b).astype(MXU_DTYPE)
        o_ref[:, D_KV:] = dv_ref[...].astype(MXU_DTYPE)

        @pl.when(pl.program_id(0) == 0)
        def _():
            cb, ab, bb = c[0:BLK], a[0:BLK], b[0:BLK]
            is_meta = lax.broadcasted_iota(jnp.int32, (BLK, 128), 0) >= PAD_ROWS
            for t in range(2):
                x = dk_ref[0:BLK, t * 128:(t + 1) * 128] + jnp.where(is_meta, dkm_ref[:, t * 128:(t + 1) * 128], 0.0)
                o_ref[0:BLK, t * 128:(t + 1) * 128] = _rope_lanes(x, cb, ab, bb).astype(MXU_DTYPE)
                xv = dv_ref[0:BLK, t * 128:(t + 1) * 128] + jnp.where(is_meta, dvm_ref[:, t * 128:(t + 1) * 128], 0.0)
                o_ref[0:BLK, D_KV + t * 128:D_KV + (t + 1) * 128] = xv.astype(MXU_DTYPE)

    tab = pl.BlockSpec((tm, 128), lambda i: (i, 0))
    kv = pl.BlockSpec((tm, D_KV), lambda i: (i, 0))
    meta = pl.BlockSpec((BLK, D_KV), lambda i: (0, 0))
    return _pcall(
        body, name=f"rope_bwd_l{layer}", grid=(rows // tm,),
        in_specs=[kv, kv, meta, meta, tab, tab, tab],
        out_specs=pl.BlockSpec((tm, 2 * D_KV), lambda i: (i, 0)),
        out_shape=SDS((rows, 2 * D_KV), MXU_DTYPE),
        compiler_params=_cparams("parallel"),
    )(dk, dv, dk_meta, dv_meta, cos, sin_a, sin_b)


def _s5_bwd(d_gated, y, u, carry_in, ssm, w_glu, b_glu3, layer):
    rows = y.shape[0]
    n_chunks = rows // BLK
    b_mat, c_mat, t_re, t_im, d_skip = (ssm[k] for k in ("b_mat", "c_mat", "t_re", "t_im", "d_skip"))

    def body(dz_ref, y_ref, u_ref, cin_ref, bm_ref, cm_ref, tre_ref, tim_ref, d_ref, wg_ref, bg_ref,
             du_ref, dwg_ref, dbg_ref, dd_ref, dbm_ref, dcm_ref, dab_ref,
             lam_carry, bu_scr, s_scr, sp_scr, g_scr, lam_scr):
        step = pl.program_id(0)
        chunk = n_chunks - 1 - step

        @pl.when(step == 0)
        def _():
            for r in (dwg_ref, dbg_ref, dd_ref, dbm_ref, dcm_ref, dab_ref, lam_carry):
                r[...] = jnp.zeros_like(r)

        y = y_ref[...]
        u = u_ref[...]
        d_o = dz_ref[...]
        z, t = _gelu_parts(y)
        zb = z.astype(MXU_DTYPE)
        sg = _sigmoid(_dot(zb, wg_ref[...]) + bg_ref[...])
        dgl = d_o * z * (sg * (1.0 - sg))
        dglb = dgl.astype(MXU_DTYPE)
        dz = d_o * sg + _dot_nt(dglb, wg_ref[...])
        dwg_ref[...] += _dot_tn(zb, dglb)
        dbg_ref[...] += jnp.sum(dgl, axis=0, keepdims=True)
        dy = dz * _gelu_grad(y, t)
        dd_ref[...] += jnp.sum(dy * u, axis=0, keepdims=True)
        grow = lax.broadcasted_iota(jnp.int32, (BLK, 128), 0) + chunk * BLK
        ub = u.astype(MXU_DTYPE)
        dyb_all = dy.astype(MXU_DTYPE)
        for sb in range(N_SB):
            cols = slice(sb * 128, (sb + 1) * 128)
            bu_scr[sb] = _dot(ub[:, cols], bm_ref[sb])
            g_scr[sb] = _dot_nt(dyb_all[:, cols], cm_ref[sb])
        entering_s = [(cin_ref[2 * sb:2 * sb + 1, :], cin_ref[2 * sb + 1:2 * sb + 2, :]) for sb in range(N_SB)]
        entering_lam = [(lam_carry[2 * sb:2 * sb + 1, :], lam_carry[2 * sb + 1:2 * sb + 2, :]) for sb in range(N_SB)]
        _, leaving = _scan_tiles(tre_ref, tim_ref, (bu_scr, s_scr, entering_s, False, sp_scr),
                                 (g_scr, lam_scr, entering_lam, True, None))
        for sb in range(N_SB):
            cols = slice(sb * 128, (sb + 1) * 128)
            u_sb = ub[:, cols]
            dy_sb = dy[:, cols]
            dyb = dyb_all[:, cols]
            lam_carry[2 * sb:2 * sb + 1, :], lam_carry[2 * sb + 1:2 * sb + 2, :] = leaving[sb]
            dcm_ref[sb] += _dot_tn(s_scr[sb].astype(MXU_DTYPE), dyb)
            lr, li = lam_scr[sb, :, :SB_STATES], lam_scr[sb, :, SB_STATES:]
            spr, spi = sp_scr[sb, :, :SB_STATES], sp_scr[sb, :, SB_STATES:]
            dab_ref[2 * sb:2 * sb + 1, :] += jnp.sum(spr * lr + spi * li, axis=0, keepdims=True)
            dab_ref[2 * sb + 1:2 * sb + 2, :] += jnp.sum(spr * li - spi * lr, axis=0, keepdims=True)
            lam = lam_scr[sb].astype(MXU_DTYPE)
            dbm_ref[sb] += _dot_tn(u_sb, lam)
            du = _dot_nt(lam, bm_ref[sb]) + d_ref[:, cols] * dy_sb
            du_ref[:, cols] = jnp.where(grow >= PAD_ROWS, du, 0.0).astype(MXU_DTYPE)

    rev = lambda j: (n_chunks - 1 - j, 0)
    full = lambda shape: pl.BlockSpec(shape, lambda j: (0,) * len(shape))
    of_layer = lambda shape: pl.BlockSpec((None,) + shape, lambda j: (layer,) + (0,) * len(shape))
    tables = [of_layer((N_SB, 8, SCAN_TILE, SB_STATES))] * 2
    chunk_scratch = pltpu.VMEM((N_SB, BLK, 2 * SB_STATES), F32)
    return _pcall(
        body, name=f"s5_bwd_l{layer}", grid=(n_chunks,),
        in_specs=[pl.BlockSpec((BLK, D_SSM), rev), pl.BlockSpec((BLK, D_SSM), rev), pl.BlockSpec((BLK, D_SSM), rev),
                  pl.BlockSpec((None, 8, SB_STATES), lambda j: (n_chunks - 1 - j, 0, 0)),
                  of_layer((N_SB, 128, 2 * SB_STATES)), of_layer((N_SB, 2 * SB_STATES, 128))] + tables + [
                  of_layer((1, D_SSM)), full((D_SSM, D_SSM)),
                  pl.BlockSpec((None, 1, D_SSM), lambda j: (layer, 0, 0))],
        out_specs=[pl.BlockSpec((BLK, D_SSM), rev), full((D_SSM, D_SSM)), full((1, D_SSM)), full((1, D_SSM)),
                   full((N_SB, 128, 2 * SB_STATES)), full((N_SB, 2 * SB_STATES, 128)), full((8, SB_STATES))],
        out_shape=[SDS((rows, D_SSM), MXU_DTYPE), SDS((D_SSM, D_SSM), F32), SDS((1, D_SSM), F32), SDS((1, D_SSM), F32),
                   SDS((N_SB, 128, 2 * SB_STATES), F32), SDS((N_SB, 2 * SB_STATES, 128), F32), SDS((8, SB_STATES), F32)],
        scratch_shapes=[pltpu.VMEM((8, SB_STATES), F32)] + [chunk_scratch] * 5,
        compiler_params=_cparams("arbitrary"),
    )(d_gated, y, u, carry_in, b_mat, c_mat, t_re, t_im, d_skip, w_glu, b_glu3)


DPROJ_PIECES = ((0, 1), (1, 2), (3, 1), (4, 2), (6, 2))


def _in_bwd(dproj_pieces, dhm, hres, gain3, w_in_g, layer):
    rows = hres.shape[0]
    tm = _row_tile(rows)

    def body(*refs):
        piece_refs = refs[:len(DPROJ_PIECES)]
        dh_ref, x_ref, g_ref, w_hbm, dx_ref, dg_ref, wt_scr, w_stage, w_sem = refs[len(DPROJ_PIECES):]
        i = pl.program_id(0)
        _load_resident_transposed(w_hbm, wt_scr, w_stage, w_sem, i == 0)

        @pl.when(i == 0)
        def _():
            dg_ref[...] = jnp.zeros_like(dg_ref)

        dh = None
        for piece_ref, (first, count) in zip(piece_refs, DPROJ_PIECES):
            wt = wt_scr[first:first + count].reshape(count * COL_SHARD, D)
            part = _dot(piece_ref[...], wt)
            dh = part if dh is None else dh + part
        dx, dg = _rms_bwd(x_ref[...], g_ref[...], dh)
        dg_ref[...] += dg
        dx_ref[...] = dh_ref[...] + dx

    row_d = pl.BlockSpec((tm, D), lambda i: (i, 0))
    return _pcall(
        body, name=f"in_bwd_l{layer}", grid=(rows // tm,),
        in_specs=[pl.BlockSpec((tm, count * COL_SHARD), lambda i: (i, 0)) for _, count in DPROJ_PIECES] + [
                  row_d, row_d,
                  pl.BlockSpec((None, 1, D), lambda i: (layer, 0, 0)),
                  pl.BlockSpec(memory_space=pl.ANY)],
        out_specs=[row_d, pl.BlockSpec((1, D), lambda i: (0, 0))],
        out_shape=[SDS((rows, D), F32), SDS((1, D), F32)],
        scratch_shapes=[pltpu.VMEM((N_DEV, COL_SHARD, D), MXU_DTYPE),
                        pltpu.VMEM((2, D, COL_SHARD), MXU_DTYPE), pltpu.SemaphoreType.DMA((2,))],
        compiler_params=_cparams("arbitrary"),
    )(*dproj_pieces, dhm, hres, gain3, w_in_g)


_ADAM_C1 = 1.0 / (1.0 - ADAM_B1 ** ADAM_STEP)
_ADAM_C2 = 1.0 / (1.0 - ADAM_B2 ** ADAM_STEP)


def _adam_math(w, g, m, v):
    m = ADAM_B1 * m + (1.0 - ADAM_B1) * g
    v = ADAM_B2 * v + (1.0 - ADAM_B2) * (g * g)
    delta = -ADAM_LR * ((m * _ADAM_C1) / (jnp.sqrt(v * _ADAM_C2) + ADAM_EPS) + ADAM_WD * w)
    return delta, m, v


def _adamw_layers(parts0, parts1, w, m, v, name):
    _, rows, cols = w.shape
    tr = min(rows, (1 << 17) // cols)
    nt = rows // tr

    def body(p0_ref, p1_ref, w_ref, m_ref, v_ref, g_ref, d_ref, nm_ref, nv_ref):
        layer = pl.program_id(0)

        def run(p_ref):
            g = p_ref[0].astype(F32)
            for s in range(1, N_DEV):
                g = g + p_ref[s].astype(F32)
            delta, nm, nv = _adam_math(w_ref[...], g, m_ref[...], v_ref[...])
            g_ref[...] = g
            d_ref[...] = delta
            nm_ref[...] = nm
            nv_ref[...] = nv

        @pl.when(layer == 0)
        def _():
            run(p0_ref)

        @pl.when(layer == 1)
        def _():
            run(p1_ref)

    wspec = pl.BlockSpec((None, tr, cols), lambda l, i: (l, i, 0))
    return _pcall(
        body, name=name, grid=(2, nt),
        in_specs=[pl.BlockSpec((N_DEV, tr, cols), lambda l, i: (0, jnp.where(l == 0, i, nt - 1), 0)),
                  pl.BlockSpec((N_DEV, tr, cols), lambda l, i: (0, jnp.where(l == 1, i, 0), 0)),
                  wspec, wspec, wspec],
        out_specs=[wspec] * 4, out_shape=[SDS(w.shape, F32)] * 4,
        compiler_params=_cparams("arbitrary", "arbitrary"),
    )(parts0, parts1, w, m, v)


def _sum_slots(parts, name):
    def body(p_ref, o_ref):
        acc = p_ref[0]
        for s in range(1, N_DEV):
            acc = acc + p_ref[s]
        o_ref[...] = acc

    vmem = pl.BlockSpec(memory_space=pltpu.VMEM)
    return _pcall(body, name=name, out_shape=SDS(parts.shape[1:], F32), in_specs=[vmem], out_specs=vmem,
                  compiler_params=_cparams())(parts)


def _adamw_packed(g, w, m, v, name):
    def body(g_ref, w_ref, m_ref, v_ref, d_ref, nm_ref, nv_ref):
        delta, nm, nv = _adam_math(w_ref[...], g_ref[...], m_ref[...], v_ref[...])
        d_ref[...] = delta
        nm_ref[...] = nm
        nv_ref[...] = nv

    vmem = pl.BlockSpec(memory_space=pltpu.VMEM)
    return _pcall(body, name=name, out_shape=[SDS(g.shape, F32)] * 3, in_specs=[vmem] * 4, out_specs=[vmem] * 3,
                  compiler_params=_cparams())(g, w, m, v)


def _ssm_discretize(a_re, a_im, log_dt, b_re, b_im):
    dt = jnp.exp(log_dt)[:, None]
    mag = jnp.exp(a_re * dt)
    ang = a_im * dt
    ab_re, ab_im = mag * jnp.cos(ang), mag * jnp.sin(ang)
    xr, xi = ab_re - 1.0, ab_im
    den = a_re * a_re + a_im * a_im
    q_re = (xr * a_re + xi * a_im) / den
    q_im = (xi * a_re - xr * a_im) / den
    bb_re = q_re[..., None] * b_re - q_im[..., None] * b_im
    bb_im = q_re[..., None] * b_im + q_im[..., None] * b_re
    return ab_re, ab_im, bb_re, bb_im


def _block_diag_b(bb):
    m = jnp.einsum("sgnc,gh->sgchn", bb.reshape(N_SB, 8, N_STATE, GROUP_CH), jnp.eye(8, dtype=F32))
    return m.reshape(N_SB, 128, SB_STATES)


def _block_diag_b_t(dm):
    return jnp.einsum("sgchn,gh->sgnc", dm.reshape(N_SB, 8, GROUP_CH, 8, N_STATE),
                      jnp.eye(8, dtype=F32)).reshape(N_GROUPS, N_STATE, GROUP_CH)


def _block_diag_c(cc):
    m = jnp.einsum("sgcn,gh->sgnhc", cc.reshape(N_SB, 8, GROUP_CH, N_STATE), jnp.eye(8, dtype=F32))
    return m.reshape(N_SB, SB_STATES, 128)


def _block_diag_c_t(dm):
    return jnp.einsum("sgnhc,gh->sgcn", dm.reshape(N_SB, 8, N_STATE, 8, GROUP_CH),
                      jnp.eye(8, dtype=F32)).reshape(N_GROUPS, GROUP_CH, N_STATE)


def _ssm_tables(ab_re, ab_im, bb_re, bb_im, c_re, c_im, d_skip):
    pr, pi = ab_re.reshape(1, -1), ab_im.reshape(1, -1)
    cr, ci = pr, pi
    squares = []
    for _ in range(3):
        squares.append((cr, ci))
        pr, pi = (jnp.concatenate([pr, pr * cr - pi * ci], axis=0),
                  jnp.concatenate([pi, pr * ci + pi * cr], axis=0))
        cr, ci = cr * cr - ci * ci, 2.0 * cr * ci
    r = jnp.arange(SCAN_TILE)[:, None]
    fwd = [(jnp.where(r >= (1 << k), squares[k][0], 0.0), jnp.where(r >= (1 << k), squares[k][1], 0.0))
           for k in range(3)] + [(pr, pi)]
    rev = [(jnp.where(r < SCAN_TILE - (1 << k), squares[k][0], 0.0),
            jnp.where(r < SCAN_TILE - (1 << k), -squares[k][1], 0.0)) for k in range(3)] + [(pr[::-1], -pi[::-1])]
    table = lambda part: jnp.stack([e[part] for e in fwd + rev]).reshape(
        8, SCAN_TILE, N_SB, SB_STATES).transpose(2, 0, 1, 3)
    return dict(
        b_mat=jnp.concatenate([_block_diag_b(bb_re), _block_diag_b(bb_im)], axis=-1).astype(MXU_DTYPE),
        c_mat=jnp.concatenate([_block_diag_c(c_re), -_block_diag_c(c_im)], axis=1).astype(MXU_DTYPE),
        t_re=table(0), t_im=table(1),
        d_skip=d_skip.reshape(1, D_SSM))


def _rope_tables(rows):
    pos = (jnp.arange(rows, dtype=jnp.int32) - PAD_ROWS).astype(F32)
    inv_freq = 1.0 / (ROPE_THETA ** (jnp.arange(0, HEAD_DIM, 2, dtype=F32) / HEAD_DIM))
    ang = pos[:, None] * inv_freq[None, :]
    ang = jnp.concatenate([ang, ang, ang, ang], axis=-1)
    first_half = (jnp.arange(128) % HEAD_DIM) < HEAD_DIM // 2
    sin = jnp.sin(ang)
    return jnp.cos(ang), jnp.where(first_half, -sin, 0.0), jnp.where(first_half, 0.0, sin)


def _pack(arrays):
    flat = jnp.concatenate([a.reshape(-1).astype(F32) for a in arrays])
    pad = (-flat.shape[0]) % 1024
    return jnp.pad(flat, (0, pad)).reshape(-1, 128)


def _unpack(packed, like):
    flat = packed.reshape(-1)
    out, off = [], 0
    for a in like:
        n = math.prod(a.shape)
        out.append(flat[off:off + n].reshape(a.shape))
        off += n
    return out


BIG = ("w_in", "w_glu", "w_o_ssm", "w_o_attn", "w_out", "w_up", "w_down")
WEIGHTS = ("meta_tokens", "norm_mix_pre", "norm_mix_post", "norm_mlp_pre", "norm_mlp_post", "w_in",
           "ssm_a_re", "ssm_a_im", "ssm_log_dt", "ssm_b_re", "ssm_b_im", "ssm_c_re", "ssm_c_im", "ssm_d",
           "w_glu", "b_glu", "attn_sinks", "w_o_ssm", "w_o_attn", "w_out", "w_up", "w_down")
SMALL = tuple(n for n in WEIGHTS if n not in BIG)


def kernel(x, meta_tokens, norm_mix_pre, norm_mix_post, norm_mlp_pre, norm_mlp_post, w_in, ssm_a_re, ssm_a_im, ssm_log_dt, ssm_b_re, ssm_b_im, ssm_c_re, ssm_c_im, ssm_d, w_glu, b_glu, attn_sinks, w_o_ssm, w_o_attn, w_out, w_up, w_down, loss_target, m_meta_tokens, m_norm_mix_pre, m_norm_mix_post, m_norm_mlp_pre, m_norm_mlp_post, m_w_in, m_ssm_a_re, m_ssm_a_im, m_ssm_log_dt, m_ssm_b_re, m_ssm_b_im, m_ssm_c_re, m_ssm_c_im, m_ssm_d, m_w_glu, m_b_glu, m_attn_sinks, m_w_o_ssm, m_w_o_attn, m_w_out, m_w_up, m_w_down, v_meta_tokens, v_norm_mix_pre, v_norm_mix_post, v_norm_mlp_pre, v_norm_mlp_post, v_w_in, v_ssm_a_re, v_ssm_a_im, v_ssm_log_dt, v_ssm_b_re, v_ssm_b_im, v_ssm_c_re, v_ssm_c_im, v_ssm_d, v_w_glu, v_b_glu, v_attn_sinks, v_w_o_ssm, v_w_o_attn, v_w_out, v_w_up, v_w_down):
    args = locals()
    w = {n: args[n] for n in WEIGHTS}
    m = {n: args["m_" + n] for n in WEIGHTS}
    v = {n: args["v_" + n] for n in WEIGHTS}
    n_layers = w_in.shape[0]
    seq = x.shape[1]
    rows = seq + BLK
    my_slot = _slot(_mesh_pos())

    assert n_layers == 2
    xfer = {n: [w[n][l].astype(XFER_DTYPE) for l in range(n_layers)] for n in BIG}
    mixer_small = ("w_glu", "w_o_ssm", "w_o_attn", "w_out")
    meta_g, w_in_g0 = _exchange_by_sequencer([meta_tokens, xfer["w_in"][0]], True, 0, "gather_in0")
    mix0_g = _exchange_by_sequencer([xfer[n][0] for n in mixer_small], True, 1, "gather_mix0")
    meta_full = meta_g.transpose(1, 0, 2).reshape(N_META, D)

    def mixer_weights(w_glu_g, w_o_ssm_g, w_o_attn_g, w_out_g):
        return dict(w_glu=w_glu_g.reshape(D_SSM, D_SSM), w_o_ssm=w_o_ssm_g.transpose(1, 0, 2).reshape(D_SSM, D),
                    w_o_attn=w_o_attn_g.reshape(D_ATTN, D), w_out=w_out_g.reshape(D, D),
                    w_o_ssm_t=w_o_ssm_g.transpose(0, 2, 1).reshape(D, D_SSM),
                    w_o_attn_t=w_o_attn_g.reshape(D_ATTN, D).T, w_out_t=w_out_g.reshape(D, D).T)

    gathered = [dict(w_in=w_in_g0, **mixer_weights(*mix0_g)), {}]

    gains = {n: w[n].reshape(n_layers, 1, D) for n in ("norm_mix_pre", "norm_mix_post", "norm_mlp_pre", "norm_mlp_post")}
    b_glu3 = b_glu.reshape(n_layers, 1, D_SSM)
    cos, sin_a, sin_b = _rope_tables(rows)

    disc, disc_vjp = jax.vjp(jax.vmap(_ssm_discretize), ssm_a_re, ssm_a_im, ssm_log_dt, ssm_b_re, ssm_b_im)
    ssm = jax.vmap(_ssm_tables)(*disc, ssm_c_re, ssm_c_im, ssm_d)

    hres = jnp.concatenate([jnp.zeros((PAD_ROWS, D), F32), meta_full, x[0]], axis=0)

    saved = []
    for l in range(n_layers):
        wl = gathered[l]
        u, gates, q, k, vv, h = _in_proj(hres, gains["norm_mix_pre"], wl["w_in"], cos, sin_a, sin_b, l,
                                         after=[ssm["b_mat"], ssm["c_mat"], ssm["t_re"], ssm["t_im"]] if l == 0 else ())
        if l == 0:
            wl["w_up"], wl["w_down"] = _exchange_by_sequencer([xfer["w_up"][0], xfer["w_down"][0]], True, 2,
                                                              "gather_mlp0")
        y, y_ssm, carry_in = _s5_fwd(u, ssm, wl["w_glu"], b_glu3, l)
        if l == 0:
            l1_g = _exchange_by_sequencer([xfer[n][1] for n in ("w_in",) + mixer_small + ("w_up", "w_down")], True, 3,
                                          "gather_l1", after=[y])
            gathered[1] = dict(w_in=l1_g[0], w_up=l1_g[5], w_down=l1_g[6], **mixer_weights(*l1_g[1:5]))
            last_exchange = l1_g[:1]
        y_attn = _attn_fwd(q, k, vv, attn_sinks, l)
        merged, mix, hres_mid = _merge_fwd(y_ssm, y_attn, gates, hres, wl["w_o_ssm"], wl["w_o_attn"], wl["w_out"],
                                           gains["norm_mix_post"], l)
        hres_in = hres
        if l + 1 < n_layers:
            up, h2, ff, hres = _mlp_fwd(hres_mid, gains["norm_mlp_pre"], gains["norm_mlp_post"], wl["w_up"],
                                        wl["w_down"], l)
        else:
            target = jnp.concatenate([jnp.zeros((BLK, D), F32), loss_target[0]], axis=0)
            up, h2, ff, dhres, loss_vec = _mlp_fwd(hres_mid, gains["norm_mlp_pre"], gains["norm_mlp_post"], wl["w_up"],
                                                   wl["w_down"], l, target=target)
        saved.append(dict(hres=hres_in, u=u, gates=gates, h=h, q=q, k=k, v=vv, y=y, y_ssm=y_ssm,
                          carry_in=carry_in, y_attn=y_attn, merged=merged, mix=mix, hres_mid=hres_mid,
                          up=up, h2=h2, ff=ff))

    small_grads = {}
    recv_up, recv_down, recv_mix = [None] * n_layers, [None] * n_layers, [None] * n_layers
    for l in reversed(range(n_layers)):
        s = saved[l]
        wl = gathered[l]
        dff, dup, dhm, dg_mlp_post, dg_mlp_pre = _mlp_bwd(dhres, s["ff"], s["up"], s["hres_mid"], gains["norm_mlp_pre"],
                                                          gains["norm_mlp_post"], wl["w_up"], wl["w_down"], l)
        dw_up = _matmul_tn(s["h2"], dup, f"dw_up_l{l}", dev_major_cols=COL_SHARD)
        recv_up[l] = _exchange_by_sequencer([dw_up], False, 4 + 3 * l, f"scatter_up{l}", after=last_exchange)
        dw_down = _matmul_tn(s["up"], dff, f"dw_down_l{l}", a_fn=_relu_squared).reshape(N_DEV, COL_SHARD, D)
        recv_down[l] = _exchange_by_sequencer([dw_down], False, 5 + 3 * l, f"scatter_down{l}", after=recv_up[l])
        last_exchange = recv_down[l]
        dmix, da1, da2, dgs, dga, dy_ssm, dy_attn, dg_mix_post = _merge_bwd(
            dhm, s["mix"], s["y_ssm"], s["y_attn"], s["gates"], wl["w_o_ssm"], wl["w_o_attn"], wl["w_o_ssm_t"],
            wl["w_o_attn_t"], wl["w_out_t"], gains["norm_mix_post"], l)
        dw_out = _matmul_tn(s["merged"], dmix, f"dw_out_l{l}").reshape(N_DEV, D // N_DEV, D)
        dw_o_attn = _matmul_tn(s["y_attn"], da2, f"dw_o_attn_l{l}").reshape(N_DEV, D_ATTN // N_DEV, D)
        dw_o_ssm = _matmul_tn(s["y_ssm"], da1, f"dw_o_ssm_l{l}", dev_major_cols=D // N_DEV)
        if l == 0:
            recv_out0 = _exchange_by_sequencer([dw_o_ssm, dw_o_attn, dw_out], False, 11, "scatter_out0",
                                               after=last_exchange)
            last_exchange = recv_out0[:1]
        dq, dk, dv, dk_meta, dv_meta, dsink = _attn_bwd(s["q"], s["k"], s["v"], dy_attn, attn_sinks, cos, sin_a, sin_b, l)
        dkv = _rope_bwd(dk, dv, dk_meta, dv_meta, cos, sin_a, sin_b, l)
        du, dw_glu, db_glu, dd_skip, db_mat, dc_mat, dab = _s5_bwd(dy_ssm, s["y"], s["u"], s["carry_in"], ssm,
                                                                    wl["w_glu"], b_glu3, l)
        dproj = (du, dq, dkv, dgs, dga)
        dw_in = _dw_in(s["h"], dproj, l)
        mix_parts = [dw_in, dw_glu.astype(XFER_DTYPE).reshape(N_DEV, D_SSM // N_DEV, D_SSM), dw_o_ssm, dw_o_attn, dw_out]
        if l > 0:
            recv_mix[l] = _exchange_by_sequencer(mix_parts, False, 6 + 3 * l, f"scatter_mix{l}", after=last_exchange)
            last_exchange = recv_mix[l][:1]
        else:
            recv_mix[0] = _exchange_by_sequencer(mix_parts[:2], False, 6, "scatter_in0", after=last_exchange) + recv_out0
            last_exchange = recv_mix[0][:1]
        dhres, dg_mix_pre = _in_bwd(dproj, dhm, s["hres"], gains["norm_mix_pre"], wl["w_in"], l)

        for name, val in (("norm_mix_pre", dg_mix_pre[0]), ("norm_mix_post", dg_mix_post[0]),
                          ("norm_mlp_pre", dg_mlp_pre[0]), ("norm_mlp_post", dg_mlp_post[0]),
                          ("dab", dab), ("db_mat", db_mat), ("dc_mat", dc_mat),
                          ("ssm_d", dd_skip.reshape(N_GROUPS, GROUP_CH)), ("b_glu", db_glu[0]),
                          ("attn_sinks", dsink[:, 0])):
            small_grads.setdefault(name, [None] * n_layers)[l] = val

    grad_x = dhres[BLK:][None]
    stacked = {n: jnp.stack(v) for n, v in small_grads.items()}
    dab = stacked["dab"].reshape(n_layers, N_SB, 2, SB_STATES)
    db_mat, dc_mat = stacked["db_mat"], stacked["dc_mat"]
    b_t, c_t = jax.vmap(_block_diag_b_t), jax.vmap(_block_diag_c_t)
    (stacked["ssm_a_re"], stacked["ssm_a_im"], stacked["ssm_log_dt"], stacked["ssm_b_re"],
     stacked["ssm_b_im"]) = disc_vjp((dab[:, :, 0].reshape(n_layers, N_GROUPS, N_STATE),
                                      dab[:, :, 1].reshape(n_layers, N_GROUPS, N_STATE),
                                      b_t(db_mat[..., :SB_STATES]), b_t(db_mat[..., SB_STATES:])))
    stacked["ssm_c_re"] = c_t(dc_mat[:, :, :SB_STATES])
    stacked["ssm_c_im"] = -c_t(dc_mat[:, :, SB_STATES:])
    small_names = [n for n in SMALL if n != "meta_tokens"]
    partial_small = [dhres[PAD_ROWS:BLK]] + [stacked[n] for n in small_names] + [loss_vec[0, :1]]
    small_parts, = _exchange_by_sequencer([_pack(partial_small)], True, 10, "gather_small", after=last_exchange)

    grads, delta, new_m, new_v = {}, {}, {}, {}

    def adamw_big(names, recv0, recv1):
        for n, p0, p1 in zip(names, recv0, recv1):
            grads[n], delta[n], new_m[n], new_v[n] = _adamw_layers(p0, p1, w[n], m[n], v[n], f"adamw_{n}")

    adamw_big(("w_up", "w_down"), recv_up[0] + recv_down[0], recv_up[1] + recv_down[1])
    summed = _unpack(_sum_slots(small_parts, "sum_small_grads"), partial_small)
    loss = summed[-1][0]
    grads.update(zip(small_names, summed[1:-1]))
    grads["meta_tokens"] = lax.dynamic_slice_in_dim(summed[0], my_slot * (D // N_DEV), D // N_DEV, axis=1)
    like = [w[n] for n in SMALL]
    d_s, m_s, v_s = _adamw_packed(_pack([grads[n] for n in SMALL]), _pack(like), _pack([m[n] for n in SMALL]),
                                  _pack([v[n] for n in SMALL]), "adamw_small")
    adamw_big(("w_in",) + mixer_small, recv_mix[0], recv_mix[1])
    for n, dd, mm, vs in zip(SMALL, _unpack(d_s, like), _unpack(m_s, like), _unpack(v_s, like)):
        delta[n], new_m[n], new_v[n] = dd, mm, vs

    return (loss, grad_x, *[grads[n] for n in WEIGHTS], *[delta[n] for n in WEIGHTS],
            *[new_m[n] for n in WEIGHTS], *[new_v[n] for n in WEIGHTS])
```

```python
import functools
import math

import jax
import jax.numpy as jnp
from jax import lax
from jax.experimental import pallas as pl
from jax.experimental.pallas import tpu as pltpu
from jax.experimental.pallas import tpu_sc as plsc

F32 = jnp.float32
MXU_DTYPE = jnp.bfloat16
XFER_DTYPE = MXU_DTYPE
_pcall = pl.pallas_call
SDS = jax.ShapeDtypeStruct

D = 1024
D_SSM = 512
D_ATTN = 1024
D_KV = 256
D_FF = 4096
D_IN = 4096
HEAD_DIM = 64
N_Q_HEADS = 16
N_KV_HEADS = 4
Q_PER_KV = 4
N_META = 16
BLK = 128
PAD_ROWS = BLK - N_META
N_GROUPS = 32
N_STATE = 64
GROUP_CH = 16
N_SB = 4
SB_STATES = 512
ROPE_THETA = 10000.0
ATTN_SCALE = HEAD_DIM ** -0.5
NEG_INF = -1e30
RMS_EPS = 1e-6
N_DEV = 8
COL_SHARD = 512

ADAM_LR = 0.001
ADAM_B1 = 0.9
ADAM_B2 = 0.999
ADAM_EPS = 1e-08
ADAM_WD = 0.01
ADAM_STEP = 10

VMEM_LIMIT = 56 * 1024 * 1024

_NT = (((1,), (1,)), ((), ()))
_TN = (((0,), (0,)), ((), ()))


def _cparams(*sem):
    return pltpu.CompilerParams(dimension_semantics=tuple(sem) if sem else None,
                                vmem_limit_bytes=VMEM_LIMIT)


def _row_tile(rows, cap=640):
    for t in (1664, 640, 512, 320, 256, 128):
        if t <= cap and rows % t == 0:
            return t
    raise ValueError(f"unsupported row count {rows}")


def _dot(a, b):
    return jnp.dot(a, b, preferred_element_type=F32)


def _dot_nt(a, b):
    return lax.dot_general(a, b, _NT, preferred_element_type=F32)


def _dot_tn(a, b):
    return lax.dot_general(a, b, _TN, preferred_element_type=F32)


def _sigmoid(x):
    return 1.0 / (1.0 + jnp.exp(-x))


_GELU_C = math.sqrt(2.0 / math.pi)


def _gelu_parts(y):
    t = jnp.tanh(_GELU_C * (y + 0.044715 * (y * y * y)))
    return 0.5 * y * (1.0 + t), t


def _gelu_grad(y, t):
    return 0.5 * (1.0 + t) + 0.5 * y * (1.0 - t * t) * (_GELU_C * (1.0 + 0.134145 * (y * y)))


def _rms_fwd(x, gain):
    r = lax.rsqrt(jnp.mean(x * x, axis=-1, keepdims=True) + RMS_EPS)
    return (x * r) * gain


def _rms_bwd(x, gain, dout):
    r = lax.rsqrt(jnp.mean(x * x, axis=-1, keepdims=True) + RMS_EPS)
    xh = x * r
    dxh = dout * gain
    dx = r * (dxh - xh * jnp.mean(dxh * xh, axis=-1, keepdims=True))
    return dx, jnp.sum(dout * xh, axis=0, keepdims=True)


def _mesh_pos():
    return lax.axis_index("x"), lax.axis_index("y"), lax.axis_index("c")


def _peer(pos, d):
    x, y, c = pos
    return (1 - x if d & 4 else x, 1 - y if d & 2 else y, 1 - c if d & 1 else c)


def _slot(pos):
    return 4 * pos[0] + 2 * pos[1] + pos[2]


def _exchange_copy(gather, src_ref, land_ref, sems, k, d, me, send_side):
    peer = _peer(me, d)
    sender = me if send_side else peer
    src = src_ref if gather else src_ref.at[_slot(peer) if send_side else _slot(me)]
    return pltpu.make_async_remote_copy(
        src_ref=src, dst_ref=land_ref.at[_slot(sender)],
        send_sem=sems[0].at[k * (N_DEV - 1) + d - 1], recv_sem=sems[1].at[k * (N_DEV - 1) + d - 1],
        device_id=peer, device_id_type=pl.DeviceIdType.MESH)


def _exchange_by_sequencer(srcs, gather, collective_id, name, after=()):
    n = len(srcs)
    flags = [gather] * n if isinstance(gather, bool) else list(gather)
    land_types = [SDS(((N_DEV,) + s.shape) if g else s.shape, s.dtype) for s, g in zip(srcs, flags)]

    def body(*refs):
        src_refs = refs[:n]
        land_refs = refs[n + len(after):2 * n + len(after)]
        sems = refs[2 * n + len(after):2 * n + len(after) + 2]
        local_sems = refs[2 * n + len(after) + 2]
        me = _mesh_pos()
        barrier = pltpu.get_barrier_semaphore()
        for d in range(1, N_DEV):
            pl.semaphore_signal(barrier, inc=1, device_id=_peer(me, d), device_id_type=pl.DeviceIdType.MESH)
        pl.semaphore_wait(barrier, N_DEV - 1)
        own = [pltpu.make_async_copy(src_refs[k] if flags[k] else src_refs[k].at[_slot(me)],
                                     land_refs[k].at[_slot(me)], local_sems.at[k]) for k in range(n)]
        for cp in own:
            cp.start()
        for k in range(n):
            for d in range(1, N_DEV):
                _exchange_copy(flags[k], src_refs[k], land_refs[k], sems, k, d, me, True).start()
        for cp in own:
            cp.wait()
        for k in range(n):
            for d in range(1, N_DEV):
                _exchange_copy(flags[k], src_refs[k], land_refs[k], sems, k, d, me, True).wait_send()
        for k in range(n):
            for d in range(1, N_DEV):
                _exchange_copy(flags[k], src_refs[k], land_refs[k], sems, k, d, me, False).wait_recv()

    sem_type = pltpu.SemaphoreType.DMA((n * (N_DEV - 1),))
    return pl.kernel(
        body, out_type=land_types, mesh=plsc.ScalarSubcoreMesh(axis_name="sequencer", num_cores=1), name=name,
        scratch_types=(sem_type, sem_type, pltpu.SemaphoreType.DMA((n,))),
        compiler_params=pltpu.CompilerParams(collective_id=collective_id),
    )(*srcs, *after)


def _load_resident(w_hbm, w_scr, sems, first_step):
    @pl.when(first_step)
    def _():
        copies = [pltpu.make_async_copy(w_hbm.at[s], w_scr.at[s], sems.at[s]) for s in range(N_DEV)]
        for cp in copies:
            cp.start()
        for cp in copies:
            cp.wait()


def _load_resident_transposed(w_hbm, w_scr, stage, sems, first_step):
    @pl.when(first_step)
    def _():
        copies = [pltpu.make_async_copy(w_hbm.at[s], stage.at[s % 2], sems.at[s % 2]) for s in range(N_DEV)]
        copies[0].start()
        for s in range(N_DEV):
            if s + 1 < N_DEV:
                copies[s + 1].start()
            copies[s].wait()
            w_scr[s] = stage[s % 2].T


def _rope_lanes(t, cos, sin_a, sin_b):
    return t * cos + pltpu.roll(t, 96, 1) * sin_a + pltpu.roll(t, 32, 1) * sin_b


def _in_proj(hres, gain3, w_in_g, cos, sin_a, sin_b, layer, after=()):
    rows = hres.shape[0]
    tm = _row_tile(rows, 320)

    def body(x_ref, g_ref, w_hbm, c_ref, a_ref, b_ref, *refs):
        u_ref, gate_ref, q_ref, k_ref, v_ref, h_ref, w_scr, w_sem = refs[len(after):]
        _load_resident(w_hbm, w_scr, w_sem, pl.program_id(0) == 0)
        hn = _rms_fwd(x_ref[...], g_ref[...]).astype(MXU_DTYPE)
        h_ref[...] = hn
        c, a, b = c_ref[...], a_ref[...], b_ref[...]
        u_ref[...] = _dot(hn, w_scr[0])
        for shard in (1, 2):
            res = _dot(hn, w_scr[shard])
            for t in range(4):
                lanes = slice(t * 128, (t + 1) * 128)
                out = slice((shard - 1) * COL_SHARD + t * 128, (shard - 1) * COL_SHARD + (t + 1) * 128)
                q_ref[:, out] = (_rope_lanes(res[:, lanes], c, a, b) * ATTN_SCALE).astype(MXU_DTYPE)
        res = _dot(hn, w_scr[3])
        for t in range(2):
            lanes = slice(t * 128, (t + 1) * 128)
            k_ref[:, lanes] = _rope_lanes(res[:, lanes], c, a, b).astype(MXU_DTYPE)
        v_ref[...] = res[:, D_KV:].astype(MXU_DTYPE)
        for shard in range(4, N_DEV):
            gate_ref[:, (shard - 4) * COL_SHARD:(shard - 3) * COL_SHARD] = _dot(hn, w_scr[shard])

    tab = pl.BlockSpec((tm, 128), lambda i: (i, 0))
    kv = pl.BlockSpec((tm, D_KV), lambda i: (i, 0))
    row_d = pl.BlockSpec((tm, D), lambda i: (i, 0))
    return _pcall(
        body, name=f"in_proj_l{layer}", grid=(rows // tm,),
        in_specs=[row_d, pl.BlockSpec((None, 1, D), lambda i: (layer, 0, 0)),
                  pl.BlockSpec(memory_space=pl.ANY), tab, tab, tab] + [pl.BlockSpec(memory_space=pl.ANY)] * len(after),
        out_specs=[pl.BlockSpec((tm, D_SSM), lambda i: (i, 0)), pl.BlockSpec((tm, 2 * D), lambda i: (i, 0)),
                   row_d, kv, kv, row_d],
        out_shape=[SDS((rows, D_SSM), F32), SDS((rows, 2 * D), F32), SDS((rows, D_ATTN), MXU_DTYPE),
                   SDS((rows, D_KV), MXU_DTYPE), SDS((rows, D_KV), MXU_DTYPE), SDS((rows, D), MXU_DTYPE)],
        scratch_shapes=[pltpu.VMEM((N_DEV, D, COL_SHARD), MXU_DTYPE), pltpu.SemaphoreType.DMA((N_DEV,))],
        compiler_params=_cparams("arbitrary"),
    )(hres, gain3, w_in_g, cos, sin_a, sin_b, *after)


SCAN_TILE = 8


def _scan_tiles(tre_ref, tim_ref, *scans):
    n_tiles = BLK // SCAN_TILE
    row = lax.broadcasted_iota(jnp.int32, (SCAN_TILE, SB_STATES), 0)
    leaving = [list(scan[2]) for scan in scans]
    for step in range(n_tiles):
        for n, (x_scr, out_scr, _, reverse, prev_scr) in enumerate(scans):
            base = 4 if reverse else 0
            j = n_tiles - 1 - step if reverse else step
            rows = slice(SCAN_TILE * j, SCAN_TILE * (j + 1))
            for sb in range(N_SB):
                t_r, t_i = leaving[n][sb]
                xr = x_scr[sb, rows, :SB_STATES]
                xi = x_scr[sb, rows, SB_STATES:]
                for k in range(3):
                    shift = SCAN_TILE - (1 << k) if reverse else (1 << k)
                    rr = pltpu.roll(xr, shift, 0)
                    ri = pltpu.roll(xi, shift, 0)
                    ar = tre_ref[sb, base + k]
                    ai = tim_ref[sb, base + k]
                    xr, xi = xr + (ar * rr - ai * ri), xi + (ar * ri + ai * rr)
                pr = tre_ref[sb, base + 3]
                pi = tim_ref[sb, base + 3]
                xr, xi = xr + (pr * t_r - pi * t_i), xi + (pr * t_i + pi * t_r)
                out_scr[sb, rows, :SB_STATES] = xr
                out_scr[sb, rows, SB_STATES:] = xi
                if prev_scr is not None:
                    prev_scr[sb, rows, :SB_STATES] = jnp.where(row == 0, t_r, pltpu.roll(xr, 1, 0))
                    prev_scr[sb, rows, SB_STATES:] = jnp.where(row == 0, t_i, pltpu.roll(xi, 1, 0))
                edge = slice(0, 1) if reverse else slice(SCAN_TILE - 1, SCAN_TILE)
                leaving[n][sb] = (xr[edge], xi[edge])
    return leaving


def _s5_fwd(u, ssm, w_glu, b_glu3, layer):
    rows = u.shape[0]
    n_chunks = rows // BLK
    b_mat, c_mat, t_re, t_im, d_skip = (ssm[k] for k in ("b_mat", "c_mat", "t_re", "t_im", "d_skip"))

    def body(u_ref, bm_ref, cm_ref, tre_ref, tim_ref, d_ref, wg_ref, bg_ref,
             y_ref, ys_ref, cin_ref, carry, bu_scr, s_scr):
        @pl.when(pl.program_id(0) == 0)
        def _():
            carry[...] = jnp.zeros_like(carry)

        cin_ref[...] = carry[...]
        u = u_ref[...]
        for sb in range(N_SB):
            bu_scr[sb] = _dot(u[:, sb * 128:(sb + 1) * 128].astype(MXU_DTYPE), bm_ref[sb])
        entering = [(carry[2 * sb:2 * sb + 1, :], carry[2 * sb + 1:2 * sb + 2, :]) for sb in range(N_SB)]
        leaving, = _scan_tiles(tre_ref, tim_ref, (bu_scr, s_scr, entering, False, None))
        for sb in range(N_SB):
            cols = slice(sb * 128, (sb + 1) * 128)
            carry[2 * sb:2 * sb + 1, :], carry[2 * sb + 1:2 * sb + 2, :] = leaving[sb]
            y_ref[:, cols] = _dot(s_scr[sb].astype(MXU_DTYPE), cm_ref[sb]) + d_ref[:, cols] * u[:, cols]
        z, _ = _gelu_parts(y_ref[...])
        gl = _dot(z.astype(MXU_DTYPE), wg_ref[...]) + bg_ref[...]
        ys_ref[...] = (z * _sigmoid(gl)).astype(MXU_DTYPE)

    full = lambda shape: pl.BlockSpec(shape, lambda j: (0,) * len(shape))
    of_layer = lambda shape: pl.BlockSpec((None,) + shape, lambda j: (layer,) + (0,) * len(shape))
    return _pcall(
        body, name=f"s5_fwd_l{layer}", grid=(n_chunks,),
        in_specs=[pl.BlockSpec((BLK, D_SSM), lambda j: (j, 0)),
                  of_layer((N_SB, 128, 2 * SB_STATES)), of_layer((N_SB, 2 * SB_STATES, 128)),
                  of_layer((N_SB, 8, SCAN_TILE, SB_STATES)), of_layer((N_SB, 8, SCAN_TILE, SB_STATES)),
                  of_layer((1, D_SSM)), full((D_SSM, D_SSM)),
                  pl.BlockSpec((None, 1, D_SSM), lambda j: (layer, 0, 0))],
        out_specs=[pl.BlockSpec((BLK, D_SSM), lambda j: (j, 0)), pl.BlockSpec((BLK, D_SSM), lambda j: (j, 0)),
                   pl.BlockSpec((None, 8, SB_STATES), lambda j: (j, 0, 0))],
        out_shape=[SDS((rows, D_SSM), F32), SDS((rows, D_SSM), MXU_DTYPE), SDS((n_chunks, 8, SB_STATES), F32)],
        scratch_shapes=[pltpu.VMEM((8, SB_STATES), F32), pltpu.VMEM((N_SB, BLK, 2 * SB_STATES), F32),
                        pltpu.VMEM((N_SB, BLK, 2 * SB_STATES), F32)],
        compiler_params=_cparams("arbitrary"),
    )(u, b_mat, c_mat, t_re, t_im, d_skip, w_glu, b_glu3)


def _attn_mask(i):
    row = lax.broadcasted_iota(jnp.int32, (BLK, 3 * BLK), 0) + i * BLK
    col = lax.broadcasted_iota(jnp.int32, (BLK, 3 * BLK), 1)
    seg = jnp.right_shift(col, 7)
    c = jnp.bitwise_and(col, BLK - 1)
    kidx = c + (i + seg - 2) * BLK
    ok_meta = (seg == 0) & (c >= PAD_ROWS) & (row - c >= BLK)
    ok_win = (seg > 0) & (kidx >= PAD_ROWS) & (kidx <= row) & (row - kidx < BLK)
    return jnp.where(ok_meta | ok_win, 0.0, NEG_INF)


def _head_lanes(h):
    return slice(h * HEAD_DIM, (h + 1) * HEAD_DIM)


def _group_rows(ref, kvh):
    return jnp.concatenate([ref[:, _head_lanes(kvh * Q_PER_KV + g)] for g in range(Q_PER_KV)], axis=0)


def _group_bias(bias, sink_ref, layer, kvh):
    first_col = lax.broadcasted_iota(jnp.int32, (BLK, BLK), 1) == 0
    slabs = []
    for g in range(Q_PER_KV):
        first = jnp.where(first_col, sink_ref[layer, kvh * Q_PER_KV + g], bias[:, :BLK])
        slabs.append(jnp.concatenate([first, bias[:, BLK:]], axis=1))
    return jnp.concatenate(slabs, axis=0)


def _attn_probs(q4, k3, bias4):
    s = _dot_nt(q4, k3) + bias4
    e = jnp.exp(s - jnp.max(s, axis=-1, keepdims=True))
    return e * (1.0 / jnp.sum(e, axis=-1, keepdims=True))


def _attn_fwd(q, k, v, sinks, layer):
    rows = q.shape[0]
    n_blk = rows // BLK

    def body(sink_ref, q_ref, km_ref, kp_ref, kc_ref, vm_ref, vp_ref, vc_ref, o_ref):
        bias = _attn_mask(pl.program_id(0))
        for kvh in range(N_KV_HEADS):
            lanes = _head_lanes(kvh)
            k3 = jnp.concatenate([km_ref[:, lanes], kp_ref[:, lanes], kc_ref[:, lanes]], axis=0)
            v3 = jnp.concatenate([vm_ref[:, lanes], vp_ref[:, lanes], vc_ref[:, lanes]], axis=0)
            p = _attn_probs(_group_rows(q_ref, kvh), k3, _group_bias(bias, sink_ref, layer, kvh))
            o4 = _dot(p.astype(MXU_DTYPE), v3).astype(MXU_DTYPE)
            for g in range(Q_PER_KV):
                o_ref[:, _head_lanes(kvh * Q_PER_KV + g)] = o4[g * BLK:(g + 1) * BLK]

    kv_meta = pl.BlockSpec((BLK, D_KV), lambda i: (0, 0))
    kv_prev = pl.BlockSpec((BLK, D_KV), lambda i: (jnp.maximum(i - 1, 0), 0))
    kv_cur = pl.BlockSpec((BLK, D_KV), lambda i: (i, 0))
    return _pcall(
        body, name=f"attn_fwd_l{layer}", grid=(n_blk,),
        in_specs=[pl.BlockSpec(memory_space=pltpu.SMEM),
                  pl.BlockSpec((BLK, D_ATTN), lambda i: (i, 0)),
                  kv_meta, kv_prev, kv_cur, kv_meta, kv_prev, kv_cur],
        out_specs=pl.BlockSpec((BLK, D_ATTN), lambda i: (i, 0)),
        out_shape=SDS((rows, D_ATTN), MXU_DTYPE),
        compiler_params=_cparams("parallel"),
    )(sinks, q, k, k, k, v, v, v)


def _merge_fwd(y_ssm, y_attn, gates, hres, w_o_ssm, w_o_attn, w_out, gain3, layer):
    rows = hres.shape[0]
    tm = _row_tile(rows, 320)

    def body(ys_ref, ya_ref, gs_ref, ga_ref, x_ref, wos_ref, woa_ref, wout_ref, g_ref,
             mg_ref, mix_ref, out_ref):
        a1 = _dot(ys_ref[...], wos_ref[...])
        a2 = _dot(ya_ref[...], woa_ref[...])
        merged = (_sigmoid(gs_ref[...]) * a1 + _sigmoid(ga_ref[...]) * a2).astype(MXU_DTYPE)
        mg_ref[...] = merged
        mix = _dot(merged, wout_ref[...])
        mix_ref[...] = mix
        out_ref[...] = x_ref[...] + _rms_fwd(mix, g_ref[...])

    row_d = pl.BlockSpec((tm, D), lambda i: (i, 0))
    full = lambda shape: pl.BlockSpec(shape, lambda i: (0,) * len(shape))
    return _pcall(
        body, name=f"merge_fwd_l{layer}", grid=(rows // tm,),
        in_specs=[pl.BlockSpec((tm, D_SSM), lambda i: (i, 0)), row_d,
                  row_d, pl.BlockSpec((tm, D), lambda i: (i, 1)), row_d,
                  full((D_SSM, D)), full((D_ATTN, D)), full((D, D)),
                  pl.BlockSpec((None, 1, D), lambda i: (layer, 0, 0))],
        out_specs=[row_d, row_d, row_d],
        out_shape=[SDS((rows, D), MXU_DTYPE), SDS((rows, D), F32), SDS((rows, D), F32)],
        compiler_params=_cparams("parallel"),
    )(y_ssm, y_attn, gates, gates, hres, w_o_ssm, w_o_attn, w_out, gain3)


def _mlp_fwd(hres, gain_pre3, gain_post3, w_up_g, w_down_g, layer, target=None):
    rows = hres.shape[0]
    tm = _row_tile(rows, 320)

    def body(x_ref, gp_ref, gq_ref, wu_hbm, wd_hbm, *refs):
        if target is None:
            up_ref, h_ref, ff_ref, out_ref, act_scr, wu_scr, wd_scr, wu_sem, wd_sem = refs
        else:
            t_ref, up_ref, h_ref, ff_ref, out_ref, loss_ref, act_scr, wu_scr, wd_scr, wu_sem, wd_sem = refs
        first = pl.program_id(0) == 0
        _load_resident(wu_hbm, wu_scr, wu_sem, first)
        _load_resident(wd_hbm, wd_scr, wd_sem, first)
        hn = _rms_fwd(x_ref[...], gp_ref[...]).astype(MXU_DTYPE)
        h_ref[...] = hn
        for kf in range(N_DEV):
            cols = slice(kf * COL_SHARD, (kf + 1) * COL_SHARD)
            up = _dot(hn, wu_scr[kf])
            up_ref[:, cols] = up.astype(MXU_DTYPE)
            r = jnp.maximum(up, 0.0)
            act_scr[:, cols] = (r * r).astype(MXU_DTYPE)
        ff = _dot(act_scr[...], wd_scr[...].reshape(D_FF, D))
        ff_ref[...] = ff
        out = x_ref[...] + _rms_fwd(ff, gq_ref[...])
        if target is None:
            out_ref[...] = out
        else:
            @pl.when(first)
            def _():
                loss_ref[...] = jnp.zeros_like(loss_ref)

            row = lax.broadcasted_iota(jnp.int32, (tm, D), 0) + pl.program_id(0) * tm
            err = jnp.where(row >= BLK, out - t_ref[...], 0.0)
            out_ref[...] = err * (1.0 / D)
            loss_ref[...] += jnp.sum(err * err) * (0.5 / D)

    row_d = pl.BlockSpec((tm, D), lambda i: (i, 0))
    gain = pl.BlockSpec((None, 1, D), lambda i: (layer, 0, 0))
    with_loss = target is not None
    return _pcall(
        body, name=f"mlp_fwd_l{layer}", grid=(rows // tm,),
        in_specs=[row_d, gain, gain, pl.BlockSpec(memory_space=pl.ANY), pl.BlockSpec(memory_space=pl.ANY)]
        + [row_d] * with_loss,
        out_specs=[pl.BlockSpec((tm, D_FF), lambda i: (i, 0)), row_d, row_d, row_d]
        + [pl.BlockSpec((1, 128), lambda i: (0, 0))] * with_loss,
        out_shape=[SDS((rows, D_FF), MXU_DTYPE), SDS((rows, D), MXU_DTYPE), SDS((rows, D), F32), SDS((rows, D), F32)]
        + [SDS((1, 128), F32)] * with_loss,
        scratch_shapes=[pltpu.VMEM((tm, D_FF), MXU_DTYPE),
                        pltpu.VMEM((N_DEV, D, COL_SHARD), MXU_DTYPE), pltpu.VMEM((N_DEV, COL_SHARD, D), MXU_DTYPE),
                        pltpu.SemaphoreType.DMA((N_DEV,)), pltpu.SemaphoreType.DMA((N_DEV,))],
        compiler_params=_cparams("arbitrary"),
    )(hres, gain_pre3, gain_post3, w_up_g, w_down_g, *([target] if with_loss else []))


def _relu_squared(up):
    r = jnp.maximum(up.astype(F32), 0.0)
    return (r * r).astype(MXU_DTYPE)


def _matmul_tn(a, b, name, dev_major_cols=None, a_fn=None):
    rows, ka = a.shape
    n = b.shape[1]
    ta = min(ka, 1024)
    tn = 1024 if n % 1024 == 0 else 512
    tr = _row_tile(rows, 1664)
    n_r = rows // tr

    def body(a_ref, b_ref, o_ref, acc):
        r = pl.program_id(2)

        @pl.when(r == 0)
        def _():
            acc[...] = jnp.zeros_like(acc)

        a_blk = a_ref[...] if a_fn is None else a_fn(a_ref[...])
        acc[...] += _dot_tn(a_blk, b_ref[...])

        @pl.when(r == n_r - 1)
        def _():
            if dev_major_cols is None:
                o_ref[...] = acc[...].astype(XFER_DTYPE)
            else:
                for s in range(tn // dev_major_cols):
                    o_ref[s] = acc[:, s * dev_major_cols:(s + 1) * dev_major_cols].astype(XFER_DTYPE)

    if dev_major_cols is None:
        out_spec = pl.BlockSpec((ta, tn), lambda i, j, r: (i, j))
        out_shape = SDS((ka, n), XFER_DTYPE)
    else:
        w = dev_major_cols
        out_spec = pl.BlockSpec((tn // w, ta, w), lambda i, j, r: (j, i, 0))
        out_shape = SDS((n // w, ka, w), XFER_DTYPE)
    return _pcall(
        body, name=name, grid=(ka // ta, n // tn, n_r),
        in_specs=[pl.BlockSpec((tr, ta), lambda i, j, r: (r, i)), pl.BlockSpec((tr, tn), lambda i, j, r: (r, j))],
        out_specs=out_spec, out_shape=out_shape,
        scratch_shapes=[pltpu.VMEM((ta, tn), F32)],
        compiler_params=_cparams("parallel", "parallel", "arbitrary"),
    )(a, b)


def _dw_in(h, dproj_pieces, layer):
    rows = h.shape[0]
    tr = _row_tile(rows, 1664)
    n_r = rows // tr

    def body(h_ref, *refs):
        piece_refs, (o_ref, acc) = refs[:len(DPROJ_PIECES)], refs[len(DPROJ_PIECES):]
        j = pl.program_id(0)
        r = pl.program_id(1)

        @pl.when(r == 0)
        def _():
            acc[...] = jnp.zeros_like(acc)

        for piece_ref, (first, count) in zip(piece_refs, DPROJ_PIECES):
            @pl.when((j >= first) & (j < first + count))
            def _():
                acc[...] += _dot_tn(h_ref[...], piece_ref[...])

        @pl.when(r == n_r - 1)
        def _():
            o_ref[...] = acc[...].astype(XFER_DTYPE)

    def piece_spec(first, count):
        def index(j, r):
            mine = (j >= first) & (j < first + count)
            return jnp.where(mine, r, 0), jnp.clip(j - first, 0, count - 1)
        return pl.BlockSpec((tr, COL_SHARD), index)

    return _pcall(
        body, name=f"dw_in_l{layer}", grid=(N_DEV, n_r),
        in_specs=[pl.BlockSpec((tr, D), lambda j, r: (r, 0))] + [piece_spec(*p) for p in DPROJ_PIECES],
        out_specs=pl.BlockSpec((None, D, COL_SHARD), lambda j, r: (j, 0, 0)),
        out_shape=SDS((N_DEV, D, COL_SHARD), XFER_DTYPE),
        scratch_shapes=[pltpu.VMEM((D, COL_SHARD), F32)],
        compiler_params=_cparams("arbitrary", "arbitrary"),
    )(h, *dproj_pieces)


def _mlp_bwd(dout, ff, up, hres_mid, gain_pre3, gain_post3, w_up_g, w_down_g, layer):
    rows = dout.shape[0]
    tm = _row_tile(rows, 320)

    def body(do_ref, ff_ref, up_ref, x_ref, gp_ref, gq_ref, wu_hbm, wd_hbm,
             dff_ref, dup_ref, dx_ref, dgq_ref, dgp_ref, wut_scr, wdt_scr, wu_stage, wd_stage, wu_sem, wd_sem):
        i = pl.program_id(0)
        _load_resident_transposed(wu_hbm, wut_scr, wu_stage, wu_sem, i == 0)
        _load_resident_transposed(wd_hbm, wdt_scr, wd_stage, wd_sem, i == 0)

        @pl.when(i == 0)
        def _():
            dgq_ref[...] = jnp.zeros_like(dgq_ref)
            dgp_ref[...] = jnp.zeros_like(dgp_ref)

        dff, dg = _rms_bwd(ff_ref[...], gq_ref[...], do_ref[...])
        dgq_ref[...] += dg
        dffb = dff.astype(MXU_DTYPE)
        dff_ref[...] = dffb
        for kf in range(N_DEV):
            cols = slice(kf * COL_SHARD, (kf + 1) * COL_SHARD)
            dact = _dot(dffb, wdt_scr[kf])
            dup_ref[:, cols] = (dact * (2.0 * jnp.maximum(up_ref[:, cols].astype(F32), 0.0))).astype(MXU_DTYPE)
        dh = _dot(dup_ref[...], wut_scr[...].reshape(D_FF, D))
        dx, dg = _rms_bwd(x_ref[...], gp_ref[...], dh)
        dgp_ref[...] += dg
        dx_ref[...] = do_ref[...] + dx

    row_d = pl.BlockSpec((tm, D), lambda i: (i, 0))
    row_ff = pl.BlockSpec((tm, D_FF), lambda i: (i, 0))
    gain = pl.BlockSpec((None, 1, D), lambda i: (layer, 0, 0))
    dgain = pl.BlockSpec((1, D), lambda i: (0, 0))
    return _pcall(
        body, name=f"mlp_bwd_l{layer}", grid=(rows // tm,),
        in_specs=[row_d, row_d, row_ff, row_d, gain, gain,
                  pl.BlockSpec(memory_space=pl.ANY), pl.BlockSpec(memory_space=pl.ANY)],
        out_specs=[row_d, row_ff, row_d, dgain, dgain],
        out_shape=[SDS((rows, D), MXU_DTYPE), SDS((rows, D_FF), MXU_DTYPE), SDS((rows, D), F32),
                   SDS((1, D), F32), SDS((1, D), F32)],
        scratch_shapes=[pltpu.VMEM((N_DEV, COL_SHARD, D), MXU_DTYPE), pltpu.VMEM((N_DEV, D, COL_SHARD), MXU_DTYPE),
                        pltpu.VMEM((2, D, COL_SHARD), MXU_DTYPE), pltpu.VMEM((2, COL_SHARD, D), MXU_DTYPE),
                        pltpu.SemaphoreType.DMA((2,)), pltpu.SemaphoreType.DMA((2,))],
        compiler_params=_cparams("arbitrary"),
    )(dout, ff, up, hres_mid, gain_pre3, gain_post3, w_up_g, w_down_g)


def _merge_bwd(dhm, mix, y_ssm, y_attn, gates, w_o_ssm, w_o_attn, w_o_ssm_t, w_o_attn_t, w_out_t, gain3, layer):
    rows = dhm.shape[0]
    tm = _row_tile(rows, 320)

    def body(dh_ref, mix_ref, ys_ref, ya_ref, gs_ref, ga_ref, wos_ref, woa_ref, wost_ref, woat_ref, woutt_ref, g_ref,
             dmix_ref, da1_ref, da2_ref, dgs_ref, dga_ref, dys_ref, dya_ref, dg_ref):
        @pl.when(pl.program_id(0) == 0)
        def _():
            dg_ref[...] = jnp.zeros_like(dg_ref)

        dmix, dg = _rms_bwd(mix_ref[...], g_ref[...], dh_ref[...])
        dg_ref[...] += dg
        dmixb = dmix.astype(MXU_DTYPE)
        dmix_ref[...] = dmixb
        dmerged = _dot(dmixb, woutt_ref[...])
        sg_s = _sigmoid(gs_ref[...])
        sg_a = _sigmoid(ga_ref[...])
        da1 = (dmerged * sg_s).astype(MXU_DTYPE)
        da2 = (dmerged * sg_a).astype(MXU_DTYPE)
        da1_ref[...] = da1
        da2_ref[...] = da2
        a1 = _dot(ys_ref[...], wos_ref[...])
        a2 = _dot(ya_ref[...], woa_ref[...])
        dgs_ref[...] = (dmerged * a1 * (sg_s * (1.0 - sg_s))).astype(MXU_DTYPE)
        dga_ref[...] = (dmerged * a2 * (sg_a * (1.0 - sg_a))).astype(MXU_DTYPE)
        dys_ref[...] = _dot(da1, wost_ref[...])
        dya_ref[...] = _dot(da2, woat_ref[...])

    row_d = pl.BlockSpec((tm, D), lambda i: (i, 0))
    full = lambda shape: pl.BlockSpec(shape, lambda i: (0,) * len(shape))
    return _pcall(
        body, name=f"merge_bwd_l{layer}", grid=(rows // tm,),
        in_specs=[row_d, row_d, pl.BlockSpec((tm, D_SSM), lambda i: (i, 0)), row_d,
                  row_d, pl.BlockSpec((tm, D), lambda i: (i, 1)),
                  full((D_SSM, D)), full((D_ATTN, D)), full((D, D_SSM)), full((D, D_ATTN)), full((D, D)),
                  pl.BlockSpec((None, 1, D), lambda i: (layer, 0, 0))],
        out_specs=[row_d, row_d, row_d, row_d, row_d, pl.BlockSpec((tm, D_SSM), lambda i: (i, 0)), row_d,
                   pl.BlockSpec((1, D), lambda i: (0, 0))],
        out_shape=[SDS((rows, D), MXU_DTYPE)] * 5 + [SDS((rows, D_SSM), F32), SDS((rows, D_ATTN), F32),
                                                      SDS((1, D), F32)],
        compiler_params=_cparams("arbitrary"),
    )(dhm, mix, y_ssm, y_attn, gates, gates, w_o_ssm, w_o_attn, w_o_ssm_t, w_o_attn_t, w_out_t, gain3)


def _attn_bwd(q, k, v, d_out, sinks, cos, sin_a, sin_b, layer):
    rows = q.shape[0]
    n_blk = rows // BLK
    last = n_blk - 1

    def body(sink_ref, q_ref, km_ref, kp_ref, kc_ref, vm_ref, vp_ref, vc_ref, do_ref, c_ref, a_ref, b_ref,
             dqb_ref, dk_ref, dv_ref, dkm_ref, dvm_ref, ds_ref, dk_carry, dv_carry, dq_ref):
        i = pl.program_id(0)

        @pl.when(i == 0)
        def _():
            dkm_ref[...] = jnp.zeros_like(dkm_ref)
            dvm_ref[...] = jnp.zeros_like(dvm_ref)
            ds_ref[...] = jnp.zeros_like(ds_ref)
            dk_carry[...] = jnp.zeros_like(dk_carry)
            dv_carry[...] = jnp.zeros_like(dv_carry)

        @pl.when(i <= last)
        def _():
            bias = _attn_mask(i)
            for kvh in range(N_KV_HEADS):
                lanes = _head_lanes(kvh)
                k3 = jnp.concatenate([km_ref[:, lanes], kp_ref[:, lanes], kc_ref[:, lanes]], axis=0)
                v3 = jnp.concatenate([vm_ref[:, lanes], vp_ref[:, lanes], vc_ref[:, lanes]], axis=0)
                q4 = _group_rows(q_ref, kvh)
                do4 = _group_rows(do_ref, kvh).astype(MXU_DTYPE)
                p = _attn_probs(q4, k3, _group_bias(bias, sink_ref, layer, kvh))
                dp = _dot_nt(do4, v3)
                dsf = p * (dp - jnp.sum(dp * p, axis=-1, keepdims=True))
                dsc = dsf.astype(MXU_DTYPE)
                dv3 = _dot_tn(p.astype(MXU_DTYPE), do4)
                dk3 = _dot_tn(dsc, q4)
                dq4 = _dot(dsc, k3)
                for g in range(Q_PER_KV):
                    h = kvh * Q_PER_KV + g
                    dq_ref[:, _head_lanes(h)] = dq4[g * BLK:(g + 1) * BLK]
                    ds_ref[h:h + 1, :] += jnp.sum(dsf[g * BLK:(g + 1) * BLK, 0:BLK], axis=0, keepdims=True)
                dkm_ref[:, lanes] += dk3[0:BLK]
                dvm_ref[:, lanes] += dv3[0:BLK]
                dk_ref[:, lanes] = dk_carry[:, lanes] + dk3[BLK:2 * BLK]
                dv_ref[:, lanes] = dv_carry[:, lanes] + dv3[BLK:2 * BLK]
                dk_carry[:, lanes] = dk3[2 * BLK:3 * BLK]
                dv_carry[:, lanes] = dv3[2 * BLK:3 * BLK]
            c, a, b = c_ref[...], -a_ref[...], -b_ref[...]
            for t in range(D_ATTN // 128):
                lanes = slice(t * 128, (t + 1) * 128)
                dqb_ref[:, lanes] = (_rope_lanes(dq_ref[:, lanes], c, a, b) * ATTN_SCALE).astype(MXU_DTYPE)

        @pl.when(i == last + 1)
        def _():
            dk_ref[...] = dk_carry[...]
            dv_ref[...] = dv_carry[...]

    cur = lambda i: (jnp.minimum(i, last), 0)
    prev = lambda i: (jnp.clip(i - 1, 0, last), 0)
    kv_meta = pl.BlockSpec((BLK, D_KV), lambda i: (0, 0))
    kv_prev = pl.BlockSpec((BLK, D_KV), prev)
    kv_cur = pl.BlockSpec((BLK, D_KV), cur)
    tab = pl.BlockSpec((BLK, 128), cur)
    return _pcall(
        body, name=f"attn_bwd_l{layer}", grid=(n_blk + 1,),
        in_specs=[pl.BlockSpec(memory_space=pltpu.SMEM),
                  pl.BlockSpec((BLK, D_ATTN), cur),
                  kv_meta, kv_prev, kv_cur, kv_meta, kv_prev, kv_cur,
                  pl.BlockSpec((BLK, D_ATTN), cur), tab, tab, tab],
        out_specs=[pl.BlockSpec((BLK, D_ATTN), cur), kv_prev, kv_prev, kv_meta, kv_meta,
                   pl.BlockSpec((N_Q_HEADS, 128), lambda i: (0, 0))],
        out_shape=[SDS((rows, D_ATTN), MXU_DTYPE), SDS((rows, D_KV), F32), SDS((rows, D_KV), F32),
                   SDS((BLK, D_KV), F32), SDS((BLK, D_KV), F32), SDS((N_Q_HEADS, 128), F32)],
        scratch_shapes=[pltpu.VMEM((BLK, D_KV), F32), pltpu.VMEM((BLK, D_KV), F32), pltpu.VMEM((BLK, D_ATTN), F32)],
        compiler_params=_cparams("arbitrary"),
    )(sinks, q, k, k, k, v, v, v, d_out, cos, sin_a, sin_b)


def _rope_bwd(dk, dv, dk_meta, dv_meta, cos, sin_a, sin_b, layer):
    rows = dk.shape[0]
    tm = _row_tile(rows)

    def body(dk_ref, dv_ref, dkm_ref, dvm_ref, c_ref, a_ref, b_ref, o_ref):
        c, a, b = c_ref[...], -a_ref[...], -b_ref[...]
        for t in range(2):
            x = dk_ref[:, t * 128:(t + 1) * 128]
            o_ref[:, t * 128:(t + 1) * 128] = _rope_lanes(x, c, a, b).astype(MXU_DTYPE)
        o_ref[:, D_KV:] = dv_ref[...].astype(MXU_DTYPE)

        @pl.when(pl.program_id(0) == 0)
        def _():
            cb, ab, bb = c[0:BLK], a[0:BLK], b[0:BLK]
            is_meta = lax.broadcasted_iota(jnp.int32, (BLK, 128), 0) >= PAD_ROWS
            for t in range(2):
                x = dk_ref[0:BLK, t * 128:(t + 1) * 128] + jnp.where(is_meta, dkm_ref[:, t * 128:(t + 1) * 128], 0.0)
                o_ref[0:BLK, t * 128:(t + 1) * 128] = _rope_lanes(x, cb, ab, bb).astype(MXU_DTYPE)
                xv = dv_ref[0:BLK, t * 128:(t + 1) * 128] + jnp.where(is_meta, dvm_ref[:, t * 128:(t + 1) * 128], 0.0)
                o_ref[0:BLK, D_KV + t * 128:D_KV + (t + 1) * 128] = xv.astype(MXU_DTYPE)

    tab = pl.BlockSpec((tm, 128), lambda i: (i, 0))
    kv = pl.BlockSpec((tm, D_KV), lambda i: (i, 0))
    meta = pl.BlockSpec((BLK, D_KV), lambda i: (0, 0))
    return _pcall(
        body, name=f"rope_bwd_l{layer}", grid=(rows // tm,),
        in_specs=[kv, kv, meta, meta, tab, tab, tab],
        out_specs=pl.BlockSpec((tm, 2 * D_KV), lambda i: (i, 0)),
        out_shape=SDS((rows, 2 * D_KV), MXU_DTYPE),
        compiler_params=_cparams("parallel"),
    )(dk, dv, dk_meta, dv_meta, cos, sin_a, sin_b)


def _s5_bwd(d_gated, y, u, carry_in, ssm, w_glu, b_glu3, layer):
    rows = y.shape[0]
    n_chunks = rows // BLK
    b_mat, c_mat, t_re, t_im, d_skip = (ssm[k] for k in ("b_mat", "c_mat", "t_re", "t_im", "d_skip"))

    def body(dz_ref, y_ref, u_ref, cin_ref, bm_ref, cm_ref, tre_ref, tim_ref, d_ref, wg_ref, bg_ref,
             du_ref, dwg_ref, dbg_ref, dd_ref, dbm_ref, dcm_ref, dab_ref,
             lam_carry, bu_scr, s_scr, sp_scr, g_scr, lam_scr):
        step = pl.program_id(0)
        chunk = n_chunks - 1 - step

        @pl.when(step == 0)
        def _():
            for r in (dwg_ref, dbg_ref, dd_ref, dbm_ref, dcm_ref, dab_ref, lam_carry):
                r[...] = jnp.zeros_like(r)

        y = y_ref[...]
        u = u_ref[...]
        d_o = dz_ref[...]
        z, t = _gelu_parts(y)
        zb = z.astype(MXU_DTYPE)
        sg = _sigmoid(_dot(zb, wg_ref[...]) + bg_ref[...])
        dgl = d_o * z * (sg * (1.0 - sg))
        dglb = dgl.astype(MXU_DTYPE)
        dz = d_o * sg + _dot_nt(dglb, wg_ref[...])
        dwg_ref[...] += _dot_tn(zb, dglb)
        dbg_ref[...] += jnp.sum(dgl, axis=0, keepdims=True)
        dy = dz * _gelu_grad(y, t)
        dd_ref[...] += jnp.sum(dy * u, axis=0, keepdims=True)
        grow = lax.broadcasted_iota(jnp.int32, (BLK, 128), 0) + chunk * BLK
        ub = u.astype(MXU_DTYPE)
        dyb_all = dy.astype(MXU_DTYPE)
        for sb in range(N_SB):
            cols = slice(sb * 128, (sb + 1) * 128)
            bu_scr[sb] = _dot(ub[:, cols], bm_ref[sb])
            g_scr[sb] = _dot_nt(dyb_all[:, cols], cm_ref[sb])
        entering_s = [(cin_ref[2 * sb:2 * sb + 1, :], cin_ref[2 * sb + 1:2 * sb + 2, :]) for sb in range(N_SB)]
        entering_lam = [(lam_carry[2 * sb:2 * sb + 1, :], lam_carry[2 * sb + 1:2 * sb + 2, :]) for sb in range(N_SB)]
        _, leaving = _scan_tiles(tre_ref, tim_ref, (bu_scr, s_scr, entering_s, False, sp_scr),
                                 (g_scr, lam_scr, entering_lam, True, None))
        for sb in range(N_SB):
            cols = slice(sb * 128, (sb + 1) * 128)
            u_sb = ub[:, cols]
            dy_sb = dy[:, cols]
            dyb = dyb_all[:, cols]
            lam_carry[2 * sb:2 * sb + 1, :], lam_carry[2 * sb + 1:2 * sb + 2, :] = leaving[sb]
            dcm_ref[sb] += _dot_tn(s_scr[sb].astype(MXU_DTYPE), dyb)
            lr, li = lam_scr[sb, :, :SB_STATES], lam_scr[sb, :, SB_STATES:]
            spr, spi = sp_scr[sb, :, :SB_STATES], sp_scr[sb, :, SB_STATES:]
            dab_ref[2 * sb:2 * sb + 1, :] += jnp.sum(spr * lr + spi * li, axis=0, keepdims=True)
            dab_ref[2 * sb + 1:2 * sb + 2, :] += jnp.sum(spr * li - spi * lr, axis=0, keepdims=True)
            lam = lam_scr[sb].astype(MXU_DTYPE)
            dbm_ref[sb] += _dot_tn(u_sb, lam)
            du = _dot_nt(lam, bm_ref[sb]) + d_ref[:, cols] * dy_sb
            du_ref[:, cols] = jnp.where(grow >= PAD_ROWS, du, 0.0).astype(MXU_DTYPE)

    rev = lambda j: (n_chunks - 1 - j, 0)
    full = lambda shape: pl.BlockSpec(shape, lambda j: (0,) * len(shape))
    of_layer = lambda shape: pl.BlockSpec((None,) + shape, lambda j: (layer,) + (0,) * len(shape))
    tables = [of_layer((N_SB, 8, SCAN_TILE, SB_STATES))] * 2
    chunk_scratch = pltpu.VMEM((N_SB, BLK, 2 * SB_STATES), F32)
    return _pcall(
        body, name=f"s5_bwd_l{layer}", grid=(n_chunks,),
        in_specs=[pl.BlockSpec((BLK, D_SSM), rev), pl.BlockSpec((BLK, D_SSM), rev), pl.BlockSpec((BLK, D_SSM), rev),
                  pl.BlockSpec((None, 8, SB_STATES), lambda j: (n_chunks - 1 - j, 0, 0)),
                  of_layer((N_SB, 128, 2 * SB_STATES)), of_layer((N_SB, 2 * SB_STATES, 128))] + tables + [
                  of_layer((1, D_SSM)), full((D_SSM, D_SSM)),
                  pl.BlockSpec((None, 1, D_SSM), lambda j: (layer, 0, 0))],
        out_specs=[pl.BlockSpec((BLK, D_SSM), rev), full((D_SSM, D_SSM)), full((1, D_SSM)), full((1, D_SSM)),
                   full((N_SB, 128, 2 * SB_STATES)), full((N_SB, 2 * SB_STATES, 128)), full((8, SB_STATES))],
        out_shape=[SDS((rows, D_SSM), MXU_DTYPE), SDS((D_SSM, D_SSM), F32), SDS((1, D_SSM), F32), SDS((1, D_SSM), F32),
                   SDS((N_SB, 128, 2 * SB_STATES), F32), SDS((N_SB, 2 * SB_STATES, 128), F32), SDS((8, SB_STATES), F32)],
        scratch_shapes=[pltpu.VMEM((8, SB_STATES), F32)] + [chunk_scratch] * 5,
        compiler_params=_cparams("arbitrary"),
    )(d_gated, y, u, carry_in, b_mat, c_mat, t_re, t_im, d_skip, w_glu, b_glu3)


DPROJ_PIECES = ((0, 1), (1, 2), (3, 1), (4, 2), (6, 2))


def _in_bwd(dproj_pieces, dhm, hres, gain3, w_in_g, layer):
    rows = hres.shape[0]
    tm = _row_tile(rows)

    def body(*refs):
        piece_refs = refs[:len(DPROJ_PIECES)]
        dh_ref, x_ref, g_ref, w_hbm, dx_ref, dg_ref, wt_scr, w_stage, w_sem = refs[len(DPROJ_PIECES):]
        i = pl.program_id(0)
        _load_resident_transposed(w_hbm, wt_scr, w_stage, w_sem, i == 0)

        @pl.when(i == 0)
        def _():
            dg_ref[...] = jnp.zeros_like(dg_ref)

        dh = None
        for piece_ref, (first, count) in zip(piece_refs, DPROJ_PIECES):
            wt = wt_scr[first:first + count].reshape(count * COL_SHARD, D)
            part = _dot(piece_ref[...], wt)
            dh = part if dh is None else dh + part
        dx, dg = _rms_bwd(x_ref[...], g_ref[...], dh)
        dg_ref[...] += dg
        dx_ref[...] = dh_ref[...] + dx

    row_d = pl.BlockSpec((tm, D), lambda i: (i, 0))
    return _pcall(
        body, name=f"in_bwd_l{layer}", grid=(rows // tm,),
        in_specs=[pl.BlockSpec((tm, count * COL_SHARD), lambda i: (i, 0)) for _, count in DPROJ_PIECES] + [
                  row_d, row_d,
                  pl.BlockSpec((None, 1, D), lambda i: (layer, 0, 0)),
                  pl.BlockSpec(memory_space=pl.ANY)],
        out_specs=[row_d, pl.BlockSpec((1, D), lambda i: (0, 0))],
        out_shape=[SDS((rows, D), F32), SDS((1, D), F32)],
        scratch_shapes=[pltpu.VMEM((N_DEV, COL_SHARD, D), MXU_DTYPE),
                        pltpu.VMEM((2, D, COL_SHARD), MXU_DTYPE), pltpu.SemaphoreType.DMA((2,))],
        compiler_params=_cparams("arbitrary"),
    )(*dproj_pieces, dhm, hres, gain3, w_in_g)


_ADAM_C1 = 1.0 / (1.0 - ADAM_B1 ** ADAM_STEP)
_ADAM_C2 = 1.0 / (1.0 - ADAM_B2 ** ADAM_STEP)


def _adam_math(w, g, m, v):
    m = ADAM_B1 * m + (1.0 - ADAM_B1) * g
    v = ADAM_B2 * v + (1.0 - ADAM_B2) * (g * g)
    delta = -ADAM_LR * ((m * _ADAM_C1) / (jnp.sqrt(v * _ADAM_C2) + ADAM_EPS) + ADAM_WD * w)
    return delta, m, v


def _adamw_layers(parts0, parts1, w, m, v, name):
    _, rows, cols = w.shape
    tr = min(rows, (1 << 17) // cols)
    nt = rows // tr

    def body(p0_ref, p1_ref, w_ref, m_ref, v_ref, g_ref, d_ref, nm_ref, nv_ref):
        layer = pl.program_id(0)

        def run(p_ref):
            g = p_ref[0].astype(F32)
            for s in range(1, N_DEV):
                g = g + p_ref[s].astype(F32)
            delta, nm, nv = _adam_math(w_ref[...], g, m_ref[...], v_ref[...])
            g_ref[...] = g
            d_ref[...] = delta
            nm_ref[...] = nm
            nv_ref[...] = nv

        @pl.when(layer == 0)
        def _():
            run(p0_ref)

        @pl.when(layer == 1)
        def _():
            run(p1_ref)

    wspec = pl.BlockSpec((None, tr, cols), lambda l, i: (l, i, 0))
    return _pcall(
        body, name=name, grid=(2, nt),
        in_specs=[pl.BlockSpec((N_DEV, tr, cols), lambda l, i: (0, jnp.where(l == 0, i, nt - 1), 0)),
                  pl.BlockSpec((N_DEV, tr, cols), lambda l, i: (0, jnp.where(l == 1, i, 0), 0)),
                  wspec, wspec, wspec],
        out_specs=[wspec] * 4, out_shape=[SDS(w.shape, F32)] * 4,
        compiler_params=_cparams("arbitrary", "arbitrary"),
    )(parts0, parts1, w, m, v)


def _sum_slots(parts, name):
    def body(p_ref, o_ref):
        acc = p_ref[0]
        for s in range(1, N_DEV):
            acc = acc + p_ref[s]
        o_ref[...] = acc

    vmem = pl.BlockSpec(memory_space=pltpu.VMEM)
    return _pcall(body, name=name, out_shape=SDS(parts.shape[1:], F32), in_specs=[vmem], out_specs=vmem,
                  compiler_params=_cparams())(parts)


def _adamw_packed(g, w, m, v, name):
    def body(g_ref, w_ref, m_ref, v_ref, d_ref, nm_ref, nv_ref):
        delta, nm, nv = _adam_math(w_ref[...], g_ref[...], m_ref[...], v_ref[...])
        d_ref[...] = delta
        nm_ref[...] = nm
        nv_ref[...] = nv

    vmem = pl.BlockSpec(memory_space=pltpu.VMEM)
    return _pcall(body, name=name, out_shape=[SDS(g.shape, F32)] * 3, in_specs=[vmem] * 4, out_specs=[vmem] * 3,
                  compiler_params=_cparams())(g, w, m, v)


def _ssm_discretize(a_re, a_im, log_dt, b_re, b_im):
    dt = jnp.exp(log_dt)[:, None]
    mag = jnp.exp(a_re * dt)
    ang = a_im * dt
    ab_re, ab_im = mag * jnp.cos(ang), mag * jnp.sin(ang)
    xr, xi = ab_re - 1.0, ab_im
    den = a_re * a_re + a_im * a_im
    q_re = (xr * a_re + xi * a_im) / den
    q_im = (xi * a_re - xr * a_im) / den
    bb_re = q_re[..., None] * b_re - q_im[..., None] * b_im
    bb_im = q_re[..., None] * b_im + q_im[..., None] * b_re
    return ab_re, ab_im, bb_re, bb_im


def _block_diag_b(bb):
    m = jnp.einsum("sgnc,gh->sgchn", bb.reshape(N_SB, 8, N_STATE, GROUP_CH), jnp.eye(8, dtype=F32))
    return m.reshape(N_SB, 128, SB_STATES)


def _block_diag_b_t(dm):
    return jnp.einsum("sgchn,gh->sgnc", dm.reshape(N_SB, 8, GROUP_CH, 8, N_STATE),
                      jnp.eye(8, dtype=F32)).reshape(N_GROUPS, N_STATE, GROUP_CH)


def _block_diag_c(cc):
    m = jnp.einsum("sgcn,gh->sgnhc", cc.reshape(N_SB, 8, GROUP_CH, N_STATE), jnp.eye(8, dtype=F32))
    return m.reshape(N_SB, SB_STATES, 128)


def _block_diag_c_t(dm):
    return jnp.einsum("sgnhc,gh->sgcn", dm.reshape(N_SB, 8, N_STATE, 8, GROUP_CH),
                      jnp.eye(8, dtype=F32)).reshape(N_GROUPS, GROUP_CH, N_STATE)


def _ssm_tables(ab_re, ab_im, bb_re, bb_im, c_re, c_im, d_skip):
    pr, pi = ab_re.reshape(1, -1), ab_im.reshape(1, -1)
    cr, ci = pr, pi
    squares = []
    for _ in range(3):
        squares.append((cr, ci))
        pr, pi = (jnp.concatenate([pr, pr * cr - pi * ci], axis=0),
                  jnp.concatenate([pi, pr * ci + pi * cr], axis=0))
        cr, ci = cr * cr - ci * ci, 2.0 * cr * ci
    r = jnp.arange(SCAN_TILE)[:, None]
    fwd = [(jnp.where(r >= (1 << k), squares[k][0], 0.0), jnp.where(r >= (1 << k), squares[k][1], 0.0))
           for k in range(3)] + [(pr, pi)]
    rev = [(jnp.where(r < SCAN_TILE - (1 << k), squares[k][0], 0.0),
            jnp.where(r < SCAN_TILE - (1 << k), -squares[k][1], 0.0)) for k in range(3)] + [(pr[::-1], -pi[::-1])]
    table = lambda part: jnp.stack([e[part] for e in fwd + rev]).reshape(
        8, SCAN_TILE, N_SB, SB_STATES).transpose(2, 0, 1, 3)
    return dict(
        b_mat=jnp.concatenate([_block_diag_b(bb_re), _block_diag_b(bb_im)], axis=-1).astype(MXU_DTYPE),
        c_mat=jnp.concatenate([_block_diag_c(c_re), -_block_diag_c(c_im)], axis=1).astype(MXU_DTYPE),
        t_re=table(0), t_im=table(1),
        d_skip=d_skip.reshape(1, D_SSM))


def _rope_tables(rows):
    pos = (jnp.arange(rows, dtype=jnp.int32) - PAD_ROWS).astype(F32)
    inv_freq = 1.0 / (ROPE_THETA ** (jnp.arange(0, HEAD_DIM, 2, dtype=F32) / HEAD_DIM))
    ang = pos[:, None] * inv_freq[None, :]
    ang = jnp.concatenate([ang, ang, ang, ang], axis=-1)
    first_half = (jnp.arange(128) % HEAD_DIM) < HEAD_DIM // 2
    sin = jnp.sin(ang)
    return jnp.cos(ang), jnp.where(first_half, -sin, 0.0), jnp.where(first_half, 0.0, sin)


def _pack(arrays):
    flat = jnp.concatenate([a.reshape(-1).astype(F32) for a in arrays])
    pad = (-flat.shape[0]) % 1024
    return jnp.pad(flat, (0, pad)).reshape(-1, 128)


def _unpack(packed, like):
    flat = packed.reshape(-1)
    out, off = [], 0
    for a in like:
        n = math.prod(a.shape)
        out.append(flat[off:off + n].reshape(a.shape))
        off += n
    return out


BIG = ("w_in", "w_glu", "w_o_ssm", "w_o_attn", "w_out", "w_up", "w_down")
WEIGHTS = ("meta_tokens", "norm_mix_pre", "norm_mix_post", "norm_mlp_pre", "norm_mlp_post", "w_in",
           "ssm_a_re", "ssm_a_im", "ssm_log_dt", "ssm_b_re", "ssm_b_im", "ssm_c_re", "ssm_c_im", "ssm_d",
           "w_glu", "b_glu", "attn_sinks", "w_o_ssm", "w_o_attn", "w_out", "w_up", "w_down")
SMALL = tuple(n for n in WEIGHTS if n not in BIG)


def kernel(x, meta_tokens, norm_mix_pre, norm_mix_post, norm_mlp_pre, norm_mlp_post, w_in, ssm_a_re, ssm_a_im, ssm_log_dt, ssm_b_re, ssm_b_im, ssm_c_re, ssm_c_im, ssm_d, w_glu, b_glu, attn_sinks, w_o_ssm, w_o_attn, w_out, w_up, w_down, loss_target, m_meta_tokens, m_norm_mix_pre, m_norm_mix_post, m_norm_mlp_pre, m_norm_mlp_post, m_w_in, m_ssm_a_re, m_ssm_a_im, m_ssm_log_dt, m_ssm_b_re, m_ssm_b_im, m_ssm_c_re, m_ssm_c_im, m_ssm_d, m_w_glu, m_b_glu, m_attn_sinks, m_w_o_ssm, m_w_o_attn, m_w_out, m_w_up, m_w_down, v_meta_tokens, v_norm_mix_pre, v_norm_mix_post, v_norm_mlp_pre, v_norm_mlp_post, v_w_in, v_ssm_a_re, v_ssm_a_im, v_ssm_log_dt, v_ssm_b_re, v_ssm_b_im, v_ssm_c_re, v_ssm_c_im, v_ssm_d, v_w_glu, v_b_glu, v_attn_sinks, v_w_o_ssm, v_w_o_attn, v_w_out, v_w_up, v_w_down):
    args = locals()
    w = {n: args[n] for n in WEIGHTS}
    m = {n: args["m_" + n] for n in WEIGHTS}
    v = {n: args["v_" + n] for n in WEIGHTS}
    n_layers = w_in.shape[0]
    seq = x.shape[1]
    rows = seq + BLK
    my_slot = _slot(_mesh_pos())

    assert n_layers == 2
    xfer = {n: [w[n][l].astype(XFER_DTYPE) for l in range(n_layers)] for n in BIG}
    mixer_small = ("w_glu", "w_o_ssm", "w_o_attn", "w_out")
    meta_g, w_in_g0 = _exchange_by_sequencer([meta_tokens, xfer["w_in"][0]], True, 0, "gather_in0")
    mix0_g = _exchange_by_sequencer([xfer[n][0] for n in mixer_small], True, 1, "gather_mix0")
    meta_full = meta_g.transpose(1, 0, 2).reshape(N_META, D)

    def mixer_weights(w_glu_g, w_o_ssm_g, w_o_attn_g, w_out_g):
        return dict(w_glu=w_glu_g.reshape(D_SSM, D_SSM), w_o_ssm=w_o_ssm_g.transpose(1, 0, 2).reshape(D_SSM, D),
                    w_o_attn=w_o_attn_g.reshape(D_ATTN, D), w_out=w_out_g.reshape(D, D),
                    w_o_ssm_t=w_o_ssm_g.transpose(0, 2, 1).reshape(D, D_SSM),
                    w_o_attn_t=w_o_attn_g.reshape(D_ATTN, D).T, w_out_t=w_out_g.reshape(D, D).T)

    gathered = [dict(w_in=w_in_g0, **mixer_weights(*mix0_g)), {}]

    gains = {n: w[n].reshape(n_layers, 1, D) for n in ("norm_mix_pre", "norm_mix_post", "norm_mlp_pre", "norm_mlp_post")}
    b_glu3 = b_glu.reshape(n_layers, 1, D_SSM)
    cos, sin_a, sin_b = _rope_tables(rows)

    disc, disc_vjp = jax.vjp(jax.vmap(_ssm_discretize), ssm_a_re, ssm_a_im, ssm_log_dt, ssm_b_re, ssm_b_im)
    ssm = jax.vmap(_ssm_tables)(*disc, ssm_c_re, ssm_c_im, ssm_d)

    hres = jnp.concatenate([jnp.zeros((PAD_ROWS, D), F32), meta_full, x[0]], axis=0)
    target = jnp.concatenate([jnp.zeros((BLK, D), F32), loss_target[0]], axis=0)

    saved = []
    for l in range(n_layers):
        wl = gathered[l]
        u, gates, q, k, vv, h = _in_proj(hres, gains["norm_mix_pre"], wl["w_in"], cos, sin_a, sin_b, l,
                                         after=[ssm["b_mat"], ssm["c_mat"], ssm["t_re"], ssm["t_im"], target] if l == 0 else ())
        if l == 0:
            wl["w_up"], wl["w_down"] = _exchange_by_sequencer([xfer["w_up"][0], xfer["w_down"][0]], True, 2,
                                                              "gather_mlp0")
        y, y_ssm, carry_in = _s5_fwd(u, ssm, wl["w_glu"], b_glu3, l)
        if l == 0:
            l1_g = _exchange_by_sequencer([xfer[n][1] for n in ("w_in",) + mixer_small + ("w_up", "w_down")], True, 3,
                                          "gather_l1", after=[y])
            gathered[1] = dict(w_in=l1_g[0], w_up=l1_g[5], w_down=l1_g[6], **mixer_weights(*l1_g[1:5]))
            last_exchange = l1_g[:1]
        y_attn = _attn_fwd(q, k, vv, attn_sinks, l)
        merged, mix, hres_mid = _merge_fwd(y_ssm, y_attn, gates, hres, wl["w_o_ssm"], wl["w_o_attn"], wl["w_out"],
                                           gains["norm_mix_post"], l)
        hres_in = hres
        if l + 1 < n_layers:
            up, h2, ff, hres = _mlp_fwd(hres_mid, gains["norm_mlp_pre"], gains["norm_mlp_post"], wl["w_up"],
                                        wl["w_down"], l)
        else:
            up, h2, ff, dhres, loss_vec = _mlp_fwd(hres_mid, gains["norm_mlp_pre"], gains["norm_mlp_post"], wl["w_up"],
                                                   wl["w_down"], l, target=target)
        saved.append(dict(hres=hres_in, u=u, gates=gates, h=h, q=q, k=k, v=vv, y=y, y_ssm=y_ssm,
                          carry_in=carry_in, y_attn=y_attn, merged=merged, mix=mix, hres_mid=hres_mid,
                          up=up, h2=h2, ff=ff))

    small_grads = {}
    recv_up, recv_down, recv_mix = [None] * n_layers, [None] * n_layers, [None] * n_layers
    for l in reversed(range(n_layers)):
        s = saved[l]
        wl = gathered[l]
        dff, dup, dhm, dg_mlp_post, dg_mlp_pre = _mlp_bwd(dhres, s["ff"], s["up"], s["hres_mid"], gains["norm_mlp_pre"],
                                                          gains["norm_mlp_post"], wl["w_up"], wl["w_down"], l)
        dw_up = _matmul_tn(s["h2"], dup, f"dw_up_l{l}", dev_major_cols=COL_SHARD)
        recv_up[l] = _exchange_by_sequencer([dw_up], False, 4 + 3 * l, f"scatter_up{l}", after=last_exchange)
        dw_down = _matmul_tn(s["up"], dff, f"dw_down_l{l}", a_fn=_relu_squared).reshape(N_DEV, COL_SHARD, D)
        recv_down[l] = _exchange_by_sequencer([dw_down], False, 5 + 3 * l, f"scatter_down{l}", after=recv_up[l])
        last_exchange = recv_down[l]
        dmix, da1, da2, dgs, dga, dy_ssm, dy_attn, dg_mix_post = _merge_bwd(
            dhm, s["mix"], s["y_ssm"], s["y_attn"], s["gates"], wl["w_o_ssm"], wl["w_o_attn"], wl["w_o_ssm_t"],
            wl["w_o_attn_t"], wl["w_out_t"], gains["norm_mix_post"], l)
        dw_out = _matmul_tn(s["merged"], dmix, f"dw_out_l{l}").reshape(N_DEV, D // N_DEV, D)
        dw_o_attn = _matmul_tn(s["y_attn"], da2, f"dw_o_attn_l{l}").reshape(N_DEV, D_ATTN // N_DEV, D)
        dw_o_ssm = _matmul_tn(s["y_ssm"], da1, f"dw_o_ssm_l{l}", dev_major_cols=D // N_DEV)
        if l == 0:
            recv_out0 = _exchange_by_sequencer([dw_o_ssm, dw_o_attn, dw_out], False, 11, "scatter_out0",
                                               after=last_exchange)
            last_exchange = recv_out0[:1]
        dq, dk, dv, dk_meta, dv_meta, dsink = _attn_bwd(s["q"], s["k"], s["v"], dy_attn, attn_sinks, cos, sin_a, sin_b, l)
        dkv = _rope_bwd(dk, dv, dk_meta, dv_meta, cos, sin_a, sin_b, l)
        du, dw_glu, db_glu, dd_skip, db_mat, dc_mat, dab = _s5_bwd(dy_ssm, s["y"], s["u"], s["carry_in"], ssm,
                                                                    wl["w_glu"], b_glu3, l)
        dproj = (du, dq, dkv, dgs, dga)
        dw_in = _dw_in(s["h"], dproj, l)
        mix_parts = [dw_in, dw_glu.astype(XFER_DTYPE).reshape(N_DEV, D_SSM // N_DEV, D_SSM), dw_o_ssm, dw_o_attn, dw_out]
        if l > 0:
            recv_mix[l] = _exchange_by_sequencer(mix_parts, False, 6 + 3 * l, f"scatter_mix{l}", after=last_exchange)
            last_exchange = recv_mix[l][:1]
        else:
            recv_mix[0] = _exchange_by_sequencer(mix_parts[:2], False, 6, "scatter_in0", after=last_exchange) + recv_out0
            last_exchange = recv_mix[0][:1]
        dhres, dg_mix_pre = _in_bwd(dproj, dhm, s["hres"], gains["norm_mix_pre"], wl["w_in"], l)

        for name, val in (("norm_mix_pre", dg_mix_pre[0]), ("norm_mix_post", dg_mix_post[0]),
                          ("norm_mlp_pre", dg_mlp_pre[0]), ("norm_mlp_post", dg_mlp_post[0]),
                          ("dab", dab), ("db_mat", db_mat), ("dc_mat", dc_mat),
                          ("ssm_d", dd_skip.reshape(N_GROUPS, GROUP_CH)), ("b_glu", db_glu[0]),
                          ("attn_sinks", dsink[:, 0])):
            small_grads.setdefault(name, [None] * n_layers)[l] = val

    grad_x = dhres[BLK:][None]
    stacked = {n: jnp.stack(v) for n, v in small_grads.items()}
    dab = stacked["dab"].reshape(n_layers, N_SB, 2, SB_STATES)
    db_mat, dc_mat = stacked["db_mat"], stacked["dc_mat"]
    b_t, c_t = jax.vmap(_block_diag_b_t), jax.vmap(_block_diag_c_t)
    (stacked["ssm_a_re"], stacked["ssm_a_im"], stacked["ssm_log_dt"], stacked["ssm_b_re"],
     stacked["ssm_b_im"]) = disc_vjp((dab[:, :, 0].reshape(n_layers, N_GROUPS, N_STATE),
                                      dab[:, :, 1].reshape(n_layers, N_GROUPS, N_STATE),
                                      b_t(db_mat[..., :SB_STATES]), b_t(db_mat[..., SB_STATES:])))
    stacked["ssm_c_re"] = c_t(dc_mat[:, :, :SB_STATES])
    stacked["ssm_c_im"] = -c_t(dc_mat[:, :, SB_STATES:])
    small_names = [n for n in SMALL if n != "meta_tokens"]
    partial_small = [dhres[PAD_ROWS:BLK]] + [stacked[n] for n in small_names] + [loss_vec[0, :1]]
    small_parts, = _exchange_by_sequencer([_pack(partial_small)], True, 10, "gather_small", after=last_exchange)

    grads, delta, new_m, new_v = {}, {}, {}, {}

    def adamw_big(names, recv0, recv1):
        for n, p0, p1 in zip(names, recv0, recv1):
            grads[n], delta[n], new_m[n], new_v[n] = _adamw_layers(p0, p1, w[n], m[n], v[n], f"adamw_{n}")

    adamw_big(("w_up", "w_down"), recv_up[0] + recv_down[0], recv_up[1] + recv_down[1])
    summed = _unpack(_sum_slots(small_parts, "sum_small_grads"), partial_small)
    loss = summed[-1][0]
    grads.update(zip(small_names, summed[1:-1]))
    grads["meta_tokens"] = lax.dynamic_slice_in_dim(summed[0], my_slot * (D // N_DEV), D // N_DEV, axis=1)
    like = [w[n] for n in SMALL]
    d_s, m_s, v_s = _adamw_packed(_pack([grads[n] for n in SMALL]), _pack(like), _pack([m[n] for n in SMALL]),
                                  _pack([v[n] for n in SMALL]), "adamw_small")
    adamw_big(("w_in",) + mixer_small, recv_mix[0], recv_mix[1])
    for n, dd, mm, vs in zip(SMALL, _unpack(d_s, like), _unpack(m_s, like), _unpack(v_s, like)):
        delta[n], new_m[n], new_v[n] = dd, mm, vs

    return (loss, grad_x, *[grads[n] for n in WEIGHTS], *[delta[n] for n in WEIGHTS],
            *[new_m[n] for n in WEIGHTS], *[new_v[n] for n in WEIGHTS])
```
